```python
import math
import jax, jax.numpy as jnp
from jax import lax
import numpy as np

D_MODEL = 1024
BATCH = 8
SEQ = 8192
DEPTH = 2

CHUNK = 64
HEAD_DIM = 64
N_HEADS_A = 8
N_HEADS_B = 8
WIDTH_A = N_HEADS_A * HEAD_DIM
WIDTH_B = N_HEADS_B * HEAD_DIM
LEFT_CHUNKS = 8
BAND = (LEFT_CHUNKS + 1) * CHUNK
REL_CLIP = 256
N_REL = 2 * REL_CLIP + 1
SB_BLOCK = 128
D_FF = -(-8 * D_MODEL // (3 * 256)) * 256
IN_COLS = 3 * WIDTH_A + 3 * WIDTH_B + 2 * D_MODEL
DEEPNORM_ALPHA = (2 * DEPTH) ** 0.25
DEEPNORM_BETA = (8 * DEPTH) ** -0.25
LN_EPS = 1e-5

kernel_name = "hybrid_chunk_relbias_stickbreaking_deepnorm"


def layer_norm(x, g, b):
    xf = x.astype(jnp.float32)
    mu = jnp.mean(xf, axis=-1, keepdims=True)
    var = jnp.mean(jnp.square(xf - mu), axis=-1, keepdims=True)
    y = (xf - mu) * lax.rsqrt(var + LN_EPS) * g.astype(jnp.float32) + b.astype(jnp.float32)
    return y.astype(x.dtype)


def chunk_band_attention(q, k, v, rel_bias):
    B, S, H, dh = q.shape
    nc = S // CHUNK
    qc = q.reshape(B, nc, CHUNK, H, dh)
    pad = ((0, 0), (LEFT_CHUNKS * CHUNK, 0), (0, 0), (0, 0))
    kp = jnp.pad(k, pad).reshape(B, nc + LEFT_CHUNKS, CHUNK, H, dh)
    vp = jnp.pad(v, pad).reshape(B, nc + LEFT_CHUNKS, CHUNK, H, dh)
    kb = jnp.concatenate([kp[:, j:j + nc] for j in range(LEFT_CHUNKS + 1)], axis=2)
    vb = jnp.concatenate([vp[:, j:j + nc] for j in range(LEFT_CHUNKS + 1)], axis=2)
    scores = jnp.einsum('bcqhd,bckhd->bhcqk', qc, kb).astype(jnp.float32) / math.sqrt(dh)
    i = jnp.arange(CHUNK)[:, None]
    p = jnp.arange(BAND)[None, :]
    dist = LEFT_CHUNKS * CHUNK + i - p
    idx = jnp.clip(dist, -REL_CLIP, REL_CLIP) + REL_CLIP
    bias = rel_bias.astype(jnp.float32)[:, idx]
    valid = (jnp.arange(nc)[:, None] + jnp.arange(BAND)[None, :] // CHUNK - LEFT_CHUNKS) >= 0
    scores = scores + bias[None, :, None, :, :]
    scores = jnp.where(valid[None, None, :, None, :], scores, -jnp.inf)
    probs = jax.nn.softmax(scores, axis=-1).astype(v.dtype)
    out = jnp.einsum('bhcqk,bckhd->bcqhd', probs, vb)
    return out.reshape(B, S, H * dh)


def stick_breaking_attention(q, k, v):
    B, S, H, dh = q.shape
    nb = S // SB_BLOCK
    scale = 1.0 / math.sqrt(dh)
    qb = q.reshape(B, nb, SB_BLOCK, H, dh).transpose(1, 0, 2, 3, 4)
    key_pos = jnp.arange(S)

    def one_block(args):
        q_blk, blk = args
        z = jnp.einsum('bqhd,bshd->bhqs', q_blk, k).astype(jnp.float32) * scale
        t = blk * SB_BLOCK + jnp.arange(SB_BLOCK)
        causal = (key_pos[None, :] < t[:, None])[None, None]
        log_keep = jnp.where(causal, jax.nn.log_sigmoid(-z), 0.0)
        suffix = lax.cumsum(log_keep, axis=3, reverse=True) - log_keep
        log_w = jnp.where(causal, jax.nn.log_sigmoid(z) + suffix, -jnp.inf)
        w = jnp.exp(log_w).astype(v.dtype)
        return jnp.einsum('bhqs,bshd->bqhd', w, v)

    out = lax.map(one_block, (qb, jnp.arange(nb)))
    return out.transpose(1, 0, 2, 3, 4).reshape(B, S, H * dh)


def _fwd_setup_inputs(seed: int = 0) -> dict:
    key = jax.random.key(seed)
    ks = jax.random.split(key, 16)
    D = D_MODEL
    x = jax.random.normal(ks[0], (BATCH, SEQ, D), jnp.float32)
    w_in = jax.random.normal(ks[1], (DEPTH, D, IN_COLS), jnp.float32) * D ** -0.5
    col_scale = jnp.concatenate([
        jnp.ones((2 * WIDTH_A,)), jnp.full((WIDTH_A,), DEEPNORM_BETA),
        jnp.ones((2 * WIDTH_B,)), jnp.full((WIDTH_B,), DEEPNORM_BETA),
        jnp.ones((2 * D,))]).astype(jnp.float32)
    w_in = w_in * col_scale
    b_gate = 0.01 * jax.random.normal(ks[2], (DEPTH, 2 * D), jnp.float32)
    rel_bias = 0.1 * jax.random.normal(ks[3], (DEPTH, N_HEADS_A, N_REL), jnp.float32)
    w_proj_a = jax.random.normal(ks[4], (DEPTH, WIDTH_A, D), jnp.float32) * WIDTH_A ** -0.5
    w_proj_b = jax.random.normal(ks[5], (DEPTH, WIDTH_B, D), jnp.float32) * WIDTH_B ** -0.5
    w_out = jax.random.normal(ks[6], (DEPTH, D, D), jnp.float32) * (D ** -0.5 * DEEPNORM_BETA)
    ln1_g = 1.0 + 0.02 * jax.random.normal(ks[7], (DEPTH, D), jnp.float32)
    ln1_b = 0.02 * jax.random.normal(ks[8], (DEPTH, D), jnp.float32)
    w_ffn_in = jax.random.normal(ks[9], (DEPTH, D, 2 * D_FF), jnp.float32) * D ** -0.5
    w_ffn_out = jax.random.normal(ks[10], (DEPTH, D_FF, D), jnp.float32) * (D_FF ** -0.5 * DEEPNORM_BETA)
    ln2_g = 1.0 + 0.02 * jax.random.normal(ks[11], (DEPTH, D), jnp.float32)
    ln2_b = 0.02 * jax.random.normal(ks[12], (DEPTH, D), jnp.float32)
    return {"x": x, "w_in": w_in, "b_gate": b_gate, "rel_bias": rel_bias,
            "w_proj_a": w_proj_a, "w_proj_b": w_proj_b, "w_out": w_out,
            "ln1_g": ln1_g, "ln1_b": ln1_b, "w_ffn_in": w_ffn_in,
            "w_ffn_out": w_ffn_out, "ln2_g": ln2_g, "ln2_b": ln2_b}


def _fwd_reference(x, w_in, b_gate, rel_bias, w_proj_a, w_proj_b, w_out,
              ln1_g, ln1_b, w_ffn_in, w_ffn_out, ln2_g, ln2_b):
    B, S, D = x.shape
    split_pts = np.cumsum([WIDTH_A, WIDTH_A, WIDTH_A, WIDTH_B, WIDTH_B, WIDTH_B, D]).tolist()
    for l in range(DEPTH):
        h = x @ w_in[l]
        qa, ka, va, qb, kb, vb, ga, gb = jnp.split(h, split_pts, axis=-1)
        heads_a = lambda t: t.reshape(B, S, N_HEADS_A, HEAD_DIM)
        heads_b = lambda t: t.reshape(B, S, N_HEADS_B, HEAD_DIM)
        y_a = chunk_band_attention(heads_a(qa), heads_a(ka), heads_a(va), rel_bias[l]) @ w_proj_a[l]
        y_b = stick_breaking_attention(heads_b(qb), heads_b(kb), heads_b(vb)) @ w_proj_b[l]
        gate_a = jax.nn.sigmoid(ga + b_gate[l, :D])
        gate_b = jax.nn.sigmoid(gb + b_gate[l, D:])
        mix = (gate_a * y_a + gate_b * y_b) @ w_out[l]
        x = layer_norm(DEEPNORM_ALPHA * x + mix, ln1_g[l], ln1_b[l])
        gu = x @ w_ffn_in[l]
        g, u = jnp.split(gu, 2, axis=-1)
        ffn = (jax.nn.silu(g) * u) @ w_ffn_out[l]
        x = layer_norm(DEEPNORM_ALPHA * x + ffn, ln2_g[l], ln2_b[l])
    return x


import jax as _jax
import jax.numpy as _jnp

TWIN_FORMAT = 'train_step'
FWD_PARAMS = ['x', 'w_in', 'b_gate', 'rel_bias', 'w_proj_a', 'w_proj_b', 'w_out', 'ln1_g', 'ln1_b', 'w_ffn_in', 'w_ffn_out', 'ln2_g', 'ln2_b']
TWIN_WEIGHTS = ['w_in', 'b_gate', 'rel_bias', 'w_proj_a', 'w_proj_b', 'w_out', 'ln1_g', 'ln1_b', 'w_ffn_in', 'w_ffn_out', 'ln2_g', 'ln2_b']
TWIN_DIFF_INPUT = 'x'
TWIN_INPUTS = ['x', 'w_in', 'b_gate', 'rel_bias', 'w_proj_a', 'w_proj_b', 'w_out', 'ln1_g', 'ln1_b', 'w_ffn_in', 'w_ffn_out', 'ln2_g', 'ln2_b', 'loss_target', 'm_w_in', 'm_b_gate', 'm_rel_bias', 'm_w_proj_a', 'm_w_proj_b', 'm_w_out', 'm_ln1_g', 'm_ln1_b', 'm_w_ffn_in', 'm_w_ffn_out', 'm_ln2_g', 'm_ln2_b', 'v_w_in', 'v_b_gate', 'v_rel_bias', 'v_w_proj_a', 'v_w_proj_b', 'v_w_out', 'v_ln1_g', 'v_ln1_b', 'v_w_ffn_in', 'v_w_ffn_out', 'v_ln2_g', 'v_ln2_b']
TWIN_OUTPUTS = ['loss', 'grad_x', 'grad_w_in', 'grad_b_gate', 'grad_rel_bias', 'grad_w_proj_a', 'grad_w_proj_b', 'grad_w_out', 'grad_ln1_g', 'grad_ln1_b', 'grad_w_ffn_in', 'grad_w_ffn_out', 'grad_ln2_g', 'grad_ln2_b', 'delta_w_in', 'delta_b_gate', 'delta_rel_bias', 'delta_w_proj_a', 'delta_w_proj_b', 'delta_w_out', 'delta_ln1_g', 'delta_ln1_b', 'delta_w_ffn_in', 'delta_w_ffn_out', 'delta_ln2_g', 'delta_ln2_b', 'new_m_w_in', 'new_m_b_gate', 'new_m_rel_bias', 'new_m_w_proj_a', 'new_m_w_proj_b', 'new_m_w_out', 'new_m_ln1_g', 'new_m_ln1_b', 'new_m_w_ffn_in', 'new_m_w_ffn_out', 'new_m_ln2_g', 'new_m_ln2_b', 'new_v_w_in', 'new_v_b_gate', 'new_v_rel_bias', 'new_v_w_proj_a', 'new_v_w_proj_b', 'new_v_w_out', 'new_v_ln1_g', 'new_v_ln1_b', 'new_v_w_ffn_in', 'new_v_w_ffn_out', 'new_v_ln2_g', 'new_v_ln2_b']
TWIN_LEAF_KINDS = {'loss': 'loss', 'grad_x': 'grad_x', 'grad_w_in': 'grad_w', 'grad_b_gate': 'grad_w', 'grad_rel_bias': 'grad_w', 'grad_w_proj_a': 'grad_w', 'grad_w_proj_b': 'grad_w', 'grad_w_out': 'grad_w', 'grad_ln1_g': 'grad_w', 'grad_ln1_b': 'grad_w', 'grad_w_ffn_in': 'grad_w', 'grad_w_ffn_out': 'grad_w', 'grad_ln2_g': 'grad_w', 'grad_ln2_b': 'grad_w', 'delta_w_in': 'delta_w', 'delta_b_gate': 'delta_w', 'delta_rel_bias': 'delta_w', 'delta_w_proj_a': 'delta_w', 'delta_w_proj_b': 'delta_w', 'delta_w_out': 'delta_w', 'delta_ln1_g': 'delta_w', 'delta_ln1_b': 'delta_w', 'delta_w_ffn_in': 'delta_w', 'delta_w_ffn_out': 'delta_w', 'delta_ln2_g': 'delta_w', 'delta_ln2_b': 'delta_w', 'new_m_w_in': 'new_m', 'new_m_b_gate': 'new_m', 'new_m_rel_bias': 'new_m', 'new_m_w_proj_a': 'new_m', 'new_m_w_proj_b': 'new_m', 'new_m_w_out': 'new_m', 'new_m_ln1_g': 'new_m', 'new_m_ln1_b': 'new_m', 'new_m_w_ffn_in': 'new_m', 'new_m_w_ffn_out': 'new_m', 'new_m_ln2_g': 'new_m', 'new_m_ln2_b': 'new_m', 'new_v_w_in': 'new_v', 'new_v_b_gate': 'new_v', 'new_v_rel_bias': 'new_v', 'new_v_w_proj_a': 'new_v', 'new_v_w_proj_b': 'new_v', 'new_v_w_out': 'new_v', 'new_v_ln1_g': 'new_v', 'new_v_ln1_b': 'new_v', 'new_v_w_ffn_in': 'new_v', 'new_v_w_ffn_out': 'new_v', 'new_v_ln2_g': 'new_v', 'new_v_ln2_b': 'new_v'}


def _forward(args):
    return _fwd_reference(*[args[k] for k in FWD_PARAMS])


def _output_shape():
    def fwd():
        inp = _fwd_setup_inputs(0)
        return _fwd_reference(*[inp[k] for k in FWD_PARAMS])
    out = _jax.eval_shape(fwd)
    return out.shape, out.dtype

N_MICROBATCH = 1
ADAM_LR = 0.001
ADAM_B1 = 0.9
ADAM_B2 = 0.999
ADAM_EPS = 1e-08
ADAM_WD = 0.01
ADAM_STEP = 10
PER_EXAMPLE_BATCH_AXIS = {'x': 0, 'loss_target': 0}
SHARED_INPUTS = []
_WEIGHT_DTYPES = {'w_in': _jnp.float32, 'b_gate': _jnp.float32, 'rel_bias': _jnp.float32, 'w_proj_a': _jnp.float32, 'w_proj_b': _jnp.float32, 'w_out': _jnp.float32, 'ln1_g': _jnp.float32, 'ln1_b': _jnp.float32, 'w_ffn_in': _jnp.float32, 'w_ffn_out': _jnp.float32, 'ln2_g': _jnp.float32, 'ln2_b': _jnp.float32}
MOMENT_SCALE = {'w_in': 1.568856e-02, 'b_gate': 4.497057e-03, 'rel_bias': 1.708200e-03, 'w_proj_a': 3.767275e-03, 'w_proj_b': 1.597592e-02, 'w_out': 3.255453e-02, 'ln1_g': 1.988763e+00, 'ln1_b': 9.757198e-01, 'w_ffn_in': 3.204021e-02, 'w_ffn_out': 1.049123e-01, 'ln2_g': 4.535148e+01, 'ln2_b': 1.636299e+00}


def _to_microbatches(a, axis):
    t = _jnp.moveaxis(a, axis, 0)
    t = t.reshape((N_MICROBATCH, t.shape[0] // N_MICROBATCH) + t.shape[1:])
    return _jnp.moveaxis(t, 1, axis + 1)


def setup_inputs(seed: int = 0) -> dict:
    inp = _fwd_setup_inputs(seed)
    key = _jax.random.fold_in(_jax.random.key(seed), 7919)
    shape, _ = _output_shape()
    out = dict(inp)
    out["loss_target"] = _jax.random.normal(_jax.random.fold_in(key, 0), shape, _jnp.float32)
    for i, name in enumerate(TWIN_WEIGHTS):
        w = inp[name].astype(_jnp.float32)
        if MOMENT_SCALE is None:
            s = _jnp.sqrt(_jnp.mean(_jnp.square(w)) + 1e-30)
        else:
            s = MOMENT_SCALE[name]
        km, kv = _jax.random.split(_jax.random.fold_in(key, i + 1))
        out[name] = w
        out["m_" + name] = s * _jax.random.normal(km, w.shape, _jnp.float32)
        out["v_" + name] = (s * s) * _jax.random.uniform(kv, w.shape, _jnp.float32, 0.5, 1.5)
    if N_MICROBATCH > 1:
        for name, axis in PER_EXAMPLE_BATCH_AXIS.items():
            out[name] = _to_microbatches(out[name], axis)
    return {'x': out['x'], 'w_in': out['w_in'], 'b_gate': out['b_gate'], 'rel_bias': out['rel_bias'], 'w_proj_a': out['w_proj_a'], 'w_proj_b': out['w_proj_b'], 'w_out': out['w_out'], 'ln1_g': out['ln1_g'], 'ln1_b': out['ln1_b'], 'w_ffn_in': out['w_ffn_in'], 'w_ffn_out': out['w_ffn_out'], 'ln2_g': out['ln2_g'], 'ln2_b': out['ln2_b'], 'loss_target': out['loss_target'], 'm_w_in': out['m_w_in'], 'm_b_gate': out['m_b_gate'], 'm_rel_bias': out['m_rel_bias'], 'm_w_proj_a': out['m_w_proj_a'], 'm_w_proj_b': out['m_w_proj_b'], 'm_w_out': out['m_w_out'], 'm_ln1_g': out['m_ln1_g'], 'm_ln1_b': out['m_ln1_b'], 'm_w_ffn_in': out['m_w_ffn_in'], 'm_w_ffn_out': out['m_w_ffn_out'], 'm_ln2_g': out['m_ln2_g'], 'm_ln2_b': out['m_ln2_b'], 'v_w_in': out['v_w_in'], 'v_b_gate': out['v_b_gate'], 'v_rel_bias': out['v_rel_bias'], 'v_w_proj_a': out['v_w_proj_a'], 'v_w_proj_b': out['v_w_proj_b'], 'v_w_out': out['v_w_out'], 'v_ln1_g': out['v_ln1_g'], 'v_ln1_b': out['v_ln1_b'], 'v_w_ffn_in': out['v_w_ffn_in'], 'v_w_ffn_out': out['v_w_ffn_out'], 'v_ln2_g': out['v_ln2_g'], 'v_ln2_b': out['v_ln2_b']}


def _loss(weights, diff, rest, loss_target):
    with _jax.named_scope("forward"):
        args = {**rest, TWIN_DIFF_INPUT: diff, **{k: w.astype(_WEIGHT_DTYPES[k]) for k, w in weights.items()}}
        y = _forward(args)
    with _jax.named_scope("loss_head"):
        err = _jnp.square(y.astype(_jnp.float32) - loss_target)
        return 0.5 * _jnp.sum(_jnp.mean(err, axis=-1)) if err.ndim else 0.5 * err


def _adamw(w, g, m, v):
    m = ADAM_B1 * m + (1.0 - ADAM_B1) * g
    v = ADAM_B2 * v + (1.0 - ADAM_B2) * _jnp.square(g)
    m_hat = m / (1.0 - ADAM_B1 ** ADAM_STEP)
    v_hat = v / (1.0 - ADAM_B2 ** ADAM_STEP)
    delta = -ADAM_LR * (m_hat / (_jnp.sqrt(v_hat) + ADAM_EPS) + ADAM_WD * w)
    return delta, m, v


def reference(x, w_in, b_gate, rel_bias, w_proj_a, w_proj_b, w_out, ln1_g, ln1_b, w_ffn_in, w_ffn_out, ln2_g, ln2_b, loss_target, m_w_in, m_b_gate, m_rel_bias, m_w_proj_a, m_w_proj_b, m_w_out, m_ln1_g, m_ln1_b, m_w_ffn_in, m_w_ffn_out, m_ln2_g, m_ln2_b, v_w_in, v_b_gate, v_rel_bias, v_w_proj_a, v_w_proj_b, v_w_out, v_ln1_g, v_ln1_b, v_w_ffn_in, v_w_ffn_out, v_ln2_g, v_ln2_b):
    given = dict(x=x, w_in=w_in, b_gate=b_gate, rel_bias=rel_bias, w_proj_a=w_proj_a, w_proj_b=w_proj_b, w_out=w_out, ln1_g=ln1_g, ln1_b=ln1_b, w_ffn_in=w_ffn_in, w_ffn_out=w_ffn_out, ln2_g=ln2_g, ln2_b=ln2_b, loss_target=loss_target, m_w_in=m_w_in, m_b_gate=m_b_gate, m_rel_bias=m_rel_bias, m_w_proj_a=m_w_proj_a, m_w_proj_b=m_w_proj_b, m_w_out=m_w_out, m_ln1_g=m_ln1_g, m_ln1_b=m_ln1_b, m_w_ffn_in=m_w_ffn_in, m_w_ffn_out=m_w_ffn_out, m_ln2_g=m_ln2_g, m_ln2_b=m_ln2_b, v_w_in=v_w_in, v_b_gate=v_b_gate, v_rel_bias=v_rel_bias, v_w_proj_a=v_w_proj_a, v_w_proj_b=v_w_proj_b, v_w_out=v_w_out, v_ln1_g=v_ln1_g, v_ln1_b=v_ln1_b, v_w_ffn_in=v_w_ffn_in, v_w_ffn_out=v_w_ffn_out, v_ln2_g=v_ln2_g, v_ln2_b=v_ln2_b)
    weights = {n: given[n] for n in TWIN_WEIGHTS}
    shared = {n: given[n] for n in SHARED_INPUTS}
    per_example = {n: given[n] for n in ['x']}
    grad_fn = _jax.value_and_grad(_loss, argnums=(0, 1))

    def one_microbatch(ex, loss_target):
        ex = dict(ex)
        diff = ex.pop(TWIN_DIFF_INPUT)
        return grad_fn(weights, diff, {**shared, **ex}, loss_target)

    if N_MICROBATCH == 1:
        loss, (grad_w, grad_x) = one_microbatch(per_example, given["loss_target"])
    else:
        def body(carry, xs):
            loss_sum, grad_sum = carry
            l_k, (gw_k, gx_k) = one_microbatch(xs[0], xs[1])
            with _jax.named_scope("update"):
                return (loss_sum + l_k, _jax.tree.map(_jnp.add, grad_sum, gw_k)), gx_k

        init = (_jnp.zeros((), _jnp.float32), _jax.tree.map(_jnp.zeros_like, weights))
        (loss, grad_w), grad_x = _jax.lax.scan(body, init, (per_example, given["loss_target"]))
    with _jax.named_scope("update"):
        delta_w, new_m, new_v = {}, {}, {}
        for n in TWIN_WEIGHTS:
            delta_w[n], new_m[n], new_v[n] = _adamw(weights[n], grad_w[n], given["m_" + n], given["v_" + n])
    return (loss, grad_x, *[grad_w[n] for n in TWIN_WEIGHTS], *[delta_w[n] for n in TWIN_WEIGHTS],
            *[new_m[n] for n in TWIN_WEIGHTS], *[new_v[n] for n in TWIN_WEIGHTS])
```

```python
import functools

import jax
import jax.numpy as jnp
from jax import lax
from jax.experimental import pallas as pl
from jax.experimental.pallas import tpu as pltpu

_MXU = jnp.bfloat16
_ACT = jnp.bfloat16
_F32 = jnp.float32

_HEAD = 64
_CHUNK = 64
_LANES = 128
_TQ = 128
_BAND_TILES = 5
_BIAS_TILES = 9
_REL_CLIP = 256
_LN_EPS = 1e-5
_MASKED = -1e30
_EXP_ZERO_BELOW = -104.0
_VMEM_LIMIT = 56 * 1024 * 1024

_LR, _B1, _B2, _EPS, _WD, _STEP = 0.001, 0.9, 0.999, 1e-08, 0.01, 10

_MESH = pl.DeviceIdType.MESH


def _dot(a, b):
    return jnp.dot(a, b, preferred_element_type=_F32)


def _dot_nt(a, b):
    return lax.dot_general(a, b, (((1,), (1,)), ((), ())), preferred_element_type=_F32)


def _dot_tn(a, b):
    return lax.dot_general(a, b, (((0,), (0,)), ((), ())), preferred_element_type=_F32)


def _cparams(*sem):
    return pltpu.CompilerParams(dimension_semantics=sem, vmem_limit_bytes=_VMEM_LIMIT)


def _rows(t, c):
    return pl.BlockSpec((t, c), lambda i: (i, 0))


def _whole(shape):
    return pl.BlockSpec(shape, lambda i: tuple(0 for _ in shape))


_ANY = pl.BlockSpec(memory_space=pl.ANY)


def _load_cols(w_hbm, layer, w_vmem, sem):
    n = w_hbm.shape[-1]
    cps = [pltpu.make_async_copy(w_hbm.at[layer, k], w_vmem.at[:, pl.ds(k * n, n)], sem.at[k]) for k in range(4)]
    for cp in cps:
        cp.start()
    for cp in cps:
        cp.wait()


def _load_rows(w_hbm, layer, w_vmem, sem):
    r = w_hbm.shape[-2]
    cps = [pltpu.make_async_copy(w_hbm.at[layer, k], w_vmem.at[pl.ds(k * r, r), :], sem.at[k]) for k in range(4)]
    for cp in cps:
        cp.start()
    for cp in cps:
        cp.wait()


def _ln_stats(u):
    mu = jnp.mean(u, axis=-1, keepdims=True)
    xc = u - mu
    var = jnp.mean(xc * xc, axis=-1, keepdims=True)
    rstd = lax.rsqrt(var + _LN_EPS)
    return xc * rstd, rstd


def _ln_bwd(u, dy, gamma):
    xhat, rstd = _ln_stats(u)
    dxh = dy * gamma
    m1 = jnp.mean(dxh, axis=-1, keepdims=True)
    m2 = jnp.mean(dxh * xhat, axis=-1, keepdims=True)
    du = rstd * (dxh - m1 - xhat * m2)
    return du, jnp.sum(dy * xhat, axis=0, keepdims=True), jnp.sum(dy, axis=0, keepdims=True), xhat


def _divisor_tile(n, cap):
    best = None
    for t in range(_LANES, min(n, cap) + 1, _LANES):
        if n % t == 0:
            best = t
    return best or n


def _in_proj(x, w_in, layer):
    T, D = x.shape
    N = 4 * w_in.shape[-1]
    NQ = N - 2 * D
    tm = 256

    def kern(x_ref, w_hbm, hq_ref, hg_ref, w_v, sem):
        @pl.when(pl.program_id(0) == 0)
        def _():
            _load_cols(w_hbm, layer, w_v, sem)

        xb = x_ref[...].astype(_MXU)
        hq_ref[...] = _dot(xb, w_v[:, :NQ]).astype(hq_ref.dtype)
        hg_ref[...] = _dot(xb, w_v[:, NQ:])

    return pl.pallas_call(
        kern, name=f"in_proj_{layer}", grid=(T // tm,),
        in_specs=[_rows(tm, D), _ANY],
        out_specs=[_rows(tm, NQ), _rows(tm, 2 * D)],
        out_shape=[jax.ShapeDtypeStruct((T, NQ), _ACT), jax.ShapeDtypeStruct((T, 2 * D), _F32)],
        scratch_shapes=[pltpu.VMEM((D, N), w_in.dtype), pltpu.SemaphoreType.DMA((4,))],
        compiler_params=_cparams("arbitrary"),
    )(x, w_in)


def _mix_fwd(oa, ob, hg, x, wpa, wpb, wo, bg, gamma, beta, layer):
    T, D = x.shape
    WA, WB = oa.shape[1], ob.shape[1]
    alpha = float((2 * wo.shape[0]) ** 0.25)
    tm = 256

    def kern(oa_ref, ob_ref, hg_ref, x_ref, bg_ref, g_ref, b_ref, wpa_h, wpb_h, wo_h,
             x1_ref, u1_ref, pre_ref, ya_ref, yb_ref, wpa_v, wpb_v, wo_v, sa, sb, so):
        @pl.when(pl.program_id(0) == 0)
        def _():
            _load_cols(wpa_h, layer, wpa_v, sa)
            _load_cols(wpb_h, layer, wpb_v, sb)
            _load_rows(wo_h, layer, wo_v, so)

        ya = _dot(oa_ref[...].astype(_MXU), wpa_v[...])
        yb = _dot(ob_ref[...].astype(_MXU), wpb_v[...])
        hgv = hg_ref[...]
        bgv = bg_ref[...]
        ga = jax.nn.sigmoid(hgv[:, :D] + bgv[:, :D])
        gb = jax.nn.sigmoid(hgv[:, D:] + bgv[:, D:])
        pre = ga * ya + gb * yb
        mix = _dot(pre.astype(_MXU), wo_v[...])
        u = alpha * x_ref[...] + mix
        xhat, _ = _ln_stats(u)
        x1_ref[...] = xhat * g_ref[...] + b_ref[...]
        u1_ref[...] = u
        pre_ref[...] = pre.astype(pre_ref.dtype)
        ya_ref[...] = ya.astype(ya_ref.dtype)
        yb_ref[...] = yb.astype(yb_ref.dtype)

    return pl.pallas_call(
        kern, name=f"mix_fwd_{layer}", grid=(T // tm,),
        in_specs=[_rows(tm, WA), _rows(tm, WB), _rows(tm, 2 * D), _rows(tm, D),
                  _whole((1, 2 * D)), _whole((1, D)), _whole((1, D)), _ANY, _ANY, _ANY],
        out_specs=[_rows(tm, D)] * 5,
        out_shape=[jax.ShapeDtypeStruct((T, D), _F32), jax.ShapeDtypeStruct((T, D), _F32)]
        + [jax.ShapeDtypeStruct((T, D), _ACT)] * 3,
        scratch_shapes=[pltpu.VMEM((WA, D), wpa.dtype), pltpu.VMEM((WB, D), wpb.dtype), pltpu.VMEM((D, D), wo.dtype),
                        pltpu.SemaphoreType.DMA((4,)), pltpu.SemaphoreType.DMA((4,)), pltpu.SemaphoreType.DMA((4,))],
        compiler_params=_cparams("arbitrary"),
    )(oa, ob, hg, x, bg, gamma, beta, wpa, wpb, wo)


def _ffn_fwd(x1, wfi, wfo, gamma, beta, layer):
    T, D = x1.shape
    F2 = 4 * wfi.shape[-1]
    F = F2 // 2
    alpha = float((2 * wfi.shape[0]) ** 0.25)
    tm = 256
    fc = F // 2

    def kern(x_ref, g_ref, b_ref, wi_h, wo_h, x2_ref, u2_ref, act_ref, gu_ref, wi_v, wo_v, si, so):
        @pl.when(pl.program_id(0) == 0)
        def _():
            _load_cols(wi_h, layer, wi_v, si)
            _load_rows(wo_h, layer, wo_v, so)

        x = x_ref[...]
        xb = x.astype(_MXU)
        ffn = jnp.zeros((tm, D), _F32)
        for c in range(2):
            g = _dot(xb, wi_v[:, c * fc:(c + 1) * fc])
            u = _dot(xb, wi_v[:, F + c * fc:F + (c + 1) * fc])
            act = g * jax.nn.sigmoid(g) * u
            ab = act.astype(_MXU)
            ffn = ffn + _dot(ab, wo_v[c * fc:(c + 1) * fc, :])
            act_ref[:, c * fc:(c + 1) * fc] = ab.astype(act_ref.dtype)
            gu_ref[:, c * fc:(c + 1) * fc] = g.astype(gu_ref.dtype)
            gu_ref[:, F + c * fc:F + (c + 1) * fc] = u.astype(gu_ref.dtype)
        uu = alpha * x + ffn
        xhat, _ = _ln_stats(uu)
        x2_ref[...] = xhat * g_ref[...] + b_ref[...]
        u2_ref[...] = uu

    return pl.pallas_call(
        kern, name=f"ffn_fwd_{layer}", grid=(T // tm,),
        in_specs=[_rows(tm, D), _whole((1, D)), _whole((1, D)), _ANY, _ANY],
        out_specs=[_rows(tm, D), _rows(tm, D), _rows(tm, F), _rows(tm, F2)],
        out_shape=[jax.ShapeDtypeStruct((T, D), _F32), jax.ShapeDtypeStruct((T, D), _F32),
                   jax.ShapeDtypeStruct((T, F), _ACT), jax.ShapeDtypeStruct((T, F2), _ACT)],
        scratch_shapes=[pltpu.VMEM((D, F2), wfi.dtype), pltpu.VMEM((F, D), wfo.dtype),
                        pltpu.SemaphoreType.DMA((4,)), pltpu.SemaphoreType.DMA((4,))],
        compiler_params=_cparams("arbitrary"),
    )(x1, gamma, beta, wfi, wfo)


def _ffn_bwd_a(u2, dy_or_target, gu, gamma, beta, wfo, layer, last):
    T, D = u2.shape
    F2 = gu.shape[1]
    F = F2 // 2
    tm = 256
    fc = F // 2

    def kern(u_ref, dy_ref, gu_ref, g_ref, b_ref, wo_h, du_ref, dub_ref, dgu_ref, st_ref, wo_v, so):
        @pl.when(pl.program_id(0) == 0)
        def _():
            _load_rows(wo_h, layer, wo_v, so)
            st_ref[...] = jnp.zeros_like(st_ref)

        gam = g_ref[...]
        u = u_ref[...]
        if last:
            xhat0, _ = _ln_stats(u)
            err = xhat0 * gam + b_ref[...] - dy_ref[...]
            dy = err * (1.0 / D)
            st_ref[2:3, :] += jnp.sum(err * err, axis=0, keepdims=True)
        else:
            dy = dy_ref[...]
        du, dgam, dbet, _ = _ln_bwd(u, dy, gam)
        st_ref[0:1, :] += dgam
        st_ref[1:2, :] += dbet
        du_ref[...] = du
        dub = du.astype(_MXU)
        dub_ref[...] = dub.astype(dub_ref.dtype)
        for c in range(2):
            dact = _dot_nt(dub, wo_v[c * fc:(c + 1) * fc, :])
            g = gu_ref[:, c * fc:(c + 1) * fc].astype(_F32)
            uu = gu_ref[:, F + c * fc:F + (c + 1) * fc].astype(_F32)
            sg = jax.nn.sigmoid(g)
            dgu_ref[:, c * fc:(c + 1) * fc] = (dact * uu * (sg * (1.0 + g * (1.0 - sg)))).astype(dgu_ref.dtype)
            dgu_ref[:, F + c * fc:F + (c + 1) * fc] = (dact * (g * sg)).astype(dgu_ref.dtype)

    return pl.pallas_call(
        kern, name=f"ffn_bwd_a_{layer}", grid=(T // tm,),
        in_specs=[_rows(tm, D), _rows(tm, D), _rows(tm, F2), _whole((1, D)), _whole((1, D)), _ANY],
        out_specs=[_rows(tm, D), _rows(tm, D), _rows(tm, F2), _whole((8, D))],
        out_shape=[jax.ShapeDtypeStruct((T, D), _F32), jax.ShapeDtypeStruct((T, D), _ACT),
                   jax.ShapeDtypeStruct((T, F2), _ACT), jax.ShapeDtypeStruct((8, D), _F32)],
        scratch_shapes=[pltpu.VMEM((F, D), wfo.dtype), pltpu.SemaphoreType.DMA((4,))],
        compiler_params=_cparams("arbitrary"),
    )(u2, dy_or_target, gu, gamma, beta, wfo)


def _residual_nt(res, res_scale, d, w, layer, name):
    T, K = res.shape
    N = d.shape[1]
    tm = 256

    def kern(r_ref, d_ref, w_hbm, o_ref, w_v, sem):
        @pl.when(pl.program_id(0) == 0)
        def _():
            _load_cols(w_hbm, layer, w_v, sem)

        o_ref[...] = res_scale * r_ref[...] + _dot_nt(d_ref[...].astype(_MXU), w_v[...])

    return pl.pallas_call(
        kern, name=f"{name}_{layer}", grid=(T // tm,),
        in_specs=[_rows(tm, K), _rows(tm, N), _ANY],
        out_specs=_rows(tm, K),
        out_shape=jax.ShapeDtypeStruct((T, K), _F32),
        scratch_shapes=[pltpu.VMEM((K, N), w.dtype), pltpu.SemaphoreType.DMA((4,))],
        compiler_params=_cparams("arbitrary"),
    )(res, d, w)


def _mix_bwd(u1, dx1, ya, yb, hg, wpa, wpb, wo, bg, gamma, layer):
    T, D = u1.shape
    WA, WB = wpa.shape[-2], wpb.shape[-2]
    tm = 256

    def kern(u_ref, dx_ref, ya_ref, yb_ref, hg_ref, bg_ref, g_ref, wpa_h, wpb_h, wo_h,
             du_ref, dub_ref, dya_ref, dyb_ref, dhg_ref, doa_ref, dob_ref, st_ref,
             wpa_v, wpb_v, wo_v, sa, sb, so):
        @pl.when(pl.program_id(0) == 0)
        def _():
            _load_cols(wpa_h, layer, wpa_v, sa)
            _load_cols(wpb_h, layer, wpb_v, sb)
            _load_rows(wo_h, layer, wo_v, so)
            st_ref[...] = jnp.zeros_like(st_ref)

        du, dgam, dbet, _ = _ln_bwd(u_ref[...], dx_ref[...], g_ref[...])
        st_ref[1:2, :D] += dgam
        st_ref[1:2, D:] += dbet
        du_ref[...] = du
        dub = du.astype(_MXU)
        dub_ref[...] = dub.astype(dub_ref.dtype)
        dpre = _dot_nt(dub, wo_v[...])
        hgv = hg_ref[...]
        bgv = bg_ref[...]
        ga = jax.nn.sigmoid(hgv[:, :D] + bgv[:, :D])
        gb = jax.nn.sigmoid(hgv[:, D:] + bgv[:, D:])
        dya = (dpre * ga).astype(_MXU)
        dyb = (dpre * gb).astype(_MXU)
        dsa = dpre * ya_ref[...].astype(_F32) * (ga * (1.0 - ga))
        dsb = dpre * yb_ref[...].astype(_F32) * (gb * (1.0 - gb))
        st_ref[0:1, :D] += jnp.sum(dsa, axis=0, keepdims=True)
        st_ref[0:1, D:] += jnp.sum(dsb, axis=0, keepdims=True)
        dya_ref[...] = dya.astype(dya_ref.dtype)
        dyb_ref[...] = dyb.astype(dyb_ref.dtype)
        dhg_ref[:, :D] = dsa.astype(dhg_ref.dtype)
        dhg_ref[:, D:] = dsb.astype(dhg_ref.dtype)
        doa_ref[...] = _dot_nt(dya, wpa_v[...]).astype(doa_ref.dtype)
        dob_ref[...] = _dot_nt(dyb, wpb_v[...]).astype(dob_ref.dtype)

    return pl.pallas_call(
        kern, name=f"mix_bwd_{layer}", grid=(T // tm,),
        in_specs=[_rows(tm, D)] * 4 + [_rows(tm, 2 * D), _whole((1, 2 * D)), _whole((1, D)), _ANY, _ANY, _ANY],
        out_specs=[_rows(tm, D)] * 4 + [_rows(tm, 2 * D), _rows(tm, WA), _rows(tm, WB), _whole((8, 2 * D))],
        out_shape=[jax.ShapeDtypeStruct((T, D), _F32)] + [jax.ShapeDtypeStruct((T, D), _ACT)] * 3
        + [jax.ShapeDtypeStruct((T, 2 * D), _ACT), jax.ShapeDtypeStruct((T, WA), _ACT),
           jax.ShapeDtypeStruct((T, WB), _ACT), jax.ShapeDtypeStruct((8, 2 * D), _F32)],
        scratch_shapes=[pltpu.VMEM((WA, D), wpa.dtype), pltpu.VMEM((WB, D), wpb.dtype), pltpu.VMEM((D, D), wo.dtype),
                        pltpu.SemaphoreType.DMA((4,)), pltpu.SemaphoreType.DMA((4,)), pltpu.SemaphoreType.DMA((4,))],
        compiler_params=_cparams("arbitrary"),
    )(u1, dx1, ya, yb, hg, bg, gamma, wpa, wpb, wo)


def _grad_w(a, b, *, col_shards, name):
    T, M = a.shape
    N = b.shape[1]
    tk = 512
    tm = _divisor_tile(M, 512)
    n = N // 4 if col_shards else N
    tn = _divisor_tile(n, 1536)
    nt = n // tn
    nk = T // tk

    def kern(a_ref, b_ref, o_ref, acc):
        k = pl.program_id(2)

        @pl.when(k == 0)
        def _():
            acc[...] = jnp.zeros_like(acc)

        acc[...] += _dot_tn(a_ref[...].astype(_MXU), b_ref[...].astype(_MXU))

        @pl.when(k == nk - 1)
        def _():
            o_ref[...] = acc[...].astype(o_ref.dtype)

    if col_shards:
        out_spec = pl.BlockSpec((None, tm, tn), lambda i, j, k: (j // nt, i, j % nt))
        out_shape = jax.ShapeDtypeStruct((4, M, n), _ACT)
    else:
        out_spec = pl.BlockSpec((tm, tn), lambda i, j, k: (i, j))
        out_shape = jax.ShapeDtypeStruct((M, N), _ACT)
    return pl.pallas_call(
        kern, name=name, grid=(M // tm, N // tn, nk),
        in_specs=[pl.BlockSpec((tk, tm), lambda i, j, k: (k, i)), pl.BlockSpec((tk, tn), lambda i, j, k: (k, j))],
        out_specs=out_spec, out_shape=out_shape,
        scratch_shapes=[pltpu.VMEM((tm, tn), _F32)],
        compiler_params=_cparams("parallel", "parallel", "arbitrary"),
    )(a, b)


def _bias_tiles(rel):
    H = rel.shape[0]
    span = _TQ * _BAND_TILES - 1
    edge = span - _REL_CLIP
    gvec = jnp.concatenate([jnp.broadcast_to(rel[:, :1], (H, edge)), rel, jnp.broadcast_to(rel[:, -1:], (H, edge))], axis=1)
    grev = gvec[:, ::-1]
    width = _BIAS_TILES * _TQ
    rows = jnp.stack([grev[:, _TQ - 1 - r:_TQ - 1 - r + width] for r in range(_TQ)], axis=1)
    r = jnp.arange(_TQ)[:, None]
    u = jnp.arange(width)[None, :]
    d = 4 * _TQ + r - u
    rm = r % _CHUNK
    valid = (d >= rm - (_CHUNK - 1)) & (d <= rm + 8 * _CHUNK)
    tiles = jnp.where(valid[None], rows, _MASKED)
    return tiles.reshape(H, _TQ, _BIAS_TILES, _TQ).transpose(0, 2, 1, 3)


def _fold_bias_grad(db):
    H = db.shape[0]
    width = _BIAS_TILES * _TQ
    x = db.transpose(0, 2, 1, 3).reshape(H, _TQ, width)[:, ::-1, :]
    x = jnp.pad(x, ((0, 0), (0, 0), (0, _TQ)))
    skew = x.reshape(H, _TQ * (width + _TQ))[:, :_TQ * (width + _TQ - 1)].reshape(H, _TQ, width + _TQ - 1)
    dg = skew.sum(axis=1)[:, ::-1]
    span = _TQ * _BAND_TILES - 1
    edge = span - _REL_CLIP
    mid = dg[:, edge:edge + 2 * _REL_CLIP + 1]
    lo = dg[:, :edge].sum(axis=1)
    hi = dg[:, edge + 2 * _REL_CLIP + 1:].sum(axis=1)
    return mid.at[:, 0].add(lo).at[:, -1].add(hi)


def _band_window(i):
    j0 = jnp.maximum(i - (_BAND_TILES - 1), 0)
    return j0, (_BAND_TILES - 1) - (i - j0)


def _head_masks():
    lane = lax.broadcasted_iota(jnp.int32, (1, _LANES), 1)
    return [(lane // _HEAD) == hh for hh in range(2)]


def _band_probs(qm, k_ref, b_ref, hh, j0, boff):
    s = []
    for j in range(_BAND_TILES):
        kj = k_ref[pl.ds(pl.multiple_of((j0 + j) * _TQ, _TQ), _TQ), :]
        s.append(_dot_nt(qm, kj) * (_HEAD ** -0.5) + b_ref[hh, boff + j])
    m = functools.reduce(jnp.maximum, [jnp.max(x, axis=-1, keepdims=True) for x in s])
    p = [jnp.exp(x - m) for x in s]
    l = functools.reduce(lambda a, b: a + b, [jnp.sum(x, axis=-1, keepdims=True) for x in p])
    inv = 1.0 / l
    return [x * inv for x in p]


def _attn_a_fwd(hq, bias, col0, width, layer):
    T = hq.shape[0]
    npair = width // _LANES
    nq = T // _TQ
    cb = col0 // _LANES

    def kern(q_ref, k_ref, v_ref, b_ref, o_ref):
        i = pl.program_id(1)
        j0, boff = _band_window(i)
        masks = _head_masks()
        q = q_ref[...]
        outs = []
        for hh in range(2):
            qm = jnp.where(masks[hh], q, jnp.zeros_like(q))
            p = _band_probs(qm, k_ref, b_ref, hh, j0, boff)
            o = jnp.zeros((_TQ, _LANES), _F32)
            for j in range(_BAND_TILES):
                vj = v_ref[pl.ds(pl.multiple_of((j0 + j) * _TQ, _TQ), _TQ), :]
                o = o + _dot(p[j].astype(_MXU), vj)
            outs.append(o)
        o_ref[...] = jnp.where(masks[0], outs[0], outs[1]).astype(o_ref.dtype)

    return pl.pallas_call(
        kern, name=f"band_attn_fwd_{layer}", grid=(npair, nq),
        in_specs=[pl.BlockSpec((_TQ, _LANES), lambda h, i: (i, cb + h)),
                  pl.BlockSpec((T, _LANES), lambda h, i: (0, cb + npair + h)),
                  pl.BlockSpec((T, _LANES), lambda h, i: (0, cb + 2 * npair + h)),
                  pl.BlockSpec((2, _BIAS_TILES, _TQ, _TQ), lambda h, i: (h, 0, 0, 0))],
        out_specs=pl.BlockSpec((_TQ, _LANES), lambda h, i: (i, h)),
        out_shape=jax.ShapeDtypeStruct((T, width), _ACT),
        compiler_params=_cparams("arbitrary", "arbitrary"),
    )(hq, hq, hq, bias)


def _attn_a_bwd(hq, bias, do, col0, width, layer):
    T = hq.shape[0]
    npair = width // _LANES
    nq = T // _TQ
    cb = col0 // _LANES
    scale = _HEAD ** -0.5

    def kern(q_ref, k_ref, v_ref, b_ref, do_ref, dq_ref, dk_ref, dv_ref, db_ref, dk_acc, dv_acc):
        i = pl.program_id(1)

        @pl.when(i == 0)
        def _():
            dk_acc[...] = jnp.zeros_like(dk_acc)
            dv_acc[...] = jnp.zeros_like(dv_acc)
            db_ref[...] = jnp.zeros_like(db_ref)

        j0, boff = _band_window(i)
        masks = _head_masks()
        q = q_ref[...]
        do_t = do_ref[...]
        dqs = []
        for hh in range(2):
            qm = jnp.where(masks[hh], q, jnp.zeros_like(q))
            dom = jnp.where(masks[hh], do_t, jnp.zeros_like(do_t)).astype(_MXU)
            p = _band_probs(qm, k_ref, b_ref, hh, j0, boff)
            rows = [pl.ds(pl.multiple_of((j0 + j) * _TQ, _TQ), _TQ) for j in range(_BAND_TILES)]
            dp = [_dot_nt(dom, v_ref[rows[j], :]) for j in range(_BAND_TILES)]
            delta = functools.reduce(lambda a, b: a + b,
                                     [jnp.sum(p[j] * dp[j], axis=-1, keepdims=True) for j in range(_BAND_TILES)])
            dq = jnp.zeros((_TQ, _LANES), _F32)
            for j in range(_BAND_TILES):
                ds = p[j] * (dp[j] - delta)
                db_ref[hh, boff + j] += ds
                dsb = (ds * scale).astype(_MXU)
                dq = dq + _dot(dsb, k_ref[rows[j], :])
                dk_acc[rows[j], :] += _dot_tn(dsb, qm)
                dv_acc[rows[j], :] += _dot_tn(p[j].astype(_MXU), dom)
            dqs.append(dq)
        dq_ref[...] = jnp.where(masks[0], dqs[0], dqs[1]).astype(dq_ref.dtype)

        @pl.when(i == nq - 1)
        def _():
            dk_ref[...] = dk_acc[...].astype(dk_ref.dtype)
            dv_ref[...] = dv_acc[...].astype(dv_ref.dtype)

    H = 2 * npair
    return pl.pallas_call(
        kern, name=f"band_attn_bwd_{layer}", grid=(npair, nq),
        in_specs=[pl.BlockSpec((_TQ, _LANES), lambda h, i: (i, cb + h)),
                  pl.BlockSpec((T, _LANES), lambda h, i: (0, cb + npair + h)),
                  pl.BlockSpec((T, _LANES), lambda h, i: (0, cb + 2 * npair + h)),
                  pl.BlockSpec((2, _BIAS_TILES, _TQ, _TQ), lambda h, i: (h, 0, 0, 0)),
                  pl.BlockSpec((_TQ, _LANES), lambda h, i: (i, h))],
        out_specs=[pl.BlockSpec((_TQ, _LANES), lambda h, i: (i, h)),
                   pl.BlockSpec((T, _LANES), lambda h, i: (0, h)),
                   pl.BlockSpec((T, _LANES), lambda h, i: (0, h)),
                   pl.BlockSpec((2, _BIAS_TILES, _TQ, _TQ), lambda h, i: (h, 0, 0, 0))],
        out_shape=[jax.ShapeDtypeStruct((T, width), _ACT)] * 3
        + [jax.ShapeDtypeStruct((H, _BIAS_TILES, _TQ, _TQ), _F32)],
        scratch_shapes=[pltpu.VMEM((T, _LANES), _F32), pltpu.VMEM((T, _LANES), _F32)],
        compiler_params=_cparams("arbitrary", "arbitrary"),
    )(hq, hq, hq, bias, do)


def _suffix_matrix():
    r = lax.broadcasted_iota(jnp.int32, (_TQ, _TQ), 0)
    c = lax.broadcasted_iota(jnp.int32, (_TQ, _TQ), 1)
    return (r > c).astype(_MXU), c - r


def _suffix_sum(x, tri):
    hi = x.astype(_MXU)
    lo = (x - hi.astype(_F32)).astype(_MXU)
    return _dot(hi, tri) + _dot(lo, tri)


def _stick_tile(qm, kj, jj, rel, carry_l, tri):
    z = _dot_nt(qm, kj) * (_HEAD ** -0.5)
    mask = rel < jnp.where(jj == 0, 0, _TQ)
    sp = jnp.maximum(z, 0.0) + jnp.log(1.0 + jnp.exp(-jnp.abs(z)))
    L = jnp.where(mask, -sp, 0.0)
    logw = z + L + _suffix_sum(L, tri) + carry_l
    return z, L, logw, mask


def _sb_fwd(hq, col0, width, layer):
    T = hq.shape[0]
    npair = width // _LANES
    nq = T // _TQ
    cb = col0 // _LANES

    def kern(q_ref, k_ref, v_ref, o_ref):
        i = pl.program_id(1)
        masks = _head_masks()
        tri, rel = _suffix_matrix()
        q = q_ref[...]
        qms = [jnp.where(masks[hh], q, jnp.zeros_like(q)) for hh in range(2)]

        def cond(c):
            jj, cl0, cl1, _, _ = c
            return jnp.logical_and(jj <= i, jnp.maximum(jnp.max(cl0), jnp.max(cl1)) >= _EXP_ZERO_BELOW)

        def body(c):
            jj, cl0, cl1, a0, a1 = c
            rows = pl.ds(pl.multiple_of((i - jj) * _TQ, _TQ), _TQ)
            kj = k_ref[rows, :]
            vj = v_ref[rows, :]
            new = []
            for qm, cl, a in ((qms[0], cl0, a0), (qms[1], cl1, a1)):
                _, L, logw, mask = _stick_tile(qm, kj, jj, rel, cl, tri)
                w = jnp.where(mask, jnp.exp(logw), 0.0)
                new.append((cl + jnp.sum(L, axis=-1, keepdims=True), a + _dot(w.astype(_MXU), vj)))
            return jj + 1, new[0][0], new[1][0], new[0][1], new[1][1]

        zc = jnp.zeros((_TQ, 1), _F32)
        za = jnp.zeros((_TQ, _LANES), _F32)
        _, _, _, a0, a1 = lax.while_loop(cond, body, (jnp.int32(0), zc, zc, za, za))
        o_ref[...] = jnp.where(masks[0], a0, a1)

    return pl.pallas_call(
        kern, name=f"stick_attn_fwd_{layer}", grid=(npair, nq),
        in_specs=[pl.BlockSpec((_TQ, _LANES), lambda h, i: (i, cb + h)),
                  pl.BlockSpec((T, _LANES), lambda h, i: (0, cb + npair + h)),
                  pl.BlockSpec((T, _LANES), lambda h, i: (0, cb + 2 * npair + h))],
        out_specs=pl.BlockSpec((_TQ, _LANES), lambda h, i: (i, h)),
        out_shape=jax.ShapeDtypeStruct((T, width), _F32),
        compiler_params=_cparams("arbitrary", "arbitrary"),
    )(hq, hq, hq)


def _sb_bwd(hq, o, do, col0, width, layer):
    T = hq.shape[0]
    npair = width // _LANES
    nq = T // _TQ
    cb = col0 // _LANES
    scale = _HEAD ** -0.5

    def kern(q_ref, k_ref, v_ref, o_ref, do_ref, dq_ref, dk_ref, dv_ref, dk_acc, dv_acc):
        i = pl.program_id(1)

        @pl.when(i == 0)
        def _():
            dk_acc[...] = jnp.zeros_like(dk_acc)
            dv_acc[...] = jnp.zeros_like(dv_acc)

        masks = _head_masks()
        tri, rel = _suffix_matrix()
        q = q_ref[...]
        do_t = do_ref[...]
        prod = do_t.astype(_F32) * o_ref[...]
        qms, doms, dsum = [], [], []
        for hh in range(2):
            qms.append(jnp.where(masks[hh], q, jnp.zeros_like(q)))
            doms.append(jnp.where(masks[hh], do_t, jnp.zeros_like(do_t)).astype(_MXU))
            dsum.append(jnp.sum(jnp.where(masks[hh], prod, 0.0), axis=-1, keepdims=True))

        def cond(c):
            jj, cl0, cl1 = c[0], c[1], c[2]
            return jnp.logical_and(jj <= i, jnp.maximum(jnp.max(cl0), jnp.max(cl1)) >= _EXP_ZERO_BELOW)

        def body(c):
            jj, cl0, cl1, cg0, cg1, dq0, dq1 = c
            rows = pl.ds(pl.multiple_of((i - jj) * _TQ, _TQ), _TQ)
            kj = k_ref[rows, :]
            vj = v_ref[rows, :]
            new = []
            dk_t = jnp.zeros((_TQ, _LANES), _F32)
            dv_t = jnp.zeros((_TQ, _LANES), _F32)
            for hh, cl, cg, dq in ((0, cl0, cg0, dq0), (1, cl1, cg1, dq1)):
                z, L, logw, mask = _stick_tile(qms[hh], kj, jj, rel, cl, tri)
                wb = jnp.where(mask, jnp.exp(logw), 0.0).astype(_MXU)
                g = wb.astype(_F32) * _dot_nt(doms[hh], vj)
                gs = _suffix_sum(g, tri) + cg
                beta = jnp.exp(z + L)
                dz = jnp.where(mask, g - beta * (dsum[hh] - gs), 0.0)
                dzb = (dz * scale).astype(_MXU)
                dk_t = dk_t + _dot_tn(dzb, qms[hh])
                dv_t = dv_t + _dot_tn(wb, doms[hh])
                new.append((cl + jnp.sum(L, axis=-1, keepdims=True), cg + jnp.sum(g, axis=-1, keepdims=True),
                            dq + _dot(dzb, kj)))
            dk_acc[rows, :] += dk_t
            dv_acc[rows, :] += dv_t
            return jj + 1, new[0][0], new[1][0], new[0][1], new[1][1], new[0][2], new[1][2]

        zc = jnp.zeros((_TQ, 1), _F32)
        za = jnp.zeros((_TQ, _LANES), _F32)
        out = lax.while_loop(cond, body, (jnp.int32(0), zc, zc, zc, zc, za, za))
        dq_ref[...] = jnp.where(masks[0], out[5], out[6]).astype(dq_ref.dtype)

        @pl.when(i == nq - 1)
        def _():
            dk_ref[...] = dk_acc[...].astype(dk_ref.dtype)
            dv_ref[...] = dv_acc[...].astype(dv_ref.dtype)

    return pl.pallas_call(
        kern, name=f"stick_attn_bwd_{layer}", grid=(npair, nq),
        in_specs=[pl.BlockSpec((_TQ, _LANES), lambda h, i: (i, cb + h)),
                  pl.BlockSpec((T, _LANES), lambda h, i: (0, cb + npair + h)),
                  pl.BlockSpec((T, _LANES), lambda h, i: (0, cb + 2 * npair + h)),
                  pl.BlockSpec((_TQ, _LANES), lambda h, i: (i, h)),
                  pl.BlockSpec((_TQ, _LANES), lambda h, i: (i, h))],
        out_specs=[pl.BlockSpec((_TQ, _LANES), lambda h, i: (i, h)),
                   pl.BlockSpec((T, _LANES), lambda h, i: (0, h)),
                   pl.BlockSpec((T, _LANES), lambda h, i: (0, h))],
        out_shape=[jax.ShapeDtypeStruct((T, width), _ACT)] * 3,
        scratch_shapes=[pltpu.VMEM((T, _LANES), _F32), pltpu.VMEM((T, _LANES), _F32)],
        compiler_params=_cparams("arbitrary", "arbitrary"),
    )(hq, hq, hq, o, do)


_DENSE = ("w_in", "w_proj_a", "w_proj_b", "w_out", "w_ffn_in", "w_ffn_out")
_COL_SHARDED = {"w_in": True, "w_proj_a": True, "w_proj_b": True, "w_out": False, "w_ffn_in": True, "w_ffn_out": False}
_SMALL = ("b_gate", "rel_bias", "ln1_g", "ln1_b", "ln2_g", "ln2_b")


def _layer_fwd(x, W, small, l):
    D = x.shape[1]
    WA = W["w_proj_a"].shape[-2]
    WB = W["w_proj_b"].shape[-2]
    hq, hg = _in_proj(x, W["w_in"], l)
    bias = _bias_tiles(small["rel_bias"][l])
    oa = _attn_a_fwd(hq, bias, 0, WA, l)
    ob = _sb_fwd(hq, 3 * WA, WB, l)
    row = lambda v: v[l].reshape(1, -1)
    x1, u1, pre, ya, yb = _mix_fwd(oa, ob, hg, x, W["w_proj_a"], W["w_proj_b"], W["w_out"],
                                   row(small["b_gate"]), row(small["ln1_g"]), row(small["ln1_b"]), l)
    x2, u2, act, gu = _ffn_fwd(x1, W["w_ffn_in"], W["w_ffn_out"], row(small["ln2_g"]), row(small["ln2_b"]), l)
    return x2, dict(x=x, hq=hq, hg=hg, bias=bias, oa=oa, ob=ob, x1=x1, u1=u1, pre=pre, ya=ya, yb=yb, u2=u2, act=act, gu=gu)


def _layer_bwd(dy_or_target, S, W, small, l, last):
    D = S["x"].shape[1]
    WA = W["w_proj_a"].shape[-2]
    WB = W["w_proj_b"].shape[-2]
    row = lambda v: v[l].reshape(1, -1)
    du2, du2b, dgu, st2 = _ffn_bwd_a(S["u2"], dy_or_target, S["gu"], row(small["ln2_g"]), row(small["ln2_b"]),
                                     W["w_ffn_out"], l, last)
    dx1 = _residual_nt(du2, float((2 * W["w_in"].shape[0]) ** 0.25), dgu, W["w_ffn_in"], l, "ffn_bwd_b")
    gw = {}
    gw["w_ffn_in"] = _grad_w(S["x1"], dgu, col_shards=True, name=f"grad_w_ffn_in_{l}")
    gw["w_ffn_out"] = _grad_w(S["act"], du2b, col_shards=False, name=f"grad_w_ffn_out_{l}")
    du1, du1b, dya, dyb, dhg, doa, dob, st1 = _mix_bwd(S["u1"], dx1, S["ya"], S["yb"], S["hg"], W["w_proj_a"],
                                                       W["w_proj_b"], W["w_out"], row(small["b_gate"]), row(small["ln1_g"]), l)
    gw["w_out"] = _grad_w(S["pre"], du1b, col_shards=False, name=f"grad_w_out_{l}")
    gw["w_proj_a"] = _grad_w(S["oa"], dya, col_shards=True, name=f"grad_w_proj_a_{l}")
    gw["w_proj_b"] = _grad_w(S["ob"], dyb, col_shards=True, name=f"grad_w_proj_b_{l}")
    dqa, dka, dva, dbias = _attn_a_bwd(S["hq"], S["bias"], doa, 0, WA, l)
    dqb, dkb, dvb = _sb_bwd(S["hq"], S["ob"], dob, 3 * WA, WB, l)
    dh = jnp.concatenate([dqa, dka, dva, dqb, dkb, dvb, dhg], axis=1)
    dx = _residual_nt(du1, float((2 * W["w_in"].shape[0]) ** 0.25), dh, W["w_in"], l, "in_proj_bwd")
    gw["w_in"] = _grad_w(S["x"], dh, col_shards=True, name=f"grad_w_in_{l}")
    gs = dict(b_gate=st1[0], rel_bias=_fold_bias_grad(dbias), ln1_g=st1[1, :D], ln1_b=st1[1, D:],
              ln2_g=st2[0], ln2_b=st2[1])
    return dx, gw, gs, st2[2]


def _local_step(x, target, W, small):
    depth = W["w_in"].shape[0]
    saved = []
    h = x
    for l in range(depth):
        h, S = _layer_fwd(h, W, small, l)
        saved.append(S)
    gws, gss = [None] * depth, [None] * depth
    d = target
    sq = None
    for l in reversed(range(depth)):
        d, gws[l], gss[l], sq_l = _layer_bwd(d, saved[l], W, small, l, l == depth - 1)
        if l == depth - 1:
            sq = sq_l
    return sq, d, gws, gss


def _place():
    return lax.axis_index("x"), lax.axis_index("y"), lax.axis_index("c")


def _all_gather_weights(shards):
    nt = len(shards)

    def body(*refs):
        ins, outs = refs[:nt], refs[nt:2 * nt]
        send_sems, recv_sems, loc_sems = refs[2 * nt:]
        x, y, c = _place()
        k = 2 * x + y
        sibling = (x, y, 1 - c)
        chips = [(1 - x, y), (x, 1 - y), (1 - x, 1 - y)]

        def copy(t, s, layer, chip_k, to, src=None):
            block = outs[t].at[layer, chip_k]
            return pltpu.make_async_remote_copy(src_ref=block if src is None else src, dst_ref=block,
                                                send_sem=send_sems.at[t, s], recv_sem=recv_sems.at[t, s],
                                                device_id=to, device_id_type=_MESH)

        local = []
        for t in range(nt):
            for layer in range(2):
                cp = pltpu.make_async_copy(ins[t].at[layer], outs[t].at[layer, k], loc_sems.at[t, layer])
                cp.start()
                local.append(cp)
        sent = []
        for t in range(nt):
            for s, chip in enumerate(chips):
                cp = copy(t, s, c, k, (*chip, c), src=ins[t].at[c])
                cp.start()
                sent.append(cp)
        for t in range(nt):
            for s, chip in enumerate(chips):
                ck = 2 * chip[0] + chip[1]
                copy(t, s, c, ck, (x, y, c)).wait_recv()
                cp = copy(t, 3 + s, c, ck, sibling)
                cp.start()
                sent.append(cp)
        for t in range(nt):
            for s, chip in enumerate(chips):
                copy(t, 3 + s, 1 - c, 2 * chip[0] + chip[1], (x, y, c)).wait_recv()
        for cp in sent:
            cp.wait_send()
        for cp in local:
            cp.wait()

    return pl.pallas_call(
        body, name="all_gather_weights",
        in_specs=[_ANY] * nt, out_specs=[_ANY] * nt,
        out_shape=[jax.ShapeDtypeStruct((2, 4) + s.shape[1:], s.dtype) for s in shards],
        scratch_shapes=[pltpu.SemaphoreType.DMA((nt, 6)), pltpu.SemaphoreType.DMA((nt, 6)), pltpu.SemaphoreType.DMA((nt, 2))],
    )(*shards)


def _peer(x, y, c, r):
    px = 1 - x if r & 4 else x
    py = 1 - y if r & 2 else y
    pc = 1 - c if r & 1 else c
    return (px, py, pc), 4 * px + 2 * py + pc


def _scatter_grads(gws):
    nt = len(gws[0])

    def body(*refs):
        g0, g1, outs = refs[:nt], refs[nt:2 * nt], refs[2 * nt:3 * nt]
        send_sems, recv_sems, loc_sems = refs[3 * nt:]
        x, y, c = _place()
        me = 4 * x + 2 * y + c

        def copy(t, r, src, slot, to):
            return pltpu.make_async_remote_copy(src_ref=src, dst_ref=outs[t].at[slot],
                                                send_sem=send_sems.at[t, r - 1], recv_sem=recv_sems.at[t, r - 1],
                                                device_id=to, device_id_type=_MESH)

        for t in range(nt):
            for layer, g in ((0, g0), (1, g1)):
                @pl.when(c == layer)
                def _():
                    pltpu.make_async_copy(g[t].at[2 * x + y], outs[t].at[me], loc_sems.at[t]).start()
            for r in range(1, 8):
                to, _ = _peer(x, y, c, r)
                for layer, g in ((0, g0), (1, g1)):
                    @pl.when(to[2] == layer)
                    def _():
                        copy(t, r, g[t].at[2 * to[0] + to[1]], me, to).start()
        for t in range(nt):
            for r in range(1, 8):
                _, src_dev = _peer(x, y, c, r)
                copy(t, r, g0[t].at[0], src_dev, (x, y, c)).wait_recv()
        for t in range(nt):
            for r in range(1, 8):
                copy(t, r, g0[t].at[0], me, (x, y, c)).wait_send()
            pltpu.make_async_copy(g0[t].at[0], outs[t].at[me], loc_sems.at[t]).wait()

    flat = list(gws[0]) + list(gws[1])
    return pl.pallas_call(
        body, name="scatter_grads",
        in_specs=[_ANY] * (2 * nt), out_specs=[_ANY] * nt,
        out_shape=[jax.ShapeDtypeStruct((8,) + g.shape[1:], g.dtype) for g in gws[0]],
        scratch_shapes=[pltpu.SemaphoreType.DMA((nt, 7)), pltpu.SemaphoreType.DMA((nt, 7)), pltpu.SemaphoreType.DMA((nt,))],
    )(*flat)


def _sum_slots(st, name):
    _, K, n = st.shape
    tr = next(t for t in (256, 128, 64, 32, 16) if K % t == 0)

    def kern(s_ref, o_ref):
        acc = s_ref[0].astype(_F32)
        for d in range(1, 8):
            acc = acc + s_ref[d].astype(_F32)
        o_ref[...] = acc

    return pl.pallas_call(
        kern, name=name, grid=(K // tr,),
        in_specs=[pl.BlockSpec((8, tr, n), lambda i: (0, i, 0))], out_specs=_rows(tr, n),
        out_shape=jax.ShapeDtypeStruct((K, n), _F32),
        compiler_params=_cparams("parallel"),
    )(st)


def _pair_layers(halves):
    nt = len(halves)

    def body(*refs):
        ins, outs = refs[:nt], refs[nt:2 * nt]
        send_sems, recv_sems, loc_sems = refs[2 * nt:]
        x, y, c = _place()
        started = []
        for t in range(nt):
            lc = pltpu.make_async_copy(ins[t], outs[t].at[c], loc_sems.at[t])
            lc.start()
            rc = pltpu.make_async_remote_copy(src_ref=ins[t], dst_ref=outs[t].at[c], send_sem=send_sems.at[t],
                                              recv_sem=recv_sems.at[t], device_id=(x, y, 1 - c), device_id_type=_MESH)
            rc.start()
            started.append((lc, rc))
        for t in range(nt):
            pltpu.make_async_remote_copy(src_ref=ins[t], dst_ref=outs[t].at[1 - c], send_sem=send_sems.at[t],
                                         recv_sem=recv_sems.at[t], device_id=(x, y, c), device_id_type=_MESH).wait_recv()
        for lc, rc in started:
            rc.wait_send()
            lc.wait()

    return pl.pallas_call(
        body, name="pair_layers",
        in_specs=[_ANY] * nt, out_specs=[_ANY] * nt,
        out_shape=[jax.ShapeDtypeStruct((2,) + h.shape, h.dtype) for h in halves],
        scratch_shapes=[pltpu.SemaphoreType.DMA((nt,)), pltpu.SemaphoreType.DMA((nt,)), pltpu.SemaphoreType.DMA((nt,))],
    )(*halves)


def _all_reduce_small(p):
    R = p.shape[0]

    def body(p_ref, o_ref, stage, send_sems, recv_sems):
        x, y, c = _place()
        me = 4 * x + 2 * y + c
        stage[me] = p_ref[...]
        sent = []
        for r in range(1, 8):
            to, _ = _peer(x, y, c, r)
            cp = pltpu.make_async_remote_copy(src_ref=p_ref, dst_ref=stage.at[me], send_sem=send_sems.at[r - 1],
                                              recv_sem=recv_sems.at[r - 1], device_id=to, device_id_type=_MESH)
            cp.start()
            sent.append(cp)
        for r in range(1, 8):
            _, src_dev = _peer(x, y, c, r)
            pltpu.make_async_remote_copy(src_ref=p_ref, dst_ref=stage.at[src_dev], send_sem=send_sems.at[r - 1],
                                         recv_sem=recv_sems.at[r - 1], device_id=(x, y, c), device_id_type=_MESH).wait_recv()
        acc = stage[0]
        for d in range(1, 8):
            acc = acc + stage[d]
        o_ref[...] = acc
        for cp in sent:
            cp.wait_send()

    vm = pl.BlockSpec(memory_space=pltpu.VMEM)
    return pl.pallas_call(
        body, name="all_reduce_small",
        in_specs=[vm], out_specs=vm,
        out_shape=jax.ShapeDtypeStruct((R, _LANES), _F32),
        scratch_shapes=[pltpu.VMEM((8, R, _LANES), _F32), pltpu.SemaphoreType.DMA((7,)), pltpu.SemaphoreType.DMA((7,))],
    )(p)


def _adamw(w, g, m, v, name):
    shape = w.shape
    w2, g2, m2, v2 = (a.reshape(-1, shape[-1]) for a in (w, g, m, v))
    R, C = w2.shape
    tr = next((t for t in (256, 128, 64, 32, 16, 8) if R % t == 0), R)

    def kern(w_ref, g_ref, m_ref, v_ref, d_ref, nm_ref, nv_ref):
        gv = g_ref[...]
        nm = _B1 * m_ref[...] + (1.0 - _B1) * gv
        nv = _B2 * v_ref[...] + (1.0 - _B2) * (gv * gv)
        m_hat = nm / (1.0 - _B1 ** _STEP)
        v_hat = nv / (1.0 - _B2 ** _STEP)
        d_ref[...] = -_LR * (m_hat / (jnp.sqrt(v_hat) + _EPS) + _WD * w_ref[...])
        nm_ref[...] = nm
        nv_ref[...] = nv

    outs = pl.pallas_call(
        kern, name=name, grid=(R // tr,),
        in_specs=[_rows(tr, C)] * 4, out_specs=[_rows(tr, C)] * 3,
        out_shape=[jax.ShapeDtypeStruct((R, C), _F32)] * 3,
        compiler_params=_cparams("parallel"),
    )(w2, g2, m2, v2)
    return tuple(o.reshape(shape) for o in outs)


def _pack_small(gss, sq):
    parts = [gss[l][n].reshape(-1) for n in _SMALL for l in range(len(gss))] + [jnp.sum(sq).reshape(1)]
    flat = jnp.concatenate(parts)
    rows = -(-flat.shape[0] // (8 * _LANES)) * 8
    return jnp.pad(flat, (0, rows * _LANES - flat.shape[0])).reshape(rows, _LANES)


def _unpack_small(total, shapes):
    flat = total.reshape(-1)
    out, off = {}, 0
    for n in _SMALL:
        layers = []
        for _ in range(shapes[n][0]):
            size = 1
            for s in shapes[n][1:]:
                size *= s
            layers.append(flat[off:off + size].reshape(shapes[n][1:]))
            off += size
        out[n] = jnp.stack(layers)
    return out, flat[off]


def kernel(x, w_in, b_gate, rel_bias, w_proj_a, w_proj_b, w_out, ln1_g, ln1_b, w_ffn_in, w_ffn_out, ln2_g, ln2_b, loss_target, m_w_in, m_b_gate, m_rel_bias, m_w_proj_a, m_w_proj_b, m_w_out, m_ln1_g, m_ln1_b, m_w_ffn_in, m_w_ffn_out, m_ln2_g, m_ln2_b, v_w_in, v_b_gate, v_rel_bias, v_w_proj_a, v_w_proj_b, v_w_out, v_ln1_g, v_ln1_b, v_w_ffn_in, v_w_ffn_out, v_ln2_g, v_ln2_b):
    names = ("w_in", "b_gate", "rel_bias", "w_proj_a", "w_proj_b", "w_out", "ln1_g", "ln1_b", "w_ffn_in", "w_ffn_out", "ln2_g", "ln2_b")
    w = dict(zip(names, (w_in, b_gate, rel_bias, w_proj_a, w_proj_b, w_out, ln1_g, ln1_b, w_ffn_in, w_ffn_out, ln2_g, ln2_b)))
    m = dict(zip(names, (m_w_in, m_b_gate, m_rel_bias, m_w_proj_a, m_w_proj_b, m_w_out, m_ln1_g, m_ln1_b, m_w_ffn_in, m_w_ffn_out, m_ln2_g, m_ln2_b)))
    v = dict(zip(names, (v_w_in, v_b_gate, v_rel_bias, v_w_proj_a, v_w_proj_b, v_w_out, v_ln1_g, v_ln1_b, v_w_ffn_in, v_w_ffn_out, v_ln2_g, v_ln2_b)))
    T, D = x.shape[-2], x.shape[-1]

    gathered = _all_gather_weights([w[n].astype(_MXU) for n in _DENSE])
    W = dict(zip(_DENSE, gathered))
    small = {n: w[n] for n in _SMALL}
    sq, dx, gws, gss = _local_step(x.reshape(T, D), loss_target.reshape(T, D), W, small)

    def blocks(l, n):
        g = gws[l][n]
        return g if _COL_SHARDED[n] else g.reshape(4, g.shape[0] // 4, g.shape[1])

    slots = _scatter_grads([[blocks(l, n) for n in _DENSE] for l in range(2)])
    halves = [_sum_slots(s, f"sum_grad_{n}") for n, s in zip(_DENSE, slots)]
    grads = dict(zip(_DENSE, _pair_layers(halves)))
    total = _all_reduce_small(_pack_small(gss, sq))
    small_grads, sq_all = _unpack_small(total, {n: w[n].shape for n in _SMALL})
    grads.update(small_grads)
    loss = 0.5 * sq_all / D

    grad, delta, new_m, new_v = {}, {}, {}, {}
    for n in names:
        grad[n] = grads[n].reshape(w[n].shape)
        delta[n], new_m[n], new_v[n] = _adamw(w[n], grad[n], m[n], v[n], f"adamw_{n}")
    return (loss, dx.reshape(x.shape), *[grad[n] for n in names], *[delta[n] for n in names],
            *[new_m[n] for n in names], *[new_v[n] for n in names])
```

```python
import functools

import jax
import jax.numpy as jnp
from jax import lax
from jax.experimental import pallas as pl
from jax.experimental.pallas import tpu as pltpu

_MXU = jnp.bfloat16
_ACT = jnp.bfloat16
_F32 = jnp.float32

_HEAD = 64
_CHUNK = 64
_LANES = 128
_TQ = 128
_BAND_TILES = 5
_BIAS_TILES = 9
_REL_CLIP = 256
_LN_EPS = 1e-5
_MASKED = -1e30
_EXP_ZERO_BELOW = -104.0
_SB_WINDOW = 3
_VMEM_LIMIT = 56 * 1024 * 1024

_LR, _B1, _B2, _EPS, _WD, _STEP = 0.001, 0.9, 0.999, 1e-08, 0.01, 10

_MESH = pl.DeviceIdType.MESH


def _dot(a, b):
    return jnp.dot(a, b, preferred_element_type=_F32)


def _dot_nt(a, b):
    return lax.dot_general(a, b, (((1,), (1,)), ((), ())), preferred_element_type=_F32)


def _dot_tn(a, b):
    return lax.dot_general(a, b, (((0,), (0,)), ((), ())), preferred_element_type=_F32)


def _cparams(*sem):
    return pltpu.CompilerParams(dimension_semantics=sem, vmem_limit_bytes=_VMEM_LIMIT)


def _rows(t, c):
    return pl.BlockSpec((t, c), lambda i: (i, 0))


def _whole(shape):
    return pl.BlockSpec(shape, lambda i: tuple(0 for _ in shape))


_ANY = pl.BlockSpec(memory_space=pl.ANY)


def _load_cols(w_hbm, layer, w_vmem, sem):
    n = w_hbm.shape[-1]
    cps = [pltpu.make_async_copy(w_hbm.at[layer, k], w_vmem.at[:, pl.ds(k * n, n)], sem.at[k]) for k in range(4)]
    for cp in cps:
        cp.start()
    for cp in cps:
        cp.wait()


def _load_rows(w_hbm, layer, w_vmem, sem):
    r = w_hbm.shape[-2]
    cps = [pltpu.make_async_copy(w_hbm.at[layer, k], w_vmem.at[pl.ds(k * r, r), :], sem.at[k]) for k in range(4)]
    for cp in cps:
        cp.start()
    for cp in cps:
        cp.wait()


def _ln_stats(u):
    mu = jnp.mean(u, axis=-1, keepdims=True)
    xc = u - mu
    var = jnp.mean(xc * xc, axis=-1, keepdims=True)
    rstd = lax.rsqrt(var + _LN_EPS)
    return xc * rstd, rstd


def _ln_bwd(u, dy, gamma):
    xhat, rstd = _ln_stats(u)
    dxh = dy * gamma
    m1 = jnp.mean(dxh, axis=-1, keepdims=True)
    m2 = jnp.mean(dxh * xhat, axis=-1, keepdims=True)
    du = rstd * (dxh - m1 - xhat * m2)
    return du, jnp.sum(dy * xhat, axis=0, keepdims=True), jnp.sum(dy, axis=0, keepdims=True), xhat


def _divisor_tile(n, cap):
    best = None
    for t in range(_LANES, min(n, cap) + 1, _LANES):
        if n % t == 0:
            best = t
    return best or n


def _in_proj(x, w_in, layer):
    T, D = x.shape
    N = 4 * w_in.shape[-1]
    NQ = N - 2 * D
    tm = 256

    def kern(x_ref, w_hbm, hq_ref, hg_ref, w_v, sem):
        @pl.when(pl.program_id(0) == 0)
        def _():
            _load_cols(w_hbm, layer, w_v, sem)

        xb = x_ref[...].astype(_MXU)
        hq_ref[...] = _dot(xb, w_v[:, :NQ]).astype(hq_ref.dtype)
        hg_ref[...] = _dot(xb, w_v[:, NQ:])

    return pl.pallas_call(
        kern, name=f"in_proj_{layer}", grid=(T // tm,),
        in_specs=[_rows(tm, D), _ANY],
        out_specs=[_rows(tm, NQ), _rows(tm, 2 * D)],
        out_shape=[jax.ShapeDtypeStruct((T, NQ), _ACT), jax.ShapeDtypeStruct((T, 2 * D), _F32)],
        scratch_shapes=[pltpu.VMEM((D, N), w_in.dtype), pltpu.SemaphoreType.DMA((4,))],
        compiler_params=_cparams("arbitrary"),
    )(x, w_in)


def _mix_fwd(oa, ob, hg, x, wpa, wpb, wo, bg, gamma, beta, layer):
    T, D = x.shape
    WA, WB = oa.shape[1], ob.shape[1]
    alpha = float((2 * wo.shape[0]) ** 0.25)
    tm = 256

    def kern(oa_ref, ob_ref, hg_ref, x_ref, bg_ref, g_ref, b_ref, wpa_h, wpb_h, wo_h,
             x1_ref, u1_ref, pre_ref, ya_ref, yb_ref, wpa_v, wpb_v, wo_v, sa, sb, so):
        @pl.when(pl.program_id(0) == 0)
        def _():
            _load_cols(wpa_h, layer, wpa_v, sa)
            _load_cols(wpb_h, layer, wpb_v, sb)
            _load_rows(wo_h, layer, wo_v, so)

        ya = _dot(oa_ref[...].astype(_MXU), wpa_v[...])
        yb = _dot(ob_ref[...].astype(_MXU), wpb_v[...])
        hgv = hg_ref[...]
        bgv = bg_ref[...]
        ga = jax.nn.sigmoid(hgv[:, :D] + bgv[:, :D])
        gb = jax.nn.sigmoid(hgv[:, D:] + bgv[:, D:])
        pre = ga * ya + gb * yb
        mix = _dot(pre.astype(_MXU), wo_v[...])
        u = alpha * x_ref[...] + mix
        xhat, _ = _ln_stats(u)
        x1_ref[...] = xhat * g_ref[...] + b_ref[...]
        u1_ref[...] = u
        pre_ref[...] = pre.astype(pre_ref.dtype)
        ya_ref[...] = ya.astype(ya_ref.dtype)
        yb_ref[...] = yb.astype(yb_ref.dtype)

    return pl.pallas_call(
        kern, name=f"mix_fwd_{layer}", grid=(T // tm,),
        in_specs=[_rows(tm, WA), _rows(tm, WB), _rows(tm, 2 * D), _rows(tm, D),
                  _whole((1, 2 * D)), _whole((1, D)), _whole((1, D)), _ANY, _ANY, _ANY],
        out_specs=[_rows(tm, D)] * 5,
        out_shape=[jax.ShapeDtypeStruct((T, D), _F32), jax.ShapeDtypeStruct((T, D), _F32)]
        + [jax.ShapeDtypeStruct((T, D), _ACT)] * 3,
        scratch_shapes=[pltpu.VMEM((WA, D), wpa.dtype), pltpu.VMEM((WB, D), wpb.dtype), pltpu.VMEM((D, D), wo.dtype),
                        pltpu.SemaphoreType.DMA((4,)), pltpu.SemaphoreType.DMA((4,)), pltpu.SemaphoreType.DMA((4,))],
        compiler_params=_cparams("arbitrary"),
    )(oa, ob, hg, x, bg, gamma, beta, wpa, wpb, wo)


def _ffn_fwd(x1, wfi, wfo, gamma, beta, layer):
    T, D = x1.shape
    F2 = 4 * wfi.shape[-1]
    F = F2 // 2
    alpha = float((2 * wfi.shape[0]) ** 0.25)
    tm = 256
    fc = F // 2

    def kern(x_ref, g_ref, b_ref, wi_h, wo_h, x2_ref, u2_ref, act_ref, gu_ref, wi_v, wo_v, si, so):
        @pl.when(pl.program_id(0) == 0)
        def _():
            _load_cols(wi_h, layer, wi_v, si)
            _load_rows(wo_h, layer, wo_v, so)

        x = x_ref[...]
        xb = x.astype(_MXU)
        ffn = jnp.zeros((tm, D), _F32)
        for c in range(2):
            g = _dot(xb, wi_v[:, c * fc:(c + 1) * fc])
            u = _dot(xb, wi_v[:, F + c * fc:F + (c + 1) * fc])
            act = g * jax.nn.sigmoid(g) * u
            ab = act.astype(_MXU)
            ffn = ffn + _dot(ab, wo_v[c * fc:(c + 1) * fc, :])
            act_ref[:, c * fc:(c + 1) * fc] = ab.astype(act_ref.dtype)
            gu_ref[:, c * fc:(c + 1) * fc] = g.astype(gu_ref.dtype)
            gu_ref[:, F + c * fc:F + (c + 1) * fc] = u.astype(gu_ref.dtype)
        uu = alpha * x + ffn
        xhat, _ = _ln_stats(uu)
        x2_ref[...] = xhat * g_ref[...] + b_ref[...]
        u2_ref[...] = uu

    return pl.pallas_call(
        kern, name=f"ffn_fwd_{layer}", grid=(T // tm,),
        in_specs=[_rows(tm, D), _whole((1, D)), _whole((1, D)), _ANY, _ANY],
        out_specs=[_rows(tm, D), _rows(tm, D), _rows(tm, F), _rows(tm, F2)],
        out_shape=[jax.ShapeDtypeStruct((T, D), _F32), jax.ShapeDtypeStruct((T, D), _F32),
                   jax.ShapeDtypeStruct((T, F), _ACT), jax.ShapeDtypeStruct((T, F2), _ACT)],
        scratch_shapes=[pltpu.VMEM((D, F2), wfi.dtype), pltpu.VMEM((F, D), wfo.dtype),
                        pltpu.SemaphoreType.DMA((4,)), pltpu.SemaphoreType.DMA((4,))],
        compiler_params=_cparams("arbitrary"),
    )(x1, gamma, beta, wfi, wfo)


def _ffn_bwd_a(u2, dy_or_target, gu, gamma, beta, wfo, layer, last):
    T, D = u2.shape
    F2 = gu.shape[1]
    F = F2 // 2
    tm = 256
    fc = F // 2

    def kern(u_ref, dy_ref, gu_ref, g_ref, b_ref, wo_h, du_ref, dub_ref, dgu_ref, st_ref, wo_v, so):
        @pl.when(pl.program_id(0) == 0)
        def _():
            _load_rows(wo_h, layer, wo_v, so)
            st_ref[...] = jnp.zeros_like(st_ref)

        gam = g_ref[...]
        u = u_ref[...]
        if last:
            xhat0, _ = _ln_stats(u)
            err = xhat0 * gam + b_ref[...] - dy_ref[...]
            dy = err * (1.0 / D)
            st_ref[2:3, :] += jnp.sum(err * err, axis=0, keepdims=True)
        else:
            dy = dy_ref[...]
        du, dgam, dbet, _ = _ln_bwd(u, dy, gam)
        st_ref[0:1, :] += dgam
        st_ref[1:2, :] += dbet
        du_ref[...] = du
        dub = du.astype(_MXU)
        dub_ref[...] = dub.astype(dub_ref.dtype)
        for c in range(2):
            dact = _dot_nt(dub, wo_v[c * fc:(c + 1) * fc, :])
            g = gu_ref[:, c * fc:(c + 1) * fc].astype(_F32)
            uu = gu_ref[:, F + c * fc:F + (c + 1) * fc].astype(_F32)
            sg = jax.nn.sigmoid(g)
            dgu_ref[:, c * fc:(c + 1) * fc] = (dact * uu * (sg * (1.0 + g * (1.0 - sg)))).astype(dgu_ref.dtype)
            dgu_ref[:, F + c * fc:F + (c + 1) * fc] = (dact * (g * sg)).astype(dgu_ref.dtype)

    return pl.pallas_call(
        kern, name=f"ffn_bwd_a_{layer}", grid=(T // tm,),
        in_specs=[_rows(tm, D), _rows(tm, D), _rows(tm, F2), _whole((1, D)), _whole((1, D)), _ANY],
        out_specs=[_rows(tm, D), _rows(tm, D), _rows(tm, F2), _whole((8, D))],
        out_shape=[jax.ShapeDtypeStruct((T, D), _F32), jax.ShapeDtypeStruct((T, D), _ACT),
                   jax.ShapeDtypeStruct((T, F2), _ACT), jax.ShapeDtypeStruct((8, D), _F32)],
        scratch_shapes=[pltpu.VMEM((F, D), wfo.dtype), pltpu.SemaphoreType.DMA((4,))],
        compiler_params=_cparams("arbitrary"),
    )(u2, dy_or_target, gu, gamma, beta, wfo)


def _residual_nt(res, res_scale, d, w, layer, name):
    T, K = res.shape
    N = d.shape[1]
    tm = 256

    def kern(r_ref, d_ref, w_hbm, o_ref, w_v, sem):
        @pl.when(pl.program_id(0) == 0)
        def _():
            _load_cols(w_hbm, layer, w_v, sem)

        o_ref[...] = res_scale * r_ref[...] + _dot_nt(d_ref[...].astype(_MXU), w_v[...])

    return pl.pallas_call(
        kern, name=f"{name}_{layer}", grid=(T // tm,),
        in_specs=[_rows(tm, K), _rows(tm, N), _ANY],
        out_specs=_rows(tm, K),
        out_shape=jax.ShapeDtypeStruct((T, K), _F32),
        scratch_shapes=[pltpu.VMEM((K, N), w.dtype), pltpu.SemaphoreType.DMA((4,))],
        compiler_params=_cparams("arbitrary"),
    )(res, d, w)


def _mix_bwd(u1, dx1, ya, yb, hg, wpa, wpb, wo, bg, gamma, layer):
    T, D = u1.shape
    WA, WB = wpa.shape[-2], wpb.shape[-2]
    tm = 256

    def kern(u_ref, dx_ref, ya_ref, yb_ref, hg_ref, bg_ref, g_ref, wpa_h, wpb_h, wo_h,
             du_ref, dub_ref, dya_ref, dyb_ref, dhg_ref, doa_ref, dob_ref, st_ref,
             wpa_v, wpb_v, wo_v, sa, sb, so):
        @pl.when(pl.program_id(0) == 0)
        def _():
            _load_cols(wpa_h, layer, wpa_v, sa)
            _load_cols(wpb_h, layer, wpb_v, sb)
            _load_rows(wo_h, layer, wo_v, so)
            st_ref[...] = jnp.zeros_like(st_ref)

        du, dgam, dbet, _ = _ln_bwd(u_ref[...], dx_ref[...], g_ref[...])
        st_ref[1:2, :D] += dgam
        st_ref[1:2, D:] += dbet
        du_ref[...] = du
        dub = du.astype(_MXU)
        dub_ref[...] = dub.astype(dub_ref.dtype)
        dpre = _dot_nt(dub, wo_v[...])
        hgv = hg_ref[...]
        bgv = bg_ref[...]
        ga = jax.nn.sigmoid(hgv[:, :D] + bgv[:, :D])
        gb = jax.nn.sigmoid(hgv[:, D:] + bgv[:, D:])
        dya = (dpre * ga).astype(_MXU)
        dyb = (dpre * gb).astype(_MXU)
        dsa = dpre * ya_ref[...].astype(_F32) * (ga * (1.0 - ga))
        dsb = dpre * yb_ref[...].astype(_F32) * (gb * (1.0 - gb))
        st_ref[0:1, :D] += jnp.sum(dsa, axis=0, keepdims=True)
        st_ref[0:1, D:] += jnp.sum(dsb, axis=0, keepdims=True)
        dya_ref[...] = dya.astype(dya_ref.dtype)
        dyb_ref[...] = dyb.astype(dyb_ref.dtype)
        dhg_ref[:, :D] = dsa.astype(dhg_ref.dtype)
        dhg_ref[:, D:] = dsb.astype(dhg_ref.dtype)
        doa_ref[...] = _dot_nt(dya, wpa_v[...]).astype(doa_ref.dtype)
        dob_ref[...] = _dot_nt(dyb, wpb_v[...]).astype(dob_ref.dtype)

    return pl.pallas_call(
        kern, name=f"mix_bwd_{layer}", grid=(T // tm,),
        in_specs=[_rows(tm, D)] * 4 + [_rows(tm, 2 * D), _whole((1, 2 * D)), _whole((1, D)), _ANY, _ANY, _ANY],
        out_specs=[_rows(tm, D)] * 4 + [_rows(tm, 2 * D), _rows(tm, WA), _rows(tm, WB), _whole((8, 2 * D))],
        out_shape=[jax.ShapeDtypeStruct((T, D), _F32)] + [jax.ShapeDtypeStruct((T, D), _ACT)] * 3
        + [jax.ShapeDtypeStruct((T, 2 * D), _ACT), jax.ShapeDtypeStruct((T, WA), _ACT),
           jax.ShapeDtypeStruct((T, WB), _ACT), jax.ShapeDtypeStruct((8, 2 * D), _F32)],
        scratch_shapes=[pltpu.VMEM((WA, D), wpa.dtype), pltpu.VMEM((WB, D), wpb.dtype), pltpu.VMEM((D, D), wo.dtype),
                        pltpu.SemaphoreType.DMA((4,)), pltpu.SemaphoreType.DMA((4,)), pltpu.SemaphoreType.DMA((4,))],
        compiler_params=_cparams("arbitrary"),
    )(u1, dx1, ya, yb, hg, bg, gamma, wpa, wpb, wo)


def _grad_w(a, b, *, col_shards, name):
    T, M = a.shape
    N = b.shape[1]
    tk = 512
    tm = _divisor_tile(M, 512)
    n = N // 4 if col_shards else N
    tn = _divisor_tile(n, 1536)
    nt = n // tn
    nk = T // tk

    def kern(a_ref, b_ref, o_ref, acc):
        k = pl.program_id(2)

        @pl.when(k == 0)
        def _():
            acc[...] = jnp.zeros_like(acc)

        acc[...] += _dot_tn(a_ref[...].astype(_MXU), b_ref[...].astype(_MXU))

        @pl.when(k == nk - 1)
        def _():
            o_ref[...] = acc[...].astype(o_ref.dtype)

    if col_shards:
        out_spec = pl.BlockSpec((None, tm, tn), lambda i, j, k: (j // nt, i, j % nt))
        out_shape = jax.ShapeDtypeStruct((4, M, n), _ACT)
    else:
        out_spec = pl.BlockSpec((tm, tn), lambda i, j, k: (i, j))
        out_shape = jax.ShapeDtypeStruct((M, N), _ACT)
    return pl.pallas_call(
        kern, name=name, grid=(M // tm, N // tn, nk),
        in_specs=[pl.BlockSpec((tk, tm), lambda i, j, k: (k, i)), pl.BlockSpec((tk, tn), lambda i, j, k: (k, j))],
        out_specs=out_spec, out_shape=out_shape,
        scratch_shapes=[pltpu.VMEM((tm, tn), _F32)],
        compiler_params=_cparams("parallel", "parallel", "arbitrary"),
    )(a, b)


def _bias_tiles(rel):
    H = rel.shape[0]
    span = _TQ * _BAND_TILES - 1
    edge = span - _REL_CLIP
    gvec = jnp.concatenate([jnp.broadcast_to(rel[:, :1], (H, edge)), rel, jnp.broadcast_to(rel[:, -1:], (H, edge))], axis=1)
    width = _BIAS_TILES * _TQ
    period = width + _TQ
    tiled = jnp.broadcast_to(jnp.pad(gvec[:, ::-1], ((0, 0), (0, 1)))[:, None, :], (H, _TQ, period))
    rows = tiled.reshape(H, _TQ * period)[:, :_TQ * (period - 1)].reshape(H, _TQ, period - 1)[:, :, _TQ - 1:]
    r = jnp.arange(_TQ)[:, None]
    u = jnp.arange(width)[None, :]
    d = 4 * _TQ + r - u
    rm = r % _CHUNK
    valid = (d >= rm - (_CHUNK - 1)) & (d <= rm + 8 * _CHUNK)
    tiles = jnp.where(valid[None], rows, _MASKED)
    return tiles.reshape(H // 2, 2 * _TQ, _BIAS_TILES, _TQ).transpose(0, 2, 1, 3)


def _fold_bias_grad(db):
    H = 2 * db.shape[0]
    width = _BIAS_TILES * _TQ
    period = width + _TQ
    x = jnp.pad(db.transpose(0, 2, 1, 3).reshape(H, _TQ, width), ((0, 0), (0, 0), (_TQ - 1, 0)))
    skew = jnp.pad(x.reshape(H, _TQ * (period - 1)), ((0, 0), (0, _TQ))).reshape(H, _TQ, period)
    dg = skew.sum(axis=1)[:, :period - 1][:, ::-1]
    span = _TQ * _BAND_TILES - 1
    edge = span - _REL_CLIP
    mid = dg[:, edge:edge + 2 * _REL_CLIP + 1]
    lo = dg[:, :edge].sum(axis=1)
    hi = dg[:, edge + 2 * _REL_CLIP + 1:].sum(axis=1)
    return mid.at[:, 0].add(lo).at[:, -1].add(hi)


def _band_window(i):
    j0 = jnp.maximum(i - (_BAND_TILES - 1), 0)
    return j0, (_BAND_TILES - 1) - (i - j0)


def _head_masks():
    lane = lax.broadcasted_iota(jnp.int32, (1, _LANES), 1)
    return [(lane // _HEAD) == hh for hh in range(2)]


def _stack_heads(x, masks):
    return jnp.concatenate([jnp.where(m, x, jnp.zeros_like(x)) for m in masks], axis=0)


def _unstack_heads(y, masks):
    return jnp.where(masks[0], y[:_TQ], y[_TQ:])


def _scaled(q):
    return q * jnp.asarray(_HEAD ** -0.5, q.dtype)


def _band_probs(q2, k_ref, b_ref, j0, boff):
    s = []
    for j in range(_BAND_TILES):
        kj = k_ref[pl.ds(pl.multiple_of((j0 + j) * _TQ, _TQ), _TQ), :]
        s.append(_dot_nt(q2, kj) + b_ref[boff + j])
    m = jnp.max(functools.reduce(jnp.maximum, s), axis=-1, keepdims=True)
    p = [jnp.exp(x - m) for x in s]
    l = jnp.sum(functools.reduce(lambda a, b: a + b, p), axis=-1, keepdims=True)
    return p, 1.0 / l


def _attn_a_fwd(hq, bias, col0, width, layer):
    T = hq.shape[0]
    npair = width // _LANES
    nq = T // _TQ
    cb = col0 // _LANES

    def kern(q_ref, k_ref, v_ref, b_ref, o_ref):
        i = pl.program_id(1)
        j0, boff = _band_window(i)
        masks = _head_masks()
        q2 = _stack_heads(_scaled(q_ref[...]), masks)
        p, inv = _band_probs(q2, k_ref, b_ref, j0, boff)
        o = jnp.zeros((2 * _TQ, _LANES), _F32)
        for j in range(_BAND_TILES):
            vj = v_ref[pl.ds(pl.multiple_of((j0 + j) * _TQ, _TQ), _TQ), :]
            o = o + _dot(p[j].astype(_MXU), vj)
        o_ref[...] = _unstack_heads(o * inv, masks).astype(o_ref.dtype)

    return pl.pallas_call(
        kern, name=f"band_attn_fwd_{layer}", grid=(npair, nq),
        in_specs=[pl.BlockSpec((_TQ, _LANES), lambda h, i: (i, cb + h)),
                  pl.BlockSpec((T, _LANES), lambda h, i: (0, cb + npair + h)),
                  pl.BlockSpec((T, _LANES), lambda h, i: (0, cb + 2 * npair + h)),
                  pl.BlockSpec((None, _BIAS_TILES, 2 * _TQ, _TQ), lambda h, i: (h, 0, 0, 0))],
        out_specs=pl.BlockSpec((_TQ, _LANES), lambda h, i: (i, h)),
        out_shape=jax.ShapeDtypeStruct((T, width), _ACT),
        compiler_params=_cparams("arbitrary", "arbitrary"),
    )(hq, hq, hq, bias)


def _attn_a_bwd(hq, bias, do, col0, width, layer):
    T = hq.shape[0]
    npair = width // _LANES
    nq = T // _TQ
    cb = col0 // _LANES
    scale = _HEAD ** -0.5

    def kern(q_ref, k_ref, v_ref, b_ref, do_ref, dq_ref, dk_ref, dv_ref, db_ref, dk_acc, dv_acc):
        i = pl.program_id(1)

        @pl.when(i == 0)
        def _():
            dk_acc[...] = jnp.zeros_like(dk_acc)
            dv_acc[...] = jnp.zeros_like(dv_acc)
            db_ref[...] = jnp.zeros_like(db_ref)

        j0, boff = _band_window(i)
        masks = _head_masks()
        q2 = _stack_heads(_scaled(q_ref[...]), masks)
        do2 = _stack_heads(do_ref[...], masks).astype(_MXU)
        p, inv = _band_probs(q2, k_ref, b_ref, j0, boff)
        rows = [pl.ds(pl.multiple_of((j0 + j) * _TQ, _TQ), _TQ) for j in range(_BAND_TILES)]
        p = [x * inv for x in p]
        dp = [_dot_nt(do2, v_ref[rows[j], :]) for j in range(_BAND_TILES)]
        delta = jnp.sum(functools.reduce(lambda a, b: a + b, [p[j] * dp[j] for j in range(_BAND_TILES)]),
                        axis=-1, keepdims=True)
        dq = jnp.zeros((2 * _TQ, _LANES), _F32)
        for j in range(_BAND_TILES):
            ds = p[j] * (dp[j] - delta)
            db_ref[boff + j] += ds
            dsb = ds.astype(_MXU)
            dq = dq + _dot(dsb, k_ref[rows[j], :])
            dk_acc[rows[j], :] += _dot_tn(dsb, q2)
            dv_acc[rows[j], :] += _dot_tn(p[j].astype(_MXU), do2)
        dq_ref[...] = (_unstack_heads(dq, masks) * scale).astype(dq_ref.dtype)

        @pl.when(i == nq - 1)
        def _():
            dk_ref[...] = dk_acc[...].astype(dk_ref.dtype)
            dv_ref[...] = dv_acc[...].astype(dv_ref.dtype)

    strip = pl.BlockSpec((None, _BIAS_TILES, 2 * _TQ, _TQ), lambda h, i: (h, 0, 0, 0))
    return pl.pallas_call(
        kern, name=f"band_attn_bwd_{layer}", grid=(npair, nq),
        in_specs=[pl.BlockSpec((_TQ, _LANES), lambda h, i: (i, cb + h)),
                  pl.BlockSpec((T, _LANES), lambda h, i: (0, cb + npair + h)),
                  pl.BlockSpec((T, _LANES), lambda h, i: (0, cb + 2 * npair + h)),
                  strip,
                  pl.BlockSpec((_TQ, _LANES), lambda h, i: (i, h))],
        out_specs=[pl.BlockSpec((_TQ, _LANES), lambda h, i: (i, h)),
                   pl.BlockSpec((T, _LANES), lambda h, i: (0, h)),
                   pl.BlockSpec((T, _LANES), lambda h, i: (0, h)),
                   strip],
        out_shape=[jax.ShapeDtypeStruct((T, width), _ACT)] * 3
        + [jax.ShapeDtypeStruct((npair, _BIAS_TILES, 2 * _TQ, _TQ), _F32)],
        scratch_shapes=[pltpu.VMEM((T, _LANES), _F32), pltpu.VMEM((T, _LANES), _F32)],
        compiler_params=_cparams("arbitrary", "arbitrary"),
    )(hq, hq, hq, bias, do)


def _suffix_matrix():
    r = lax.broadcasted_iota(jnp.int32, (_TQ, _TQ), 0)
    c = lax.broadcasted_iota(jnp.int32, (_TQ, _TQ), 1)
    r2 = lax.broadcasted_iota(jnp.int32, (2 * _TQ, _TQ), 0)
    c2 = lax.broadcasted_iota(jnp.int32, (2 * _TQ, _TQ), 1)
    return (r > c).astype(_MXU), c2 - (r2 & (_TQ - 1))


def _suffix_sum(x, tri):
    n = x.shape[0]
    hi = x.astype(_MXU)
    lo = (x - hi.astype(_F32)).astype(_MXU)
    y = _dot(jnp.concatenate([hi, lo], axis=0), tri)
    return y[:n] + y[n:]


def _stick_tile(qs, kj, jj, rel, carry_l, tri):
    z = _dot_nt(qs, kj)
    nsp = -(jnp.maximum(z, 0.0) + jnp.log(1.0 + jnp.exp(-jnp.abs(z))))
    if isinstance(jj, int):
        mask = (rel < 0) if jj == 0 else None
    else:
        mask = rel < jnp.where(jj == 0, 0, _TQ)
    L = nsp if mask is None else jnp.where(mask, nsp, 0.0)
    w = jnp.exp(z + L + _suffix_sum(L, tri) + carry_l)
    if mask is not None:
        w = jnp.where(mask, w, 0.0)
    return z, L, w, mask


def _sweep_done(i, jj, cl):
    return jnp.logical_or(jj > i, jnp.max(cl) < _EXP_ZERO_BELOW)


def _sb_fwd(hq, col0, width, layer):
    T = hq.shape[0]
    npair = width // _LANES
    nq = T // _TQ
    cb = col0 // _LANES

    def kern(q_ref, k_ref, v_ref, o_ref):
        i = pl.program_id(1)
        masks = _head_masks()
        tri, rel = _suffix_matrix()
        q2 = _stack_heads(_scaled(q_ref[...]), masks)

        def tile(jj, c):
            cl, a = c
            rows = pl.ds(pl.multiple_of((i - jj) * _TQ, _TQ), _TQ)
            _, L, w, _ = _stick_tile(q2, k_ref[rows, :], jj, rel, cl, tri)
            return cl + jnp.sum(L, axis=-1, keepdims=True), a + _dot(w.astype(_MXU), v_ref[rows, :])

        zero = (jnp.zeros((2 * _TQ, 1), _F32), jnp.zeros((2 * _TQ, _LANES), _F32))

        def window():
            c = zero
            for jj in range(_SB_WINDOW):
                c = tile(jj, c)
            return (jnp.int32(_SB_WINDOW),) + c

        start = lax.cond(i >= _SB_WINDOW - 1, window, lambda: (jnp.int32(0),) + zero)
        out = lax.while_loop(lambda c: jnp.logical_not(_sweep_done(i, c[0], c[1])),
                             lambda c: (c[0] + 1,) + tile(c[0], c[1:]), start)
        o_ref[...] = _unstack_heads(out[2], masks)

    return pl.pallas_call(
        kern, name=f"stick_attn_fwd_{layer}", grid=(npair, nq),
        in_specs=[pl.BlockSpec((_TQ, _LANES), lambda h, i: (i, cb + h)),
                  pl.BlockSpec((T, _LANES), lambda h, i: (0, cb + npair + h)),
                  pl.BlockSpec((T, _LANES), lambda h, i: (0, cb + 2 * npair + h))],
        out_specs=pl.BlockSpec((_TQ, _LANES), lambda h, i: (i, h)),
        out_shape=jax.ShapeDtypeStruct((T, width), _F32),
        compiler_params=_cparams("arbitrary", "arbitrary"),
    )(hq, hq, hq)


def _sb_bwd(hq, o, do, col0, width, layer):
    T = hq.shape[0]
    npair = width // _LANES
    nq = T // _TQ
    cb = col0 // _LANES
    scale = _HEAD ** -0.5

    def kern(q_ref, k_ref, v_ref, o_ref, do_ref, dq_ref, dk_ref, dv_ref, dk_acc, dv_acc):
        i = pl.program_id(1)

        @pl.when(i == 0)
        def _():
            dk_acc[...] = jnp.zeros_like(dk_acc)
            dv_acc[...] = jnp.zeros_like(dv_acc)

        masks = _head_masks()
        tri, rel = _suffix_matrix()
        q2 = _stack_heads(_scaled(q_ref[...]), masks)
        do_t = do_ref[...]
        do2 = _stack_heads(do_t, masks).astype(_MXU)
        dsum = jnp.sum(_stack_heads(do_t.astype(_F32) * o_ref[...], masks), axis=-1, keepdims=True)

        def tile(jj, c):
            cl, cg, dq = c
            rows = pl.ds(pl.multiple_of((i - jj) * _TQ, _TQ), _TQ)
            kj = k_ref[rows, :]
            vj = v_ref[rows, :]
            z, L, w, mask = _stick_tile(q2, kj, jj, rel, cl, tri)
            wb = w.astype(_MXU)
            g = wb.astype(_F32) * _dot_nt(do2, vj)
            gs = _suffix_sum(g, tri) + cg
            dz = g - jnp.exp(z + L) * (dsum - gs)
            if mask is not None:
                dz = jnp.where(mask, dz, 0.0)
            dzb = dz.astype(_MXU)
            dk_acc[rows, :] += _dot_tn(dzb, q2)
            dv_acc[rows, :] += _dot_tn(wb, do2)
            return (cl + jnp.sum(L, axis=-1, keepdims=True), cg + jnp.sum(g, axis=-1, keepdims=True),
                    dq + _dot(dzb, kj))

        zc = jnp.zeros((2 * _TQ, 1), _F32)
        zero = (zc, zc, jnp.zeros((2 * _TQ, _LANES), _F32))

        def window():
            c = zero
            for jj in range(_SB_WINDOW):
                c = tile(jj, c)
            return (jnp.int32(_SB_WINDOW),) + c

        start = lax.cond(i >= _SB_WINDOW - 1, window, lambda: (jnp.int32(0),) + zero)
        out = lax.while_loop(lambda c: jnp.logical_not(_sweep_done(i, c[0], c[1])),
                             lambda c: (c[0] + 1,) + tile(c[0], c[1:]), start)
        dq_ref[...] = (_unstack_heads(out[3], masks) * scale).astype(dq_ref.dtype)

        @pl.when(i == nq - 1)
        def _():
            dk_ref[...] = dk_acc[...].astype(dk_ref.dtype)
            dv_ref[...] = dv_acc[...].astype(dv_ref.dtype)

    return pl.pallas_call(
        kern, name=f"stick_attn_bwd_{layer}", grid=(npair, nq),
        in_specs=[pl.BlockSpec((_TQ, _LANES), lambda h, i: (i, cb + h)),
                  pl.BlockSpec((T, _LANES), lambda h, i: (0, cb + npair + h)),
                  pl.BlockSpec((T, _LANES), lambda h, i: (0, cb + 2 * npair + h)),
                  pl.BlockSpec((_TQ, _LANES), lambda h, i: (i, h)),
                  pl.BlockSpec((_TQ, _LANES), lambda h, i: (i, h))],
        out_specs=[pl.BlockSpec((_TQ, _LANES), lambda h, i: (i, h)),
                   pl.BlockSpec((T, _LANES), lambda h, i: (0, h)),
                   pl.BlockSpec((T, _LANES), lambda h, i: (0, h))],
        out_shape=[jax.ShapeDtypeStruct((T, width), _ACT)] * 3,
        scratch_shapes=[pltpu.VMEM((T, _LANES), _F32), pltpu.VMEM((T, _LANES), _F32)],
        compiler_params=_cparams("arbitrary", "arbitrary"),
    )(hq, hq, hq, o, do)


_DENSE = ("w_in", "w_proj_a", "w_proj_b", "w_out", "w_ffn_in", "w_ffn_out")
_COL_SHARDED = {"w_in": True, "w_proj_a": True, "w_proj_b": True, "w_out": False, "w_ffn_in": True, "w_ffn_out": False}
_SMALL = ("b_gate", "rel_bias", "ln1_g", "ln1_b", "ln2_g", "ln2_b")


def _layer_fwd(x, W, small, l):
    D = x.shape[1]
    WA = W["w_proj_a"].shape[-2]
    WB = W["w_proj_b"].shape[-2]
    hq, hg = _in_proj(x, W["w_in"], l)
    bias = _bias_tiles(small["rel_bias"][l])
    oa = _attn_a_fwd(hq, bias, 0, WA, l)
    ob = _sb_fwd(hq, 3 * WA, WB, l)
    row = lambda v: v[l].reshape(1, -1)
    x1, u1, pre, ya, yb = _mix_fwd(oa, ob, hg, x, W["w_proj_a"], W["w_proj_b"], W["w_out"],
                                   row(small["b_gate"]), row(small["ln1_g"]), row(small["ln1_b"]), l)
    x2, u2, act, gu = _ffn_fwd(x1, W["w_ffn_in"], W["w_ffn_out"], row(small["ln2_g"]), row(small["ln2_b"]), l)
    return x2, dict(x=x, hq=hq, hg=hg, bias=bias, oa=oa, ob=ob, x1=x1, u1=u1, pre=pre, ya=ya, yb=yb, u2=u2, act=act, gu=gu)


def _layer_bwd(dy_or_target, S, W, small, l, last):
    D = S["x"].shape[1]
    WA = W["w_proj_a"].shape[-2]
    WB = W["w_proj_b"].shape[-2]
    row = lambda v: v[l].reshape(1, -1)
    du2, du2b, dgu, st2 = _ffn_bwd_a(S["u2"], dy_or_target, S["gu"], row(small["ln2_g"]), row(small["ln2_b"]),
                                     W["w_ffn_out"], l, last)
    dx1 = _residual_nt(du2, float((2 * W["w_in"].shape[0]) ** 0.25), dgu, W["w_ffn_in"], l, "ffn_bwd_b")
    gw = {}
    gw["w_ffn_in"] = _grad_w(S["x1"], dgu, col_shards=True, name=f"grad_w_ffn_in_{l}")
    gw["w_ffn_out"] = _grad_w(S["act"], du2b, col_shards=False, name=f"grad_w_ffn_out_{l}")
    du1, du1b, dya, dyb, dhg, doa, dob, st1 = _mix_bwd(S["u1"], dx1, S["ya"], S["yb"], S["hg"], W["w_proj_a"],
                                                       W["w_proj_b"], W["w_out"], row(small["b_gate"]), row(small["ln1_g"]), l)
    gw["w_out"] = _grad_w(S["pre"], du1b, col_shards=False, name=f"grad_w_out_{l}")
    gw["w_proj_a"] = _grad_w(S["oa"], dya, col_shards=True, name=f"grad_w_proj_a_{l}")
    gw["w_proj_b"] = _grad_w(S["ob"], dyb, col_shards=True, name=f"grad_w_proj_b_{l}")
    dqa, dka, dva, dbias = _attn_a_bwd(S["hq"], S["bias"], doa, 0, WA, l)
    dqb, dkb, dvb = _sb_bwd(S["hq"], S["ob"], dob, 3 * WA, WB, l)
    dh = jnp.concatenate([dqa, dka, dva, dqb, dkb, dvb, dhg], axis=1)
    dx = _residual_nt(du1, float((2 * W["w_in"].shape[0]) ** 0.25), dh, W["w_in"], l, "in_proj_bwd")
    gw["w_in"] = _grad_w(S["x"], dh, col_shards=True, name=f"grad_w_in_{l}")
    gs = dict(b_gate=st1[0], rel_bias=_fold_bias_grad(dbias), ln1_g=st1[1, :D], ln1_b=st1[1, D:],
              ln2_g=st2[0], ln2_b=st2[1])
    return dx, gw, gs, st2[2]


def _local_step(x, target, W, small):
    depth = W["w_in"].shape[0]
    saved = []
    h = x
    for l in range(depth):
        h, S = _layer_fwd(h, W, small, l)
        saved.append(S)
    gws, gss = [None] * depth, [None] * depth
    d = target
    sq = None
    for l in reversed(range(depth)):
        d, gws[l], gss[l], sq_l = _layer_bwd(d, saved[l], W, small, l, l == depth - 1)
        if l == depth - 1:
            sq = sq_l
    return sq, d, gws, gss


def _place():
    return lax.axis_index("x"), lax.axis_index("y"), lax.axis_index("c")


def _all_gather_weights(shards):
    nt = len(shards)

    def body(*refs):
        ins, outs = refs[:nt], refs[nt:2 * nt]
        send_sems, recv_sems, loc_sems = refs[2 * nt:]
        x, y, c = _place()
        k = 2 * x + y
        sibling = (x, y, 1 - c)
        chips = [(1 - x, y), (x, 1 - y), (1 - x, 1 - y)]

        def copy(t, s, layer, chip_k, to, src=None):
            block = outs[t].at[layer, chip_k]
            return pltpu.make_async_remote_copy(src_ref=block if src is None else src, dst_ref=block,
                                                send_sem=send_sems.at[t, s], recv_sem=recv_sems.at[t, s],
                                                device_id=to, device_id_type=_MESH)

        local = []
        for t in range(nt):
            for layer in range(2):
                cp = pltpu.make_async_copy(ins[t].at[layer], outs[t].at[layer, k], loc_sems.at[t, layer])
                cp.start()
                local.append(cp)
        sent = []
        for t in range(nt):
            for s, chip in enumerate(chips):
                cp = copy(t, s, c, k, (*chip, c), src=ins[t].at[c])
                cp.start()
                sent.append(cp)
        for t in range(nt):
            for s, chip in enumerate(chips):
                ck = 2 * chip[0] + chip[1]
                copy(t, s, c, ck, (x, y, c)).wait_recv()
                cp = copy(t, 3 + s, c, ck, sibling)
                cp.start()
                sent.append(cp)
        for t in range(nt):
            for s, chip in enumerate(chips):
                copy(t, 3 + s, 1 - c, 2 * chip[0] + chip[1], (x, y, c)).wait_recv()
        for cp in sent:
            cp.wait_send()
        for cp in local:
            cp.wait()

    return pl.pallas_call(
        body, name="all_gather_weights",
        in_specs=[_ANY] * nt, out_specs=[_ANY] * nt,
        out_shape=[jax.ShapeDtypeStruct((2, 4) + s.shape[1:], s.dtype) for s in shards],
        scratch_shapes=[pltpu.SemaphoreType.DMA((nt, 6)), pltpu.SemaphoreType.DMA((nt, 6)), pltpu.SemaphoreType.DMA((nt, 2))],
    )(*shards)


def _peer(x, y, c, r):
    px = 1 - x if r & 4 else x
    py = 1 - y if r & 2 else y
    pc = 1 - c if r & 1 else c
    return (px, py, pc), 4 * px + 2 * py + pc


def _scatter_grads(gws):
    nt = len(gws[0])

    def body(*refs):
        g0, g1, outs = refs[:nt], refs[nt:2 * nt], refs[2 * nt:3 * nt]
        send_sems, recv_sems, loc_sems = refs[3 * nt:]
        x, y, c = _place()
        me = 4 * x + 2 * y + c

        def copy(t, r, src, slot, to):
            return pltpu.make_async_remote_copy(src_ref=src, dst_ref=outs[t].at[slot],
                                                send_sem=send_sems.at[t, r - 1], recv_sem=recv_sems.at[t, r - 1],
                                                device_id=to, device_id_type=_MESH)

        for t in range(nt):
            for layer, g in ((0, g0), (1, g1)):
                @pl.when(c == layer)
                def _():
                    pltpu.make_async_copy(g[t].at[2 * x + y], outs[t].at[me], loc_sems.at[t]).start()
            for r in range(1, 8):
                to, _ = _peer(x, y, c, r)
                for layer, g in ((0, g0), (1, g1)):
                    @pl.when(to[2] == layer)
                    def _():
                        copy(t, r, g[t].at[2 * to[0] + to[1]], me, to).start()
        for t in range(nt):
            for r in range(1, 8):
                _, src_dev = _peer(x, y, c, r)
                copy(t, r, g0[t].at[0], src_dev, (x, y, c)).wait_recv()
        for t in range(nt):
            for r in range(1, 8):
                copy(t, r, g0[t].at[0], me, (x, y, c)).wait_send()
            pltpu.make_async_copy(g0[t].at[0], outs[t].at[me], loc_sems.at[t]).wait()

    flat = list(gws[0]) + list(gws[1])
    return pl.pallas_call(
        body, name="scatter_grads",
        in_specs=[_ANY] * (2 * nt), out_specs=[_ANY] * nt,
        out_shape=[jax.ShapeDtypeStruct((8,) + g.shape[1:], g.dtype) for g in gws[0]],
        scratch_shapes=[pltpu.SemaphoreType.DMA((nt, 7)), pltpu.SemaphoreType.DMA((nt, 7)), pltpu.SemaphoreType.DMA((nt,))],
    )(*flat)


def _sum_slots(st, name):
    _, K, n = st.shape
    tr = next(t for t in (256, 128, 64, 32, 16) if K % t == 0)

    def kern(s_ref, o_ref):
        acc = s_ref[0].astype(_F32)
        for d in range(1, 8):
            acc = acc + s_ref[d].astype(_F32)
        o_ref[...] = acc

    return pl.pallas_call(
        kern, name=name, grid=(K // tr,),
        in_specs=[pl.BlockSpec((8, tr, n), lambda i: (0, i, 0))], out_specs=_rows(tr, n),
        out_shape=jax.ShapeDtypeStruct((K, n), _F32),
        compiler_params=_cparams("parallel"),
    )(st)


def _pair_layers(halves):
    nt = len(halves)

    def body(*refs):
        ins, outs = refs[:nt], refs[nt:2 * nt]
        send_sems, recv_sems, loc_sems = refs[2 * nt:]
        x, y, c = _place()
        started = []
        for t in range(nt):
            lc = pltpu.make_async_copy(ins[t], outs[t].at[c], loc_sems.at[t])
            lc.start()
            rc = pltpu.make_async_remote_copy(src_ref=ins[t], dst_ref=outs[t].at[c], send_sem=send_sems.at[t],
                                              recv_sem=recv_sems.at[t], device_id=(x, y, 1 - c), device_id_type=_MESH)
            rc.start()
            started.append((lc, rc))
        for t in range(nt):
            pltpu.make_async_remote_copy(src_ref=ins[t], dst_ref=outs[t].at[1 - c], send_sem=send_sems.at[t],
                                         recv_sem=recv_sems.at[t], device_id=(x, y, c), device_id_type=_MESH).wait_recv()
        for lc, rc in started:
            rc.wait_send()
            lc.wait()

    return pl.pallas_call(
        body, name="pair_layers",
        in_specs=[_ANY] * nt, out_specs=[_ANY] * nt,
        out_shape=[jax.ShapeDtypeStruct((2,) + h.shape, h.dtype) for h in halves],
        scratch_shapes=[pltpu.SemaphoreType.DMA((nt,)), pltpu.SemaphoreType.DMA((nt,)), pltpu.SemaphoreType.DMA((nt,))],
    )(*halves)


def _all_reduce_small(p):
    R = p.shape[0]

    def body(p_ref, o_ref, stage, send_sems, recv_sems):
        x, y, c = _place()
        me = 4 * x + 2 * y + c
        stage[me] = p_ref[...]
        sent = []
        for r in range(1, 8):
            to, _ = _peer(x, y, c, r)
            cp = pltpu.make_async_remote_copy(src_ref=p_ref, dst_ref=stage.at[me], send_sem=send_sems.at[r - 1],
                                              recv_sem=recv_sems.at[r - 1], device_id=to, device_id_type=_MESH)
            cp.start()
            sent.append(cp)
        for r in range(1, 8):
            _, src_dev = _peer(x, y, c, r)
            pltpu.make_async_remote_copy(src_ref=p_ref, dst_ref=stage.at[src_dev], send_sem=send_sems.at[r - 1],
                                         recv_sem=recv_sems.at[r - 1], device_id=(x, y, c), device_id_type=_MESH).wait_recv()
        acc = stage[0]
        for d in range(1, 8):
            acc = acc + stage[d]
        o_ref[...] = acc
        for cp in sent:
            cp.wait_send()

    vm = pl.BlockSpec(memory_space=pltpu.VMEM)
    return pl.pallas_call(
        body, name="all_reduce_small",
        in_specs=[vm], out_specs=vm,
        out_shape=jax.ShapeDtypeStruct((R, _LANES), _F32),
        scratch_shapes=[pltpu.VMEM((8, R, _LANES), _F32), pltpu.SemaphoreType.DMA((7,)), pltpu.SemaphoreType.DMA((7,))],
    )(p)


def _adamw(w, g, m, v, name):
    shape = w.shape
    w2, g2, m2, v2 = (a.reshape(-1, shape[-1]) for a in (w, g, m, v))
    R, C = w2.shape
    tr = next((t for t in (256, 128, 64, 32, 16, 8) if R % t == 0), R)

    def kern(w_ref, g_ref, m_ref, v_ref, d_ref, nm_ref, nv_ref):
        gv = g_ref[...]
        nm = _B1 * m_ref[...] + (1.0 - _B1) * gv
        nv = _B2 * v_ref[...] + (1.0 - _B2) * (gv * gv)
        m_hat = nm / (1.0 - _B1 ** _STEP)
        v_hat = nv / (1.0 - _B2 ** _STEP)
        d_ref[...] = -_LR * (m_hat / (jnp.sqrt(v_hat) + _EPS) + _WD * w_ref[...])
        nm_ref[...] = nm
        nv_ref[...] = nv

    outs = pl.pallas_call(
        kern, name=name, grid=(R // tr,),
        in_specs=[_rows(tr, C)] * 4, out_specs=[_rows(tr, C)] * 3,
        out_shape=[jax.ShapeDtypeStruct((R, C), _F32)] * 3,
        compiler_params=_cparams("parallel"),
    )(w2, g2, m2, v2)
    return tuple(o.reshape(shape) for o in outs)


def _pack_small(gss, sq):
    parts = [gss[l][n].reshape(-1) for n in _SMALL for l in range(len(gss))] + [jnp.sum(sq).reshape(1)]
    flat = jnp.concatenate(parts)
    rows = -(-flat.shape[0] // (8 * _LANES)) * 8
    return jnp.pad(flat, (0, rows * _LANES - flat.shape[0])).reshape(rows, _LANES)


def _unpack_small(total, shapes):
    flat = total.reshape(-1)
    out, off = {}, 0
    for n in _SMALL:
        layers = []
        for _ in range(shapes[n][0]):
            size = 1
            for s in shapes[n][1:]:
                size *= s
            layers.append(flat[off:off + size].reshape(shapes[n][1:]))
            off += size
        out[n] = jnp.stack(layers)
    return out, flat[off]


def kernel(x, w_in, b_gate, rel_bias, w_proj_a, w_proj_b, w_out, ln1_g, ln1_b, w_ffn_in, w_ffn_out, ln2_g, ln2_b, loss_target, m_w_in, m_b_gate, m_rel_bias, m_w_proj_a, m_w_proj_b, m_w_out, m_ln1_g, m_ln1_b, m_w_ffn_in, m_w_ffn_out, m_ln2_g, m_ln2_b, v_w_in, v_b_gate, v_rel_bias, v_w_proj_a, v_w_proj_b, v_w_out, v_ln1_g, v_ln1_b, v_w_ffn_in, v_w_ffn_out, v_ln2_g, v_ln2_b):
    names = ("w_in", "b_gate", "rel_bias", "w_proj_a", "w_proj_b", "w_out", "ln1_g", "ln1_b", "w_ffn_in", "w_ffn_out", "ln2_g", "ln2_b")
    w = dict(zip(names, (w_in, b_gate, rel_bias, w_proj_a, w_proj_b, w_out, ln1_g, ln1_b, w_ffn_in, w_ffn_out, ln2_g, ln2_b)))
    m = dict(zip(names, (m_w_in, m_b_gate, m_rel_bias, m_w_proj_a, m_w_proj_b, m_w_out, m_ln1_g, m_ln1_b, m_w_ffn_in, m_w_ffn_out, m_ln2_g, m_ln2_b)))
    v = dict(zip(names, (v_w_in, v_b_gate, v_rel_bias, v_w_proj_a, v_w_proj_b, v_w_out, v_ln1_g, v_ln1_b, v_w_ffn_in, v_w_ffn_out, v_ln2_g, v_ln2_b)))
    T, D = x.shape[-2], x.shape[-1]

    gathered = _all_gather_weights([w[n].astype(_MXU) for n in _DENSE])
    W = dict(zip(_DENSE, gathered))
    small = {n: w[n] for n in _SMALL}
    sq, dx, gws, gss = _local_step(x.reshape(T, D), loss_target.reshape(T, D), W, small)

    def blocks(l, n):
        g = gws[l][n]
        return g if _COL_SHARDED[n] else g.reshape(4, g.shape[0] // 4, g.shape[1])

    slots = _scatter_grads([[blocks(l, n) for n in _DENSE] for l in range(2)])
    halves = [_sum_slots(s, f"sum_grad_{n}") for n, s in zip(_DENSE, slots)]
    grads = dict(zip(_DENSE, _pair_layers(halves)))
    total = _all_reduce_small(_pack_small(gss, sq))
    small_grads, sq_all = _unpack_small(total, {n: w[n].shape for n in _SMALL})
    grads.update(small_grads)
    loss = 0.5 * sq_all / D

    grad, delta, new_m, new_v = {}, {}, {}, {}
    for n in names:
        grad[n] = grads[n].reshape(w[n].shape)
        delta[n], new_m[n], new_v[n] = _adamw(w[n], grad[n], m[n], v[n], f"adamw_{n}")
    return (loss, dx.reshape(x.shape), *[grad[n] for n in names], *[delta[n] for n in names],
            *[new_m[n] for n in names], *[new_v[n] for n in names])
```

```python
import functools

import jax
import jax.numpy as jnp
from jax import lax
from jax.experimental import pallas as pl
from jax.experimental.pallas import tpu as pltpu

_MXU = jnp.bfloat16
_ACT = jnp.bfloat16
_F32 = jnp.float32

_HEAD = 64
_CHUNK = 64
_LANES = 128
_TQ = 128
_BAND_TILES = 5
_BIAS_TILES = 9
_REL_CLIP = 256
_LN_EPS = 1e-5
_MASKED = -1e30
_EXP_ZERO_BELOW = -104.0
_SB_WINDOW = 3
_VMEM_LIMIT = 56 * 1024 * 1024
_GRAD_ACC_BYTES = 12 * 1024 * 1024

_LR, _B1, _B2, _EPS, _WD, _STEP = 0.001, 0.9, 0.999, 1e-08, 0.01, 10

_MESH = pl.DeviceIdType.MESH


def _dot(a, b):
    return jnp.dot(a, b, preferred_element_type=_F32)


def _dot_nt(a, b):
    return lax.dot_general(a, b, (((1,), (1,)), ((), ())), preferred_element_type=_F32)


def _dot_tn(a, b):
    return lax.dot_general(a, b, (((0,), (0,)), ((), ())), preferred_element_type=_F32)


def _cparams(*sem):
    return pltpu.CompilerParams(dimension_semantics=sem, vmem_limit_bytes=_VMEM_LIMIT)


def _rows(t, c):
    return pl.BlockSpec((t, c), lambda i: (i, 0))


def _whole(shape):
    return pl.BlockSpec(shape, lambda i: tuple(0 for _ in shape))


_ANY = pl.BlockSpec(memory_space=pl.ANY)


def _load_cols(w_hbm, layer, w_vmem, sem):
    n = w_hbm.shape[-1]
    cps = [pltpu.make_async_copy(w_hbm.at[layer, k], w_vmem.at[:, pl.ds(k * n, n)], sem.at[k]) for k in range(4)]
    for cp in cps:
        cp.start()
    for cp in cps:
        cp.wait()


def _load_rows(w_hbm, layer, w_vmem, sem):
    r = w_hbm.shape[-2]
    cps = [pltpu.make_async_copy(w_hbm.at[layer, k], w_vmem.at[pl.ds(k * r, r), :], sem.at[k]) for k in range(4)]
    for cp in cps:
        cp.start()
    for cp in cps:
        cp.wait()


def _ln_stats(u):
    mu = jnp.mean(u, axis=-1, keepdims=True)
    xc = u - mu
    var = jnp.mean(xc * xc, axis=-1, keepdims=True)
    rstd = lax.rsqrt(var + _LN_EPS)
    return xc * rstd, rstd


def _ln_bwd(u, dy, gamma):
    xhat, rstd = _ln_stats(u)
    dxh = dy * gamma
    m1 = jnp.mean(dxh, axis=-1, keepdims=True)
    m2 = jnp.mean(dxh * xhat, axis=-1, keepdims=True)
    du = rstd * (dxh - m1 - xhat * m2)
    return du, jnp.sum(dy * xhat, axis=0, keepdims=True), jnp.sum(dy, axis=0, keepdims=True), xhat


def _divisor_tile(n, cap):
    best = None
    for t in range(_LANES, min(n, cap) + 1, _LANES):
        if n % t == 0:
            best = t
    return best or n


def _in_proj(x, w_in, layer):
    T, D = x.shape
    N = 4 * w_in.shape[-1]
    NQ = N - 2 * D
    tm = 256

    def kern(x_ref, w_hbm, hq_ref, hg_ref, w_v, sem):
        @pl.when(pl.program_id(0) == 0)
        def _():
            _load_cols(w_hbm, layer, w_v, sem)

        xb = x_ref[...].astype(_MXU)
        hq_ref[...] = _dot(xb, w_v[:, :NQ]).astype(hq_ref.dtype)
        hg_ref[...] = _dot(xb, w_v[:, NQ:])

    return pl.pallas_call(
        kern, name=f"in_proj_{layer}", grid=(T // tm,),
        in_specs=[_rows(tm, D), _ANY],
        out_specs=[_rows(tm, NQ), _rows(tm, 2 * D)],
        out_shape=[jax.ShapeDtypeStruct((T, NQ), _ACT), jax.ShapeDtypeStruct((T, 2 * D), _F32)],
        scratch_shapes=[pltpu.VMEM((D, N), w_in.dtype), pltpu.SemaphoreType.DMA((4,))],
        compiler_params=_cparams("arbitrary"),
    )(x, w_in)


def _mix_fwd(oa, ob, hg, x, wpa, wpb, wo, bg, gamma, beta, layer):
    T, D = x.shape
    WA, WB = oa.shape[1], ob.shape[1]
    alpha = float((2 * wo.shape[0]) ** 0.25)
    tm = 256

    def kern(oa_ref, ob_ref, hg_ref, x_ref, bg_ref, g_ref, b_ref, wpa_h, wpb_h, wo_h,
             x1_ref, u1_ref, pre_ref, ya_ref, yb_ref, wpa_v, wpb_v, wo_v, sa, sb, so):
        @pl.when(pl.program_id(0) == 0)
        def _():
            _load_cols(wpa_h, layer, wpa_v, sa)
            _load_cols(wpb_h, layer, wpb_v, sb)
            _load_rows(wo_h, layer, wo_v, so)

        ya = _dot(oa_ref[...].astype(_MXU), wpa_v[...])
        yb = _dot(ob_ref[...].astype(_MXU), wpb_v[...])
        hgv = hg_ref[...]
        bgv = bg_ref[...]
        ga = jax.nn.sigmoid(hgv[:, :D] + bgv[:, :D])
        gb = jax.nn.sigmoid(hgv[:, D:] + bgv[:, D:])
        pre = ga * ya + gb * yb
        mix = _dot(pre.astype(_MXU), wo_v[...])
        u = alpha * x_ref[...] + mix
        xhat, _ = _ln_stats(u)
        x1_ref[...] = xhat * g_ref[...] + b_ref[...]
        u1_ref[...] = u
        pre_ref[...] = pre.astype(pre_ref.dtype)
        ya_ref[...] = ya.astype(ya_ref.dtype)
        yb_ref[...] = yb.astype(yb_ref.dtype)

    return pl.pallas_call(
        kern, name=f"mix_fwd_{layer}", grid=(T // tm,),
        in_specs=[_rows(tm, WA), _rows(tm, WB), _rows(tm, 2 * D), _rows(tm, D),
                  _whole((1, 2 * D)), _whole((1, D)), _whole((1, D)), _ANY, _ANY, _ANY],
        out_specs=[_rows(tm, D)] * 5,
        out_shape=[jax.ShapeDtypeStruct((T, D), _F32), jax.ShapeDtypeStruct((T, D), _F32)]
        + [jax.ShapeDtypeStruct((T, D), _ACT)] * 3,
        scratch_shapes=[pltpu.VMEM((WA, D), wpa.dtype), pltpu.VMEM((WB, D), wpb.dtype), pltpu.VMEM((D, D), wo.dtype),
                        pltpu.SemaphoreType.DMA((4,)), pltpu.SemaphoreType.DMA((4,)), pltpu.SemaphoreType.DMA((4,))],
        compiler_params=_cparams("arbitrary"),
    )(oa, ob, hg, x, bg, gamma, beta, wpa, wpb, wo)


def _ffn_fwd(x1, wfi, wfo, gamma, beta, layer):
    T, D = x1.shape
    F2 = 4 * wfi.shape[-1]
    F = F2 // 2
    alpha = float((2 * wfi.shape[0]) ** 0.25)
    tm = 256
    fc = F // 2

    def kern(x_ref, g_ref, b_ref, wi_h, wo_h, x2_ref, u2_ref, act_ref, gu_ref, wi_v, wo_v, si, so):
        @pl.when(pl.program_id(0) == 0)
        def _():
            _load_cols(wi_h, layer, wi_v, si)
            _load_rows(wo_h, layer, wo_v, so)

        x = x_ref[...]
        xb = x.astype(_MXU)
        ffn = jnp.zeros((tm, D), _F32)
        for c in range(2):
            g = _dot(xb, wi_v[:, c * fc:(c + 1) * fc])
            u = _dot(xb, wi_v[:, F + c * fc:F + (c + 1) * fc])
            act = g * jax.nn.sigmoid(g) * u
            ab = act.astype(_MXU)
            ffn = ffn + _dot(ab, wo_v[c * fc:(c + 1) * fc, :])
            act_ref[:, c * fc:(c + 1) * fc] = ab.astype(act_ref.dtype)
            gu_ref[:, c * fc:(c + 1) * fc] = g.astype(gu_ref.dtype)
            gu_ref[:, F + c * fc:F + (c + 1) * fc] = u.astype(gu_ref.dtype)
        uu = alpha * x + ffn
        xhat, _ = _ln_stats(uu)
        x2_ref[...] = xhat * g_ref[...] + b_ref[...]
        u2_ref[...] = uu

    return pl.pallas_call(
        kern, name=f"ffn_fwd_{layer}", grid=(T // tm,),
        in_specs=[_rows(tm, D), _whole((1, D)), _whole((1, D)), _ANY, _ANY],
        out_specs=[_rows(tm, D), _rows(tm, D), _rows(tm, F), _rows(tm, F2)],
        out_shape=[jax.ShapeDtypeStruct((T, D), _F32), jax.ShapeDtypeStruct((T, D), _F32),
                   jax.ShapeDtypeStruct((T, F), _ACT), jax.ShapeDtypeStruct((T, F2), _ACT)],
        scratch_shapes=[pltpu.VMEM((D, F2), wfi.dtype), pltpu.VMEM((F, D), wfo.dtype),
                        pltpu.SemaphoreType.DMA((4,)), pltpu.SemaphoreType.DMA((4,))],
        compiler_params=_cparams("arbitrary"),
    )(x1, gamma, beta, wfi, wfo)


def _ffn_bwd_a(u2, dy_or_target, gu, gamma, beta, wfo, layer, last):
    T, D = u2.shape
    F2 = gu.shape[1]
    F = F2 // 2
    tm = 256
    fc = F // 2

    def kern(u_ref, dy_ref, gu_ref, g_ref, b_ref, wo_h, du_ref, dub_ref, dgu_ref, st_ref, wo_v, so):
        @pl.when(pl.program_id(0) == 0)
        def _():
            _load_rows(wo_h, layer, wo_v, so)
            st_ref[...] = jnp.zeros_like(st_ref)

        gam = g_ref[...]
        u = u_ref[...]
        if last:
            xhat0, _ = _ln_stats(u)
            err = xhat0 * gam + b_ref[...] - dy_ref[...]
            dy = err * (1.0 / D)
            st_ref[2:3, :] += jnp.sum(err * err, axis=0, keepdims=True)
        else:
            dy = dy_ref[...]
        du, dgam, dbet, _ = _ln_bwd(u, dy, gam)
        st_ref[0:1, :] += dgam
        st_ref[1:2, :] += dbet
        du_ref[...] = du
        dub = du.astype(_MXU)
        dub_ref[...] = dub.astype(dub_ref.dtype)
        for c in range(2):
            dact = _dot_nt(dub, wo_v[c * fc:(c + 1) * fc, :])
            g = gu_ref[:, c * fc:(c + 1) * fc].astype(_F32)
            uu = gu_ref[:, F + c * fc:F + (c + 1) * fc].astype(_F32)
            sg = jax.nn.sigmoid(g)
            dgu_ref[:, c * fc:(c + 1) * fc] = (dact * uu * (sg * (1.0 + g * (1.0 - sg)))).astype(dgu_ref.dtype)
            dgu_ref[:, F + c * fc:F + (c + 1) * fc] = (dact * (g * sg)).astype(dgu_ref.dtype)

    return pl.pallas_call(
        kern, name=f"ffn_bwd_a_{layer}", grid=(T // tm,),
        in_specs=[_rows(tm, D), _rows(tm, D), _rows(tm, F2), _whole((1, D)), _whole((1, D)), _ANY],
        out_specs=[_rows(tm, D), _rows(tm, D), _rows(tm, F2), _whole((8, D))],
        out_shape=[jax.ShapeDtypeStruct((T, D), _F32), jax.ShapeDtypeStruct((T, D), _ACT),
                   jax.ShapeDtypeStruct((T, F2), _ACT), jax.ShapeDtypeStruct((8, D), _F32)],
        scratch_shapes=[pltpu.VMEM((F, D), wfo.dtype), pltpu.SemaphoreType.DMA((4,))],
        compiler_params=_cparams("arbitrary"),
    )(u2, dy_or_target, gu, gamma, beta, wfo)


def _residual_nt(res, res_scale, d, w, layer, name):
    T, K = res.shape
    N = d.shape[1]
    tm = 256

    def kern(r_ref, d_ref, w_hbm, o_ref, w_v, sem):
        @pl.when(pl.program_id(0) == 0)
        def _():
            _load_cols(w_hbm, layer, w_v, sem)

        o_ref[...] = res_scale * r_ref[...] + _dot_nt(d_ref[...].astype(_MXU), w_v[...])

    return pl.pallas_call(
        kern, name=f"{name}_{layer}", grid=(T // tm,),
        in_specs=[_rows(tm, K), _rows(tm, N), _ANY],
        out_specs=_rows(tm, K),
        out_shape=jax.ShapeDtypeStruct((T, K), _F32),
        scratch_shapes=[pltpu.VMEM((K, N), w.dtype), pltpu.SemaphoreType.DMA((4,))],
        compiler_params=_cparams("arbitrary"),
    )(res, d, w)


def _mix_bwd(u1, dx1, ya, yb, hg, wpa, wpb, wo, bg, gamma, layer):
    T, D = u1.shape
    WA, WB = wpa.shape[-2], wpb.shape[-2]
    tm = 256

    def kern(u_ref, dx_ref, ya_ref, yb_ref, hg_ref, bg_ref, g_ref, wpa_h, wpb_h, wo_h,
             du_ref, dub_ref, dya_ref, dyb_ref, dhg_ref, doa_ref, dob_ref, st_ref,
             wpa_v, wpb_v, wo_v, sa, sb, so):
        @pl.when(pl.program_id(0) == 0)
        def _():
            _load_cols(wpa_h, layer, wpa_v, sa)
            _load_cols(wpb_h, layer, wpb_v, sb)
            _load_rows(wo_h, layer, wo_v, so)
            st_ref[...] = jnp.zeros_like(st_ref)

        du, dgam, dbet, _ = _ln_bwd(u_ref[...], dx_ref[...], g_ref[...])
        st_ref[1:2, :D] += dgam
        st_ref[1:2, D:] += dbet
        du_ref[...] = du
        dub = du.astype(_MXU)
        dub_ref[...] = dub.astype(dub_ref.dtype)
        dpre = _dot_nt(dub, wo_v[...])
        hgv = hg_ref[...]
        bgv = bg_ref[...]
        ga = jax.nn.sigmoid(hgv[:, :D] + bgv[:, :D])
        gb = jax.nn.sigmoid(hgv[:, D:] + bgv[:, D:])
        dya = (dpre * ga).astype(_MXU)
        dyb = (dpre * gb).astype(_MXU)
        dsa = dpre * ya_ref[...].astype(_F32) * (ga * (1.0 - ga))
        dsb = dpre * yb_ref[...].astype(_F32) * (gb * (1.0 - gb))
        st_ref[0:1, :D] += jnp.sum(dsa, axis=0, keepdims=True)
        st_ref[0:1, D:] += jnp.sum(dsb, axis=0, keepdims=True)
        dya_ref[...] = dya.astype(dya_ref.dtype)
        dyb_ref[...] = dyb.astype(dyb_ref.dtype)
        dhg_ref[:, :D] = dsa.astype(dhg_ref.dtype)
        dhg_ref[:, D:] = dsb.astype(dhg_ref.dtype)
        doa_ref[...] = _dot_nt(dya, wpa_v[...]).astype(doa_ref.dtype)
        dob_ref[...] = _dot_nt(dyb, wpb_v[...]).astype(dob_ref.dtype)

    return pl.pallas_call(
        kern, name=f"mix_bwd_{layer}", grid=(T // tm,),
        in_specs=[_rows(tm, D)] * 4 + [_rows(tm, 2 * D), _whole((1, 2 * D)), _whole((1, D)), _ANY, _ANY, _ANY],
        out_specs=[_rows(tm, D)] * 4 + [_rows(tm, 2 * D), _rows(tm, WA), _rows(tm, WB), _whole((8, 2 * D))],
        out_shape=[jax.ShapeDtypeStruct((T, D), _F32)] + [jax.ShapeDtypeStruct((T, D), _ACT)] * 3
        + [jax.ShapeDtypeStruct((T, 2 * D), _ACT), jax.ShapeDtypeStruct((T, WA), _ACT),
           jax.ShapeDtypeStruct((T, WB), _ACT), jax.ShapeDtypeStruct((8, 2 * D), _F32)],
        scratch_shapes=[pltpu.VMEM((WA, D), wpa.dtype), pltpu.VMEM((WB, D), wpb.dtype), pltpu.VMEM((D, D), wo.dtype),
                        pltpu.SemaphoreType.DMA((4,)), pltpu.SemaphoreType.DMA((4,)), pltpu.SemaphoreType.DMA((4,))],
        compiler_params=_cparams("arbitrary"),
    )(u1, dx1, ya, yb, hg, bg, gamma, wpa, wpb, wo)


def _grad_w(a, b, *, col_shards, name):
    T, M = a.shape
    N = b.shape[1]
    tk = 512
    n = N // 4 if col_shards else N
    whole = M * N * 4 <= _GRAD_ACC_BYTES
    tn = N if whole else (n if col_shards else _divisor_tile(N, _GRAD_ACC_BYTES // (4 * M)))
    nk = T // tk

    def kern(a_ref, b_ref, o_ref, acc):
        k = pl.program_id(1)

        @pl.when(k == 0)
        def _():
            acc[...] = jnp.zeros_like(acc)

        acc[...] += _dot_tn(a_ref[...].astype(_MXU), b_ref[...].astype(_MXU))

        @pl.when(k == nk - 1)
        def _():
            if col_shards and whole:
                for s in range(4):
                    o_ref[s] = acc[:, s * n:(s + 1) * n].astype(o_ref.dtype)
            else:
                o_ref[...] = acc[...].astype(o_ref.dtype)

    if col_shards:
        out_spec = (pl.BlockSpec((4, M, n), lambda j, k: (0, 0, 0)) if whole
                    else pl.BlockSpec((None, M, n), lambda j, k: (j, 0, 0)))
        out_shape = jax.ShapeDtypeStruct((4, M, n), _ACT)
    else:
        out_spec = pl.BlockSpec((M, tn), lambda j, k: (0, j))
        out_shape = jax.ShapeDtypeStruct((M, N), _ACT)
    return pl.pallas_call(
        kern, name=name, grid=(N // tn, nk),
        in_specs=[pl.BlockSpec((tk, M), lambda j, k: (k, 0)), pl.BlockSpec((tk, tn), lambda j, k: (k, j))],
        out_specs=out_spec, out_shape=out_shape,
        scratch_shapes=[pltpu.VMEM((M, tn), _F32)],
        compiler_params=_cparams("parallel", "arbitrary"),
    )(a, b)


def _bias_tiles(rel):
    H = rel.shape[0]
    span = _TQ * _BAND_TILES - 1
    edge = span - _REL_CLIP
    gvec = jnp.concatenate([jnp.broadcast_to(rel[:, :1], (H, edge)), rel, jnp.broadcast_to(rel[:, -1:], (H, edge))], axis=1)
    width = _BIAS_TILES * _TQ
    period = width + _TQ
    tiled = jnp.broadcast_to(jnp.pad(gvec[:, ::-1], ((0, 0), (0, 1)))[:, None, :], (H, _TQ, period))
    rows = tiled.reshape(H, _TQ * period)[:, :_TQ * (period - 1)].reshape(H, _TQ, period - 1)[:, :, _TQ - 1:]
    r = jnp.arange(_TQ)[:, None]
    u = jnp.arange(width)[None, :]
    d = 4 * _TQ + r - u
    rm = r % _CHUNK
    valid = (d >= rm - (_CHUNK - 1)) & (d <= rm + 8 * _CHUNK)
    tiles = jnp.where(valid[None], rows, _MASKED)
    return tiles.reshape(H // 2, 2 * _TQ, _BIAS_TILES, _TQ).transpose(0, 2, 1, 3)


def _fold_bias_grad(db):
    H = 2 * db.shape[0]
    width = _BIAS_TILES * _TQ
    period = width + _TQ
    x = jnp.pad(db.transpose(0, 2, 1, 3).reshape(H, _TQ, width), ((0, 0), (0, 0), (_TQ - 1, 0)))
    skew = jnp.pad(x.reshape(H, _TQ * (period - 1)), ((0, 0), (0, _TQ))).reshape(H, _TQ, period)
    dg = skew.sum(axis=1)[:, :period - 1][:, ::-1]
    span = _TQ * _BAND_TILES - 1
    edge = span - _REL_CLIP
    mid = dg[:, edge:edge + 2 * _REL_CLIP + 1]
    lo = dg[:, :edge].sum(axis=1)
    hi = dg[:, edge + 2 * _REL_CLIP + 1:].sum(axis=1)
    return mid.at[:, 0].add(lo).at[:, -1].add(hi)


def _band_window(i):
    j0 = jnp.maximum(i - (_BAND_TILES - 1), 0)
    return j0, (_BAND_TILES - 1) - (i - j0)


def _head_masks():
    lane = lax.broadcasted_iota(jnp.int32, (1, _LANES), 1)
    return [(lane // _HEAD) == hh for hh in range(2)]


def _stack_heads(x, masks):
    return jnp.concatenate([jnp.where(m, x, jnp.zeros_like(x)) for m in masks], axis=0)


def _unstack_heads(y, masks):
    return jnp.where(masks[0], y[:_TQ], y[_TQ:])


def _scaled(q):
    return q * jnp.asarray(_HEAD ** -0.5, q.dtype)


def _band_probs(q2, k_ref, b_ref, j0, boff):
    s = []
    for j in range(_BAND_TILES):
        kj = k_ref[pl.ds(pl.multiple_of((j0 + j) * _TQ, _TQ), _TQ), :]
        s.append(_dot_nt(q2, kj) + b_ref[boff + j])
    m = jnp.max(functools.reduce(jnp.maximum, s), axis=-1, keepdims=True)
    p = [jnp.exp(x - m) for x in s]
    l = jnp.sum(functools.reduce(lambda a, b: a + b, p), axis=-1, keepdims=True)
    return p, 1.0 / l


def _attn_a_fwd(hq, bias, col0, width, layer):
    T = hq.shape[0]
    npair = width // _LANES
    nq = T // _TQ
    cb = col0 // _LANES

    def kern(q_ref, k_ref, v_ref, b_ref, o_ref):
        i = pl.program_id(1)
        j0, boff = _band_window(i)
        masks = _head_masks()
        q2 = _stack_heads(_scaled(q_ref[...]), masks)
        p, inv = _band_probs(q2, k_ref, b_ref, j0, boff)
        o = jnp.zeros((2 * _TQ, _LANES), _F32)
        for j in range(_BAND_TILES):
            vj = v_ref[pl.ds(pl.multiple_of((j0 + j) * _TQ, _TQ), _TQ), :]
            o = o + _dot(p[j].astype(_MXU), vj)
        o_ref[...] = _unstack_heads(o * inv, masks).astype(o_ref.dtype)

    return pl.pallas_call(
        kern, name=f"band_attn_fwd_{layer}", grid=(npair, nq),
        in_specs=[pl.BlockSpec((_TQ, _LANES), lambda h, i: (i, cb + h)),
                  pl.BlockSpec((T, _LANES), lambda h, i: (0, cb + npair + h)),
                  pl.BlockSpec((T, _LANES), lambda h, i: (0, cb + 2 * npair + h)),
                  pl.BlockSpec((None, _BIAS_TILES, 2 * _TQ, _TQ), lambda h, i: (h, 0, 0, 0))],
        out_specs=pl.BlockSpec((_TQ, _LANES), lambda h, i: (i, h)),
        out_shape=jax.ShapeDtypeStruct((T, width), _ACT),
        compiler_params=_cparams("arbitrary", "arbitrary"),
    )(hq, hq, hq, bias)


def _attn_a_bwd(hq, bias, do, col0, width, layer):
    T = hq.shape[0]
    npair = width // _LANES
    nq = T // _TQ
    cb = col0 // _LANES
    scale = _HEAD ** -0.5

    def kern(q_ref, k_ref, v_ref, b_ref, do_ref, dq_ref, dk_ref, dv_ref, db_ref, dk_acc, dv_acc):
        i = pl.program_id(1)

        @pl.when(i == 0)
        def _():
            dk_acc[...] = jnp.zeros_like(dk_acc)
            dv_acc[...] = jnp.zeros_like(dv_acc)
            db_ref[...] = jnp.zeros_like(db_ref)

        j0, boff = _band_window(i)
        masks = _head_masks()
        q2 = _stack_heads(_scaled(q_ref[...]), masks)
        do2 = _stack_heads(do_ref[...], masks).astype(_MXU)
        p, inv = _band_probs(q2, k_ref, b_ref, j0, boff)
        rows = [pl.ds(pl.multiple_of((j0 + j) * _TQ, _TQ), _TQ) for j in range(_BAND_TILES)]
        p = [x * inv for x in p]
        dp = [_dot_nt(do2, v_ref[rows[j], :]) for j in range(_BAND_TILES)]
        delta = jnp.sum(functools.reduce(lambda a, b: a + b, [p[j] * dp[j] for j in range(_BAND_TILES)]),
                        axis=-1, keepdims=True)
        dq = jnp.zeros((2 * _TQ, _LANES), _F32)
        for j in range(_BAND_TILES):
            ds = p[j] * (dp[j] - delta)
            db_ref[boff + j] += ds
            dsb = ds.astype(_MXU)
            dq = dq + _dot(dsb, k_ref[rows[j], :])
            dk_acc[rows[j], :] += _dot_tn(dsb, q2)
            dv_acc[rows[j], :] += _dot_tn(p[j].astype(_MXU), do2)
        dq_ref[...] = (_unstack_heads(dq, masks) * scale).astype(dq_ref.dtype)

        @pl.when(i == nq - 1)
        def _():
            dk_ref[...] = dk_acc[...].astype(dk_ref.dtype)
            dv_ref[...] = dv_acc[...].astype(dv_ref.dtype)

    strip = pl.BlockSpec((None, _BIAS_TILES, 2 * _TQ, _TQ), lambda h, i: (h, 0, 0, 0))
    return pl.pallas_call(
        kern, name=f"band_attn_bwd_{layer}", grid=(npair, nq),
        in_specs=[pl.BlockSpec((_TQ, _LANES), lambda h, i: (i, cb + h)),
                  pl.BlockSpec((T, _LANES), lambda h, i: (0, cb + npair + h)),
                  pl.BlockSpec((T, _LANES), lambda h, i: (0, cb + 2 * npair + h)),
                  strip,
                  pl.BlockSpec((_TQ, _LANES), lambda h, i: (i, h))],
        out_specs=[pl.BlockSpec((_TQ, _LANES), lambda h, i: (i, h)),
                   pl.BlockSpec((T, _LANES), lambda h, i: (0, h)),
                   pl.BlockSpec((T, _LANES), lambda h, i: (0, h)),
                   strip],
        out_shape=[jax.ShapeDtypeStruct((T, width), _ACT)] * 3
        + [jax.ShapeDtypeStruct((npair, _BIAS_TILES, 2 * _TQ, _TQ), _F32)],
        scratch_shapes=[pltpu.VMEM((T, _LANES), _F32), pltpu.VMEM((T, _LANES), _F32)],
        compiler_params=_cparams("arbitrary", "arbitrary"),
    )(hq, hq, hq, bias, do)


def _suffix_matrix():
    r = lax.broadcasted_iota(jnp.int32, (_TQ, _TQ), 0)
    c = lax.broadcasted_iota(jnp.int32, (_TQ, _TQ), 1)
    r2 = lax.broadcasted_iota(jnp.int32, (2 * _TQ, _TQ), 0)
    c2 = lax.broadcasted_iota(jnp.int32, (2 * _TQ, _TQ), 1)
    return (r > c).astype(_MXU), c2 - (r2 & (_TQ - 1))


def _suffix_sum(x, tri):
    n = x.shape[0]
    hi = x.astype(_MXU)
    lo = (x - hi.astype(_F32)).astype(_MXU)
    y = _dot(jnp.concatenate([hi, lo], axis=0), tri)
    return y[:n] + y[n:]


def _stick_tile(qs, kj, jj, rel, carry_l, tri):
    z = _dot_nt(qs, kj)
    nsp = -(jnp.maximum(z, 0.0) + jnp.log(1.0 + jnp.exp(-jnp.abs(z))))
    if isinstance(jj, int):
        mask = (rel < 0) if jj == 0 else None
    else:
        mask = rel < jnp.where(jj == 0, 0, _TQ)
    L = nsp if mask is None else jnp.where(mask, nsp, 0.0)
    w = jnp.exp(z + L + _suffix_sum(L, tri) + carry_l)
    if mask is not None:
        w = jnp.where(mask, w, 0.0)
    return z, L, w, mask


def _sweep_done(i, jj, cl):
    return jnp.logical_or(jj > i, jnp.max(cl) < _EXP_ZERO_BELOW)


def _sb_fwd(hq, col0, width, layer):
    T = hq.shape[0]
    npair = width // _LANES
    nq = T // _TQ
    cb = col0 // _LANES

    def kern(q_ref, k_ref, v_ref, o_ref):
        i = pl.program_id(1)
        masks = _head_masks()
        tri, rel = _suffix_matrix()
        q2 = _stack_heads(_scaled(q_ref[...]), masks)

        def tile(jj, c):
            cl, a = c
            rows = pl.ds(pl.multiple_of((i - jj) * _TQ, _TQ), _TQ)
            _, L, w, _ = _stick_tile(q2, k_ref[rows, :], jj, rel, cl, tri)
            return cl + jnp.sum(L, axis=-1, keepdims=True), a + _dot(w.astype(_MXU), v_ref[rows, :])

        zero = (jnp.zeros((2 * _TQ, 1), _F32), jnp.zeros((2 * _TQ, _LANES), _F32))

        def window():
            c = zero
            for jj in range(_SB_WINDOW):
                c = tile(jj, c)
            return (jnp.int32(_SB_WINDOW),) + c

        start = lax.cond(i >= _SB_WINDOW - 1, window, lambda: (jnp.int32(0),) + zero)
        out = lax.while_loop(lambda c: jnp.logical_not(_sweep_done(i, c[0], c[1])),
                             lambda c: (c[0] + 1,) + tile(c[0], c[1:]), start)
        o_ref[...] = _unstack_heads(out[2], masks)

    return pl.pallas_call(
        kern, name=f"stick_attn_fwd_{layer}", grid=(npair, nq),
        in_specs=[pl.BlockSpec((_TQ, _LANES), lambda h, i: (i, cb + h)),
                  pl.BlockSpec((T, _LANES), lambda h, i: (0, cb + npair + h)),
                  pl.BlockSpec((T, _LANES), lambda h, i: (0, cb + 2 * npair + h))],
        out_specs=pl.BlockSpec((_TQ, _LANES), lambda h, i: (i, h)),
        out_shape=jax.ShapeDtypeStruct((T, width), _F32),
        compiler_params=_cparams("arbitrary", "arbitrary"),
    )(hq, hq, hq)


def _sb_bwd(hq, o, do, col0, width, layer):
    T = hq.shape[0]
    npair = width // _LANES
    nq = T // _TQ
    cb = col0 // _LANES
    scale = _HEAD ** -0.5

    def kern(q_ref, k_ref, v_ref, o_ref, do_ref, dq_ref, dk_ref, dv_ref, dk_acc, dv_acc):
        i = pl.program_id(1)

        @pl.when(i == 0)
        def _():
            dk_acc[...] = jnp.zeros_like(dk_acc)
            dv_acc[...] = jnp.zeros_like(dv_acc)

        masks = _head_masks()
        tri, rel = _suffix_matrix()
        q2 = _stack_heads(_scaled(q_ref[...]), masks)
        do_t = do_ref[...]
        do2 = _stack_heads(do_t, masks).astype(_MXU)
        dsum = jnp.sum(_stack_heads(do_t.astype(_F32) * o_ref[...], masks), axis=-1, keepdims=True)

        def tile(jj, c):
            cl, cg, dq = c
            rows = pl.ds(pl.multiple_of((i - jj) * _TQ, _TQ), _TQ)
            kj = k_ref[rows, :]
            vj = v_ref[rows, :]
            z, L, w, mask = _stick_tile(q2, kj, jj, rel, cl, tri)
            wb = w.astype(_MXU)
            g = wb.astype(_F32) * _dot_nt(do2, vj)
            gs = _suffix_sum(g, tri) + cg
            dz = g - jnp.exp(z + L) * (dsum - gs)
            if mask is not None:
                dz = jnp.where(mask, dz, 0.0)
            dzb = dz.astype(_MXU)
            dk_acc[rows, :] += _dot_tn(dzb, q2)
            dv_acc[rows, :] += _dot_tn(wb, do2)
            return (cl + jnp.sum(L, axis=-1, keepdims=True), cg + jnp.sum(g, axis=-1, keepdims=True),
                    dq + _dot(dzb, kj))

        zc = jnp.zeros((2 * _TQ, 1), _F32)
        zero = (zc, zc, jnp.zeros((2 * _TQ, _LANES), _F32))

        def window():
            c = zero
            for jj in range(_SB_WINDOW):
                c = tile(jj, c)
            return (jnp.int32(_SB_WINDOW),) + c

        start = lax.cond(i >= _SB_WINDOW - 1, window, lambda: (jnp.int32(0),) + zero)
        out = lax.while_loop(lambda c: jnp.logical_not(_sweep_done(i, c[0], c[1])),
                             lambda c: (c[0] + 1,) + tile(c[0], c[1:]), start)
        dq_ref[...] = (_unstack_heads(out[3], masks) * scale).astype(dq_ref.dtype)

        @pl.when(i == nq - 1)
        def _():
            dk_ref[...] = dk_acc[...].astype(dk_ref.dtype)
            dv_ref[...] = dv_acc[...].astype(dv_ref.dtype)

    return pl.pallas_call(
        kern, name=f"stick_attn_bwd_{layer}", grid=(npair, nq),
        in_specs=[pl.BlockSpec((_TQ, _LANES), lambda h, i: (i, cb + h)),
                  pl.BlockSpec((T, _LANES), lambda h, i: (0, cb + npair + h)),
                  pl.BlockSpec((T, _LANES), lambda h, i: (0, cb + 2 * npair + h)),
                  pl.BlockSpec((_TQ, _LANES), lambda h, i: (i, h)),
                  pl.BlockSpec((_TQ, _LANES), lambda h, i: (i, h))],
        out_specs=[pl.BlockSpec((_TQ, _LANES), lambda h, i: (i, h)),
                   pl.BlockSpec((T, _LANES), lambda h, i: (0, h)),
                   pl.BlockSpec((T, _LANES), lambda h, i: (0, h))],
        out_shape=[jax.ShapeDtypeStruct((T, width), _ACT)] * 3,
        scratch_shapes=[pltpu.VMEM((T, _LANES), _F32), pltpu.VMEM((T, _LANES), _F32)],
        compiler_params=_cparams("arbitrary", "arbitrary"),
    )(hq, hq, hq, o, do)


_DENSE = ("w_in", "w_proj_a", "w_proj_b", "w_out", "w_ffn_in", "w_ffn_out")
_COL_SHARDED = {"w_in": True, "w_proj_a": True, "w_proj_b": True, "w_out": False, "w_ffn_in": True, "w_ffn_out": False}
_SMALL = ("b_gate", "rel_bias", "ln1_g", "ln1_b", "ln2_g", "ln2_b")


def _layer_fwd(x, W, small, l):
    D = x.shape[1]
    WA = W["w_proj_a"].shape[-2]
    WB = W["w_proj_b"].shape[-2]
    hq, hg = _in_proj(x, W["w_in"], l)
    bias = _bias_tiles(small["rel_bias"][l])
    oa = _attn_a_fwd(hq, bias, 0, WA, l)
    ob = _sb_fwd(hq, 3 * WA, WB, l)
    row = lambda v: v[l].reshape(1, -1)
    x1, u1, pre, ya, yb = _mix_fwd(oa, ob, hg, x, W["w_proj_a"], W["w_proj_b"], W["w_out"],
                                   row(small["b_gate"]), row(small["ln1_g"]), row(small["ln1_b"]), l)
    x2, u2, act, gu = _ffn_fwd(x1, W["w_ffn_in"], W["w_ffn_out"], row(small["ln2_g"]), row(small["ln2_b"]), l)
    return x2, dict(x=x, hq=hq, hg=hg, bias=bias, oa=oa, ob=ob, x1=x1, u1=u1, pre=pre, ya=ya, yb=yb, u2=u2, act=act, gu=gu)


def _layer_bwd(dy_or_target, S, W, small, l, last):
    D = S["x"].shape[1]
    WA = W["w_proj_a"].shape[-2]
    WB = W["w_proj_b"].shape[-2]
    row = lambda v: v[l].reshape(1, -1)
    du2, du2b, dgu, st2 = _ffn_bwd_a(S["u2"], dy_or_target, S["gu"], row(small["ln2_g"]), row(small["ln2_b"]),
                                     W["w_ffn_out"], l, last)
    dx1 = _residual_nt(du2, float((2 * W["w_in"].shape[0]) ** 0.25), dgu, W["w_ffn_in"], l, "ffn_bwd_b")
    gw = {}
    gw["w_ffn_in"] = _grad_w(S["x1"], dgu, col_shards=True, name=f"grad_w_ffn_in_{l}")
    gw["w_ffn_out"] = _grad_w(S["act"], du2b, col_shards=False, name=f"grad_w_ffn_out_{l}")
    du1, du1b, dya, dyb, dhg, doa, dob, st1 = _mix_bwd(S["u1"], dx1, S["ya"], S["yb"], S["hg"], W["w_proj_a"],
                                                       W["w_proj_b"], W["w_out"], row(small["b_gate"]), row(small["ln1_g"]), l)
    gw["w_out"] = _grad_w(S["pre"], du1b, col_shards=False, name=f"grad_w_out_{l}")
    gw["w_proj_a"] = _grad_w(S["oa"], dya, col_shards=True, name=f"grad_w_proj_a_{l}")
    gw["w_proj_b"] = _grad_w(S["ob"], dyb, col_shards=True, name=f"grad_w_proj_b_{l}")
    dqa, dka, dva, dbias = _attn_a_bwd(S["hq"], S["bias"], doa, 0, WA, l)
    dqb, dkb, dvb = _sb_bwd(S["hq"], S["ob"], dob, 3 * WA, WB, l)
    dh = jnp.concatenate([dqa, dka, dva, dqb, dkb, dvb, dhg], axis=1)
    dx = _residual_nt(du1, float((2 * W["w_in"].shape[0]) ** 0.25), dh, W["w_in"], l, "in_proj_bwd")
    gw["w_in"] = _grad_w(S["x"], dh, col_shards=True, name=f"grad_w_in_{l}")
    gs = dict(b_gate=st1[0], rel_bias=_fold_bias_grad(dbias), ln1_g=st1[1, :D], ln1_b=st1[1, D:],
              ln2_g=st2[0], ln2_b=st2[1])
    return dx, gw, gs, st2[2]


def _local_step(x, target, W, small):
    depth = W["w_in"].shape[0]
    saved = []
    h = x
    for l in range(depth):
        h, S = _layer_fwd(h, W, small, l)
        saved.append(S)
    gws, gss = [None] * depth, [None] * depth
    d = target
    sq = None
    for l in reversed(range(depth)):
        d, gws[l], gss[l], sq_l = _layer_bwd(d, saved[l], W, small, l, l == depth - 1)
        if l == depth - 1:
            sq = sq_l
    return sq, d, gws, gss


def _place():
    return lax.axis_index("x"), lax.axis_index("y"), lax.axis_index("c")


def _all_gather_weights(shards):
    nt = len(shards)

    def body(*refs):
        ins, outs = refs[:nt], refs[nt:2 * nt]
        send_sems, recv_sems, loc_sems = refs[2 * nt:]
        x, y, c = _place()
        k = 2 * x + y
        sibling = (x, y, 1 - c)
        chips = [(1 - x, y), (x, 1 - y), (1 - x, 1 - y)]

        def copy(t, s, layer, chip_k, to, src=None):
            block = outs[t].at[layer, chip_k]
            return pltpu.make_async_remote_copy(src_ref=block if src is None else src, dst_ref=block,
                                                send_sem=send_sems.at[t, s], recv_sem=recv_sems.at[t, s],
                                                device_id=to, device_id_type=_MESH)

        local = []
        for t in range(nt):
            for layer in range(2):
                cp = pltpu.make_async_copy(ins[t].at[layer], outs[t].at[layer, k], loc_sems.at[t, layer])
                cp.start()
                local.append(cp)
        sent = []
        for t in range(nt):
            for s, chip in enumerate(chips):
                cp = copy(t, s, c, k, (*chip, c), src=ins[t].at[c])
                cp.start()
                sent.append(cp)
        for t in range(nt):
            for s, chip in enumerate(chips):
                ck = 2 * chip[0] + chip[1]
                copy(t, s, c, ck, (x, y, c)).wait_recv()
                cp = copy(t, 3 + s, c, ck, sibling)
                cp.start()
                sent.append(cp)
        for t in range(nt):
            for s, chip in enumerate(chips):
                copy(t, 3 + s, 1 - c, 2 * chip[0] + chip[1], (x, y, c)).wait_recv()
        for cp in sent:
            cp.wait_send()
        for cp in local:
            cp.wait()

    return pl.pallas_call(
        body, name="all_gather_weights",
        in_specs=[_ANY] * nt, out_specs=[_ANY] * nt,
        out_shape=[jax.ShapeDtypeStruct((2, 4) + s.shape[1:], s.dtype) for s in shards],
        scratch_shapes=[pltpu.SemaphoreType.DMA((nt, 6)), pltpu.SemaphoreType.DMA((nt, 6)), pltpu.SemaphoreType.DMA((nt, 2))],
    )(*shards)


def _peer(x, y, c, r):
    px = 1 - x if r & 4 else x
    py = 1 - y if r & 2 else y
    pc = 1 - c if r & 1 else c
    return (px, py, pc), 4 * px + 2 * py + pc


def _scatter_grads(gws):
    nt = len(gws[0])

    def body(*refs):
        g0, g1, outs = refs[:nt], refs[nt:2 * nt], refs[2 * nt:3 * nt]
        send_sems, recv_sems, loc_sems = refs[3 * nt:]
        x, y, c = _place()
        me = 4 * x + 2 * y + c

        def copy(t, r, src, slot, to):
            return pltpu.make_async_remote_copy(src_ref=src, dst_ref=outs[t].at[slot],
                                                send_sem=send_sems.at[t, r - 1], recv_sem=recv_sems.at[t, r - 1],
                                                device_id=to, device_id_type=_MESH)

        for t in range(nt):
            for layer, g in ((0, g0), (1, g1)):
                @pl.when(c == layer)
                def _():
                    pltpu.make_async_copy(g[t].at[2 * x + y], outs[t].at[me], loc_sems.at[t]).start()
            for r in range(1, 8):
                to, _ = _peer(x, y, c, r)
                for layer, g in ((0, g0), (1, g1)):
                    @pl.when(to[2] == layer)
                    def _():
                        copy(t, r, g[t].at[2 * to[0] + to[1]], me, to).start()
        for t in range(nt):
            for r in range(1, 8):
                _, src_dev = _peer(x, y, c, r)
                copy(t, r, g0[t].at[0], src_dev, (x, y, c)).wait_recv()
        for t in range(nt):
            for r in range(1, 8):
                copy(t, r, g0[t].at[0], me, (x, y, c)).wait_send()
            pltpu.make_async_copy(g0[t].at[0], outs[t].at[me], loc_sems.at[t]).wait()

    flat = list(gws[0]) + list(gws[1])
    return pl.pallas_call(
        body, name="scatter_grads",
        in_specs=[_ANY] * (2 * nt), out_specs=[_ANY] * nt,
        out_shape=[jax.ShapeDtypeStruct((8,) + g.shape[1:], g.dtype) for g in gws[0]],
        scratch_shapes=[pltpu.SemaphoreType.DMA((nt, 7)), pltpu.SemaphoreType.DMA((nt, 7)), pltpu.SemaphoreType.DMA((nt,))],
    )(*flat)


def _sum_slots(st, name):
    _, K, n = st.shape
    tr = next(t for t in (256, 128, 64, 32, 16) if K % t == 0)

    def kern(s_ref, o_ref):
        acc = s_ref[0].astype(_F32)
        for d in range(1, 8):
            acc = acc + s_ref[d].astype(_F32)
        o_ref[...] = acc.astype(o_ref.dtype)

    return pl.pallas_call(
        kern, name=name, grid=(K // tr,),
        in_specs=[pl.BlockSpec((8, tr, n), lambda i: (0, i, 0))], out_specs=_rows(tr, n),
        out_shape=jax.ShapeDtypeStruct((K, n), _ACT),
        compiler_params=_cparams("parallel"),
    )(st)


_PAIR_CHUNKS = 4


def _pair_layers(halves):
    nt = len(halves)

    def body(*refs):
        ins, outs = refs[:nt], refs[nt:2 * nt]
        send_sems, recv_sems, loc_sems = refs[2 * nt:]
        x, y, c = _place()

        def chunk(ref, t, q):
            rows = halves[t].shape[0] // _PAIR_CHUNKS
            return ref.at[pl.ds(q * rows, rows), :]

        def remote(t, q, layer, to):
            return pltpu.make_async_remote_copy(src_ref=chunk(ins[t], t, q), dst_ref=chunk(outs[t].at[layer], t, q),
                                                send_sem=send_sems.at[t, q], recv_sem=recv_sems.at[t, q],
                                                device_id=to, device_id_type=_MESH)

        for t in range(nt):
            pltpu.make_async_copy(ins[t], outs[t].at[c], loc_sems.at[t]).start()
            for q in range(_PAIR_CHUNKS):
                remote(t, q, c, (x, y, 1 - c)).start()
        for t in range(nt):
            for q in range(_PAIR_CHUNKS):
                remote(t, q, 1 - c, (x, y, c)).wait_recv()
        for t in range(nt):
            for q in range(_PAIR_CHUNKS):
                remote(t, q, c, (x, y, 1 - c)).wait_send()
            pltpu.make_async_copy(ins[t], outs[t].at[c], loc_sems.at[t]).wait()

    return pl.pallas_call(
        body, name="pair_layers",
        in_specs=[_ANY] * nt, out_specs=[_ANY] * nt,
        out_shape=[jax.ShapeDtypeStruct((2,) + h.shape, h.dtype) for h in halves],
        scratch_shapes=[pltpu.SemaphoreType.DMA((nt, _PAIR_CHUNKS)), pltpu.SemaphoreType.DMA((nt, _PAIR_CHUNKS)),
                        pltpu.SemaphoreType.DMA((nt,))],
    )(*halves)


def _all_reduce_small(p):
    R = p.shape[0]

    def body(p_ref, o_ref, stage, send_sems, recv_sems):
        x, y, c = _place()
        me = 4 * x + 2 * y + c
        stage[me] = p_ref[...]
        sent = []
        for r in range(1, 8):
            to, _ = _peer(x, y, c, r)
            cp = pltpu.make_async_remote_copy(src_ref=p_ref, dst_ref=stage.at[me], send_sem=send_sems.at[r - 1],
                                              recv_sem=recv_sems.at[r - 1], device_id=to, device_id_type=_MESH)
            cp.start()
            sent.append(cp)
        for r in range(1, 8):
            _, src_dev = _peer(x, y, c, r)
            pltpu.make_async_remote_copy(src_ref=p_ref, dst_ref=stage.at[src_dev], send_sem=send_sems.at[r - 1],
                                         recv_sem=recv_sems.at[r - 1], device_id=(x, y, c), device_id_type=_MESH).wait_recv()
        acc = stage[0]
        for d in range(1, 8):
            acc = acc + stage[d]
        o_ref[...] = acc
        for cp in sent:
            cp.wait_send()

    vm = pl.BlockSpec(memory_space=pltpu.VMEM)
    return pl.pallas_call(
        body, name="all_reduce_small",
        in_specs=[vm], out_specs=vm,
        out_shape=jax.ShapeDtypeStruct((R, _LANES), _F32),
        scratch_shapes=[pltpu.VMEM((8, R, _LANES), _F32), pltpu.SemaphoreType.DMA((7,)), pltpu.SemaphoreType.DMA((7,))],
    )(p)


def _adamw(w, g, m, v, name):
    shape = w.shape
    w2, g2, m2, v2 = (a.reshape(-1, shape[-1]) for a in (w, g, m, v))
    R, C = w2.shape
    tr = next((t for t in (256, 128, 64, 32, 16) if R % t == 0), R)

    def kern(w_ref, g_ref, m_ref, v_ref, gf_ref, d_ref, nm_ref, nv_ref):
        gv = g_ref[...].astype(_F32)
        nm = _B1 * m_ref[...] + (1.0 - _B1) * gv
        nv = _B2 * v_ref[...] + (1.0 - _B2) * (gv * gv)
        m_hat = nm / (1.0 - _B1 ** _STEP)
        v_hat = nv / (1.0 - _B2 ** _STEP)
        gf_ref[...] = gv
        d_ref[...] = -_LR * (m_hat / (jnp.sqrt(v_hat) + _EPS) + _WD * w_ref[...])
        nm_ref[...] = nm
        nv_ref[...] = nv

    outs = pl.pallas_call(
        kern, name=name, grid=(R // tr,),
        in_specs=[_rows(tr, C)] * 4, out_specs=[_rows(tr, C)] * 4,
        out_shape=[jax.ShapeDtypeStruct((R, C), _F32)] * 4,
        compiler_params=_cparams("parallel"),
    )(w2, g2, m2, v2)
    return tuple(o.reshape(shape) for o in outs)


def _pack_small(gss, sq):
    parts = [gss[l][n].reshape(-1) for n in _SMALL for l in range(len(gss))] + [jnp.sum(sq).reshape(1)]
    flat = jnp.concatenate(parts)
    rows = -(-flat.shape[0] // (8 * _LANES)) * 8
    return jnp.pad(flat, (0, rows * _LANES - flat.shape[0])).reshape(rows, _LANES)


def _unpack_small(total, shapes):
    flat = total.reshape(-1)
    out, off = {}, 0
    for n in _SMALL:
        layers = []
        for _ in range(shapes[n][0]):
            size = 1
            for s in shapes[n][1:]:
                size *= s
            layers.append(flat[off:off + size].reshape(shapes[n][1:]))
            off += size
        out[n] = jnp.stack(layers)
    return out, flat[off]


def kernel(x, w_in, b_gate, rel_bias, w_proj_a, w_proj_b, w_out, ln1_g, ln1_b, w_ffn_in, w_ffn_out, ln2_g, ln2_b, loss_target, m_w_in, m_b_gate, m_rel_bias, m_w_proj_a, m_w_proj_b, m_w_out, m_ln1_g, m_ln1_b, m_w_ffn_in, m_w_ffn_out, m_ln2_g, m_ln2_b, v_w_in, v_b_gate, v_rel_bias, v_w_proj_a, v_w_proj_b, v_w_out, v_ln1_g, v_ln1_b, v_w_ffn_in, v_w_ffn_out, v_ln2_g, v_ln2_b):
    names = ("w_in", "b_gate", "rel_bias", "w_proj_a", "w_proj_b", "w_out", "ln1_g", "ln1_b", "w_ffn_in", "w_ffn_out", "ln2_g", "ln2_b")
    w = dict(zip(names, (w_in, b_gate, rel_bias, w_proj_a, w_proj_b, w_out, ln1_g, ln1_b, w_ffn_in, w_ffn_out, ln2_g, ln2_b)))
    m = dict(zip(names, (m_w_in, m_b_gate, m_rel_bias, m_w_proj_a, m_w_proj_b, m_w_out, m_ln1_g, m_ln1_b, m_w_ffn_in, m_w_ffn_out, m_ln2_g, m_ln2_b)))
    v = dict(zip(names, (v_w_in, v_b_gate, v_rel_bias, v_w_proj_a, v_w_proj_b, v_w_out, v_ln1_g, v_ln1_b, v_w_ffn_in, v_w_ffn_out, v_ln2_g, v_ln2_b)))
    T, D = x.shape[-2], x.shape[-1]

    gathered = _all_gather_weights([w[n].astype(_MXU) for n in _DENSE])
    W = dict(zip(_DENSE, gathered))
    small = {n: w[n] for n in _SMALL}
    sq, dx, gws, gss = _local_step(x.reshape(T, D), loss_target.reshape(T, D), W, small)

    def blocks(l, n):
        g = gws[l][n]
        return g if _COL_SHARDED[n] else g.reshape(4, g.shape[0] // 4, g.shape[1])

    slots = _scatter_grads([[blocks(l, n) for n in _DENSE] for l in range(2)])
    halves = [_sum_slots(s, f"sum_grad_{n}") for n, s in zip(_DENSE, slots)]
    grads = dict(zip(_DENSE, _pair_layers(halves)))
    total = _all_reduce_small(_pack_small(gss, sq))
    small_grads, sq_all = _unpack_small(total, {n: w[n].shape for n in _SMALL})
    grads.update(small_grads)
    loss = 0.5 * sq_all / D

    grad, delta, new_m, new_v = {}, {}, {}, {}
    for n in names:
        grad[n], delta[n], new_m[n], new_v[n] = _adamw(w[n], grads[n].reshape(w[n].shape), m[n], v[n], f"adamw_{n}")
    return (loss, dx.reshape(x.shape), *[grad[n] for n in names], *[delta[n] for n in names],
            *[new_m[n] for n in names], *[new_v[n] for n in names])
```

```python
import functools

import jax
import jax.numpy as jnp
from jax import lax
from jax.experimental import pallas as pl
from jax.experimental.pallas import tpu as pltpu

_MXU = jnp.bfloat16
_ACT = jnp.bfloat16
_F32 = jnp.float32

_HEAD = 64
_CHUNK = 64
_LANES = 128
_TQ = 128
_BAND_TILES = 5
_BIAS_TILES = 9
_REL_CLIP = 256
_LN_EPS = 1e-5
_MASKED = -1e30
_EXP_ZERO_BELOW = -104.0
_SB_WINDOW = 3
_VMEM_LIMIT = 56 * 1024 * 1024
_GRAD_ACC_BYTES = 12 * 1024 * 1024

_LR, _B1, _B2, _EPS, _WD, _STEP = 0.001, 0.9, 0.999, 1e-08, 0.01, 10

_MESH = pl.DeviceIdType.MESH


def _dot(a, b):
    return jnp.dot(a, b, preferred_element_type=_F32)


def _dot_nt(a, b):
    return lax.dot_general(a, b, (((1,), (1,)), ((), ())), preferred_element_type=_F32)


def _dot_tn(a, b):
    return lax.dot_general(a, b, (((0,), (0,)), ((), ())), preferred_element_type=_F32)


def _cparams(*sem):
    return pltpu.CompilerParams(dimension_semantics=sem, vmem_limit_bytes=_VMEM_LIMIT)


def _rows(t, c):
    return pl.BlockSpec((t, c), lambda i: (i, 0))


def _whole(shape):
    return pl.BlockSpec(shape, lambda i: tuple(0 for _ in shape))


_ANY = pl.BlockSpec(memory_space=pl.ANY)


def _load_cols(w_hbm, w_vmem, sem):
    n = w_hbm.shape[-1]
    cps = [pltpu.make_async_copy(w_hbm.at[k], w_vmem.at[:, pl.ds(k * n, n)], sem.at[k]) for k in range(4)]
    for cp in cps:
        cp.start()
    for cp in cps:
        cp.wait()


def _load_rows(w_hbm, w_vmem, sem):
    r = w_hbm.shape[-2]
    cps = [pltpu.make_async_copy(w_hbm.at[k], w_vmem.at[pl.ds(k * r, r), :], sem.at[k]) for k in range(4)]
    for cp in cps:
        cp.start()
    for cp in cps:
        cp.wait()


def _ln_stats(u):
    mu = jnp.mean(u, axis=-1, keepdims=True)
    xc = u - mu
    var = jnp.mean(xc * xc, axis=-1, keepdims=True)
    rstd = lax.rsqrt(var + _LN_EPS)
    return xc * rstd, rstd


def _ln_bwd(u, dy, gamma):
    xhat, rstd = _ln_stats(u)
    dxh = dy * gamma
    m1 = jnp.mean(dxh, axis=-1, keepdims=True)
    m2 = jnp.mean(dxh * xhat, axis=-1, keepdims=True)
    du = rstd * (dxh - m1 - xhat * m2)
    return du, jnp.sum(dy * xhat, axis=0, keepdims=True), jnp.sum(dy, axis=0, keepdims=True), xhat


def _divisor_tile(n, cap):
    best = None
    for t in range(_LANES, min(n, cap) + 1, _LANES):
        if n % t == 0:
            best = t
    return best or n


class _Comm:
    def __init__(self, inputs, out_shapes, sems, run):
        self.inputs, self.out_shapes, self.sems, self.run = list(inputs), list(out_shapes), list(sems), run


def _call(kern, comm, *, name, grid, in_specs, out_specs, out_shape, scratch_shapes, args, semantics):
    in_specs, out_specs, out_shape, scratch_shapes = list(in_specs), list(out_specs), list(out_shape), list(scratch_shapes)
    if comm is None:
        outs = pl.pallas_call(kern, name=name, grid=grid, in_specs=in_specs, out_specs=out_specs, out_shape=out_shape,
                              scratch_shapes=scratch_shapes, compiler_params=_cparams(*semantics))(*args)
        return list(outs), []
    n_in, n_out, n_scr = len(in_specs), len(out_specs), len(scratch_shapes)
    ci, co = len(comm.inputs), len(comm.out_shapes)
    nsteps = functools.reduce(lambda a, b: a * b, grid, 1)

    def fused(*refs):
        a, b = n_in, n_in + ci
        c, d = b + n_out, b + n_out + co
        e = d + n_scr
        step = pl.program_id(0)
        for ax in range(1, len(grid)):
            step = step * grid[ax] + pl.program_id(ax)
        comm.run(step, nsteps, refs[a:b], refs[c:d], refs[e:])
        kern(*refs[:a], *refs[b:c], *refs[d:e])

    outs = pl.pallas_call(
        fused, name=name, grid=grid, in_specs=in_specs + [_ANY] * ci, out_specs=out_specs + [_ANY] * co,
        out_shape=out_shape + comm.out_shapes, scratch_shapes=scratch_shapes + comm.sems,
        compiler_params=_cparams(*("arbitrary" for _ in grid)))(*args, *comm.inputs)
    return list(outs[:n_out]), list(outs[n_out:])


def _comm_only(comm, name):
    def body(*refs):
        ci, co = len(comm.inputs), len(comm.out_shapes)
        comm.run(0, 1, refs[:ci], refs[ci:ci + co], refs[ci + co:])

    outs = pl.pallas_call(body, name=name, in_specs=[_ANY] * len(comm.inputs), out_specs=[_ANY] * len(comm.out_shapes),
                          out_shape=comm.out_shapes, scratch_shapes=comm.sems)(*comm.inputs)
    return list(outs)


def _in_proj(x, w_in, layer):
    T, D = x.shape
    N = 4 * w_in.shape[-1]
    NQ = N - 2 * D
    tm = 256

    def kern(x_ref, w_hbm, hq_ref, hg_ref, w_v, sem):
        @pl.when(pl.program_id(0) == 0)
        def _():
            _load_cols(w_hbm, w_v, sem)

        xb = x_ref[...].astype(_MXU)
        hq_ref[...] = _dot(xb, w_v[:, :NQ]).astype(hq_ref.dtype)
        hg_ref[...] = _dot(xb, w_v[:, NQ:])

    return pl.pallas_call(
        kern, name=f"in_proj_{layer}", grid=(T // tm,),
        in_specs=[_rows(tm, D), _ANY],
        out_specs=[_rows(tm, NQ), _rows(tm, 2 * D)],
        out_shape=[jax.ShapeDtypeStruct((T, NQ), _ACT), jax.ShapeDtypeStruct((T, 2 * D), _F32)],
        scratch_shapes=[pltpu.VMEM((D, N), w_in.dtype), pltpu.SemaphoreType.DMA((4,))],
        compiler_params=_cparams("arbitrary"),
    )(x, w_in)


def _mix_fwd(oa, ob, hg, x, wpa, wpb, wo, bg, gamma, beta, alpha, layer):
    T, D = x.shape
    WA, WB = oa.shape[1], ob.shape[1]
    tm = 256

    def kern(oa_ref, ob_ref, hg_ref, x_ref, bg_ref, g_ref, b_ref, wpa_h, wpb_h, wo_h,
             x1_ref, u1_ref, pre_ref, ya_ref, yb_ref, wpa_v, wpb_v, wo_v, sa, sb, so):
        @pl.when(pl.program_id(0) == 0)
        def _():
            _load_cols(wpa_h, wpa_v, sa)
            _load_cols(wpb_h, wpb_v, sb)
            _load_rows(wo_h, wo_v, so)

        ya = _dot(oa_ref[...].astype(_MXU), wpa_v[...])
        yb = _dot(ob_ref[...].astype(_MXU), wpb_v[...])
        hgv = hg_ref[...]
        bgv = bg_ref[...]
        ga = jax.nn.sigmoid(hgv[:, :D] + bgv[:, :D])
        gb = jax.nn.sigmoid(hgv[:, D:] + bgv[:, D:])
        pre = ga * ya + gb * yb
        mix = _dot(pre.astype(_MXU), wo_v[...])
        u = alpha * x_ref[...] + mix
        xhat, _ = _ln_stats(u)
        x1_ref[...] = xhat * g_ref[...] + b_ref[...]
        u1_ref[...] = u
        pre_ref[...] = pre.astype(pre_ref.dtype)
        ya_ref[...] = ya.astype(ya_ref.dtype)
        yb_ref[...] = yb.astype(yb_ref.dtype)

    return pl.pallas_call(
        kern, name=f"mix_fwd_{layer}", grid=(T // tm,),
        in_specs=[_rows(tm, WA), _rows(tm, WB), _rows(tm, 2 * D), _rows(tm, D),
                  _whole((1, 2 * D)), _whole((1, D)), _whole((1, D)), _ANY, _ANY, _ANY],
        out_specs=[_rows(tm, D)] * 5,
        out_shape=[jax.ShapeDtypeStruct((T, D), _F32), jax.ShapeDtypeStruct((T, D), _F32)]
        + [jax.ShapeDtypeStruct((T, D), _ACT)] * 3,
        scratch_shapes=[pltpu.VMEM((WA, D), wpa.dtype), pltpu.VMEM((WB, D), wpb.dtype), pltpu.VMEM((D, D), wo.dtype),
                        pltpu.SemaphoreType.DMA((4,)), pltpu.SemaphoreType.DMA((4,)), pltpu.SemaphoreType.DMA((4,))],
        compiler_params=_cparams("arbitrary"),
    )(oa, ob, hg, x, bg, gamma, beta, wpa, wpb, wo)


def _ffn_fwd(x1, wfi, wfo, gamma, beta, alpha, layer):
    T, D = x1.shape
    F2 = 4 * wfi.shape[-1]
    F = F2 // 2
    tm = 256
    fc = F // 2

    def kern(x_ref, g_ref, b_ref, wi_h, wo_h, x2_ref, u2_ref, act_ref, gu_ref, wi_v, wo_v, si, so):
        @pl.when(pl.program_id(0) == 0)
        def _():
            _load_cols(wi_h, wi_v, si)
            _load_rows(wo_h, wo_v, so)

        x = x_ref[...]
        xb = x.astype(_MXU)
        ffn = jnp.zeros((tm, D), _F32)
        for c in range(2):
            g = _dot(xb, wi_v[:, c * fc:(c + 1) * fc])
            u = _dot(xb, wi_v[:, F + c * fc:F + (c + 1) * fc])
            act = g * jax.nn.sigmoid(g) * u
            ab = act.astype(_MXU)
            ffn = ffn + _dot(ab, wo_v[c * fc:(c + 1) * fc, :])
            act_ref[:, c * fc:(c + 1) * fc] = ab.astype(act_ref.dtype)
            gu_ref[:, c * fc:(c + 1) * fc] = g.astype(gu_ref.dtype)
            gu_ref[:, F + c * fc:F + (c + 1) * fc] = u.astype(gu_ref.dtype)
        uu = alpha * x + ffn
        xhat, _ = _ln_stats(uu)
        x2_ref[...] = xhat * g_ref[...] + b_ref[...]
        u2_ref[...] = uu

    return pl.pallas_call(
        kern, name=f"ffn_fwd_{layer}", grid=(T // tm,),
        in_specs=[_rows(tm, D), _whole((1, D)), _whole((1, D)), _ANY, _ANY],
        out_specs=[_rows(tm, D), _rows(tm, D), _rows(tm, F), _rows(tm, F2)],
        out_shape=[jax.ShapeDtypeStruct((T, D), _F32), jax.ShapeDtypeStruct((T, D), _F32),
                   jax.ShapeDtypeStruct((T, F), _ACT), jax.ShapeDtypeStruct((T, F2), _ACT)],
        scratch_shapes=[pltpu.VMEM((D, F2), wfi.dtype), pltpu.VMEM((F, D), wfo.dtype),
                        pltpu.SemaphoreType.DMA((4,)), pltpu.SemaphoreType.DMA((4,))],
        compiler_params=_cparams("arbitrary"),
    )(x1, gamma, beta, wfi, wfo)


def _ffn_bwd_a(u2, dy_or_target, gu, gamma, beta, wfo, layer, last):
    T, D = u2.shape
    F2 = gu.shape[1]
    F = F2 // 2
    tm = 256
    fc = F // 2

    def kern(u_ref, dy_ref, gu_ref, g_ref, b_ref, wo_h, du_ref, dub_ref, dgu_ref, st_ref, wo_v, so):
        @pl.when(pl.program_id(0) == 0)
        def _():
            _load_rows(wo_h, wo_v, so)
            st_ref[...] = jnp.zeros_like(st_ref)

        gam = g_ref[...]
        u = u_ref[...]
        if last:
            xhat0, _ = _ln_stats(u)
            err = xhat0 * gam + b_ref[...] - dy_ref[...]
            dy = err * (1.0 / D)
            st_ref[2:3, :] += jnp.sum(err * err, axis=0, keepdims=True)
        else:
            dy = dy_ref[...]
        du, dgam, dbet, _ = _ln_bwd(u, dy, gam)
        st_ref[0:1, :] += dgam
        st_ref[1:2, :] += dbet
        du_ref[...] = du
        dub = du.astype(_MXU)
        dub_ref[...] = dub.astype(dub_ref.dtype)
        for c in range(2):
            dact = _dot_nt(dub, wo_v[c * fc:(c + 1) * fc, :])
            g = gu_ref[:, c * fc:(c + 1) * fc].astype(_F32)
            uu = gu_ref[:, F + c * fc:F + (c + 1) * fc].astype(_F32)
            sg = jax.nn.sigmoid(g)
            dgu_ref[:, c * fc:(c + 1) * fc] = (dact * uu * (sg * (1.0 + g * (1.0 - sg)))).astype(dgu_ref.dtype)
            dgu_ref[:, F + c * fc:F + (c + 1) * fc] = (dact * (g * sg)).astype(dgu_ref.dtype)

    return pl.pallas_call(
        kern, name=f"ffn_bwd_a_{layer}", grid=(T // tm,),
        in_specs=[_rows(tm, D), _rows(tm, D), _rows(tm, F2), _whole((1, D)), _whole((1, D)), _ANY],
        out_specs=[_rows(tm, D), _rows(tm, D), _rows(tm, F2), _whole((8, D))],
        out_shape=[jax.ShapeDtypeStruct((T, D), _F32), jax.ShapeDtypeStruct((T, D), _ACT),
                   jax.ShapeDtypeStruct((T, F2), _ACT), jax.ShapeDtypeStruct((8, D), _F32)],
        scratch_shapes=[pltpu.VMEM((F, D), wfo.dtype), pltpu.SemaphoreType.DMA((4,))],
        compiler_params=_cparams("arbitrary"),
    )(u2, dy_or_target, gu, gamma, beta, wfo)


def _residual_nt(res, res_scale, d, w, name, comm=None):
    T, K = res.shape
    N = d.shape[1]
    tm = 256

    def kern(r_ref, d_ref, w_hbm, o_ref, w_v, sem):
        @pl.when(pl.program_id(0) == 0)
        def _():
            _load_cols(w_hbm, w_v, sem)

        o_ref[...] = res_scale * r_ref[...] + _dot_nt(d_ref[...].astype(_MXU), w_v[...])

    outs, extra = _call(
        kern, comm, name=name, grid=(T // tm,),
        in_specs=[_rows(tm, K), _rows(tm, N), _ANY], out_specs=[_rows(tm, K)],
        out_shape=[jax.ShapeDtypeStruct((T, K), _F32)],
        scratch_shapes=[pltpu.VMEM((K, N), w.dtype), pltpu.SemaphoreType.DMA((4,))],
        args=(res, d, w), semantics=("arbitrary",))
    return outs[0], extra


def _mix_bwd(u1, dx1, ya, yb, hg, wpa, wpb, wo, bg, gamma, alpha, layer):
    del alpha
    T, D = u1.shape
    WA, WB = wpa.shape[-2], wpb.shape[-2]
    tm = 256

    def kern(u_ref, dx_ref, ya_ref, yb_ref, hg_ref, bg_ref, g_ref, wpa_h, wpb_h, wo_h,
             du_ref, dub_ref, dya_ref, dyb_ref, dhg_ref, doa_ref, dob_ref, st_ref,
             wpa_v, wpb_v, wo_v, sa, sb, so):
        @pl.when(pl.program_id(0) == 0)
        def _():
            _load_cols(wpa_h, wpa_v, sa)
            _load_cols(wpb_h, wpb_v, sb)
            _load_rows(wo_h, wo_v, so)
            st_ref[...] = jnp.zeros_like(st_ref)

        du, dgam, dbet, _ = _ln_bwd(u_ref[...], dx_ref[...], g_ref[...])
        st_ref[1:2, :D] += dgam
        st_ref[1:2, D:] += dbet
        du_ref[...] = du
        dub = du.astype(_MXU)
        dub_ref[...] = dub.astype(dub_ref.dtype)
        dpre = _dot_nt(dub, wo_v[...])
        hgv = hg_ref[...]
        bgv = bg_ref[...]
        ga = jax.nn.sigmoid(hgv[:, :D] + bgv[:, :D])
        gb = jax.nn.sigmoid(hgv[:, D:] + bgv[:, D:])
        dya = (dpre * ga).astype(_MXU)
        dyb = (dpre * gb).astype(_MXU)
        dsa = dpre * ya_ref[...].astype(_F32) * (ga * (1.0 - ga))
        dsb = dpre * yb_ref[...].astype(_F32) * (gb * (1.0 - gb))
        st_ref[0:1, :D] += jnp.sum(dsa, axis=0, keepdims=True)
        st_ref[0:1, D:] += jnp.sum(dsb, axis=0, keepdims=True)
        dya_ref[...] = dya.astype(dya_ref.dtype)
        dyb_ref[...] = dyb.astype(dyb_ref.dtype)
        dhg_ref[:, :D] = dsa.astype(dhg_ref.dtype)
        dhg_ref[:, D:] = dsb.astype(dhg_ref.dtype)
        doa_ref[...] = _dot_nt(dya, wpa_v[...]).astype(doa_ref.dtype)
        dob_ref[...] = _dot_nt(dyb, wpb_v[...]).astype(dob_ref.dtype)

    return pl.pallas_call(
        kern, name=f"mix_bwd_{layer}", grid=(T // tm,),
        in_specs=[_rows(tm, D)] * 4 + [_rows(tm, 2 * D), _whole((1, 2 * D)), _whole((1, D)), _ANY, _ANY, _ANY],
        out_specs=[_rows(tm, D)] * 4 + [_rows(tm, 2 * D), _rows(tm, WA), _rows(tm, WB), _whole((8, 2 * D))],
        out_shape=[jax.ShapeDtypeStruct((T, D), _F32)] + [jax.ShapeDtypeStruct((T, D), _ACT)] * 3
        + [jax.ShapeDtypeStruct((T, 2 * D), _ACT), jax.ShapeDtypeStruct((T, WA), _ACT),
           jax.ShapeDtypeStruct((T, WB), _ACT), jax.ShapeDtypeStruct((8, 2 * D), _F32)],
        scratch_shapes=[pltpu.VMEM((WA, D), wpa.dtype), pltpu.VMEM((WB, D), wpb.dtype), pltpu.VMEM((D, D), wo.dtype),
                        pltpu.SemaphoreType.DMA((4,)), pltpu.SemaphoreType.DMA((4,)), pltpu.SemaphoreType.DMA((4,))],
        compiler_params=_cparams("arbitrary"),
    )(u1, dx1, ya, yb, hg, bg, gamma, wpa, wpb, wo)


def _grad_w(a, b, *, col_shards, name, comm=None):
    T, M = a.shape
    N = b.shape[1]
    tk = 512
    n = N // 4 if col_shards else N
    whole = M * N * 4 <= _GRAD_ACC_BYTES
    tn = N if whole else (n if col_shards else _divisor_tile(N, _GRAD_ACC_BYTES // (4 * M)))
    nk = T // tk

    def kern(a_ref, b_ref, o_ref, acc):
        k = pl.program_id(1)

        @pl.when(k == 0)
        def _():
            acc[...] = jnp.zeros_like(acc)

        acc[...] += _dot_tn(a_ref[...].astype(_MXU), b_ref[...].astype(_MXU))

        @pl.when(k == nk - 1)
        def _():
            if col_shards and whole:
                for s in range(4):
                    o_ref[s] = acc[:, s * n:(s + 1) * n].astype(o_ref.dtype)
            else:
                o_ref[...] = acc[...].astype(o_ref.dtype)

    if col_shards:
        out_spec = (pl.BlockSpec((4, M, n), lambda j, k: (0, 0, 0)) if whole
                    else pl.BlockSpec((None, M, n), lambda j, k: (j, 0, 0)))
        out_shape = jax.ShapeDtypeStruct((4, M, n), _ACT)
    else:
        out_spec = pl.BlockSpec((M, tn), lambda j, k: (0, j))
        out_shape = jax.ShapeDtypeStruct((M, N), _ACT)
    outs, extra = _call(
        kern, comm, name=name, grid=(N // tn, nk),
        in_specs=[pl.BlockSpec((tk, M), lambda j, k: (k, 0)), pl.BlockSpec((tk, tn), lambda j, k: (k, j))],
        out_specs=[out_spec], out_shape=[out_shape], scratch_shapes=[pltpu.VMEM((M, tn), _F32)],
        args=(a, b), semantics=("parallel", "arbitrary"))
    return outs[0], extra


def _bias_tiles(rel):
    H = rel.shape[0]
    span = _TQ * _BAND_TILES - 1
    edge = span - _REL_CLIP
    gvec = jnp.concatenate([jnp.broadcast_to(rel[:, :1], (H, edge)), rel, jnp.broadcast_to(rel[:, -1:], (H, edge))], axis=1)
    width = _BIAS_TILES * _TQ
    period = width + _TQ
    tiled = jnp.broadcast_to(jnp.pad(gvec[:, ::-1], ((0, 0), (0, 1)))[:, None, :], (H, _TQ, period))
    rows = tiled.reshape(H, _TQ * period)[:, :_TQ * (period - 1)].reshape(H, _TQ, period - 1)[:, :, _TQ - 1:]
    r = jnp.arange(_TQ)[:, None]
    u = jnp.arange(width)[None, :]
    d = 4 * _TQ + r - u
    rm = r % _CHUNK
    valid = (d >= rm - (_CHUNK - 1)) & (d <= rm + 8 * _CHUNK)
    tiles = jnp.where(valid[None], rows, _MASKED)
    return tiles.reshape(H // 2, 2 * _TQ, _BIAS_TILES, _TQ).transpose(0, 2, 1, 3)


def _fold_bias_grad(db):
    H = 2 * db.shape[0]
    width = _BIAS_TILES * _TQ
    period = width + _TQ
    x = jnp.pad(db.transpose(0, 2, 1, 3).reshape(H, _TQ, width), ((0, 0), (0, 0), (_TQ - 1, 0)))
    skew = jnp.pad(x.reshape(H, _TQ * (period - 1)), ((0, 0), (0, _TQ))).reshape(H, _TQ, period)
    dg = skew.sum(axis=1)[:, :period - 1][:, ::-1]
    span = _TQ * _BAND_TILES - 1
    edge = span - _REL_CLIP
    mid = dg[:, edge:edge + 2 * _REL_CLIP + 1]
    lo = dg[:, :edge].sum(axis=1)
    hi = dg[:, edge + 2 * _REL_CLIP + 1:].sum(axis=1)
    return mid.at[:, 0].add(lo).at[:, -1].add(hi)


def _band_window(i):
    j0 = jnp.maximum(i - (_BAND_TILES - 1), 0)
    return j0, (_BAND_TILES - 1) - (i - j0)


def _head_masks():
    lane = lax.broadcasted_iota(jnp.int32, (1, _LANES), 1)
    return [(lane // _HEAD) == hh for hh in range(2)]


def _stack_heads(x, masks):
    return jnp.concatenate([jnp.where(m, x, jnp.zeros_like(x)) for m in masks], axis=0)


def _unstack_heads(y, masks):
    return jnp.where(masks[0], y[:_TQ], y[_TQ:])


def _scaled(q):
    return q * jnp.asarray(_HEAD ** -0.5, q.dtype)


def _band_probs(q2, k_ref, b_ref, j0, boff):
    s = []
    for j in range(_BAND_TILES):
        kj = k_ref[pl.ds(pl.multiple_of((j0 + j) * _TQ, _TQ), _TQ), :]
        s.append(_dot_nt(q2, kj) + b_ref[boff + j])
    m = jnp.max(functools.reduce(jnp.maximum, s), axis=-1, keepdims=True)
    p = [jnp.exp(x - m) for x in s]
    l = jnp.sum(functools.reduce(lambda a, b: a + b, p), axis=-1, keepdims=True)
    return p, 1.0 / l


def _qkv_specs(T, cb, npair):
    return [pl.BlockSpec((_TQ, _LANES), lambda h, i: (i, cb + h)),
            pl.BlockSpec((T, _LANES), lambda h, i: (0, cb + npair + h)),
            pl.BlockSpec((T, _LANES), lambda h, i: (0, cb + 2 * npair + h))]


def _attn_a_fwd(hq, bias, col0, width, layer, comm=None):
    T = hq.shape[0]
    npair = width // _LANES
    nq = T // _TQ

    def kern(q_ref, k_ref, v_ref, b_ref, o_ref):
        i = pl.program_id(1)
        j0, boff = _band_window(i)
        masks = _head_masks()
        q2 = _stack_heads(_scaled(q_ref[...]), masks)
        p, inv = _band_probs(q2, k_ref, b_ref, j0, boff)
        o = jnp.zeros((2 * _TQ, _LANES), _F32)
        for j in range(_BAND_TILES):
            vj = v_ref[pl.ds(pl.multiple_of((j0 + j) * _TQ, _TQ), _TQ), :]
            o = o + _dot(p[j].astype(_MXU), vj)
        o_ref[...] = _unstack_heads(o * inv, masks).astype(o_ref.dtype)

    outs, extra = _call(
        kern, comm, name=f"band_attn_fwd_{layer}", grid=(npair, nq),
        in_specs=_qkv_specs(T, col0 // _LANES, npair)
        + [pl.BlockSpec((None, _BIAS_TILES, 2 * _TQ, _TQ), lambda h, i: (h, 0, 0, 0))],
        out_specs=[pl.BlockSpec((_TQ, _LANES), lambda h, i: (i, h))],
        out_shape=[jax.ShapeDtypeStruct((T, width), _ACT)], scratch_shapes=[],
        args=(hq, hq, hq, bias), semantics=("arbitrary", "arbitrary"))
    return outs[0], extra


def _attn_a_bwd(hq, bias, do, col0, width, layer, comm=None):
    T = hq.shape[0]
    npair = width // _LANES
    nq = T // _TQ
    scale = _HEAD ** -0.5

    def kern(q_ref, k_ref, v_ref, b_ref, do_ref, dq_ref, dk_ref, dv_ref, db_ref, dk_acc, dv_acc):
        i = pl.program_id(1)

        @pl.when(i == 0)
        def _():
            dk_acc[...] = jnp.zeros_like(dk_acc)
            dv_acc[...] = jnp.zeros_like(dv_acc)
            db_ref[...] = jnp.zeros_like(db_ref)

        j0, boff = _band_window(i)
        masks = _head_masks()
        q2 = _stack_heads(_scaled(q_ref[...]), masks)
        do2 = _stack_heads(do_ref[...], masks).astype(_MXU)
        p, inv = _band_probs(q2, k_ref, b_ref, j0, boff)
        rows = [pl.ds(pl.multiple_of((j0 + j) * _TQ, _TQ), _TQ) for j in range(_BAND_TILES)]
        p = [x * inv for x in p]
        dp = [_dot_nt(do2, v_ref[rows[j], :]) for j in range(_BAND_TILES)]
        delta = jnp.sum(functools.reduce(lambda a, b: a + b, [p[j] * dp[j] for j in range(_BAND_TILES)]),
                        axis=-1, keepdims=True)
        dq = jnp.zeros((2 * _TQ, _LANES), _F32)
        for j in range(_BAND_TILES):
            ds = p[j] * (dp[j] - delta)
            db_ref[boff + j] += ds
            dsb = ds.astype(_MXU)
            dq = dq + _dot(dsb, k_ref[rows[j], :])
            dk_acc[rows[j], :] += _dot_tn(dsb, q2)
            dv_acc[rows[j], :] += _dot_tn(p[j].astype(_MXU), do2)
        dq_ref[...] = (_unstack_heads(dq, masks) * scale).astype(dq_ref.dtype)

        @pl.when(i == nq - 1)
        def _():
            dk_ref[...] = dk_acc[...].astype(dk_ref.dtype)
            dv_ref[...] = dv_acc[...].astype(dv_ref.dtype)

    strip = pl.BlockSpec((None, _BIAS_TILES, 2 * _TQ, _TQ), lambda h, i: (h, 0, 0, 0))
    tile = pl.BlockSpec((_TQ, _LANES), lambda h, i: (i, h))
    column = pl.BlockSpec((T, _LANES), lambda h, i: (0, h))
    outs, extra = _call(
        kern, comm, name=f"band_attn_bwd_{layer}", grid=(npair, nq),
        in_specs=_qkv_specs(T, col0 // _LANES, npair) + [strip, tile],
        out_specs=[tile, column, column, strip],
        out_shape=[jax.ShapeDtypeStruct((T, width), _ACT)] * 3
        + [jax.ShapeDtypeStruct((npair, _BIAS_TILES, 2 * _TQ, _TQ), _F32)],
        scratch_shapes=[pltpu.VMEM((T, _LANES), _F32), pltpu.VMEM((T, _LANES), _F32)],
        args=(hq, hq, hq, bias, do), semantics=("arbitrary", "arbitrary"))
    return outs, extra


def _suffix_matrix():
    r = lax.broadcasted_iota(jnp.int32, (_TQ, _TQ), 0)
    c = lax.broadcasted_iota(jnp.int32, (_TQ, _TQ), 1)
    r2 = lax.broadcasted_iota(jnp.int32, (2 * _TQ, _TQ), 0)
    c2 = lax.broadcasted_iota(jnp.int32, (2 * _TQ, _TQ), 1)
    return (r > c).astype(_MXU), c2 - (r2 & (_TQ - 1))


def _suffix_sum(x, tri):
    n = x.shape[0]
    hi = x.astype(_MXU)
    lo = (x - hi.astype(_F32)).astype(_MXU)
    y = _dot(jnp.concatenate([hi, lo], axis=0), tri)
    return y[:n] + y[n:]


def _stick_tile(qs, kj, jj, rel, carry_l, tri):
    z = _dot_nt(qs, kj)
    nsp = -(jnp.maximum(z, 0.0) + jnp.log(1.0 + jnp.exp(-jnp.abs(z))))
    if isinstance(jj, int):
        mask = (rel < 0) if jj == 0 else None
    else:
        mask = rel < jnp.where(jj == 0, 0, _TQ)
    L = nsp if mask is None else jnp.where(mask, nsp, 0.0)
    w = jnp.exp(z + L + _suffix_sum(L, tri) + carry_l)
    if mask is not None:
        w = jnp.where(mask, w, 0.0)
    return z, L, w, mask


def _sweep_done(i, jj, cl):
    return jnp.logical_or(jj > i, jnp.max(cl) < _EXP_ZERO_BELOW)


def _sweep(i, tile, zero):
    def window():
        c = zero
        for jj in range(_SB_WINDOW):
            c = tile(jj, c)
        return (jnp.int32(_SB_WINDOW),) + c

    start = lax.cond(i >= _SB_WINDOW - 1, window, lambda: (jnp.int32(0),) + zero)
    return lax.while_loop(lambda c: jnp.logical_not(_sweep_done(i, c[0], c[1])),
                          lambda c: (c[0] + 1,) + tile(c[0], c[1:]), start)


def _sb_fwd(hq, col0, width, layer, comm=None):
    T = hq.shape[0]
    npair = width // _LANES
    nq = T // _TQ

    def kern(q_ref, k_ref, v_ref, o_ref):
        i = pl.program_id(1)
        masks = _head_masks()
        tri, rel = _suffix_matrix()
        q2 = _stack_heads(_scaled(q_ref[...]), masks)

        def tile(jj, c):
            cl, a = c
            rows = pl.ds(pl.multiple_of((i - jj) * _TQ, _TQ), _TQ)
            _, L, w, _ = _stick_tile(q2, k_ref[rows, :], jj, rel, cl, tri)
            return cl + jnp.sum(L, axis=-1, keepdims=True), a + _dot(w.astype(_MXU), v_ref[rows, :])

        out = _sweep(i, tile, (jnp.zeros((2 * _TQ, 1), _F32), jnp.zeros((2 * _TQ, _LANES), _F32)))
        o_ref[...] = _unstack_heads(out[2], masks)

    outs, extra = _call(
        kern, comm, name=f"stick_attn_fwd_{layer}", grid=(npair, nq),
        in_specs=_qkv_specs(T, col0 // _LANES, npair),
        out_specs=[pl.BlockSpec((_TQ, _LANES), lambda h, i: (i, h))],
        out_shape=[jax.ShapeDtypeStruct((T, width), _F32)], scratch_shapes=[],
        args=(hq, hq, hq), semantics=("arbitrary", "arbitrary"))
    return outs[0], extra


def _sb_bwd(hq, o, do, col0, width, layer, comm=None):
    T = hq.shape[0]
    npair = width // _LANES
    nq = T // _TQ
    scale = _HEAD ** -0.5

    def kern(q_ref, k_ref, v_ref, o_ref, do_ref, dq_ref, dk_ref, dv_ref, dk_acc, dv_acc):
        i = pl.program_id(1)

        @pl.when(i == 0)
        def _():
            dk_acc[...] = jnp.zeros_like(dk_acc)
            dv_acc[...] = jnp.zeros_like(dv_acc)

        masks = _head_masks()
        tri, rel = _suffix_matrix()
        q2 = _stack_heads(_scaled(q_ref[...]), masks)
        do_t = do_ref[...]
        do2 = _stack_heads(do_t, masks).astype(_MXU)
        dsum = jnp.sum(_stack_heads(do_t.astype(_F32) * o_ref[...], masks), axis=-1, keepdims=True)

        def tile(jj, c):
            cl, cg, dq = c
            rows = pl.ds(pl.multiple_of((i - jj) * _TQ, _TQ), _TQ)
            kj = k_ref[rows, :]
            vj = v_ref[rows, :]
            z, L, w, mask = _stick_tile(q2, kj, jj, rel, cl, tri)
            wb = w.astype(_MXU)
            g = wb.astype(_F32) * _dot_nt(do2, vj)
            gs = _suffix_sum(g, tri) + cg
            dz = g - jnp.exp(z + L) * (dsum - gs)
            if mask is not None:
                dz = jnp.where(mask, dz, 0.0)
            dzb = dz.astype(_MXU)
            dk_acc[rows, :] += _dot_tn(dzb, q2)
            dv_acc[rows, :] += _dot_tn(wb, do2)
            return (cl + jnp.sum(L, axis=-1, keepdims=True), cg + jnp.sum(g, axis=-1, keepdims=True),
                    dq + _dot(dzb, kj))

        zc = jnp.zeros((2 * _TQ, 1), _F32)
        out = _sweep(i, tile, (zc, zc, jnp.zeros((2 * _TQ, _LANES), _F32)))
        dq_ref[...] = (_unstack_heads(out[3], masks) * scale).astype(dq_ref.dtype)

        @pl.when(i == nq - 1)
        def _():
            dk_ref[...] = dk_acc[...].astype(dk_ref.dtype)
            dv_ref[...] = dv_acc[...].astype(dv_ref.dtype)

    tile_spec = pl.BlockSpec((_TQ, _LANES), lambda h, i: (i, h))
    column = pl.BlockSpec((T, _LANES), lambda h, i: (0, h))
    outs, extra = _call(
        kern, comm, name=f"stick_attn_bwd_{layer}", grid=(npair, nq),
        in_specs=_qkv_specs(T, col0 // _LANES, npair) + [tile_spec, tile_spec],
        out_specs=[tile_spec, column, column],
        out_shape=[jax.ShapeDtypeStruct((T, width), _ACT)] * 3,
        scratch_shapes=[pltpu.VMEM((T, _LANES), _F32), pltpu.VMEM((T, _LANES), _F32)],
        args=(hq, hq, hq, o, do), semantics=("arbitrary", "arbitrary"))
    return outs, extra


_DENSE = ("w_in", "w_proj_a", "w_proj_b", "w_out", "w_ffn_in", "w_ffn_out")
_COL_SHARDED = {"w_in": True, "w_proj_a": True, "w_proj_b": True, "w_out": False, "w_ffn_in": True, "w_ffn_out": False}
_SMALL = ("b_gate", "rel_bias", "ln1_g", "ln1_b", "ln2_g", "ln2_b")


class _Plans:
    def __init__(self, plans=None):
        self.plans = plans or {}

    def start(self, key):
        if key not in self.plans:
            return None, None
        return self.plans[key]()

    @staticmethod
    def finish(done, extra):
        if done is not None:
            done(extra)


def _layer_fwd(x, W, small, l, alpha, plans):
    WA = small["rel_bias"].shape[1] * _HEAD
    row = lambda v: v[l].reshape(1, -1)
    hq, hg = _in_proj(x, W["w_in"], l)
    WB = (hq.shape[1] - 3 * WA) // 3
    bias = _bias_tiles(small["rel_bias"][l])
    comm, done = plans.start(f"band_fwd_{l}")
    oa, extra = _attn_a_fwd(hq, bias, 0, WA, l, comm)
    plans.finish(done, extra)
    comm, done = plans.start(f"stick_fwd_{l}")
    ob, extra = _sb_fwd(hq, 3 * WA, WB, l, comm)
    plans.finish(done, extra)
    x1, u1, pre, ya, yb = _mix_fwd(oa, ob, hg, x, W["w_proj_a"], W["w_proj_b"], W["w_out"],
                                   row(small["b_gate"]), row(small["ln1_g"]), row(small["ln1_b"]), alpha, l)
    x2, u2, act, gu = _ffn_fwd(x1, W["w_ffn_in"], W["w_ffn_out"], row(small["ln2_g"]), row(small["ln2_b"]), alpha, l)
    return x2, dict(x=x, hq=hq, hg=hg, bias=bias, oa=oa, ob=ob, x1=x1, u1=u1, pre=pre, ya=ya, yb=yb, u2=u2, act=act, gu=gu)


def _layer_bwd(dy_or_target, S, W, small, l, last, alpha, plans, gw):
    D = S["x"].shape[1]
    WA, WB = S["oa"].shape[1], S["ob"].shape[1]
    row = lambda v: v[l].reshape(1, -1)

    def blocks(g, n):
        return g if _COL_SHARDED[n] else g.reshape(4, g.shape[0] // 4, g.shape[1])

    du2, du2b, dgu, st2 = _ffn_bwd_a(S["u2"], dy_or_target, S["gu"], row(small["ln2_g"]), row(small["ln2_b"]),
                                     W["w_ffn_out"], l, last)
    dx1, _ = _residual_nt(du2, alpha, dgu, W["w_ffn_in"], f"ffn_bwd_b_{l}")
    gw["w_ffn_in"] = blocks(_grad_w(S["x1"], dgu, col_shards=True, name=f"grad_w_ffn_in_{l}")[0], "w_ffn_in")
    gw["w_ffn_out"] = blocks(_grad_w(S["act"], du2b, col_shards=False, name=f"grad_w_ffn_out_{l}")[0], "w_ffn_out")
    du1, du1b, dya, dyb, dhg, doa, dob, st1 = _mix_bwd(S["u1"], dx1, S["ya"], S["yb"], S["hg"], W["w_proj_a"],
                                                       W["w_proj_b"], W["w_out"], row(small["b_gate"]),
                                                       row(small["ln1_g"]), alpha, l)
    gw["w_out"] = blocks(_grad_w(S["pre"], du1b, col_shards=False, name=f"grad_w_out_{l}")[0], "w_out")
    gw["w_proj_a"] = blocks(_grad_w(S["oa"], dya, col_shards=True, name=f"grad_w_proj_a_{l}")[0], "w_proj_a")
    gw["w_proj_b"] = blocks(_grad_w(S["ob"], dyb, col_shards=True, name=f"grad_w_proj_b_{l}")[0], "w_proj_b")
    comm, done = plans.start(f"band_bwd_{l}")
    (dqa, dka, dva, dbias), extra = _attn_a_bwd(S["hq"], S["bias"], doa, 0, WA, l, comm)
    plans.finish(done, extra)
    comm, done = plans.start(f"stick_bwd_{l}")
    (dqb, dkb, dvb), extra = _sb_bwd(S["hq"], S["ob"], dob, 3 * WA, WB, l, comm)
    plans.finish(done, extra)
    dh = jnp.concatenate([dqa, dka, dva, dqb, dkb, dvb, dhg], axis=1)
    comm, done = plans.start(f"in_proj_bwd_{l}")
    dx, extra = _residual_nt(du1, alpha, dh, W["w_in"], f"in_proj_bwd_{l}", comm)
    plans.finish(done, extra)
    comm, done = plans.start(f"grad_w_in_{l}")
    g, extra = _grad_w(S["x"], dh, col_shards=True, name=f"grad_w_in_{l}", comm=comm)
    gw["w_in"] = blocks(g, "w_in")
    plans.finish(done, extra)
    gs = dict(b_gate=st1[0], rel_bias=_fold_bias_grad(dbias), ln1_g=st1[1, :D], ln1_b=st1[1, D:],
              ln2_g=st2[0], ln2_b=st2[1])
    return dx, gs, st2[2]


def _local_step(x, target, W, small, plans=None, gws=None):
    depth = len(W)
    alpha = float((2 * depth) ** 0.25)
    plans = plans or _Plans()
    gws = gws if gws is not None else [dict() for _ in range(depth)]
    saved = []
    h = x
    for l in range(depth):
        h, S = _layer_fwd(h, W[l], small, l, alpha, plans)
        saved.append(S)
    gss = [None] * depth
    d = target
    sq = None
    for l in reversed(range(depth)):
        d, gss[l], sq_l = _layer_bwd(d, saved[l], W[l], small, l, l == depth - 1, alpha, plans, gws[l])
        if l == depth - 1:
            sq = sq_l
    return sq, d, gws, gss


def _place():
    return lax.axis_index("x"), lax.axis_index("y"), lax.axis_index("c")


def _remote(src, dst, send_sem, recv_sem, to):
    return pltpu.make_async_remote_copy(src_ref=src, dst_ref=dst, send_sem=send_sem, recv_sem=recv_sem,
                                        device_id=to, device_id_type=_MESH)


def _half(ref, hc):
    kh = ref.shape[0] // 2
    return ref.at[pl.ds(pl.multiple_of(hc * kh, 16), kh), :]


def _gather_plan(blocks, fractions):
    nt = len(blocks)

    def run(step, nsteps, ins, outs, sems):
        send_sems, recv_sems, loc_sems = sems
        x, y, c = _place()
        k = 2 * x + y
        me, sibling = (x, y, c), (x, y, 1 - c)
        chips = [(1 - x, y), (x, 1 - y), (1 - x, 1 - y)]
        chip_k = [2 * cx + cy for cx, cy in chips]

        def ici(t, s, owner_k, to, src=None):
            dst = _half(outs[t].at[owner_k], c)
            return _remote(dst if src is None else src, dst, send_sems.at[t, s], recv_sems.at[t, s], to)

        def passed(t, s, hc, to):
            blk = _half(outs[t].at[chip_k[s]], hc)
            return _remote(blk, blk, send_sems.at[t, 3 + s], recv_sems.at[t, 3 + s], to)

        def local(t):
            return pltpu.make_async_copy(ins[t], outs[t].at[k], loc_sems.at[t])

        @pl.when(step == 0)
        def _():
            for t in range(nt):
                local(t).start()
                for s, chip in enumerate(chips):
                    ici(t, s, k, (*chip, c), src=_half(ins[t], c)).start()

        for t in range(nt):
            @pl.when(step == min(nsteps - 1, int(fractions[t] * nsteps)))
            def _():
                for s in range(3):
                    ici(t, s, chip_k[s], me).wait_recv()
                    passed(t, s, c, sibling).start()

        @pl.when(step == nsteps - 1)
        def _():
            for t in range(nt):
                for s, chip in enumerate(chips):
                    passed(t, s, 1 - c, me).wait_recv()
            for t in range(nt):
                for s, chip in enumerate(chips):
                    ici(t, s, k, (*chip, c), src=_half(ins[t], c)).wait_send()
                    passed(t, s, c, sibling).wait_send()
                local(t).wait()

    return _Comm(blocks, [jax.ShapeDtypeStruct((4,) + b.shape, b.dtype) for b in blocks],
                 [pltpu.SemaphoreType.DMA((nt, 6)), pltpu.SemaphoreType.DMA((nt, 6)), pltpu.SemaphoreType.DMA((nt,))], run)


def _scatter_plan(grads, owners):
    nt = len(grads)

    def run(step, nsteps, ins, outs, sems):
        send_sems, recv_sems, loc_sems = sems
        x, y, c = _place()
        me = 4 * x + 2 * y + c

        def target(r):
            tx = 1 - x if r & 2 else x
            ty = 1 - y if r & 1 else y
            return tx, ty

        def send(t, r):
            tx, ty = target(r)
            return _remote(ins[t].at[2 * tx + ty], outs[t].at[me], send_sems.at[t, r], recv_sems.at[t, 2 * r + c],
                           (tx, ty, owners[t]))

        def local(t):
            return pltpu.make_async_copy(ins[t].at[2 * x + y], outs[t].at[me], loc_sems.at[t])

        @pl.when(step == 0)
        def _():
            for t in range(nt):
                @pl.when(c == owners[t])
                def _():
                    local(t).start()

                @pl.when(c != owners[t])
                def _():
                    send(t, 0).start()

                for r in range(1, 4):
                    send(t, r).start()

        @pl.when(step == nsteps - 1)
        def _():
            for t in range(nt):
                @pl.when(c == owners[t])
                def _():
                    for r in range(4):
                        sx, sy = target(r)
                        for cs in range(2):
                            if r == 0 and cs == owners[t]:
                                continue
                            src_dev = 4 * sx + 2 * sy + cs
                            _remote(ins[t].at[0], outs[t].at[src_dev], send_sems.at[t, r], recv_sems.at[t, 2 * r + cs],
                                    (x, y, c)).wait_recv()
                    local(t).wait()

                @pl.when(c != owners[t])
                def _():
                    send(t, 0).wait_send()

                for r in range(1, 4):
                    send(t, r).wait_send()

    return _Comm(grads, [jax.ShapeDtypeStruct((8,) + g.shape[1:], g.dtype) for g in grads],
                 [pltpu.SemaphoreType.DMA((nt, 4)), pltpu.SemaphoreType.DMA((nt, 8)), pltpu.SemaphoreType.DMA((nt,))], run)


def _share_plan(reduced, owners):
    nt = len(reduced)

    def run(step, nsteps, ins, outs, sems):
        send_sems, recv_sems, loc_sems = sems
        x, y, c = _place()

        def give(t):
            return _remote(ins[t], outs[t], send_sems.at[t], recv_sems.at[t], (x, y, 1 - c))

        def local(t):
            return pltpu.make_async_copy(ins[t], outs[t], loc_sems.at[t])

        @pl.when(step == 0)
        def _():
            for t in range(nt):
                @pl.when(c == owners[t])
                def _():
                    local(t).start()
                    give(t).start()

        @pl.when(step == nsteps - 1)
        def _():
            for t in range(nt):
                @pl.when(c == owners[t])
                def _():
                    give(t).wait_send()
                    local(t).wait()

                @pl.when(c != owners[t])
                def _():
                    _remote(ins[t], outs[t], send_sems.at[t], recv_sems.at[t], (x, y, c)).wait_recv()

    return _Comm(reduced, [jax.ShapeDtypeStruct(r.shape, r.dtype) for r in reduced],
                 [pltpu.SemaphoreType.DMA((nt,)), pltpu.SemaphoreType.DMA((nt,)), pltpu.SemaphoreType.DMA((nt,))], run)


def _peer(x, y, c, r):
    px = 1 - x if r & 4 else x
    py = 1 - y if r & 2 else y
    pc = 1 - c if r & 1 else c
    return (px, py, pc), 4 * px + 2 * py + pc


def _sum_slots(st, name):
    _, K, n = st.shape
    tr = next(t for t in (256, 128, 64, 32, 16) if K % t == 0)

    def kern(s_ref, o_ref):
        acc = s_ref[0].astype(_F32)
        for d in range(1, 8):
            acc = acc + s_ref[d].astype(_F32)
        o_ref[...] = acc.astype(o_ref.dtype)

    return pl.pallas_call(
        kern, name=name, grid=(K // tr,),
        in_specs=[pl.BlockSpec((8, tr, n), lambda i: (0, i, 0))], out_specs=_rows(tr, n),
        out_shape=jax.ShapeDtypeStruct((K, n), _ACT),
        compiler_params=_cparams("parallel"),
    )(st)


def _all_reduce_small(p):
    R = p.shape[0]

    def body(p_ref, o_ref, stage, send_sems, recv_sems):
        x, y, c = _place()
        me = 4 * x + 2 * y + c
        stage[me] = p_ref[...]
        sent = []
        for r in range(1, 8):
            to, _ = _peer(x, y, c, r)
            cp = _remote(p_ref, stage.at[me], send_sems.at[r - 1], recv_sems.at[r - 1], to)
            cp.start()
            sent.append(cp)
        for r in range(1, 8):
            _, src_dev = _peer(x, y, c, r)
            _remote(p_ref, stage.at[src_dev], send_sems.at[r - 1], recv_sems.at[r - 1], (x, y, c)).wait_recv()
        acc = stage[0]
        for d in range(1, 8):
            acc = acc + stage[d]
        o_ref[...] = acc
        for cp in sent:
            cp.wait_send()

    vm = pl.BlockSpec(memory_space=pltpu.VMEM)
    return pl.pallas_call(
        body, name="all_reduce_small",
        in_specs=[vm], out_specs=vm,
        out_shape=jax.ShapeDtypeStruct((R, _LANES), _F32),
        scratch_shapes=[pltpu.VMEM((8, R, _LANES), _F32), pltpu.SemaphoreType.DMA((7,)), pltpu.SemaphoreType.DMA((7,))],
    )(p)


def _adamw(w, g, m, v, name):
    shape = w.shape
    w2, g2, m2, v2 = (a.reshape(-1, shape[-1]) for a in (w, g, m, v))
    R, C = w2.shape
    tr = next((t for t in (256, 128, 64, 32, 16) if R % t == 0), R)

    def kern(w_ref, g_ref, m_ref, v_ref, gf_ref, d_ref, nm_ref, nv_ref):
        gv = g_ref[...].astype(_F32)
        nm = _B1 * m_ref[...] + (1.0 - _B1) * gv
        nv = _B2 * v_ref[...] + (1.0 - _B2) * (gv * gv)
        m_hat = nm / (1.0 - _B1 ** _STEP)
        v_hat = nv / (1.0 - _B2 ** _STEP)
        gf_ref[...] = gv
        d_ref[...] = -_LR * (m_hat / (jnp.sqrt(v_hat) + _EPS) + _WD * w_ref[...])
        nm_ref[...] = nm
        nv_ref[...] = nv

    outs = pl.pallas_call(
        kern, name=name, grid=(R // tr,),
        in_specs=[_rows(tr, C)] * 4, out_specs=[_rows(tr, C)] * 4,
        out_shape=[jax.ShapeDtypeStruct((R, C), _F32)] * 4,
        compiler_params=_cparams("parallel"),
    )(w2, g2, m2, v2)
    return tuple(o.reshape(shape) for o in outs)


def _pack_small(gss, sq):
    parts = [gss[l][n].reshape(-1) for n in _SMALL for l in range(len(gss))] + [jnp.sum(sq).reshape(1)]
    flat = jnp.concatenate(parts)
    rows = -(-flat.shape[0] // (8 * _LANES)) * 8
    return jnp.pad(flat, (0, rows * _LANES - flat.shape[0])).reshape(rows, _LANES)


def _unpack_small(total, shapes):
    flat = total.reshape(-1)
    out, off = {}, 0
    for n in _SMALL:
        layers = []
        for _ in range(shapes[n][0]):
            size = 1
            for s in shapes[n][1:]:
                size *= s
            layers.append(flat[off:off + size].reshape(shapes[n][1:]))
            off += size
        out[n] = jnp.stack(layers)
    return out, flat[off]


_EARLY = ("w_in",)
_LATE = ("w_proj_a", "w_proj_b", "w_out", "w_ffn_in", "w_ffn_out")
_FFN = ("w_ffn_in", "w_ffn_out")
_NOT_FFN = ("w_in", "w_proj_a", "w_proj_b", "w_out")


def kernel(x, w_in, b_gate, rel_bias, w_proj_a, w_proj_b, w_out, ln1_g, ln1_b, w_ffn_in, w_ffn_out, ln2_g, ln2_b, loss_target, m_w_in, m_b_gate, m_rel_bias, m_w_proj_a, m_w_proj_b, m_w_out, m_ln1_g, m_ln1_b, m_w_ffn_in, m_w_ffn_out, m_ln2_g, m_ln2_b, v_w_in, v_b_gate, v_rel_bias, v_w_proj_a, v_w_proj_b, v_w_out, v_ln1_g, v_ln1_b, v_w_ffn_in, v_w_ffn_out, v_ln2_g, v_ln2_b):
    names = ("w_in", "b_gate", "rel_bias", "w_proj_a", "w_proj_b", "w_out", "ln1_g", "ln1_b", "w_ffn_in", "w_ffn_out", "ln2_g", "ln2_b")
    w = dict(zip(names, (w_in, b_gate, rel_bias, w_proj_a, w_proj_b, w_out, ln1_g, ln1_b, w_ffn_in, w_ffn_out, ln2_g, ln2_b)))
    m = dict(zip(names, (m_w_in, m_b_gate, m_rel_bias, m_w_proj_a, m_w_proj_b, m_w_out, m_ln1_g, m_ln1_b, m_w_ffn_in, m_w_ffn_out, m_ln2_g, m_ln2_b)))
    v = dict(zip(names, (v_w_in, v_b_gate, v_rel_bias, v_w_proj_a, v_w_proj_b, v_w_out, v_ln1_g, v_ln1_b, v_w_ffn_in, v_w_ffn_out, v_ln2_g, v_ln2_b)))
    T, D = x.shape[-2], x.shape[-1]
    assert w_in.shape[0] == 2, "the exchange schedule below is written for two layers"

    mine = [{n: w[n][l].astype(_MXU) for n in _DENSE} for l in range(2)]
    W = [dict(), dict()]
    gws = [dict(), dict()]
    slots, final = {}, {}

    def gathered(layer, tensors):
        return lambda outs: W[layer].update(zip(tensors, outs))

    def even(count):
        return [0.2 + 0.6 * (t + 1) / count for t in range(count)]

    def scatter(keys):
        def build():
            comm = _scatter_plan([gws[l][n] for l, n in keys], [l for l, _ in keys])
            return comm, lambda outs: slots.update(zip(keys, outs))
        return build

    def share(keys):
        def build():
            reduced = [_sum_slots(slots[key], f"sum_grad_{key[1]}_{key[0]}") for key in keys]
            comm = _share_plan(reduced, [l for l, _ in keys])
            return comm, lambda outs: final.update(zip(keys, outs))
        return build

    gathered(0, _EARLY)(_comm_only(_gather_plan([mine[0][n] for n in _EARLY], even(len(_EARLY))), "gather_first"))
    first = [(1, n) for n in _NOT_FFN]
    second = [(1, n) for n in _FFN] + [(0, n) for n in _FFN]
    last = [(0, n) for n in _NOT_FFN]
    plans = _Plans({
        "band_fwd_0": lambda: (_gather_plan([mine[0][n] for n in _LATE], even(len(_LATE))), gathered(0, _LATE)),
        "stick_fwd_0": lambda: (_gather_plan([mine[1][n] for n in _DENSE], even(len(_DENSE))), gathered(1, _DENSE)),
        "band_bwd_0": scatter(first),
        "stick_bwd_0": scatter(second),
        "in_proj_bwd_0": share(first),
        "grad_w_in_0": share(second),
    })
    small = {n: w[n] for n in _SMALL}
    sq, dx, _, gss = _local_step(x.reshape(T, D), loss_target.reshape(T, D), W, small, plans, gws)

    comm, done = scatter(last)()
    done(_comm_only(comm, "scatter_last"))
    comm, done = share(last)()
    done(_comm_only(comm, "share_last"))
    grads = {n: jnp.stack([final[(l, n)] for l in range(2)]) for n in _DENSE}

    total = _all_reduce_small(_pack_small(gss, sq))
    small_grads, sq_all = _unpack_small(total, {n: w[n].shape for n in _SMALL})
    grads.update(small_grads)
    loss = 0.5 * sq_all / D

    grad, delta, new_m, new_v = {}, {}, {}, {}
    for n in names:
        grad[n], delta[n], new_m[n], new_v[n] = _adamw(w[n], grads[n].reshape(w[n].shape), m[n], v[n], f"adamw_{n}")
    return (loss, dx.reshape(x.shape), *[grad[n] for n in names], *[delta[n] for n in names],
            *[new_m[n] for n in names], *[new_v[n] for n in names])
```

```python
import functools

import jax
import jax.numpy as jnp
from jax import lax
from jax.experimental import pallas as pl
from jax.experimental.pallas import tpu as pltpu

_MXU = jnp.bfloat16
_ACT = jnp.bfloat16
_F32 = jnp.float32

_HEAD = 64
_CHUNK = 64
_LANES = 128
_TQ = 128
_BAND_TILES = 5
_BIAS_TILES = 9
_REL_CLIP = 256
_LN_EPS = 1e-5
_MASKED = -1e30
_EXP_ZERO_BELOW = -104.0
_SB_WINDOW = 3
_VMEM_LIMIT = 56 * 1024 * 1024
_GRAD_ACC_BYTES = 12 * 1024 * 1024

_LR, _B1, _B2, _EPS, _WD, _STEP = 0.001, 0.9, 0.999, 1e-08, 0.01, 10

_MESH = pl.DeviceIdType.MESH


def _dot(a, b):
    return jnp.dot(a, b, preferred_element_type=_F32)


def _dot_nt(a, b):
    return lax.dot_general(a, b, (((1,), (1,)), ((), ())), preferred_element_type=_F32)


def _dot_tn(a, b):
    return lax.dot_general(a, b, (((0,), (0,)), ((), ())), preferred_element_type=_F32)


def _cparams(*sem):
    return pltpu.CompilerParams(dimension_semantics=sem, vmem_limit_bytes=_VMEM_LIMIT)


def _rows(t, c):
    return pl.BlockSpec((t, c), lambda i: (i, 0))


def _whole(shape):
    return pl.BlockSpec(shape, lambda i: tuple(0 for _ in shape))


_ANY = pl.BlockSpec(memory_space=pl.ANY)


def _load_cols(w_hbm, w_vmem, sem):
    n = w_hbm.shape[-1]
    cps = [pltpu.make_async_copy(w_hbm.at[k], w_vmem.at[:, pl.ds(k * n, n)], sem.at[k]) for k in range(4)]
    for cp in cps:
        cp.start()
    for cp in cps:
        cp.wait()


def _load_rows(w_hbm, w_vmem, sem):
    r = w_hbm.shape[-2]
    cps = [pltpu.make_async_copy(w_hbm.at[k], w_vmem.at[pl.ds(k * r, r), :], sem.at[k]) for k in range(4)]
    for cp in cps:
        cp.start()
    for cp in cps:
        cp.wait()


def _ln_stats(u):
    mu = jnp.mean(u, axis=-1, keepdims=True)
    xc = u - mu
    var = jnp.mean(xc * xc, axis=-1, keepdims=True)
    rstd = lax.rsqrt(var + _LN_EPS)
    return xc * rstd, rstd


def _ln_bwd(u, dy, gamma):
    xhat, rstd = _ln_stats(u)
    dxh = dy * gamma
    m1 = jnp.mean(dxh, axis=-1, keepdims=True)
    m2 = jnp.mean(dxh * xhat, axis=-1, keepdims=True)
    du = rstd * (dxh - m1 - xhat * m2)
    return du, jnp.sum(dy * xhat, axis=0, keepdims=True), jnp.sum(dy, axis=0, keepdims=True), xhat


def _divisor_tile(n, cap):
    best = None
    for t in range(_LANES, min(n, cap) + 1, _LANES):
        if n % t == 0:
            best = t
    return best or n


class _Comm:
    def __init__(self, inputs, out_shapes, sems, run, aliases=None):
        self.inputs, self.out_shapes, self.sems, self.run = list(inputs), list(out_shapes), list(sems), run
        self.aliases = aliases or {}


def _call(kern, comm, *, name, grid, in_specs, out_specs, out_shape, scratch_shapes, args, semantics):
    in_specs, out_specs, out_shape, scratch_shapes = list(in_specs), list(out_specs), list(out_shape), list(scratch_shapes)
    if comm is None:
        outs = pl.pallas_call(kern, name=name, grid=grid, in_specs=in_specs, out_specs=out_specs, out_shape=out_shape,
                              scratch_shapes=scratch_shapes, compiler_params=_cparams(*semantics))(*args)
        return list(outs), []
    n_in, n_out, n_scr = len(in_specs), len(out_specs), len(scratch_shapes)
    ci, co = len(comm.inputs), len(comm.out_shapes)
    nsteps = functools.reduce(lambda a, b: a * b, grid, 1)

    def fused(*refs):
        a, b = n_in, n_in + ci
        c, d = b + n_out, b + n_out + co
        e = d + n_scr
        step = pl.program_id(0)
        for ax in range(1, len(grid)):
            step = step * grid[ax] + pl.program_id(ax)
        comm.run(step, nsteps, refs[a:b], refs[c:d], refs[e:])
        kern(*refs[:a], *refs[b:c], *refs[d:e])

    outs = pl.pallas_call(
        fused, name=name, grid=grid, in_specs=in_specs + [_ANY] * ci, out_specs=out_specs + [_ANY] * co,
        out_shape=out_shape + comm.out_shapes, scratch_shapes=scratch_shapes + comm.sems,
        input_output_aliases={n_in + i: n_out + o for i, o in comm.aliases.items()},
        compiler_params=_cparams(*("arbitrary" for _ in grid)))(*args, *comm.inputs)
    return list(outs[:n_out]), list(outs[n_out:])


def _comm_only(comm, name):
    def body(*refs):
        ci, co = len(comm.inputs), len(comm.out_shapes)
        comm.run(0, 1, refs[:ci], refs[ci:ci + co], refs[ci + co:])

    outs = pl.pallas_call(body, name=name, in_specs=[_ANY] * len(comm.inputs), out_specs=[_ANY] * len(comm.out_shapes),
                          out_shape=comm.out_shapes, scratch_shapes=comm.sems,
                          input_output_aliases=dict(comm.aliases))(*comm.inputs)
    return list(outs)


def _in_proj(x, w_in, layer):
    T, D = x.shape
    N = 4 * w_in.shape[-1]
    NQ = N - 2 * D
    tm = 256

    def kern(x_ref, w_hbm, hq_ref, hg_ref, w_v, sem):
        @pl.when(pl.program_id(0) == 0)
        def _():
            _load_cols(w_hbm, w_v, sem)

        xb = x_ref[...].astype(_MXU)
        hq_ref[...] = _dot(xb, w_v[:, :NQ]).astype(hq_ref.dtype)
        hg_ref[...] = _dot(xb, w_v[:, NQ:])

    return pl.pallas_call(
        kern, name=f"in_proj_{layer}", grid=(T // tm,),
        in_specs=[_rows(tm, D), _ANY],
        out_specs=[_rows(tm, NQ), _rows(tm, 2 * D)],
        out_shape=[jax.ShapeDtypeStruct((T, NQ), _ACT), jax.ShapeDtypeStruct((T, 2 * D), _F32)],
        scratch_shapes=[pltpu.VMEM((D, N), w_in.dtype), pltpu.SemaphoreType.DMA((4,))],
        compiler_params=_cparams("arbitrary"),
    )(x, w_in)


def _mix_fwd(oa, ob, hg, x, wpa, wpb, wo, bg, gamma, beta, alpha, layer):
    T, D = x.shape
    WA, WB = oa.shape[1], ob.shape[1]
    tm = 256

    def kern(oa_ref, ob_ref, hg_ref, x_ref, bg_ref, g_ref, b_ref, wpa_h, wpb_h, wo_h,
             x1_ref, u1_ref, pre_ref, ya_ref, yb_ref, wpa_v, wpb_v, wo_v, sa, sb, so):
        @pl.when(pl.program_id(0) == 0)
        def _():
            _load_cols(wpa_h, wpa_v, sa)
            _load_cols(wpb_h, wpb_v, sb)
            _load_rows(wo_h, wo_v, so)

        ya = _dot(oa_ref[...].astype(_MXU), wpa_v[...])
        yb = _dot(ob_ref[...].astype(_MXU), wpb_v[...])
        hgv = hg_ref[...]
        bgv = bg_ref[...]
        ga = jax.nn.sigmoid(hgv[:, :D] + bgv[:, :D])
        gb = jax.nn.sigmoid(hgv[:, D:] + bgv[:, D:])
        pre = ga * ya + gb * yb
        mix = _dot(pre.astype(_MXU), wo_v[...])
        u = alpha * x_ref[...] + mix
        xhat, _ = _ln_stats(u)
        x1_ref[...] = xhat * g_ref[...] + b_ref[...]
        u1_ref[...] = u
        pre_ref[...] = pre.astype(pre_ref.dtype)
        ya_ref[...] = ya.astype(ya_ref.dtype)
        yb_ref[...] = yb.astype(yb_ref.dtype)

    return pl.pallas_call(
        kern, name=f"mix_fwd_{layer}", grid=(T // tm,),
        in_specs=[_rows(tm, WA), _rows(tm, WB), _rows(tm, 2 * D), _rows(tm, D),
                  _whole((1, 2 * D)), _whole((1, D)), _whole((1, D)), _ANY, _ANY, _ANY],
        out_specs=[_rows(tm, D)] * 5,
        out_shape=[jax.ShapeDtypeStruct((T, D), _F32), jax.ShapeDtypeStruct((T, D), _F32)]
        + [jax.ShapeDtypeStruct((T, D), _ACT)] * 3,
        scratch_shapes=[pltpu.VMEM((WA, D), wpa.dtype), pltpu.VMEM((WB, D), wpb.dtype), pltpu.VMEM((D, D), wo.dtype),
                        pltpu.SemaphoreType.DMA((4,)), pltpu.SemaphoreType.DMA((4,)), pltpu.SemaphoreType.DMA((4,))],
        compiler_params=_cparams("arbitrary"),
    )(oa, ob, hg, x, bg, gamma, beta, wpa, wpb, wo)


def _ffn_fwd(x1, wfi, wfo, gamma, beta, alpha, layer, comm=None):
    T, D = x1.shape
    F2 = 4 * wfi.shape[-1]
    F = F2 // 2
    tm = 256
    fc = F // 2

    def kern(x_ref, g_ref, b_ref, wi_h, wo_h, x2_ref, u2_ref, act_ref, gu_ref, wi_v, wo_v, si, so):
        @pl.when(pl.program_id(0) == 0)
        def _():
            _load_cols(wi_h, wi_v, si)
            _load_rows(wo_h, wo_v, so)

        x = x_ref[...]
        xb = x.astype(_MXU)
        ffn = jnp.zeros((tm, D), _F32)
        for c in range(2):
            g = _dot(xb, wi_v[:, c * fc:(c + 1) * fc])
            u = _dot(xb, wi_v[:, F + c * fc:F + (c + 1) * fc])
            act = g * jax.nn.sigmoid(g) * u
            ab = act.astype(_MXU)
            ffn = ffn + _dot(ab, wo_v[c * fc:(c + 1) * fc, :])
            act_ref[:, c * fc:(c + 1) * fc] = ab.astype(act_ref.dtype)
            gu_ref[:, c * fc:(c + 1) * fc] = g.astype(gu_ref.dtype)
            gu_ref[:, F + c * fc:F + (c + 1) * fc] = u.astype(gu_ref.dtype)
        uu = alpha * x + ffn
        xhat, _ = _ln_stats(uu)
        x2_ref[...] = xhat * g_ref[...] + b_ref[...]
        u2_ref[...] = uu

    return _call(
        kern, comm, name=f"ffn_fwd_{layer}", grid=(T // tm,),
        in_specs=[_rows(tm, D), _whole((1, D)), _whole((1, D)), _ANY, _ANY],
        out_specs=[_rows(tm, D), _rows(tm, D), _rows(tm, F), _rows(tm, F2)],
        out_shape=[jax.ShapeDtypeStruct((T, D), _F32), jax.ShapeDtypeStruct((T, D), _F32),
                   jax.ShapeDtypeStruct((T, F), _ACT), jax.ShapeDtypeStruct((T, F2), _ACT)],
        scratch_shapes=[pltpu.VMEM((D, F2), wfi.dtype), pltpu.VMEM((F, D), wfo.dtype),
                        pltpu.SemaphoreType.DMA((4,)), pltpu.SemaphoreType.DMA((4,))],
        args=(x1, gamma, beta, wfi, wfo), semantics=("arbitrary",))


def _ffn_bwd_a(u2, dy_or_target, gu, gamma, beta, wfo, layer, last):
    T, D = u2.shape
    F2 = gu.shape[1]
    F = F2 // 2
    tm = 256
    fc = F // 2

    def kern(u_ref, dy_ref, gu_ref, g_ref, b_ref, wo_h, du_ref, dub_ref, dgu_ref, st_ref, wo_v, so):
        @pl.when(pl.program_id(0) == 0)
        def _():
            _load_rows(wo_h, wo_v, so)
            st_ref[...] = jnp.zeros_like(st_ref)

        gam = g_ref[...]
        u = u_ref[...]
        if last:
            xhat0, _ = _ln_stats(u)
            err = xhat0 * gam + b_ref[...] - dy_ref[...]
            dy = err * (1.0 / D)
            st_ref[2:3, :] += jnp.sum(err * err, axis=0, keepdims=True)
        else:
            dy = dy_ref[...]
        du, dgam, dbet, _ = _ln_bwd(u, dy, gam)
        st_ref[0:1, :] += dgam
        st_ref[1:2, :] += dbet
        du_ref[...] = du
        dub = du.astype(_MXU)
        dub_ref[...] = dub.astype(dub_ref.dtype)
        for c in range(2):
            dact = _dot_nt(dub, wo_v[c * fc:(c + 1) * fc, :])
            g = gu_ref[:, c * fc:(c + 1) * fc].astype(_F32)
            uu = gu_ref[:, F + c * fc:F + (c + 1) * fc].astype(_F32)
            sg = jax.nn.sigmoid(g)
            dgu_ref[:, c * fc:(c + 1) * fc] = (dact * uu * (sg * (1.0 + g * (1.0 - sg)))).astype(dgu_ref.dtype)
            dgu_ref[:, F + c * fc:F + (c + 1) * fc] = (dact * (g * sg)).astype(dgu_ref.dtype)

    return pl.pallas_call(
        kern, name=f"ffn_bwd_a_{layer}", grid=(T // tm,),
        in_specs=[_rows(tm, D), _rows(tm, D), _rows(tm, F2), _whole((1, D)), _whole((1, D)), _ANY],
        out_specs=[_rows(tm, D), _rows(tm, D), _rows(tm, F2), _whole((8, D))],
        out_shape=[jax.ShapeDtypeStruct((T, D), _F32), jax.ShapeDtypeStruct((T, D), _ACT),
                   jax.ShapeDtypeStruct((T, F2), _ACT), jax.ShapeDtypeStruct((8, D), _F32)],
        scratch_shapes=[pltpu.VMEM((F, D), wfo.dtype), pltpu.SemaphoreType.DMA((4,))],
        compiler_params=_cparams("arbitrary"),
    )(u2, dy_or_target, gu, gamma, beta, wfo)


def _residual_nt(res, res_scale, d, w, name, comm=None):
    T, K = res.shape
    N = d.shape[1]
    tm = 256

    def kern(r_ref, d_ref, w_hbm, o_ref, w_v, sem):
        @pl.when(pl.program_id(0) == 0)
        def _():
            _load_cols(w_hbm, w_v, sem)

        o_ref[...] = res_scale * r_ref[...] + _dot_nt(d_ref[...].astype(_MXU), w_v[...])

    outs, extra = _call(
        kern, comm, name=name, grid=(T // tm,),
        in_specs=[_rows(tm, K), _rows(tm, N), _ANY], out_specs=[_rows(tm, K)],
        out_shape=[jax.ShapeDtypeStruct((T, K), _F32)],
        scratch_shapes=[pltpu.VMEM((K, N), w.dtype), pltpu.SemaphoreType.DMA((4,))],
        args=(res, d, w), semantics=("arbitrary",))
    return outs[0], extra


def _mix_bwd(u1, dx1, ya, yb, hg, wpa, wpb, wo, bg, gamma, alpha, layer):
    del alpha
    T, D = u1.shape
    WA, WB = wpa.shape[-2], wpb.shape[-2]
    tm = 256

    def kern(u_ref, dx_ref, ya_ref, yb_ref, hg_ref, bg_ref, g_ref, wpa_h, wpb_h, wo_h,
             du_ref, dub_ref, dya_ref, dyb_ref, dhg_ref, doa_ref, dob_ref, st_ref,
             wpa_v, wpb_v, wo_v, sa, sb, so):
        @pl.when(pl.program_id(0) == 0)
        def _():
            _load_cols(wpa_h, wpa_v, sa)
            _load_cols(wpb_h, wpb_v, sb)
            _load_rows(wo_h, wo_v, so)
            st_ref[...] = jnp.zeros_like(st_ref)

        du, dgam, dbet, _ = _ln_bwd(u_ref[...], dx_ref[...], g_ref[...])
        st_ref[1:2, :D] += dgam
        st_ref[1:2, D:] += dbet
        du_ref[...] = du
        dub = du.astype(_MXU)
        dub_ref[...] = dub.astype(dub_ref.dtype)
        dpre = _dot_nt(dub, wo_v[...])
        hgv = hg_ref[...]
        bgv = bg_ref[...]
        ga = jax.nn.sigmoid(hgv[:, :D] + bgv[:, :D])
        gb = jax.nn.sigmoid(hgv[:, D:] + bgv[:, D:])
        dya = (dpre * ga).astype(_MXU)
        dyb = (dpre * gb).astype(_MXU)
        dsa = dpre * ya_ref[...].astype(_F32) * (ga * (1.0 - ga))
        dsb = dpre * yb_ref[...].astype(_F32) * (gb * (1.0 - gb))
        st_ref[0:1, :D] += jnp.sum(dsa, axis=0, keepdims=True)
        st_ref[0:1, D:] += jnp.sum(dsb, axis=0, keepdims=True)
        dya_ref[...] = dya.astype(dya_ref.dtype)
        dyb_ref[...] = dyb.astype(dyb_ref.dtype)
        dhg_ref[:, :D] = dsa.astype(dhg_ref.dtype)
        dhg_ref[:, D:] = dsb.astype(dhg_ref.dtype)
        doa_ref[...] = _dot_nt(dya, wpa_v[...]).astype(doa_ref.dtype)
        dob_ref[...] = _dot_nt(dyb, wpb_v[...]).astype(dob_ref.dtype)

    return pl.pallas_call(
        kern, name=f"mix_bwd_{layer}", grid=(T // tm,),
        in_specs=[_rows(tm, D)] * 4 + [_rows(tm, 2 * D), _whole((1, 2 * D)), _whole((1, D)), _ANY, _ANY, _ANY],
        out_specs=[_rows(tm, D)] * 4 + [_rows(tm, 2 * D), _rows(tm, WA), _rows(tm, WB), _whole((8, 2 * D))],
        out_shape=[jax.ShapeDtypeStruct((T, D), _F32)] + [jax.ShapeDtypeStruct((T, D), _ACT)] * 3
        + [jax.ShapeDtypeStruct((T, 2 * D), _ACT), jax.ShapeDtypeStruct((T, WA), _ACT),
           jax.ShapeDtypeStruct((T, WB), _ACT), jax.ShapeDtypeStruct((8, 2 * D), _F32)],
        scratch_shapes=[pltpu.VMEM((WA, D), wpa.dtype), pltpu.VMEM((WB, D), wpb.dtype), pltpu.VMEM((D, D), wo.dtype),
                        pltpu.SemaphoreType.DMA((4,)), pltpu.SemaphoreType.DMA((4,)), pltpu.SemaphoreType.DMA((4,))],
        compiler_params=_cparams("arbitrary"),
    )(u1, dx1, ya, yb, hg, bg, gamma, wpa, wpb, wo)


def _grad_w(a, b, *, col_shards, name, comm=None):
    T, M = a.shape
    N = b.shape[1]
    tk = 512
    n = N // 4 if col_shards else N
    whole = M * N * 4 <= _GRAD_ACC_BYTES
    tn = N if whole else (n if col_shards else _divisor_tile(N, _GRAD_ACC_BYTES // (4 * M)))
    nk = T // tk

    def kern(a_ref, b_ref, o_ref, acc):
        k = pl.program_id(1)

        @pl.when(k == 0)
        def _():
            acc[...] = jnp.zeros_like(acc)

        acc[...] += _dot_tn(a_ref[...].astype(_MXU), b_ref[...].astype(_MXU))

        @pl.when(k == nk - 1)
        def _():
            if col_shards and whole:
                for s in range(4):
                    o_ref[s] = acc[:, s * n:(s + 1) * n].astype(o_ref.dtype)
            else:
                o_ref[...] = acc[...].astype(o_ref.dtype)

    if col_shards:
        out_spec = (pl.BlockSpec((4, M, n), lambda j, k: (0, 0, 0)) if whole
                    else pl.BlockSpec((None, M, n), lambda j, k: (j, 0, 0)))
        out_shape = jax.ShapeDtypeStruct((4, M, n), _ACT)
    else:
        out_spec = pl.BlockSpec((M, tn), lambda j, k: (0, j))
        out_shape = jax.ShapeDtypeStruct((M, N), _ACT)
    outs, extra = _call(
        kern, comm, name=name, grid=(N // tn, nk),
        in_specs=[pl.BlockSpec((tk, M), lambda j, k: (k, 0)), pl.BlockSpec((tk, tn), lambda j, k: (k, j))],
        out_specs=[out_spec], out_shape=[out_shape], scratch_shapes=[pltpu.VMEM((M, tn), _F32)],
        args=(a, b), semantics=("parallel", "arbitrary"))
    return outs[0], extra


def _bias_tiles(rel):
    H = rel.shape[0]
    span = _TQ * _BAND_TILES - 1
    edge = span - _REL_CLIP
    gvec = jnp.concatenate([jnp.broadcast_to(rel[:, :1], (H, edge)), rel, jnp.broadcast_to(rel[:, -1:], (H, edge))], axis=1)
    width = _BIAS_TILES * _TQ
    period = width + _TQ
    tiled = jnp.broadcast_to(jnp.pad(gvec[:, ::-1], ((0, 0), (0, 1)))[:, None, :], (H, _TQ, period))
    rows = tiled.reshape(H, _TQ * period)[:, :_TQ * (period - 1)].reshape(H, _TQ, period - 1)[:, :, _TQ - 1:]
    r = jnp.arange(_TQ)[:, None]
    u = jnp.arange(width)[None, :]
    d = 4 * _TQ + r - u
    rm = r % _CHUNK
    valid = (d >= rm - (_CHUNK - 1)) & (d <= rm + 8 * _CHUNK)
    tiles = jnp.where(valid[None], rows, _MASKED)
    return tiles.reshape(H // 2, 2 * _TQ, _BIAS_TILES, _TQ).transpose(0, 2, 1, 3)


def _fold_bias_grad(db):
    H = 2 * db.shape[0]
    width = _BIAS_TILES * _TQ
    period = width + _TQ
    x = jnp.pad(db.transpose(0, 2, 1, 3).reshape(H, _TQ, width), ((0, 0), (0, 0), (_TQ - 1, 0)))
    skew = jnp.pad(x.reshape(H, _TQ * (period - 1)), ((0, 0), (0, _TQ))).reshape(H, _TQ, period)
    dg = skew.sum(axis=1)[:, :period - 1][:, ::-1]
    span = _TQ * _BAND_TILES - 1
    edge = span - _REL_CLIP
    mid = dg[:, edge:edge + 2 * _REL_CLIP + 1]
    lo = dg[:, :edge].sum(axis=1)
    hi = dg[:, edge + 2 * _REL_CLIP + 1:].sum(axis=1)
    return mid.at[:, 0].add(lo).at[:, -1].add(hi)


def _band_window(i):
    j0 = jnp.maximum(i - (_BAND_TILES - 1), 0)
    return j0, (_BAND_TILES - 1) - (i - j0)


def _head_masks():
    lane = lax.broadcasted_iota(jnp.int32, (1, _LANES), 1)
    return [(lane // _HEAD) == hh for hh in range(2)]


def _stack_heads(x, masks):
    return jnp.concatenate([jnp.where(m, x, jnp.zeros_like(x)) for m in masks], axis=0)


def _unstack_heads(y, masks):
    return jnp.where(masks[0], y[:_TQ], y[_TQ:])


def _scaled(q):
    return q * jnp.asarray(_HEAD ** -0.5, q.dtype)


def _band_probs(q2, k_ref, b_ref, j0, boff):
    s = []
    for j in range(_BAND_TILES):
        kj = k_ref[pl.ds(pl.multiple_of((j0 + j) * _TQ, _TQ), _TQ), :]
        s.append(_dot_nt(q2, kj) + b_ref[boff + j])
    m = jnp.max(functools.reduce(jnp.maximum, s), axis=-1, keepdims=True)
    p = [jnp.exp(x - m) for x in s]
    l = jnp.sum(functools.reduce(lambda a, b: a + b, p), axis=-1, keepdims=True)
    return p, 1.0 / l


def _qkv_specs(T, cb, npair):
    return [pl.BlockSpec((_TQ, _LANES), lambda h, i: (i, cb + h)),
            pl.BlockSpec((T, _LANES), lambda h, i: (0, cb + npair + h)),
            pl.BlockSpec((T, _LANES), lambda h, i: (0, cb + 2 * npair + h))]


def _attn_a_fwd(hq, bias, col0, width, layer, comm=None):
    T = hq.shape[0]
    npair = width // _LANES
    nq = T // _TQ

    def kern(q_ref, k_ref, v_ref, b_ref, o_ref):
        i = pl.program_id(1)
        j0, boff = _band_window(i)
        masks = _head_masks()
        q2 = _stack_heads(_scaled(q_ref[...]), masks)
        p, inv = _band_probs(q2, k_ref, b_ref, j0, boff)
        o = jnp.zeros((2 * _TQ, _LANES), _F32)
        for j in range(_BAND_TILES):
            vj = v_ref[pl.ds(pl.multiple_of((j0 + j) * _TQ, _TQ), _TQ), :]
            o = o + _dot(p[j].astype(_MXU), vj)
        o_ref[...] = _unstack_heads(o * inv, masks).astype(o_ref.dtype)

    outs, extra = _call(
        kern, comm, name=f"band_attn_fwd_{layer}", grid=(npair, nq),
        in_specs=_qkv_specs(T, col0 // _LANES, npair)
        + [pl.BlockSpec((None, _BIAS_TILES, 2 * _TQ, _TQ), lambda h, i: (h, 0, 0, 0))],
        out_specs=[pl.BlockSpec((_TQ, _LANES), lambda h, i: (i, h))],
        out_shape=[jax.ShapeDtypeStruct((T, width), _ACT)], scratch_shapes=[],
        args=(hq, hq, hq, bias), semantics=("arbitrary", "arbitrary"))
    return outs[0], extra


def _attn_a_bwd(hq, bias, do, col0, width, layer, comm=None):
    T = hq.shape[0]
    npair = width // _LANES
    nq = T // _TQ
    scale = _HEAD ** -0.5

    def kern(q_ref, k_ref, v_ref, b_ref, do_ref, dq_ref, dk_ref, dv_ref, db_ref, dk_acc, dv_acc):
        i = pl.program_id(1)

        @pl.when(i == 0)
        def _():
            dk_acc[...] = jnp.zeros_like(dk_acc)
            dv_acc[...] = jnp.zeros_like(dv_acc)
            db_ref[...] = jnp.zeros_like(db_ref)

        j0, boff = _band_window(i)
        masks = _head_masks()
        q2 = _stack_heads(_scaled(q_ref[...]), masks)
        do2 = _stack_heads(do_ref[...], masks).astype(_MXU)
        p, inv = _band_probs(q2, k_ref, b_ref, j0, boff)
        rows = [pl.ds(pl.multiple_of((j0 + j) * _TQ, _TQ), _TQ) for j in range(_BAND_TILES)]
        p = [x * inv for x in p]
        dp = [_dot_nt(do2, v_ref[rows[j], :]) for j in range(_BAND_TILES)]
        delta = jnp.sum(functools.reduce(lambda a, b: a + b, [p[j] * dp[j] for j in range(_BAND_TILES)]),
                        axis=-1, keepdims=True)
        dq = jnp.zeros((2 * _TQ, _LANES), _F32)
        for j in range(_BAND_TILES):
            ds = p[j] * (dp[j] - delta)
            db_ref[boff + j] += ds
            dsb = ds.astype(_MXU)
            dq = dq + _dot(dsb, k_ref[rows[j], :])
            dk_acc[rows[j], :] += _dot_tn(dsb, q2)
            dv_acc[rows[j], :] += _dot_tn(p[j].astype(_MXU), do2)
        dq_ref[...] = (_unstack_heads(dq, masks) * scale).astype(dq_ref.dtype)

        @pl.when(i == nq - 1)
        def _():
            dk_ref[...] = dk_acc[...].astype(dk_ref.dtype)
            dv_ref[...] = dv_acc[...].astype(dv_ref.dtype)

    strip = pl.BlockSpec((None, _BIAS_TILES, 2 * _TQ, _TQ), lambda h, i: (h, 0, 0, 0))
    tile = pl.BlockSpec((_TQ, _LANES), lambda h, i: (i, h))
    column = pl.BlockSpec((T, _LANES), lambda h, i: (0, h))
    outs, extra = _call(
        kern, comm, name=f"band_attn_bwd_{layer}", grid=(npair, nq),
        in_specs=_qkv_specs(T, col0 // _LANES, npair) + [strip, tile],
        out_specs=[tile, column, column, strip],
        out_shape=[jax.ShapeDtypeStruct((T, width), _ACT)] * 3
        + [jax.ShapeDtypeStruct((npair, _BIAS_TILES, 2 * _TQ, _TQ), _F32)],
        scratch_shapes=[pltpu.VMEM((T, _LANES), _F32), pltpu.VMEM((T, _LANES), _F32)],
        args=(hq, hq, hq, bias, do), semantics=("arbitrary", "arbitrary"))
    return outs, extra


def _suffix_matrix():
    r = lax.broadcasted_iota(jnp.int32, (_TQ, _TQ), 0)
    c = lax.broadcasted_iota(jnp.int32, (_TQ, _TQ), 1)
    r2 = lax.broadcasted_iota(jnp.int32, (2 * _TQ, _TQ), 0)
    c2 = lax.broadcasted_iota(jnp.int32, (2 * _TQ, _TQ), 1)
    return (r > c).astype(_MXU), c2 - (r2 & (_TQ - 1))


def _suffix_sum(x, tri):
    n = x.shape[0]
    hi = x.astype(_MXU)
    lo = (x - hi.astype(_F32)).astype(_MXU)
    y = _dot(jnp.concatenate([hi, lo], axis=0), tri)
    return y[:n] + y[n:]


def _stick_tile(qs, kj, jj, rel, carry_l, tri):
    z = _dot_nt(qs, kj)
    nsp = -(jnp.maximum(z, 0.0) + jnp.log(1.0 + jnp.exp(-jnp.abs(z))))
    if isinstance(jj, int):
        mask = (rel < 0) if jj == 0 else None
    else:
        mask = rel < jnp.where(jj == 0, 0, _TQ)
    L = nsp if mask is None else jnp.where(mask, nsp, 0.0)
    w = jnp.exp(z + L + _suffix_sum(L, tri) + carry_l)
    if mask is not None:
        w = jnp.where(mask, w, 0.0)
    return z, L, w, mask


def _sweep_done(i, jj, cl):
    return jnp.logical_or(jj > i, jnp.max(cl) < _EXP_ZERO_BELOW)


def _sweep(i, tile, zero):
    def window():
        c = zero
        for jj in range(_SB_WINDOW):
            c = tile(jj, c)
        return (jnp.int32(_SB_WINDOW),) + c

    start = lax.cond(i >= _SB_WINDOW - 1, window, lambda: (jnp.int32(0),) + zero)
    return lax.while_loop(lambda c: jnp.logical_not(_sweep_done(i, c[0], c[1])),
                          lambda c: (c[0] + 1,) + tile(c[0], c[1:]), start)


def _sb_fwd(hq, col0, width, layer, comm=None):
    T = hq.shape[0]
    npair = width // _LANES
    nq = T // _TQ

    def kern(q_ref, k_ref, v_ref, o_ref):
        i = pl.program_id(1)
        masks = _head_masks()
        tri, rel = _suffix_matrix()
        q2 = _stack_heads(_scaled(q_ref[...]), masks)

        def tile(jj, c):
            cl, a = c
            rows = pl.ds(pl.multiple_of((i - jj) * _TQ, _TQ), _TQ)
            _, L, w, _ = _stick_tile(q2, k_ref[rows, :], jj, rel, cl, tri)
            return cl + jnp.sum(L, axis=-1, keepdims=True), a + _dot(w.astype(_MXU), v_ref[rows, :])

        out = _sweep(i, tile, (jnp.zeros((2 * _TQ, 1), _F32), jnp.zeros((2 * _TQ, _LANES), _F32)))
        o_ref[...] = _unstack_heads(out[2], masks)

    outs, extra = _call(
        kern, comm, name=f"stick_attn_fwd_{layer}", grid=(npair, nq),
        in_specs=_qkv_specs(T, col0 // _LANES, npair),
        out_specs=[pl.BlockSpec((_TQ, _LANES), lambda h, i: (i, h))],
        out_shape=[jax.ShapeDtypeStruct((T, width), _F32)], scratch_shapes=[],
        args=(hq, hq, hq), semantics=("arbitrary", "arbitrary"))
    return outs[0], extra


def _sb_bwd(hq, o, do, col0, width, layer, comm=None):
    T = hq.shape[0]
    npair = width // _LANES
    nq = T // _TQ
    scale = _HEAD ** -0.5

    def kern(q_ref, k_ref, v_ref, o_ref, do_ref, dq_ref, dk_ref, dv_ref, dk_acc, dv_acc):
        i = pl.program_id(1)

        @pl.when(i == 0)
        def _():
            dk_acc[...] = jnp.zeros_like(dk_acc)
            dv_acc[...] = jnp.zeros_like(dv_acc)

        masks = _head_masks()
        tri, rel = _suffix_matrix()
        q2 = _stack_heads(_scaled(q_ref[...]), masks)
        do_t = do_ref[...]
        do2 = _stack_heads(do_t, masks).astype(_MXU)
        dsum = jnp.sum(_stack_heads(do_t.astype(_F32) * o_ref[...], masks), axis=-1, keepdims=True)

        def tile(jj, c):
            cl, cg, dq = c
            rows = pl.ds(pl.multiple_of((i - jj) * _TQ, _TQ), _TQ)
            kj = k_ref[rows, :]
            vj = v_ref[rows, :]
            z, L, w, mask = _stick_tile(q2, kj, jj, rel, cl, tri)
            wb = w.astype(_MXU)
            g = wb.astype(_F32) * _dot_nt(do2, vj)
            gs = _suffix_sum(g, tri) + cg
            dz = g - jnp.exp(z + L) * (dsum - gs)
            if mask is not None:
                dz = jnp.where(mask, dz, 0.0)
            dzb = dz.astype(_MXU)
            dk_acc[rows, :] += _dot_tn(dzb, q2)
            dv_acc[rows, :] += _dot_tn(wb, do2)
            return (cl + jnp.sum(L, axis=-1, keepdims=True), cg + jnp.sum(g, axis=-1, keepdims=True),
                    dq + _dot(dzb, kj))

        zc = jnp.zeros((2 * _TQ, 1), _F32)
        out = _sweep(i, tile, (zc, zc, jnp.zeros((2 * _TQ, _LANES), _F32)))
        dq_ref[...] = (_unstack_heads(out[3], masks) * scale).astype(dq_ref.dtype)

        @pl.when(i == nq - 1)
        def _():
            dk_ref[...] = dk_acc[...].astype(dk_ref.dtype)
            dv_ref[...] = dv_acc[...].astype(dv_ref.dtype)

    tile_spec = pl.BlockSpec((_TQ, _LANES), lambda h, i: (i, h))
    column = pl.BlockSpec((T, _LANES), lambda h, i: (0, h))
    outs, extra = _call(
        kern, comm, name=f"stick_attn_bwd_{layer}", grid=(npair, nq),
        in_specs=_qkv_specs(T, col0 // _LANES, npair) + [tile_spec, tile_spec],
        out_specs=[tile_spec, column, column],
        out_shape=[jax.ShapeDtypeStruct((T, width), _ACT)] * 3,
        scratch_shapes=[pltpu.VMEM((T, _LANES), _F32), pltpu.VMEM((T, _LANES), _F32)],
        args=(hq, hq, hq, o, do), semantics=("arbitrary", "arbitrary"))
    return outs, extra


_DENSE = ("w_in", "w_proj_a", "w_proj_b", "w_out", "w_ffn_in", "w_ffn_out")
_COL_SHARDED = {"w_in": True, "w_proj_a": True, "w_proj_b": True, "w_out": False, "w_ffn_in": True, "w_ffn_out": False}
_SMALL = ("b_gate", "rel_bias", "ln1_g", "ln1_b", "ln2_g", "ln2_b")


class _Plans:
    def __init__(self, plans=None):
        self.plans = plans or {}

    def start(self, key):
        if key not in self.plans:
            return None, None
        return self.plans[key]()

    @staticmethod
    def finish(done, extra):
        if done is not None:
            done(extra)


def _layer_fwd(x, W, small, l, alpha, plans):
    WA = small["rel_bias"].shape[1] * _HEAD
    row = lambda v: v[l].reshape(1, -1)
    hq, hg = _in_proj(x, W["w_in"], l)
    WB = (hq.shape[1] - 3 * WA) // 3
    bias = _bias_tiles(small["rel_bias"][l])
    comm, done = plans.start(f"band_fwd_{l}")
    oa, extra = _attn_a_fwd(hq, bias, 0, WA, l, comm)
    plans.finish(done, extra)
    comm, done = plans.start(f"stick_fwd_{l}")
    ob, extra = _sb_fwd(hq, 3 * WA, WB, l, comm)
    plans.finish(done, extra)
    x1, u1, pre, ya, yb = _mix_fwd(oa, ob, hg, x, W["w_proj_a"], W["w_proj_b"], W["w_out"],
                                   row(small["b_gate"]), row(small["ln1_g"]), row(small["ln1_b"]), alpha, l)
    comm, done = plans.start(f"ffn_fwd_{l}")
    (x2, u2, act, gu), extra = _ffn_fwd(x1, W["w_ffn_in"], W["w_ffn_out"], row(small["ln2_g"]), row(small["ln2_b"]),
                                        alpha, l, comm)
    plans.finish(done, extra)
    return x2, dict(x=x, hq=hq, hg=hg, bias=bias, oa=oa, ob=ob, x1=x1, u1=u1, pre=pre, ya=ya, yb=yb, u2=u2, act=act, gu=gu)


def _layer_bwd(dy_or_target, S, W, small, l, last, alpha, plans, gw):
    D = S["x"].shape[1]
    WA, WB = S["oa"].shape[1], S["ob"].shape[1]
    row = lambda v: v[l].reshape(1, -1)

    def blocks(g, n):
        return g if _COL_SHARDED[n] else g.reshape(4, g.shape[0] // 4, g.shape[1])

    du2, du2b, dgu, st2 = _ffn_bwd_a(S["u2"], dy_or_target, S["gu"], row(small["ln2_g"]), row(small["ln2_b"]),
                                     W["w_ffn_out"], l, last)
    dx1, _ = _residual_nt(du2, alpha, dgu, W["w_ffn_in"], f"ffn_bwd_b_{l}")
    gw["w_ffn_in"] = blocks(_grad_w(S["x1"], dgu, col_shards=True, name=f"grad_w_ffn_in_{l}")[0], "w_ffn_in")
    gw["w_ffn_out"] = blocks(_grad_w(S["act"], du2b, col_shards=False, name=f"grad_w_ffn_out_{l}")[0], "w_ffn_out")
    du1, du1b, dya, dyb, dhg, doa, dob, st1 = _mix_bwd(S["u1"], dx1, S["ya"], S["yb"], S["hg"], W["w_proj_a"],
                                                       W["w_proj_b"], W["w_out"], row(small["b_gate"]),
                                                       row(small["ln1_g"]), alpha, l)
    gw["w_out"] = blocks(_grad_w(S["pre"], du1b, col_shards=False, name=f"grad_w_out_{l}")[0], "w_out")
    gw["w_proj_a"] = blocks(_grad_w(S["oa"], dya, col_shards=True, name=f"grad_w_proj_a_{l}")[0], "w_proj_a")
    gw["w_proj_b"] = blocks(_grad_w(S["ob"], dyb, col_shards=True, name=f"grad_w_proj_b_{l}")[0], "w_proj_b")
    comm, done = plans.start(f"band_bwd_{l}")
    (dqa, dka, dva, dbias), extra = _attn_a_bwd(S["hq"], S["bias"], doa, 0, WA, l, comm)
    plans.finish(done, extra)
    comm, done = plans.start(f"stick_bwd_{l}")
    (dqb, dkb, dvb), extra = _sb_bwd(S["hq"], S["ob"], dob, 3 * WA, WB, l, comm)
    plans.finish(done, extra)
    dh = jnp.concatenate([dqa, dka, dva, dqb, dkb, dvb, dhg], axis=1)
    comm, done = plans.start(f"grad_w_in_{l}")
    g, extra = _grad_w(S["x"], dh, col_shards=True, name=f"grad_w_in_{l}", comm=comm)
    gw["w_in"] = blocks(g, "w_in")
    plans.finish(done, extra)
    comm, done = plans.start(f"in_proj_bwd_{l}")
    dx, extra = _residual_nt(du1, alpha, dh, W["w_in"], f"in_proj_bwd_{l}", comm)
    plans.finish(done, extra)
    gs = dict(b_gate=st1[0], rel_bias=_fold_bias_grad(dbias), ln1_g=st1[1, :D], ln1_b=st1[1, D:],
              ln2_g=st2[0], ln2_b=st2[1])
    return dx, gs, st2[2]


def _local_step(x, target, W, small, plans=None, gws=None):
    depth = len(W)
    alpha = float((2 * depth) ** 0.25)
    plans = plans or _Plans()
    gws = gws if gws is not None else [dict() for _ in range(depth)]
    saved = []
    h = x
    for l in range(depth):
        h, S = _layer_fwd(h, W[l], small, l, alpha, plans)
        saved.append(S)
    gss = [None] * depth
    d = target
    sq = None
    for l in reversed(range(depth)):
        d, gss[l], sq_l = _layer_bwd(d, saved[l], W[l], small, l, l == depth - 1, alpha, plans, gws[l])
        if l == depth - 1:
            sq = sq_l
    return sq, d, gws, gss


def _place():
    return lax.axis_index("x"), lax.axis_index("y"), lax.axis_index("c")


def _remote(src, dst, send_sem, recv_sem, to):
    return pltpu.make_async_remote_copy(src_ref=src, dst_ref=dst, send_sem=send_sem, recv_sem=recv_sem,
                                        device_id=to, device_id_type=_MESH)


def _half(ref, hc):
    kh = ref.shape[0] // 2
    return ref.at[pl.ds(pl.multiple_of(hc * kh, 16), kh), :]


def _gather_plan(blocks, fractions):
    nt = len(blocks)

    def run(step, nsteps, ins, outs, sems):
        send_sems, recv_sems, loc_sems = sems
        x, y, c = _place()
        k = 2 * x + y
        me, sibling = (x, y, c), (x, y, 1 - c)
        chips = [(1 - x, y), (x, 1 - y), (1 - x, 1 - y)]
        chip_k = [2 * cx + cy for cx, cy in chips]

        def ici(t, s, owner_k, to, src=None):
            dst = _half(outs[t].at[owner_k], c)
            return _remote(dst if src is None else src, dst, send_sems.at[t, s], recv_sems.at[t, s], to)

        def passed(t, s, hc, to):
            blk = _half(outs[t].at[chip_k[s]], hc)
            return _remote(blk, blk, send_sems.at[t, 3 + s], recv_sems.at[t, 3 + s], to)

        def local(t):
            return pltpu.make_async_copy(ins[t], outs[t].at[k], loc_sems.at[t])

        @pl.when(step == 0)
        def _():
            for t in range(nt):
                local(t).start()
                for s, chip in enumerate(chips):
                    ici(t, s, k, (*chip, c), src=_half(ins[t], c)).start()

        for t in range(nt):
            @pl.when(step == min(nsteps - 1, int(fractions[t] * nsteps)))
            def _():
                for s in range(3):
                    ici(t, s, chip_k[s], me).wait_recv()
                    passed(t, s, c, sibling).start()

        @pl.when(step == nsteps - 1)
        def _():
            for t in range(nt):
                for s, chip in enumerate(chips):
                    passed(t, s, 1 - c, me).wait_recv()
            for t in range(nt):
                for s, chip in enumerate(chips):
                    ici(t, s, k, (*chip, c), src=_half(ins[t], c)).wait_send()
                    passed(t, s, c, sibling).wait_send()
                local(t).wait()

    return _Comm(blocks, [jax.ShapeDtypeStruct((4,) + b.shape, b.dtype) for b in blocks],
                 [pltpu.SemaphoreType.DMA((nt, 6)), pltpu.SemaphoreType.DMA((nt, 6)), pltpu.SemaphoreType.DMA((nt,))], run)


def _scatter_plan(grads, owners):
    nt = len(grads)

    def run(step, nsteps, ins, outs, sems):
        send_sems, recv_sems, loc_sems = sems
        x, y, c = _place()
        me = 4 * x + 2 * y + c

        def target(r):
            tx = 1 - x if r & 2 else x
            ty = 1 - y if r & 1 else y
            return tx, ty

        def send(t, r):
            tx, ty = target(r)
            return _remote(ins[t].at[2 * tx + ty], outs[t].at[me], send_sems.at[t, r], recv_sems.at[t, 2 * r + c],
                           (tx, ty, owners[t]))

        def local(t):
            return pltpu.make_async_copy(ins[t].at[2 * x + y], outs[t].at[me], loc_sems.at[t])

        @pl.when(step == 0)
        def _():
            for t in range(nt):
                @pl.when(c == owners[t])
                def _():
                    local(t).start()

                @pl.when(c != owners[t])
                def _():
                    send(t, 0).start()

                for r in range(1, 4):
                    send(t, r).start()

        @pl.when(step == nsteps - 1)
        def _():
            for t in range(nt):
                @pl.when(c == owners[t])
                def _():
                    for r in range(4):
                        sx, sy = target(r)
                        for cs in range(2):
                            if r == 0 and cs == owners[t]:
                                continue
                            src_dev = 4 * sx + 2 * sy + cs
                            _remote(ins[t].at[0], outs[t].at[src_dev], send_sems.at[t, r], recv_sems.at[t, 2 * r + cs],
                                    (x, y, c)).wait_recv()
                    local(t).wait()

                @pl.when(c != owners[t])
                def _():
                    send(t, 0).wait_send()

                for r in range(1, 4):
                    send(t, r).wait_send()

    return _Comm(grads, [jax.ShapeDtypeStruct((8,) + g.shape[1:], g.dtype) for g in grads],
                 [pltpu.SemaphoreType.DMA((nt, 4)), pltpu.SemaphoreType.DMA((nt, 8)), pltpu.SemaphoreType.DMA((nt,))], run)


def _share_plan(reduced, owners):
    nt = len(reduced)

    def run(step, nsteps, ins, outs, sems):
        del ins
        send_sems, recv_sems = sems
        x, y, c = _place()

        def give(t, to):
            return _remote(outs[t], outs[t], send_sems.at[t], recv_sems.at[t], to)

        @pl.when(step == 0)
        def _():
            for t in range(nt):
                @pl.when(c == owners[t])
                def _():
                    give(t, (x, y, 1 - c)).start()

        @pl.when(step == nsteps - 1)
        def _():
            for t in range(nt):
                @pl.when(c == owners[t])
                def _():
                    give(t, (x, y, 1 - c)).wait_send()

                @pl.when(c != owners[t])
                def _():
                    give(t, (x, y, c)).wait_recv()

    return _Comm(reduced, [jax.ShapeDtypeStruct(r.shape, r.dtype) for r in reduced],
                 [pltpu.SemaphoreType.DMA((nt,)), pltpu.SemaphoreType.DMA((nt,))], run,
                 aliases={t: t for t in range(nt)})


def _join(a, b):
    ni, no, ns = len(a.inputs), len(a.out_shapes), len(a.sems)

    def run(step, nsteps, ins, outs, sems):
        a.run(step, nsteps, ins[:ni], outs[:no], sems[:ns])
        b.run(step, nsteps, ins[ni:], outs[no:], sems[ns:])

    aliases = dict(a.aliases)
    aliases.update({ni + i: no + o for i, o in b.aliases.items()})
    return _Comm(a.inputs + b.inputs, a.out_shapes + b.out_shapes, a.sems + b.sems, run, aliases)


def _peer(x, y, c, r):
    px = 1 - x if r & 4 else x
    py = 1 - y if r & 2 else y
    pc = 1 - c if r & 1 else c
    return (px, py, pc), 4 * px + 2 * py + pc


def _sum_slots(st, name):
    _, K, n = st.shape
    tr = next(t for t in (256, 128, 64, 32, 16) if K % t == 0)

    def kern(s_ref, o_ref):
        acc = s_ref[0].astype(_F32)
        for d in range(1, 8):
            acc = acc + s_ref[d].astype(_F32)
        o_ref[...] = acc.astype(o_ref.dtype)

    return pl.pallas_call(
        kern, name=name, grid=(K // tr,),
        in_specs=[pl.BlockSpec((8, tr, n), lambda i: (0, i, 0))], out_specs=_rows(tr, n),
        out_shape=jax.ShapeDtypeStruct((K, n), _ACT),
        compiler_params=_cparams("parallel"),
    )(st)


def _all_reduce_small(p):
    R = p.shape[0]

    def body(p_ref, o_ref, stage, send_sems, recv_sems):
        x, y, c = _place()
        me = 4 * x + 2 * y + c
        stage[me] = p_ref[...]
        sent = []
        for r in range(1, 8):
            to, _ = _peer(x, y, c, r)
            cp = _remote(p_ref, stage.at[me], send_sems.at[r - 1], recv_sems.at[r - 1], to)
            cp.start()
            sent.append(cp)
        for r in range(1, 8):
            _, src_dev = _peer(x, y, c, r)
            _remote(p_ref, stage.at[src_dev], send_sems.at[r - 1], recv_sems.at[r - 1], (x, y, c)).wait_recv()
        acc = stage[0]
        for d in range(1, 8):
            acc = acc + stage[d]
        o_ref[...] = acc
        for cp in sent:
            cp.wait_send()

    vm = pl.BlockSpec(memory_space=pltpu.VMEM)
    return pl.pallas_call(
        body, name="all_reduce_small",
        in_specs=[vm], out_specs=vm,
        out_shape=jax.ShapeDtypeStruct((R, _LANES), _F32),
        scratch_shapes=[pltpu.VMEM((8, R, _LANES), _F32), pltpu.SemaphoreType.DMA((7,)), pltpu.SemaphoreType.DMA((7,))],
    )(p)


def _adamw(w, g, m, v, name):
    shape = w.shape
    w2, g2, m2, v2 = (a.reshape(-1, shape[-1]) for a in (w, g, m, v))
    R, C = w2.shape
    tr = next((t for t in (256, 128, 64, 32, 16) if R % t == 0), R)

    def kern(w_ref, g_ref, m_ref, v_ref, gf_ref, d_ref, nm_ref, nv_ref):
        gv = g_ref[...].astype(_F32)
        nm = _B1 * m_ref[...] + (1.0 - _B1) * gv
        nv = _B2 * v_ref[...] + (1.0 - _B2) * (gv * gv)
        m_hat = nm / (1.0 - _B1 ** _STEP)
        v_hat = nv / (1.0 - _B2 ** _STEP)
        gf_ref[...] = gv
        d_ref[...] = -_LR * (m_hat / (jnp.sqrt(v_hat) + _EPS) + _WD * w_ref[...])
        nm_ref[...] = nm
        nv_ref[...] = nv

    outs = pl.pallas_call(
        kern, name=name, grid=(R // tr,),
        in_specs=[_rows(tr, C)] * 4, out_specs=[_rows(tr, C)] * 4,
        out_shape=[jax.ShapeDtypeStruct((R, C), _F32)] * 4,
        compiler_params=_cparams("parallel"),
    )(w2, g2, m2, v2)
    return tuple(o.reshape(shape) for o in outs)


def _pack_small(gss, sq):
    parts = [gss[l][n].reshape(-1) for n in _SMALL for l in range(len(gss))] + [jnp.sum(sq).reshape(1)]
    flat = jnp.concatenate(parts)
    rows = -(-flat.shape[0] // (8 * _LANES)) * 8
    return jnp.pad(flat, (0, rows * _LANES - flat.shape[0])).reshape(rows, _LANES)


def _unpack_small(total, shapes):
    flat = total.reshape(-1)
    out, off = {}, 0
    for n in _SMALL:
        layers = []
        for _ in range(shapes[n][0]):
            size = 1
            for s in shapes[n][1:]:
                size *= s
            layers.append(flat[off:off + size].reshape(shapes[n][1:]))
            off += size
        out[n] = jnp.stack(layers)
    return out, flat[off]


_GATHER = {
    "band_fwd_0": [(0, "w_proj_a"), (0, "w_proj_b"), (0, "w_out"), (0, "w_ffn_out")],
    "stick_fwd_0": [(0, "w_ffn_in"), (1, "w_in")],
    "ffn_fwd_0": [(1, "w_proj_a"), (1, "w_proj_b"), (1, "w_out"), (1, "w_ffn_in"), (1, "w_ffn_out")],
}
_SCATTER = {
    "band_bwd_0": [(1, "w_in"), (1, "w_proj_a"), (1, "w_proj_b"), (1, "w_out")],
    "stick_bwd_0": [(1, "w_ffn_in"), (1, "w_ffn_out"), (0, "w_ffn_in"), (0, "w_ffn_out"),
                    (0, "w_proj_a"), (0, "w_proj_b"), (0, "w_out")],
    "in_proj_bwd_0": [(0, "w_in")],
}
_SHARE = {"stick_bwd_0": "band_bwd_0", "grad_w_in_0": "stick_bwd_0"}


def _owner(key):
    return 1 if key[0] == 1 or key[1] == "w_in" else 0


def kernel(x, w_in, b_gate, rel_bias, w_proj_a, w_proj_b, w_out, ln1_g, ln1_b, w_ffn_in, w_ffn_out, ln2_g, ln2_b, loss_target, m_w_in, m_b_gate, m_rel_bias, m_w_proj_a, m_w_proj_b, m_w_out, m_ln1_g, m_ln1_b, m_w_ffn_in, m_w_ffn_out, m_ln2_g, m_ln2_b, v_w_in, v_b_gate, v_rel_bias, v_w_proj_a, v_w_proj_b, v_w_out, v_ln1_g, v_ln1_b, v_w_ffn_in, v_w_ffn_out, v_ln2_g, v_ln2_b):
    names = ("w_in", "b_gate", "rel_bias", "w_proj_a", "w_proj_b", "w_out", "ln1_g", "ln1_b", "w_ffn_in", "w_ffn_out", "ln2_g", "ln2_b")
    w = dict(zip(names, (w_in, b_gate, rel_bias, w_proj_a, w_proj_b, w_out, ln1_g, ln1_b, w_ffn_in, w_ffn_out, ln2_g, ln2_b)))
    m = dict(zip(names, (m_w_in, m_b_gate, m_rel_bias, m_w_proj_a, m_w_proj_b, m_w_out, m_ln1_g, m_ln1_b, m_w_ffn_in, m_w_ffn_out, m_ln2_g, m_ln2_b)))
    v = dict(zip(names, (v_w_in, v_b_gate, v_rel_bias, v_w_proj_a, v_w_proj_b, v_w_out, v_ln1_g, v_ln1_b, v_w_ffn_in, v_w_ffn_out, v_ln2_g, v_ln2_b)))
    T, D = x.shape[-2], x.shape[-1]
    assert w_in.shape[0] == 2, "the exchange schedule below is written for two layers"

    mine = [{n: w[n][l].astype(_MXU) for n in _DENSE} for l in range(2)]
    W = [dict(), dict()]
    gws = [dict(), dict()]
    slots, final = {}, {}

    def gather(keys):
        sizes = [mine[l][n].size for l, n in keys]
        passed, fractions = 0, []
        for s in sizes:
            passed += s
            fractions.append(0.15 + 0.6 * passed / sum(sizes))

        def done(outs):
            for (l, n), o in zip(keys, outs):
                W[l][n] = o
        return _gather_plan([mine[l][n] for l, n in keys], fractions), done

    def scatter(keys):
        comm = _scatter_plan([gws[l][n] for l, n in keys], [_owner(key) for key in keys])
        return comm, lambda outs: slots.update(zip(keys, outs))

    def share(keys):
        reduced = [_sum_slots(slots[key], f"sum_grad_{key[1]}_{key[0]}") for key in keys]
        comm = _share_plan(reduced, [_owner(key) for key in keys])
        return comm, lambda outs: final.update(zip(keys, outs))

    def both(first, second):
        (ca, da), (cb, db) = first, second
        na = len(ca.out_shapes)
        return _join(ca, cb), lambda outs: (da(outs[:na]), db(outs[na:]))

    comm, done = gather([(0, "w_in")])
    done(_comm_only(comm, "gather_first"))
    plans = {key: functools.partial(gather, keys) for key, keys in _GATHER.items()}
    for key, keys in _SCATTER.items():
        plans[key] = functools.partial(scatter, keys)
    for key, scattered_under in _SHARE.items():
        handed = functools.partial(share, _SCATTER[scattered_under])
        carried = plans.get(key)
        plans[key] = handed if carried is None else (lambda carried=carried, handed=handed: both(carried(), handed()))
    small = {n: w[n] for n in _SMALL}
    sq, dx, _, gss = _local_step(x.reshape(T, D), loss_target.reshape(T, D), W, small, _Plans(plans), gws)

    comm, done = share(_SCATTER["in_proj_bwd_0"])
    done(_comm_only(comm, "share_last"))
    grads = {n: jnp.stack([final[(l, n)] for l in range(2)]) for n in _DENSE}

    total = _all_reduce_small(_pack_small(gss, sq))
    small_grads, sq_all = _unpack_small(total, {n: w[n].shape for n in _SMALL})
    grads.update(small_grads)
    loss = 0.5 * sq_all / D

    grad, delta, new_m, new_v = {}, {}, {}, {}
    for n in names:
        grad[n], delta[n], new_m[n], new_v[n] = _adamw(w[n], grads[n].reshape(w[n].shape), m[n], v[n], f"adamw_{n}")
    return (loss, dx.reshape(x.shape), *[grad[n] for n in names], *[delta[n] for n in names],
            *[new_m[n] for n in names], *[new_v[n] for n in names])
```

```python
import functools

import jax
import jax.numpy as jnp
from jax import lax
from jax.experimental import pallas as pl
from jax.experimental.pallas import tpu as pltpu

_MXU = jnp.bfloat16
_ACT = jnp.bfloat16
_F32 = jnp.float32

_HEAD = 64
_CHUNK = 64
_LANES = 128
_TQ = 128
_BAND_TILES = 5
_BIAS_TILES = 9
_REL_CLIP = 256
_LN_EPS = 1e-5
_MASKED = -1e30
_EXP_ZERO_BELOW = -87.34
_SB_WINDOW = 2
_SB_SUBTILES = 2
_VMEM_LIMIT = 56 * 1024 * 1024
_GRAD_ACC_BYTES = 12 * 1024 * 1024

_LR, _B1, _B2, _EPS, _WD, _STEP = 0.001, 0.9, 0.999, 1e-08, 0.01, 10

_MESH = pl.DeviceIdType.MESH


def _dot(a, b):
    return jnp.dot(a, b, preferred_element_type=_F32)


def _dot_nt(a, b):
    return lax.dot_general(a, b, (((1,), (1,)), ((), ())), preferred_element_type=_F32)


def _dot_tn(a, b):
    return lax.dot_general(a, b, (((0,), (0,)), ((), ())), preferred_element_type=_F32)


def _cparams(*sem):
    return pltpu.CompilerParams(dimension_semantics=sem, vmem_limit_bytes=_VMEM_LIMIT)


def _rows(t, c):
    return pl.BlockSpec((t, c), lambda i: (i, 0))


def _whole(shape):
    return pl.BlockSpec(shape, lambda i: tuple(0 for _ in shape))


_ANY = pl.BlockSpec(memory_space=pl.ANY)


def _load_cols(w_hbm, w_vmem, sem):
    n = w_hbm.shape[-1]
    cps = [pltpu.make_async_copy(w_hbm.at[k], w_vmem.at[:, pl.ds(k * n, n)], sem.at[k]) for k in range(4)]
    for cp in cps:
        cp.start()
    for cp in cps:
        cp.wait()


def _load_rows(w_hbm, w_vmem, sem):
    r = w_hbm.shape[-2]
    cps = [pltpu.make_async_copy(w_hbm.at[k], w_vmem.at[pl.ds(k * r, r), :], sem.at[k]) for k in range(4)]
    for cp in cps:
        cp.start()
    for cp in cps:
        cp.wait()


def _ln_stats(u):
    mu = jnp.mean(u, axis=-1, keepdims=True)
    xc = u - mu
    var = jnp.mean(xc * xc, axis=-1, keepdims=True)
    rstd = lax.rsqrt(var + _LN_EPS)
    return xc * rstd, rstd


def _ln_bwd(u, dy, gamma):
    xhat, rstd = _ln_stats(u)
    dxh = dy * gamma
    m1 = jnp.mean(dxh, axis=-1, keepdims=True)
    m2 = jnp.mean(dxh * xhat, axis=-1, keepdims=True)
    du = rstd * (dxh - m1 - xhat * m2)
    return du, jnp.sum(dy * xhat, axis=0, keepdims=True), jnp.sum(dy, axis=0, keepdims=True), xhat


def _divisor_tile(n, cap):
    best = None
    for t in range(_LANES, min(n, cap) + 1, _LANES):
        if n % t == 0:
            best = t
    return best or n


class _Comm:
    def __init__(self, inputs, out_shapes, sems, run, aliases=None):
        self.inputs, self.out_shapes, self.sems, self.run = list(inputs), list(out_shapes), list(sems), run
        self.aliases = aliases or {}


def _call(kern, comm, *, name, grid, in_specs, out_specs, out_shape, scratch_shapes, args, semantics):
    in_specs, out_specs, out_shape, scratch_shapes = list(in_specs), list(out_specs), list(out_shape), list(scratch_shapes)
    if comm is None:
        outs = pl.pallas_call(kern, name=name, grid=grid, in_specs=in_specs, out_specs=out_specs, out_shape=out_shape,
                              scratch_shapes=scratch_shapes, compiler_params=_cparams(*semantics))(*args)
        return list(outs), []
    n_in, n_out, n_scr = len(in_specs), len(out_specs), len(scratch_shapes)
    ci, co = len(comm.inputs), len(comm.out_shapes)
    nsteps = functools.reduce(lambda a, b: a * b, grid, 1)

    def fused(*refs):
        a, b = n_in, n_in + ci
        c, d = b + n_out, b + n_out + co
        e = d + n_scr
        step = pl.program_id(0)
        for ax in range(1, len(grid)):
            step = step * grid[ax] + pl.program_id(ax)
        comm.run(step, nsteps, refs[a:b], refs[c:d], refs[e:])
        kern(*refs[:a], *refs[b:c], *refs[d:e])

    outs = pl.pallas_call(
        fused, name=name, grid=grid, in_specs=in_specs + [_ANY] * ci, out_specs=out_specs + [_ANY] * co,
        out_shape=out_shape + comm.out_shapes, scratch_shapes=scratch_shapes + comm.sems,
        input_output_aliases={n_in + i: n_out + o for i, o in comm.aliases.items()},
        compiler_params=_cparams(*("arbitrary" for _ in grid)))(*args, *comm.inputs)
    return list(outs[:n_out]), list(outs[n_out:])


def _comm_only(comm, name):
    def body(*refs):
        ci, co = len(comm.inputs), len(comm.out_shapes)
        comm.run(0, 1, refs[:ci], refs[ci:ci + co], refs[ci + co:])

    outs = pl.pallas_call(body, name=name, in_specs=[_ANY] * len(comm.inputs), out_specs=[_ANY] * len(comm.out_shapes),
                          out_shape=comm.out_shapes, scratch_shapes=comm.sems,
                          input_output_aliases=dict(comm.aliases))(*comm.inputs)
    return list(outs)


def _in_proj(x, w_in, layer):
    T, D = x.shape
    N = 4 * w_in.shape[-1]
    NQ = N - 2 * D
    tm = 256

    def kern(x_ref, w_hbm, hq_ref, hg_ref, w_v, sem):
        @pl.when(pl.program_id(0) == 0)
        def _():
            _load_cols(w_hbm, w_v, sem)

        xb = x_ref[...].astype(_MXU)
        hq_ref[...] = _dot(xb, w_v[:, :NQ]).astype(hq_ref.dtype)
        hg_ref[...] = _dot(xb, w_v[:, NQ:])

    return pl.pallas_call(
        kern, name=f"in_proj_{layer}", grid=(T // tm,),
        in_specs=[_rows(tm, D), _ANY],
        out_specs=[_rows(tm, NQ), _rows(tm, 2 * D)],
        out_shape=[jax.ShapeDtypeStruct((T, NQ), _ACT), jax.ShapeDtypeStruct((T, 2 * D), _F32)],
        scratch_shapes=[pltpu.VMEM((D, N), w_in.dtype), pltpu.SemaphoreType.DMA((4,))],
        compiler_params=_cparams("arbitrary"),
    )(x, w_in)


def _mix_fwd(oa, ob, hg, x, wpa, wpb, wo, bg, gamma, beta, alpha, layer):
    T, D = x.shape
    WA, WB = oa.shape[1], ob.shape[1]
    tm = 256

    def kern(oa_ref, ob_ref, hg_ref, x_ref, bg_ref, g_ref, b_ref, wpa_h, wpb_h, wo_h,
             x1_ref, u1_ref, pre_ref, ya_ref, yb_ref, wpa_v, wpb_v, wo_v, sa, sb, so):
        @pl.when(pl.program_id(0) == 0)
        def _():
            _load_cols(wpa_h, wpa_v, sa)
            _load_cols(wpb_h, wpb_v, sb)
            _load_rows(wo_h, wo_v, so)

        ya = _dot(oa_ref[...].astype(_MXU), wpa_v[...])
        yb = _dot(ob_ref[...].astype(_MXU), wpb_v[...])
        hgv = hg_ref[...]
        bgv = bg_ref[...]
        ga = jax.nn.sigmoid(hgv[:, :D] + bgv[:, :D])
        gb = jax.nn.sigmoid(hgv[:, D:] + bgv[:, D:])
        pre = ga * ya + gb * yb
        mix = _dot(pre.astype(_MXU), wo_v[...])
        u = alpha * x_ref[...] + mix
        xhat, _ = _ln_stats(u)
        x1_ref[...] = xhat * g_ref[...] + b_ref[...]
        u1_ref[...] = u
        pre_ref[...] = pre.astype(pre_ref.dtype)
        ya_ref[...] = ya.astype(ya_ref.dtype)
        yb_ref[...] = yb.astype(yb_ref.dtype)

    return pl.pallas_call(
        kern, name=f"mix_fwd_{layer}", grid=(T // tm,),
        in_specs=[_rows(tm, WA), _rows(tm, WB), _rows(tm, 2 * D), _rows(tm, D),
                  _whole((1, 2 * D)), _whole((1, D)), _whole((1, D)), _ANY, _ANY, _ANY],
        out_specs=[_rows(tm, D)] * 5,
        out_shape=[jax.ShapeDtypeStruct((T, D), _F32), jax.ShapeDtypeStruct((T, D), _F32)]
        + [jax.ShapeDtypeStruct((T, D), _ACT)] * 3,
        scratch_shapes=[pltpu.VMEM((WA, D), wpa.dtype), pltpu.VMEM((WB, D), wpb.dtype), pltpu.VMEM((D, D), wo.dtype),
                        pltpu.SemaphoreType.DMA((4,)), pltpu.SemaphoreType.DMA((4,)), pltpu.SemaphoreType.DMA((4,))],
        compiler_params=_cparams("arbitrary"),
    )(oa, ob, hg, x, bg, gamma, beta, wpa, wpb, wo)


def _ffn_fwd(x1, wfi, wfo, gamma, beta, alpha, layer, comm=None):
    T, D = x1.shape
    F2 = 4 * wfi.shape[-1]
    F = F2 // 2
    tm = 256
    fc = F // 2

    def kern(x_ref, g_ref, b_ref, wi_h, wo_h, x2_ref, u2_ref, act_ref, gu_ref, wi_v, wo_v, si, so):
        @pl.when(pl.program_id(0) == 0)
        def _():
            _load_cols(wi_h, wi_v, si)
            _load_rows(wo_h, wo_v, so)

        x = x_ref[...]
        xb = x.astype(_MXU)
        ffn = jnp.zeros((tm, D), _F32)
        for c in range(2):
            g = _dot(xb, wi_v[:, c * fc:(c + 1) * fc])
            u = _dot(xb, wi_v[:, F + c * fc:F + (c + 1) * fc])
            act = g * jax.nn.sigmoid(g) * u
            ab = act.astype(_MXU)
            ffn = ffn + _dot(ab, wo_v[c * fc:(c + 1) * fc, :])
            act_ref[:, c * fc:(c + 1) * fc] = ab.astype(act_ref.dtype)
            gu_ref[:, c * fc:(c + 1) * fc] = g.astype(gu_ref.dtype)
            gu_ref[:, F + c * fc:F + (c + 1) * fc] = u.astype(gu_ref.dtype)
        uu = alpha * x + ffn
        xhat, _ = _ln_stats(uu)
        x2_ref[...] = xhat * g_ref[...] + b_ref[...]
        u2_ref[...] = uu

    return _call(
        kern, comm, name=f"ffn_fwd_{layer}", grid=(T // tm,),
        in_specs=[_rows(tm, D), _whole((1, D)), _whole((1, D)), _ANY, _ANY],
        out_specs=[_rows(tm, D), _rows(tm, D), _rows(tm, F), _rows(tm, F2)],
        out_shape=[jax.ShapeDtypeStruct((T, D), _F32), jax.ShapeDtypeStruct((T, D), _F32),
                   jax.ShapeDtypeStruct((T, F), _ACT), jax.ShapeDtypeStruct((T, F2), _ACT)],
        scratch_shapes=[pltpu.VMEM((D, F2), wfi.dtype), pltpu.VMEM((F, D), wfo.dtype),
                        pltpu.SemaphoreType.DMA((4,)), pltpu.SemaphoreType.DMA((4,))],
        args=(x1, gamma, beta, wfi, wfo), semantics=("arbitrary",))


def _ffn_bwd_a(u2, dy_or_target, gu, gamma, beta, wfo, layer, last):
    T, D = u2.shape
    F2 = gu.shape[1]
    F = F2 // 2
    tm = 256
    fc = F // 2

    def kern(u_ref, dy_ref, gu_ref, g_ref, b_ref, wo_h, du_ref, dub_ref, dgu_ref, st_ref, wo_v, so):
        @pl.when(pl.program_id(0) == 0)
        def _():
            _load_rows(wo_h, wo_v, so)
            st_ref[...] = jnp.zeros_like(st_ref)

        gam = g_ref[...]
        u = u_ref[...]
        if last:
            xhat0, _ = _ln_stats(u)
            err = xhat0 * gam + b_ref[...] - dy_ref[...]
            dy = err * (1.0 / D)
            st_ref[2:3, :] += jnp.sum(err * err, axis=0, keepdims=True)
        else:
            dy = dy_ref[...]
        du, dgam, dbet, _ = _ln_bwd(u, dy, gam)
        st_ref[0:1, :] += dgam
        st_ref[1:2, :] += dbet
        du_ref[...] = du
        dub = du.astype(_MXU)
        dub_ref[...] = dub.astype(dub_ref.dtype)
        for c in range(2):
            dact = _dot_nt(dub, wo_v[c * fc:(c + 1) * fc, :])
            g = gu_ref[:, c * fc:(c + 1) * fc].astype(_F32)
            uu = gu_ref[:, F + c * fc:F + (c + 1) * fc].astype(_F32)
            sg = jax.nn.sigmoid(g)
            dgu_ref[:, c * fc:(c + 1) * fc] = (dact * uu * (sg * (1.0 + g * (1.0 - sg)))).astype(dgu_ref.dtype)
            dgu_ref[:, F + c * fc:F + (c + 1) * fc] = (dact * (g * sg)).astype(dgu_ref.dtype)

    return pl.pallas_call(
        kern, name=f"ffn_bwd_a_{layer}", grid=(T // tm,),
        in_specs=[_rows(tm, D), _rows(tm, D), _rows(tm, F2), _whole((1, D)), _whole((1, D)), _ANY],
        out_specs=[_rows(tm, D), _rows(tm, D), _rows(tm, F2), _whole((8, D))],
        out_shape=[jax.ShapeDtypeStruct((T, D), _F32), jax.ShapeDtypeStruct((T, D), _ACT),
                   jax.ShapeDtypeStruct((T, F2), _ACT), jax.ShapeDtypeStruct((8, D), _F32)],
        scratch_shapes=[pltpu.VMEM((F, D), wfo.dtype), pltpu.SemaphoreType.DMA((4,))],
        compiler_params=_cparams("arbitrary"),
    )(u2, dy_or_target, gu, gamma, beta, wfo)


def _residual_nt(res, res_scale, d, w, name, comm=None):
    T, K = res.shape
    N = d.shape[1]
    tm = 256

    def kern(r_ref, d_ref, w_hbm, o_ref, w_v, sem):
        @pl.when(pl.program_id(0) == 0)
        def _():
            _load_cols(w_hbm, w_v, sem)

        o_ref[...] = res_scale * r_ref[...] + _dot_nt(d_ref[...].astype(_MXU), w_v[...])

    outs, extra = _call(
        kern, comm, name=name, grid=(T // tm,),
        in_specs=[_rows(tm, K), _rows(tm, N), _ANY], out_specs=[_rows(tm, K)],
        out_shape=[jax.ShapeDtypeStruct((T, K), _F32)],
        scratch_shapes=[pltpu.VMEM((K, N), w.dtype), pltpu.SemaphoreType.DMA((4,))],
        args=(res, d, w), semantics=("arbitrary",))
    return outs[0], extra


def _mix_bwd(u1, dx1, ya, yb, hg, wpa, wpb, wo, bg, gamma, alpha, layer):
    del alpha
    T, D = u1.shape
    WA, WB = wpa.shape[-2], wpb.shape[-2]
    tm = 256

    def kern(u_ref, dx_ref, ya_ref, yb_ref, hg_ref, bg_ref, g_ref, wpa_h, wpb_h, wo_h,
             du_ref, dub_ref, dya_ref, dyb_ref, dhg_ref, doa_ref, dob_ref, st_ref,
             wpa_v, wpb_v, wo_v, sa, sb, so):
        @pl.when(pl.program_id(0) == 0)
        def _():
            _load_cols(wpa_h, wpa_v, sa)
            _load_cols(wpb_h, wpb_v, sb)
            _load_rows(wo_h, wo_v, so)
            st_ref[...] = jnp.zeros_like(st_ref)

        du, dgam, dbet, _ = _ln_bwd(u_ref[...], dx_ref[...], g_ref[...])
        st_ref[1:2, :D] += dgam
        st_ref[1:2, D:] += dbet
        du_ref[...] = du
        dub = du.astype(_MXU)
        dub_ref[...] = dub.astype(dub_ref.dtype)
        dpre = _dot_nt(dub, wo_v[...])
        hgv = hg_ref[...]
        bgv = bg_ref[...]
        ga = jax.nn.sigmoid(hgv[:, :D] + bgv[:, :D])
        gb = jax.nn.sigmoid(hgv[:, D:] + bgv[:, D:])
        dya = (dpre * ga).astype(_MXU)
        dyb = (dpre * gb).astype(_MXU)
        dsa = dpre * ya_ref[...].astype(_F32) * (ga * (1.0 - ga))
        dsb = dpre * yb_ref[...].astype(_F32) * (gb * (1.0 - gb))
        st_ref[0:1, :D] += jnp.sum(dsa, axis=0, keepdims=True)
        st_ref[0:1, D:] += jnp.sum(dsb, axis=0, keepdims=True)
        dya_ref[...] = dya.astype(dya_ref.dtype)
        dyb_ref[...] = dyb.astype(dyb_ref.dtype)
        dhg_ref[:, :D] = dsa.astype(dhg_ref.dtype)
        dhg_ref[:, D:] = dsb.astype(dhg_ref.dtype)
        doa_ref[...] = _dot_nt(dya, wpa_v[...]).astype(doa_ref.dtype)
        dob_ref[...] = _dot_nt(dyb, wpb_v[...]).astype(dob_ref.dtype)

    return pl.pallas_call(
        kern, name=f"mix_bwd_{layer}", grid=(T // tm,),
        in_specs=[_rows(tm, D)] * 4 + [_rows(tm, 2 * D), _whole((1, 2 * D)), _whole((1, D)), _ANY, _ANY, _ANY],
        out_specs=[_rows(tm, D)] * 4 + [_rows(tm, 2 * D), _rows(tm, WA), _rows(tm, WB), _whole((8, 2 * D))],
        out_shape=[jax.ShapeDtypeStruct((T, D), _F32)] + [jax.ShapeDtypeStruct((T, D), _ACT)] * 3
        + [jax.ShapeDtypeStruct((T, 2 * D), _ACT), jax.ShapeDtypeStruct((T, WA), _ACT),
           jax.ShapeDtypeStruct((T, WB), _ACT), jax.ShapeDtypeStruct((8, 2 * D), _F32)],
        scratch_shapes=[pltpu.VMEM((WA, D), wpa.dtype), pltpu.VMEM((WB, D), wpb.dtype), pltpu.VMEM((D, D), wo.dtype),
                        pltpu.SemaphoreType.DMA((4,)), pltpu.SemaphoreType.DMA((4,)), pltpu.SemaphoreType.DMA((4,))],
        compiler_params=_cparams("arbitrary"),
    )(u1, dx1, ya, yb, hg, bg, gamma, wpa, wpb, wo)


def _grad_w(a, b, *, col_shards, name, comm=None):
    T, M = a.shape
    N = b.shape[1]
    tk = 512
    n = N // 4 if col_shards else N
    whole = M * N * 4 <= _GRAD_ACC_BYTES
    tn = N if whole else (n if col_shards else _divisor_tile(N, _GRAD_ACC_BYTES // (4 * M)))
    nk = T // tk

    def kern(a_ref, b_ref, o_ref, acc):
        k = pl.program_id(1)

        @pl.when(k == 0)
        def _():
            acc[...] = jnp.zeros_like(acc)

        acc[...] += _dot_tn(a_ref[...].astype(_MXU), b_ref[...].astype(_MXU))

        @pl.when(k == nk - 1)
        def _():
            if col_shards and whole:
                for s in range(4):
                    o_ref[s] = acc[:, s * n:(s + 1) * n].astype(o_ref.dtype)
            else:
                o_ref[...] = acc[...].astype(o_ref.dtype)

    if col_shards:
        out_spec = (pl.BlockSpec((4, M, n), lambda j, k: (0, 0, 0)) if whole
                    else pl.BlockSpec((None, M, n), lambda j, k: (j, 0, 0)))
        out_shape = jax.ShapeDtypeStruct((4, M, n), _ACT)
    else:
        out_spec = pl.BlockSpec((M, tn), lambda j, k: (0, j))
        out_shape = jax.ShapeDtypeStruct((M, N), _ACT)
    outs, extra = _call(
        kern, comm, name=name, grid=(N // tn, nk),
        in_specs=[pl.BlockSpec((tk, M), lambda j, k: (k, 0)), pl.BlockSpec((tk, tn), lambda j, k: (k, j))],
        out_specs=[out_spec], out_shape=[out_shape], scratch_shapes=[pltpu.VMEM((M, tn), _F32)],
        args=(a, b), semantics=("parallel", "arbitrary"))
    return outs[0], extra


def _bias_tiles(rel):
    H = rel.shape[0]
    span = _TQ * _BAND_TILES - 1
    edge = span - _REL_CLIP
    gvec = jnp.concatenate([jnp.broadcast_to(rel[:, :1], (H, edge)), rel, jnp.broadcast_to(rel[:, -1:], (H, edge))], axis=1)
    width = _BIAS_TILES * _TQ
    period = width + _TQ
    tiled = jnp.broadcast_to(jnp.pad(gvec[:, ::-1], ((0, 0), (0, 1)))[:, None, :], (H, _TQ, period))
    rows = tiled.reshape(H, _TQ * period)[:, :_TQ * (period - 1)].reshape(H, _TQ, period - 1)[:, :, _TQ - 1:]
    r = jnp.arange(_TQ)[:, None]
    u = jnp.arange(width)[None, :]
    d = 4 * _TQ + r - u
    rm = r % _CHUNK
    valid = (d >= rm - (_CHUNK - 1)) & (d <= rm + 8 * _CHUNK)
    tiles = jnp.where(valid[None], rows, _MASKED)
    return tiles.reshape(H // 2, 2 * _TQ, _BIAS_TILES, _TQ).transpose(0, 2, 1, 3)


def _fold_bias_grad(db):
    H = 2 * db.shape[0]
    width = _BIAS_TILES * _TQ
    period = width + _TQ
    x = jnp.pad(db.transpose(0, 2, 1, 3).reshape(H, _TQ, width), ((0, 0), (0, 0), (_TQ - 1, 0)))
    skew = jnp.pad(x.reshape(H, _TQ * (period - 1)), ((0, 0), (0, _TQ))).reshape(H, _TQ, period)
    dg = skew.sum(axis=1)[:, :period - 1][:, ::-1]
    span = _TQ * _BAND_TILES - 1
    edge = span - _REL_CLIP
    mid = dg[:, edge:edge + 2 * _REL_CLIP + 1]
    lo = dg[:, :edge].sum(axis=1)
    hi = dg[:, edge + 2 * _REL_CLIP + 1:].sum(axis=1)
    return mid.at[:, 0].add(lo).at[:, -1].add(hi)


def _band_window(i):
    j0 = jnp.maximum(i - (_BAND_TILES - 1), 0)
    return j0, (_BAND_TILES - 1) - (i - j0)


def _head_masks():
    lane = lax.broadcasted_iota(jnp.int32, (1, _LANES), 1)
    return [(lane // _HEAD) == hh for hh in range(2)]


def _stack_heads(x, masks):
    return jnp.concatenate([jnp.where(m, x, jnp.zeros_like(x)) for m in masks], axis=0)


def _unstack_heads(y, masks):
    return jnp.where(masks[0], y[:_TQ], y[_TQ:])


def _scaled(q):
    return q * jnp.asarray(_HEAD ** -0.5, q.dtype)


def _band_probs(q2, k_ref, b_ref, j0, boff):
    s = []
    for j in range(_BAND_TILES):
        kj = k_ref[pl.ds(pl.multiple_of((j0 + j) * _TQ, _TQ), _TQ), :]
        s.append(_dot_nt(q2, kj) + b_ref[boff + j])
    m = jnp.max(functools.reduce(jnp.maximum, s), axis=-1, keepdims=True)
    p = [jnp.exp(x - m) for x in s]
    l = jnp.sum(functools.reduce(lambda a, b: a + b, p), axis=-1, keepdims=True)
    return p, 1.0 / l


def _qkv_specs(T, cb, npair, tq=_TQ):
    return [pl.BlockSpec((tq, _LANES), lambda h, i: (i, cb + h)),
            pl.BlockSpec((T, _LANES), lambda h, i: (0, cb + npair + h)),
            pl.BlockSpec((T, _LANES), lambda h, i: (0, cb + 2 * npair + h))]


def _attn_a_fwd(hq, bias, col0, width, layer, comm=None):
    T = hq.shape[0]
    npair = width // _LANES
    nq = T // _TQ

    def kern(q_ref, k_ref, v_ref, b_ref, o_ref):
        i = pl.program_id(1)
        j0, boff = _band_window(i)
        masks = _head_masks()
        q2 = _stack_heads(_scaled(q_ref[...]), masks)
        p, inv = _band_probs(q2, k_ref, b_ref, j0, boff)
        o = jnp.zeros((2 * _TQ, _LANES), _F32)
        for j in range(_BAND_TILES):
            vj = v_ref[pl.ds(pl.multiple_of((j0 + j) * _TQ, _TQ), _TQ), :]
            o = o + _dot(p[j].astype(_MXU), vj)
        o_ref[...] = _unstack_heads(o * inv, masks).astype(o_ref.dtype)

    outs, extra = _call(
        kern, comm, name=f"band_attn_fwd_{layer}", grid=(npair, nq),
        in_specs=_qkv_specs(T, col0 // _LANES, npair)
        + [pl.BlockSpec((None, _BIAS_TILES, 2 * _TQ, _TQ), lambda h, i: (h, 0, 0, 0))],
        out_specs=[pl.BlockSpec((_TQ, _LANES), lambda h, i: (i, h))],
        out_shape=[jax.ShapeDtypeStruct((T, width), _ACT)], scratch_shapes=[],
        args=(hq, hq, hq, bias), semantics=("arbitrary", "arbitrary"))
    return outs[0], extra


def _attn_a_bwd(hq, bias, do, col0, width, layer, comm=None):
    T = hq.shape[0]
    npair = width // _LANES
    nq = T // _TQ
    scale = _HEAD ** -0.5

    def kern(q_ref, k_ref, v_ref, b_ref, do_ref, dq_ref, dk_ref, dv_ref, db_ref, dk_acc, dv_acc):
        i = pl.program_id(1)

        @pl.when(i == 0)
        def _():
            dk_acc[...] = jnp.zeros_like(dk_acc)
            dv_acc[...] = jnp.zeros_like(dv_acc)
            db_ref[...] = jnp.zeros_like(db_ref)

        j0, boff = _band_window(i)
        masks = _head_masks()
        q2 = _stack_heads(_scaled(q_ref[...]), masks)
        do2 = _stack_heads(do_ref[...], masks).astype(_MXU)
        p, inv = _band_probs(q2, k_ref, b_ref, j0, boff)
        rows = [pl.ds(pl.multiple_of((j0 + j) * _TQ, _TQ), _TQ) for j in range(_BAND_TILES)]
        p = [x * inv for x in p]
        dp = [_dot_nt(do2, v_ref[rows[j], :]) for j in range(_BAND_TILES)]
        delta = jnp.sum(functools.reduce(lambda a, b: a + b, [p[j] * dp[j] for j in range(_BAND_TILES)]),
                        axis=-1, keepdims=True)
        dq = jnp.zeros((2 * _TQ, _LANES), _F32)
        for j in range(_BAND_TILES):
            ds = p[j] * (dp[j] - delta)
            db_ref[boff + j] += ds
            dsb = ds.astype(_MXU)
            dq = dq + _dot(dsb, k_ref[rows[j], :])
            dk_acc[rows[j], :] += _dot_tn(dsb, q2)
            dv_acc[rows[j], :] += _dot_tn(p[j].astype(_MXU), do2)
        dq_ref[...] = (_unstack_heads(dq, masks) * scale).astype(dq_ref.dtype)

        @pl.when(i == nq - 1)
        def _():
            dk_ref[...] = dk_acc[...].astype(dk_ref.dtype)
            dv_ref[...] = dv_acc[...].astype(dv_ref.dtype)

    strip = pl.BlockSpec((None, _BIAS_TILES, 2 * _TQ, _TQ), lambda h, i: (h, 0, 0, 0))
    tile = pl.BlockSpec((_TQ, _LANES), lambda h, i: (i, h))
    column = pl.BlockSpec((T, _LANES), lambda h, i: (0, h))
    outs, extra = _call(
        kern, comm, name=f"band_attn_bwd_{layer}", grid=(npair, nq),
        in_specs=_qkv_specs(T, col0 // _LANES, npair) + [strip, tile],
        out_specs=[tile, column, column, strip],
        out_shape=[jax.ShapeDtypeStruct((T, width), _ACT)] * 3
        + [jax.ShapeDtypeStruct((npair, _BIAS_TILES, 2 * _TQ, _TQ), _F32)],
        scratch_shapes=[pltpu.VMEM((T, _LANES), _F32), pltpu.VMEM((T, _LANES), _F32)],
        args=(hq, hq, hq, bias, do), semantics=("arbitrary", "arbitrary"))
    return outs, extra


def _suffix_matrix():
    r = lax.broadcasted_iota(jnp.int32, (_TQ, _TQ), 0)
    c = lax.broadcasted_iota(jnp.int32, (_TQ, _TQ), 1)
    r2 = lax.broadcasted_iota(jnp.int32, (2 * _TQ, _TQ), 0)
    c2 = lax.broadcasted_iota(jnp.int32, (2 * _TQ, _TQ), 1)
    return (r > c).astype(_MXU), c2 - (r2 & (_TQ - 1))


def _suffix_sums(xs, tri):
    n, k = xs[0].shape[0], len(xs)
    his = [x.astype(_MXU) for x in xs]
    los = [(x - h.astype(_F32)).astype(_MXU) for x, h in zip(xs, his)]
    y = _dot(jnp.concatenate(his + los, axis=0), tri)
    return [y[j * n:(j + 1) * n] + y[(k + j) * n:(k + j + 1) * n] for j in range(k)]


def _stick_tiles(tiles, rel, carry_l, tri):
    zs = [_dot_nt(qs, kj) for qs, kj, _, _ in tiles]
    Ls, masks = [], []
    for z, (_, _, jj, _) in zip(zs, tiles):
        nsp = -(jnp.maximum(z, 0.0) + jnp.log(1.0 + jnp.exp(-jnp.abs(z))))
        if isinstance(jj, int):
            mask = (rel < 0) if jj == 0 else None
        else:
            mask = rel < jnp.where(jj == 0, 0, _TQ)
        Ls.append(nsp if mask is None else jnp.where(mask, nsp, 0.0))
        masks.append(mask)
    carry_l = list(carry_l)
    ws = []
    for z, L, suffix, mask, (_, _, _, sub) in zip(zs, Ls, _suffix_sums(Ls, tri), masks, tiles):
        w = jnp.exp(z + L + suffix + carry_l[sub])
        ws.append(w if mask is None else jnp.where(mask, w, 0.0))
        carry_l[sub] = carry_l[sub] + jnp.sum(L, axis=-1, keepdims=True)
    return zs, Ls, ws, masks, carry_l


def _sweep(i, step, zero):
    nsub = _SB_SUBTILES

    def window():
        tiles = [(s, jj) for jj in range(_SB_WINDOW) for s in range(nsub)]
        return tuple((jnp.int32(_SB_WINDOW),) + c for c in step(tiles, [zero] * nsub))

    start = lax.cond(i >= -(-(_SB_WINDOW - 1) // nsub), window, lambda: tuple((jnp.int32(0),) + zero for _ in range(nsub)))
    outs = []
    for s in range(nsub):
        def done(c, s=s):
            return jnp.logical_or(c[0] > nsub * i + s, jnp.max(c[1]) < _EXP_ZERO_BELOW)

        def more(c, s=s):
            carries = [None] * nsub
            carries[s] = c[1:]
            return (c[0] + 1,) + step([(s, c[0])], carries)[s]

        outs.append(lax.while_loop(lambda c, done=done: jnp.logical_not(done(c)), more, start[s]))
    return outs


def _sb_fwd(hq, col0, width, layer, comm=None):
    T = hq.shape[0]
    npair = width // _LANES
    nsub = _SB_SUBTILES
    tq = nsub * _TQ

    def kern(q_ref, k_ref, v_ref, o_ref):
        i = pl.program_id(1)
        masks = _head_masks()
        tri, rel = _suffix_matrix()
        q = _scaled(q_ref[...])
        q2 = [_stack_heads(q[s * _TQ:(s + 1) * _TQ], masks) for s in range(nsub)]

        def step(tiles, carries):
            rows = [pl.ds(pl.multiple_of((nsub * i + s - jj) * _TQ, _TQ), _TQ) for s, jj in tiles]
            cls = [None if c is None else c[0] for c in carries]
            accs = [None if c is None else c[1] for c in carries]
            _, _, ws, _, cls = _stick_tiles([(q2[s], k_ref[r, :], jj, s) for (s, jj), r in zip(tiles, rows)], rel, cls, tri)
            for w, r, (s, _) in zip(ws, rows, tiles):
                accs[s] = accs[s] + _dot(w.astype(_MXU), v_ref[r, :])
            return [None if c is None else (cls[s], accs[s]) for s, c in enumerate(carries)]

        outs = _sweep(i, step, (jnp.zeros((2 * _TQ, 1), _F32), jnp.zeros((2 * _TQ, _LANES), _F32)))
        for s in range(nsub):
            o_ref[s * _TQ:(s + 1) * _TQ, :] = _unstack_heads(outs[s][2], masks)

    outs, extra = _call(
        kern, comm, name=f"stick_attn_fwd_{layer}", grid=(npair, T // tq),
        in_specs=_qkv_specs(T, col0 // _LANES, npair, tq),
        out_specs=[pl.BlockSpec((tq, _LANES), lambda h, i: (i, h))],
        out_shape=[jax.ShapeDtypeStruct((T, width), _F32)], scratch_shapes=[],
        args=(hq, hq, hq), semantics=("arbitrary", "arbitrary"))
    return outs[0], extra


def _sb_bwd(hq, o, do, col0, width, layer, comm=None):
    T = hq.shape[0]
    npair = width // _LANES
    nsub = _SB_SUBTILES
    tq = nsub * _TQ
    nq = T // tq
    scale = _HEAD ** -0.5

    def kern(q_ref, k_ref, v_ref, o_ref, do_ref, dq_ref, dk_ref, dv_ref, dk_acc, dv_acc):
        i = pl.program_id(1)

        @pl.when(i == 0)
        def _():
            dk_acc[...] = jnp.zeros_like(dk_acc)
            dv_acc[...] = jnp.zeros_like(dv_acc)

        masks = _head_masks()
        tri, rel = _suffix_matrix()
        q = _scaled(q_ref[...])
        do_t = do_ref[...]
        prod = do_t.astype(_F32) * o_ref[...]
        part = [slice(s * _TQ, (s + 1) * _TQ) for s in range(nsub)]
        q2 = [_stack_heads(q[p], masks) for p in part]
        do2 = [_stack_heads(do_t[p], masks).astype(_MXU) for p in part]
        dsum = [jnp.sum(_stack_heads(prod[p], masks), axis=-1, keepdims=True) for p in part]

        def step(tiles, carries):
            rows = [pl.ds(pl.multiple_of((nsub * i + s - jj) * _TQ, _TQ), _TQ) for s, jj in tiles]
            kjs = [k_ref[r, :] for r in rows]
            cls, cgs, dqs = ([None if c is None else c[n] for c in carries] for n in range(3))
            zs, Ls, ws, tile_masks, cls = _stick_tiles([(q2[s], kj, jj, s) for (s, jj), kj in zip(tiles, kjs)], rel, cls, tri)
            wbs = [w.astype(_MXU) for w in ws]
            gs = [wb.astype(_F32) * _dot_nt(do2[s], v_ref[r, :]) for wb, r, (s, _) in zip(wbs, rows, tiles)]
            for z, L, g, later, mask, wb, kj, r, (s, _) in zip(zs, Ls, gs, _suffix_sums(gs, tri), tile_masks, wbs, kjs,
                                                               rows, tiles):
                dz = g - jnp.exp(z + L) * (dsum[s] - (later + cgs[s]))
                if mask is not None:
                    dz = jnp.where(mask, dz, 0.0)
                dzb = dz.astype(_MXU)
                dk_acc[r, :] += _dot_tn(dzb, q2[s])
                dv_acc[r, :] += _dot_tn(wb, do2[s])
                dqs[s] = dqs[s] + _dot(dzb, kj)
                cgs[s] = cgs[s] + jnp.sum(g, axis=-1, keepdims=True)
            return [None if c is None else (cls[s], cgs[s], dqs[s]) for s, c in enumerate(carries)]

        zc = jnp.zeros((2 * _TQ, 1), _F32)
        outs = _sweep(i, step, (zc, zc, jnp.zeros((2 * _TQ, _LANES), _F32)))
        for s in range(nsub):
            dq_ref[part[s], :] = (_unstack_heads(outs[s][3], masks) * scale).astype(dq_ref.dtype)

        @pl.when(i == nq - 1)
        def _():
            dk_ref[...] = dk_acc[...].astype(dk_ref.dtype)
            dv_ref[...] = dv_acc[...].astype(dv_ref.dtype)

    tile_spec = pl.BlockSpec((tq, _LANES), lambda h, i: (i, h))
    column = pl.BlockSpec((T, _LANES), lambda h, i: (0, h))
    outs, extra = _call(
        kern, comm, name=f"stick_attn_bwd_{layer}", grid=(npair, nq),
        in_specs=_qkv_specs(T, col0 // _LANES, npair, tq) + [tile_spec, tile_spec],
        out_specs=[tile_spec, column, column],
        out_shape=[jax.ShapeDtypeStruct((T, width), _ACT)] * 3,
        scratch_shapes=[pltpu.VMEM((T, _LANES), _F32), pltpu.VMEM((T, _LANES), _F32)],
        args=(hq, hq, hq, o, do), semantics=("arbitrary", "arbitrary"))
    return outs, extra


_DENSE = ("w_in", "w_proj_a", "w_proj_b", "w_out", "w_ffn_in", "w_ffn_out")
_COL_SHARDED = {"w_in": True, "w_proj_a": True, "w_proj_b": True, "w_out": False, "w_ffn_in": True, "w_ffn_out": False}
_SMALL = ("b_gate", "rel_bias", "ln1_g", "ln1_b", "ln2_g", "ln2_b")


class _Plans:
    def __init__(self, plans=None):
        self.plans = plans or {}

    def start(self, key):
        if key not in self.plans:
            return None, None
        return self.plans[key]()

    @staticmethod
    def finish(done, extra):
        if done is not None:
            done(extra)


def _layer_fwd(x, W, small, l, alpha, plans):
    WA = small["rel_bias"].shape[1] * _HEAD
    row = lambda v: v[l].reshape(1, -1)
    hq, hg = _in_proj(x, W["w_in"], l)
    WB = (hq.shape[1] - 3 * WA) // 3
    bias = _bias_tiles(small["rel_bias"][l])
    comm, done = plans.start(f"band_fwd_{l}")
    oa, extra = _attn_a_fwd(hq, bias, 0, WA, l, comm)
    plans.finish(done, extra)
    comm, done = plans.start(f"stick_fwd_{l}")
    ob, extra = _sb_fwd(hq, 3 * WA, WB, l, comm)
    plans.finish(done, extra)
    x1, u1, pre, ya, yb = _mix_fwd(oa, ob, hg, x, W["w_proj_a"], W["w_proj_b"], W["w_out"],
                                   row(small["b_gate"]), row(small["ln1_g"]), row(small["ln1_b"]), alpha, l)
    comm, done = plans.start(f"ffn_fwd_{l}")
    (x2, u2, act, gu), extra = _ffn_fwd(x1, W["w_ffn_in"], W["w_ffn_out"], row(small["ln2_g"]), row(small["ln2_b"]),
                                        alpha, l, comm)
    plans.finish(done, extra)
    return x2, dict(x=x, hq=hq, hg=hg, bias=bias, oa=oa, ob=ob, x1=x1, u1=u1, pre=pre, ya=ya, yb=yb, u2=u2, act=act, gu=gu)


def _layer_bwd(dy_or_target, S, W, small, l, last, alpha, plans, gw):
    D = S["x"].shape[1]
    WA, WB = S["oa"].shape[1], S["ob"].shape[1]
    row = lambda v: v[l].reshape(1, -1)

    def blocks(g, n):
        return g if _COL_SHARDED[n] else g.reshape(4, g.shape[0] // 4, g.shape[1])

    du2, du2b, dgu, st2 = _ffn_bwd_a(S["u2"], dy_or_target, S["gu"], row(small["ln2_g"]), row(small["ln2_b"]),
                                     W["w_ffn_out"], l, last)
    dx1, _ = _residual_nt(du2, alpha, dgu, W["w_ffn_in"], f"ffn_bwd_b_{l}")
    gw["w_ffn_in"] = blocks(_grad_w(S["x1"], dgu, col_shards=True, name=f"grad_w_ffn_in_{l}")[0], "w_ffn_in")
    gw["w_ffn_out"] = blocks(_grad_w(S["act"], du2b, col_shards=False, name=f"grad_w_ffn_out_{l}")[0], "w_ffn_out")
    du1, du1b, dya, dyb, dhg, doa, dob, st1 = _mix_bwd(S["u1"], dx1, S["ya"], S["yb"], S["hg"], W["w_proj_a"],
                                                       W["w_proj_b"], W["w_out"], row(small["b_gate"]),
                                                       row(small["ln1_g"]), alpha, l)
    gw["w_out"] = blocks(_grad_w(S["pre"], du1b, col_shards=False, name=f"grad_w_out_{l}")[0], "w_out")
    gw["w_proj_a"] = blocks(_grad_w(S["oa"], dya, col_shards=True, name=f"grad_w_proj_a_{l}")[0], "w_proj_a")
    gw["w_proj_b"] = blocks(_grad_w(S["ob"], dyb, col_shards=True, name=f"grad_w_proj_b_{l}")[0], "w_proj_b")
    comm, done = plans.start(f"band_bwd_{l}")
    (dqa, dka, dva, dbias), extra = _attn_a_bwd(S["hq"], S["bias"], doa, 0, WA, l, comm)
    plans.finish(done, extra)
    comm, done = plans.start(f"stick_bwd_{l}")
    (dqb, dkb, dvb), extra = _sb_bwd(S["hq"], S["ob"], dob, 3 * WA, WB, l, comm)
    plans.finish(done, extra)
    dh = jnp.concatenate([dqa, dka, dva, dqb, dkb, dvb, dhg], axis=1)
    comm, done = plans.start(f"grad_w_in_{l}")
    g, extra = _grad_w(S["x"], dh, col_shards=True, name=f"grad_w_in_{l}", comm=comm)
    gw["w_in"] = blocks(g, "w_in")
    plans.finish(done, extra)
    comm, done = plans.start(f"in_proj_bwd_{l}")
    dx, extra = _residual_nt(du1, alpha, dh, W["w_in"], f"in_proj_bwd_{l}", comm)
    plans.finish(done, extra)
    gs = dict(b_gate=st1[0], rel_bias=_fold_bias_grad(dbias), ln1_g=st1[1, :D], ln1_b=st1[1, D:],
              ln2_g=st2[0], ln2_b=st2[1])
    return dx, gs, st2[2]


def _local_step(x, target, W, small, plans=None, gws=None):
    depth = len(W)
    alpha = float((2 * depth) ** 0.25)
    plans = plans or _Plans()
    gws = gws if gws is not None else [dict() for _ in range(depth)]
    saved = []
    h = x
    for l in range(depth):
        h, S = _layer_fwd(h, W[l], small, l, alpha, plans)
        saved.append(S)
    gss = [None] * depth
    d = target
    sq = None
    for l in reversed(range(depth)):
        d, gss[l], sq_l = _layer_bwd(d, saved[l], W[l], small, l, l == depth - 1, alpha, plans, gws[l])
        if l == depth - 1:
            sq = sq_l
    return sq, d, gws, gss


def _place():
    return lax.axis_index("x"), lax.axis_index("y"), lax.axis_index("c")


def _remote(src, dst, send_sem, recv_sem, to):
    return pltpu.make_async_remote_copy(src_ref=src, dst_ref=dst, send_sem=send_sem, recv_sem=recv_sem,
                                        device_id=to, device_id_type=_MESH)


def _half(ref, hc):
    kh = ref.shape[0] // 2
    return ref.at[pl.ds(pl.multiple_of(hc * kh, 16), kh), :]


def _gather_plan(blocks, fractions):
    nt = len(blocks)

    def run(step, nsteps, ins, outs, sems):
        send_sems, recv_sems, loc_sems = sems
        x, y, c = _place()
        k = 2 * x + y
        me, sibling = (x, y, c), (x, y, 1 - c)
        chips = [(1 - x, y), (x, 1 - y), (1 - x, 1 - y)]
        chip_k = [2 * cx + cy for cx, cy in chips]

        def ici(t, s, owner_k, to, src=None):
            dst = _half(outs[t].at[owner_k], c)
            return _remote(dst if src is None else src, dst, send_sems.at[t, s], recv_sems.at[t, s], to)

        def passed(t, s, hc, to):
            blk = _half(outs[t].at[chip_k[s]], hc)
            return _remote(blk, blk, send_sems.at[t, 3 + s], recv_sems.at[t, 3 + s], to)

        def local(t):
            return pltpu.make_async_copy(ins[t], outs[t].at[k], loc_sems.at[t])

        @pl.when(step == 0)
        def _():
            for t in range(nt):
                local(t).start()
                for s, chip in enumerate(chips):
                    ici(t, s, k, (*chip, c), src=_half(ins[t], c)).start()

        for t in range(nt):
            @pl.when(step == min(nsteps - 1, int(fractions[t] * nsteps)))
            def _():
                for s in range(3):
                    ici(t, s, chip_k[s], me).wait_recv()
                    passed(t, s, c, sibling).start()

        @pl.when(step == nsteps - 1)
        def _():
            for t in range(nt):
                for s, chip in enumerate(chips):
                    passed(t, s, 1 - c, me).wait_recv()
            for t in range(nt):
                for s, chip in enumerate(chips):
                    ici(t, s, k, (*chip, c), src=_half(ins[t], c)).wait_send()
                    passed(t, s, c, sibling).wait_send()
                local(t).wait()

    return _Comm(blocks, [jax.ShapeDtypeStruct((4,) + b.shape, b.dtype) for b in blocks],
                 [pltpu.SemaphoreType.DMA((nt, 6)), pltpu.SemaphoreType.DMA((nt, 6)), pltpu.SemaphoreType.DMA((nt,))], run)


def _scatter_plan(grads, owners):
    nt = len(grads)

    def run(step, nsteps, ins, outs, sems):
        send_sems, recv_sems, loc_sems = sems
        x, y, c = _place()
        me = 4 * x + 2 * y + c

        def target(r):
            tx = 1 - x if r & 2 else x
            ty = 1 - y if r & 1 else y
            return tx, ty

        def send(t, r):
            tx, ty = target(r)
            return _remote(ins[t].at[2 * tx + ty], outs[t].at[me], send_sems.at[t, r], recv_sems.at[t, 2 * r + c],
                           (tx, ty, owners[t]))

        def local(t):
            return pltpu.make_async_copy(ins[t].at[2 * x + y], outs[t].at[me], loc_sems.at[t])

        @pl.when(step == 0)
        def _():
            for t in range(nt):
                @pl.when(c == owners[t])
                def _():
                    local(t).start()

                @pl.when(c != owners[t])
                def _():
                    send(t, 0).start()

                for r in range(1, 4):
                    send(t, r).start()

        @pl.when(step == nsteps - 1)
        def _():
            for t in range(nt):
                @pl.when(c == owners[t])
                def _():
                    for r in range(4):
                        sx, sy = target(r)
                        for cs in range(2):
                            if r == 0 and cs == owners[t]:
                                continue
                            src_dev = 4 * sx + 2 * sy + cs
                            _remote(ins[t].at[0], outs[t].at[src_dev], send_sems.at[t, r], recv_sems.at[t, 2 * r + cs],
                                    (x, y, c)).wait_recv()
                    local(t).wait()

                @pl.when(c != owners[t])
                def _():
                    send(t, 0).wait_send()

                for r in range(1, 4):
                    send(t, r).wait_send()

    return _Comm(grads, [jax.ShapeDtypeStruct((8,) + g.shape[1:], g.dtype) for g in grads],
                 [pltpu.SemaphoreType.DMA((nt, 4)), pltpu.SemaphoreType.DMA((nt, 8)), pltpu.SemaphoreType.DMA((nt,))], run)


def _share_plan(reduced, owners):
    nt = len(reduced)

    def run(step, nsteps, ins, outs, sems):
        del ins
        send_sems, recv_sems = sems
        x, y, c = _place()

        def give(t, to):
            return _remote(outs[t], outs[t], send_sems.at[t], recv_sems.at[t], to)

        @pl.when(step == 0)
        def _():
            for t in range(nt):
                @pl.when(c == owners[t])
                def _():
                    give(t, (x, y, 1 - c)).start()

        @pl.when(step == nsteps - 1)
        def _():
            for t in range(nt):
                @pl.when(c == owners[t])
                def _():
                    give(t, (x, y, 1 - c)).wait_send()

                @pl.when(c != owners[t])
                def _():
                    give(t, (x, y, c)).wait_recv()

    return _Comm(reduced, [jax.ShapeDtypeStruct(r.shape, r.dtype) for r in reduced],
                 [pltpu.SemaphoreType.DMA((nt,)), pltpu.SemaphoreType.DMA((nt,))], run,
                 aliases={t: t for t in range(nt)})


def _join(a, b):
    ni, no, ns = len(a.inputs), len(a.out_shapes), len(a.sems)

    def run(step, nsteps, ins, outs, sems):
        a.run(step, nsteps, ins[:ni], outs[:no], sems[:ns])
        b.run(step, nsteps, ins[ni:], outs[no:], sems[ns:])

    aliases = dict(a.aliases)
    aliases.update({ni + i: no + o for i, o in b.aliases.items()})
    return _Comm(a.inputs + b.inputs, a.out_shapes + b.out_shapes, a.sems + b.sems, run, aliases)


def _peer(x, y, c, r):
    px = 1 - x if r & 4 else x
    py = 1 - y if r & 2 else y
    pc = 1 - c if r & 1 else c
    return (px, py, pc), 4 * px + 2 * py + pc


def _sum_slots(st, name):
    _, K, n = st.shape
    tr = next(t for t in (256, 128, 64, 32, 16) if K % t == 0)

    def kern(s_ref, o_ref):
        acc = s_ref[0].astype(_F32)
        for d in range(1, 8):
            acc = acc + s_ref[d].astype(_F32)
        o_ref[...] = acc.astype(o_ref.dtype)

    return pl.pallas_call(
        kern, name=name, grid=(K // tr,),
        in_specs=[pl.BlockSpec((8, tr, n), lambda i: (0, i, 0))], out_specs=_rows(tr, n),
        out_shape=jax.ShapeDtypeStruct((K, n), _ACT),
        compiler_params=_cparams("parallel"),
    )(st)


def _all_reduce_small(p):
    R = p.shape[0]

    def body(p_ref, o_ref, stage, send_sems, recv_sems):
        x, y, c = _place()
        me = 4 * x + 2 * y + c
        stage[me] = p_ref[...]
        sent = []
        for r in range(1, 8):
            to, _ = _peer(x, y, c, r)
            cp = _remote(p_ref, stage.at[me], send_sems.at[r - 1], recv_sems.at[r - 1], to)
            cp.start()
            sent.append(cp)
        for r in range(1, 8):
            _, src_dev = _peer(x, y, c, r)
            _remote(p_ref, stage.at[src_dev], send_sems.at[r - 1], recv_sems.at[r - 1], (x, y, c)).wait_recv()
        acc = stage[0]
        for d in range(1, 8):
            acc = acc + stage[d]
        o_ref[...] = acc
        for cp in sent:
            cp.wait_send()

    vm = pl.BlockSpec(memory_space=pltpu.VMEM)
    return pl.pallas_call(
        body, name="all_reduce_small",
        in_specs=[vm], out_specs=vm,
        out_shape=jax.ShapeDtypeStruct((R, _LANES), _F32),
        scratch_shapes=[pltpu.VMEM((8, R, _LANES), _F32), pltpu.SemaphoreType.DMA((7,)), pltpu.SemaphoreType.DMA((7,))],
    )(p)


def _adamw(w, g, m, v, name):
    shape = w.shape
    w2, g2, m2, v2 = (a.reshape(-1, shape[-1]) for a in (w, g, m, v))
    R, C = w2.shape
    tr = next((t for t in (256, 128, 64, 32, 16) if R % t == 0), R)

    def kern(w_ref, g_ref, m_ref, v_ref, gf_ref, d_ref, nm_ref, nv_ref):
        gv = g_ref[...].astype(_F32)
        nm = _B1 * m_ref[...] + (1.0 - _B1) * gv
        nv = _B2 * v_ref[...] + (1.0 - _B2) * (gv * gv)
        m_hat = nm / (1.0 - _B1 ** _STEP)
        v_hat = nv / (1.0 - _B2 ** _STEP)
        gf_ref[...] = gv
        d_ref[...] = -_LR * (m_hat / (jnp.sqrt(v_hat) + _EPS) + _WD * w_ref[...])
        nm_ref[...] = nm
        nv_ref[...] = nv

    outs = pl.pallas_call(
        kern, name=name, grid=(R // tr,),
        in_specs=[_rows(tr, C)] * 4, out_specs=[_rows(tr, C)] * 4,
        out_shape=[jax.ShapeDtypeStruct((R, C), _F32)] * 4,
        compiler_params=_cparams("parallel"),
    )(w2, g2, m2, v2)
    return tuple(o.reshape(shape) for o in outs)


def _pack_small(gss, sq):
    parts = [gss[l][n].reshape(-1) for n in _SMALL for l in range(len(gss))] + [jnp.sum(sq).reshape(1)]
    flat = jnp.concatenate(parts)
    rows = -(-flat.shape[0] // (8 * _LANES)) * 8
    return jnp.pad(flat, (0, rows * _LANES - flat.shape[0])).reshape(rows, _LANES)


def _unpack_small(total, shapes):
    flat = total.reshape(-1)
    out, off = {}, 0
    for n in _SMALL:
        layers = []
        for _ in range(shapes[n][0]):
            size = 1
            for s in shapes[n][1:]:
                size *= s
            layers.append(flat[off:off + size].reshape(shapes[n][1:]))
            off += size
        out[n] = jnp.stack(layers)
    return out, flat[off]


_GATHER = {
    "band_fwd_0": [(0, "w_proj_a"), (0, "w_proj_b"), (0, "w_out"), (0, "w_ffn_out")],
    "stick_fwd_0": [(0, "w_ffn_in"), (1, "w_in")],
    "ffn_fwd_0": [(1, "w_proj_a"), (1, "w_proj_b"), (1, "w_out"), (1, "w_ffn_in"), (1, "w_ffn_out")],
}
_SCATTER = {
    "band_bwd_0": [(1, "w_in"), (1, "w_proj_a"), (1, "w_proj_b"), (1, "w_out")],
    "stick_bwd_0": [(1, "w_ffn_in"), (1, "w_ffn_out"), (0, "w_ffn_in"), (0, "w_ffn_out"),
                    (0, "w_proj_a"), (0, "w_proj_b"), (0, "w_out")],
    "in_proj_bwd_0": [(0, "w_in")],
}
_SHARE = {"stick_bwd_0": "band_bwd_0", "grad_w_in_0": "stick_bwd_0"}


def _owner(key):
    return 1 if key[0] == 1 or key[1] == "w_in" else 0


def kernel(x, w_in, b_gate, rel_bias, w_proj_a, w_proj_b, w_out, ln1_g, ln1_b, w_ffn_in, w_ffn_out, ln2_g, ln2_b, loss_target, m_w_in, m_b_gate, m_rel_bias, m_w_proj_a, m_w_proj_b, m_w_out, m_ln1_g, m_ln1_b, m_w_ffn_in, m_w_ffn_out, m_ln2_g, m_ln2_b, v_w_in, v_b_gate, v_rel_bias, v_w_proj_a, v_w_proj_b, v_w_out, v_ln1_g, v_ln1_b, v_w_ffn_in, v_w_ffn_out, v_ln2_g, v_ln2_b):
    names = ("w_in", "b_gate", "rel_bias", "w_proj_a", "w_proj_b", "w_out", "ln1_g", "ln1_b", "w_ffn_in", "w_ffn_out", "ln2_g", "ln2_b")
    w = dict(zip(names, (w_in, b_gate, rel_bias, w_proj_a, w_proj_b, w_out, ln1_g, ln1_b, w_ffn_in, w_ffn_out, ln2_g, ln2_b)))
    m = dict(zip(names, (m_w_in, m_b_gate, m_rel_bias, m_w_proj_a, m_w_proj_b, m_w_out, m_ln1_g, m_ln1_b, m_w_ffn_in, m_w_ffn_out, m_ln2_g, m_ln2_b)))
    v = dict(zip(names, (v_w_in, v_b_gate, v_rel_bias, v_w_proj_a, v_w_proj_b, v_w_out, v_ln1_g, v_ln1_b, v_w_ffn_in, v_w_ffn_out, v_ln2_g, v_ln2_b)))
    T, D = x.shape[-2], x.shape[-1]
    assert w_in.shape[0] == 2, "the exchange schedule below is written for two layers"

    mine = [{n: w[n][l].astype(_MXU) for n in _DENSE} for l in range(2)]
    W = [dict(), dict()]
    gws = [dict(), dict()]
    slots, final = {}, {}

    def gather(keys):
        sizes = [mine[l][n].size for l, n in keys]
        passed, fractions = 0, []
        for s in sizes:
            passed += s
            fractions.append(0.15 + 0.6 * passed / sum(sizes))

        def done(outs):
            for (l, n), o in zip(keys, outs):
                W[l][n] = o
        return _gather_plan([mine[l][n] for l, n in keys], fractions), done

    def scatter(keys):
        comm = _scatter_plan([gws[l][n] for l, n in keys], [_owner(key) for key in keys])
        return comm, lambda outs: slots.update(zip(keys, outs))

    def share(keys):
        reduced = [_sum_slots(slots[key], f"sum_grad_{key[1]}_{key[0]}") for key in keys]
        comm = _share_plan(reduced, [_owner(key) for key in keys])
        return comm, lambda outs: final.update(zip(keys, outs))

    def both(first, second):
        (ca, da), (cb, db) = first, second
        na = len(ca.out_shapes)
        return _join(ca, cb), lambda outs: (da(outs[:na]), db(outs[na:]))

    comm, done = gather([(0, "w_in")])
    done(_comm_only(comm, "gather_first"))
    plans = {key: functools.partial(gather, keys) for key, keys in _GATHER.items()}
    for key, keys in _SCATTER.items():
        plans[key] = functools.partial(scatter, keys)
    for key, scattered_under in _SHARE.items():
        handed = functools.partial(share, _SCATTER[scattered_under])
        carried = plans.get(key)
        plans[key] = handed if carried is None else (lambda carried=carried, handed=handed: both(carried(), handed()))
    small = {n: w[n] for n in _SMALL}
    sq, dx, _, gss = _local_step(x.reshape(T, D), loss_target.reshape(T, D), W, small, _Plans(plans), gws)

    comm, done = share(_SCATTER["in_proj_bwd_0"])
    done(_comm_only(comm, "share_last"))
    grads = {n: jnp.stack([final[(l, n)] for l in range(2)]) for n in _DENSE}

    total = _all_reduce_small(_pack_small(gss, sq))
    small_grads, sq_all = _unpack_small(total, {n: w[n].shape for n in _SMALL})
    grads.update(small_grads)
    loss = 0.5 * sq_all / D

    grad, delta, new_m, new_v = {}, {}, {}, {}
    for n in names:
        grad[n], delta[n], new_m[n], new_v[n] = _adamw(w[n], grads[n].reshape(w[n].shape), m[n], v[n], f"adamw_{n}")
    return (loss, dx.reshape(x.shape), *[grad[n] for n in names], *[delta[n] for n in names],
            *[new_m[n] for n in names], *[new_v[n] for n in names])
```

```python
import functools

import jax
import jax.numpy as jnp
from jax import lax
from jax.experimental import pallas as pl
from jax.experimental.pallas import tpu as pltpu

_MXU = jnp.bfloat16
_ACT = jnp.bfloat16
_F32 = jnp.float32

_HEAD = 64
_CHUNK = 64
_LANES = 128
_TQ = 128
_BAND_TILES = 5
_BIAS_TILES = 9
_REL_CLIP = 256
_LN_EPS = 1e-5
_MASKED = -1e30
_EXP_ZERO_BELOW = -87.34
_SB_WINDOW = 2
_SB_SUBTILES = 2
_BAND_SUBTILES = 2
_VMEM_LIMIT = 56 * 1024 * 1024
_GRAD_ACC_BYTES = 12 * 1024 * 1024

_LR, _B1, _B2, _EPS, _WD, _STEP = 0.001, 0.9, 0.999, 1e-08, 0.01, 10

_MESH = pl.DeviceIdType.MESH


def _dot(a, b):
    return jnp.dot(a, b, preferred_element_type=_F32)


def _dot_nt(a, b):
    return lax.dot_general(a, b, (((1,), (1,)), ((), ())), preferred_element_type=_F32)


def _dot_tn(a, b):
    return lax.dot_general(a, b, (((0,), (0,)), ((), ())), preferred_element_type=_F32)


def _cparams(*sem):
    return pltpu.CompilerParams(dimension_semantics=sem, vmem_limit_bytes=_VMEM_LIMIT)


def _rows(t, c):
    return pl.BlockSpec((t, c), lambda i: (i, 0))


def _whole(shape):
    return pl.BlockSpec(shape, lambda i: tuple(0 for _ in shape))


_ANY = pl.BlockSpec(memory_space=pl.ANY)


def _load_cols(w_hbm, w_vmem, sem):
    n = w_hbm.shape[-1]
    cps = [pltpu.make_async_copy(w_hbm.at[k], w_vmem.at[:, pl.ds(k * n, n)], sem.at[k]) for k in range(4)]
    for cp in cps:
        cp.start()
    for cp in cps:
        cp.wait()


def _load_rows(w_hbm, w_vmem, sem):
    r = w_hbm.shape[-2]
    cps = [pltpu.make_async_copy(w_hbm.at[k], w_vmem.at[pl.ds(k * r, r), :], sem.at[k]) for k in range(4)]
    for cp in cps:
        cp.start()
    for cp in cps:
        cp.wait()


def _ln_stats(u):
    mu = jnp.mean(u, axis=-1, keepdims=True)
    xc = u - mu
    var = jnp.mean(xc * xc, axis=-1, keepdims=True)
    rstd = lax.rsqrt(var + _LN_EPS)
    return xc * rstd, rstd


def _ln_bwd(u, dy, gamma):
    xhat, rstd = _ln_stats(u)
    dxh = dy * gamma
    m1 = jnp.mean(dxh, axis=-1, keepdims=True)
    m2 = jnp.mean(dxh * xhat, axis=-1, keepdims=True)
    du = rstd * (dxh - m1 - xhat * m2)
    return du, jnp.sum(dy * xhat, axis=0, keepdims=True), jnp.sum(dy, axis=0, keepdims=True), xhat


def _divisor_tile(n, cap):
    best = None
    for t in range(_LANES, min(n, cap) + 1, _LANES):
        if n % t == 0:
            best = t
    return best or n


class _Comm:
    def __init__(self, inputs, out_shapes, sems, run, aliases=None):
        self.inputs, self.out_shapes, self.sems, self.run = list(inputs), list(out_shapes), list(sems), run
        self.aliases = aliases or {}


def _call(kern, comm, *, name, grid, in_specs, out_specs, out_shape, scratch_shapes, args, semantics):
    in_specs, out_specs, out_shape, scratch_shapes = list(in_specs), list(out_specs), list(out_shape), list(scratch_shapes)
    if comm is None:
        outs = pl.pallas_call(kern, name=name, grid=grid, in_specs=in_specs, out_specs=out_specs, out_shape=out_shape,
                              scratch_shapes=scratch_shapes, compiler_params=_cparams(*semantics))(*args)
        return list(outs), []
    n_in, n_out, n_scr = len(in_specs), len(out_specs), len(scratch_shapes)
    ci, co = len(comm.inputs), len(comm.out_shapes)
    nsteps = functools.reduce(lambda a, b: a * b, grid, 1)

    def fused(*refs):
        a, b = n_in, n_in + ci
        c, d = b + n_out, b + n_out + co
        e = d + n_scr
        step = pl.program_id(0)
        for ax in range(1, len(grid)):
            step = step * grid[ax] + pl.program_id(ax)
        comm.run(step, nsteps, refs[a:b], refs[c:d], refs[e:])
        kern(*refs[:a], *refs[b:c], *refs[d:e])

    outs = pl.pallas_call(
        fused, name=name, grid=grid, in_specs=in_specs + [_ANY] * ci, out_specs=out_specs + [_ANY] * co,
        out_shape=out_shape + comm.out_shapes, scratch_shapes=scratch_shapes + comm.sems,
        input_output_aliases={n_in + i: n_out + o for i, o in comm.aliases.items()},
        compiler_params=_cparams(*("arbitrary" for _ in grid)))(*args, *comm.inputs)
    return list(outs[:n_out]), list(outs[n_out:])


def _comm_only(comm, name):
    def body(*refs):
        ci, co = len(comm.inputs), len(comm.out_shapes)
        comm.run(0, 1, refs[:ci], refs[ci:ci + co], refs[ci + co:])

    outs = pl.pallas_call(body, name=name, in_specs=[_ANY] * len(comm.inputs), out_specs=[_ANY] * len(comm.out_shapes),
                          out_shape=comm.out_shapes, scratch_shapes=comm.sems,
                          input_output_aliases=dict(comm.aliases))(*comm.inputs)
    return list(outs)


def _in_proj(x, w_in, layer):
    T, D = x.shape
    N = 4 * w_in.shape[-1]
    NQ = N - 2 * D
    tm = 256

    def kern(x_ref, w_hbm, hq_ref, hg_ref, w_v, sem):
        @pl.when(pl.program_id(0) == 0)
        def _():
            _load_cols(w_hbm, w_v, sem)

        xb = x_ref[...].astype(_MXU)
        hq_ref[...] = _dot(xb, w_v[:, :NQ]).astype(hq_ref.dtype)
        hg_ref[...] = _dot(xb, w_v[:, NQ:])

    return pl.pallas_call(
        kern, name=f"in_proj_{layer}", grid=(T // tm,),
        in_specs=[_rows(tm, D), _ANY],
        out_specs=[_rows(tm, NQ), _rows(tm, 2 * D)],
        out_shape=[jax.ShapeDtypeStruct((T, NQ), _ACT), jax.ShapeDtypeStruct((T, 2 * D), _F32)],
        scratch_shapes=[pltpu.VMEM((D, N), w_in.dtype), pltpu.SemaphoreType.DMA((4,))],
        compiler_params=_cparams("arbitrary"),
    )(x, w_in)


def _mix_fwd(oa, ob, hg, x, wpa, wpb, wo, bg, gamma, beta, alpha, layer):
    T, D = x.shape
    WA, WB = oa.shape[1], ob.shape[1]
    tm = 256

    def kern(oa_ref, ob_ref, hg_ref, x_ref, bg_ref, g_ref, b_ref, wpa_h, wpb_h, wo_h,
             x1_ref, u1_ref, pre_ref, ya_ref, yb_ref, wpa_v, wpb_v, wo_v, sa, sb, so):
        @pl.when(pl.program_id(0) == 0)
        def _():
            _load_cols(wpa_h, wpa_v, sa)
            _load_cols(wpb_h, wpb_v, sb)
            _load_rows(wo_h, wo_v, so)

        ya = _dot(oa_ref[...].astype(_MXU), wpa_v[...])
        yb = _dot(ob_ref[...].astype(_MXU), wpb_v[...])
        hgv = hg_ref[...]
        bgv = bg_ref[...]
        ga = jax.nn.sigmoid(hgv[:, :D] + bgv[:, :D])
        gb = jax.nn.sigmoid(hgv[:, D:] + bgv[:, D:])
        pre = ga * ya + gb * yb
        mix = _dot(pre.astype(_MXU), wo_v[...])
        u = alpha * x_ref[...] + mix
        xhat, _ = _ln_stats(u)
        x1_ref[...] = xhat * g_ref[...] + b_ref[...]
        u1_ref[...] = u
        pre_ref[...] = pre.astype(pre_ref.dtype)
        ya_ref[...] = ya.astype(ya_ref.dtype)
        yb_ref[...] = yb.astype(yb_ref.dtype)

    return pl.pallas_call(
        kern, name=f"mix_fwd_{layer}", grid=(T // tm,),
        in_specs=[_rows(tm, WA), _rows(tm, WB), _rows(tm, 2 * D), _rows(tm, D),
                  _whole((1, 2 * D)), _whole((1, D)), _whole((1, D)), _ANY, _ANY, _ANY],
        out_specs=[_rows(tm, D)] * 5,
        out_shape=[jax.ShapeDtypeStruct((T, D), _F32), jax.ShapeDtypeStruct((T, D), _F32)]
        + [jax.ShapeDtypeStruct((T, D), _ACT)] * 3,
        scratch_shapes=[pltpu.VMEM((WA, D), wpa.dtype), pltpu.VMEM((WB, D), wpb.dtype), pltpu.VMEM((D, D), wo.dtype),
                        pltpu.SemaphoreType.DMA((4,)), pltpu.SemaphoreType.DMA((4,)), pltpu.SemaphoreType.DMA((4,))],
        compiler_params=_cparams("arbitrary"),
    )(oa, ob, hg, x, bg, gamma, beta, wpa, wpb, wo)


def _ffn_fwd(x1, wfi, wfo, gamma, beta, alpha, layer, comm=None):
    T, D = x1.shape
    F2 = 4 * wfi.shape[-1]
    F = F2 // 2
    tm = 256
    fc = F // 2

    def kern(x_ref, g_ref, b_ref, wi_h, wo_h, x2_ref, u2_ref, act_ref, gu_ref, wi_v, wo_v, si, so):
        @pl.when(pl.program_id(0) == 0)
        def _():
            _load_cols(wi_h, wi_v, si)
            _load_rows(wo_h, wo_v, so)

        x = x_ref[...]
        xb = x.astype(_MXU)
        ffn = jnp.zeros((tm, D), _F32)
        for c in range(2):
            g = _dot(xb, wi_v[:, c * fc:(c + 1) * fc])
            u = _dot(xb, wi_v[:, F + c * fc:F + (c + 1) * fc])
            act = g * jax.nn.sigmoid(g) * u
            ab = act.astype(_MXU)
            ffn = ffn + _dot(ab, wo_v[c * fc:(c + 1) * fc, :])
            act_ref[:, c * fc:(c + 1) * fc] = ab.astype(act_ref.dtype)
            gu_ref[:, c * fc:(c + 1) * fc] = g.astype(gu_ref.dtype)
            gu_ref[:, F + c * fc:F + (c + 1) * fc] = u.astype(gu_ref.dtype)
        uu = alpha * x + ffn
        xhat, _ = _ln_stats(uu)
        x2_ref[...] = xhat * g_ref[...] + b_ref[...]
        u2_ref[...] = uu

    return _call(
        kern, comm, name=f"ffn_fwd_{layer}", grid=(T // tm,),
        in_specs=[_rows(tm, D), _whole((1, D)), _whole((1, D)), _ANY, _ANY],
        out_specs=[_rows(tm, D), _rows(tm, D), _rows(tm, F), _rows(tm, F2)],
        out_shape=[jax.ShapeDtypeStruct((T, D), _F32), jax.ShapeDtypeStruct((T, D), _F32),
                   jax.ShapeDtypeStruct((T, F), _ACT), jax.ShapeDtypeStruct((T, F2), _ACT)],
        scratch_shapes=[pltpu.VMEM((D, F2), wfi.dtype), pltpu.VMEM((F, D), wfo.dtype),
                        pltpu.SemaphoreType.DMA((4,)), pltpu.SemaphoreType.DMA((4,))],
        args=(x1, gamma, beta, wfi, wfo), semantics=("arbitrary",))


def _ffn_bwd_a(u2, dy_or_target, gu, gamma, beta, wfo, layer, last):
    T, D = u2.shape
    F2 = gu.shape[1]
    F = F2 // 2
    tm = 256
    fc = F // 2

    def kern(u_ref, dy_ref, gu_ref, g_ref, b_ref, wo_h, du_ref, dub_ref, dgu_ref, st_ref, wo_v, so):
        @pl.when(pl.program_id(0) == 0)
        def _():
            _load_rows(wo_h, wo_v, so)
            st_ref[...] = jnp.zeros_like(st_ref)

        gam = g_ref[...]
        u = u_ref[...]
        if last:
            xhat0, _ = _ln_stats(u)
            err = xhat0 * gam + b_ref[...] - dy_ref[...]
            dy = err * (1.0 / D)
            st_ref[2:3, :] += jnp.sum(err * err, axis=0, keepdims=True)
        else:
            dy = dy_ref[...]
        du, dgam, dbet, _ = _ln_bwd(u, dy, gam)
        st_ref[0:1, :] += dgam
        st_ref[1:2, :] += dbet
        du_ref[...] = du
        dub = du.astype(_MXU)
        dub_ref[...] = dub.astype(dub_ref.dtype)
        for c in range(2):
            dact = _dot_nt(dub, wo_v[c * fc:(c + 1) * fc, :])
            g = gu_ref[:, c * fc:(c + 1) * fc].astype(_F32)
            uu = gu_ref[:, F + c * fc:F + (c + 1) * fc].astype(_F32)
            sg = jax.nn.sigmoid(g)
            dgu_ref[:, c * fc:(c + 1) * fc] = (dact * uu * (sg * (1.0 + g * (1.0 - sg)))).astype(dgu_ref.dtype)
            dgu_ref[:, F + c * fc:F + (c + 1) * fc] = (dact * (g * sg)).astype(dgu_ref.dtype)

    return pl.pallas_call(
        kern, name=f"ffn_bwd_a_{layer}", grid=(T // tm,),
        in_specs=[_rows(tm, D), _rows(tm, D), _rows(tm, F2), _whole((1, D)), _whole((1, D)), _ANY],
        out_specs=[_rows(tm, D), _rows(tm, D), _rows(tm, F2), _whole((8, D))],
        out_shape=[jax.ShapeDtypeStruct((T, D), _F32), jax.ShapeDtypeStruct((T, D), _ACT),
                   jax.ShapeDtypeStruct((T, F2), _ACT), jax.ShapeDtypeStruct((8, D), _F32)],
        scratch_shapes=[pltpu.VMEM((F, D), wfo.dtype), pltpu.SemaphoreType.DMA((4,))],
        compiler_params=_cparams("arbitrary"),
    )(u2, dy_or_target, gu, gamma, beta, wfo)


def _residual_nt(res, res_scale, d, w, name, comm=None):
    T, K = res.shape
    N = d.shape[1]
    tm = 256

    def kern(r_ref, d_ref, w_hbm, o_ref, w_v, sem):
        @pl.when(pl.program_id(0) == 0)
        def _():
            _load_cols(w_hbm, w_v, sem)

        o_ref[...] = res_scale * r_ref[...] + _dot_nt(d_ref[...].astype(_MXU), w_v[...])

    outs, extra = _call(
        kern, comm, name=name, grid=(T // tm,),
        in_specs=[_rows(tm, K), _rows(tm, N), _ANY], out_specs=[_rows(tm, K)],
        out_shape=[jax.ShapeDtypeStruct((T, K), _F32)],
        scratch_shapes=[pltpu.VMEM((K, N), w.dtype), pltpu.SemaphoreType.DMA((4,))],
        args=(res, d, w), semantics=("arbitrary",))
    return outs[0], extra


def _mix_bwd(u1, dx1, ya, yb, hg, wpa, wpb, wo, bg, gamma, alpha, layer):
    del alpha
    T, D = u1.shape
    WA, WB = wpa.shape[-2], wpb.shape[-2]
    tm = 256

    def kern(u_ref, dx_ref, ya_ref, yb_ref, hg_ref, bg_ref, g_ref, wpa_h, wpb_h, wo_h,
             du_ref, dub_ref, dya_ref, dyb_ref, dhg_ref, doa_ref, dob_ref, st_ref,
             wpa_v, wpb_v, wo_v, sa, sb, so):
        @pl.when(pl.program_id(0) == 0)
        def _():
            _load_cols(wpa_h, wpa_v, sa)
            _load_cols(wpb_h, wpb_v, sb)
            _load_rows(wo_h, wo_v, so)
            st_ref[...] = jnp.zeros_like(st_ref)

        du, dgam, dbet, _ = _ln_bwd(u_ref[...], dx_ref[...], g_ref[...])
        st_ref[1:2, :D] += dgam
        st_ref[1:2, D:] += dbet
        du_ref[...] = du
        dub = du.astype(_MXU)
        dub_ref[...] = dub.astype(dub_ref.dtype)
        dpre = _dot_nt(dub, wo_v[...])
        hgv = hg_ref[...]
        bgv = bg_ref[...]
        ga = jax.nn.sigmoid(hgv[:, :D] + bgv[:, :D])
        gb = jax.nn.sigmoid(hgv[:, D:] + bgv[:, D:])
        dya = (dpre * ga).astype(_MXU)
        dyb = (dpre * gb).astype(_MXU)
        dsa = dpre * ya_ref[...].astype(_F32) * (ga * (1.0 - ga))
        dsb = dpre * yb_ref[...].astype(_F32) * (gb * (1.0 - gb))
        st_ref[0:1, :D] += jnp.sum(dsa, axis=0, keepdims=True)
        st_ref[0:1, D:] += jnp.sum(dsb, axis=0, keepdims=True)
        dya_ref[...] = dya.astype(dya_ref.dtype)
        dyb_ref[...] = dyb.astype(dyb_ref.dtype)
        dhg_ref[:, :D] = dsa.astype(dhg_ref.dtype)
        dhg_ref[:, D:] = dsb.astype(dhg_ref.dtype)
        doa_ref[...] = _dot_nt(dya, wpa_v[...]).astype(doa_ref.dtype)
        dob_ref[...] = _dot_nt(dyb, wpb_v[...]).astype(dob_ref.dtype)

    return pl.pallas_call(
        kern, name=f"mix_bwd_{layer}", grid=(T // tm,),
        in_specs=[_rows(tm, D)] * 4 + [_rows(tm, 2 * D), _whole((1, 2 * D)), _whole((1, D)), _ANY, _ANY, _ANY],
        out_specs=[_rows(tm, D)] * 4 + [_rows(tm, 2 * D), _rows(tm, WA), _rows(tm, WB), _whole((8, 2 * D))],
        out_shape=[jax.ShapeDtypeStruct((T, D), _F32)] + [jax.ShapeDtypeStruct((T, D), _ACT)] * 3
        + [jax.ShapeDtypeStruct((T, 2 * D), _ACT), jax.ShapeDtypeStruct((T, WA), _ACT),
           jax.ShapeDtypeStruct((T, WB), _ACT), jax.ShapeDtypeStruct((8, 2 * D), _F32)],
        scratch_shapes=[pltpu.VMEM((WA, D), wpa.dtype), pltpu.VMEM((WB, D), wpb.dtype), pltpu.VMEM((D, D), wo.dtype),
                        pltpu.SemaphoreType.DMA((4,)), pltpu.SemaphoreType.DMA((4,)), pltpu.SemaphoreType.DMA((4,))],
        compiler_params=_cparams("arbitrary"),
    )(u1, dx1, ya, yb, hg, bg, gamma, wpa, wpb, wo)


def _grad_w(a, b, *, col_shards, name, comm=None):
    T, M = a.shape
    N = b.shape[1]
    tk = 512
    n = N // 4 if col_shards else N
    whole = M * N * 4 <= _GRAD_ACC_BYTES
    tn = N if whole else (n if col_shards else _divisor_tile(N, _GRAD_ACC_BYTES // (4 * M)))
    nk = T // tk

    def kern(a_ref, b_ref, o_ref, acc):
        k = pl.program_id(1)

        @pl.when(k == 0)
        def _():
            acc[...] = jnp.zeros_like(acc)

        acc[...] += _dot_tn(a_ref[...].astype(_MXU), b_ref[...].astype(_MXU))

        @pl.when(k == nk - 1)
        def _():
            if col_shards and whole:
                for s in range(4):
                    o_ref[s] = acc[:, s * n:(s + 1) * n].astype(o_ref.dtype)
            else:
                o_ref[...] = acc[...].astype(o_ref.dtype)

    if col_shards:
        out_spec = (pl.BlockSpec((4, M, n), lambda j, k: (0, 0, 0)) if whole
                    else pl.BlockSpec((None, M, n), lambda j, k: (j, 0, 0)))
        out_shape = jax.ShapeDtypeStruct((4, M, n), _ACT)
    else:
        out_spec = pl.BlockSpec((M, tn), lambda j, k: (0, j))
        out_shape = jax.ShapeDtypeStruct((M, N), _ACT)
    outs, extra = _call(
        kern, comm, name=name, grid=(N // tn, nk),
        in_specs=[pl.BlockSpec((tk, M), lambda j, k: (k, 0)), pl.BlockSpec((tk, tn), lambda j, k: (k, j))],
        out_specs=[out_spec], out_shape=[out_shape], scratch_shapes=[pltpu.VMEM((M, tn), _F32)],
        args=(a, b), semantics=("parallel", "arbitrary"))
    return outs[0], extra


def _bias_tiles(rel):
    H = rel.shape[0]
    span = _TQ * _BAND_TILES - 1
    edge = span - _REL_CLIP
    gvec = jnp.concatenate([jnp.broadcast_to(rel[:, :1], (H, edge)), rel, jnp.broadcast_to(rel[:, -1:], (H, edge))], axis=1)
    width = _BIAS_TILES * _TQ
    period = width + _TQ
    tiled = jnp.broadcast_to(jnp.pad(gvec[:, ::-1], ((0, 0), (0, 1)))[:, None, :], (H, _TQ, period))
    rows = tiled.reshape(H, _TQ * period)[:, :_TQ * (period - 1)].reshape(H, _TQ, period - 1)[:, :, _TQ - 1:]
    r = jnp.arange(_TQ)[:, None]
    u = jnp.arange(width)[None, :]
    d = 4 * _TQ + r - u
    rm = r % _CHUNK
    valid = (d >= rm - (_CHUNK - 1)) & (d <= rm + 8 * _CHUNK)
    tiles = jnp.where(valid[None], rows, _MASKED)
    return tiles.reshape(H // 2, 2 * _TQ, _BIAS_TILES, _TQ).transpose(0, 2, 1, 3)


def _fold_bias_grad(db):
    H = 2 * db.shape[0]
    width = _BIAS_TILES * _TQ
    period = width + _TQ
    x = jnp.pad(db.transpose(0, 2, 1, 3).reshape(H, _TQ, width), ((0, 0), (0, 0), (_TQ - 1, 0)))
    skew = jnp.pad(x.reshape(H, _TQ * (period - 1)), ((0, 0), (0, _TQ))).reshape(H, _TQ, period)
    dg = skew.sum(axis=1)[:, :period - 1][:, ::-1]
    span = _TQ * _BAND_TILES - 1
    edge = span - _REL_CLIP
    mid = dg[:, edge:edge + 2 * _REL_CLIP + 1]
    lo = dg[:, :edge].sum(axis=1)
    hi = dg[:, edge + 2 * _REL_CLIP + 1:].sum(axis=1)
    return mid.at[:, 0].add(lo).at[:, -1].add(hi)


def _band_window(i):
    j0 = jnp.maximum(i - (_BAND_TILES - 1), 0)
    return j0, (_BAND_TILES - 1) - (i - j0)


def _head_masks():
    lane = lax.broadcasted_iota(jnp.int32, (1, _LANES), 1)
    return [(lane // _HEAD) == hh for hh in range(2)]


def _stack_heads(x, masks):
    return jnp.concatenate([jnp.where(m, x, jnp.zeros_like(x)) for m in masks], axis=0)


def _unstack_heads(y, masks):
    return jnp.where(masks[0], y[:_TQ], y[_TQ:])


def _scaled(q):
    return q * jnp.asarray(_HEAD ** -0.5, q.dtype)


def _band_probs(q2, k_ref, b_ref, j0, boff):
    s = []
    for j in range(_BAND_TILES):
        kj = k_ref[pl.ds(pl.multiple_of((j0 + j) * _TQ, _TQ), _TQ), :]
        s.append(_dot_nt(q2, kj) + b_ref[boff + j])
    m = jnp.max(functools.reduce(jnp.maximum, s), axis=-1, keepdims=True)
    p = [jnp.exp(x - m) for x in s]
    l = jnp.sum(functools.reduce(lambda a, b: a + b, p), axis=-1, keepdims=True)
    return p, 1.0 / l


def _qkv_specs(T, cb, npair, tq=_TQ):
    return [pl.BlockSpec((tq, _LANES), lambda h, i: (i, cb + h)),
            pl.BlockSpec((T, _LANES), lambda h, i: (0, cb + npair + h)),
            pl.BlockSpec((T, _LANES), lambda h, i: (0, cb + 2 * npair + h))]


def _attn_a_fwd(hq, bias, col0, width, layer, comm=None):
    T = hq.shape[0]
    npair = width // _LANES
    nsub = _BAND_SUBTILES
    tq = nsub * _TQ

    def kern(q_ref, k_ref, v_ref, b_ref, o_ref):
        masks = _head_masks()
        q = _scaled(q_ref[...])
        for s in range(nsub):
            part = slice(s * _TQ, (s + 1) * _TQ)
            j0, boff = _band_window(nsub * pl.program_id(1) + s)
            p, inv = _band_probs(_stack_heads(q[part], masks), k_ref, b_ref, j0, boff)
            o = jnp.zeros((2 * _TQ, _LANES), _F32)
            for j in range(_BAND_TILES):
                vj = v_ref[pl.ds(pl.multiple_of((j0 + j) * _TQ, _TQ), _TQ), :]
                o = o + _dot(p[j].astype(_MXU), vj)
            o_ref[part, :] = _unstack_heads(o * inv, masks).astype(o_ref.dtype)

    outs, extra = _call(
        kern, comm, name=f"band_attn_fwd_{layer}", grid=(npair, T // tq),
        in_specs=_qkv_specs(T, col0 // _LANES, npair, tq)
        + [pl.BlockSpec((None, _BIAS_TILES, 2 * _TQ, _TQ), lambda h, i: (h, 0, 0, 0))],
        out_specs=[pl.BlockSpec((tq, _LANES), lambda h, i: (i, h))],
        out_shape=[jax.ShapeDtypeStruct((T, width), _ACT)], scratch_shapes=[],
        args=(hq, hq, hq, bias), semantics=("arbitrary", "arbitrary"))
    return outs[0], extra


def _attn_a_bwd(hq, bias, do, col0, width, layer, comm=None):
    T = hq.shape[0]
    npair = width // _LANES
    nsub = _BAND_SUBTILES
    tq = nsub * _TQ
    nq = T // tq
    scale = _HEAD ** -0.5

    def kern(q_ref, k_ref, v_ref, b_ref, do_ref, dq_ref, dk_ref, dv_ref, db_ref, dk_acc, dv_acc):
        i = pl.program_id(1)

        @pl.when(i == 0)
        def _():
            dk_acc[...] = jnp.zeros_like(dk_acc)
            dv_acc[...] = jnp.zeros_like(dv_acc)
            db_ref[...] = jnp.zeros_like(db_ref)

        masks = _head_masks()
        q = _scaled(q_ref[...])
        do_t = do_ref[...]
        for s in range(nsub):
            part = slice(s * _TQ, (s + 1) * _TQ)
            j0, boff = _band_window(nsub * i + s)
            q2 = _stack_heads(q[part], masks)
            do2 = _stack_heads(do_t[part], masks).astype(_MXU)
            p, inv = _band_probs(q2, k_ref, b_ref, j0, boff)
            rows = [pl.ds(pl.multiple_of((j0 + j) * _TQ, _TQ), _TQ) for j in range(_BAND_TILES)]
            p = [x * inv for x in p]
            dp = [_dot_nt(do2, v_ref[rows[j], :]) for j in range(_BAND_TILES)]
            delta = jnp.sum(functools.reduce(lambda a, b: a + b, [p[j] * dp[j] for j in range(_BAND_TILES)]),
                            axis=-1, keepdims=True)
            dq = jnp.zeros((2 * _TQ, _LANES), _F32)
            for j in range(_BAND_TILES):
                ds = p[j] * (dp[j] - delta)
                db_ref[boff + j] += ds
                dsb = ds.astype(_MXU)
                dq = dq + _dot(dsb, k_ref[rows[j], :])
                dk_acc[rows[j], :] += _dot_tn(dsb, q2)
                dv_acc[rows[j], :] += _dot_tn(p[j].astype(_MXU), do2)
            dq_ref[part, :] = (_unstack_heads(dq, masks) * scale).astype(dq_ref.dtype)

        @pl.when(i == nq - 1)
        def _():
            dk_ref[...] = dk_acc[...].astype(dk_ref.dtype)
            dv_ref[...] = dv_acc[...].astype(dv_ref.dtype)

    strip = pl.BlockSpec((None, _BIAS_TILES, 2 * _TQ, _TQ), lambda h, i: (h, 0, 0, 0))
    tile = pl.BlockSpec((tq, _LANES), lambda h, i: (i, h))
    column = pl.BlockSpec((T, _LANES), lambda h, i: (0, h))
    outs, extra = _call(
        kern, comm, name=f"band_attn_bwd_{layer}", grid=(npair, nq),
        in_specs=_qkv_specs(T, col0 // _LANES, npair, tq) + [strip, tile],
        out_specs=[tile, column, column, strip],
        out_shape=[jax.ShapeDtypeStruct((T, width), _ACT)] * 3
        + [jax.ShapeDtypeStruct((npair, _BIAS_TILES, 2 * _TQ, _TQ), _F32)],
        scratch_shapes=[pltpu.VMEM((T, _LANES), _F32), pltpu.VMEM((T, _LANES), _F32)],
        args=(hq, hq, hq, bias, do), semantics=("arbitrary", "arbitrary"))
    return outs, extra


def _suffix_matrix():
    r = lax.broadcasted_iota(jnp.int32, (_TQ, _TQ), 0)
    c = lax.broadcasted_iota(jnp.int32, (_TQ, _TQ), 1)
    r2 = lax.broadcasted_iota(jnp.int32, (2 * _TQ, _TQ), 0)
    c2 = lax.broadcasted_iota(jnp.int32, (2 * _TQ, _TQ), 1)
    return (r > c).astype(_MXU), c2 - (r2 & (_TQ - 1))


def _suffix_sums(xs, tri):
    n, k = xs[0].shape[0], len(xs)
    his = [x.astype(_MXU) for x in xs]
    los = [(x - h.astype(_F32)).astype(_MXU) for x, h in zip(xs, his)]
    y = _dot(jnp.concatenate(his + los, axis=0), tri)
    return [y[j * n:(j + 1) * n] + y[(k + j) * n:(k + j + 1) * n] for j in range(k)]


def _stick_tiles(tiles, rel, carry_l, tri):
    zs = [_dot_nt(qs, kj) for qs, kj, _, _ in tiles]
    Ls, masks = [], []
    for z, (_, _, jj, _) in zip(zs, tiles):
        nsp = -(jnp.maximum(z, 0.0) + jnp.log(1.0 + jnp.exp(-jnp.abs(z))))
        if isinstance(jj, int):
            mask = (rel < 0) if jj == 0 else None
        else:
            mask = rel < jnp.where(jj == 0, 0, _TQ)
        Ls.append(nsp if mask is None else jnp.where(mask, nsp, 0.0))
        masks.append(mask)
    carry_l = list(carry_l)
    ws = []
    for z, L, suffix, mask, (_, _, _, sub) in zip(zs, Ls, _suffix_sums(Ls, tri), masks, tiles):
        w = jnp.exp(z + L + suffix + carry_l[sub])
        ws.append(w if mask is None else jnp.where(mask, w, 0.0))
        carry_l[sub] = carry_l[sub] + jnp.sum(L, axis=-1, keepdims=True)
    return zs, Ls, ws, masks, carry_l


def _sweep(i, step, zero):
    nsub = _SB_SUBTILES

    def window():
        tiles = [(s, jj) for jj in range(_SB_WINDOW) for s in range(nsub)]
        return tuple((jnp.int32(_SB_WINDOW),) + c for c in step(tiles, [zero] * nsub))

    start = lax.cond(i >= -(-(_SB_WINDOW - 1) // nsub), window, lambda: tuple((jnp.int32(0),) + zero for _ in range(nsub)))
    outs = []
    for s in range(nsub):
        def done(c, s=s):
            return jnp.logical_or(c[0] > nsub * i + s, jnp.max(c[1]) < _EXP_ZERO_BELOW)

        def more(c, s=s):
            carries = [None] * nsub
            carries[s] = c[1:]
            return (c[0] + 1,) + step([(s, c[0])], carries)[s]

        outs.append(lax.while_loop(lambda c, done=done: jnp.logical_not(done(c)), more, start[s]))
    return outs


def _sb_fwd(hq, col0, width, layer, comm=None):
    T = hq.shape[0]
    npair = width // _LANES
    nsub = _SB_SUBTILES
    tq = nsub * _TQ

    def kern(q_ref, k_ref, v_ref, o_ref):
        i = pl.program_id(1)
        masks = _head_masks()
        tri, rel = _suffix_matrix()
        q = _scaled(q_ref[...])
        q2 = [_stack_heads(q[s * _TQ:(s + 1) * _TQ], masks) for s in range(nsub)]

        def step(tiles, carries):
            rows = [pl.ds(pl.multiple_of((nsub * i + s - jj) * _TQ, _TQ), _TQ) for s, jj in tiles]
            cls = [None if c is None else c[0] for c in carries]
            accs = [None if c is None else c[1] for c in carries]
            _, _, ws, _, cls = _stick_tiles([(q2[s], k_ref[r, :], jj, s) for (s, jj), r in zip(tiles, rows)], rel, cls, tri)
            for w, r, (s, _) in zip(ws, rows, tiles):
                accs[s] = accs[s] + _dot(w.astype(_MXU), v_ref[r, :])
            return [None if c is None else (cls[s], accs[s]) for s, c in enumerate(carries)]

        outs = _sweep(i, step, (jnp.zeros((2 * _TQ, 1), _F32), jnp.zeros((2 * _TQ, _LANES), _F32)))
        for s in range(nsub):
            o_ref[s * _TQ:(s + 1) * _TQ, :] = _unstack_heads(outs[s][2], masks)

    outs, extra = _call(
        kern, comm, name=f"stick_attn_fwd_{layer}", grid=(npair, T // tq),
        in_specs=_qkv_specs(T, col0 // _LANES, npair, tq),
        out_specs=[pl.BlockSpec((tq, _LANES), lambda h, i: (i, h))],
        out_shape=[jax.ShapeDtypeStruct((T, width), _F32)], scratch_shapes=[],
        args=(hq, hq, hq), semantics=("arbitrary", "arbitrary"))
    return outs[0], extra


def _sb_bwd(hq, o, do, col0, width, layer, comm=None):
    T = hq.shape[0]
    npair = width // _LANES
    nsub = _SB_SUBTILES
    tq = nsub * _TQ
    nq = T // tq
    scale = _HEAD ** -0.5

    def kern(q_ref, k_ref, v_ref, o_ref, do_ref, dq_ref, dk_ref, dv_ref, dk_acc, dv_acc):
        i = pl.program_id(1)

        @pl.when(i == 0)
        def _():
            dk_acc[...] = jnp.zeros_like(dk_acc)
            dv_acc[...] = jnp.zeros_like(dv_acc)

        masks = _head_masks()
        tri, rel = _suffix_matrix()
        q = _scaled(q_ref[...])
        do_t = do_ref[...]
        prod = do_t.astype(_F32) * o_ref[...]
        part = [slice(s * _TQ, (s + 1) * _TQ) for s in range(nsub)]
        q2 = [_stack_heads(q[p], masks) for p in part]
        do2 = [_stack_heads(do_t[p], masks).astype(_MXU) for p in part]
        dsum = [jnp.sum(_stack_heads(prod[p], masks), axis=-1, keepdims=True) for p in part]

        def step(tiles, carries):
            rows = [pl.ds(pl.multiple_of((nsub * i + s - jj) * _TQ, _TQ), _TQ) for s, jj in tiles]
            kjs = [k_ref[r, :] for r in rows]
            cls, cgs, dqs = ([None if c is None else c[n] for c in carries] for n in range(3))
            zs, Ls, ws, tile_masks, cls = _stick_tiles([(q2[s], kj, jj, s) for (s, jj), kj in zip(tiles, kjs)], rel, cls, tri)
            wbs = [w.astype(_MXU) for w in ws]
            gs = [wb.astype(_F32) * _dot_nt(do2[s], v_ref[r, :]) for wb, r, (s, _) in zip(wbs, rows, tiles)]
            for z, L, g, later, mask, wb, kj, r, (s, _) in zip(zs, Ls, gs, _suffix_sums(gs, tri), tile_masks, wbs, kjs,
                                                               rows, tiles):
                dz = g - jnp.exp(z + L) * (dsum[s] - (later + cgs[s]))
                if mask is not None:
                    dz = jnp.where(mask, dz, 0.0)
                dzb = dz.astype(_MXU)
                dk_acc[r, :] += _dot_tn(dzb, q2[s])
                dv_acc[r, :] += _dot_tn(wb, do2[s])
                dqs[s] = dqs[s] + _dot(dzb, kj)
                cgs[s] = cgs[s] + jnp.sum(g, axis=-1, keepdims=True)
            return [None if c is None else (cls[s], cgs[s], dqs[s]) for s, c in enumerate(carries)]

        zc = jnp.zeros((2 * _TQ, 1), _F32)
        outs = _sweep(i, step, (zc, zc, jnp.zeros((2 * _TQ, _LANES), _F32)))
        for s in range(nsub):
            dq_ref[part[s], :] = (_unstack_heads(outs[s][3], masks) * scale).astype(dq_ref.dtype)

        @pl.when(i == nq - 1)
        def _():
            dk_ref[...] = dk_acc[...].astype(dk_ref.dtype)
            dv_ref[...] = dv_acc[...].astype(dv_ref.dtype)

    tile_spec = pl.BlockSpec((tq, _LANES), lambda h, i: (i, h))
    column = pl.BlockSpec((T, _LANES), lambda h, i: (0, h))
    outs, extra = _call(
        kern, comm, name=f"stick_attn_bwd_{layer}", grid=(npair, nq),
        in_specs=_qkv_specs(T, col0 // _LANES, npair, tq) + [tile_spec, tile_spec],
        out_specs=[tile_spec, column, column],
        out_shape=[jax.ShapeDtypeStruct((T, width), _ACT)] * 3,
        scratch_shapes=[pltpu.VMEM((T, _LANES), _F32), pltpu.VMEM((T, _LANES), _F32)],
        args=(hq, hq, hq, o, do), semantics=("arbitrary", "arbitrary"))
    return outs, extra


_DENSE = ("w_in", "w_proj_a", "w_proj_b", "w_out", "w_ffn_in", "w_ffn_out")
_COL_SHARDED = {"w_in": True, "w_proj_a": True, "w_proj_b": True, "w_out": False, "w_ffn_in": True, "w_ffn_out": False}
_SMALL = ("b_gate", "rel_bias", "ln1_g", "ln1_b", "ln2_g", "ln2_b")


class _Plans:
    def __init__(self, plans=None):
        self.plans = plans or {}

    def start(self, key):
        if key not in self.plans:
            return None, None
        return self.plans[key]()

    @staticmethod
    def finish(done, extra):
        if done is not None:
            done(extra)


def _layer_fwd(x, W, small, l, alpha, plans):
    WA = small["rel_bias"].shape[1] * _HEAD
    row = lambda v: v[l].reshape(1, -1)
    hq, hg = _in_proj(x, W["w_in"], l)
    WB = (hq.shape[1] - 3 * WA) // 3
    bias = _bias_tiles(small["rel_bias"][l])
    comm, done = plans.start(f"band_fwd_{l}")
    oa, extra = _attn_a_fwd(hq, bias, 0, WA, l, comm)
    plans.finish(done, extra)
    comm, done = plans.start(f"stick_fwd_{l}")
    ob, extra = _sb_fwd(hq, 3 * WA, WB, l, comm)
    plans.finish(done, extra)
    x1, u1, pre, ya, yb = _mix_fwd(oa, ob, hg, x, W["w_proj_a"], W["w_proj_b"], W["w_out"],
                                   row(small["b_gate"]), row(small["ln1_g"]), row(small["ln1_b"]), alpha, l)
    comm, done = plans.start(f"ffn_fwd_{l}")
    (x2, u2, act, gu), extra = _ffn_fwd(x1, W["w_ffn_in"], W["w_ffn_out"], row(small["ln2_g"]), row(small["ln2_b"]),
                                        alpha, l, comm)
    plans.finish(done, extra)
    return x2, dict(x=x, hq=hq, hg=hg, bias=bias, oa=oa, ob=ob, x1=x1, u1=u1, pre=pre, ya=ya, yb=yb, u2=u2, act=act, gu=gu)


def _layer_bwd(dy_or_target, S, W, small, l, last, alpha, plans, gw):
    D = S["x"].shape[1]
    WA, WB = S["oa"].shape[1], S["ob"].shape[1]
    row = lambda v: v[l].reshape(1, -1)

    def blocks(g, n):
        return g if _COL_SHARDED[n] else g.reshape(4, g.shape[0] // 4, g.shape[1])

    du2, du2b, dgu, st2 = _ffn_bwd_a(S["u2"], dy_or_target, S["gu"], row(small["ln2_g"]), row(small["ln2_b"]),
                                     W["w_ffn_out"], l, last)
    dx1, _ = _residual_nt(du2, alpha, dgu, W["w_ffn_in"], f"ffn_bwd_b_{l}")
    gw["w_ffn_in"] = blocks(_grad_w(S["x1"], dgu, col_shards=True, name=f"grad_w_ffn_in_{l}")[0], "w_ffn_in")
    gw["w_ffn_out"] = blocks(_grad_w(S["act"], du2b, col_shards=False, name=f"grad_w_ffn_out_{l}")[0], "w_ffn_out")
    du1, du1b, dya, dyb, dhg, doa, dob, st1 = _mix_bwd(S["u1"], dx1, S["ya"], S["yb"], S["hg"], W["w_proj_a"],
                                                       W["w_proj_b"], W["w_out"], row(small["b_gate"]),
                                                       row(small["ln1_g"]), alpha, l)
    gw["w_out"] = blocks(_grad_w(S["pre"], du1b, col_shards=False, name=f"grad_w_out_{l}")[0], "w_out")
    gw["w_proj_a"] = blocks(_grad_w(S["oa"], dya, col_shards=True, name=f"grad_w_proj_a_{l}")[0], "w_proj_a")
    gw["w_proj_b"] = blocks(_grad_w(S["ob"], dyb, col_shards=True, name=f"grad_w_proj_b_{l}")[0], "w_proj_b")
    comm, done = plans.start(f"band_bwd_{l}")
    (dqa, dka, dva, dbias), extra = _attn_a_bwd(S["hq"], S["bias"], doa, 0, WA, l, comm)
    plans.finish(done, extra)
    comm, done = plans.start(f"stick_bwd_{l}")
    (dqb, dkb, dvb), extra = _sb_bwd(S["hq"], S["ob"], dob, 3 * WA, WB, l, comm)
    plans.finish(done, extra)
    dh = jnp.concatenate([dqa, dka, dva, dqb, dkb, dvb, dhg], axis=1)
    comm, done = plans.start(f"grad_w_in_{l}")
    g, extra = _grad_w(S["x"], dh, col_shards=True, name=f"grad_w_in_{l}", comm=comm)
    gw["w_in"] = blocks(g, "w_in")
    plans.finish(done, extra)
    comm, done = plans.start(f"in_proj_bwd_{l}")
    dx, extra = _residual_nt(du1, alpha, dh, W["w_in"], f"in_proj_bwd_{l}", comm)
    plans.finish(done, extra)
    gs = dict(b_gate=st1[0], rel_bias=_fold_bias_grad(dbias), ln1_g=st1[1, :D], ln1_b=st1[1, D:],
              ln2_g=st2[0], ln2_b=st2[1])
    return dx, gs, st2[2]


def _local_step(x, target, W, small, plans=None, gws=None):
    depth = len(W)
    alpha = float((2 * depth) ** 0.25)
    plans = plans or _Plans()
    gws = gws if gws is not None else [dict() for _ in range(depth)]
    saved = []
    h = x
    for l in range(depth):
        h, S = _layer_fwd(h, W[l], small, l, alpha, plans)
        saved.append(S)
    gss = [None] * depth
    d = target
    sq = None
    for l in reversed(range(depth)):
        d, gss[l], sq_l = _layer_bwd(d, saved[l], W[l], small, l, l == depth - 1, alpha, plans, gws[l])
        if l == depth - 1:
            sq = sq_l
    return sq, d, gws, gss


def _place():
    return lax.axis_index("x"), lax.axis_index("y"), lax.axis_index("c")


def _remote(src, dst, send_sem, recv_sem, to):
    return pltpu.make_async_remote_copy(src_ref=src, dst_ref=dst, send_sem=send_sem, recv_sem=recv_sem,
                                        device_id=to, device_id_type=_MESH)


def _half(ref, hc):
    kh = ref.shape[0] // 2
    return ref.at[pl.ds(pl.multiple_of(hc * kh, 16), kh), :]


def _gather_plan(blocks, fractions):
    nt = len(blocks)

    def run(step, nsteps, ins, outs, sems):
        send_sems, recv_sems, loc_sems = sems
        x, y, c = _place()
        k = 2 * x + y
        me, sibling = (x, y, c), (x, y, 1 - c)
        chips = [(1 - x, y), (x, 1 - y), (1 - x, 1 - y)]
        chip_k = [2 * cx + cy for cx, cy in chips]

        def ici(t, s, owner_k, to, src=None):
            dst = _half(outs[t].at[owner_k], c)
            return _remote(dst if src is None else src, dst, send_sems.at[t, s], recv_sems.at[t, s], to)

        def passed(t, s, hc, to):
            blk = _half(outs[t].at[chip_k[s]], hc)
            return _remote(blk, blk, send_sems.at[t, 3 + s], recv_sems.at[t, 3 + s], to)

        def local(t):
            return pltpu.make_async_copy(ins[t], outs[t].at[k], loc_sems.at[t])

        @pl.when(step == 0)
        def _():
            for t in range(nt):
                local(t).start()
                for s, chip in enumerate(chips):
                    ici(t, s, k, (*chip, c), src=_half(ins[t], c)).start()

        for t in range(nt):
            @pl.when(step == min(nsteps - 1, int(fractions[t] * nsteps)))
            def _():
                for s in range(3):
                    ici(t, s, chip_k[s], me).wait_recv()
                    passed(t, s, c, sibling).start()

        @pl.when(step == nsteps - 1)
        def _():
            for t in range(nt):
                for s, chip in enumerate(chips):
                    passed(t, s, 1 - c, me).wait_recv()
            for t in range(nt):
                for s, chip in enumerate(chips):
                    ici(t, s, k, (*chip, c), src=_half(ins[t], c)).wait_send()
                    passed(t, s, c, sibling).wait_send()
                local(t).wait()

    return _Comm(blocks, [jax.ShapeDtypeStruct((4,) + b.shape, b.dtype) for b in blocks],
                 [pltpu.SemaphoreType.DMA((nt, 6)), pltpu.SemaphoreType.DMA((nt, 6)), pltpu.SemaphoreType.DMA((nt,))], run)


def _scatter_plan(grads, owners):
    nt = len(grads)

    def run(step, nsteps, ins, outs, sems):
        send_sems, recv_sems, loc_sems = sems
        x, y, c = _place()
        me = 4 * x + 2 * y + c

        def target(r):
            tx = 1 - x if r & 2 else x
            ty = 1 - y if r & 1 else y
            return tx, ty

        def send(t, r):
            tx, ty = target(r)
            return _remote(ins[t].at[2 * tx + ty], outs[t].at[me], send_sems.at[t, r], recv_sems.at[t, 2 * r + c],
                           (tx, ty, owners[t]))

        def local(t):
            return pltpu.make_async_copy(ins[t].at[2 * x + y], outs[t].at[me], loc_sems.at[t])

        @pl.when(step == 0)
        def _():
            for t in range(nt):
                @pl.when(c == owners[t])
                def _():
                    local(t).start()

                @pl.when(c != owners[t])
                def _():
                    send(t, 0).start()

                for r in range(1, 4):
                    send(t, r).start()

        @pl.when(step == nsteps - 1)
        def _():
            for t in range(nt):
                @pl.when(c == owners[t])
                def _():
                    for r in range(4):
                        sx, sy = target(r)
                        for cs in range(2):
                            if r == 0 and cs == owners[t]:
                                continue
                            src_dev = 4 * sx + 2 * sy + cs
                            _remote(ins[t].at[0], outs[t].at[src_dev], send_sems.at[t, r], recv_sems.at[t, 2 * r + cs],
                                    (x, y, c)).wait_recv()
                    local(t).wait()

                @pl.when(c != owners[t])
                def _():
                    send(t, 0).wait_send()

                for r in range(1, 4):
                    send(t, r).wait_send()

    return _Comm(grads, [jax.ShapeDtypeStruct((8,) + g.shape[1:], g.dtype) for g in grads],
                 [pltpu.SemaphoreType.DMA((nt, 4)), pltpu.SemaphoreType.DMA((nt, 8)), pltpu.SemaphoreType.DMA((nt,))], run)


def _share_plan(reduced, owners):
    nt = len(reduced)

    def run(step, nsteps, ins, outs, sems):
        del ins
        send_sems, recv_sems = sems
        x, y, c = _place()

        def give(t, to):
            return _remote(outs[t], outs[t], send_sems.at[t], recv_sems.at[t], to)

        @pl.when(step == 0)
        def _():
            for t in range(nt):
                @pl.when(c == owners[t])
                def _():
                    give(t, (x, y, 1 - c)).start()

        @pl.when(step == nsteps - 1)
        def _():
            for t in range(nt):
                @pl.when(c == owners[t])
                def _():
                    give(t, (x, y, 1 - c)).wait_send()

                @pl.when(c != owners[t])
                def _():
                    give(t, (x, y, c)).wait_recv()

    return _Comm(reduced, [jax.ShapeDtypeStruct(r.shape, r.dtype) for r in reduced],
                 [pltpu.SemaphoreType.DMA((nt,)), pltpu.SemaphoreType.DMA((nt,))], run,
                 aliases={t: t for t in range(nt)})


def _join(a, b):
    ni, no, ns = len(a.inputs), len(a.out_shapes), len(a.sems)

    def run(step, nsteps, ins, outs, sems):
        a.run(step, nsteps, ins[:ni], outs[:no], sems[:ns])
        b.run(step, nsteps, ins[ni:], outs[no:], sems[ns:])

    aliases = dict(a.aliases)
    aliases.update({ni + i: no + o for i, o in b.aliases.items()})
    return _Comm(a.inputs + b.inputs, a.out_shapes + b.out_shapes, a.sems + b.sems, run, aliases)


def _peer(x, y, c, r):
    px = 1 - x if r & 4 else x
    py = 1 - y if r & 2 else y
    pc = 1 - c if r & 1 else c
    return (px, py, pc), 4 * px + 2 * py + pc


def _sum_slots(st, name):
    _, K, n = st.shape
    tr = next(t for t in (256, 128, 64, 32, 16) if K % t == 0)

    def kern(s_ref, o_ref):
        acc = s_ref[0].astype(_F32)
        for d in range(1, 8):
            acc = acc + s_ref[d].astype(_F32)
        o_ref[...] = acc.astype(o_ref.dtype)

    return pl.pallas_call(
        kern, name=name, grid=(K // tr,),
        in_specs=[pl.BlockSpec((8, tr, n), lambda i: (0, i, 0))], out_specs=_rows(tr, n),
        out_shape=jax.ShapeDtypeStruct((K, n), _ACT),
        compiler_params=_cparams("parallel"),
    )(st)


def _all_reduce_small(p):
    R = p.shape[0]

    def body(p_ref, o_ref, stage, send_sems, recv_sems):
        x, y, c = _place()
        me = 4 * x + 2 * y + c
        stage[me] = p_ref[...]
        sent = []
        for r in range(1, 8):
            to, _ = _peer(x, y, c, r)
            cp = _remote(p_ref, stage.at[me], send_sems.at[r - 1], recv_sems.at[r - 1], to)
            cp.start()
            sent.append(cp)
        for r in range(1, 8):
            _, src_dev = _peer(x, y, c, r)
            _remote(p_ref, stage.at[src_dev], send_sems.at[r - 1], recv_sems.at[r - 1], (x, y, c)).wait_recv()
        acc = stage[0]
        for d in range(1, 8):
            acc = acc + stage[d]
        o_ref[...] = acc
        for cp in sent:
            cp.wait_send()

    vm = pl.BlockSpec(memory_space=pltpu.VMEM)
    return pl.pallas_call(
        body, name="all_reduce_small",
        in_specs=[vm], out_specs=vm,
        out_shape=jax.ShapeDtypeStruct((R, _LANES), _F32),
        scratch_shapes=[pltpu.VMEM((8, R, _LANES), _F32), pltpu.SemaphoreType.DMA((7,)), pltpu.SemaphoreType.DMA((7,))],
    )(p)


def _adamw(w, g, m, v, name):
    shape = w.shape
    w2, g2, m2, v2 = (a.reshape(-1, shape[-1]) for a in (w, g, m, v))
    R, C = w2.shape
    tr = next((t for t in (256, 128, 64, 32, 16) if R % t == 0), R)

    def kern(w_ref, g_ref, m_ref, v_ref, gf_ref, d_ref, nm_ref, nv_ref):
        gv = g_ref[...].astype(_F32)
        nm = _B1 * m_ref[...] + (1.0 - _B1) * gv
        nv = _B2 * v_ref[...] + (1.0 - _B2) * (gv * gv)
        m_hat = nm / (1.0 - _B1 ** _STEP)
        v_hat = nv / (1.0 - _B2 ** _STEP)
        gf_ref[...] = gv
        d_ref[...] = -_LR * (m_hat / (jnp.sqrt(v_hat) + _EPS) + _WD * w_ref[...])
        nm_ref[...] = nm
        nv_ref[...] = nv

    outs = pl.pallas_call(
        kern, name=name, grid=(R // tr,),
        in_specs=[_rows(tr, C)] * 4, out_specs=[_rows(tr, C)] * 4,
        out_shape=[jax.ShapeDtypeStruct((R, C), _F32)] * 4,
        compiler_params=_cparams("parallel"),
    )(w2, g2, m2, v2)
    return tuple(o.reshape(shape) for o in outs)


def _pack_small(gss, sq):
    parts = [gss[l][n].reshape(-1) for n in _SMALL for l in range(len(gss))] + [jnp.sum(sq).reshape(1)]
    flat = jnp.concatenate(parts)
    rows = -(-flat.shape[0] // (8 * _LANES)) * 8
    return jnp.pad(flat, (0, rows * _LANES - flat.shape[0])).reshape(rows, _LANES)


def _unpack_small(total, shapes):
    flat = total.reshape(-1)
    out, off = {}, 0
    for n in _SMALL:
        layers = []
        for _ in range(shapes[n][0]):
            size = 1
            for s in shapes[n][1:]:
                size *= s
            layers.append(flat[off:off + size].reshape(shapes[n][1:]))
            off += size
        out[n] = jnp.stack(layers)
    return out, flat[off]


_GATHER = {
    "band_fwd_0": [(0, "w_proj_a"), (0, "w_proj_b"), (0, "w_out"), (0, "w_ffn_out")],
    "stick_fwd_0": [(0, "w_ffn_in"), (1, "w_in")],
    "ffn_fwd_0": [(1, "w_proj_a"), (1, "w_proj_b"), (1, "w_out"), (1, "w_ffn_in"), (1, "w_ffn_out")],
}
_SCATTER = {
    "band_bwd_1": [(1, "w_ffn_in"), (1, "w_ffn_out")],
    "stick_bwd_1": [(1, "w_proj_a"), (1, "w_proj_b"), (1, "w_out")],
    "band_bwd_0": [(1, "w_in"), (0, "w_ffn_in")],
    "stick_bwd_0": [(0, "w_ffn_out"), (0, "w_proj_a"), (0, "w_proj_b"), (0, "w_out")],
    "in_proj_bwd_0": [(0, "w_in")],
}
_SHARE = {"stick_bwd_1": "band_bwd_1", "band_bwd_0": "stick_bwd_1", "stick_bwd_0": "band_bwd_0", "grad_w_in_0": "stick_bwd_0"}


def _owner(key):
    del key
    return 1


def kernel(x, w_in, b_gate, rel_bias, w_proj_a, w_proj_b, w_out, ln1_g, ln1_b, w_ffn_in, w_ffn_out, ln2_g, ln2_b, loss_target, m_w_in, m_b_gate, m_rel_bias, m_w_proj_a, m_w_proj_b, m_w_out, m_ln1_g, m_ln1_b, m_w_ffn_in, m_w_ffn_out, m_ln2_g, m_ln2_b, v_w_in, v_b_gate, v_rel_bias, v_w_proj_a, v_w_proj_b, v_w_out, v_ln1_g, v_ln1_b, v_w_ffn_in, v_w_ffn_out, v_ln2_g, v_ln2_b):
    names = ("w_in", "b_gate", "rel_bias", "w_proj_a", "w_proj_b", "w_out", "ln1_g", "ln1_b", "w_ffn_in", "w_ffn_out", "ln2_g", "ln2_b")
    w = dict(zip(names, (w_in, b_gate, rel_bias, w_proj_a, w_proj_b, w_out, ln1_g, ln1_b, w_ffn_in, w_ffn_out, ln2_g, ln2_b)))
    m = dict(zip(names, (m_w_in, m_b_gate, m_rel_bias, m_w_proj_a, m_w_proj_b, m_w_out, m_ln1_g, m_ln1_b, m_w_ffn_in, m_w_ffn_out, m_ln2_g, m_ln2_b)))
    v = dict(zip(names, (v_w_in, v_b_gate, v_rel_bias, v_w_proj_a, v_w_proj_b, v_w_out, v_ln1_g, v_ln1_b, v_w_ffn_in, v_w_ffn_out, v_ln2_g, v_ln2_b)))
    T, D = x.shape[-2], x.shape[-1]
    assert w_in.shape[0] == 2, "the exchange schedule below is written for two layers"

    mine = [{n: w[n][l].astype(_MXU) for n in _DENSE} for l in range(2)]
    W = [dict(), dict()]
    gws = [dict(), dict()]
    slots, final = {}, {}

    def gather(keys):
        sizes = [mine[l][n].size for l, n in keys]
        passed, fractions = 0, []
        for s in sizes:
            passed += s
            fractions.append(0.15 + 0.6 * passed / sum(sizes))

        def done(outs):
            for (l, n), o in zip(keys, outs):
                W[l][n] = o
        return _gather_plan([mine[l][n] for l, n in keys], fractions), done

    def scatter(keys):
        comm = _scatter_plan([gws[l][n] for l, n in keys], [_owner(key) for key in keys])
        return comm, lambda outs: slots.update(zip(keys, outs))

    def share(keys):
        reduced = [_sum_slots(slots[key], f"sum_grad_{key[1]}_{key[0]}") for key in keys]
        comm = _share_plan(reduced, [_owner(key) for key in keys])
        return comm, lambda outs: final.update(zip(keys, outs))

    def both(first, second):
        (ca, da), (cb, db) = first, second
        na = len(ca.out_shapes)
        return _join(ca, cb), lambda outs: (da(outs[:na]), db(outs[na:]))

    comm, done = gather([(0, "w_in")])
    done(_comm_only(comm, "gather_first"))
    plans = {key: functools.partial(gather, keys) for key, keys in _GATHER.items()}
    for key, keys in _SCATTER.items():
        plans[key] = functools.partial(scatter, keys)
    for key, scattered_under in _SHARE.items():
        handed = functools.partial(share, _SCATTER[scattered_under])
        carried = plans.get(key)
        plans[key] = handed if carried is None else (lambda carried=carried, handed=handed: both(carried(), handed()))
    small = {n: w[n] for n in _SMALL}
    sq, dx, _, gss = _local_step(x.reshape(T, D), loss_target.reshape(T, D), W, small, _Plans(plans), gws)

    comm, done = share(_SCATTER["in_proj_bwd_0"])
    done(_comm_only(comm, "share_last"))
    grads = {n: jnp.stack([final[(l, n)] for l in range(2)]) for n in _DENSE}

    total = _all_reduce_small(_pack_small(gss, sq))
    small_grads, sq_all = _unpack_small(total, {n: w[n].shape for n in _SMALL})
    grads.update(small_grads)
    loss = 0.5 * sq_all / D

    grad, delta, new_m, new_v = {}, {}, {}, {}
    for n in names:
        grad[n], delta[n], new_m[n], new_v[n] = _adamw(w[n], grads[n].reshape(w[n].shape), m[n], v[n], f"adamw_{n}")
    return (loss, dx.reshape(x.shape), *[grad[n] for n in names], *[delta[n] for n in names],
            *[new_m[n] for n in names], *[new_v[n] for n in names])
```

```python
import functools

import jax
import jax.numpy as jnp
from jax import lax
from jax.experimental import pallas as pl
from jax.experimental.pallas import tpu as pltpu

_MXU = jnp.bfloat16
_ACT = jnp.bfloat16
_F32 = jnp.float32

_HEAD = 64
_CHUNK = 64
_LANES = 128
_TQ = 128
_BAND_TILES = 5
_BIAS_TILES = 9
_REL_CLIP = 256
_LN_EPS = 1e-5
_MASKED = -1e30
_EXP_ZERO_BELOW = -87.34
_SB_WINDOW = 2
_SB_SUBTILES = 2
_BAND_SUBTILES = 2
_VMEM_LIMIT = 56 * 1024 * 1024
_GRAD_ACC_BYTES = 12 * 1024 * 1024

_LR, _B1, _B2, _EPS, _WD, _STEP = 0.001, 0.9, 0.999, 1e-08, 0.01, 10

_MESH = pl.DeviceIdType.MESH


def _dot(a, b):
    return jnp.dot(a, b, preferred_element_type=_F32)


def _dot_nt(a, b):
    return lax.dot_general(a, b, (((1,), (1,)), ((), ())), preferred_element_type=_F32)


def _dot_tn(a, b):
    return lax.dot_general(a, b, (((0,), (0,)), ((), ())), preferred_element_type=_F32)


def _cparams(*sem):
    return pltpu.CompilerParams(dimension_semantics=sem, vmem_limit_bytes=_VMEM_LIMIT)


def _rows(t, c):
    return pl.BlockSpec((t, c), lambda i: (i, 0))


def _whole(shape):
    return pl.BlockSpec(shape, lambda i: tuple(0 for _ in shape))


_ANY = pl.BlockSpec(memory_space=pl.ANY)


def _load_cols(w_hbm, w_vmem, sem):
    n = w_hbm.shape[-1]
    cps = [pltpu.make_async_copy(w_hbm.at[k], w_vmem.at[:, pl.ds(k * n, n)], sem.at[k]) for k in range(4)]
    for cp in cps:
        cp.start()
    for cp in cps:
        cp.wait()


def _load_rows(w_hbm, w_vmem, sem):
    r = w_hbm.shape[-2]
    cps = [pltpu.make_async_copy(w_hbm.at[k], w_vmem.at[pl.ds(k * r, r), :], sem.at[k]) for k in range(4)]
    for cp in cps:
        cp.start()
    for cp in cps:
        cp.wait()


def _ln_stats(u):
    mu = jnp.mean(u, axis=-1, keepdims=True)
    xc = u - mu
    var = jnp.mean(xc * xc, axis=-1, keepdims=True)
    rstd = lax.rsqrt(var + _LN_EPS)
    return xc * rstd, rstd


def _ln_bwd(u, dy, gamma):
    xhat, rstd = _ln_stats(u)
    dxh = dy * gamma
    m1 = jnp.mean(dxh, axis=-1, keepdims=True)
    m2 = jnp.mean(dxh * xhat, axis=-1, keepdims=True)
    du = rstd * (dxh - m1 - xhat * m2)
    return du, jnp.sum(dy * xhat, axis=0, keepdims=True), jnp.sum(dy, axis=0, keepdims=True), xhat


def _divisor_tile(n, cap):
    best = None
    for t in range(_LANES, min(n, cap) + 1, _LANES):
        if n % t == 0:
            best = t
    return best or n


class _Comm:
    def __init__(self, inputs, out_shapes, sems, run, aliases=None):
        self.inputs, self.out_shapes, self.sems, self.run = list(inputs), list(out_shapes), list(sems), run
        self.aliases = aliases or {}


def _call(kern, comm, *, name, grid, in_specs, out_specs, out_shape, scratch_shapes, args, semantics):
    in_specs, out_specs, out_shape, scratch_shapes = list(in_specs), list(out_specs), list(out_shape), list(scratch_shapes)
    if comm is None:
        outs = pl.pallas_call(kern, name=name, grid=grid, in_specs=in_specs, out_specs=out_specs, out_shape=out_shape,
                              scratch_shapes=scratch_shapes, compiler_params=_cparams(*semantics))(*args)
        return list(outs), []
    n_in, n_out, n_scr = len(in_specs), len(out_specs), len(scratch_shapes)
    ci, co = len(comm.inputs), len(comm.out_shapes)
    nsteps = functools.reduce(lambda a, b: a * b, grid, 1)

    def fused(*refs):
        a, b = n_in, n_in + ci
        c, d = b + n_out, b + n_out + co
        e = d + n_scr
        step = pl.program_id(0)
        for ax in range(1, len(grid)):
            step = step * grid[ax] + pl.program_id(ax)
        comm.run(step, nsteps, refs[a:b], refs[c:d], refs[e:])
        kern(*refs[:a], *refs[b:c], *refs[d:e])

    outs = pl.pallas_call(
        fused, name=name, grid=grid, in_specs=in_specs + [_ANY] * ci, out_specs=out_specs + [_ANY] * co,
        out_shape=out_shape + comm.out_shapes, scratch_shapes=scratch_shapes + comm.sems,
        input_output_aliases={n_in + i: n_out + o for i, o in comm.aliases.items()},
        compiler_params=_cparams(*("arbitrary" for _ in grid)))(*args, *comm.inputs)
    return list(outs[:n_out]), list(outs[n_out:])


def _comm_only(comm, name):
    def body(*refs):
        ci, co = len(comm.inputs), len(comm.out_shapes)
        comm.run(0, 1, refs[:ci], refs[ci:ci + co], refs[ci + co:])

    outs = pl.pallas_call(body, name=name, in_specs=[_ANY] * len(comm.inputs), out_specs=[_ANY] * len(comm.out_shapes),
                          out_shape=comm.out_shapes, scratch_shapes=comm.sems,
                          input_output_aliases=dict(comm.aliases))(*comm.inputs)
    return list(outs)


def _in_proj(x, w_in, layer):
    T, D = x.shape
    N = 4 * w_in.shape[-1]
    NQ = N - 2 * D
    tm = 256

    def kern(x_ref, w_hbm, hq_ref, hg_ref, xb_ref, w_v, sem):
        @pl.when(pl.program_id(0) == 0)
        def _():
            _load_cols(w_hbm, w_v, sem)

        xb = x_ref[...].astype(_MXU)
        hq_ref[...] = _dot(xb, w_v[:, :NQ]).astype(hq_ref.dtype)
        hg_ref[...] = _dot(xb, w_v[:, NQ:])
        xb_ref[...] = xb.astype(xb_ref.dtype)

    return pl.pallas_call(
        kern, name=f"in_proj_{layer}", grid=(T // tm,),
        in_specs=[_rows(tm, D), _ANY],
        out_specs=[_rows(tm, NQ), _rows(tm, 2 * D), _rows(tm, D)],
        out_shape=[jax.ShapeDtypeStruct((T, NQ), _ACT), jax.ShapeDtypeStruct((T, 2 * D), _F32),
                   jax.ShapeDtypeStruct((T, D), _ACT)],
        scratch_shapes=[pltpu.VMEM((D, N), w_in.dtype), pltpu.SemaphoreType.DMA((4,))],
        compiler_params=_cparams("arbitrary"),
    )(x, w_in)


def _mix_fwd(oa, ob, hg, x, wpa, wpb, wo, bg, gamma, beta, alpha, layer):
    T, D = x.shape
    WA, WB = oa.shape[1], ob.shape[1]
    tm = 256

    def kern(oa_ref, ob_ref, hg_ref, x_ref, bg_ref, g_ref, b_ref, wpa_h, wpb_h, wo_h,
             x1_ref, u1_ref, pre_ref, ya_ref, yb_ref, wpa_v, wpb_v, wo_v, sa, sb, so):
        @pl.when(pl.program_id(0) == 0)
        def _():
            _load_cols(wpa_h, wpa_v, sa)
            _load_cols(wpb_h, wpb_v, sb)
            _load_rows(wo_h, wo_v, so)

        ya = _dot(oa_ref[...].astype(_MXU), wpa_v[...])
        yb = _dot(ob_ref[...].astype(_MXU), wpb_v[...])
        hgv = hg_ref[...]
        bgv = bg_ref[...]
        ga = jax.nn.sigmoid(hgv[:, :D] + bgv[:, :D])
        gb = jax.nn.sigmoid(hgv[:, D:] + bgv[:, D:])
        pre = ga * ya + gb * yb
        mix = _dot(pre.astype(_MXU), wo_v[...])
        u = alpha * x_ref[...] + mix
        xhat, _ = _ln_stats(u)
        x1_ref[...] = xhat * g_ref[...] + b_ref[...]
        u1_ref[...] = u
        pre_ref[...] = pre.astype(pre_ref.dtype)
        ya_ref[...] = ya.astype(ya_ref.dtype)
        yb_ref[...] = yb.astype(yb_ref.dtype)

    return pl.pallas_call(
        kern, name=f"mix_fwd_{layer}", grid=(T // tm,),
        in_specs=[_rows(tm, WA), _rows(tm, WB), _rows(tm, 2 * D), _rows(tm, D),
                  _whole((1, 2 * D)), _whole((1, D)), _whole((1, D)), _ANY, _ANY, _ANY],
        out_specs=[_rows(tm, D)] * 5,
        out_shape=[jax.ShapeDtypeStruct((T, D), _F32), jax.ShapeDtypeStruct((T, D), _F32)]
        + [jax.ShapeDtypeStruct((T, D), _ACT)] * 3,
        scratch_shapes=[pltpu.VMEM((WA, D), wpa.dtype), pltpu.VMEM((WB, D), wpb.dtype), pltpu.VMEM((D, D), wo.dtype),
                        pltpu.SemaphoreType.DMA((4,)), pltpu.SemaphoreType.DMA((4,)), pltpu.SemaphoreType.DMA((4,))],
        compiler_params=_cparams("arbitrary"),
    )(oa, ob, hg, x, bg, gamma, beta, wpa, wpb, wo)


def _ffn_fwd(x1, wfi, wfo, gamma, beta, alpha, layer, comm=None):
    T, D = x1.shape
    F2 = 4 * wfi.shape[-1]
    F = F2 // 2
    tm = 256
    fc = F // 2

    def kern(x_ref, g_ref, b_ref, wi_h, wo_h, x2_ref, u2_ref, act_ref, gu_ref, xb_ref, wi_v, wo_v, si, so):
        @pl.when(pl.program_id(0) == 0)
        def _():
            _load_cols(wi_h, wi_v, si)
            _load_rows(wo_h, wo_v, so)

        x = x_ref[...]
        xb = x.astype(_MXU)
        xb_ref[...] = xb.astype(xb_ref.dtype)
        ffn = jnp.zeros((tm, D), _F32)
        for c in range(2):
            g = _dot(xb, wi_v[:, c * fc:(c + 1) * fc])
            u = _dot(xb, wi_v[:, F + c * fc:F + (c + 1) * fc])
            act = g * jax.nn.sigmoid(g) * u
            ab = act.astype(_MXU)
            ffn = ffn + _dot(ab, wo_v[c * fc:(c + 1) * fc, :])
            act_ref[:, c * fc:(c + 1) * fc] = ab.astype(act_ref.dtype)
            gu_ref[:, c * fc:(c + 1) * fc] = g.astype(gu_ref.dtype)
            gu_ref[:, F + c * fc:F + (c + 1) * fc] = u.astype(gu_ref.dtype)
        uu = alpha * x + ffn
        xhat, _ = _ln_stats(uu)
        x2_ref[...] = xhat * g_ref[...] + b_ref[...]
        u2_ref[...] = uu

    return _call(
        kern, comm, name=f"ffn_fwd_{layer}", grid=(T // tm,),
        in_specs=[_rows(tm, D), _whole((1, D)), _whole((1, D)), _ANY, _ANY],
        out_specs=[_rows(tm, D), _rows(tm, D), _rows(tm, F), _rows(tm, F2), _rows(tm, D)],
        out_shape=[jax.ShapeDtypeStruct((T, D), _F32), jax.ShapeDtypeStruct((T, D), _F32),
                   jax.ShapeDtypeStruct((T, F), _ACT), jax.ShapeDtypeStruct((T, F2), _ACT),
                   jax.ShapeDtypeStruct((T, D), _ACT)],
        scratch_shapes=[pltpu.VMEM((D, F2), wfi.dtype), pltpu.VMEM((F, D), wfo.dtype),
                        pltpu.SemaphoreType.DMA((4,)), pltpu.SemaphoreType.DMA((4,))],
        args=(x1, gamma, beta, wfi, wfo), semantics=("arbitrary",))


def _ffn_bwd(u2, dy_or_target, gu, gamma, beta, wfi, wfo, alpha, layer, last):
    T, D = u2.shape
    F2 = gu.shape[1]
    F = F2 // 2
    tm = 256
    fc = F // 2

    def kern(u_ref, dy_ref, gu_ref, g_ref, b_ref, wi_h, wo_h, dx_ref, dub_ref, dgu_ref, st_ref, wi_v, wo_v, si, so):
        @pl.when(pl.program_id(0) == 0)
        def _():
            _load_cols(wi_h, wi_v, si)
            _load_rows(wo_h, wo_v, so)
            st_ref[...] = jnp.zeros_like(st_ref)

        gam = g_ref[...]
        u = u_ref[...]
        if last:
            xhat0, _ = _ln_stats(u)
            err = xhat0 * gam + b_ref[...] - dy_ref[...]
            dy = err * (1.0 / D)
            st_ref[2:3, :] += jnp.sum(err * err, axis=0, keepdims=True)
        else:
            dy = dy_ref[...]
        du, dgam, dbet, _ = _ln_bwd(u, dy, gam)
        st_ref[0:1, :] += dgam
        st_ref[1:2, :] += dbet
        dub = du.astype(_MXU)
        dub_ref[...] = dub.astype(dub_ref.dtype)
        dx = alpha * du
        for c in range(2):
            dact = _dot_nt(dub, wo_v[c * fc:(c + 1) * fc, :])
            g = gu_ref[:, c * fc:(c + 1) * fc].astype(_F32)
            uu = gu_ref[:, F + c * fc:F + (c + 1) * fc].astype(_F32)
            sg = jax.nn.sigmoid(g)
            dg = (dact * uu * (sg * (1.0 + g * (1.0 - sg)))).astype(_MXU)
            dup = (dact * (g * sg)).astype(_MXU)
            dgu_ref[:, c * fc:(c + 1) * fc] = dg.astype(dgu_ref.dtype)
            dgu_ref[:, F + c * fc:F + (c + 1) * fc] = dup.astype(dgu_ref.dtype)
            dx = dx + _dot_nt(dg, wi_v[:, c * fc:(c + 1) * fc]) + _dot_nt(dup, wi_v[:, F + c * fc:F + (c + 1) * fc])
        dx_ref[...] = dx

    return pl.pallas_call(
        kern, name=f"ffn_bwd_{layer}", grid=(T // tm,),
        in_specs=[_rows(tm, D), _rows(tm, D), _rows(tm, F2), _whole((1, D)), _whole((1, D)), _ANY, _ANY],
        out_specs=[_rows(tm, D), _rows(tm, D), _rows(tm, F2), _whole((8, D))],
        out_shape=[jax.ShapeDtypeStruct((T, D), _F32), jax.ShapeDtypeStruct((T, D), _ACT),
                   jax.ShapeDtypeStruct((T, F2), _ACT), jax.ShapeDtypeStruct((8, D), _F32)],
        scratch_shapes=[pltpu.VMEM((D, F2), wfi.dtype), pltpu.VMEM((F, D), wfo.dtype),
                        pltpu.SemaphoreType.DMA((4,)), pltpu.SemaphoreType.DMA((4,))],
        compiler_params=_cparams("arbitrary"),
    )(u2, dy_or_target, gu, gamma, beta, wfi, wfo)


def _residual_nt(res, res_scale, d, w, name, comm=None):
    T, K = res.shape
    N = d.shape[1]
    tm = 256

    def kern(r_ref, d_ref, w_hbm, o_ref, w_v, sem):
        @pl.when(pl.program_id(0) == 0)
        def _():
            _load_cols(w_hbm, w_v, sem)

        o_ref[...] = res_scale * r_ref[...] + _dot_nt(d_ref[...].astype(_MXU), w_v[...])

    outs, extra = _call(
        kern, comm, name=name, grid=(T // tm,),
        in_specs=[_rows(tm, K), _rows(tm, N), _ANY], out_specs=[_rows(tm, K)],
        out_shape=[jax.ShapeDtypeStruct((T, K), _F32)],
        scratch_shapes=[pltpu.VMEM((K, N), w.dtype), pltpu.SemaphoreType.DMA((4,))],
        args=(res, d, w), semantics=("arbitrary",))
    return outs[0], extra


def _mix_bwd(u1, dx1, ya, yb, hg, wpa, wpb, wo, bg, gamma, alpha, layer):
    del alpha
    T, D = u1.shape
    WA, WB = wpa.shape[-2], wpb.shape[-2]
    tm = 256

    def kern(u_ref, dx_ref, ya_ref, yb_ref, hg_ref, bg_ref, g_ref, wpa_h, wpb_h, wo_h,
             du_ref, dub_ref, dya_ref, dyb_ref, dhg_ref, doa_ref, dob_ref, st_ref,
             wpa_v, wpb_v, wo_v, sa, sb, so):
        @pl.when(pl.program_id(0) == 0)
        def _():
            _load_cols(wpa_h, wpa_v, sa)
            _load_cols(wpb_h, wpb_v, sb)
            _load_rows(wo_h, wo_v, so)
            st_ref[...] = jnp.zeros_like(st_ref)

        du, dgam, dbet, _ = _ln_bwd(u_ref[...], dx_ref[...], g_ref[...])
        st_ref[1:2, :D] += dgam
        st_ref[1:2, D:] += dbet
        du_ref[...] = du
        dub = du.astype(_MXU)
        dub_ref[...] = dub.astype(dub_ref.dtype)
        dpre = _dot_nt(dub, wo_v[...])
        hgv = hg_ref[...]
        bgv = bg_ref[...]
        ga = jax.nn.sigmoid(hgv[:, :D] + bgv[:, :D])
        gb = jax.nn.sigmoid(hgv[:, D:] + bgv[:, D:])
        dya = (dpre * ga).astype(_MXU)
        dyb = (dpre * gb).astype(_MXU)
        dsa = dpre * ya_ref[...].astype(_F32) * (ga * (1.0 - ga))
        dsb = dpre * yb_ref[...].astype(_F32) * (gb * (1.0 - gb))
        st_ref[0:1, :D] += jnp.sum(dsa, axis=0, keepdims=True)
        st_ref[0:1, D:] += jnp.sum(dsb, axis=0, keepdims=True)
        dya_ref[...] = dya.astype(dya_ref.dtype)
        dyb_ref[...] = dyb.astype(dyb_ref.dtype)
        dhg_ref[:, :D] = dsa.astype(dhg_ref.dtype)
        dhg_ref[:, D:] = dsb.astype(dhg_ref.dtype)
        doa_ref[...] = _dot_nt(dya, wpa_v[...]).astype(doa_ref.dtype)
        dob_ref[...] = _dot_nt(dyb, wpb_v[...]).astype(dob_ref.dtype)

    return pl.pallas_call(
        kern, name=f"mix_bwd_{layer}", grid=(T // tm,),
        in_specs=[_rows(tm, D)] * 4 + [_rows(tm, 2 * D), _whole((1, 2 * D)), _whole((1, D)), _ANY, _ANY, _ANY],
        out_specs=[_rows(tm, D)] * 4 + [_rows(tm, 2 * D), _rows(tm, WA), _rows(tm, WB), _whole((8, 2 * D))],
        out_shape=[jax.ShapeDtypeStruct((T, D), _F32)] + [jax.ShapeDtypeStruct((T, D), _ACT)] * 3
        + [jax.ShapeDtypeStruct((T, 2 * D), _ACT), jax.ShapeDtypeStruct((T, WA), _ACT),
           jax.ShapeDtypeStruct((T, WB), _ACT), jax.ShapeDtypeStruct((8, 2 * D), _F32)],
        scratch_shapes=[pltpu.VMEM((WA, D), wpa.dtype), pltpu.VMEM((WB, D), wpb.dtype), pltpu.VMEM((D, D), wo.dtype),
                        pltpu.SemaphoreType.DMA((4,)), pltpu.SemaphoreType.DMA((4,)), pltpu.SemaphoreType.DMA((4,))],
        compiler_params=_cparams("arbitrary"),
    )(u1, dx1, ya, yb, hg, bg, gamma, wpa, wpb, wo)


def _grad_w(a, b, *, col_shards, name, comm=None):
    T, M = a.shape
    N = b.shape[1]
    tk = 512
    n = N // 4 if col_shards else N
    whole = M * N * 4 <= _GRAD_ACC_BYTES
    tn = N if whole else (n if col_shards else _divisor_tile(N, _GRAD_ACC_BYTES // (4 * M)))
    nk = T // tk

    def kern(a_ref, b_ref, o_ref, acc):
        k = pl.program_id(1)

        @pl.when(k == 0)
        def _():
            acc[...] = jnp.zeros_like(acc)

        acc[...] += _dot_tn(a_ref[...].astype(_MXU), b_ref[...].astype(_MXU))

        @pl.when(k == nk - 1)
        def _():
            if col_shards and whole:
                for s in range(4):
                    o_ref[s] = acc[:, s * n:(s + 1) * n].astype(o_ref.dtype)
            else:
                o_ref[...] = acc[...].astype(o_ref.dtype)

    if col_shards:
        out_spec = (pl.BlockSpec((4, M, n), lambda j, k: (0, 0, 0)) if whole
                    else pl.BlockSpec((None, M, n), lambda j, k: (j, 0, 0)))
        out_shape = jax.ShapeDtypeStruct((4, M, n), _ACT)
    else:
        out_spec = pl.BlockSpec((M, tn), lambda j, k: (0, j))
        out_shape = jax.ShapeDtypeStruct((M, N), _ACT)
    outs, extra = _call(
        kern, comm, name=name, grid=(N // tn, nk),
        in_specs=[pl.BlockSpec((tk, M), lambda j, k: (k, 0)), pl.BlockSpec((tk, tn), lambda j, k: (k, j))],
        out_specs=[out_spec], out_shape=[out_shape], scratch_shapes=[pltpu.VMEM((M, tn), _F32)],
        args=(a, b), semantics=("parallel", "arbitrary"))
    return outs[0], extra


def _bias_tiles(rel):
    H = rel.shape[0]
    span = _TQ * _BAND_TILES - 1
    edge = span - _REL_CLIP
    gvec = jnp.concatenate([jnp.broadcast_to(rel[:, :1], (H, edge)), rel, jnp.broadcast_to(rel[:, -1:], (H, edge))], axis=1)
    width = _BIAS_TILES * _TQ
    period = width + _TQ
    tiled = jnp.broadcast_to(jnp.pad(gvec[:, ::-1], ((0, 0), (0, 1)))[:, None, :], (H, _TQ, period))
    rows = tiled.reshape(H, _TQ * period)[:, :_TQ * (period - 1)].reshape(H, _TQ, period - 1)[:, :, _TQ - 1:]
    r = jnp.arange(_TQ)[:, None]
    u = jnp.arange(width)[None, :]
    d = 4 * _TQ + r - u
    rm = r % _CHUNK
    valid = (d >= rm - (_CHUNK - 1)) & (d <= rm + 8 * _CHUNK)
    tiles = jnp.where(valid[None], rows, _MASKED)
    return tiles.reshape(H // 2, 2 * _TQ, _BIAS_TILES, _TQ).transpose(0, 2, 1, 3)


def _fold_bias_grad(db):
    H = 2 * db.shape[0]
    width = _BIAS_TILES * _TQ
    period = width + _TQ
    x = jnp.pad(db.transpose(0, 2, 1, 3).reshape(H, _TQ, width), ((0, 0), (0, 0), (_TQ - 1, 0)))
    skew = jnp.pad(x.reshape(H, _TQ * (period - 1)), ((0, 0), (0, _TQ))).reshape(H, _TQ, period)
    dg = skew.sum(axis=1)[:, :period - 1][:, ::-1]
    span = _TQ * _BAND_TILES - 1
    edge = span - _REL_CLIP
    mid = dg[:, edge:edge + 2 * _REL_CLIP + 1]
    lo = dg[:, :edge].sum(axis=1)
    hi = dg[:, edge + 2 * _REL_CLIP + 1:].sum(axis=1)
    return mid.at[:, 0].add(lo).at[:, -1].add(hi)


def _band_window(i):
    j0 = jnp.maximum(i - (_BAND_TILES - 1), 0)
    return j0, (_BAND_TILES - 1) - (i - j0)


def _head_masks():
    lane = lax.broadcasted_iota(jnp.int32, (1, _LANES), 1)
    return [(lane // _HEAD) == hh for hh in range(2)]


def _stack_heads(x, masks):
    return jnp.concatenate([jnp.where(m, x, jnp.zeros_like(x)) for m in masks], axis=0)


def _unstack_heads(y, masks):
    return jnp.where(masks[0], y[:_TQ], y[_TQ:])


def _scaled(q):
    return q * jnp.asarray(_HEAD ** -0.5, q.dtype)


def _band_probs(q2, k_ref, b_ref, j0, boff):
    s = []
    for j in range(_BAND_TILES):
        kj = k_ref[pl.ds(pl.multiple_of((j0 + j) * _TQ, _TQ), _TQ), :]
        s.append(_dot_nt(q2, kj) + b_ref[boff + j])
    m = jnp.max(functools.reduce(jnp.maximum, s), axis=-1, keepdims=True)
    p = [jnp.exp(x - m) for x in s]
    l = jnp.sum(functools.reduce(lambda a, b: a + b, p), axis=-1, keepdims=True)
    return p, 1.0 / l


def _qkv_specs(T, cb, npair, tq=_TQ):
    return [pl.BlockSpec((tq, _LANES), lambda h, i: (i, cb + h)),
            pl.BlockSpec((T, _LANES), lambda h, i: (0, cb + npair + h)),
            pl.BlockSpec((T, _LANES), lambda h, i: (0, cb + 2 * npair + h))]


def _attn_a_fwd(hq, bias, col0, width, layer, comm=None):
    T = hq.shape[0]
    npair = width // _LANES
    nsub = _BAND_SUBTILES
    tq = nsub * _TQ

    def kern(q_ref, k_ref, v_ref, b_ref, o_ref):
        masks = _head_masks()
        q = _scaled(q_ref[...])
        for s in range(nsub):
            part = slice(s * _TQ, (s + 1) * _TQ)
            j0, boff = _band_window(nsub * pl.program_id(1) + s)
            p, inv = _band_probs(_stack_heads(q[part], masks), k_ref, b_ref, j0, boff)
            o = jnp.zeros((2 * _TQ, _LANES), _F32)
            for j in range(_BAND_TILES):
                vj = v_ref[pl.ds(pl.multiple_of((j0 + j) * _TQ, _TQ), _TQ), :]
                o = o + _dot(p[j].astype(_MXU), vj)
            o_ref[part, :] = _unstack_heads(o * inv, masks).astype(o_ref.dtype)

    outs, extra = _call(
        kern, comm, name=f"band_attn_fwd_{layer}", grid=(npair, T // tq),
        in_specs=_qkv_specs(T, col0 // _LANES, npair, tq)
        + [pl.BlockSpec((None, _BIAS_TILES, 2 * _TQ, _TQ), lambda h, i: (h, 0, 0, 0))],
        out_specs=[pl.BlockSpec((tq, _LANES), lambda h, i: (i, h))],
        out_shape=[jax.ShapeDtypeStruct((T, width), _ACT)], scratch_shapes=[],
        args=(hq, hq, hq, bias), semantics=("arbitrary", "arbitrary"))
    return outs[0], extra


def _attn_a_bwd(hq, bias, do, col0, width, layer, comm=None):
    T = hq.shape[0]
    npair = width // _LANES
    nsub = _BAND_SUBTILES
    tq = nsub * _TQ
    nq = T // tq
    scale = _HEAD ** -0.5

    def kern(q_ref, k_ref, v_ref, b_ref, do_ref, dq_ref, dk_ref, dv_ref, db_ref, dk_acc, dv_acc):
        i = pl.program_id(1)

        @pl.when(i == 0)
        def _():
            dk_acc[...] = jnp.zeros_like(dk_acc)
            dv_acc[...] = jnp.zeros_like(dv_acc)
            db_ref[...] = jnp.zeros_like(db_ref)

        masks = _head_masks()
        q = _scaled(q_ref[...])
        do_t = do_ref[...]
        for s in range(nsub):
            part = slice(s * _TQ, (s + 1) * _TQ)
            j0, boff = _band_window(nsub * i + s)
            q2 = _stack_heads(q[part], masks)
            do2 = _stack_heads(do_t[part], masks).astype(_MXU)
            p, inv = _band_probs(q2, k_ref, b_ref, j0, boff)
            rows = [pl.ds(pl.multiple_of((j0 + j) * _TQ, _TQ), _TQ) for j in range(_BAND_TILES)]
            p = [x * inv for x in p]
            dp = [_dot_nt(do2, v_ref[rows[j], :]) for j in range(_BAND_TILES)]
            delta = jnp.sum(functools.reduce(lambda a, b: a + b, [p[j] * dp[j] for j in range(_BAND_TILES)]),
                            axis=-1, keepdims=True)
            dq = jnp.zeros((2 * _TQ, _LANES), _F32)
            for j in range(_BAND_TILES):
                ds = p[j] * (dp[j] - delta)
                db_ref[boff + j] += ds
                dsb = ds.astype(_MXU)
                dq = dq + _dot(dsb, k_ref[rows[j], :])
                dk_acc[rows[j], :] += _dot_tn(dsb, q2)
                dv_acc[rows[j], :] += _dot_tn(p[j].astype(_MXU), do2)
            dq_ref[part, :] = (_unstack_heads(dq, masks) * scale).astype(dq_ref.dtype)

        @pl.when(i == nq - 1)
        def _():
            dk_ref[...] = dk_acc[...].astype(dk_ref.dtype)
            dv_ref[...] = dv_acc[...].astype(dv_ref.dtype)

    strip = pl.BlockSpec((None, _BIAS_TILES, 2 * _TQ, _TQ), lambda h, i: (h, 0, 0, 0))
    tile = pl.BlockSpec((tq, _LANES), lambda h, i: (i, h))
    column = pl.BlockSpec((T, _LANES), lambda h, i: (0, h))
    outs, extra = _call(
        kern, comm, name=f"band_attn_bwd_{layer}", grid=(npair, nq),
        in_specs=_qkv_specs(T, col0 // _LANES, npair, tq) + [strip, tile],
        out_specs=[tile, column, column, strip],
        out_shape=[jax.ShapeDtypeStruct((T, width), _ACT)] * 3
        + [jax.ShapeDtypeStruct((npair, _BIAS_TILES, 2 * _TQ, _TQ), _F32)],
        scratch_shapes=[pltpu.VMEM((T, _LANES), _F32), pltpu.VMEM((T, _LANES), _F32)],
        args=(hq, hq, hq, bias, do), semantics=("arbitrary", "arbitrary"))
    return outs, extra


def _suffix_matrix():
    r = lax.broadcasted_iota(jnp.int32, (_TQ, _TQ), 0)
    c = lax.broadcasted_iota(jnp.int32, (_TQ, _TQ), 1)
    r2 = lax.broadcasted_iota(jnp.int32, (2 * _TQ, _TQ), 0)
    c2 = lax.broadcasted_iota(jnp.int32, (2 * _TQ, _TQ), 1)
    return (r > c).astype(_MXU), c2 - (r2 & (_TQ - 1))


def _suffix_sums(xs, tri):
    n, k = xs[0].shape[0], len(xs)
    his = [x.astype(_MXU) for x in xs]
    los = [(x - h.astype(_F32)).astype(_MXU) for x, h in zip(xs, his)]
    y = _dot(jnp.concatenate(his + los, axis=0), tri)
    return [y[j * n:(j + 1) * n] + y[(k + j) * n:(k + j + 1) * n] for j in range(k)]


def _stick_tiles(tiles, rel, carry_l, tri):
    zs = [_dot_nt(qs, kj) for qs, kj, _, _ in tiles]
    Ls, masks = [], []
    for z, (_, _, jj, _) in zip(zs, tiles):
        nsp = -(jnp.maximum(z, 0.0) + jnp.log(1.0 + jnp.exp(-jnp.abs(z))))
        if isinstance(jj, int):
            mask = (rel < 0) if jj == 0 else None
        else:
            mask = rel < jnp.where(jj == 0, 0, _TQ)
        Ls.append(nsp if mask is None else jnp.where(mask, nsp, 0.0))
        masks.append(mask)
    carry_l = list(carry_l)
    ws = []
    for z, L, suffix, mask, (_, _, _, sub) in zip(zs, Ls, _suffix_sums(Ls, tri), masks, tiles):
        w = jnp.exp(z + L + suffix + carry_l[sub])
        ws.append(w if mask is None else jnp.where(mask, w, 0.0))
        carry_l[sub] = carry_l[sub] + jnp.sum(L, axis=-1, keepdims=True)
    return zs, Ls, ws, masks, carry_l


def _sweep(i, step, zero):
    nsub = _SB_SUBTILES

    def window():
        tiles = [(s, jj) for jj in range(_SB_WINDOW) for s in range(nsub)]
        return tuple((jnp.int32(_SB_WINDOW),) + c for c in step(tiles, [zero] * nsub))

    start = lax.cond(i >= -(-(_SB_WINDOW - 1) // nsub), window, lambda: tuple((jnp.int32(0),) + zero for _ in range(nsub)))
    outs = []
    for s in range(nsub):
        def done(c, s=s):
            return jnp.logical_or(c[0] > nsub * i + s, jnp.max(c[1]) < _EXP_ZERO_BELOW)

        def more(c, s=s):
            carries = [None] * nsub
            carries[s] = c[1:]
            return (c[0] + 1,) + step([(s, c[0])], carries)[s]

        outs.append(lax.while_loop(lambda c, done=done: jnp.logical_not(done(c)), more, start[s]))
    return outs


def _sb_fwd(hq, col0, width, layer, comm=None):
    T = hq.shape[0]
    npair = width // _LANES
    nsub = _SB_SUBTILES
    tq = nsub * _TQ

    def kern(q_ref, k_ref, v_ref, o_ref):
        i = pl.program_id(1)
        masks = _head_masks()
        tri, rel = _suffix_matrix()
        q = _scaled(q_ref[...])
        q2 = [_stack_heads(q[s * _TQ:(s + 1) * _TQ], masks) for s in range(nsub)]

        def step(tiles, carries):
            rows = [pl.ds(pl.multiple_of((nsub * i + s - jj) * _TQ, _TQ), _TQ) for s, jj in tiles]
            cls = [None if c is None else c[0] for c in carries]
            accs = [None if c is None else c[1] for c in carries]
            _, _, ws, _, cls = _stick_tiles([(q2[s], k_ref[r, :], jj, s) for (s, jj), r in zip(tiles, rows)], rel, cls, tri)
            for w, r, (s, _) in zip(ws, rows, tiles):
                accs[s] = accs[s] + _dot(w.astype(_MXU), v_ref[r, :])
            return [None if c is None else (cls[s], accs[s]) for s, c in enumerate(carries)]

        outs = _sweep(i, step, (jnp.zeros((2 * _TQ, 1), _F32), jnp.zeros((2 * _TQ, _LANES), _F32)))
        for s in range(nsub):
            o_ref[s * _TQ:(s + 1) * _TQ, :] = _unstack_heads(outs[s][2], masks)

    outs, extra = _call(
        kern, comm, name=f"stick_attn_fwd_{layer}", grid=(npair, T // tq),
        in_specs=_qkv_specs(T, col0 // _LANES, npair, tq),
        out_specs=[pl.BlockSpec((tq, _LANES), lambda h, i: (i, h))],
        out_shape=[jax.ShapeDtypeStruct((T, width), _F32)], scratch_shapes=[],
        args=(hq, hq, hq), semantics=("arbitrary", "arbitrary"))
    return outs[0], extra


def _sb_bwd(hq, o, do, col0, width, layer, comm=None):
    T = hq.shape[0]
    npair = width // _LANES
    nsub = _SB_SUBTILES
    tq = nsub * _TQ
    nq = T // tq
    scale = _HEAD ** -0.5

    def kern(q_ref, k_ref, v_ref, o_ref, do_ref, dq_ref, dk_ref, dv_ref, dk_acc, dv_acc):
        i = pl.program_id(1)

        @pl.when(i == 0)
        def _():
            dk_acc[...] = jnp.zeros_like(dk_acc)
            dv_acc[...] = jnp.zeros_like(dv_acc)

        masks = _head_masks()
        tri, rel = _suffix_matrix()
        q = _scaled(q_ref[...])
        do_t = do_ref[...]
        prod = do_t.astype(_F32) * o_ref[...]
        part = [slice(s * _TQ, (s + 1) * _TQ) for s in range(nsub)]
        q2 = [_stack_heads(q[p], masks) for p in part]
        do2 = [_stack_heads(do_t[p], masks).astype(_MXU) for p in part]
        dsum = [jnp.sum(_stack_heads(prod[p], masks), axis=-1, keepdims=True) for p in part]

        def step(tiles, carries):
            rows = [pl.ds(pl.multiple_of((nsub * i + s - jj) * _TQ, _TQ), _TQ) for s, jj in tiles]
            kjs = [k_ref[r, :] for r in rows]
            cls, cgs, dqs = ([None if c is None else c[n] for c in carries] for n in range(3))
            zs, Ls, ws, tile_masks, cls = _stick_tiles([(q2[s], kj, jj, s) for (s, jj), kj in zip(tiles, kjs)], rel, cls, tri)
            wbs = [w.astype(_MXU) for w in ws]
            gs = [wb.astype(_F32) * _dot_nt(do2[s], v_ref[r, :]) for wb, r, (s, _) in zip(wbs, rows, tiles)]
            for z, L, g, later, mask, wb, kj, r, (s, _) in zip(zs, Ls, gs, _suffix_sums(gs, tri), tile_masks, wbs, kjs,
                                                               rows, tiles):
                dz = g - jnp.exp(z + L) * (dsum[s] - (later + cgs[s]))
                if mask is not None:
                    dz = jnp.where(mask, dz, 0.0)
                dzb = dz.astype(_MXU)
                dk_acc[r, :] += _dot_tn(dzb, q2[s])
                dv_acc[r, :] += _dot_tn(wb, do2[s])
                dqs[s] = dqs[s] + _dot(dzb, kj)
                cgs[s] = cgs[s] + jnp.sum(g, axis=-1, keepdims=True)
            return [None if c is None else (cls[s], cgs[s], dqs[s]) for s, c in enumerate(carries)]

        zc = jnp.zeros((2 * _TQ, 1), _F32)
        outs = _sweep(i, step, (zc, zc, jnp.zeros((2 * _TQ, _LANES), _F32)))
        for s in range(nsub):
            dq_ref[part[s], :] = (_unstack_heads(outs[s][3], masks) * scale).astype(dq_ref.dtype)

        @pl.when(i == nq - 1)
        def _():
            dk_ref[...] = dk_acc[...].astype(dk_ref.dtype)
            dv_ref[...] = dv_acc[...].astype(dv_ref.dtype)

    tile_spec = pl.BlockSpec((tq, _LANES), lambda h, i: (i, h))
    column = pl.BlockSpec((T, _LANES), lambda h, i: (0, h))
    outs, extra = _call(
        kern, comm, name=f"stick_attn_bwd_{layer}", grid=(npair, nq),
        in_specs=_qkv_specs(T, col0 // _LANES, npair, tq) + [tile_spec, tile_spec],
        out_specs=[tile_spec, column, column],
        out_shape=[jax.ShapeDtypeStruct((T, width), _ACT)] * 3,
        scratch_shapes=[pltpu.VMEM((T, _LANES), _F32), pltpu.VMEM((T, _LANES), _F32)],
        args=(hq, hq, hq, o, do), semantics=("arbitrary", "arbitrary"))
    return outs, extra


_DENSE = ("w_in", "w_proj_a", "w_proj_b", "w_out", "w_ffn_in", "w_ffn_out")
_COL_SHARDED = {"w_in": True, "w_proj_a": True, "w_proj_b": True, "w_out": False, "w_ffn_in": True, "w_ffn_out": False}
_SMALL = ("b_gate", "rel_bias", "ln1_g", "ln1_b", "ln2_g", "ln2_b")


class _Plans:
    def __init__(self, plans=None):
        self.plans = plans or {}

    def start(self, key):
        if key not in self.plans:
            return None, None
        return self.plans[key]()

    @staticmethod
    def finish(done, extra):
        if done is not None:
            done(extra)


def _layer_fwd(x, W, small, l, alpha, plans):
    WA = small["rel_bias"].shape[1] * _HEAD
    row = lambda v: v[l].reshape(1, -1)
    hq, hg, xb = _in_proj(x, W["w_in"], l)
    WB = (hq.shape[1] - 3 * WA) // 3
    bias = _bias_tiles(small["rel_bias"][l])
    comm, done = plans.start(f"band_fwd_{l}")
    oa, extra = _attn_a_fwd(hq, bias, 0, WA, l, comm)
    plans.finish(done, extra)
    comm, done = plans.start(f"stick_fwd_{l}")
    ob, extra = _sb_fwd(hq, 3 * WA, WB, l, comm)
    plans.finish(done, extra)
    x1, u1, pre, ya, yb = _mix_fwd(oa, ob, hg, x, W["w_proj_a"], W["w_proj_b"], W["w_out"],
                                   row(small["b_gate"]), row(small["ln1_g"]), row(small["ln1_b"]), alpha, l)
    comm, done = plans.start(f"ffn_fwd_{l}")
    (x2, u2, act, gu, x1b), extra = _ffn_fwd(x1, W["w_ffn_in"], W["w_ffn_out"], row(small["ln2_g"]),
                                             row(small["ln2_b"]), alpha, l, comm)
    plans.finish(done, extra)
    return x2, dict(xb=xb, hq=hq, hg=hg, bias=bias, oa=oa, ob=ob, x1b=x1b, u1=u1, pre=pre, ya=ya, yb=yb, u2=u2,
                    act=act, gu=gu)


def _layer_bwd(dy_or_target, S, W, small, l, last, alpha, plans, gw):
    D = S["xb"].shape[1]
    WA, WB = S["oa"].shape[1], S["ob"].shape[1]
    row = lambda v: v[l].reshape(1, -1)

    def blocks(g, n):
        return g if _COL_SHARDED[n] else g.reshape(4, g.shape[0] // 4, g.shape[1])

    dx1, du2b, dgu, st2 = _ffn_bwd(S["u2"], dy_or_target, S["gu"], row(small["ln2_g"]), row(small["ln2_b"]),
                                   W["w_ffn_in"], W["w_ffn_out"], alpha, l, last)
    gw["w_ffn_in"] = blocks(_grad_w(S["x1b"], dgu, col_shards=True, name=f"grad_w_ffn_in_{l}")[0], "w_ffn_in")
    gw["w_ffn_out"] = blocks(_grad_w(S["act"], du2b, col_shards=False, name=f"grad_w_ffn_out_{l}")[0], "w_ffn_out")
    du1, du1b, dya, dyb, dhg, doa, dob, st1 = _mix_bwd(S["u1"], dx1, S["ya"], S["yb"], S["hg"], W["w_proj_a"],
                                                       W["w_proj_b"], W["w_out"], row(small["b_gate"]),
                                                       row(small["ln1_g"]), alpha, l)
    gw["w_out"] = blocks(_grad_w(S["pre"], du1b, col_shards=False, name=f"grad_w_out_{l}")[0], "w_out")
    gw["w_proj_a"] = blocks(_grad_w(S["oa"], dya, col_shards=True, name=f"grad_w_proj_a_{l}")[0], "w_proj_a")
    gw["w_proj_b"] = blocks(_grad_w(S["ob"], dyb, col_shards=True, name=f"grad_w_proj_b_{l}")[0], "w_proj_b")
    comm, done = plans.start(f"band_bwd_{l}")
    (dqa, dka, dva, dbias), extra = _attn_a_bwd(S["hq"], S["bias"], doa, 0, WA, l, comm)
    plans.finish(done, extra)
    comm, done = plans.start(f"stick_bwd_{l}")
    (dqb, dkb, dvb), extra = _sb_bwd(S["hq"], S["ob"], dob, 3 * WA, WB, l, comm)
    plans.finish(done, extra)
    dh = jnp.concatenate([dqa, dka, dva, dqb, dkb, dvb, dhg], axis=1)
    comm, done = plans.start(f"grad_w_in_{l}")
    g, extra = _grad_w(S["xb"], dh, col_shards=True, name=f"grad_w_in_{l}", comm=comm)
    gw["w_in"] = blocks(g, "w_in")
    plans.finish(done, extra)
    comm, done = plans.start(f"in_proj_bwd_{l}")
    dx, extra = _residual_nt(du1, alpha, dh, W["w_in"], f"in_proj_bwd_{l}", comm)
    plans.finish(done, extra)
    gs = dict(b_gate=st1[0], rel_bias=_fold_bias_grad(dbias), ln1_g=st1[1, :D], ln1_b=st1[1, D:],
              ln2_g=st2[0], ln2_b=st2[1])
    return dx, gs, st2[2]


def _local_step(x, target, W, small, plans=None, gws=None):
    depth = len(W)
    alpha = float((2 * depth) ** 0.25)
    plans = plans or _Plans()
    gws = gws if gws is not None else [dict() for _ in range(depth)]
    saved = []
    h = x
    for l in range(depth):
        h, S = _layer_fwd(h, W[l], small, l, alpha, plans)
        saved.append(S)
    gss = [None] * depth
    d = target
    sq = None
    for l in reversed(range(depth)):
        d, gss[l], sq_l = _layer_bwd(d, saved[l], W[l], small, l, l == depth - 1, alpha, plans, gws[l])
        if l == depth - 1:
            sq = sq_l
    return sq, d, gws, gss


def _place():
    return lax.axis_index("x"), lax.axis_index("y"), lax.axis_index("c")


def _remote(src, dst, send_sem, recv_sem, to):
    return pltpu.make_async_remote_copy(src_ref=src, dst_ref=dst, send_sem=send_sem, recv_sem=recv_sem,
                                        device_id=to, device_id_type=_MESH)


def _half(ref, hc):
    kh = ref.shape[0] // 2
    return ref.at[pl.ds(pl.multiple_of(hc * kh, 16), kh), :]


def _gather_plan(blocks, fractions):
    nt = len(blocks)

    def run(step, nsteps, ins, outs, sems):
        send_sems, recv_sems, loc_sems = sems
        x, y, c = _place()
        k = 2 * x + y
        me, sibling = (x, y, c), (x, y, 1 - c)
        chips = [(1 - x, y), (x, 1 - y), (1 - x, 1 - y)]
        chip_k = [2 * cx + cy for cx, cy in chips]

        def ici(t, s, owner_k, to, src=None):
            dst = _half(outs[t].at[owner_k], c)
            return _remote(dst if src is None else src, dst, send_sems.at[t, s], recv_sems.at[t, s], to)

        def passed(t, s, hc, to):
            blk = _half(outs[t].at[chip_k[s]], hc)
            return _remote(blk, blk, send_sems.at[t, 3 + s], recv_sems.at[t, 3 + s], to)

        def local(t):
            return pltpu.make_async_copy(ins[t], outs[t].at[k], loc_sems.at[t])

        @pl.when(step == 0)
        def _():
            for t in range(nt):
                local(t).start()
                for s, chip in enumerate(chips):
                    ici(t, s, k, (*chip, c), src=_half(ins[t], c)).start()

        for t in range(nt):
            @pl.when(step == min(nsteps - 1, int(fractions[t] * nsteps)))
            def _():
                for s in range(3):
                    ici(t, s, chip_k[s], me).wait_recv()
                    passed(t, s, c, sibling).start()

        @pl.when(step == nsteps - 1)
        def _():
            for t in range(nt):
                for s, chip in enumerate(chips):
                    passed(t, s, 1 - c, me).wait_recv()
            for t in range(nt):
                for s, chip in enumerate(chips):
                    ici(t, s, k, (*chip, c), src=_half(ins[t], c)).wait_send()
                    passed(t, s, c, sibling).wait_send()
                local(t).wait()

    return _Comm(blocks, [jax.ShapeDtypeStruct((4,) + b.shape, b.dtype) for b in blocks],
                 [pltpu.SemaphoreType.DMA((nt, 6)), pltpu.SemaphoreType.DMA((nt, 6)), pltpu.SemaphoreType.DMA((nt,))], run)


def _scatter_plan(grads, owners):
    nt = len(grads)

    def run(step, nsteps, ins, outs, sems):
        send_sems, recv_sems, loc_sems = sems
        x, y, c = _place()
        me = 4 * x + 2 * y + c

        def target(r):
            tx = 1 - x if r & 2 else x
            ty = 1 - y if r & 1 else y
            return tx, ty

        def send(t, r):
            tx, ty = target(r)
            return _remote(ins[t].at[2 * tx + ty], outs[t].at[me], send_sems.at[t, r], recv_sems.at[t, 2 * r + c],
                           (tx, ty, owners[t]))

        def local(t):
            return pltpu.make_async_copy(ins[t].at[2 * x + y], outs[t].at[me], loc_sems.at[t])

        @pl.when(step == 0)
        def _():
            for t in range(nt):
                @pl.when(c == owners[t])
                def _():
                    local(t).start()

                @pl.when(c != owners[t])
                def _():
                    send(t, 0).start()

                for r in range(1, 4):
                    send(t, r).start()

        @pl.when(step == nsteps - 1)
        def _():
            for t in range(nt):
                @pl.when(c == owners[t])
                def _():
                    for r in range(4):
                        sx, sy = target(r)
                        for cs in range(2):
                            if r == 0 and cs == owners[t]:
                                continue
                            src_dev = 4 * sx + 2 * sy + cs
                            _remote(ins[t].at[0], outs[t].at[src_dev], send_sems.at[t, r], recv_sems.at[t, 2 * r + cs],
                                    (x, y, c)).wait_recv()
                    local(t).wait()

                @pl.when(c != owners[t])
                def _():
                    send(t, 0).wait_send()

                for r in range(1, 4):
                    send(t, r).wait_send()

    return _Comm(grads, [jax.ShapeDtypeStruct((8,) + g.shape[1:], g.dtype) for g in grads],
                 [pltpu.SemaphoreType.DMA((nt, 4)), pltpu.SemaphoreType.DMA((nt, 8)), pltpu.SemaphoreType.DMA((nt,))], run)


def _share_plan(reduced, owners):
    nt = len(reduced)

    def run(step, nsteps, ins, outs, sems):
        del ins
        send_sems, recv_sems = sems
        x, y, c = _place()

        def give(t, to):
            return _remote(outs[t], outs[t], send_sems.at[t], recv_sems.at[t], to)

        @pl.when(step == 0)
        def _():
            for t in range(nt):
                @pl.when(c == owners[t])
                def _():
                    give(t, (x, y, 1 - c)).start()

        @pl.when(step == nsteps - 1)
        def _():
            for t in range(nt):
                @pl.when(c == owners[t])
                def _():
                    give(t, (x, y, 1 - c)).wait_send()

                @pl.when(c != owners[t])
                def _():
                    give(t, (x, y, c)).wait_recv()

    return _Comm(reduced, [jax.ShapeDtypeStruct(r.shape, r.dtype) for r in reduced],
                 [pltpu.SemaphoreType.DMA((nt,)), pltpu.SemaphoreType.DMA((nt,))], run,
                 aliases={t: t for t in range(nt)})


def _join(a, b):
    ni, no, ns = len(a.inputs), len(a.out_shapes), len(a.sems)

    def run(step, nsteps, ins, outs, sems):
        a.run(step, nsteps, ins[:ni], outs[:no], sems[:ns])
        b.run(step, nsteps, ins[ni:], outs[no:], sems[ns:])

    aliases = dict(a.aliases)
    aliases.update({ni + i: no + o for i, o in b.aliases.items()})
    return _Comm(a.inputs + b.inputs, a.out_shapes + b.out_shapes, a.sems + b.sems, run, aliases)


def _peer(x, y, c, r):
    px = 1 - x if r & 4 else x
    py = 1 - y if r & 2 else y
    pc = 1 - c if r & 1 else c
    return (px, py, pc), 4 * px + 2 * py + pc


def _sum_slots(st, name):
    _, K, n = st.shape
    tr = next(t for t in (256, 128, 64, 32, 16) if K % t == 0)

    def kern(s_ref, o_ref):
        acc = s_ref[0].astype(_F32)
        for d in range(1, 8):
            acc = acc + s_ref[d].astype(_F32)
        o_ref[...] = acc.astype(o_ref.dtype)

    return pl.pallas_call(
        kern, name=name, grid=(K // tr,),
        in_specs=[pl.BlockSpec((8, tr, n), lambda i: (0, i, 0))], out_specs=_rows(tr, n),
        out_shape=jax.ShapeDtypeStruct((K, n), _ACT),
        compiler_params=_cparams("parallel"),
    )(st)


def _all_reduce_small(p):
    R = p.shape[0]

    def body(p_ref, o_ref, stage, send_sems, recv_sems):
        x, y, c = _place()
        me = 4 * x + 2 * y + c
        stage[me] = p_ref[...]
        sent = []
        for r in range(1, 8):
            to, _ = _peer(x, y, c, r)
            cp = _remote(p_ref, stage.at[me], send_sems.at[r - 1], recv_sems.at[r - 1], to)
            cp.start()
            sent.append(cp)
        for r in range(1, 8):
            _, src_dev = _peer(x, y, c, r)
            _remote(p_ref, stage.at[src_dev], send_sems.at[r - 1], recv_sems.at[r - 1], (x, y, c)).wait_recv()
        acc = stage[0]
        for d in range(1, 8):
            acc = acc + stage[d]
        o_ref[...] = acc
        for cp in sent:
            cp.wait_send()

    vm = pl.BlockSpec(memory_space=pltpu.VMEM)
    return pl.pallas_call(
        body, name="all_reduce_small",
        in_specs=[vm], out_specs=vm,
        out_shape=jax.ShapeDtypeStruct((R, _LANES), _F32),
        scratch_shapes=[pltpu.VMEM((8, R, _LANES), _F32), pltpu.SemaphoreType.DMA((7,)), pltpu.SemaphoreType.DMA((7,))],
    )(p)


def _adamw(w, g, m, v, name):
    shape = w.shape
    w2, g2, m2, v2 = (a.reshape(-1, shape[-1]) for a in (w, g, m, v))
    R, C = w2.shape
    tr = next((t for t in (256, 128, 64, 32, 16) if R % t == 0), R)

    def kern(w_ref, g_ref, m_ref, v_ref, gf_ref, d_ref, nm_ref, nv_ref):
        gv = g_ref[...].astype(_F32)
        nm = _B1 * m_ref[...] + (1.0 - _B1) * gv
        nv = _B2 * v_ref[...] + (1.0 - _B2) * (gv * gv)
        m_hat = nm / (1.0 - _B1 ** _STEP)
        v_hat = nv / (1.0 - _B2 ** _STEP)
        gf_ref[...] = gv
        d_ref[...] = -_LR * (m_hat / (jnp.sqrt(v_hat) + _EPS) + _WD * w_ref[...])
        nm_ref[...] = nm
        nv_ref[...] = nv

    outs = pl.pallas_call(
        kern, name=name, grid=(R // tr,),
        in_specs=[_rows(tr, C)] * 4, out_specs=[_rows(tr, C)] * 4,
        out_shape=[jax.ShapeDtypeStruct((R, C), _F32)] * 4,
        compiler_params=_cparams("parallel"),
    )(w2, g2, m2, v2)
    return tuple(o.reshape(shape) for o in outs)


def _pack_small(gss, sq):
    parts = [gss[l][n].reshape(-1) for n in _SMALL for l in range(len(gss))] + [jnp.sum(sq).reshape(1)]
    flat = jnp.concatenate(parts)
    rows = -(-flat.shape[0] // (8 * _LANES)) * 8
    return jnp.pad(flat, (0, rows * _LANES - flat.shape[0])).reshape(rows, _LANES)


def _unpack_small(total, shapes):
    flat = total.reshape(-1)
    out, off = {}, 0
    for n in _SMALL:
        layers = []
        for _ in range(shapes[n][0]):
            size = 1
            for s in shapes[n][1:]:
                size *= s
            layers.append(flat[off:off + size].reshape(shapes[n][1:]))
            off += size
        out[n] = jnp.stack(layers)
    return out, flat[off]


_GATHER = {
    "band_fwd_0": [(0, "w_proj_a"), (0, "w_proj_b"), (0, "w_out"), (0, "w_ffn_out")],
    "stick_fwd_0": [(0, "w_ffn_in"), (1, "w_in")],
    "ffn_fwd_0": [(1, "w_proj_a"), (1, "w_proj_b"), (1, "w_out"), (1, "w_ffn_in"), (1, "w_ffn_out")],
}
_SCATTER = {
    "band_bwd_1": [(1, "w_ffn_in"), (1, "w_ffn_out")],
    "stick_bwd_1": [(1, "w_proj_a"), (1, "w_proj_b"), (1, "w_out")],
    "band_bwd_0": [(1, "w_in"), (0, "w_ffn_in")],
    "stick_bwd_0": [(0, "w_ffn_out"), (0, "w_proj_a"), (0, "w_proj_b"), (0, "w_out")],
    "in_proj_bwd_0": [(0, "w_in")],
}
_SHARE = {"stick_bwd_1": "band_bwd_1", "band_bwd_0": "stick_bwd_1", "stick_bwd_0": "band_bwd_0", "grad_w_in_0": "stick_bwd_0"}


def _owner(key):
    del key
    return 1


def kernel(x, w_in, b_gate, rel_bias, w_proj_a, w_proj_b, w_out, ln1_g, ln1_b, w_ffn_in, w_ffn_out, ln2_g, ln2_b, loss_target, m_w_in, m_b_gate, m_rel_bias, m_w_proj_a, m_w_proj_b, m_w_out, m_ln1_g, m_ln1_b, m_w_ffn_in, m_w_ffn_out, m_ln2_g, m_ln2_b, v_w_in, v_b_gate, v_rel_bias, v_w_proj_a, v_w_proj_b, v_w_out, v_ln1_g, v_ln1_b, v_w_ffn_in, v_w_ffn_out, v_ln2_g, v_ln2_b):
    names = ("w_in", "b_gate", "rel_bias", "w_proj_a", "w_proj_b", "w_out", "ln1_g", "ln1_b", "w_ffn_in", "w_ffn_out", "ln2_g", "ln2_b")
    w = dict(zip(names, (w_in, b_gate, rel_bias, w_proj_a, w_proj_b, w_out, ln1_g, ln1_b, w_ffn_in, w_ffn_out, ln2_g, ln2_b)))
    m = dict(zip(names, (m_w_in, m_b_gate, m_rel_bias, m_w_proj_a, m_w_proj_b, m_w_out, m_ln1_g, m_ln1_b, m_w_ffn_in, m_w_ffn_out, m_ln2_g, m_ln2_b)))
    v = dict(zip(names, (v_w_in, v_b_gate, v_rel_bias, v_w_proj_a, v_w_proj_b, v_w_out, v_ln1_g, v_ln1_b, v_w_ffn_in, v_w_ffn_out, v_ln2_g, v_ln2_b)))
    T, D = x.shape[-2], x.shape[-1]
    assert w_in.shape[0] == 2, "the exchange schedule below is written for two layers"

    mine = [{n: w[n][l].astype(_MXU) for n in _DENSE} for l in range(2)]
    W = [dict(), dict()]
    gws = [dict(), dict()]
    slots, final = {}, {}

    def gather(keys):
        sizes = [mine[l][n].size for l, n in keys]
        passed, fractions = 0, []
        for s in sizes:
            passed += s
            fractions.append(0.15 + 0.6 * passed / sum(sizes))

        def done(outs):
            for (l, n), o in zip(keys, outs):
                W[l][n] = o
        return _gather_plan([mine[l][n] for l, n in keys], fractions), done

    def scatter(keys):
        comm = _scatter_plan([gws[l][n] for l, n in keys], [_owner(key) for key in keys])
        return comm, lambda outs: slots.update(zip(keys, outs))

    def share(keys):
        reduced = [_sum_slots(slots[key], f"sum_grad_{key[1]}_{key[0]}") for key in keys]
        comm = _share_plan(reduced, [_owner(key) for key in keys])
        return comm, lambda outs: final.update(zip(keys, outs))

    def both(first, second):
        (ca, da), (cb, db) = first, second
        na = len(ca.out_shapes)
        return _join(ca, cb), lambda outs: (da(outs[:na]), db(outs[na:]))

    comm, done = gather([(0, "w_in")])
    done(_comm_only(comm, "gather_first"))
    plans = {key: functools.partial(gather, keys) for key, keys in _GATHER.items()}
    for key, keys in _SCATTER.items():
        plans[key] = functools.partial(scatter, keys)
    for key, scattered_under in _SHARE.items():
        handed = functools.partial(share, _SCATTER[scattered_under])
        carried = plans.get(key)
        plans[key] = handed if carried is None else (lambda carried=carried, handed=handed: both(carried(), handed()))
    small = {n: w[n] for n in _SMALL}
    sq, dx, _, gss = _local_step(x.reshape(T, D), loss_target.reshape(T, D), W, small, _Plans(plans), gws)

    comm, done = share(_SCATTER["in_proj_bwd_0"])
    done(_comm_only(comm, "share_last"))
    grads = {n: jnp.stack([final[(l, n)] for l in range(2)]) for n in _DENSE}

    total = _all_reduce_small(_pack_small(gss, sq))
    small_grads, sq_all = _unpack_small(total, {n: w[n].shape for n in _SMALL})
    grads.update(small_grads)
    loss = 0.5 * sq_all / D

    grad, delta, new_m, new_v = {}, {}, {}, {}
    for n in names:
        grad[n], delta[n], new_m[n], new_v[n] = _adamw(w[n], grads[n].reshape(w[n].shape), m[n], v[n], f"adamw_{n}")
    return (loss, dx.reshape(x.shape), *[grad[n] for n in names], *[delta[n] for n in names],
            *[new_m[n] for n in names], *[new_v[n] for n in names])
```

```python
import functools

import jax
import jax.numpy as jnp
from jax import lax
from jax.experimental import pallas as pl
from jax.experimental.pallas import tpu as pltpu

_MXU = jnp.bfloat16
_ACT = jnp.bfloat16
_F32 = jnp.float32

_HEAD = 64
_CHUNK = 64
_LANES = 128
_TQ = 128
_BAND_TILES = 5
_BIAS_TILES = 9
_REL_CLIP = 256
_LN_EPS = 1e-5
_MASKED = -1e30
_EXP_ZERO_BELOW = -87.34
_SB_WINDOW = 2
_SB_SUBTILES = 4
_BAND_SUBTILES = 4
_VMEM_LIMIT = 56 * 1024 * 1024
_GRAD_ACC_BYTES = 12 * 1024 * 1024

_LR, _B1, _B2, _EPS, _WD, _STEP = 0.001, 0.9, 0.999, 1e-08, 0.01, 10

_MESH = pl.DeviceIdType.MESH


def _dot(a, b):
    return jnp.dot(a, b, preferred_element_type=_F32)


def _dot_nt(a, b):
    return lax.dot_general(a, b, (((1,), (1,)), ((), ())), preferred_element_type=_F32)


def _dot_tn(a, b):
    return lax.dot_general(a, b, (((0,), (0,)), ((), ())), preferred_element_type=_F32)


def _cparams(*sem):
    return pltpu.CompilerParams(dimension_semantics=sem, vmem_limit_bytes=_VMEM_LIMIT)


def _rows(t, c):
    return pl.BlockSpec((t, c), lambda i: (i, 0))


def _whole(shape):
    return pl.BlockSpec(shape, lambda i: tuple(0 for _ in shape))


_ANY = pl.BlockSpec(memory_space=pl.ANY)


def _load_cols(w_hbm, w_vmem, sem):
    n = w_hbm.shape[-1]
    cps = [pltpu.make_async_copy(w_hbm.at[k], w_vmem.at[:, pl.ds(k * n, n)], sem.at[k]) for k in range(4)]
    for cp in cps:
        cp.start()
    for cp in cps:
        cp.wait()


def _load_rows(w_hbm, w_vmem, sem):
    r = w_hbm.shape[-2]
    cps = [pltpu.make_async_copy(w_hbm.at[k], w_vmem.at[pl.ds(k * r, r), :], sem.at[k]) for k in range(4)]
    for cp in cps:
        cp.start()
    for cp in cps:
        cp.wait()


def _ln_stats(u):
    mu = jnp.mean(u, axis=-1, keepdims=True)
    xc = u - mu
    var = jnp.mean(xc * xc, axis=-1, keepdims=True)
    rstd = lax.rsqrt(var + _LN_EPS)
    return xc * rstd, rstd


def _ln_bwd(u, dy, gamma):
    xhat, rstd = _ln_stats(u)
    dxh = dy * gamma
    m1 = jnp.mean(dxh, axis=-1, keepdims=True)
    m2 = jnp.mean(dxh * xhat, axis=-1, keepdims=True)
    du = rstd * (dxh - m1 - xhat * m2)
    return du, jnp.sum(dy * xhat, axis=0, keepdims=True), jnp.sum(dy, axis=0, keepdims=True), xhat


def _divisor_tile(n, cap):
    best = None
    for t in range(_LANES, min(n, cap) + 1, _LANES):
        if n % t == 0:
            best = t
    return best or n


class _Comm:
    def __init__(self, inputs, out_shapes, sems, run, aliases=None):
        self.inputs, self.out_shapes, self.sems, self.run = list(inputs), list(out_shapes), list(sems), run
        self.aliases = aliases or {}


def _call(kern, comm, *, name, grid, in_specs, out_specs, out_shape, scratch_shapes, args, semantics):
    in_specs, out_specs, out_shape, scratch_shapes = list(in_specs), list(out_specs), list(out_shape), list(scratch_shapes)
    if comm is None:
        outs = pl.pallas_call(kern, name=name, grid=grid, in_specs=in_specs, out_specs=out_specs, out_shape=out_shape,
                              scratch_shapes=scratch_shapes, compiler_params=_cparams(*semantics))(*args)
        return list(outs), []
    n_in, n_out, n_scr = len(in_specs), len(out_specs), len(scratch_shapes)
    ci, co = len(comm.inputs), len(comm.out_shapes)
    nsteps = functools.reduce(lambda a, b: a * b, grid, 1)

    def fused(*refs):
        a, b = n_in, n_in + ci
        c, d = b + n_out, b + n_out + co
        e = d + n_scr
        step = pl.program_id(0)
        for ax in range(1, len(grid)):
            step = step * grid[ax] + pl.program_id(ax)
        comm.run(step, nsteps, refs[a:b], refs[c:d], refs[e:])
        kern(*refs[:a], *refs[b:c], *refs[d:e])

    outs = pl.pallas_call(
        fused, name=name, grid=grid, in_specs=in_specs + [_ANY] * ci, out_specs=out_specs + [_ANY] * co,
        out_shape=out_shape + comm.out_shapes, scratch_shapes=scratch_shapes + comm.sems,
        input_output_aliases={n_in + i: n_out + o for i, o in comm.aliases.items()},
        compiler_params=_cparams(*("arbitrary" for _ in grid)))(*args, *comm.inputs)
    return list(outs[:n_out]), list(outs[n_out:])


def _comm_only(comm, name):
    def body(*refs):
        ci, co = len(comm.inputs), len(comm.out_shapes)
        comm.run(0, 1, refs[:ci], refs[ci:ci + co], refs[ci + co:])

    outs = pl.pallas_call(body, name=name, in_specs=[_ANY] * len(comm.inputs), out_specs=[_ANY] * len(comm.out_shapes),
                          out_shape=comm.out_shapes, scratch_shapes=comm.sems,
                          input_output_aliases=dict(comm.aliases))(*comm.inputs)
    return list(outs)


def _in_proj(x, w_in, layer):
    T, D = x.shape
    N = 4 * w_in.shape[-1]
    NQ = N - 2 * D
    tm = 256

    def kern(x_ref, w_hbm, hq_ref, hg_ref, xb_ref, w_v, sem):
        @pl.when(pl.program_id(0) == 0)
        def _():
            _load_cols(w_hbm, w_v, sem)

        xb = x_ref[...].astype(_MXU)
        hq_ref[...] = _dot(xb, w_v[:, :NQ]).astype(hq_ref.dtype)
        hg_ref[...] = _dot(xb, w_v[:, NQ:])
        xb_ref[...] = xb.astype(xb_ref.dtype)

    return pl.pallas_call(
        kern, name=f"in_proj_{layer}", grid=(T // tm,),
        in_specs=[_rows(tm, D), _ANY],
        out_specs=[_rows(tm, NQ), _rows(tm, 2 * D), _rows(tm, D)],
        out_shape=[jax.ShapeDtypeStruct((T, NQ), _ACT), jax.ShapeDtypeStruct((T, 2 * D), _F32),
                   jax.ShapeDtypeStruct((T, D), _ACT)],
        scratch_shapes=[pltpu.VMEM((D, N), w_in.dtype), pltpu.SemaphoreType.DMA((4,))],
        compiler_params=_cparams("arbitrary"),
    )(x, w_in)


def _mix_fwd(oa, ob, hg, x, wpa, wpb, wo, bg, gamma, beta, alpha, layer):
    T, D = x.shape
    WA, WB = oa.shape[1], ob.shape[1]
    tm = 256

    def kern(oa_ref, ob_ref, hg_ref, x_ref, bg_ref, g_ref, b_ref, wpa_h, wpb_h, wo_h,
             x1_ref, u1_ref, pre_ref, ya_ref, yb_ref, wpa_v, wpb_v, wo_v, sa, sb, so):
        @pl.when(pl.program_id(0) == 0)
        def _():
            _load_cols(wpa_h, wpa_v, sa)
            _load_cols(wpb_h, wpb_v, sb)
            _load_rows(wo_h, wo_v, so)

        ya = _dot(oa_ref[...].astype(_MXU), wpa_v[...])
        yb = _dot(ob_ref[...].astype(_MXU), wpb_v[...])
        hgv = hg_ref[...]
        bgv = bg_ref[...]
        ga = jax.nn.sigmoid(hgv[:, :D] + bgv[:, :D])
        gb = jax.nn.sigmoid(hgv[:, D:] + bgv[:, D:])
        pre = ga * ya + gb * yb
        mix = _dot(pre.astype(_MXU), wo_v[...])
        u = alpha * x_ref[...] + mix
        xhat, _ = _ln_stats(u)
        x1_ref[...] = xhat * g_ref[...] + b_ref[...]
        u1_ref[...] = u
        pre_ref[...] = pre.astype(pre_ref.dtype)
        ya_ref[...] = ya.astype(ya_ref.dtype)
        yb_ref[...] = yb.astype(yb_ref.dtype)

    return pl.pallas_call(
        kern, name=f"mix_fwd_{layer}", grid=(T // tm,),
        in_specs=[_rows(tm, WA), _rows(tm, WB), _rows(tm, 2 * D), _rows(tm, D),
                  _whole((1, 2 * D)), _whole((1, D)), _whole((1, D)), _ANY, _ANY, _ANY],
        out_specs=[_rows(tm, D)] * 5,
        out_shape=[jax.ShapeDtypeStruct((T, D), _F32), jax.ShapeDtypeStruct((T, D), _F32)]
        + [jax.ShapeDtypeStruct((T, D), _ACT)] * 3,
        scratch_shapes=[pltpu.VMEM((WA, D), wpa.dtype), pltpu.VMEM((WB, D), wpb.dtype), pltpu.VMEM((D, D), wo.dtype),
                        pltpu.SemaphoreType.DMA((4,)), pltpu.SemaphoreType.DMA((4,)), pltpu.SemaphoreType.DMA((4,))],
        compiler_params=_cparams("arbitrary"),
    )(oa, ob, hg, x, bg, gamma, beta, wpa, wpb, wo)


def _ffn_fwd(x1, wfi, wfo, gamma, beta, alpha, layer, comm=None):
    T, D = x1.shape
    F2 = 4 * wfi.shape[-1]
    F = F2 // 2
    tm = 256
    fc = F // 2

    def kern(x_ref, g_ref, b_ref, wi_h, wo_h, x2_ref, u2_ref, act_ref, gu_ref, xb_ref, wi_v, wo_v, si, so):
        @pl.when(pl.program_id(0) == 0)
        def _():
            _load_cols(wi_h, wi_v, si)
            _load_rows(wo_h, wo_v, so)

        x = x_ref[...]
        xb = x.astype(_MXU)
        xb_ref[...] = xb.astype(xb_ref.dtype)
        ffn = jnp.zeros((tm, D), _F32)
        for c in range(2):
            g = _dot(xb, wi_v[:, c * fc:(c + 1) * fc])
            u = _dot(xb, wi_v[:, F + c * fc:F + (c + 1) * fc])
            act = g * jax.nn.sigmoid(g) * u
            ab = act.astype(_MXU)
            ffn = ffn + _dot(ab, wo_v[c * fc:(c + 1) * fc, :])
            act_ref[:, c * fc:(c + 1) * fc] = ab.astype(act_ref.dtype)
            gu_ref[:, c * fc:(c + 1) * fc] = g.astype(gu_ref.dtype)
            gu_ref[:, F + c * fc:F + (c + 1) * fc] = u.astype(gu_ref.dtype)
        uu = alpha * x + ffn
        xhat, _ = _ln_stats(uu)
        x2_ref[...] = xhat * g_ref[...] + b_ref[...]
        u2_ref[...] = uu

    return _call(
        kern, comm, name=f"ffn_fwd_{layer}", grid=(T // tm,),
        in_specs=[_rows(tm, D), _whole((1, D)), _whole((1, D)), _ANY, _ANY],
        out_specs=[_rows(tm, D), _rows(tm, D), _rows(tm, F), _rows(tm, F2), _rows(tm, D)],
        out_shape=[jax.ShapeDtypeStruct((T, D), _F32), jax.ShapeDtypeStruct((T, D), _F32),
                   jax.ShapeDtypeStruct((T, F), _ACT), jax.ShapeDtypeStruct((T, F2), _ACT),
                   jax.ShapeDtypeStruct((T, D), _ACT)],
        scratch_shapes=[pltpu.VMEM((D, F2), wfi.dtype), pltpu.VMEM((F, D), wfo.dtype),
                        pltpu.SemaphoreType.DMA((4,)), pltpu.SemaphoreType.DMA((4,))],
        args=(x1, gamma, beta, wfi, wfo), semantics=("arbitrary",))


def _ffn_bwd(u2, dy_or_target, gu, gamma, beta, wfi, wfo, alpha, layer, last):
    T, D = u2.shape
    F2 = gu.shape[1]
    F = F2 // 2
    tm = 256
    fc = F // 2

    def kern(u_ref, dy_ref, gu_ref, g_ref, b_ref, wi_h, wo_h, dx_ref, dub_ref, dgu_ref, st_ref, wi_v, wo_v, si, so):
        @pl.when(pl.program_id(0) == 0)
        def _():
            _load_cols(wi_h, wi_v, si)
            _load_rows(wo_h, wo_v, so)
            st_ref[...] = jnp.zeros_like(st_ref)

        gam = g_ref[...]
        u = u_ref[...]
        if last:
            xhat0, _ = _ln_stats(u)
            err = xhat0 * gam + b_ref[...] - dy_ref[...]
            dy = err * (1.0 / D)
            st_ref[2:3, :] += jnp.sum(err * err, axis=0, keepdims=True)
        else:
            dy = dy_ref[...]
        du, dgam, dbet, _ = _ln_bwd(u, dy, gam)
        st_ref[0:1, :] += dgam
        st_ref[1:2, :] += dbet
        dub = du.astype(_MXU)
        dub_ref[...] = dub.astype(dub_ref.dtype)
        dx = alpha * du
        for c in range(2):
            dact = _dot_nt(dub, wo_v[c * fc:(c + 1) * fc, :])
            g = gu_ref[:, c * fc:(c + 1) * fc].astype(_F32)
            uu = gu_ref[:, F + c * fc:F + (c + 1) * fc].astype(_F32)
            sg = jax.nn.sigmoid(g)
            dg = (dact * uu * (sg * (1.0 + g * (1.0 - sg)))).astype(_MXU)
            dup = (dact * (g * sg)).astype(_MXU)
            dgu_ref[:, c * fc:(c + 1) * fc] = dg.astype(dgu_ref.dtype)
            dgu_ref[:, F + c * fc:F + (c + 1) * fc] = dup.astype(dgu_ref.dtype)
            dx = dx + _dot_nt(dg, wi_v[:, c * fc:(c + 1) * fc]) + _dot_nt(dup, wi_v[:, F + c * fc:F + (c + 1) * fc])
        dx_ref[...] = dx

    return pl.pallas_call(
        kern, name=f"ffn_bwd_{layer}", grid=(T // tm,),
        in_specs=[_rows(tm, D), _rows(tm, D), _rows(tm, F2), _whole((1, D)), _whole((1, D)), _ANY, _ANY],
        out_specs=[_rows(tm, D), _rows(tm, D), _rows(tm, F2), _whole((8, D))],
        out_shape=[jax.ShapeDtypeStruct((T, D), _F32), jax.ShapeDtypeStruct((T, D), _ACT),
                   jax.ShapeDtypeStruct((T, F2), _ACT), jax.ShapeDtypeStruct((8, D), _F32)],
        scratch_shapes=[pltpu.VMEM((D, F2), wfi.dtype), pltpu.VMEM((F, D), wfo.dtype),
                        pltpu.SemaphoreType.DMA((4,)), pltpu.SemaphoreType.DMA((4,))],
        compiler_params=_cparams("arbitrary"),
    )(u2, dy_or_target, gu, gamma, beta, wfi, wfo)


def _residual_nt(res, res_scale, d, w, name, comm=None):
    T, K = res.shape
    N = d.shape[1]
    tm = 256

    def kern(r_ref, d_ref, w_hbm, o_ref, w_v, sem):
        @pl.when(pl.program_id(0) == 0)
        def _():
            _load_cols(w_hbm, w_v, sem)

        o_ref[...] = res_scale * r_ref[...] + _dot_nt(d_ref[...].astype(_MXU), w_v[...])

    outs, extra = _call(
        kern, comm, name=name, grid=(T // tm,),
        in_specs=[_rows(tm, K), _rows(tm, N), _ANY], out_specs=[_rows(tm, K)],
        out_shape=[jax.ShapeDtypeStruct((T, K), _F32)],
        scratch_shapes=[pltpu.VMEM((K, N), w.dtype), pltpu.SemaphoreType.DMA((4,))],
        args=(res, d, w), semantics=("arbitrary",))
    return outs[0], extra


def _mix_bwd(u1, dx1, ya, yb, hg, wpa, wpb, wo, bg, gamma, alpha, layer):
    del alpha
    T, D = u1.shape
    WA, WB = wpa.shape[-2], wpb.shape[-2]
    tm = 256

    def kern(u_ref, dx_ref, ya_ref, yb_ref, hg_ref, bg_ref, g_ref, wpa_h, wpb_h, wo_h,
             du_ref, dub_ref, dya_ref, dyb_ref, dhg_ref, doa_ref, dob_ref, st_ref,
             wpa_v, wpb_v, wo_v, sa, sb, so):
        @pl.when(pl.program_id(0) == 0)
        def _():
            _load_cols(wpa_h, wpa_v, sa)
            _load_cols(wpb_h, wpb_v, sb)
            _load_rows(wo_h, wo_v, so)
            st_ref[...] = jnp.zeros_like(st_ref)

        du, dgam, dbet, _ = _ln_bwd(u_ref[...], dx_ref[...], g_ref[...])
        st_ref[1:2, :D] += dgam
        st_ref[1:2, D:] += dbet
        du_ref[...] = du
        dub = du.astype(_MXU)
        dub_ref[...] = dub.astype(dub_ref.dtype)
        dpre = _dot_nt(dub, wo_v[...])
        hgv = hg_ref[...]
        bgv = bg_ref[...]
        ga = jax.nn.sigmoid(hgv[:, :D] + bgv[:, :D])
        gb = jax.nn.sigmoid(hgv[:, D:] + bgv[:, D:])
        dya = (dpre * ga).astype(_MXU)
        dyb = (dpre * gb).astype(_MXU)
        dsa = dpre * ya_ref[...].astype(_F32) * (ga * (1.0 - ga))
        dsb = dpre * yb_ref[...].astype(_F32) * (gb * (1.0 - gb))
        st_ref[0:1, :D] += jnp.sum(dsa, axis=0, keepdims=True)
        st_ref[0:1, D:] += jnp.sum(dsb, axis=0, keepdims=True)
        dya_ref[...] = dya.astype(dya_ref.dtype)
        dyb_ref[...] = dyb.astype(dyb_ref.dtype)
        dhg_ref[:, :D] = dsa.astype(dhg_ref.dtype)
        dhg_ref[:, D:] = dsb.astype(dhg_ref.dtype)
        doa_ref[...] = _dot_nt(dya, wpa_v[...]).astype(doa_ref.dtype)
        dob_ref[...] = _dot_nt(dyb, wpb_v[...]).astype(dob_ref.dtype)

    return pl.pallas_call(
        kern, name=f"mix_bwd_{layer}", grid=(T // tm,),
        in_specs=[_rows(tm, D)] * 4 + [_rows(tm, 2 * D), _whole((1, 2 * D)), _whole((1, D)), _ANY, _ANY, _ANY],
        out_specs=[_rows(tm, D)] * 4 + [_rows(tm, 2 * D), _rows(tm, WA), _rows(tm, WB), _whole((8, 2 * D))],
        out_shape=[jax.ShapeDtypeStruct((T, D), _F32)] + [jax.ShapeDtypeStruct((T, D), _ACT)] * 3
        + [jax.ShapeDtypeStruct((T, 2 * D), _ACT), jax.ShapeDtypeStruct((T, WA), _ACT),
           jax.ShapeDtypeStruct((T, WB), _ACT), jax.ShapeDtypeStruct((8, 2 * D), _F32)],
        scratch_shapes=[pltpu.VMEM((WA, D), wpa.dtype), pltpu.VMEM((WB, D), wpb.dtype), pltpu.VMEM((D, D), wo.dtype),
                        pltpu.SemaphoreType.DMA((4,)), pltpu.SemaphoreType.DMA((4,)), pltpu.SemaphoreType.DMA((4,))],
        compiler_params=_cparams("arbitrary"),
    )(u1, dx1, ya, yb, hg, bg, gamma, wpa, wpb, wo)


def _grad_w(a, b, *, col_shards, name, comm=None):
    T, M = a.shape
    N = b.shape[1]
    tk = 512
    n = N // 4 if col_shards else N
    whole = M * N * 4 <= _GRAD_ACC_BYTES
    tn = N if whole else (n if col_shards else _divisor_tile(N, _GRAD_ACC_BYTES // (4 * M)))
    nk = T // tk

    def kern(a_ref, b_ref, o_ref, acc):
        k = pl.program_id(1)

        @pl.when(k == 0)
        def _():
            acc[...] = jnp.zeros_like(acc)

        acc[...] += _dot_tn(a_ref[...].astype(_MXU), b_ref[...].astype(_MXU))

        @pl.when(k == nk - 1)
        def _():
            if col_shards and whole:
                for s in range(4):
                    o_ref[s] = acc[:, s * n:(s + 1) * n].astype(o_ref.dtype)
            else:
                o_ref[...] = acc[...].astype(o_ref.dtype)

    if col_shards:
        out_spec = (pl.BlockSpec((4, M, n), lambda j, k: (0, 0, 0)) if whole
                    else pl.BlockSpec((None, M, n), lambda j, k: (j, 0, 0)))
        out_shape = jax.ShapeDtypeStruct((4, M, n), _ACT)
    else:
        out_spec = pl.BlockSpec((M, tn), lambda j, k: (0, j))
        out_shape = jax.ShapeDtypeStruct((M, N), _ACT)
    outs, extra = _call(
        kern, comm, name=name, grid=(N // tn, nk),
        in_specs=[pl.BlockSpec((tk, M), lambda j, k: (k, 0)), pl.BlockSpec((tk, tn), lambda j, k: (k, j))],
        out_specs=[out_spec], out_shape=[out_shape], scratch_shapes=[pltpu.VMEM((M, tn), _F32)],
        args=(a, b), semantics=("parallel", "arbitrary"))
    return outs[0], extra


def _bias_tiles(rel):
    H = rel.shape[0]
    span = _TQ * _BAND_TILES - 1
    edge = span - _REL_CLIP
    gvec = jnp.concatenate([jnp.broadcast_to(rel[:, :1], (H, edge)), rel, jnp.broadcast_to(rel[:, -1:], (H, edge))], axis=1)
    width = _BIAS_TILES * _TQ
    period = width + _TQ
    tiled = jnp.broadcast_to(jnp.pad(gvec[:, ::-1], ((0, 0), (0, 1)))[:, None, :], (H, _TQ, period))
    rows = tiled.reshape(H, _TQ * period)[:, :_TQ * (period - 1)].reshape(H, _TQ, period - 1)[:, :, _TQ - 1:]
    r = jnp.arange(_TQ)[:, None]
    u = jnp.arange(width)[None, :]
    d = 4 * _TQ + r - u
    rm = r % _CHUNK
    valid = (d >= rm - (_CHUNK - 1)) & (d <= rm + 8 * _CHUNK)
    tiles = jnp.where(valid[None], rows, _MASKED)
    return tiles.reshape(H // 2, 2 * _TQ, _BIAS_TILES, _TQ).transpose(0, 2, 1, 3)


def _fold_bias_grad(db):
    H = 2 * db.shape[0]
    width = _BIAS_TILES * _TQ
    period = width + _TQ
    x = jnp.pad(db.transpose(0, 2, 1, 3).reshape(H, _TQ, width), ((0, 0), (0, 0), (_TQ - 1, 0)))
    skew = jnp.pad(x.reshape(H, _TQ * (period - 1)), ((0, 0), (0, _TQ))).reshape(H, _TQ, period)
    dg = skew.sum(axis=1)[:, :period - 1][:, ::-1]
    span = _TQ * _BAND_TILES - 1
    edge = span - _REL_CLIP
    mid = dg[:, edge:edge + 2 * _REL_CLIP + 1]
    lo = dg[:, :edge].sum(axis=1)
    hi = dg[:, edge + 2 * _REL_CLIP + 1:].sum(axis=1)
    return mid.at[:, 0].add(lo).at[:, -1].add(hi)


def _band_window(i):
    j0 = jnp.maximum(i - (_BAND_TILES - 1), 0)
    return j0, (_BAND_TILES - 1) - (i - j0)


def _head_masks():
    lane = lax.broadcasted_iota(jnp.int32, (1, _LANES), 1)
    return [(lane // _HEAD) == hh for hh in range(2)]


def _stack_heads(x, masks):
    return jnp.concatenate([jnp.where(m, x, jnp.zeros_like(x)) for m in masks], axis=0)


def _unstack_heads(y, masks):
    return jnp.where(masks[0], y[:_TQ], y[_TQ:])


def _scaled(q):
    return q * jnp.asarray(_HEAD ** -0.5, q.dtype)


def _band_probs(q2, k_ref, b_ref, j0, boff):
    s = []
    for j in range(_BAND_TILES):
        kj = k_ref[pl.ds(pl.multiple_of((j0 + j) * _TQ, _TQ), _TQ), :]
        s.append(_dot_nt(q2, kj) + b_ref[boff + j])
    m = jnp.max(functools.reduce(jnp.maximum, s), axis=-1, keepdims=True)
    p = [jnp.exp(x - m) for x in s]
    l = jnp.sum(functools.reduce(lambda a, b: a + b, p), axis=-1, keepdims=True)
    return p, 1.0 / l


def _qkv_specs(T, cb, npair, tq=_TQ):
    return [pl.BlockSpec((tq, _LANES), lambda h, i: (i, cb + h)),
            pl.BlockSpec((T, _LANES), lambda h, i: (0, cb + npair + h)),
            pl.BlockSpec((T, _LANES), lambda h, i: (0, cb + 2 * npair + h))]


def _attn_a_fwd(hq, bias, col0, width, layer, comm=None):
    T = hq.shape[0]
    npair = width // _LANES
    nsub = _BAND_SUBTILES
    tq = nsub * _TQ

    def kern(q_ref, k_ref, v_ref, b_ref, o_ref):
        masks = _head_masks()
        q = _scaled(q_ref[...])
        for s in range(nsub):
            part = slice(s * _TQ, (s + 1) * _TQ)
            j0, boff = _band_window(nsub * pl.program_id(1) + s)
            p, inv = _band_probs(_stack_heads(q[part], masks), k_ref, b_ref, j0, boff)
            o = jnp.zeros((2 * _TQ, _LANES), _F32)
            for j in range(_BAND_TILES):
                vj = v_ref[pl.ds(pl.multiple_of((j0 + j) * _TQ, _TQ), _TQ), :]
                o = o + _dot(p[j].astype(_MXU), vj)
            o_ref[part, :] = _unstack_heads(o * inv, masks).astype(o_ref.dtype)

    outs, extra = _call(
        kern, comm, name=f"band_attn_fwd_{layer}", grid=(npair, T // tq),
        in_specs=_qkv_specs(T, col0 // _LANES, npair, tq)
        + [pl.BlockSpec((None, _BIAS_TILES, 2 * _TQ, _TQ), lambda h, i: (h, 0, 0, 0))],
        out_specs=[pl.BlockSpec((tq, _LANES), lambda h, i: (i, h))],
        out_shape=[jax.ShapeDtypeStruct((T, width), _ACT)], scratch_shapes=[],
        args=(hq, hq, hq, bias), semantics=("arbitrary", "arbitrary"))
    return outs[0], extra


def _attn_a_bwd(hq, bias, do, col0, width, layer, comm=None):
    T = hq.shape[0]
    npair = width // _LANES
    nsub = _BAND_SUBTILES
    tq = nsub * _TQ
    nq = T // tq
    scale = _HEAD ** -0.5

    def kern(q_ref, k_ref, v_ref, b_ref, do_ref, dq_ref, dk_ref, dv_ref, db_ref, dk_acc, dv_acc):
        i = pl.program_id(1)

        @pl.when(i == 0)
        def _():
            dk_acc[...] = jnp.zeros_like(dk_acc)
            dv_acc[...] = jnp.zeros_like(dv_acc)
            db_ref[...] = jnp.zeros_like(db_ref)

        masks = _head_masks()
        q = _scaled(q_ref[...])
        do_t = do_ref[...]
        for s in range(nsub):
            part = slice(s * _TQ, (s + 1) * _TQ)
            j0, boff = _band_window(nsub * i + s)
            q2 = _stack_heads(q[part], masks)
            do2 = _stack_heads(do_t[part], masks).astype(_MXU)
            p, inv = _band_probs(q2, k_ref, b_ref, j0, boff)
            rows = [pl.ds(pl.multiple_of((j0 + j) * _TQ, _TQ), _TQ) for j in range(_BAND_TILES)]
            p = [x * inv for x in p]
            dp = [_dot_nt(do2, v_ref[rows[j], :]) for j in range(_BAND_TILES)]
            delta = jnp.sum(functools.reduce(lambda a, b: a + b, [p[j] * dp[j] for j in range(_BAND_TILES)]),
                            axis=-1, keepdims=True)
            dq = jnp.zeros((2 * _TQ, _LANES), _F32)
            for j in range(_BAND_TILES):
                ds = p[j] * (dp[j] - delta)
                db_ref[boff + j] += ds
                dsb = ds.astype(_MXU)
                dq = dq + _dot(dsb, k_ref[rows[j], :])
                dk_acc[rows[j], :] += _dot_tn(dsb, q2)
                dv_acc[rows[j], :] += _dot_tn(p[j].astype(_MXU), do2)
            dq_ref[part, :] = (_unstack_heads(dq, masks) * scale).astype(dq_ref.dtype)

        @pl.when(i == nq - 1)
        def _():
            dk_ref[...] = dk_acc[...].astype(dk_ref.dtype)
            dv_ref[...] = dv_acc[...].astype(dv_ref.dtype)

    strip = pl.BlockSpec((None, _BIAS_TILES, 2 * _TQ, _TQ), lambda h, i: (h, 0, 0, 0))
    tile = pl.BlockSpec((tq, _LANES), lambda h, i: (i, h))
    column = pl.BlockSpec((T, _LANES), lambda h, i: (0, h))
    outs, extra = _call(
        kern, comm, name=f"band_attn_bwd_{layer}", grid=(npair, nq),
        in_specs=_qkv_specs(T, col0 // _LANES, npair, tq) + [strip, tile],
        out_specs=[tile, column, column, strip],
        out_shape=[jax.ShapeDtypeStruct((T, width), _ACT)] * 3
        + [jax.ShapeDtypeStruct((npair, _BIAS_TILES, 2 * _TQ, _TQ), _F32)],
        scratch_shapes=[pltpu.VMEM((T, _LANES), _F32), pltpu.VMEM((T, _LANES), _F32)],
        args=(hq, hq, hq, bias, do), semantics=("arbitrary", "arbitrary"))
    return outs, extra


def _suffix_matrix():
    r = lax.broadcasted_iota(jnp.int32, (_TQ, _TQ), 0)
    c = lax.broadcasted_iota(jnp.int32, (_TQ, _TQ), 1)
    r2 = lax.broadcasted_iota(jnp.int32, (2 * _TQ, _TQ), 0)
    c2 = lax.broadcasted_iota(jnp.int32, (2 * _TQ, _TQ), 1)
    return (r > c).astype(_MXU), c2 - (r2 & (_TQ - 1))


def _suffix_sums(xs, tri):
    n, k = xs[0].shape[0], len(xs)
    his = [x.astype(_MXU) for x in xs]
    los = [(x - h.astype(_F32)).astype(_MXU) for x, h in zip(xs, his)]
    y = _dot(jnp.concatenate(his + los, axis=0), tri)
    return [y[j * n:(j + 1) * n] + y[(k + j) * n:(k + j + 1) * n] for j in range(k)]


def _stick_tiles(tiles, rel, carry_l, tri):
    zs = [_dot_nt(qs, kj) for qs, kj, _, _ in tiles]
    Ls, masks = [], []
    for z, (_, _, jj, _) in zip(zs, tiles):
        nsp = -(jnp.maximum(z, 0.0) + jnp.log(1.0 + jnp.exp(-jnp.abs(z))))
        if isinstance(jj, int):
            mask = (rel < 0) if jj == 0 else None
        else:
            mask = rel < jnp.where(jj == 0, 0, _TQ)
        Ls.append(nsp if mask is None else jnp.where(mask, nsp, 0.0))
        masks.append(mask)
    carry_l = list(carry_l)
    ws = []
    for z, L, suffix, mask, (_, _, _, sub) in zip(zs, Ls, _suffix_sums(Ls, tri), masks, tiles):
        w = jnp.exp(z + L + suffix + carry_l[sub])
        ws.append(w if mask is None else jnp.where(mask, w, 0.0))
        carry_l[sub] = carry_l[sub] + jnp.sum(L, axis=-1, keepdims=True)
    return zs, Ls, ws, masks, carry_l


def _sweep(i, step, zero):
    nsub = _SB_SUBTILES

    def window():
        tiles = [(s, jj) for jj in range(_SB_WINDOW) for s in range(nsub)]
        return tuple((jnp.int32(_SB_WINDOW),) + c for c in step(tiles, [zero] * nsub))

    start = lax.cond(i >= -(-(_SB_WINDOW - 1) // nsub), window, lambda: tuple((jnp.int32(0),) + zero for _ in range(nsub)))
    outs = []
    for s in range(nsub):
        def done(c, s=s):
            return jnp.logical_or(c[0] > nsub * i + s, jnp.max(c[1]) < _EXP_ZERO_BELOW)

        def more(c, s=s):
            carries = [None] * nsub
            carries[s] = c[1:]
            return (c[0] + 1,) + step([(s, c[0])], carries)[s]

        outs.append(lax.while_loop(lambda c, done=done: jnp.logical_not(done(c)), more, start[s]))
    return outs


def _sb_fwd(hq, col0, width, layer, comm=None):
    T = hq.shape[0]
    npair = width // _LANES
    nsub = _SB_SUBTILES
    tq = nsub * _TQ

    def kern(q_ref, k_ref, v_ref, o_ref):
        i = pl.program_id(1)
        masks = _head_masks()
        tri, rel = _suffix_matrix()
        q = _scaled(q_ref[...])
        q2 = [_stack_heads(q[s * _TQ:(s + 1) * _TQ], masks) for s in range(nsub)]

        def step(tiles, carries):
            rows = [pl.ds(pl.multiple_of((nsub * i + s - jj) * _TQ, _TQ), _TQ) for s, jj in tiles]
            cls = [None if c is None else c[0] for c in carries]
            accs = [None if c is None else c[1] for c in carries]
            _, _, ws, _, cls = _stick_tiles([(q2[s], k_ref[r, :], jj, s) for (s, jj), r in zip(tiles, rows)], rel, cls, tri)
            for w, r, (s, _) in zip(ws, rows, tiles):
                accs[s] = accs[s] + _dot(w.astype(_MXU), v_ref[r, :])
            return [None if c is None else (cls[s], accs[s]) for s, c in enumerate(carries)]

        outs = _sweep(i, step, (jnp.zeros((2 * _TQ, 1), _F32), jnp.zeros((2 * _TQ, _LANES), _F32)))
        for s in range(nsub):
            o_ref[s * _TQ:(s + 1) * _TQ, :] = _unstack_heads(outs[s][2], masks)

    outs, extra = _call(
        kern, comm, name=f"stick_attn_fwd_{layer}", grid=(npair, T // tq),
        in_specs=_qkv_specs(T, col0 // _LANES, npair, tq),
        out_specs=[pl.BlockSpec((tq, _LANES), lambda h, i: (i, h))],
        out_shape=[jax.ShapeDtypeStruct((T, width), _F32)], scratch_shapes=[],
        args=(hq, hq, hq), semantics=("arbitrary", "arbitrary"))
    return outs[0], extra


def _sb_bwd(hq, o, do, col0, width, layer, comm=None):
    T = hq.shape[0]
    npair = width // _LANES
    nsub = _SB_SUBTILES
    tq = nsub * _TQ
    nq = T // tq
    scale = _HEAD ** -0.5

    def kern(q_ref, k_ref, v_ref, o_ref, do_ref, dq_ref, dk_ref, dv_ref, dk_acc, dv_acc):
        i = pl.program_id(1)

        @pl.when(i == 0)
        def _():
            dk_acc[...] = jnp.zeros_like(dk_acc)
            dv_acc[...] = jnp.zeros_like(dv_acc)

        masks = _head_masks()
        tri, rel = _suffix_matrix()
        q = _scaled(q_ref[...])
        do_t = do_ref[...]
        prod = do_t.astype(_F32) * o_ref[...]
        part = [slice(s * _TQ, (s + 1) * _TQ) for s in range(nsub)]
        q2 = [_stack_heads(q[p], masks) for p in part]
        do2 = [_stack_heads(do_t[p], masks).astype(_MXU) for p in part]
        dsum = [jnp.sum(_stack_heads(prod[p], masks), axis=-1, keepdims=True) for p in part]

        def step(tiles, carries):
            rows = [pl.ds(pl.multiple_of((nsub * i + s - jj) * _TQ, _TQ), _TQ) for s, jj in tiles]
            kjs = [k_ref[r, :] for r in rows]
            cls, cgs, dqs = ([None if c is None else c[n] for c in carries] for n in range(3))
            zs, Ls, ws, tile_masks, cls = _stick_tiles([(q2[s], kj, jj, s) for (s, jj), kj in zip(tiles, kjs)], rel, cls, tri)
            wbs = [w.astype(_MXU) for w in ws]
            gs = [wb.astype(_F32) * _dot_nt(do2[s], v_ref[r, :]) for wb, r, (s, _) in zip(wbs, rows, tiles)]
            for z, L, g, later, mask, wb, kj, r, (s, _) in zip(zs, Ls, gs, _suffix_sums(gs, tri), tile_masks, wbs, kjs,
                                                               rows, tiles):
                dz = g - jnp.exp(z + L) * (dsum[s] - (later + cgs[s]))
                if mask is not None:
                    dz = jnp.where(mask, dz, 0.0)
                dzb = dz.astype(_MXU)
                dk_acc[r, :] += _dot_tn(dzb, q2[s])
                dv_acc[r, :] += _dot_tn(wb, do2[s])
                dqs[s] = dqs[s] + _dot(dzb, kj)
                cgs[s] = cgs[s] + jnp.sum(g, axis=-1, keepdims=True)
            return [None if c is None else (cls[s], cgs[s], dqs[s]) for s, c in enumerate(carries)]

        zc = jnp.zeros((2 * _TQ, 1), _F32)
        outs = _sweep(i, step, (zc, zc, jnp.zeros((2 * _TQ, _LANES), _F32)))
        for s in range(nsub):
            dq_ref[part[s], :] = (_unstack_heads(outs[s][3], masks) * scale).astype(dq_ref.dtype)

        @pl.when(i == nq - 1)
        def _():
            dk_ref[...] = dk_acc[...].astype(dk_ref.dtype)
            dv_ref[...] = dv_acc[...].astype(dv_ref.dtype)

    tile_spec = pl.BlockSpec((tq, _LANES), lambda h, i: (i, h))
    column = pl.BlockSpec((T, _LANES), lambda h, i: (0, h))
    outs, extra = _call(
        kern, comm, name=f"stick_attn_bwd_{layer}", grid=(npair, nq),
        in_specs=_qkv_specs(T, col0 // _LANES, npair, tq) + [tile_spec, tile_spec],
        out_specs=[tile_spec, column, column],
        out_shape=[jax.ShapeDtypeStruct((T, width), _ACT)] * 3,
        scratch_shapes=[pltpu.VMEM((T, _LANES), _F32), pltpu.VMEM((T, _LANES), _F32)],
        args=(hq, hq, hq, o, do), semantics=("arbitrary", "arbitrary"))
    return outs, extra


_DENSE = ("w_in", "w_proj_a", "w_proj_b", "w_out", "w_ffn_in", "w_ffn_out")
_COL_SHARDED = {"w_in": True, "w_proj_a": True, "w_proj_b": True, "w_out": False, "w_ffn_in": True, "w_ffn_out": False}
_SMALL = ("b_gate", "rel_bias", "ln1_g", "ln1_b", "ln2_g", "ln2_b")


class _Plans:
    def __init__(self, plans=None):
        self.plans = plans or {}

    def start(self, key):
        if key not in self.plans:
            return None, None
        return self.plans[key]()

    @staticmethod
    def finish(done, extra):
        if done is not None:
            done(extra)


def _layer_fwd(x, W, small, l, alpha, plans):
    WA = small["rel_bias"].shape[1] * _HEAD
    row = lambda v: v[l].reshape(1, -1)
    hq, hg, xb = _in_proj(x, W["w_in"], l)
    WB = (hq.shape[1] - 3 * WA) // 3
    bias = _bias_tiles(small["rel_bias"][l])
    comm, done = plans.start(f"band_fwd_{l}")
    oa, extra = _attn_a_fwd(hq, bias, 0, WA, l, comm)
    plans.finish(done, extra)
    comm, done = plans.start(f"stick_fwd_{l}")
    ob, extra = _sb_fwd(hq, 3 * WA, WB, l, comm)
    plans.finish(done, extra)
    x1, u1, pre, ya, yb = _mix_fwd(oa, ob, hg, x, W["w_proj_a"], W["w_proj_b"], W["w_out"],
                                   row(small["b_gate"]), row(small["ln1_g"]), row(small["ln1_b"]), alpha, l)
    comm, done = plans.start(f"ffn_fwd_{l}")
    (x2, u2, act, gu, x1b), extra = _ffn_fwd(x1, W["w_ffn_in"], W["w_ffn_out"], row(small["ln2_g"]),
                                             row(small["ln2_b"]), alpha, l, comm)
    plans.finish(done, extra)
    return x2, dict(xb=xb, hq=hq, hg=hg, bias=bias, oa=oa, ob=ob, x1b=x1b, u1=u1, pre=pre, ya=ya, yb=yb, u2=u2,
                    act=act, gu=gu)


def _layer_bwd(dy_or_target, S, W, small, l, last, alpha, plans, gw):
    D = S["xb"].shape[1]
    WA, WB = S["oa"].shape[1], S["ob"].shape[1]
    row = lambda v: v[l].reshape(1, -1)

    def blocks(g, n):
        return g if _COL_SHARDED[n] else g.reshape(4, g.shape[0] // 4, g.shape[1])

    dx1, du2b, dgu, st2 = _ffn_bwd(S["u2"], dy_or_target, S["gu"], row(small["ln2_g"]), row(small["ln2_b"]),
                                   W["w_ffn_in"], W["w_ffn_out"], alpha, l, last)
    gw["w_ffn_in"] = blocks(_grad_w(S["x1b"], dgu, col_shards=True, name=f"grad_w_ffn_in_{l}")[0], "w_ffn_in")
    gw["w_ffn_out"] = blocks(_grad_w(S["act"], du2b, col_shards=False, name=f"grad_w_ffn_out_{l}")[0], "w_ffn_out")
    du1, du1b, dya, dyb, dhg, doa, dob, st1 = _mix_bwd(S["u1"], dx1, S["ya"], S["yb"], S["hg"], W["w_proj_a"],
                                                       W["w_proj_b"], W["w_out"], row(small["b_gate"]),
                                                       row(small["ln1_g"]), alpha, l)
    gw["w_out"] = blocks(_grad_w(S["pre"], du1b, col_shards=False, name=f"grad_w_out_{l}")[0], "w_out")
    gw["w_proj_a"] = blocks(_grad_w(S["oa"], dya, col_shards=True, name=f"grad_w_proj_a_{l}")[0], "w_proj_a")
    gw["w_proj_b"] = blocks(_grad_w(S["ob"], dyb, col_shards=True, name=f"grad_w_proj_b_{l}")[0], "w_proj_b")
    comm, done = plans.start(f"band_bwd_{l}")
    (dqa, dka, dva, dbias), extra = _attn_a_bwd(S["hq"], S["bias"], doa, 0, WA, l, comm)
    plans.finish(done, extra)
    comm, done = plans.start(f"stick_bwd_{l}")
    (dqb, dkb, dvb), extra = _sb_bwd(S["hq"], S["ob"], dob, 3 * WA, WB, l, comm)
    plans.finish(done, extra)
    dh = jnp.concatenate([dqa, dka, dva, dqb, dkb, dvb, dhg], axis=1)
    comm, done = plans.start(f"grad_w_in_{l}")
    g, extra = _grad_w(S["xb"], dh, col_shards=True, name=f"grad_w_in_{l}", comm=comm)
    gw["w_in"] = blocks(g, "w_in")
    plans.finish(done, extra)
    comm, done = plans.start(f"in_proj_bwd_{l}")
    dx, extra = _residual_nt(du1, alpha, dh, W["w_in"], f"in_proj_bwd_{l}", comm)
    plans.finish(done, extra)
    gs = dict(b_gate=st1[0], rel_bias=_fold_bias_grad(dbias), ln1_g=st1[1, :D], ln1_b=st1[1, D:],
              ln2_g=st2[0], ln2_b=st2[1])
    return dx, gs, st2[2]


def _local_step(x, target, W, small, plans=None, gws=None):
    depth = len(W)
    alpha = float((2 * depth) ** 0.25)
    plans = plans or _Plans()
    gws = gws if gws is not None else [dict() for _ in range(depth)]
    saved = []
    h = x
    for l in range(depth):
        h, S = _layer_fwd(h, W[l], small, l, alpha, plans)
        saved.append(S)
    gss = [None] * depth
    d = target
    sq = None
    for l in reversed(range(depth)):
        d, gss[l], sq_l = _layer_bwd(d, saved[l], W[l], small, l, l == depth - 1, alpha, plans, gws[l])
        if l == depth - 1:
            sq = sq_l
    return sq, d, gws, gss


def _place():
    return lax.axis_index("x"), lax.axis_index("y"), lax.axis_index("c")


def _remote(src, dst, send_sem, recv_sem, to):
    return pltpu.make_async_remote_copy(src_ref=src, dst_ref=dst, send_sem=send_sem, recv_sem=recv_sem,
                                        device_id=to, device_id_type=_MESH)


def _half(ref, hc):
    kh = ref.shape[0] // 2
    return ref.at[pl.ds(pl.multiple_of(hc * kh, 16), kh), :]


def _gather_plan(blocks, fractions):
    nt = len(blocks)

    def run(step, nsteps, ins, outs, sems):
        send_sems, recv_sems, loc_sems = sems
        x, y, c = _place()
        k = 2 * x + y
        me, sibling = (x, y, c), (x, y, 1 - c)
        chips = [(1 - x, y), (x, 1 - y), (1 - x, 1 - y)]
        chip_k = [2 * cx + cy for cx, cy in chips]

        def ici(t, s, owner_k, to, src=None):
            dst = _half(outs[t].at[owner_k], c)
            return _remote(dst if src is None else src, dst, send_sems.at[t, s], recv_sems.at[t, s], to)

        def passed(t, s, hc, to):
            blk = _half(outs[t].at[chip_k[s]], hc)
            return _remote(blk, blk, send_sems.at[t, 3 + s], recv_sems.at[t, 3 + s], to)

        def local(t):
            return pltpu.make_async_copy(ins[t], outs[t].at[k], loc_sems.at[t])

        @pl.when(step == 0)
        def _():
            for t in range(nt):
                local(t).start()
                for s, chip in enumerate(chips):
                    ici(t, s, k, (*chip, c), src=_half(ins[t], c)).start()

        for t in range(nt):
            @pl.when(step == min(nsteps - 1, int(fractions[t] * nsteps)))
            def _():
                for s in range(3):
                    ici(t, s, chip_k[s], me).wait_recv()
                    passed(t, s, c, sibling).start()

        @pl.when(step == nsteps - 1)
        def _():
            for t in range(nt):
                for s, chip in enumerate(chips):
                    passed(t, s, 1 - c, me).wait_recv()
            for t in range(nt):
                for s, chip in enumerate(chips):
                    ici(t, s, k, (*chip, c), src=_half(ins[t], c)).wait_send()
                    passed(t, s, c, sibling).wait_send()
                local(t).wait()

    return _Comm(blocks, [jax.ShapeDtypeStruct((4,) + b.shape, b.dtype) for b in blocks],
                 [pltpu.SemaphoreType.DMA((nt, 6)), pltpu.SemaphoreType.DMA((nt, 6)), pltpu.SemaphoreType.DMA((nt,))], run)


def _scatter_plan(grads, owners):
    nt = len(grads)

    def run(step, nsteps, ins, outs, sems):
        send_sems, recv_sems, loc_sems = sems
        x, y, c = _place()
        me = 4 * x + 2 * y + c

        def target(r):
            tx = 1 - x if r & 2 else x
            ty = 1 - y if r & 1 else y
            return tx, ty

        def send(t, r):
            tx, ty = target(r)
            return _remote(ins[t].at[2 * tx + ty], outs[t].at[me], send_sems.at[t, r], recv_sems.at[t, 2 * r + c],
                           (tx, ty, owners[t]))

        def local(t):
            return pltpu.make_async_copy(ins[t].at[2 * x + y], outs[t].at[me], loc_sems.at[t])

        @pl.when(step == 0)
        def _():
            for t in range(nt):
                @pl.when(c == owners[t])
                def _():
                    local(t).start()

                @pl.when(c != owners[t])
                def _():
                    send(t, 0).start()

                for r in range(1, 4):
                    send(t, r).start()

        @pl.when(step == nsteps - 1)
        def _():
            for t in range(nt):
                @pl.when(c == owners[t])
                def _():
                    for r in range(4):
                        sx, sy = target(r)
                        for cs in range(2):
                            if r == 0 and cs == owners[t]:
                                continue
                            src_dev = 4 * sx + 2 * sy + cs
                            _remote(ins[t].at[0], outs[t].at[src_dev], send_sems.at[t, r], recv_sems.at[t, 2 * r + cs],
                                    (x, y, c)).wait_recv()
                    local(t).wait()

                @pl.when(c != owners[t])
                def _():
                    send(t, 0).wait_send()

                for r in range(1, 4):
                    send(t, r).wait_send()

    return _Comm(grads, [jax.ShapeDtypeStruct((8,) + g.shape[1:], g.dtype) for g in grads],
                 [pltpu.SemaphoreType.DMA((nt, 4)), pltpu.SemaphoreType.DMA((nt, 8)), pltpu.SemaphoreType.DMA((nt,))], run)


def _share_plan(reduced, owners):
    nt = len(reduced)

    def run(step, nsteps, ins, outs, sems):
        del ins
        send_sems, recv_sems = sems
        x, y, c = _place()

        def give(t, to):
            return _remote(outs[t], outs[t], send_sems.at[t], recv_sems.at[t], to)

        @pl.when(step == 0)
        def _():
            for t in range(nt):
                @pl.when(c == owners[t])
                def _():
                    give(t, (x, y, 1 - c)).start()

        @pl.when(step == nsteps - 1)
        def _():
            for t in range(nt):
                @pl.when(c == owners[t])
                def _():
                    give(t, (x, y, 1 - c)).wait_send()

                @pl.when(c != owners[t])
                def _():
                    give(t, (x, y, c)).wait_recv()

    return _Comm(reduced, [jax.ShapeDtypeStruct(r.shape, r.dtype) for r in reduced],
                 [pltpu.SemaphoreType.DMA((nt,)), pltpu.SemaphoreType.DMA((nt,))], run,
                 aliases={t: t for t in range(nt)})


def _join(a, b):
    ni, no, ns = len(a.inputs), len(a.out_shapes), len(a.sems)

    def run(step, nsteps, ins, outs, sems):
        a.run(step, nsteps, ins[:ni], outs[:no], sems[:ns])
        b.run(step, nsteps, ins[ni:], outs[no:], sems[ns:])

    aliases = dict(a.aliases)
    aliases.update({ni + i: no + o for i, o in b.aliases.items()})
    return _Comm(a.inputs + b.inputs, a.out_shapes + b.out_shapes, a.sems + b.sems, run, aliases)


def _peer(x, y, c, r):
    px = 1 - x if r & 4 else x
    py = 1 - y if r & 2 else y
    pc = 1 - c if r & 1 else c
    return (px, py, pc), 4 * px + 2 * py + pc


def _sum_slots(st, name):
    _, K, n = st.shape
    tr = next(t for t in (256, 128, 64, 32, 16) if K % t == 0)

    def kern(s_ref, o_ref):
        acc = s_ref[0].astype(_F32)
        for d in range(1, 8):
            acc = acc + s_ref[d].astype(_F32)
        o_ref[...] = acc.astype(o_ref.dtype)

    return pl.pallas_call(
        kern, name=name, grid=(K // tr,),
        in_specs=[pl.BlockSpec((8, tr, n), lambda i: (0, i, 0))], out_specs=_rows(tr, n),
        out_shape=jax.ShapeDtypeStruct((K, n), _ACT),
        compiler_params=_cparams("parallel"),
    )(st)


def _all_reduce_small(p):
    R = p.shape[0]

    def body(p_ref, o_ref, stage, send_sems, recv_sems):
        x, y, c = _place()
        me = 4 * x + 2 * y + c
        stage[me] = p_ref[...]
        sent = []
        for r in range(1, 8):
            to, _ = _peer(x, y, c, r)
            cp = _remote(p_ref, stage.at[me], send_sems.at[r - 1], recv_sems.at[r - 1], to)
            cp.start()
            sent.append(cp)
        for r in range(1, 8):
            _, src_dev = _peer(x, y, c, r)
            _remote(p_ref, stage.at[src_dev], send_sems.at[r - 1], recv_sems.at[r - 1], (x, y, c)).wait_recv()
        acc = stage[0]
        for d in range(1, 8):
            acc = acc + stage[d]
        o_ref[...] = acc
        for cp in sent:
            cp.wait_send()

    vm = pl.BlockSpec(memory_space=pltpu.VMEM)
    return pl.pallas_call(
        body, name="all_reduce_small",
        in_specs=[vm], out_specs=vm,
        out_shape=jax.ShapeDtypeStruct((R, _LANES), _F32),
        scratch_shapes=[pltpu.VMEM((8, R, _LANES), _F32), pltpu.SemaphoreType.DMA((7,)), pltpu.SemaphoreType.DMA((7,))],
    )(p)


def _adamw_update(gv, w_ref, m_ref, v_ref, gf_ref, d_ref, nm_ref, nv_ref):
    nm = _B1 * m_ref[...] + (1.0 - _B1) * gv
    nv = _B2 * v_ref[...] + (1.0 - _B2) * (gv * gv)
    m_hat = nm / (1.0 - _B1 ** _STEP)
    v_hat = nv / (1.0 - _B2 ** _STEP)
    gf_ref[...] = gv
    d_ref[...] = -_LR * (m_hat / (jnp.sqrt(v_hat) + _EPS) + _WD * w_ref[...])
    nm_ref[...] = nm
    nv_ref[...] = nv


def _adamw_layers(w, g_layers, m, v, name):
    _, K, n = w.shape
    tr = next(t for t in (256, 128, 64, 32, 16) if K % t == 0)

    def kern(w_ref, g0_ref, g1_ref, m_ref, v_ref, *out_refs):
        first = pl.program_id(0) == 0
        gv = jnp.where(first, g0_ref[...].astype(_F32), g1_ref[...].astype(_F32))
        _adamw_update(gv, w_ref, m_ref, v_ref, *out_refs)

    stacked = pl.BlockSpec((None, tr, n), lambda l, i: (l, i, 0))
    layer = pl.BlockSpec((tr, n), lambda l, i: (i, 0))
    return tuple(pl.pallas_call(
        kern, name=name, grid=(2, K // tr),
        in_specs=[stacked, layer, layer, stacked, stacked], out_specs=[stacked] * 4,
        out_shape=[jax.ShapeDtypeStruct(w.shape, _F32)] * 4,
        compiler_params=_cparams("parallel", "parallel"),
    )(w, g_layers[0], g_layers[1], m, v))


def _adamw(w, g, m, v, name):
    shape = w.shape
    w2, g2, m2, v2 = (a.reshape(-1, shape[-1]) for a in (w, g, m, v))
    R, C = w2.shape
    tr = next((t for t in (256, 128, 64, 32, 16) if R % t == 0), R)

    def kern(w_ref, g_ref, m_ref, v_ref, *out_refs):
        _adamw_update(g_ref[...].astype(_F32), w_ref, m_ref, v_ref, *out_refs)

    outs = pl.pallas_call(
        kern, name=name, grid=(R // tr,),
        in_specs=[_rows(tr, C)] * 4, out_specs=[_rows(tr, C)] * 4,
        out_shape=[jax.ShapeDtypeStruct((R, C), _F32)] * 4,
        compiler_params=_cparams("parallel"),
    )(w2, g2, m2, v2)
    return tuple(o.reshape(shape) for o in outs)


def _pack_small(gss, sq):
    parts = [gss[l][n].reshape(-1) for n in _SMALL for l in range(len(gss))] + [jnp.sum(sq).reshape(1)]
    flat = jnp.concatenate(parts)
    rows = -(-flat.shape[0] // (8 * _LANES)) * 8
    return jnp.pad(flat, (0, rows * _LANES - flat.shape[0])).reshape(rows, _LANES)


def _unpack_small(total, shapes):
    flat = total.reshape(-1)
    out, off = {}, 0
    for n in _SMALL:
        layers = []
        for _ in range(shapes[n][0]):
            size = 1
            for s in shapes[n][1:]:
                size *= s
            layers.append(flat[off:off + size].reshape(shapes[n][1:]))
            off += size
        out[n] = jnp.stack(layers)
    return out, flat[off]


_GATHER = {
    "band_fwd_0": [(0, "w_proj_a"), (0, "w_proj_b"), (0, "w_out"), (0, "w_ffn_out")],
    "stick_fwd_0": [(0, "w_ffn_in"), (1, "w_in")],
    "ffn_fwd_0": [(1, "w_proj_a"), (1, "w_proj_b"), (1, "w_out"), (1, "w_ffn_in"), (1, "w_ffn_out")],
}
_SCATTER = {
    "band_bwd_1": [(1, "w_ffn_in"), (1, "w_ffn_out")],
    "stick_bwd_1": [(1, "w_proj_a"), (1, "w_proj_b"), (1, "w_out")],
    "band_bwd_0": [(1, "w_in"), (0, "w_ffn_in")],
    "stick_bwd_0": [(0, "w_ffn_out"), (0, "w_proj_a"), (0, "w_proj_b"), (0, "w_out")],
    "in_proj_bwd_0": [(0, "w_in")],
}
_SHARE = {"stick_bwd_1": "band_bwd_1", "band_bwd_0": "stick_bwd_1", "stick_bwd_0": "band_bwd_0", "grad_w_in_0": "stick_bwd_0"}


def _owner(key):
    del key
    return 1


def kernel(x, w_in, b_gate, rel_bias, w_proj_a, w_proj_b, w_out, ln1_g, ln1_b, w_ffn_in, w_ffn_out, ln2_g, ln2_b, loss_target, m_w_in, m_b_gate, m_rel_bias, m_w_proj_a, m_w_proj_b, m_w_out, m_ln1_g, m_ln1_b, m_w_ffn_in, m_w_ffn_out, m_ln2_g, m_ln2_b, v_w_in, v_b_gate, v_rel_bias, v_w_proj_a, v_w_proj_b, v_w_out, v_ln1_g, v_ln1_b, v_w_ffn_in, v_w_ffn_out, v_ln2_g, v_ln2_b):
    names = ("w_in", "b_gate", "rel_bias", "w_proj_a", "w_proj_b", "w_out", "ln1_g", "ln1_b", "w_ffn_in", "w_ffn_out", "ln2_g", "ln2_b")
    w = dict(zip(names, (w_in, b_gate, rel_bias, w_proj_a, w_proj_b, w_out, ln1_g, ln1_b, w_ffn_in, w_ffn_out, ln2_g, ln2_b)))
    m = dict(zip(names, (m_w_in, m_b_gate, m_rel_bias, m_w_proj_a, m_w_proj_b, m_w_out, m_ln1_g, m_ln1_b, m_w_ffn_in, m_w_ffn_out, m_ln2_g, m_ln2_b)))
    v = dict(zip(names, (v_w_in, v_b_gate, v_rel_bias, v_w_proj_a, v_w_proj_b, v_w_out, v_ln1_g, v_ln1_b, v_w_ffn_in, v_w_ffn_out, v_ln2_g, v_ln2_b)))
    T, D = x.shape[-2], x.shape[-1]
    assert w_in.shape[0] == 2, "the exchange schedule below is written for two layers"

    mine = [{n: w[n][l].astype(_MXU) for n in _DENSE} for l in range(2)]
    W = [dict(), dict()]
    gws = [dict(), dict()]
    slots, final = {}, {}

    def gather(keys):
        sizes = [mine[l][n].size for l, n in keys]
        passed, fractions = 0, []
        for s in sizes:
            passed += s
            fractions.append(0.15 + 0.6 * passed / sum(sizes))

        def done(outs):
            for (l, n), o in zip(keys, outs):
                W[l][n] = o
        return _gather_plan([mine[l][n] for l, n in keys], fractions), done

    def scatter(keys):
        comm = _scatter_plan([gws[l][n] for l, n in keys], [_owner(key) for key in keys])
        return comm, lambda outs: slots.update(zip(keys, outs))

    def share(keys):
        reduced = [_sum_slots(slots[key], f"sum_grad_{key[1]}_{key[0]}") for key in keys]
        comm = _share_plan(reduced, [_owner(key) for key in keys])
        return comm, lambda outs: final.update(zip(keys, outs))

    def both(first, second):
        (ca, da), (cb, db) = first, second
        na = len(ca.out_shapes)
        return _join(ca, cb), lambda outs: (da(outs[:na]), db(outs[na:]))

    comm, done = gather([(0, "w_in")])
    done(_comm_only(comm, "gather_first"))
    plans = {key: functools.partial(gather, keys) for key, keys in _GATHER.items()}
    for key, keys in _SCATTER.items():
        plans[key] = functools.partial(scatter, keys)
    for key, scattered_under in _SHARE.items():
        handed = functools.partial(share, _SCATTER[scattered_under])
        carried = plans.get(key)
        plans[key] = handed if carried is None else (lambda carried=carried, handed=handed: both(carried(), handed()))
    small = {n: w[n] for n in _SMALL}
    sq, dx, _, gss = _local_step(x.reshape(T, D), loss_target.reshape(T, D), W, small, _Plans(plans), gws)

    comm, done = share(_SCATTER["in_proj_bwd_0"])
    done(_comm_only(comm, "share_last"))
    total = _all_reduce_small(_pack_small(gss, sq))
    small_grads, sq_all = _unpack_small(total, {n: w[n].shape for n in _SMALL})
    loss = 0.5 * sq_all / D

    grad, delta, new_m, new_v = {}, {}, {}, {}
    for n in names:
        if n in _DENSE:
            updated = _adamw_layers(w[n], [final[(l, n)] for l in range(2)], m[n], v[n], f"adamw_{n}")
        else:
            updated = _adamw(w[n], small_grads[n], m[n], v[n], f"adamw_{n}")
        grad[n], delta[n], new_m[n], new_v[n] = updated
    return (loss, dx.reshape(x.shape), *[grad[n] for n in names], *[delta[n] for n in names],
            *[new_m[n] for n in names], *[new_v[n] for n in names])
```

```python
import functools

import jax
import jax.numpy as jnp
from jax import lax
from jax.experimental import pallas as pl
from jax.experimental.pallas import tpu as pltpu

_MXU = jnp.bfloat16
_ACT = jnp.bfloat16
_F32 = jnp.float32

_HEAD = 64
_CHUNK = 64
_LANES = 128
_TQ = 128
_BAND_TILES = 5
_BIAS_TILES = 9
_REL_CLIP = 256
_LN_EPS = 1e-5
_MASKED = -1e30
_EXP_ZERO_BELOW = -87.34
_SB_WINDOW = 2
_SB_SUBTILES = 4
_BAND_SUBTILES = 4
_VMEM_LIMIT = 56 * 1024 * 1024
_GRAD_ACC_BYTES = 12 * 1024 * 1024

_LR, _B1, _B2, _EPS, _WD, _STEP = 0.001, 0.9, 0.999, 1e-08, 0.01, 10

_MESH = pl.DeviceIdType.MESH


def _dot(a, b):
    return jnp.dot(a, b, preferred_element_type=_F32)


def _dot_nt(a, b):
    return lax.dot_general(a, b, (((1,), (1,)), ((), ())), preferred_element_type=_F32)


def _dot_tn(a, b):
    return lax.dot_general(a, b, (((0,), (0,)), ((), ())), preferred_element_type=_F32)


def _cparams(*sem):
    return pltpu.CompilerParams(dimension_semantics=sem, vmem_limit_bytes=_VMEM_LIMIT)


def _rows(t, c):
    return pl.BlockSpec((t, c), lambda i: (i, 0))


def _whole(shape):
    return pl.BlockSpec(shape, lambda i: tuple(0 for _ in shape))


_ANY = pl.BlockSpec(memory_space=pl.ANY)


def _load_cols(w_hbm, w_vmem, sem):
    n = w_hbm.shape[-1]
    cps = [pltpu.make_async_copy(w_hbm.at[k], w_vmem.at[:, pl.ds(k * n, n)], sem.at[k]) for k in range(4)]
    for cp in cps:
        cp.start()
    for cp in cps:
        cp.wait()


def _load_rows(w_hbm, w_vmem, sem):
    r = w_hbm.shape[-2]
    cps = [pltpu.make_async_copy(w_hbm.at[k], w_vmem.at[pl.ds(k * r, r), :], sem.at[k]) for k in range(4)]
    for cp in cps:
        cp.start()
    for cp in cps:
        cp.wait()


def _ln_stats(u):
    mu = jnp.mean(u, axis=-1, keepdims=True)
    xc = u - mu
    var = jnp.mean(xc * xc, axis=-1, keepdims=True)
    rstd = lax.rsqrt(var + _LN_EPS)
    return xc * rstd, rstd


def _ln_bwd(u, dy, gamma):
    xhat, rstd = _ln_stats(u)
    dxh = dy * gamma
    m1 = jnp.mean(dxh, axis=-1, keepdims=True)
    m2 = jnp.mean(dxh * xhat, axis=-1, keepdims=True)
    du = rstd * (dxh - m1 - xhat * m2)
    return du, jnp.sum(dy * xhat, axis=0, keepdims=True), jnp.sum(dy, axis=0, keepdims=True), xhat


def _divisor_tile(n, cap):
    best = None
    for t in range(_LANES, min(n, cap) + 1, _LANES):
        if n % t == 0:
            best = t
    return best or n


class _Comm:
    def __init__(self, inputs, out_shapes, sems, run, aliases=None):
        self.inputs, self.out_shapes, self.sems, self.run = list(inputs), list(out_shapes), list(sems), run
        self.aliases = aliases or {}


def _call(kern, comm, *, name, grid, in_specs, out_specs, out_shape, scratch_shapes, args, semantics):
    in_specs, out_specs, out_shape, scratch_shapes = list(in_specs), list(out_specs), list(out_shape), list(scratch_shapes)
    if comm is None:
        outs = pl.pallas_call(kern, name=name, grid=grid, in_specs=in_specs, out_specs=out_specs, out_shape=out_shape,
                              scratch_shapes=scratch_shapes, compiler_params=_cparams(*semantics))(*args)
        return list(outs), []
    n_in, n_out, n_scr = len(in_specs), len(out_specs), len(scratch_shapes)
    ci, co = len(comm.inputs), len(comm.out_shapes)
    nsteps = functools.reduce(lambda a, b: a * b, grid, 1)

    def fused(*refs):
        a, b = n_in, n_in + ci
        c, d = b + n_out, b + n_out + co
        e = d + n_scr
        step = pl.program_id(0)
        for ax in range(1, len(grid)):
            step = step * grid[ax] + pl.program_id(ax)
        comm.run(step, nsteps, refs[a:b], refs[c:d], refs[e:])
        kern(*refs[:a], *refs[b:c], *refs[d:e])

    outs = pl.pallas_call(
        fused, name=name, grid=grid, in_specs=in_specs + [_ANY] * ci, out_specs=out_specs + [_ANY] * co,
        out_shape=out_shape + comm.out_shapes, scratch_shapes=scratch_shapes + comm.sems,
        input_output_aliases={n_in + i: n_out + o for i, o in comm.aliases.items()},
        compiler_params=_cparams(*("arbitrary" for _ in grid)))(*args, *comm.inputs)
    return list(outs[:n_out]), list(outs[n_out:])


def _comm_only(comm, name):
    def body(*refs):
        ci, co = len(comm.inputs), len(comm.out_shapes)
        comm.run(0, 1, refs[:ci], refs[ci:ci + co], refs[ci + co:])

    outs = pl.pallas_call(body, name=name, in_specs=[_ANY] * len(comm.inputs), out_specs=[_ANY] * len(comm.out_shapes),
                          out_shape=comm.out_shapes, scratch_shapes=comm.sems,
                          input_output_aliases=dict(comm.aliases))(*comm.inputs)
    return list(outs)


def _in_proj(x, w_in, layer):
    T, D = x.shape
    N = 4 * w_in.shape[-1]
    NQ = N - 2 * D
    tm = 512

    def kern(x_ref, w_hbm, hq_ref, hg_ref, xb_ref, w_v, sem):
        @pl.when(pl.program_id(0) == 0)
        def _():
            _load_cols(w_hbm, w_v, sem)

        xb = x_ref[...].astype(_MXU)
        hq_ref[...] = _dot(xb, w_v[:, :NQ]).astype(hq_ref.dtype)
        hg_ref[...] = _dot(xb, w_v[:, NQ:])
        xb_ref[...] = xb.astype(xb_ref.dtype)

    return pl.pallas_call(
        kern, name=f"in_proj_{layer}", grid=(T // tm,),
        in_specs=[_rows(tm, D), _ANY],
        out_specs=[_rows(tm, NQ), _rows(tm, 2 * D), _rows(tm, D)],
        out_shape=[jax.ShapeDtypeStruct((T, NQ), _ACT), jax.ShapeDtypeStruct((T, 2 * D), _F32),
                   jax.ShapeDtypeStruct((T, D), _ACT)],
        scratch_shapes=[pltpu.VMEM((D, N), w_in.dtype), pltpu.SemaphoreType.DMA((4,))],
        compiler_params=_cparams("arbitrary"),
    )(x, w_in)


def _mix_fwd(oa, ob, hg, x, wpa, wpb, wo, bg, gamma, beta, alpha, layer, comm=None):
    T, D = x.shape
    WA, WB = oa.shape[1], ob.shape[1]
    tm = 256

    def kern(oa_ref, ob_ref, hg_ref, x_ref, bg_ref, g_ref, b_ref, wpa_h, wpb_h, wo_h,
             x1_ref, u1_ref, pre_ref, ya_ref, yb_ref, wpa_v, wpb_v, wo_v, sa, sb, so):
        @pl.when(pl.program_id(0) == 0)
        def _():
            _load_cols(wpa_h, wpa_v, sa)
            _load_cols(wpb_h, wpb_v, sb)
            _load_rows(wo_h, wo_v, so)

        ya = _dot(oa_ref[...].astype(_MXU), wpa_v[...])
        yb = _dot(ob_ref[...].astype(_MXU), wpb_v[...])
        hgv = hg_ref[...]
        bgv = bg_ref[...]
        ga = jax.nn.sigmoid(hgv[:, :D] + bgv[:, :D])
        gb = jax.nn.sigmoid(hgv[:, D:] + bgv[:, D:])
        pre = ga * ya + gb * yb
        mix = _dot(pre.astype(_MXU), wo_v[...])
        u = alpha * x_ref[...] + mix
        xhat, _ = _ln_stats(u)
        x1_ref[...] = xhat * g_ref[...] + b_ref[...]
        u1_ref[...] = u
        pre_ref[...] = pre.astype(pre_ref.dtype)
        ya_ref[...] = ya.astype(ya_ref.dtype)
        yb_ref[...] = yb.astype(yb_ref.dtype)

    return _call(
        kern, comm, name=f"mix_fwd_{layer}", grid=(T // tm,),
        in_specs=[_rows(tm, WA), _rows(tm, WB), _rows(tm, 2 * D), _rows(tm, D),
                  _whole((1, 2 * D)), _whole((1, D)), _whole((1, D)), _ANY, _ANY, _ANY],
        out_specs=[_rows(tm, D)] * 5,
        out_shape=[jax.ShapeDtypeStruct((T, D), _F32), jax.ShapeDtypeStruct((T, D), _F32)]
        + [jax.ShapeDtypeStruct((T, D), _ACT)] * 3,
        scratch_shapes=[pltpu.VMEM((WA, D), wpa.dtype), pltpu.VMEM((WB, D), wpb.dtype), pltpu.VMEM((D, D), wo.dtype),
                        pltpu.SemaphoreType.DMA((4,)), pltpu.SemaphoreType.DMA((4,)), pltpu.SemaphoreType.DMA((4,))],
        args=(oa, ob, hg, x, bg, gamma, beta, wpa, wpb, wo), semantics=("arbitrary",))


def _ffn_fwd(x1, wfi, wfo, gamma, beta, alpha, layer, comm=None):
    T, D = x1.shape
    F2 = 4 * wfi.shape[-1]
    F = F2 // 2
    tm = 256
    fc = F // 2

    def kern(x_ref, g_ref, b_ref, wi_h, wo_h, x2_ref, u2_ref, act_ref, gu_ref, xb_ref, wi_v, wo_v, si, so):
        @pl.when(pl.program_id(0) == 0)
        def _():
            _load_cols(wi_h, wi_v, si)
            _load_rows(wo_h, wo_v, so)

        x = x_ref[...]
        xb = x.astype(_MXU)
        xb_ref[...] = xb.astype(xb_ref.dtype)
        ffn = jnp.zeros((tm, D), _F32)
        for c in range(2):
            g = _dot(xb, wi_v[:, c * fc:(c + 1) * fc])
            u = _dot(xb, wi_v[:, F + c * fc:F + (c + 1) * fc])
            act = g * jax.nn.sigmoid(g) * u
            ab = act.astype(_MXU)
            ffn = ffn + _dot(ab, wo_v[c * fc:(c + 1) * fc, :])
            act_ref[:, c * fc:(c + 1) * fc] = ab.astype(act_ref.dtype)
            gu_ref[:, c * fc:(c + 1) * fc] = g.astype(gu_ref.dtype)
            gu_ref[:, F + c * fc:F + (c + 1) * fc] = u.astype(gu_ref.dtype)
        uu = alpha * x + ffn
        xhat, _ = _ln_stats(uu)
        x2_ref[...] = xhat * g_ref[...] + b_ref[...]
        u2_ref[...] = uu

    return _call(
        kern, comm, name=f"ffn_fwd_{layer}", grid=(T // tm,),
        in_specs=[_rows(tm, D), _whole((1, D)), _whole((1, D)), _ANY, _ANY],
        out_specs=[_rows(tm, D), _rows(tm, D), _rows(tm, F), _rows(tm, F2), _rows(tm, D)],
        out_shape=[jax.ShapeDtypeStruct((T, D), _F32), jax.ShapeDtypeStruct((T, D), _F32),
                   jax.ShapeDtypeStruct((T, F), _ACT), jax.ShapeDtypeStruct((T, F2), _ACT),
                   jax.ShapeDtypeStruct((T, D), _ACT)],
        scratch_shapes=[pltpu.VMEM((D, F2), wfi.dtype), pltpu.VMEM((F, D), wfo.dtype),
                        pltpu.SemaphoreType.DMA((4,)), pltpu.SemaphoreType.DMA((4,))],
        args=(x1, gamma, beta, wfi, wfo), semantics=("arbitrary",))


def _ffn_bwd(u2, dy_or_target, gu, gamma, beta, wfi, wfo, alpha, layer, last):
    T, D = u2.shape
    F2 = gu.shape[1]
    F = F2 // 2
    tm = 256
    fc = F // 2

    def kern(u_ref, dy_ref, gu_ref, g_ref, b_ref, wi_h, wo_h, dx_ref, dub_ref, dgu_ref, st_ref, wi_v, wo_v, si, so):
        @pl.when(pl.program_id(0) == 0)
        def _():
            _load_cols(wi_h, wi_v, si)
            _load_rows(wo_h, wo_v, so)
            st_ref[...] = jnp.zeros_like(st_ref)

        gam = g_ref[...]
        u = u_ref[...]
        if last:
            xhat0, _ = _ln_stats(u)
            err = xhat0 * gam + b_ref[...] - dy_ref[...]
            dy = err * (1.0 / D)
            st_ref[2:3, :] += jnp.sum(err * err, axis=0, keepdims=True)
        else:
            dy = dy_ref[...]
        du, dgam, dbet, _ = _ln_bwd(u, dy, gam)
        st_ref[0:1, :] += dgam
        st_ref[1:2, :] += dbet
        dub = du.astype(_MXU)
        dub_ref[...] = dub.astype(dub_ref.dtype)
        dx = alpha * du
        for c in range(2):
            dact = _dot_nt(dub, wo_v[c * fc:(c + 1) * fc, :])
            g = gu_ref[:, c * fc:(c + 1) * fc].astype(_F32)
            uu = gu_ref[:, F + c * fc:F + (c + 1) * fc].astype(_F32)
            sg = jax.nn.sigmoid(g)
            dg = (dact * uu * (sg * (1.0 + g * (1.0 - sg)))).astype(_MXU)
            dup = (dact * (g * sg)).astype(_MXU)
            dgu_ref[:, c * fc:(c + 1) * fc] = dg.astype(dgu_ref.dtype)
            dgu_ref[:, F + c * fc:F + (c + 1) * fc] = dup.astype(dgu_ref.dtype)
            dx = dx + _dot_nt(dg, wi_v[:, c * fc:(c + 1) * fc]) + _dot_nt(dup, wi_v[:, F + c * fc:F + (c + 1) * fc])
        dx_ref[...] = dx

    return pl.pallas_call(
        kern, name=f"ffn_bwd_{layer}", grid=(T // tm,),
        in_specs=[_rows(tm, D), _rows(tm, D), _rows(tm, F2), _whole((1, D)), _whole((1, D)), _ANY, _ANY],
        out_specs=[_rows(tm, D), _rows(tm, D), _rows(tm, F2), _whole((8, D))],
        out_shape=[jax.ShapeDtypeStruct((T, D), _F32), jax.ShapeDtypeStruct((T, D), _ACT),
                   jax.ShapeDtypeStruct((T, F2), _ACT), jax.ShapeDtypeStruct((8, D), _F32)],
        scratch_shapes=[pltpu.VMEM((D, F2), wfi.dtype), pltpu.VMEM((F, D), wfo.dtype),
                        pltpu.SemaphoreType.DMA((4,)), pltpu.SemaphoreType.DMA((4,))],
        compiler_params=_cparams("arbitrary"),
    )(u2, dy_or_target, gu, gamma, beta, wfi, wfo)


def _residual_nt(res, res_scale, d, w, name, comm=None):
    T, K = res.shape
    N = d.shape[1]
    tm = 512

    def kern(r_ref, d_ref, w_hbm, o_ref, w_v, sem):
        @pl.when(pl.program_id(0) == 0)
        def _():
            _load_cols(w_hbm, w_v, sem)

        o_ref[...] = res_scale * r_ref[...] + _dot_nt(d_ref[...].astype(_MXU), w_v[...])

    outs, extra = _call(
        kern, comm, name=name, grid=(T // tm,),
        in_specs=[_rows(tm, K), _rows(tm, N), _ANY], out_specs=[_rows(tm, K)],
        out_shape=[jax.ShapeDtypeStruct((T, K), _F32)],
        scratch_shapes=[pltpu.VMEM((K, N), w.dtype), pltpu.SemaphoreType.DMA((4,))],
        args=(res, d, w), semantics=("arbitrary",))
    return outs[0], extra


def _mix_bwd(u1, dx1, ya, yb, hg, wpa, wpb, wo, bg, gamma, layer):
    T, D = u1.shape
    WA, WB = wpa.shape[-2], wpb.shape[-2]
    tm = 256

    def kern(u_ref, dx_ref, ya_ref, yb_ref, hg_ref, bg_ref, g_ref, wpa_h, wpb_h, wo_h,
             du_ref, dub_ref, dya_ref, dyb_ref, dhg_ref, doa_ref, dob_ref, st_ref,
             wpa_v, wpb_v, wo_v, sa, sb, so):
        @pl.when(pl.program_id(0) == 0)
        def _():
            _load_cols(wpa_h, wpa_v, sa)
            _load_cols(wpb_h, wpb_v, sb)
            _load_rows(wo_h, wo_v, so)
            st_ref[...] = jnp.zeros_like(st_ref)

        du, dgam, dbet, _ = _ln_bwd(u_ref[...], dx_ref[...], g_ref[...])
        st_ref[1:2, :D] += dgam
        st_ref[1:2, D:] += dbet
        du_ref[...] = du
        dub = du.astype(_MXU)
        dub_ref[...] = dub.astype(dub_ref.dtype)
        dpre = _dot_nt(dub, wo_v[...])
        hgv = hg_ref[...]
        bgv = bg_ref[...]
        ga = jax.nn.sigmoid(hgv[:, :D] + bgv[:, :D])
        gb = jax.nn.sigmoid(hgv[:, D:] + bgv[:, D:])
        dya = (dpre * ga).astype(_MXU)
        dyb = (dpre * gb).astype(_MXU)
        dsa = dpre * ya_ref[...].astype(_F32) * (ga * (1.0 - ga))
        dsb = dpre * yb_ref[...].astype(_F32) * (gb * (1.0 - gb))
        st_ref[0:1, :D] += jnp.sum(dsa, axis=0, keepdims=True)
        st_ref[0:1, D:] += jnp.sum(dsb, axis=0, keepdims=True)
        dya_ref[...] = dya.astype(dya_ref.dtype)
        dyb_ref[...] = dyb.astype(dyb_ref.dtype)
        dhg_ref[:, :D] = dsa.astype(dhg_ref.dtype)
        dhg_ref[:, D:] = dsb.astype(dhg_ref.dtype)
        doa_ref[...] = _dot_nt(dya, wpa_v[...]).astype(doa_ref.dtype)
        dob_ref[...] = _dot_nt(dyb, wpb_v[...]).astype(dob_ref.dtype)

    return pl.pallas_call(
        kern, name=f"mix_bwd_{layer}", grid=(T // tm,),
        in_specs=[_rows(tm, D)] * 4 + [_rows(tm, 2 * D), _whole((1, 2 * D)), _whole((1, D)), _ANY, _ANY, _ANY],
        out_specs=[_rows(tm, D)] * 4 + [_rows(tm, 2 * D), _rows(tm, WA), _rows(tm, WB), _whole((8, 2 * D))],
        out_shape=[jax.ShapeDtypeStruct((T, D), _F32)] + [jax.ShapeDtypeStruct((T, D), _ACT)] * 3
        + [jax.ShapeDtypeStruct((T, 2 * D), _ACT), jax.ShapeDtypeStruct((T, WA), _ACT),
           jax.ShapeDtypeStruct((T, WB), _ACT), jax.ShapeDtypeStruct((8, 2 * D), _F32)],
        scratch_shapes=[pltpu.VMEM((WA, D), wpa.dtype), pltpu.VMEM((WB, D), wpb.dtype), pltpu.VMEM((D, D), wo.dtype),
                        pltpu.SemaphoreType.DMA((4,)), pltpu.SemaphoreType.DMA((4,)), pltpu.SemaphoreType.DMA((4,))],
        compiler_params=_cparams("arbitrary"),
    )(u1, dx1, ya, yb, hg, bg, gamma, wpa, wpb, wo)


def _grad_w(a, b, *, col_shards, name, comm=None):
    T, M = a.shape
    N = b.shape[1]
    tk = 512
    n = N // 4 if col_shards else N
    whole = M * N * 4 <= _GRAD_ACC_BYTES
    tn = N if whole else (n if col_shards else _divisor_tile(N, _GRAD_ACC_BYTES // (4 * M)))
    nk = T // tk

    def kern(a_ref, b_ref, o_ref, acc):
        k = pl.program_id(1)

        @pl.when(k == 0)
        def _():
            acc[...] = jnp.zeros_like(acc)

        acc[...] += _dot_tn(a_ref[...].astype(_MXU), b_ref[...].astype(_MXU))

        @pl.when(k == nk - 1)
        def _():
            if col_shards and whole:
                for s in range(4):
                    o_ref[s] = acc[:, s * n:(s + 1) * n].astype(o_ref.dtype)
            else:
                o_ref[...] = acc[...].astype(o_ref.dtype)

    if col_shards:
        out_spec = (pl.BlockSpec((4, M, n), lambda j, k: (0, 0, 0)) if whole
                    else pl.BlockSpec((None, M, n), lambda j, k: (j, 0, 0)))
        out_shape = jax.ShapeDtypeStruct((4, M, n), _ACT)
    else:
        out_spec = pl.BlockSpec((M, tn), lambda j, k: (0, j))
        out_shape = jax.ShapeDtypeStruct((M, N), _ACT)
    outs, extra = _call(
        kern, comm, name=name, grid=(N // tn, nk),
        in_specs=[pl.BlockSpec((tk, M), lambda j, k: (k, 0)), pl.BlockSpec((tk, tn), lambda j, k: (k, j))],
        out_specs=[out_spec], out_shape=[out_shape], scratch_shapes=[pltpu.VMEM((M, tn), _F32)],
        args=(a, b), semantics=("parallel", "arbitrary"))
    return outs[0], extra


def _bias_tiles(rel):
    H = rel.shape[0]
    span = _TQ * _BAND_TILES - 1
    edge = span - _REL_CLIP
    gvec = jnp.concatenate([jnp.broadcast_to(rel[:, :1], (H, edge)), rel, jnp.broadcast_to(rel[:, -1:], (H, edge))], axis=1)
    width = _BIAS_TILES * _TQ
    period = width + _TQ
    tiled = jnp.broadcast_to(jnp.pad(gvec[:, ::-1], ((0, 0), (0, 1)))[:, None, :], (H, _TQ, period))
    rows = tiled.reshape(H, _TQ * period)[:, :_TQ * (period - 1)].reshape(H, _TQ, period - 1)[:, :, _TQ - 1:]
    r = jnp.arange(_TQ)[:, None]
    u = jnp.arange(width)[None, :]
    d = 4 * _TQ + r - u
    rm = r % _CHUNK
    valid = (d >= rm - (_CHUNK - 1)) & (d <= rm + 8 * _CHUNK)
    tiles = jnp.where(valid[None], rows, _MASKED)
    return tiles.reshape(H // 2, 2 * _TQ, _BIAS_TILES, _TQ).transpose(0, 2, 1, 3)


def _fold_bias_grad(db):
    H = 2 * db.shape[0]
    width = _BIAS_TILES * _TQ
    period = width + _TQ
    x = jnp.pad(db.transpose(0, 2, 1, 3).reshape(H, _TQ, width), ((0, 0), (0, 0), (_TQ - 1, 0)))
    skew = jnp.pad(x.reshape(H, _TQ * (period - 1)), ((0, 0), (0, _TQ))).reshape(H, _TQ, period)
    dg = skew.sum(axis=1)[:, :period - 1][:, ::-1]
    span = _TQ * _BAND_TILES - 1
    edge = span - _REL_CLIP
    mid = dg[:, edge:edge + 2 * _REL_CLIP + 1]
    lo = dg[:, :edge].sum(axis=1)
    hi = dg[:, edge + 2 * _REL_CLIP + 1:].sum(axis=1)
    return mid.at[:, 0].add(lo).at[:, -1].add(hi)


def _band_window(i):
    j0 = jnp.maximum(i - (_BAND_TILES - 1), 0)
    return j0, (_BAND_TILES - 1) - (i - j0)


def _head_masks():
    lane = lax.broadcasted_iota(jnp.int32, (1, _LANES), 1)
    return [(lane // _HEAD) == hh for hh in range(2)]


def _stack_heads(x, masks):
    return jnp.concatenate([jnp.where(m, x, jnp.zeros_like(x)) for m in masks], axis=0)


def _unstack_heads(y, masks):
    return jnp.where(masks[0], y[:_TQ], y[_TQ:])


def _scaled(q):
    return q * jnp.asarray(_HEAD ** -0.5, q.dtype)


def _band_probs(q2, k_ref, b_ref, j0, boff):
    s = []
    for j in range(_BAND_TILES):
        kj = k_ref[pl.ds(pl.multiple_of((j0 + j) * _TQ, _TQ), _TQ), :]
        s.append(_dot_nt(q2, kj) + b_ref[boff + j])
    m = jnp.max(functools.reduce(jnp.maximum, s), axis=-1, keepdims=True)
    p = [jnp.exp(x - m) for x in s]
    l = jnp.sum(functools.reduce(lambda a, b: a + b, p), axis=-1, keepdims=True)
    return p, 1.0 / l


def _qkv_specs(T, cb, npair, tq=_TQ):
    return [pl.BlockSpec((tq, _LANES), lambda h, i: (i, cb + h)),
            pl.BlockSpec((T, _LANES), lambda h, i: (0, cb + npair + h)),
            pl.BlockSpec((T, _LANES), lambda h, i: (0, cb + 2 * npair + h))]


def _attn_a_fwd(hq, bias, col0, width, layer, comm=None):
    T = hq.shape[0]
    npair = width // _LANES
    nsub = _BAND_SUBTILES
    tq = nsub * _TQ

    def kern(q_ref, k_ref, v_ref, b_ref, o_ref):
        masks = _head_masks()
        q = _scaled(q_ref[...])
        for s in range(nsub):
            part = slice(s * _TQ, (s + 1) * _TQ)
            j0, boff = _band_window(nsub * pl.program_id(1) + s)
            p, inv = _band_probs(_stack_heads(q[part], masks), k_ref, b_ref, j0, boff)
            o = jnp.zeros((2 * _TQ, _LANES), _F32)
            for j in range(_BAND_TILES):
                vj = v_ref[pl.ds(pl.multiple_of((j0 + j) * _TQ, _TQ), _TQ), :]
                o = o + _dot(p[j].astype(_MXU), vj)
            o_ref[part, :] = _unstack_heads(o * inv, masks).astype(o_ref.dtype)

    outs, extra = _call(
        kern, comm, name=f"band_attn_fwd_{layer}", grid=(npair, T // tq),
        in_specs=_qkv_specs(T, col0 // _LANES, npair, tq)
        + [pl.BlockSpec((None, _BIAS_TILES, 2 * _TQ, _TQ), lambda h, i: (h, 0, 0, 0))],
        out_specs=[pl.BlockSpec((tq, _LANES), lambda h, i: (i, h))],
        out_shape=[jax.ShapeDtypeStruct((T, width), _ACT)], scratch_shapes=[],
        args=(hq, hq, hq, bias), semantics=("arbitrary", "arbitrary"))
    return outs[0], extra


def _attn_a_bwd(hq, bias, do, col0, width, layer, comm=None):
    T = hq.shape[0]
    npair = width // _LANES
    nsub = _BAND_SUBTILES
    tq = nsub * _TQ
    nq = T // tq
    scale = _HEAD ** -0.5

    def kern(q_ref, k_ref, v_ref, b_ref, do_ref, dq_ref, dk_ref, dv_ref, db_ref, dk_acc, dv_acc):
        i = pl.program_id(1)

        @pl.when(i == 0)
        def _():
            dk_acc[...] = jnp.zeros_like(dk_acc)
            dv_acc[...] = jnp.zeros_like(dv_acc)
            db_ref[...] = jnp.zeros_like(db_ref)

        masks = _head_masks()
        q = _scaled(q_ref[...])
        do_t = do_ref[...]
        for s in range(nsub):
            part = slice(s * _TQ, (s + 1) * _TQ)
            j0, boff = _band_window(nsub * i + s)
            q2 = _stack_heads(q[part], masks)
            do2 = _stack_heads(do_t[part], masks).astype(_MXU)
            p, inv = _band_probs(q2, k_ref, b_ref, j0, boff)
            rows = [pl.ds(pl.multiple_of((j0 + j) * _TQ, _TQ), _TQ) for j in range(_BAND_TILES)]
            p = [x * inv for x in p]
            dp = [_dot_nt(do2, v_ref[rows[j], :]) for j in range(_BAND_TILES)]
            delta = jnp.sum(functools.reduce(lambda a, b: a + b, [p[j] * dp[j] for j in range(_BAND_TILES)]),
                            axis=-1, keepdims=True)
            dq = jnp.zeros((2 * _TQ, _LANES), _F32)
            for j in range(_BAND_TILES):
                ds = p[j] * (dp[j] - delta)
                db_ref[boff + j] += ds
                dsb = ds.astype(_MXU)
                dq = dq + _dot(dsb, k_ref[rows[j], :])
                dk_acc[rows[j], :] += _dot_tn(dsb, q2)
                dv_acc[rows[j], :] += _dot_tn(p[j].astype(_MXU), do2)
            dq_ref[part, :] = (_unstack_heads(dq, masks) * scale).astype(dq_ref.dtype)

        @pl.when(i == nq - 1)
        def _():
            dk_ref[...] = dk_acc[...].astype(dk_ref.dtype)
            dv_ref[...] = dv_acc[...].astype(dv_ref.dtype)

    strip = pl.BlockSpec((None, _BIAS_TILES, 2 * _TQ, _TQ), lambda h, i: (h, 0, 0, 0))
    tile = pl.BlockSpec((tq, _LANES), lambda h, i: (i, h))
    column = pl.BlockSpec((T, _LANES), lambda h, i: (0, h))
    outs, extra = _call(
        kern, comm, name=f"band_attn_bwd_{layer}", grid=(npair, nq),
        in_specs=_qkv_specs(T, col0 // _LANES, npair, tq) + [strip, tile],
        out_specs=[tile, column, column, strip],
        out_shape=[jax.ShapeDtypeStruct((T, width), _ACT)] * 3
        + [jax.ShapeDtypeStruct((npair, _BIAS_TILES, 2 * _TQ, _TQ), _F32)],
        scratch_shapes=[pltpu.VMEM((T, _LANES), _F32), pltpu.VMEM((T, _LANES), _F32)],
        args=(hq, hq, hq, bias, do), semantics=("arbitrary", "arbitrary"))
    return outs, extra


def _suffix_matrix():
    r = lax.broadcasted_iota(jnp.int32, (_TQ, _TQ), 0)
    c = lax.broadcasted_iota(jnp.int32, (_TQ, _TQ), 1)
    r2 = lax.broadcasted_iota(jnp.int32, (2 * _TQ, _TQ), 0)
    c2 = lax.broadcasted_iota(jnp.int32, (2 * _TQ, _TQ), 1)
    return (r > c).astype(_MXU), c2 - (r2 & (_TQ - 1))


def _suffix_sums(xs, tri):
    n, k = xs[0].shape[0], len(xs)
    his = [x.astype(_MXU) for x in xs]
    los = [(x - h.astype(_F32)).astype(_MXU) for x, h in zip(xs, his)]
    y = _dot(jnp.concatenate(his + los, axis=0), tri)
    return [y[j * n:(j + 1) * n] + y[(k + j) * n:(k + j + 1) * n] for j in range(k)]


def _stick_tiles(tiles, rel, carry_l, tri):
    zs = [_dot_nt(qs, kj) for qs, kj, _, _ in tiles]
    Ls, masks = [], []
    for z, (_, _, jj, _) in zip(zs, tiles):
        nsp = -(jnp.maximum(z, 0.0) + jnp.log(1.0 + jnp.exp(-jnp.abs(z))))
        if isinstance(jj, int):
            mask = (rel < 0) if jj == 0 else None
        else:
            mask = rel < jnp.where(jj == 0, 0, _TQ)
        Ls.append(nsp if mask is None else jnp.where(mask, nsp, 0.0))
        masks.append(mask)
    carry_l = list(carry_l)
    ws = []
    for z, L, suffix, mask, (_, _, _, sub) in zip(zs, Ls, _suffix_sums(Ls, tri), masks, tiles):
        w = jnp.exp(z + L + suffix + carry_l[sub])
        ws.append(w if mask is None else jnp.where(mask, w, 0.0))
        carry_l[sub] = carry_l[sub] + jnp.sum(L, axis=-1, keepdims=True)
    return zs, Ls, ws, masks, carry_l


def _sweep(i, step, zero):
    nsub = _SB_SUBTILES

    def window():
        tiles = [(s, jj) for jj in range(_SB_WINDOW) for s in range(nsub)]
        return tuple((jnp.int32(_SB_WINDOW),) + c for c in step(tiles, [zero] * nsub))

    start = lax.cond(i >= -(-(_SB_WINDOW - 1) // nsub), window, lambda: tuple((jnp.int32(0),) + zero for _ in range(nsub)))
    outs = []
    for s in range(nsub):
        def done(c, s=s):
            return jnp.logical_or(c[0] > nsub * i + s, jnp.max(c[1]) < _EXP_ZERO_BELOW)

        def more(c, s=s):
            carries = [None] * nsub
            carries[s] = c[1:]
            return (c[0] + 1,) + step([(s, c[0])], carries)[s]

        outs.append(lax.while_loop(lambda c, done=done: jnp.logical_not(done(c)), more, start[s]))
    return outs


def _sb_fwd(hq, col0, width, layer, comm=None):
    T = hq.shape[0]
    npair = width // _LANES
    nsub = _SB_SUBTILES
    tq = nsub * _TQ

    def kern(q_ref, k_ref, v_ref, o_ref):
        i = pl.program_id(1)
        masks = _head_masks()
        tri, rel = _suffix_matrix()
        q = _scaled(q_ref[...])
        q2 = [_stack_heads(q[s * _TQ:(s + 1) * _TQ], masks) for s in range(nsub)]

        def step(tiles, carries):
            rows = [pl.ds(pl.multiple_of((nsub * i + s - jj) * _TQ, _TQ), _TQ) for s, jj in tiles]
            cls = [None if c is None else c[0] for c in carries]
            accs = [None if c is None else c[1] for c in carries]
            _, _, ws, _, cls = _stick_tiles([(q2[s], k_ref[r, :], jj, s) for (s, jj), r in zip(tiles, rows)], rel, cls, tri)
            for w, r, (s, _) in zip(ws, rows, tiles):
                accs[s] = accs[s] + _dot(w.astype(_MXU), v_ref[r, :])
            return [None if c is None else (cls[s], accs[s]) for s, c in enumerate(carries)]

        outs = _sweep(i, step, (jnp.zeros((2 * _TQ, 1), _F32), jnp.zeros((2 * _TQ, _LANES), _F32)))
        for s in range(nsub):
            o_ref[s * _TQ:(s + 1) * _TQ, :] = _unstack_heads(outs[s][2], masks)

    outs, extra = _call(
        kern, comm, name=f"stick_attn_fwd_{layer}", grid=(npair, T // tq),
        in_specs=_qkv_specs(T, col0 // _LANES, npair, tq),
        out_specs=[pl.BlockSpec((tq, _LANES), lambda h, i: (i, h))],
        out_shape=[jax.ShapeDtypeStruct((T, width), _F32)], scratch_shapes=[],
        args=(hq, hq, hq), semantics=("arbitrary", "arbitrary"))
    return outs[0], extra


def _sb_bwd(hq, o, do, col0, width, layer, comm=None):
    T = hq.shape[0]
    npair = width // _LANES
    nsub = _SB_SUBTILES
    tq = nsub * _TQ
    nq = T // tq
    scale = _HEAD ** -0.5

    def kern(q_ref, k_ref, v_ref, o_ref, do_ref, dq_ref, dk_ref, dv_ref, dk_acc, dv_acc):
        i = pl.program_id(1)

        @pl.when(i == 0)
        def _():
            dk_acc[...] = jnp.zeros_like(dk_acc)
            dv_acc[...] = jnp.zeros_like(dv_acc)

        masks = _head_masks()
        tri, rel = _suffix_matrix()
        q = _scaled(q_ref[...])
        do_t = do_ref[...]
        prod = do_t.astype(_F32) * o_ref[...]
        part = [slice(s * _TQ, (s + 1) * _TQ) for s in range(nsub)]
        q2 = [_stack_heads(q[p], masks) for p in part]
        do2 = [_stack_heads(do_t[p], masks).astype(_MXU) for p in part]
        dsum = [jnp.sum(_stack_heads(prod[p], masks), axis=-1, keepdims=True) for p in part]

        def step(tiles, carries):
            rows = [pl.ds(pl.multiple_of((nsub * i + s - jj) * _TQ, _TQ), _TQ) for s, jj in tiles]
            kjs = [k_ref[r, :] for r in rows]
            cls, cgs, dqs = ([None if c is None else c[n] for c in carries] for n in range(3))
            zs, Ls, ws, tile_masks, cls = _stick_tiles([(q2[s], kj, jj, s) for (s, jj), kj in zip(tiles, kjs)], rel, cls, tri)
            wbs = [w.astype(_MXU) for w in ws]
            gs = [wb.astype(_F32) * _dot_nt(do2[s], v_ref[r, :]) for wb, r, (s, _) in zip(wbs, rows, tiles)]
            for z, L, g, later, mask, wb, kj, r, (s, _) in zip(zs, Ls, gs, _suffix_sums(gs, tri), tile_masks, wbs, kjs,
                                                               rows, tiles):
                dz = g - jnp.exp(z + L) * (dsum[s] - (later + cgs[s]))
                if mask is not None:
                    dz = jnp.where(mask, dz, 0.0)
                dzb = dz.astype(_MXU)
                dk_acc[r, :] += _dot_tn(dzb, q2[s])
                dv_acc[r, :] += _dot_tn(wb, do2[s])
                dqs[s] = dqs[s] + _dot(dzb, kj)
                cgs[s] = cgs[s] + jnp.sum(g, axis=-1, keepdims=True)
            return [None if c is None else (cls[s], cgs[s], dqs[s]) for s, c in enumerate(carries)]

        zc = jnp.zeros((2 * _TQ, 1), _F32)
        outs = _sweep(i, step, (zc, zc, jnp.zeros((2 * _TQ, _LANES), _F32)))
        for s in range(nsub):
            dq_ref[part[s], :] = (_unstack_heads(outs[s][3], masks) * scale).astype(dq_ref.dtype)

        @pl.when(i == nq - 1)
        def _():
            dk_ref[...] = dk_acc[...].astype(dk_ref.dtype)
            dv_ref[...] = dv_acc[...].astype(dv_ref.dtype)

    tile_spec = pl.BlockSpec((tq, _LANES), lambda h, i: (i, h))
    column = pl.BlockSpec((T, _LANES), lambda h, i: (0, h))
    outs, extra = _call(
        kern, comm, name=f"stick_attn_bwd_{layer}", grid=(npair, nq),
        in_specs=_qkv_specs(T, col0 // _LANES, npair, tq) + [tile_spec, tile_spec],
        out_specs=[tile_spec, column, column],
        out_shape=[jax.ShapeDtypeStruct((T, width), _ACT)] * 3,
        scratch_shapes=[pltpu.VMEM((T, _LANES), _F32), pltpu.VMEM((T, _LANES), _F32)],
        args=(hq, hq, hq, o, do), semantics=("arbitrary", "arbitrary"))
    return outs, extra


_DENSE = ("w_in", "w_proj_a", "w_proj_b", "w_out", "w_ffn_in", "w_ffn_out")
_COL_SHARDED = {"w_in": True, "w_proj_a": True, "w_proj_b": True, "w_out": False, "w_ffn_in": True, "w_ffn_out": False}
_SMALL = ("b_gate", "rel_bias", "ln1_g", "ln1_b", "ln2_g", "ln2_b")


class _Plans:
    def __init__(self, plans=None):
        self.plans = plans or {}

    def start(self, key):
        if key not in self.plans:
            return None, None
        return self.plans[key]()

    @staticmethod
    def finish(done, extra):
        if done is not None:
            done(extra)


def _layer_fwd(x, W, small, l, alpha, plans):
    WA = small["rel_bias"].shape[1] * _HEAD
    row = lambda v: v[l].reshape(1, -1)
    hq, hg, xb = _in_proj(x, W["w_in"], l)
    WB = (hq.shape[1] - 3 * WA) // 3
    bias = _bias_tiles(small["rel_bias"][l])
    comm, done = plans.start(f"band_fwd_{l}")
    oa, extra = _attn_a_fwd(hq, bias, 0, WA, l, comm)
    plans.finish(done, extra)
    comm, done = plans.start(f"stick_fwd_{l}")
    ob, extra = _sb_fwd(hq, 3 * WA, WB, l, comm)
    plans.finish(done, extra)
    comm, done = plans.start(f"mix_fwd_{l}")
    (x1, u1, pre, ya, yb), extra = _mix_fwd(oa, ob, hg, x, W["w_proj_a"], W["w_proj_b"], W["w_out"], row(small["b_gate"]),
                                            row(small["ln1_g"]), row(small["ln1_b"]), alpha, l, comm)
    plans.finish(done, extra)
    comm, done = plans.start(f"ffn_fwd_{l}")
    (x2, u2, act, gu, x1b), extra = _ffn_fwd(x1, W["w_ffn_in"], W["w_ffn_out"], row(small["ln2_g"]),
                                             row(small["ln2_b"]), alpha, l, comm)
    plans.finish(done, extra)
    return x2, dict(xb=xb, hq=hq, hg=hg, bias=bias, oa=oa, ob=ob, x1b=x1b, u1=u1, pre=pre, ya=ya, yb=yb, u2=u2,
                    act=act, gu=gu)


def _layer_bwd(dy_or_target, S, W, small, l, last, alpha, plans, gw):
    D = S["xb"].shape[1]
    WA, WB = S["oa"].shape[1], S["ob"].shape[1]
    row = lambda v: v[l].reshape(1, -1)

    def blocks(g, n):
        return g if _COL_SHARDED[n] else g.reshape(4, g.shape[0] // 4, g.shape[1])

    dx1, du2b, dgu, st2 = _ffn_bwd(S["u2"], dy_or_target, S["gu"], row(small["ln2_g"]), row(small["ln2_b"]),
                                   W["w_ffn_in"], W["w_ffn_out"], alpha, l, last)
    gw["w_ffn_in"] = blocks(_grad_w(S["x1b"], dgu, col_shards=True, name=f"grad_w_ffn_in_{l}")[0], "w_ffn_in")
    gw["w_ffn_out"] = blocks(_grad_w(S["act"], du2b, col_shards=False, name=f"grad_w_ffn_out_{l}")[0], "w_ffn_out")
    du1, du1b, dya, dyb, dhg, doa, dob, st1 = _mix_bwd(S["u1"], dx1, S["ya"], S["yb"], S["hg"], W["w_proj_a"],
                                                       W["w_proj_b"], W["w_out"], row(small["b_gate"]),
                                                       row(small["ln1_g"]), l)
    gw["w_out"] = blocks(_grad_w(S["pre"], du1b, col_shards=False, name=f"grad_w_out_{l}")[0], "w_out")
    gw["w_proj_a"] = blocks(_grad_w(S["oa"], dya, col_shards=True, name=f"grad_w_proj_a_{l}")[0], "w_proj_a")
    gw["w_proj_b"] = blocks(_grad_w(S["ob"], dyb, col_shards=True, name=f"grad_w_proj_b_{l}")[0], "w_proj_b")
    comm, done = plans.start(f"band_bwd_{l}")
    (dqa, dka, dva, dbias), extra = _attn_a_bwd(S["hq"], S["bias"], doa, 0, WA, l, comm)
    plans.finish(done, extra)
    comm, done = plans.start(f"stick_bwd_{l}")
    (dqb, dkb, dvb), extra = _sb_bwd(S["hq"], S["ob"], dob, 3 * WA, WB, l, comm)
    plans.finish(done, extra)
    dh = jnp.concatenate([dqa, dka, dva, dqb, dkb, dvb, dhg], axis=1)
    comm, done = plans.start(f"grad_w_in_{l}")
    g, extra = _grad_w(S["xb"], dh, col_shards=True, name=f"grad_w_in_{l}", comm=comm)
    gw["w_in"] = blocks(g, "w_in")
    plans.finish(done, extra)
    comm, done = plans.start(f"in_proj_bwd_{l}")
    dx, extra = _residual_nt(du1, alpha, dh, W["w_in"], f"in_proj_bwd_{l}", comm)
    plans.finish(done, extra)
    gs = dict(b_gate=st1[0], rel_bias=_fold_bias_grad(dbias), ln1_g=st1[1, :D], ln1_b=st1[1, D:],
              ln2_g=st2[0], ln2_b=st2[1])
    return dx, gs, st2[2]


def _local_step(x, target, W, small, plans=None, gws=None):
    depth = len(W)
    alpha = float((2 * depth) ** 0.25)
    plans = plans or _Plans()
    gws = gws if gws is not None else [dict() for _ in range(depth)]
    saved = []
    h = x
    for l in range(depth):
        h, S = _layer_fwd(h, W[l], small, l, alpha, plans)
        saved.append(S)
    gss = [None] * depth
    d = target
    sq = None
    for l in reversed(range(depth)):
        d, gss[l], sq_l = _layer_bwd(d, saved[l], W[l], small, l, l == depth - 1, alpha, plans, gws[l])
        if l == depth - 1:
            sq = sq_l
    return sq, d, gws, gss


def _place():
    return lax.axis_index("x"), lax.axis_index("y"), lax.axis_index("c")


def _remote(src, dst, send_sem, recv_sem, to):
    return pltpu.make_async_remote_copy(src_ref=src, dst_ref=dst, send_sem=send_sem, recv_sem=recv_sem,
                                        device_id=to, device_id_type=_MESH)


def _half(ref, hc):
    kh = ref.shape[0] // 2
    return ref.at[pl.ds(pl.multiple_of(hc * kh, 16), kh), :]


def _gather_plan(blocks, fractions):
    nt = len(blocks)

    def run(step, nsteps, ins, outs, sems):
        send_sems, recv_sems, loc_sems = sems
        x, y, c = _place()
        k = 2 * x + y
        me, sibling = (x, y, c), (x, y, 1 - c)
        chips = [(1 - x, y), (x, 1 - y), (1 - x, 1 - y)]
        chip_k = [2 * cx + cy for cx, cy in chips]

        def ici(t, s, owner_k, to, src=None):
            dst = _half(outs[t].at[owner_k], c)
            return _remote(dst if src is None else src, dst, send_sems.at[t, s], recv_sems.at[t, s], to)

        def passed(t, s, hc, to):
            blk = _half(outs[t].at[chip_k[s]], hc)
            return _remote(blk, blk, send_sems.at[t, 3 + s], recv_sems.at[t, 3 + s], to)

        def local(t):
            return pltpu.make_async_copy(ins[t], outs[t].at[k], loc_sems.at[t])

        @pl.when(step == 0)
        def _():
            for t in range(nt):
                local(t).start()
                for s, chip in enumerate(chips):
                    ici(t, s, k, (*chip, c), src=_half(ins[t], c)).start()

        for t in range(nt):
            @pl.when(step == min(nsteps - 1, int(fractions[t] * nsteps)))
            def _():
                for s in range(3):
                    ici(t, s, chip_k[s], me).wait_recv()
                    passed(t, s, c, sibling).start()

        @pl.when(step == nsteps - 1)
        def _():
            for t in range(nt):
                for s, chip in enumerate(chips):
                    passed(t, s, 1 - c, me).wait_recv()
            for t in range(nt):
                for s, chip in enumerate(chips):
                    ici(t, s, k, (*chip, c), src=_half(ins[t], c)).wait_send()
                    passed(t, s, c, sibling).wait_send()
                local(t).wait()

    return _Comm(blocks, [jax.ShapeDtypeStruct((4,) + b.shape, b.dtype) for b in blocks],
                 [pltpu.SemaphoreType.DMA((nt, 6)), pltpu.SemaphoreType.DMA((nt, 6)), pltpu.SemaphoreType.DMA((nt,))], run)


def _scatter_plan(grads, owners):
    nt = len(grads)

    def run(step, nsteps, ins, outs, sems):
        send_sems, recv_sems, loc_sems = sems
        x, y, c = _place()
        me = 4 * x + 2 * y + c

        def target(r):
            tx = 1 - x if r & 2 else x
            ty = 1 - y if r & 1 else y
            return tx, ty

        def send(t, r):
            tx, ty = target(r)
            return _remote(ins[t].at[2 * tx + ty], outs[t].at[me], send_sems.at[t, r], recv_sems.at[t, 2 * r + c],
                           (tx, ty, owners[t]))

        def local(t):
            return pltpu.make_async_copy(ins[t].at[2 * x + y], outs[t].at[me], loc_sems.at[t])

        @pl.when(step == 0)
        def _():
            for t in range(nt):
                @pl.when(c == owners[t])
                def _():
                    local(t).start()

                @pl.when(c != owners[t])
                def _():
                    send(t, 0).start()

                for r in range(1, 4):
                    send(t, r).start()

        @pl.when(step == nsteps - 1)
        def _():
            for t in range(nt):
                @pl.when(c == owners[t])
                def _():
                    for r in range(4):
                        sx, sy = target(r)
                        for cs in range(2):
                            if r == 0 and cs == owners[t]:
                                continue
                            src_dev = 4 * sx + 2 * sy + cs
                            _remote(ins[t].at[0], outs[t].at[src_dev], send_sems.at[t, r], recv_sems.at[t, 2 * r + cs],
                                    (x, y, c)).wait_recv()
                    local(t).wait()

                @pl.when(c != owners[t])
                def _():
                    send(t, 0).wait_send()

                for r in range(1, 4):
                    send(t, r).wait_send()

    return _Comm(grads, [jax.ShapeDtypeStruct((8,) + g.shape[1:], g.dtype) for g in grads],
                 [pltpu.SemaphoreType.DMA((nt, 4)), pltpu.SemaphoreType.DMA((nt, 8)), pltpu.SemaphoreType.DMA((nt,))], run)


def _share_plan(reduced, owners):
    nt = len(reduced)

    def run(step, nsteps, ins, outs, sems):
        del ins
        send_sems, recv_sems = sems
        x, y, c = _place()

        def give(t, to):
            return _remote(outs[t], outs[t], send_sems.at[t], recv_sems.at[t], to)

        @pl.when(step == 0)
        def _():
            for t in range(nt):
                @pl.when(c == owners[t])
                def _():
                    give(t, (x, y, 1 - c)).start()

        @pl.when(step == nsteps - 1)
        def _():
            for t in range(nt):
                @pl.when(c == owners[t])
                def _():
                    give(t, (x, y, 1 - c)).wait_send()

                @pl.when(c != owners[t])
                def _():
                    give(t, (x, y, c)).wait_recv()

    return _Comm(reduced, [jax.ShapeDtypeStruct(r.shape, r.dtype) for r in reduced],
                 [pltpu.SemaphoreType.DMA((nt,)), pltpu.SemaphoreType.DMA((nt,))], run,
                 aliases={t: t for t in range(nt)})


def _join(a, b):
    ni, no, ns = len(a.inputs), len(a.out_shapes), len(a.sems)

    def run(step, nsteps, ins, outs, sems):
        a.run(step, nsteps, ins[:ni], outs[:no], sems[:ns])
        b.run(step, nsteps, ins[ni:], outs[no:], sems[ns:])

    aliases = dict(a.aliases)
    aliases.update({ni + i: no + o for i, o in b.aliases.items()})
    return _Comm(a.inputs + b.inputs, a.out_shapes + b.out_shapes, a.sems + b.sems, run, aliases)


def _peer(x, y, c, r):
    px = 1 - x if r & 4 else x
    py = 1 - y if r & 2 else y
    pc = 1 - c if r & 1 else c
    return (px, py, pc), 4 * px + 2 * py + pc


def _sum_slots(st, name):
    _, K, n = st.shape
    tr = next(t for t in (256, 128, 64, 32, 16) if K % t == 0)

    def kern(s_ref, o_ref):
        acc = s_ref[0].astype(_F32)
        for d in range(1, 8):
            acc = acc + s_ref[d].astype(_F32)
        o_ref[...] = acc.astype(o_ref.dtype)

    return pl.pallas_call(
        kern, name=name, grid=(K // tr,),
        in_specs=[pl.BlockSpec((8, tr, n), lambda i: (0, i, 0))], out_specs=_rows(tr, n),
        out_shape=jax.ShapeDtypeStruct((K, n), _ACT),
        compiler_params=_cparams("parallel"),
    )(st)


def _all_reduce_small(p):
    R = p.shape[0]

    def body(p_ref, o_ref, stage, send_sems, recv_sems):
        x, y, c = _place()
        me = 4 * x + 2 * y + c
        stage[me] = p_ref[...]
        sent = []
        for r in range(1, 8):
            to, _ = _peer(x, y, c, r)
            cp = _remote(p_ref, stage.at[me], send_sems.at[r - 1], recv_sems.at[r - 1], to)
            cp.start()
            sent.append(cp)
        for r in range(1, 8):
            _, src_dev = _peer(x, y, c, r)
            _remote(p_ref, stage.at[src_dev], send_sems.at[r - 1], recv_sems.at[r - 1], (x, y, c)).wait_recv()
        acc = stage[0]
        for d in range(1, 8):
            acc = acc + stage[d]
        o_ref[...] = acc
        for cp in sent:
            cp.wait_send()

    vm = pl.BlockSpec(memory_space=pltpu.VMEM)
    return pl.pallas_call(
        body, name="all_reduce_small",
        in_specs=[vm], out_specs=vm,
        out_shape=jax.ShapeDtypeStruct((R, _LANES), _F32),
        scratch_shapes=[pltpu.VMEM((8, R, _LANES), _F32), pltpu.SemaphoreType.DMA((7,)), pltpu.SemaphoreType.DMA((7,))],
    )(p)


def _adamw_update(gv, w_ref, m_ref, v_ref, gf_ref, d_ref, nm_ref, nv_ref):
    nm = _B1 * m_ref[...] + (1.0 - _B1) * gv
    nv = _B2 * v_ref[...] + (1.0 - _B2) * (gv * gv)
    m_hat = nm / (1.0 - _B1 ** _STEP)
    v_hat = nv / (1.0 - _B2 ** _STEP)
    gf_ref[...] = gv
    d_ref[...] = -_LR * (m_hat / (jnp.sqrt(v_hat) + _EPS) + _WD * w_ref[...])
    nm_ref[...] = nm
    nv_ref[...] = nv


def _adamw_layers(w, g_layers, m, v, name):
    _, K, n = w.shape
    tr = next(t for t in (256, 128, 64, 32, 16) if K % t == 0)

    def kern(w_ref, g0_ref, g1_ref, m_ref, v_ref, *out_refs):
        first = pl.program_id(0) == 0
        gv = jnp.where(first, g0_ref[...].astype(_F32), g1_ref[...].astype(_F32))
        _adamw_update(gv, w_ref, m_ref, v_ref, *out_refs)

    stacked = pl.BlockSpec((None, tr, n), lambda l, i: (l, i, 0))
    layer = pl.BlockSpec((tr, n), lambda l, i: (i, 0))
    return tuple(pl.pallas_call(
        kern, name=name, grid=(2, K // tr),
        in_specs=[stacked, layer, layer, stacked, stacked], out_specs=[stacked] * 4,
        out_shape=[jax.ShapeDtypeStruct(w.shape, _F32)] * 4,
        compiler_params=_cparams("parallel", "parallel"),
    )(w, g_layers[0], g_layers[1], m, v))


def _adamw(w, g, m, v, name):
    shape = w.shape
    w2, g2, m2, v2 = (a.reshape(-1, shape[-1]) for a in (w, g, m, v))
    R, C = w2.shape
    tr = next((t for t in (256, 128, 64, 32, 16) if R % t == 0), R)

    def kern(w_ref, g_ref, m_ref, v_ref, *out_refs):
        _adamw_update(g_ref[...].astype(_F32), w_ref, m_ref, v_ref, *out_refs)

    outs = pl.pallas_call(
        kern, name=name, grid=(R // tr,),
        in_specs=[_rows(tr, C)] * 4, out_specs=[_rows(tr, C)] * 4,
        out_shape=[jax.ShapeDtypeStruct((R, C), _F32)] * 4,
        compiler_params=_cparams("parallel"),
    )(w2, g2, m2, v2)
    return tuple(o.reshape(shape) for o in outs)


def _pack_small(gss, sq):
    parts = [gss[l][n].reshape(-1) for n in _SMALL for l in range(len(gss))] + [jnp.sum(sq).reshape(1)]
    flat = jnp.concatenate(parts)
    rows = -(-flat.shape[0] // (8 * _LANES)) * 8
    return jnp.pad(flat, (0, rows * _LANES - flat.shape[0])).reshape(rows, _LANES)


def _unpack_small(total, shapes):
    flat = total.reshape(-1)
    out, off = {}, 0
    for n in _SMALL:
        layers = []
        for _ in range(shapes[n][0]):
            size = 1
            for s in shapes[n][1:]:
                size *= s
            layers.append(flat[off:off + size].reshape(shapes[n][1:]))
            off += size
        out[n] = jnp.stack(layers)
    return out, flat[off]


_GATHER = {
    "band_fwd_0": [(0, "w_proj_a"), (0, "w_proj_b"), (0, "w_out"), (0, "w_ffn_out")],
    "stick_fwd_0": [(0, "w_ffn_in")],
    "mix_fwd_0": [(1, "w_in")],
    "ffn_fwd_0": [(1, "w_proj_a"), (1, "w_proj_b"), (1, "w_out"), (1, "w_ffn_in"), (1, "w_ffn_out")],
}
_SCATTER = {
    "band_bwd_1": [(1, "w_ffn_in"), (1, "w_ffn_out")],
    "stick_bwd_1": [(1, "w_proj_a"), (1, "w_proj_b"), (1, "w_out")],
    "band_bwd_0": [(1, "w_in"), (0, "w_ffn_in")],
    "stick_bwd_0": [(0, "w_ffn_out"), (0, "w_proj_a"), (0, "w_proj_b"), (0, "w_out")],
    "in_proj_bwd_0": [(0, "w_in")],
}
_SHARE = {"stick_bwd_1": "band_bwd_1", "band_bwd_0": "stick_bwd_1", "stick_bwd_0": "band_bwd_0", "grad_w_in_0": "stick_bwd_0"}


def _owner(key):
    del key
    return 1


def kernel(x, w_in, b_gate, rel_bias, w_proj_a, w_proj_b, w_out, ln1_g, ln1_b, w_ffn_in, w_ffn_out, ln2_g, ln2_b, loss_target, m_w_in, m_b_gate, m_rel_bias, m_w_proj_a, m_w_proj_b, m_w_out, m_ln1_g, m_ln1_b, m_w_ffn_in, m_w_ffn_out, m_ln2_g, m_ln2_b, v_w_in, v_b_gate, v_rel_bias, v_w_proj_a, v_w_proj_b, v_w_out, v_ln1_g, v_ln1_b, v_w_ffn_in, v_w_ffn_out, v_ln2_g, v_ln2_b):
    names = ("w_in", "b_gate", "rel_bias", "w_proj_a", "w_proj_b", "w_out", "ln1_g", "ln1_b", "w_ffn_in", "w_ffn_out", "ln2_g", "ln2_b")
    w = dict(zip(names, (w_in, b_gate, rel_bias, w_proj_a, w_proj_b, w_out, ln1_g, ln1_b, w_ffn_in, w_ffn_out, ln2_g, ln2_b)))
    m = dict(zip(names, (m_w_in, m_b_gate, m_rel_bias, m_w_proj_a, m_w_proj_b, m_w_out, m_ln1_g, m_ln1_b, m_w_ffn_in, m_w_ffn_out, m_ln2_g, m_ln2_b)))
    v = dict(zip(names, (v_w_in, v_b_gate, v_rel_bias, v_w_proj_a, v_w_proj_b, v_w_out, v_ln1_g, v_ln1_b, v_w_ffn_in, v_w_ffn_out, v_ln2_g, v_ln2_b)))
    T, D = x.shape[-2], x.shape[-1]
    assert w_in.shape[0] == 2, "the exchange schedule below is written for two layers"

    mine = [{n: w[n][l].astype(_MXU) for n in _DENSE} for l in range(2)]
    W = [dict(), dict()]
    gws = [dict(), dict()]
    slots, final = {}, {}

    def gather(keys):
        sizes = [mine[l][n].size for l, n in keys]
        passed, fractions = 0, []
        for s in sizes:
            passed += s
            fractions.append(0.15 + 0.6 * passed / sum(sizes))

        def done(outs):
            for (l, n), o in zip(keys, outs):
                W[l][n] = o
        return _gather_plan([mine[l][n] for l, n in keys], fractions), done

    def scatter(keys):
        comm = _scatter_plan([gws[l][n] for l, n in keys], [_owner(key) for key in keys])
        return comm, lambda outs: slots.update(zip(keys, outs))

    def share(keys):
        reduced = [_sum_slots(slots[key], f"sum_grad_{key[1]}_{key[0]}") for key in keys]
        comm = _share_plan(reduced, [_owner(key) for key in keys])
        return comm, lambda outs: final.update(zip(keys, outs))

    def both(first, second):
        (ca, da), (cb, db) = first, second
        na = len(ca.out_shapes)
        return _join(ca, cb), lambda outs: (da(outs[:na]), db(outs[na:]))

    comm, done = gather([(0, "w_in")])
    done(_comm_only(comm, "gather_first"))
    plans = {key: functools.partial(gather, keys) for key, keys in _GATHER.items()}
    for key, keys in _SCATTER.items():
        plans[key] = functools.partial(scatter, keys)
    for key, scattered_under in _SHARE.items():
        handed = functools.partial(share, _SCATTER[scattered_under])
        carried = plans.get(key)
        plans[key] = handed if carried is None else (lambda carried=carried, handed=handed: both(carried(), handed()))
    small = {n: w[n] for n in _SMALL}
    sq, dx, _, gss = _local_step(x.reshape(T, D), loss_target.reshape(T, D), W, small, _Plans(plans), gws)

    comm, done = share(_SCATTER["in_proj_bwd_0"])
    done(_comm_only(comm, "share_last"))
    total = _all_reduce_small(_pack_small(gss, sq))
    small_grads, sq_all = _unpack_small(total, {n: w[n].shape for n in _SMALL})
    loss = 0.5 * sq_all / D

    grad, delta, new_m, new_v = {}, {}, {}, {}
    for n in names:
        if n in _DENSE:
            updated = _adamw_layers(w[n], [final[(l, n)] for l in range(2)], m[n], v[n], f"adamw_{n}")
        else:
            updated = _adamw(w[n], small_grads[n], m[n], v[n], f"adamw_{n}")
        grad[n], delta[n], new_m[n], new_v[n] = updated
    return (loss, dx.reshape(x.shape), *[grad[n] for n in names], *[delta[n] for n in names],
            *[new_m[n] for n in names], *[new_v[n] for n in names])
```

```python
import functools

import jax
import jax.numpy as jnp
from jax import lax
from jax.experimental import pallas as pl
from jax.experimental.pallas import tpu as pltpu

_MXU = jnp.bfloat16
_ACT = jnp.bfloat16
_F32 = jnp.float32

_HEAD = 64
_CHUNK = 64
_LANES = 128
_TQ = 128
_BAND_TILES = 5
_BIAS_TILES = 9
_REL_CLIP = 256
_LN_EPS = 1e-5
_MASKED = -1e30
_EXP_ZERO_BELOW = -87.34
_SB_WINDOW = 2
_SB_SUBTILES = 4
_BAND_SUBTILES = 4
_VMEM_LIMIT = 56 * 1024 * 1024
_GRAD_ACC_BYTES = 12 * 1024 * 1024

_LR, _B1, _B2, _EPS, _WD, _STEP = 0.001, 0.9, 0.999, 1e-08, 0.01, 10

_MESH = pl.DeviceIdType.MESH


def _dot(a, b):
    return jnp.dot(a, b, preferred_element_type=_F32)


def _dot_nt(a, b):
    return lax.dot_general(a, b, (((1,), (1,)), ((), ())), preferred_element_type=_F32)


def _dot_tn(a, b):
    return lax.dot_general(a, b, (((0,), (0,)), ((), ())), preferred_element_type=_F32)


def _cparams(*sem):
    return pltpu.CompilerParams(dimension_semantics=sem, vmem_limit_bytes=_VMEM_LIMIT)


def _rows(t, c):
    return pl.BlockSpec((t, c), lambda i: (i, 0))


def _whole(shape):
    return pl.BlockSpec(shape, lambda i: tuple(0 for _ in shape))


_ANY = pl.BlockSpec(memory_space=pl.ANY)


def _load_cols(w_hbm, w_vmem, sem):
    n = w_hbm.shape[-1]
    cps = [pltpu.make_async_copy(w_hbm.at[k], w_vmem.at[:, pl.ds(k * n, n)], sem.at[k]) for k in range(4)]
    for cp in cps:
        cp.start()
    for cp in cps:
        cp.wait()


def _load_rows(w_hbm, w_vmem, sem):
    r = w_hbm.shape[-2]
    cps = [pltpu.make_async_copy(w_hbm.at[k], w_vmem.at[pl.ds(k * r, r), :], sem.at[k]) for k in range(4)]
    for cp in cps:
        cp.start()
    for cp in cps:
        cp.wait()


def _ln_stats(u):
    mu = jnp.mean(u, axis=-1, keepdims=True)
    xc = u - mu
    var = jnp.mean(xc * xc, axis=-1, keepdims=True)
    rstd = lax.rsqrt(var + _LN_EPS)
    return xc * rstd, rstd


def _ln_bwd(u, dy, gamma):
    xhat, rstd = _ln_stats(u)
    dxh = dy * gamma
    m1 = jnp.mean(dxh, axis=-1, keepdims=True)
    m2 = jnp.mean(dxh * xhat, axis=-1, keepdims=True)
    du = rstd * (dxh - m1 - xhat * m2)
    return du, jnp.sum(dy * xhat, axis=0, keepdims=True), jnp.sum(dy, axis=0, keepdims=True), xhat


def _divisor_tile(n, cap):
    best = None
    for t in range(_LANES, min(n, cap) + 1, _LANES):
        if n % t == 0:
            best = t
    return best or n


class _Comm:
    def __init__(self, inputs, out_shapes, sems, run, aliases=None):
        self.inputs, self.out_shapes, self.sems, self.run = list(inputs), list(out_shapes), list(sems), run
        self.aliases = aliases or {}


def _call(kern, comm, *, name, grid, in_specs, out_specs, out_shape, scratch_shapes, args, semantics):
    in_specs, out_specs, out_shape, scratch_shapes = list(in_specs), list(out_specs), list(out_shape), list(scratch_shapes)
    if comm is None:
        outs = pl.pallas_call(kern, name=name, grid=grid, in_specs=in_specs, out_specs=out_specs, out_shape=out_shape,
                              scratch_shapes=scratch_shapes, compiler_params=_cparams(*semantics))(*args)
        return list(outs), []
    n_in, n_out, n_scr = len(in_specs), len(out_specs), len(scratch_shapes)
    ci, co = len(comm.inputs), len(comm.out_shapes)
    nsteps = functools.reduce(lambda a, b: a * b, grid, 1)

    def fused(*refs):
        a, b = n_in, n_in + ci
        c, d = b + n_out, b + n_out + co
        e = d + n_scr
        step = pl.program_id(0)
        for ax in range(1, len(grid)):
            step = step * grid[ax] + pl.program_id(ax)
        comm.run(step, nsteps, refs[a:b], refs[c:d], refs[e:])
        kern(*refs[:a], *refs[b:c], *refs[d:e])

    outs = pl.pallas_call(
        fused, name=name, grid=grid, in_specs=in_specs + [_ANY] * ci, out_specs=out_specs + [_ANY] * co,
        out_shape=out_shape + comm.out_shapes, scratch_shapes=scratch_shapes + comm.sems,
        input_output_aliases={n_in + i: n_out + o for i, o in comm.aliases.items()},
        compiler_params=_cparams(*("arbitrary" for _ in grid)))(*args, *comm.inputs)
    return list(outs[:n_out]), list(outs[n_out:])


def _comm_only(comm, name):
    def body(*refs):
        ci, co = len(comm.inputs), len(comm.out_shapes)
        comm.run(0, 1, refs[:ci], refs[ci:ci + co], refs[ci + co:])

    outs = pl.pallas_call(body, name=name, in_specs=[_ANY] * len(comm.inputs), out_specs=[_ANY] * len(comm.out_shapes),
                          out_shape=comm.out_shapes, scratch_shapes=comm.sems,
                          input_output_aliases=dict(comm.aliases))(*comm.inputs)
    return list(outs)


def _in_proj(x, w_in, layer):
    T, D = x.shape
    N = 4 * w_in.shape[-1]
    NQ = N - 2 * D
    tm = 512

    def kern(x_ref, w_hbm, hq_ref, hg_ref, xb_ref, w_v, sem):
        @pl.when(pl.program_id(0) == 0)
        def _():
            _load_cols(w_hbm, w_v, sem)

        xb = x_ref[...].astype(_MXU)
        hq_ref[...] = _dot(xb, w_v[:, :NQ]).astype(hq_ref.dtype)
        hg_ref[...] = _dot(xb, w_v[:, NQ:]).astype(hg_ref.dtype)
        xb_ref[...] = xb.astype(xb_ref.dtype)

    return pl.pallas_call(
        kern, name=f"in_proj_{layer}", grid=(T // tm,),
        in_specs=[_rows(tm, D), _ANY],
        out_specs=[_rows(tm, NQ), _rows(tm, 2 * D), _rows(tm, D)],
        out_shape=[jax.ShapeDtypeStruct((T, NQ), _ACT), jax.ShapeDtypeStruct((T, 2 * D), _ACT),
                   jax.ShapeDtypeStruct((T, D), _ACT)],
        scratch_shapes=[pltpu.VMEM((D, N), w_in.dtype), pltpu.SemaphoreType.DMA((4,))],
        compiler_params=_cparams("arbitrary"),
    )(x, w_in)


def _mix_fwd(oa, ob, hg, x, wpa, wpb, wo, bg, gamma, beta, alpha, layer, comm=None):
    T, D = x.shape
    WA, WB = oa.shape[1], ob.shape[1]
    tm = 256

    def kern(oa_ref, ob_ref, hg_ref, x_ref, bg_ref, g_ref, b_ref, wpa_h, wpb_h, wo_h,
             x1_ref, u1_ref, pre_ref, ya_ref, yb_ref, wpa_v, wpb_v, wo_v, sa, sb, so):
        @pl.when(pl.program_id(0) == 0)
        def _():
            _load_cols(wpa_h, wpa_v, sa)
            _load_cols(wpb_h, wpb_v, sb)
            _load_rows(wo_h, wo_v, so)

        ya = _dot(oa_ref[...].astype(_MXU), wpa_v[...])
        yb = _dot(ob_ref[...].astype(_MXU), wpb_v[...])
        hgv = hg_ref[...].astype(_F32)
        bgv = bg_ref[...]
        ga = jax.nn.sigmoid(hgv[:, :D] + bgv[:, :D])
        gb = jax.nn.sigmoid(hgv[:, D:] + bgv[:, D:])
        pre = ga * ya + gb * yb
        mix = _dot(pre.astype(_MXU), wo_v[...])
        u = alpha * x_ref[...] + mix
        xhat, _ = _ln_stats(u)
        x1_ref[...] = xhat * g_ref[...] + b_ref[...]
        u1_ref[...] = u
        pre_ref[...] = pre.astype(pre_ref.dtype)
        ya_ref[...] = ya.astype(ya_ref.dtype)
        yb_ref[...] = yb.astype(yb_ref.dtype)

    return _call(
        kern, comm, name=f"mix_fwd_{layer}", grid=(T // tm,),
        in_specs=[_rows(tm, WA), _rows(tm, WB), _rows(tm, 2 * D), _rows(tm, D),
                  _whole((1, 2 * D)), _whole((1, D)), _whole((1, D)), _ANY, _ANY, _ANY],
        out_specs=[_rows(tm, D)] * 5,
        out_shape=[jax.ShapeDtypeStruct((T, D), _F32), jax.ShapeDtypeStruct((T, D), _F32)]
        + [jax.ShapeDtypeStruct((T, D), _ACT)] * 3,
        scratch_shapes=[pltpu.VMEM((WA, D), wpa.dtype), pltpu.VMEM((WB, D), wpb.dtype), pltpu.VMEM((D, D), wo.dtype),
                        pltpu.SemaphoreType.DMA((4,)), pltpu.SemaphoreType.DMA((4,)), pltpu.SemaphoreType.DMA((4,))],
        args=(oa, ob, hg, x, bg, gamma, beta, wpa, wpb, wo), semantics=("arbitrary",))


def _ffn_fwd(x1, wfi, wfo, gamma, beta, alpha, layer, comm=None):
    T, D = x1.shape
    F2 = 4 * wfi.shape[-1]
    F = F2 // 2
    tm = 256
    fc = F // 2

    def kern(x_ref, g_ref, b_ref, wi_h, wo_h, x2_ref, u2_ref, act_ref, gu_ref, xb_ref, wi_v, wo_v, si, so):
        @pl.when(pl.program_id(0) == 0)
        def _():
            _load_cols(wi_h, wi_v, si)
            _load_rows(wo_h, wo_v, so)

        x = x_ref[...]
        xb = x.astype(_MXU)
        xb_ref[...] = xb.astype(xb_ref.dtype)
        ffn = jnp.zeros((tm, D), _F32)
        for c in range(2):
            g = _dot(xb, wi_v[:, c * fc:(c + 1) * fc])
            u = _dot(xb, wi_v[:, F + c * fc:F + (c + 1) * fc])
            act = g * jax.nn.sigmoid(g) * u
            ab = act.astype(_MXU)
            ffn = ffn + _dot(ab, wo_v[c * fc:(c + 1) * fc, :])
            act_ref[:, c * fc:(c + 1) * fc] = ab.astype(act_ref.dtype)
            gu_ref[:, c * fc:(c + 1) * fc] = g.astype(gu_ref.dtype)
            gu_ref[:, F + c * fc:F + (c + 1) * fc] = u.astype(gu_ref.dtype)
        uu = alpha * x + ffn
        xhat, _ = _ln_stats(uu)
        x2_ref[...] = xhat * g_ref[...] + b_ref[...]
        u2_ref[...] = uu

    return _call(
        kern, comm, name=f"ffn_fwd_{layer}", grid=(T // tm,),
        in_specs=[_rows(tm, D), _whole((1, D)), _whole((1, D)), _ANY, _ANY],
        out_specs=[_rows(tm, D), _rows(tm, D), _rows(tm, F), _rows(tm, F2), _rows(tm, D)],
        out_shape=[jax.ShapeDtypeStruct((T, D), _F32), jax.ShapeDtypeStruct((T, D), _F32),
                   jax.ShapeDtypeStruct((T, F), _ACT), jax.ShapeDtypeStruct((T, F2), _ACT),
                   jax.ShapeDtypeStruct((T, D), _ACT)],
        scratch_shapes=[pltpu.VMEM((D, F2), wfi.dtype), pltpu.VMEM((F, D), wfo.dtype),
                        pltpu.SemaphoreType.DMA((4,)), pltpu.SemaphoreType.DMA((4,))],
        args=(x1, gamma, beta, wfi, wfo), semantics=("arbitrary",))


def _ffn_bwd(u2, dy_or_target, gu, gamma, beta, wfi, wfo, alpha, layer, last):
    T, D = u2.shape
    F2 = gu.shape[1]
    F = F2 // 2
    tm = 256
    fc = F // 2

    def kern(u_ref, dy_ref, gu_ref, g_ref, b_ref, wi_h, wo_h, dx_ref, dub_ref, dgu_ref, st_ref, wi_v, wo_v, si, so):
        @pl.when(pl.program_id(0) == 0)
        def _():
            _load_cols(wi_h, wi_v, si)
            _load_rows(wo_h, wo_v, so)
            st_ref[...] = jnp.zeros_like(st_ref)

        gam = g_ref[...]
        u = u_ref[...]
        if last:
            xhat0, _ = _ln_stats(u)
            err = xhat0 * gam + b_ref[...] - dy_ref[...]
            dy = err * (1.0 / D)
            st_ref[2:3, :] += jnp.sum(err * err, axis=0, keepdims=True)
        else:
            dy = dy_ref[...]
        du, dgam, dbet, _ = _ln_bwd(u, dy, gam)
        st_ref[0:1, :] += dgam
        st_ref[1:2, :] += dbet
        dub = du.astype(_MXU)
        dub_ref[...] = dub.astype(dub_ref.dtype)
        dx = alpha * du
        for c in range(2):
            dact = _dot_nt(dub, wo_v[c * fc:(c + 1) * fc, :])
            g = gu_ref[:, c * fc:(c + 1) * fc].astype(_F32)
            uu = gu_ref[:, F + c * fc:F + (c + 1) * fc].astype(_F32)
            sg = jax.nn.sigmoid(g)
            dg = (dact * uu * (sg * (1.0 + g * (1.0 - sg)))).astype(_MXU)
            dup = (dact * (g * sg)).astype(_MXU)
            dgu_ref[:, c * fc:(c + 1) * fc] = dg.astype(dgu_ref.dtype)
            dgu_ref[:, F + c * fc:F + (c + 1) * fc] = dup.astype(dgu_ref.dtype)
            dx = dx + _dot_nt(dg, wi_v[:, c * fc:(c + 1) * fc]) + _dot_nt(dup, wi_v[:, F + c * fc:F + (c + 1) * fc])
        dx_ref[...] = dx

    return pl.pallas_call(
        kern, name=f"ffn_bwd_{layer}", grid=(T // tm,),
        in_specs=[_rows(tm, D), _rows(tm, D), _rows(tm, F2), _whole((1, D)), _whole((1, D)), _ANY, _ANY],
        out_specs=[_rows(tm, D), _rows(tm, D), _rows(tm, F2), _whole((8, D))],
        out_shape=[jax.ShapeDtypeStruct((T, D), _F32), jax.ShapeDtypeStruct((T, D), _ACT),
                   jax.ShapeDtypeStruct((T, F2), _ACT), jax.ShapeDtypeStruct((8, D), _F32)],
        scratch_shapes=[pltpu.VMEM((D, F2), wfi.dtype), pltpu.VMEM((F, D), wfo.dtype),
                        pltpu.SemaphoreType.DMA((4,)), pltpu.SemaphoreType.DMA((4,))],
        compiler_params=_cparams("arbitrary"),
    )(u2, dy_or_target, gu, gamma, beta, wfi, wfo)


def _residual_nt(res, res_scale, d, w, name, comm=None):
    T, K = res.shape
    N = d.shape[1]
    tm = 512

    def kern(r_ref, d_ref, w_hbm, o_ref, w_v, sem):
        @pl.when(pl.program_id(0) == 0)
        def _():
            _load_cols(w_hbm, w_v, sem)

        o_ref[...] = res_scale * r_ref[...] + _dot_nt(d_ref[...].astype(_MXU), w_v[...])

    outs, extra = _call(
        kern, comm, name=name, grid=(T // tm,),
        in_specs=[_rows(tm, K), _rows(tm, N), _ANY], out_specs=[_rows(tm, K)],
        out_shape=[jax.ShapeDtypeStruct((T, K), _F32)],
        scratch_shapes=[pltpu.VMEM((K, N), w.dtype), pltpu.SemaphoreType.DMA((4,))],
        args=(res, d, w), semantics=("arbitrary",))
    return outs[0], extra


def _mix_bwd(u1, dx1, ya, yb, hg, wpa, wpb, wo, bg, gamma, layer):
    T, D = u1.shape
    WA, WB = wpa.shape[-2], wpb.shape[-2]
    tm = 256

    def kern(u_ref, dx_ref, ya_ref, yb_ref, hg_ref, bg_ref, g_ref, wpa_h, wpb_h, wo_h,
             du_ref, dub_ref, dya_ref, dyb_ref, dhg_ref, doa_ref, dob_ref, st_ref,
             wpa_v, wpb_v, wo_v, sa, sb, so):
        @pl.when(pl.program_id(0) == 0)
        def _():
            _load_cols(wpa_h, wpa_v, sa)
            _load_cols(wpb_h, wpb_v, sb)
            _load_rows(wo_h, wo_v, so)
            st_ref[...] = jnp.zeros_like(st_ref)

        du, dgam, dbet, _ = _ln_bwd(u_ref[...], dx_ref[...], g_ref[...])
        st_ref[1:2, :D] += dgam
        st_ref[1:2, D:] += dbet
        du_ref[...] = du
        dub = du.astype(_MXU)
        dub_ref[...] = dub.astype(dub_ref.dtype)
        dpre = _dot_nt(dub, wo_v[...])
        hgv = hg_ref[...].astype(_F32)
        bgv = bg_ref[...]
        ga = jax.nn.sigmoid(hgv[:, :D] + bgv[:, :D])
        gb = jax.nn.sigmoid(hgv[:, D:] + bgv[:, D:])
        dya = (dpre * ga).astype(_MXU)
        dyb = (dpre * gb).astype(_MXU)
        dsa = dpre * ya_ref[...].astype(_F32) * (ga * (1.0 - ga))
        dsb = dpre * yb_ref[...].astype(_F32) * (gb * (1.0 - gb))
        st_ref[0:1, :D] += jnp.sum(dsa, axis=0, keepdims=True)
        st_ref[0:1, D:] += jnp.sum(dsb, axis=0, keepdims=True)
        dya_ref[...] = dya.astype(dya_ref.dtype)
        dyb_ref[...] = dyb.astype(dyb_ref.dtype)
        dhg_ref[:, :D] = dsa.astype(dhg_ref.dtype)
        dhg_ref[:, D:] = dsb.astype(dhg_ref.dtype)
        doa_ref[...] = _dot_nt(dya, wpa_v[...]).astype(doa_ref.dtype)
        dob_ref[...] = _dot_nt(dyb, wpb_v[...]).astype(dob_ref.dtype)

    return pl.pallas_call(
        kern, name=f"mix_bwd_{layer}", grid=(T // tm,),
        in_specs=[_rows(tm, D)] * 4 + [_rows(tm, 2 * D), _whole((1, 2 * D)), _whole((1, D)), _ANY, _ANY, _ANY],
        out_specs=[_rows(tm, D)] * 4 + [_rows(tm, 2 * D), _rows(tm, WA), _rows(tm, WB), _whole((8, 2 * D))],
        out_shape=[jax.ShapeDtypeStruct((T, D), _F32)] + [jax.ShapeDtypeStruct((T, D), _ACT)] * 3
        + [jax.ShapeDtypeStruct((T, 2 * D), _ACT), jax.ShapeDtypeStruct((T, WA), _ACT),
           jax.ShapeDtypeStruct((T, WB), _ACT), jax.ShapeDtypeStruct((8, 2 * D), _F32)],
        scratch_shapes=[pltpu.VMEM((WA, D), wpa.dtype), pltpu.VMEM((WB, D), wpb.dtype), pltpu.VMEM((D, D), wo.dtype),
                        pltpu.SemaphoreType.DMA((4,)), pltpu.SemaphoreType.DMA((4,)), pltpu.SemaphoreType.DMA((4,))],
        compiler_params=_cparams("arbitrary"),
    )(u1, dx1, ya, yb, hg, bg, gamma, wpa, wpb, wo)


def _grad_w(a, b, *, col_shards, name, comm=None):
    T, M = a.shape
    N = b.shape[1]
    tk = 512
    n = N // 4 if col_shards else N
    whole = M * N * 4 <= _GRAD_ACC_BYTES
    tn = N if whole else (n if col_shards else _divisor_tile(N, _GRAD_ACC_BYTES // (4 * M)))
    nk = T // tk

    def kern(a_ref, b_ref, o_ref, acc):
        k = pl.program_id(1)

        @pl.when(k == 0)
        def _():
            acc[...] = jnp.zeros_like(acc)

        acc[...] += _dot_tn(a_ref[...].astype(_MXU), b_ref[...].astype(_MXU))

        @pl.when(k == nk - 1)
        def _():
            if col_shards and whole:
                for s in range(4):
                    o_ref[s] = acc[:, s * n:(s + 1) * n].astype(o_ref.dtype)
            else:
                o_ref[...] = acc[...].astype(o_ref.dtype)

    if col_shards:
        out_spec = (pl.BlockSpec((4, M, n), lambda j, k: (0, 0, 0)) if whole
                    else pl.BlockSpec((None, M, n), lambda j, k: (j, 0, 0)))
        out_shape = jax.ShapeDtypeStruct((4, M, n), _ACT)
    else:
        out_spec = pl.BlockSpec((M, tn), lambda j, k: (0, j))
        out_shape = jax.ShapeDtypeStruct((M, N), _ACT)
    outs, extra = _call(
        kern, comm, name=name, grid=(N // tn, nk),
        in_specs=[pl.BlockSpec((tk, M), lambda j, k: (k, 0)), pl.BlockSpec((tk, tn), lambda j, k: (k, j))],
        out_specs=[out_spec], out_shape=[out_shape], scratch_shapes=[pltpu.VMEM((M, tn), _F32)],
        args=(a, b), semantics=("parallel", "arbitrary"))
    return outs[0], extra


def _bias_tiles(rel):
    H = rel.shape[0]
    span = _TQ * _BAND_TILES - 1
    edge = span - _REL_CLIP
    gvec = jnp.concatenate([jnp.broadcast_to(rel[:, :1], (H, edge)), rel, jnp.broadcast_to(rel[:, -1:], (H, edge))], axis=1)
    width = _BIAS_TILES * _TQ
    period = width + _TQ
    tiled = jnp.broadcast_to(jnp.pad(gvec[:, ::-1], ((0, 0), (0, 1)))[:, None, :], (H, _TQ, period))
    rows = tiled.reshape(H, _TQ * period)[:, :_TQ * (period - 1)].reshape(H, _TQ, period - 1)[:, :, _TQ - 1:]
    r = jnp.arange(_TQ)[:, None]
    u = jnp.arange(width)[None, :]
    d = 4 * _TQ + r - u
    rm = r % _CHUNK
    valid = (d >= rm - (_CHUNK - 1)) & (d <= rm + 8 * _CHUNK)
    tiles = jnp.where(valid[None], rows, _MASKED)
    return tiles.reshape(H // 2, 2 * _TQ, _BIAS_TILES, _TQ).transpose(0, 2, 1, 3)


def _fold_bias_grad(db):
    H = 2 * db.shape[0]
    width = _BIAS_TILES * _TQ
    period = width + _TQ
    x = jnp.pad(db.transpose(0, 2, 1, 3).reshape(H, _TQ, width), ((0, 0), (0, 0), (_TQ - 1, 0)))
    skew = jnp.pad(x.reshape(H, _TQ * (period - 1)), ((0, 0), (0, _TQ))).reshape(H, _TQ, period)
    dg = skew.sum(axis=1)[:, :period - 1][:, ::-1]
    span = _TQ * _BAND_TILES - 1
    edge = span - _REL_CLIP
    mid = dg[:, edge:edge + 2 * _REL_CLIP + 1]
    lo = dg[:, :edge].sum(axis=1)
    hi = dg[:, edge + 2 * _REL_CLIP + 1:].sum(axis=1)
    return mid.at[:, 0].add(lo).at[:, -1].add(hi)


def _band_window(i):
    j0 = jnp.maximum(i - (_BAND_TILES - 1), 0)
    return j0, (_BAND_TILES - 1) - (i - j0)


def _head_masks():
    lane = lax.broadcasted_iota(jnp.int32, (1, _LANES), 1)
    return [(lane // _HEAD) == hh for hh in range(2)]


def _stack_heads(x, masks):
    return jnp.concatenate([jnp.where(m, x, jnp.zeros_like(x)) for m in masks], axis=0)


def _unstack_heads(y, masks):
    return jnp.where(masks[0], y[:_TQ], y[_TQ:])


def _scaled(q):
    return q * jnp.asarray(_HEAD ** -0.5, q.dtype)


def _band_probs(q2, k_ref, b_ref, j0, boff):
    s = []
    for j in range(_BAND_TILES):
        kj = k_ref[pl.ds(pl.multiple_of((j0 + j) * _TQ, _TQ), _TQ), :]
        s.append(_dot_nt(q2, kj) + b_ref[boff + j])
    m = jnp.max(functools.reduce(jnp.maximum, s), axis=-1, keepdims=True)
    p = [jnp.exp(x - m) for x in s]
    l = jnp.sum(functools.reduce(lambda a, b: a + b, p), axis=-1, keepdims=True)
    return p, 1.0 / l


def _qkv_specs(T, cb, npair, tq=_TQ):
    return [pl.BlockSpec((tq, _LANES), lambda h, i: (i, cb + h)),
            pl.BlockSpec((T, _LANES), lambda h, i: (0, cb + npair + h)),
            pl.BlockSpec((T, _LANES), lambda h, i: (0, cb + 2 * npair + h))]


def _attn_a_fwd(hq, bias, col0, width, layer, comm=None):
    T = hq.shape[0]
    npair = width // _LANES
    nsub = _BAND_SUBTILES
    tq = nsub * _TQ

    def kern(q_ref, k_ref, v_ref, b_ref, o_ref):
        masks = _head_masks()
        q = _scaled(q_ref[...])
        for s in range(nsub):
            part = slice(s * _TQ, (s + 1) * _TQ)
            j0, boff = _band_window(nsub * pl.program_id(1) + s)
            p, inv = _band_probs(_stack_heads(q[part], masks), k_ref, b_ref, j0, boff)
            o = jnp.zeros((2 * _TQ, _LANES), _F32)
            for j in range(_BAND_TILES):
                vj = v_ref[pl.ds(pl.multiple_of((j0 + j) * _TQ, _TQ), _TQ), :]
                o = o + _dot(p[j].astype(_MXU), vj)
            o_ref[part, :] = _unstack_heads(o * inv, masks).astype(o_ref.dtype)

    outs, extra = _call(
        kern, comm, name=f"band_attn_fwd_{layer}", grid=(npair, T // tq),
        in_specs=_qkv_specs(T, col0 // _LANES, npair, tq)
        + [pl.BlockSpec((None, _BIAS_TILES, 2 * _TQ, _TQ), lambda h, i: (h, 0, 0, 0))],
        out_specs=[pl.BlockSpec((tq, _LANES), lambda h, i: (i, h))],
        out_shape=[jax.ShapeDtypeStruct((T, width), _ACT)], scratch_shapes=[],
        args=(hq, hq, hq, bias), semantics=("arbitrary", "arbitrary"))
    return outs[0], extra


def _attn_a_bwd(hq, bias, do, col0, width, layer, comm=None):
    T = hq.shape[0]
    npair = width // _LANES
    nsub = _BAND_SUBTILES
    tq = nsub * _TQ
    nq = T // tq
    scale = _HEAD ** -0.5

    def kern(q_ref, k_ref, v_ref, b_ref, do_ref, dq_ref, dk_ref, dv_ref, db_ref, dk_acc, dv_acc):
        i = pl.program_id(1)

        @pl.when(i == 0)
        def _():
            dk_acc[...] = jnp.zeros_like(dk_acc)
            dv_acc[...] = jnp.zeros_like(dv_acc)
            db_ref[...] = jnp.zeros_like(db_ref)

        masks = _head_masks()
        q = _scaled(q_ref[...])
        do_t = do_ref[...]
        for s in range(nsub):
            part = slice(s * _TQ, (s + 1) * _TQ)
            j0, boff = _band_window(nsub * i + s)
            q2 = _stack_heads(q[part], masks)
            do2 = _stack_heads(do_t[part], masks).astype(_MXU)
            p, inv = _band_probs(q2, k_ref, b_ref, j0, boff)
            rows = [pl.ds(pl.multiple_of((j0 + j) * _TQ, _TQ), _TQ) for j in range(_BAND_TILES)]
            p = [x * inv for x in p]
            dp = [_dot_nt(do2, v_ref[rows[j], :]) for j in range(_BAND_TILES)]
            delta = jnp.sum(functools.reduce(lambda a, b: a + b, [p[j] * dp[j] for j in range(_BAND_TILES)]),
                            axis=-1, keepdims=True)
            dq = jnp.zeros((2 * _TQ, _LANES), _F32)
            for j in range(_BAND_TILES):
                ds = p[j] * (dp[j] - delta)
                db_ref[boff + j] += ds
                dsb = ds.astype(_MXU)
                dq = dq + _dot(dsb, k_ref[rows[j], :])
                dk_acc[rows[j], :] += _dot_tn(dsb, q2)
                dv_acc[rows[j], :] += _dot_tn(p[j].astype(_MXU), do2)
            dq_ref[part, :] = (_unstack_heads(dq, masks) * scale).astype(dq_ref.dtype)

        @pl.when(i == nq - 1)
        def _():
            dk_ref[...] = dk_acc[...].astype(dk_ref.dtype)
            dv_ref[...] = dv_acc[...].astype(dv_ref.dtype)

    strip = pl.BlockSpec((None, _BIAS_TILES, 2 * _TQ, _TQ), lambda h, i: (h, 0, 0, 0))
    tile = pl.BlockSpec((tq, _LANES), lambda h, i: (i, h))
    column = pl.BlockSpec((T, _LANES), lambda h, i: (0, h))
    outs, extra = _call(
        kern, comm, name=f"band_attn_bwd_{layer}", grid=(npair, nq),
        in_specs=_qkv_specs(T, col0 // _LANES, npair, tq) + [strip, tile],
        out_specs=[tile, column, column, strip],
        out_shape=[jax.ShapeDtypeStruct((T, width), _ACT)] * 3
        + [jax.ShapeDtypeStruct((npair, _BIAS_TILES, 2 * _TQ, _TQ), _F32)],
        scratch_shapes=[pltpu.VMEM((T, _LANES), _F32), pltpu.VMEM((T, _LANES), _F32)],
        args=(hq, hq, hq, bias, do), semantics=("arbitrary", "arbitrary"))
    return outs, extra


def _suffix_matrix():
    r = lax.broadcasted_iota(jnp.int32, (_TQ, _TQ), 0)
    c = lax.broadcasted_iota(jnp.int32, (_TQ, _TQ), 1)
    r2 = lax.broadcasted_iota(jnp.int32, (2 * _TQ, _TQ), 0)
    c2 = lax.broadcasted_iota(jnp.int32, (2 * _TQ, _TQ), 1)
    return (r > c).astype(_MXU), c2 - (r2 & (_TQ - 1))


def _suffix_sums(xs, tri):
    n, k = xs[0].shape[0], len(xs)
    his = [x.astype(_MXU) for x in xs]
    los = [(x - h.astype(_F32)).astype(_MXU) for x, h in zip(xs, his)]
    y = _dot(jnp.concatenate(his + los, axis=0), tri)
    return [y[j * n:(j + 1) * n] + y[(k + j) * n:(k + j + 1) * n] for j in range(k)]


def _stick_tiles(tiles, rel, carry_l, tri):
    zs = [_dot_nt(qs, kj) for qs, kj, _, _ in tiles]
    Ls, masks = [], []
    for z, (_, _, jj, _) in zip(zs, tiles):
        nsp = -(jnp.maximum(z, 0.0) + jnp.log(1.0 + jnp.exp(-jnp.abs(z))))
        if isinstance(jj, int):
            mask = (rel < 0) if jj == 0 else None
        else:
            mask = rel < jnp.where(jj == 0, 0, _TQ)
        Ls.append(nsp if mask is None else jnp.where(mask, nsp, 0.0))
        masks.append(mask)
    carry_l = list(carry_l)
    ws = []
    for z, L, suffix, mask, (_, _, _, sub) in zip(zs, Ls, _suffix_sums(Ls, tri), masks, tiles):
        w = jnp.exp(z + L + suffix + carry_l[sub])
        ws.append(w if mask is None else jnp.where(mask, w, 0.0))
        carry_l[sub] = carry_l[sub] + jnp.sum(L, axis=-1, keepdims=True)
    return zs, Ls, ws, masks, carry_l


def _sweep(i, step, zero):
    nsub = _SB_SUBTILES

    def window():
        tiles = [(s, jj) for jj in range(_SB_WINDOW) for s in range(nsub)]
        return tuple((jnp.int32(_SB_WINDOW),) + c for c in step(tiles, [zero] * nsub))

    start = lax.cond(i >= -(-(_SB_WINDOW - 1) // nsub), window, lambda: tuple((jnp.int32(0),) + zero for _ in range(nsub)))
    outs = []
    for s in range(nsub):
        def done(c, s=s):
            return jnp.logical_or(c[0] > nsub * i + s, jnp.max(c[1]) < _EXP_ZERO_BELOW)

        def more(c, s=s):
            carries = [None] * nsub
            carries[s] = c[1:]
            return (c[0] + 1,) + step([(s, c[0])], carries)[s]

        outs.append(lax.while_loop(lambda c, done=done: jnp.logical_not(done(c)), more, start[s]))
    return outs


def _sb_fwd(hq, col0, width, layer, comm=None):
    T = hq.shape[0]
    npair = width // _LANES
    nsub = _SB_SUBTILES
    tq = nsub * _TQ

    def kern(q_ref, k_ref, v_ref, o_ref):
        i = pl.program_id(1)
        masks = _head_masks()
        tri, rel = _suffix_matrix()
        q = _scaled(q_ref[...])
        q2 = [_stack_heads(q[s * _TQ:(s + 1) * _TQ], masks) for s in range(nsub)]

        def step(tiles, carries):
            rows = [pl.ds(pl.multiple_of((nsub * i + s - jj) * _TQ, _TQ), _TQ) for s, jj in tiles]
            cls = [None if c is None else c[0] for c in carries]
            accs = [None if c is None else c[1] for c in carries]
            _, _, ws, _, cls = _stick_tiles([(q2[s], k_ref[r, :], jj, s) for (s, jj), r in zip(tiles, rows)], rel, cls, tri)
            for w, r, (s, _) in zip(ws, rows, tiles):
                accs[s] = accs[s] + _dot(w.astype(_MXU), v_ref[r, :])
            return [None if c is None else (cls[s], accs[s]) for s, c in enumerate(carries)]

        outs = _sweep(i, step, (jnp.zeros((2 * _TQ, 1), _F32), jnp.zeros((2 * _TQ, _LANES), _F32)))
        for s in range(nsub):
            o_ref[s * _TQ:(s + 1) * _TQ, :] = _unstack_heads(outs[s][2], masks)

    outs, extra = _call(
        kern, comm, name=f"stick_attn_fwd_{layer}", grid=(npair, T // tq),
        in_specs=_qkv_specs(T, col0 // _LANES, npair, tq),
        out_specs=[pl.BlockSpec((tq, _LANES), lambda h, i: (i, h))],
        out_shape=[jax.ShapeDtypeStruct((T, width), _F32)], scratch_shapes=[],
        args=(hq, hq, hq), semantics=("arbitrary", "arbitrary"))
    return outs[0], extra


def _sb_bwd(hq, o, do, col0, width, layer, comm=None):
    T = hq.shape[0]
    npair = width // _LANES
    nsub = _SB_SUBTILES
    tq = nsub * _TQ
    nq = T // tq
    scale = _HEAD ** -0.5

    def kern(q_ref, k_ref, v_ref, o_ref, do_ref, dq_ref, dk_ref, dv_ref, dk_acc, dv_acc):
        i = pl.program_id(1)

        @pl.when(i == 0)
        def _():
            dk_acc[...] = jnp.zeros_like(dk_acc)
            dv_acc[...] = jnp.zeros_like(dv_acc)

        masks = _head_masks()
        tri, rel = _suffix_matrix()
        q = _scaled(q_ref[...])
        do_t = do_ref[...]
        prod = do_t.astype(_F32) * o_ref[...]
        part = [slice(s * _TQ, (s + 1) * _TQ) for s in range(nsub)]
        q2 = [_stack_heads(q[p], masks) for p in part]
        do2 = [_stack_heads(do_t[p], masks).astype(_MXU) for p in part]
        dsum = [jnp.sum(_stack_heads(prod[p], masks), axis=-1, keepdims=True) for p in part]

        def step(tiles, carries):
            rows = [pl.ds(pl.multiple_of((nsub * i + s - jj) * _TQ, _TQ), _TQ) for s, jj in tiles]
            kjs = [k_ref[r, :] for r in rows]
            cls, cgs, dqs = ([None if c is None else c[n] for c in carries] for n in range(3))
            zs, Ls, ws, tile_masks, cls = _stick_tiles([(q2[s], kj, jj, s) for (s, jj), kj in zip(tiles, kjs)], rel, cls, tri)
            wbs = [w.astype(_MXU) for w in ws]
            gs = [wb.astype(_F32) * _dot_nt(do2[s], v_ref[r, :]) for wb, r, (s, _) in zip(wbs, rows, tiles)]
            for z, L, g, later, mask, wb, kj, r, (s, _) in zip(zs, Ls, gs, _suffix_sums(gs, tri), tile_masks, wbs, kjs,
                                                               rows, tiles):
                dz = g - jnp.exp(z + L) * (dsum[s] - (later + cgs[s]))
                if mask is not None:
                    dz = jnp.where(mask, dz, 0.0)
                dzb = dz.astype(_MXU)
                dk_acc[r, :] += _dot_tn(dzb, q2[s])
                dv_acc[r, :] += _dot_tn(wb, do2[s])
                dqs[s] = dqs[s] + _dot(dzb, kj)
                cgs[s] = cgs[s] + jnp.sum(g, axis=-1, keepdims=True)
            return [None if c is None else (cls[s], cgs[s], dqs[s]) for s, c in enumerate(carries)]

        zc = jnp.zeros((2 * _TQ, 1), _F32)
        outs = _sweep(i, step, (zc, zc, jnp.zeros((2 * _TQ, _LANES), _F32)))
        for s in range(nsub):
            dq_ref[part[s], :] = (_unstack_heads(outs[s][3], masks) * scale).astype(dq_ref.dtype)

        @pl.when(i == nq - 1)
        def _():
            dk_ref[...] = dk_acc[...].astype(dk_ref.dtype)
            dv_ref[...] = dv_acc[...].astype(dv_ref.dtype)

    tile_spec = pl.BlockSpec((tq, _LANES), lambda h, i: (i, h))
    column = pl.BlockSpec((T, _LANES), lambda h, i: (0, h))
    outs, extra = _call(
        kern, comm, name=f"stick_attn_bwd_{layer}", grid=(npair, nq),
        in_specs=_qkv_specs(T, col0 // _LANES, npair, tq) + [tile_spec, tile_spec],
        out_specs=[tile_spec, column, column],
        out_shape=[jax.ShapeDtypeStruct((T, width), _ACT)] * 3,
        scratch_shapes=[pltpu.VMEM((T, _LANES), _F32), pltpu.VMEM((T, _LANES), _F32)],
        args=(hq, hq, hq, o, do), semantics=("arbitrary", "arbitrary"))
    return outs, extra


_DENSE = ("w_in", "w_proj_a", "w_proj_b", "w_out", "w_ffn_in", "w_ffn_out")
_COL_SHARDED = {"w_in": True, "w_proj_a": True, "w_proj_b": True, "w_out": False, "w_ffn_in": True, "w_ffn_out": False}
_SMALL = ("b_gate", "rel_bias", "ln1_g", "ln1_b", "ln2_g", "ln2_b")


class _Plans:
    def __init__(self, plans=None):
        self.plans = plans or {}

    def start(self, key):
        if key not in self.plans:
            return None, None
        return self.plans[key]()

    @staticmethod
    def finish(done, extra):
        if done is not None:
            done(extra)


def _layer_fwd(x, W, small, l, alpha, plans):
    WA = small["rel_bias"].shape[1] * _HEAD
    row = lambda v: v[l].reshape(1, -1)
    hq, hg, xb = _in_proj(x, W["w_in"], l)
    WB = (hq.shape[1] - 3 * WA) // 3
    bias = _bias_tiles(small["rel_bias"][l])
    comm, done = plans.start(f"band_fwd_{l}")
    oa, extra = _attn_a_fwd(hq, bias, 0, WA, l, comm)
    plans.finish(done, extra)
    comm, done = plans.start(f"stick_fwd_{l}")
    ob, extra = _sb_fwd(hq, 3 * WA, WB, l, comm)
    plans.finish(done, extra)
    comm, done = plans.start(f"mix_fwd_{l}")
    (x1, u1, pre, ya, yb), extra = _mix_fwd(oa, ob, hg, x, W["w_proj_a"], W["w_proj_b"], W["w_out"], row(small["b_gate"]),
                                            row(small["ln1_g"]), row(small["ln1_b"]), alpha, l, comm)
    plans.finish(done, extra)
    comm, done = plans.start(f"ffn_fwd_{l}")
    (x2, u2, act, gu, x1b), extra = _ffn_fwd(x1, W["w_ffn_in"], W["w_ffn_out"], row(small["ln2_g"]),
                                             row(small["ln2_b"]), alpha, l, comm)
    plans.finish(done, extra)
    return x2, dict(xb=xb, hq=hq, hg=hg, bias=bias, oa=oa, ob=ob, x1b=x1b, u1=u1, pre=pre, ya=ya, yb=yb, u2=u2,
                    act=act, gu=gu)


def _layer_bwd(dy_or_target, S, W, small, l, last, alpha, plans, gw):
    D = S["xb"].shape[1]
    WA, WB = S["oa"].shape[1], S["ob"].shape[1]
    row = lambda v: v[l].reshape(1, -1)

    def blocks(g, n):
        return g if _COL_SHARDED[n] else g.reshape(4, g.shape[0] // 4, g.shape[1])

    dx1, du2b, dgu, st2 = _ffn_bwd(S["u2"], dy_or_target, S["gu"], row(small["ln2_g"]), row(small["ln2_b"]),
                                   W["w_ffn_in"], W["w_ffn_out"], alpha, l, last)
    gw["w_ffn_in"] = blocks(_grad_w(S["x1b"], dgu, col_shards=True, name=f"grad_w_ffn_in_{l}")[0], "w_ffn_in")
    gw["w_ffn_out"] = blocks(_grad_w(S["act"], du2b, col_shards=False, name=f"grad_w_ffn_out_{l}")[0], "w_ffn_out")
    du1, du1b, dya, dyb, dhg, doa, dob, st1 = _mix_bwd(S["u1"], dx1, S["ya"], S["yb"], S["hg"], W["w_proj_a"],
                                                       W["w_proj_b"], W["w_out"], row(small["b_gate"]),
                                                       row(small["ln1_g"]), l)
    gw["w_out"] = blocks(_grad_w(S["pre"], du1b, col_shards=False, name=f"grad_w_out_{l}")[0], "w_out")
    gw["w_proj_a"] = blocks(_grad_w(S["oa"], dya, col_shards=True, name=f"grad_w_proj_a_{l}")[0], "w_proj_a")
    gw["w_proj_b"] = blocks(_grad_w(S["ob"], dyb, col_shards=True, name=f"grad_w_proj_b_{l}")[0], "w_proj_b")
    comm, done = plans.start(f"band_bwd_{l}")
    (dqa, dka, dva, dbias), extra = _attn_a_bwd(S["hq"], S["bias"], doa, 0, WA, l, comm)
    plans.finish(done, extra)
    comm, done = plans.start(f"stick_bwd_{l}")
    (dqb, dkb, dvb), extra = _sb_bwd(S["hq"], S["ob"], dob, 3 * WA, WB, l, comm)
    plans.finish(done, extra)
    dh = jnp.concatenate([dqa, dka, dva, dqb, dkb, dvb, dhg], axis=1)
    comm, done = plans.start(f"grad_w_in_{l}")
    g, extra = _grad_w(S["xb"], dh, col_shards=True, name=f"grad_w_in_{l}", comm=comm)
    gw["w_in"] = blocks(g, "w_in")
    plans.finish(done, extra)
    comm, done = plans.start(f"in_proj_bwd_{l}")
    dx, extra = _residual_nt(du1, alpha, dh, W["w_in"], f"in_proj_bwd_{l}", comm)
    plans.finish(done, extra)
    gs = dict(b_gate=st1[0], rel_bias=_fold_bias_grad(dbias), ln1_g=st1[1, :D], ln1_b=st1[1, D:],
              ln2_g=st2[0], ln2_b=st2[1])
    return dx, gs, st2[2]


def _local_step(x, target, W, small, plans=None, gws=None):
    depth = len(W)
    alpha = float((2 * depth) ** 0.25)
    plans = plans or _Plans()
    gws = gws if gws is not None else [dict() for _ in range(depth)]
    saved = []
    h = x
    for l in range(depth):
        h, S = _layer_fwd(h, W[l], small, l, alpha, plans)
        saved.append(S)
    gss = [None] * depth
    d = target
    sq = None
    for l in reversed(range(depth)):
        d, gss[l], sq_l = _layer_bwd(d, saved[l], W[l], small, l, l == depth - 1, alpha, plans, gws[l])
        if l == depth - 1:
            sq = sq_l
    return sq, d, gws, gss


def _place():
    return lax.axis_index("x"), lax.axis_index("y"), lax.axis_index("c")


def _remote(src, dst, send_sem, recv_sem, to):
    return pltpu.make_async_remote_copy(src_ref=src, dst_ref=dst, send_sem=send_sem, recv_sem=recv_sem,
                                        device_id=to, device_id_type=_MESH)


def _half(ref, hc):
    kh = ref.shape[0] // 2
    return ref.at[pl.ds(pl.multiple_of(hc * kh, 16), kh), :]


def _gather_plan(blocks, fractions):
    nt = len(blocks)

    def run(step, nsteps, ins, outs, sems):
        send_sems, recv_sems, loc_sems = sems
        x, y, c = _place()
        k = 2 * x + y
        me, sibling = (x, y, c), (x, y, 1 - c)
        chips = [(1 - x, y), (x, 1 - y), (1 - x, 1 - y)]
        chip_k = [2 * cx + cy for cx, cy in chips]

        def ici(t, s, owner_k, to, src=None):
            dst = _half(outs[t].at[owner_k], c)
            return _remote(dst if src is None else src, dst, send_sems.at[t, s], recv_sems.at[t, s], to)

        def passed(t, s, hc, to):
            blk = _half(outs[t].at[chip_k[s]], hc)
            return _remote(blk, blk, send_sems.at[t, 3 + s], recv_sems.at[t, 3 + s], to)

        def local(t):
            return pltpu.make_async_copy(ins[t], outs[t].at[k], loc_sems.at[t])

        @pl.when(step == 0)
        def _():
            for t in range(nt):
                local(t).start()
                for s, chip in enumerate(chips):
                    ici(t, s, k, (*chip, c), src=_half(ins[t], c)).start()

        for t in range(nt):
            @pl.when(step == min(nsteps - 1, int(fractions[t] * nsteps)))
            def _():
                for s in range(3):
                    ici(t, s, chip_k[s], me).wait_recv()
                    passed(t, s, c, sibling).start()

        @pl.when(step == nsteps - 1)
        def _():
            for t in range(nt):
                for s, chip in enumerate(chips):
                    passed(t, s, 1 - c, me).wait_recv()
            for t in range(nt):
                for s, chip in enumerate(chips):
                    ici(t, s, k, (*chip, c), src=_half(ins[t], c)).wait_send()
                    passed(t, s, c, sibling).wait_send()
                local(t).wait()

    return _Comm(blocks, [jax.ShapeDtypeStruct((4,) + b.shape, b.dtype) for b in blocks],
                 [pltpu.SemaphoreType.DMA((nt, 6)), pltpu.SemaphoreType.DMA((nt, 6)), pltpu.SemaphoreType.DMA((nt,))], run)


def _scatter_plan(grads, owners):
    nt = len(grads)

    def run(step, nsteps, ins, outs, sems):
        send_sems, recv_sems, loc_sems = sems
        x, y, c = _place()
        me = 4 * x + 2 * y + c

        def target(r):
            tx = 1 - x if r & 2 else x
            ty = 1 - y if r & 1 else y
            return tx, ty

        def send(t, r):
            tx, ty = target(r)
            return _remote(ins[t].at[2 * tx + ty], outs[t].at[me], send_sems.at[t, r], recv_sems.at[t, 2 * r + c],
                           (tx, ty, owners[t]))

        def local(t):
            return pltpu.make_async_copy(ins[t].at[2 * x + y], outs[t].at[me], loc_sems.at[t])

        @pl.when(step == 0)
        def _():
            for t in range(nt):
                @pl.when(c == owners[t])
                def _():
                    local(t).start()

                @pl.when(c != owners[t])
                def _():
                    send(t, 0).start()

                for r in range(1, 4):
                    send(t, r).start()

        @pl.when(step == nsteps - 1)
        def _():
            for t in range(nt):
                @pl.when(c == owners[t])
                def _():
                    for r in range(4):
                        sx, sy = target(r)
                        for cs in range(2):
                            if r == 0 and cs == owners[t]:
                                continue
                            src_dev = 4 * sx + 2 * sy + cs
                            _remote(ins[t].at[0], outs[t].at[src_dev], send_sems.at[t, r], recv_sems.at[t, 2 * r + cs],
                                    (x, y, c)).wait_recv()
                    local(t).wait()

                @pl.when(c != owners[t])
                def _():
                    send(t, 0).wait_send()

                for r in range(1, 4):
                    send(t, r).wait_send()

    return _Comm(grads, [jax.ShapeDtypeStruct((8,) + g.shape[1:], g.dtype) for g in grads],
                 [pltpu.SemaphoreType.DMA((nt, 4)), pltpu.SemaphoreType.DMA((nt, 8)), pltpu.SemaphoreType.DMA((nt,))], run)


def _share_plan(reduced, owners):
    nt = len(reduced)

    def run(step, nsteps, ins, outs, sems):
        del ins
        send_sems, recv_sems = sems
        x, y, c = _place()

        def give(t, to):
            return _remote(outs[t], outs[t], send_sems.at[t], recv_sems.at[t], to)

        @pl.when(step == 0)
        def _():
            for t in range(nt):
                @pl.when(c == owners[t])
                def _():
                    give(t, (x, y, 1 - c)).start()

        @pl.when(step == nsteps - 1)
        def _():
            for t in range(nt):
                @pl.when(c == owners[t])
                def _():
                    give(t, (x, y, 1 - c)).wait_send()

                @pl.when(c != owners[t])
                def _():
                    give(t, (x, y, c)).wait_recv()

    return _Comm(reduced, [jax.ShapeDtypeStruct(r.shape, r.dtype) for r in reduced],
                 [pltpu.SemaphoreType.DMA((nt,)), pltpu.SemaphoreType.DMA((nt,))], run,
                 aliases={t: t for t in range(nt)})


def _join(a, b):
    ni, no, ns = len(a.inputs), len(a.out_shapes), len(a.sems)

    def run(step, nsteps, ins, outs, sems):
        a.run(step, nsteps, ins[:ni], outs[:no], sems[:ns])
        b.run(step, nsteps, ins[ni:], outs[no:], sems[ns:])

    aliases = dict(a.aliases)
    aliases.update({ni + i: no + o for i, o in b.aliases.items()})
    return _Comm(a.inputs + b.inputs, a.out_shapes + b.out_shapes, a.sems + b.sems, run, aliases)


def _peer(x, y, c, r):
    px = 1 - x if r & 4 else x
    py = 1 - y if r & 2 else y
    pc = 1 - c if r & 1 else c
    return (px, py, pc), 4 * px + 2 * py + pc


def _sum_slots(st, name):
    _, K, n = st.shape
    tr = next(t for t in (256, 128, 64, 32, 16) if K % t == 0)

    def kern(s_ref, o_ref):
        acc = s_ref[0].astype(_F32)
        for d in range(1, 8):
            acc = acc + s_ref[d].astype(_F32)
        o_ref[...] = acc.astype(o_ref.dtype)

    return pl.pallas_call(
        kern, name=name, grid=(K // tr,),
        in_specs=[pl.BlockSpec((8, tr, n), lambda i: (0, i, 0))], out_specs=_rows(tr, n),
        out_shape=jax.ShapeDtypeStruct((K, n), _ACT),
        compiler_params=_cparams("parallel"),
    )(st)


def _all_reduce_small(p):
    R = p.shape[0]

    def body(p_ref, o_ref, stage, send_sems, recv_sems):
        x, y, c = _place()
        me = 4 * x + 2 * y + c
        stage[me] = p_ref[...]
        sent = []
        for r in range(1, 8):
            to, _ = _peer(x, y, c, r)
            cp = _remote(p_ref, stage.at[me], send_sems.at[r - 1], recv_sems.at[r - 1], to)
            cp.start()
            sent.append(cp)
        for r in range(1, 8):
            _, src_dev = _peer(x, y, c, r)
            _remote(p_ref, stage.at[src_dev], send_sems.at[r - 1], recv_sems.at[r - 1], (x, y, c)).wait_recv()
        acc = stage[0]
        for d in range(1, 8):
            acc = acc + stage[d]
        o_ref[...] = acc
        for cp in sent:
            cp.wait_send()

    vm = pl.BlockSpec(memory_space=pltpu.VMEM)
    return pl.pallas_call(
        body, name="all_reduce_small",
        in_specs=[vm], out_specs=vm,
        out_shape=jax.ShapeDtypeStruct((R, _LANES), _F32),
        scratch_shapes=[pltpu.VMEM((8, R, _LANES), _F32), pltpu.SemaphoreType.DMA((7,)), pltpu.SemaphoreType.DMA((7,))],
    )(p)


def _adamw_update(gv, w_ref, m_ref, v_ref, gf_ref, d_ref, nm_ref, nv_ref):
    nm = _B1 * m_ref[...] + (1.0 - _B1) * gv
    nv = _B2 * v_ref[...] + (1.0 - _B2) * (gv * gv)
    m_hat = nm / (1.0 - _B1 ** _STEP)
    v_hat = nv / (1.0 - _B2 ** _STEP)
    gf_ref[...] = gv
    d_ref[...] = -_LR * (m_hat / (jnp.sqrt(v_hat) + _EPS) + _WD * w_ref[...])
    nm_ref[...] = nm
    nv_ref[...] = nv


def _adamw_layers(w, g_layers, m, v, name):
    _, K, n = w.shape
    tr = next(t for t in (256, 128, 64, 32, 16) if K % t == 0)

    def kern(w_ref, g0_ref, g1_ref, m_ref, v_ref, *out_refs):
        first = pl.program_id(0) == 0
        gv = jnp.where(first, g0_ref[...].astype(_F32), g1_ref[...].astype(_F32))
        _adamw_update(gv, w_ref, m_ref, v_ref, *out_refs)

    stacked = pl.BlockSpec((None, tr, n), lambda l, i: (l, i, 0))
    layer = pl.BlockSpec((tr, n), lambda l, i: (i, 0))
    return tuple(pl.pallas_call(
        kern, name=name, grid=(2, K // tr),
        in_specs=[stacked, layer, layer, stacked, stacked], out_specs=[stacked] * 4,
        out_shape=[jax.ShapeDtypeStruct(w.shape, _F32)] * 4,
        compiler_params=_cparams("parallel", "parallel"),
    )(w, g_layers[0], g_layers[1], m, v))


def _adamw(w, g, m, v, name):
    shape = w.shape
    w2, g2, m2, v2 = (a.reshape(-1, shape[-1]) for a in (w, g, m, v))
    R, C = w2.shape
    tr = next((t for t in (256, 128, 64, 32, 16) if R % t == 0), R)

    def kern(w_ref, g_ref, m_ref, v_ref, *out_refs):
        _adamw_update(g_ref[...].astype(_F32), w_ref, m_ref, v_ref, *out_refs)

    outs = pl.pallas_call(
        kern, name=name, grid=(R // tr,),
        in_specs=[_rows(tr, C)] * 4, out_specs=[_rows(tr, C)] * 4,
        out_shape=[jax.ShapeDtypeStruct((R, C), _F32)] * 4,
        compiler_params=_cparams("parallel"),
    )(w2, g2, m2, v2)
    return tuple(o.reshape(shape) for o in outs)


def _pack_small(gss, sq):
    parts = [gss[l][n].reshape(-1) for n in _SMALL for l in range(len(gss))] + [jnp.sum(sq).reshape(1)]
    flat = jnp.concatenate(parts)
    rows = -(-flat.shape[0] // (8 * _LANES)) * 8
    return jnp.pad(flat, (0, rows * _LANES - flat.shape[0])).reshape(rows, _LANES)


def _unpack_small(total, shapes):
    flat = total.reshape(-1)
    out, off = {}, 0
    for n in _SMALL:
        layers = []
        for _ in range(shapes[n][0]):
            size = 1
            for s in shapes[n][1:]:
                size *= s
            layers.append(flat[off:off + size].reshape(shapes[n][1:]))
            off += size
        out[n] = jnp.stack(layers)
    return out, flat[off]


_GATHER = {
    "band_fwd_0": [(0, "w_proj_a"), (0, "w_proj_b"), (0, "w_out"), (0, "w_ffn_out")],
    "stick_fwd_0": [(0, "w_ffn_in")],
    "mix_fwd_0": [(1, "w_in")],
    "ffn_fwd_0": [(1, "w_proj_a"), (1, "w_proj_b"), (1, "w_out"), (1, "w_ffn_in"), (1, "w_ffn_out")],
}
_SCATTER = {
    "band_bwd_1": [(1, "w_ffn_in"), (1, "w_ffn_out")],
    "stick_bwd_1": [(1, "w_proj_a"), (1, "w_proj_b"), (1, "w_out")],
    "band_bwd_0": [(1, "w_in"), (0, "w_ffn_in")],
    "stick_bwd_0": [(0, "w_ffn_out"), (0, "w_proj_a"), (0, "w_proj_b"), (0, "w_out")],
    "in_proj_bwd_0": [(0, "w_in")],
}
_SHARE = {"stick_bwd_1": "band_bwd_1", "band_bwd_0": "stick_bwd_1", "stick_bwd_0": "band_bwd_0", "grad_w_in_0": "stick_bwd_0"}


def _owner(key):
    del key
    return 1


def kernel(x, w_in, b_gate, rel_bias, w_proj_a, w_proj_b, w_out, ln1_g, ln1_b, w_ffn_in, w_ffn_out, ln2_g, ln2_b, loss_target, m_w_in, m_b_gate, m_rel_bias, m_w_proj_a, m_w_proj_b, m_w_out, m_ln1_g, m_ln1_b, m_w_ffn_in, m_w_ffn_out, m_ln2_g, m_ln2_b, v_w_in, v_b_gate, v_rel_bias, v_w_proj_a, v_w_proj_b, v_w_out, v_ln1_g, v_ln1_b, v_w_ffn_in, v_w_ffn_out, v_ln2_g, v_ln2_b):
    names = ("w_in", "b_gate", "rel_bias", "w_proj_a", "w_proj_b", "w_out", "ln1_g", "ln1_b", "w_ffn_in", "w_ffn_out", "ln2_g", "ln2_b")
    w = dict(zip(names, (w_in, b_gate, rel_bias, w_proj_a, w_proj_b, w_out, ln1_g, ln1_b, w_ffn_in, w_ffn_out, ln2_g, ln2_b)))
    m = dict(zip(names, (m_w_in, m_b_gate, m_rel_bias, m_w_proj_a, m_w_proj_b, m_w_out, m_ln1_g, m_ln1_b, m_w_ffn_in, m_w_ffn_out, m_ln2_g, m_ln2_b)))
    v = dict(zip(names, (v_w_in, v_b_gate, v_rel_bias, v_w_proj_a, v_w_proj_b, v_w_out, v_ln1_g, v_ln1_b, v_w_ffn_in, v_w_ffn_out, v_ln2_g, v_ln2_b)))
    T, D = x.shape[-2], x.shape[-1]
    assert w_in.shape[0] == 2, "the exchange schedule below is written for two layers"

    mine = [{n: w[n][l].astype(_MXU) for n in _DENSE} for l in range(2)]
    W = [dict(), dict()]
    gws = [dict(), dict()]
    slots, final = {}, {}

    def gather(keys):
        sizes = [mine[l][n].size for l, n in keys]
        passed, fractions = 0, []
        for s in sizes:
            passed += s
            fractions.append(0.15 + 0.6 * passed / sum(sizes))

        def done(outs):
            for (l, n), o in zip(keys, outs):
                W[l][n] = o
        return _gather_plan([mine[l][n] for l, n in keys], fractions), done

    def scatter(keys):
        comm = _scatter_plan([gws[l][n] for l, n in keys], [_owner(key) for key in keys])
        return comm, lambda outs: slots.update(zip(keys, outs))

    def share(keys):
        reduced = [_sum_slots(slots[key], f"sum_grad_{key[1]}_{key[0]}") for key in keys]
        comm = _share_plan(reduced, [_owner(key) for key in keys])
        return comm, lambda outs: final.update(zip(keys, outs))

    def both(first, second):
        (ca, da), (cb, db) = first, second
        na = len(ca.out_shapes)
        return _join(ca, cb), lambda outs: (da(outs[:na]), db(outs[na:]))

    comm, done = gather([(0, "w_in")])
    done(_comm_only(comm, "gather_first"))
    plans = {key: functools.partial(gather, keys) for key, keys in _GATHER.items()}
    for key, keys in _SCATTER.items():
        plans[key] = functools.partial(scatter, keys)
    for key, scattered_under in _SHARE.items():
        handed = functools.partial(share, _SCATTER[scattered_under])
        carried = plans.get(key)
        plans[key] = handed if carried is None else (lambda carried=carried, handed=handed: both(carried(), handed()))
    small = {n: w[n] for n in _SMALL}
    sq, dx, _, gss = _local_step(x.reshape(T, D), loss_target.reshape(T, D), W, small, _Plans(plans), gws)

    comm, done = share(_SCATTER["in_proj_bwd_0"])
    done(_comm_only(comm, "share_last"))
    total = _all_reduce_small(_pack_small(gss, sq))
    small_grads, sq_all = _unpack_small(total, {n: w[n].shape for n in _SMALL})
    loss = 0.5 * sq_all / D

    grad, delta, new_m, new_v = {}, {}, {}, {}
    for n in names:
        if n in _DENSE:
            updated = _adamw_layers(w[n], [final[(l, n)] for l in range(2)], m[n], v[n], f"adamw_{n}")
        else:
            updated = _adamw(w[n], small_grads[n], m[n], v[n], f"adamw_{n}")
        grad[n], delta[n], new_m[n], new_v[n] = updated
    return (loss, dx.reshape(x.shape), *[grad[n] for n in names], *[delta[n] for n in names],
            *[new_m[n] for n in names], *[new_v[n] for n in names])
```

```python
import functools

import jax
import jax.numpy as jnp
from jax import lax
from jax.experimental import pallas as pl
from jax.experimental.pallas import tpu as pltpu

_MXU = jnp.bfloat16
_ACT = jnp.bfloat16
_F32 = jnp.float32

_HEAD = 64
_CHUNK = 64
_LANES = 128
_TQ = 128
_BAND_TILES = 5
_BIAS_TILES = 9
_REL_CLIP = 256
_LN_EPS = 1e-5
_MASKED = -1e30
_EXP_ZERO_BELOW = -87.34
_SB_WINDOW = 2
_SB_SUBTILES = 8
_BAND_SUBTILES = 8
_VMEM_LIMIT = 56 * 1024 * 1024
_GRAD_ACC_BYTES = 12 * 1024 * 1024

_LR, _B1, _B2, _EPS, _WD, _STEP = 0.001, 0.9, 0.999, 1e-08, 0.01, 10

_MESH = pl.DeviceIdType.MESH


def _dot(a, b):
    return jnp.dot(a, b, preferred_element_type=_F32)


def _dot_nt(a, b):
    return lax.dot_general(a, b, (((1,), (1,)), ((), ())), preferred_element_type=_F32)


def _dot_tn(a, b):
    return lax.dot_general(a, b, (((0,), (0,)), ((), ())), preferred_element_type=_F32)


def _cparams(*sem):
    return pltpu.CompilerParams(dimension_semantics=sem, vmem_limit_bytes=_VMEM_LIMIT)


def _rows(t, c):
    return pl.BlockSpec((t, c), lambda i: (i, 0))


def _whole(shape):
    return pl.BlockSpec(shape, lambda i: tuple(0 for _ in shape))


_ANY = pl.BlockSpec(memory_space=pl.ANY)


def _load_cols(w_hbm, w_vmem, sem):
    n = w_hbm.shape[-1]
    cps = [pltpu.make_async_copy(w_hbm.at[k], w_vmem.at[:, pl.ds(k * n, n)], sem.at[k]) for k in range(4)]
    for cp in cps:
        cp.start()
    for cp in cps:
        cp.wait()


def _load_rows(w_hbm, w_vmem, sem):
    r = w_hbm.shape[-2]
    cps = [pltpu.make_async_copy(w_hbm.at[k], w_vmem.at[pl.ds(k * r, r), :], sem.at[k]) for k in range(4)]
    for cp in cps:
        cp.start()
    for cp in cps:
        cp.wait()


def _ln_stats(u):
    mu = jnp.mean(u, axis=-1, keepdims=True)
    xc = u - mu
    var = jnp.mean(xc * xc, axis=-1, keepdims=True)
    rstd = lax.rsqrt(var + _LN_EPS)
    return xc * rstd, rstd


def _ln_bwd(u, dy, gamma):
    xhat, rstd = _ln_stats(u)
    dxh = dy * gamma
    m1 = jnp.mean(dxh, axis=-1, keepdims=True)
    m2 = jnp.mean(dxh * xhat, axis=-1, keepdims=True)
    du = rstd * (dxh - m1 - xhat * m2)
    return du, jnp.sum(dy * xhat, axis=0, keepdims=True), jnp.sum(dy, axis=0, keepdims=True), xhat


def _divisor_tile(n, cap):
    best = None
    for t in range(_LANES, min(n, cap) + 1, _LANES):
        if n % t == 0:
            best = t
    return best or n


class _Comm:
    def __init__(self, inputs, out_shapes, sems, run, aliases=None):
        self.inputs, self.out_shapes, self.sems, self.run = list(inputs), list(out_shapes), list(sems), run
        self.aliases = aliases or {}


def _call(kern, comm, *, name, grid, in_specs, out_specs, out_shape, scratch_shapes, args, semantics):
    in_specs, out_specs, out_shape, scratch_shapes = list(in_specs), list(out_specs), list(out_shape), list(scratch_shapes)
    if comm is None:
        outs = pl.pallas_call(kern, name=name, grid=grid, in_specs=in_specs, out_specs=out_specs, out_shape=out_shape,
                              scratch_shapes=scratch_shapes, compiler_params=_cparams(*semantics))(*args)
        return list(outs), []
    n_in, n_out, n_scr = len(in_specs), len(out_specs), len(scratch_shapes)
    ci, co = len(comm.inputs), len(comm.out_shapes)
    nsteps = functools.reduce(lambda a, b: a * b, grid, 1)

    def fused(*refs):
        a, b = n_in, n_in + ci
        c, d = b + n_out, b + n_out + co
        e = d + n_scr
        step = pl.program_id(0)
        for ax in range(1, len(grid)):
            step = step * grid[ax] + pl.program_id(ax)
        comm.run(step, nsteps, refs[a:b], refs[c:d], refs[e:])
        kern(*refs[:a], *refs[b:c], *refs[d:e])

    outs = pl.pallas_call(
        fused, name=name, grid=grid, in_specs=in_specs + [_ANY] * ci, out_specs=out_specs + [_ANY] * co,
        out_shape=out_shape + comm.out_shapes, scratch_shapes=scratch_shapes + comm.sems,
        input_output_aliases={n_in + i: n_out + o for i, o in comm.aliases.items()},
        compiler_params=_cparams(*("arbitrary" for _ in grid)))(*args, *comm.inputs)
    return list(outs[:n_out]), list(outs[n_out:])


def _comm_only(comm, name):
    def body(*refs):
        ci, co = len(comm.inputs), len(comm.out_shapes)
        comm.run(0, 1, refs[:ci], refs[ci:ci + co], refs[ci + co:])

    outs = pl.pallas_call(body, name=name, in_specs=[_ANY] * len(comm.inputs), out_specs=[_ANY] * len(comm.out_shapes),
                          out_shape=comm.out_shapes, scratch_shapes=comm.sems,
                          input_output_aliases=dict(comm.aliases))(*comm.inputs)
    return list(outs)


def _in_proj(x, w_in, layer):
    T, D = x.shape
    N = 4 * w_in.shape[-1]
    NQ = N - 2 * D
    tm = 512

    def kern(x_ref, w_hbm, hq_ref, hg_ref, xb_ref, w_v, sem):
        @pl.when(pl.program_id(0) == 0)
        def _():
            _load_cols(w_hbm, w_v, sem)

        xb = x_ref[...].astype(_MXU)
        hq_ref[...] = _dot(xb, w_v[:, :NQ]).astype(hq_ref.dtype)
        hg_ref[...] = _dot(xb, w_v[:, NQ:]).astype(hg_ref.dtype)
        xb_ref[...] = xb.astype(xb_ref.dtype)

    return pl.pallas_call(
        kern, name=f"in_proj_{layer}", grid=(T // tm,),
        in_specs=[_rows(tm, D), _ANY],
        out_specs=[_rows(tm, NQ), _rows(tm, 2 * D), _rows(tm, D)],
        out_shape=[jax.ShapeDtypeStruct((T, NQ), _ACT), jax.ShapeDtypeStruct((T, 2 * D), _ACT),
                   jax.ShapeDtypeStruct((T, D), _ACT)],
        scratch_shapes=[pltpu.VMEM((D, N), w_in.dtype), pltpu.SemaphoreType.DMA((4,))],
        compiler_params=_cparams("arbitrary"),
    )(x, w_in)


def _mix_fwd(oa, ob, hg, x, wpa, wpb, wo, bg, gamma, beta, alpha, layer, comm=None):
    T, D = x.shape
    WA, WB = oa.shape[1], ob.shape[1]
    tm = 256

    def kern(oa_ref, ob_ref, hg_ref, x_ref, bg_ref, g_ref, b_ref, wpa_h, wpb_h, wo_h,
             x1_ref, u1_ref, pre_ref, ya_ref, yb_ref, wpa_v, wpb_v, wo_v, sa, sb, so):
        @pl.when(pl.program_id(0) == 0)
        def _():
            _load_cols(wpa_h, wpa_v, sa)
            _load_cols(wpb_h, wpb_v, sb)
            _load_rows(wo_h, wo_v, so)

        ya = _dot(oa_ref[...].astype(_MXU), wpa_v[...])
        yb = _dot(ob_ref[...].astype(_MXU), wpb_v[...])
        hgv = hg_ref[...].astype(_F32)
        bgv = bg_ref[...]
        ga = jax.nn.sigmoid(hgv[:, :D] + bgv[:, :D])
        gb = jax.nn.sigmoid(hgv[:, D:] + bgv[:, D:])
        pre = ga * ya + gb * yb
        mix = _dot(pre.astype(_MXU), wo_v[...])
        u = alpha * x_ref[...] + mix
        xhat, _ = _ln_stats(u)
        x1_ref[...] = xhat * g_ref[...] + b_ref[...]
        u1_ref[...] = u
        pre_ref[...] = pre.astype(pre_ref.dtype)
        ya_ref[...] = ya.astype(ya_ref.dtype)
        yb_ref[...] = yb.astype(yb_ref.dtype)

    return _call(
        kern, comm, name=f"mix_fwd_{layer}", grid=(T // tm,),
        in_specs=[_rows(tm, WA), _rows(tm, WB), _rows(tm, 2 * D), _rows(tm, D),
                  _whole((1, 2 * D)), _whole((1, D)), _whole((1, D)), _ANY, _ANY, _ANY],
        out_specs=[_rows(tm, D)] * 5,
        out_shape=[jax.ShapeDtypeStruct((T, D), _F32), jax.ShapeDtypeStruct((T, D), _F32)]
        + [jax.ShapeDtypeStruct((T, D), _ACT)] * 3,
        scratch_shapes=[pltpu.VMEM((WA, D), wpa.dtype), pltpu.VMEM((WB, D), wpb.dtype), pltpu.VMEM((D, D), wo.dtype),
                        pltpu.SemaphoreType.DMA((4,)), pltpu.SemaphoreType.DMA((4,)), pltpu.SemaphoreType.DMA((4,))],
        args=(oa, ob, hg, x, bg, gamma, beta, wpa, wpb, wo), semantics=("arbitrary",))


def _ffn_fwd(x1, wfi, wfo, gamma, beta, alpha, layer, comm=None):
    T, D = x1.shape
    F2 = 4 * wfi.shape[-1]
    F = F2 // 2
    tm = 256
    fc = F // 2

    def kern(x_ref, g_ref, b_ref, wi_h, wo_h, x2_ref, u2_ref, act_ref, gu_ref, xb_ref, wi_v, wo_v, si, so):
        @pl.when(pl.program_id(0) == 0)
        def _():
            _load_cols(wi_h, wi_v, si)
            _load_rows(wo_h, wo_v, so)

        x = x_ref[...]
        xb = x.astype(_MXU)
        xb_ref[...] = xb.astype(xb_ref.dtype)
        ffn = jnp.zeros((tm, D), _F32)
        for c in range(2):
            g = _dot(xb, wi_v[:, c * fc:(c + 1) * fc])
            u = _dot(xb, wi_v[:, F + c * fc:F + (c + 1) * fc])
            act = g * jax.nn.sigmoid(g) * u
            ab = act.astype(_MXU)
            ffn = ffn + _dot(ab, wo_v[c * fc:(c + 1) * fc, :])
            act_ref[:, c * fc:(c + 1) * fc] = ab.astype(act_ref.dtype)
            gu_ref[:, c * fc:(c + 1) * fc] = g.astype(gu_ref.dtype)
            gu_ref[:, F + c * fc:F + (c + 1) * fc] = u.astype(gu_ref.dtype)
        uu = alpha * x + ffn
        xhat, _ = _ln_stats(uu)
        x2_ref[...] = xhat * g_ref[...] + b_ref[...]
        u2_ref[...] = uu

    return _call(
        kern, comm, name=f"ffn_fwd_{layer}", grid=(T // tm,),
        in_specs=[_rows(tm, D), _whole((1, D)), _whole((1, D)), _ANY, _ANY],
        out_specs=[_rows(tm, D), _rows(tm, D), _rows(tm, F), _rows(tm, F2), _rows(tm, D)],
        out_shape=[jax.ShapeDtypeStruct((T, D), _F32), jax.ShapeDtypeStruct((T, D), _F32),
                   jax.ShapeDtypeStruct((T, F), _ACT), jax.ShapeDtypeStruct((T, F2), _ACT),
                   jax.ShapeDtypeStruct((T, D), _ACT)],
        scratch_shapes=[pltpu.VMEM((D, F2), wfi.dtype), pltpu.VMEM((F, D), wfo.dtype),
                        pltpu.SemaphoreType.DMA((4,)), pltpu.SemaphoreType.DMA((4,))],
        args=(x1, gamma, beta, wfi, wfo), semantics=("arbitrary",))


def _ffn_bwd(u2, dy_or_target, gu, gamma, beta, wfi, wfo, alpha, layer, last):
    T, D = u2.shape
    F2 = gu.shape[1]
    F = F2 // 2
    tm = 256
    fc = F // 2

    def kern(u_ref, dy_ref, gu_ref, g_ref, b_ref, wi_h, wo_h, dx_ref, dub_ref, dgu_ref, st_ref, wi_v, wo_v, si, so):
        @pl.when(pl.program_id(0) == 0)
        def _():
            _load_cols(wi_h, wi_v, si)
            _load_rows(wo_h, wo_v, so)
            st_ref[...] = jnp.zeros_like(st_ref)

        gam = g_ref[...]
        u = u_ref[...]
        if last:
            xhat0, _ = _ln_stats(u)
            err = xhat0 * gam + b_ref[...] - dy_ref[...]
            dy = err * (1.0 / D)
            st_ref[2:3, :] += jnp.sum(err * err, axis=0, keepdims=True)
        else:
            dy = dy_ref[...]
        du, dgam, dbet, _ = _ln_bwd(u, dy, gam)
        st_ref[0:1, :] += dgam
        st_ref[1:2, :] += dbet
        dub = du.astype(_MXU)
        dub_ref[...] = dub.astype(dub_ref.dtype)
        dx = alpha * du
        for c in range(2):
            dact = _dot_nt(dub, wo_v[c * fc:(c + 1) * fc, :])
            g = gu_ref[:, c * fc:(c + 1) * fc].astype(_F32)
            uu = gu_ref[:, F + c * fc:F + (c + 1) * fc].astype(_F32)
            sg = jax.nn.sigmoid(g)
            dg = (dact * uu * (sg * (1.0 + g * (1.0 - sg)))).astype(_MXU)
            dup = (dact * (g * sg)).astype(_MXU)
            dgu_ref[:, c * fc:(c + 1) * fc] = dg.astype(dgu_ref.dtype)
            dgu_ref[:, F + c * fc:F + (c + 1) * fc] = dup.astype(dgu_ref.dtype)
            dx = dx + _dot_nt(dg, wi_v[:, c * fc:(c + 1) * fc]) + _dot_nt(dup, wi_v[:, F + c * fc:F + (c + 1) * fc])
        dx_ref[...] = dx

    return pl.pallas_call(
        kern, name=f"ffn_bwd_{layer}", grid=(T // tm,),
        in_specs=[_rows(tm, D), _rows(tm, D), _rows(tm, F2), _whole((1, D)), _whole((1, D)), _ANY, _ANY],
        out_specs=[_rows(tm, D), _rows(tm, D), _rows(tm, F2), _whole((8, D))],
        out_shape=[jax.ShapeDtypeStruct((T, D), _F32), jax.ShapeDtypeStruct((T, D), _ACT),
                   jax.ShapeDtypeStruct((T, F2), _ACT), jax.ShapeDtypeStruct((8, D), _F32)],
        scratch_shapes=[pltpu.VMEM((D, F2), wfi.dtype), pltpu.VMEM((F, D), wfo.dtype),
                        pltpu.SemaphoreType.DMA((4,)), pltpu.SemaphoreType.DMA((4,))],
        compiler_params=_cparams("arbitrary"),
    )(u2, dy_or_target, gu, gamma, beta, wfi, wfo)


def _residual_nt(res, res_scale, d, w, name, comm=None):
    T, K = res.shape
    N = d.shape[1]
    tm = 512

    def kern(r_ref, d_ref, w_hbm, o_ref, w_v, sem):
        @pl.when(pl.program_id(0) == 0)
        def _():
            _load_cols(w_hbm, w_v, sem)

        o_ref[...] = res_scale * r_ref[...] + _dot_nt(d_ref[...].astype(_MXU), w_v[...])

    outs, extra = _call(
        kern, comm, name=name, grid=(T // tm,),
        in_specs=[_rows(tm, K), _rows(tm, N), _ANY], out_specs=[_rows(tm, K)],
        out_shape=[jax.ShapeDtypeStruct((T, K), _F32)],
        scratch_shapes=[pltpu.VMEM((K, N), w.dtype), pltpu.SemaphoreType.DMA((4,))],
        args=(res, d, w), semantics=("arbitrary",))
    return outs[0], extra


def _mix_bwd(u1, dx1, ya, yb, hg, wpa, wpb, wo, bg, gamma, layer):
    T, D = u1.shape
    WA, WB = wpa.shape[-2], wpb.shape[-2]
    tm = 256

    def kern(u_ref, dx_ref, ya_ref, yb_ref, hg_ref, bg_ref, g_ref, wpa_h, wpb_h, wo_h,
             du_ref, dub_ref, dya_ref, dyb_ref, dhg_ref, doa_ref, dob_ref, st_ref,
             wpa_v, wpb_v, wo_v, sa, sb, so):
        @pl.when(pl.program_id(0) == 0)
        def _():
            _load_cols(wpa_h, wpa_v, sa)
            _load_cols(wpb_h, wpb_v, sb)
            _load_rows(wo_h, wo_v, so)
            st_ref[...] = jnp.zeros_like(st_ref)

        du, dgam, dbet, _ = _ln_bwd(u_ref[...], dx_ref[...], g_ref[...])
        st_ref[1:2, :D] += dgam
        st_ref[1:2, D:] += dbet
        du_ref[...] = du
        dub = du.astype(_MXU)
        dub_ref[...] = dub.astype(dub_ref.dtype)
        dpre = _dot_nt(dub, wo_v[...])
        hgv = hg_ref[...].astype(_F32)
        bgv = bg_ref[...]
        ga = jax.nn.sigmoid(hgv[:, :D] + bgv[:, :D])
        gb = jax.nn.sigmoid(hgv[:, D:] + bgv[:, D:])
        dya = (dpre * ga).astype(_MXU)
        dyb = (dpre * gb).astype(_MXU)
        dsa = dpre * ya_ref[...].astype(_F32) * (ga * (1.0 - ga))
        dsb = dpre * yb_ref[...].astype(_F32) * (gb * (1.0 - gb))
        st_ref[0:1, :D] += jnp.sum(dsa, axis=0, keepdims=True)
        st_ref[0:1, D:] += jnp.sum(dsb, axis=0, keepdims=True)
        dya_ref[...] = dya.astype(dya_ref.dtype)
        dyb_ref[...] = dyb.astype(dyb_ref.dtype)
        dhg_ref[:, :D] = dsa.astype(dhg_ref.dtype)
        dhg_ref[:, D:] = dsb.astype(dhg_ref.dtype)
        doa_ref[...] = _dot_nt(dya, wpa_v[...]).astype(doa_ref.dtype)
        dob_ref[...] = _dot_nt(dyb, wpb_v[...]).astype(dob_ref.dtype)

    return pl.pallas_call(
        kern, name=f"mix_bwd_{layer}", grid=(T // tm,),
        in_specs=[_rows(tm, D)] * 4 + [_rows(tm, 2 * D), _whole((1, 2 * D)), _whole((1, D)), _ANY, _ANY, _ANY],
        out_specs=[_rows(tm, D)] * 4 + [_rows(tm, 2 * D), _rows(tm, WA), _rows(tm, WB), _whole((8, 2 * D))],
        out_shape=[jax.ShapeDtypeStruct((T, D), _F32)] + [jax.ShapeDtypeStruct((T, D), _ACT)] * 3
        + [jax.ShapeDtypeStruct((T, 2 * D), _ACT), jax.ShapeDtypeStruct((T, WA), _ACT),
           jax.ShapeDtypeStruct((T, WB), _ACT), jax.ShapeDtypeStruct((8, 2 * D), _F32)],
        scratch_shapes=[pltpu.VMEM((WA, D), wpa.dtype), pltpu.VMEM((WB, D), wpb.dtype), pltpu.VMEM((D, D), wo.dtype),
                        pltpu.SemaphoreType.DMA((4,)), pltpu.SemaphoreType.DMA((4,)), pltpu.SemaphoreType.DMA((4,))],
        compiler_params=_cparams("arbitrary"),
    )(u1, dx1, ya, yb, hg, bg, gamma, wpa, wpb, wo)


def _grad_w(a, b, *, col_shards, name, comm=None):
    T, M = a.shape
    N = b.shape[1]
    tk = 512
    n = N // 4 if col_shards else N
    whole = M * N * 4 <= _GRAD_ACC_BYTES
    tn = N if whole else (n if col_shards else _divisor_tile(N, _GRAD_ACC_BYTES // (4 * M)))
    nk = T // tk

    def kern(a_ref, b_ref, o_ref, acc):
        k = pl.program_id(1)

        @pl.when(k == 0)
        def _():
            acc[...] = jnp.zeros_like(acc)

        acc[...] += _dot_tn(a_ref[...].astype(_MXU), b_ref[...].astype(_MXU))

        @pl.when(k == nk - 1)
        def _():
            if col_shards and whole:
                for s in range(4):
                    o_ref[s] = acc[:, s * n:(s + 1) * n].astype(o_ref.dtype)
            else:
                o_ref[...] = acc[...].astype(o_ref.dtype)

    if col_shards:
        out_spec = (pl.BlockSpec((4, M, n), lambda j, k: (0, 0, 0)) if whole
                    else pl.BlockSpec((None, M, n), lambda j, k: (j, 0, 0)))
        out_shape = jax.ShapeDtypeStruct((4, M, n), _ACT)
    else:
        out_spec = pl.BlockSpec((M, tn), lambda j, k: (0, j))
        out_shape = jax.ShapeDtypeStruct((M, N), _ACT)
    outs, extra = _call(
        kern, comm, name=name, grid=(N // tn, nk),
        in_specs=[pl.BlockSpec((tk, M), lambda j, k: (k, 0)), pl.BlockSpec((tk, tn), lambda j, k: (k, j))],
        out_specs=[out_spec], out_shape=[out_shape], scratch_shapes=[pltpu.VMEM((M, tn), _F32)],
        args=(a, b), semantics=("parallel", "arbitrary"))
    return outs[0], extra


def _bias_tiles(rel):
    H = rel.shape[0]
    span = _TQ * _BAND_TILES - 1
    edge = span - _REL_CLIP
    gvec = jnp.concatenate([jnp.broadcast_to(rel[:, :1], (H, edge)), rel, jnp.broadcast_to(rel[:, -1:], (H, edge))], axis=1)
    width = _BIAS_TILES * _TQ
    period = width + _TQ
    tiled = jnp.broadcast_to(jnp.pad(gvec[:, ::-1], ((0, 0), (0, 1)))[:, None, :], (H, _TQ, period))
    rows = tiled.reshape(H, _TQ * period)[:, :_TQ * (period - 1)].reshape(H, _TQ, period - 1)[:, :, _TQ - 1:]
    r = jnp.arange(_TQ)[:, None]
    u = jnp.arange(width)[None, :]
    d = 4 * _TQ + r - u
    rm = r % _CHUNK
    valid = (d >= rm - (_CHUNK - 1)) & (d <= rm + 8 * _CHUNK)
    tiles = jnp.where(valid[None], rows, _MASKED)
    return tiles.reshape(H // 2, 2 * _TQ, _BIAS_TILES, _TQ).transpose(0, 2, 1, 3)


def _fold_bias_grad(db):
    H = 2 * db.shape[0]
    width = _BIAS_TILES * _TQ
    period = width + _TQ
    x = jnp.pad(db.transpose(0, 2, 1, 3).reshape(H, _TQ, width), ((0, 0), (0, 0), (_TQ - 1, 0)))
    skew = jnp.pad(x.reshape(H, _TQ * (period - 1)), ((0, 0), (0, _TQ))).reshape(H, _TQ, period)
    dg = skew.sum(axis=1)[:, :period - 1][:, ::-1]
    span = _TQ * _BAND_TILES - 1
    edge = span - _REL_CLIP
    mid = dg[:, edge:edge + 2 * _REL_CLIP + 1]
    lo = dg[:, :edge].sum(axis=1)
    hi = dg[:, edge + 2 * _REL_CLIP + 1:].sum(axis=1)
    return mid.at[:, 0].add(lo).at[:, -1].add(hi)


def _band_window(i):
    j0 = jnp.maximum(i - (_BAND_TILES - 1), 0)
    return j0, (_BAND_TILES - 1) - (i - j0)


def _head_masks():
    lane = lax.broadcasted_iota(jnp.int32, (1, _LANES), 1)
    return [(lane // _HEAD) == hh for hh in range(2)]


def _stack_heads(x, masks):
    return jnp.concatenate([jnp.where(m, x, jnp.zeros_like(x)) for m in masks], axis=0)


def _unstack_heads(y, masks):
    return jnp.where(masks[0], y[:_TQ], y[_TQ:])


def _scaled(q):
    return q * jnp.asarray(_HEAD ** -0.5, q.dtype)


def _band_probs(q2, k_ref, b_ref, j0, boff):
    s = []
    for j in range(_BAND_TILES):
        kj = k_ref[pl.ds(pl.multiple_of((j0 + j) * _TQ, _TQ), _TQ), :]
        s.append(_dot_nt(q2, kj) + b_ref[boff + j])
    m = jnp.max(functools.reduce(jnp.maximum, s), axis=-1, keepdims=True)
    p = [jnp.exp(x - m) for x in s]
    l = jnp.sum(functools.reduce(lambda a, b: a + b, p), axis=-1, keepdims=True)
    return p, 1.0 / l


def _qkv_specs(T, cb, npair, tq=_TQ):
    return [pl.BlockSpec((tq, _LANES), lambda h, i: (i, cb + h)),
            pl.BlockSpec((T, _LANES), lambda h, i: (0, cb + npair + h)),
            pl.BlockSpec((T, _LANES), lambda h, i: (0, cb + 2 * npair + h))]


def _attn_a_fwd(hq, bias, col0, width, layer, comm=None):
    T = hq.shape[0]
    npair = width // _LANES
    nsub = _BAND_SUBTILES
    tq = nsub * _TQ

    def kern(q_ref, k_ref, v_ref, b_ref, o_ref):
        masks = _head_masks()
        q = _scaled(q_ref[...])
        for s in range(nsub):
            part = slice(s * _TQ, (s + 1) * _TQ)
            j0, boff = _band_window(nsub * pl.program_id(1) + s)
            p, inv = _band_probs(_stack_heads(q[part], masks), k_ref, b_ref, j0, boff)
            o = jnp.zeros((2 * _TQ, _LANES), _F32)
            for j in range(_BAND_TILES):
                vj = v_ref[pl.ds(pl.multiple_of((j0 + j) * _TQ, _TQ), _TQ), :]
                o = o + _dot(p[j].astype(_MXU), vj)
            o_ref[part, :] = _unstack_heads(o * inv, masks).astype(o_ref.dtype)

    outs, extra = _call(
        kern, comm, name=f"band_attn_fwd_{layer}", grid=(npair, T // tq),
        in_specs=_qkv_specs(T, col0 // _LANES, npair, tq)
        + [pl.BlockSpec((None, _BIAS_TILES, 2 * _TQ, _TQ), lambda h, i: (h, 0, 0, 0))],
        out_specs=[pl.BlockSpec((tq, _LANES), lambda h, i: (i, h))],
        out_shape=[jax.ShapeDtypeStruct((T, width), _ACT)], scratch_shapes=[],
        args=(hq, hq, hq, bias), semantics=("arbitrary", "arbitrary"))
    return outs[0], extra


def _attn_a_bwd(hq, bias, do, col0, width, layer, comm=None):
    T = hq.shape[0]
    npair = width // _LANES
    nsub = _BAND_SUBTILES
    tq = nsub * _TQ
    nq = T // tq
    scale = _HEAD ** -0.5

    def kern(q_ref, k_ref, v_ref, b_ref, do_ref, dq_ref, dk_ref, dv_ref, db_ref, dk_acc, dv_acc):
        i = pl.program_id(1)

        @pl.when(i == 0)
        def _():
            dk_acc[...] = jnp.zeros_like(dk_acc)
            dv_acc[...] = jnp.zeros_like(dv_acc)
            db_ref[...] = jnp.zeros_like(db_ref)

        masks = _head_masks()
        q = _scaled(q_ref[...])
        do_t = do_ref[...]
        for s in range(nsub):
            part = slice(s * _TQ, (s + 1) * _TQ)
            j0, boff = _band_window(nsub * i + s)
            q2 = _stack_heads(q[part], masks)
            do2 = _stack_heads(do_t[part], masks).astype(_MXU)
            p, inv = _band_probs(q2, k_ref, b_ref, j0, boff)
            rows = [pl.ds(pl.multiple_of((j0 + j) * _TQ, _TQ), _TQ) for j in range(_BAND_TILES)]
            p = [x * inv for x in p]
            dp = [_dot_nt(do2, v_ref[rows[j], :]) for j in range(_BAND_TILES)]
            delta = jnp.sum(functools.reduce(lambda a, b: a + b, [p[j] * dp[j] for j in range(_BAND_TILES)]),
                            axis=-1, keepdims=True)
            dq = jnp.zeros((2 * _TQ, _LANES), _F32)
            for j in range(_BAND_TILES):
                ds = p[j] * (dp[j] - delta)
                db_ref[boff + j] += ds
                dsb = ds.astype(_MXU)
                dq = dq + _dot(dsb, k_ref[rows[j], :])
                dk_acc[rows[j], :] += _dot_tn(dsb, q2)
                dv_acc[rows[j], :] += _dot_tn(p[j].astype(_MXU), do2)
            dq_ref[part, :] = (_unstack_heads(dq, masks) * scale).astype(dq_ref.dtype)

        @pl.when(i == nq - 1)
        def _():
            dk_ref[...] = dk_acc[...].astype(dk_ref.dtype)
            dv_ref[...] = dv_acc[...].astype(dv_ref.dtype)

    strip = pl.BlockSpec((None, _BIAS_TILES, 2 * _TQ, _TQ), lambda h, i: (h, 0, 0, 0))
    tile = pl.BlockSpec((tq, _LANES), lambda h, i: (i, h))
    column = pl.BlockSpec((T, _LANES), lambda h, i: (0, h))
    outs, extra = _call(
        kern, comm, name=f"band_attn_bwd_{layer}", grid=(npair, nq),
        in_specs=_qkv_specs(T, col0 // _LANES, npair, tq) + [strip, tile],
        out_specs=[tile, column, column, strip],
        out_shape=[jax.ShapeDtypeStruct((T, width), _ACT)] * 3
        + [jax.ShapeDtypeStruct((npair, _BIAS_TILES, 2 * _TQ, _TQ), _F32)],
        scratch_shapes=[pltpu.VMEM((T, _LANES), _F32), pltpu.VMEM((T, _LANES), _F32)],
        args=(hq, hq, hq, bias, do), semantics=("arbitrary", "arbitrary"))
    return outs, extra


def _suffix_matrix():
    r = lax.broadcasted_iota(jnp.int32, (_TQ, _TQ), 0)
    c = lax.broadcasted_iota(jnp.int32, (_TQ, _TQ), 1)
    r2 = lax.broadcasted_iota(jnp.int32, (2 * _TQ, _TQ), 0)
    c2 = lax.broadcasted_iota(jnp.int32, (2 * _TQ, _TQ), 1)
    return (r > c).astype(_MXU), c2 - (r2 & (_TQ - 1))


def _suffix_sums(xs, tri):
    n, k = xs[0].shape[0], len(xs)
    his = [x.astype(_MXU) for x in xs]
    los = [(x - h.astype(_F32)).astype(_MXU) for x, h in zip(xs, his)]
    y = _dot(jnp.concatenate(his + los, axis=0), tri)
    return [y[j * n:(j + 1) * n] + y[(k + j) * n:(k + j + 1) * n] for j in range(k)]


def _stick_tiles(tiles, rel, carry_l, tri):
    zs = [_dot_nt(qs, kj) for qs, kj, _, _ in tiles]
    Ls, masks = [], []
    for z, (_, _, jj, _) in zip(zs, tiles):
        nsp = -(jnp.maximum(z, 0.0) + jnp.log(1.0 + jnp.exp(-jnp.abs(z))))
        if isinstance(jj, int):
            mask = (rel < 0) if jj == 0 else None
        else:
            mask = rel < jnp.where(jj == 0, 0, _TQ)
        Ls.append(nsp if mask is None else jnp.where(mask, nsp, 0.0))
        masks.append(mask)
    carry_l = list(carry_l)
    ws = []
    for z, L, suffix, mask, (_, _, _, sub) in zip(zs, Ls, _suffix_sums(Ls, tri), masks, tiles):
        w = jnp.exp(z + L + suffix + carry_l[sub])
        ws.append(w if mask is None else jnp.where(mask, w, 0.0))
        carry_l[sub] = carry_l[sub] + jnp.sum(L, axis=-1, keepdims=True)
    return zs, Ls, ws, masks, carry_l


def _sweep(i, step, zero):
    nsub = _SB_SUBTILES

    def window():
        tiles = [(s, jj) for jj in range(_SB_WINDOW) for s in range(nsub)]
        return tuple((jnp.int32(_SB_WINDOW),) + c for c in step(tiles, [zero] * nsub))

    start = lax.cond(i >= -(-(_SB_WINDOW - 1) // nsub), window, lambda: tuple((jnp.int32(0),) + zero for _ in range(nsub)))
    outs = []
    for s in range(nsub):
        def done(c, s=s):
            return jnp.logical_or(c[0] > nsub * i + s, jnp.max(c[1]) < _EXP_ZERO_BELOW)

        def more(c, s=s):
            carries = [None] * nsub
            carries[s] = c[1:]
            return (c[0] + 1,) + step([(s, c[0])], carries)[s]

        outs.append(lax.while_loop(lambda c, done=done: jnp.logical_not(done(c)), more, start[s]))
    return outs


def _sb_fwd(hq, col0, width, layer, comm=None):
    T = hq.shape[0]
    npair = width // _LANES
    nsub = _SB_SUBTILES
    tq = nsub * _TQ

    def kern(q_ref, k_ref, v_ref, o_ref):
        i = pl.program_id(1)
        masks = _head_masks()
        tri, rel = _suffix_matrix()
        q = _scaled(q_ref[...])
        q2 = [_stack_heads(q[s * _TQ:(s + 1) * _TQ], masks) for s in range(nsub)]

        def step(tiles, carries):
            rows = [pl.ds(pl.multiple_of((nsub * i + s - jj) * _TQ, _TQ), _TQ) for s, jj in tiles]
            cls = [None if c is None else c[0] for c in carries]
            accs = [None if c is None else c[1] for c in carries]
            _, _, ws, _, cls = _stick_tiles([(q2[s], k_ref[r, :], jj, s) for (s, jj), r in zip(tiles, rows)], rel, cls, tri)
            for w, r, (s, _) in zip(ws, rows, tiles):
                accs[s] = accs[s] + _dot(w.astype(_MXU), v_ref[r, :])
            return [None if c is None else (cls[s], accs[s]) for s, c in enumerate(carries)]

        outs = _sweep(i, step, (jnp.zeros((2 * _TQ, 1), _F32), jnp.zeros((2 * _TQ, _LANES), _F32)))
        for s in range(nsub):
            o_ref[s * _TQ:(s + 1) * _TQ, :] = _unstack_heads(outs[s][2], masks)

    outs, extra = _call(
        kern, comm, name=f"stick_attn_fwd_{layer}", grid=(npair, T // tq),
        in_specs=_qkv_specs(T, col0 // _LANES, npair, tq),
        out_specs=[pl.BlockSpec((tq, _LANES), lambda h, i: (i, h))],
        out_shape=[jax.ShapeDtypeStruct((T, width), _F32)], scratch_shapes=[],
        args=(hq, hq, hq), semantics=("arbitrary", "arbitrary"))
    return outs[0], extra


def _sb_bwd(hq, o, do, col0, width, layer, comm=None):
    T = hq.shape[0]
    npair = width // _LANES
    nsub = _SB_SUBTILES
    tq = nsub * _TQ
    nq = T // tq
    scale = _HEAD ** -0.5

    def kern(q_ref, k_ref, v_ref, o_ref, do_ref, dq_ref, dk_ref, dv_ref, dk_acc, dv_acc):
        i = pl.program_id(1)

        @pl.when(i == 0)
        def _():
            dk_acc[...] = jnp.zeros_like(dk_acc)
            dv_acc[...] = jnp.zeros_like(dv_acc)

        masks = _head_masks()
        tri, rel = _suffix_matrix()
        q = _scaled(q_ref[...])
        do_t = do_ref[...]
        prod = do_t.astype(_F32) * o_ref[...]
        part = [slice(s * _TQ, (s + 1) * _TQ) for s in range(nsub)]
        q2 = [_stack_heads(q[p], masks) for p in part]
        do2 = [_stack_heads(do_t[p], masks).astype(_MXU) for p in part]
        dsum = [jnp.sum(_stack_heads(prod[p], masks), axis=-1, keepdims=True) for p in part]

        def step(tiles, carries):
            rows = [pl.ds(pl.multiple_of((nsub * i + s - jj) * _TQ, _TQ), _TQ) for s, jj in tiles]
            kjs = [k_ref[r, :] for r in rows]
            cls, cgs, dqs = ([None if c is None else c[n] for c in carries] for n in range(3))
            zs, Ls, ws, tile_masks, cls = _stick_tiles([(q2[s], kj, jj, s) for (s, jj), kj in zip(tiles, kjs)], rel, cls, tri)
            wbs = [w.astype(_MXU) for w in ws]
            gs = [wb.astype(_F32) * _dot_nt(do2[s], v_ref[r, :]) for wb, r, (s, _) in zip(wbs, rows, tiles)]
            for z, L, g, later, mask, wb, kj, r, (s, _) in zip(zs, Ls, gs, _suffix_sums(gs, tri), tile_masks, wbs, kjs,
                                                               rows, tiles):
                dz = g - jnp.exp(z + L) * (dsum[s] - (later + cgs[s]))
                if mask is not None:
                    dz = jnp.where(mask, dz, 0.0)
                dzb = dz.astype(_MXU)
                dk_acc[r, :] += _dot_tn(dzb, q2[s])
                dv_acc[r, :] += _dot_tn(wb, do2[s])
                dqs[s] = dqs[s] + _dot(dzb, kj)
                cgs[s] = cgs[s] + jnp.sum(g, axis=-1, keepdims=True)
            return [None if c is None else (cls[s], cgs[s], dqs[s]) for s, c in enumerate(carries)]

        zc = jnp.zeros((2 * _TQ, 1), _F32)
        outs = _sweep(i, step, (zc, zc, jnp.zeros((2 * _TQ, _LANES), _F32)))
        for s in range(nsub):
            dq_ref[part[s], :] = (_unstack_heads(outs[s][3], masks) * scale).astype(dq_ref.dtype)

        @pl.when(i == nq - 1)
        def _():
            dk_ref[...] = dk_acc[...].astype(dk_ref.dtype)
            dv_ref[...] = dv_acc[...].astype(dv_ref.dtype)

    tile_spec = pl.BlockSpec((tq, _LANES), lambda h, i: (i, h))
    column = pl.BlockSpec((T, _LANES), lambda h, i: (0, h))
    outs, extra = _call(
        kern, comm, name=f"stick_attn_bwd_{layer}", grid=(npair, nq),
        in_specs=_qkv_specs(T, col0 // _LANES, npair, tq) + [tile_spec, tile_spec],
        out_specs=[tile_spec, column, column],
        out_shape=[jax.ShapeDtypeStruct((T, width), _ACT)] * 3,
        scratch_shapes=[pltpu.VMEM((T, _LANES), _F32), pltpu.VMEM((T, _LANES), _F32)],
        args=(hq, hq, hq, o, do), semantics=("arbitrary", "arbitrary"))
    return outs, extra


_DENSE = ("w_in", "w_proj_a", "w_proj_b", "w_out", "w_ffn_in", "w_ffn_out")
_COL_SHARDED = {"w_in": True, "w_proj_a": True, "w_proj_b": True, "w_out": False, "w_ffn_in": True, "w_ffn_out": False}
_SMALL = ("b_gate", "rel_bias", "ln1_g", "ln1_b", "ln2_g", "ln2_b")


class _Plans:
    def __init__(self, plans=None):
        self.plans = plans or {}

    def start(self, key):
        if key not in self.plans:
            return None, None
        return self.plans[key]()

    @staticmethod
    def finish(done, extra):
        if done is not None:
            done(extra)


def _layer_fwd(x, W, small, l, alpha, plans):
    WA = small["rel_bias"].shape[1] * _HEAD
    row = lambda v: v[l].reshape(1, -1)
    hq, hg, xb = _in_proj(x, W["w_in"], l)
    WB = (hq.shape[1] - 3 * WA) // 3
    bias = _bias_tiles(small["rel_bias"][l])
    comm, done = plans.start(f"band_fwd_{l}")
    oa, extra = _attn_a_fwd(hq, bias, 0, WA, l, comm)
    plans.finish(done, extra)
    comm, done = plans.start(f"stick_fwd_{l}")
    ob, extra = _sb_fwd(hq, 3 * WA, WB, l, comm)
    plans.finish(done, extra)
    comm, done = plans.start(f"mix_fwd_{l}")
    (x1, u1, pre, ya, yb), extra = _mix_fwd(oa, ob, hg, x, W["w_proj_a"], W["w_proj_b"], W["w_out"], row(small["b_gate"]),
                                            row(small["ln1_g"]), row(small["ln1_b"]), alpha, l, comm)
    plans.finish(done, extra)
    comm, done = plans.start(f"ffn_fwd_{l}")
    (x2, u2, act, gu, x1b), extra = _ffn_fwd(x1, W["w_ffn_in"], W["w_ffn_out"], row(small["ln2_g"]),
                                             row(small["ln2_b"]), alpha, l, comm)
    plans.finish(done, extra)
    return x2, dict(xb=xb, hq=hq, hg=hg, bias=bias, oa=oa, ob=ob, x1b=x1b, u1=u1, pre=pre, ya=ya, yb=yb, u2=u2,
                    act=act, gu=gu)


def _layer_bwd(dy_or_target, S, W, small, l, last, alpha, plans, gw):
    D = S["xb"].shape[1]
    WA, WB = S["oa"].shape[1], S["ob"].shape[1]
    row = lambda v: v[l].reshape(1, -1)

    def blocks(g, n):
        return g if _COL_SHARDED[n] else g.reshape(4, g.shape[0] // 4, g.shape[1])

    dx1, du2b, dgu, st2 = _ffn_bwd(S["u2"], dy_or_target, S["gu"], row(small["ln2_g"]), row(small["ln2_b"]),
                                   W["w_ffn_in"], W["w_ffn_out"], alpha, l, last)
    gw["w_ffn_in"] = blocks(_grad_w(S["x1b"], dgu, col_shards=True, name=f"grad_w_ffn_in_{l}")[0], "w_ffn_in")
    gw["w_ffn_out"] = blocks(_grad_w(S["act"], du2b, col_shards=False, name=f"grad_w_ffn_out_{l}")[0], "w_ffn_out")
    du1, du1b, dya, dyb, dhg, doa, dob, st1 = _mix_bwd(S["u1"], dx1, S["ya"], S["yb"], S["hg"], W["w_proj_a"],
                                                       W["w_proj_b"], W["w_out"], row(small["b_gate"]),
                                                       row(small["ln1_g"]), l)
    gw["w_out"] = blocks(_grad_w(S["pre"], du1b, col_shards=False, name=f"grad_w_out_{l}")[0], "w_out")
    gw["w_proj_a"] = blocks(_grad_w(S["oa"], dya, col_shards=True, name=f"grad_w_proj_a_{l}")[0], "w_proj_a")
    gw["w_proj_b"] = blocks(_grad_w(S["ob"], dyb, col_shards=True, name=f"grad_w_proj_b_{l}")[0], "w_proj_b")
    comm, done = plans.start(f"band_bwd_{l}")
    (dqa, dka, dva, dbias), extra = _attn_a_bwd(S["hq"], S["bias"], doa, 0, WA, l, comm)
    plans.finish(done, extra)
    comm, done = plans.start(f"stick_bwd_{l}")
    (dqb, dkb, dvb), extra = _sb_bwd(S["hq"], S["ob"], dob, 3 * WA, WB, l, comm)
    plans.finish(done, extra)
    dh = jnp.concatenate([dqa, dka, dva, dqb, dkb, dvb, dhg], axis=1)
    comm, done = plans.start(f"grad_w_in_{l}")
    g, extra = _grad_w(S["xb"], dh, col_shards=True, name=f"grad_w_in_{l}", comm=comm)
    gw["w_in"] = blocks(g, "w_in")
    plans.finish(done, extra)
    comm, done = plans.start(f"in_proj_bwd_{l}")
    dx, extra = _residual_nt(du1, alpha, dh, W["w_in"], f"in_proj_bwd_{l}", comm)
    plans.finish(done, extra)
    gs = dict(b_gate=st1[0], rel_bias=_fold_bias_grad(dbias), ln1_g=st1[1, :D], ln1_b=st1[1, D:],
              ln2_g=st2[0], ln2_b=st2[1])
    return dx, gs, st2[2]


def _local_step(x, target, W, small, plans=None, gws=None):
    depth = len(W)
    alpha = float((2 * depth) ** 0.25)
    plans = plans or _Plans()
    gws = gws if gws is not None else [dict() for _ in range(depth)]
    saved = []
    h = x
    for l in range(depth):
        h, S = _layer_fwd(h, W[l], small, l, alpha, plans)
        saved.append(S)
    gss = [None] * depth
    d = target
    sq = None
    for l in reversed(range(depth)):
        d, gss[l], sq_l = _layer_bwd(d, saved[l], W[l], small, l, l == depth - 1, alpha, plans, gws[l])
        if l == depth - 1:
            sq = sq_l
    return sq, d, gws, gss


def _place():
    return lax.axis_index("x"), lax.axis_index("y"), lax.axis_index("c")


def _remote(src, dst, send_sem, recv_sem, to):
    return pltpu.make_async_remote_copy(src_ref=src, dst_ref=dst, send_sem=send_sem, recv_sem=recv_sem,
                                        device_id=to, device_id_type=_MESH)


def _half(ref, hc):
    kh = ref.shape[0] // 2
    return ref.at[pl.ds(pl.multiple_of(hc * kh, 16), kh), :]


def _gather_plan(blocks, fractions):
    nt = len(blocks)

    def run(step, nsteps, ins, outs, sems):
        send_sems, recv_sems, loc_sems = sems
        x, y, c = _place()
        k = 2 * x + y
        me, sibling = (x, y, c), (x, y, 1 - c)
        chips = [(1 - x, y), (x, 1 - y), (1 - x, 1 - y)]
        chip_k = [2 * cx + cy for cx, cy in chips]

        def ici(t, s, owner_k, to, src=None):
            dst = _half(outs[t].at[owner_k], c)
            return _remote(dst if src is None else src, dst, send_sems.at[t, s], recv_sems.at[t, s], to)

        def passed(t, s, hc, to):
            blk = _half(outs[t].at[chip_k[s]], hc)
            return _remote(blk, blk, send_sems.at[t, 3 + s], recv_sems.at[t, 3 + s], to)

        def local(t):
            return pltpu.make_async_copy(ins[t], outs[t].at[k], loc_sems.at[t])

        @pl.when(step == 0)
        def _():
            for t in range(nt):
                local(t).start()
                for s, chip in enumerate(chips):
                    ici(t, s, k, (*chip, c), src=_half(ins[t], c)).start()

        for t in range(nt):
            @pl.when(step == min(nsteps - 1, int(fractions[t] * nsteps)))
            def _():
                for s in range(3):
                    ici(t, s, chip_k[s], me).wait_recv()
                    passed(t, s, c, sibling).start()

        @pl.when(step == nsteps - 1)
        def _():
            for t in range(nt):
                for s, chip in enumerate(chips):
                    passed(t, s, 1 - c, me).wait_recv()
            for t in range(nt):
                for s, chip in enumerate(chips):
                    ici(t, s, k, (*chip, c), src=_half(ins[t], c)).wait_send()
                    passed(t, s, c, sibling).wait_send()
                local(t).wait()

    return _Comm(blocks, [jax.ShapeDtypeStruct((4,) + b.shape, b.dtype) for b in blocks],
                 [pltpu.SemaphoreType.DMA((nt, 6)), pltpu.SemaphoreType.DMA((nt, 6)), pltpu.SemaphoreType.DMA((nt,))], run)


def _scatter_plan(grads, owners):
    nt = len(grads)

    def run(step, nsteps, ins, outs, sems):
        send_sems, recv_sems, loc_sems = sems
        x, y, c = _place()
        me = 4 * x + 2 * y + c

        def target(r):
            tx = 1 - x if r & 2 else x
            ty = 1 - y if r & 1 else y
            return tx, ty

        def send(t, r):
            tx, ty = target(r)
            return _remote(ins[t].at[2 * tx + ty], outs[t].at[me], send_sems.at[t, r], recv_sems.at[t, 2 * r + c],
                           (tx, ty, owners[t]))

        def local(t):
            return pltpu.make_async_copy(ins[t].at[2 * x + y], outs[t].at[me], loc_sems.at[t])

        @pl.when(step == 0)
        def _():
            for t in range(nt):
                @pl.when(c == owners[t])
                def _():
                    local(t).start()

                @pl.when(c != owners[t])
                def _():
                    send(t, 0).start()

                for r in range(1, 4):
                    send(t, r).start()

        @pl.when(step == nsteps - 1)
        def _():
            for t in range(nt):
                @pl.when(c == owners[t])
                def _():
                    for r in range(4):
                        sx, sy = target(r)
                        for cs in range(2):
                            if r == 0 and cs == owners[t]:
                                continue
                            src_dev = 4 * sx + 2 * sy + cs
                            _remote(ins[t].at[0], outs[t].at[src_dev], send_sems.at[t, r], recv_sems.at[t, 2 * r + cs],
                                    (x, y, c)).wait_recv()
                    local(t).wait()

                @pl.when(c != owners[t])
                def _():
                    send(t, 0).wait_send()

                for r in range(1, 4):
                    send(t, r).wait_send()

    return _Comm(grads, [jax.ShapeDtypeStruct((8,) + g.shape[1:], g.dtype) for g in grads],
                 [pltpu.SemaphoreType.DMA((nt, 4)), pltpu.SemaphoreType.DMA((nt, 8)), pltpu.SemaphoreType.DMA((nt,))], run)


def _share_plan(reduced, owners):
    nt = len(reduced)

    def run(step, nsteps, ins, outs, sems):
        del ins
        send_sems, recv_sems = sems
        x, y, c = _place()

        def give(t, to):
            return _remote(outs[t], outs[t], send_sems.at[t], recv_sems.at[t], to)

        @pl.when(step == 0)
        def _():
            for t in range(nt):
                @pl.when(c == owners[t])
                def _():
                    give(t, (x, y, 1 - c)).start()

        @pl.when(step == nsteps - 1)
        def _():
            for t in range(nt):
                @pl.when(c == owners[t])
                def _():
                    give(t, (x, y, 1 - c)).wait_send()

                @pl.when(c != owners[t])
                def _():
                    give(t, (x, y, c)).wait_recv()

    return _Comm(reduced, [jax.ShapeDtypeStruct(r.shape, r.dtype) for r in reduced],
                 [pltpu.SemaphoreType.DMA((nt,)), pltpu.SemaphoreType.DMA((nt,))], run,
                 aliases={t: t for t in range(nt)})


def _join(a, b):
    ni, no, ns = len(a.inputs), len(a.out_shapes), len(a.sems)

    def run(step, nsteps, ins, outs, sems):
        a.run(step, nsteps, ins[:ni], outs[:no], sems[:ns])
        b.run(step, nsteps, ins[ni:], outs[no:], sems[ns:])

    aliases = dict(a.aliases)
    aliases.update({ni + i: no + o for i, o in b.aliases.items()})
    return _Comm(a.inputs + b.inputs, a.out_shapes + b.out_shapes, a.sems + b.sems, run, aliases)


def _peer(x, y, c, r):
    px = 1 - x if r & 4 else x
    py = 1 - y if r & 2 else y
    pc = 1 - c if r & 1 else c
    return (px, py, pc), 4 * px + 2 * py + pc


def _sum_slots(st, name):
    _, K, n = st.shape
    tr = next(t for t in (256, 128, 64, 32, 16) if K % t == 0)

    def kern(s_ref, o_ref):
        acc = s_ref[0].astype(_F32)
        for d in range(1, 8):
            acc = acc + s_ref[d].astype(_F32)
        o_ref[...] = acc.astype(o_ref.dtype)

    return pl.pallas_call(
        kern, name=name, grid=(K // tr,),
        in_specs=[pl.BlockSpec((8, tr, n), lambda i: (0, i, 0))], out_specs=_rows(tr, n),
        out_shape=jax.ShapeDtypeStruct((K, n), _ACT),
        compiler_params=_cparams("parallel"),
    )(st)


def _all_reduce_small(p):
    R = p.shape[0]

    def body(p_ref, o_ref, stage, send_sems, recv_sems):
        x, y, c = _place()
        me = 4 * x + 2 * y + c
        stage[me] = p_ref[...]
        sent = []
        for r in range(1, 8):
            to, _ = _peer(x, y, c, r)
            cp = _remote(p_ref, stage.at[me], send_sems.at[r - 1], recv_sems.at[r - 1], to)
            cp.start()
            sent.append(cp)
        for r in range(1, 8):
            _, src_dev = _peer(x, y, c, r)
            _remote(p_ref, stage.at[src_dev], send_sems.at[r - 1], recv_sems.at[r - 1], (x, y, c)).wait_recv()
        acc = stage[0]
        for d in range(1, 8):
            acc = acc + stage[d]
        o_ref[...] = acc
        for cp in sent:
            cp.wait_send()

    vm = pl.BlockSpec(memory_space=pltpu.VMEM)
    return pl.pallas_call(
        body, name="all_reduce_small",
        in_specs=[vm], out_specs=vm,
        out_shape=jax.ShapeDtypeStruct((R, _LANES), _F32),
        scratch_shapes=[pltpu.VMEM((8, R, _LANES), _F32), pltpu.SemaphoreType.DMA((7,)), pltpu.SemaphoreType.DMA((7,))],
    )(p)


def _adamw_update(gv, w_ref, m_ref, v_ref, gf_ref, d_ref, nm_ref, nv_ref):
    nm = _B1 * m_ref[...] + (1.0 - _B1) * gv
    nv = _B2 * v_ref[...] + (1.0 - _B2) * (gv * gv)
    m_hat = nm / (1.0 - _B1 ** _STEP)
    v_hat = nv / (1.0 - _B2 ** _STEP)
    gf_ref[...] = gv
    d_ref[...] = -_LR * (m_hat / (jnp.sqrt(v_hat) + _EPS) + _WD * w_ref[...])
    nm_ref[...] = nm
    nv_ref[...] = nv


def _adamw_layers(w, g_layers, m, v, name):
    _, K, n = w.shape
    tr = next(t for t in (256, 128, 64, 32, 16) if K % t == 0)

    def kern(w_ref, g0_ref, g1_ref, m_ref, v_ref, *out_refs):
        first = pl.program_id(0) == 0
        gv = jnp.where(first, g0_ref[...].astype(_F32), g1_ref[...].astype(_F32))
        _adamw_update(gv, w_ref, m_ref, v_ref, *out_refs)

    stacked = pl.BlockSpec((None, tr, n), lambda l, i: (l, i, 0))
    layer = pl.BlockSpec((tr, n), lambda l, i: (i, 0))
    return tuple(pl.pallas_call(
        kern, name=name, grid=(2, K // tr),
        in_specs=[stacked, layer, layer, stacked, stacked], out_specs=[stacked] * 4,
        out_shape=[jax.ShapeDtypeStruct(w.shape, _F32)] * 4,
        compiler_params=_cparams("parallel", "parallel"),
    )(w, g_layers[0], g_layers[1], m, v))


def _adamw(w, g, m, v, name):
    shape = w.shape
    w2, g2, m2, v2 = (a.reshape(-1, shape[-1]) for a in (w, g, m, v))
    R, C = w2.shape
    tr = next((t for t in (256, 128, 64, 32, 16) if R % t == 0), R)

    def kern(w_ref, g_ref, m_ref, v_ref, *out_refs):
        _adamw_update(g_ref[...].astype(_F32), w_ref, m_ref, v_ref, *out_refs)

    outs = pl.pallas_call(
        kern, name=name, grid=(R // tr,),
        in_specs=[_rows(tr, C)] * 4, out_specs=[_rows(tr, C)] * 4,
        out_shape=[jax.ShapeDtypeStruct((R, C), _F32)] * 4,
        compiler_params=_cparams("parallel"),
    )(w2, g2, m2, v2)
    return tuple(o.reshape(shape) for o in outs)


def _pack_small(gss, sq):
    parts = [gss[l][n].reshape(-1) for n in _SMALL for l in range(len(gss))] + [jnp.sum(sq).reshape(1)]
    flat = jnp.concatenate(parts)
    rows = -(-flat.shape[0] // (8 * _LANES)) * 8
    return jnp.pad(flat, (0, rows * _LANES - flat.shape[0])).reshape(rows, _LANES)


def _unpack_small(total, shapes):
    flat = total.reshape(-1)
    out, off = {}, 0
    for n in _SMALL:
        layers = []
        for _ in range(shapes[n][0]):
            size = 1
            for s in shapes[n][1:]:
                size *= s
            layers.append(flat[off:off + size].reshape(shapes[n][1:]))
            off += size
        out[n] = jnp.stack(layers)
    return out, flat[off]


_GATHER = {
    "band_fwd_0": [(0, "w_proj_a"), (0, "w_proj_b"), (0, "w_out"), (0, "w_ffn_out")],
    "stick_fwd_0": [(0, "w_ffn_in")],
    "mix_fwd_0": [(1, "w_in")],
    "ffn_fwd_0": [(1, "w_proj_a"), (1, "w_proj_b"), (1, "w_out"), (1, "w_ffn_in"), (1, "w_ffn_out")],
}
_SCATTER = {
    "band_bwd_1": [(1, "w_ffn_in"), (1, "w_ffn_out")],
    "stick_bwd_1": [(1, "w_proj_a"), (1, "w_proj_b"), (1, "w_out")],
    "band_bwd_0": [(1, "w_in"), (0, "w_ffn_in")],
    "stick_bwd_0": [(0, "w_ffn_out"), (0, "w_proj_a"), (0, "w_proj_b"), (0, "w_out")],
    "in_proj_bwd_0": [(0, "w_in")],
}
_SHARE = {"stick_bwd_1": "band_bwd_1", "band_bwd_0": "stick_bwd_1", "stick_bwd_0": "band_bwd_0", "grad_w_in_0": "stick_bwd_0"}


def _owner(key):
    del key
    return 1


def kernel(x, w_in, b_gate, rel_bias, w_proj_a, w_proj_b, w_out, ln1_g, ln1_b, w_ffn_in, w_ffn_out, ln2_g, ln2_b, loss_target, m_w_in, m_b_gate, m_rel_bias, m_w_proj_a, m_w_proj_b, m_w_out, m_ln1_g, m_ln1_b, m_w_ffn_in, m_w_ffn_out, m_ln2_g, m_ln2_b, v_w_in, v_b_gate, v_rel_bias, v_w_proj_a, v_w_proj_b, v_w_out, v_ln1_g, v_ln1_b, v_w_ffn_in, v_w_ffn_out, v_ln2_g, v_ln2_b):
    names = ("w_in", "b_gate", "rel_bias", "w_proj_a", "w_proj_b", "w_out", "ln1_g", "ln1_b", "w_ffn_in", "w_ffn_out", "ln2_g", "ln2_b")
    w = dict(zip(names, (w_in, b_gate, rel_bias, w_proj_a, w_proj_b, w_out, ln1_g, ln1_b, w_ffn_in, w_ffn_out, ln2_g, ln2_b)))
    m = dict(zip(names, (m_w_in, m_b_gate, m_rel_bias, m_w_proj_a, m_w_proj_b, m_w_out, m_ln1_g, m_ln1_b, m_w_ffn_in, m_w_ffn_out, m_ln2_g, m_ln2_b)))
    v = dict(zip(names, (v_w_in, v_b_gate, v_rel_bias, v_w_proj_a, v_w_proj_b, v_w_out, v_ln1_g, v_ln1_b, v_w_ffn_in, v_w_ffn_out, v_ln2_g, v_ln2_b)))
    T, D = x.shape[-2], x.shape[-1]
    assert w_in.shape[0] == 2, "the exchange schedule below is written for two layers"

    mine = [{n: w[n][l].astype(_MXU) for n in _DENSE} for l in range(2)]
    W = [dict(), dict()]
    gws = [dict(), dict()]
    slots, final = {}, {}

    def gather(keys):
        sizes = [mine[l][n].size for l, n in keys]
        passed, fractions = 0, []
        for s in sizes:
            passed += s
            fractions.append(0.15 + 0.6 * passed / sum(sizes))

        def done(outs):
            for (l, n), o in zip(keys, outs):
                W[l][n] = o
        return _gather_plan([mine[l][n] for l, n in keys], fractions), done

    def scatter(keys):
        comm = _scatter_plan([gws[l][n] for l, n in keys], [_owner(key) for key in keys])
        return comm, lambda outs: slots.update(zip(keys, outs))

    def share(keys):
        reduced = [_sum_slots(slots[key], f"sum_grad_{key[1]}_{key[0]}") for key in keys]
        comm = _share_plan(reduced, [_owner(key) for key in keys])
        return comm, lambda outs: final.update(zip(keys, outs))

    def both(first, second):
        (ca, da), (cb, db) = first, second
        na = len(ca.out_shapes)
        return _join(ca, cb), lambda outs: (da(outs[:na]), db(outs[na:]))

    comm, done = gather([(0, "w_in")])
    done(_comm_only(comm, "gather_first"))
    plans = {key: functools.partial(gather, keys) for key, keys in _GATHER.items()}
    for key, keys in _SCATTER.items():
        plans[key] = functools.partial(scatter, keys)
    for key, scattered_under in _SHARE.items():
        handed = functools.partial(share, _SCATTER[scattered_under])
        carried = plans.get(key)
        plans[key] = handed if carried is None else (lambda carried=carried, handed=handed: both(carried(), handed()))
    small = {n: w[n] for n in _SMALL}
    sq, dx, _, gss = _local_step(x.reshape(T, D), loss_target.reshape(T, D), W, small, _Plans(plans), gws)

    comm, done = share(_SCATTER["in_proj_bwd_0"])
    done(_comm_only(comm, "share_last"))
    total = _all_reduce_small(_pack_small(gss, sq))
    small_grads, sq_all = _unpack_small(total, {n: w[n].shape for n in _SMALL})
    loss = 0.5 * sq_all / D

    grad, delta, new_m, new_v = {}, {}, {}, {}
    for n in names:
        if n in _DENSE:
            updated = _adamw_layers(w[n], [final[(l, n)] for l in range(2)], m[n], v[n], f"adamw_{n}")
        else:
            updated = _adamw(w[n], small_grads[n], m[n], v[n], f"adamw_{n}")
        grad[n], delta[n], new_m[n], new_v[n] = updated
    return (loss, dx.reshape(x.shape), *[grad[n] for n in names], *[delta[n] for n in names],
            *[new_m[n] for n in names], *[new_v[n] for n in names])
```

```python
import functools

import jax
import jax.numpy as jnp
from jax import lax
from jax.experimental import pallas as pl
from jax.experimental.pallas import tpu as pltpu

_MXU = jnp.bfloat16
_ACT = jnp.bfloat16
_F32 = jnp.float32

_HEAD = 64
_CHUNK = 64
_LANES = 128
_TQ = 128
_BAND_TILES = 5
_BIAS_TILES = 9
_REL_CLIP = 256
_LN_EPS = 1e-5
_MASKED = -1e30
_EXP_ZERO_BELOW = -87.34
_SB_WINDOW = 2
_SB_SUBTILES = 4
_BAND_SUBTILES = 8
_VMEM_LIMIT = 56 * 1024 * 1024
_GRAD_ACC_BYTES = 12 * 1024 * 1024

_LR, _B1, _B2, _EPS, _WD, _STEP = 0.001, 0.9, 0.999, 1e-08, 0.01, 10

_MESH = pl.DeviceIdType.MESH


def _dot(a, b):
    return jnp.dot(a, b, preferred_element_type=_F32)


def _dot_nt(a, b):
    return lax.dot_general(a, b, (((1,), (1,)), ((), ())), preferred_element_type=_F32)


def _dot_tn(a, b):
    return lax.dot_general(a, b, (((0,), (0,)), ((), ())), preferred_element_type=_F32)


def _cparams(*sem):
    return pltpu.CompilerParams(dimension_semantics=sem, vmem_limit_bytes=_VMEM_LIMIT)


def _rows(t, c):
    return pl.BlockSpec((t, c), lambda i: (i, 0))


def _whole(shape):
    return pl.BlockSpec(shape, lambda i: tuple(0 for _ in shape))


_ANY = pl.BlockSpec(memory_space=pl.ANY)


def _load_cols(w_hbm, w_vmem, sem):
    n = w_hbm.shape[-1]
    cps = [pltpu.make_async_copy(w_hbm.at[k], w_vmem.at[:, pl.ds(k * n, n)], sem.at[k]) for k in range(4)]
    for cp in cps:
        cp.start()
    for cp in cps:
        cp.wait()


def _load_rows(w_hbm, w_vmem, sem):
    r = w_hbm.shape[-2]
    cps = [pltpu.make_async_copy(w_hbm.at[k], w_vmem.at[pl.ds(k * r, r), :], sem.at[k]) for k in range(4)]
    for cp in cps:
        cp.start()
    for cp in cps:
        cp.wait()


def _ln_stats(u):
    mu = jnp.mean(u, axis=-1, keepdims=True)
    xc = u - mu
    var = jnp.mean(xc * xc, axis=-1, keepdims=True)
    rstd = lax.rsqrt(var + _LN_EPS)
    return xc * rstd, rstd


def _ln_bwd(u, dy, gamma):
    xhat, rstd = _ln_stats(u)
    dxh = dy * gamma
    m1 = jnp.mean(dxh, axis=-1, keepdims=True)
    m2 = jnp.mean(dxh * xhat, axis=-1, keepdims=True)
    du = rstd * (dxh - m1 - xhat * m2)
    return du, jnp.sum(dy * xhat, axis=0, keepdims=True), jnp.sum(dy, axis=0, keepdims=True), xhat


def _divisor_tile(n, cap):
    best = None
    for t in range(_LANES, min(n, cap) + 1, _LANES):
        if n % t == 0:
            best = t
    return best or n


class _Comm:
    def __init__(self, inputs, out_shapes, sems, run, aliases=None):
        self.inputs, self.out_shapes, self.sems, self.run = list(inputs), list(out_shapes), list(sems), run
        self.aliases = aliases or {}


def _call(kern, comm, *, name, grid, in_specs, out_specs, out_shape, scratch_shapes, args, semantics):
    in_specs, out_specs, out_shape, scratch_shapes = list(in_specs), list(out_specs), list(out_shape), list(scratch_shapes)
    if comm is None:
        outs = pl.pallas_call(kern, name=name, grid=grid, in_specs=in_specs, out_specs=out_specs, out_shape=out_shape,
                              scratch_shapes=scratch_shapes, compiler_params=_cparams(*semantics))(*args)
        return list(outs), []
    n_in, n_out, n_scr = len(in_specs), len(out_specs), len(scratch_shapes)
    ci, co = len(comm.inputs), len(comm.out_shapes)
    nsteps = functools.reduce(lambda a, b: a * b, grid, 1)

    def fused(*refs):
        a, b = n_in, n_in + ci
        c, d = b + n_out, b + n_out + co
        e = d + n_scr
        step = pl.program_id(0)
        for ax in range(1, len(grid)):
            step = step * grid[ax] + pl.program_id(ax)
        comm.run(step, nsteps, refs[a:b], refs[c:d], refs[e:])
        kern(*refs[:a], *refs[b:c], *refs[d:e])

    outs = pl.pallas_call(
        fused, name=name, grid=grid, in_specs=in_specs + [_ANY] * ci, out_specs=out_specs + [_ANY] * co,
        out_shape=out_shape + comm.out_shapes, scratch_shapes=scratch_shapes + comm.sems,
        input_output_aliases={n_in + i: n_out + o for i, o in comm.aliases.items()},
        compiler_params=_cparams(*("arbitrary" for _ in grid)))(*args, *comm.inputs)
    return list(outs[:n_out]), list(outs[n_out:])


def _comm_only(comm, name):
    def body(*refs):
        ci, co = len(comm.inputs), len(comm.out_shapes)
        comm.run(0, 1, refs[:ci], refs[ci:ci + co], refs[ci + co:])

    outs = pl.pallas_call(body, name=name, in_specs=[_ANY] * len(comm.inputs), out_specs=[_ANY] * len(comm.out_shapes),
                          out_shape=comm.out_shapes, scratch_shapes=comm.sems,
                          input_output_aliases=dict(comm.aliases))(*comm.inputs)
    return list(outs)


def _in_proj(x, w_in, layer):
    T, D = x.shape
    N = 4 * w_in.shape[-1]
    NQ = N - 2 * D
    tm = 512

    def kern(x_ref, w_hbm, hq_ref, hg_ref, xb_ref, w_v, sem):
        @pl.when(pl.program_id(0) == 0)
        def _():
            _load_cols(w_hbm, w_v, sem)

        xb = x_ref[...].astype(_MXU)
        hq_ref[...] = _dot(xb, w_v[:, :NQ]).astype(hq_ref.dtype)
        hg_ref[...] = _dot(xb, w_v[:, NQ:]).astype(hg_ref.dtype)
        xb_ref[...] = xb.astype(xb_ref.dtype)

    return pl.pallas_call(
        kern, name=f"in_proj_{layer}", grid=(T // tm,),
        in_specs=[_rows(tm, D), _ANY],
        out_specs=[_rows(tm, NQ), _rows(tm, 2 * D), _rows(tm, D)],
        out_shape=[jax.ShapeDtypeStruct((T, NQ), _ACT), jax.ShapeDtypeStruct((T, 2 * D), _ACT),
                   jax.ShapeDtypeStruct((T, D), _ACT)],
        scratch_shapes=[pltpu.VMEM((D, N), w_in.dtype), pltpu.SemaphoreType.DMA((4,))],
        compiler_params=_cparams("arbitrary"),
    )(x, w_in)


def _mix_fwd(oa, ob, hg, x, wpa, wpb, wo, bg, gamma, beta, alpha, layer, comm=None):
    T, D = x.shape
    WA, WB = oa.shape[1], ob.shape[1]
    tm = 256

    def kern(oa_ref, ob_ref, hg_ref, x_ref, bg_ref, g_ref, b_ref, wpa_h, wpb_h, wo_h,
             x1_ref, u1_ref, pre_ref, wpa_v, wpb_v, wo_v, sa, sb, so):
        @pl.when(pl.program_id(0) == 0)
        def _():
            _load_cols(wpa_h, wpa_v, sa)
            _load_cols(wpb_h, wpb_v, sb)
            _load_rows(wo_h, wo_v, so)

        ya = _dot(oa_ref[...].astype(_MXU), wpa_v[...])
        yb = _dot(ob_ref[...].astype(_MXU), wpb_v[...])
        hgv = hg_ref[...].astype(_F32)
        bgv = bg_ref[...]
        ga = jax.nn.sigmoid(hgv[:, :D] + bgv[:, :D])
        gb = jax.nn.sigmoid(hgv[:, D:] + bgv[:, D:])
        pre = ga * ya + gb * yb
        mix = _dot(pre.astype(_MXU), wo_v[...])
        u = alpha * x_ref[...] + mix
        xhat, _ = _ln_stats(u)
        x1_ref[...] = xhat * g_ref[...] + b_ref[...]
        u1_ref[...] = u
        pre_ref[...] = pre.astype(pre_ref.dtype)

    return _call(
        kern, comm, name=f"mix_fwd_{layer}", grid=(T // tm,),
        in_specs=[_rows(tm, WA), _rows(tm, WB), _rows(tm, 2 * D), _rows(tm, D),
                  _whole((1, 2 * D)), _whole((1, D)), _whole((1, D)), _ANY, _ANY, _ANY],
        out_specs=[_rows(tm, D)] * 3,
        out_shape=[jax.ShapeDtypeStruct((T, D), _F32), jax.ShapeDtypeStruct((T, D), _F32),
                   jax.ShapeDtypeStruct((T, D), _ACT)],
        scratch_shapes=[pltpu.VMEM((WA, D), wpa.dtype), pltpu.VMEM((WB, D), wpb.dtype), pltpu.VMEM((D, D), wo.dtype),
                        pltpu.SemaphoreType.DMA((4,)), pltpu.SemaphoreType.DMA((4,)), pltpu.SemaphoreType.DMA((4,))],
        args=(oa, ob, hg, x, bg, gamma, beta, wpa, wpb, wo), semantics=("arbitrary",))


def _ffn_fwd(x1, wfi, wfo, gamma, beta, alpha, layer, comm=None):
    T, D = x1.shape
    F2 = 4 * wfi.shape[-1]
    F = F2 // 2
    tm = 256
    fc = F // 2

    def kern(x_ref, g_ref, b_ref, wi_h, wo_h, x2_ref, u2_ref, act_ref, gu_ref, xb_ref, wi_v, wo_v, si, so):
        @pl.when(pl.program_id(0) == 0)
        def _():
            _load_cols(wi_h, wi_v, si)
            _load_rows(wo_h, wo_v, so)

        x = x_ref[...]
        xb = x.astype(_MXU)
        xb_ref[...] = xb.astype(xb_ref.dtype)
        ffn = jnp.zeros((tm, D), _F32)
        for c in range(2):
            g = _dot(xb, wi_v[:, c * fc:(c + 1) * fc])
            u = _dot(xb, wi_v[:, F + c * fc:F + (c + 1) * fc])
            act = g * jax.nn.sigmoid(g) * u
            ab = act.astype(_MXU)
            ffn = ffn + _dot(ab, wo_v[c * fc:(c + 1) * fc, :])
            act_ref[:, c * fc:(c + 1) * fc] = ab.astype(act_ref.dtype)
            gu_ref[:, c * fc:(c + 1) * fc] = g.astype(gu_ref.dtype)
            gu_ref[:, F + c * fc:F + (c + 1) * fc] = u.astype(gu_ref.dtype)
        uu = alpha * x + ffn
        xhat, _ = _ln_stats(uu)
        x2_ref[...] = xhat * g_ref[...] + b_ref[...]
        u2_ref[...] = uu

    return _call(
        kern, comm, name=f"ffn_fwd_{layer}", grid=(T // tm,),
        in_specs=[_rows(tm, D), _whole((1, D)), _whole((1, D)), _ANY, _ANY],
        out_specs=[_rows(tm, D), _rows(tm, D), _rows(tm, F), _rows(tm, F2), _rows(tm, D)],
        out_shape=[jax.ShapeDtypeStruct((T, D), _F32), jax.ShapeDtypeStruct((T, D), _F32),
                   jax.ShapeDtypeStruct((T, F), _ACT), jax.ShapeDtypeStruct((T, F2), _ACT),
                   jax.ShapeDtypeStruct((T, D), _ACT)],
        scratch_shapes=[pltpu.VMEM((D, F2), wfi.dtype), pltpu.VMEM((F, D), wfo.dtype),
                        pltpu.SemaphoreType.DMA((4,)), pltpu.SemaphoreType.DMA((4,))],
        args=(x1, gamma, beta, wfi, wfo), semantics=("arbitrary",))


def _ffn_bwd(u2, dy_or_target, gu, gamma, beta, wfi, wfo, alpha, layer, last):
    T, D = u2.shape
    F2 = gu.shape[1]
    F = F2 // 2
    tm = 256
    fc = F // 2

    def kern(u_ref, dy_ref, gu_ref, g_ref, b_ref, wi_h, wo_h, dx_ref, dub_ref, dgu_ref, st_ref, wi_v, wo_v, si, so):
        @pl.when(pl.program_id(0) == 0)
        def _():
            _load_cols(wi_h, wi_v, si)
            _load_rows(wo_h, wo_v, so)
            st_ref[...] = jnp.zeros_like(st_ref)

        gam = g_ref[...]
        u = u_ref[...]
        if last:
            xhat0, _ = _ln_stats(u)
            err = xhat0 * gam + b_ref[...] - dy_ref[...]
            dy = err * (1.0 / D)
            st_ref[2:3, :] += jnp.sum(err * err, axis=0, keepdims=True)
        else:
            dy = dy_ref[...]
        du, dgam, dbet, _ = _ln_bwd(u, dy, gam)
        st_ref[0:1, :] += dgam
        st_ref[1:2, :] += dbet
        dub = du.astype(_MXU)
        dub_ref[...] = dub.astype(dub_ref.dtype)
        dx = alpha * du
        for c in range(2):
            dact = _dot_nt(dub, wo_v[c * fc:(c + 1) * fc, :])
            g = gu_ref[:, c * fc:(c + 1) * fc].astype(_F32)
            uu = gu_ref[:, F + c * fc:F + (c + 1) * fc].astype(_F32)
            sg = jax.nn.sigmoid(g)
            dg = (dact * uu * (sg * (1.0 + g * (1.0 - sg)))).astype(_MXU)
            dup = (dact * (g * sg)).astype(_MXU)
            dgu_ref[:, c * fc:(c + 1) * fc] = dg.astype(dgu_ref.dtype)
            dgu_ref[:, F + c * fc:F + (c + 1) * fc] = dup.astype(dgu_ref.dtype)
            dx = dx + _dot_nt(dg, wi_v[:, c * fc:(c + 1) * fc]) + _dot_nt(dup, wi_v[:, F + c * fc:F + (c + 1) * fc])
        dx_ref[...] = dx

    return pl.pallas_call(
        kern, name=f"ffn_bwd_{layer}", grid=(T // tm,),
        in_specs=[_rows(tm, D), _rows(tm, D), _rows(tm, F2), _whole((1, D)), _whole((1, D)), _ANY, _ANY],
        out_specs=[_rows(tm, D), _rows(tm, D), _rows(tm, F2), _whole((8, D))],
        out_shape=[jax.ShapeDtypeStruct((T, D), _F32), jax.ShapeDtypeStruct((T, D), _ACT),
                   jax.ShapeDtypeStruct((T, F2), _ACT), jax.ShapeDtypeStruct((8, D), _F32)],
        scratch_shapes=[pltpu.VMEM((D, F2), wfi.dtype), pltpu.VMEM((F, D), wfo.dtype),
                        pltpu.SemaphoreType.DMA((4,)), pltpu.SemaphoreType.DMA((4,))],
        compiler_params=_cparams("arbitrary"),
    )(u2, dy_or_target, gu, gamma, beta, wfi, wfo)


def _residual_nt(res, res_scale, d, w, name, comm=None):
    T, K = res.shape
    N = d.shape[1]
    tm = 512

    def kern(r_ref, d_ref, w_hbm, o_ref, w_v, sem):
        @pl.when(pl.program_id(0) == 0)
        def _():
            _load_cols(w_hbm, w_v, sem)

        o_ref[...] = res_scale * r_ref[...] + _dot_nt(d_ref[...].astype(_MXU), w_v[...])

    outs, extra = _call(
        kern, comm, name=name, grid=(T // tm,),
        in_specs=[_rows(tm, K), _rows(tm, N), _ANY], out_specs=[_rows(tm, K)],
        out_shape=[jax.ShapeDtypeStruct((T, K), _F32)],
        scratch_shapes=[pltpu.VMEM((K, N), w.dtype), pltpu.SemaphoreType.DMA((4,))],
        args=(res, d, w), semantics=("arbitrary",))
    return outs[0], extra


def _mix_bwd(u1, dx1, oa, ob, hg, wpa, wpb, wo, bg, gamma, layer):
    T, D = u1.shape
    WA, WB = wpa.shape[-2], wpb.shape[-2]
    tm = 256

    def kern(u_ref, dx_ref, oa_ref, ob_ref, hg_ref, bg_ref, g_ref, wpa_h, wpb_h, wo_h,
             du_ref, dub_ref, dya_ref, dyb_ref, dhg_ref, doa_ref, dob_ref, st_ref,
             wpa_v, wpb_v, wo_v, sa, sb, so):
        @pl.when(pl.program_id(0) == 0)
        def _():
            _load_cols(wpa_h, wpa_v, sa)
            _load_cols(wpb_h, wpb_v, sb)
            _load_rows(wo_h, wo_v, so)
            st_ref[...] = jnp.zeros_like(st_ref)

        du, dgam, dbet, _ = _ln_bwd(u_ref[...], dx_ref[...], g_ref[...])
        st_ref[1:2, :D] += dgam
        st_ref[1:2, D:] += dbet
        du_ref[...] = du
        dub = du.astype(_MXU)
        dub_ref[...] = dub.astype(dub_ref.dtype)
        dpre = _dot_nt(dub, wo_v[...])
        hgv = hg_ref[...].astype(_F32)
        bgv = bg_ref[...]
        ga = jax.nn.sigmoid(hgv[:, :D] + bgv[:, :D])
        gb = jax.nn.sigmoid(hgv[:, D:] + bgv[:, D:])
        dya = (dpre * ga).astype(_MXU)
        dyb = (dpre * gb).astype(_MXU)
        dsa = dpre * _dot(oa_ref[...].astype(_MXU), wpa_v[...]) * (ga * (1.0 - ga))
        dsb = dpre * _dot(ob_ref[...].astype(_MXU), wpb_v[...]) * (gb * (1.0 - gb))
        st_ref[0:1, :D] += jnp.sum(dsa, axis=0, keepdims=True)
        st_ref[0:1, D:] += jnp.sum(dsb, axis=0, keepdims=True)
        dya_ref[...] = dya.astype(dya_ref.dtype)
        dyb_ref[...] = dyb.astype(dyb_ref.dtype)
        dhg_ref[:, :D] = dsa.astype(dhg_ref.dtype)
        dhg_ref[:, D:] = dsb.astype(dhg_ref.dtype)
        doa_ref[...] = _dot_nt(dya, wpa_v[...]).astype(doa_ref.dtype)
        dob_ref[...] = _dot_nt(dyb, wpb_v[...]).astype(dob_ref.dtype)

    return pl.pallas_call(
        kern, name=f"mix_bwd_{layer}", grid=(T // tm,),
        in_specs=[_rows(tm, D), _rows(tm, D), _rows(tm, WA), _rows(tm, WB), _rows(tm, 2 * D), _whole((1, 2 * D)),
                  _whole((1, D)), _ANY, _ANY, _ANY],
        out_specs=[_rows(tm, D)] * 4 + [_rows(tm, 2 * D), _rows(tm, WA), _rows(tm, WB), _whole((8, 2 * D))],
        out_shape=[jax.ShapeDtypeStruct((T, D), _F32)] + [jax.ShapeDtypeStruct((T, D), _ACT)] * 3
        + [jax.ShapeDtypeStruct((T, 2 * D), _ACT), jax.ShapeDtypeStruct((T, WA), _ACT),
           jax.ShapeDtypeStruct((T, WB), _ACT), jax.ShapeDtypeStruct((8, 2 * D), _F32)],
        scratch_shapes=[pltpu.VMEM((WA, D), wpa.dtype), pltpu.VMEM((WB, D), wpb.dtype), pltpu.VMEM((D, D), wo.dtype),
                        pltpu.SemaphoreType.DMA((4,)), pltpu.SemaphoreType.DMA((4,)), pltpu.SemaphoreType.DMA((4,))],
        compiler_params=_cparams("arbitrary"),
    )(u1, dx1, oa, ob, hg, bg, gamma, wpa, wpb, wo)


def _grad_w(a, b, *, col_shards, name, comm=None):
    T, M = a.shape
    N = b.shape[1]
    tk = 1024 if T % 1024 == 0 else 512
    n = N // 4 if col_shards else N
    whole = M * N * 4 <= _GRAD_ACC_BYTES
    tn = N if whole else (n if col_shards else _divisor_tile(N, _GRAD_ACC_BYTES // (4 * M)))
    nk = T // tk

    def kern(a_ref, b_ref, o_ref, acc):
        k = pl.program_id(1)

        @pl.when(k == 0)
        def _():
            acc[...] = jnp.zeros_like(acc)

        acc[...] += _dot_tn(a_ref[...].astype(_MXU), b_ref[...].astype(_MXU))

        @pl.when(k == nk - 1)
        def _():
            if col_shards and whole:
                for s in range(4):
                    o_ref[s] = acc[:, s * n:(s + 1) * n].astype(o_ref.dtype)
            else:
                o_ref[...] = acc[...].astype(o_ref.dtype)

    if col_shards:
        out_spec = (pl.BlockSpec((4, M, n), lambda j, k: (0, 0, 0)) if whole
                    else pl.BlockSpec((None, M, n), lambda j, k: (j, 0, 0)))
        out_shape = jax.ShapeDtypeStruct((4, M, n), _ACT)
    else:
        out_spec = pl.BlockSpec((M, tn), lambda j, k: (0, j))
        out_shape = jax.ShapeDtypeStruct((M, N), _ACT)
    outs, extra = _call(
        kern, comm, name=name, grid=(N // tn, nk),
        in_specs=[pl.BlockSpec((tk, M), lambda j, k: (k, 0)), pl.BlockSpec((tk, tn), lambda j, k: (k, j))],
        out_specs=[out_spec], out_shape=[out_shape], scratch_shapes=[pltpu.VMEM((M, tn), _F32)],
        args=(a, b), semantics=("parallel", "arbitrary"))
    return outs[0], extra


def _bias_tiles(rel):
    H = rel.shape[0]
    span = _TQ * _BAND_TILES - 1
    edge = span - _REL_CLIP
    gvec = jnp.concatenate([jnp.broadcast_to(rel[:, :1], (H, edge)), rel, jnp.broadcast_to(rel[:, -1:], (H, edge))], axis=1)
    width = _BIAS_TILES * _TQ
    period = width + _TQ
    tiled = jnp.broadcast_to(jnp.pad(gvec[:, ::-1], ((0, 0), (0, 1)))[:, None, :], (H, _TQ, period))
    rows = tiled.reshape(H, _TQ * period)[:, :_TQ * (period - 1)].reshape(H, _TQ, period - 1)[:, :, _TQ - 1:]
    r = jnp.arange(_TQ)[:, None]
    u = jnp.arange(width)[None, :]
    d = 4 * _TQ + r - u
    rm = r % _CHUNK
    valid = (d >= rm - (_CHUNK - 1)) & (d <= rm + 8 * _CHUNK)
    tiles = jnp.where(valid[None], rows, _MASKED)
    return tiles.reshape(H // 2, 2 * _TQ, _BIAS_TILES, _TQ).transpose(0, 2, 1, 3)


def _fold_bias_grad(db):
    H = 2 * db.shape[0]
    width = _BIAS_TILES * _TQ
    period = width + _TQ
    x = jnp.pad(db.transpose(0, 2, 1, 3).reshape(H, _TQ, width), ((0, 0), (0, 0), (_TQ - 1, 0)))
    skew = jnp.pad(x.reshape(H, _TQ * (period - 1)), ((0, 0), (0, _TQ))).reshape(H, _TQ, period)
    dg = skew.sum(axis=1)[:, :period - 1][:, ::-1]
    span = _TQ * _BAND_TILES - 1
    edge = span - _REL_CLIP
    mid = dg[:, edge:edge + 2 * _REL_CLIP + 1]
    lo = dg[:, :edge].sum(axis=1)
    hi = dg[:, edge + 2 * _REL_CLIP + 1:].sum(axis=1)
    return mid.at[:, 0].add(lo).at[:, -1].add(hi)


def _band_window(i):
    j0 = jnp.maximum(i - (_BAND_TILES - 1), 0)
    return j0, (_BAND_TILES - 1) - (i - j0)


def _head_masks():
    lane = lax.broadcasted_iota(jnp.int32, (1, _LANES), 1)
    return [(lane // _HEAD) == hh for hh in range(2)]


def _stack_heads(x, masks):
    return jnp.concatenate([jnp.where(m, x, jnp.zeros_like(x)) for m in masks], axis=0)


def _unstack_heads(y, masks):
    return jnp.where(masks[0], y[:_TQ], y[_TQ:])


def _scaled(q):
    return q * jnp.asarray(_HEAD ** -0.5, q.dtype)


def _band_probs(q2, k_ref, b_ref, j0, boff):
    s = []
    for j in range(_BAND_TILES):
        kj = k_ref[pl.ds(pl.multiple_of((j0 + j) * _TQ, _TQ), _TQ), :]
        s.append(_dot_nt(q2, kj) + b_ref[boff + j])
    m = jnp.max(functools.reduce(jnp.maximum, s), axis=-1, keepdims=True)
    p = [jnp.exp(x - m) for x in s]
    l = jnp.sum(functools.reduce(lambda a, b: a + b, p), axis=-1, keepdims=True)
    return p, 1.0 / l


def _qkv_specs(T, cb, npair, tq=_TQ):
    return [pl.BlockSpec((tq, _LANES), lambda h, i: (i, cb + h)),
            pl.BlockSpec((T, _LANES), lambda h, i: (0, cb + npair + h)),
            pl.BlockSpec((T, _LANES), lambda h, i: (0, cb + 2 * npair + h))]


def _attn_a_fwd(hq, bias, col0, width, layer, comm=None):
    T = hq.shape[0]
    npair = width // _LANES
    nsub = _BAND_SUBTILES
    tq = nsub * _TQ

    def kern(q_ref, k_ref, v_ref, b_ref, o_ref):
        masks = _head_masks()
        q = _scaled(q_ref[...])
        for s in range(nsub):
            part = slice(s * _TQ, (s + 1) * _TQ)
            j0, boff = _band_window(nsub * pl.program_id(1) + s)
            p, inv = _band_probs(_stack_heads(q[part], masks), k_ref, b_ref, j0, boff)
            o = jnp.zeros((2 * _TQ, _LANES), _F32)
            for j in range(_BAND_TILES):
                vj = v_ref[pl.ds(pl.multiple_of((j0 + j) * _TQ, _TQ), _TQ), :]
                o = o + _dot(p[j].astype(_MXU), vj)
            o_ref[part, :] = _unstack_heads(o * inv, masks).astype(o_ref.dtype)

    outs, extra = _call(
        kern, comm, name=f"band_attn_fwd_{layer}", grid=(npair, T // tq),
        in_specs=_qkv_specs(T, col0 // _LANES, npair, tq)
        + [pl.BlockSpec((None, _BIAS_TILES, 2 * _TQ, _TQ), lambda h, i: (h, 0, 0, 0))],
        out_specs=[pl.BlockSpec((tq, _LANES), lambda h, i: (i, h))],
        out_shape=[jax.ShapeDtypeStruct((T, width), _ACT)], scratch_shapes=[],
        args=(hq, hq, hq, bias), semantics=("arbitrary", "arbitrary"))
    return outs[0], extra


def _attn_a_bwd(hq, bias, do, col0, width, layer, comm=None):
    T = hq.shape[0]
    npair = width // _LANES
    nsub = _BAND_SUBTILES
    tq = nsub * _TQ
    nq = T // tq
    scale = _HEAD ** -0.5

    def kern(q_ref, k_ref, v_ref, b_ref, do_ref, dq_ref, dk_ref, dv_ref, db_ref, dk_acc, dv_acc):
        i = pl.program_id(1)

        @pl.when(i == 0)
        def _():
            dk_acc[...] = jnp.zeros_like(dk_acc)
            dv_acc[...] = jnp.zeros_like(dv_acc)
            db_ref[...] = jnp.zeros_like(db_ref)

        masks = _head_masks()
        q = _scaled(q_ref[...])
        do_t = do_ref[...]
        for s in range(nsub):
            part = slice(s * _TQ, (s + 1) * _TQ)
            j0, boff = _band_window(nsub * i + s)
            q2 = _stack_heads(q[part], masks)
            do2 = _stack_heads(do_t[part], masks).astype(_MXU)
            p, inv = _band_probs(q2, k_ref, b_ref, j0, boff)
            rows = [pl.ds(pl.multiple_of((j0 + j) * _TQ, _TQ), _TQ) for j in range(_BAND_TILES)]
            p = [x * inv for x in p]
            dp = [_dot_nt(do2, v_ref[rows[j], :]) for j in range(_BAND_TILES)]
            delta = jnp.sum(functools.reduce(lambda a, b: a + b, [p[j] * dp[j] for j in range(_BAND_TILES)]),
                            axis=-1, keepdims=True)
            dq = jnp.zeros((2 * _TQ, _LANES), _F32)
            for j in range(_BAND_TILES):
                ds = p[j] * (dp[j] - delta)
                db_ref[boff + j] += ds
                dsb = ds.astype(_MXU)
                dq = dq + _dot(dsb, k_ref[rows[j], :])
                dk_acc[rows[j], :] += _dot_tn(dsb, q2)
                dv_acc[rows[j], :] += _dot_tn(p[j].astype(_MXU), do2)
            dq_ref[part, :] = (_unstack_heads(dq, masks) * scale).astype(dq_ref.dtype)

        @pl.when(i == nq - 1)
        def _():
            dk_ref[...] = dk_acc[...].astype(dk_ref.dtype)
            dv_ref[...] = dv_acc[...].astype(dv_ref.dtype)

    strip = pl.BlockSpec((None, _BIAS_TILES, 2 * _TQ, _TQ), lambda h, i: (h, 0, 0, 0))
    tile = pl.BlockSpec((tq, _LANES), lambda h, i: (i, h))
    column = pl.BlockSpec((T, _LANES), lambda h, i: (0, h))
    outs, extra = _call(
        kern, comm, name=f"band_attn_bwd_{layer}", grid=(npair, nq),
        in_specs=_qkv_specs(T, col0 // _LANES, npair, tq) + [strip, tile],
        out_specs=[tile, column, column, strip],
        out_shape=[jax.ShapeDtypeStruct((T, width), _ACT)] * 3
        + [jax.ShapeDtypeStruct((npair, _BIAS_TILES, 2 * _TQ, _TQ), _F32)],
        scratch_shapes=[pltpu.VMEM((T, _LANES), _F32), pltpu.VMEM((T, _LANES), _F32)],
        args=(hq, hq, hq, bias, do), semantics=("arbitrary", "arbitrary"))
    return outs, extra


def _suffix_matrix():
    r = lax.broadcasted_iota(jnp.int32, (_TQ, _TQ), 0)
    c = lax.broadcasted_iota(jnp.int32, (_TQ, _TQ), 1)
    r2 = lax.broadcasted_iota(jnp.int32, (2 * _TQ, _TQ), 0)
    c2 = lax.broadcasted_iota(jnp.int32, (2 * _TQ, _TQ), 1)
    return (r > c).astype(_MXU), c2 - (r2 & (_TQ - 1))


def _suffix_sums(xs, tri):
    n, k = xs[0].shape[0], len(xs)
    his = [x.astype(_MXU) for x in xs]
    los = [(x - h.astype(_F32)).astype(_MXU) for x, h in zip(xs, his)]
    y = _dot(jnp.concatenate(his + los, axis=0), tri)
    return [y[j * n:(j + 1) * n] + y[(k + j) * n:(k + j + 1) * n] for j in range(k)]


def _stick_tiles(tiles, rel, carry_l, tri):
    zs = [_dot_nt(qs, kj) for qs, kj, _, _ in tiles]
    Ls, masks = [], []
    for z, (_, _, jj, _) in zip(zs, tiles):
        nsp = -(jnp.maximum(z, 0.0) + jnp.log(1.0 + jnp.exp(-jnp.abs(z))))
        if isinstance(jj, int):
            mask = (rel < 0) if jj == 0 else None
        else:
            mask = rel < jnp.where(jj == 0, 0, _TQ)
        Ls.append(nsp if mask is None else jnp.where(mask, nsp, 0.0))
        masks.append(mask)
    carry_l = list(carry_l)
    ws = []
    for z, L, suffix, mask, (_, _, _, sub) in zip(zs, Ls, _suffix_sums(Ls, tri), masks, tiles):
        w = jnp.exp(z + L + suffix + carry_l[sub])
        ws.append(w if mask is None else jnp.where(mask, w, 0.0))
        carry_l[sub] = carry_l[sub] + jnp.sum(L, axis=-1, keepdims=True)
    return zs, Ls, ws, masks, carry_l


def _sweep(i, step, zero):
    nsub = _SB_SUBTILES

    def window():
        tiles = [(s, jj) for jj in range(_SB_WINDOW) for s in range(nsub)]
        return tuple((jnp.int32(_SB_WINDOW),) + c for c in step(tiles, [zero] * nsub))

    start = lax.cond(i >= -(-(_SB_WINDOW - 1) // nsub), window, lambda: tuple((jnp.int32(0),) + zero for _ in range(nsub)))
    outs = []
    for s in range(nsub):
        def done(c, s=s):
            return jnp.logical_or(c[0] > nsub * i + s, jnp.max(c[1]) < _EXP_ZERO_BELOW)

        def more(c, s=s):
            carries = [None] * nsub
            carries[s] = c[1:]
            return (c[0] + 1,) + step([(s, c[0])], carries)[s]

        outs.append(lax.while_loop(lambda c, done=done: jnp.logical_not(done(c)), more, start[s]))
    return outs


def _sb_fwd(hq, col0, width, layer, comm=None):
    T = hq.shape[0]
    npair = width // _LANES
    nsub = _SB_SUBTILES
    tq = nsub * _TQ

    def kern(q_ref, k_ref, v_ref, o_ref):
        i = pl.program_id(1)
        masks = _head_masks()
        tri, rel = _suffix_matrix()
        q = _scaled(q_ref[...])
        q2 = [_stack_heads(q[s * _TQ:(s + 1) * _TQ], masks) for s in range(nsub)]

        def step(tiles, carries):
            rows = [pl.ds(pl.multiple_of((nsub * i + s - jj) * _TQ, _TQ), _TQ) for s, jj in tiles]
            cls = [None if c is None else c[0] for c in carries]
            accs = [None if c is None else c[1] for c in carries]
            _, _, ws, _, cls = _stick_tiles([(q2[s], k_ref[r, :], jj, s) for (s, jj), r in zip(tiles, rows)], rel, cls, tri)
            for w, r, (s, _) in zip(ws, rows, tiles):
                accs[s] = accs[s] + _dot(w.astype(_MXU), v_ref[r, :])
            return [None if c is None else (cls[s], accs[s]) for s, c in enumerate(carries)]

        outs = _sweep(i, step, (jnp.zeros((2 * _TQ, 1), _F32), jnp.zeros((2 * _TQ, _LANES), _F32)))
        for s in range(nsub):
            o_ref[s * _TQ:(s + 1) * _TQ, :] = _unstack_heads(outs[s][2], masks)

    outs, extra = _call(
        kern, comm, name=f"stick_attn_fwd_{layer}", grid=(npair, T // tq),
        in_specs=_qkv_specs(T, col0 // _LANES, npair, tq),
        out_specs=[pl.BlockSpec((tq, _LANES), lambda h, i: (i, h))],
        out_shape=[jax.ShapeDtypeStruct((T, width), _F32)], scratch_shapes=[],
        args=(hq, hq, hq), semantics=("arbitrary", "arbitrary"))
    return outs[0], extra


def _sb_bwd(hq, o, do, col0, width, layer, comm=None):
    T = hq.shape[0]
    npair = width // _LANES
    nsub = _SB_SUBTILES
    tq = nsub * _TQ
    nq = T // tq
    scale = _HEAD ** -0.5

    def kern(q_ref, k_ref, v_ref, o_ref, do_ref, dq_ref, dk_ref, dv_ref, dk_acc, dv_acc):
        i = pl.program_id(1)

        @pl.when(i == 0)
        def _():
            dk_acc[...] = jnp.zeros_like(dk_acc)
            dv_acc[...] = jnp.zeros_like(dv_acc)

        masks = _head_masks()
        tri, rel = _suffix_matrix()
        q = _scaled(q_ref[...])
        do_t = do_ref[...]
        prod = do_t.astype(_F32) * o_ref[...]
        part = [slice(s * _TQ, (s + 1) * _TQ) for s in range(nsub)]
        q2 = [_stack_heads(q[p], masks) for p in part]
        do2 = [_stack_heads(do_t[p], masks).astype(_MXU) for p in part]
        dsum = [jnp.sum(_stack_heads(prod[p], masks), axis=-1, keepdims=True) for p in part]

        def step(tiles, carries):
            rows = [pl.ds(pl.multiple_of((nsub * i + s - jj) * _TQ, _TQ), _TQ) for s, jj in tiles]
            kjs = [k_ref[r, :] for r in rows]
            cls, cgs, dqs = ([None if c is None else c[n] for c in carries] for n in range(3))
            zs, Ls, ws, tile_masks, cls = _stick_tiles([(q2[s], kj, jj, s) for (s, jj), kj in zip(tiles, kjs)], rel, cls, tri)
            wbs = [w.astype(_MXU) for w in ws]
            gs = [wb.astype(_F32) * _dot_nt(do2[s], v_ref[r, :]) for wb, r, (s, _) in zip(wbs, rows, tiles)]
            for z, L, g, later, mask, wb, kj, r, (s, _) in zip(zs, Ls, gs, _suffix_sums(gs, tri), tile_masks, wbs, kjs,
                                                               rows, tiles):
                dz = g - jnp.exp(z + L) * (dsum[s] - (later + cgs[s]))
                if mask is not None:
                    dz = jnp.where(mask, dz, 0.0)
                dzb = dz.astype(_MXU)
                dk_acc[r, :] += _dot_tn(dzb, q2[s])
                dv_acc[r, :] += _dot_tn(wb, do2[s])
                dqs[s] = dqs[s] + _dot(dzb, kj)
                cgs[s] = cgs[s] + jnp.sum(g, axis=-1, keepdims=True)
            return [None if c is None else (cls[s], cgs[s], dqs[s]) for s, c in enumerate(carries)]

        zc = jnp.zeros((2 * _TQ, 1), _F32)
        outs = _sweep(i, step, (zc, zc, jnp.zeros((2 * _TQ, _LANES), _F32)))
        for s in range(nsub):
            dq_ref[part[s], :] = (_unstack_heads(outs[s][3], masks) * scale).astype(dq_ref.dtype)

        @pl.when(i == nq - 1)
        def _():
            dk_ref[...] = dk_acc[...].astype(dk_ref.dtype)
            dv_ref[...] = dv_acc[...].astype(dv_ref.dtype)

    tile_spec = pl.BlockSpec((tq, _LANES), lambda h, i: (i, h))
    column = pl.BlockSpec((T, _LANES), lambda h, i: (0, h))
    outs, extra = _call(
        kern, comm, name=f"stick_attn_bwd_{layer}", grid=(npair, nq),
        in_specs=_qkv_specs(T, col0 // _LANES, npair, tq) + [tile_spec, tile_spec],
        out_specs=[tile_spec, column, column],
        out_shape=[jax.ShapeDtypeStruct((T, width), _ACT)] * 3,
        scratch_shapes=[pltpu.VMEM((T, _LANES), _F32), pltpu.VMEM((T, _LANES), _F32)],
        args=(hq, hq, hq, o, do), semantics=("arbitrary", "arbitrary"))
    return outs, extra


_DENSE = ("w_in", "w_proj_a", "w_proj_b", "w_out", "w_ffn_in", "w_ffn_out")
_COL_SHARDED = {"w_in": True, "w_proj_a": True, "w_proj_b": True, "w_out": False, "w_ffn_in": True, "w_ffn_out": False}
_SMALL = ("b_gate", "rel_bias", "ln1_g", "ln1_b", "ln2_g", "ln2_b")


class _Plans:
    def __init__(self, plans=None):
        self.plans = plans or {}

    def start(self, key):
        if key not in self.plans:
            return None, None
        return self.plans[key]()

    @staticmethod
    def finish(done, extra):
        if done is not None:
            done(extra)


def _layer_fwd(x, W, small, l, alpha, plans):
    WA = small["rel_bias"].shape[1] * _HEAD
    row = lambda v: v[l].reshape(1, -1)
    hq, hg, xb = _in_proj(x, W["w_in"], l)
    WB = (hq.shape[1] - 3 * WA) // 3
    bias = _bias_tiles(small["rel_bias"][l])
    comm, done = plans.start(f"band_fwd_{l}")
    oa, extra = _attn_a_fwd(hq, bias, 0, WA, l, comm)
    plans.finish(done, extra)
    comm, done = plans.start(f"stick_fwd_{l}")
    ob, extra = _sb_fwd(hq, 3 * WA, WB, l, comm)
    plans.finish(done, extra)
    comm, done = plans.start(f"mix_fwd_{l}")
    (x1, u1, pre), extra = _mix_fwd(oa, ob, hg, x, W["w_proj_a"], W["w_proj_b"], W["w_out"], row(small["b_gate"]),
                                            row(small["ln1_g"]), row(small["ln1_b"]), alpha, l, comm)
    plans.finish(done, extra)
    comm, done = plans.start(f"ffn_fwd_{l}")
    (x2, u2, act, gu, x1b), extra = _ffn_fwd(x1, W["w_ffn_in"], W["w_ffn_out"], row(small["ln2_g"]),
                                             row(small["ln2_b"]), alpha, l, comm)
    plans.finish(done, extra)
    return x2, dict(xb=xb, hq=hq, hg=hg, bias=bias, oa=oa, ob=ob, x1b=x1b, u1=u1, pre=pre, u2=u2, act=act, gu=gu)


def _layer_bwd(dy_or_target, S, W, small, l, last, alpha, plans, gw):
    D = S["xb"].shape[1]
    WA, WB = S["oa"].shape[1], S["ob"].shape[1]
    row = lambda v: v[l].reshape(1, -1)

    def blocks(g, n):
        return g if _COL_SHARDED[n] else g.reshape(4, g.shape[0] // 4, g.shape[1])

    dx1, du2b, dgu, st2 = _ffn_bwd(S["u2"], dy_or_target, S["gu"], row(small["ln2_g"]), row(small["ln2_b"]),
                                   W["w_ffn_in"], W["w_ffn_out"], alpha, l, last)
    gw["w_ffn_in"] = blocks(_grad_w(S["x1b"], dgu, col_shards=True, name=f"grad_w_ffn_in_{l}")[0], "w_ffn_in")
    gw["w_ffn_out"] = blocks(_grad_w(S["act"], du2b, col_shards=False, name=f"grad_w_ffn_out_{l}")[0], "w_ffn_out")
    du1, du1b, dya, dyb, dhg, doa, dob, st1 = _mix_bwd(S["u1"], dx1, S["oa"], S["ob"], S["hg"], W["w_proj_a"],
                                                       W["w_proj_b"], W["w_out"], row(small["b_gate"]),
                                                       row(small["ln1_g"]), l)
    gw["w_out"] = blocks(_grad_w(S["pre"], du1b, col_shards=False, name=f"grad_w_out_{l}")[0], "w_out")
    gw["w_proj_a"] = blocks(_grad_w(S["oa"], dya, col_shards=True, name=f"grad_w_proj_a_{l}")[0], "w_proj_a")
    gw["w_proj_b"] = blocks(_grad_w(S["ob"], dyb, col_shards=True, name=f"grad_w_proj_b_{l}")[0], "w_proj_b")
    comm, done = plans.start(f"band_bwd_{l}")
    (dqa, dka, dva, dbias), extra = _attn_a_bwd(S["hq"], S["bias"], doa, 0, WA, l, comm)
    plans.finish(done, extra)
    comm, done = plans.start(f"stick_bwd_{l}")
    (dqb, dkb, dvb), extra = _sb_bwd(S["hq"], S["ob"], dob, 3 * WA, WB, l, comm)
    plans.finish(done, extra)
    dh = jnp.concatenate([dqa, dka, dva, dqb, dkb, dvb, dhg], axis=1)
    comm, done = plans.start(f"grad_w_in_{l}")
    g, extra = _grad_w(S["xb"], dh, col_shards=True, name=f"grad_w_in_{l}", comm=comm)
    gw["w_in"] = blocks(g, "w_in")
    plans.finish(done, extra)
    comm, done = plans.start(f"in_proj_bwd_{l}")
    dx, extra = _residual_nt(du1, alpha, dh, W["w_in"], f"in_proj_bwd_{l}", comm)
    plans.finish(done, extra)
    gs = dict(b_gate=st1[0], rel_bias=_fold_bias_grad(dbias), ln1_g=st1[1, :D], ln1_b=st1[1, D:],
              ln2_g=st2[0], ln2_b=st2[1])
    return dx, gs, st2[2]


def _local_step(x, target, W, small, plans=None, gws=None):
    depth = len(W)
    alpha = float((2 * depth) ** 0.25)
    plans = plans or _Plans()
    gws = gws if gws is not None else [dict() for _ in range(depth)]
    saved = []
    h = x
    for l in range(depth):
        h, S = _layer_fwd(h, W[l], small, l, alpha, plans)
        saved.append(S)
    gss = [None] * depth
    d = target
    sq = None
    for l in reversed(range(depth)):
        d, gss[l], sq_l = _layer_bwd(d, saved[l], W[l], small, l, l == depth - 1, alpha, plans, gws[l])
        if l == depth - 1:
            sq = sq_l
    return sq, d, gws, gss


def _place():
    return lax.axis_index("x"), lax.axis_index("y"), lax.axis_index("c")


def _remote(src, dst, send_sem, recv_sem, to):
    return pltpu.make_async_remote_copy(src_ref=src, dst_ref=dst, send_sem=send_sem, recv_sem=recv_sem,
                                        device_id=to, device_id_type=_MESH)


def _half(ref, hc):
    kh = ref.shape[0] // 2
    return ref.at[pl.ds(pl.multiple_of(hc * kh, 16), kh), :]


def _gather_plan(blocks, fractions):
    nt = len(blocks)

    def run(step, nsteps, ins, outs, sems):
        send_sems, recv_sems, loc_sems = sems
        x, y, c = _place()
        k = 2 * x + y
        me, sibling = (x, y, c), (x, y, 1 - c)
        chips = [(1 - x, y), (x, 1 - y), (1 - x, 1 - y)]
        chip_k = [2 * cx + cy for cx, cy in chips]

        def ici(t, s, owner_k, to, src=None):
            dst = _half(outs[t].at[owner_k], c)
            return _remote(dst if src is None else src, dst, send_sems.at[t, s], recv_sems.at[t, s], to)

        def passed(t, s, hc, to):
            blk = _half(outs[t].at[chip_k[s]], hc)
            return _remote(blk, blk, send_sems.at[t, 3 + s], recv_sems.at[t, 3 + s], to)

        def local(t):
            return pltpu.make_async_copy(ins[t], outs[t].at[k], loc_sems.at[t])

        @pl.when(step == 0)
        def _():
            for t in range(nt):
                local(t).start()
                for s, chip in enumerate(chips):
                    ici(t, s, k, (*chip, c), src=_half(ins[t], c)).start()

        for t in range(nt):
            @pl.when(step == min(nsteps - 1, int(fractions[t] * nsteps)))
            def _():
                for s in range(3):
                    ici(t, s, chip_k[s], me).wait_recv()
                    passed(t, s, c, sibling).start()

        @pl.when(step == nsteps - 1)
        def _():
            for t in range(nt):
                for s, chip in enumerate(chips):
                    passed(t, s, 1 - c, me).wait_recv()
            for t in range(nt):
                for s, chip in enumerate(chips):
                    ici(t, s, k, (*chip, c), src=_half(ins[t], c)).wait_send()
                    passed(t, s, c, sibling).wait_send()
                local(t).wait()

    return _Comm(blocks, [jax.ShapeDtypeStruct((4,) + b.shape, b.dtype) for b in blocks],
                 [pltpu.SemaphoreType.DMA((nt, 6)), pltpu.SemaphoreType.DMA((nt, 6)), pltpu.SemaphoreType.DMA((nt,))], run)


def _scatter_plan(grads, owners):
    nt = len(grads)

    def run(step, nsteps, ins, outs, sems):
        send_sems, recv_sems, loc_sems = sems
        x, y, c = _place()
        me = 4 * x + 2 * y + c

        def target(r):
            tx = 1 - x if r & 2 else x
            ty = 1 - y if r & 1 else y
            return tx, ty

        def send(t, r):
            tx, ty = target(r)
            return _remote(ins[t].at[2 * tx + ty], outs[t].at[me], send_sems.at[t, r], recv_sems.at[t, 2 * r + c],
                           (tx, ty, owners[t]))

        def local(t):
            return pltpu.make_async_copy(ins[t].at[2 * x + y], outs[t].at[me], loc_sems.at[t])

        @pl.when(step == 0)
        def _():
            for t in range(nt):
                @pl.when(c == owners[t])
                def _():
                    local(t).start()

                @pl.when(c != owners[t])
                def _():
                    send(t, 0).start()

                for r in range(1, 4):
                    send(t, r).start()

        @pl.when(step == nsteps - 1)
        def _():
            for t in range(nt):
                @pl.when(c == owners[t])
                def _():
                    for r in range(4):
                        sx, sy = target(r)
                        for cs in range(2):
                            if r == 0 and cs == owners[t]:
                                continue
                            src_dev = 4 * sx + 2 * sy + cs
                            _remote(ins[t].at[0], outs[t].at[src_dev], send_sems.at[t, r], recv_sems.at[t, 2 * r + cs],
                                    (x, y, c)).wait_recv()
                    local(t).wait()

                @pl.when(c != owners[t])
                def _():
                    send(t, 0).wait_send()

                for r in range(1, 4):
                    send(t, r).wait_send()

    return _Comm(grads, [jax.ShapeDtypeStruct((8,) + g.shape[1:], g.dtype) for g in grads],
                 [pltpu.SemaphoreType.DMA((nt, 4)), pltpu.SemaphoreType.DMA((nt, 8)), pltpu.SemaphoreType.DMA((nt,))], run)


def _share_plan(reduced, owners):
    nt = len(reduced)

    def run(step, nsteps, ins, outs, sems):
        del ins
        send_sems, recv_sems = sems
        x, y, c = _place()

        def give(t, to):
            return _remote(outs[t], outs[t], send_sems.at[t], recv_sems.at[t], to)

        @pl.when(step == 0)
        def _():
            for t in range(nt):
                @pl.when(c == owners[t])
                def _():
                    give(t, (x, y, 1 - c)).start()

        @pl.when(step == nsteps - 1)
        def _():
            for t in range(nt):
                @pl.when(c == owners[t])
                def _():
                    give(t, (x, y, 1 - c)).wait_send()

                @pl.when(c != owners[t])
                def _():
                    give(t, (x, y, c)).wait_recv()

    return _Comm(reduced, [jax.ShapeDtypeStruct(r.shape, r.dtype) for r in reduced],
                 [pltpu.SemaphoreType.DMA((nt,)), pltpu.SemaphoreType.DMA((nt,))], run,
                 aliases={t: t for t in range(nt)})


def _join(a, b):
    ni, no, ns = len(a.inputs), len(a.out_shapes), len(a.sems)

    def run(step, nsteps, ins, outs, sems):
        a.run(step, nsteps, ins[:ni], outs[:no], sems[:ns])
        b.run(step, nsteps, ins[ni:], outs[no:], sems[ns:])

    aliases = dict(a.aliases)
    aliases.update({ni + i: no + o for i, o in b.aliases.items()})
    return _Comm(a.inputs + b.inputs, a.out_shapes + b.out_shapes, a.sems + b.sems, run, aliases)


def _peer(x, y, c, r):
    px = 1 - x if r & 4 else x
    py = 1 - y if r & 2 else y
    pc = 1 - c if r & 1 else c
    return (px, py, pc), 4 * px + 2 * py + pc


def _sum_slots(st, name):
    _, K, n = st.shape
    tr = next(t for t in (256, 128, 64, 32, 16) if K % t == 0)

    def kern(s_ref, o_ref):
        acc = s_ref[0].astype(_F32)
        for d in range(1, 8):
            acc = acc + s_ref[d].astype(_F32)
        o_ref[...] = acc.astype(o_ref.dtype)

    return pl.pallas_call(
        kern, name=name, grid=(K // tr,),
        in_specs=[pl.BlockSpec((8, tr, n), lambda i: (0, i, 0))], out_specs=_rows(tr, n),
        out_shape=jax.ShapeDtypeStruct((K, n), _ACT),
        compiler_params=_cparams("parallel"),
    )(st)


def _all_reduce_small(p):
    R = p.shape[0]

    def body(p_ref, o_ref, stage, send_sems, recv_sems):
        x, y, c = _place()
        me = 4 * x + 2 * y + c
        stage[me] = p_ref[...]
        sent = []
        for r in range(1, 8):
            to, _ = _peer(x, y, c, r)
            cp = _remote(p_ref, stage.at[me], send_sems.at[r - 1], recv_sems.at[r - 1], to)
            cp.start()
            sent.append(cp)
        for r in range(1, 8):
            _, src_dev = _peer(x, y, c, r)
            _remote(p_ref, stage.at[src_dev], send_sems.at[r - 1], recv_sems.at[r - 1], (x, y, c)).wait_recv()
        acc = stage[0]
        for d in range(1, 8):
            acc = acc + stage[d]
        o_ref[...] = acc
        for cp in sent:
            cp.wait_send()

    vm = pl.BlockSpec(memory_space=pltpu.VMEM)
    return pl.pallas_call(
        body, name="all_reduce_small",
        in_specs=[vm], out_specs=vm,
        out_shape=jax.ShapeDtypeStruct((R, _LANES), _F32),
        scratch_shapes=[pltpu.VMEM((8, R, _LANES), _F32), pltpu.SemaphoreType.DMA((7,)), pltpu.SemaphoreType.DMA((7,))],
    )(p)


def _adamw_update(gv, w_ref, m_ref, v_ref, gf_ref, d_ref, nm_ref, nv_ref):
    nm = _B1 * m_ref[...] + (1.0 - _B1) * gv
    nv = _B2 * v_ref[...] + (1.0 - _B2) * (gv * gv)
    m_hat = nm / (1.0 - _B1 ** _STEP)
    v_hat = nv / (1.0 - _B2 ** _STEP)
    gf_ref[...] = gv
    d_ref[...] = -_LR * (m_hat / (jnp.sqrt(v_hat) + _EPS) + _WD * w_ref[...])
    nm_ref[...] = nm
    nv_ref[...] = nv


def _adamw_layers(w, g_layers, m, v, name):
    _, K, n = w.shape
    tr = next(t for t in (256, 128, 64, 32, 16) if K % t == 0)

    def kern(w_ref, g0_ref, g1_ref, m_ref, v_ref, *out_refs):
        first = pl.program_id(0) == 0
        gv = jnp.where(first, g0_ref[...].astype(_F32), g1_ref[...].astype(_F32))
        _adamw_update(gv, w_ref, m_ref, v_ref, *out_refs)

    stacked = pl.BlockSpec((None, tr, n), lambda l, i: (l, i, 0))
    layer = pl.BlockSpec((tr, n), lambda l, i: (i, 0))
    return tuple(pl.pallas_call(
        kern, name=name, grid=(2, K // tr),
        in_specs=[stacked, layer, layer, stacked, stacked], out_specs=[stacked] * 4,
        out_shape=[jax.ShapeDtypeStruct(w.shape, _F32)] * 4,
        compiler_params=_cparams("parallel", "parallel"),
    )(w, g_layers[0], g_layers[1], m, v))


def _adamw(w, g, m, v, name):
    shape = w.shape
    w2, g2, m2, v2 = (a.reshape(-1, shape[-1]) for a in (w, g, m, v))
    R, C = w2.shape
    tr = next((t for t in (256, 128, 64, 32, 16) if R % t == 0), R)

    def kern(w_ref, g_ref, m_ref, v_ref, *out_refs):
        _adamw_update(g_ref[...].astype(_F32), w_ref, m_ref, v_ref, *out_refs)

    outs = pl.pallas_call(
        kern, name=name, grid=(R // tr,),
        in_specs=[_rows(tr, C)] * 4, out_specs=[_rows(tr, C)] * 4,
        out_shape=[jax.ShapeDtypeStruct((R, C), _F32)] * 4,
        compiler_params=_cparams("parallel"),
    )(w2, g2, m2, v2)
    return tuple(o.reshape(shape) for o in outs)


def _pack_small(gss, sq):
    parts = [gss[l][n].reshape(-1) for n in _SMALL for l in range(len(gss))] + [jnp.sum(sq).reshape(1)]
    flat = jnp.concatenate(parts)
    rows = -(-flat.shape[0] // (8 * _LANES)) * 8
    return jnp.pad(flat, (0, rows * _LANES - flat.shape[0])).reshape(rows, _LANES)


def _unpack_small(total, shapes):
    flat = total.reshape(-1)
    out, off = {}, 0
    for n in _SMALL:
        layers = []
        for _ in range(shapes[n][0]):
            size = 1
            for s in shapes[n][1:]:
                size *= s
            layers.append(flat[off:off + size].reshape(shapes[n][1:]))
            off += size
        out[n] = jnp.stack(layers)
    return out, flat[off]


_GATHER = {
    "band_fwd_0": [(0, "w_proj_a"), (0, "w_proj_b"), (0, "w_out"), (0, "w_ffn_out")],
    "stick_fwd_0": [(0, "w_ffn_in")],
    "mix_fwd_0": [(1, "w_in")],
    "ffn_fwd_0": [(1, "w_proj_a"), (1, "w_proj_b"), (1, "w_out"), (1, "w_ffn_in"), (1, "w_ffn_out")],
}
_SCATTER = {
    "band_bwd_1": [(1, "w_ffn_in"), (1, "w_ffn_out")],
    "stick_bwd_1": [(1, "w_proj_a"), (1, "w_proj_b"), (1, "w_out")],
    "band_bwd_0": [(1, "w_in"), (0, "w_ffn_in")],
    "stick_bwd_0": [(0, "w_ffn_out"), (0, "w_proj_a"), (0, "w_proj_b"), (0, "w_out")],
    "in_proj_bwd_0": [(0, "w_in")],
}
_SHARE = {"stick_bwd_1": "band_bwd_1", "band_bwd_0": "stick_bwd_1", "stick_bwd_0": "band_bwd_0", "grad_w_in_0": "stick_bwd_0"}


def _owner(key):
    del key
    return 1


def kernel(x, w_in, b_gate, rel_bias, w_proj_a, w_proj_b, w_out, ln1_g, ln1_b, w_ffn_in, w_ffn_out, ln2_g, ln2_b, loss_target, m_w_in, m_b_gate, m_rel_bias, m_w_proj_a, m_w_proj_b, m_w_out, m_ln1_g, m_ln1_b, m_w_ffn_in, m_w_ffn_out, m_ln2_g, m_ln2_b, v_w_in, v_b_gate, v_rel_bias, v_w_proj_a, v_w_proj_b, v_w_out, v_ln1_g, v_ln1_b, v_w_ffn_in, v_w_ffn_out, v_ln2_g, v_ln2_b):
    names = ("w_in", "b_gate", "rel_bias", "w_proj_a", "w_proj_b", "w_out", "ln1_g", "ln1_b", "w_ffn_in", "w_ffn_out", "ln2_g", "ln2_b")
    w = dict(zip(names, (w_in, b_gate, rel_bias, w_proj_a, w_proj_b, w_out, ln1_g, ln1_b, w_ffn_in, w_ffn_out, ln2_g, ln2_b)))
    m = dict(zip(names, (m_w_in, m_b_gate, m_rel_bias, m_w_proj_a, m_w_proj_b, m_w_out, m_ln1_g, m_ln1_b, m_w_ffn_in, m_w_ffn_out, m_ln2_g, m_ln2_b)))
    v = dict(zip(names, (v_w_in, v_b_gate, v_rel_bias, v_w_proj_a, v_w_proj_b, v_w_out, v_ln1_g, v_ln1_b, v_w_ffn_in, v_w_ffn_out, v_ln2_g, v_ln2_b)))
    T, D = x.shape[-2], x.shape[-1]
    assert w_in.shape[0] == 2, "the exchange schedule below is written for two layers"

    mine = [{n: w[n][l].astype(_MXU) for n in _DENSE} for l in range(2)]
    W = [dict(), dict()]
    gws = [dict(), dict()]
    slots, final = {}, {}

    def gather(keys):
        sizes = [mine[l][n].size for l, n in keys]
        passed, fractions = 0, []
        for s in sizes:
            passed += s
            fractions.append(0.15 + 0.6 * passed / sum(sizes))

        def done(outs):
            for (l, n), o in zip(keys, outs):
                W[l][n] = o
        return _gather_plan([mine[l][n] for l, n in keys], fractions), done

    def scatter(keys):
        comm = _scatter_plan([gws[l][n] for l, n in keys], [_owner(key) for key in keys])
        return comm, lambda outs: slots.update(zip(keys, outs))

    def share(keys):
        reduced = [_sum_slots(slots[key], f"sum_grad_{key[1]}_{key[0]}") for key in keys]
        comm = _share_plan(reduced, [_owner(key) for key in keys])
        return comm, lambda outs: final.update(zip(keys, outs))

    def both(first, second):
        (ca, da), (cb, db) = first, second
        na = len(ca.out_shapes)
        return _join(ca, cb), lambda outs: (da(outs[:na]), db(outs[na:]))

    comm, done = gather([(0, "w_in")])
    done(_comm_only(comm, "gather_first"))
    plans = {key: functools.partial(gather, keys) for key, keys in _GATHER.items()}
    for key, keys in _SCATTER.items():
        plans[key] = functools.partial(scatter, keys)
    for key, scattered_under in _SHARE.items():
        handed = functools.partial(share, _SCATTER[scattered_under])
        carried = plans.get(key)
        plans[key] = handed if carried is None else (lambda carried=carried, handed=handed: both(carried(), handed()))
    small = {n: w[n] for n in _SMALL}
    sq, dx, _, gss = _local_step(x.reshape(T, D), loss_target.reshape(T, D), W, small, _Plans(plans), gws)

    comm, done = share(_SCATTER["in_proj_bwd_0"])
    done(_comm_only(comm, "share_last"))
    total = _all_reduce_small(_pack_small(gss, sq))
    small_grads, sq_all = _unpack_small(total, {n: w[n].shape for n in _SMALL})
    loss = 0.5 * sq_all / D

    grad, delta, new_m, new_v = {}, {}, {}, {}
    for n in names:
        if n in _DENSE:
            updated = _adamw_layers(w[n], [final[(l, n)] for l in range(2)], m[n], v[n], f"adamw_{n}")
        else:
            updated = _adamw(w[n], small_grads[n], m[n], v[n], f"adamw_{n}")
        grad[n], delta[n], new_m[n], new_v[n] = updated
    return (loss, dx.reshape(x.shape), *[grad[n] for n in names], *[delta[n] for n in names],
            *[new_m[n] for n in names], *[new_v[n] for n in names])
```

```python
import functools

import jax
import jax.numpy as jnp
from jax import lax
from jax.experimental import pallas as pl
from jax.experimental.pallas import tpu as pltpu

_MXU = jnp.bfloat16
_ACT = jnp.bfloat16
_F32 = jnp.float32

_HEAD = 64
_CHUNK = 64
_LANES = 128
_TQ = 128
_BAND_TILES = 5
_BIAS_TILES = 9
_REL_CLIP = 256
_LN_EPS = 1e-5
_MASKED = -1e30
_EXP_ZERO_BELOW = -87.34
_SB_WINDOW = 2
_SB_SUBTILES = 4
_BAND_SUBTILES = 8
_VMEM_LIMIT = 56 * 1024 * 1024
_GRAD_ACC_BYTES = 12 * 1024 * 1024

_LR, _B1, _B2, _EPS, _WD, _STEP = 0.001, 0.9, 0.999, 1e-08, 0.01, 10

_MESH = pl.DeviceIdType.MESH


def _dot(a, b):
    return jnp.dot(a, b, preferred_element_type=_F32)


def _dot_nt(a, b):
    return lax.dot_general(a, b, (((1,), (1,)), ((), ())), preferred_element_type=_F32)


def _dot_tn(a, b):
    return lax.dot_general(a, b, (((0,), (0,)), ((), ())), preferred_element_type=_F32)


def _cparams(*sem):
    return pltpu.CompilerParams(dimension_semantics=sem, vmem_limit_bytes=_VMEM_LIMIT)


def _rows(t, c):
    return pl.BlockSpec((t, c), lambda i: (i, 0))


def _whole(shape):
    return pl.BlockSpec(shape, lambda i: tuple(0 for _ in shape))


_ANY = pl.BlockSpec(memory_space=pl.ANY)


def _load_cols(w_hbm, w_vmem, sem):
    n = w_hbm.shape[-1]
    cps = [pltpu.make_async_copy(w_hbm.at[k], w_vmem.at[:, pl.ds(k * n, n)], sem.at[k]) for k in range(4)]
    for cp in cps:
        cp.start()
    for cp in cps:
        cp.wait()


def _load_rows(w_hbm, w_vmem, sem):
    r = w_hbm.shape[-2]
    cps = [pltpu.make_async_copy(w_hbm.at[k], w_vmem.at[pl.ds(k * r, r), :], sem.at[k]) for k in range(4)]
    for cp in cps:
        cp.start()
    for cp in cps:
        cp.wait()


def _ln_stats(u):
    mu = jnp.mean(u, axis=-1, keepdims=True)
    xc = u - mu
    var = jnp.mean(xc * xc, axis=-1, keepdims=True)
    rstd = lax.rsqrt(var + _LN_EPS)
    return xc * rstd, rstd


def _ln_bwd(u, dy, gamma):
    xhat, rstd = _ln_stats(u)
    dxh = dy * gamma
    m1 = jnp.mean(dxh, axis=-1, keepdims=True)
    m2 = jnp.mean(dxh * xhat, axis=-1, keepdims=True)
    du = rstd * (dxh - m1 - xhat * m2)
    return du, jnp.sum(dy * xhat, axis=0, keepdims=True), jnp.sum(dy, axis=0, keepdims=True), xhat


def _divisor_tile(n, cap):
    best = None
    for t in range(_LANES, min(n, cap) + 1, _LANES):
        if n % t == 0:
            best = t
    return best or n


class _Comm:
    def __init__(self, inputs, out_shapes, sems, run, aliases=None):
        self.inputs, self.out_shapes, self.sems, self.run = list(inputs), list(out_shapes), list(sems), run
        self.aliases = aliases or {}


def _call(kern, comm, *, name, grid, in_specs, out_specs, out_shape, scratch_shapes, args, semantics):
    in_specs, out_specs, out_shape, scratch_shapes = list(in_specs), list(out_specs), list(out_shape), list(scratch_shapes)
    if comm is None:
        outs = pl.pallas_call(kern, name=name, grid=grid, in_specs=in_specs, out_specs=out_specs, out_shape=out_shape,
                              scratch_shapes=scratch_shapes, compiler_params=_cparams(*semantics))(*args)
        return list(outs), []
    n_in, n_out, n_scr = len(in_specs), len(out_specs), len(scratch_shapes)
    ci, co = len(comm.inputs), len(comm.out_shapes)
    nsteps = functools.reduce(lambda a, b: a * b, grid, 1)

    def fused(*refs):
        a, b = n_in, n_in + ci
        c, d = b + n_out, b + n_out + co
        e = d + n_scr
        step = pl.program_id(0)
        for ax in range(1, len(grid)):
            step = step * grid[ax] + pl.program_id(ax)
        comm.run(step, nsteps, refs[a:b], refs[c:d], refs[e:])
        kern(*refs[:a], *refs[b:c], *refs[d:e])

    outs = pl.pallas_call(
        fused, name=name, grid=grid, in_specs=in_specs + [_ANY] * ci, out_specs=out_specs + [_ANY] * co,
        out_shape=out_shape + comm.out_shapes, scratch_shapes=scratch_shapes + comm.sems,
        input_output_aliases={n_in + i: n_out + o for i, o in comm.aliases.items()},
        compiler_params=_cparams(*("arbitrary" for _ in grid)))(*args, *comm.inputs)
    return list(outs[:n_out]), list(outs[n_out:])


def _comm_only(comm, name):
    def body(*refs):
        ci, co = len(comm.inputs), len(comm.out_shapes)
        comm.run(0, 1, refs[:ci], refs[ci:ci + co], refs[ci + co:])

    outs = pl.pallas_call(body, name=name, in_specs=[_ANY] * len(comm.inputs), out_specs=[_ANY] * len(comm.out_shapes),
                          out_shape=comm.out_shapes, scratch_shapes=comm.sems,
                          input_output_aliases=dict(comm.aliases))(*comm.inputs)
    return list(outs)


def _in_proj(x, w_in, layer):
    T, D = x.shape
    N = 4 * w_in.shape[-1]
    NQ = N - 2 * D
    tm = 512

    def kern(x_ref, w_hbm, hq_ref, hg_ref, xb_ref, w_v, sem):
        @pl.when(pl.program_id(0) == 0)
        def _():
            _load_cols(w_hbm, w_v, sem)

        xb = x_ref[...].astype(_MXU)
        hq_ref[...] = _dot(xb, w_v[:, :NQ]).astype(hq_ref.dtype)
        hg_ref[...] = _dot(xb, w_v[:, NQ:]).astype(hg_ref.dtype)
        xb_ref[...] = xb.astype(xb_ref.dtype)

    return pl.pallas_call(
        kern, name=f"in_proj_{layer}", grid=(T // tm,),
        in_specs=[_rows(tm, D), _ANY],
        out_specs=[_rows(tm, NQ), _rows(tm, 2 * D), _rows(tm, D)],
        out_shape=[jax.ShapeDtypeStruct((T, NQ), _ACT), jax.ShapeDtypeStruct((T, 2 * D), _ACT),
                   jax.ShapeDtypeStruct((T, D), _ACT)],
        scratch_shapes=[pltpu.VMEM((D, N), w_in.dtype), pltpu.SemaphoreType.DMA((4,))],
        compiler_params=_cparams("arbitrary"),
    )(x, w_in)


def _mix_fwd(oa, ob, hg, x, wpa, wpb, wo, bg, gamma, beta, alpha, layer, comm=None):
    T, D = x.shape
    WA, WB = oa.shape[1], ob.shape[1]
    tm = 512

    def kern(oa_ref, ob_ref, hg_ref, x_ref, bg_ref, g_ref, b_ref, wpa_h, wpb_h, wo_h,
             x1_ref, u1_ref, pre_ref, wpa_v, wpb_v, wo_v, sa, sb, so):
        @pl.when(pl.program_id(0) == 0)
        def _():
            _load_cols(wpa_h, wpa_v, sa)
            _load_cols(wpb_h, wpb_v, sb)
            _load_rows(wo_h, wo_v, so)

        ya = _dot(oa_ref[...].astype(_MXU), wpa_v[...])
        yb = _dot(ob_ref[...].astype(_MXU), wpb_v[...])
        hgv = hg_ref[...].astype(_F32)
        bgv = bg_ref[...]
        ga = jax.nn.sigmoid(hgv[:, :D] + bgv[:, :D])
        gb = jax.nn.sigmoid(hgv[:, D:] + bgv[:, D:])
        pre = ga * ya + gb * yb
        mix = _dot(pre.astype(_MXU), wo_v[...])
        u = alpha * x_ref[...] + mix
        xhat, _ = _ln_stats(u)
        x1_ref[...] = xhat * g_ref[...] + b_ref[...]
        u1_ref[...] = u
        pre_ref[...] = pre.astype(pre_ref.dtype)

    return _call(
        kern, comm, name=f"mix_fwd_{layer}", grid=(T // tm,),
        in_specs=[_rows(tm, WA), _rows(tm, WB), _rows(tm, 2 * D), _rows(tm, D),
                  _whole((1, 2 * D)), _whole((1, D)), _whole((1, D)), _ANY, _ANY, _ANY],
        out_specs=[_rows(tm, D)] * 3,
        out_shape=[jax.ShapeDtypeStruct((T, D), _F32), jax.ShapeDtypeStruct((T, D), _F32),
                   jax.ShapeDtypeStruct((T, D), _ACT)],
        scratch_shapes=[pltpu.VMEM((WA, D), wpa.dtype), pltpu.VMEM((WB, D), wpb.dtype), pltpu.VMEM((D, D), wo.dtype),
                        pltpu.SemaphoreType.DMA((4,)), pltpu.SemaphoreType.DMA((4,)), pltpu.SemaphoreType.DMA((4,))],
        args=(oa, ob, hg, x, bg, gamma, beta, wpa, wpb, wo), semantics=("arbitrary",))


def _ffn_fwd(x1, wfi, wfo, gamma, beta, alpha, layer, comm=None):
    T, D = x1.shape
    F2 = 4 * wfi.shape[-1]
    F = F2 // 2
    tm = 256
    fc = F // 2

    def kern(x_ref, g_ref, b_ref, wi_h, wo_h, x2_ref, u2_ref, act_ref, gu_ref, xb_ref, wi_v, wo_v, si, so):
        @pl.when(pl.program_id(0) == 0)
        def _():
            _load_cols(wi_h, wi_v, si)
            _load_rows(wo_h, wo_v, so)

        x = x_ref[...]
        xb = x.astype(_MXU)
        xb_ref[...] = xb.astype(xb_ref.dtype)
        ffn = jnp.zeros((tm, D), _F32)
        for c in range(2):
            g = _dot(xb, wi_v[:, c * fc:(c + 1) * fc])
            u = _dot(xb, wi_v[:, F + c * fc:F + (c + 1) * fc])
            act = g * jax.nn.sigmoid(g) * u
            ab = act.astype(_MXU)
            ffn = ffn + _dot(ab, wo_v[c * fc:(c + 1) * fc, :])
            act_ref[:, c * fc:(c + 1) * fc] = ab.astype(act_ref.dtype)
            gu_ref[:, c * fc:(c + 1) * fc] = g.astype(gu_ref.dtype)
            gu_ref[:, F + c * fc:F + (c + 1) * fc] = u.astype(gu_ref.dtype)
        uu = alpha * x + ffn
        xhat, _ = _ln_stats(uu)
        x2_ref[...] = xhat * g_ref[...] + b_ref[...]
        u2_ref[...] = uu

    return _call(
        kern, comm, name=f"ffn_fwd_{layer}", grid=(T // tm,),
        in_specs=[_rows(tm, D), _whole((1, D)), _whole((1, D)), _ANY, _ANY],
        out_specs=[_rows(tm, D), _rows(tm, D), _rows(tm, F), _rows(tm, F2), _rows(tm, D)],
        out_shape=[jax.ShapeDtypeStruct((T, D), _F32), jax.ShapeDtypeStruct((T, D), _F32),
                   jax.ShapeDtypeStruct((T, F), _ACT), jax.ShapeDtypeStruct((T, F2), _ACT),
                   jax.ShapeDtypeStruct((T, D), _ACT)],
        scratch_shapes=[pltpu.VMEM((D, F2), wfi.dtype), pltpu.VMEM((F, D), wfo.dtype),
                        pltpu.SemaphoreType.DMA((4,)), pltpu.SemaphoreType.DMA((4,))],
        args=(x1, gamma, beta, wfi, wfo), semantics=("arbitrary",))


def _ffn_bwd(u2, dy_or_target, gu, gamma, beta, wfi, wfo, alpha, layer, last):
    T, D = u2.shape
    F2 = gu.shape[1]
    F = F2 // 2
    tm = 256
    fc = F // 2

    def kern(u_ref, dy_ref, gu_ref, g_ref, b_ref, wi_h, wo_h, dx_ref, dub_ref, dgu_ref, st_ref, wi_v, wo_v, si, so):
        @pl.when(pl.program_id(0) == 0)
        def _():
            _load_cols(wi_h, wi_v, si)
            _load_rows(wo_h, wo_v, so)
            st_ref[...] = jnp.zeros_like(st_ref)

        gam = g_ref[...]
        u = u_ref[...]
        if last:
            xhat0, _ = _ln_stats(u)
            err = xhat0 * gam + b_ref[...] - dy_ref[...]
            dy = err * (1.0 / D)
            st_ref[2:3, :] += jnp.sum(err * err, axis=0, keepdims=True)
        else:
            dy = dy_ref[...]
        du, dgam, dbet, _ = _ln_bwd(u, dy, gam)
        st_ref[0:1, :] += dgam
        st_ref[1:2, :] += dbet
        dub = du.astype(_MXU)
        dub_ref[...] = dub.astype(dub_ref.dtype)
        dx = alpha * du
        for c in range(2):
            dact = _dot_nt(dub, wo_v[c * fc:(c + 1) * fc, :])
            g = gu_ref[:, c * fc:(c + 1) * fc].astype(_F32)
            uu = gu_ref[:, F + c * fc:F + (c + 1) * fc].astype(_F32)
            sg = jax.nn.sigmoid(g)
            dg = (dact * uu * (sg * (1.0 + g * (1.0 - sg)))).astype(_MXU)
            dup = (dact * (g * sg)).astype(_MXU)
            dgu_ref[:, c * fc:(c + 1) * fc] = dg.astype(dgu_ref.dtype)
            dgu_ref[:, F + c * fc:F + (c + 1) * fc] = dup.astype(dgu_ref.dtype)
            dx = dx + _dot_nt(dg, wi_v[:, c * fc:(c + 1) * fc]) + _dot_nt(dup, wi_v[:, F + c * fc:F + (c + 1) * fc])
        dx_ref[...] = dx

    return pl.pallas_call(
        kern, name=f"ffn_bwd_{layer}", grid=(T // tm,),
        in_specs=[_rows(tm, D), _rows(tm, D), _rows(tm, F2), _whole((1, D)), _whole((1, D)), _ANY, _ANY],
        out_specs=[_rows(tm, D), _rows(tm, D), _rows(tm, F2), _whole((8, D))],
        out_shape=[jax.ShapeDtypeStruct((T, D), _F32), jax.ShapeDtypeStruct((T, D), _ACT),
                   jax.ShapeDtypeStruct((T, F2), _ACT), jax.ShapeDtypeStruct((8, D), _F32)],
        scratch_shapes=[pltpu.VMEM((D, F2), wfi.dtype), pltpu.VMEM((F, D), wfo.dtype),
                        pltpu.SemaphoreType.DMA((4,)), pltpu.SemaphoreType.DMA((4,))],
        compiler_params=_cparams("arbitrary"),
    )(u2, dy_or_target, gu, gamma, beta, wfi, wfo)


def _residual_nt(res, res_scale, d, w, name, comm=None):
    T, K = res.shape
    N = d.shape[1]
    tm = 512

    def kern(r_ref, d_ref, w_hbm, o_ref, w_v, sem):
        @pl.when(pl.program_id(0) == 0)
        def _():
            _load_cols(w_hbm, w_v, sem)

        o_ref[...] = res_scale * r_ref[...] + _dot_nt(d_ref[...].astype(_MXU), w_v[...])

    outs, extra = _call(
        kern, comm, name=name, grid=(T // tm,),
        in_specs=[_rows(tm, K), _rows(tm, N), _ANY], out_specs=[_rows(tm, K)],
        out_shape=[jax.ShapeDtypeStruct((T, K), _F32)],
        scratch_shapes=[pltpu.VMEM((K, N), w.dtype), pltpu.SemaphoreType.DMA((4,))],
        args=(res, d, w), semantics=("arbitrary",))
    return outs[0], extra


def _mix_bwd(u1, dx1, oa, ob, hg, wpa, wpb, wo, bg, gamma, layer):
    T, D = u1.shape
    WA, WB = wpa.shape[-2], wpb.shape[-2]
    tm = 512

    def kern(u_ref, dx_ref, oa_ref, ob_ref, hg_ref, bg_ref, g_ref, wpa_h, wpb_h, wo_h,
             du_ref, dub_ref, dya_ref, dyb_ref, dhg_ref, doa_ref, dob_ref, st_ref,
             wpa_v, wpb_v, wo_v, sa, sb, so):
        @pl.when(pl.program_id(0) == 0)
        def _():
            _load_cols(wpa_h, wpa_v, sa)
            _load_cols(wpb_h, wpb_v, sb)
            _load_rows(wo_h, wo_v, so)
            st_ref[...] = jnp.zeros_like(st_ref)

        du, dgam, dbet, _ = _ln_bwd(u_ref[...], dx_ref[...], g_ref[...])
        st_ref[1:2, :D] += dgam
        st_ref[1:2, D:] += dbet
        du_ref[...] = du
        dub = du.astype(_MXU)
        dub_ref[...] = dub.astype(dub_ref.dtype)
        dpre = _dot_nt(dub, wo_v[...])
        hgv = hg_ref[...].astype(_F32)
        bgv = bg_ref[...]
        ga = jax.nn.sigmoid(hgv[:, :D] + bgv[:, :D])
        gb = jax.nn.sigmoid(hgv[:, D:] + bgv[:, D:])
        dya = (dpre * ga).astype(_MXU)
        dyb = (dpre * gb).astype(_MXU)
        dsa = dpre * _dot(oa_ref[...].astype(_MXU), wpa_v[...]) * (ga * (1.0 - ga))
        dsb = dpre * _dot(ob_ref[...].astype(_MXU), wpb_v[...]) * (gb * (1.0 - gb))
        st_ref[0:1, :D] += jnp.sum(dsa, axis=0, keepdims=True)
        st_ref[0:1, D:] += jnp.sum(dsb, axis=0, keepdims=True)
        dya_ref[...] = dya.astype(dya_ref.dtype)
        dyb_ref[...] = dyb.astype(dyb_ref.dtype)
        dhg_ref[:, :D] = dsa.astype(dhg_ref.dtype)
        dhg_ref[:, D:] = dsb.astype(dhg_ref.dtype)
        doa_ref[...] = _dot_nt(dya, wpa_v[...]).astype(doa_ref.dtype)
        dob_ref[...] = _dot_nt(dyb, wpb_v[...]).astype(dob_ref.dtype)

    return pl.pallas_call(
        kern, name=f"mix_bwd_{layer}", grid=(T // tm,),
        in_specs=[_rows(tm, D), _rows(tm, D), _rows(tm, WA), _rows(tm, WB), _rows(tm, 2 * D), _whole((1, 2 * D)),
                  _whole((1, D)), _ANY, _ANY, _ANY],
        out_specs=[_rows(tm, D)] * 4 + [_rows(tm, 2 * D), _rows(tm, WA), _rows(tm, WB), _whole((8, 2 * D))],
        out_shape=[jax.ShapeDtypeStruct((T, D), _F32)] + [jax.ShapeDtypeStruct((T, D), _ACT)] * 3
        + [jax.ShapeDtypeStruct((T, 2 * D), _ACT), jax.ShapeDtypeStruct((T, WA), _ACT),
           jax.ShapeDtypeStruct((T, WB), _ACT), jax.ShapeDtypeStruct((8, 2 * D), _F32)],
        scratch_shapes=[pltpu.VMEM((WA, D), wpa.dtype), pltpu.VMEM((WB, D), wpb.dtype), pltpu.VMEM((D, D), wo.dtype),
                        pltpu.SemaphoreType.DMA((4,)), pltpu.SemaphoreType.DMA((4,)), pltpu.SemaphoreType.DMA((4,))],
        compiler_params=_cparams("arbitrary"),
    )(u1, dx1, oa, ob, hg, bg, gamma, wpa, wpb, wo)


def _grad_w(a, b, *, col_shards, name, comm=None):
    T, M = a.shape
    N = b.shape[1]
    tk = 1024 if T % 1024 == 0 else 512
    n = N // 4 if col_shards else N
    whole = M * N * 4 <= _GRAD_ACC_BYTES
    tn = N if whole else (n if col_shards else _divisor_tile(N, _GRAD_ACC_BYTES // (4 * M)))
    nk = T // tk

    def kern(a_ref, b_ref, o_ref, acc):
        k = pl.program_id(1)

        @pl.when(k == 0)
        def _():
            acc[...] = jnp.zeros_like(acc)

        acc[...] += _dot_tn(a_ref[...].astype(_MXU), b_ref[...].astype(_MXU))

        @pl.when(k == nk - 1)
        def _():
            if col_shards and whole:
                for s in range(4):
                    o_ref[s] = acc[:, s * n:(s + 1) * n].astype(o_ref.dtype)
            else:
                o_ref[...] = acc[...].astype(o_ref.dtype)

    if col_shards:
        out_spec = (pl.BlockSpec((4, M, n), lambda j, k: (0, 0, 0)) if whole
                    else pl.BlockSpec((None, M, n), lambda j, k: (j, 0, 0)))
        out_shape = jax.ShapeDtypeStruct((4, M, n), _ACT)
    else:
        out_spec = pl.BlockSpec((M, tn), lambda j, k: (0, j))
        out_shape = jax.ShapeDtypeStruct((M, N), _ACT)
    outs, extra = _call(
        kern, comm, name=name, grid=(N // tn, nk),
        in_specs=[pl.BlockSpec((tk, M), lambda j, k: (k, 0)), pl.BlockSpec((tk, tn), lambda j, k: (k, j))],
        out_specs=[out_spec], out_shape=[out_shape], scratch_shapes=[pltpu.VMEM((M, tn), _F32)],
        args=(a, b), semantics=("parallel", "arbitrary"))
    return outs[0], extra


def _bias_tiles(rel):
    H = rel.shape[0]
    span = _TQ * _BAND_TILES - 1
    edge = span - _REL_CLIP
    gvec = jnp.concatenate([jnp.broadcast_to(rel[:, :1], (H, edge)), rel, jnp.broadcast_to(rel[:, -1:], (H, edge))], axis=1)
    width = _BIAS_TILES * _TQ
    period = width + _TQ
    tiled = jnp.broadcast_to(jnp.pad(gvec[:, ::-1], ((0, 0), (0, 1)))[:, None, :], (H, _TQ, period))
    rows = tiled.reshape(H, _TQ * period)[:, :_TQ * (period - 1)].reshape(H, _TQ, period - 1)[:, :, _TQ - 1:]
    r = jnp.arange(_TQ)[:, None]
    u = jnp.arange(width)[None, :]
    d = 4 * _TQ + r - u
    rm = r % _CHUNK
    valid = (d >= rm - (_CHUNK - 1)) & (d <= rm + 8 * _CHUNK)
    tiles = jnp.where(valid[None], rows, _MASKED)
    return tiles.reshape(H // 2, 2 * _TQ, _BIAS_TILES, _TQ).transpose(0, 2, 1, 3)


def _fold_bias_grad(db):
    H = 2 * db.shape[0]
    width = _BIAS_TILES * _TQ
    period = width + _TQ
    x = jnp.pad(db.transpose(0, 2, 1, 3).reshape(H, _TQ, width), ((0, 0), (0, 0), (_TQ - 1, 0)))
    skew = jnp.pad(x.reshape(H, _TQ * (period - 1)), ((0, 0), (0, _TQ))).reshape(H, _TQ, period)
    dg = skew.sum(axis=1)[:, :period - 1][:, ::-1]
    span = _TQ * _BAND_TILES - 1
    edge = span - _REL_CLIP
    mid = dg[:, edge:edge + 2 * _REL_CLIP + 1]
    lo = dg[:, :edge].sum(axis=1)
    hi = dg[:, edge + 2 * _REL_CLIP + 1:].sum(axis=1)
    return mid.at[:, 0].add(lo).at[:, -1].add(hi)


def _band_window(i):
    j0 = jnp.maximum(i - (_BAND_TILES - 1), 0)
    return j0, (_BAND_TILES - 1) - (i - j0)


def _head_masks():
    lane = lax.broadcasted_iota(jnp.int32, (1, _LANES), 1)
    return [(lane // _HEAD) == hh for hh in range(2)]


def _stack_heads(x, masks):
    return jnp.concatenate([jnp.where(m, x, jnp.zeros_like(x)) for m in masks], axis=0)


def _unstack_heads(y, masks):
    return jnp.where(masks[0], y[:_TQ], y[_TQ:])


def _scaled(q):
    return q * jnp.asarray(_HEAD ** -0.5, q.dtype)


def _band_probs(q2, k_ref, b_ref, j0, boff):
    s = []
    for j in range(_BAND_TILES):
        kj = k_ref[pl.ds(pl.multiple_of((j0 + j) * _TQ, _TQ), _TQ), :]
        s.append(_dot_nt(q2, kj) + b_ref[boff + j])
    m = jnp.max(functools.reduce(jnp.maximum, s), axis=-1, keepdims=True)
    p = [jnp.exp(x - m) for x in s]
    l = jnp.sum(functools.reduce(lambda a, b: a + b, p), axis=-1, keepdims=True)
    return p, 1.0 / l


def _qkv_specs(T, cb, npair, tq=_TQ):
    return [pl.BlockSpec((tq, _LANES), lambda h, i: (i, cb + h)),
            pl.BlockSpec((T, _LANES), lambda h, i: (0, cb + npair + h)),
            pl.BlockSpec((T, _LANES), lambda h, i: (0, cb + 2 * npair + h))]


def _attn_a_fwd(hq, bias, col0, width, layer, comm=None):
    T = hq.shape[0]
    npair = width // _LANES
    nsub = _BAND_SUBTILES
    tq = nsub * _TQ

    def kern(q_ref, k_ref, v_ref, b_ref, o_ref):
        masks = _head_masks()
        q = _scaled(q_ref[...])
        for s in range(nsub):
            part = slice(s * _TQ, (s + 1) * _TQ)
            j0, boff = _band_window(nsub * pl.program_id(1) + s)
            p, inv = _band_probs(_stack_heads(q[part], masks), k_ref, b_ref, j0, boff)
            o = jnp.zeros((2 * _TQ, _LANES), _F32)
            for j in range(_BAND_TILES):
                vj = v_ref[pl.ds(pl.multiple_of((j0 + j) * _TQ, _TQ), _TQ), :]
                o = o + _dot(p[j].astype(_MXU), vj)
            o_ref[part, :] = _unstack_heads(o * inv, masks).astype(o_ref.dtype)

    outs, extra = _call(
        kern, comm, name=f"band_attn_fwd_{layer}", grid=(npair, T // tq),
        in_specs=_qkv_specs(T, col0 // _LANES, npair, tq)
        + [pl.BlockSpec((None, _BIAS_TILES, 2 * _TQ, _TQ), lambda h, i: (h, 0, 0, 0))],
        out_specs=[pl.BlockSpec((tq, _LANES), lambda h, i: (i, h))],
        out_shape=[jax.ShapeDtypeStruct((T, width), _ACT)], scratch_shapes=[],
        args=(hq, hq, hq, bias), semantics=("arbitrary", "arbitrary"))
    return outs[0], extra


def _attn_a_bwd(hq, bias, do, col0, width, layer, comm=None):
    T = hq.shape[0]
    npair = width // _LANES
    nsub = _BAND_SUBTILES
    tq = nsub * _TQ
    nq = T // tq
    scale = _HEAD ** -0.5

    def kern(q_ref, k_ref, v_ref, b_ref, do_ref, dq_ref, dk_ref, dv_ref, db_ref, dk_acc, dv_acc):
        i = pl.program_id(1)

        @pl.when(i == 0)
        def _():
            dk_acc[...] = jnp.zeros_like(dk_acc)
            dv_acc[...] = jnp.zeros_like(dv_acc)
            db_ref[...] = jnp.zeros_like(db_ref)

        masks = _head_masks()
        q = _scaled(q_ref[...])
        do_t = do_ref[...]
        for s in range(nsub):
            part = slice(s * _TQ, (s + 1) * _TQ)
            j0, boff = _band_window(nsub * i + s)
            q2 = _stack_heads(q[part], masks)
            do2 = _stack_heads(do_t[part], masks).astype(_MXU)
            p, inv = _band_probs(q2, k_ref, b_ref, j0, boff)
            rows = [pl.ds(pl.multiple_of((j0 + j) * _TQ, _TQ), _TQ) for j in range(_BAND_TILES)]
            p = [x * inv for x in p]
            dp = [_dot_nt(do2, v_ref[rows[j], :]) for j in range(_BAND_TILES)]
            delta = jnp.sum(functools.reduce(lambda a, b: a + b, [p[j] * dp[j] for j in range(_BAND_TILES)]),
                            axis=-1, keepdims=True)
            dq = jnp.zeros((2 * _TQ, _LANES), _F32)
            for j in range(_BAND_TILES):
                ds = p[j] * (dp[j] - delta)
                db_ref[boff + j] += ds
                dsb = ds.astype(_MXU)
                dq = dq + _dot(dsb, k_ref[rows[j], :])
                dk_acc[rows[j], :] += _dot_tn(dsb, q2)
                dv_acc[rows[j], :] += _dot_tn(p[j].astype(_MXU), do2)
            dq_ref[part, :] = (_unstack_heads(dq, masks) * scale).astype(dq_ref.dtype)

        @pl.when(i == nq - 1)
        def _():
            dk_ref[...] = dk_acc[...].astype(dk_ref.dtype)
            dv_ref[...] = dv_acc[...].astype(dv_ref.dtype)

    strip = pl.BlockSpec((None, _BIAS_TILES, 2 * _TQ, _TQ), lambda h, i: (h, 0, 0, 0))
    tile = pl.BlockSpec((tq, _LANES), lambda h, i: (i, h))
    column = pl.BlockSpec((T, _LANES), lambda h, i: (0, h))
    outs, extra = _call(
        kern, comm, name=f"band_attn_bwd_{layer}", grid=(npair, nq),
        in_specs=_qkv_specs(T, col0 // _LANES, npair, tq) + [strip, tile],
        out_specs=[tile, column, column, strip],
        out_shape=[jax.ShapeDtypeStruct((T, width), _ACT)] * 3
        + [jax.ShapeDtypeStruct((npair, _BIAS_TILES, 2 * _TQ, _TQ), _F32)],
        scratch_shapes=[pltpu.VMEM((T, _LANES), _F32), pltpu.VMEM((T, _LANES), _F32)],
        args=(hq, hq, hq, bias, do), semantics=("arbitrary", "arbitrary"))
    return outs, extra


def _suffix_matrix():
    r = lax.broadcasted_iota(jnp.int32, (_TQ, _TQ), 0)
    c = lax.broadcasted_iota(jnp.int32, (_TQ, _TQ), 1)
    r2 = lax.broadcasted_iota(jnp.int32, (2 * _TQ, _TQ), 0)
    c2 = lax.broadcasted_iota(jnp.int32, (2 * _TQ, _TQ), 1)
    return (r > c).astype(_MXU), c2 - (r2 & (_TQ - 1))


def _suffix_sums(xs, tri):
    n, k = xs[0].shape[0], len(xs)
    his = [x.astype(_MXU) for x in xs]
    los = [(x - h.astype(_F32)).astype(_MXU) for x, h in zip(xs, his)]
    y = _dot(jnp.concatenate(his + los, axis=0), tri)
    return [y[j * n:(j + 1) * n] + y[(k + j) * n:(k + j + 1) * n] for j in range(k)]


def _stick_tiles(tiles, rel, carry_l, tri):
    zs = [_dot_nt(qs, kj) for qs, kj, _, _ in tiles]
    Ls, masks = [], []
    for z, (_, _, jj, _) in zip(zs, tiles):
        nsp = -(jnp.maximum(z, 0.0) + jnp.log(1.0 + jnp.exp(-jnp.abs(z))))
        if isinstance(jj, int):
            mask = (rel < 0) if jj == 0 else None
        else:
            mask = rel < jnp.where(jj == 0, 0, _TQ)
        Ls.append(nsp if mask is None else jnp.where(mask, nsp, 0.0))
        masks.append(mask)
    carry_l = list(carry_l)
    ws = []
    for z, L, suffix, mask, (_, _, _, sub) in zip(zs, Ls, _suffix_sums(Ls, tri), masks, tiles):
        w = jnp.exp(z + L + suffix + carry_l[sub])
        ws.append(w if mask is None else jnp.where(mask, w, 0.0))
        carry_l[sub] = carry_l[sub] + jnp.sum(L, axis=-1, keepdims=True)
    return zs, Ls, ws, masks, carry_l


def _sweep(i, step, zero):
    nsub = _SB_SUBTILES

    def window():
        tiles = [(s, jj) for jj in range(_SB_WINDOW) for s in range(nsub)]
        return tuple((jnp.int32(_SB_WINDOW),) + c for c in step(tiles, [zero] * nsub))

    start = lax.cond(i >= -(-(_SB_WINDOW - 1) // nsub), window, lambda: tuple((jnp.int32(0),) + zero for _ in range(nsub)))
    outs = []
    for s in range(nsub):
        def done(c, s=s):
            return jnp.logical_or(c[0] > nsub * i + s, jnp.max(c[1]) < _EXP_ZERO_BELOW)

        def more(c, s=s):
            carries = [None] * nsub
            carries[s] = c[1:]
            return (c[0] + 1,) + step([(s, c[0])], carries)[s]

        outs.append(lax.while_loop(lambda c, done=done: jnp.logical_not(done(c)), more, start[s]))
    return outs


def _sb_fwd(hq, col0, width, layer, comm=None):
    T = hq.shape[0]
    npair = width // _LANES
    nsub = _SB_SUBTILES
    tq = nsub * _TQ

    def kern(q_ref, k_ref, v_ref, o_ref):
        i = pl.program_id(1)
        masks = _head_masks()
        tri, rel = _suffix_matrix()
        q = _scaled(q_ref[...])
        q2 = [_stack_heads(q[s * _TQ:(s + 1) * _TQ], masks) for s in range(nsub)]

        def step(tiles, carries):
            rows = [pl.ds(pl.multiple_of((nsub * i + s - jj) * _TQ, _TQ), _TQ) for s, jj in tiles]
            cls = [None if c is None else c[0] for c in carries]
            accs = [None if c is None else c[1] for c in carries]
            _, _, ws, _, cls = _stick_tiles([(q2[s], k_ref[r, :], jj, s) for (s, jj), r in zip(tiles, rows)], rel, cls, tri)
            for w, r, (s, _) in zip(ws, rows, tiles):
                accs[s] = accs[s] + _dot(w.astype(_MXU), v_ref[r, :])
            return [None if c is None else (cls[s], accs[s]) for s, c in enumerate(carries)]

        outs = _sweep(i, step, (jnp.zeros((2 * _TQ, 1), _F32), jnp.zeros((2 * _TQ, _LANES), _F32)))
        for s in range(nsub):
            o_ref[s * _TQ:(s + 1) * _TQ, :] = _unstack_heads(outs[s][2], masks)

    outs, extra = _call(
        kern, comm, name=f"stick_attn_fwd_{layer}", grid=(npair, T // tq),
        in_specs=_qkv_specs(T, col0 // _LANES, npair, tq),
        out_specs=[pl.BlockSpec((tq, _LANES), lambda h, i: (i, h))],
        out_shape=[jax.ShapeDtypeStruct((T, width), _F32)], scratch_shapes=[],
        args=(hq, hq, hq), semantics=("arbitrary", "arbitrary"))
    return outs[0], extra


def _sb_bwd(hq, o, do, col0, width, layer, comm=None):
    T = hq.shape[0]
    npair = width // _LANES
    nsub = _SB_SUBTILES
    tq = nsub * _TQ
    nq = T // tq
    scale = _HEAD ** -0.5

    def kern(q_ref, k_ref, v_ref, o_ref, do_ref, dq_ref, dk_ref, dv_ref, dk_acc, dv_acc):
        i = pl.program_id(1)

        @pl.when(i == 0)
        def _():
            dk_acc[...] = jnp.zeros_like(dk_acc)
            dv_acc[...] = jnp.zeros_like(dv_acc)

        masks = _head_masks()
        tri, rel = _suffix_matrix()
        q = _scaled(q_ref[...])
        do_t = do_ref[...]
        prod = do_t.astype(_F32) * o_ref[...]
        part = [slice(s * _TQ, (s + 1) * _TQ) for s in range(nsub)]
        q2 = [_stack_heads(q[p], masks) for p in part]
        do2 = [_stack_heads(do_t[p], masks).astype(_MXU) for p in part]
        dsum = [jnp.sum(_stack_heads(prod[p], masks), axis=-1, keepdims=True) for p in part]

        def step(tiles, carries):
            rows = [pl.ds(pl.multiple_of((nsub * i + s - jj) * _TQ, _TQ), _TQ) for s, jj in tiles]
            kjs = [k_ref[r, :] for r in rows]
            cls, cgs, dqs = ([None if c is None else c[n] for c in carries] for n in range(3))
            zs, Ls, ws, tile_masks, cls = _stick_tiles([(q2[s], kj, jj, s) for (s, jj), kj in zip(tiles, kjs)], rel, cls, tri)
            wbs = [w.astype(_MXU) for w in ws]
            gs = [wb.astype(_F32) * _dot_nt(do2[s], v_ref[r, :]) for wb, r, (s, _) in zip(wbs, rows, tiles)]
            for z, L, g, later, mask, wb, kj, r, (s, _) in zip(zs, Ls, gs, _suffix_sums(gs, tri), tile_masks, wbs, kjs,
                                                               rows, tiles):
                dz = g - jnp.exp(z + L) * (dsum[s] - (later + cgs[s]))
                if mask is not None:
                    dz = jnp.where(mask, dz, 0.0)
                dzb = dz.astype(_MXU)
                dk_acc[r, :] += _dot_tn(dzb, q2[s])
                dv_acc[r, :] += _dot_tn(wb, do2[s])
                dqs[s] = dqs[s] + _dot(dzb, kj)
                cgs[s] = cgs[s] + jnp.sum(g, axis=-1, keepdims=True)
            return [None if c is None else (cls[s], cgs[s], dqs[s]) for s, c in enumerate(carries)]

        zc = jnp.zeros((2 * _TQ, 1), _F32)
        outs = _sweep(i, step, (zc, zc, jnp.zeros((2 * _TQ, _LANES), _F32)))
        for s in range(nsub):
            dq_ref[part[s], :] = (_unstack_heads(outs[s][3], masks) * scale).astype(dq_ref.dtype)

        @pl.when(i == nq - 1)
        def _():
            dk_ref[...] = dk_acc[...].astype(dk_ref.dtype)
            dv_ref[...] = dv_acc[...].astype(dv_ref.dtype)

    tile_spec = pl.BlockSpec((tq, _LANES), lambda h, i: (i, h))
    column = pl.BlockSpec((T, _LANES), lambda h, i: (0, h))
    outs, extra = _call(
        kern, comm, name=f"stick_attn_bwd_{layer}", grid=(npair, nq),
        in_specs=_qkv_specs(T, col0 // _LANES, npair, tq) + [tile_spec, tile_spec],
        out_specs=[tile_spec, column, column],
        out_shape=[jax.ShapeDtypeStruct((T, width), _ACT)] * 3,
        scratch_shapes=[pltpu.VMEM((T, _LANES), _F32), pltpu.VMEM((T, _LANES), _F32)],
        args=(hq, hq, hq, o, do), semantics=("arbitrary", "arbitrary"))
    return outs, extra


_DENSE = ("w_in", "w_proj_a", "w_proj_b", "w_out", "w_ffn_in", "w_ffn_out")
_COL_SHARDED = {"w_in": True, "w_proj_a": True, "w_proj_b": True, "w_out": False, "w_ffn_in": True, "w_ffn_out": False}
_SMALL = ("b_gate", "rel_bias", "ln1_g", "ln1_b", "ln2_g", "ln2_b")


class _Plans:
    def __init__(self, plans=None):
        self.plans = plans or {}

    def start(self, key):
        if key not in self.plans:
            return None, None
        return self.plans[key]()

    @staticmethod
    def finish(done, extra):
        if done is not None:
            done(extra)


def _layer_fwd(x, W, small, l, alpha, plans):
    WA = small["rel_bias"].shape[1] * _HEAD
    row = lambda v: v[l].reshape(1, -1)
    hq, hg, xb = _in_proj(x, W["w_in"], l)
    WB = (hq.shape[1] - 3 * WA) // 3
    bias = _bias_tiles(small["rel_bias"][l])
    comm, done = plans.start(f"band_fwd_{l}")
    oa, extra = _attn_a_fwd(hq, bias, 0, WA, l, comm)
    plans.finish(done, extra)
    comm, done = plans.start(f"stick_fwd_{l}")
    ob, extra = _sb_fwd(hq, 3 * WA, WB, l, comm)
    plans.finish(done, extra)
    comm, done = plans.start(f"mix_fwd_{l}")
    (x1, u1, pre), extra = _mix_fwd(oa, ob, hg, x, W["w_proj_a"], W["w_proj_b"], W["w_out"], row(small["b_gate"]),
                                            row(small["ln1_g"]), row(small["ln1_b"]), alpha, l, comm)
    plans.finish(done, extra)
    comm, done = plans.start(f"ffn_fwd_{l}")
    (x2, u2, act, gu, x1b), extra = _ffn_fwd(x1, W["w_ffn_in"], W["w_ffn_out"], row(small["ln2_g"]),
                                             row(small["ln2_b"]), alpha, l, comm)
    plans.finish(done, extra)
    return x2, dict(xb=xb, hq=hq, hg=hg, bias=bias, oa=oa, ob=ob, x1b=x1b, u1=u1, pre=pre, u2=u2, act=act, gu=gu)


def _layer_bwd(dy_or_target, S, W, small, l, last, alpha, plans, gw):
    D = S["xb"].shape[1]
    WA, WB = S["oa"].shape[1], S["ob"].shape[1]
    row = lambda v: v[l].reshape(1, -1)

    def blocks(g, n):
        return g if _COL_SHARDED[n] else g.reshape(4, g.shape[0] // 4, g.shape[1])

    dx1, du2b, dgu, st2 = _ffn_bwd(S["u2"], dy_or_target, S["gu"], row(small["ln2_g"]), row(small["ln2_b"]),
                                   W["w_ffn_in"], W["w_ffn_out"], alpha, l, last)
    gw["w_ffn_in"] = blocks(_grad_w(S["x1b"], dgu, col_shards=True, name=f"grad_w_ffn_in_{l}")[0], "w_ffn_in")
    gw["w_ffn_out"] = blocks(_grad_w(S["act"], du2b, col_shards=False, name=f"grad_w_ffn_out_{l}")[0], "w_ffn_out")
    du1, du1b, dya, dyb, dhg, doa, dob, st1 = _mix_bwd(S["u1"], dx1, S["oa"], S["ob"], S["hg"], W["w_proj_a"],
                                                       W["w_proj_b"], W["w_out"], row(small["b_gate"]),
                                                       row(small["ln1_g"]), l)
    gw["w_out"] = blocks(_grad_w(S["pre"], du1b, col_shards=False, name=f"grad_w_out_{l}")[0], "w_out")
    gw["w_proj_a"] = blocks(_grad_w(S["oa"], dya, col_shards=True, name=f"grad_w_proj_a_{l}")[0], "w_proj_a")
    gw["w_proj_b"] = blocks(_grad_w(S["ob"], dyb, col_shards=True, name=f"grad_w_proj_b_{l}")[0], "w_proj_b")
    comm, done = plans.start(f"band_bwd_{l}")
    (dqa, dka, dva, dbias), extra = _attn_a_bwd(S["hq"], S["bias"], doa, 0, WA, l, comm)
    plans.finish(done, extra)
    comm, done = plans.start(f"stick_bwd_{l}")
    (dqb, dkb, dvb), extra = _sb_bwd(S["hq"], S["ob"], dob, 3 * WA, WB, l, comm)
    plans.finish(done, extra)
    dh = jnp.concatenate([dqa, dka, dva, dqb, dkb, dvb, dhg], axis=1)
    comm, done = plans.start(f"grad_w_in_{l}")
    g, extra = _grad_w(S["xb"], dh, col_shards=True, name=f"grad_w_in_{l}", comm=comm)
    gw["w_in"] = blocks(g, "w_in")
    plans.finish(done, extra)
    comm, done = plans.start(f"in_proj_bwd_{l}")
    dx, extra = _residual_nt(du1, alpha, dh, W["w_in"], f"in_proj_bwd_{l}", comm)
    plans.finish(done, extra)
    gs = dict(b_gate=st1[0], rel_bias=_fold_bias_grad(dbias), ln1_g=st1[1, :D], ln1_b=st1[1, D:],
              ln2_g=st2[0], ln2_b=st2[1])
    return dx, gs, st2[2]


def _local_step(x, target, W, small, plans=None, gws=None):
    depth = len(W)
    alpha = float((2 * depth) ** 0.25)
    plans = plans or _Plans()
    gws = gws if gws is not None else [dict() for _ in range(depth)]
    saved = []
    h = x
    for l in range(depth):
        h, S = _layer_fwd(h, W[l], small, l, alpha, plans)
        saved.append(S)
    gss = [None] * depth
    d = target
    sq = None
    for l in reversed(range(depth)):
        d, gss[l], sq_l = _layer_bwd(d, saved[l], W[l], small, l, l == depth - 1, alpha, plans, gws[l])
        if l == depth - 1:
            sq = sq_l
    return sq, d, gws, gss


def _place():
    return lax.axis_index("x"), lax.axis_index("y"), lax.axis_index("c")


def _remote(src, dst, send_sem, recv_sem, to):
    return pltpu.make_async_remote_copy(src_ref=src, dst_ref=dst, send_sem=send_sem, recv_sem=recv_sem,
                                        device_id=to, device_id_type=_MESH)


def _half(ref, hc):
    kh = ref.shape[0] // 2
    return ref.at[pl.ds(pl.multiple_of(hc * kh, 16), kh), :]


def _gather_plan(blocks, fractions):
    nt = len(blocks)

    def run(step, nsteps, ins, outs, sems):
        send_sems, recv_sems, loc_sems = sems
        x, y, c = _place()
        k = 2 * x + y
        me, sibling = (x, y, c), (x, y, 1 - c)
        chips = [(1 - x, y), (x, 1 - y), (1 - x, 1 - y)]
        chip_k = [2 * cx + cy for cx, cy in chips]

        def ici(t, s, owner_k, to, src=None):
            dst = _half(outs[t].at[owner_k], c)
            return _remote(dst if src is None else src, dst, send_sems.at[t, s], recv_sems.at[t, s], to)

        def passed(t, s, hc, to):
            blk = _half(outs[t].at[chip_k[s]], hc)
            return _remote(blk, blk, send_sems.at[t, 3 + s], recv_sems.at[t, 3 + s], to)

        def local(t):
            return pltpu.make_async_copy(ins[t], outs[t].at[k], loc_sems.at[t])

        @pl.when(step == 0)
        def _():
            for t in range(nt):
                local(t).start()
                for s, chip in enumerate(chips):
                    ici(t, s, k, (*chip, c), src=_half(ins[t], c)).start()

        for t in range(nt):
            @pl.when(step == min(nsteps - 1, int(fractions[t] * nsteps)))
            def _():
                for s in range(3):
                    ici(t, s, chip_k[s], me).wait_recv()
                    passed(t, s, c, sibling).start()

        @pl.when(step == nsteps - 1)
        def _():
            for t in range(nt):
                for s, chip in enumerate(chips):
                    passed(t, s, 1 - c, me).wait_recv()
            for t in range(nt):
                for s, chip in enumerate(chips):
                    ici(t, s, k, (*chip, c), src=_half(ins[t], c)).wait_send()
                    passed(t, s, c, sibling).wait_send()
                local(t).wait()

    return _Comm(blocks, [jax.ShapeDtypeStruct((4,) + b.shape, b.dtype) for b in blocks],
                 [pltpu.SemaphoreType.DMA((nt, 6)), pltpu.SemaphoreType.DMA((nt, 6)), pltpu.SemaphoreType.DMA((nt,))], run)


def _scatter_plan(grads, owners):
    nt = len(grads)

    def run(step, nsteps, ins, outs, sems):
        send_sems, recv_sems, loc_sems = sems
        x, y, c = _place()
        me = 4 * x + 2 * y + c

        def target(r):
            tx = 1 - x if r & 2 else x
            ty = 1 - y if r & 1 else y
            return tx, ty

        def send(t, r):
            tx, ty = target(r)
            return _remote(ins[t].at[2 * tx + ty], outs[t].at[me], send_sems.at[t, r], recv_sems.at[t, 2 * r + c],
                           (tx, ty, owners[t]))

        def local(t):
            return pltpu.make_async_copy(ins[t].at[2 * x + y], outs[t].at[me], loc_sems.at[t])

        @pl.when(step == 0)
        def _():
            for t in range(nt):
                @pl.when(c == owners[t])
                def _():
                    local(t).start()

                @pl.when(c != owners[t])
                def _():
                    send(t, 0).start()

                for r in range(1, 4):
                    send(t, r).start()

        @pl.when(step == nsteps - 1)
        def _():
            for t in range(nt):
                @pl.when(c == owners[t])
                def _():
                    for r in range(4):
                        sx, sy = target(r)
                        for cs in range(2):
                            if r == 0 and cs == owners[t]:
                                continue
                            src_dev = 4 * sx + 2 * sy + cs
                            _remote(ins[t].at[0], outs[t].at[src_dev], send_sems.at[t, r], recv_sems.at[t, 2 * r + cs],
                                    (x, y, c)).wait_recv()
                    local(t).wait()

                @pl.when(c != owners[t])
                def _():
                    send(t, 0).wait_send()

                for r in range(1, 4):
                    send(t, r).wait_send()

    return _Comm(grads, [jax.ShapeDtypeStruct((8,) + g.shape[1:], g.dtype) for g in grads],
                 [pltpu.SemaphoreType.DMA((nt, 4)), pltpu.SemaphoreType.DMA((nt, 8)), pltpu.SemaphoreType.DMA((nt,))], run)


def _share_plan(reduced, owners):
    nt = len(reduced)

    def run(step, nsteps, ins, outs, sems):
        del ins
        send_sems, recv_sems = sems
        x, y, c = _place()

        def give(t, to):
            return _remote(outs[t], outs[t], send_sems.at[t], recv_sems.at[t], to)

        @pl.when(step == 0)
        def _():
            for t in range(nt):
                @pl.when(c == owners[t])
                def _():
                    give(t, (x, y, 1 - c)).start()

        @pl.when(step == nsteps - 1)
        def _():
            for t in range(nt):
                @pl.when(c == owners[t])
                def _():
                    give(t, (x, y, 1 - c)).wait_send()

                @pl.when(c != owners[t])
                def _():
                    give(t, (x, y, c)).wait_recv()

    return _Comm(reduced, [jax.ShapeDtypeStruct(r.shape, r.dtype) for r in reduced],
                 [pltpu.SemaphoreType.DMA((nt,)), pltpu.SemaphoreType.DMA((nt,))], run,
                 aliases={t: t for t in range(nt)})


def _join(a, b):
    ni, no, ns = len(a.inputs), len(a.out_shapes), len(a.sems)

    def run(step, nsteps, ins, outs, sems):
        a.run(step, nsteps, ins[:ni], outs[:no], sems[:ns])
        b.run(step, nsteps, ins[ni:], outs[no:], sems[ns:])

    aliases = dict(a.aliases)
    aliases.update({ni + i: no + o for i, o in b.aliases.items()})
    return _Comm(a.inputs + b.inputs, a.out_shapes + b.out_shapes, a.sems + b.sems, run, aliases)


def _peer(x, y, c, r):
    px = 1 - x if r & 4 else x
    py = 1 - y if r & 2 else y
    pc = 1 - c if r & 1 else c
    return (px, py, pc), 4 * px + 2 * py + pc


def _sum_slots(st, name):
    _, K, n = st.shape
    tr = next(t for t in (256, 128, 64, 32, 16) if K % t == 0)

    def kern(s_ref, o_ref):
        acc = s_ref[0].astype(_F32)
        for d in range(1, 8):
            acc = acc + s_ref[d].astype(_F32)
        o_ref[...] = acc.astype(o_ref.dtype)

    return pl.pallas_call(
        kern, name=name, grid=(K // tr,),
        in_specs=[pl.BlockSpec((8, tr, n), lambda i: (0, i, 0))], out_specs=_rows(tr, n),
        out_shape=jax.ShapeDtypeStruct((K, n), _ACT),
        compiler_params=_cparams("parallel"),
    )(st)


def _all_reduce_small(p):
    R = p.shape[0]

    def body(p_ref, o_ref, stage, send_sems, recv_sems):
        x, y, c = _place()
        me = 4 * x + 2 * y + c
        stage[me] = p_ref[...]
        sent = []
        for r in range(1, 8):
            to, _ = _peer(x, y, c, r)
            cp = _remote(p_ref, stage.at[me], send_sems.at[r - 1], recv_sems.at[r - 1], to)
            cp.start()
            sent.append(cp)
        for r in range(1, 8):
            _, src_dev = _peer(x, y, c, r)
            _remote(p_ref, stage.at[src_dev], send_sems.at[r - 1], recv_sems.at[r - 1], (x, y, c)).wait_recv()
        acc = stage[0]
        for d in range(1, 8):
            acc = acc + stage[d]
        o_ref[...] = acc
        for cp in sent:
            cp.wait_send()

    vm = pl.BlockSpec(memory_space=pltpu.VMEM)
    return pl.pallas_call(
        body, name="all_reduce_small",
        in_specs=[vm], out_specs=vm,
        out_shape=jax.ShapeDtypeStruct((R, _LANES), _F32),
        scratch_shapes=[pltpu.VMEM((8, R, _LANES), _F32), pltpu.SemaphoreType.DMA((7,)), pltpu.SemaphoreType.DMA((7,))],
    )(p)


def _adamw_update(gv, w_ref, m_ref, v_ref, gf_ref, d_ref, nm_ref, nv_ref):
    nm = _B1 * m_ref[...] + (1.0 - _B1) * gv
    nv = _B2 * v_ref[...] + (1.0 - _B2) * (gv * gv)
    m_hat = nm / (1.0 - _B1 ** _STEP)
    v_hat = nv / (1.0 - _B2 ** _STEP)
    gf_ref[...] = gv
    d_ref[...] = -_LR * (m_hat / (jnp.sqrt(v_hat) + _EPS) + _WD * w_ref[...])
    nm_ref[...] = nm
    nv_ref[...] = nv


def _adamw_layers(w, g_layers, m, v, name):
    _, K, n = w.shape
    tr = next(t for t in (256, 128, 64, 32, 16) if K % t == 0)

    def kern(w_ref, g0_ref, g1_ref, m_ref, v_ref, *out_refs):
        first = pl.program_id(0) == 0
        gv = jnp.where(first, g0_ref[...].astype(_F32), g1_ref[...].astype(_F32))
        _adamw_update(gv, w_ref, m_ref, v_ref, *out_refs)

    stacked = pl.BlockSpec((None, tr, n), lambda l, i: (l, i, 0))
    layer = pl.BlockSpec((tr, n), lambda l, i: (i, 0))
    return tuple(pl.pallas_call(
        kern, name=name, grid=(2, K // tr),
        in_specs=[stacked, layer, layer, stacked, stacked], out_specs=[stacked] * 4,
        out_shape=[jax.ShapeDtypeStruct(w.shape, _F32)] * 4,
        compiler_params=_cparams("parallel", "parallel"),
    )(w, g_layers[0], g_layers[1], m, v))


def _adamw(w, g, m, v, name):
    shape = w.shape
    w2, g2, m2, v2 = (a.reshape(-1, shape[-1]) for a in (w, g, m, v))
    R, C = w2.shape
    tr = next((t for t in (256, 128, 64, 32, 16) if R % t == 0), R)

    def kern(w_ref, g_ref, m_ref, v_ref, *out_refs):
        _adamw_update(g_ref[...].astype(_F32), w_ref, m_ref, v_ref, *out_refs)

    outs = pl.pallas_call(
        kern, name=name, grid=(R // tr,),
        in_specs=[_rows(tr, C)] * 4, out_specs=[_rows(tr, C)] * 4,
        out_shape=[jax.ShapeDtypeStruct((R, C), _F32)] * 4,
        compiler_params=_cparams("parallel"),
    )(w2, g2, m2, v2)
    return tuple(o.reshape(shape) for o in outs)


def _pack_small(gss, sq):
    parts = [gss[l][n].reshape(-1) for n in _SMALL for l in range(len(gss))] + [jnp.sum(sq).reshape(1)]
    flat = jnp.concatenate(parts)
    rows = -(-flat.shape[0] // (8 * _LANES)) * 8
    return jnp.pad(flat, (0, rows * _LANES - flat.shape[0])).reshape(rows, _LANES)


def _unpack_small(total, shapes):
    flat = total.reshape(-1)
    out, off = {}, 0
    for n in _SMALL:
        layers = []
        for _ in range(shapes[n][0]):
            size = 1
            for s in shapes[n][1:]:
                size *= s
            layers.append(flat[off:off + size].reshape(shapes[n][1:]))
            off += size
        out[n] = jnp.stack(layers)
    return out, flat[off]


_GATHER = {
    "band_fwd_0": [(0, "w_proj_a"), (0, "w_proj_b"), (0, "w_out"), (0, "w_ffn_out")],
    "stick_fwd_0": [(0, "w_ffn_in")],
    "mix_fwd_0": [(1, "w_in")],
    "ffn_fwd_0": [(1, "w_proj_a"), (1, "w_proj_b"), (1, "w_out"), (1, "w_ffn_in"), (1, "w_ffn_out")],
}
_SCATTER = {
    "band_bwd_1": [(1, "w_ffn_in"), (1, "w_ffn_out")],
    "stick_bwd_1": [(1, "w_proj_a"), (1, "w_proj_b"), (1, "w_out")],
    "band_bwd_0": [(1, "w_in"), (0, "w_ffn_in")],
    "stick_bwd_0": [(0, "w_ffn_out"), (0, "w_proj_a"), (0, "w_proj_b"), (0, "w_out")],
    "in_proj_bwd_0": [(0, "w_in")],
}
_SHARE = {"stick_bwd_1": "band_bwd_1", "band_bwd_0": "stick_bwd_1", "stick_bwd_0": "band_bwd_0", "grad_w_in_0": "stick_bwd_0"}


def _owner(key):
    del key
    return 1


def kernel(x, w_in, b_gate, rel_bias, w_proj_a, w_proj_b, w_out, ln1_g, ln1_b, w_ffn_in, w_ffn_out, ln2_g, ln2_b, loss_target, m_w_in, m_b_gate, m_rel_bias, m_w_proj_a, m_w_proj_b, m_w_out, m_ln1_g, m_ln1_b, m_w_ffn_in, m_w_ffn_out, m_ln2_g, m_ln2_b, v_w_in, v_b_gate, v_rel_bias, v_w_proj_a, v_w_proj_b, v_w_out, v_ln1_g, v_ln1_b, v_w_ffn_in, v_w_ffn_out, v_ln2_g, v_ln2_b):
    names = ("w_in", "b_gate", "rel_bias", "w_proj_a", "w_proj_b", "w_out", "ln1_g", "ln1_b", "w_ffn_in", "w_ffn_out", "ln2_g", "ln2_b")
    w = dict(zip(names, (w_in, b_gate, rel_bias, w_proj_a, w_proj_b, w_out, ln1_g, ln1_b, w_ffn_in, w_ffn_out, ln2_g, ln2_b)))
    m = dict(zip(names, (m_w_in, m_b_gate, m_rel_bias, m_w_proj_a, m_w_proj_b, m_w_out, m_ln1_g, m_ln1_b, m_w_ffn_in, m_w_ffn_out, m_ln2_g, m_ln2_b)))
    v = dict(zip(names, (v_w_in, v_b_gate, v_rel_bias, v_w_proj_a, v_w_proj_b, v_w_out, v_ln1_g, v_ln1_b, v_w_ffn_in, v_w_ffn_out, v_ln2_g, v_ln2_b)))
    T, D = x.shape[-2], x.shape[-1]
    assert w_in.shape[0] == 2, "the exchange schedule below is written for two layers"

    mine = [{n: w[n][l].astype(_MXU) for n in _DENSE} for l in range(2)]
    W = [dict(), dict()]
    gws = [dict(), dict()]
    slots, final = {}, {}

    def gather(keys):
        sizes = [mine[l][n].size for l, n in keys]
        passed, fractions = 0, []
        for s in sizes:
            passed += s
            fractions.append(0.15 + 0.6 * passed / sum(sizes))

        def done(outs):
            for (l, n), o in zip(keys, outs):
                W[l][n] = o
        return _gather_plan([mine[l][n] for l, n in keys], fractions), done

    def scatter(keys):
        comm = _scatter_plan([gws[l][n] for l, n in keys], [_owner(key) for key in keys])
        return comm, lambda outs: slots.update(zip(keys, outs))

    def share(keys):
        reduced = [_sum_slots(slots[key], f"sum_grad_{key[1]}_{key[0]}") for key in keys]
        comm = _share_plan(reduced, [_owner(key) for key in keys])
        return comm, lambda outs: final.update(zip(keys, outs))

    def both(first, second):
        (ca, da), (cb, db) = first, second
        na = len(ca.out_shapes)
        return _join(ca, cb), lambda outs: (da(outs[:na]), db(outs[na:]))

    comm, done = gather([(0, "w_in")])
    done(_comm_only(comm, "gather_first"))
    plans = {key: functools.partial(gather, keys) for key, keys in _GATHER.items()}
    for key, keys in _SCATTER.items():
        plans[key] = functools.partial(scatter, keys)
    for key, scattered_under in _SHARE.items():
        handed = functools.partial(share, _SCATTER[scattered_under])
        carried = plans.get(key)
        plans[key] = handed if carried is None else (lambda carried=carried, handed=handed: both(carried(), handed()))
    small = {n: w[n] for n in _SMALL}
    sq, dx, _, gss = _local_step(x.reshape(T, D), loss_target.reshape(T, D), W, small, _Plans(plans), gws)

    comm, done = share(_SCATTER["in_proj_bwd_0"])
    done(_comm_only(comm, "share_last"))
    total = _all_reduce_small(_pack_small(gss, sq))
    small_grads, sq_all = _unpack_small(total, {n: w[n].shape for n in _SMALL})
    loss = 0.5 * sq_all / D

    grad, delta, new_m, new_v = {}, {}, {}, {}
    for n in names:
        if n in _DENSE:
            updated = _adamw_layers(w[n], [final[(l, n)] for l in range(2)], m[n], v[n], f"adamw_{n}")
        else:
            updated = _adamw(w[n], small_grads[n], m[n], v[n], f"adamw_{n}")
        grad[n], delta[n], new_m[n], new_v[n] = updated
    return (loss, dx.reshape(x.shape), *[grad[n] for n in names], *[delta[n] for n in names],
            *[new_m[n] for n in names], *[new_v[n] for n in names])
```

```python
import functools

import jax
import jax.numpy as jnp
from jax import lax
from jax.experimental import pallas as pl
from jax.experimental.pallas import tpu as pltpu

_MXU = jnp.bfloat16
_ACT = jnp.bfloat16
_F32 = jnp.float32

_HEAD = 64
_CHUNK = 64
_LANES = 128
_TQ = 128
_BAND_TILES = 5
_BIAS_TILES = 9
_REL_CLIP = 256
_LN_EPS = 1e-5
_MASKED = -1e30
_EXP_ZERO_BELOW = -87.34
_SB_WINDOW = 2
_SB_SUBTILES = 4
_BAND_SUBTILES = 8
_VMEM_LIMIT = 56 * 1024 * 1024
_GRAD_ACC_BYTES = 12 * 1024 * 1024

_LR, _B1, _B2, _EPS, _WD, _STEP = 0.001, 0.9, 0.999, 1e-08, 0.01, 10

_MESH = pl.DeviceIdType.MESH


def _dot(a, b):
    return jnp.dot(a, b, preferred_element_type=_F32)


def _dot_nt(a, b):
    return lax.dot_general(a, b, (((1,), (1,)), ((), ())), preferred_element_type=_F32)


def _dot_tn(a, b):
    return lax.dot_general(a, b, (((0,), (0,)), ((), ())), preferred_element_type=_F32)


def _cparams(*sem):
    return pltpu.CompilerParams(dimension_semantics=sem, vmem_limit_bytes=_VMEM_LIMIT)


def _rows(t, c):
    return pl.BlockSpec((t, c), lambda i: (i, 0))


def _whole(shape):
    return pl.BlockSpec(shape, lambda i: tuple(0 for _ in shape))


_ANY = pl.BlockSpec(memory_space=pl.ANY)


def _load_cols(w_hbm, w_vmem, sem):
    n = w_hbm.shape[-1]
    cps = [pltpu.make_async_copy(w_hbm.at[k], w_vmem.at[:, pl.ds(k * n, n)], sem.at[k]) for k in range(4)]
    for cp in cps:
        cp.start()
    for cp in cps:
        cp.wait()


def _load_rows(w_hbm, w_vmem, sem):
    r = w_hbm.shape[-2]
    cps = [pltpu.make_async_copy(w_hbm.at[k], w_vmem.at[pl.ds(k * r, r), :], sem.at[k]) for k in range(4)]
    for cp in cps:
        cp.start()
    for cp in cps:
        cp.wait()


def _ln_stats(u):
    mu = jnp.mean(u, axis=-1, keepdims=True)
    xc = u - mu
    var = jnp.mean(xc * xc, axis=-1, keepdims=True)
    rstd = lax.rsqrt(var + _LN_EPS)
    return xc * rstd, rstd


def _ln_bwd(u, dy, gamma):
    xhat, rstd = _ln_stats(u)
    dxh = dy * gamma
    m1 = jnp.mean(dxh, axis=-1, keepdims=True)
    m2 = jnp.mean(dxh * xhat, axis=-1, keepdims=True)
    du = rstd * (dxh - m1 - xhat * m2)
    return du, jnp.sum(dy * xhat, axis=0, keepdims=True), jnp.sum(dy, axis=0, keepdims=True), xhat


def _divisor_tile(n, cap):
    best = None
    for t in range(_LANES, min(n, cap) + 1, _LANES):
        if n % t == 0:
            best = t
    return best or n


class _Comm:
    def __init__(self, inputs, out_shapes, sems, run, aliases=None):
        self.inputs, self.out_shapes, self.sems, self.run = list(inputs), list(out_shapes), list(sems), run
        self.aliases = aliases or {}


def _call(kern, comm, *, name, grid, in_specs, out_specs, out_shape, scratch_shapes, args, semantics):
    in_specs, out_specs, out_shape, scratch_shapes = list(in_specs), list(out_specs), list(out_shape), list(scratch_shapes)
    if comm is None:
        outs = pl.pallas_call(kern, name=name, grid=grid, in_specs=in_specs, out_specs=out_specs, out_shape=out_shape,
                              scratch_shapes=scratch_shapes, compiler_params=_cparams(*semantics))(*args)
        return list(outs), []
    n_in, n_out, n_scr = len(in_specs), len(out_specs), len(scratch_shapes)
    ci, co = len(comm.inputs), len(comm.out_shapes)
    nsteps = functools.reduce(lambda a, b: a * b, grid, 1)

    def fused(*refs):
        a, b = n_in, n_in + ci
        c, d = b + n_out, b + n_out + co
        e = d + n_scr
        step = pl.program_id(0)
        for ax in range(1, len(grid)):
            step = step * grid[ax] + pl.program_id(ax)
        comm.run(step, nsteps, refs[a:b], refs[c:d], refs[e:])
        kern(*refs[:a], *refs[b:c], *refs[d:e])

    outs = pl.pallas_call(
        fused, name=name, grid=grid, in_specs=in_specs + [_ANY] * ci, out_specs=out_specs + [_ANY] * co,
        out_shape=out_shape + comm.out_shapes, scratch_shapes=scratch_shapes + comm.sems,
        input_output_aliases={n_in + i: n_out + o for i, o in comm.aliases.items()},
        compiler_params=_cparams(*("arbitrary" for _ in grid)))(*args, *comm.inputs)
    return list(outs[:n_out]), list(outs[n_out:])


def _comm_only(comm, name):
    def body(*refs):
        ci, co = len(comm.inputs), len(comm.out_shapes)
        comm.run(0, 1, refs[:ci], refs[ci:ci + co], refs[ci + co:])

    outs = pl.pallas_call(body, name=name, in_specs=[_ANY] * len(comm.inputs), out_specs=[_ANY] * len(comm.out_shapes),
                          out_shape=comm.out_shapes, scratch_shapes=comm.sems,
                          input_output_aliases=dict(comm.aliases))(*comm.inputs)
    return list(outs)


def _in_proj(x, w_in, layer, comm=None):
    T, D = x.shape
    N = 4 * w_in.shape[-1]
    NQ = N - 2 * D
    tm = 512

    def kern(x_ref, w_hbm, hq_ref, hg_ref, xb_ref, w_v, sem):
        @pl.when(pl.program_id(0) == 0)
        def _():
            _load_cols(w_hbm, w_v, sem)

        xb = x_ref[...].astype(_MXU)
        hq_ref[...] = _dot(xb, w_v[:, :NQ]).astype(hq_ref.dtype)
        hg_ref[...] = _dot(xb, w_v[:, NQ:]).astype(hg_ref.dtype)
        xb_ref[...] = xb.astype(xb_ref.dtype)

    return _call(
        kern, comm, name=f"in_proj_{layer}", grid=(T // tm,),
        in_specs=[_rows(tm, D), _ANY],
        out_specs=[_rows(tm, NQ), _rows(tm, 2 * D), _rows(tm, D)],
        out_shape=[jax.ShapeDtypeStruct((T, NQ), _ACT), jax.ShapeDtypeStruct((T, 2 * D), _ACT),
                   jax.ShapeDtypeStruct((T, D), _ACT)],
        scratch_shapes=[pltpu.VMEM((D, N), w_in.dtype), pltpu.SemaphoreType.DMA((4,))],
        args=(x, w_in), semantics=("arbitrary",))


def _mix_fwd(oa, ob, hg, x, wpa, wpb, wo, bg, gamma, beta, alpha, layer, comm=None):
    T, D = x.shape
    WA, WB = oa.shape[1], ob.shape[1]
    tm = 512

    def kern(oa_ref, ob_ref, hg_ref, x_ref, bg_ref, g_ref, b_ref, wpa_h, wpb_h, wo_h,
             x1_ref, u1_ref, pre_ref, wpa_v, wpb_v, wo_v, sa, sb, so):
        @pl.when(pl.program_id(0) == 0)
        def _():
            _load_cols(wpa_h, wpa_v, sa)
            _load_cols(wpb_h, wpb_v, sb)
            _load_rows(wo_h, wo_v, so)

        ya = _dot(oa_ref[...].astype(_MXU), wpa_v[...])
        yb = _dot(ob_ref[...].astype(_MXU), wpb_v[...])
        hgv = hg_ref[...].astype(_F32)
        bgv = bg_ref[...]
        ga = jax.nn.sigmoid(hgv[:, :D] + bgv[:, :D])
        gb = jax.nn.sigmoid(hgv[:, D:] + bgv[:, D:])
        pre = ga * ya + gb * yb
        mix = _dot(pre.astype(_MXU), wo_v[...])
        u = alpha * x_ref[...] + mix
        xhat, _ = _ln_stats(u)
        x1_ref[...] = xhat * g_ref[...] + b_ref[...]
        u1_ref[...] = u
        pre_ref[...] = pre.astype(pre_ref.dtype)

    return _call(
        kern, comm, name=f"mix_fwd_{layer}", grid=(T // tm,),
        in_specs=[_rows(tm, WA), _rows(tm, WB), _rows(tm, 2 * D), _rows(tm, D),
                  _whole((1, 2 * D)), _whole((1, D)), _whole((1, D)), _ANY, _ANY, _ANY],
        out_specs=[_rows(tm, D)] * 3,
        out_shape=[jax.ShapeDtypeStruct((T, D), _F32), jax.ShapeDtypeStruct((T, D), _F32),
                   jax.ShapeDtypeStruct((T, D), _ACT)],
        scratch_shapes=[pltpu.VMEM((WA, D), wpa.dtype), pltpu.VMEM((WB, D), wpb.dtype), pltpu.VMEM((D, D), wo.dtype),
                        pltpu.SemaphoreType.DMA((4,)), pltpu.SemaphoreType.DMA((4,)), pltpu.SemaphoreType.DMA((4,))],
        args=(oa, ob, hg, x, bg, gamma, beta, wpa, wpb, wo), semantics=("arbitrary",))


def _ffn_fwd(x1, wfi, wfo, gamma, beta, alpha, layer, comm=None):
    T, D = x1.shape
    F2 = 4 * wfi.shape[-1]
    F = F2 // 2
    tm = 256
    fc = F // 2

    def kern(x_ref, g_ref, b_ref, wi_h, wo_h, x2_ref, u2_ref, act_ref, gu_ref, xb_ref, wi_v, wo_v, si, so):
        @pl.when(pl.program_id(0) == 0)
        def _():
            _load_cols(wi_h, wi_v, si)
            _load_rows(wo_h, wo_v, so)

        x = x_ref[...]
        xb = x.astype(_MXU)
        xb_ref[...] = xb.astype(xb_ref.dtype)
        ffn = jnp.zeros((tm, D), _F32)
        for c in range(2):
            g = _dot(xb, wi_v[:, c * fc:(c + 1) * fc])
            u = _dot(xb, wi_v[:, F + c * fc:F + (c + 1) * fc])
            act = g * jax.nn.sigmoid(g) * u
            ab = act.astype(_MXU)
            ffn = ffn + _dot(ab, wo_v[c * fc:(c + 1) * fc, :])
            act_ref[:, c * fc:(c + 1) * fc] = ab.astype(act_ref.dtype)
            gu_ref[:, c * fc:(c + 1) * fc] = g.astype(gu_ref.dtype)
            gu_ref[:, F + c * fc:F + (c + 1) * fc] = u.astype(gu_ref.dtype)
        uu = alpha * x + ffn
        xhat, _ = _ln_stats(uu)
        x2_ref[...] = xhat * g_ref[...] + b_ref[...]
        u2_ref[...] = uu

    return _call(
        kern, comm, name=f"ffn_fwd_{layer}", grid=(T // tm,),
        in_specs=[_rows(tm, D), _whole((1, D)), _whole((1, D)), _ANY, _ANY],
        out_specs=[_rows(tm, D), _rows(tm, D), _rows(tm, F), _rows(tm, F2), _rows(tm, D)],
        out_shape=[jax.ShapeDtypeStruct((T, D), _F32), jax.ShapeDtypeStruct((T, D), _F32),
                   jax.ShapeDtypeStruct((T, F), _ACT), jax.ShapeDtypeStruct((T, F2), _ACT),
                   jax.ShapeDtypeStruct((T, D), _ACT)],
        scratch_shapes=[pltpu.VMEM((D, F2), wfi.dtype), pltpu.VMEM((F, D), wfo.dtype),
                        pltpu.SemaphoreType.DMA((4,)), pltpu.SemaphoreType.DMA((4,))],
        args=(x1, gamma, beta, wfi, wfo), semantics=("arbitrary",))


def _ffn_bwd(u2, dy_or_target, gu, gamma, beta, wfi, wfo, alpha, layer, last):
    T, D = u2.shape
    F2 = gu.shape[1]
    F = F2 // 2
    tm = 256
    fc = F // 2

    def kern(u_ref, dy_ref, gu_ref, g_ref, b_ref, wi_h, wo_h, dx_ref, dub_ref, dgu_ref, st_ref, wi_v, wo_v, si, so):
        @pl.when(pl.program_id(0) == 0)
        def _():
            _load_cols(wi_h, wi_v, si)
            _load_rows(wo_h, wo_v, so)
            st_ref[...] = jnp.zeros_like(st_ref)

        gam = g_ref[...]
        u = u_ref[...]
        if last:
            xhat0, _ = _ln_stats(u)
            err = xhat0 * gam + b_ref[...] - dy_ref[...]
            dy = err * (1.0 / D)
            st_ref[2:3, :] += jnp.sum(err * err, axis=0, keepdims=True)
        else:
            dy = dy_ref[...]
        du, dgam, dbet, _ = _ln_bwd(u, dy, gam)
        st_ref[0:1, :] += dgam
        st_ref[1:2, :] += dbet
        dub = du.astype(_MXU)
        dub_ref[...] = dub.astype(dub_ref.dtype)
        dx = alpha * du
        for c in range(2):
            dact = _dot_nt(dub, wo_v[c * fc:(c + 1) * fc, :])
            g = gu_ref[:, c * fc:(c + 1) * fc].astype(_F32)
            uu = gu_ref[:, F + c * fc:F + (c + 1) * fc].astype(_F32)
            sg = jax.nn.sigmoid(g)
            dg = (dact * uu * (sg * (1.0 + g * (1.0 - sg)))).astype(_MXU)
            dup = (dact * (g * sg)).astype(_MXU)
            dgu_ref[:, c * fc:(c + 1) * fc] = dg.astype(dgu_ref.dtype)
            dgu_ref[:, F + c * fc:F + (c + 1) * fc] = dup.astype(dgu_ref.dtype)
            dx = dx + _dot_nt(dg, wi_v[:, c * fc:(c + 1) * fc]) + _dot_nt(dup, wi_v[:, F + c * fc:F + (c + 1) * fc])
        dx_ref[...] = dx

    return pl.pallas_call(
        kern, name=f"ffn_bwd_{layer}", grid=(T // tm,),
        in_specs=[_rows(tm, D), _rows(tm, D), _rows(tm, F2), _whole((1, D)), _whole((1, D)), _ANY, _ANY],
        out_specs=[_rows(tm, D), _rows(tm, D), _rows(tm, F2), _whole((8, D))],
        out_shape=[jax.ShapeDtypeStruct((T, D), _F32), jax.ShapeDtypeStruct((T, D), _ACT),
                   jax.ShapeDtypeStruct((T, F2), _ACT), jax.ShapeDtypeStruct((8, D), _F32)],
        scratch_shapes=[pltpu.VMEM((D, F2), wfi.dtype), pltpu.VMEM((F, D), wfo.dtype),
                        pltpu.SemaphoreType.DMA((4,)), pltpu.SemaphoreType.DMA((4,))],
        compiler_params=_cparams("arbitrary"),
    )(u2, dy_or_target, gu, gamma, beta, wfi, wfo)


def _residual_nt(res, res_scale, d, w, name, comm=None):
    T, K = res.shape
    N = d.shape[1]
    tm = 512

    def kern(r_ref, d_ref, w_hbm, o_ref, w_v, sem):
        @pl.when(pl.program_id(0) == 0)
        def _():
            _load_cols(w_hbm, w_v, sem)

        o_ref[...] = res_scale * r_ref[...] + _dot_nt(d_ref[...].astype(_MXU), w_v[...])

    outs, extra = _call(
        kern, comm, name=name, grid=(T // tm,),
        in_specs=[_rows(tm, K), _rows(tm, N), _ANY], out_specs=[_rows(tm, K)],
        out_shape=[jax.ShapeDtypeStruct((T, K), _F32)],
        scratch_shapes=[pltpu.VMEM((K, N), w.dtype), pltpu.SemaphoreType.DMA((4,))],
        args=(res, d, w), semantics=("arbitrary",))
    return outs[0], extra


def _mix_bwd(u1, dx1, oa, ob, hg, wpa, wpb, wo, bg, gamma, layer):
    T, D = u1.shape
    WA, WB = wpa.shape[-2], wpb.shape[-2]
    tm = 512

    def kern(u_ref, dx_ref, oa_ref, ob_ref, hg_ref, bg_ref, g_ref, wpa_h, wpb_h, wo_h,
             du_ref, dub_ref, dya_ref, dyb_ref, dhg_ref, doa_ref, dob_ref, st_ref,
             wpa_v, wpb_v, wo_v, sa, sb, so):
        @pl.when(pl.program_id(0) == 0)
        def _():
            _load_cols(wpa_h, wpa_v, sa)
            _load_cols(wpb_h, wpb_v, sb)
            _load_rows(wo_h, wo_v, so)
            st_ref[...] = jnp.zeros_like(st_ref)

        du, dgam, dbet, _ = _ln_bwd(u_ref[...], dx_ref[...], g_ref[...])
        st_ref[1:2, :D] += dgam
        st_ref[1:2, D:] += dbet
        du_ref[...] = du
        dub = du.astype(_MXU)
        dub_ref[...] = dub.astype(dub_ref.dtype)
        dpre = _dot_nt(dub, wo_v[...])
        hgv = hg_ref[...].astype(_F32)
        bgv = bg_ref[...]
        ga = jax.nn.sigmoid(hgv[:, :D] + bgv[:, :D])
        gb = jax.nn.sigmoid(hgv[:, D:] + bgv[:, D:])
        dya = (dpre * ga).astype(_MXU)
        dyb = (dpre * gb).astype(_MXU)
        dsa = dpre * _dot(oa_ref[...].astype(_MXU), wpa_v[...]) * (ga * (1.0 - ga))
        dsb = dpre * _dot(ob_ref[...].astype(_MXU), wpb_v[...]) * (gb * (1.0 - gb))
        st_ref[0:1, :D] += jnp.sum(dsa, axis=0, keepdims=True)
        st_ref[0:1, D:] += jnp.sum(dsb, axis=0, keepdims=True)
        dya_ref[...] = dya.astype(dya_ref.dtype)
        dyb_ref[...] = dyb.astype(dyb_ref.dtype)
        dhg_ref[:, :D] = dsa.astype(dhg_ref.dtype)
        dhg_ref[:, D:] = dsb.astype(dhg_ref.dtype)
        doa_ref[...] = _dot_nt(dya, wpa_v[...]).astype(doa_ref.dtype)
        dob_ref[...] = _dot_nt(dyb, wpb_v[...]).astype(dob_ref.dtype)

    return pl.pallas_call(
        kern, name=f"mix_bwd_{layer}", grid=(T // tm,),
        in_specs=[_rows(tm, D), _rows(tm, D), _rows(tm, WA), _rows(tm, WB), _rows(tm, 2 * D), _whole((1, 2 * D)),
                  _whole((1, D)), _ANY, _ANY, _ANY],
        out_specs=[_rows(tm, D)] * 4 + [_rows(tm, 2 * D), _rows(tm, WA), _rows(tm, WB), _whole((8, 2 * D))],
        out_shape=[jax.ShapeDtypeStruct((T, D), _F32)] + [jax.ShapeDtypeStruct((T, D), _ACT)] * 3
        + [jax.ShapeDtypeStruct((T, 2 * D), _ACT), jax.ShapeDtypeStruct((T, WA), _ACT),
           jax.ShapeDtypeStruct((T, WB), _ACT), jax.ShapeDtypeStruct((8, 2 * D), _F32)],
        scratch_shapes=[pltpu.VMEM((WA, D), wpa.dtype), pltpu.VMEM((WB, D), wpb.dtype), pltpu.VMEM((D, D), wo.dtype),
                        pltpu.SemaphoreType.DMA((4,)), pltpu.SemaphoreType.DMA((4,)), pltpu.SemaphoreType.DMA((4,))],
        compiler_params=_cparams("arbitrary"),
    )(u1, dx1, oa, ob, hg, bg, gamma, wpa, wpb, wo)


def _grad_w(a, b, *, col_shards, name, comm=None):
    T, M = a.shape
    N = b.shape[1]
    tk = 1024 if T % 1024 == 0 else 512
    n = N // 4 if col_shards else N
    whole = M * N * 4 <= _GRAD_ACC_BYTES
    tn = N if whole else (n if col_shards else _divisor_tile(N, _GRAD_ACC_BYTES // (4 * M)))
    nk = T // tk

    def kern(a_ref, b_ref, o_ref, acc):
        k = pl.program_id(1)

        @pl.when(k == 0)
        def _():
            acc[...] = jnp.zeros_like(acc)

        acc[...] += _dot_tn(a_ref[...].astype(_MXU), b_ref[...].astype(_MXU))

        @pl.when(k == nk - 1)
        def _():
            if col_shards and whole:
                for s in range(4):
                    o_ref[s] = acc[:, s * n:(s + 1) * n].astype(o_ref.dtype)
            else:
                o_ref[...] = acc[...].astype(o_ref.dtype)

    if col_shards:
        out_spec = (pl.BlockSpec((4, M, n), lambda j, k: (0, 0, 0)) if whole
                    else pl.BlockSpec((None, M, n), lambda j, k: (j, 0, 0)))
        out_shape = jax.ShapeDtypeStruct((4, M, n), _ACT)
    else:
        out_spec = pl.BlockSpec((M, tn), lambda j, k: (0, j))
        out_shape = jax.ShapeDtypeStruct((M, N), _ACT)
    outs, extra = _call(
        kern, comm, name=name, grid=(N // tn, nk),
        in_specs=[pl.BlockSpec((tk, M), lambda j, k: (k, 0)), pl.BlockSpec((tk, tn), lambda j, k: (k, j))],
        out_specs=[out_spec], out_shape=[out_shape], scratch_shapes=[pltpu.VMEM((M, tn), _F32)],
        args=(a, b), semantics=("parallel", "arbitrary"))
    return outs[0], extra


def _bias_tiles(rel):
    H = rel.shape[0]
    span = _TQ * _BAND_TILES - 1
    edge = span - _REL_CLIP
    gvec = jnp.concatenate([jnp.broadcast_to(rel[:, :1], (H, edge)), rel, jnp.broadcast_to(rel[:, -1:], (H, edge))], axis=1)
    width = _BIAS_TILES * _TQ
    period = width + _TQ
    tiled = jnp.broadcast_to(jnp.pad(gvec[:, ::-1], ((0, 0), (0, 1)))[:, None, :], (H, _TQ, period))
    rows = tiled.reshape(H, _TQ * period)[:, :_TQ * (period - 1)].reshape(H, _TQ, period - 1)[:, :, _TQ - 1:]
    r = jnp.arange(_TQ)[:, None]
    u = jnp.arange(width)[None, :]
    d = 4 * _TQ + r - u
    rm = r % _CHUNK
    valid = (d >= rm - (_CHUNK - 1)) & (d <= rm + 8 * _CHUNK)
    tiles = jnp.where(valid[None], rows, _MASKED)
    return tiles.reshape(H // 2, 2 * _TQ, _BIAS_TILES, _TQ).transpose(0, 2, 1, 3)


def _fold_bias_grad(db):
    H = 2 * db.shape[0]
    width = _BIAS_TILES * _TQ
    period = width + _TQ
    x = jnp.pad(db.transpose(0, 2, 1, 3).reshape(H, _TQ, width), ((0, 0), (0, 0), (_TQ - 1, 0)))
    skew = jnp.pad(x.reshape(H, _TQ * (period - 1)), ((0, 0), (0, _TQ))).reshape(H, _TQ, period)
    dg = skew.sum(axis=1)[:, :period - 1][:, ::-1]
    span = _TQ * _BAND_TILES - 1
    edge = span - _REL_CLIP
    mid = dg[:, edge:edge + 2 * _REL_CLIP + 1]
    lo = dg[:, :edge].sum(axis=1)
    hi = dg[:, edge + 2 * _REL_CLIP + 1:].sum(axis=1)
    return mid.at[:, 0].add(lo).at[:, -1].add(hi)


def _band_window(i):
    j0 = jnp.maximum(i - (_BAND_TILES - 1), 0)
    return j0, (_BAND_TILES - 1) - (i - j0)


def _head_masks():
    lane = lax.broadcasted_iota(jnp.int32, (1, _LANES), 1)
    return [(lane // _HEAD) == hh for hh in range(2)]


def _stack_heads(x, masks):
    return jnp.concatenate([jnp.where(m, x, jnp.zeros_like(x)) for m in masks], axis=0)


def _unstack_heads(y, masks):
    return jnp.where(masks[0], y[:_TQ], y[_TQ:])


def _scaled(q):
    return q * jnp.asarray(_HEAD ** -0.5, q.dtype)


def _band_probs(q2, k_ref, b_ref, j0, boff):
    s = []
    for j in range(_BAND_TILES):
        kj = k_ref[pl.ds(pl.multiple_of((j0 + j) * _TQ, _TQ), _TQ), :]
        s.append(_dot_nt(q2, kj) + b_ref[boff + j])
    m = jnp.max(functools.reduce(jnp.maximum, s), axis=-1, keepdims=True)
    p = [jnp.exp(x - m) for x in s]
    l = jnp.sum(functools.reduce(lambda a, b: a + b, p), axis=-1, keepdims=True)
    return p, 1.0 / l


def _qkv_specs(T, cb, npair, tq=_TQ):
    return [pl.BlockSpec((tq, _LANES), lambda h, i: (i, cb + h)),
            pl.BlockSpec((T, _LANES), lambda h, i: (0, cb + npair + h)),
            pl.BlockSpec((T, _LANES), lambda h, i: (0, cb + 2 * npair + h))]


def _attn_a_fwd(hq, bias, col0, width, layer, comm=None):
    T = hq.shape[0]
    npair = width // _LANES
    nsub = _BAND_SUBTILES
    tq = nsub * _TQ

    def kern(q_ref, k_ref, v_ref, b_ref, o_ref):
        masks = _head_masks()
        q = _scaled(q_ref[...])
        for s in range(nsub):
            part = slice(s * _TQ, (s + 1) * _TQ)
            j0, boff = _band_window(nsub * pl.program_id(1) + s)
            p, inv = _band_probs(_stack_heads(q[part], masks), k_ref, b_ref, j0, boff)
            o = jnp.zeros((2 * _TQ, _LANES), _F32)
            for j in range(_BAND_TILES):
                vj = v_ref[pl.ds(pl.multiple_of((j0 + j) * _TQ, _TQ), _TQ), :]
                o = o + _dot(p[j].astype(_MXU), vj)
            o_ref[part, :] = _unstack_heads(o * inv, masks).astype(o_ref.dtype)

    outs, extra = _call(
        kern, comm, name=f"band_attn_fwd_{layer}", grid=(npair, T // tq),
        in_specs=_qkv_specs(T, col0 // _LANES, npair, tq)
        + [pl.BlockSpec((None, _BIAS_TILES, 2 * _TQ, _TQ), lambda h, i: (h, 0, 0, 0))],
        out_specs=[pl.BlockSpec((tq, _LANES), lambda h, i: (i, h))],
        out_shape=[jax.ShapeDtypeStruct((T, width), _ACT)], scratch_shapes=[],
        args=(hq, hq, hq, bias), semantics=("arbitrary", "arbitrary"))
    return outs[0], extra


def _attn_a_bwd(hq, bias, do, col0, width, layer, comm=None):
    T = hq.shape[0]
    npair = width // _LANES
    nsub = _BAND_SUBTILES
    tq = nsub * _TQ
    nq = T // tq
    scale = _HEAD ** -0.5

    def kern(q_ref, k_ref, v_ref, b_ref, do_ref, dq_ref, dk_ref, dv_ref, db_ref, dk_acc, dv_acc):
        i = pl.program_id(1)

        @pl.when(i == 0)
        def _():
            dk_acc[...] = jnp.zeros_like(dk_acc)
            dv_acc[...] = jnp.zeros_like(dv_acc)
            db_ref[...] = jnp.zeros_like(db_ref)

        masks = _head_masks()
        q = _scaled(q_ref[...])
        do_t = do_ref[...]
        for s in range(nsub):
            part = slice(s * _TQ, (s + 1) * _TQ)
            j0, boff = _band_window(nsub * i + s)
            q2 = _stack_heads(q[part], masks)
            do2 = _stack_heads(do_t[part], masks).astype(_MXU)
            p, inv = _band_probs(q2, k_ref, b_ref, j0, boff)
            rows = [pl.ds(pl.multiple_of((j0 + j) * _TQ, _TQ), _TQ) for j in range(_BAND_TILES)]
            p = [x * inv for x in p]
            dp = [_dot_nt(do2, v_ref[rows[j], :]) for j in range(_BAND_TILES)]
            delta = jnp.sum(functools.reduce(lambda a, b: a + b, [p[j] * dp[j] for j in range(_BAND_TILES)]),
                            axis=-1, keepdims=True)
            dq = jnp.zeros((2 * _TQ, _LANES), _F32)
            for j in range(_BAND_TILES):
                ds = p[j] * (dp[j] - delta)
                db_ref[boff + j] += ds
                dsb = ds.astype(_MXU)
                dq = dq + _dot(dsb, k_ref[rows[j], :])
                dk_acc[rows[j], :] += _dot_tn(dsb, q2)
                dv_acc[rows[j], :] += _dot_tn(p[j].astype(_MXU), do2)
            dq_ref[part, :] = (_unstack_heads(dq, masks) * scale).astype(dq_ref.dtype)

        @pl.when(i == nq - 1)
        def _():
            dk_ref[...] = dk_acc[...].astype(dk_ref.dtype)
            dv_ref[...] = dv_acc[...].astype(dv_ref.dtype)

    strip = pl.BlockSpec((None, _BIAS_TILES, 2 * _TQ, _TQ), lambda h, i: (h, 0, 0, 0))
    tile = pl.BlockSpec((tq, _LANES), lambda h, i: (i, h))
    column = pl.BlockSpec((T, _LANES), lambda h, i: (0, h))
    outs, extra = _call(
        kern, comm, name=f"band_attn_bwd_{layer}", grid=(npair, nq),
        in_specs=_qkv_specs(T, col0 // _LANES, npair, tq) + [strip, tile],
        out_specs=[tile, column, column, strip],
        out_shape=[jax.ShapeDtypeStruct((T, width), _ACT)] * 3
        + [jax.ShapeDtypeStruct((npair, _BIAS_TILES, 2 * _TQ, _TQ), _F32)],
        scratch_shapes=[pltpu.VMEM((T, _LANES), _F32), pltpu.VMEM((T, _LANES), _F32)],
        args=(hq, hq, hq, bias, do), semantics=("arbitrary", "arbitrary"))
    return outs, extra


def _suffix_matrix():
    r = lax.broadcasted_iota(jnp.int32, (_TQ, _TQ), 0)
    c = lax.broadcasted_iota(jnp.int32, (_TQ, _TQ), 1)
    r2 = lax.broadcasted_iota(jnp.int32, (2 * _TQ, _TQ), 0)
    c2 = lax.broadcasted_iota(jnp.int32, (2 * _TQ, _TQ), 1)
    return (r > c).astype(_MXU), c2 - (r2 & (_TQ - 1))


def _suffix_sums(xs, tri):
    n, k = xs[0].shape[0], len(xs)
    his = [x.astype(_MXU) for x in xs]
    los = [(x - h.astype(_F32)).astype(_MXU) for x, h in zip(xs, his)]
    y = _dot(jnp.concatenate(his + los, axis=0), tri)
    return [y[j * n:(j + 1) * n] + y[(k + j) * n:(k + j + 1) * n] for j in range(k)]


def _stick_tiles(tiles, rel, carry_l, tri):
    zs = [_dot_nt(qs, kj) for qs, kj, _, _ in tiles]
    Ls, masks = [], []
    for z, (_, _, jj, _) in zip(zs, tiles):
        nsp = -(jnp.maximum(z, 0.0) + jnp.log(1.0 + jnp.exp(-jnp.abs(z))))
        if isinstance(jj, int):
            mask = (rel < 0) if jj == 0 else None
        else:
            mask = rel < jnp.where(jj == 0, 0, _TQ)
        Ls.append(nsp if mask is None else jnp.where(mask, nsp, 0.0))
        masks.append(mask)
    carry_l = list(carry_l)
    ws = []
    for z, L, suffix, mask, (_, _, _, sub) in zip(zs, Ls, _suffix_sums(Ls, tri), masks, tiles):
        w = jnp.exp(z + L + suffix + carry_l[sub])
        ws.append(w if mask is None else jnp.where(mask, w, 0.0))
        carry_l[sub] = carry_l[sub] + jnp.sum(L, axis=-1, keepdims=True)
    return zs, Ls, ws, masks, carry_l


def _sweep(i, step, zero):
    nsub = _SB_SUBTILES

    def window():
        tiles = [(s, jj) for jj in range(_SB_WINDOW) for s in range(nsub)]
        return tuple((jnp.int32(_SB_WINDOW),) + c for c in step(tiles, [zero] * nsub))

    start = lax.cond(i >= -(-(_SB_WINDOW - 1) // nsub), window, lambda: tuple((jnp.int32(0),) + zero for _ in range(nsub)))
    outs = []
    for s in range(nsub):
        def done(c, s=s):
            return jnp.logical_or(c[0] > nsub * i + s, jnp.max(c[1]) < _EXP_ZERO_BELOW)

        def more(c, s=s):
            carries = [None] * nsub
            carries[s] = c[1:]
            return (c[0] + 1,) + step([(s, c[0])], carries)[s]

        outs.append(lax.while_loop(lambda c, done=done: jnp.logical_not(done(c)), more, start[s]))
    return outs


def _sb_fwd(hq, col0, width, layer, comm=None):
    T = hq.shape[0]
    npair = width // _LANES
    nsub = _SB_SUBTILES
    tq = nsub * _TQ

    def kern(q_ref, k_ref, v_ref, o_ref):
        i = pl.program_id(1)
        masks = _head_masks()
        tri, rel = _suffix_matrix()
        q = _scaled(q_ref[...])
        q2 = [_stack_heads(q[s * _TQ:(s + 1) * _TQ], masks) for s in range(nsub)]

        def step(tiles, carries):
            rows = [pl.ds(pl.multiple_of((nsub * i + s - jj) * _TQ, _TQ), _TQ) for s, jj in tiles]
            cls = [None if c is None else c[0] for c in carries]
            accs = [None if c is None else c[1] for c in carries]
            _, _, ws, _, cls = _stick_tiles([(q2[s], k_ref[r, :], jj, s) for (s, jj), r in zip(tiles, rows)], rel, cls, tri)
            for w, r, (s, _) in zip(ws, rows, tiles):
                accs[s] = accs[s] + _dot(w.astype(_MXU), v_ref[r, :])
            return [None if c is None else (cls[s], accs[s]) for s, c in enumerate(carries)]

        outs = _sweep(i, step, (jnp.zeros((2 * _TQ, 1), _F32), jnp.zeros((2 * _TQ, _LANES), _F32)))
        for s in range(nsub):
            o_ref[s * _TQ:(s + 1) * _TQ, :] = _unstack_heads(outs[s][2], masks)

    outs, extra = _call(
        kern, comm, name=f"stick_attn_fwd_{layer}", grid=(npair, T // tq),
        in_specs=_qkv_specs(T, col0 // _LANES, npair, tq),
        out_specs=[pl.BlockSpec((tq, _LANES), lambda h, i: (i, h))],
        out_shape=[jax.ShapeDtypeStruct((T, width), _F32)], scratch_shapes=[],
        args=(hq, hq, hq), semantics=("arbitrary", "arbitrary"))
    return outs[0], extra


def _sb_bwd(hq, o, do, col0, width, layer, comm=None):
    T = hq.shape[0]
    npair = width // _LANES
    nsub = _SB_SUBTILES
    tq = nsub * _TQ
    nq = T // tq
    scale = _HEAD ** -0.5

    def kern(q_ref, k_ref, v_ref, o_ref, do_ref, dq_ref, dk_ref, dv_ref, dk_acc, dv_acc):
        i = pl.program_id(1)

        @pl.when(i == 0)
        def _():
            dk_acc[...] = jnp.zeros_like(dk_acc)
            dv_acc[...] = jnp.zeros_like(dv_acc)

        masks = _head_masks()
        tri, rel = _suffix_matrix()
        q = _scaled(q_ref[...])
        do_t = do_ref[...]
        prod = do_t.astype(_F32) * o_ref[...]
        part = [slice(s * _TQ, (s + 1) * _TQ) for s in range(nsub)]
        q2 = [_stack_heads(q[p], masks) for p in part]
        do2 = [_stack_heads(do_t[p], masks).astype(_MXU) for p in part]
        dsum = [jnp.sum(_stack_heads(prod[p], masks), axis=-1, keepdims=True) for p in part]

        def step(tiles, carries):
            rows = [pl.ds(pl.multiple_of((nsub * i + s - jj) * _TQ, _TQ), _TQ) for s, jj in tiles]
            kjs = [k_ref[r, :] for r in rows]
            cls, cgs, dqs = ([None if c is None else c[n] for c in carries] for n in range(3))
            zs, Ls, ws, tile_masks, cls = _stick_tiles([(q2[s], kj, jj, s) for (s, jj), kj in zip(tiles, kjs)], rel, cls, tri)
            wbs = [w.astype(_MXU) for w in ws]
            gs = [wb.astype(_F32) * _dot_nt(do2[s], v_ref[r, :]) for wb, r, (s, _) in zip(wbs, rows, tiles)]
            for z, L, g, later, mask, wb, kj, r, (s, _) in zip(zs, Ls, gs, _suffix_sums(gs, tri), tile_masks, wbs, kjs,
                                                               rows, tiles):
                dz = g - jnp.exp(z + L) * (dsum[s] - (later + cgs[s]))
                if mask is not None:
                    dz = jnp.where(mask, dz, 0.0)
                dzb = dz.astype(_MXU)
                dk_acc[r, :] += _dot_tn(dzb, q2[s])
                dv_acc[r, :] += _dot_tn(wb, do2[s])
                dqs[s] = dqs[s] + _dot(dzb, kj)
                cgs[s] = cgs[s] + jnp.sum(g, axis=-1, keepdims=True)
            return [None if c is None else (cls[s], cgs[s], dqs[s]) for s, c in enumerate(carries)]

        zc = jnp.zeros((2 * _TQ, 1), _F32)
        outs = _sweep(i, step, (zc, zc, jnp.zeros((2 * _TQ, _LANES), _F32)))
        for s in range(nsub):
            dq_ref[part[s], :] = (_unstack_heads(outs[s][3], masks) * scale).astype(dq_ref.dtype)

        @pl.when(i == nq - 1)
        def _():
            dk_ref[...] = dk_acc[...].astype(dk_ref.dtype)
            dv_ref[...] = dv_acc[...].astype(dv_ref.dtype)

    tile_spec = pl.BlockSpec((tq, _LANES), lambda h, i: (i, h))
    column = pl.BlockSpec((T, _LANES), lambda h, i: (0, h))
    outs, extra = _call(
        kern, comm, name=f"stick_attn_bwd_{layer}", grid=(npair, nq),
        in_specs=_qkv_specs(T, col0 // _LANES, npair, tq) + [tile_spec, tile_spec],
        out_specs=[tile_spec, column, column],
        out_shape=[jax.ShapeDtypeStruct((T, width), _ACT)] * 3,
        scratch_shapes=[pltpu.VMEM((T, _LANES), _F32), pltpu.VMEM((T, _LANES), _F32)],
        args=(hq, hq, hq, o, do), semantics=("arbitrary", "arbitrary"))
    return outs, extra


_DENSE = ("w_in", "w_proj_a", "w_proj_b", "w_out", "w_ffn_in", "w_ffn_out")
_COL_SHARDED = {"w_in": True, "w_proj_a": True, "w_proj_b": True, "w_out": False, "w_ffn_in": True, "w_ffn_out": False}
_SMALL = ("b_gate", "rel_bias", "ln1_g", "ln1_b", "ln2_g", "ln2_b")


class _Plans:
    def __init__(self, plans=None):
        self.plans = plans or {}

    def start(self, key):
        if key not in self.plans:
            return None, None
        return self.plans[key]()

    @staticmethod
    def finish(done, extra):
        if done is not None:
            done(extra)


def _layer_fwd(x, W, small, l, alpha, plans):
    WA = small["rel_bias"].shape[1] * _HEAD
    row = lambda v: v[l].reshape(1, -1)
    comm, done = plans.start(f"in_proj_{l}")
    (hq, hg, xb), extra = _in_proj(x, W["w_in"], l, comm)
    plans.finish(done, extra)
    WB = (hq.shape[1] - 3 * WA) // 3
    bias = _bias_tiles(small["rel_bias"][l])
    comm, done = plans.start(f"band_fwd_{l}")
    oa, extra = _attn_a_fwd(hq, bias, 0, WA, l, comm)
    plans.finish(done, extra)
    comm, done = plans.start(f"stick_fwd_{l}")
    ob, extra = _sb_fwd(hq, 3 * WA, WB, l, comm)
    plans.finish(done, extra)
    comm, done = plans.start(f"mix_fwd_{l}")
    (x1, u1, pre), extra = _mix_fwd(oa, ob, hg, x, W["w_proj_a"], W["w_proj_b"], W["w_out"], row(small["b_gate"]),
                                            row(small["ln1_g"]), row(small["ln1_b"]), alpha, l, comm)
    plans.finish(done, extra)
    comm, done = plans.start(f"ffn_fwd_{l}")
    (x2, u2, act, gu, x1b), extra = _ffn_fwd(x1, W["w_ffn_in"], W["w_ffn_out"], row(small["ln2_g"]),
                                             row(small["ln2_b"]), alpha, l, comm)
    plans.finish(done, extra)
    return x2, dict(xb=xb, hq=hq, hg=hg, bias=bias, oa=oa, ob=ob, x1b=x1b, u1=u1, pre=pre, u2=u2, act=act, gu=gu)


def _layer_bwd(dy_or_target, S, W, small, l, last, alpha, plans, gw):
    D = S["xb"].shape[1]
    WA, WB = S["oa"].shape[1], S["ob"].shape[1]
    row = lambda v: v[l].reshape(1, -1)

    def blocks(g, n):
        return g if _COL_SHARDED[n] else g.reshape(4, g.shape[0] // 4, g.shape[1])

    dx1, du2b, dgu, st2 = _ffn_bwd(S["u2"], dy_or_target, S["gu"], row(small["ln2_g"]), row(small["ln2_b"]),
                                   W["w_ffn_in"], W["w_ffn_out"], alpha, l, last)
    gw["w_ffn_in"] = blocks(_grad_w(S["x1b"], dgu, col_shards=True, name=f"grad_w_ffn_in_{l}")[0], "w_ffn_in")
    gw["w_ffn_out"] = blocks(_grad_w(S["act"], du2b, col_shards=False, name=f"grad_w_ffn_out_{l}")[0], "w_ffn_out")
    du1, du1b, dya, dyb, dhg, doa, dob, st1 = _mix_bwd(S["u1"], dx1, S["oa"], S["ob"], S["hg"], W["w_proj_a"],
                                                       W["w_proj_b"], W["w_out"], row(small["b_gate"]),
                                                       row(small["ln1_g"]), l)
    gw["w_out"] = blocks(_grad_w(S["pre"], du1b, col_shards=False, name=f"grad_w_out_{l}")[0], "w_out")
    gw["w_proj_a"] = blocks(_grad_w(S["oa"], dya, col_shards=True, name=f"grad_w_proj_a_{l}")[0], "w_proj_a")
    gw["w_proj_b"] = blocks(_grad_w(S["ob"], dyb, col_shards=True, name=f"grad_w_proj_b_{l}")[0], "w_proj_b")
    comm, done = plans.start(f"band_bwd_{l}")
    (dqa, dka, dva, dbias), extra = _attn_a_bwd(S["hq"], S["bias"], doa, 0, WA, l, comm)
    plans.finish(done, extra)
    comm, done = plans.start(f"stick_bwd_{l}")
    (dqb, dkb, dvb), extra = _sb_bwd(S["hq"], S["ob"], dob, 3 * WA, WB, l, comm)
    plans.finish(done, extra)
    dh = jnp.concatenate([dqa, dka, dva, dqb, dkb, dvb, dhg], axis=1)
    comm, done = plans.start(f"grad_w_in_{l}")
    g, extra = _grad_w(S["xb"], dh, col_shards=True, name=f"grad_w_in_{l}", comm=comm)
    gw["w_in"] = blocks(g, "w_in")
    plans.finish(done, extra)
    comm, done = plans.start(f"in_proj_bwd_{l}")
    dx, extra = _residual_nt(du1, alpha, dh, W["w_in"], f"in_proj_bwd_{l}", comm)
    plans.finish(done, extra)
    gs = dict(b_gate=st1[0], rel_bias=_fold_bias_grad(dbias), ln1_g=st1[1, :D], ln1_b=st1[1, D:],
              ln2_g=st2[0], ln2_b=st2[1])
    return dx, gs, st2[2]


def _local_step(x, target, W, small, plans=None, gws=None):
    depth = len(W)
    alpha = float((2 * depth) ** 0.25)
    plans = plans or _Plans()
    gws = gws if gws is not None else [dict() for _ in range(depth)]
    saved = []
    h = x
    for l in range(depth):
        h, S = _layer_fwd(h, W[l], small, l, alpha, plans)
        saved.append(S)
    gss = [None] * depth
    d = target
    sq = None
    for l in reversed(range(depth)):
        d, gss[l], sq_l = _layer_bwd(d, saved[l], W[l], small, l, l == depth - 1, alpha, plans, gws[l])
        if l == depth - 1:
            sq = sq_l
    return sq, d, gws, gss


def _place():
    return lax.axis_index("x"), lax.axis_index("y"), lax.axis_index("c")


def _remote(src, dst, send_sem, recv_sem, to):
    return pltpu.make_async_remote_copy(src_ref=src, dst_ref=dst, send_sem=send_sem, recv_sem=recv_sem,
                                        device_id=to, device_id_type=_MESH)


def _half(ref, hc):
    kh = ref.shape[0] // 2
    return ref.at[pl.ds(pl.multiple_of(hc * kh, 16), kh), :]


def _gather_plan(blocks, fractions):
    nt = len(blocks)

    def run(step, nsteps, ins, outs, sems):
        send_sems, recv_sems, loc_sems = sems
        x, y, c = _place()
        k = 2 * x + y
        me, sibling = (x, y, c), (x, y, 1 - c)
        chips = [(1 - x, y), (x, 1 - y), (1 - x, 1 - y)]
        chip_k = [2 * cx + cy for cx, cy in chips]

        def ici(t, s, owner_k, to, src=None):
            dst = _half(outs[t].at[owner_k], c)
            return _remote(dst if src is None else src, dst, send_sems.at[t, s], recv_sems.at[t, s], to)

        def passed(t, s, hc, to):
            blk = _half(outs[t].at[chip_k[s]], hc)
            return _remote(blk, blk, send_sems.at[t, 3 + s], recv_sems.at[t, 3 + s], to)

        def local(t):
            return pltpu.make_async_copy(ins[t], outs[t].at[k], loc_sems.at[t])

        @pl.when(step == 0)
        def _():
            for t in range(nt):
                local(t).start()
                for s, chip in enumerate(chips):
                    ici(t, s, k, (*chip, c), src=_half(ins[t], c)).start()

        for t in range(nt):
            @pl.when(step == min(nsteps - 1, int(fractions[t] * nsteps)))
            def _():
                for s in range(3):
                    ici(t, s, chip_k[s], me).wait_recv()
                    passed(t, s, c, sibling).start()

        @pl.when(step == nsteps - 1)
        def _():
            for t in range(nt):
                for s, chip in enumerate(chips):
                    passed(t, s, 1 - c, me).wait_recv()
            for t in range(nt):
                for s, chip in enumerate(chips):
                    ici(t, s, k, (*chip, c), src=_half(ins[t], c)).wait_send()
                    passed(t, s, c, sibling).wait_send()
                local(t).wait()

    return _Comm(blocks, [jax.ShapeDtypeStruct((4,) + b.shape, b.dtype) for b in blocks],
                 [pltpu.SemaphoreType.DMA((nt, 6)), pltpu.SemaphoreType.DMA((nt, 6)), pltpu.SemaphoreType.DMA((nt,))], run)


def _scatter_plan(grads, owners):
    nt = len(grads)

    def run(step, nsteps, ins, outs, sems):
        send_sems, recv_sems, loc_sems = sems
        x, y, c = _place()
        me = 4 * x + 2 * y + c

        def target(r):
            tx = 1 - x if r & 2 else x
            ty = 1 - y if r & 1 else y
            return tx, ty

        def send(t, r):
            tx, ty = target(r)
            return _remote(ins[t].at[2 * tx + ty], outs[t].at[me], send_sems.at[t, r], recv_sems.at[t, 2 * r + c],
                           (tx, ty, owners[t]))

        def local(t):
            return pltpu.make_async_copy(ins[t].at[2 * x + y], outs[t].at[me], loc_sems.at[t])

        @pl.when(step == 0)
        def _():
            for t in range(nt):
                @pl.when(c == owners[t])
                def _():
                    local(t).start()

                @pl.when(c != owners[t])
                def _():
                    send(t, 0).start()

                for r in range(1, 4):
                    send(t, r).start()

        @pl.when(step == nsteps - 1)
        def _():
            for t in range(nt):
                @pl.when(c == owners[t])
                def _():
                    for r in range(4):
                        sx, sy = target(r)
                        for cs in range(2):
                            if r == 0 and cs == owners[t]:
                                continue
                            src_dev = 4 * sx + 2 * sy + cs
                            _remote(ins[t].at[0], outs[t].at[src_dev], send_sems.at[t, r], recv_sems.at[t, 2 * r + cs],
                                    (x, y, c)).wait_recv()
                    local(t).wait()

                @pl.when(c != owners[t])
                def _():
                    send(t, 0).wait_send()

                for r in range(1, 4):
                    send(t, r).wait_send()

    return _Comm(grads, [jax.ShapeDtypeStruct((8,) + g.shape[1:], g.dtype) for g in grads],
                 [pltpu.SemaphoreType.DMA((nt, 4)), pltpu.SemaphoreType.DMA((nt, 8)), pltpu.SemaphoreType.DMA((nt,))], run)


def _share_plan(reduced, owners):
    nt = len(reduced)

    def run(step, nsteps, ins, outs, sems):
        del ins
        send_sems, recv_sems = sems
        x, y, c = _place()

        def give(t, to):
            return _remote(outs[t], outs[t], send_sems.at[t], recv_sems.at[t], to)

        @pl.when(step == 0)
        def _():
            for t in range(nt):
                @pl.when(c == owners[t])
                def _():
                    give(t, (x, y, 1 - c)).start()

        @pl.when(step == nsteps - 1)
        def _():
            for t in range(nt):
                @pl.when(c == owners[t])
                def _():
                    give(t, (x, y, 1 - c)).wait_send()

                @pl.when(c != owners[t])
                def _():
                    give(t, (x, y, c)).wait_recv()

    return _Comm(reduced, [jax.ShapeDtypeStruct(r.shape, r.dtype) for r in reduced],
                 [pltpu.SemaphoreType.DMA((nt,)), pltpu.SemaphoreType.DMA((nt,))], run,
                 aliases={t: t for t in range(nt)})


def _join(a, b):
    ni, no, ns = len(a.inputs), len(a.out_shapes), len(a.sems)

    def run(step, nsteps, ins, outs, sems):
        a.run(step, nsteps, ins[:ni], outs[:no], sems[:ns])
        b.run(step, nsteps, ins[ni:], outs[no:], sems[ns:])

    aliases = dict(a.aliases)
    aliases.update({ni + i: no + o for i, o in b.aliases.items()})
    return _Comm(a.inputs + b.inputs, a.out_shapes + b.out_shapes, a.sems + b.sems, run, aliases)


def _peer(x, y, c, r):
    px = 1 - x if r & 4 else x
    py = 1 - y if r & 2 else y
    pc = 1 - c if r & 1 else c
    return (px, py, pc), 4 * px + 2 * py + pc


def _sum_slots(st, name):
    _, K, n = st.shape
    tr = next(t for t in (256, 128, 64, 32, 16) if K % t == 0)

    def kern(s_ref, o_ref):
        acc = s_ref[0].astype(_F32)
        for d in range(1, 8):
            acc = acc + s_ref[d].astype(_F32)
        o_ref[...] = acc.astype(o_ref.dtype)

    return pl.pallas_call(
        kern, name=name, grid=(K // tr,),
        in_specs=[pl.BlockSpec((8, tr, n), lambda i: (0, i, 0))], out_specs=_rows(tr, n),
        out_shape=jax.ShapeDtypeStruct((K, n), _ACT),
        compiler_params=_cparams("parallel"),
    )(st)


def _all_reduce_small(p):
    R = p.shape[0]

    def body(p_ref, o_ref, stage, send_sems, recv_sems):
        x, y, c = _place()
        me = 4 * x + 2 * y + c
        stage[me] = p_ref[...]
        sent = []
        for r in range(1, 8):
            to, _ = _peer(x, y, c, r)
            cp = _remote(p_ref, stage.at[me], send_sems.at[r - 1], recv_sems.at[r - 1], to)
            cp.start()
            sent.append(cp)
        for r in range(1, 8):
            _, src_dev = _peer(x, y, c, r)
            _remote(p_ref, stage.at[src_dev], send_sems.at[r - 1], recv_sems.at[r - 1], (x, y, c)).wait_recv()
        acc = stage[0]
        for d in range(1, 8):
            acc = acc + stage[d]
        o_ref[...] = acc
        for cp in sent:
            cp.wait_send()

    vm = pl.BlockSpec(memory_space=pltpu.VMEM)
    return pl.pallas_call(
        body, name="all_reduce_small",
        in_specs=[vm], out_specs=vm,
        out_shape=jax.ShapeDtypeStruct((R, _LANES), _F32),
        scratch_shapes=[pltpu.VMEM((8, R, _LANES), _F32), pltpu.SemaphoreType.DMA((7,)), pltpu.SemaphoreType.DMA((7,))],
    )(p)


def _adamw_update(gv, w_ref, m_ref, v_ref, gf_ref, d_ref, nm_ref, nv_ref):
    nm = _B1 * m_ref[...] + (1.0 - _B1) * gv
    nv = _B2 * v_ref[...] + (1.0 - _B2) * (gv * gv)
    m_hat = nm / (1.0 - _B1 ** _STEP)
    v_hat = nv / (1.0 - _B2 ** _STEP)
    gf_ref[...] = gv
    d_ref[...] = -_LR * (m_hat / (jnp.sqrt(v_hat) + _EPS) + _WD * w_ref[...])
    nm_ref[...] = nm
    nv_ref[...] = nv


def _adamw_layers(w, g_layers, m, v, name):
    _, K, n = w.shape
    tr = next(t for t in (256, 128, 64, 32, 16) if K % t == 0)

    def kern(w_ref, g0_ref, g1_ref, m_ref, v_ref, *out_refs):
        first = pl.program_id(0) == 0
        gv = jnp.where(first, g0_ref[...].astype(_F32), g1_ref[...].astype(_F32))
        _adamw_update(gv, w_ref, m_ref, v_ref, *out_refs)

    stacked = pl.BlockSpec((None, tr, n), lambda l, i: (l, i, 0))
    layer = pl.BlockSpec((tr, n), lambda l, i: (i, 0))
    return tuple(pl.pallas_call(
        kern, name=name, grid=(2, K // tr),
        in_specs=[stacked, layer, layer, stacked, stacked], out_specs=[stacked] * 4,
        out_shape=[jax.ShapeDtypeStruct(w.shape, _F32)] * 4,
        compiler_params=_cparams("parallel", "parallel"),
    )(w, g_layers[0], g_layers[1], m, v))


def _adamw(w, g, m, v, name):
    shape = w.shape
    w2, g2, m2, v2 = (a.reshape(-1, shape[-1]) for a in (w, g, m, v))
    R, C = w2.shape
    tr = next((t for t in (256, 128, 64, 32, 16) if R % t == 0), R)

    def kern(w_ref, g_ref, m_ref, v_ref, *out_refs):
        _adamw_update(g_ref[...].astype(_F32), w_ref, m_ref, v_ref, *out_refs)

    outs = pl.pallas_call(
        kern, name=name, grid=(R // tr,),
        in_specs=[_rows(tr, C)] * 4, out_specs=[_rows(tr, C)] * 4,
        out_shape=[jax.ShapeDtypeStruct((R, C), _F32)] * 4,
        compiler_params=_cparams("parallel"),
    )(w2, g2, m2, v2)
    return tuple(o.reshape(shape) for o in outs)


def _pack_small(gss, sq):
    parts = [gss[l][n].reshape(-1) for n in _SMALL for l in range(len(gss))] + [jnp.sum(sq).reshape(1)]
    flat = jnp.concatenate(parts)
    rows = -(-flat.shape[0] // (8 * _LANES)) * 8
    return jnp.pad(flat, (0, rows * _LANES - flat.shape[0])).reshape(rows, _LANES)


def _unpack_small(total, shapes):
    flat = total.reshape(-1)
    out, off = {}, 0
    for n in _SMALL:
        layers = []
        for _ in range(shapes[n][0]):
            size = 1
            for s in shapes[n][1:]:
                size *= s
            layers.append(flat[off:off + size].reshape(shapes[n][1:]))
            off += size
        out[n] = jnp.stack(layers)
    return out, flat[off]


_GATHER = {
    "in_proj_0": [(0, "w_proj_a"), (0, "w_proj_b"), (0, "w_out"), (0, "w_ffn_out")],
    "band_fwd_0": [(0, "w_ffn_in")],
    "stick_fwd_0": [(1, "w_in"), (1, "w_ffn_out")],
    "ffn_fwd_0": [(1, "w_proj_a"), (1, "w_proj_b"), (1, "w_out"), (1, "w_ffn_in")],
}
_SCATTER = {
    "band_bwd_1": [(1, "w_ffn_in"), (1, "w_ffn_out")],
    "stick_bwd_1": [(1, "w_proj_a"), (1, "w_proj_b"), (1, "w_out")],
    "band_bwd_0": [(1, "w_in"), (0, "w_ffn_in")],
    "stick_bwd_0": [(0, "w_ffn_out"), (0, "w_proj_a"), (0, "w_proj_b"), (0, "w_out")],
    "in_proj_bwd_0": [(0, "w_in")],
}
_SHARE = {"stick_bwd_1": "band_bwd_1", "band_bwd_0": "stick_bwd_1", "stick_bwd_0": "band_bwd_0", "grad_w_in_0": "stick_bwd_0"}


def _owner(key):
    del key
    return 1


def kernel(x, w_in, b_gate, rel_bias, w_proj_a, w_proj_b, w_out, ln1_g, ln1_b, w_ffn_in, w_ffn_out, ln2_g, ln2_b, loss_target, m_w_in, m_b_gate, m_rel_bias, m_w_proj_a, m_w_proj_b, m_w_out, m_ln1_g, m_ln1_b, m_w_ffn_in, m_w_ffn_out, m_ln2_g, m_ln2_b, v_w_in, v_b_gate, v_rel_bias, v_w_proj_a, v_w_proj_b, v_w_out, v_ln1_g, v_ln1_b, v_w_ffn_in, v_w_ffn_out, v_ln2_g, v_ln2_b):
    names = ("w_in", "b_gate", "rel_bias", "w_proj_a", "w_proj_b", "w_out", "ln1_g", "ln1_b", "w_ffn_in", "w_ffn_out", "ln2_g", "ln2_b")
    w = dict(zip(names, (w_in, b_gate, rel_bias, w_proj_a, w_proj_b, w_out, ln1_g, ln1_b, w_ffn_in, w_ffn_out, ln2_g, ln2_b)))
    m = dict(zip(names, (m_w_in, m_b_gate, m_rel_bias, m_w_proj_a, m_w_proj_b, m_w_out, m_ln1_g, m_ln1_b, m_w_ffn_in, m_w_ffn_out, m_ln2_g, m_ln2_b)))
    v = dict(zip(names, (v_w_in, v_b_gate, v_rel_bias, v_w_proj_a, v_w_proj_b, v_w_out, v_ln1_g, v_ln1_b, v_w_ffn_in, v_w_ffn_out, v_ln2_g, v_ln2_b)))
    T, D = x.shape[-2], x.shape[-1]
    assert w_in.shape[0] == 2, "the exchange schedule below is written for two layers"

    mine = [{n: w[n][l].astype(_MXU) for n in _DENSE} for l in range(2)]
    W = [dict(), dict()]
    gws = [dict(), dict()]
    slots, final = {}, {}

    def gather(keys):
        sizes = [mine[l][n].size for l, n in keys]
        passed, fractions = 0, []
        for s in sizes:
            passed += s
            fractions.append(0.15 + 0.6 * passed / sum(sizes))

        def done(outs):
            for (l, n), o in zip(keys, outs):
                W[l][n] = o
        return _gather_plan([mine[l][n] for l, n in keys], fractions), done

    def scatter(keys):
        comm = _scatter_plan([gws[l][n] for l, n in keys], [_owner(key) for key in keys])
        return comm, lambda outs: slots.update(zip(keys, outs))

    def share(keys):
        reduced = [_sum_slots(slots[key], f"sum_grad_{key[1]}_{key[0]}") for key in keys]
        comm = _share_plan(reduced, [_owner(key) for key in keys])
        return comm, lambda outs: final.update(zip(keys, outs))

    def both(first, second):
        (ca, da), (cb, db) = first, second
        na = len(ca.out_shapes)
        return _join(ca, cb), lambda outs: (da(outs[:na]), db(outs[na:]))

    comm, done = gather([(0, "w_in")])
    done(_comm_only(comm, "gather_first"))
    plans = {key: functools.partial(gather, keys) for key, keys in _GATHER.items()}
    for key, keys in _SCATTER.items():
        plans[key] = functools.partial(scatter, keys)
    for key, scattered_under in _SHARE.items():
        handed = functools.partial(share, _SCATTER[scattered_under])
        carried = plans.get(key)
        plans[key] = handed if carried is None else (lambda carried=carried, handed=handed: both(carried(), handed()))
    small = {n: w[n] for n in _SMALL}
    sq, dx, _, gss = _local_step(x.reshape(T, D), loss_target.reshape(T, D), W, small, _Plans(plans), gws)

    comm, done = share(_SCATTER["in_proj_bwd_0"])
    done(_comm_only(comm, "share_last"))
    total = _all_reduce_small(_pack_small(gss, sq))
    small_grads, sq_all = _unpack_small(total, {n: w[n].shape for n in _SMALL})
    loss = 0.5 * sq_all / D

    grad, delta, new_m, new_v = {}, {}, {}, {}
    for n in names:
        if n in _DENSE:
            updated = _adamw_layers(w[n], [final[(l, n)] for l in range(2)], m[n], v[n], f"adamw_{n}")
        else:
            updated = _adamw(w[n], small_grads[n], m[n], v[n], f"adamw_{n}")
        grad[n], delta[n], new_m[n], new_v[n] = updated
    return (loss, dx.reshape(x.shape), *[grad[n] for n in names], *[delta[n] for n in names],
            *[new_m[n] for n in names], *[new_v[n] for n in names])
```

```python
import functools

import jax
import jax.numpy as jnp
from jax import lax
from jax.experimental import pallas as pl
from jax.experimental.pallas import tpu as pltpu

_MXU = jnp.bfloat16
_ACT = jnp.bfloat16
_F32 = jnp.float32

_HEAD = 64
_CHUNK = 64
_LANES = 128
_TQ = 128
_BAND_TILES = 5
_BIAS_TILES = 9
_REL_CLIP = 256
_LN_EPS = 1e-5
_MASKED = -1e30
_EXP_ZERO_BELOW = -87.34
_SB_WINDOW = 2
_SB_SUBTILES = 4
_BAND_SUBTILES = 8
_VMEM_LIMIT = 56 * 1024 * 1024
_GRAD_ACC_BYTES = 12 * 1024 * 1024

_LR, _B1, _B2, _EPS, _WD, _STEP = 0.001, 0.9, 0.999, 1e-08, 0.01, 10

_MESH = pl.DeviceIdType.MESH


def _dot(a, b):
    return jnp.dot(a, b, preferred_element_type=_F32)


def _dot_nt(a, b):
    return lax.dot_general(a, b, (((1,), (1,)), ((), ())), preferred_element_type=_F32)


def _dot_tn(a, b):
    return lax.dot_general(a, b, (((0,), (0,)), ((), ())), preferred_element_type=_F32)


def _cparams(*sem):
    return pltpu.CompilerParams(dimension_semantics=sem, vmem_limit_bytes=_VMEM_LIMIT)


def _rows(t, c):
    return pl.BlockSpec((t, c), lambda i: (i, 0))


def _whole(shape):
    return pl.BlockSpec(shape, lambda i: tuple(0 for _ in shape))


_ANY = pl.BlockSpec(memory_space=pl.ANY)


def _load_cols(w_hbm, w_vmem, sem):
    n = w_hbm.shape[-1]
    cps = [pltpu.make_async_copy(w_hbm.at[k], w_vmem.at[:, pl.ds(k * n, n)], sem.at[k]) for k in range(4)]
    for cp in cps:
        cp.start()
    for cp in cps:
        cp.wait()


def _load_rows(w_hbm, w_vmem, sem):
    r = w_hbm.shape[-2]
    cps = [pltpu.make_async_copy(w_hbm.at[k], w_vmem.at[pl.ds(k * r, r), :], sem.at[k]) for k in range(4)]
    for cp in cps:
        cp.start()
    for cp in cps:
        cp.wait()


def _ln_stats(u):
    mu = jnp.mean(u, axis=-1, keepdims=True)
    xc = u - mu
    var = jnp.mean(xc * xc, axis=-1, keepdims=True)
    rstd = lax.rsqrt(var + _LN_EPS)
    return xc * rstd, rstd


def _ln_bwd(u, dy, gamma):
    xhat, rstd = _ln_stats(u)
    dxh = dy * gamma
    m1 = jnp.mean(dxh, axis=-1, keepdims=True)
    m2 = jnp.mean(dxh * xhat, axis=-1, keepdims=True)
    du = rstd * (dxh - m1 - xhat * m2)
    return du, jnp.sum(dy * xhat, axis=0, keepdims=True), jnp.sum(dy, axis=0, keepdims=True), xhat


def _divisor_tile(n, cap):
    best = None
    for t in range(_LANES, min(n, cap) + 1, _LANES):
        if n % t == 0:
            best = t
    return best or n


class _Comm:
    def __init__(self, inputs, out_shapes, sems, run, aliases=None):
        self.inputs, self.out_shapes, self.sems, self.run = list(inputs), list(out_shapes), list(sems), run
        self.aliases = aliases or {}


def _call(kern, comm, *, name, grid, in_specs, out_specs, out_shape, scratch_shapes, args, semantics):
    in_specs, out_specs, out_shape, scratch_shapes = list(in_specs), list(out_specs), list(out_shape), list(scratch_shapes)
    if comm is None:
        outs = pl.pallas_call(kern, name=name, grid=grid, in_specs=in_specs, out_specs=out_specs, out_shape=out_shape,
                              scratch_shapes=scratch_shapes, compiler_params=_cparams(*semantics))(*args)
        return list(outs), []
    n_in, n_out, n_scr = len(in_specs), len(out_specs), len(scratch_shapes)
    ci, co = len(comm.inputs), len(comm.out_shapes)
    nsteps = functools.reduce(lambda a, b: a * b, grid, 1)

    def fused(*refs):
        a, b = n_in, n_in + ci
        c, d = b + n_out, b + n_out + co
        e = d + n_scr
        step = pl.program_id(0)
        for ax in range(1, len(grid)):
            step = step * grid[ax] + pl.program_id(ax)
        comm.run(step, nsteps, refs[a:b], refs[c:d], refs[e:])
        kern(*refs[:a], *refs[b:c], *refs[d:e])

    outs = pl.pallas_call(
        fused, name=name, grid=grid, in_specs=in_specs + [_ANY] * ci, out_specs=out_specs + [_ANY] * co,
        out_shape=out_shape + comm.out_shapes, scratch_shapes=scratch_shapes + comm.sems,
        input_output_aliases={n_in + i: n_out + o for i, o in comm.aliases.items()},
        compiler_params=_cparams(*("arbitrary" for _ in grid)))(*args, *comm.inputs)
    return list(outs[:n_out]), list(outs[n_out:])


def _comm_only(comm, name):
    def body(*refs):
        ci, co = len(comm.inputs), len(comm.out_shapes)
        comm.run(0, 1, refs[:ci], refs[ci:ci + co], refs[ci + co:])

    outs = pl.pallas_call(body, name=name, in_specs=[_ANY] * len(comm.inputs), out_specs=[_ANY] * len(comm.out_shapes),
                          out_shape=comm.out_shapes, scratch_shapes=comm.sems,
                          input_output_aliases=dict(comm.aliases))(*comm.inputs)
    return list(outs)


def _in_proj(x, w_in, layer, comm=None):
    T, D = x.shape
    N = 4 * w_in.shape[-1]
    NQ = N - 2 * D
    tm = 512

    def kern(x_ref, w_hbm, hq_ref, hg_ref, xb_ref, w_v, sem):
        @pl.when(pl.program_id(0) == 0)
        def _():
            _load_cols(w_hbm, w_v, sem)

        xb = x_ref[...].astype(_MXU)
        hq_ref[...] = _dot(xb, w_v[:, :NQ]).astype(hq_ref.dtype)
        hg_ref[...] = _dot(xb, w_v[:, NQ:]).astype(hg_ref.dtype)
        xb_ref[...] = xb.astype(xb_ref.dtype)

    return _call(
        kern, comm, name=f"in_proj_{layer}", grid=(T // tm,),
        in_specs=[_rows(tm, D), _ANY],
        out_specs=[_rows(tm, NQ), _rows(tm, 2 * D), _rows(tm, D)],
        out_shape=[jax.ShapeDtypeStruct((T, NQ), _ACT), jax.ShapeDtypeStruct((T, 2 * D), _ACT),
                   jax.ShapeDtypeStruct((T, D), _ACT)],
        scratch_shapes=[pltpu.VMEM((D, N), w_in.dtype), pltpu.SemaphoreType.DMA((4,))],
        args=(x, w_in), semantics=("arbitrary",))


def _mix_fwd(oa, ob, hg, x, wpa, wpb, wo, bg, gamma, beta, alpha, layer, comm=None):
    T, D = x.shape
    WA, WB = oa.shape[1], ob.shape[1]
    tm = 512

    def kern(oa_ref, ob_ref, hg_ref, x_ref, bg_ref, g_ref, b_ref, wpa_h, wpb_h, wo_h,
             x1_ref, u1_ref, pre_ref, wpa_v, wpb_v, wo_v, sa, sb, so):
        @pl.when(pl.program_id(0) == 0)
        def _():
            _load_cols(wpa_h, wpa_v, sa)
            _load_cols(wpb_h, wpb_v, sb)
            _load_rows(wo_h, wo_v, so)

        ya = _dot(oa_ref[...].astype(_MXU), wpa_v[...])
        yb = _dot(ob_ref[...].astype(_MXU), wpb_v[...])
        hgv = hg_ref[...].astype(_F32)
        bgv = bg_ref[...]
        ga = jax.nn.sigmoid(hgv[:, :D] + bgv[:, :D])
        gb = jax.nn.sigmoid(hgv[:, D:] + bgv[:, D:])
        pre = ga * ya + gb * yb
        mix = _dot(pre.astype(_MXU), wo_v[...])
        u = alpha * x_ref[...] + mix
        xhat, _ = _ln_stats(u)
        x1_ref[...] = xhat * g_ref[...] + b_ref[...]
        u1_ref[...] = u
        pre_ref[...] = pre.astype(pre_ref.dtype)

    return _call(
        kern, comm, name=f"mix_fwd_{layer}", grid=(T // tm,),
        in_specs=[_rows(tm, WA), _rows(tm, WB), _rows(tm, 2 * D), _rows(tm, D),
                  _whole((1, 2 * D)), _whole((1, D)), _whole((1, D)), _ANY, _ANY, _ANY],
        out_specs=[_rows(tm, D)] * 3,
        out_shape=[jax.ShapeDtypeStruct((T, D), _F32), jax.ShapeDtypeStruct((T, D), _F32),
                   jax.ShapeDtypeStruct((T, D), _ACT)],
        scratch_shapes=[pltpu.VMEM((WA, D), wpa.dtype), pltpu.VMEM((WB, D), wpb.dtype), pltpu.VMEM((D, D), wo.dtype),
                        pltpu.SemaphoreType.DMA((4,)), pltpu.SemaphoreType.DMA((4,)), pltpu.SemaphoreType.DMA((4,))],
        args=(oa, ob, hg, x, bg, gamma, beta, wpa, wpb, wo), semantics=("arbitrary",))


def _ffn_fwd(x1, wfi, wfo, gamma, beta, alpha, layer, comm=None):
    T, D = x1.shape
    F2 = 4 * wfi.shape[-1]
    F = F2 // 2
    tm = 256
    fc = F // 2

    def kern(x_ref, g_ref, b_ref, wi_h, wo_h, x2_ref, u2_ref, act_ref, gu_ref, xb_ref, wi_v, wo_v, si, so):
        @pl.when(pl.program_id(0) == 0)
        def _():
            _load_cols(wi_h, wi_v, si)
            _load_rows(wo_h, wo_v, so)

        x = x_ref[...]
        xb = x.astype(_MXU)
        xb_ref[...] = xb.astype(xb_ref.dtype)
        ffn = jnp.zeros((tm, D), _F32)
        for c in range(2):
            g = _dot(xb, wi_v[:, c * fc:(c + 1) * fc])
            u = _dot(xb, wi_v[:, F + c * fc:F + (c + 1) * fc])
            act = g * jax.nn.sigmoid(g) * u
            ab = act.astype(_MXU)
            ffn = ffn + _dot(ab, wo_v[c * fc:(c + 1) * fc, :])
            act_ref[:, c * fc:(c + 1) * fc] = ab.astype(act_ref.dtype)
            gu_ref[:, c * fc:(c + 1) * fc] = g.astype(gu_ref.dtype)
            gu_ref[:, F + c * fc:F + (c + 1) * fc] = u.astype(gu_ref.dtype)
        uu = alpha * x + ffn
        xhat, _ = _ln_stats(uu)
        x2_ref[...] = xhat * g_ref[...] + b_ref[...]
        u2_ref[...] = uu

    return _call(
        kern, comm, name=f"ffn_fwd_{layer}", grid=(T // tm,),
        in_specs=[_rows(tm, D), _whole((1, D)), _whole((1, D)), _ANY, _ANY],
        out_specs=[_rows(tm, D), _rows(tm, D), _rows(tm, F), _rows(tm, F2), _rows(tm, D)],
        out_shape=[jax.ShapeDtypeStruct((T, D), _F32), jax.ShapeDtypeStruct((T, D), _F32),
                   jax.ShapeDtypeStruct((T, F), _ACT), jax.ShapeDtypeStruct((T, F2), _ACT),
                   jax.ShapeDtypeStruct((T, D), _ACT)],
        scratch_shapes=[pltpu.VMEM((D, F2), wfi.dtype), pltpu.VMEM((F, D), wfo.dtype),
                        pltpu.SemaphoreType.DMA((4,)), pltpu.SemaphoreType.DMA((4,))],
        args=(x1, gamma, beta, wfi, wfo), semantics=("arbitrary",))


def _ffn_bwd(u2, dy_or_target, gu, gamma, beta, wfi, wfo, alpha, layer, last):
    T, D = u2.shape
    F2 = gu.shape[1]
    F = F2 // 2
    tm = 256
    fc = F // 2

    def kern(u_ref, dy_ref, gu_ref, g_ref, b_ref, wi_h, wo_h, dx_ref, dub_ref, dgu_ref, st_ref, wi_v, wo_v, si, so):
        @pl.when(pl.program_id(0) == 0)
        def _():
            _load_cols(wi_h, wi_v, si)
            _load_rows(wo_h, wo_v, so)
            st_ref[...] = jnp.zeros_like(st_ref)

        gam = g_ref[...]
        u = u_ref[...]
        if last:
            xhat0, _ = _ln_stats(u)
            err = xhat0 * gam + b_ref[...] - dy_ref[...]
            dy = err * (1.0 / D)
            st_ref[2:3, :] += jnp.sum(err * err, axis=0, keepdims=True)
        else:
            dy = dy_ref[...]
        du, dgam, dbet, _ = _ln_bwd(u, dy, gam)
        st_ref[0:1, :] += dgam
        st_ref[1:2, :] += dbet
        dub = du.astype(_MXU)
        dub_ref[...] = dub.astype(dub_ref.dtype)
        dx = alpha * du
        for c in range(2):
            dact = _dot_nt(dub, wo_v[c * fc:(c + 1) * fc, :])
            g = gu_ref[:, c * fc:(c + 1) * fc].astype(_F32)
            uu = gu_ref[:, F + c * fc:F + (c + 1) * fc].astype(_F32)
            sg = jax.nn.sigmoid(g)
            dg = (dact * uu * (sg * (1.0 + g * (1.0 - sg)))).astype(_MXU)
            dup = (dact * (g * sg)).astype(_MXU)
            dgu_ref[:, c * fc:(c + 1) * fc] = dg.astype(dgu_ref.dtype)
            dgu_ref[:, F + c * fc:F + (c + 1) * fc] = dup.astype(dgu_ref.dtype)
            dx = dx + _dot_nt(dg, wi_v[:, c * fc:(c + 1) * fc]) + _dot_nt(dup, wi_v[:, F + c * fc:F + (c + 1) * fc])
        dx_ref[...] = dx

    return pl.pallas_call(
        kern, name=f"ffn_bwd_{layer}", grid=(T // tm,),
        in_specs=[_rows(tm, D), _rows(tm, D), _rows(tm, F2), _whole((1, D)), _whole((1, D)), _ANY, _ANY],
        out_specs=[_rows(tm, D), _rows(tm, D), _rows(tm, F2), _whole((8, D))],
        out_shape=[jax.ShapeDtypeStruct((T, D), _F32), jax.ShapeDtypeStruct((T, D), _ACT),
                   jax.ShapeDtypeStruct((T, F2), _ACT), jax.ShapeDtypeStruct((8, D), _F32)],
        scratch_shapes=[pltpu.VMEM((D, F2), wfi.dtype), pltpu.VMEM((F, D), wfo.dtype),
                        pltpu.SemaphoreType.DMA((4,)), pltpu.SemaphoreType.DMA((4,))],
        compiler_params=_cparams("arbitrary"),
    )(u2, dy_or_target, gu, gamma, beta, wfi, wfo)


def _residual_nt(res, res_scale, pieces, w, name, comm=None):
    T, K = res.shape
    widths = [p.shape[1] for p in pieces]
    N = sum(widths)
    tm = 512

    def kern(r_ref, *refs):
        d_refs, (w_hbm, o_ref, w_v, sem) = refs[:len(pieces)], refs[len(pieces):]

        @pl.when(pl.program_id(0) == 0)
        def _():
            _load_cols(w_hbm, w_v, sem)

        acc = res_scale * r_ref[...]
        off = 0
        for d_ref, width in zip(d_refs, widths):
            acc = acc + _dot_nt(d_ref[...].astype(_MXU), w_v[:, off:off + width])
            off += width
        o_ref[...] = acc

    outs, extra = _call(
        kern, comm, name=name, grid=(T // tm,),
        in_specs=[_rows(tm, K)] + [_rows(tm, width) for width in widths] + [_ANY], out_specs=[_rows(tm, K)],
        out_shape=[jax.ShapeDtypeStruct((T, K), _F32)],
        scratch_shapes=[pltpu.VMEM((K, N), w.dtype), pltpu.SemaphoreType.DMA((4,))],
        args=(res, *pieces, w), semantics=("arbitrary",))
    return outs[0], extra


def _grad_w_pieces(a, pieces, name, comm=None):
    T, M = a.shape
    widths = [p.shape[1] for p in pieces]
    bw = functools.reduce(_gcd, widths + [512])
    first = [sum(widths[:p]) // bw for p in range(len(pieces))]
    count = [width // bw for width in widths]
    N = sum(widths)
    tk = 1024 if T % 1024 == 0 else 512
    nk = T // tk

    def kern(a_ref, *refs):
        b_refs, (o_ref, acc) = refs[:len(pieces)], refs[len(pieces):]
        j, k = pl.program_id(0), pl.program_id(1)

        @pl.when(k == 0)
        def _():
            acc[...] = jnp.zeros_like(acc)

        for b_ref, start, blocks in zip(b_refs, first, count):
            @pl.when(jnp.logical_and(j >= start, j < start + blocks))
            def _():
                acc[...] += _dot_tn(a_ref[...].astype(_MXU), b_ref[...].astype(_MXU))

        @pl.when(k == nk - 1)
        def _():
            o_ref[...] = acc[...].astype(o_ref.dtype)

    def piece_spec(start, blocks):
        def index(j, k):
            mine = jnp.logical_and(j >= start, j < start + blocks)
            return jnp.where(mine, k, 0), jnp.where(mine, j - start, 0)
        return pl.BlockSpec((tk, bw), index)

    outs, extra = _call(
        kern, comm, name=name, grid=(N // bw, nk),
        in_specs=[pl.BlockSpec((tk, M), lambda j, k: (k, 0))] + [piece_spec(s, c) for s, c in zip(first, count)],
        out_specs=[pl.BlockSpec((M, bw), lambda j, k: (0, j))],
        out_shape=[jax.ShapeDtypeStruct((M, N), _ACT)], scratch_shapes=[pltpu.VMEM((M, bw), _F32)],
        args=(a, *pieces), semantics=("parallel", "arbitrary"))
    return outs[0], extra


def _gcd(a, b):
    while b:
        a, b = b, a % b
    return a


def _mix_bwd(u1, dx1, oa, ob, hg, wpa, wpb, wo, bg, gamma, layer):
    T, D = u1.shape
    WA, WB = wpa.shape[-2], wpb.shape[-2]
    tm = 512

    def kern(u_ref, dx_ref, oa_ref, ob_ref, hg_ref, bg_ref, g_ref, wpa_h, wpb_h, wo_h,
             du_ref, dub_ref, dya_ref, dyb_ref, dhg_ref, doa_ref, dob_ref, st_ref,
             wpa_v, wpb_v, wo_v, sa, sb, so):
        @pl.when(pl.program_id(0) == 0)
        def _():
            _load_cols(wpa_h, wpa_v, sa)
            _load_cols(wpb_h, wpb_v, sb)
            _load_rows(wo_h, wo_v, so)
            st_ref[...] = jnp.zeros_like(st_ref)

        du, dgam, dbet, _ = _ln_bwd(u_ref[...], dx_ref[...], g_ref[...])
        st_ref[1:2, :D] += dgam
        st_ref[1:2, D:] += dbet
        du_ref[...] = du
        dub = du.astype(_MXU)
        dub_ref[...] = dub.astype(dub_ref.dtype)
        dpre = _dot_nt(dub, wo_v[...])
        hgv = hg_ref[...].astype(_F32)
        bgv = bg_ref[...]
        ga = jax.nn.sigmoid(hgv[:, :D] + bgv[:, :D])
        gb = jax.nn.sigmoid(hgv[:, D:] + bgv[:, D:])
        dya = (dpre * ga).astype(_MXU)
        dyb = (dpre * gb).astype(_MXU)
        dsa = dpre * _dot(oa_ref[...].astype(_MXU), wpa_v[...]) * (ga * (1.0 - ga))
        dsb = dpre * _dot(ob_ref[...].astype(_MXU), wpb_v[...]) * (gb * (1.0 - gb))
        st_ref[0:1, :D] += jnp.sum(dsa, axis=0, keepdims=True)
        st_ref[0:1, D:] += jnp.sum(dsb, axis=0, keepdims=True)
        dya_ref[...] = dya.astype(dya_ref.dtype)
        dyb_ref[...] = dyb.astype(dyb_ref.dtype)
        dhg_ref[:, :D] = dsa.astype(dhg_ref.dtype)
        dhg_ref[:, D:] = dsb.astype(dhg_ref.dtype)
        doa_ref[...] = _dot_nt(dya, wpa_v[...]).astype(doa_ref.dtype)
        dob_ref[...] = _dot_nt(dyb, wpb_v[...]).astype(dob_ref.dtype)

    return pl.pallas_call(
        kern, name=f"mix_bwd_{layer}", grid=(T // tm,),
        in_specs=[_rows(tm, D), _rows(tm, D), _rows(tm, WA), _rows(tm, WB), _rows(tm, 2 * D), _whole((1, 2 * D)),
                  _whole((1, D)), _ANY, _ANY, _ANY],
        out_specs=[_rows(tm, D)] * 4 + [_rows(tm, 2 * D), _rows(tm, WA), _rows(tm, WB), _whole((8, 2 * D))],
        out_shape=[jax.ShapeDtypeStruct((T, D), _F32)] + [jax.ShapeDtypeStruct((T, D), _ACT)] * 3
        + [jax.ShapeDtypeStruct((T, 2 * D), _ACT), jax.ShapeDtypeStruct((T, WA), _ACT),
           jax.ShapeDtypeStruct((T, WB), _ACT), jax.ShapeDtypeStruct((8, 2 * D), _F32)],
        scratch_shapes=[pltpu.VMEM((WA, D), wpa.dtype), pltpu.VMEM((WB, D), wpb.dtype), pltpu.VMEM((D, D), wo.dtype),
                        pltpu.SemaphoreType.DMA((4,)), pltpu.SemaphoreType.DMA((4,)), pltpu.SemaphoreType.DMA((4,))],
        compiler_params=_cparams("arbitrary"),
    )(u1, dx1, oa, ob, hg, bg, gamma, wpa, wpb, wo)


def _grad_w(a, b, *, col_shards, name, comm=None):
    T, M = a.shape
    N = b.shape[1]
    tk = 1024 if T % 1024 == 0 else 512
    n = N // 4 if col_shards else N
    whole = M * N * 4 <= _GRAD_ACC_BYTES
    tn = N if whole else (n if col_shards else _divisor_tile(N, _GRAD_ACC_BYTES // (4 * M)))
    nk = T // tk

    def kern(a_ref, b_ref, o_ref, acc):
        k = pl.program_id(1)

        @pl.when(k == 0)
        def _():
            acc[...] = jnp.zeros_like(acc)

        acc[...] += _dot_tn(a_ref[...].astype(_MXU), b_ref[...].astype(_MXU))

        @pl.when(k == nk - 1)
        def _():
            if col_shards and whole:
                for s in range(4):
                    o_ref[s] = acc[:, s * n:(s + 1) * n].astype(o_ref.dtype)
            else:
                o_ref[...] = acc[...].astype(o_ref.dtype)

    if col_shards:
        out_spec = (pl.BlockSpec((4, M, n), lambda j, k: (0, 0, 0)) if whole
                    else pl.BlockSpec((None, M, n), lambda j, k: (j, 0, 0)))
        out_shape = jax.ShapeDtypeStruct((4, M, n), _ACT)
    else:
        out_spec = pl.BlockSpec((M, tn), lambda j, k: (0, j))
        out_shape = jax.ShapeDtypeStruct((M, N), _ACT)
    outs, extra = _call(
        kern, comm, name=name, grid=(N // tn, nk),
        in_specs=[pl.BlockSpec((tk, M), lambda j, k: (k, 0)), pl.BlockSpec((tk, tn), lambda j, k: (k, j))],
        out_specs=[out_spec], out_shape=[out_shape], scratch_shapes=[pltpu.VMEM((M, tn), _F32)],
        args=(a, b), semantics=("parallel", "arbitrary"))
    return outs[0], extra


def _bias_tiles(rel):
    H = rel.shape[0]
    span = _TQ * _BAND_TILES - 1
    edge = span - _REL_CLIP
    gvec = jnp.concatenate([jnp.broadcast_to(rel[:, :1], (H, edge)), rel, jnp.broadcast_to(rel[:, -1:], (H, edge))], axis=1)
    width = _BIAS_TILES * _TQ
    period = width + _TQ
    tiled = jnp.broadcast_to(jnp.pad(gvec[:, ::-1], ((0, 0), (0, 1)))[:, None, :], (H, _TQ, period))
    rows = tiled.reshape(H, _TQ * period)[:, :_TQ * (period - 1)].reshape(H, _TQ, period - 1)[:, :, _TQ - 1:]
    r = jnp.arange(_TQ)[:, None]
    u = jnp.arange(width)[None, :]
    d = 4 * _TQ + r - u
    rm = r % _CHUNK
    valid = (d >= rm - (_CHUNK - 1)) & (d <= rm + 8 * _CHUNK)
    tiles = jnp.where(valid[None], rows, _MASKED)
    return tiles.reshape(H // 2, 2 * _TQ, _BIAS_TILES, _TQ).transpose(0, 2, 1, 3)


def _fold_bias_grad(db):
    H = 2 * db.shape[0]
    width = _BIAS_TILES * _TQ
    period = width + _TQ
    x = jnp.pad(db.transpose(0, 2, 1, 3).reshape(H, _TQ, width), ((0, 0), (0, 0), (_TQ - 1, 0)))
    skew = jnp.pad(x.reshape(H, _TQ * (period - 1)), ((0, 0), (0, _TQ))).reshape(H, _TQ, period)
    dg = skew.sum(axis=1)[:, :period - 1][:, ::-1]
    span = _TQ * _BAND_TILES - 1
    edge = span - _REL_CLIP
    mid = dg[:, edge:edge + 2 * _REL_CLIP + 1]
    lo = dg[:, :edge].sum(axis=1)
    hi = dg[:, edge + 2 * _REL_CLIP + 1:].sum(axis=1)
    return mid.at[:, 0].add(lo).at[:, -1].add(hi)


def _band_window(i):
    j0 = jnp.maximum(i - (_BAND_TILES - 1), 0)
    return j0, (_BAND_TILES - 1) - (i - j0)


def _head_masks():
    lane = lax.broadcasted_iota(jnp.int32, (1, _LANES), 1)
    return [(lane // _HEAD) == hh for hh in range(2)]


def _stack_heads(x, masks):
    return jnp.concatenate([jnp.where(m, x, jnp.zeros_like(x)) for m in masks], axis=0)


def _unstack_heads(y, masks):
    return jnp.where(masks[0], y[:_TQ], y[_TQ:])


def _scaled(q):
    return q * jnp.asarray(_HEAD ** -0.5, q.dtype)


def _band_probs(q2, k_ref, b_ref, j0, boff):
    s = []
    for j in range(_BAND_TILES):
        kj = k_ref[pl.ds(pl.multiple_of((j0 + j) * _TQ, _TQ), _TQ), :]
        s.append(_dot_nt(q2, kj) + b_ref[boff + j])
    m = jnp.max(functools.reduce(jnp.maximum, s), axis=-1, keepdims=True)
    p = [jnp.exp(x - m) for x in s]
    l = jnp.sum(functools.reduce(lambda a, b: a + b, p), axis=-1, keepdims=True)
    return p, 1.0 / l


def _qkv_specs(T, cb, npair, tq=_TQ):
    return [pl.BlockSpec((tq, _LANES), lambda h, i: (i, cb + h)),
            pl.BlockSpec((T, _LANES), lambda h, i: (0, cb + npair + h)),
            pl.BlockSpec((T, _LANES), lambda h, i: (0, cb + 2 * npair + h))]


def _attn_a_fwd(hq, bias, col0, width, layer, comm=None):
    T = hq.shape[0]
    npair = width // _LANES
    nsub = _BAND_SUBTILES
    tq = nsub * _TQ

    def kern(q_ref, k_ref, v_ref, b_ref, o_ref):
        masks = _head_masks()
        q = _scaled(q_ref[...])
        for s in range(nsub):
            part = slice(s * _TQ, (s + 1) * _TQ)
            j0, boff = _band_window(nsub * pl.program_id(1) + s)
            p, inv = _band_probs(_stack_heads(q[part], masks), k_ref, b_ref, j0, boff)
            o = jnp.zeros((2 * _TQ, _LANES), _F32)
            for j in range(_BAND_TILES):
                vj = v_ref[pl.ds(pl.multiple_of((j0 + j) * _TQ, _TQ), _TQ), :]
                o = o + _dot(p[j].astype(_MXU), vj)
            o_ref[part, :] = _unstack_heads(o * inv, masks).astype(o_ref.dtype)

    outs, extra = _call(
        kern, comm, name=f"band_attn_fwd_{layer}", grid=(npair, T // tq),
        in_specs=_qkv_specs(T, col0 // _LANES, npair, tq)
        + [pl.BlockSpec((None, _BIAS_TILES, 2 * _TQ, _TQ), lambda h, i: (h, 0, 0, 0))],
        out_specs=[pl.BlockSpec((tq, _LANES), lambda h, i: (i, h))],
        out_shape=[jax.ShapeDtypeStruct((T, width), _ACT)], scratch_shapes=[],
        args=(hq, hq, hq, bias), semantics=("arbitrary", "arbitrary"))
    return outs[0], extra


def _attn_a_bwd(hq, bias, do, col0, width, layer, comm=None):
    T = hq.shape[0]
    npair = width // _LANES
    nsub = _BAND_SUBTILES
    tq = nsub * _TQ
    nq = T // tq
    scale = _HEAD ** -0.5

    def kern(q_ref, k_ref, v_ref, b_ref, do_ref, dq_ref, dk_ref, dv_ref, db_ref, dk_acc, dv_acc):
        i = pl.program_id(1)

        @pl.when(i == 0)
        def _():
            dk_acc[...] = jnp.zeros_like(dk_acc)
            dv_acc[...] = jnp.zeros_like(dv_acc)
            db_ref[...] = jnp.zeros_like(db_ref)

        masks = _head_masks()
        q = _scaled(q_ref[...])
        do_t = do_ref[...]
        for s in range(nsub):
            part = slice(s * _TQ, (s + 1) * _TQ)
            j0, boff = _band_window(nsub * i + s)
            q2 = _stack_heads(q[part], masks)
            do2 = _stack_heads(do_t[part], masks).astype(_MXU)
            p, inv = _band_probs(q2, k_ref, b_ref, j0, boff)
            rows = [pl.ds(pl.multiple_of((j0 + j) * _TQ, _TQ), _TQ) for j in range(_BAND_TILES)]
            p = [x * inv for x in p]
            dp = [_dot_nt(do2, v_ref[rows[j], :]) for j in range(_BAND_TILES)]
            delta = jnp.sum(functools.reduce(lambda a, b: a + b, [p[j] * dp[j] for j in range(_BAND_TILES)]),
                            axis=-1, keepdims=True)
            dq = jnp.zeros((2 * _TQ, _LANES), _F32)
            for j in range(_BAND_TILES):
                ds = p[j] * (dp[j] - delta)
                db_ref[boff + j] += ds
                dsb = ds.astype(_MXU)
                dq = dq + _dot(dsb, k_ref[rows[j], :])
                dk_acc[rows[j], :] += _dot_tn(dsb, q2)
                dv_acc[rows[j], :] += _dot_tn(p[j].astype(_MXU), do2)
            dq_ref[part, :] = (_unstack_heads(dq, masks) * scale).astype(dq_ref.dtype)

        @pl.when(i == nq - 1)
        def _():
            dk_ref[...] = dk_acc[...].astype(dk_ref.dtype)
            dv_ref[...] = dv_acc[...].astype(dv_ref.dtype)

    strip = pl.BlockSpec((None, _BIAS_TILES, 2 * _TQ, _TQ), lambda h, i: (h, 0, 0, 0))
    tile = pl.BlockSpec((tq, _LANES), lambda h, i: (i, h))
    column = pl.BlockSpec((T, _LANES), lambda h, i: (0, h))
    outs, extra = _call(
        kern, comm, name=f"band_attn_bwd_{layer}", grid=(npair, nq),
        in_specs=_qkv_specs(T, col0 // _LANES, npair, tq) + [strip, tile],
        out_specs=[tile, column, column, strip],
        out_shape=[jax.ShapeDtypeStruct((T, width), _ACT)] * 3
        + [jax.ShapeDtypeStruct((npair, _BIAS_TILES, 2 * _TQ, _TQ), _F32)],
        scratch_shapes=[pltpu.VMEM((T, _LANES), _F32), pltpu.VMEM((T, _LANES), _F32)],
        args=(hq, hq, hq, bias, do), semantics=("arbitrary", "arbitrary"))
    return outs, extra


def _suffix_matrix():
    r = lax.broadcasted_iota(jnp.int32, (_TQ, _TQ), 0)
    c = lax.broadcasted_iota(jnp.int32, (_TQ, _TQ), 1)
    r2 = lax.broadcasted_iota(jnp.int32, (2 * _TQ, _TQ), 0)
    c2 = lax.broadcasted_iota(jnp.int32, (2 * _TQ, _TQ), 1)
    return (r > c).astype(_MXU), c2 - (r2 & (_TQ - 1))


def _suffix_sums(xs, tri):
    n, k = xs[0].shape[0], len(xs)
    his = [x.astype(_MXU) for x in xs]
    los = [(x - h.astype(_F32)).astype(_MXU) for x, h in zip(xs, his)]
    y = _dot(jnp.concatenate(his + los, axis=0), tri)
    return [y[j * n:(j + 1) * n] + y[(k + j) * n:(k + j + 1) * n] for j in range(k)]


def _stick_tiles(tiles, rel, carry_l, tri):
    zs = [_dot_nt(qs, kj) for qs, kj, _, _ in tiles]
    Ls, masks = [], []
    for z, (_, _, jj, _) in zip(zs, tiles):
        nsp = -(jnp.maximum(z, 0.0) + jnp.log(1.0 + jnp.exp(-jnp.abs(z))))
        if isinstance(jj, int):
            mask = (rel < 0) if jj == 0 else None
        else:
            mask = rel < jnp.where(jj == 0, 0, _TQ)
        Ls.append(nsp if mask is None else jnp.where(mask, nsp, 0.0))
        masks.append(mask)
    carry_l = list(carry_l)
    ws = []
    for z, L, suffix, mask, (_, _, _, sub) in zip(zs, Ls, _suffix_sums(Ls, tri), masks, tiles):
        w = jnp.exp(z + L + suffix + carry_l[sub])
        ws.append(w if mask is None else jnp.where(mask, w, 0.0))
        carry_l[sub] = carry_l[sub] + jnp.sum(L, axis=-1, keepdims=True)
    return zs, Ls, ws, masks, carry_l


def _sweep(i, step, zero):
    nsub = _SB_SUBTILES

    def window():
        tiles = [(s, jj) for jj in range(_SB_WINDOW) for s in range(nsub)]
        return tuple((jnp.int32(_SB_WINDOW),) + c for c in step(tiles, [zero] * nsub))

    start = lax.cond(i >= -(-(_SB_WINDOW - 1) // nsub), window, lambda: tuple((jnp.int32(0),) + zero for _ in range(nsub)))
    outs = []
    for s in range(nsub):
        def done(c, s=s):
            return jnp.logical_or(c[0] > nsub * i + s, jnp.max(c[1]) < _EXP_ZERO_BELOW)

        def more(c, s=s):
            carries = [None] * nsub
            carries[s] = c[1:]
            return (c[0] + 1,) + step([(s, c[0])], carries)[s]

        outs.append(lax.while_loop(lambda c, done=done: jnp.logical_not(done(c)), more, start[s]))
    return outs


def _sb_fwd(hq, col0, width, layer, comm=None):
    T = hq.shape[0]
    npair = width // _LANES
    nsub = _SB_SUBTILES
    tq = nsub * _TQ

    def kern(q_ref, k_ref, v_ref, o_ref):
        i = pl.program_id(1)
        masks = _head_masks()
        tri, rel = _suffix_matrix()
        q = _scaled(q_ref[...])
        q2 = [_stack_heads(q[s * _TQ:(s + 1) * _TQ], masks) for s in range(nsub)]

        def step(tiles, carries):
            rows = [pl.ds(pl.multiple_of((nsub * i + s - jj) * _TQ, _TQ), _TQ) for s, jj in tiles]
            cls = [None if c is None else c[0] for c in carries]
            accs = [None if c is None else c[1] for c in carries]
            _, _, ws, _, cls = _stick_tiles([(q2[s], k_ref[r, :], jj, s) for (s, jj), r in zip(tiles, rows)], rel, cls, tri)
            for w, r, (s, _) in zip(ws, rows, tiles):
                accs[s] = accs[s] + _dot(w.astype(_MXU), v_ref[r, :])
            return [None if c is None else (cls[s], accs[s]) for s, c in enumerate(carries)]

        outs = _sweep(i, step, (jnp.zeros((2 * _TQ, 1), _F32), jnp.zeros((2 * _TQ, _LANES), _F32)))
        for s in range(nsub):
            o_ref[s * _TQ:(s + 1) * _TQ, :] = _unstack_heads(outs[s][2], masks)

    outs, extra = _call(
        kern, comm, name=f"stick_attn_fwd_{layer}", grid=(npair, T // tq),
        in_specs=_qkv_specs(T, col0 // _LANES, npair, tq),
        out_specs=[pl.BlockSpec((tq, _LANES), lambda h, i: (i, h))],
        out_shape=[jax.ShapeDtypeStruct((T, width), _F32)], scratch_shapes=[],
        args=(hq, hq, hq), semantics=("arbitrary", "arbitrary"))
    return outs[0], extra


def _sb_bwd(hq, o, do, col0, width, layer, comm=None):
    T = hq.shape[0]
    npair = width // _LANES
    nsub = _SB_SUBTILES
    tq = nsub * _TQ
    nq = T // tq
    scale = _HEAD ** -0.5

    def kern(q_ref, k_ref, v_ref, o_ref, do_ref, dq_ref, dk_ref, dv_ref, dk_acc, dv_acc):
        i = pl.program_id(1)

        @pl.when(i == 0)
        def _():
            dk_acc[...] = jnp.zeros_like(dk_acc)
            dv_acc[...] = jnp.zeros_like(dv_acc)

        masks = _head_masks()
        tri, rel = _suffix_matrix()
        q = _scaled(q_ref[...])
        do_t = do_ref[...]
        prod = do_t.astype(_F32) * o_ref[...]
        part = [slice(s * _TQ, (s + 1) * _TQ) for s in range(nsub)]
        q2 = [_stack_heads(q[p], masks) for p in part]
        do2 = [_stack_heads(do_t[p], masks).astype(_MXU) for p in part]
        dsum = [jnp.sum(_stack_heads(prod[p], masks), axis=-1, keepdims=True) for p in part]

        def step(tiles, carries):
            rows = [pl.ds(pl.multiple_of((nsub * i + s - jj) * _TQ, _TQ), _TQ) for s, jj in tiles]
            kjs = [k_ref[r, :] for r in rows]
            cls, cgs, dqs = ([None if c is None else c[n] for c in carries] for n in range(3))
            zs, Ls, ws, tile_masks, cls = _stick_tiles([(q2[s], kj, jj, s) for (s, jj), kj in zip(tiles, kjs)], rel, cls, tri)
            wbs = [w.astype(_MXU) for w in ws]
            gs = [wb.astype(_F32) * _dot_nt(do2[s], v_ref[r, :]) for wb, r, (s, _) in zip(wbs, rows, tiles)]
            for z, L, g, later, mask, wb, kj, r, (s, _) in zip(zs, Ls, gs, _suffix_sums(gs, tri), tile_masks, wbs, kjs,
                                                               rows, tiles):
                dz = g - jnp.exp(z + L) * (dsum[s] - (later + cgs[s]))
                if mask is not None:
                    dz = jnp.where(mask, dz, 0.0)
                dzb = dz.astype(_MXU)
                dk_acc[r, :] += _dot_tn(dzb, q2[s])
                dv_acc[r, :] += _dot_tn(wb, do2[s])
                dqs[s] = dqs[s] + _dot(dzb, kj)
                cgs[s] = cgs[s] + jnp.sum(g, axis=-1, keepdims=True)
            return [None if c is None else (cls[s], cgs[s], dqs[s]) for s, c in enumerate(carries)]

        zc = jnp.zeros((2 * _TQ, 1), _F32)
        outs = _sweep(i, step, (zc, zc, jnp.zeros((2 * _TQ, _LANES), _F32)))
        for s in range(nsub):
            dq_ref[part[s], :] = (_unstack_heads(outs[s][3], masks) * scale).astype(dq_ref.dtype)

        @pl.when(i == nq - 1)
        def _():
            dk_ref[...] = dk_acc[...].astype(dk_ref.dtype)
            dv_ref[...] = dv_acc[...].astype(dv_ref.dtype)

    tile_spec = pl.BlockSpec((tq, _LANES), lambda h, i: (i, h))
    column = pl.BlockSpec((T, _LANES), lambda h, i: (0, h))
    outs, extra = _call(
        kern, comm, name=f"stick_attn_bwd_{layer}", grid=(npair, nq),
        in_specs=_qkv_specs(T, col0 // _LANES, npair, tq) + [tile_spec, tile_spec],
        out_specs=[tile_spec, column, column],
        out_shape=[jax.ShapeDtypeStruct((T, width), _ACT)] * 3,
        scratch_shapes=[pltpu.VMEM((T, _LANES), _F32), pltpu.VMEM((T, _LANES), _F32)],
        args=(hq, hq, hq, o, do), semantics=("arbitrary", "arbitrary"))
    return outs, extra


_DENSE = ("w_in", "w_proj_a", "w_proj_b", "w_out", "w_ffn_in", "w_ffn_out")
_COL_SHARDED = {"w_in": True, "w_proj_a": True, "w_proj_b": True, "w_out": False, "w_ffn_in": True, "w_ffn_out": False}
_SMALL = ("b_gate", "rel_bias", "ln1_g", "ln1_b", "ln2_g", "ln2_b")


class _Plans:
    def __init__(self, plans=None):
        self.plans = plans or {}

    def start(self, key):
        if key not in self.plans:
            return None, None
        return self.plans[key]()

    @staticmethod
    def finish(done, extra):
        if done is not None:
            done(extra)


def _layer_fwd(x, W, small, l, alpha, plans):
    WA = small["rel_bias"].shape[1] * _HEAD
    row = lambda v: v[l].reshape(1, -1)
    comm, done = plans.start(f"in_proj_{l}")
    (hq, hg, xb), extra = _in_proj(x, W["w_in"], l, comm)
    plans.finish(done, extra)
    WB = (hq.shape[1] - 3 * WA) // 3
    bias = _bias_tiles(small["rel_bias"][l])
    comm, done = plans.start(f"band_fwd_{l}")
    oa, extra = _attn_a_fwd(hq, bias, 0, WA, l, comm)
    plans.finish(done, extra)
    comm, done = plans.start(f"stick_fwd_{l}")
    ob, extra = _sb_fwd(hq, 3 * WA, WB, l, comm)
    plans.finish(done, extra)
    comm, done = plans.start(f"mix_fwd_{l}")
    (x1, u1, pre), extra = _mix_fwd(oa, ob, hg, x, W["w_proj_a"], W["w_proj_b"], W["w_out"], row(small["b_gate"]),
                                            row(small["ln1_g"]), row(small["ln1_b"]), alpha, l, comm)
    plans.finish(done, extra)
    comm, done = plans.start(f"ffn_fwd_{l}")
    (x2, u2, act, gu, x1b), extra = _ffn_fwd(x1, W["w_ffn_in"], W["w_ffn_out"], row(small["ln2_g"]),
                                             row(small["ln2_b"]), alpha, l, comm)
    plans.finish(done, extra)
    return x2, dict(xb=xb, hq=hq, hg=hg, bias=bias, oa=oa, ob=ob, x1b=x1b, u1=u1, pre=pre, u2=u2, act=act, gu=gu)


def _layer_bwd(dy_or_target, S, W, small, l, last, alpha, plans, gw):
    D = S["xb"].shape[1]
    WA, WB = S["oa"].shape[1], S["ob"].shape[1]
    row = lambda v: v[l].reshape(1, -1)

    def blocks(g, n):
        return g if _COL_SHARDED[n] else g.reshape(4, g.shape[0] // 4, g.shape[1])

    dx1, du2b, dgu, st2 = _ffn_bwd(S["u2"], dy_or_target, S["gu"], row(small["ln2_g"]), row(small["ln2_b"]),
                                   W["w_ffn_in"], W["w_ffn_out"], alpha, l, last)
    gw["w_ffn_in"] = blocks(_grad_w(S["x1b"], dgu, col_shards=True, name=f"grad_w_ffn_in_{l}")[0], "w_ffn_in")
    gw["w_ffn_out"] = blocks(_grad_w(S["act"], du2b, col_shards=False, name=f"grad_w_ffn_out_{l}")[0], "w_ffn_out")
    du1, du1b, dya, dyb, dhg, doa, dob, st1 = _mix_bwd(S["u1"], dx1, S["oa"], S["ob"], S["hg"], W["w_proj_a"],
                                                       W["w_proj_b"], W["w_out"], row(small["b_gate"]),
                                                       row(small["ln1_g"]), l)
    gw["w_out"] = blocks(_grad_w(S["pre"], du1b, col_shards=False, name=f"grad_w_out_{l}")[0], "w_out")
    gw["w_proj_a"] = blocks(_grad_w(S["oa"], dya, col_shards=True, name=f"grad_w_proj_a_{l}")[0], "w_proj_a")
    gw["w_proj_b"] = blocks(_grad_w(S["ob"], dyb, col_shards=True, name=f"grad_w_proj_b_{l}")[0], "w_proj_b")
    comm, done = plans.start(f"band_bwd_{l}")
    (dqa, dka, dva, dbias), extra = _attn_a_bwd(S["hq"], S["bias"], doa, 0, WA, l, comm)
    plans.finish(done, extra)
    comm, done = plans.start(f"stick_bwd_{l}")
    (dqb, dkb, dvb), extra = _sb_bwd(S["hq"], S["ob"], dob, 3 * WA, WB, l, comm)
    plans.finish(done, extra)
    dh = [dqa, dka, dva, dqb, dkb, dvb, dhg]
    comm, done = plans.start(f"grad_w_in_{l}")
    gw["w_in"], extra = _grad_w_pieces(S["xb"], dh, f"grad_w_in_{l}", comm)
    plans.finish(done, extra)
    comm, done = plans.start(f"in_proj_bwd_{l}")
    dx, extra = _residual_nt(du1, alpha, dh, W["w_in"], f"in_proj_bwd_{l}", comm)
    plans.finish(done, extra)
    gs = dict(b_gate=st1[0], rel_bias=_fold_bias_grad(dbias), ln1_g=st1[1, :D], ln1_b=st1[1, D:],
              ln2_g=st2[0], ln2_b=st2[1])
    return dx, gs, st2[2]


def _local_step(x, target, W, small, plans=None, gws=None):
    depth = len(W)
    alpha = float((2 * depth) ** 0.25)
    plans = plans or _Plans()
    gws = gws if gws is not None else [dict() for _ in range(depth)]
    saved = []
    h = x
    for l in range(depth):
        h, S = _layer_fwd(h, W[l], small, l, alpha, plans)
        saved.append(S)
    gss = [None] * depth
    d = target
    sq = None
    for l in reversed(range(depth)):
        d, gss[l], sq_l = _layer_bwd(d, saved[l], W[l], small, l, l == depth - 1, alpha, plans, gws[l])
        if l == depth - 1:
            sq = sq_l
    return sq, d, gws, gss


def _place():
    return lax.axis_index("x"), lax.axis_index("y"), lax.axis_index("c")


def _remote(src, dst, send_sem, recv_sem, to):
    return pltpu.make_async_remote_copy(src_ref=src, dst_ref=dst, send_sem=send_sem, recv_sem=recv_sem,
                                        device_id=to, device_id_type=_MESH)


def _half(ref, hc):
    kh = ref.shape[0] // 2
    return ref.at[pl.ds(pl.multiple_of(hc * kh, 16), kh), :]


def _gather_plan(blocks, fractions):
    nt = len(blocks)

    def run(step, nsteps, ins, outs, sems):
        send_sems, recv_sems, loc_sems = sems
        x, y, c = _place()
        k = 2 * x + y
        me, sibling = (x, y, c), (x, y, 1 - c)
        chips = [(1 - x, y), (x, 1 - y), (1 - x, 1 - y)]
        chip_k = [2 * cx + cy for cx, cy in chips]

        def ici(t, s, owner_k, to, src=None):
            dst = _half(outs[t].at[owner_k], c)
            return _remote(dst if src is None else src, dst, send_sems.at[t, s], recv_sems.at[t, s], to)

        def passed(t, s, hc, to):
            blk = _half(outs[t].at[chip_k[s]], hc)
            return _remote(blk, blk, send_sems.at[t, 3 + s], recv_sems.at[t, 3 + s], to)

        def local(t):
            return pltpu.make_async_copy(ins[t], outs[t].at[k], loc_sems.at[t])

        @pl.when(step == 0)
        def _():
            for t in range(nt):
                local(t).start()
                for s, chip in enumerate(chips):
                    ici(t, s, k, (*chip, c), src=_half(ins[t], c)).start()

        for t in range(nt):
            @pl.when(step == min(nsteps - 1, int(fractions[t] * nsteps)))
            def _():
                for s in range(3):
                    ici(t, s, chip_k[s], me).wait_recv()
                    passed(t, s, c, sibling).start()

        @pl.when(step == nsteps - 1)
        def _():
            for t in range(nt):
                for s, chip in enumerate(chips):
                    passed(t, s, 1 - c, me).wait_recv()
            for t in range(nt):
                for s, chip in enumerate(chips):
                    ici(t, s, k, (*chip, c), src=_half(ins[t], c)).wait_send()
                    passed(t, s, c, sibling).wait_send()
                local(t).wait()

    return _Comm(blocks, [jax.ShapeDtypeStruct((4,) + b.shape, b.dtype) for b in blocks],
                 [pltpu.SemaphoreType.DMA((nt, 6)), pltpu.SemaphoreType.DMA((nt, 6)), pltpu.SemaphoreType.DMA((nt,))], run)


def _scatter_plan(grads, owners):
    nt = len(grads)
    shapes = [g.shape[1:] if g.ndim == 3 else (g.shape[0], g.shape[1] // 4) for g in grads]

    def run(step, nsteps, ins, outs, sems):
        send_sems, recv_sems, loc_sems = sems
        x, y, c = _place()
        me = 4 * x + 2 * y + c

        def target(r):
            tx = 1 - x if r & 2 else x
            ty = 1 - y if r & 1 else y
            return tx, ty

        def block(t, chip):
            if len(ins[t].shape) == 3:
                return ins[t].at[chip]
            n = shapes[t][1]
            return ins[t].at[:, pl.ds(pl.multiple_of(chip * n, _LANES), n)]

        def send(t, r):
            tx, ty = target(r)
            return _remote(block(t, 2 * tx + ty), outs[t].at[me], send_sems.at[t, r], recv_sems.at[t, 2 * r + c],
                           (tx, ty, owners[t]))

        def local(t):
            return pltpu.make_async_copy(block(t, 2 * x + y), outs[t].at[me], loc_sems.at[t])

        @pl.when(step == 0)
        def _():
            for t in range(nt):
                @pl.when(c == owners[t])
                def _():
                    local(t).start()

                @pl.when(c != owners[t])
                def _():
                    send(t, 0).start()

                for r in range(1, 4):
                    send(t, r).start()

        @pl.when(step == nsteps - 1)
        def _():
            for t in range(nt):
                @pl.when(c == owners[t])
                def _():
                    for r in range(4):
                        sx, sy = target(r)
                        for cs in range(2):
                            if r == 0 and cs == owners[t]:
                                continue
                            src_dev = 4 * sx + 2 * sy + cs
                            _remote(block(t, 0), outs[t].at[src_dev], send_sems.at[t, r], recv_sems.at[t, 2 * r + cs],
                                    (x, y, c)).wait_recv()
                    local(t).wait()

                @pl.when(c != owners[t])
                def _():
                    send(t, 0).wait_send()

                for r in range(1, 4):
                    send(t, r).wait_send()

    return _Comm(grads, [jax.ShapeDtypeStruct((8,) + s, g.dtype) for s, g in zip(shapes, grads)],
                 [pltpu.SemaphoreType.DMA((nt, 4)), pltpu.SemaphoreType.DMA((nt, 8)), pltpu.SemaphoreType.DMA((nt,))], run)


def _share_plan(reduced, owners):
    nt = len(reduced)

    def run(step, nsteps, ins, outs, sems):
        del ins
        send_sems, recv_sems = sems
        x, y, c = _place()

        def give(t, to):
            return _remote(outs[t], outs[t], send_sems.at[t], recv_sems.at[t], to)

        @pl.when(step == 0)
        def _():
            for t in range(nt):
                @pl.when(c == owners[t])
                def _():
                    give(t, (x, y, 1 - c)).start()

        @pl.when(step == nsteps - 1)
        def _():
            for t in range(nt):
                @pl.when(c == owners[t])
                def _():
                    give(t, (x, y, 1 - c)).wait_send()

                @pl.when(c != owners[t])
                def _():
                    give(t, (x, y, c)).wait_recv()

    return _Comm(reduced, [jax.ShapeDtypeStruct(r.shape, r.dtype) for r in reduced],
                 [pltpu.SemaphoreType.DMA((nt,)), pltpu.SemaphoreType.DMA((nt,))], run,
                 aliases={t: t for t in range(nt)})


def _join(a, b):
    ni, no, ns = len(a.inputs), len(a.out_shapes), len(a.sems)

    def run(step, nsteps, ins, outs, sems):
        a.run(step, nsteps, ins[:ni], outs[:no], sems[:ns])
        b.run(step, nsteps, ins[ni:], outs[no:], sems[ns:])

    aliases = dict(a.aliases)
    aliases.update({ni + i: no + o for i, o in b.aliases.items()})
    return _Comm(a.inputs + b.inputs, a.out_shapes + b.out_shapes, a.sems + b.sems, run, aliases)


def _peer(x, y, c, r):
    px = 1 - x if r & 4 else x
    py = 1 - y if r & 2 else y
    pc = 1 - c if r & 1 else c
    return (px, py, pc), 4 * px + 2 * py + pc


def _sum_slots(st, name):
    _, K, n = st.shape
    tr = next(t for t in (256, 128, 64, 32, 16) if K % t == 0)

    def kern(s_ref, o_ref):
        acc = s_ref[0].astype(_F32)
        for d in range(1, 8):
            acc = acc + s_ref[d].astype(_F32)
        o_ref[...] = acc.astype(o_ref.dtype)

    return pl.pallas_call(
        kern, name=name, grid=(K // tr,),
        in_specs=[pl.BlockSpec((8, tr, n), lambda i: (0, i, 0))], out_specs=_rows(tr, n),
        out_shape=jax.ShapeDtypeStruct((K, n), _ACT),
        compiler_params=_cparams("parallel"),
    )(st)


def _all_reduce_small(p):
    R = p.shape[0]

    def body(p_ref, o_ref, stage, send_sems, recv_sems):
        x, y, c = _place()
        me = 4 * x + 2 * y + c
        stage[me] = p_ref[...]
        sent = []
        for r in range(1, 8):
            to, _ = _peer(x, y, c, r)
            cp = _remote(p_ref, stage.at[me], send_sems.at[r - 1], recv_sems.at[r - 1], to)
            cp.start()
            sent.append(cp)
        for r in range(1, 8):
            _, src_dev = _peer(x, y, c, r)
            _remote(p_ref, stage.at[src_dev], send_sems.at[r - 1], recv_sems.at[r - 1], (x, y, c)).wait_recv()
        acc = stage[0]
        for d in range(1, 8):
            acc = acc + stage[d]
        o_ref[...] = acc
        for cp in sent:
            cp.wait_send()

    vm = pl.BlockSpec(memory_space=pltpu.VMEM)
    return pl.pallas_call(
        body, name="all_reduce_small",
        in_specs=[vm], out_specs=vm,
        out_shape=jax.ShapeDtypeStruct((R, _LANES), _F32),
        scratch_shapes=[pltpu.VMEM((8, R, _LANES), _F32), pltpu.SemaphoreType.DMA((7,)), pltpu.SemaphoreType.DMA((7,))],
    )(p)


def _adamw_update(gv, w_ref, m_ref, v_ref, gf_ref, d_ref, nm_ref, nv_ref):
    nm = _B1 * m_ref[...] + (1.0 - _B1) * gv
    nv = _B2 * v_ref[...] + (1.0 - _B2) * (gv * gv)
    m_hat = nm / (1.0 - _B1 ** _STEP)
    v_hat = nv / (1.0 - _B2 ** _STEP)
    gf_ref[...] = gv
    d_ref[...] = -_LR * (m_hat / (jnp.sqrt(v_hat) + _EPS) + _WD * w_ref[...])
    nm_ref[...] = nm
    nv_ref[...] = nv


def _adamw_layers(w, g_layers, m, v, name):
    _, K, n = w.shape
    tr = next(t for t in (256, 128, 64, 32, 16) if K % t == 0)

    def kern(w_ref, g0_ref, g1_ref, m_ref, v_ref, *out_refs):
        first = pl.program_id(0) == 0
        gv = jnp.where(first, g0_ref[...].astype(_F32), g1_ref[...].astype(_F32))
        _adamw_update(gv, w_ref, m_ref, v_ref, *out_refs)

    stacked = pl.BlockSpec((None, tr, n), lambda l, i: (l, i, 0))
    layer = pl.BlockSpec((tr, n), lambda l, i: (i, 0))
    return tuple(pl.pallas_call(
        kern, name=name, grid=(2, K // tr),
        in_specs=[stacked, layer, layer, stacked, stacked], out_specs=[stacked] * 4,
        out_shape=[jax.ShapeDtypeStruct(w.shape, _F32)] * 4,
        compiler_params=_cparams("parallel", "parallel"),
    )(w, g_layers[0], g_layers[1], m, v))


def _adamw(w, g, m, v, name):
    shape = w.shape
    w2, g2, m2, v2 = (a.reshape(-1, shape[-1]) for a in (w, g, m, v))
    R, C = w2.shape
    tr = next((t for t in (256, 128, 64, 32, 16) if R % t == 0), R)

    def kern(w_ref, g_ref, m_ref, v_ref, *out_refs):
        _adamw_update(g_ref[...].astype(_F32), w_ref, m_ref, v_ref, *out_refs)

    outs = pl.pallas_call(
        kern, name=name, grid=(R // tr,),
        in_specs=[_rows(tr, C)] * 4, out_specs=[_rows(tr, C)] * 4,
        out_shape=[jax.ShapeDtypeStruct((R, C), _F32)] * 4,
        compiler_params=_cparams("parallel"),
    )(w2, g2, m2, v2)
    return tuple(o.reshape(shape) for o in outs)


def _pack_small(gss, sq):
    parts = [gss[l][n].reshape(-1) for n in _SMALL for l in range(len(gss))] + [jnp.sum(sq).reshape(1)]
    flat = jnp.concatenate(parts)
    rows = -(-flat.shape[0] // (8 * _LANES)) * 8
    return jnp.pad(flat, (0, rows * _LANES - flat.shape[0])).reshape(rows, _LANES)


def _unpack_small(total, shapes):
    flat = total.reshape(-1)
    out, off = {}, 0
    for n in _SMALL:
        layers = []
        for _ in range(shapes[n][0]):
            size = 1
            for s in shapes[n][1:]:
                size *= s
            layers.append(flat[off:off + size].reshape(shapes[n][1:]))
            off += size
        out[n] = jnp.stack(layers)
    return out, flat[off]


_GATHER = {
    "band_fwd_0": [(0, "w_proj_a"), (0, "w_proj_b"), (0, "w_out"), (0, "w_ffn_out")],
    "stick_fwd_0": [(0, "w_ffn_in")],
    "mix_fwd_0": [(1, "w_in")],
    "ffn_fwd_0": [(1, "w_proj_a"), (1, "w_proj_b"), (1, "w_out"), (1, "w_ffn_in"), (1, "w_ffn_out")],
}
_SCATTER = {
    "band_bwd_1": [(1, "w_ffn_in"), (1, "w_ffn_out")],
    "stick_bwd_1": [(1, "w_proj_a"), (1, "w_proj_b"), (1, "w_out")],
    "band_bwd_0": [(1, "w_in"), (0, "w_ffn_in")],
    "stick_bwd_0": [(0, "w_ffn_out"), (0, "w_proj_a"), (0, "w_proj_b"), (0, "w_out")],
    "in_proj_bwd_0": [(0, "w_in")],
}
_SHARE = {"stick_bwd_1": "band_bwd_1", "band_bwd_0": "stick_bwd_1", "stick_bwd_0": "band_bwd_0", "grad_w_in_0": "stick_bwd_0"}


def _owner(key):
    del key
    return 1


def kernel(x, w_in, b_gate, rel_bias, w_proj_a, w_proj_b, w_out, ln1_g, ln1_b, w_ffn_in, w_ffn_out, ln2_g, ln2_b, loss_target, m_w_in, m_b_gate, m_rel_bias, m_w_proj_a, m_w_proj_b, m_w_out, m_ln1_g, m_ln1_b, m_w_ffn_in, m_w_ffn_out, m_ln2_g, m_ln2_b, v_w_in, v_b_gate, v_rel_bias, v_w_proj_a, v_w_proj_b, v_w_out, v_ln1_g, v_ln1_b, v_w_ffn_in, v_w_ffn_out, v_ln2_g, v_ln2_b):
    names = ("w_in", "b_gate", "rel_bias", "w_proj_a", "w_proj_b", "w_out", "ln1_g", "ln1_b", "w_ffn_in", "w_ffn_out", "ln2_g", "ln2_b")
    w = dict(zip(names, (w_in, b_gate, rel_bias, w_proj_a, w_proj_b, w_out, ln1_g, ln1_b, w_ffn_in, w_ffn_out, ln2_g, ln2_b)))
    m = dict(zip(names, (m_w_in, m_b_gate, m_rel_bias, m_w_proj_a, m_w_proj_b, m_w_out, m_ln1_g, m_ln1_b, m_w_ffn_in, m_w_ffn_out, m_ln2_g, m_ln2_b)))
    v = dict(zip(names, (v_w_in, v_b_gate, v_rel_bias, v_w_proj_a, v_w_proj_b, v_w_out, v_ln1_g, v_ln1_b, v_w_ffn_in, v_w_ffn_out, v_ln2_g, v_ln2_b)))
    T, D = x.shape[-2], x.shape[-1]
    assert w_in.shape[0] == 2, "the exchange schedule below is written for two layers"

    mine = [{n: w[n][l].astype(_MXU) for n in _DENSE} for l in range(2)]
    W = [dict(), dict()]
    gws = [dict(), dict()]
    slots, final = {}, {}

    def gather(keys):
        sizes = [mine[l][n].size for l, n in keys]
        passed, fractions = 0, []
        for s in sizes:
            passed += s
            fractions.append(0.15 + 0.6 * passed / sum(sizes))

        def done(outs):
            for (l, n), o in zip(keys, outs):
                W[l][n] = o
        return _gather_plan([mine[l][n] for l, n in keys], fractions), done

    def scatter(keys):
        comm = _scatter_plan([gws[l][n] for l, n in keys], [_owner(key) for key in keys])
        return comm, lambda outs: slots.update(zip(keys, outs))

    def share(keys):
        reduced = [_sum_slots(slots[key], f"sum_grad_{key[1]}_{key[0]}") for key in keys]
        comm = _share_plan(reduced, [_owner(key) for key in keys])
        return comm, lambda outs: final.update(zip(keys, outs))

    def both(first, second):
        (ca, da), (cb, db) = first, second
        na = len(ca.out_shapes)
        return _join(ca, cb), lambda outs: (da(outs[:na]), db(outs[na:]))

    comm, done = gather([(0, "w_in")])
    done(_comm_only(comm, "gather_first"))
    plans = {key: functools.partial(gather, keys) for key, keys in _GATHER.items()}
    for key, keys in _SCATTER.items():
        plans[key] = functools.partial(scatter, keys)
    for key, scattered_under in _SHARE.items():
        handed = functools.partial(share, _SCATTER[scattered_under])
        carried = plans.get(key)
        plans[key] = handed if carried is None else (lambda carried=carried, handed=handed: both(carried(), handed()))
    small = {n: w[n] for n in _SMALL}
    sq, dx, _, gss = _local_step(x.reshape(T, D), loss_target.reshape(T, D), W, small, _Plans(plans), gws)

    comm, done = share(_SCATTER["in_proj_bwd_0"])
    done(_comm_only(comm, "share_last"))
    total = _all_reduce_small(_pack_small(gss, sq))
    small_grads, sq_all = _unpack_small(total, {n: w[n].shape for n in _SMALL})
    loss = 0.5 * sq_all / D

    grad, delta, new_m, new_v = {}, {}, {}, {}
    for n in names:
        if n in _DENSE:
            updated = _adamw_layers(w[n], [final[(l, n)] for l in range(2)], m[n], v[n], f"adamw_{n}")
        else:
            updated = _adamw(w[n], small_grads[n], m[n], v[n], f"adamw_{n}")
        grad[n], delta[n], new_m[n], new_v[n] = updated
    return (loss, dx.reshape(x.shape), *[grad[n] for n in names], *[delta[n] for n in names],
            *[new_m[n] for n in names], *[new_v[n] for n in names])
```

```python
import functools

import jax
import jax.numpy as jnp
from jax import lax
from jax.experimental import pallas as pl
from jax.experimental.pallas import tpu as pltpu

_MXU = jnp.bfloat16
_ACT = jnp.bfloat16
_F32 = jnp.float32

_HEAD = 64
_CHUNK = 64
_LANES = 128
_TQ = 128
_BAND_TILES = 5
_BIAS_TILES = 9
_REL_CLIP = 256
_LN_EPS = 1e-5
_MASKED = -1e30
_EXP_ZERO_BELOW = -87.34
_SB_WINDOW = 2
_SB_SUBTILES = 4
_BAND_SUBTILES = 8
_VMEM_LIMIT = 56 * 1024 * 1024
_GRAD_ACC_BYTES = 12 * 1024 * 1024

_LR, _B1, _B2, _EPS, _WD, _STEP = 0.001, 0.9, 0.999, 1e-08, 0.01, 10

_MESH = pl.DeviceIdType.MESH


def _dot(a, b):
    return jnp.dot(a, b, preferred_element_type=_F32)


def _dot_nt(a, b):
    return lax.dot_general(a, b, (((1,), (1,)), ((), ())), preferred_element_type=_F32)


def _dot_tn(a, b):
    return lax.dot_general(a, b, (((0,), (0,)), ((), ())), preferred_element_type=_F32)


def _cparams(*sem):
    return pltpu.CompilerParams(dimension_semantics=sem, vmem_limit_bytes=_VMEM_LIMIT)


def _rows(t, c):
    return pl.BlockSpec((t, c), lambda i: (i, 0))


def _whole(shape):
    return pl.BlockSpec(shape, lambda i: tuple(0 for _ in shape))


_ANY = pl.BlockSpec(memory_space=pl.ANY)


def _load_cols(w_hbm, w_vmem, sem):
    n = w_hbm.shape[-1]
    cps = [pltpu.make_async_copy(w_hbm.at[k], w_vmem.at[:, pl.ds(k * n, n)], sem.at[k]) for k in range(4)]
    for cp in cps:
        cp.start()
    for cp in cps:
        cp.wait()


def _load_rows(w_hbm, w_vmem, sem):
    r = w_hbm.shape[-2]
    cps = [pltpu.make_async_copy(w_hbm.at[k], w_vmem.at[pl.ds(k * r, r), :], sem.at[k]) for k in range(4)]
    for cp in cps:
        cp.start()
    for cp in cps:
        cp.wait()


def _ln_stats(u):
    mu = jnp.mean(u, axis=-1, keepdims=True)
    xc = u - mu
    var = jnp.mean(xc * xc, axis=-1, keepdims=True)
    rstd = lax.rsqrt(var + _LN_EPS)
    return xc * rstd, rstd


def _ln_bwd(u, dy, gamma):
    xhat, rstd = _ln_stats(u)
    dxh = dy * gamma
    m1 = jnp.mean(dxh, axis=-1, keepdims=True)
    m2 = jnp.mean(dxh * xhat, axis=-1, keepdims=True)
    du = rstd * (dxh - m1 - xhat * m2)
    return du, jnp.sum(dy * xhat, axis=0, keepdims=True), jnp.sum(dy, axis=0, keepdims=True), xhat


def _divisor_tile(n, cap):
    best = None
    for t in range(_LANES, min(n, cap) + 1, _LANES):
        if n % t == 0:
            best = t
    return best or n


class _Comm:
    def __init__(self, inputs, out_shapes, sems, run, aliases=None):
        self.inputs, self.out_shapes, self.sems, self.run = list(inputs), list(out_shapes), list(sems), run
        self.aliases = aliases or {}


def _call(kern, comm, *, name, grid, in_specs, out_specs, out_shape, scratch_shapes, args, semantics):
    in_specs, out_specs, out_shape, scratch_shapes = list(in_specs), list(out_specs), list(out_shape), list(scratch_shapes)
    if comm is None:
        outs = pl.pallas_call(kern, name=name, grid=grid, in_specs=in_specs, out_specs=out_specs, out_shape=out_shape,
                              scratch_shapes=scratch_shapes, compiler_params=_cparams(*semantics))(*args)
        return list(outs), []
    n_in, n_out, n_scr = len(in_specs), len(out_specs), len(scratch_shapes)
    ci, co = len(comm.inputs), len(comm.out_shapes)
    nsteps = functools.reduce(lambda a, b: a * b, grid, 1)

    def fused(*refs):
        a, b = n_in, n_in + ci
        c, d = b + n_out, b + n_out + co
        e = d + n_scr
        step = pl.program_id(0)
        for ax in range(1, len(grid)):
            step = step * grid[ax] + pl.program_id(ax)
        comm.run(step, nsteps, refs[a:b], refs[c:d], refs[e:])
        kern(*refs[:a], *refs[b:c], *refs[d:e])

    outs = pl.pallas_call(
        fused, name=name, grid=grid, in_specs=in_specs + [_ANY] * ci, out_specs=out_specs + [_ANY] * co,
        out_shape=out_shape + comm.out_shapes, scratch_shapes=scratch_shapes + comm.sems,
        input_output_aliases={n_in + i: n_out + o for i, o in comm.aliases.items()},
        compiler_params=_cparams(*("arbitrary" for _ in grid)))(*args, *comm.inputs)
    return list(outs[:n_out]), list(outs[n_out:])


def _comm_only(comm, name):
    def body(*refs):
        ci, co = len(comm.inputs), len(comm.out_shapes)
        comm.run(0, 1, refs[:ci], refs[ci:ci + co], refs[ci + co:])

    outs = pl.pallas_call(body, name=name, in_specs=[_ANY] * len(comm.inputs), out_specs=[_ANY] * len(comm.out_shapes),
                          out_shape=comm.out_shapes, scratch_shapes=comm.sems,
                          input_output_aliases=dict(comm.aliases))(*comm.inputs)
    return list(outs)


def _in_proj(x, w_in, layer, comm=None):
    T, D = x.shape
    N = 4 * w_in.shape[-1]
    tm = 512

    def kern(x_ref, w_hbm, h_ref, xb_ref, w_v, sem):
        @pl.when(pl.program_id(0) == 0)
        def _():
            _load_cols(w_hbm, w_v, sem)

        xb = x_ref[...].astype(_MXU)
        h_ref[...] = _dot(xb, w_v[...]).astype(h_ref.dtype)
        xb_ref[...] = xb.astype(xb_ref.dtype)

    return _call(
        kern, comm, name=f"in_proj_{layer}", grid=(T // tm,),
        in_specs=[_rows(tm, D), _ANY],
        out_specs=[_rows(tm, N), _rows(tm, D)],
        out_shape=[jax.ShapeDtypeStruct((T, N), _ACT), jax.ShapeDtypeStruct((T, D), _ACT)],
        scratch_shapes=[pltpu.VMEM((D, N), w_in.dtype), pltpu.SemaphoreType.DMA((4,))],
        args=(x, w_in), semantics=("arbitrary",))


def _in_proj_gathering(x, block, layer):
    T, D = x.shape
    n = block.shape[1]
    tm = min(T, 1024)
    nrows = T // tm
    pass_steps = [int(f * nrows) for f in (0.6, 1.0, 1.7)]
    px, py, _ = _place()
    order = jnp.stack([2 * px + py, 2 * (1 - px) + py, 2 * px + (1 - py), 2 * (1 - px) + (1 - py)]).astype(jnp.int32)

    def kern(order_ref, x_ref, blk_hbm, h_ref, xb_ref, w_hbm, w_v, send_sems, recv_sems, loc_sem, load_sem):
        del order_ref
        step = pl.program_id(0) * nrows + pl.program_id(1)
        x_, y_, c = _place()
        k = 2 * x_ + y_
        me, sibling = (x_, y_, c), (x_, y_, 1 - c)
        chips = [(1 - x_, y_), (x_, 1 - y_), (1 - x_, 1 - y_)]
        chip_k = [2 * cx + cy for cx, cy in chips]

        def ici(s, owner_k, to, src=None):
            dst = _half(w_hbm.at[owner_k], c)
            return _remote(dst if src is None else src, dst, send_sems.at[s], recv_sems.at[s], to)

        def passed(s, hc, to):
            blk = _half(w_hbm.at[chip_k[s]], hc)
            return _remote(blk, blk, send_sems.at[3 + s], recv_sems.at[3 + s], to)

        local = pltpu.make_async_copy(blk_hbm, w_hbm.at[k], loc_sem.at[0])

        def load(src):
            cp = pltpu.make_async_copy(src, w_v, load_sem.at[0])
            cp.start()
            cp.wait()

        @pl.when(step == 0)
        def _():
            for s, chip in enumerate(chips):
                ici(s, k, (*chip, c), src=_half(blk_hbm, c)).start()
            local.start()
            load(blk_hbm)

        for s in range(3):
            @pl.when(step == pass_steps[s])
            def _():
                ici(s, chip_k[s], me).wait_recv()
                passed(s, c, sibling).start()

            @pl.when(step == (s + 1) * nrows)
            def _():
                passed(s, 1 - c, me).wait_recv()
                load(w_hbm.at[chip_k[s]])

        xb = x_ref[...].astype(_MXU)
        h_ref[...] = _dot(xb, w_v[...]).astype(h_ref.dtype)

        @pl.when(pl.program_id(0) == 0)
        def _():
            xb_ref[...] = xb.astype(xb_ref.dtype)

        @pl.when(step == 4 * nrows - 1)
        def _():
            for s, chip in enumerate(chips):
                ici(s, k, (*chip, c), src=_half(blk_hbm, c)).wait_send()
                passed(s, c, sibling).wait_send()
            local.wait()

    assert all(pass_steps[s] <= (s + 1) * nrows for s in range(3))
    h, xb, w_in = pl.pallas_call(
        kern, name=f"in_proj_{layer}",
        grid_spec=pltpu.PrefetchScalarGridSpec(
            num_scalar_prefetch=1, grid=(4, nrows),
            in_specs=[pl.BlockSpec((tm, D), lambda j, i, o: (i, 0)), _ANY],
            out_specs=[pl.BlockSpec((tm, n), lambda j, i, o: (i, o[j])),
                       pl.BlockSpec((tm, D), lambda j, i, o: (jnp.where(j == 0, i, nrows - 1), 0)), _ANY],
            scratch_shapes=[pltpu.VMEM((D, n), block.dtype), pltpu.SemaphoreType.DMA((6,)),
                            pltpu.SemaphoreType.DMA((6,)), pltpu.SemaphoreType.DMA((1,)), pltpu.SemaphoreType.DMA((1,))]),
        out_shape=[jax.ShapeDtypeStruct((T, 4 * n), _ACT), jax.ShapeDtypeStruct((T, D), _ACT),
                   jax.ShapeDtypeStruct((4,) + block.shape, block.dtype)],
        compiler_params=_cparams("arbitrary", "arbitrary"))(order, x, block)
    return h, xb, w_in


def _gate_specs(h, tm, D):
    first = (h.shape[1] - 2 * D) // D
    assert first * D + 2 * D == h.shape[1]
    return [pl.BlockSpec((tm, D), lambda i: (i, first)), pl.BlockSpec((tm, D), lambda i: (i, first + 1))]


def _mix_fwd(oa, ob, h, x, wpa, wpb, wo, bg, gamma, beta, alpha, layer, comm=None):
    T, D = x.shape
    WA, WB = oa.shape[1], ob.shape[1]
    tm = 512

    def kern(oa_ref, ob_ref, hga_ref, hgb_ref, x_ref, bg_ref, g_ref, b_ref, wpa_h, wpb_h, wo_h,
             x1_ref, u1_ref, pre_ref, wpa_v, wpb_v, wo_v, sa, sb, so):
        @pl.when(pl.program_id(0) == 0)
        def _():
            _load_cols(wpa_h, wpa_v, sa)
            _load_cols(wpb_h, wpb_v, sb)
            _load_rows(wo_h, wo_v, so)

        ya = _dot(oa_ref[...].astype(_MXU), wpa_v[...])
        yb = _dot(ob_ref[...].astype(_MXU), wpb_v[...])
        bgv = bg_ref[...]
        ga = jax.nn.sigmoid(hga_ref[...].astype(_F32) + bgv[:, :D])
        gb = jax.nn.sigmoid(hgb_ref[...].astype(_F32) + bgv[:, D:])
        pre = ga * ya + gb * yb
        mix = _dot(pre.astype(_MXU), wo_v[...])
        u = alpha * x_ref[...] + mix
        xhat, _ = _ln_stats(u)
        x1_ref[...] = xhat * g_ref[...] + b_ref[...]
        u1_ref[...] = u
        pre_ref[...] = pre.astype(pre_ref.dtype)

    return _call(
        kern, comm, name=f"mix_fwd_{layer}", grid=(T // tm,),
        in_specs=[_rows(tm, WA), _rows(tm, WB), *_gate_specs(h, tm, D), _rows(tm, D),
                  _whole((1, 2 * D)), _whole((1, D)), _whole((1, D)), _ANY, _ANY, _ANY],
        out_specs=[_rows(tm, D)] * 3,
        out_shape=[jax.ShapeDtypeStruct((T, D), _F32), jax.ShapeDtypeStruct((T, D), _F32),
                   jax.ShapeDtypeStruct((T, D), _ACT)],
        scratch_shapes=[pltpu.VMEM((WA, D), wpa.dtype), pltpu.VMEM((WB, D), wpb.dtype), pltpu.VMEM((D, D), wo.dtype),
                        pltpu.SemaphoreType.DMA((4,)), pltpu.SemaphoreType.DMA((4,)), pltpu.SemaphoreType.DMA((4,))],
        args=(oa, ob, h, h, x, bg, gamma, beta, wpa, wpb, wo), semantics=("arbitrary",))


def _ffn_fwd(x1, wfi, wfo, gamma, beta, alpha, layer, comm=None):
    T, D = x1.shape
    F2 = 4 * wfi.shape[-1]
    F = F2 // 2
    tm = 256
    fc = F // 2

    def kern(x_ref, g_ref, b_ref, wi_h, wo_h, x2_ref, u2_ref, act_ref, gu_ref, xb_ref, wi_v, wo_v, si, so):
        @pl.when(pl.program_id(0) == 0)
        def _():
            _load_cols(wi_h, wi_v, si)
            _load_rows(wo_h, wo_v, so)

        x = x_ref[...]
        xb = x.astype(_MXU)
        xb_ref[...] = xb.astype(xb_ref.dtype)
        ffn = jnp.zeros((tm, D), _F32)
        for c in range(2):
            g = _dot(xb, wi_v[:, c * fc:(c + 1) * fc])
            u = _dot(xb, wi_v[:, F + c * fc:F + (c + 1) * fc])
            act = g * jax.nn.sigmoid(g) * u
            ab = act.astype(_MXU)
            ffn = ffn + _dot(ab, wo_v[c * fc:(c + 1) * fc, :])
            act_ref[:, c * fc:(c + 1) * fc] = ab.astype(act_ref.dtype)
            gu_ref[:, c * fc:(c + 1) * fc] = g.astype(gu_ref.dtype)
            gu_ref[:, F + c * fc:F + (c + 1) * fc] = u.astype(gu_ref.dtype)
        uu = alpha * x + ffn
        xhat, _ = _ln_stats(uu)
        x2_ref[...] = xhat * g_ref[...] + b_ref[...]
        u2_ref[...] = uu

    return _call(
        kern, comm, name=f"ffn_fwd_{layer}", grid=(T // tm,),
        in_specs=[_rows(tm, D), _whole((1, D)), _whole((1, D)), _ANY, _ANY],
        out_specs=[_rows(tm, D), _rows(tm, D), _rows(tm, F), _rows(tm, F2), _rows(tm, D)],
        out_shape=[jax.ShapeDtypeStruct((T, D), _F32), jax.ShapeDtypeStruct((T, D), _F32),
                   jax.ShapeDtypeStruct((T, F), _ACT), jax.ShapeDtypeStruct((T, F2), _ACT),
                   jax.ShapeDtypeStruct((T, D), _ACT)],
        scratch_shapes=[pltpu.VMEM((D, F2), wfi.dtype), pltpu.VMEM((F, D), wfo.dtype),
                        pltpu.SemaphoreType.DMA((4,)), pltpu.SemaphoreType.DMA((4,))],
        args=(x1, gamma, beta, wfi, wfo), semantics=("arbitrary",))


def _ffn_bwd(u2, dy_or_target, gu, gamma, beta, wfi, wfo, alpha, layer, last):
    T, D = u2.shape
    F2 = gu.shape[1]
    F = F2 // 2
    tm = 256
    fc = F // 2

    def kern(u_ref, dy_ref, gu_ref, g_ref, b_ref, wi_h, wo_h, dx_ref, dub_ref, dgu_ref, st_ref, wi_v, wo_v, si, so):
        @pl.when(pl.program_id(0) == 0)
        def _():
            _load_cols(wi_h, wi_v, si)
            _load_rows(wo_h, wo_v, so)
            st_ref[...] = jnp.zeros_like(st_ref)

        gam = g_ref[...]
        u = u_ref[...]
        if last:
            xhat0, _ = _ln_stats(u)
            err = xhat0 * gam + b_ref[...] - dy_ref[...]
            dy = err * (1.0 / D)
            st_ref[2:3, :] += jnp.sum(err * err, axis=0, keepdims=True)
        else:
            dy = dy_ref[...]
        du, dgam, dbet, _ = _ln_bwd(u, dy, gam)
        st_ref[0:1, :] += dgam
        st_ref[1:2, :] += dbet
        dub = du.astype(_MXU)
        dub_ref[...] = dub.astype(dub_ref.dtype)
        dx = alpha * du
        for c in range(2):
            dact = _dot_nt(dub, wo_v[c * fc:(c + 1) * fc, :])
            g = gu_ref[:, c * fc:(c + 1) * fc].astype(_F32)
            uu = gu_ref[:, F + c * fc:F + (c + 1) * fc].astype(_F32)
            sg = jax.nn.sigmoid(g)
            dg = (dact * uu * (sg * (1.0 + g * (1.0 - sg)))).astype(_MXU)
            dup = (dact * (g * sg)).astype(_MXU)
            dgu_ref[:, c * fc:(c + 1) * fc] = dg.astype(dgu_ref.dtype)
            dgu_ref[:, F + c * fc:F + (c + 1) * fc] = dup.astype(dgu_ref.dtype)
            dx = dx + _dot_nt(dg, wi_v[:, c * fc:(c + 1) * fc]) + _dot_nt(dup, wi_v[:, F + c * fc:F + (c + 1) * fc])
        dx_ref[...] = dx

    return pl.pallas_call(
        kern, name=f"ffn_bwd_{layer}", grid=(T // tm,),
        in_specs=[_rows(tm, D), _rows(tm, D), _rows(tm, F2), _whole((1, D)), _whole((1, D)), _ANY, _ANY],
        out_specs=[_rows(tm, D), _rows(tm, D), _rows(tm, F2), _whole((8, D))],
        out_shape=[jax.ShapeDtypeStruct((T, D), _F32), jax.ShapeDtypeStruct((T, D), _ACT),
                   jax.ShapeDtypeStruct((T, F2), _ACT), jax.ShapeDtypeStruct((8, D), _F32)],
        scratch_shapes=[pltpu.VMEM((D, F2), wfi.dtype), pltpu.VMEM((F, D), wfo.dtype),
                        pltpu.SemaphoreType.DMA((4,)), pltpu.SemaphoreType.DMA((4,))],
        compiler_params=_cparams("arbitrary"),
    )(u2, dy_or_target, gu, gamma, beta, wfi, wfo)


def _residual_nt(res, res_scale, pieces, w, name, comm=None):
    T, K = res.shape
    widths = [p.shape[1] for p in pieces]
    N = sum(widths)
    tm = 512

    def kern(r_ref, *refs):
        d_refs, (w_hbm, o_ref, w_v, sem) = refs[:len(pieces)], refs[len(pieces):]

        @pl.when(pl.program_id(0) == 0)
        def _():
            _load_cols(w_hbm, w_v, sem)

        acc = res_scale * r_ref[...]
        off = 0
        for d_ref, width in zip(d_refs, widths):
            acc = acc + _dot_nt(d_ref[...].astype(_MXU), w_v[:, off:off + width])
            off += width
        o_ref[...] = acc

    outs, extra = _call(
        kern, comm, name=name, grid=(T // tm,),
        in_specs=[_rows(tm, K)] + [_rows(tm, width) for width in widths] + [_ANY], out_specs=[_rows(tm, K)],
        out_shape=[jax.ShapeDtypeStruct((T, K), _F32)],
        scratch_shapes=[pltpu.VMEM((K, N), w.dtype), pltpu.SemaphoreType.DMA((4,))],
        args=(res, *pieces, w), semantics=("arbitrary",))
    return outs[0], extra


def _grad_w_pieces(a, pieces, name, comm=None):
    T, M = a.shape
    widths = [p.shape[1] for p in pieces]
    bw = functools.reduce(_gcd, widths + [512])
    first = [sum(widths[:p]) // bw for p in range(len(pieces))]
    count = [width // bw for width in widths]
    N = sum(widths)
    tk = 1024 if T % 1024 == 0 else 512
    nk = T // tk

    def kern(a_ref, *refs):
        b_refs, (o_ref, acc) = refs[:len(pieces)], refs[len(pieces):]
        j, k = pl.program_id(0), pl.program_id(1)

        @pl.when(k == 0)
        def _():
            acc[...] = jnp.zeros_like(acc)

        for b_ref, start, blocks in zip(b_refs, first, count):
            @pl.when(jnp.logical_and(j >= start, j < start + blocks))
            def _():
                acc[...] += _dot_tn(a_ref[...].astype(_MXU), b_ref[...].astype(_MXU))

        @pl.when(k == nk - 1)
        def _():
            o_ref[...] = acc[...].astype(o_ref.dtype)

    def piece_spec(start, blocks):
        def index(j, k):
            mine = jnp.logical_and(j >= start, j < start + blocks)
            return jnp.where(mine, k, 0), jnp.where(mine, j - start, 0)
        return pl.BlockSpec((tk, bw), index)

    outs, extra = _call(
        kern, comm, name=name, grid=(N // bw, nk),
        in_specs=[pl.BlockSpec((tk, M), lambda j, k: (k, 0))] + [piece_spec(s, c) for s, c in zip(first, count)],
        out_specs=[pl.BlockSpec((M, bw), lambda j, k: (0, j))],
        out_shape=[jax.ShapeDtypeStruct((M, N), _ACT)], scratch_shapes=[pltpu.VMEM((M, bw), _F32)],
        args=(a, *pieces), semantics=("parallel", "arbitrary"))
    return outs[0], extra


def _gcd(a, b):
    while b:
        a, b = b, a % b
    return a


def _mix_bwd(u1, dx1, oa, ob, h, wpa, wpb, wo, bg, gamma, layer):
    T, D = u1.shape
    WA, WB = wpa.shape[-2], wpb.shape[-2]
    tm = 512

    def kern(u_ref, dx_ref, oa_ref, ob_ref, hga_ref, hgb_ref, bg_ref, g_ref, wpa_h, wpb_h, wo_h,
             du_ref, dub_ref, dya_ref, dyb_ref, dhg_ref, doa_ref, dob_ref, st_ref,
             wpa_v, wpb_v, wo_v, sa, sb, so):
        @pl.when(pl.program_id(0) == 0)
        def _():
            _load_cols(wpa_h, wpa_v, sa)
            _load_cols(wpb_h, wpb_v, sb)
            _load_rows(wo_h, wo_v, so)
            st_ref[...] = jnp.zeros_like(st_ref)

        du, dgam, dbet, _ = _ln_bwd(u_ref[...], dx_ref[...], g_ref[...])
        st_ref[1:2, :D] += dgam
        st_ref[1:2, D:] += dbet
        du_ref[...] = du
        dub = du.astype(_MXU)
        dub_ref[...] = dub.astype(dub_ref.dtype)
        dpre = _dot_nt(dub, wo_v[...])
        bgv = bg_ref[...]
        ga = jax.nn.sigmoid(hga_ref[...].astype(_F32) + bgv[:, :D])
        gb = jax.nn.sigmoid(hgb_ref[...].astype(_F32) + bgv[:, D:])
        dya = (dpre * ga).astype(_MXU)
        dyb = (dpre * gb).astype(_MXU)
        dsa = dpre * _dot(oa_ref[...].astype(_MXU), wpa_v[...]) * (ga * (1.0 - ga))
        dsb = dpre * _dot(ob_ref[...].astype(_MXU), wpb_v[...]) * (gb * (1.0 - gb))
        st_ref[0:1, :D] += jnp.sum(dsa, axis=0, keepdims=True)
        st_ref[0:1, D:] += jnp.sum(dsb, axis=0, keepdims=True)
        dya_ref[...] = dya.astype(dya_ref.dtype)
        dyb_ref[...] = dyb.astype(dyb_ref.dtype)
        dhg_ref[:, :D] = dsa.astype(dhg_ref.dtype)
        dhg_ref[:, D:] = dsb.astype(dhg_ref.dtype)
        doa_ref[...] = _dot_nt(dya, wpa_v[...]).astype(doa_ref.dtype)
        dob_ref[...] = _dot_nt(dyb, wpb_v[...]).astype(dob_ref.dtype)

    return pl.pallas_call(
        kern, name=f"mix_bwd_{layer}", grid=(T // tm,),
        in_specs=[_rows(tm, D), _rows(tm, D), _rows(tm, WA), _rows(tm, WB), *_gate_specs(h, tm, D), _whole((1, 2 * D)),
                  _whole((1, D)), _ANY, _ANY, _ANY],
        out_specs=[_rows(tm, D)] * 4 + [_rows(tm, 2 * D), _rows(tm, WA), _rows(tm, WB), _whole((8, 2 * D))],
        out_shape=[jax.ShapeDtypeStruct((T, D), _F32)] + [jax.ShapeDtypeStruct((T, D), _ACT)] * 3
        + [jax.ShapeDtypeStruct((T, 2 * D), _ACT), jax.ShapeDtypeStruct((T, WA), _ACT),
           jax.ShapeDtypeStruct((T, WB), _ACT), jax.ShapeDtypeStruct((8, 2 * D), _F32)],
        scratch_shapes=[pltpu.VMEM((WA, D), wpa.dtype), pltpu.VMEM((WB, D), wpb.dtype), pltpu.VMEM((D, D), wo.dtype),
                        pltpu.SemaphoreType.DMA((4,)), pltpu.SemaphoreType.DMA((4,)), pltpu.SemaphoreType.DMA((4,))],
        compiler_params=_cparams("arbitrary"),
    )(u1, dx1, oa, ob, h, h, bg, gamma, wpa, wpb, wo)


def _grad_w(a, b, *, col_shards, name, comm=None):
    T, M = a.shape
    N = b.shape[1]
    tk = 1024 if T % 1024 == 0 else 512
    n = N // 4 if col_shards else N
    whole = M * N * 4 <= _GRAD_ACC_BYTES
    tn = N if whole else (n if col_shards else _divisor_tile(N, _GRAD_ACC_BYTES // (4 * M)))
    nk = T // tk

    def kern(a_ref, b_ref, o_ref, acc):
        k = pl.program_id(1)

        @pl.when(k == 0)
        def _():
            acc[...] = jnp.zeros_like(acc)

        acc[...] += _dot_tn(a_ref[...].astype(_MXU), b_ref[...].astype(_MXU))

        @pl.when(k == nk - 1)
        def _():
            if col_shards and whole:
                for s in range(4):
                    o_ref[s] = acc[:, s * n:(s + 1) * n].astype(o_ref.dtype)
            else:
                o_ref[...] = acc[...].astype(o_ref.dtype)

    if col_shards:
        out_spec = (pl.BlockSpec((4, M, n), lambda j, k: (0, 0, 0)) if whole
                    else pl.BlockSpec((None, M, n), lambda j, k: (j, 0, 0)))
        out_shape = jax.ShapeDtypeStruct((4, M, n), _ACT)
    else:
        out_spec = pl.BlockSpec((M, tn), lambda j, k: (0, j))
        out_shape = jax.ShapeDtypeStruct((M, N), _ACT)
    outs, extra = _call(
        kern, comm, name=name, grid=(N // tn, nk),
        in_specs=[pl.BlockSpec((tk, M), lambda j, k: (k, 0)), pl.BlockSpec((tk, tn), lambda j, k: (k, j))],
        out_specs=[out_spec], out_shape=[out_shape], scratch_shapes=[pltpu.VMEM((M, tn), _F32)],
        args=(a, b), semantics=("parallel", "arbitrary"))
    return outs[0], extra


def _bias_tiles(rel):
    H = rel.shape[0]
    span = _TQ * _BAND_TILES - 1
    edge = span - _REL_CLIP
    gvec = jnp.concatenate([jnp.broadcast_to(rel[:, :1], (H, edge)), rel, jnp.broadcast_to(rel[:, -1:], (H, edge))], axis=1)
    width = _BIAS_TILES * _TQ
    period = width + _TQ
    tiled = jnp.broadcast_to(jnp.pad(gvec[:, ::-1], ((0, 0), (0, 1)))[:, None, :], (H, _TQ, period))
    rows = tiled.reshape(H, _TQ * period)[:, :_TQ * (period - 1)].reshape(H, _TQ, period - 1)[:, :, _TQ - 1:]
    r = jnp.arange(_TQ)[:, None]
    u = jnp.arange(width)[None, :]
    d = 4 * _TQ + r - u
    rm = r % _CHUNK
    valid = (d >= rm - (_CHUNK - 1)) & (d <= rm + 8 * _CHUNK)
    tiles = jnp.where(valid[None], rows, _MASKED)
    return tiles.reshape(H // 2, 2 * _TQ, _BIAS_TILES, _TQ).transpose(0, 2, 1, 3)


def _fold_bias_grad(db):
    H = 2 * db.shape[0]
    width = _BIAS_TILES * _TQ
    period = width + _TQ
    x = jnp.pad(db.transpose(0, 2, 1, 3).reshape(H, _TQ, width), ((0, 0), (0, 0), (_TQ - 1, 0)))
    skew = jnp.pad(x.reshape(H, _TQ * (period - 1)), ((0, 0), (0, _TQ))).reshape(H, _TQ, period)
    dg = skew.sum(axis=1)[:, :period - 1][:, ::-1]
    span = _TQ * _BAND_TILES - 1
    edge = span - _REL_CLIP
    mid = dg[:, edge:edge + 2 * _REL_CLIP + 1]
    lo = dg[:, :edge].sum(axis=1)
    hi = dg[:, edge + 2 * _REL_CLIP + 1:].sum(axis=1)
    return mid.at[:, 0].add(lo).at[:, -1].add(hi)


def _band_window(i):
    j0 = jnp.maximum(i - (_BAND_TILES - 1), 0)
    return j0, (_BAND_TILES - 1) - (i - j0)


def _head_masks():
    lane = lax.broadcasted_iota(jnp.int32, (1, _LANES), 1)
    return [(lane // _HEAD) == hh for hh in range(2)]


def _stack_heads(x, masks):
    return jnp.concatenate([jnp.where(m, x, jnp.zeros_like(x)) for m in masks], axis=0)


def _unstack_heads(y, masks):
    return jnp.where(masks[0], y[:_TQ], y[_TQ:])


def _scaled(q):
    return q * jnp.asarray(_HEAD ** -0.5, q.dtype)


def _band_probs(q2, k_ref, b_ref, j0, boff):
    s = []
    for j in range(_BAND_TILES):
        kj = k_ref[pl.ds(pl.multiple_of((j0 + j) * _TQ, _TQ), _TQ), :]
        s.append(_dot_nt(q2, kj) + b_ref[boff + j])
    m = jnp.max(functools.reduce(jnp.maximum, s), axis=-1, keepdims=True)
    p = [jnp.exp(x - m) for x in s]
    l = jnp.sum(functools.reduce(lambda a, b: a + b, p), axis=-1, keepdims=True)
    return p, 1.0 / l


def _qkv_specs(T, cb, npair, tq=_TQ):
    return [pl.BlockSpec((tq, _LANES), lambda h, i: (i, cb + h)),
            pl.BlockSpec((T, _LANES), lambda h, i: (0, cb + npair + h)),
            pl.BlockSpec((T, _LANES), lambda h, i: (0, cb + 2 * npair + h))]


def _attn_a_fwd(hq, bias, col0, width, layer, comm=None):
    T = hq.shape[0]
    npair = width // _LANES
    nsub = _BAND_SUBTILES
    tq = nsub * _TQ

    def kern(q_ref, k_ref, v_ref, b_ref, o_ref):
        masks = _head_masks()
        q = _scaled(q_ref[...])
        for s in range(nsub):
            part = slice(s * _TQ, (s + 1) * _TQ)
            j0, boff = _band_window(nsub * pl.program_id(1) + s)
            p, inv = _band_probs(_stack_heads(q[part], masks), k_ref, b_ref, j0, boff)
            o = jnp.zeros((2 * _TQ, _LANES), _F32)
            for j in range(_BAND_TILES):
                vj = v_ref[pl.ds(pl.multiple_of((j0 + j) * _TQ, _TQ), _TQ), :]
                o = o + _dot(p[j].astype(_MXU), vj)
            o_ref[part, :] = _unstack_heads(o * inv, masks).astype(o_ref.dtype)

    outs, extra = _call(
        kern, comm, name=f"band_attn_fwd_{layer}", grid=(npair, T // tq),
        in_specs=_qkv_specs(T, col0 // _LANES, npair, tq)
        + [pl.BlockSpec((None, _BIAS_TILES, 2 * _TQ, _TQ), lambda h, i: (h, 0, 0, 0))],
        out_specs=[pl.BlockSpec((tq, _LANES), lambda h, i: (i, h))],
        out_shape=[jax.ShapeDtypeStruct((T, width), _ACT)], scratch_shapes=[],
        args=(hq, hq, hq, bias), semantics=("arbitrary", "arbitrary"))
    return outs[0], extra


def _attn_a_bwd(hq, bias, do, col0, width, layer, comm=None):
    T = hq.shape[0]
    npair = width // _LANES
    nsub = _BAND_SUBTILES
    tq = nsub * _TQ
    nq = T // tq
    scale = _HEAD ** -0.5

    def kern(q_ref, k_ref, v_ref, b_ref, do_ref, dq_ref, dk_ref, dv_ref, db_ref, dk_acc, dv_acc):
        i = pl.program_id(1)

        @pl.when(i == 0)
        def _():
            dk_acc[...] = jnp.zeros_like(dk_acc)
            dv_acc[...] = jnp.zeros_like(dv_acc)
            db_ref[...] = jnp.zeros_like(db_ref)

        masks = _head_masks()
        q = _scaled(q_ref[...])
        do_t = do_ref[...]
        for s in range(nsub):
            part = slice(s * _TQ, (s + 1) * _TQ)
            j0, boff = _band_window(nsub * i + s)
            q2 = _stack_heads(q[part], masks)
            do2 = _stack_heads(do_t[part], masks).astype(_MXU)
            p, inv = _band_probs(q2, k_ref, b_ref, j0, boff)
            rows = [pl.ds(pl.multiple_of((j0 + j) * _TQ, _TQ), _TQ) for j in range(_BAND_TILES)]
            p = [x * inv for x in p]
            dp = [_dot_nt(do2, v_ref[rows[j], :]) for j in range(_BAND_TILES)]
            delta = jnp.sum(functools.reduce(lambda a, b: a + b, [p[j] * dp[j] for j in range(_BAND_TILES)]),
                            axis=-1, keepdims=True)
            dq = jnp.zeros((2 * _TQ, _LANES), _F32)
            for j in range(_BAND_TILES):
                ds = p[j] * (dp[j] - delta)
                db_ref[boff + j] += ds
                dsb = ds.astype(_MXU)
                dq = dq + _dot(dsb, k_ref[rows[j], :])
                dk_acc[rows[j], :] += _dot_tn(dsb, q2)
                dv_acc[rows[j], :] += _dot_tn(p[j].astype(_MXU), do2)
            dq_ref[part, :] = (_unstack_heads(dq, masks) * scale).astype(dq_ref.dtype)

        @pl.when(i == nq - 1)
        def _():
            dk_ref[...] = dk_acc[...].astype(dk_ref.dtype)
            dv_ref[...] = dv_acc[...].astype(dv_ref.dtype)

    strip = pl.BlockSpec((None, _BIAS_TILES, 2 * _TQ, _TQ), lambda h, i: (h, 0, 0, 0))
    tile = pl.BlockSpec((tq, _LANES), lambda h, i: (i, h))
    column = pl.BlockSpec((T, _LANES), lambda h, i: (0, h))
    outs, extra = _call(
        kern, comm, name=f"band_attn_bwd_{layer}", grid=(npair, nq),
        in_specs=_qkv_specs(T, col0 // _LANES, npair, tq) + [strip, tile],
        out_specs=[tile, column, column, strip],
        out_shape=[jax.ShapeDtypeStruct((T, width), _ACT)] * 3
        + [jax.ShapeDtypeStruct((npair, _BIAS_TILES, 2 * _TQ, _TQ), _F32)],
        scratch_shapes=[pltpu.VMEM((T, _LANES), _F32), pltpu.VMEM((T, _LANES), _F32)],
        args=(hq, hq, hq, bias, do), semantics=("arbitrary", "arbitrary"))
    return outs, extra


def _suffix_matrix():
    r = lax.broadcasted_iota(jnp.int32, (_TQ, _TQ), 0)
    c = lax.broadcasted_iota(jnp.int32, (_TQ, _TQ), 1)
    r2 = lax.broadcasted_iota(jnp.int32, (2 * _TQ, _TQ), 0)
    c2 = lax.broadcasted_iota(jnp.int32, (2 * _TQ, _TQ), 1)
    return (r > c).astype(_MXU), c2 - (r2 & (_TQ - 1))


def _suffix_sums(xs, tri):
    n, k = xs[0].shape[0], len(xs)
    his = [x.astype(_MXU) for x in xs]
    los = [(x - h.astype(_F32)).astype(_MXU) for x, h in zip(xs, his)]
    y = _dot(jnp.concatenate(his + los, axis=0), tri)
    return [y[j * n:(j + 1) * n] + y[(k + j) * n:(k + j + 1) * n] for j in range(k)]


def _stick_tiles(tiles, rel, carry_l, tri):
    zs = [_dot_nt(qs, kj) for qs, kj, _, _ in tiles]
    Ls, masks = [], []
    for z, (_, _, jj, _) in zip(zs, tiles):
        nsp = -(jnp.maximum(z, 0.0) + jnp.log(1.0 + jnp.exp(-jnp.abs(z))))
        if isinstance(jj, int):
            mask = (rel < 0) if jj == 0 else None
        else:
            mask = rel < jnp.where(jj == 0, 0, _TQ)
        Ls.append(nsp if mask is None else jnp.where(mask, nsp, 0.0))
        masks.append(mask)
    carry_l = list(carry_l)
    ws = []
    for z, L, suffix, mask, (_, _, _, sub) in zip(zs, Ls, _suffix_sums(Ls, tri), masks, tiles):
        w = jnp.exp(z + L + suffix + carry_l[sub])
        ws.append(w if mask is None else jnp.where(mask, w, 0.0))
        carry_l[sub] = carry_l[sub] + jnp.sum(L, axis=-1, keepdims=True)
    return zs, Ls, ws, masks, carry_l


def _sweep(i, step, zero):
    nsub = _SB_SUBTILES

    def window():
        tiles = [(s, jj) for jj in range(_SB_WINDOW) for s in range(nsub)]
        return tuple((jnp.int32(_SB_WINDOW),) + c for c in step(tiles, [zero] * nsub))

    start = lax.cond(i >= -(-(_SB_WINDOW - 1) // nsub), window, lambda: tuple((jnp.int32(0),) + zero for _ in range(nsub)))
    outs = []
    for s in range(nsub):
        def done(c, s=s):
            return jnp.logical_or(c[0] > nsub * i + s, jnp.max(c[1]) < _EXP_ZERO_BELOW)

        def more(c, s=s):
            carries = [None] * nsub
            carries[s] = c[1:]
            return (c[0] + 1,) + step([(s, c[0])], carries)[s]

        outs.append(lax.while_loop(lambda c, done=done: jnp.logical_not(done(c)), more, start[s]))
    return outs


def _sb_fwd(hq, col0, width, layer, comm=None):
    T = hq.shape[0]
    npair = width // _LANES
    nsub = _SB_SUBTILES
    tq = nsub * _TQ

    def kern(q_ref, k_ref, v_ref, o_ref):
        i = pl.program_id(1)
        masks = _head_masks()
        tri, rel = _suffix_matrix()
        q = _scaled(q_ref[...])
        q2 = [_stack_heads(q[s * _TQ:(s + 1) * _TQ], masks) for s in range(nsub)]

        def step(tiles, carries):
            rows = [pl.ds(pl.multiple_of((nsub * i + s - jj) * _TQ, _TQ), _TQ) for s, jj in tiles]
            cls = [None if c is None else c[0] for c in carries]
            accs = [None if c is None else c[1] for c in carries]
            _, _, ws, _, cls = _stick_tiles([(q2[s], k_ref[r, :], jj, s) for (s, jj), r in zip(tiles, rows)], rel, cls, tri)
            for w, r, (s, _) in zip(ws, rows, tiles):
                accs[s] = accs[s] + _dot(w.astype(_MXU), v_ref[r, :])
            return [None if c is None else (cls[s], accs[s]) for s, c in enumerate(carries)]

        outs = _sweep(i, step, (jnp.zeros((2 * _TQ, 1), _F32), jnp.zeros((2 * _TQ, _LANES), _F32)))
        for s in range(nsub):
            o_ref[s * _TQ:(s + 1) * _TQ, :] = _unstack_heads(outs[s][2], masks)

    outs, extra = _call(
        kern, comm, name=f"stick_attn_fwd_{layer}", grid=(npair, T // tq),
        in_specs=_qkv_specs(T, col0 // _LANES, npair, tq),
        out_specs=[pl.BlockSpec((tq, _LANES), lambda h, i: (i, h))],
        out_shape=[jax.ShapeDtypeStruct((T, width), _F32)], scratch_shapes=[],
        args=(hq, hq, hq), semantics=("arbitrary", "arbitrary"))
    return outs[0], extra


def _sb_bwd(hq, o, do, col0, width, layer, comm=None):
    T = hq.shape[0]
    npair = width // _LANES
    nsub = _SB_SUBTILES
    tq = nsub * _TQ
    nq = T // tq
    scale = _HEAD ** -0.5

    def kern(q_ref, k_ref, v_ref, o_ref, do_ref, dq_ref, dk_ref, dv_ref, dk_acc, dv_acc):
        i = pl.program_id(1)

        @pl.when(i == 0)
        def _():
            dk_acc[...] = jnp.zeros_like(dk_acc)
            dv_acc[...] = jnp.zeros_like(dv_acc)

        masks = _head_masks()
        tri, rel = _suffix_matrix()
        q = _scaled(q_ref[...])
        do_t = do_ref[...]
        prod = do_t.astype(_F32) * o_ref[...]
        part = [slice(s * _TQ, (s + 1) * _TQ) for s in range(nsub)]
        q2 = [_stack_heads(q[p], masks) for p in part]
        do2 = [_stack_heads(do_t[p], masks).astype(_MXU) for p in part]
        dsum = [jnp.sum(_stack_heads(prod[p], masks), axis=-1, keepdims=True) for p in part]

        def step(tiles, carries):
            rows = [pl.ds(pl.multiple_of((nsub * i + s - jj) * _TQ, _TQ), _TQ) for s, jj in tiles]
            kjs = [k_ref[r, :] for r in rows]
            cls, cgs, dqs = ([None if c is None else c[n] for c in carries] for n in range(3))
            zs, Ls, ws, tile_masks, cls = _stick_tiles([(q2[s], kj, jj, s) for (s, jj), kj in zip(tiles, kjs)], rel, cls, tri)
            wbs = [w.astype(_MXU) for w in ws]
            gs = [wb.astype(_F32) * _dot_nt(do2[s], v_ref[r, :]) for wb, r, (s, _) in zip(wbs, rows, tiles)]
            for z, L, g, later, mask, wb, kj, r, (s, _) in zip(zs, Ls, gs, _suffix_sums(gs, tri), tile_masks, wbs, kjs,
                                                               rows, tiles):
                dz = g - jnp.exp(z + L) * (dsum[s] - (later + cgs[s]))
                if mask is not None:
                    dz = jnp.where(mask, dz, 0.0)
                dzb = dz.astype(_MXU)
                dk_acc[r, :] += _dot_tn(dzb, q2[s])
                dv_acc[r, :] += _dot_tn(wb, do2[s])
                dqs[s] = dqs[s] + _dot(dzb, kj)
                cgs[s] = cgs[s] + jnp.sum(g, axis=-1, keepdims=True)
            return [None if c is None else (cls[s], cgs[s], dqs[s]) for s, c in enumerate(carries)]

        zc = jnp.zeros((2 * _TQ, 1), _F32)
        outs = _sweep(i, step, (zc, zc, jnp.zeros((2 * _TQ, _LANES), _F32)))
        for s in range(nsub):
            dq_ref[part[s], :] = (_unstack_heads(outs[s][3], masks) * scale).astype(dq_ref.dtype)

        @pl.when(i == nq - 1)
        def _():
            dk_ref[...] = dk_acc[...].astype(dk_ref.dtype)
            dv_ref[...] = dv_acc[...].astype(dv_ref.dtype)

    tile_spec = pl.BlockSpec((tq, _LANES), lambda h, i: (i, h))
    column = pl.BlockSpec((T, _LANES), lambda h, i: (0, h))
    outs, extra = _call(
        kern, comm, name=f"stick_attn_bwd_{layer}", grid=(npair, nq),
        in_specs=_qkv_specs(T, col0 // _LANES, npair, tq) + [tile_spec, tile_spec],
        out_specs=[tile_spec, column, column],
        out_shape=[jax.ShapeDtypeStruct((T, width), _ACT)] * 3,
        scratch_shapes=[pltpu.VMEM((T, _LANES), _F32), pltpu.VMEM((T, _LANES), _F32)],
        args=(hq, hq, hq, o, do), semantics=("arbitrary", "arbitrary"))
    return outs, extra


_DENSE = ("w_in", "w_proj_a", "w_proj_b", "w_out", "w_ffn_in", "w_ffn_out")
_COL_SHARDED = {"w_in": True, "w_proj_a": True, "w_proj_b": True, "w_out": False, "w_ffn_in": True, "w_ffn_out": False}
_SMALL = ("b_gate", "rel_bias", "ln1_g", "ln1_b", "ln2_g", "ln2_b")


class _Plans:
    def __init__(self, plans=None, own_w_in=None):
        self.plans = plans or {}
        self.own_w_in = own_w_in or {}

    def start(self, key):
        if key not in self.plans:
            return None, None
        return self.plans[key]()

    @staticmethod
    def finish(done, extra):
        if done is not None:
            done(extra)


def _layer_fwd(x, W, small, l, alpha, plans):
    WA = small["rel_bias"].shape[1] * _HEAD
    row = lambda v: v[l].reshape(1, -1)
    if l in plans.own_w_in:
        h, xb, W["w_in"] = _in_proj_gathering(x, plans.own_w_in[l], l)
    else:
        comm, done = plans.start(f"in_proj_{l}")
        (h, xb), extra = _in_proj(x, W["w_in"], l, comm)
        plans.finish(done, extra)
    WB = (h.shape[1] - 2 * x.shape[1] - 3 * WA) // 3
    bias = _bias_tiles(small["rel_bias"][l])
    comm, done = plans.start(f"band_fwd_{l}")
    oa, extra = _attn_a_fwd(h, bias, 0, WA, l, comm)
    plans.finish(done, extra)
    comm, done = plans.start(f"stick_fwd_{l}")
    ob, extra = _sb_fwd(h, 3 * WA, WB, l, comm)
    plans.finish(done, extra)
    comm, done = plans.start(f"mix_fwd_{l}")
    (x1, u1, pre), extra = _mix_fwd(oa, ob, h, x, W["w_proj_a"], W["w_proj_b"], W["w_out"], row(small["b_gate"]),
                                            row(small["ln1_g"]), row(small["ln1_b"]), alpha, l, comm)
    plans.finish(done, extra)
    comm, done = plans.start(f"ffn_fwd_{l}")
    (x2, u2, act, gu, x1b), extra = _ffn_fwd(x1, W["w_ffn_in"], W["w_ffn_out"], row(small["ln2_g"]),
                                             row(small["ln2_b"]), alpha, l, comm)
    plans.finish(done, extra)
    return x2, dict(xb=xb, h=h, bias=bias, oa=oa, ob=ob, x1b=x1b, u1=u1, pre=pre, u2=u2, act=act, gu=gu)


def _layer_bwd(dy_or_target, S, W, small, l, last, alpha, plans, gw):
    D = S["xb"].shape[1]
    WA, WB = S["oa"].shape[1], S["ob"].shape[1]
    row = lambda v: v[l].reshape(1, -1)

    def blocks(g, n):
        return g if _COL_SHARDED[n] else g.reshape(4, g.shape[0] // 4, g.shape[1])

    dx1, du2b, dgu, st2 = _ffn_bwd(S["u2"], dy_or_target, S["gu"], row(small["ln2_g"]), row(small["ln2_b"]),
                                   W["w_ffn_in"], W["w_ffn_out"], alpha, l, last)
    gw["w_ffn_in"] = blocks(_grad_w(S["x1b"], dgu, col_shards=True, name=f"grad_w_ffn_in_{l}")[0], "w_ffn_in")
    gw["w_ffn_out"] = blocks(_grad_w(S["act"], du2b, col_shards=False, name=f"grad_w_ffn_out_{l}")[0], "w_ffn_out")
    du1, du1b, dya, dyb, dhg, doa, dob, st1 = _mix_bwd(S["u1"], dx1, S["oa"], S["ob"], S["h"], W["w_proj_a"],
                                                       W["w_proj_b"], W["w_out"], row(small["b_gate"]),
                                                       row(small["ln1_g"]), l)
    gw["w_out"] = blocks(_grad_w(S["pre"], du1b, col_shards=False, name=f"grad_w_out_{l}")[0], "w_out")
    gw["w_proj_a"] = blocks(_grad_w(S["oa"], dya, col_shards=True, name=f"grad_w_proj_a_{l}")[0], "w_proj_a")
    gw["w_proj_b"] = blocks(_grad_w(S["ob"], dyb, col_shards=True, name=f"grad_w_proj_b_{l}")[0], "w_proj_b")
    comm, done = plans.start(f"band_bwd_{l}")
    (dqa, dka, dva, dbias), extra = _attn_a_bwd(S["h"], S["bias"], doa, 0, WA, l, comm)
    plans.finish(done, extra)
    comm, done = plans.start(f"stick_bwd_{l}")
    (dqb, dkb, dvb), extra = _sb_bwd(S["h"], S["ob"], dob, 3 * WA, WB, l, comm)
    plans.finish(done, extra)
    dh = [dqa, dka, dva, dqb, dkb, dvb, dhg]
    comm, done = plans.start(f"grad_w_in_{l}")
    gw["w_in"], extra = _grad_w_pieces(S["xb"], dh, f"grad_w_in_{l}", comm)
    plans.finish(done, extra)
    comm, done = plans.start(f"in_proj_bwd_{l}")
    dx, extra = _residual_nt(du1, alpha, dh, W["w_in"], f"in_proj_bwd_{l}", comm)
    plans.finish(done, extra)
    gs = dict(b_gate=st1[0], rel_bias=_fold_bias_grad(dbias), ln1_g=st1[1, :D], ln1_b=st1[1, D:],
              ln2_g=st2[0], ln2_b=st2[1])
    return dx, gs, st2[2]


def _local_step(x, target, W, small, plans=None, gws=None):
    depth = len(W)
    alpha = float((2 * depth) ** 0.25)
    plans = plans or _Plans()
    gws = gws if gws is not None else [dict() for _ in range(depth)]
    saved = []
    h = x
    for l in range(depth):
        h, S = _layer_fwd(h, W[l], small, l, alpha, plans)
        saved.append(S)
    gss = [None] * depth
    d = target
    sq = None
    for l in reversed(range(depth)):
        d, gss[l], sq_l = _layer_bwd(d, saved[l], W[l], small, l, l == depth - 1, alpha, plans, gws[l])
        if l == depth - 1:
            sq = sq_l
    return sq, d, gws, gss


def _place():
    return lax.axis_index("x"), lax.axis_index("y"), lax.axis_index("c")


def _remote(src, dst, send_sem, recv_sem, to):
    return pltpu.make_async_remote_copy(src_ref=src, dst_ref=dst, send_sem=send_sem, recv_sem=recv_sem,
                                        device_id=to, device_id_type=_MESH)


def _half(ref, hc):
    kh = ref.shape[0] // 2
    return ref.at[pl.ds(pl.multiple_of(hc * kh, 16), kh), :]


def _gather_plan(blocks, fractions):
    nt = len(blocks)

    def run(step, nsteps, ins, outs, sems):
        send_sems, recv_sems, loc_sems = sems
        x, y, c = _place()
        k = 2 * x + y
        me, sibling = (x, y, c), (x, y, 1 - c)
        chips = [(1 - x, y), (x, 1 - y), (1 - x, 1 - y)]
        chip_k = [2 * cx + cy for cx, cy in chips]

        def ici(t, s, owner_k, to, src=None):
            dst = _half(outs[t].at[owner_k], c)
            return _remote(dst if src is None else src, dst, send_sems.at[t, s], recv_sems.at[t, s], to)

        def passed(t, s, hc, to):
            blk = _half(outs[t].at[chip_k[s]], hc)
            return _remote(blk, blk, send_sems.at[t, 3 + s], recv_sems.at[t, 3 + s], to)

        def local(t):
            return pltpu.make_async_copy(ins[t], outs[t].at[k], loc_sems.at[t])

        @pl.when(step == 0)
        def _():
            for t in range(nt):
                local(t).start()
                for s, chip in enumerate(chips):
                    ici(t, s, k, (*chip, c), src=_half(ins[t], c)).start()

        for t in range(nt):
            @pl.when(step == min(nsteps - 1, int(fractions[t] * nsteps)))
            def _():
                for s in range(3):
                    ici(t, s, chip_k[s], me).wait_recv()
                    passed(t, s, c, sibling).start()

        @pl.when(step == nsteps - 1)
        def _():
            for t in range(nt):
                for s, chip in enumerate(chips):
                    passed(t, s, 1 - c, me).wait_recv()
            for t in range(nt):
                for s, chip in enumerate(chips):
                    ici(t, s, k, (*chip, c), src=_half(ins[t], c)).wait_send()
                    passed(t, s, c, sibling).wait_send()
                local(t).wait()

    return _Comm(blocks, [jax.ShapeDtypeStruct((4,) + b.shape, b.dtype) for b in blocks],
                 [pltpu.SemaphoreType.DMA((nt, 6)), pltpu.SemaphoreType.DMA((nt, 6)), pltpu.SemaphoreType.DMA((nt,))], run)


def _scatter_plan(grads, owners):
    nt = len(grads)
    shapes = [g.shape[1:] if g.ndim == 3 else (g.shape[0], g.shape[1] // 4) for g in grads]

    def run(step, nsteps, ins, outs, sems):
        send_sems, recv_sems, loc_sems = sems
        x, y, c = _place()
        me = 4 * x + 2 * y + c

        def target(r):
            tx = 1 - x if r & 2 else x
            ty = 1 - y if r & 1 else y
            return tx, ty

        def block(t, chip):
            if len(ins[t].shape) == 3:
                return ins[t].at[chip]
            n = shapes[t][1]
            return ins[t].at[:, pl.ds(pl.multiple_of(chip * n, _LANES), n)]

        def send(t, r):
            tx, ty = target(r)
            return _remote(block(t, 2 * tx + ty), outs[t].at[me], send_sems.at[t, r], recv_sems.at[t, 2 * r + c],
                           (tx, ty, owners[t]))

        def local(t):
            return pltpu.make_async_copy(block(t, 2 * x + y), outs[t].at[me], loc_sems.at[t])

        @pl.when(step == 0)
        def _():
            for t in range(nt):
                @pl.when(c == owners[t])
                def _():
                    local(t).start()

                @pl.when(c != owners[t])
                def _():
                    send(t, 0).start()

                for r in range(1, 4):
                    send(t, r).start()

        @pl.when(step == nsteps - 1)
        def _():
            for t in range(nt):
                @pl.when(c == owners[t])
                def _():
                    for r in range(4):
                        sx, sy = target(r)
                        for cs in range(2):
                            if r == 0 and cs == owners[t]:
                                continue
                            src_dev = 4 * sx + 2 * sy + cs
                            _remote(block(t, 0), outs[t].at[src_dev], send_sems.at[t, r], recv_sems.at[t, 2 * r + cs],
                                    (x, y, c)).wait_recv()
                    local(t).wait()

                @pl.when(c != owners[t])
                def _():
                    send(t, 0).wait_send()

                for r in range(1, 4):
                    send(t, r).wait_send()

    return _Comm(grads, [jax.ShapeDtypeStruct((8,) + s, g.dtype) for s, g in zip(shapes, grads)],
                 [pltpu.SemaphoreType.DMA((nt, 4)), pltpu.SemaphoreType.DMA((nt, 8)), pltpu.SemaphoreType.DMA((nt,))], run)


def _share_plan(reduced, owners):
    nt = len(reduced)

    def run(step, nsteps, ins, outs, sems):
        del ins
        send_sems, recv_sems = sems
        x, y, c = _place()

        def give(t, to):
            return _remote(outs[t], outs[t], send_sems.at[t], recv_sems.at[t], to)

        @pl.when(step == 0)
        def _():
            for t in range(nt):
                @pl.when(c == owners[t])
                def _():
                    give(t, (x, y, 1 - c)).start()

        @pl.when(step == nsteps - 1)
        def _():
            for t in range(nt):
                @pl.when(c == owners[t])
                def _():
                    give(t, (x, y, 1 - c)).wait_send()

                @pl.when(c != owners[t])
                def _():
                    give(t, (x, y, c)).wait_recv()

    return _Comm(reduced, [jax.ShapeDtypeStruct(r.shape, r.dtype) for r in reduced],
                 [pltpu.SemaphoreType.DMA((nt,)), pltpu.SemaphoreType.DMA((nt,))], run,
                 aliases={t: t for t in range(nt)})


def _join(a, b):
    ni, no, ns = len(a.inputs), len(a.out_shapes), len(a.sems)

    def run(step, nsteps, ins, outs, sems):
        a.run(step, nsteps, ins[:ni], outs[:no], sems[:ns])
        b.run(step, nsteps, ins[ni:], outs[no:], sems[ns:])

    aliases = dict(a.aliases)
    aliases.update({ni + i: no + o for i, o in b.aliases.items()})
    return _Comm(a.inputs + b.inputs, a.out_shapes + b.out_shapes, a.sems + b.sems, run, aliases)


def _peer(x, y, c, r):
    px = 1 - x if r & 4 else x
    py = 1 - y if r & 2 else y
    pc = 1 - c if r & 1 else c
    return (px, py, pc), 4 * px + 2 * py + pc


def _sum_slots(st, name):
    _, K, n = st.shape
    tr = next(t for t in (256, 128, 64, 32, 16) if K % t == 0)

    def kern(s_ref, o_ref):
        acc = s_ref[0].astype(_F32)
        for d in range(1, 8):
            acc = acc + s_ref[d].astype(_F32)
        o_ref[...] = acc.astype(o_ref.dtype)

    return pl.pallas_call(
        kern, name=name, grid=(K // tr,),
        in_specs=[pl.BlockSpec((8, tr, n), lambda i: (0, i, 0))], out_specs=_rows(tr, n),
        out_shape=jax.ShapeDtypeStruct((K, n), _ACT),
        compiler_params=_cparams("parallel"),
    )(st)


def _all_reduce_small(p):
    R = p.shape[0]

    def body(p_ref, o_ref, stage, send_sems, recv_sems):
        x, y, c = _place()
        me = 4 * x + 2 * y + c
        stage[me] = p_ref[...]
        sent = []
        for r in range(1, 8):
            to, _ = _peer(x, y, c, r)
            cp = _remote(p_ref, stage.at[me], send_sems.at[r - 1], recv_sems.at[r - 1], to)
            cp.start()
            sent.append(cp)
        for r in range(1, 8):
            _, src_dev = _peer(x, y, c, r)
            _remote(p_ref, stage.at[src_dev], send_sems.at[r - 1], recv_sems.at[r - 1], (x, y, c)).wait_recv()
        acc = stage[0]
        for d in range(1, 8):
            acc = acc + stage[d]
        o_ref[...] = acc
        for cp in sent:
            cp.wait_send()

    vm = pl.BlockSpec(memory_space=pltpu.VMEM)
    return pl.pallas_call(
        body, name="all_reduce_small",
        in_specs=[vm], out_specs=vm,
        out_shape=jax.ShapeDtypeStruct((R, _LANES), _F32),
        scratch_shapes=[pltpu.VMEM((8, R, _LANES), _F32), pltpu.SemaphoreType.DMA((7,)), pltpu.SemaphoreType.DMA((7,))],
    )(p)


def _adamw_update(gv, w_ref, m_ref, v_ref, gf_ref, d_ref, nm_ref, nv_ref):
    nm = _B1 * m_ref[...] + (1.0 - _B1) * gv
    nv = _B2 * v_ref[...] + (1.0 - _B2) * (gv * gv)
    m_hat = nm / (1.0 - _B1 ** _STEP)
    v_hat = nv / (1.0 - _B2 ** _STEP)
    gf_ref[...] = gv
    d_ref[...] = -_LR * (m_hat / (jnp.sqrt(v_hat) + _EPS) + _WD * w_ref[...])
    nm_ref[...] = nm
    nv_ref[...] = nv


def _adamw_layers(w, g_layers, m, v, name):
    _, K, n = w.shape
    tr = next(t for t in (256, 128, 64, 32, 16) if K % t == 0)

    def kern(w_ref, g0_ref, g1_ref, m_ref, v_ref, *out_refs):
        first = pl.program_id(0) == 0
        gv = jnp.where(first, g0_ref[...].astype(_F32), g1_ref[...].astype(_F32))
        _adamw_update(gv, w_ref, m_ref, v_ref, *out_refs)

    stacked = pl.BlockSpec((None, tr, n), lambda l, i: (l, i, 0))
    layer = pl.BlockSpec((tr, n), lambda l, i: (i, 0))
    return tuple(pl.pallas_call(
        kern, name=name, grid=(2, K // tr),
        in_specs=[stacked, layer, layer, stacked, stacked], out_specs=[stacked] * 4,
        out_shape=[jax.ShapeDtypeStruct(w.shape, _F32)] * 4,
        compiler_params=_cparams("parallel", "parallel"),
    )(w, g_layers[0], g_layers[1], m, v))


def _adamw(w, g, m, v, name):
    shape = w.shape
    w2, g2, m2, v2 = (a.reshape(-1, shape[-1]) for a in (w, g, m, v))
    R, C = w2.shape
    tr = next((t for t in (256, 128, 64, 32, 16) if R % t == 0), R)

    def kern(w_ref, g_ref, m_ref, v_ref, *out_refs):
        _adamw_update(g_ref[...].astype(_F32), w_ref, m_ref, v_ref, *out_refs)

    outs = pl.pallas_call(
        kern, name=name, grid=(R // tr,),
        in_specs=[_rows(tr, C)] * 4, out_specs=[_rows(tr, C)] * 4,
        out_shape=[jax.ShapeDtypeStruct((R, C), _F32)] * 4,
        compiler_params=_cparams("parallel"),
    )(w2, g2, m2, v2)
    return tuple(o.reshape(shape) for o in outs)


def _pack_small(gss, sq):
    parts = [gss[l][n].reshape(-1) for n in _SMALL for l in range(len(gss))] + [jnp.sum(sq).reshape(1)]
    flat = jnp.concatenate(parts)
    rows = -(-flat.shape[0] // (8 * _LANES)) * 8
    return jnp.pad(flat, (0, rows * _LANES - flat.shape[0])).reshape(rows, _LANES)


def _unpack_small(total, shapes):
    flat = total.reshape(-1)
    out, off = {}, 0
    for n in _SMALL:
        layers = []
        for _ in range(shapes[n][0]):
            size = 1
            for s in shapes[n][1:]:
                size *= s
            layers.append(flat[off:off + size].reshape(shapes[n][1:]))
            off += size
        out[n] = jnp.stack(layers)
    return out, flat[off]


_GATHER = {
    "band_fwd_0": [(0, "w_proj_a"), (0, "w_proj_b"), (0, "w_out"), (0, "w_ffn_out")],
    "stick_fwd_0": [(0, "w_ffn_in")],
    "mix_fwd_0": [(1, "w_in")],
    "ffn_fwd_0": [(1, "w_proj_a"), (1, "w_proj_b"), (1, "w_out"), (1, "w_ffn_in"), (1, "w_ffn_out")],
}
_SCATTER = {
    "band_bwd_1": [(1, "w_ffn_in"), (1, "w_ffn_out")],
    "stick_bwd_1": [(1, "w_proj_a"), (1, "w_proj_b"), (1, "w_out")],
    "band_bwd_0": [(1, "w_in"), (0, "w_ffn_in")],
    "stick_bwd_0": [(0, "w_ffn_out"), (0, "w_proj_a"), (0, "w_proj_b"), (0, "w_out")],
    "in_proj_bwd_0": [(0, "w_in")],
}
_SHARE = {"stick_bwd_1": "band_bwd_1", "band_bwd_0": "stick_bwd_1", "stick_bwd_0": "band_bwd_0", "grad_w_in_0": "stick_bwd_0"}


def _owner(key):
    del key
    return 1


def kernel(x, w_in, b_gate, rel_bias, w_proj_a, w_proj_b, w_out, ln1_g, ln1_b, w_ffn_in, w_ffn_out, ln2_g, ln2_b, loss_target, m_w_in, m_b_gate, m_rel_bias, m_w_proj_a, m_w_proj_b, m_w_out, m_ln1_g, m_ln1_b, m_w_ffn_in, m_w_ffn_out, m_ln2_g, m_ln2_b, v_w_in, v_b_gate, v_rel_bias, v_w_proj_a, v_w_proj_b, v_w_out, v_ln1_g, v_ln1_b, v_w_ffn_in, v_w_ffn_out, v_ln2_g, v_ln2_b):
    names = ("w_in", "b_gate", "rel_bias", "w_proj_a", "w_proj_b", "w_out", "ln1_g", "ln1_b", "w_ffn_in", "w_ffn_out", "ln2_g", "ln2_b")
    w = dict(zip(names, (w_in, b_gate, rel_bias, w_proj_a, w_proj_b, w_out, ln1_g, ln1_b, w_ffn_in, w_ffn_out, ln2_g, ln2_b)))
    m = dict(zip(names, (m_w_in, m_b_gate, m_rel_bias, m_w_proj_a, m_w_proj_b, m_w_out, m_ln1_g, m_ln1_b, m_w_ffn_in, m_w_ffn_out, m_ln2_g, m_ln2_b)))
    v = dict(zip(names, (v_w_in, v_b_gate, v_rel_bias, v_w_proj_a, v_w_proj_b, v_w_out, v_ln1_g, v_ln1_b, v_w_ffn_in, v_w_ffn_out, v_ln2_g, v_ln2_b)))
    T, D = x.shape[-2], x.shape[-1]
    assert w_in.shape[0] == 2, "the exchange schedule below is written for two layers"

    mine = [{n: w[n][l].astype(_MXU) for n in _DENSE} for l in range(2)]
    W = [dict(), dict()]
    gws = [dict(), dict()]
    slots, final = {}, {}

    def gather(keys):
        sizes = [mine[l][n].size for l, n in keys]
        passed, fractions = 0, []
        for s in sizes:
            passed += s
            fractions.append(0.15 + 0.6 * passed / sum(sizes))

        def done(outs):
            for (l, n), o in zip(keys, outs):
                W[l][n] = o
        return _gather_plan([mine[l][n] for l, n in keys], fractions), done

    def scatter(keys):
        comm = _scatter_plan([gws[l][n] for l, n in keys], [_owner(key) for key in keys])
        return comm, lambda outs: slots.update(zip(keys, outs))

    def share(keys):
        reduced = [_sum_slots(slots[key], f"sum_grad_{key[1]}_{key[0]}") for key in keys]
        comm = _share_plan(reduced, [_owner(key) for key in keys])
        return comm, lambda outs: final.update(zip(keys, outs))

    def both(first, second):
        (ca, da), (cb, db) = first, second
        na = len(ca.out_shapes)
        return _join(ca, cb), lambda outs: (da(outs[:na]), db(outs[na:]))

    plans = {key: functools.partial(gather, keys) for key, keys in _GATHER.items()}
    for key, keys in _SCATTER.items():
        plans[key] = functools.partial(scatter, keys)
    for key, scattered_under in _SHARE.items():
        handed = functools.partial(share, _SCATTER[scattered_under])
        carried = plans.get(key)
        plans[key] = handed if carried is None else (lambda carried=carried, handed=handed: both(carried(), handed()))
    small = {n: w[n] for n in _SMALL}
    sq, dx, _, gss = _local_step(x.reshape(T, D), loss_target.reshape(T, D), W, small,
                                  _Plans(plans, {0: mine[0]["w_in"]}), gws)

    comm, done = share(_SCATTER["in_proj_bwd_0"])
    done(_comm_only(comm, "share_last"))
    total = _all_reduce_small(_pack_small(gss, sq))
    small_grads, sq_all = _unpack_small(total, {n: w[n].shape for n in _SMALL})
    loss = 0.5 * sq_all / D

    grad, delta, new_m, new_v = {}, {}, {}, {}
    for n in names:
        if n in _DENSE:
            updated = _adamw_layers(w[n], [final[(l, n)] for l in range(2)], m[n], v[n], f"adamw_{n}")
        else:
            updated = _adamw(w[n], small_grads[n], m[n], v[n], f"adamw_{n}")
        grad[n], delta[n], new_m[n], new_v[n] = updated
    return (loss, dx.reshape(x.shape), *[grad[n] for n in names], *[delta[n] for n in names],
            *[new_m[n] for n in names], *[new_v[n] for n in names])
```

```python
import functools

import jax
import jax.numpy as jnp
from jax import lax
from jax.experimental import pallas as pl
from jax.experimental.pallas import tpu as pltpu

_MXU = jnp.bfloat16
_ACT = jnp.bfloat16
_F32 = jnp.float32

_HEAD = 64
_CHUNK = 64
_LANES = 128
_TQ = 128
_BAND_TILES = 5
_BIAS_TILES = 9
_REL_CLIP = 256
_LN_EPS = 1e-5
_MASKED = -1e30
_EXP_ZERO_BELOW = -87.34
_SB_WINDOW = 2
_SB_SUBTILES = 4
_BAND_SUBTILES = 8
_VMEM_LIMIT = 56 * 1024 * 1024
_GRAD_ACC_BYTES = 12 * 1024 * 1024

_LR, _B1, _B2, _EPS, _WD, _STEP = 0.001, 0.9, 0.999, 1e-08, 0.01, 10

_MESH = pl.DeviceIdType.MESH


def _dot(a, b):
    return jnp.dot(a, b, preferred_element_type=_F32)


def _dot_nt(a, b):
    return lax.dot_general(a, b, (((1,), (1,)), ((), ())), preferred_element_type=_F32)


def _dot_tn(a, b):
    return lax.dot_general(a, b, (((0,), (0,)), ((), ())), preferred_element_type=_F32)


def _cparams(*sem):
    return pltpu.CompilerParams(dimension_semantics=sem, vmem_limit_bytes=_VMEM_LIMIT)


def _rows(t, c):
    return pl.BlockSpec((t, c), lambda i: (i, 0))


def _whole(shape):
    return pl.BlockSpec(shape, lambda i: tuple(0 for _ in shape))


_ANY = pl.BlockSpec(memory_space=pl.ANY)


def _load_cols(w_hbm, w_vmem, sem):
    n = w_hbm.shape[-1]
    cps = [pltpu.make_async_copy(w_hbm.at[k], w_vmem.at[:, pl.ds(k * n, n)], sem.at[k]) for k in range(4)]
    for cp in cps:
        cp.start()
    for cp in cps:
        cp.wait()


def _load_rows(w_hbm, w_vmem, sem):
    r = w_hbm.shape[-2]
    cps = [pltpu.make_async_copy(w_hbm.at[k], w_vmem.at[pl.ds(k * r, r), :], sem.at[k]) for k in range(4)]
    for cp in cps:
        cp.start()
    for cp in cps:
        cp.wait()


def _ln_stats(u):
    mu = jnp.mean(u, axis=-1, keepdims=True)
    xc = u - mu
    var = jnp.mean(xc * xc, axis=-1, keepdims=True)
    rstd = lax.rsqrt(var + _LN_EPS)
    return xc * rstd, rstd


def _ln_bwd(u, dy, gamma):
    xhat, rstd = _ln_stats(u)
    dxh = dy * gamma
    m1 = jnp.mean(dxh, axis=-1, keepdims=True)
    m2 = jnp.mean(dxh * xhat, axis=-1, keepdims=True)
    du = rstd * (dxh - m1 - xhat * m2)
    return du, jnp.sum(dy * xhat, axis=0, keepdims=True), jnp.sum(dy, axis=0, keepdims=True), xhat


def _divisor_tile(n, cap):
    best = None
    for t in range(_LANES, min(n, cap) + 1, _LANES):
        if n % t == 0:
            best = t
    return best or n


class _Comm:
    def __init__(self, inputs, out_shapes, sems, run, aliases=None):
        self.inputs, self.out_shapes, self.sems, self.run = list(inputs), list(out_shapes), list(sems), run
        self.aliases = aliases or {}


def _call(kern, comm, *, name, grid, in_specs, out_specs, out_shape, scratch_shapes, args, semantics):
    in_specs, out_specs, out_shape, scratch_shapes = list(in_specs), list(out_specs), list(out_shape), list(scratch_shapes)
    if comm is None:
        outs = pl.pallas_call(kern, name=name, grid=grid, in_specs=in_specs, out_specs=out_specs, out_shape=out_shape,
                              scratch_shapes=scratch_shapes, compiler_params=_cparams(*semantics))(*args)
        return list(outs), []
    n_in, n_out, n_scr = len(in_specs), len(out_specs), len(scratch_shapes)
    ci, co = len(comm.inputs), len(comm.out_shapes)
    nsteps = functools.reduce(lambda a, b: a * b, grid, 1)

    def fused(*refs):
        a, b = n_in, n_in + ci
        c, d = b + n_out, b + n_out + co
        e = d + n_scr
        step = pl.program_id(0)
        for ax in range(1, len(grid)):
            step = step * grid[ax] + pl.program_id(ax)
        comm.run(step, nsteps, refs[a:b], refs[c:d], refs[e:])
        kern(*refs[:a], *refs[b:c], *refs[d:e])

    outs = pl.pallas_call(
        fused, name=name, grid=grid, in_specs=in_specs + [_ANY] * ci, out_specs=out_specs + [_ANY] * co,
        out_shape=out_shape + comm.out_shapes, scratch_shapes=scratch_shapes + comm.sems,
        input_output_aliases={n_in + i: n_out + o for i, o in comm.aliases.items()},
        compiler_params=_cparams(*("arbitrary" for _ in grid)))(*args, *comm.inputs)
    return list(outs[:n_out]), list(outs[n_out:])


def _comm_only(comm, name):
    def body(*refs):
        ci, co = len(comm.inputs), len(comm.out_shapes)
        comm.run(0, 1, refs[:ci], refs[ci:ci + co], refs[ci + co:])

    outs = pl.pallas_call(body, name=name, in_specs=[_ANY] * len(comm.inputs), out_specs=[_ANY] * len(comm.out_shapes),
                          out_shape=comm.out_shapes, scratch_shapes=comm.sems,
                          input_output_aliases=dict(comm.aliases))(*comm.inputs)
    return list(outs)


def _in_proj(x, w_in, layer, comm=None):
    T, D = x.shape
    N = 4 * w_in.shape[-1]
    tm = 512

    def kern(x_ref, w_hbm, h_ref, xb_ref, w_v, sem):
        @pl.when(pl.program_id(0) == 0)
        def _():
            _load_cols(w_hbm, w_v, sem)

        xb = x_ref[...].astype(_MXU)
        h_ref[...] = _dot(xb, w_v[...]).astype(h_ref.dtype)
        xb_ref[...] = xb.astype(xb_ref.dtype)

    return _call(
        kern, comm, name=f"in_proj_{layer}", grid=(T // tm,),
        in_specs=[_rows(tm, D), _ANY],
        out_specs=[_rows(tm, N), _rows(tm, D)],
        out_shape=[jax.ShapeDtypeStruct((T, N), _ACT), jax.ShapeDtypeStruct((T, D), _ACT)],
        scratch_shapes=[pltpu.VMEM((D, N), w_in.dtype), pltpu.SemaphoreType.DMA((4,))],
        args=(x, w_in), semantics=("arbitrary",))


def _in_proj_gathering(x, block, layer):
    T, D = x.shape
    n = block.shape[1]
    tm = min(T, 1024)
    nrows = T // tm
    pass_steps = [int(f * nrows) for f in (0.6, 1.0, 1.7)]
    px, py, _ = _place()
    order = jnp.stack([2 * px + py, 2 * (1 - px) + py, 2 * px + (1 - py), 2 * (1 - px) + (1 - py)]).astype(jnp.int32)

    def kern(order_ref, x_ref, blk_hbm, h_ref, xb_ref, w_hbm, w_v, send_sems, recv_sems, loc_sem, load_sem):
        del order_ref
        step = pl.program_id(0) * nrows + pl.program_id(1)
        x_, y_, c = _place()
        k = 2 * x_ + y_
        me, sibling = (x_, y_, c), (x_, y_, 1 - c)
        chips = [(1 - x_, y_), (x_, 1 - y_), (1 - x_, 1 - y_)]
        chip_k = [2 * cx + cy for cx, cy in chips]

        def ici(s, owner_k, to, src=None):
            dst = _half(w_hbm.at[owner_k], c)
            return _remote(dst if src is None else src, dst, send_sems.at[s], recv_sems.at[s], to)

        def passed(s, hc, to):
            blk = _half(w_hbm.at[chip_k[s]], hc)
            return _remote(blk, blk, send_sems.at[3 + s], recv_sems.at[3 + s], to)

        local = pltpu.make_async_copy(blk_hbm, w_hbm.at[k], loc_sem.at[0])

        def load(src):
            cp = pltpu.make_async_copy(src, w_v, load_sem.at[0])
            cp.start()
            cp.wait()

        @pl.when(step == 0)
        def _():
            for s, chip in enumerate(chips):
                ici(s, k, (*chip, c), src=_half(blk_hbm, c)).start()
            local.start()
            load(blk_hbm)

        for s in range(3):
            @pl.when(step == pass_steps[s])
            def _():
                ici(s, chip_k[s], me).wait_recv()
                passed(s, c, sibling).start()

            @pl.when(step == (s + 1) * nrows)
            def _():
                passed(s, 1 - c, me).wait_recv()
                load(w_hbm.at[chip_k[s]])

        xb = x_ref[...].astype(_MXU)
        h_ref[...] = _dot(xb, w_v[...]).astype(h_ref.dtype)

        @pl.when(pl.program_id(0) == 0)
        def _():
            xb_ref[...] = xb.astype(xb_ref.dtype)

        @pl.when(step == 4 * nrows - 1)
        def _():
            for s, chip in enumerate(chips):
                ici(s, k, (*chip, c), src=_half(blk_hbm, c)).wait_send()
                passed(s, c, sibling).wait_send()
            local.wait()

    assert all(pass_steps[s] <= (s + 1) * nrows for s in range(3))
    h, xb, w_in = pl.pallas_call(
        kern, name=f"in_proj_{layer}",
        grid_spec=pltpu.PrefetchScalarGridSpec(
            num_scalar_prefetch=1, grid=(4, nrows),
            in_specs=[pl.BlockSpec((tm, D), lambda j, i, o: (i, 0)), _ANY],
            out_specs=[pl.BlockSpec((tm, n), lambda j, i, o: (i, o[j])),
                       pl.BlockSpec((tm, D), lambda j, i, o: (jnp.where(j == 0, i, nrows - 1), 0)), _ANY],
            scratch_shapes=[pltpu.VMEM((D, n), block.dtype), pltpu.SemaphoreType.DMA((6,)),
                            pltpu.SemaphoreType.DMA((6,)), pltpu.SemaphoreType.DMA((1,)), pltpu.SemaphoreType.DMA((1,))]),
        out_shape=[jax.ShapeDtypeStruct((T, 4 * n), _ACT), jax.ShapeDtypeStruct((T, D), _ACT),
                   jax.ShapeDtypeStruct((4,) + block.shape, block.dtype)],
        compiler_params=_cparams("arbitrary", "arbitrary"))(order, x, block)
    return h, xb, w_in


def _gate_specs(h, tm, D):
    first = (h.shape[1] - 2 * D) // D
    assert first * D + 2 * D == h.shape[1]
    return [pl.BlockSpec((tm, D), lambda i: (i, first)), pl.BlockSpec((tm, D), lambda i: (i, first + 1))]


def _mix_fwd(oa, ob, h, x, wpa, wpb, wo, bg, gamma, beta, alpha, layer, comm=None):
    T, D = x.shape
    WA, WB = oa.shape[1], ob.shape[1]
    tm = 512

    def kern(oa_ref, ob_ref, hga_ref, hgb_ref, x_ref, bg_ref, g_ref, b_ref, wpa_h, wpb_h, wo_h,
             x1_ref, u1_ref, pre_ref, wpa_v, wpb_v, wo_v, sa, sb, so):
        @pl.when(pl.program_id(0) == 0)
        def _():
            _load_cols(wpa_h, wpa_v, sa)
            _load_cols(wpb_h, wpb_v, sb)
            _load_rows(wo_h, wo_v, so)

        ya = _dot(oa_ref[...].astype(_MXU), wpa_v[...])
        yb = _dot(ob_ref[...].astype(_MXU), wpb_v[...])
        bgv = bg_ref[...]
        ga = jax.nn.sigmoid(hga_ref[...].astype(_F32) + bgv[:, :D])
        gb = jax.nn.sigmoid(hgb_ref[...].astype(_F32) + bgv[:, D:])
        pre = ga * ya + gb * yb
        mix = _dot(pre.astype(_MXU), wo_v[...])
        u = alpha * x_ref[...] + mix
        xhat, _ = _ln_stats(u)
        x1_ref[...] = xhat * g_ref[...] + b_ref[...]
        u1_ref[...] = u
        pre_ref[...] = pre.astype(pre_ref.dtype)

    return _call(
        kern, comm, name=f"mix_fwd_{layer}", grid=(T // tm,),
        in_specs=[_rows(tm, WA), _rows(tm, WB), *_gate_specs(h, tm, D), _rows(tm, D),
                  _whole((1, 2 * D)), _whole((1, D)), _whole((1, D)), _ANY, _ANY, _ANY],
        out_specs=[_rows(tm, D)] * 3,
        out_shape=[jax.ShapeDtypeStruct((T, D), _F32), jax.ShapeDtypeStruct((T, D), _F32),
                   jax.ShapeDtypeStruct((T, D), _ACT)],
        scratch_shapes=[pltpu.VMEM((WA, D), wpa.dtype), pltpu.VMEM((WB, D), wpb.dtype), pltpu.VMEM((D, D), wo.dtype),
                        pltpu.SemaphoreType.DMA((4,)), pltpu.SemaphoreType.DMA((4,)), pltpu.SemaphoreType.DMA((4,))],
        args=(oa, ob, h, h, x, bg, gamma, beta, wpa, wpb, wo), semantics=("arbitrary",))


def _ffn_fwd(x1, wfi, wfo, gamma, beta, alpha, layer, comm=None):
    T, D = x1.shape
    F2 = 4 * wfi.shape[-1]
    F = F2 // 2
    tm = 256
    fc = F // 2

    def kern(x_ref, g_ref, b_ref, wi_h, wo_h, x2_ref, u2_ref, act_ref, gu_ref, xb_ref, wi_v, wo_v, si, so):
        @pl.when(pl.program_id(0) == 0)
        def _():
            _load_cols(wi_h, wi_v, si)
            _load_rows(wo_h, wo_v, so)

        x = x_ref[...]
        xb = x.astype(_MXU)
        xb_ref[...] = xb.astype(xb_ref.dtype)
        ffn = jnp.zeros((tm, D), _F32)
        for c in range(2):
            g = _dot(xb, wi_v[:, c * fc:(c + 1) * fc])
            u = _dot(xb, wi_v[:, F + c * fc:F + (c + 1) * fc])
            act = g * jax.nn.sigmoid(g) * u
            ab = act.astype(_MXU)
            ffn = ffn + _dot(ab, wo_v[c * fc:(c + 1) * fc, :])
            act_ref[:, c * fc:(c + 1) * fc] = ab.astype(act_ref.dtype)
            gu_ref[:, c * fc:(c + 1) * fc] = g.astype(gu_ref.dtype)
            gu_ref[:, F + c * fc:F + (c + 1) * fc] = u.astype(gu_ref.dtype)
        uu = alpha * x + ffn
        xhat, _ = _ln_stats(uu)
        x2_ref[...] = xhat * g_ref[...] + b_ref[...]
        u2_ref[...] = uu

    return _call(
        kern, comm, name=f"ffn_fwd_{layer}", grid=(T // tm,),
        in_specs=[_rows(tm, D), _whole((1, D)), _whole((1, D)), _ANY, _ANY],
        out_specs=[_rows(tm, D), _rows(tm, D), _rows(tm, F), _rows(tm, F2), _rows(tm, D)],
        out_shape=[jax.ShapeDtypeStruct((T, D), _F32), jax.ShapeDtypeStruct((T, D), _F32),
                   jax.ShapeDtypeStruct((T, F), _ACT), jax.ShapeDtypeStruct((T, F2), _ACT),
                   jax.ShapeDtypeStruct((T, D), _ACT)],
        scratch_shapes=[pltpu.VMEM((D, F2), wfi.dtype), pltpu.VMEM((F, D), wfo.dtype),
                        pltpu.SemaphoreType.DMA((4,)), pltpu.SemaphoreType.DMA((4,))],
        args=(x1, gamma, beta, wfi, wfo), semantics=("arbitrary",))


def _ffn_bwd(u2, dy_or_target, gu, gamma, beta, wfi, wfo, alpha, layer, last):
    T, D = u2.shape
    F2 = gu.shape[1]
    F = F2 // 2
    tm = 256
    fc = F // 2

    def kern(u_ref, dy_ref, gu_ref, g_ref, b_ref, wi_h, wo_h, dx_ref, dub_ref, dgu_ref, st_ref, wi_v, wo_v, si, so):
        @pl.when(pl.program_id(0) == 0)
        def _():
            _load_cols(wi_h, wi_v, si)
            _load_rows(wo_h, wo_v, so)
            st_ref[...] = jnp.zeros_like(st_ref)

        gam = g_ref[...]
        u = u_ref[...]
        if last:
            xhat0, _ = _ln_stats(u)
            err = xhat0 * gam + b_ref[...] - dy_ref[...]
            dy = err * (1.0 / D)
            st_ref[2:3, :] += jnp.sum(err * err, axis=0, keepdims=True)
        else:
            dy = dy_ref[...]
        du, dgam, dbet, _ = _ln_bwd(u, dy, gam)
        st_ref[0:1, :] += dgam
        st_ref[1:2, :] += dbet
        dub = du.astype(_MXU)
        dub_ref[...] = dub.astype(dub_ref.dtype)
        dx = alpha * du
        for c in range(2):
            dact = _dot_nt(dub, wo_v[c * fc:(c + 1) * fc, :])
            g = gu_ref[:, c * fc:(c + 1) * fc].astype(_F32)
            uu = gu_ref[:, F + c * fc:F + (c + 1) * fc].astype(_F32)
            sg = jax.nn.sigmoid(g)
            dg = (dact * uu * (sg * (1.0 + g * (1.0 - sg)))).astype(_MXU)
            dup = (dact * (g * sg)).astype(_MXU)
            dgu_ref[:, c * fc:(c + 1) * fc] = dg.astype(dgu_ref.dtype)
            dgu_ref[:, F + c * fc:F + (c + 1) * fc] = dup.astype(dgu_ref.dtype)
            dx = dx + _dot_nt(dg, wi_v[:, c * fc:(c + 1) * fc]) + _dot_nt(dup, wi_v[:, F + c * fc:F + (c + 1) * fc])
        dx_ref[...] = dx

    return pl.pallas_call(
        kern, name=f"ffn_bwd_{layer}", grid=(T // tm,),
        in_specs=[_rows(tm, D), _rows(tm, D), _rows(tm, F2), _whole((1, D)), _whole((1, D)), _ANY, _ANY],
        out_specs=[_rows(tm, D), _rows(tm, D), _rows(tm, F2), _whole((8, D))],
        out_shape=[jax.ShapeDtypeStruct((T, D), _F32), jax.ShapeDtypeStruct((T, D), _ACT),
                   jax.ShapeDtypeStruct((T, F2), _ACT), jax.ShapeDtypeStruct((8, D), _F32)],
        scratch_shapes=[pltpu.VMEM((D, F2), wfi.dtype), pltpu.VMEM((F, D), wfo.dtype),
                        pltpu.SemaphoreType.DMA((4,)), pltpu.SemaphoreType.DMA((4,))],
        compiler_params=_cparams("arbitrary"),
    )(u2, dy_or_target, gu, gamma, beta, wfi, wfo)


def _residual_nt(res, res_scale, pieces, w, name, comm=None):
    T, K = res.shape
    widths = [p.shape[1] for p in pieces]
    N = sum(widths)
    tm = 512

    def kern(r_ref, *refs):
        d_refs, (w_hbm, o_ref, w_v, sem) = refs[:len(pieces)], refs[len(pieces):]

        @pl.when(pl.program_id(0) == 0)
        def _():
            _load_cols(w_hbm, w_v, sem)

        acc = res_scale * r_ref[...]
        off = 0
        for d_ref, width in zip(d_refs, widths):
            acc = acc + _dot_nt(d_ref[...].astype(_MXU), w_v[:, off:off + width])
            off += width
        o_ref[...] = acc

    outs, extra = _call(
        kern, comm, name=name, grid=(T // tm,),
        in_specs=[_rows(tm, K)] + [_rows(tm, width) for width in widths] + [_ANY], out_specs=[_rows(tm, K)],
        out_shape=[jax.ShapeDtypeStruct((T, K), _F32)],
        scratch_shapes=[pltpu.VMEM((K, N), w.dtype), pltpu.SemaphoreType.DMA((4,))],
        args=(res, *pieces, w), semantics=("arbitrary",))
    return outs[0], extra


def _grad_w_pieces(a, pieces, name, comm=None):
    T, M = a.shape
    widths = [p.shape[1] for p in pieces]
    bw = functools.reduce(_gcd, widths + [512])
    first = [sum(widths[:p]) // bw for p in range(len(pieces))]
    count = [width // bw for width in widths]
    N = sum(widths)
    tk = 1024 if T % 1024 == 0 else 512
    nk = T // tk

    def kern(a_ref, *refs):
        b_refs, (o_ref, acc) = refs[:len(pieces)], refs[len(pieces):]
        j, k = pl.program_id(0), pl.program_id(1)

        @pl.when(k == 0)
        def _():
            acc[...] = jnp.zeros_like(acc)

        for b_ref, start, blocks in zip(b_refs, first, count):
            @pl.when(jnp.logical_and(j >= start, j < start + blocks))
            def _():
                acc[...] += _dot_tn(a_ref[...].astype(_MXU), b_ref[...].astype(_MXU))

        @pl.when(k == nk - 1)
        def _():
            o_ref[...] = acc[...].astype(o_ref.dtype)

    def piece_spec(start, blocks):
        def index(j, k):
            mine = jnp.logical_and(j >= start, j < start + blocks)
            return jnp.where(mine, k, 0), jnp.where(mine, j - start, 0)
        return pl.BlockSpec((tk, bw), index)

    outs, extra = _call(
        kern, comm, name=name, grid=(N // bw, nk),
        in_specs=[pl.BlockSpec((tk, M), lambda j, k: (k, 0))] + [piece_spec(s, c) for s, c in zip(first, count)],
        out_specs=[pl.BlockSpec((M, bw), lambda j, k: (0, j))],
        out_shape=[jax.ShapeDtypeStruct((M, N), _ACT)], scratch_shapes=[pltpu.VMEM((M, bw), _F32)],
        args=(a, *pieces), semantics=("parallel", "arbitrary"))
    return outs[0], extra


def _gcd(a, b):
    while b:
        a, b = b, a % b
    return a


def _mix_bwd(u1, dx1, oa, ob, h, wpa, wpb, wo, bg, gamma, layer):
    T, D = u1.shape
    WA, WB = wpa.shape[-2], wpb.shape[-2]
    tm = 512

    def kern(u_ref, dx_ref, oa_ref, ob_ref, hga_ref, hgb_ref, bg_ref, g_ref, wpa_h, wpb_h, wo_h,
             du_ref, dub_ref, dya_ref, dyb_ref, dhg_ref, doa_ref, dob_ref, st_ref,
             wpa_v, wpb_v, wo_v, sa, sb, so):
        @pl.when(pl.program_id(0) == 0)
        def _():
            _load_cols(wpa_h, wpa_v, sa)
            _load_cols(wpb_h, wpb_v, sb)
            _load_rows(wo_h, wo_v, so)
            st_ref[...] = jnp.zeros_like(st_ref)

        du, dgam, dbet, _ = _ln_bwd(u_ref[...], dx_ref[...], g_ref[...])
        st_ref[1:2, :D] += dgam
        st_ref[1:2, D:] += dbet
        du_ref[...] = du
        dub = du.astype(_MXU)
        dub_ref[...] = dub.astype(dub_ref.dtype)
        dpre = _dot_nt(dub, wo_v[...])
        bgv = bg_ref[...]
        ga = jax.nn.sigmoid(hga_ref[...].astype(_F32) + bgv[:, :D])
        gb = jax.nn.sigmoid(hgb_ref[...].astype(_F32) + bgv[:, D:])
        dya = (dpre * ga).astype(_MXU)
        dyb = (dpre * gb).astype(_MXU)
        dsa = dpre * _dot(oa_ref[...].astype(_MXU), wpa_v[...]) * (ga * (1.0 - ga))
        dsb = dpre * _dot(ob_ref[...].astype(_MXU), wpb_v[...]) * (gb * (1.0 - gb))
        st_ref[0:1, :D] += jnp.sum(dsa, axis=0, keepdims=True)
        st_ref[0:1, D:] += jnp.sum(dsb, axis=0, keepdims=True)
        dya_ref[...] = dya.astype(dya_ref.dtype)
        dyb_ref[...] = dyb.astype(dyb_ref.dtype)
        dhg_ref[:, :D] = dsa.astype(dhg_ref.dtype)
        dhg_ref[:, D:] = dsb.astype(dhg_ref.dtype)
        doa_ref[...] = _dot_nt(dya, wpa_v[...]).astype(doa_ref.dtype)
        dob_ref[...] = _dot_nt(dyb, wpb_v[...]).astype(dob_ref.dtype)

    return pl.pallas_call(
        kern, name=f"mix_bwd_{layer}", grid=(T // tm,),
        in_specs=[_rows(tm, D), _rows(tm, D), _rows(tm, WA), _rows(tm, WB), *_gate_specs(h, tm, D), _whole((1, 2 * D)),
                  _whole((1, D)), _ANY, _ANY, _ANY],
        out_specs=[_rows(tm, D)] * 4 + [_rows(tm, 2 * D), _rows(tm, WA), _rows(tm, WB), _whole((8, 2 * D))],
        out_shape=[jax.ShapeDtypeStruct((T, D), _F32)] + [jax.ShapeDtypeStruct((T, D), _ACT)] * 3
        + [jax.ShapeDtypeStruct((T, 2 * D), _ACT), jax.ShapeDtypeStruct((T, WA), _ACT),
           jax.ShapeDtypeStruct((T, WB), _ACT), jax.ShapeDtypeStruct((8, 2 * D), _F32)],
        scratch_shapes=[pltpu.VMEM((WA, D), wpa.dtype), pltpu.VMEM((WB, D), wpb.dtype), pltpu.VMEM((D, D), wo.dtype),
                        pltpu.SemaphoreType.DMA((4,)), pltpu.SemaphoreType.DMA((4,)), pltpu.SemaphoreType.DMA((4,))],
        compiler_params=_cparams("arbitrary"),
    )(u1, dx1, oa, ob, h, h, bg, gamma, wpa, wpb, wo)


def _grad_w(a, b, *, col_shards, name, comm=None):
    T, M = a.shape
    N = b.shape[1]
    tk = 1024 if T % 1024 == 0 else 512
    n = N // 4 if col_shards else N
    whole = M * N * 4 <= _GRAD_ACC_BYTES
    tn = N if whole else (n if col_shards else _divisor_tile(N, _GRAD_ACC_BYTES // (4 * M)))
    nk = T // tk

    def kern(a_ref, b_ref, o_ref, acc):
        k = pl.program_id(1)

        @pl.when(k == 0)
        def _():
            acc[...] = jnp.zeros_like(acc)

        acc[...] += _dot_tn(a_ref[...].astype(_MXU), b_ref[...].astype(_MXU))

        @pl.when(k == nk - 1)
        def _():
            if col_shards and whole:
                for s in range(4):
                    o_ref[s] = acc[:, s * n:(s + 1) * n].astype(o_ref.dtype)
            else:
                o_ref[...] = acc[...].astype(o_ref.dtype)

    if col_shards:
        out_spec = (pl.BlockSpec((4, M, n), lambda j, k: (0, 0, 0)) if whole
                    else pl.BlockSpec((None, M, n), lambda j, k: (j, 0, 0)))
        out_shape = jax.ShapeDtypeStruct((4, M, n), _ACT)
    else:
        out_spec = pl.BlockSpec((M, tn), lambda j, k: (0, j))
        out_shape = jax.ShapeDtypeStruct((M, N), _ACT)
    outs, extra = _call(
        kern, comm, name=name, grid=(N // tn, nk),
        in_specs=[pl.BlockSpec((tk, M), lambda j, k: (k, 0)), pl.BlockSpec((tk, tn), lambda j, k: (k, j))],
        out_specs=[out_spec], out_shape=[out_shape], scratch_shapes=[pltpu.VMEM((M, tn), _F32)],
        args=(a, b), semantics=("parallel", "arbitrary"))
    return outs[0], extra


def _bias_tiles(rel):
    H = rel.shape[0]
    span = _TQ * _BAND_TILES - 1
    edge = span - _REL_CLIP
    gvec = jnp.concatenate([jnp.broadcast_to(rel[:, :1], (H, edge)), rel, jnp.broadcast_to(rel[:, -1:], (H, edge))], axis=1)
    width = _BIAS_TILES * _TQ
    period = width + _TQ
    tiled = jnp.broadcast_to(jnp.pad(gvec[:, ::-1], ((0, 0), (0, 1)))[:, None, :], (H, _TQ, period))
    rows = tiled.reshape(H, _TQ * period)[:, :_TQ * (period - 1)].reshape(H, _TQ, period - 1)[:, :, _TQ - 1:]
    r = jnp.arange(_TQ)[:, None]
    u = jnp.arange(width)[None, :]
    d = 4 * _TQ + r - u
    rm = r % _CHUNK
    valid = (d >= rm - (_CHUNK - 1)) & (d <= rm + 8 * _CHUNK)
    tiles = jnp.where(valid[None], rows, _MASKED)
    return tiles.reshape(H // 2, 2 * _TQ, _BIAS_TILES, _TQ).transpose(0, 2, 1, 3)


def _fold_bias_grad(db):
    H = 2 * db.shape[0]
    width = _BIAS_TILES * _TQ
    period = width + _TQ
    x = jnp.pad(db.transpose(0, 2, 1, 3).reshape(H, _TQ, width), ((0, 0), (0, 0), (_TQ - 1, 0)))
    skew = jnp.pad(x.reshape(H, _TQ * (period - 1)), ((0, 0), (0, _TQ))).reshape(H, _TQ, period)
    dg = skew.sum(axis=1)[:, :period - 1][:, ::-1]
    span = _TQ * _BAND_TILES - 1
    edge = span - _REL_CLIP
    mid = dg[:, edge:edge + 2 * _REL_CLIP + 1]
    lo = dg[:, :edge].sum(axis=1)
    hi = dg[:, edge + 2 * _REL_CLIP + 1:].sum(axis=1)
    return mid.at[:, 0].add(lo).at[:, -1].add(hi)


def _band_window(i):
    j0 = jnp.maximum(i - (_BAND_TILES - 1), 0)
    return j0, (_BAND_TILES - 1) - (i - j0)


def _head_masks():
    lane = lax.broadcasted_iota(jnp.int32, (1, _LANES), 1)
    return [(lane // _HEAD) == hh for hh in range(2)]


def _stack_heads(x, masks):
    return jnp.concatenate([jnp.where(m, x, jnp.zeros_like(x)) for m in masks], axis=0)


def _unstack_heads(y, masks):
    return jnp.where(masks[0], y[:_TQ], y[_TQ:])


def _scaled(q):
    return q * jnp.asarray(_HEAD ** -0.5, q.dtype)


def _band_probs(q2, k_ref, b_ref, j0, boff):
    s = []
    for j in range(_BAND_TILES):
        kj = k_ref[pl.ds(pl.multiple_of((j0 + j) * _TQ, _TQ), _TQ), :]
        s.append(_dot_nt(q2, kj) + b_ref[boff + j])
    m = jnp.max(functools.reduce(jnp.maximum, s), axis=-1, keepdims=True)
    p = [jnp.exp(x - m) for x in s]
    l = jnp.sum(functools.reduce(lambda a, b: a + b, p), axis=-1, keepdims=True)
    return p, 1.0 / l


def _qkv_specs(T, cb, npair, tq=_TQ):
    return [pl.BlockSpec((tq, _LANES), lambda h, i: (i, cb + h)),
            pl.BlockSpec((T, _LANES), lambda h, i: (0, cb + npair + h)),
            pl.BlockSpec((T, _LANES), lambda h, i: (0, cb + 2 * npair + h))]


def _attn_a_fwd(hq, bias, col0, width, layer, comm=None):
    T = hq.shape[0]
    npair = width // _LANES
    nsub = _BAND_SUBTILES
    tq = nsub * _TQ

    def kern(q_ref, k_ref, v_ref, b_ref, o_ref):
        masks = _head_masks()
        q = _scaled(q_ref[...])
        for s in range(nsub):
            part = slice(s * _TQ, (s + 1) * _TQ)
            j0, boff = _band_window(nsub * pl.program_id(1) + s)
            p, inv = _band_probs(_stack_heads(q[part], masks), k_ref, b_ref, j0, boff)
            o = jnp.zeros((2 * _TQ, _LANES), _F32)
            for j in range(_BAND_TILES):
                vj = v_ref[pl.ds(pl.multiple_of((j0 + j) * _TQ, _TQ), _TQ), :]
                o = o + _dot(p[j].astype(_MXU), vj)
            o_ref[part, :] = _unstack_heads(o * inv, masks).astype(o_ref.dtype)

    outs, extra = _call(
        kern, comm, name=f"band_attn_fwd_{layer}", grid=(npair, T // tq),
        in_specs=_qkv_specs(T, col0 // _LANES, npair, tq)
        + [pl.BlockSpec((None, _BIAS_TILES, 2 * _TQ, _TQ), lambda h, i: (h, 0, 0, 0))],
        out_specs=[pl.BlockSpec((tq, _LANES), lambda h, i: (i, h))],
        out_shape=[jax.ShapeDtypeStruct((T, width), _ACT)], scratch_shapes=[],
        args=(hq, hq, hq, bias), semantics=("arbitrary", "arbitrary"))
    return outs[0], extra


def _attn_a_bwd(hq, bias, do, col0, width, layer, comm=None):
    T = hq.shape[0]
    npair = width // _LANES
    nsub = _BAND_SUBTILES
    tq = nsub * _TQ
    nq = T // tq
    scale = _HEAD ** -0.5

    def kern(q_ref, k_ref, v_ref, b_ref, do_ref, dq_ref, dk_ref, dv_ref, db_ref, dk_acc, dv_acc):
        i = pl.program_id(1)

        @pl.when(i == 0)
        def _():
            dk_acc[...] = jnp.zeros_like(dk_acc)
            dv_acc[...] = jnp.zeros_like(dv_acc)
            db_ref[...] = jnp.zeros_like(db_ref)

        masks = _head_masks()
        q = _scaled(q_ref[...])
        do_t = do_ref[...]
        for s in range(nsub):
            part = slice(s * _TQ, (s + 1) * _TQ)
            j0, boff = _band_window(nsub * i + s)
            q2 = _stack_heads(q[part], masks)
            do2 = _stack_heads(do_t[part], masks).astype(_MXU)
            p, inv = _band_probs(q2, k_ref, b_ref, j0, boff)
            rows = [pl.ds(pl.multiple_of((j0 + j) * _TQ, _TQ), _TQ) for j in range(_BAND_TILES)]
            p = [x * inv for x in p]
            dp = [_dot_nt(do2, v_ref[rows[j], :]) for j in range(_BAND_TILES)]
            delta = jnp.sum(functools.reduce(lambda a, b: a + b, [p[j] * dp[j] for j in range(_BAND_TILES)]),
                            axis=-1, keepdims=True)
            dq = jnp.zeros((2 * _TQ, _LANES), _F32)
            for j in range(_BAND_TILES):
                ds = p[j] * (dp[j] - delta)
                db_ref[boff + j] += ds
                dsb = ds.astype(_MXU)
                dq = dq + _dot(dsb, k_ref[rows[j], :])
                dk_acc[rows[j], :] += _dot_tn(dsb, q2)
                dv_acc[rows[j], :] += _dot_tn(p[j].astype(_MXU), do2)
            dq_ref[part, :] = (_unstack_heads(dq, masks) * scale).astype(dq_ref.dtype)

        @pl.when(i == nq - 1)
        def _():
            dk_ref[...] = dk_acc[...].astype(dk_ref.dtype)
            dv_ref[...] = dv_acc[...].astype(dv_ref.dtype)

    strip = pl.BlockSpec((None, _BIAS_TILES, 2 * _TQ, _TQ), lambda h, i: (h, 0, 0, 0))
    tile = pl.BlockSpec((tq, _LANES), lambda h, i: (i, h))
    column = pl.BlockSpec((T, _LANES), lambda h, i: (0, h))
    outs, extra = _call(
        kern, comm, name=f"band_attn_bwd_{layer}", grid=(npair, nq),
        in_specs=_qkv_specs(T, col0 // _LANES, npair, tq) + [strip, tile],
        out_specs=[tile, column, column, strip],
        out_shape=[jax.ShapeDtypeStruct((T, width), _ACT)] * 3
        + [jax.ShapeDtypeStruct((npair, _BIAS_TILES, 2 * _TQ, _TQ), _F32)],
        scratch_shapes=[pltpu.VMEM((T, _LANES), _F32), pltpu.VMEM((T, _LANES), _F32)],
        args=(hq, hq, hq, bias, do), semantics=("arbitrary", "arbitrary"))
    return outs, extra


def _suffix_matrix():
    r = lax.broadcasted_iota(jnp.int32, (_TQ, _TQ), 0)
    c = lax.broadcasted_iota(jnp.int32, (_TQ, _TQ), 1)
    r2 = lax.broadcasted_iota(jnp.int32, (2 * _TQ, _TQ), 0)
    c2 = lax.broadcasted_iota(jnp.int32, (2 * _TQ, _TQ), 1)
    return (r > c).astype(_MXU), c2 - (r2 & (_TQ - 1))


def _suffix_sums(xs, tri):
    n, k = xs[0].shape[0], len(xs)
    his = [x.astype(_MXU) for x in xs]
    los = [(x - h.astype(_F32)).astype(_MXU) for x, h in zip(xs, his)]
    y = _dot(jnp.concatenate(his + los, axis=0), tri)
    return [y[j * n:(j + 1) * n] + y[(k + j) * n:(k + j + 1) * n] for j in range(k)]


def _stick_tiles(tiles, rel, carry_l, tri):
    zs = [_dot_nt(qs, kj) for qs, kj, _, _ in tiles]
    Ls, masks = [], []
    for z, (_, _, jj, _) in zip(zs, tiles):
        nsp = -(jnp.maximum(z, 0.0) + jnp.log(1.0 + jnp.exp(-jnp.abs(z))))
        if isinstance(jj, int):
            mask = (rel < 0) if jj == 0 else None
        else:
            mask = rel < jnp.where(jj == 0, 0, _TQ)
        Ls.append(nsp if mask is None else jnp.where(mask, nsp, 0.0))
        masks.append(mask)
    carry_l = list(carry_l)
    ws = []
    for z, L, suffix, mask, (_, _, _, sub) in zip(zs, Ls, _suffix_sums(Ls, tri), masks, tiles):
        w = jnp.exp(z + L + suffix + carry_l[sub])
        ws.append(w if mask is None else jnp.where(mask, w, 0.0))
        carry_l[sub] = carry_l[sub] + jnp.sum(L, axis=-1, keepdims=True)
    return zs, Ls, ws, masks, carry_l


def _sweep(i, step, zero):
    nsub = _SB_SUBTILES

    def window():
        tiles = [(s, jj) for jj in range(_SB_WINDOW) for s in range(nsub)]
        return tuple((jnp.int32(_SB_WINDOW),) + c for c in step(tiles, [zero] * nsub))

    start = lax.cond(i >= -(-(_SB_WINDOW - 1) // nsub), window, lambda: tuple((jnp.int32(0),) + zero for _ in range(nsub)))
    outs = []
    for s in range(nsub):
        def done(c, s=s):
            return jnp.logical_or(c[0] > nsub * i + s, jnp.max(c[1]) < _EXP_ZERO_BELOW)

        def more(c, s=s):
            carries = [None] * nsub
            carries[s] = c[1:]
            return (c[0] + 1,) + step([(s, c[0])], carries)[s]

        outs.append(lax.while_loop(lambda c, done=done: jnp.logical_not(done(c)), more, start[s]))
    return outs


def _sb_fwd(hq, col0, width, layer, comm=None):
    T = hq.shape[0]
    npair = width // _LANES
    nsub = _SB_SUBTILES
    tq = nsub * _TQ

    def kern(q_ref, k_ref, v_ref, o_ref):
        i = pl.program_id(1)
        masks = _head_masks()
        tri, rel = _suffix_matrix()
        q = _scaled(q_ref[...])
        q2 = [_stack_heads(q[s * _TQ:(s + 1) * _TQ], masks) for s in range(nsub)]

        def step(tiles, carries):
            rows = [pl.ds(pl.multiple_of((nsub * i + s - jj) * _TQ, _TQ), _TQ) for s, jj in tiles]
            cls = [None if c is None else c[0] for c in carries]
            accs = [None if c is None else c[1] for c in carries]
            _, _, ws, _, cls = _stick_tiles([(q2[s], k_ref[r, :], jj, s) for (s, jj), r in zip(tiles, rows)], rel, cls, tri)
            for w, r, (s, _) in zip(ws, rows, tiles):
                accs[s] = accs[s] + _dot(w.astype(_MXU), v_ref[r, :])
            return [None if c is None else (cls[s], accs[s]) for s, c in enumerate(carries)]

        outs = _sweep(i, step, (jnp.zeros((2 * _TQ, 1), _F32), jnp.zeros((2 * _TQ, _LANES), _F32)))
        for s in range(nsub):
            o_ref[s * _TQ:(s + 1) * _TQ, :] = _unstack_heads(outs[s][2], masks)

    outs, extra = _call(
        kern, comm, name=f"stick_attn_fwd_{layer}", grid=(npair, T // tq),
        in_specs=_qkv_specs(T, col0 // _LANES, npair, tq),
        out_specs=[pl.BlockSpec((tq, _LANES), lambda h, i: (i, h))],
        out_shape=[jax.ShapeDtypeStruct((T, width), _F32)], scratch_shapes=[],
        args=(hq, hq, hq), semantics=("arbitrary", "arbitrary"))
    return outs[0], extra


def _sb_bwd(hq, o, do, col0, width, layer, comm=None):
    T = hq.shape[0]
    npair = width // _LANES
    nsub = _SB_SUBTILES
    tq = nsub * _TQ
    nq = T // tq
    scale = _HEAD ** -0.5

    def kern(q_ref, k_ref, v_ref, o_ref, do_ref, dq_ref, dk_ref, dv_ref, dk_acc, dv_acc):
        i = pl.program_id(1)

        @pl.when(i == 0)
        def _():
            dk_acc[...] = jnp.zeros_like(dk_acc)
            dv_acc[...] = jnp.zeros_like(dv_acc)

        masks = _head_masks()
        tri, rel = _suffix_matrix()
        q = _scaled(q_ref[...])
        do_t = do_ref[...]
        prod = do_t.astype(_F32) * o_ref[...]
        part = [slice(s * _TQ, (s + 1) * _TQ) for s in range(nsub)]
        q2 = [_stack_heads(q[p], masks) for p in part]
        do2 = [_stack_heads(do_t[p], masks).astype(_MXU) for p in part]
        dsum = [jnp.sum(_stack_heads(prod[p], masks), axis=-1, keepdims=True) for p in part]

        def step(tiles, carries):
            rows = [pl.ds(pl.multiple_of((nsub * i + s - jj) * _TQ, _TQ), _TQ) for s, jj in tiles]
            kjs = [k_ref[r, :] for r in rows]
            cls, cgs, dqs = ([None if c is None else c[n] for c in carries] for n in range(3))
            zs, Ls, ws, tile_masks, cls = _stick_tiles([(q2[s], kj, jj, s) for (s, jj), kj in zip(tiles, kjs)], rel, cls, tri)
            wbs = [w.astype(_MXU) for w in ws]
            gs = [wb.astype(_F32) * _dot_nt(do2[s], v_ref[r, :]) for wb, r, (s, _) in zip(wbs, rows, tiles)]
            for z, L, g, later, mask, wb, kj, r, (s, _) in zip(zs, Ls, gs, _suffix_sums(gs, tri), tile_masks, wbs, kjs,
                                                               rows, tiles):
                dz = g - jnp.exp(z + L) * (dsum[s] - (later + cgs[s]))
                if mask is not None:
                    dz = jnp.where(mask, dz, 0.0)
                dzb = dz.astype(_MXU)
                dk_acc[r, :] += _dot_tn(dzb, q2[s])
                dv_acc[r, :] += _dot_tn(wb, do2[s])
                dqs[s] = dqs[s] + _dot(dzb, kj)
                cgs[s] = cgs[s] + jnp.sum(g, axis=-1, keepdims=True)
            return [None if c is None else (cls[s], cgs[s], dqs[s]) for s, c in enumerate(carries)]

        zc = jnp.zeros((2 * _TQ, 1), _F32)
        outs = _sweep(i, step, (zc, zc, jnp.zeros((2 * _TQ, _LANES), _F32)))
        for s in range(nsub):
            dq_ref[part[s], :] = (_unstack_heads(outs[s][3], masks) * scale).astype(dq_ref.dtype)

        @pl.when(i == nq - 1)
        def _():
            dk_ref[...] = dk_acc[...].astype(dk_ref.dtype)
            dv_ref[...] = dv_acc[...].astype(dv_ref.dtype)

    tile_spec = pl.BlockSpec((tq, _LANES), lambda h, i: (i, h))
    column = pl.BlockSpec((T, _LANES), lambda h, i: (0, h))
    outs, extra = _call(
        kern, comm, name=f"stick_attn_bwd_{layer}", grid=(npair, nq),
        in_specs=_qkv_specs(T, col0 // _LANES, npair, tq) + [tile_spec, tile_spec],
        out_specs=[tile_spec, column, column],
        out_shape=[jax.ShapeDtypeStruct((T, width), _ACT)] * 3,
        scratch_shapes=[pltpu.VMEM((T, _LANES), _F32), pltpu.VMEM((T, _LANES), _F32)],
        args=(hq, hq, hq, o, do), semantics=("arbitrary", "arbitrary"))
    return outs, extra


_DENSE = ("w_in", "w_proj_a", "w_proj_b", "w_out", "w_ffn_in", "w_ffn_out")
_COL_SHARDED = {"w_in": True, "w_proj_a": True, "w_proj_b": True, "w_out": False, "w_ffn_in": True, "w_ffn_out": False}
_SMALL = ("b_gate", "rel_bias", "ln1_g", "ln1_b", "ln2_g", "ln2_b")


class _Plans:
    def __init__(self, plans=None, own_w_in=None):
        self.plans = plans or {}
        self.own_w_in = own_w_in or {}

    def start(self, key):
        if key not in self.plans:
            return None, None
        return self.plans[key]()

    @staticmethod
    def finish(done, extra):
        if done is not None:
            done(extra)


def _layer_fwd(x, W, small, l, alpha, plans):
    WA = small["rel_bias"].shape[1] * _HEAD
    row = lambda v: v[l].reshape(1, -1)
    if l in plans.own_w_in:
        h, xb, W["w_in"] = _in_proj_gathering(x, plans.own_w_in[l], l)
    else:
        comm, done = plans.start(f"in_proj_{l}")
        (h, xb), extra = _in_proj(x, W["w_in"], l, comm)
        plans.finish(done, extra)
    WB = (h.shape[1] - 2 * x.shape[1] - 3 * WA) // 3
    bias = _bias_tiles(small["rel_bias"][l])
    comm, done = plans.start(f"band_fwd_{l}")
    oa, extra = _attn_a_fwd(h, bias, 0, WA, l, comm)
    plans.finish(done, extra)
    comm, done = plans.start(f"stick_fwd_{l}")
    ob, extra = _sb_fwd(h, 3 * WA, WB, l, comm)
    plans.finish(done, extra)
    comm, done = plans.start(f"mix_fwd_{l}")
    (x1, u1, pre), extra = _mix_fwd(oa, ob, h, x, W["w_proj_a"], W["w_proj_b"], W["w_out"], row(small["b_gate"]),
                                            row(small["ln1_g"]), row(small["ln1_b"]), alpha, l, comm)
    plans.finish(done, extra)
    comm, done = plans.start(f"ffn_fwd_{l}")
    (x2, u2, act, gu, x1b), extra = _ffn_fwd(x1, W["w_ffn_in"], W["w_ffn_out"], row(small["ln2_g"]),
                                             row(small["ln2_b"]), alpha, l, comm)
    plans.finish(done, extra)
    return x2, dict(xb=xb, h=h, bias=bias, oa=oa, ob=ob, x1b=x1b, u1=u1, pre=pre, u2=u2, act=act, gu=gu)


def _layer_bwd(dy_or_target, S, W, small, l, last, alpha, plans, gw):
    D = S["xb"].shape[1]
    WA, WB = S["oa"].shape[1], S["ob"].shape[1]
    row = lambda v: v[l].reshape(1, -1)

    def blocks(g, n):
        return g if _COL_SHARDED[n] else g.reshape(4, g.shape[0] // 4, g.shape[1])

    dx1, du2b, dgu, st2 = _ffn_bwd(S["u2"], dy_or_target, S["gu"], row(small["ln2_g"]), row(small["ln2_b"]),
                                   W["w_ffn_in"], W["w_ffn_out"], alpha, l, last)
    gw["w_ffn_in"] = blocks(_grad_w(S["x1b"], dgu, col_shards=True, name=f"grad_w_ffn_in_{l}")[0], "w_ffn_in")
    gw["w_ffn_out"] = blocks(_grad_w(S["act"], du2b, col_shards=False, name=f"grad_w_ffn_out_{l}")[0], "w_ffn_out")
    du1, du1b, dya, dyb, dhg, doa, dob, st1 = _mix_bwd(S["u1"], dx1, S["oa"], S["ob"], S["h"], W["w_proj_a"],
                                                       W["w_proj_b"], W["w_out"], row(small["b_gate"]),
                                                       row(small["ln1_g"]), l)
    gw["w_out"] = blocks(_grad_w(S["pre"], du1b, col_shards=False, name=f"grad_w_out_{l}")[0], "w_out")
    gw["w_proj_a"] = blocks(_grad_w(S["oa"], dya, col_shards=True, name=f"grad_w_proj_a_{l}")[0], "w_proj_a")
    gw["w_proj_b"] = blocks(_grad_w(S["ob"], dyb, col_shards=True, name=f"grad_w_proj_b_{l}")[0], "w_proj_b")
    comm, done = plans.start(f"band_bwd_{l}")
    (dqa, dka, dva, dbias), extra = _attn_a_bwd(S["h"], S["bias"], doa, 0, WA, l, comm)
    plans.finish(done, extra)
    comm, done = plans.start(f"stick_bwd_{l}")
    (dqb, dkb, dvb), extra = _sb_bwd(S["h"], S["ob"], dob, 3 * WA, WB, l, comm)
    plans.finish(done, extra)
    dh = [dqa, dka, dva, dqb, dkb, dvb, dhg]
    comm, done = plans.start(f"grad_w_in_{l}")
    gw["w_in"], extra = _grad_w_pieces(S["xb"], dh, f"grad_w_in_{l}", comm)
    plans.finish(done, extra)
    comm, done = plans.start(f"in_proj_bwd_{l}")
    dx, extra = _residual_nt(du1, alpha, dh, W["w_in"], f"in_proj_bwd_{l}", comm)
    plans.finish(done, extra)
    gs = dict(b_gate=st1[0], rel_bias=_fold_bias_grad(dbias), ln1_g=st1[1, :D], ln1_b=st1[1, D:],
              ln2_g=st2[0], ln2_b=st2[1])
    return dx, gs, st2[2]


def _local_step(x, target, W, small, plans=None, gws=None):
    depth = len(W)
    alpha = float((2 * depth) ** 0.25)
    plans = plans or _Plans()
    gws = gws if gws is not None else [dict() for _ in range(depth)]
    saved = []
    h = x
    for l in range(depth):
        h, S = _layer_fwd(h, W[l], small, l, alpha, plans)
        saved.append(S)
    gss = [None] * depth
    d = target
    sq = None
    for l in reversed(range(depth)):
        d, gss[l], sq_l = _layer_bwd(d, saved[l], W[l], small, l, l == depth - 1, alpha, plans, gws[l])
        if l == depth - 1:
            sq = sq_l
    return sq, d, gws, gss


def _place():
    return lax.axis_index("x"), lax.axis_index("y"), lax.axis_index("c")


def _remote(src, dst, send_sem, recv_sem, to):
    return pltpu.make_async_remote_copy(src_ref=src, dst_ref=dst, send_sem=send_sem, recv_sem=recv_sem,
                                        device_id=to, device_id_type=_MESH)


def _half(ref, hc):
    kh = ref.shape[0] // 2
    return ref.at[pl.ds(pl.multiple_of(hc * kh, 16), kh), :]


def _gather_plan(blocks, fractions):
    nt = len(blocks)

    def run(step, nsteps, ins, outs, sems):
        send_sems, recv_sems, loc_sems = sems
        x, y, c = _place()
        k = 2 * x + y
        me, sibling = (x, y, c), (x, y, 1 - c)
        chips = [(1 - x, y), (x, 1 - y), (1 - x, 1 - y)]
        chip_k = [2 * cx + cy for cx, cy in chips]

        def ici(t, s, owner_k, to, src=None):
            dst = _half(outs[t].at[owner_k], c)
            return _remote(dst if src is None else src, dst, send_sems.at[t, s], recv_sems.at[t, s], to)

        def passed(t, s, hc, to):
            blk = _half(outs[t].at[chip_k[s]], hc)
            return _remote(blk, blk, send_sems.at[t, 3 + s], recv_sems.at[t, 3 + s], to)

        def local(t):
            return pltpu.make_async_copy(ins[t], outs[t].at[k], loc_sems.at[t])

        @pl.when(step == 0)
        def _():
            for t in range(nt):
                local(t).start()
                for s, chip in enumerate(chips):
                    ici(t, s, k, (*chip, c), src=_half(ins[t], c)).start()

        for t in range(nt):
            @pl.when(step == min(nsteps - 1, int(fractions[t] * nsteps)))
            def _():
                for s in range(3):
                    ici(t, s, chip_k[s], me).wait_recv()
                    passed(t, s, c, sibling).start()

        @pl.when(step == nsteps - 1)
        def _():
            for t in range(nt):
                for s, chip in enumerate(chips):
                    passed(t, s, 1 - c, me).wait_recv()
            for t in range(nt):
                for s, chip in enumerate(chips):
                    ici(t, s, k, (*chip, c), src=_half(ins[t], c)).wait_send()
                    passed(t, s, c, sibling).wait_send()
                local(t).wait()

    return _Comm(blocks, [jax.ShapeDtypeStruct((4,) + b.shape, b.dtype) for b in blocks],
                 [pltpu.SemaphoreType.DMA((nt, 6)), pltpu.SemaphoreType.DMA((nt, 6)), pltpu.SemaphoreType.DMA((nt,))], run)


def _scatter_plan(grads, owners):
    nt = len(grads)
    shapes = [g.shape[1:] if g.ndim == 3 else (g.shape[0], g.shape[1] // 4) for g in grads]

    def run(step, nsteps, ins, outs, sems):
        send_sems, recv_sems, loc_sems = sems
        x, y, c = _place()
        me = 4 * x + 2 * y + c

        def target(r):
            tx = 1 - x if r & 2 else x
            ty = 1 - y if r & 1 else y
            return tx, ty

        def block(t, chip):
            if len(ins[t].shape) == 3:
                return ins[t].at[chip]
            n = shapes[t][1]
            return ins[t].at[:, pl.ds(pl.multiple_of(chip * n, _LANES), n)]

        def send(t, r):
            tx, ty = target(r)
            return _remote(block(t, 2 * tx + ty), outs[t].at[me], send_sems.at[t, r], recv_sems.at[t, 2 * r + c],
                           (tx, ty, owners[t]))

        def local(t):
            return pltpu.make_async_copy(block(t, 2 * x + y), outs[t].at[me], loc_sems.at[t])

        @pl.when(step == 0)
        def _():
            for t in range(nt):
                @pl.when(c == owners[t])
                def _():
                    local(t).start()

                @pl.when(c != owners[t])
                def _():
                    send(t, 0).start()

                for r in range(1, 4):
                    send(t, r).start()

        @pl.when(step == nsteps - 1)
        def _():
            for t in range(nt):
                @pl.when(c == owners[t])
                def _():
                    for r in range(4):
                        sx, sy = target(r)
                        for cs in range(2):
                            if r == 0 and cs == owners[t]:
                                continue
                            src_dev = 4 * sx + 2 * sy + cs
                            _remote(block(t, 0), outs[t].at[src_dev], send_sems.at[t, r], recv_sems.at[t, 2 * r + cs],
                                    (x, y, c)).wait_recv()
                    local(t).wait()

                @pl.when(c != owners[t])
                def _():
                    send(t, 0).wait_send()

                for r in range(1, 4):
                    send(t, r).wait_send()

    return _Comm(grads, [jax.ShapeDtypeStruct((8,) + s, g.dtype) for s, g in zip(shapes, grads)],
                 [pltpu.SemaphoreType.DMA((nt, 4)), pltpu.SemaphoreType.DMA((nt, 8)), pltpu.SemaphoreType.DMA((nt,))], run)


def _share_plan(reduced, owners):
    nt = len(reduced)

    def run(step, nsteps, ins, outs, sems):
        del ins
        send_sems, recv_sems = sems
        x, y, c = _place()

        def give(t, to):
            return _remote(outs[t], outs[t], send_sems.at[t], recv_sems.at[t], to)

        @pl.when(step == 0)
        def _():
            for t in range(nt):
                @pl.when(c == owners[t])
                def _():
                    give(t, (x, y, 1 - c)).start()

        @pl.when(step == nsteps - 1)
        def _():
            for t in range(nt):
                @pl.when(c == owners[t])
                def _():
                    give(t, (x, y, 1 - c)).wait_send()

                @pl.when(c != owners[t])
                def _():
                    give(t, (x, y, c)).wait_recv()

    return _Comm(reduced, [jax.ShapeDtypeStruct(r.shape, r.dtype) for r in reduced],
                 [pltpu.SemaphoreType.DMA((nt,)), pltpu.SemaphoreType.DMA((nt,))], run,
                 aliases={t: t for t in range(nt)})


def _join(a, b):
    ni, no, ns = len(a.inputs), len(a.out_shapes), len(a.sems)

    def run(step, nsteps, ins, outs, sems):
        a.run(step, nsteps, ins[:ni], outs[:no], sems[:ns])
        b.run(step, nsteps, ins[ni:], outs[no:], sems[ns:])

    aliases = dict(a.aliases)
    aliases.update({ni + i: no + o for i, o in b.aliases.items()})
    return _Comm(a.inputs + b.inputs, a.out_shapes + b.out_shapes, a.sems + b.sems, run, aliases)


def _peer(x, y, c, r):
    px = 1 - x if r & 4 else x
    py = 1 - y if r & 2 else y
    pc = 1 - c if r & 1 else c
    return (px, py, pc), 4 * px + 2 * py + pc


def _sum_slots(st, name):
    _, K, n = st.shape
    tr = next(t for t in (256, 128, 64, 32, 16) if K % t == 0)

    def kern(s_ref, o_ref):
        acc = s_ref[0].astype(_F32)
        for d in range(1, 8):
            acc = acc + s_ref[d].astype(_F32)
        o_ref[...] = acc.astype(o_ref.dtype)

    return pl.pallas_call(
        kern, name=name, grid=(K // tr,),
        in_specs=[pl.BlockSpec((8, tr, n), lambda i: (0, i, 0))], out_specs=_rows(tr, n),
        out_shape=jax.ShapeDtypeStruct((K, n), _ACT),
        compiler_params=_cparams("parallel"),
    )(st)


def _all_reduce_small(p):
    R = p.shape[0]

    def body(p_ref, o_ref, stage, send_sems, recv_sems):
        x, y, c = _place()
        me = 4 * x + 2 * y + c
        stage[me] = p_ref[...]
        sent = []
        for r in range(1, 8):
            to, _ = _peer(x, y, c, r)
            cp = _remote(p_ref, stage.at[me], send_sems.at[r - 1], recv_sems.at[r - 1], to)
            cp.start()
            sent.append(cp)
        for r in range(1, 8):
            _, src_dev = _peer(x, y, c, r)
            _remote(p_ref, stage.at[src_dev], send_sems.at[r - 1], recv_sems.at[r - 1], (x, y, c)).wait_recv()
        acc = stage[0]
        for d in range(1, 8):
            acc = acc + stage[d]
        o_ref[...] = acc
        for cp in sent:
            cp.wait_send()

    vm = pl.BlockSpec(memory_space=pltpu.VMEM)
    return pl.pallas_call(
        body, name="all_reduce_small",
        in_specs=[vm], out_specs=vm,
        out_shape=jax.ShapeDtypeStruct((R, _LANES), _F32),
        scratch_shapes=[pltpu.VMEM((8, R, _LANES), _F32), pltpu.SemaphoreType.DMA((7,)), pltpu.SemaphoreType.DMA((7,))],
    )(p)


def _adamw_update(gv, w_ref, m_ref, v_ref, gf_ref, d_ref, nm_ref, nv_ref):
    nm = _B1 * m_ref[...] + (1.0 - _B1) * gv
    nv = _B2 * v_ref[...] + (1.0 - _B2) * (gv * gv)
    m_hat = nm / (1.0 - _B1 ** _STEP)
    v_hat = nv / (1.0 - _B2 ** _STEP)
    gf_ref[...] = gv
    d_ref[...] = -_LR * (m_hat / (jnp.sqrt(v_hat) + _EPS) + _WD * w_ref[...])
    nm_ref[...] = nm
    nv_ref[...] = nv


def _adamw_layers(w, g_layers, m, v, name):
    _, K, n = w.shape
    tr = next(t for t in (256, 128, 64, 32, 16) if K % t == 0)

    def kern(w_ref, g0_ref, g1_ref, m_ref, v_ref, *out_refs):
        first = pl.program_id(0) == 0
        gv = jnp.where(first, g0_ref[...].astype(_F32), g1_ref[...].astype(_F32))
        _adamw_update(gv, w_ref, m_ref, v_ref, *out_refs)

    stacked = pl.BlockSpec((None, tr, n), lambda l, i: (l, i, 0))
    layer = pl.BlockSpec((tr, n), lambda l, i: (i, 0))
    return tuple(pl.pallas_call(
        kern, name=name, grid=(2, K // tr),
        in_specs=[stacked, layer, layer, stacked, stacked], out_specs=[stacked] * 4,
        out_shape=[jax.ShapeDtypeStruct(w.shape, _F32)] * 4,
        compiler_params=_cparams("parallel", "parallel"),
    )(w, g_layers[0], g_layers[1], m, v))


def _adamw(w, g, m, v, name):
    shape = w.shape
    w2, g2, m2, v2 = (a.reshape(-1, shape[-1]) for a in (w, g, m, v))
    R, C = w2.shape
    tr = next((t for t in (256, 128, 64, 32, 16) if R % t == 0), R)

    def kern(w_ref, g_ref, m_ref, v_ref, *out_refs):
        _adamw_update(g_ref[...].astype(_F32), w_ref, m_ref, v_ref, *out_refs)

    outs = pl.pallas_call(
        kern, name=name, grid=(R // tr,),
        in_specs=[_rows(tr, C)] * 4, out_specs=[_rows(tr, C)] * 4,
        out_shape=[jax.ShapeDtypeStruct((R, C), _F32)] * 4,
        compiler_params=_cparams("parallel"),
    )(w2, g2, m2, v2)
    return tuple(o.reshape(shape) for o in outs)


def _pack_small(gss, sq):
    parts = [gss[l][n].reshape(-1) for n in _SMALL for l in range(len(gss))] + [jnp.sum(sq).reshape(1)]
    flat = jnp.concatenate(parts)
    rows = -(-flat.shape[0] // (8 * _LANES)) * 8
    return jnp.pad(flat, (0, rows * _LANES - flat.shape[0])).reshape(rows, _LANES)


def _unpack_small(total, shapes):
    flat = total.reshape(-1)
    out, off = {}, 0
    for n in _SMALL:
        layers = []
        for _ in range(shapes[n][0]):
            size = 1
            for s in shapes[n][1:]:
                size *= s
            layers.append(flat[off:off + size].reshape(shapes[n][1:]))
            off += size
        out[n] = jnp.stack(layers)
    return out, flat[off]


_GATHER = {
    "band_fwd_0": [(0, "w_proj_a"), (0, "w_proj_b"), (0, "w_out"), (0, "w_ffn_out")],
    "stick_fwd_0": [(0, "w_ffn_in"), (1, "w_proj_a"), (1, "w_proj_b"), (1, "w_out")],
    "mix_fwd_0": [(1, "w_ffn_out")],
    "ffn_fwd_0": [(1, "w_in"), (1, "w_ffn_in")],
}
_SCATTER = {
    "band_bwd_1": [(1, "w_ffn_in"), (1, "w_ffn_out")],
    "stick_bwd_1": [(1, "w_proj_a"), (1, "w_proj_b"), (1, "w_out")],
    "band_bwd_0": [(1, "w_in"), (0, "w_ffn_in")],
    "stick_bwd_0": [(0, "w_ffn_out"), (0, "w_proj_a"), (0, "w_proj_b"), (0, "w_out")],
    "in_proj_bwd_0": [(0, "w_in")],
}
_SHARE = {"stick_bwd_1": "band_bwd_1", "band_bwd_0": "stick_bwd_1", "stick_bwd_0": "band_bwd_0", "grad_w_in_0": "stick_bwd_0"}


def _owner(key):
    del key
    return 1


def kernel(x, w_in, b_gate, rel_bias, w_proj_a, w_proj_b, w_out, ln1_g, ln1_b, w_ffn_in, w_ffn_out, ln2_g, ln2_b, loss_target, m_w_in, m_b_gate, m_rel_bias, m_w_proj_a, m_w_proj_b, m_w_out, m_ln1_g, m_ln1_b, m_w_ffn_in, m_w_ffn_out, m_ln2_g, m_ln2_b, v_w_in, v_b_gate, v_rel_bias, v_w_proj_a, v_w_proj_b, v_w_out, v_ln1_g, v_ln1_b, v_w_ffn_in, v_w_ffn_out, v_ln2_g, v_ln2_b):
    names = ("w_in", "b_gate", "rel_bias", "w_proj_a", "w_proj_b", "w_out", "ln1_g", "ln1_b", "w_ffn_in", "w_ffn_out", "ln2_g", "ln2_b")
    w = dict(zip(names, (w_in, b_gate, rel_bias, w_proj_a, w_proj_b, w_out, ln1_g, ln1_b, w_ffn_in, w_ffn_out, ln2_g, ln2_b)))
    m = dict(zip(names, (m_w_in, m_b_gate, m_rel_bias, m_w_proj_a, m_w_proj_b, m_w_out, m_ln1_g, m_ln1_b, m_w_ffn_in, m_w_ffn_out, m_ln2_g, m_ln2_b)))
    v = dict(zip(names, (v_w_in, v_b_gate, v_rel_bias, v_w_proj_a, v_w_proj_b, v_w_out, v_ln1_g, v_ln1_b, v_w_ffn_in, v_w_ffn_out, v_ln2_g, v_ln2_b)))
    T, D = x.shape[-2], x.shape[-1]
    assert w_in.shape[0] == 2, "the exchange schedule below is written for two layers"

    mine = [{n: w[n][l].astype(_MXU) for n in _DENSE} for l in range(2)]
    W = [dict(), dict()]
    gws = [dict(), dict()]
    slots, final = {}, {}

    def gather(keys):
        sizes = [mine[l][n].size for l, n in keys]
        passed, fractions = 0, []
        for s in sizes:
            passed += s
            fractions.append(0.15 + 0.6 * passed / sum(sizes))

        def done(outs):
            for (l, n), o in zip(keys, outs):
                W[l][n] = o
        return _gather_plan([mine[l][n] for l, n in keys], fractions), done

    def scatter(keys):
        comm = _scatter_plan([gws[l][n] for l, n in keys], [_owner(key) for key in keys])
        return comm, lambda outs: slots.update(zip(keys, outs))

    def share(keys):
        reduced = [_sum_slots(slots[key], f"sum_grad_{key[1]}_{key[0]}") for key in keys]
        comm = _share_plan(reduced, [_owner(key) for key in keys])
        return comm, lambda outs: final.update(zip(keys, outs))

    def both(first, second):
        (ca, da), (cb, db) = first, second
        na = len(ca.out_shapes)
        return _join(ca, cb), lambda outs: (da(outs[:na]), db(outs[na:]))

    plans = {key: functools.partial(gather, keys) for key, keys in _GATHER.items()}
    for key, keys in _SCATTER.items():
        plans[key] = functools.partial(scatter, keys)
    for key, scattered_under in _SHARE.items():
        handed = functools.partial(share, _SCATTER[scattered_under])
        carried = plans.get(key)
        plans[key] = handed if carried is None else (lambda carried=carried, handed=handed: both(carried(), handed()))
    small = {n: w[n] for n in _SMALL}
    sq, dx, _, gss = _local_step(x.reshape(T, D), loss_target.reshape(T, D), W, small,
                                  _Plans(plans, {0: mine[0]["w_in"]}), gws)

    comm, done = share(_SCATTER["in_proj_bwd_0"])
    done(_comm_only(comm, "share_last"))
    total = _all_reduce_small(_pack_small(gss, sq))
    small_grads, sq_all = _unpack_small(total, {n: w[n].shape for n in _SMALL})
    loss = 0.5 * sq_all / D

    grad, delta, new_m, new_v = {}, {}, {}, {}
    for n in names:
        if n in _DENSE:
            updated = _adamw_layers(w[n], [final[(l, n)] for l in range(2)], m[n], v[n], f"adamw_{n}")
        else:
            updated = _adamw(w[n], small_grads[n], m[n], v[n], f"adamw_{n}")
        grad[n], delta[n], new_m[n], new_v[n] = updated
    return (loss, dx.reshape(x.shape), *[grad[n] for n in names], *[delta[n] for n in names],
            *[new_m[n] for n in names], *[new_v[n] for n in names])
```

```python
import functools

import jax
import jax.numpy as jnp
from jax import lax
from jax.experimental import pallas as pl
from jax.experimental.pallas import tpu as pltpu

_MXU = jnp.bfloat16
_ACT = jnp.bfloat16
_F32 = jnp.float32

_HEAD = 64
_CHUNK = 64
_LANES = 128
_TQ = 128
_BAND_TILES = 5
_BIAS_TILES = 9
_REL_CLIP = 256
_LN_EPS = 1e-5
_MASKED = -1e30
_EXP_ZERO_BELOW = -87.34
_SB_WINDOW = 2
_SB_SUBTILES = 4
_BAND_SUBTILES = 8
_VMEM_LIMIT = 56 * 1024 * 1024
_GRAD_ACC_BYTES = 12 * 1024 * 1024

_LR, _B1, _B2, _EPS, _WD, _STEP = 0.001, 0.9, 0.999, 1e-08, 0.01, 10

_MESH = pl.DeviceIdType.MESH


def _dot(a, b):
    return jnp.dot(a, b, preferred_element_type=_F32)


def _dot_nt(a, b):
    return lax.dot_general(a, b, (((1,), (1,)), ((), ())), preferred_element_type=_F32)


def _dot_tn(a, b):
    return lax.dot_general(a, b, (((0,), (0,)), ((), ())), preferred_element_type=_F32)


def _cparams(*sem):
    return pltpu.CompilerParams(dimension_semantics=sem, vmem_limit_bytes=_VMEM_LIMIT)


def _rows(t, c):
    return pl.BlockSpec((t, c), lambda i: (i, 0))


def _whole(shape):
    return pl.BlockSpec(shape, lambda i: tuple(0 for _ in shape))


_ANY = pl.BlockSpec(memory_space=pl.ANY)


def _load_cols(w_hbm, w_vmem, sem):
    n = w_hbm.shape[-1]
    cps = [pltpu.make_async_copy(w_hbm.at[k], w_vmem.at[:, pl.ds(k * n, n)], sem.at[k]) for k in range(4)]
    for cp in cps:
        cp.start()
    for cp in cps:
        cp.wait()


def _load_rows(w_hbm, w_vmem, sem):
    r = w_hbm.shape[-2]
    cps = [pltpu.make_async_copy(w_hbm.at[k], w_vmem.at[pl.ds(k * r, r), :], sem.at[k]) for k in range(4)]
    for cp in cps:
        cp.start()
    for cp in cps:
        cp.wait()


def _ln_stats(u):
    mu = jnp.mean(u, axis=-1, keepdims=True)
    xc = u - mu
    var = jnp.mean(xc * xc, axis=-1, keepdims=True)
    rstd = lax.rsqrt(var + _LN_EPS)
    return xc * rstd, rstd


def _ln_bwd(u, dy, gamma):
    xhat, rstd = _ln_stats(u)
    dxh = dy * gamma
    m1 = jnp.mean(dxh, axis=-1, keepdims=True)
    m2 = jnp.mean(dxh * xhat, axis=-1, keepdims=True)
    du = rstd * (dxh - m1 - xhat * m2)
    return du, jnp.sum(dy * xhat, axis=0, keepdims=True), jnp.sum(dy, axis=0, keepdims=True), xhat


def _divisor_tile(n, cap):
    best = None
    for t in range(_LANES, min(n, cap) + 1, _LANES):
        if n % t == 0:
            best = t
    return best or n


class _Comm:
    def __init__(self, inputs, out_shapes, sems, run, aliases=None):
        self.inputs, self.out_shapes, self.sems, self.run = list(inputs), list(out_shapes), list(sems), run
        self.aliases = aliases or {}


def _call(kern, comm, *, name, grid, in_specs, out_specs, out_shape, scratch_shapes, args, semantics):
    in_specs, out_specs, out_shape, scratch_shapes = list(in_specs), list(out_specs), list(out_shape), list(scratch_shapes)
    if comm is None:
        outs = pl.pallas_call(kern, name=name, grid=grid, in_specs=in_specs, out_specs=out_specs, out_shape=out_shape,
                              scratch_shapes=scratch_shapes, compiler_params=_cparams(*semantics))(*args)
        return list(outs), []
    n_in, n_out, n_scr = len(in_specs), len(out_specs), len(scratch_shapes)
    ci, co = len(comm.inputs), len(comm.out_shapes)
    nsteps = functools.reduce(lambda a, b: a * b, grid, 1)

    def fused(*refs):
        a, b = n_in, n_in + ci
        c, d = b + n_out, b + n_out + co
        e = d + n_scr
        step = pl.program_id(0)
        for ax in range(1, len(grid)):
            step = step * grid[ax] + pl.program_id(ax)
        comm.run(step, nsteps, refs[a:b], refs[c:d], refs[e:])
        kern(*refs[:a], *refs[b:c], *refs[d:e])

    outs = pl.pallas_call(
        fused, name=name, grid=grid, in_specs=in_specs + [_ANY] * ci, out_specs=out_specs + [_ANY] * co,
        out_shape=out_shape + comm.out_shapes, scratch_shapes=scratch_shapes + comm.sems,
        input_output_aliases={n_in + i: n_out + o for i, o in comm.aliases.items()},
        compiler_params=_cparams(*("arbitrary" for _ in grid)))(*args, *comm.inputs)
    return list(outs[:n_out]), list(outs[n_out:])


def _comm_only(comm, name):
    def body(*refs):
        ci, co = len(comm.inputs), len(comm.out_shapes)
        comm.run(0, 1, refs[:ci], refs[ci:ci + co], refs[ci + co:])

    outs = pl.pallas_call(body, name=name, in_specs=[_ANY] * len(comm.inputs), out_specs=[_ANY] * len(comm.out_shapes),
                          out_shape=comm.out_shapes, scratch_shapes=comm.sems,
                          input_output_aliases=dict(comm.aliases))(*comm.inputs)
    return list(outs)


def _in_proj(x, w_in, layer, comm=None):
    T, D = x.shape
    N = 4 * w_in.shape[-1]
    tm = 512

    def kern(x_ref, w_hbm, h_ref, xb_ref, w_v, sem):
        @pl.when(pl.program_id(0) == 0)
        def _():
            _load_cols(w_hbm, w_v, sem)

        xb = x_ref[...].astype(_MXU)
        h_ref[...] = _dot(xb, w_v[...]).astype(h_ref.dtype)
        xb_ref[...] = xb.astype(xb_ref.dtype)

    return _call(
        kern, comm, name=f"in_proj_{layer}", grid=(T // tm,),
        in_specs=[_rows(tm, D), _ANY],
        out_specs=[_rows(tm, N), _rows(tm, D)],
        out_shape=[jax.ShapeDtypeStruct((T, N), _ACT), jax.ShapeDtypeStruct((T, D), _ACT)],
        scratch_shapes=[pltpu.VMEM((D, N), w_in.dtype), pltpu.SemaphoreType.DMA((4,))],
        args=(x, w_in), semantics=("arbitrary",))


def _in_proj_gathering(x, block, layer):
    T, D = x.shape
    n = block.shape[1]
    tm = min(T, 1024)
    nrows = T // tm
    pass_steps = [int(f * nrows) for f in (0.6, 1.0, 1.7)]
    px, py, _ = _place()
    order = jnp.stack([2 * px + py, 2 * (1 - px) + py, 2 * px + (1 - py), 2 * (1 - px) + (1 - py)]).astype(jnp.int32)

    def kern(order_ref, x_ref, blk_hbm, h_ref, xb_ref, w_hbm, w_v, send_sems, recv_sems, loc_sem, load_sem):
        del order_ref
        step = pl.program_id(0) * nrows + pl.program_id(1)
        x_, y_, c = _place()
        k = 2 * x_ + y_
        me, sibling = (x_, y_, c), (x_, y_, 1 - c)
        chips = [(1 - x_, y_), (x_, 1 - y_), (1 - x_, 1 - y_)]
        chip_k = [2 * cx + cy for cx, cy in chips]

        def ici(s, owner_k, to, src=None):
            dst = _half(w_hbm.at[owner_k], c)
            return _remote(dst if src is None else src, dst, send_sems.at[s], recv_sems.at[s], to)

        def passed(s, hc, to):
            blk = _half(w_hbm.at[chip_k[s]], hc)
            return _remote(blk, blk, send_sems.at[3 + s], recv_sems.at[3 + s], to)

        local = pltpu.make_async_copy(blk_hbm, w_hbm.at[k], loc_sem.at[0])

        def load(src):
            cp = pltpu.make_async_copy(src, w_v, load_sem.at[0])
            cp.start()
            cp.wait()

        @pl.when(step == 0)
        def _():
            for s, chip in enumerate(chips):
                ici(s, k, (*chip, c), src=_half(blk_hbm, c)).start()
            local.start()
            load(blk_hbm)

        for s in range(3):
            @pl.when(step == pass_steps[s])
            def _():
                ici(s, chip_k[s], me).wait_recv()
                passed(s, c, sibling).start()

            @pl.when(step == (s + 1) * nrows)
            def _():
                passed(s, 1 - c, me).wait_recv()
                load(w_hbm.at[chip_k[s]])

        xb = x_ref[...].astype(_MXU)
        h_ref[...] = _dot(xb, w_v[...]).astype(h_ref.dtype)

        @pl.when(pl.program_id(0) == 0)
        def _():
            xb_ref[...] = xb.astype(xb_ref.dtype)

        @pl.when(step == 4 * nrows - 1)
        def _():
            for s, chip in enumerate(chips):
                ici(s, k, (*chip, c), src=_half(blk_hbm, c)).wait_send()
                passed(s, c, sibling).wait_send()
            local.wait()

    assert all(pass_steps[s] <= (s + 1) * nrows for s in range(3))
    h, xb, w_in = pl.pallas_call(
        kern, name=f"in_proj_{layer}",
        grid_spec=pltpu.PrefetchScalarGridSpec(
            num_scalar_prefetch=1, grid=(4, nrows),
            in_specs=[pl.BlockSpec((tm, D), lambda j, i, o: (i, 0)), _ANY],
            out_specs=[pl.BlockSpec((tm, n), lambda j, i, o: (i, o[j])),
                       pl.BlockSpec((tm, D), lambda j, i, o: (jnp.where(j == 0, i, nrows - 1), 0)), _ANY],
            scratch_shapes=[pltpu.VMEM((D, n), block.dtype), pltpu.SemaphoreType.DMA((6,)),
                            pltpu.SemaphoreType.DMA((6,)), pltpu.SemaphoreType.DMA((1,)), pltpu.SemaphoreType.DMA((1,))]),
        out_shape=[jax.ShapeDtypeStruct((T, 4 * n), _ACT), jax.ShapeDtypeStruct((T, D), _ACT),
                   jax.ShapeDtypeStruct((4,) + block.shape, block.dtype)],
        compiler_params=_cparams("arbitrary", "arbitrary"))(order, x, block)
    return h, xb, w_in


def _gate_specs(h, tm, D):
    first = (h.shape[1] - 2 * D) // D
    assert first * D + 2 * D == h.shape[1]
    return [pl.BlockSpec((tm, D), lambda i: (i, first)), pl.BlockSpec((tm, D), lambda i: (i, first + 1))]


def _mix_fwd(oa, ob, h, x, wpa, wpb, wo, bg, gamma, beta, alpha, layer, comm=None):
    T, D = x.shape
    WA, WB = oa.shape[1], ob.shape[1]
    tm = 512

    def kern(oa_ref, ob_ref, hga_ref, hgb_ref, x_ref, bg_ref, g_ref, b_ref, wpa_h, wpb_h, wo_h,
             x1_ref, u1_ref, pre_ref, wpa_v, wpb_v, wo_v, sa, sb, so):
        @pl.when(pl.program_id(0) == 0)
        def _():
            _load_cols(wpa_h, wpa_v, sa)
            _load_cols(wpb_h, wpb_v, sb)
            _load_rows(wo_h, wo_v, so)

        ya = _dot(oa_ref[...].astype(_MXU), wpa_v[...])
        yb = _dot(ob_ref[...].astype(_MXU), wpb_v[...])
        bgv = bg_ref[...]
        ga = jax.nn.sigmoid(hga_ref[...].astype(_F32) + bgv[:, :D])
        gb = jax.nn.sigmoid(hgb_ref[...].astype(_F32) + bgv[:, D:])
        pre = ga * ya + gb * yb
        mix = _dot(pre.astype(_MXU), wo_v[...])
        u = alpha * x_ref[...] + mix
        xhat, _ = _ln_stats(u)
        x1_ref[...] = xhat * g_ref[...] + b_ref[...]
        u1_ref[...] = u
        pre_ref[...] = pre.astype(pre_ref.dtype)

    return _call(
        kern, comm, name=f"mix_fwd_{layer}", grid=(T // tm,),
        in_specs=[_rows(tm, WA), _rows(tm, WB), *_gate_specs(h, tm, D), _rows(tm, D),
                  _whole((1, 2 * D)), _whole((1, D)), _whole((1, D)), _ANY, _ANY, _ANY],
        out_specs=[_rows(tm, D)] * 3,
        out_shape=[jax.ShapeDtypeStruct((T, D), _F32), jax.ShapeDtypeStruct((T, D), _F32),
                   jax.ShapeDtypeStruct((T, D), _ACT)],
        scratch_shapes=[pltpu.VMEM((WA, D), wpa.dtype), pltpu.VMEM((WB, D), wpb.dtype), pltpu.VMEM((D, D), wo.dtype),
                        pltpu.SemaphoreType.DMA((4,)), pltpu.SemaphoreType.DMA((4,)), pltpu.SemaphoreType.DMA((4,))],
        args=(oa, ob, h, h, x, bg, gamma, beta, wpa, wpb, wo), semantics=("arbitrary",))


def _ffn_fwd(x1, wfi, wfo, gamma, beta, alpha, layer, comm=None):
    T, D = x1.shape
    F2 = 4 * wfi.shape[-1]
    F = F2 // 2
    tm = 512
    fc = F // 2

    def kern(x_ref, g_ref, b_ref, wi_h, wo_h, x2_ref, u2_ref, act_ref, gu_ref, xb_ref, wi_v, wo_v, si, so):
        @pl.when(pl.program_id(0) == 0)
        def _():
            _load_cols(wi_h, wi_v, si)
            _load_rows(wo_h, wo_v, so)

        x = x_ref[...]
        xb = x.astype(_MXU)
        xb_ref[...] = xb.astype(xb_ref.dtype)
        ffn = jnp.zeros((tm, D), _F32)
        for c in range(2):
            g = _dot(xb, wi_v[:, c * fc:(c + 1) * fc])
            u = _dot(xb, wi_v[:, F + c * fc:F + (c + 1) * fc])
            act = g * jax.nn.sigmoid(g) * u
            ab = act.astype(_MXU)
            ffn = ffn + _dot(ab, wo_v[c * fc:(c + 1) * fc, :])
            act_ref[:, c * fc:(c + 1) * fc] = ab.astype(act_ref.dtype)
            gu_ref[:, c * fc:(c + 1) * fc] = g.astype(gu_ref.dtype)
            gu_ref[:, F + c * fc:F + (c + 1) * fc] = u.astype(gu_ref.dtype)
        uu = alpha * x + ffn
        xhat, _ = _ln_stats(uu)
        x2_ref[...] = xhat * g_ref[...] + b_ref[...]
        u2_ref[...] = uu

    return _call(
        kern, comm, name=f"ffn_fwd_{layer}", grid=(T // tm,),
        in_specs=[_rows(tm, D), _whole((1, D)), _whole((1, D)), _ANY, _ANY],
        out_specs=[_rows(tm, D), _rows(tm, D), _rows(tm, F), _rows(tm, F2), _rows(tm, D)],
        out_shape=[jax.ShapeDtypeStruct((T, D), _F32), jax.ShapeDtypeStruct((T, D), _F32),
                   jax.ShapeDtypeStruct((T, F), _ACT), jax.ShapeDtypeStruct((T, F2), _ACT),
                   jax.ShapeDtypeStruct((T, D), _ACT)],
        scratch_shapes=[pltpu.VMEM((D, F2), wfi.dtype), pltpu.VMEM((F, D), wfo.dtype),
                        pltpu.SemaphoreType.DMA((4,)), pltpu.SemaphoreType.DMA((4,))],
        args=(x1, gamma, beta, wfi, wfo), semantics=("arbitrary",))


def _ffn_bwd(u2, dy_or_target, gu, gamma, beta, wfi, wfo, alpha, layer, last):
    T, D = u2.shape
    F2 = gu.shape[1]
    F = F2 // 2
    tm = 256
    fc = F // 2

    def kern(u_ref, dy_ref, gu_ref, g_ref, b_ref, wi_h, wo_h, dx_ref, dub_ref, dgu_ref, st_ref, wi_v, wo_v, si, so):
        @pl.when(pl.program_id(0) == 0)
        def _():
            _load_cols(wi_h, wi_v, si)
            _load_rows(wo_h, wo_v, so)
            st_ref[...] = jnp.zeros_like(st_ref)

        gam = g_ref[...]
        u = u_ref[...]
        if last:
            xhat0, _ = _ln_stats(u)
            err = xhat0 * gam + b_ref[...] - dy_ref[...]
            dy = err * (1.0 / D)
            st_ref[2:3, :] += jnp.sum(err * err, axis=0, keepdims=True)
        else:
            dy = dy_ref[...]
        du, dgam, dbet, _ = _ln_bwd(u, dy, gam)
        st_ref[0:1, :] += dgam
        st_ref[1:2, :] += dbet
        dub = du.astype(_MXU)
        dub_ref[...] = dub.astype(dub_ref.dtype)
        dx = alpha * du
        for c in range(2):
            dact = _dot_nt(dub, wo_v[c * fc:(c + 1) * fc, :])
            g = gu_ref[:, c * fc:(c + 1) * fc].astype(_F32)
            uu = gu_ref[:, F + c * fc:F + (c + 1) * fc].astype(_F32)
            sg = jax.nn.sigmoid(g)
            dg = (dact * uu * (sg * (1.0 + g * (1.0 - sg)))).astype(_MXU)
            dup = (dact * (g * sg)).astype(_MXU)
            dgu_ref[:, c * fc:(c + 1) * fc] = dg.astype(dgu_ref.dtype)
            dgu_ref[:, F + c * fc:F + (c + 1) * fc] = dup.astype(dgu_ref.dtype)
            dx = dx + _dot_nt(dg, wi_v[:, c * fc:(c + 1) * fc]) + _dot_nt(dup, wi_v[:, F + c * fc:F + (c + 1) * fc])
        dx_ref[...] = dx

    return pl.pallas_call(
        kern, name=f"ffn_bwd_{layer}", grid=(T // tm,),
        in_specs=[_rows(tm, D), _rows(tm, D), _rows(tm, F2), _whole((1, D)), _whole((1, D)), _ANY, _ANY],
        out_specs=[_rows(tm, D), _rows(tm, D), _rows(tm, F2), _whole((8, D))],
        out_shape=[jax.ShapeDtypeStruct((T, D), _F32), jax.ShapeDtypeStruct((T, D), _ACT),
                   jax.ShapeDtypeStruct((T, F2), _ACT), jax.ShapeDtypeStruct((8, D), _F32)],
        scratch_shapes=[pltpu.VMEM((D, F2), wfi.dtype), pltpu.VMEM((F, D), wfo.dtype),
                        pltpu.SemaphoreType.DMA((4,)), pltpu.SemaphoreType.DMA((4,))],
        compiler_params=_cparams("arbitrary"),
    )(u2, dy_or_target, gu, gamma, beta, wfi, wfo)


def _residual_nt(res, res_scale, pieces, w, name, comm=None):
    T, K = res.shape
    widths = [p.shape[1] for p in pieces]
    N = sum(widths)
    tm = 512

    def kern(r_ref, *refs):
        d_refs, (w_hbm, o_ref, w_v, sem) = refs[:len(pieces)], refs[len(pieces):]

        @pl.when(pl.program_id(0) == 0)
        def _():
            _load_cols(w_hbm, w_v, sem)

        acc = res_scale * r_ref[...]
        off = 0
        for d_ref, width in zip(d_refs, widths):
            acc = acc + _dot_nt(d_ref[...].astype(_MXU), w_v[:, off:off + width])
            off += width
        o_ref[...] = acc

    outs, extra = _call(
        kern, comm, name=name, grid=(T // tm,),
        in_specs=[_rows(tm, K)] + [_rows(tm, width) for width in widths] + [_ANY], out_specs=[_rows(tm, K)],
        out_shape=[jax.ShapeDtypeStruct((T, K), _F32)],
        scratch_shapes=[pltpu.VMEM((K, N), w.dtype), pltpu.SemaphoreType.DMA((4,))],
        args=(res, *pieces, w), semantics=("arbitrary",))
    return outs[0], extra


def _grad_w_pieces(a, pieces, name, comm=None):
    T, M = a.shape
    widths = [p.shape[1] for p in pieces]
    bw = functools.reduce(_gcd, widths + [512])
    first = [sum(widths[:p]) // bw for p in range(len(pieces))]
    count = [width // bw for width in widths]
    N = sum(widths)
    tk = 1024 if T % 1024 == 0 else 512
    nk = T // tk

    def kern(a_ref, *refs):
        b_refs, (o_ref, acc) = refs[:len(pieces)], refs[len(pieces):]
        j, k = pl.program_id(0), pl.program_id(1)

        @pl.when(k == 0)
        def _():
            acc[...] = jnp.zeros_like(acc)

        for b_ref, start, blocks in zip(b_refs, first, count):
            @pl.when(jnp.logical_and(j >= start, j < start + blocks))
            def _():
                acc[...] += _dot_tn(a_ref[...].astype(_MXU), b_ref[...].astype(_MXU))

        @pl.when(k == nk - 1)
        def _():
            o_ref[...] = acc[...].astype(o_ref.dtype)

    def piece_spec(start, blocks):
        def index(j, k):
            mine = jnp.logical_and(j >= start, j < start + blocks)
            return jnp.where(mine, k, 0), jnp.where(mine, j - start, 0)
        return pl.BlockSpec((tk, bw), index)

    outs, extra = _call(
        kern, comm, name=name, grid=(N // bw, nk),
        in_specs=[pl.BlockSpec((tk, M), lambda j, k: (k, 0))] + [piece_spec(s, c) for s, c in zip(first, count)],
        out_specs=[pl.BlockSpec((M, bw), lambda j, k: (0, j))],
        out_shape=[jax.ShapeDtypeStruct((M, N), _ACT)], scratch_shapes=[pltpu.VMEM((M, bw), _F32)],
        args=(a, *pieces), semantics=("parallel", "arbitrary"))
    return outs[0], extra


def _gcd(a, b):
    while b:
        a, b = b, a % b
    return a


def _mix_bwd(u1, dx1, oa, ob, h, wpa, wpb, wo, bg, gamma, layer):
    T, D = u1.shape
    WA, WB = wpa.shape[-2], wpb.shape[-2]
    tm = 512

    def kern(u_ref, dx_ref, oa_ref, ob_ref, hga_ref, hgb_ref, bg_ref, g_ref, wpa_h, wpb_h, wo_h,
             du_ref, dub_ref, dya_ref, dyb_ref, dhg_ref, doa_ref, dob_ref, st_ref,
             wpa_v, wpb_v, wo_v, sa, sb, so):
        @pl.when(pl.program_id(0) == 0)
        def _():
            _load_cols(wpa_h, wpa_v, sa)
            _load_cols(wpb_h, wpb_v, sb)
            _load_rows(wo_h, wo_v, so)
            st_ref[...] = jnp.zeros_like(st_ref)

        du, dgam, dbet, _ = _ln_bwd(u_ref[...], dx_ref[...], g_ref[...])
        st_ref[1:2, :D] += dgam
        st_ref[1:2, D:] += dbet
        du_ref[...] = du
        dub = du.astype(_MXU)
        dub_ref[...] = dub.astype(dub_ref.dtype)
        dpre = _dot_nt(dub, wo_v[...])
        bgv = bg_ref[...]
        ga = jax.nn.sigmoid(hga_ref[...].astype(_F32) + bgv[:, :D])
        gb = jax.nn.sigmoid(hgb_ref[...].astype(_F32) + bgv[:, D:])
        dya = (dpre * ga).astype(_MXU)
        dyb = (dpre * gb).astype(_MXU)
        dsa = dpre * _dot(oa_ref[...].astype(_MXU), wpa_v[...]) * (ga * (1.0 - ga))
        dsb = dpre * _dot(ob_ref[...].astype(_MXU), wpb_v[...]) * (gb * (1.0 - gb))
        st_ref[0:1, :D] += jnp.sum(dsa, axis=0, keepdims=True)
        st_ref[0:1, D:] += jnp.sum(dsb, axis=0, keepdims=True)
        dya_ref[...] = dya.astype(dya_ref.dtype)
        dyb_ref[...] = dyb.astype(dyb_ref.dtype)
        dhg_ref[:, :D] = dsa.astype(dhg_ref.dtype)
        dhg_ref[:, D:] = dsb.astype(dhg_ref.dtype)
        doa_ref[...] = _dot_nt(dya, wpa_v[...]).astype(doa_ref.dtype)
        dob_ref[...] = _dot_nt(dyb, wpb_v[...]).astype(dob_ref.dtype)

    return pl.pallas_call(
        kern, name=f"mix_bwd_{layer}", grid=(T // tm,),
        in_specs=[_rows(tm, D), _rows(tm, D), _rows(tm, WA), _rows(tm, WB), *_gate_specs(h, tm, D), _whole((1, 2 * D)),
                  _whole((1, D)), _ANY, _ANY, _ANY],
        out_specs=[_rows(tm, D)] * 4 + [_rows(tm, 2 * D), _rows(tm, WA), _rows(tm, WB), _whole((8, 2 * D))],
        out_shape=[jax.ShapeDtypeStruct((T, D), _F32)] + [jax.ShapeDtypeStruct((T, D), _ACT)] * 3
        + [jax.ShapeDtypeStruct((T, 2 * D), _ACT), jax.ShapeDtypeStruct((T, WA), _ACT),
           jax.ShapeDtypeStruct((T, WB), _ACT), jax.ShapeDtypeStruct((8, 2 * D), _F32)],
        scratch_shapes=[pltpu.VMEM((WA, D), wpa.dtype), pltpu.VMEM((WB, D), wpb.dtype), pltpu.VMEM((D, D), wo.dtype),
                        pltpu.SemaphoreType.DMA((4,)), pltpu.SemaphoreType.DMA((4,)), pltpu.SemaphoreType.DMA((4,))],
        compiler_params=_cparams("arbitrary"),
    )(u1, dx1, oa, ob, h, h, bg, gamma, wpa, wpb, wo)


def _grad_w(a, b, *, col_shards, name, comm=None):
    T, M = a.shape
    N = b.shape[1]
    tk = 1024 if T % 1024 == 0 else 512
    n = N // 4 if col_shards else N
    whole = M * N * 4 <= _GRAD_ACC_BYTES
    tn = N if whole else (n if col_shards else _divisor_tile(N, _GRAD_ACC_BYTES // (4 * M)))
    nk = T // tk

    def kern(a_ref, b_ref, o_ref, acc):
        k = pl.program_id(1)

        @pl.when(k == 0)
        def _():
            acc[...] = jnp.zeros_like(acc)

        acc[...] += _dot_tn(a_ref[...].astype(_MXU), b_ref[...].astype(_MXU))

        @pl.when(k == nk - 1)
        def _():
            if col_shards and whole:
                for s in range(4):
                    o_ref[s] = acc[:, s * n:(s + 1) * n].astype(o_ref.dtype)
            else:
                o_ref[...] = acc[...].astype(o_ref.dtype)

    if col_shards:
        out_spec = (pl.BlockSpec((4, M, n), lambda j, k: (0, 0, 0)) if whole
                    else pl.BlockSpec((None, M, n), lambda j, k: (j, 0, 0)))
        out_shape = jax.ShapeDtypeStruct((4, M, n), _ACT)
    else:
        out_spec = pl.BlockSpec((M, tn), lambda j, k: (0, j))
        out_shape = jax.ShapeDtypeStruct((M, N), _ACT)
    outs, extra = _call(
        kern, comm, name=name, grid=(N // tn, nk),
        in_specs=[pl.BlockSpec((tk, M), lambda j, k: (k, 0)), pl.BlockSpec((tk, tn), lambda j, k: (k, j))],
        out_specs=[out_spec], out_shape=[out_shape], scratch_shapes=[pltpu.VMEM((M, tn), _F32)],
        args=(a, b), semantics=("parallel", "arbitrary"))
    return outs[0], extra


def _bias_tiles(rel):
    H = rel.shape[0]
    span = _TQ * _BAND_TILES - 1
    edge = span - _REL_CLIP
    gvec = jnp.concatenate([jnp.broadcast_to(rel[:, :1], (H, edge)), rel, jnp.broadcast_to(rel[:, -1:], (H, edge))], axis=1)
    width = _BIAS_TILES * _TQ
    period = width + _TQ
    tiled = jnp.broadcast_to(jnp.pad(gvec[:, ::-1], ((0, 0), (0, 1)))[:, None, :], (H, _TQ, period))
    rows = tiled.reshape(H, _TQ * period)[:, :_TQ * (period - 1)].reshape(H, _TQ, period - 1)[:, :, _TQ - 1:]
    r = jnp.arange(_TQ)[:, None]
    u = jnp.arange(width)[None, :]
    d = 4 * _TQ + r - u
    rm = r % _CHUNK
    valid = (d >= rm - (_CHUNK - 1)) & (d <= rm + 8 * _CHUNK)
    tiles = jnp.where(valid[None], rows, _MASKED)
    return tiles.reshape(H // 2, 2 * _TQ, _BIAS_TILES, _TQ).transpose(0, 2, 1, 3)


def _fold_bias_grad(db):
    H = 2 * db.shape[0]
    width = _BIAS_TILES * _TQ
    period = width + _TQ
    x = jnp.pad(db.transpose(0, 2, 1, 3).reshape(H, _TQ, width), ((0, 0), (0, 0), (_TQ - 1, 0)))
    skew = jnp.pad(x.reshape(H, _TQ * (period - 1)), ((0, 0), (0, _TQ))).reshape(H, _TQ, period)
    dg = skew.sum(axis=1)[:, :period - 1][:, ::-1]
    span = _TQ * _BAND_TILES - 1
    edge = span - _REL_CLIP
    mid = dg[:, edge:edge + 2 * _REL_CLIP + 1]
    lo = dg[:, :edge].sum(axis=1)
    hi = dg[:, edge + 2 * _REL_CLIP + 1:].sum(axis=1)
    return mid.at[:, 0].add(lo).at[:, -1].add(hi)


def _band_window(i):
    j0 = jnp.maximum(i - (_BAND_TILES - 1), 0)
    return j0, (_BAND_TILES - 1) - (i - j0)


def _head_masks():
    lane = lax.broadcasted_iota(jnp.int32, (1, _LANES), 1)
    return [(lane // _HEAD) == hh for hh in range(2)]


def _stack_heads(x, masks):
    return jnp.concatenate([jnp.where(m, x, jnp.zeros_like(x)) for m in masks], axis=0)


def _unstack_heads(y, masks):
    return jnp.where(masks[0], y[:_TQ], y[_TQ:])


def _scaled(q):
    return q * jnp.asarray(_HEAD ** -0.5, q.dtype)


def _band_probs(q2, k_ref, b_ref, j0, boff):
    s = []
    for j in range(_BAND_TILES):
        kj = k_ref[pl.ds(pl.multiple_of((j0 + j) * _TQ, _TQ), _TQ), :]
        s.append(_dot_nt(q2, kj) + b_ref[boff + j])
    m = jnp.max(functools.reduce(jnp.maximum, s), axis=-1, keepdims=True)
    p = [jnp.exp(x - m) for x in s]
    l = jnp.sum(functools.reduce(lambda a, b: a + b, p), axis=-1, keepdims=True)
    return p, 1.0 / l


def _qkv_specs(T, cb, npair, tq=_TQ):
    return [pl.BlockSpec((tq, _LANES), lambda h, i: (i, cb + h)),
            pl.BlockSpec((T, _LANES), lambda h, i: (0, cb + npair + h)),
            pl.BlockSpec((T, _LANES), lambda h, i: (0, cb + 2 * npair + h))]


def _attn_a_fwd(hq, bias, col0, width, layer, comm=None):
    T = hq.shape[0]
    npair = width // _LANES
    nsub = _BAND_SUBTILES
    tq = nsub * _TQ

    def kern(q_ref, k_ref, v_ref, b_ref, o_ref):
        masks = _head_masks()
        q = _scaled(q_ref[...])
        for s in range(nsub):
            part = slice(s * _TQ, (s + 1) * _TQ)
            j0, boff = _band_window(nsub * pl.program_id(1) + s)
            p, inv = _band_probs(_stack_heads(q[part], masks), k_ref, b_ref, j0, boff)
            o = jnp.zeros((2 * _TQ, _LANES), _F32)
            for j in range(_BAND_TILES):
                vj = v_ref[pl.ds(pl.multiple_of((j0 + j) * _TQ, _TQ), _TQ), :]
                o = o + _dot(p[j].astype(_MXU), vj)
            o_ref[part, :] = _unstack_heads(o * inv, masks).astype(o_ref.dtype)

    outs, extra = _call(
        kern, comm, name=f"band_attn_fwd_{layer}", grid=(npair, T // tq),
        in_specs=_qkv_specs(T, col0 // _LANES, npair, tq)
        + [pl.BlockSpec((None, _BIAS_TILES, 2 * _TQ, _TQ), lambda h, i: (h, 0, 0, 0))],
        out_specs=[pl.BlockSpec((tq, _LANES), lambda h, i: (i, h))],
        out_shape=[jax.ShapeDtypeStruct((T, width), _ACT)], scratch_shapes=[],
        args=(hq, hq, hq, bias), semantics=("arbitrary", "arbitrary"))
    return outs[0], extra


def _attn_a_bwd(hq, bias, do, col0, width, layer, comm=None):
    T = hq.shape[0]
    npair = width // _LANES
    nsub = _BAND_SUBTILES
    tq = nsub * _TQ
    nq = T // tq
    scale = _HEAD ** -0.5

    def kern(q_ref, k_ref, v_ref, b_ref, do_ref, dq_ref, dk_ref, dv_ref, db_ref, dk_acc, dv_acc):
        i = pl.program_id(1)

        @pl.when(i == 0)
        def _():
            dk_acc[...] = jnp.zeros_like(dk_acc)
            dv_acc[...] = jnp.zeros_like(dv_acc)
            db_ref[...] = jnp.zeros_like(db_ref)

        masks = _head_masks()
        q = _scaled(q_ref[...])
        do_t = do_ref[...]
        for s in range(nsub):
            part = slice(s * _TQ, (s + 1) * _TQ)
            j0, boff = _band_window(nsub * i + s)
            q2 = _stack_heads(q[part], masks)
            do2 = _stack_heads(do_t[part], masks).astype(_MXU)
            p, inv = _band_probs(q2, k_ref, b_ref, j0, boff)
            rows = [pl.ds(pl.multiple_of((j0 + j) * _TQ, _TQ), _TQ) for j in range(_BAND_TILES)]
            p = [x * inv for x in p]
            dp = [_dot_nt(do2, v_ref[rows[j], :]) for j in range(_BAND_TILES)]
            delta = jnp.sum(functools.reduce(lambda a, b: a + b, [p[j] * dp[j] for j in range(_BAND_TILES)]),
                            axis=-1, keepdims=True)
            dq = jnp.zeros((2 * _TQ, _LANES), _F32)
            for j in range(_BAND_TILES):
                ds = p[j] * (dp[j] - delta)
                db_ref[boff + j] += ds
                dsb = ds.astype(_MXU)
                dq = dq + _dot(dsb, k_ref[rows[j], :])
                dk_acc[rows[j], :] += _dot_tn(dsb, q2)
                dv_acc[rows[j], :] += _dot_tn(p[j].astype(_MXU), do2)
            dq_ref[part, :] = (_unstack_heads(dq, masks) * scale).astype(dq_ref.dtype)

        @pl.when(i == nq - 1)
        def _():
            dk_ref[...] = dk_acc[...].astype(dk_ref.dtype)
            dv_ref[...] = dv_acc[...].astype(dv_ref.dtype)

    strip = pl.BlockSpec((None, _BIAS_TILES, 2 * _TQ, _TQ), lambda h, i: (h, 0, 0, 0))
    tile = pl.BlockSpec((tq, _LANES), lambda h, i: (i, h))
    column = pl.BlockSpec((T, _LANES), lambda h, i: (0, h))
    outs, extra = _call(
        kern, comm, name=f"band_attn_bwd_{layer}", grid=(npair, nq),
        in_specs=_qkv_specs(T, col0 // _LANES, npair, tq) + [strip, tile],
        out_specs=[tile, column, column, strip],
        out_shape=[jax.ShapeDtypeStruct((T, width), _ACT)] * 3
        + [jax.ShapeDtypeStruct((npair, _BIAS_TILES, 2 * _TQ, _TQ), _F32)],
        scratch_shapes=[pltpu.VMEM((T, _LANES), _F32), pltpu.VMEM((T, _LANES), _F32)],
        args=(hq, hq, hq, bias, do), semantics=("arbitrary", "arbitrary"))
    return outs, extra


def _suffix_matrix():
    r = lax.broadcasted_iota(jnp.int32, (_TQ, _TQ), 0)
    c = lax.broadcasted_iota(jnp.int32, (_TQ, _TQ), 1)
    r2 = lax.broadcasted_iota(jnp.int32, (2 * _TQ, _TQ), 0)
    c2 = lax.broadcasted_iota(jnp.int32, (2 * _TQ, _TQ), 1)
    return (r > c).astype(_MXU), c2 - (r2 & (_TQ - 1))


def _suffix_sums(xs, tri):
    n, k = xs[0].shape[0], len(xs)
    his = [x.astype(_MXU) for x in xs]
    los = [(x - h.astype(_F32)).astype(_MXU) for x, h in zip(xs, his)]
    y = _dot(jnp.concatenate(his + los, axis=0), tri)
    return [y[j * n:(j + 1) * n] + y[(k + j) * n:(k + j + 1) * n] for j in range(k)]


def _stick_tiles(tiles, rel, carry_l, tri):
    zs = [_dot_nt(qs, kj) for qs, kj, _, _ in tiles]
    Ls, masks = [], []
    for z, (_, _, jj, _) in zip(zs, tiles):
        nsp = -(jnp.maximum(z, 0.0) + jnp.log(1.0 + jnp.exp(-jnp.abs(z))))
        if isinstance(jj, int):
            mask = (rel < 0) if jj == 0 else None
        else:
            mask = rel < jnp.where(jj == 0, 0, _TQ)
        Ls.append(nsp if mask is None else jnp.where(mask, nsp, 0.0))
        masks.append(mask)
    carry_l = list(carry_l)
    ws = []
    for z, L, suffix, mask, (_, _, _, sub) in zip(zs, Ls, _suffix_sums(Ls, tri), masks, tiles):
        w = jnp.exp(z + L + suffix + carry_l[sub])
        ws.append(w if mask is None else jnp.where(mask, w, 0.0))
        carry_l[sub] = carry_l[sub] + jnp.sum(L, axis=-1, keepdims=True)
    return zs, Ls, ws, masks, carry_l


def _sweep(i, step, zero):
    nsub = _SB_SUBTILES

    def window():
        tiles = [(s, jj) for jj in range(_SB_WINDOW) for s in range(nsub)]
        return tuple((jnp.int32(_SB_WINDOW),) + c for c in step(tiles, [zero] * nsub))

    start = lax.cond(i >= -(-(_SB_WINDOW - 1) // nsub), window, lambda: tuple((jnp.int32(0),) + zero for _ in range(nsub)))
    outs = []
    for s in range(nsub):
        def done(c, s=s):
            return jnp.logical_or(c[0] > nsub * i + s, jnp.max(c[1]) < _EXP_ZERO_BELOW)

        def more(c, s=s):
            carries = [None] * nsub
            carries[s] = c[1:]
            return (c[0] + 1,) + step([(s, c[0])], carries)[s]

        outs.append(lax.while_loop(lambda c, done=done: jnp.logical_not(done(c)), more, start[s]))
    return outs


def _sb_fwd(hq, col0, width, layer, comm=None):
    T = hq.shape[0]
    npair = width // _LANES
    nsub = _SB_SUBTILES
    tq = nsub * _TQ

    def kern(q_ref, k_ref, v_ref, o_ref):
        i = pl.program_id(1)
        masks = _head_masks()
        tri, rel = _suffix_matrix()
        q = _scaled(q_ref[...])
        q2 = [_stack_heads(q[s * _TQ:(s + 1) * _TQ], masks) for s in range(nsub)]

        def step(tiles, carries):
            rows = [pl.ds(pl.multiple_of((nsub * i + s - jj) * _TQ, _TQ), _TQ) for s, jj in tiles]
            cls = [None if c is None else c[0] for c in carries]
            accs = [None if c is None else c[1] for c in carries]
            _, _, ws, _, cls = _stick_tiles([(q2[s], k_ref[r, :], jj, s) for (s, jj), r in zip(tiles, rows)], rel, cls, tri)
            for w, r, (s, _) in zip(ws, rows, tiles):
                accs[s] = accs[s] + _dot(w.astype(_MXU), v_ref[r, :])
            return [None if c is None else (cls[s], accs[s]) for s, c in enumerate(carries)]

        outs = _sweep(i, step, (jnp.zeros((2 * _TQ, 1), _F32), jnp.zeros((2 * _TQ, _LANES), _F32)))
        for s in range(nsub):
            o_ref[s * _TQ:(s + 1) * _TQ, :] = _unstack_heads(outs[s][2], masks)

    outs, extra = _call(
        kern, comm, name=f"stick_attn_fwd_{layer}", grid=(npair, T // tq),
        in_specs=_qkv_specs(T, col0 // _LANES, npair, tq),
        out_specs=[pl.BlockSpec((tq, _LANES), lambda h, i: (i, h))],
        out_shape=[jax.ShapeDtypeStruct((T, width), _F32)], scratch_shapes=[],
        args=(hq, hq, hq), semantics=("arbitrary", "arbitrary"))
    return outs[0], extra


def _sb_bwd(hq, o, do, col0, width, layer, comm=None):
    T = hq.shape[0]
    npair = width // _LANES
    nsub = _SB_SUBTILES
    tq = nsub * _TQ
    nq = T // tq
    scale = _HEAD ** -0.5

    def kern(q_ref, k_ref, v_ref, o_ref, do_ref, dq_ref, dk_ref, dv_ref, dk_acc, dv_acc):
        i = pl.program_id(1)

        @pl.when(i == 0)
        def _():
            dk_acc[...] = jnp.zeros_like(dk_acc)
            dv_acc[...] = jnp.zeros_like(dv_acc)

        masks = _head_masks()
        tri, rel = _suffix_matrix()
        q = _scaled(q_ref[...])
        do_t = do_ref[...]
        prod = do_t.astype(_F32) * o_ref[...]
        part = [slice(s * _TQ, (s + 1) * _TQ) for s in range(nsub)]
        q2 = [_stack_heads(q[p], masks) for p in part]
        do2 = [_stack_heads(do_t[p], masks).astype(_MXU) for p in part]
        dsum = [jnp.sum(_stack_heads(prod[p], masks), axis=-1, keepdims=True) for p in part]

        def step(tiles, carries):
            rows = [pl.ds(pl.multiple_of((nsub * i + s - jj) * _TQ, _TQ), _TQ) for s, jj in tiles]
            kjs = [k_ref[r, :] for r in rows]
            cls, cgs, dqs = ([None if c is None else c[n] for c in carries] for n in range(3))
            zs, Ls, ws, tile_masks, cls = _stick_tiles([(q2[s], kj, jj, s) for (s, jj), kj in zip(tiles, kjs)], rel, cls, tri)
            wbs = [w.astype(_MXU) for w in ws]
            gs = [wb.astype(_F32) * _dot_nt(do2[s], v_ref[r, :]) for wb, r, (s, _) in zip(wbs, rows, tiles)]
            for z, L, g, later, mask, wb, kj, r, (s, _) in zip(zs, Ls, gs, _suffix_sums(gs, tri), tile_masks, wbs, kjs,
                                                               rows, tiles):
                dz = g - jnp.exp(z + L) * (dsum[s] - (later + cgs[s]))
                if mask is not None:
                    dz = jnp.where(mask, dz, 0.0)
                dzb = dz.astype(_MXU)
                dk_acc[r, :] += _dot_tn(dzb, q2[s])
                dv_acc[r, :] += _dot_tn(wb, do2[s])
                dqs[s] = dqs[s] + _dot(dzb, kj)
                cgs[s] = cgs[s] + jnp.sum(g, axis=-1, keepdims=True)
            return [None if c is None else (cls[s], cgs[s], dqs[s]) for s, c in enumerate(carries)]

        zc = jnp.zeros((2 * _TQ, 1), _F32)
        outs = _sweep(i, step, (zc, zc, jnp.zeros((2 * _TQ, _LANES), _F32)))
        for s in range(nsub):
            dq_ref[part[s], :] = (_unstack_heads(outs[s][3], masks) * scale).astype(dq_ref.dtype)

        @pl.when(i == nq - 1)
        def _():
            dk_ref[...] = dk_acc[...].astype(dk_ref.dtype)
            dv_ref[...] = dv_acc[...].astype(dv_ref.dtype)

    tile_spec = pl.BlockSpec((tq, _LANES), lambda h, i: (i, h))
    column = pl.BlockSpec((T, _LANES), lambda h, i: (0, h))
    outs, extra = _call(
        kern, comm, name=f"stick_attn_bwd_{layer}", grid=(npair, nq),
        in_specs=_qkv_specs(T, col0 // _LANES, npair, tq) + [tile_spec, tile_spec],
        out_specs=[tile_spec, column, column],
        out_shape=[jax.ShapeDtypeStruct((T, width), _ACT)] * 3,
        scratch_shapes=[pltpu.VMEM((T, _LANES), _F32), pltpu.VMEM((T, _LANES), _F32)],
        args=(hq, hq, hq, o, do), semantics=("arbitrary", "arbitrary"))
    return outs, extra


_DENSE = ("w_in", "w_proj_a", "w_proj_b", "w_out", "w_ffn_in", "w_ffn_out")
_COL_SHARDED = {"w_in": True, "w_proj_a": True, "w_proj_b": True, "w_out": False, "w_ffn_in": True, "w_ffn_out": False}
_SMALL = ("b_gate", "rel_bias", "ln1_g", "ln1_b", "ln2_g", "ln2_b")


class _Plans:
    def __init__(self, plans=None, own_w_in=None):
        self.plans = plans or {}
        self.own_w_in = own_w_in or {}

    def start(self, key):
        if key not in self.plans:
            return None, None
        return self.plans[key]()

    @staticmethod
    def finish(done, extra):
        if done is not None:
            done(extra)


def _layer_fwd(x, W, small, l, alpha, plans):
    WA = small["rel_bias"].shape[1] * _HEAD
    row = lambda v: v[l].reshape(1, -1)
    if l in plans.own_w_in:
        h, xb, W["w_in"] = _in_proj_gathering(x, plans.own_w_in[l], l)
    else:
        comm, done = plans.start(f"in_proj_{l}")
        (h, xb), extra = _in_proj(x, W["w_in"], l, comm)
        plans.finish(done, extra)
    WB = (h.shape[1] - 2 * x.shape[1] - 3 * WA) // 3
    bias = _bias_tiles(small["rel_bias"][l])
    comm, done = plans.start(f"band_fwd_{l}")
    oa, extra = _attn_a_fwd(h, bias, 0, WA, l, comm)
    plans.finish(done, extra)
    comm, done = plans.start(f"stick_fwd_{l}")
    ob, extra = _sb_fwd(h, 3 * WA, WB, l, comm)
    plans.finish(done, extra)
    comm, done = plans.start(f"mix_fwd_{l}")
    (x1, u1, pre), extra = _mix_fwd(oa, ob, h, x, W["w_proj_a"], W["w_proj_b"], W["w_out"], row(small["b_gate"]),
                                            row(small["ln1_g"]), row(small["ln1_b"]), alpha, l, comm)
    plans.finish(done, extra)
    comm, done = plans.start(f"ffn_fwd_{l}")
    (x2, u2, act, gu, x1b), extra = _ffn_fwd(x1, W["w_ffn_in"], W["w_ffn_out"], row(small["ln2_g"]),
                                             row(small["ln2_b"]), alpha, l, comm)
    plans.finish(done, extra)
    return x2, dict(xb=xb, h=h, bias=bias, oa=oa, ob=ob, x1b=x1b, u1=u1, pre=pre, u2=u2, act=act, gu=gu)


def _layer_bwd(dy_or_target, S, W, small, l, last, alpha, plans, gw):
    D = S["xb"].shape[1]
    WA, WB = S["oa"].shape[1], S["ob"].shape[1]
    row = lambda v: v[l].reshape(1, -1)

    def blocks(g, n):
        return g if _COL_SHARDED[n] else g.reshape(4, g.shape[0] // 4, g.shape[1])

    dx1, du2b, dgu, st2 = _ffn_bwd(S["u2"], dy_or_target, S["gu"], row(small["ln2_g"]), row(small["ln2_b"]),
                                   W["w_ffn_in"], W["w_ffn_out"], alpha, l, last)
    gw["w_ffn_in"] = blocks(_grad_w(S["x1b"], dgu, col_shards=True, name=f"grad_w_ffn_in_{l}")[0], "w_ffn_in")
    gw["w_ffn_out"] = blocks(_grad_w(S["act"], du2b, col_shards=False, name=f"grad_w_ffn_out_{l}")[0], "w_ffn_out")
    du1, du1b, dya, dyb, dhg, doa, dob, st1 = _mix_bwd(S["u1"], dx1, S["oa"], S["ob"], S["h"], W["w_proj_a"],
                                                       W["w_proj_b"], W["w_out"], row(small["b_gate"]),
                                                       row(small["ln1_g"]), l)
    gw["w_out"] = blocks(_grad_w(S["pre"], du1b, col_shards=False, name=f"grad_w_out_{l}")[0], "w_out")
    gw["w_proj_a"] = blocks(_grad_w(S["oa"], dya, col_shards=True, name=f"grad_w_proj_a_{l}")[0], "w_proj_a")
    gw["w_proj_b"] = blocks(_grad_w(S["ob"], dyb, col_shards=True, name=f"grad_w_proj_b_{l}")[0], "w_proj_b")
    comm, done = plans.start(f"band_bwd_{l}")
    (dqa, dka, dva, dbias), extra = _attn_a_bwd(S["h"], S["bias"], doa, 0, WA, l, comm)
    plans.finish(done, extra)
    comm, done = plans.start(f"stick_bwd_{l}")
    (dqb, dkb, dvb), extra = _sb_bwd(S["h"], S["ob"], dob, 3 * WA, WB, l, comm)
    plans.finish(done, extra)
    dh = [dqa, dka, dva, dqb, dkb, dvb, dhg]
    comm, done = plans.start(f"grad_w_in_{l}")
    gw["w_in"], extra = _grad_w_pieces(S["xb"], dh, f"grad_w_in_{l}", comm)
    plans.finish(done, extra)
    comm, done = plans.start(f"in_proj_bwd_{l}")
    dx, extra = _residual_nt(du1, alpha, dh, W["w_in"], f"in_proj_bwd_{l}", comm)
    plans.finish(done, extra)
    gs = dict(b_gate=st1[0], rel_bias=_fold_bias_grad(dbias), ln1_g=st1[1, :D], ln1_b=st1[1, D:],
              ln2_g=st2[0], ln2_b=st2[1])
    return dx, gs, st2[2]


def _local_step(x, target, W, small, plans=None, gws=None):
    depth = len(W)
    alpha = float((2 * depth) ** 0.25)
    plans = plans or _Plans()
    gws = gws if gws is not None else [dict() for _ in range(depth)]
    saved = []
    h = x
    for l in range(depth):
        h, S = _layer_fwd(h, W[l], small, l, alpha, plans)
        saved.append(S)
    gss = [None] * depth
    d = target
    sq = None
    for l in reversed(range(depth)):
        d, gss[l], sq_l = _layer_bwd(d, saved[l], W[l], small, l, l == depth - 1, alpha, plans, gws[l])
        if l == depth - 1:
            sq = sq_l
    return sq, d, gws, gss


def _place():
    return lax.axis_index("x"), lax.axis_index("y"), lax.axis_index("c")


def _remote(src, dst, send_sem, recv_sem, to):
    return pltpu.make_async_remote_copy(src_ref=src, dst_ref=dst, send_sem=send_sem, recv_sem=recv_sem,
                                        device_id=to, device_id_type=_MESH)


def _half(ref, hc):
    kh = ref.shape[0] // 2
    return ref.at[pl.ds(pl.multiple_of(hc * kh, 16), kh), :]


def _gather_plan(blocks, fractions):
    nt = len(blocks)

    def run(step, nsteps, ins, outs, sems):
        send_sems, recv_sems, loc_sems = sems
        x, y, c = _place()
        k = 2 * x + y
        me, sibling = (x, y, c), (x, y, 1 - c)
        chips = [(1 - x, y), (x, 1 - y), (1 - x, 1 - y)]
        chip_k = [2 * cx + cy for cx, cy in chips]

        def ici(t, s, owner_k, to, src=None):
            dst = _half(outs[t].at[owner_k], c)
            return _remote(dst if src is None else src, dst, send_sems.at[t, s], recv_sems.at[t, s], to)

        def passed(t, s, hc, to):
            blk = _half(outs[t].at[chip_k[s]], hc)
            return _remote(blk, blk, send_sems.at[t, 3 + s], recv_sems.at[t, 3 + s], to)

        def local(t):
            return pltpu.make_async_copy(ins[t], outs[t].at[k], loc_sems.at[t])

        @pl.when(step == 0)
        def _():
            for t in range(nt):
                local(t).start()
                for s, chip in enumerate(chips):
                    ici(t, s, k, (*chip, c), src=_half(ins[t], c)).start()

        for t in range(nt):
            @pl.when(step == min(nsteps - 1, int(fractions[t] * nsteps)))
            def _():
                for s in range(3):
                    ici(t, s, chip_k[s], me).wait_recv()
                    passed(t, s, c, sibling).start()

        @pl.when(step == nsteps - 1)
        def _():
            for t in range(nt):
                for s, chip in enumerate(chips):
                    passed(t, s, 1 - c, me).wait_recv()
            for t in range(nt):
                for s, chip in enumerate(chips):
                    ici(t, s, k, (*chip, c), src=_half(ins[t], c)).wait_send()
                    passed(t, s, c, sibling).wait_send()
                local(t).wait()

    return _Comm(blocks, [jax.ShapeDtypeStruct((4,) + b.shape, b.dtype) for b in blocks],
                 [pltpu.SemaphoreType.DMA((nt, 6)), pltpu.SemaphoreType.DMA((nt, 6)), pltpu.SemaphoreType.DMA((nt,))], run)


def _scatter_plan(grads, owners):
    nt = len(grads)
    shapes = [g.shape[1:] if g.ndim == 3 else (g.shape[0], g.shape[1] // 4) for g in grads]

    def run(step, nsteps, ins, outs, sems):
        send_sems, recv_sems, loc_sems = sems
        x, y, c = _place()
        me = 4 * x + 2 * y + c

        def target(r):
            tx = 1 - x if r & 2 else x
            ty = 1 - y if r & 1 else y
            return tx, ty

        def block(t, chip):
            if len(ins[t].shape) == 3:
                return ins[t].at[chip]
            n = shapes[t][1]
            return ins[t].at[:, pl.ds(pl.multiple_of(chip * n, _LANES), n)]

        def send(t, r):
            tx, ty = target(r)
            return _remote(block(t, 2 * tx + ty), outs[t].at[me], send_sems.at[t, r], recv_sems.at[t, 2 * r + c],
                           (tx, ty, owners[t]))

        def local(t):
            return pltpu.make_async_copy(block(t, 2 * x + y), outs[t].at[me], loc_sems.at[t])

        @pl.when(step == 0)
        def _():
            for t in range(nt):
                @pl.when(c == owners[t])
                def _():
                    local(t).start()

                @pl.when(c != owners[t])
                def _():
                    send(t, 0).start()

                for r in range(1, 4):
                    send(t, r).start()

        @pl.when(step == nsteps - 1)
        def _():
            for t in range(nt):
                @pl.when(c == owners[t])
                def _():
                    for r in range(4):
                        sx, sy = target(r)
                        for cs in range(2):
                            if r == 0 and cs == owners[t]:
                                continue
                            src_dev = 4 * sx + 2 * sy + cs
                            _remote(block(t, 0), outs[t].at[src_dev], send_sems.at[t, r], recv_sems.at[t, 2 * r + cs],
                                    (x, y, c)).wait_recv()
                    local(t).wait()

                @pl.when(c != owners[t])
                def _():
                    send(t, 0).wait_send()

                for r in range(1, 4):
                    send(t, r).wait_send()

    return _Comm(grads, [jax.ShapeDtypeStruct((8,) + s, g.dtype) for s, g in zip(shapes, grads)],
                 [pltpu.SemaphoreType.DMA((nt, 4)), pltpu.SemaphoreType.DMA((nt, 8)), pltpu.SemaphoreType.DMA((nt,))], run)


def _share_plan(reduced, owners):
    nt = len(reduced)

    def run(step, nsteps, ins, outs, sems):
        del ins
        send_sems, recv_sems = sems
        x, y, c = _place()

        def give(t, to):
            return _remote(outs[t], outs[t], send_sems.at[t], recv_sems.at[t], to)

        @pl.when(step == 0)
        def _():
            for t in range(nt):
                @pl.when(c == owners[t])
                def _():
                    give(t, (x, y, 1 - c)).start()

        @pl.when(step == nsteps - 1)
        def _():
            for t in range(nt):
                @pl.when(c == owners[t])
                def _():
                    give(t, (x, y, 1 - c)).wait_send()

                @pl.when(c != owners[t])
                def _():
                    give(t, (x, y, c)).wait_recv()

    return _Comm(reduced, [jax.ShapeDtypeStruct(r.shape, r.dtype) for r in reduced],
                 [pltpu.SemaphoreType.DMA((nt,)), pltpu.SemaphoreType.DMA((nt,))], run,
                 aliases={t: t for t in range(nt)})


def _join(a, b):
    ni, no, ns = len(a.inputs), len(a.out_shapes), len(a.sems)

    def run(step, nsteps, ins, outs, sems):
        a.run(step, nsteps, ins[:ni], outs[:no], sems[:ns])
        b.run(step, nsteps, ins[ni:], outs[no:], sems[ns:])

    aliases = dict(a.aliases)
    aliases.update({ni + i: no + o for i, o in b.aliases.items()})
    return _Comm(a.inputs + b.inputs, a.out_shapes + b.out_shapes, a.sems + b.sems, run, aliases)


def _peer(x, y, c, r):
    px = 1 - x if r & 4 else x
    py = 1 - y if r & 2 else y
    pc = 1 - c if r & 1 else c
    return (px, py, pc), 4 * px + 2 * py + pc


def _sum_slots(st, name):
    _, K, n = st.shape
    tr = next(t for t in (256, 128, 64, 32, 16) if K % t == 0)

    def kern(s_ref, o_ref):
        acc = s_ref[0].astype(_F32)
        for d in range(1, 8):
            acc = acc + s_ref[d].astype(_F32)
        o_ref[...] = acc.astype(o_ref.dtype)

    return pl.pallas_call(
        kern, name=name, grid=(K // tr,),
        in_specs=[pl.BlockSpec((8, tr, n), lambda i: (0, i, 0))], out_specs=_rows(tr, n),
        out_shape=jax.ShapeDtypeStruct((K, n), _ACT),
        compiler_params=_cparams("parallel"),
    )(st)


def _all_reduce_small(p):
    R = p.shape[0]

    def body(p_ref, o_ref, stage, send_sems, recv_sems):
        x, y, c = _place()
        me = 4 * x + 2 * y + c
        stage[me] = p_ref[...]
        sent = []
        for r in range(1, 8):
            to, _ = _peer(x, y, c, r)
            cp = _remote(p_ref, stage.at[me], send_sems.at[r - 1], recv_sems.at[r - 1], to)
            cp.start()
            sent.append(cp)
        for r in range(1, 8):
            _, src_dev = _peer(x, y, c, r)
            _remote(p_ref, stage.at[src_dev], send_sems.at[r - 1], recv_sems.at[r - 1], (x, y, c)).wait_recv()
        acc = stage[0]
        for d in range(1, 8):
            acc = acc + stage[d]
        o_ref[...] = acc
        for cp in sent:
            cp.wait_send()

    vm = pl.BlockSpec(memory_space=pltpu.VMEM)
    return pl.pallas_call(
        body, name="all_reduce_small",
        in_specs=[vm], out_specs=vm,
        out_shape=jax.ShapeDtypeStruct((R, _LANES), _F32),
        scratch_shapes=[pltpu.VMEM((8, R, _LANES), _F32), pltpu.SemaphoreType.DMA((7,)), pltpu.SemaphoreType.DMA((7,))],
    )(p)


def _adamw_update(gv, w_ref, m_ref, v_ref, gf_ref, d_ref, nm_ref, nv_ref):
    nm = _B1 * m_ref[...] + (1.0 - _B1) * gv
    nv = _B2 * v_ref[...] + (1.0 - _B2) * (gv * gv)
    m_hat = nm / (1.0 - _B1 ** _STEP)
    v_hat = nv / (1.0 - _B2 ** _STEP)
    gf_ref[...] = gv
    d_ref[...] = -_LR * (m_hat / (jnp.sqrt(v_hat) + _EPS) + _WD * w_ref[...])
    nm_ref[...] = nm
    nv_ref[...] = nv


def _adamw_layers(w, g_layers, m, v, name):
    _, K, n = w.shape
    tr = next(t for t in (256, 128, 64, 32, 16) if K % t == 0)

    def kern(w_ref, g0_ref, g1_ref, m_ref, v_ref, *out_refs):
        first = pl.program_id(0) == 0
        gv = jnp.where(first, g0_ref[...].astype(_F32), g1_ref[...].astype(_F32))
        _adamw_update(gv, w_ref, m_ref, v_ref, *out_refs)

    stacked = pl.BlockSpec((None, tr, n), lambda l, i: (l, i, 0))
    layer = pl.BlockSpec((tr, n), lambda l, i: (i, 0))
    return tuple(pl.pallas_call(
        kern, name=name, grid=(2, K // tr),
        in_specs=[stacked, layer, layer, stacked, stacked], out_specs=[stacked] * 4,
        out_shape=[jax.ShapeDtypeStruct(w.shape, _F32)] * 4,
        compiler_params=_cparams("parallel", "parallel"),
    )(w, g_layers[0], g_layers[1], m, v))


def _adamw(w, g, m, v, name):
    shape = w.shape
    w2, g2, m2, v2 = (a.reshape(-1, shape[-1]) for a in (w, g, m, v))
    R, C = w2.shape
    tr = next((t for t in (256, 128, 64, 32, 16) if R % t == 0), R)

    def kern(w_ref, g_ref, m_ref, v_ref, *out_refs):
        _adamw_update(g_ref[...].astype(_F32), w_ref, m_ref, v_ref, *out_refs)

    outs = pl.pallas_call(
        kern, name=name, grid=(R // tr,),
        in_specs=[_rows(tr, C)] * 4, out_specs=[_rows(tr, C)] * 4,
        out_shape=[jax.ShapeDtypeStruct((R, C), _F32)] * 4,
        compiler_params=_cparams("parallel"),
    )(w2, g2, m2, v2)
    return tuple(o.reshape(shape) for o in outs)


def _pack_small(gss, sq):
    parts = [gss[l][n].reshape(-1) for n in _SMALL for l in range(len(gss))] + [jnp.sum(sq).reshape(1)]
    flat = jnp.concatenate(parts)
    rows = -(-flat.shape[0] // (8 * _LANES)) * 8
    return jnp.pad(flat, (0, rows * _LANES - flat.shape[0])).reshape(rows, _LANES)


def _unpack_small(total, shapes):
    flat = total.reshape(-1)
    out, off = {}, 0
    for n in _SMALL:
        layers = []
        for _ in range(shapes[n][0]):
            size = 1
            for s in shapes[n][1:]:
                size *= s
            layers.append(flat[off:off + size].reshape(shapes[n][1:]))
            off += size
        out[n] = jnp.stack(layers)
    return out, flat[off]


_GATHER = {
    "band_fwd_0": [(0, "w_proj_a"), (0, "w_proj_b"), (0, "w_out"), (0, "w_ffn_out")],
    "stick_fwd_0": [(0, "w_ffn_in"), (1, "w_proj_a"), (1, "w_proj_b"), (1, "w_out")],
    "mix_fwd_0": [(1, "w_ffn_out")],
    "ffn_fwd_0": [(1, "w_in"), (1, "w_ffn_in")],
}
_SCATTER = {
    "band_bwd_1": [(1, "w_ffn_in"), (1, "w_ffn_out")],
    "stick_bwd_1": [(1, "w_proj_a"), (1, "w_proj_b"), (1, "w_out")],
    "band_bwd_0": [(1, "w_in"), (0, "w_ffn_in")],
    "stick_bwd_0": [(0, "w_ffn_out"), (0, "w_proj_a"), (0, "w_proj_b"), (0, "w_out")],
    "in_proj_bwd_0": [(0, "w_in")],
}
_SHARE = {"stick_bwd_1": "band_bwd_1", "band_bwd_0": "stick_bwd_1", "stick_bwd_0": "band_bwd_0", "grad_w_in_0": "stick_bwd_0"}


def _owner(key):
    del key
    return 1


def kernel(x, w_in, b_gate, rel_bias, w_proj_a, w_proj_b, w_out, ln1_g, ln1_b, w_ffn_in, w_ffn_out, ln2_g, ln2_b, loss_target, m_w_in, m_b_gate, m_rel_bias, m_w_proj_a, m_w_proj_b, m_w_out, m_ln1_g, m_ln1_b, m_w_ffn_in, m_w_ffn_out, m_ln2_g, m_ln2_b, v_w_in, v_b_gate, v_rel_bias, v_w_proj_a, v_w_proj_b, v_w_out, v_ln1_g, v_ln1_b, v_w_ffn_in, v_w_ffn_out, v_ln2_g, v_ln2_b):
    names = ("w_in", "b_gate", "rel_bias", "w_proj_a", "w_proj_b", "w_out", "ln1_g", "ln1_b", "w_ffn_in", "w_ffn_out", "ln2_g", "ln2_b")
    w = dict(zip(names, (w_in, b_gate, rel_bias, w_proj_a, w_proj_b, w_out, ln1_g, ln1_b, w_ffn_in, w_ffn_out, ln2_g, ln2_b)))
    m = dict(zip(names, (m_w_in, m_b_gate, m_rel_bias, m_w_proj_a, m_w_proj_b, m_w_out, m_ln1_g, m_ln1_b, m_w_ffn_in, m_w_ffn_out, m_ln2_g, m_ln2_b)))
    v = dict(zip(names, (v_w_in, v_b_gate, v_rel_bias, v_w_proj_a, v_w_proj_b, v_w_out, v_ln1_g, v_ln1_b, v_w_ffn_in, v_w_ffn_out, v_ln2_g, v_ln2_b)))
    T, D = x.shape[-2], x.shape[-1]
    assert w_in.shape[0] == 2, "the exchange schedule below is written for two layers"

    mine = [{n: w[n][l].astype(_MXU) for n in _DENSE} for l in range(2)]
    W = [dict(), dict()]
    gws = [dict(), dict()]
    slots, final = {}, {}

    def gather(keys):
        sizes = [mine[l][n].size for l, n in keys]
        passed, fractions = 0, []
        for s in sizes:
            passed += s
            fractions.append(0.15 + 0.6 * passed / sum(sizes))

        def done(outs):
            for (l, n), o in zip(keys, outs):
                W[l][n] = o
        return _gather_plan([mine[l][n] for l, n in keys], fractions), done

    def scatter(keys):
        comm = _scatter_plan([gws[l][n] for l, n in keys], [_owner(key) for key in keys])
        return comm, lambda outs: slots.update(zip(keys, outs))

    def share(keys):
        reduced = [_sum_slots(slots[key], f"sum_grad_{key[1]}_{key[0]}") for key in keys]
        comm = _share_plan(reduced, [_owner(key) for key in keys])
        return comm, lambda outs: final.update(zip(keys, outs))

    def both(first, second):
        (ca, da), (cb, db) = first, second
        na = len(ca.out_shapes)
        return _join(ca, cb), lambda outs: (da(outs[:na]), db(outs[na:]))

    plans = {key: functools.partial(gather, keys) for key, keys in _GATHER.items()}
    for key, keys in _SCATTER.items():
        plans[key] = functools.partial(scatter, keys)
    for key, scattered_under in _SHARE.items():
        handed = functools.partial(share, _SCATTER[scattered_under])
        carried = plans.get(key)
        plans[key] = handed if carried is None else (lambda carried=carried, handed=handed: both(carried(), handed()))
    small = {n: w[n] for n in _SMALL}
    sq, dx, _, gss = _local_step(x.reshape(T, D), loss_target.reshape(T, D), W, small,
                                  _Plans(plans, {0: mine[0]["w_in"]}), gws)

    comm, done = share(_SCATTER["in_proj_bwd_0"])
    done(_comm_only(comm, "share_last"))
    total = _all_reduce_small(_pack_small(gss, sq))
    small_grads, sq_all = _unpack_small(total, {n: w[n].shape for n in _SMALL})
    loss = 0.5 * sq_all / D

    grad, delta, new_m, new_v = {}, {}, {}, {}
    for n in names:
        if n in _DENSE:
            updated = _adamw_layers(w[n], [final[(l, n)] for l in range(2)], m[n], v[n], f"adamw_{n}")
        else:
            updated = _adamw(w[n], small_grads[n], m[n], v[n], f"adamw_{n}")
        grad[n], delta[n], new_m[n], new_v[n] = updated
    return (loss, dx.reshape(x.shape), *[grad[n] for n in names], *[delta[n] for n in names],
            *[new_m[n] for n in names], *[new_v[n] for n in names])
```

```python
import functools

import jax
import jax.numpy as jnp
from jax import lax
from jax.experimental import pallas as pl
from jax.experimental.pallas import tpu as pltpu

_MXU = jnp.bfloat16
_ACT = jnp.bfloat16
_F32 = jnp.float32

_HEAD = 64
_CHUNK = 64
_LANES = 128
_TQ = 128
_BAND_TILES = 5
_BIAS_TILES = 9
_REL_CLIP = 256
_LN_EPS = 1e-5
_MASKED = -1e30
_EXP_ZERO_BELOW = -87.34
_SB_WINDOW = 2
_SB_SUBTILES = 4
_BAND_SUBTILES = 8
_VMEM_LIMIT = 56 * 1024 * 1024
_GRAD_ACC_BYTES = 12 * 1024 * 1024

_LR, _B1, _B2, _EPS, _WD, _STEP = 0.001, 0.9, 0.999, 1e-08, 0.01, 10

_MESH = pl.DeviceIdType.MESH


def _dot(a, b):
    return jnp.dot(a, b, preferred_element_type=_F32)


def _dot_nt(a, b):
    return lax.dot_general(a, b, (((1,), (1,)), ((), ())), preferred_element_type=_F32)


def _dot_tn(a, b):
    return lax.dot_general(a, b, (((0,), (0,)), ((), ())), preferred_element_type=_F32)


def _cparams(*sem):
    return pltpu.CompilerParams(dimension_semantics=sem, vmem_limit_bytes=_VMEM_LIMIT)


def _rows(t, c):
    return pl.BlockSpec((t, c), lambda i: (i, 0))


def _whole(shape):
    return pl.BlockSpec(shape, lambda i: tuple(0 for _ in shape))


_ANY = pl.BlockSpec(memory_space=pl.ANY)


def _load_cols(w_hbm, w_vmem, sem):
    n = w_hbm.shape[-1]
    cps = [pltpu.make_async_copy(w_hbm.at[k], w_vmem.at[:, pl.ds(k * n, n)], sem.at[k]) for k in range(4)]
    for cp in cps:
        cp.start()
    for cp in cps:
        cp.wait()


def _load_rows(w_hbm, w_vmem, sem):
    r = w_hbm.shape[-2]
    cps = [pltpu.make_async_copy(w_hbm.at[k], w_vmem.at[pl.ds(k * r, r), :], sem.at[k]) for k in range(4)]
    for cp in cps:
        cp.start()
    for cp in cps:
        cp.wait()


def _ln_stats(u):
    mu = jnp.mean(u, axis=-1, keepdims=True)
    xc = u - mu
    var = jnp.mean(xc * xc, axis=-1, keepdims=True)
    rstd = lax.rsqrt(var + _LN_EPS)
    return xc * rstd, rstd


def _ln_bwd(u, dy, gamma):
    xhat, rstd = _ln_stats(u)
    dxh = dy * gamma
    m1 = jnp.mean(dxh, axis=-1, keepdims=True)
    m2 = jnp.mean(dxh * xhat, axis=-1, keepdims=True)
    du = rstd * (dxh - m1 - xhat * m2)
    return du, jnp.sum(dy * xhat, axis=0, keepdims=True), jnp.sum(dy, axis=0, keepdims=True), xhat


def _divisor_tile(n, cap):
    best = None
    for t in range(_LANES, min(n, cap) + 1, _LANES):
        if n % t == 0:
            best = t
    return best or n


class _Comm:
    def __init__(self, inputs, out_shapes, sems, run, aliases=None):
        self.inputs, self.out_shapes, self.sems, self.run = list(inputs), list(out_shapes), list(sems), run
        self.aliases = aliases or {}


def _call(kern, comm, *, name, grid, in_specs, out_specs, out_shape, scratch_shapes, args, semantics):
    in_specs, out_specs, out_shape, scratch_shapes = list(in_specs), list(out_specs), list(out_shape), list(scratch_shapes)
    if comm is None:
        outs = pl.pallas_call(kern, name=name, grid=grid, in_specs=in_specs, out_specs=out_specs, out_shape=out_shape,
                              scratch_shapes=scratch_shapes, compiler_params=_cparams(*semantics))(*args)
        return list(outs), []
    n_in, n_out, n_scr = len(in_specs), len(out_specs), len(scratch_shapes)
    ci, co = len(comm.inputs), len(comm.out_shapes)
    nsteps = functools.reduce(lambda a, b: a * b, grid, 1)

    def fused(*refs):
        a, b = n_in, n_in + ci
        c, d = b + n_out, b + n_out + co
        e = d + n_scr
        step = pl.program_id(0)
        for ax in range(1, len(grid)):
            step = step * grid[ax] + pl.program_id(ax)
        comm.run(step, nsteps, refs[a:b], refs[c:d], refs[e:])
        kern(*refs[:a], *refs[b:c], *refs[d:e])

    outs = pl.pallas_call(
        fused, name=name, grid=grid, in_specs=in_specs + [_ANY] * ci, out_specs=out_specs + [_ANY] * co,
        out_shape=out_shape + comm.out_shapes, scratch_shapes=scratch_shapes + comm.sems,
        input_output_aliases={n_in + i: n_out + o for i, o in comm.aliases.items()},
        compiler_params=_cparams(*("arbitrary" for _ in grid)))(*args, *comm.inputs)
    return list(outs[:n_out]), list(outs[n_out:])


def _comm_only(comm, name):
    def body(*refs):
        ci, co = len(comm.inputs), len(comm.out_shapes)
        comm.run(0, 1, refs[:ci], refs[ci:ci + co], refs[ci + co:])

    outs = pl.pallas_call(body, name=name, in_specs=[_ANY] * len(comm.inputs), out_specs=[_ANY] * len(comm.out_shapes),
                          out_shape=comm.out_shapes, scratch_shapes=comm.sems,
                          input_output_aliases=dict(comm.aliases))(*comm.inputs)
    return list(outs)


def _in_proj(x, w_in, layer, comm=None):
    T, D = x.shape
    N = 4 * w_in.shape[-1]
    tm = 512

    def kern(x_ref, w_hbm, h_ref, xb_ref, w_v, sem):
        @pl.when(pl.program_id(0) == 0)
        def _():
            _load_cols(w_hbm, w_v, sem)

        xb = x_ref[...].astype(_MXU)
        h_ref[...] = _dot(xb, w_v[...]).astype(h_ref.dtype)
        xb_ref[...] = xb.astype(xb_ref.dtype)

    return _call(
        kern, comm, name=f"in_proj_{layer}", grid=(T // tm,),
        in_specs=[_rows(tm, D), _ANY],
        out_specs=[_rows(tm, N), _rows(tm, D)],
        out_shape=[jax.ShapeDtypeStruct((T, N), _ACT), jax.ShapeDtypeStruct((T, D), _ACT)],
        scratch_shapes=[pltpu.VMEM((D, N), w_in.dtype), pltpu.SemaphoreType.DMA((4,))],
        args=(x, w_in), semantics=("arbitrary",))


def _in_proj_gathering(x, block, layer):
    T, D = x.shape
    n = block.shape[1]
    tm = min(T, 1024)
    nrows = T // tm
    pass_steps = [int(f * nrows) for f in (0.6, 1.0, 1.7)]
    px, py, _ = _place()
    order = jnp.stack([2 * px + py, 2 * (1 - px) + py, 2 * px + (1 - py), 2 * (1 - px) + (1 - py)]).astype(jnp.int32)

    def kern(order_ref, x_ref, blk_hbm, h_ref, xb_ref, w_hbm, w_v, send_sems, recv_sems, loc_sem, load_sem):
        del order_ref
        step = pl.program_id(0) * nrows + pl.program_id(1)
        x_, y_, c = _place()
        k = 2 * x_ + y_
        me, sibling = (x_, y_, c), (x_, y_, 1 - c)
        chips = [(1 - x_, y_), (x_, 1 - y_), (1 - x_, 1 - y_)]
        chip_k = [2 * cx + cy for cx, cy in chips]

        def ici(s, owner_k, to, src=None):
            dst = _half(w_hbm.at[owner_k], c)
            return _remote(dst if src is None else src, dst, send_sems.at[s], recv_sems.at[s], to)

        def passed(s, hc, to):
            blk = _half(w_hbm.at[chip_k[s]], hc)
            return _remote(blk, blk, send_sems.at[3 + s], recv_sems.at[3 + s], to)

        local = pltpu.make_async_copy(blk_hbm, w_hbm.at[k], loc_sem.at[0])

        def load(src):
            cp = pltpu.make_async_copy(src, w_v, load_sem.at[0])
            cp.start()
            cp.wait()

        @pl.when(step == 0)
        def _():
            for s, chip in enumerate(chips):
                ici(s, k, (*chip, c), src=_half(blk_hbm, c)).start()
            local.start()
            load(blk_hbm)

        for s in range(3):
            @pl.when(step == pass_steps[s])
            def _():
                ici(s, chip_k[s], me).wait_recv()
                passed(s, c, sibling).start()

            @pl.when(step == (s + 1) * nrows)
            def _():
                passed(s, 1 - c, me).wait_recv()
                load(w_hbm.at[chip_k[s]])

        xb = x_ref[...].astype(_MXU)
        h_ref[...] = _dot(xb, w_v[...]).astype(h_ref.dtype)

        @pl.when(pl.program_id(0) == 0)
        def _():
            xb_ref[...] = xb.astype(xb_ref.dtype)

        @pl.when(step == 4 * nrows - 1)
        def _():
            for s, chip in enumerate(chips):
                ici(s, k, (*chip, c), src=_half(blk_hbm, c)).wait_send()
                passed(s, c, sibling).wait_send()
            local.wait()

    assert all(pass_steps[s] <= (s + 1) * nrows for s in range(3))
    h, xb, w_in = pl.pallas_call(
        kern, name=f"in_proj_{layer}",
        grid_spec=pltpu.PrefetchScalarGridSpec(
            num_scalar_prefetch=1, grid=(4, nrows),
            in_specs=[pl.BlockSpec((tm, D), lambda j, i, o: (i, 0)), _ANY],
            out_specs=[pl.BlockSpec((tm, n), lambda j, i, o: (i, o[j])),
                       pl.BlockSpec((tm, D), lambda j, i, o: (jnp.where(j == 0, i, nrows - 1), 0)), _ANY],
            scratch_shapes=[pltpu.VMEM((D, n), block.dtype), pltpu.SemaphoreType.DMA((6,)),
                            pltpu.SemaphoreType.DMA((6,)), pltpu.SemaphoreType.DMA((1,)), pltpu.SemaphoreType.DMA((1,))]),
        out_shape=[jax.ShapeDtypeStruct((T, 4 * n), _ACT), jax.ShapeDtypeStruct((T, D), _ACT),
                   jax.ShapeDtypeStruct((4,) + block.shape, block.dtype)],
        compiler_params=_cparams("arbitrary", "arbitrary"))(order, x, block)
    return h, xb, w_in


def _gate_specs(h, tm, D):
    first = (h.shape[1] - 2 * D) // D
    assert first * D + 2 * D == h.shape[1]
    return [pl.BlockSpec((tm, D), lambda i: (i, first)), pl.BlockSpec((tm, D), lambda i: (i, first + 1))]


def _mix_fwd(oa, ob, h, x, wpa, wpb, wo, bg, gamma, beta, alpha, layer, comm=None):
    T, D = x.shape
    WA, WB = oa.shape[1], ob.shape[1]
    tm = 512

    def kern(oa_ref, ob_ref, hga_ref, hgb_ref, x_ref, bg_ref, g_ref, b_ref, wpa_h, wpb_h, wo_h,
             x1_ref, u1_ref, pre_ref, wpa_v, wpb_v, wo_v, sa, sb, so):
        @pl.when(pl.program_id(0) == 0)
        def _():
            _load_cols(wpa_h, wpa_v, sa)
            _load_cols(wpb_h, wpb_v, sb)
            _load_rows(wo_h, wo_v, so)

        ya = _dot(oa_ref[...].astype(_MXU), wpa_v[...])
        yb = _dot(ob_ref[...].astype(_MXU), wpb_v[...])
        bgv = bg_ref[...]
        ga = jax.nn.sigmoid(hga_ref[...].astype(_F32) + bgv[:, :D])
        gb = jax.nn.sigmoid(hgb_ref[...].astype(_F32) + bgv[:, D:])
        pre = ga * ya + gb * yb
        mix = _dot(pre.astype(_MXU), wo_v[...])
        u = alpha * x_ref[...] + mix
        xhat, _ = _ln_stats(u)
        x1_ref[...] = xhat * g_ref[...] + b_ref[...]
        u1_ref[...] = u
        pre_ref[...] = pre.astype(pre_ref.dtype)

    return _call(
        kern, comm, name=f"mix_fwd_{layer}", grid=(T // tm,),
        in_specs=[_rows(tm, WA), _rows(tm, WB), *_gate_specs(h, tm, D), _rows(tm, D),
                  _whole((1, 2 * D)), _whole((1, D)), _whole((1, D)), _ANY, _ANY, _ANY],
        out_specs=[_rows(tm, D)] * 3,
        out_shape=[jax.ShapeDtypeStruct((T, D), _F32), jax.ShapeDtypeStruct((T, D), _F32),
                   jax.ShapeDtypeStruct((T, D), _ACT)],
        scratch_shapes=[pltpu.VMEM((WA, D), wpa.dtype), pltpu.VMEM((WB, D), wpb.dtype), pltpu.VMEM((D, D), wo.dtype),
                        pltpu.SemaphoreType.DMA((4,)), pltpu.SemaphoreType.DMA((4,)), pltpu.SemaphoreType.DMA((4,))],
        args=(oa, ob, h, h, x, bg, gamma, beta, wpa, wpb, wo), semantics=("arbitrary",))


def _ffn_fwd(x1, wfi, wfo, gamma, beta, alpha, layer, comm=None):
    T, D = x1.shape
    F2 = 4 * wfi.shape[-1]
    F = F2 // 2
    tm = 512
    fc = F // 2

    def kern(x_ref, g_ref, b_ref, wi_h, wo_h, x2_ref, u2_ref, act_ref, gu_ref, xb_ref, wi_v, wo_v, si, so):
        @pl.when(pl.program_id(0) == 0)
        def _():
            _load_cols(wi_h, wi_v, si)
            _load_rows(wo_h, wo_v, so)

        x = x_ref[...]
        xb = x.astype(_MXU)
        xb_ref[...] = xb.astype(xb_ref.dtype)
        ffn = jnp.zeros((tm, D), _F32)
        for c in range(2):
            g = _dot(xb, wi_v[:, c * fc:(c + 1) * fc])
            u = _dot(xb, wi_v[:, F + c * fc:F + (c + 1) * fc])
            act = g * jax.nn.sigmoid(g) * u
            ab = act.astype(_MXU)
            ffn = ffn + _dot(ab, wo_v[c * fc:(c + 1) * fc, :])
            act_ref[:, c * fc:(c + 1) * fc] = ab.astype(act_ref.dtype)
            gu_ref[:, c * fc:(c + 1) * fc] = g.astype(gu_ref.dtype)
            gu_ref[:, F + c * fc:F + (c + 1) * fc] = u.astype(gu_ref.dtype)
        uu = alpha * x + ffn
        xhat, _ = _ln_stats(uu)
        x2_ref[...] = xhat * g_ref[...] + b_ref[...]
        u2_ref[...] = uu

    return _call(
        kern, comm, name=f"ffn_fwd_{layer}", grid=(T // tm,),
        in_specs=[_rows(tm, D), _whole((1, D)), _whole((1, D)), _ANY, _ANY],
        out_specs=[_rows(tm, D), _rows(tm, D), _rows(tm, F), _rows(tm, F2), _rows(tm, D)],
        out_shape=[jax.ShapeDtypeStruct((T, D), _F32), jax.ShapeDtypeStruct((T, D), _F32),
                   jax.ShapeDtypeStruct((T, F), _ACT), jax.ShapeDtypeStruct((T, F2), _ACT),
                   jax.ShapeDtypeStruct((T, D), _ACT)],
        scratch_shapes=[pltpu.VMEM((D, F2), wfi.dtype), pltpu.VMEM((F, D), wfo.dtype),
                        pltpu.SemaphoreType.DMA((4,)), pltpu.SemaphoreType.DMA((4,))],
        args=(x1, gamma, beta, wfi, wfo), semantics=("arbitrary",))


def _ffn_bwd(u2, dy_or_target, gu, gamma, beta, wfi, wfo, alpha, layer, last):
    T, D = u2.shape
    F2 = gu.shape[1]
    F = F2 // 2
    tm = 256
    fc = F // 2

    def kern(u_ref, dy_ref, gu_ref, g_ref, b_ref, wi_h, wo_h, dx_ref, dub_ref, dgu_ref, st_ref, wi_v, wo_v, si, so):
        @pl.when(pl.program_id(0) == 0)
        def _():
            _load_cols(wi_h, wi_v, si)
            _load_rows(wo_h, wo_v, so)
            st_ref[...] = jnp.zeros_like(st_ref)

        gam = g_ref[...]
        u = u_ref[...]
        if last:
            xhat0, _ = _ln_stats(u)
            err = xhat0 * gam + b_ref[...] - dy_ref[...]
            dy = err * (1.0 / D)
            st_ref[2:3, :] += jnp.sum(err * err, axis=0, keepdims=True)
        else:
            dy = dy_ref[...]
        du, dgam, dbet, _ = _ln_bwd(u, dy, gam)
        st_ref[0:1, :] += dgam
        st_ref[1:2, :] += dbet
        dub = du.astype(_MXU)
        dub_ref[...] = dub.astype(dub_ref.dtype)
        dx = alpha * du
        for c in range(2):
            dact = _dot_nt(dub, wo_v[c * fc:(c + 1) * fc, :])
            g = gu_ref[:, c * fc:(c + 1) * fc].astype(_F32)
            uu = gu_ref[:, F + c * fc:F + (c + 1) * fc].astype(_F32)
            sg = jax.nn.sigmoid(g)
            dg = (dact * uu * (sg * (1.0 + g * (1.0 - sg)))).astype(_MXU)
            dup = (dact * (g * sg)).astype(_MXU)
            dgu_ref[:, c * fc:(c + 1) * fc] = dg.astype(dgu_ref.dtype)
            dgu_ref[:, F + c * fc:F + (c + 1) * fc] = dup.astype(dgu_ref.dtype)
            dx = dx + _dot_nt(dg, wi_v[:, c * fc:(c + 1) * fc]) + _dot_nt(dup, wi_v[:, F + c * fc:F + (c + 1) * fc])
        dx_ref[...] = dx

    return pl.pallas_call(
        kern, name=f"ffn_bwd_{layer}", grid=(T // tm,),
        in_specs=[_rows(tm, D), _rows(tm, D), _rows(tm, F2), _whole((1, D)), _whole((1, D)), _ANY, _ANY],
        out_specs=[_rows(tm, D), _rows(tm, D), _rows(tm, F2), _whole((8, D))],
        out_shape=[jax.ShapeDtypeStruct((T, D), _F32), jax.ShapeDtypeStruct((T, D), _ACT),
                   jax.ShapeDtypeStruct((T, F2), _ACT), jax.ShapeDtypeStruct((8, D), _F32)],
        scratch_shapes=[pltpu.VMEM((D, F2), wfi.dtype), pltpu.VMEM((F, D), wfo.dtype),
                        pltpu.SemaphoreType.DMA((4,)), pltpu.SemaphoreType.DMA((4,))],
        compiler_params=_cparams("arbitrary"),
    )(u2, dy_or_target, gu, gamma, beta, wfi, wfo)


def _residual_nt(res, res_scale, pieces, w, name, comm=None):
    T, K = res.shape
    widths = [p.shape[1] for p in pieces]
    N = sum(widths)
    tm = 512

    def kern(r_ref, *refs):
        d_refs, (w_hbm, o_ref, w_v, sem) = refs[:len(pieces)], refs[len(pieces):]

        @pl.when(pl.program_id(0) == 0)
        def _():
            _load_cols(w_hbm, w_v, sem)

        acc = res_scale * r_ref[...]
        off = 0
        for d_ref, width in zip(d_refs, widths):
            acc = acc + _dot_nt(d_ref[...].astype(_MXU), w_v[:, off:off + width])
            off += width
        o_ref[...] = acc

    outs, extra = _call(
        kern, comm, name=name, grid=(T // tm,),
        in_specs=[_rows(tm, K)] + [_rows(tm, width) for width in widths] + [_ANY], out_specs=[_rows(tm, K)],
        out_shape=[jax.ShapeDtypeStruct((T, K), _F32)],
        scratch_shapes=[pltpu.VMEM((K, N), w.dtype), pltpu.SemaphoreType.DMA((4,))],
        args=(res, *pieces, w), semantics=("arbitrary",))
    return outs[0], extra


def _grad_w_pieces(a, pieces, name, comm=None):
    T, M = a.shape
    widths = [p.shape[1] for p in pieces]
    bw = functools.reduce(_gcd, widths + [512])
    first = [sum(widths[:p]) // bw for p in range(len(pieces))]
    count = [width // bw for width in widths]
    N = sum(widths)
    tk = 1024 if T % 1024 == 0 else 512
    nk = T // tk

    def kern(a_ref, *refs):
        b_refs, (o_ref, acc) = refs[:len(pieces)], refs[len(pieces):]
        j, k = pl.program_id(0), pl.program_id(1)

        @pl.when(k == 0)
        def _():
            acc[...] = jnp.zeros_like(acc)

        for b_ref, start, blocks in zip(b_refs, first, count):
            @pl.when(jnp.logical_and(j >= start, j < start + blocks))
            def _():
                acc[...] += _dot_tn(a_ref[...].astype(_MXU), b_ref[...].astype(_MXU))

        @pl.when(k == nk - 1)
        def _():
            o_ref[...] = acc[...].astype(o_ref.dtype)

    def piece_spec(start, blocks):
        def index(j, k):
            mine = jnp.logical_and(j >= start, j < start + blocks)
            return jnp.where(mine, k, 0), jnp.where(mine, j - start, 0)
        return pl.BlockSpec((tk, bw), index)

    outs, extra = _call(
        kern, comm, name=name, grid=(N // bw, nk),
        in_specs=[pl.BlockSpec((tk, M), lambda j, k: (k, 0))] + [piece_spec(s, c) for s, c in zip(first, count)],
        out_specs=[pl.BlockSpec((M, bw), lambda j, k: (0, j))],
        out_shape=[jax.ShapeDtypeStruct((M, N), _ACT)], scratch_shapes=[pltpu.VMEM((M, bw), _F32)],
        args=(a, *pieces), semantics=("parallel", "arbitrary"))
    return outs[0], extra


def _gcd(a, b):
    while b:
        a, b = b, a % b
    return a


def _mix_bwd(u1, dx1, oa, ob, h, wpa, wpb, wo, bg, gamma, layer):
    T, D = u1.shape
    WA, WB = wpa.shape[-2], wpb.shape[-2]
    tm = 512

    def kern(u_ref, dx_ref, oa_ref, ob_ref, hga_ref, hgb_ref, bg_ref, g_ref, wpa_h, wpb_h, wo_h,
             du_ref, dub_ref, dya_ref, dyb_ref, dhg_ref, doa_ref, dob_ref, st_ref,
             wpa_v, wpb_v, wo_v, sa, sb, so):
        @pl.when(pl.program_id(0) == 0)
        def _():
            _load_cols(wpa_h, wpa_v, sa)
            _load_cols(wpb_h, wpb_v, sb)
            _load_rows(wo_h, wo_v, so)
            st_ref[...] = jnp.zeros_like(st_ref)

        du, dgam, dbet, _ = _ln_bwd(u_ref[...], dx_ref[...], g_ref[...])
        st_ref[1:2, :D] += dgam
        st_ref[1:2, D:] += dbet
        du_ref[...] = du
        dub = du.astype(_MXU)
        dub_ref[...] = dub.astype(dub_ref.dtype)
        dpre = _dot_nt(dub, wo_v[...])
        bgv = bg_ref[...]
        ga = jax.nn.sigmoid(hga_ref[...].astype(_F32) + bgv[:, :D])
        gb = jax.nn.sigmoid(hgb_ref[...].astype(_F32) + bgv[:, D:])
        dya = (dpre * ga).astype(_MXU)
        dyb = (dpre * gb).astype(_MXU)
        dsa = dpre * _dot(oa_ref[...].astype(_MXU), wpa_v[...]) * (ga * (1.0 - ga))
        dsb = dpre * _dot(ob_ref[...].astype(_MXU), wpb_v[...]) * (gb * (1.0 - gb))
        st_ref[0:1, :D] += jnp.sum(dsa, axis=0, keepdims=True)
        st_ref[0:1, D:] += jnp.sum(dsb, axis=0, keepdims=True)
        dya_ref[...] = dya.astype(dya_ref.dtype)
        dyb_ref[...] = dyb.astype(dyb_ref.dtype)
        dhg_ref[:, :D] = dsa.astype(dhg_ref.dtype)
        dhg_ref[:, D:] = dsb.astype(dhg_ref.dtype)
        doa_ref[...] = _dot_nt(dya, wpa_v[...]).astype(doa_ref.dtype)
        dob_ref[...] = _dot_nt(dyb, wpb_v[...]).astype(dob_ref.dtype)

    return pl.pallas_call(
        kern, name=f"mix_bwd_{layer}", grid=(T // tm,),
        in_specs=[_rows(tm, D), _rows(tm, D), _rows(tm, WA), _rows(tm, WB), *_gate_specs(h, tm, D), _whole((1, 2 * D)),
                  _whole((1, D)), _ANY, _ANY, _ANY],
        out_specs=[_rows(tm, D)] * 4 + [_rows(tm, 2 * D), _rows(tm, WA), _rows(tm, WB), _whole((8, 2 * D))],
        out_shape=[jax.ShapeDtypeStruct((T, D), _F32)] + [jax.ShapeDtypeStruct((T, D), _ACT)] * 3
        + [jax.ShapeDtypeStruct((T, 2 * D), _ACT), jax.ShapeDtypeStruct((T, WA), _ACT),
           jax.ShapeDtypeStruct((T, WB), _ACT), jax.ShapeDtypeStruct((8, 2 * D), _F32)],
        scratch_shapes=[pltpu.VMEM((WA, D), wpa.dtype), pltpu.VMEM((WB, D), wpb.dtype), pltpu.VMEM((D, D), wo.dtype),
                        pltpu.SemaphoreType.DMA((4,)), pltpu.SemaphoreType.DMA((4,)), pltpu.SemaphoreType.DMA((4,))],
        compiler_params=_cparams("arbitrary"),
    )(u1, dx1, oa, ob, h, h, bg, gamma, wpa, wpb, wo)


def _grad_w(a, b, *, col_shards, name, comm=None):
    T, M = a.shape
    N = b.shape[1]
    tk = 1024 if T % 1024 == 0 else 512
    n = N // 4 if col_shards else N
    whole = M * N * 4 <= _GRAD_ACC_BYTES
    tn = N if whole else (n if col_shards else _divisor_tile(N, _GRAD_ACC_BYTES // (4 * M)))
    nk = T // tk

    def kern(a_ref, b_ref, o_ref, acc):
        k = pl.program_id(1)

        @pl.when(k == 0)
        def _():
            acc[...] = jnp.zeros_like(acc)

        acc[...] += _dot_tn(a_ref[...].astype(_MXU), b_ref[...].astype(_MXU))

        @pl.when(k == nk - 1)
        def _():
            if col_shards and whole:
                for s in range(4):
                    o_ref[s] = acc[:, s * n:(s + 1) * n].astype(o_ref.dtype)
            else:
                o_ref[...] = acc[...].astype(o_ref.dtype)

    if col_shards:
        out_spec = (pl.BlockSpec((4, M, n), lambda j, k: (0, 0, 0)) if whole
                    else pl.BlockSpec((None, M, n), lambda j, k: (j, 0, 0)))
        out_shape = jax.ShapeDtypeStruct((4, M, n), _ACT)
    else:
        out_spec = pl.BlockSpec((M, tn), lambda j, k: (0, j))
        out_shape = jax.ShapeDtypeStruct((M, N), _ACT)
    outs, extra = _call(
        kern, comm, name=name, grid=(N // tn, nk),
        in_specs=[pl.BlockSpec((tk, M), lambda j, k: (k, 0)), pl.BlockSpec((tk, tn), lambda j, k: (k, j))],
        out_specs=[out_spec], out_shape=[out_shape], scratch_shapes=[pltpu.VMEM((M, tn), _F32)],
        args=(a, b), semantics=("parallel", "arbitrary"))
    return outs[0], extra


def _bias_tiles(rel):
    H = rel.shape[0]
    span = _TQ * _BAND_TILES - 1
    edge = span - _REL_CLIP
    gvec = jnp.concatenate([jnp.broadcast_to(rel[:, :1], (H, edge)), rel, jnp.broadcast_to(rel[:, -1:], (H, edge))], axis=1)
    width = _BIAS_TILES * _TQ
    period = width + _TQ
    tiled = jnp.broadcast_to(jnp.pad(gvec[:, ::-1], ((0, 0), (0, 1)))[:, None, :], (H, _TQ, period))
    rows = tiled.reshape(H, _TQ * period)[:, :_TQ * (period - 1)].reshape(H, _TQ, period - 1)[:, :, _TQ - 1:]
    r = jnp.arange(_TQ)[:, None]
    u = jnp.arange(width)[None, :]
    d = 4 * _TQ + r - u
    rm = r % _CHUNK
    valid = (d >= rm - (_CHUNK - 1)) & (d <= rm + 8 * _CHUNK)
    tiles = jnp.where(valid[None], rows, _MASKED)
    return tiles.reshape(H // 2, 2 * _TQ, _BIAS_TILES, _TQ).transpose(0, 2, 1, 3)


def _fold_bias_grad(db):
    H = 2 * db.shape[0]
    width = _BIAS_TILES * _TQ
    period = width + _TQ
    x = jnp.pad(db.transpose(0, 2, 1, 3).reshape(H, _TQ, width), ((0, 0), (0, 0), (_TQ - 1, 0)))
    skew = jnp.pad(x.reshape(H, _TQ * (period - 1)), ((0, 0), (0, _TQ))).reshape(H, _TQ, period)
    dg = skew.sum(axis=1)[:, :period - 1][:, ::-1]
    span = _TQ * _BAND_TILES - 1
    edge = span - _REL_CLIP
    mid = dg[:, edge:edge + 2 * _REL_CLIP + 1]
    lo = dg[:, :edge].sum(axis=1)
    hi = dg[:, edge + 2 * _REL_CLIP + 1:].sum(axis=1)
    return mid.at[:, 0].add(lo).at[:, -1].add(hi)


def _band_window(i):
    j0 = jnp.maximum(i - (_BAND_TILES - 1), 0)
    return j0, (_BAND_TILES - 1) - (i - j0)


def _head_masks():
    lane = lax.broadcasted_iota(jnp.int32, (1, _LANES), 1)
    return [(lane // _HEAD) == hh for hh in range(2)]


def _stack_heads(x, masks):
    return jnp.concatenate([jnp.where(m, x, jnp.zeros_like(x)) for m in masks], axis=0)


def _unstack_heads(y, masks):
    return jnp.where(masks[0], y[:_TQ], y[_TQ:])


def _scaled(q):
    return q * jnp.asarray(_HEAD ** -0.5, q.dtype)


def _band_probs(q2, k_ref, b_ref, j0, boff):
    s = []
    for j in range(_BAND_TILES):
        kj = k_ref[pl.ds(pl.multiple_of((j0 + j) * _TQ, _TQ), _TQ), :]
        s.append(_dot_nt(q2, kj) + b_ref[boff + j])
    m = jnp.max(functools.reduce(jnp.maximum, s), axis=-1, keepdims=True)
    p = [jnp.exp(x - m) for x in s]
    l = jnp.sum(functools.reduce(lambda a, b: a + b, p), axis=-1, keepdims=True)
    return p, 1.0 / l


def _qkv_specs(T, cb, npair, tq=_TQ):
    return [pl.BlockSpec((tq, _LANES), lambda h, i: (i, cb + h)),
            pl.BlockSpec((T, _LANES), lambda h, i: (0, cb + npair + h)),
            pl.BlockSpec((T, _LANES), lambda h, i: (0, cb + 2 * npair + h))]


def _attn_a_fwd(hq, bias, col0, width, layer, comm=None):
    T = hq.shape[0]
    npair = width // _LANES
    nsub = _BAND_SUBTILES
    tq = nsub * _TQ

    def kern(q_ref, k_ref, v_ref, b_ref, o_ref):
        masks = _head_masks()
        q = _scaled(q_ref[...])
        for s in range(nsub):
            part = slice(s * _TQ, (s + 1) * _TQ)
            j0, boff = _band_window(nsub * pl.program_id(1) + s)
            p, inv = _band_probs(_stack_heads(q[part], masks), k_ref, b_ref, j0, boff)
            o = jnp.zeros((2 * _TQ, _LANES), _F32)
            for j in range(_BAND_TILES):
                vj = v_ref[pl.ds(pl.multiple_of((j0 + j) * _TQ, _TQ), _TQ), :]
                o = o + _dot(p[j].astype(_MXU), vj)
            o_ref[part, :] = _unstack_heads(o * inv, masks).astype(o_ref.dtype)

    outs, extra = _call(
        kern, comm, name=f"band_attn_fwd_{layer}", grid=(npair, T // tq),
        in_specs=_qkv_specs(T, col0 // _LANES, npair, tq)
        + [pl.BlockSpec((None, _BIAS_TILES, 2 * _TQ, _TQ), lambda h, i: (h, 0, 0, 0))],
        out_specs=[pl.BlockSpec((tq, _LANES), lambda h, i: (i, h))],
        out_shape=[jax.ShapeDtypeStruct((T, width), _ACT)], scratch_shapes=[],
        args=(hq, hq, hq, bias), semantics=("arbitrary", "arbitrary"))
    return outs[0], extra


def _attn_a_bwd(hq, bias, do, col0, width, layer, comm=None):
    T = hq.shape[0]
    npair = width // _LANES
    nsub = _BAND_SUBTILES
    tq = nsub * _TQ
    nq = T // tq
    scale = _HEAD ** -0.5

    def kern(q_ref, k_ref, v_ref, b_ref, do_ref, dq_ref, dk_ref, dv_ref, db_ref, dk_acc, dv_acc):
        i = pl.program_id(1)

        @pl.when(i == 0)
        def _():
            dk_acc[...] = jnp.zeros_like(dk_acc)
            dv_acc[...] = jnp.zeros_like(dv_acc)
            db_ref[...] = jnp.zeros_like(db_ref)

        masks = _head_masks()
        q = _scaled(q_ref[...])
        do_t = do_ref[...]
        for s in range(nsub):
            part = slice(s * _TQ, (s + 1) * _TQ)
            j0, boff = _band_window(nsub * i + s)
            q2 = _stack_heads(q[part], masks)
            do2 = _stack_heads(do_t[part], masks).astype(_MXU)
            p, inv = _band_probs(q2, k_ref, b_ref, j0, boff)
            rows = [pl.ds(pl.multiple_of((j0 + j) * _TQ, _TQ), _TQ) for j in range(_BAND_TILES)]
            p = [x * inv for x in p]
            dp = [_dot_nt(do2, v_ref[rows[j], :]) for j in range(_BAND_TILES)]
            delta = jnp.sum(functools.reduce(lambda a, b: a + b, [p[j] * dp[j] for j in range(_BAND_TILES)]),
                            axis=-1, keepdims=True)
            dq = jnp.zeros((2 * _TQ, _LANES), _F32)
            for j in range(_BAND_TILES):
                ds = p[j] * (dp[j] - delta)
                db_ref[boff + j] += ds
                dsb = ds.astype(_MXU)
                dq = dq + _dot(dsb, k_ref[rows[j], :])
                dk_acc[rows[j], :] += _dot_tn(dsb, q2)
                dv_acc[rows[j], :] += _dot_tn(p[j].astype(_MXU), do2)
            dq_ref[part, :] = (_unstack_heads(dq, masks) * scale).astype(dq_ref.dtype)

        @pl.when(i == nq - 1)
        def _():
            dk_ref[...] = dk_acc[...].astype(dk_ref.dtype)
            dv_ref[...] = dv_acc[...].astype(dv_ref.dtype)

    strip = pl.BlockSpec((None, _BIAS_TILES, 2 * _TQ, _TQ), lambda h, i: (h, 0, 0, 0))
    tile = pl.BlockSpec((tq, _LANES), lambda h, i: (i, h))
    column = pl.BlockSpec((T, _LANES), lambda h, i: (0, h))
    outs, extra = _call(
        kern, comm, name=f"band_attn_bwd_{layer}", grid=(npair, nq),
        in_specs=_qkv_specs(T, col0 // _LANES, npair, tq) + [strip, tile],
        out_specs=[tile, column, column, strip],
        out_shape=[jax.ShapeDtypeStruct((T, width), _ACT)] * 3
        + [jax.ShapeDtypeStruct((npair, _BIAS_TILES, 2 * _TQ, _TQ), _F32)],
        scratch_shapes=[pltpu.VMEM((T, _LANES), _F32), pltpu.VMEM((T, _LANES), _F32)],
        args=(hq, hq, hq, bias, do), semantics=("arbitrary", "arbitrary"))
    return outs, extra


def _suffix_matrix():
    r = lax.broadcasted_iota(jnp.int32, (_TQ, _TQ), 0)
    c = lax.broadcasted_iota(jnp.int32, (_TQ, _TQ), 1)
    r2 = lax.broadcasted_iota(jnp.int32, (2 * _TQ, _TQ), 0)
    c2 = lax.broadcasted_iota(jnp.int32, (2 * _TQ, _TQ), 1)
    return (r > c).astype(_MXU), c2 - (r2 & (_TQ - 1))


def _suffix_sums(xs, tri):
    n, k = xs[0].shape[0], len(xs)
    his = [x.astype(_MXU) for x in xs]
    los = [(x - h.astype(_F32)).astype(_MXU) for x, h in zip(xs, his)]
    y = _dot(jnp.concatenate(his + los, axis=0), tri)
    return [y[j * n:(j + 1) * n] + y[(k + j) * n:(k + j + 1) * n] for j in range(k)]


def _stick_tiles(tiles, rel, carry_l, tri):
    zs = [_dot_nt(qs, kj) for qs, kj, _, _ in tiles]
    Ls, masks = [], []
    for z, (_, _, jj, _) in zip(zs, tiles):
        nsp = -(jnp.maximum(z, 0.0) + jnp.log(1.0 + jnp.exp(-jnp.abs(z))))
        if isinstance(jj, int):
            mask = (rel < 0) if jj == 0 else None
        else:
            mask = rel < jnp.where(jj == 0, 0, _TQ)
        Ls.append(nsp if mask is None else jnp.where(mask, nsp, 0.0))
        masks.append(mask)
    carry_l = list(carry_l)
    ws = []
    for z, L, suffix, mask, (_, _, _, sub) in zip(zs, Ls, _suffix_sums(Ls, tri), masks, tiles):
        w = jnp.exp(z + L + suffix + carry_l[sub])
        ws.append(w if mask is None else jnp.where(mask, w, 0.0))
        carry_l[sub] = carry_l[sub] + jnp.sum(L, axis=-1, keepdims=True)
    return zs, Ls, ws, masks, carry_l


def _sweep(i, step, zero):
    nsub = _SB_SUBTILES

    def window():
        tiles = [(s, jj) for jj in range(_SB_WINDOW) for s in range(nsub)]
        return tuple((jnp.int32(_SB_WINDOW),) + c for c in step(tiles, [zero] * nsub))

    start = lax.cond(i >= -(-(_SB_WINDOW - 1) // nsub), window, lambda: tuple((jnp.int32(0),) + zero for _ in range(nsub)))
    outs = []
    for s in range(nsub):
        def done(c, s=s):
            return jnp.logical_or(c[0] > nsub * i + s, jnp.max(c[1]) < _EXP_ZERO_BELOW)

        def more(c, s=s):
            carries = [None] * nsub
            carries[s] = c[1:]
            return (c[0] + 1,) + step([(s, c[0])], carries)[s]

        outs.append(lax.while_loop(lambda c, done=done: jnp.logical_not(done(c)), more, start[s]))
    return outs


def _sb_fwd(hq, col0, width, layer, comm=None):
    T = hq.shape[0]
    npair = width // _LANES
    nsub = _SB_SUBTILES
    tq = nsub * _TQ

    def kern(q_ref, k_ref, v_ref, o_ref):
        i = pl.program_id(1)
        masks = _head_masks()
        tri, rel = _suffix_matrix()
        q = _scaled(q_ref[...])
        q2 = [_stack_heads(q[s * _TQ:(s + 1) * _TQ], masks) for s in range(nsub)]

        def step(tiles, carries):
            rows = [pl.ds(pl.multiple_of((nsub * i + s - jj) * _TQ, _TQ), _TQ) for s, jj in tiles]
            cls = [None if c is None else c[0] for c in carries]
            accs = [None if c is None else c[1] for c in carries]
            _, _, ws, _, cls = _stick_tiles([(q2[s], k_ref[r, :], jj, s) for (s, jj), r in zip(tiles, rows)], rel, cls, tri)
            for w, r, (s, _) in zip(ws, rows, tiles):
                accs[s] = accs[s] + _dot(w.astype(_MXU), v_ref[r, :])
            return [None if c is None else (cls[s], accs[s]) for s, c in enumerate(carries)]

        outs = _sweep(i, step, (jnp.zeros((2 * _TQ, 1), _F32), jnp.zeros((2 * _TQ, _LANES), _F32)))
        for s in range(nsub):
            o_ref[s * _TQ:(s + 1) * _TQ, :] = _unstack_heads(outs[s][2], masks)

    outs, extra = _call(
        kern, comm, name=f"stick_attn_fwd_{layer}", grid=(npair, T // tq),
        in_specs=_qkv_specs(T, col0 // _LANES, npair, tq),
        out_specs=[pl.BlockSpec((tq, _LANES), lambda h, i: (i, h))],
        out_shape=[jax.ShapeDtypeStruct((T, width), _F32)], scratch_shapes=[],
        args=(hq, hq, hq), semantics=("arbitrary", "arbitrary"))
    return outs[0], extra


def _sb_bwd(hq, o, do, col0, width, layer, comm=None):
    T = hq.shape[0]
    npair = width // _LANES
    nsub = _SB_SUBTILES
    tq = nsub * _TQ
    nq = T // tq
    scale = _HEAD ** -0.5

    def kern(q_ref, k_ref, v_ref, o_ref, do_ref, dq_ref, dk_ref, dv_ref, dk_acc, dv_acc):
        i = pl.program_id(1)

        @pl.when(i == 0)
        def _():
            dk_acc[...] = jnp.zeros_like(dk_acc)
            dv_acc[...] = jnp.zeros_like(dv_acc)

        masks = _head_masks()
        tri, rel = _suffix_matrix()
        q = _scaled(q_ref[...])
        do_t = do_ref[...]
        prod = do_t.astype(_F32) * o_ref[...]
        part = [slice(s * _TQ, (s + 1) * _TQ) for s in range(nsub)]
        q2 = [_stack_heads(q[p], masks) for p in part]
        do2 = [_stack_heads(do_t[p], masks).astype(_MXU) for p in part]
        dsum = [jnp.sum(_stack_heads(prod[p], masks), axis=-1, keepdims=True) for p in part]

        def step(tiles, carries):
            rows = [pl.ds(pl.multiple_of((nsub * i + s - jj) * _TQ, _TQ), _TQ) for s, jj in tiles]
            kjs = [k_ref[r, :] for r in rows]
            cls, cgs, dqs = ([None if c is None else c[n] for c in carries] for n in range(3))
            zs, Ls, ws, tile_masks, cls = _stick_tiles([(q2[s], kj, jj, s) for (s, jj), kj in zip(tiles, kjs)], rel, cls, tri)
            wbs = [w.astype(_MXU) for w in ws]
            gs = [wb.astype(_F32) * _dot_nt(do2[s], v_ref[r, :]) for wb, r, (s, _) in zip(wbs, rows, tiles)]
            for z, L, g, later, mask, wb, kj, r, (s, _) in zip(zs, Ls, gs, _suffix_sums(gs, tri), tile_masks, wbs, kjs,
                                                               rows, tiles):
                dz = g - jnp.exp(z + L) * (dsum[s] - (later + cgs[s]))
                if mask is not None:
                    dz = jnp.where(mask, dz, 0.0)
                dzb = dz.astype(_MXU)
                dk_acc[r, :] += _dot_tn(dzb, q2[s])
                dv_acc[r, :] += _dot_tn(wb, do2[s])
                dqs[s] = dqs[s] + _dot(dzb, kj)
                cgs[s] = cgs[s] + jnp.sum(g, axis=-1, keepdims=True)
            return [None if c is None else (cls[s], cgs[s], dqs[s]) for s, c in enumerate(carries)]

        zc = jnp.zeros((2 * _TQ, 1), _F32)
        outs = _sweep(i, step, (zc, zc, jnp.zeros((2 * _TQ, _LANES), _F32)))
        for s in range(nsub):
            dq_ref[part[s], :] = (_unstack_heads(outs[s][3], masks) * scale).astype(dq_ref.dtype)

        @pl.when(i == nq - 1)
        def _():
            dk_ref[...] = dk_acc[...].astype(dk_ref.dtype)
            dv_ref[...] = dv_acc[...].astype(dv_ref.dtype)

    tile_spec = pl.BlockSpec((tq, _LANES), lambda h, i: (i, h))
    column = pl.BlockSpec((T, _LANES), lambda h, i: (0, h))
    outs, extra = _call(
        kern, comm, name=f"stick_attn_bwd_{layer}", grid=(npair, nq),
        in_specs=_qkv_specs(T, col0 // _LANES, npair, tq) + [tile_spec, tile_spec],
        out_specs=[tile_spec, column, column],
        out_shape=[jax.ShapeDtypeStruct((T, width), _ACT)] * 3,
        scratch_shapes=[pltpu.VMEM((T, _LANES), _F32), pltpu.VMEM((T, _LANES), _F32)],
        args=(hq, hq, hq, o, do), semantics=("arbitrary", "arbitrary"))
    return outs, extra


_DENSE = ("w_in", "w_proj_a", "w_proj_b", "w_out", "w_ffn_in", "w_ffn_out")
_COL_SHARDED = {"w_in": True, "w_proj_a": True, "w_proj_b": True, "w_out": False, "w_ffn_in": True, "w_ffn_out": False}
_SMALL = ("b_gate", "rel_bias", "ln1_g", "ln1_b", "ln2_g", "ln2_b")


class _Plans:
    def __init__(self, plans=None, own_w_in=None):
        self.plans = plans or {}
        self.own_w_in = own_w_in or {}

    def start(self, key):
        if key not in self.plans:
            return None, None
        return self.plans[key]()

    @staticmethod
    def finish(done, extra):
        if done is not None:
            done(extra)


def _layer_fwd(x, W, small, l, alpha, plans):
    WA = small["rel_bias"].shape[1] * _HEAD
    row = lambda v: v[l].reshape(1, -1)
    if l in plans.own_w_in:
        h, xb, W["w_in"] = _in_proj_gathering(x, plans.own_w_in[l], l)
    else:
        comm, done = plans.start(f"in_proj_{l}")
        (h, xb), extra = _in_proj(x, W["w_in"], l, comm)
        plans.finish(done, extra)
    WB = (h.shape[1] - 2 * x.shape[1] - 3 * WA) // 3
    bias = _bias_tiles(small["rel_bias"][l])
    comm, done = plans.start(f"band_fwd_{l}")
    oa, extra = _attn_a_fwd(h, bias, 0, WA, l, comm)
    plans.finish(done, extra)
    comm, done = plans.start(f"stick_fwd_{l}")
    ob, extra = _sb_fwd(h, 3 * WA, WB, l, comm)
    plans.finish(done, extra)
    comm, done = plans.start(f"mix_fwd_{l}")
    (x1, u1, pre), extra = _mix_fwd(oa, ob, h, x, W["w_proj_a"], W["w_proj_b"], W["w_out"], row(small["b_gate"]),
                                            row(small["ln1_g"]), row(small["ln1_b"]), alpha, l, comm)
    plans.finish(done, extra)
    comm, done = plans.start(f"ffn_fwd_{l}")
    (x2, u2, act, gu, x1b), extra = _ffn_fwd(x1, W["w_ffn_in"], W["w_ffn_out"], row(small["ln2_g"]),
                                             row(small["ln2_b"]), alpha, l, comm)
    plans.finish(done, extra)
    return x2, dict(xb=xb, h=h, bias=bias, oa=oa, ob=ob, x1b=x1b, u1=u1, pre=pre, u2=u2, act=act, gu=gu)


def _layer_bwd(dy_or_target, S, W, small, l, last, alpha, plans, gw):
    D = S["xb"].shape[1]
    WA, WB = S["oa"].shape[1], S["ob"].shape[1]
    row = lambda v: v[l].reshape(1, -1)

    def blocks(g, n):
        return g if _COL_SHARDED[n] else g.reshape(4, g.shape[0] // 4, g.shape[1])

    dx1, du2b, dgu, st2 = _ffn_bwd(S["u2"], dy_or_target, S["gu"], row(small["ln2_g"]), row(small["ln2_b"]),
                                   W["w_ffn_in"], W["w_ffn_out"], alpha, l, last)
    gw["w_ffn_in"] = blocks(_grad_w(S["x1b"], dgu, col_shards=True, name=f"grad_w_ffn_in_{l}")[0], "w_ffn_in")
    gw["w_ffn_out"] = blocks(_grad_w(S["act"], du2b, col_shards=False, name=f"grad_w_ffn_out_{l}")[0], "w_ffn_out")
    du1, du1b, dya, dyb, dhg, doa, dob, st1 = _mix_bwd(S["u1"], dx1, S["oa"], S["ob"], S["h"], W["w_proj_a"],
                                                       W["w_proj_b"], W["w_out"], row(small["b_gate"]),
                                                       row(small["ln1_g"]), l)
    gw["w_out"] = blocks(_grad_w(S["pre"], du1b, col_shards=False, name=f"grad_w_out_{l}")[0], "w_out")
    gw["w_proj_a"] = blocks(_grad_w(S["oa"], dya, col_shards=True, name=f"grad_w_proj_a_{l}")[0], "w_proj_a")
    gw["w_proj_b"] = blocks(_grad_w(S["ob"], dyb, col_shards=True, name=f"grad_w_proj_b_{l}")[0], "w_proj_b")
    comm, done = plans.start(f"band_bwd_{l}")
    (dqa, dka, dva, dbias), extra = _attn_a_bwd(S["h"], S["bias"], doa, 0, WA, l, comm)
    plans.finish(done, extra)
    comm, done = plans.start(f"stick_bwd_{l}")
    (dqb, dkb, dvb), extra = _sb_bwd(S["h"], S["ob"], dob, 3 * WA, WB, l, comm)
    plans.finish(done, extra)
    dh = [dqa, dka, dva, dqb, dkb, dvb, dhg]
    comm, done = plans.start(f"grad_w_in_{l}")
    gw["w_in"], extra = _grad_w_pieces(S["xb"], dh, f"grad_w_in_{l}", comm)
    plans.finish(done, extra)
    comm, done = plans.start(f"in_proj_bwd_{l}")
    dx, extra = _residual_nt(du1, alpha, dh, W["w_in"], f"in_proj_bwd_{l}", comm)
    plans.finish(done, extra)
    gs = dict(b_gate=st1[0], rel_bias=_fold_bias_grad(dbias), ln1_g=st1[1, :D], ln1_b=st1[1, D:],
              ln2_g=st2[0], ln2_b=st2[1])
    return dx, gs, st2[2]


def _local_step(x, target, W, small, plans=None, gws=None):
    depth = len(W)
    alpha = float((2 * depth) ** 0.25)
    plans = plans or _Plans()
    gws = gws if gws is not None else [dict() for _ in range(depth)]
    saved = []
    h = x
    for l in range(depth):
        h, S = _layer_fwd(h, W[l], small, l, alpha, plans)
        saved.append(S)
    gss = [None] * depth
    d = target
    sq = None
    for l in reversed(range(depth)):
        d, gss[l], sq_l = _layer_bwd(d, saved[l], W[l], small, l, l == depth - 1, alpha, plans, gws[l])
        if l == depth - 1:
            sq = sq_l
    return sq, d, gws, gss


def _place():
    return lax.axis_index("x"), lax.axis_index("y"), lax.axis_index("c")


def _remote(src, dst, send_sem, recv_sem, to):
    return pltpu.make_async_remote_copy(src_ref=src, dst_ref=dst, send_sem=send_sem, recv_sem=recv_sem,
                                        device_id=to, device_id_type=_MESH)


def _half(ref, hc):
    kh = ref.shape[0] // 2
    return ref.at[pl.ds(pl.multiple_of(hc * kh, 16), kh), :]


def _gather_plan(blocks, fractions):
    nt = len(blocks)

    def run(step, nsteps, ins, outs, sems):
        send_sems, recv_sems, loc_sems = sems
        x, y, c = _place()
        k = 2 * x + y
        me, sibling = (x, y, c), (x, y, 1 - c)
        chips = [(1 - x, y), (x, 1 - y), (1 - x, 1 - y)]
        chip_k = [2 * cx + cy for cx, cy in chips]

        def ici(t, s, owner_k, to, src=None):
            dst = _half(outs[t].at[owner_k], c)
            return _remote(dst if src is None else src, dst, send_sems.at[t, s], recv_sems.at[t, s], to)

        def passed(t, s, hc, to):
            blk = _half(outs[t].at[chip_k[s]], hc)
            return _remote(blk, blk, send_sems.at[t, 3 + s], recv_sems.at[t, 3 + s], to)

        def local(t):
            return pltpu.make_async_copy(ins[t], outs[t].at[k], loc_sems.at[t])

        @pl.when(step == 0)
        def _():
            for t in range(nt):
                local(t).start()
                for s, chip in enumerate(chips):
                    ici(t, s, k, (*chip, c), src=_half(ins[t], c)).start()

        for t in range(nt):
            @pl.when(step == min(nsteps - 1, int(fractions[t] * nsteps)))
            def _():
                for s in range(3):
                    ici(t, s, chip_k[s], me).wait_recv()
                    passed(t, s, c, sibling).start()

        @pl.when(step == nsteps - 1)
        def _():
            for t in range(nt):
                for s, chip in enumerate(chips):
                    passed(t, s, 1 - c, me).wait_recv()
            for t in range(nt):
                for s, chip in enumerate(chips):
                    ici(t, s, k, (*chip, c), src=_half(ins[t], c)).wait_send()
                    passed(t, s, c, sibling).wait_send()
                local(t).wait()

    return _Comm(blocks, [jax.ShapeDtypeStruct((4,) + b.shape, b.dtype) for b in blocks],
                 [pltpu.SemaphoreType.DMA((nt, 6)), pltpu.SemaphoreType.DMA((nt, 6)), pltpu.SemaphoreType.DMA((nt,))], run)


def _scatter_plan(grads, owners):
    nt = len(grads)
    shapes = [g.shape[1:] if g.ndim == 3 else (g.shape[0], g.shape[1] // 4) for g in grads]

    def run(step, nsteps, ins, outs, sems):
        send_sems, recv_sems, loc_sems = sems
        x, y, c = _place()
        me = 4 * x + 2 * y + c

        def target(r):
            tx = 1 - x if r & 2 else x
            ty = 1 - y if r & 1 else y
            return tx, ty

        def block(t, chip):
            if len(ins[t].shape) == 3:
                return ins[t].at[chip]
            n = shapes[t][1]
            return ins[t].at[:, pl.ds(pl.multiple_of(chip * n, _LANES), n)]

        def send(t, r):
            tx, ty = target(r)
            return _remote(block(t, 2 * tx + ty), outs[t].at[me], send_sems.at[t, r], recv_sems.at[t, 2 * r + c],
                           (tx, ty, owners[t]))

        def local(t):
            return pltpu.make_async_copy(block(t, 2 * x + y), outs[t].at[me], loc_sems.at[t])

        @pl.when(step == 0)
        def _():
            for t in range(nt):
                @pl.when(c == owners[t])
                def _():
                    local(t).start()

                @pl.when(c != owners[t])
                def _():
                    send(t, 0).start()

                for r in range(1, 4):
                    send(t, r).start()

        @pl.when(step == nsteps - 1)
        def _():
            for t in range(nt):
                @pl.when(c == owners[t])
                def _():
                    for r in range(4):
                        sx, sy = target(r)
                        for cs in range(2):
                            if r == 0 and cs == owners[t]:
                                continue
                            src_dev = 4 * sx + 2 * sy + cs
                            _remote(block(t, 0), outs[t].at[src_dev], send_sems.at[t, r], recv_sems.at[t, 2 * r + cs],
                                    (x, y, c)).wait_recv()
                    local(t).wait()

                @pl.when(c != owners[t])
                def _():
                    send(t, 0).wait_send()

                for r in range(1, 4):
                    send(t, r).wait_send()

    return _Comm(grads, [jax.ShapeDtypeStruct((8,) + s, g.dtype) for s, g in zip(shapes, grads)],
                 [pltpu.SemaphoreType.DMA((nt, 4)), pltpu.SemaphoreType.DMA((nt, 8)), pltpu.SemaphoreType.DMA((nt,))], run)


def _share_plan(reduced, owners):
    nt = len(reduced)

    def run(step, nsteps, ins, outs, sems):
        del ins
        send_sems, recv_sems = sems
        x, y, c = _place()

        def give(t, to):
            return _remote(outs[t], outs[t], send_sems.at[t], recv_sems.at[t], to)

        @pl.when(step == 0)
        def _():
            for t in range(nt):
                @pl.when(c == owners[t])
                def _():
                    give(t, (x, y, 1 - c)).start()

        @pl.when(step == nsteps - 1)
        def _():
            for t in range(nt):
                @pl.when(c == owners[t])
                def _():
                    give(t, (x, y, 1 - c)).wait_send()

                @pl.when(c != owners[t])
                def _():
                    give(t, (x, y, c)).wait_recv()

    return _Comm(reduced, [jax.ShapeDtypeStruct(r.shape, r.dtype) for r in reduced],
                 [pltpu.SemaphoreType.DMA((nt,)), pltpu.SemaphoreType.DMA((nt,))], run,
                 aliases={t: t for t in range(nt)})


def _join(a, b):
    ni, no, ns = len(a.inputs), len(a.out_shapes), len(a.sems)

    def run(step, nsteps, ins, outs, sems):
        a.run(step, nsteps, ins[:ni], outs[:no], sems[:ns])
        b.run(step, nsteps, ins[ni:], outs[no:], sems[ns:])

    aliases = dict(a.aliases)
    aliases.update({ni + i: no + o for i, o in b.aliases.items()})
    return _Comm(a.inputs + b.inputs, a.out_shapes + b.out_shapes, a.sems + b.sems, run, aliases)


def _peer(x, y, c, r):
    px = 1 - x if r & 4 else x
    py = 1 - y if r & 2 else y
    pc = 1 - c if r & 1 else c
    return (px, py, pc), 4 * px + 2 * py + pc


def _sum_slots(st, name):
    _, K, n = st.shape
    tr = next(t for t in (256, 128, 64, 32, 16) if K % t == 0)

    def kern(s_ref, o_ref):
        acc = s_ref[0].astype(_F32)
        for d in range(1, 8):
            acc = acc + s_ref[d].astype(_F32)
        o_ref[...] = acc.astype(o_ref.dtype)

    return pl.pallas_call(
        kern, name=name, grid=(K // tr,),
        in_specs=[pl.BlockSpec((8, tr, n), lambda i: (0, i, 0))], out_specs=_rows(tr, n),
        out_shape=jax.ShapeDtypeStruct((K, n), _ACT),
        compiler_params=_cparams("parallel"),
    )(st)


def _all_reduce_small(p):
    R = p.shape[0]

    def body(p_ref, o_ref, stage, send_sems, recv_sems):
        x, y, c = _place()
        me = 4 * x + 2 * y + c
        stage[me] = p_ref[...]
        sent = []
        for r in range(1, 8):
            to, _ = _peer(x, y, c, r)
            cp = _remote(p_ref, stage.at[me], send_sems.at[r - 1], recv_sems.at[r - 1], to)
            cp.start()
            sent.append(cp)
        for r in range(1, 8):
            _, src_dev = _peer(x, y, c, r)
            _remote(p_ref, stage.at[src_dev], send_sems.at[r - 1], recv_sems.at[r - 1], (x, y, c)).wait_recv()
        acc = stage[0]
        for d in range(1, 8):
            acc = acc + stage[d]
        o_ref[...] = acc
        for cp in sent:
            cp.wait_send()

    vm = pl.BlockSpec(memory_space=pltpu.VMEM)
    return pl.pallas_call(
        body, name="all_reduce_small",
        in_specs=[vm], out_specs=vm,
        out_shape=jax.ShapeDtypeStruct((R, _LANES), _F32),
        scratch_shapes=[pltpu.VMEM((8, R, _LANES), _F32), pltpu.SemaphoreType.DMA((7,)), pltpu.SemaphoreType.DMA((7,))],
    )(p)


def _adamw_update(gv, w_ref, m_ref, v_ref, gf_ref, d_ref, nm_ref, nv_ref):
    nm = _B1 * m_ref[...] + (1.0 - _B1) * gv
    nv = _B2 * v_ref[...] + (1.0 - _B2) * (gv * gv)
    m_hat = nm / (1.0 - _B1 ** _STEP)
    v_hat = nv / (1.0 - _B2 ** _STEP)
    gf_ref[...] = gv
    d_ref[...] = -_LR * (m_hat / (jnp.sqrt(v_hat) + _EPS) + _WD * w_ref[...])
    nm_ref[...] = nm
    nv_ref[...] = nv


def _adamw_layers(w, g_layers, m, v, name, comm=None):
    _, K, n = w.shape
    tr = next(t for t in (256, 128, 64, 32, 16) if K % t == 0)

    def kern(w_ref, g0_ref, g1_ref, m_ref, v_ref, *out_refs):
        first = pl.program_id(0) == 0
        gv = jnp.where(first, g0_ref[...].astype(_F32), g1_ref[...].astype(_F32))
        _adamw_update(gv, w_ref, m_ref, v_ref, *out_refs)

    stacked = pl.BlockSpec((None, tr, n), lambda l, i: (l, i, 0))
    layer = pl.BlockSpec((tr, n), lambda l, i: (i, 0))
    outs, extra = _call(
        kern, comm, name=name, grid=(2, K // tr),
        in_specs=[stacked, layer, layer, stacked, stacked], out_specs=[stacked] * 4,
        out_shape=[jax.ShapeDtypeStruct(w.shape, _F32)] * 4, scratch_shapes=[],
        args=(w, g_layers[0], g_layers[1], m, v), semantics=("parallel", "parallel"))
    return tuple(outs), extra


def _adamw(w, g, m, v, name):
    shape = w.shape
    w2, g2, m2, v2 = (a.reshape(-1, shape[-1]) for a in (w, g, m, v))
    R, C = w2.shape
    tr = next((t for t in (256, 128, 64, 32, 16) if R % t == 0), R)

    def kern(w_ref, g_ref, m_ref, v_ref, *out_refs):
        _adamw_update(g_ref[...].astype(_F32), w_ref, m_ref, v_ref, *out_refs)

    outs = pl.pallas_call(
        kern, name=name, grid=(R // tr,),
        in_specs=[_rows(tr, C)] * 4, out_specs=[_rows(tr, C)] * 4,
        out_shape=[jax.ShapeDtypeStruct((R, C), _F32)] * 4,
        compiler_params=_cparams("parallel"),
    )(w2, g2, m2, v2)
    return tuple(o.reshape(shape) for o in outs)


def _pack_small(gss, sq):
    parts = [gss[l][n].reshape(-1) for n in _SMALL for l in range(len(gss))] + [jnp.sum(sq).reshape(1)]
    flat = jnp.concatenate(parts)
    rows = -(-flat.shape[0] // (8 * _LANES)) * 8
    return jnp.pad(flat, (0, rows * _LANES - flat.shape[0])).reshape(rows, _LANES)


def _unpack_small(total, shapes):
    flat = total.reshape(-1)
    out, off = {}, 0
    for n in _SMALL:
        layers = []
        for _ in range(shapes[n][0]):
            size = 1
            for s in shapes[n][1:]:
                size *= s
            layers.append(flat[off:off + size].reshape(shapes[n][1:]))
            off += size
        out[n] = jnp.stack(layers)
    return out, flat[off]


_GATHER = {
    "band_fwd_0": [(0, "w_proj_a"), (0, "w_proj_b"), (0, "w_out"), (0, "w_ffn_out")],
    "stick_fwd_0": [(0, "w_ffn_in"), (1, "w_proj_a"), (1, "w_proj_b"), (1, "w_out")],
    "mix_fwd_0": [(1, "w_ffn_out")],
    "ffn_fwd_0": [(1, "w_in"), (1, "w_ffn_in")],
}
_SCATTER = {
    "band_bwd_1": [(1, "w_ffn_in"), (1, "w_ffn_out")],
    "stick_bwd_1": [(1, "w_proj_a"), (1, "w_proj_b"), (1, "w_out")],
    "band_bwd_0": [(1, "w_in"), (0, "w_ffn_in")],
    "stick_bwd_0": [(0, "w_ffn_out"), (0, "w_proj_a"), (0, "w_proj_b"), (0, "w_out")],
    "in_proj_bwd_0": [(0, "w_in")],
}
_SHARE = {"stick_bwd_1": "band_bwd_1", "band_bwd_0": "stick_bwd_1", "stick_bwd_0": "band_bwd_0", "grad_w_in_0": "stick_bwd_0"}


def _owner(key):
    del key
    return 1


def kernel(x, w_in, b_gate, rel_bias, w_proj_a, w_proj_b, w_out, ln1_g, ln1_b, w_ffn_in, w_ffn_out, ln2_g, ln2_b, loss_target, m_w_in, m_b_gate, m_rel_bias, m_w_proj_a, m_w_proj_b, m_w_out, m_ln1_g, m_ln1_b, m_w_ffn_in, m_w_ffn_out, m_ln2_g, m_ln2_b, v_w_in, v_b_gate, v_rel_bias, v_w_proj_a, v_w_proj_b, v_w_out, v_ln1_g, v_ln1_b, v_w_ffn_in, v_w_ffn_out, v_ln2_g, v_ln2_b):
    names = ("w_in", "b_gate", "rel_bias", "w_proj_a", "w_proj_b", "w_out", "ln1_g", "ln1_b", "w_ffn_in", "w_ffn_out", "ln2_g", "ln2_b")
    w = dict(zip(names, (w_in, b_gate, rel_bias, w_proj_a, w_proj_b, w_out, ln1_g, ln1_b, w_ffn_in, w_ffn_out, ln2_g, ln2_b)))
    m = dict(zip(names, (m_w_in, m_b_gate, m_rel_bias, m_w_proj_a, m_w_proj_b, m_w_out, m_ln1_g, m_ln1_b, m_w_ffn_in, m_w_ffn_out, m_ln2_g, m_ln2_b)))
    v = dict(zip(names, (v_w_in, v_b_gate, v_rel_bias, v_w_proj_a, v_w_proj_b, v_w_out, v_ln1_g, v_ln1_b, v_w_ffn_in, v_w_ffn_out, v_ln2_g, v_ln2_b)))
    T, D = x.shape[-2], x.shape[-1]
    assert w_in.shape[0] == 2, "the exchange schedule below is written for two layers"

    mine = [{n: w[n][l].astype(_MXU) for n in _DENSE} for l in range(2)]
    W = [dict(), dict()]
    gws = [dict(), dict()]
    slots, final = {}, {}

    def gather(keys):
        sizes = [mine[l][n].size for l, n in keys]
        passed, fractions = 0, []
        for s in sizes:
            passed += s
            fractions.append(0.15 + 0.6 * passed / sum(sizes))

        def done(outs):
            for (l, n), o in zip(keys, outs):
                W[l][n] = o
        return _gather_plan([mine[l][n] for l, n in keys], fractions), done

    def scatter(keys):
        comm = _scatter_plan([gws[l][n] for l, n in keys], [_owner(key) for key in keys])
        return comm, lambda outs: slots.update(zip(keys, outs))

    def share(keys):
        reduced = [_sum_slots(slots[key], f"sum_grad_{key[1]}_{key[0]}") for key in keys]
        comm = _share_plan(reduced, [_owner(key) for key in keys])
        return comm, lambda outs: final.update(zip(keys, outs))

    def both(first, second):
        (ca, da), (cb, db) = first, second
        na = len(ca.out_shapes)
        return _join(ca, cb), lambda outs: (da(outs[:na]), db(outs[na:]))

    plans = {key: functools.partial(gather, keys) for key, keys in _GATHER.items()}
    for key, keys in _SCATTER.items():
        plans[key] = functools.partial(scatter, keys)
    for key, scattered_under in _SHARE.items():
        handed = functools.partial(share, _SCATTER[scattered_under])
        carried = plans.get(key)
        plans[key] = handed if carried is None else (lambda carried=carried, handed=handed: both(carried(), handed()))
    small = {n: w[n] for n in _SMALL}
    sq, dx, _, gss = _local_step(x.reshape(T, D), loss_target.reshape(T, D), W, small,
                                  _Plans(plans, {0: mine[0]["w_in"]}), gws)

    total = _all_reduce_small(_pack_small(gss, sq))
    small_grads, sq_all = _unpack_small(total, {n: w[n].shape for n in _SMALL})
    loss = 0.5 * sq_all / D

    last, cover = _SCATTER["in_proj_bwd_0"], "w_ffn_in"
    assert all(n != cover for _, n in last)
    grad, delta, new_m, new_v = {}, {}, {}, {}
    for n in (cover, *[n for n in names if n != cover]):
        if n in _DENSE:
            comm, done = share(last) if n == cover else (None, None)
            updated, extra = _adamw_layers(w[n], [final[(l, n)] for l in range(2)], m[n], v[n], f"adamw_{n}", comm)
            _Plans.finish(done, extra)
        else:
            updated = _adamw(w[n], small_grads[n], m[n], v[n], f"adamw_{n}")
        grad[n], delta[n], new_m[n], new_v[n] = updated
    return (loss, dx.reshape(x.shape), *[grad[n] for n in names], *[delta[n] for n in names],
            *[new_m[n] for n in names], *[new_v[n] for n in names])
```

```python
import functools

import jax
import jax.numpy as jnp
from jax import lax
from jax.experimental import pallas as pl
from jax.experimental.pallas import tpu as pltpu

_MXU = jnp.bfloat16
_ACT = jnp.bfloat16
_F32 = jnp.float32

_HEAD = 64
_CHUNK = 64
_LANES = 128
_TQ = 128
_BAND_TILES = 5
_BIAS_TILES = 9
_REL_CLIP = 256
_LN_EPS = 1e-5
_MASKED = -1e30
_EXP_ZERO_BELOW = -87.34
_SB_WINDOW = 2
_SB_SUBTILES = 4
_BAND_SUBTILES = 8
_VMEM_LIMIT = 56 * 1024 * 1024
_GRAD_ACC_BYTES = 12 * 1024 * 1024

_LR, _B1, _B2, _EPS, _WD, _STEP = 0.001, 0.9, 0.999, 1e-08, 0.01, 10

_MESH = pl.DeviceIdType.MESH


def _dot(a, b):
    return jnp.dot(a, b, preferred_element_type=_F32)


def _dot_nt(a, b):
    return lax.dot_general(a, b, (((1,), (1,)), ((), ())), preferred_element_type=_F32)


def _dot_tn(a, b):
    return lax.dot_general(a, b, (((0,), (0,)), ((), ())), preferred_element_type=_F32)


def _cparams(*sem):
    return pltpu.CompilerParams(dimension_semantics=sem, vmem_limit_bytes=_VMEM_LIMIT)


def _rows(t, c):
    return pl.BlockSpec((t, c), lambda i: (i, 0))


def _whole(shape):
    return pl.BlockSpec(shape, lambda i: tuple(0 for _ in shape))


_ANY = pl.BlockSpec(memory_space=pl.ANY)


def _load_cols(w_hbm, w_vmem, sem):
    n = w_hbm.shape[-1]
    cps = [pltpu.make_async_copy(w_hbm.at[k], w_vmem.at[:, pl.ds(k * n, n)], sem.at[k]) for k in range(4)]
    for cp in cps:
        cp.start()
    for cp in cps:
        cp.wait()


def _load_rows(w_hbm, w_vmem, sem):
    r = w_hbm.shape[-2]
    cps = [pltpu.make_async_copy(w_hbm.at[k], w_vmem.at[pl.ds(k * r, r), :], sem.at[k]) for k in range(4)]
    for cp in cps:
        cp.start()
    for cp in cps:
        cp.wait()


def _ln_stats(u):
    mu = jnp.mean(u, axis=-1, keepdims=True)
    xc = u - mu
    var = jnp.mean(xc * xc, axis=-1, keepdims=True)
    rstd = lax.rsqrt(var + _LN_EPS)
    return xc * rstd, rstd


def _ln_bwd(u, dy, gamma):
    xhat, rstd = _ln_stats(u)
    dxh = dy * gamma
    m1 = jnp.mean(dxh, axis=-1, keepdims=True)
    m2 = jnp.mean(dxh * xhat, axis=-1, keepdims=True)
    du = rstd * (dxh - m1 - xhat * m2)
    return du, jnp.sum(dy * xhat, axis=0, keepdims=True), jnp.sum(dy, axis=0, keepdims=True), xhat


def _divisor_tile(n, cap):
    best = None
    for t in range(_LANES, min(n, cap) + 1, _LANES):
        if n % t == 0:
            best = t
    return best or n


class _Comm:
    def __init__(self, inputs, out_shapes, sems, run, aliases=None):
        self.inputs, self.out_shapes, self.sems, self.run = list(inputs), list(out_shapes), list(sems), run
        self.aliases = aliases or {}


def _call(kern, comm, *, name, grid, in_specs, out_specs, out_shape, scratch_shapes, args, semantics):
    in_specs, out_specs, out_shape, scratch_shapes = list(in_specs), list(out_specs), list(out_shape), list(scratch_shapes)
    if comm is None:
        outs = pl.pallas_call(kern, name=name, grid=grid, in_specs=in_specs, out_specs=out_specs, out_shape=out_shape,
                              scratch_shapes=scratch_shapes, compiler_params=_cparams(*semantics))(*args)
        return list(outs), []
    n_in, n_out, n_scr = len(in_specs), len(out_specs), len(scratch_shapes)
    ci, co = len(comm.inputs), len(comm.out_shapes)
    nsteps = functools.reduce(lambda a, b: a * b, grid, 1)

    def fused(*refs):
        a, b = n_in, n_in + ci
        c, d = b + n_out, b + n_out + co
        e = d + n_scr
        step = pl.program_id(0)
        for ax in range(1, len(grid)):
            step = step * grid[ax] + pl.program_id(ax)
        comm.run(step, nsteps, refs[a:b], refs[c:d], refs[e:])
        kern(*refs[:a], *refs[b:c], *refs[d:e])

    outs = pl.pallas_call(
        fused, name=name, grid=grid, in_specs=in_specs + [_ANY] * ci, out_specs=out_specs + [_ANY] * co,
        out_shape=out_shape + comm.out_shapes, scratch_shapes=scratch_shapes + comm.sems,
        input_output_aliases={n_in + i: n_out + o for i, o in comm.aliases.items()},
        compiler_params=_cparams(*("arbitrary" for _ in grid)))(*args, *comm.inputs)
    return list(outs[:n_out]), list(outs[n_out:])


def _comm_only(comm, name):
    def body(*refs):
        ci, co = len(comm.inputs), len(comm.out_shapes)
        comm.run(0, 1, refs[:ci], refs[ci:ci + co], refs[ci + co:])

    outs = pl.pallas_call(body, name=name, in_specs=[_ANY] * len(comm.inputs), out_specs=[_ANY] * len(comm.out_shapes),
                          out_shape=comm.out_shapes, scratch_shapes=comm.sems,
                          input_output_aliases=dict(comm.aliases))(*comm.inputs)
    return list(outs)


def _in_proj(x, w_in, layer, comm=None):
    T, D = x.shape
    N = 4 * w_in.shape[-1]
    tm = 512

    def kern(x_ref, w_hbm, h_ref, xb_ref, w_v, sem):
        @pl.when(pl.program_id(0) == 0)
        def _():
            _load_cols(w_hbm, w_v, sem)

        xb = x_ref[...].astype(_MXU)
        h_ref[...] = _dot(xb, w_v[...]).astype(h_ref.dtype)
        xb_ref[...] = xb.astype(xb_ref.dtype)

    return _call(
        kern, comm, name=f"in_proj_{layer}", grid=(T // tm,),
        in_specs=[_rows(tm, D), _ANY],
        out_specs=[_rows(tm, N), _rows(tm, D)],
        out_shape=[jax.ShapeDtypeStruct((T, N), _ACT), jax.ShapeDtypeStruct((T, D), _ACT)],
        scratch_shapes=[pltpu.VMEM((D, N), w_in.dtype), pltpu.SemaphoreType.DMA((4,))],
        args=(x, w_in), semantics=("arbitrary",))


def _in_proj_gathering(x, block, layer):
    T, D = x.shape
    n = block.shape[1]
    tm = min(T, 1024)
    nrows = T // tm
    pass_steps = [int(f * nrows) for f in (0.6, 1.0, 1.7)]
    px, py, _ = _place()
    order = jnp.stack([2 * px + py, 2 * (1 - px) + py, 2 * px + (1 - py), 2 * (1 - px) + (1 - py)]).astype(jnp.int32)

    def kern(order_ref, x_ref, blk_hbm, h_ref, xb_ref, w_hbm, w_v, send_sems, recv_sems, loc_sem, load_sem):
        del order_ref
        step = pl.program_id(0) * nrows + pl.program_id(1)
        x_, y_, c = _place()
        k = 2 * x_ + y_
        me, sibling = (x_, y_, c), (x_, y_, 1 - c)
        chips = [(1 - x_, y_), (x_, 1 - y_), (1 - x_, 1 - y_)]
        chip_k = [2 * cx + cy for cx, cy in chips]

        def ici(s, owner_k, to, src=None):
            dst = _half(w_hbm.at[owner_k], c)
            return _remote(dst if src is None else src, dst, send_sems.at[s], recv_sems.at[s], to)

        def passed(s, hc, to):
            blk = _half(w_hbm.at[chip_k[s]], hc)
            return _remote(blk, blk, send_sems.at[3 + s], recv_sems.at[3 + s], to)

        local = pltpu.make_async_copy(blk_hbm, w_hbm.at[k], loc_sem.at[0])

        def load(src):
            cp = pltpu.make_async_copy(src, w_v, load_sem.at[0])
            cp.start()
            cp.wait()

        @pl.when(step == 0)
        def _():
            for s, chip in enumerate(chips):
                ici(s, k, (*chip, c), src=_half(blk_hbm, c)).start()
            local.start()
            load(blk_hbm)

        for s in range(3):
            @pl.when(step == pass_steps[s])
            def _():
                ici(s, chip_k[s], me).wait_recv()
                passed(s, c, sibling).start()

            @pl.when(step == (s + 1) * nrows)
            def _():
                passed(s, 1 - c, me).wait_recv()
                load(w_hbm.at[chip_k[s]])

        xb = x_ref[...].astype(_MXU)
        h_ref[...] = _dot(xb, w_v[...]).astype(h_ref.dtype)

        @pl.when(pl.program_id(0) == 0)
        def _():
            xb_ref[...] = xb.astype(xb_ref.dtype)

        @pl.when(step == 4 * nrows - 1)
        def _():
            for s, chip in enumerate(chips):
                ici(s, k, (*chip, c), src=_half(blk_hbm, c)).wait_send()
                passed(s, c, sibling).wait_send()
            local.wait()

    assert all(pass_steps[s] <= (s + 1) * nrows for s in range(3))
    h, xb, w_in = pl.pallas_call(
        kern, name=f"in_proj_{layer}",
        grid_spec=pltpu.PrefetchScalarGridSpec(
            num_scalar_prefetch=1, grid=(4, nrows),
            in_specs=[pl.BlockSpec((tm, D), lambda j, i, o: (i, 0)), _ANY],
            out_specs=[pl.BlockSpec((tm, n), lambda j, i, o: (i, o[j])),
                       pl.BlockSpec((tm, D), lambda j, i, o: (jnp.where(j == 0, i, nrows - 1), 0)), _ANY],
            scratch_shapes=[pltpu.VMEM((D, n), block.dtype), pltpu.SemaphoreType.DMA((6,)),
                            pltpu.SemaphoreType.DMA((6,)), pltpu.SemaphoreType.DMA((1,)), pltpu.SemaphoreType.DMA((1,))]),
        out_shape=[jax.ShapeDtypeStruct((T, 4 * n), _ACT), jax.ShapeDtypeStruct((T, D), _ACT),
                   jax.ShapeDtypeStruct((4,) + block.shape, block.dtype)],
        compiler_params=_cparams("arbitrary", "arbitrary"))(order, x, block)
    return h, xb, w_in


def _gate_specs(h, tm, D):
    first = (h.shape[1] - 2 * D) // D
    assert first * D + 2 * D == h.shape[1]
    return [pl.BlockSpec((tm, D), lambda i: (i, first)), pl.BlockSpec((tm, D), lambda i: (i, first + 1))]


def _mix_fwd(oa, ob, h, x, wpa, wpb, wo, bg, gamma, beta, alpha, layer, comm=None):
    T, D = x.shape
    WA, WB = oa.shape[1], ob.shape[1]
    tm = 512

    def kern(oa_ref, ob_ref, hga_ref, hgb_ref, x_ref, bg_ref, g_ref, b_ref, wpa_h, wpb_h, wo_h,
             x1_ref, u1_ref, pre_ref, wpa_v, wpb_v, wo_v, sa, sb, so):
        @pl.when(pl.program_id(0) == 0)
        def _():
            _load_cols(wpa_h, wpa_v, sa)
            _load_cols(wpb_h, wpb_v, sb)
            _load_rows(wo_h, wo_v, so)

        ya = _dot(oa_ref[...].astype(_MXU), wpa_v[...])
        yb = _dot(ob_ref[...].astype(_MXU), wpb_v[...])
        bgv = bg_ref[...]
        ga = jax.nn.sigmoid(hga_ref[...].astype(_F32) + bgv[:, :D])
        gb = jax.nn.sigmoid(hgb_ref[...].astype(_F32) + bgv[:, D:])
        pre = ga * ya + gb * yb
        mix = _dot(pre.astype(_MXU), wo_v[...])
        u = alpha * x_ref[...] + mix
        xhat, _ = _ln_stats(u)
        x1_ref[...] = xhat * g_ref[...] + b_ref[...]
        u1_ref[...] = u
        pre_ref[...] = pre.astype(pre_ref.dtype)

    return _call(
        kern, comm, name=f"mix_fwd_{layer}", grid=(T // tm,),
        in_specs=[_rows(tm, WA), _rows(tm, WB), *_gate_specs(h, tm, D), _rows(tm, D),
                  _whole((1, 2 * D)), _whole((1, D)), _whole((1, D)), _ANY, _ANY, _ANY],
        out_specs=[_rows(tm, D)] * 3,
        out_shape=[jax.ShapeDtypeStruct((T, D), _F32), jax.ShapeDtypeStruct((T, D), _F32),
                   jax.ShapeDtypeStruct((T, D), _ACT)],
        scratch_shapes=[pltpu.VMEM((WA, D), wpa.dtype), pltpu.VMEM((WB, D), wpb.dtype), pltpu.VMEM((D, D), wo.dtype),
                        pltpu.SemaphoreType.DMA((4,)), pltpu.SemaphoreType.DMA((4,)), pltpu.SemaphoreType.DMA((4,))],
        args=(oa, ob, h, h, x, bg, gamma, beta, wpa, wpb, wo), semantics=("arbitrary",))


def _ffn_fwd(x1, wfi, wfo, gamma, beta, alpha, layer, comm=None):
    T, D = x1.shape
    F2 = 4 * wfi.shape[-1]
    F = F2 // 2
    tm = 512
    fc = F // 2

    def kern(x_ref, g_ref, b_ref, wi_h, wo_h, x2_ref, u2_ref, act_ref, gu_ref, xb_ref, wi_v, wo_v, si, so):
        @pl.when(pl.program_id(0) == 0)
        def _():
            _load_cols(wi_h, wi_v, si)
            _load_rows(wo_h, wo_v, so)

        x = x_ref[...]
        xb = x.astype(_MXU)
        xb_ref[...] = xb.astype(xb_ref.dtype)
        ffn = jnp.zeros((tm, D), _F32)
        for c in range(2):
            g = _dot(xb, wi_v[:, c * fc:(c + 1) * fc])
            u = _dot(xb, wi_v[:, F + c * fc:F + (c + 1) * fc])
            act = g * jax.nn.sigmoid(g) * u
            ab = act.astype(_MXU)
            ffn = ffn + _dot(ab, wo_v[c * fc:(c + 1) * fc, :])
            act_ref[:, c * fc:(c + 1) * fc] = ab.astype(act_ref.dtype)
            gu_ref[:, c * fc:(c + 1) * fc] = g.astype(gu_ref.dtype)
            gu_ref[:, F + c * fc:F + (c + 1) * fc] = u.astype(gu_ref.dtype)
        uu = alpha * x + ffn
        xhat, _ = _ln_stats(uu)
        x2_ref[...] = xhat * g_ref[...] + b_ref[...]
        u2_ref[...] = uu

    return _call(
        kern, comm, name=f"ffn_fwd_{layer}", grid=(T // tm,),
        in_specs=[_rows(tm, D), _whole((1, D)), _whole((1, D)), _ANY, _ANY],
        out_specs=[_rows(tm, D), _rows(tm, D), _rows(tm, F), _rows(tm, F2), _rows(tm, D)],
        out_shape=[jax.ShapeDtypeStruct((T, D), _F32), jax.ShapeDtypeStruct((T, D), _F32),
                   jax.ShapeDtypeStruct((T, F), _ACT), jax.ShapeDtypeStruct((T, F2), _ACT),
                   jax.ShapeDtypeStruct((T, D), _ACT)],
        scratch_shapes=[pltpu.VMEM((D, F2), wfi.dtype), pltpu.VMEM((F, D), wfo.dtype),
                        pltpu.SemaphoreType.DMA((4,)), pltpu.SemaphoreType.DMA((4,))],
        args=(x1, gamma, beta, wfi, wfo), semantics=("arbitrary",))


def _ffn_bwd(u2, dy_or_target, gu, gamma, beta, wfi, wfo, alpha, layer, last):
    T, D = u2.shape
    F2 = gu.shape[1]
    F = F2 // 2
    tm = 256
    fc = F // 2

    def kern(u_ref, dy_ref, gu_ref, g_ref, b_ref, wi_h, wo_h, dx_ref, dub_ref, dgu_ref, st_ref, wi_v, wo_v, si, so):
        @pl.when(pl.program_id(0) == 0)
        def _():
            _load_cols(wi_h, wi_v, si)
            _load_rows(wo_h, wo_v, so)
            st_ref[...] = jnp.zeros_like(st_ref)

        gam = g_ref[...]
        u = u_ref[...]
        if last:
            xhat0, _ = _ln_stats(u)
            err = xhat0 * gam + b_ref[...] - dy_ref[...]
            dy = err * (1.0 / D)
            st_ref[2:3, :] += jnp.sum(err * err, axis=0, keepdims=True)
        else:
            dy = dy_ref[...]
        du, dgam, dbet, _ = _ln_bwd(u, dy, gam)
        st_ref[0:1, :] += dgam
        st_ref[1:2, :] += dbet
        dub = du.astype(_MXU)
        dub_ref[...] = dub.astype(dub_ref.dtype)
        dx = alpha * du
        for c in range(2):
            dact = _dot_nt(dub, wo_v[c * fc:(c + 1) * fc, :])
            g = gu_ref[:, c * fc:(c + 1) * fc].astype(_F32)
            uu = gu_ref[:, F + c * fc:F + (c + 1) * fc].astype(_F32)
            sg = jax.nn.sigmoid(g)
            dg = (dact * uu * (sg * (1.0 + g * (1.0 - sg)))).astype(_MXU)
            dup = (dact * (g * sg)).astype(_MXU)
            dgu_ref[:, c * fc:(c + 1) * fc] = dg.astype(dgu_ref.dtype)
            dgu_ref[:, F + c * fc:F + (c + 1) * fc] = dup.astype(dgu_ref.dtype)
            dx = dx + _dot_nt(dg, wi_v[:, c * fc:(c + 1) * fc]) + _dot_nt(dup, wi_v[:, F + c * fc:F + (c + 1) * fc])
        dx_ref[...] = dx

    return pl.pallas_call(
        kern, name=f"ffn_bwd_{layer}", grid=(T // tm,),
        in_specs=[_rows(tm, D), _rows(tm, D), _rows(tm, F2), _whole((1, D)), _whole((1, D)), _ANY, _ANY],
        out_specs=[_rows(tm, D), _rows(tm, D), _rows(tm, F2), _whole((8, D))],
        out_shape=[jax.ShapeDtypeStruct((T, D), _F32), jax.ShapeDtypeStruct((T, D), _ACT),
                   jax.ShapeDtypeStruct((T, F2), _ACT), jax.ShapeDtypeStruct((8, D), _F32)],
        scratch_shapes=[pltpu.VMEM((D, F2), wfi.dtype), pltpu.VMEM((F, D), wfo.dtype),
                        pltpu.SemaphoreType.DMA((4,)), pltpu.SemaphoreType.DMA((4,))],
        compiler_params=_cparams("arbitrary"),
    )(u2, dy_or_target, gu, gamma, beta, wfi, wfo)


def _residual_nt(res, res_scale, pieces, w, name, comm=None):
    T, K = res.shape
    widths = [p.shape[1] for p in pieces]
    N = sum(widths)
    tm = 512

    def kern(r_ref, *refs):
        d_refs, (w_hbm, o_ref, w_v, sem) = refs[:len(pieces)], refs[len(pieces):]

        @pl.when(pl.program_id(0) == 0)
        def _():
            _load_cols(w_hbm, w_v, sem)

        acc = res_scale * r_ref[...]
        off = 0
        for d_ref, width in zip(d_refs, widths):
            acc = acc + _dot_nt(d_ref[...].astype(_MXU), w_v[:, off:off + width])
            off += width
        o_ref[...] = acc

    outs, extra = _call(
        kern, comm, name=name, grid=(T // tm,),
        in_specs=[_rows(tm, K)] + [_rows(tm, width) for width in widths] + [_ANY], out_specs=[_rows(tm, K)],
        out_shape=[jax.ShapeDtypeStruct((T, K), _F32)],
        scratch_shapes=[pltpu.VMEM((K, N), w.dtype), pltpu.SemaphoreType.DMA((4,))],
        args=(res, *pieces, w), semantics=("arbitrary",))
    return outs[0], extra


def _grad_w_pieces(a, pieces, name, comm=None):
    T, M = a.shape
    widths = [p.shape[1] for p in pieces]
    bw = functools.reduce(_gcd, widths + [512])
    first = [sum(widths[:p]) // bw for p in range(len(pieces))]
    count = [width // bw for width in widths]
    N = sum(widths)
    tk = 1024 if T % 1024 == 0 else 512
    nk = T // tk

    def kern(a_ref, *refs):
        b_refs, (o_ref, acc) = refs[:len(pieces)], refs[len(pieces):]
        j, k = pl.program_id(0), pl.program_id(1)

        @pl.when(k == 0)
        def _():
            acc[...] = jnp.zeros_like(acc)

        for b_ref, start, blocks in zip(b_refs, first, count):
            @pl.when(jnp.logical_and(j >= start, j < start + blocks))
            def _():
                acc[...] += _dot_tn(a_ref[...].astype(_MXU), b_ref[...].astype(_MXU))

        @pl.when(k == nk - 1)
        def _():
            o_ref[...] = acc[...].astype(o_ref.dtype)

    def piece_spec(start, blocks):
        def index(j, k):
            mine = jnp.logical_and(j >= start, j < start + blocks)
            return jnp.where(mine, k, 0), jnp.where(mine, j - start, 0)
        return pl.BlockSpec((tk, bw), index)

    outs, extra = _call(
        kern, comm, name=name, grid=(N // bw, nk),
        in_specs=[pl.BlockSpec((tk, M), lambda j, k: (k, 0))] + [piece_spec(s, c) for s, c in zip(first, count)],
        out_specs=[pl.BlockSpec((M, bw), lambda j, k: (0, j))],
        out_shape=[jax.ShapeDtypeStruct((M, N), _ACT)], scratch_shapes=[pltpu.VMEM((M, bw), _F32)],
        args=(a, *pieces), semantics=("parallel", "arbitrary"))
    return outs[0], extra


def _gcd(a, b):
    while b:
        a, b = b, a % b
    return a


def _mix_bwd(u1, dx1, oa, ob, h, wpa, wpb, wo, bg, gamma, layer):
    T, D = u1.shape
    WA, WB = wpa.shape[-2], wpb.shape[-2]
    tm = 512

    def kern(u_ref, dx_ref, oa_ref, ob_ref, hga_ref, hgb_ref, bg_ref, g_ref, wpa_h, wpb_h, wo_h,
             du_ref, dub_ref, dya_ref, dyb_ref, dhg_ref, doa_ref, dob_ref, st_ref,
             wpa_v, wpb_v, wo_v, sa, sb, so):
        @pl.when(pl.program_id(0) == 0)
        def _():
            _load_cols(wpa_h, wpa_v, sa)
            _load_cols(wpb_h, wpb_v, sb)
            _load_rows(wo_h, wo_v, so)
            st_ref[...] = jnp.zeros_like(st_ref)

        du, dgam, dbet, _ = _ln_bwd(u_ref[...], dx_ref[...], g_ref[...])
        st_ref[1:2, :D] += dgam
        st_ref[1:2, D:] += dbet
        du_ref[...] = du
        dub = du.astype(_MXU)
        dub_ref[...] = dub.astype(dub_ref.dtype)
        dpre = _dot_nt(dub, wo_v[...])
        bgv = bg_ref[...]
        ga = jax.nn.sigmoid(hga_ref[...].astype(_F32) + bgv[:, :D])
        gb = jax.nn.sigmoid(hgb_ref[...].astype(_F32) + bgv[:, D:])
        dya = (dpre * ga).astype(_MXU)
        dyb = (dpre * gb).astype(_MXU)
        dsa = dpre * _dot(oa_ref[...].astype(_MXU), wpa_v[...]) * (ga * (1.0 - ga))
        dsb = dpre * _dot(ob_ref[...].astype(_MXU), wpb_v[...]) * (gb * (1.0 - gb))
        st_ref[0:1, :D] += jnp.sum(dsa, axis=0, keepdims=True)
        st_ref[0:1, D:] += jnp.sum(dsb, axis=0, keepdims=True)
        dya_ref[...] = dya.astype(dya_ref.dtype)
        dyb_ref[...] = dyb.astype(dyb_ref.dtype)
        dhg_ref[:, :D] = dsa.astype(dhg_ref.dtype)
        dhg_ref[:, D:] = dsb.astype(dhg_ref.dtype)
        doa_ref[...] = _dot_nt(dya, wpa_v[...]).astype(doa_ref.dtype)
        dob_ref[...] = _dot_nt(dyb, wpb_v[...]).astype(dob_ref.dtype)

    return pl.pallas_call(
        kern, name=f"mix_bwd_{layer}", grid=(T // tm,),
        in_specs=[_rows(tm, D), _rows(tm, D), _rows(tm, WA), _rows(tm, WB), *_gate_specs(h, tm, D), _whole((1, 2 * D)),
                  _whole((1, D)), _ANY, _ANY, _ANY],
        out_specs=[_rows(tm, D)] * 4 + [_rows(tm, 2 * D), _rows(tm, WA), _rows(tm, WB), _whole((8, 2 * D))],
        out_shape=[jax.ShapeDtypeStruct((T, D), _F32)] + [jax.ShapeDtypeStruct((T, D), _ACT)] * 3
        + [jax.ShapeDtypeStruct((T, 2 * D), _ACT), jax.ShapeDtypeStruct((T, WA), _ACT),
           jax.ShapeDtypeStruct((T, WB), _ACT), jax.ShapeDtypeStruct((8, 2 * D), _F32)],
        scratch_shapes=[pltpu.VMEM((WA, D), wpa.dtype), pltpu.VMEM((WB, D), wpb.dtype), pltpu.VMEM((D, D), wo.dtype),
                        pltpu.SemaphoreType.DMA((4,)), pltpu.SemaphoreType.DMA((4,)), pltpu.SemaphoreType.DMA((4,))],
        compiler_params=_cparams("arbitrary"),
    )(u1, dx1, oa, ob, h, h, bg, gamma, wpa, wpb, wo)


def _grad_w(a, b, *, col_shards, name, comm=None):
    T, M = a.shape
    N = b.shape[1]
    tk = 1024 if T % 1024 == 0 else 512
    n = N // 4 if col_shards else N
    whole = M * N * 4 <= _GRAD_ACC_BYTES
    tn = N if whole else (n if col_shards else _divisor_tile(N, _GRAD_ACC_BYTES // (4 * M)))
    nk = T // tk

    def kern(a_ref, b_ref, o_ref, acc):
        k = pl.program_id(1)

        @pl.when(k == 0)
        def _():
            acc[...] = jnp.zeros_like(acc)

        acc[...] += _dot_tn(a_ref[...].astype(_MXU), b_ref[...].astype(_MXU))

        @pl.when(k == nk - 1)
        def _():
            if col_shards and whole:
                for s in range(4):
                    o_ref[s] = acc[:, s * n:(s + 1) * n].astype(o_ref.dtype)
            else:
                o_ref[...] = acc[...].astype(o_ref.dtype)

    if col_shards:
        out_spec = (pl.BlockSpec((4, M, n), lambda j, k: (0, 0, 0)) if whole
                    else pl.BlockSpec((None, M, n), lambda j, k: (j, 0, 0)))
        out_shape = jax.ShapeDtypeStruct((4, M, n), _ACT)
    else:
        out_spec = pl.BlockSpec((M, tn), lambda j, k: (0, j))
        out_shape = jax.ShapeDtypeStruct((M, N), _ACT)
    outs, extra = _call(
        kern, comm, name=name, grid=(N // tn, nk),
        in_specs=[pl.BlockSpec((tk, M), lambda j, k: (k, 0)), pl.BlockSpec((tk, tn), lambda j, k: (k, j))],
        out_specs=[out_spec], out_shape=[out_shape], scratch_shapes=[pltpu.VMEM((M, tn), _F32)],
        args=(a, b), semantics=("parallel", "arbitrary"))
    return outs[0], extra


def _bias_tiles(rel):
    H = rel.shape[0]
    span = _TQ * _BAND_TILES - 1
    edge = span - _REL_CLIP
    gvec = jnp.concatenate([jnp.broadcast_to(rel[:, :1], (H, edge)), rel, jnp.broadcast_to(rel[:, -1:], (H, edge))], axis=1)
    width = _BIAS_TILES * _TQ
    period = width + _TQ
    tiled = jnp.broadcast_to(jnp.pad(gvec[:, ::-1], ((0, 0), (0, 1)))[:, None, :], (H, _TQ, period))
    rows = tiled.reshape(H, _TQ * period)[:, :_TQ * (period - 1)].reshape(H, _TQ, period - 1)[:, :, _TQ - 1:]
    r = jnp.arange(_TQ)[:, None]
    u = jnp.arange(width)[None, :]
    d = 4 * _TQ + r - u
    rm = r % _CHUNK
    valid = (d >= rm - (_CHUNK - 1)) & (d <= rm + 8 * _CHUNK)
    tiles = jnp.where(valid[None], rows, _MASKED)
    return tiles.reshape(H // 2, 2 * _TQ, width)


def _bias_tile(b_ref, ub):
    return b_ref.at[:, pl.ds(pl.multiple_of(ub * _TQ, _TQ), _TQ)]


def _fold_bias_grad(db):
    H = 2 * db.shape[0]
    width = _BIAS_TILES * _TQ
    period = width + _TQ
    x = jnp.pad(db.reshape(H, _TQ, width), ((0, 0), (0, 0), (_TQ - 1, 0)))
    skew = jnp.pad(x.reshape(H, _TQ * (period - 1)), ((0, 0), (0, _TQ))).reshape(H, _TQ, period)
    dg = skew.sum(axis=1)[:, :period - 1][:, ::-1]
    span = _TQ * _BAND_TILES - 1
    edge = span - _REL_CLIP
    mid = dg[:, edge:edge + 2 * _REL_CLIP + 1]
    lo = dg[:, :edge].sum(axis=1)
    hi = dg[:, edge + 2 * _REL_CLIP + 1:].sum(axis=1)
    return mid.at[:, 0].add(lo).at[:, -1].add(hi)


def _band_window(i):
    j0 = jnp.maximum(i - (_BAND_TILES - 1), 0)
    return j0, (_BAND_TILES - 1) - (i - j0)


def _head_masks():
    lane = lax.broadcasted_iota(jnp.int32, (1, _LANES), 1)
    return [(lane // _HEAD) == hh for hh in range(2)]


def _stack_heads(x, masks):
    return jnp.concatenate([jnp.where(m, x, jnp.zeros_like(x)) for m in masks], axis=0)


def _unstack_heads(y, masks):
    return jnp.where(masks[0], y[:_TQ], y[_TQ:])


def _scaled(q):
    return q * jnp.asarray(_HEAD ** -0.5, q.dtype)


def _band_probs(q2, k_ref, b_ref, j0, boff):
    s = []
    for j in range(_BAND_TILES):
        kj = k_ref[pl.ds(pl.multiple_of((j0 + j) * _TQ, _TQ), _TQ), :]
        s.append(_dot_nt(q2, kj) + _bias_tile(b_ref, boff + j)[...])
    m = jnp.max(functools.reduce(jnp.maximum, s), axis=-1, keepdims=True)
    p = [jnp.exp(x - m) for x in s]
    l = jnp.sum(functools.reduce(lambda a, b: a + b, p), axis=-1, keepdims=True)
    return p, 1.0 / l


def _qkv_specs(T, cb, npair, tq=_TQ):
    return [pl.BlockSpec((tq, _LANES), lambda h, i: (i, cb + h)),
            pl.BlockSpec((T, _LANES), lambda h, i: (0, cb + npair + h)),
            pl.BlockSpec((T, _LANES), lambda h, i: (0, cb + 2 * npair + h))]


def _attn_a_fwd(hq, bias, col0, width, layer, comm=None):
    T = hq.shape[0]
    npair = width // _LANES
    nsub = _BAND_SUBTILES
    tq = nsub * _TQ

    def kern(q_ref, k_ref, v_ref, b_ref, o_ref):
        masks = _head_masks()
        q = _scaled(q_ref[...])
        for s in range(nsub):
            part = slice(s * _TQ, (s + 1) * _TQ)
            j0, boff = _band_window(nsub * pl.program_id(1) + s)
            p, inv = _band_probs(_stack_heads(q[part], masks), k_ref, b_ref, j0, boff)
            o = jnp.zeros((2 * _TQ, _LANES), _F32)
            for j in range(_BAND_TILES):
                vj = v_ref[pl.ds(pl.multiple_of((j0 + j) * _TQ, _TQ), _TQ), :]
                o = o + _dot(p[j].astype(_MXU), vj)
            o_ref[part, :] = _unstack_heads(o * inv, masks).astype(o_ref.dtype)

    outs, extra = _call(
        kern, comm, name=f"band_attn_fwd_{layer}", grid=(npair, T // tq),
        in_specs=_qkv_specs(T, col0 // _LANES, npair, tq)
        + [pl.BlockSpec((None, 2 * _TQ, _BIAS_TILES * _TQ), lambda h, i: (h, 0, 0))],
        out_specs=[pl.BlockSpec((tq, _LANES), lambda h, i: (i, h))],
        out_shape=[jax.ShapeDtypeStruct((T, width), _ACT)], scratch_shapes=[],
        args=(hq, hq, hq, bias), semantics=("arbitrary", "arbitrary"))
    return outs[0], extra


def _attn_a_bwd(hq, bias, do, col0, width, layer, comm=None):
    T = hq.shape[0]
    npair = width // _LANES
    nsub = _BAND_SUBTILES
    tq = nsub * _TQ
    nq = T // tq
    scale = _HEAD ** -0.5

    def kern(q_ref, k_ref, v_ref, b_ref, do_ref, dq_ref, dk_ref, dv_ref, db_ref, dk_acc, dv_acc):
        i = pl.program_id(1)

        @pl.when(i == 0)
        def _():
            dk_acc[...] = jnp.zeros_like(dk_acc)
            dv_acc[...] = jnp.zeros_like(dv_acc)
            db_ref[...] = jnp.zeros_like(db_ref)

        masks = _head_masks()
        q = _scaled(q_ref[...])
        do_t = do_ref[...]
        for s in range(nsub):
            part = slice(s * _TQ, (s + 1) * _TQ)
            j0, boff = _band_window(nsub * i + s)
            q2 = _stack_heads(q[part], masks)
            do2 = _stack_heads(do_t[part], masks).astype(_MXU)
            p, inv = _band_probs(q2, k_ref, b_ref, j0, boff)
            rows = [pl.ds(pl.multiple_of((j0 + j) * _TQ, _TQ), _TQ) for j in range(_BAND_TILES)]
            p = [x * inv for x in p]
            dp = [_dot_nt(do2, v_ref[rows[j], :]) for j in range(_BAND_TILES)]
            delta = jnp.sum(functools.reduce(lambda a, b: a + b, [p[j] * dp[j] for j in range(_BAND_TILES)]),
                            axis=-1, keepdims=True)
            dq = jnp.zeros((2 * _TQ, _LANES), _F32)
            for j in range(_BAND_TILES):
                ds = p[j] * (dp[j] - delta)
                _bias_tile(db_ref, boff + j)[...] += ds
                dsb = ds.astype(_MXU)
                dq = dq + _dot(dsb, k_ref[rows[j], :])
                dk_acc[rows[j], :] += _dot_tn(dsb, q2)
                dv_acc[rows[j], :] += _dot_tn(p[j].astype(_MXU), do2)
            dq_ref[part, :] = (_unstack_heads(dq, masks) * scale).astype(dq_ref.dtype)

        @pl.when(i == nq - 1)
        def _():
            dk_ref[...] = dk_acc[...].astype(dk_ref.dtype)
            dv_ref[...] = dv_acc[...].astype(dv_ref.dtype)

    strip = pl.BlockSpec((None, 2 * _TQ, _BIAS_TILES * _TQ), lambda h, i: (h, 0, 0))
    tile = pl.BlockSpec((tq, _LANES), lambda h, i: (i, h))
    column = pl.BlockSpec((T, _LANES), lambda h, i: (0, h))
    outs, extra = _call(
        kern, comm, name=f"band_attn_bwd_{layer}", grid=(npair, nq),
        in_specs=_qkv_specs(T, col0 // _LANES, npair, tq) + [strip, tile],
        out_specs=[tile, column, column, strip],
        out_shape=[jax.ShapeDtypeStruct((T, width), _ACT)] * 3
        + [jax.ShapeDtypeStruct((npair, 2 * _TQ, _BIAS_TILES * _TQ), _F32)],
        scratch_shapes=[pltpu.VMEM((T, _LANES), _F32), pltpu.VMEM((T, _LANES), _F32)],
        args=(hq, hq, hq, bias, do), semantics=("arbitrary", "arbitrary"))
    return outs, extra


def _suffix_matrix():
    r = lax.broadcasted_iota(jnp.int32, (_TQ, _TQ), 0)
    c = lax.broadcasted_iota(jnp.int32, (_TQ, _TQ), 1)
    r2 = lax.broadcasted_iota(jnp.int32, (2 * _TQ, _TQ), 0)
    c2 = lax.broadcasted_iota(jnp.int32, (2 * _TQ, _TQ), 1)
    return (r > c).astype(_MXU), c2 - (r2 & (_TQ - 1))


def _suffix_sums(xs, tri):
    n, k = xs[0].shape[0], len(xs)
    his = [x.astype(_MXU) for x in xs]
    los = [(x - h.astype(_F32)).astype(_MXU) for x, h in zip(xs, his)]
    y = _dot(jnp.concatenate(his + los, axis=0), tri)
    return [y[j * n:(j + 1) * n] + y[(k + j) * n:(k + j + 1) * n] for j in range(k)]


def _stick_tiles(tiles, rel, carry_l, tri):
    zs = [_dot_nt(qs, kj) for qs, kj, _, _ in tiles]
    Ls, masks = [], []
    for z, (_, _, jj, _) in zip(zs, tiles):
        nsp = -(jnp.maximum(z, 0.0) + jnp.log(1.0 + jnp.exp(-jnp.abs(z))))
        if isinstance(jj, int):
            mask = (rel < 0) if jj == 0 else None
        else:
            mask = rel < jnp.where(jj == 0, 0, _TQ)
        Ls.append(nsp if mask is None else jnp.where(mask, nsp, 0.0))
        masks.append(mask)
    carry_l = list(carry_l)
    ws = []
    for z, L, suffix, mask, (_, _, _, sub) in zip(zs, Ls, _suffix_sums(Ls, tri), masks, tiles):
        w = jnp.exp(z + L + suffix + carry_l[sub])
        ws.append(w if mask is None else jnp.where(mask, w, 0.0))
        carry_l[sub] = carry_l[sub] + jnp.sum(L, axis=-1, keepdims=True)
    return zs, Ls, ws, masks, carry_l


def _sweep(i, step, zero):
    nsub = _SB_SUBTILES

    def window():
        tiles = [(s, jj) for jj in range(_SB_WINDOW) for s in range(nsub)]
        return tuple((jnp.int32(_SB_WINDOW),) + c for c in step(tiles, [zero] * nsub))

    start = lax.cond(i >= -(-(_SB_WINDOW - 1) // nsub), window, lambda: tuple((jnp.int32(0),) + zero for _ in range(nsub)))
    outs = []
    for s in range(nsub):
        def done(c, s=s):
            return jnp.logical_or(c[0] > nsub * i + s, jnp.max(c[1]) < _EXP_ZERO_BELOW)

        def more(c, s=s):
            carries = [None] * nsub
            carries[s] = c[1:]
            return (c[0] + 1,) + step([(s, c[0])], carries)[s]

        outs.append(lax.while_loop(lambda c, done=done: jnp.logical_not(done(c)), more, start[s]))
    return outs


def _sb_fwd(hq, col0, width, layer, comm=None):
    T = hq.shape[0]
    npair = width // _LANES
    nsub = _SB_SUBTILES
    tq = nsub * _TQ

    def kern(q_ref, k_ref, v_ref, o_ref):
        i = pl.program_id(1)
        masks = _head_masks()
        tri, rel = _suffix_matrix()
        q = _scaled(q_ref[...])
        q2 = [_stack_heads(q[s * _TQ:(s + 1) * _TQ], masks) for s in range(nsub)]

        def step(tiles, carries):
            rows = [pl.ds(pl.multiple_of((nsub * i + s - jj) * _TQ, _TQ), _TQ) for s, jj in tiles]
            cls = [None if c is None else c[0] for c in carries]
            accs = [None if c is None else c[1] for c in carries]
            _, _, ws, _, cls = _stick_tiles([(q2[s], k_ref[r, :], jj, s) for (s, jj), r in zip(tiles, rows)], rel, cls, tri)
            for w, r, (s, _) in zip(ws, rows, tiles):
                accs[s] = accs[s] + _dot(w.astype(_MXU), v_ref[r, :])
            return [None if c is None else (cls[s], accs[s]) for s, c in enumerate(carries)]

        outs = _sweep(i, step, (jnp.zeros((2 * _TQ, 1), _F32), jnp.zeros((2 * _TQ, _LANES), _F32)))
        for s in range(nsub):
            o_ref[s * _TQ:(s + 1) * _TQ, :] = _unstack_heads(outs[s][2], masks)

    outs, extra = _call(
        kern, comm, name=f"stick_attn_fwd_{layer}", grid=(npair, T // tq),
        in_specs=_qkv_specs(T, col0 // _LANES, npair, tq),
        out_specs=[pl.BlockSpec((tq, _LANES), lambda h, i: (i, h))],
        out_shape=[jax.ShapeDtypeStruct((T, width), _F32)], scratch_shapes=[],
        args=(hq, hq, hq), semantics=("arbitrary", "arbitrary"))
    return outs[0], extra


def _sb_bwd(hq, o, do, col0, width, layer, comm=None):
    T = hq.shape[0]
    npair = width // _LANES
    nsub = _SB_SUBTILES
    tq = nsub * _TQ
    nq = T // tq
    scale = _HEAD ** -0.5

    def kern(q_ref, k_ref, v_ref, o_ref, do_ref, dq_ref, dk_ref, dv_ref, dk_acc, dv_acc):
        i = pl.program_id(1)

        @pl.when(i == 0)
        def _():
            dk_acc[...] = jnp.zeros_like(dk_acc)
            dv_acc[...] = jnp.zeros_like(dv_acc)

        masks = _head_masks()
        tri, rel = _suffix_matrix()
        q = _scaled(q_ref[...])
        do_t = do_ref[...]
        prod = do_t.astype(_F32) * o_ref[...]
        part = [slice(s * _TQ, (s + 1) * _TQ) for s in range(nsub)]
        q2 = [_stack_heads(q[p], masks) for p in part]
        do2 = [_stack_heads(do_t[p], masks).astype(_MXU) for p in part]
        dsum = [jnp.sum(_stack_heads(prod[p], masks), axis=-1, keepdims=True) for p in part]

        def step(tiles, carries):
            rows = [pl.ds(pl.multiple_of((nsub * i + s - jj) * _TQ, _TQ), _TQ) for s, jj in tiles]
            kjs = [k_ref[r, :] for r in rows]
            cls, cgs, dqs = ([None if c is None else c[n] for c in carries] for n in range(3))
            zs, Ls, ws, tile_masks, cls = _stick_tiles([(q2[s], kj, jj, s) for (s, jj), kj in zip(tiles, kjs)], rel, cls, tri)
            wbs = [w.astype(_MXU) for w in ws]
            gs = [wb.astype(_F32) * _dot_nt(do2[s], v_ref[r, :]) for wb, r, (s, _) in zip(wbs, rows, tiles)]
            for z, L, g, later, mask, wb, kj, r, (s, _) in zip(zs, Ls, gs, _suffix_sums(gs, tri), tile_masks, wbs, kjs,
                                                               rows, tiles):
                dz = g - jnp.exp(z + L) * (dsum[s] - (later + cgs[s]))
                if mask is not None:
                    dz = jnp.where(mask, dz, 0.0)
                dzb = dz.astype(_MXU)
                dk_acc[r, :] += _dot_tn(dzb, q2[s])
                dv_acc[r, :] += _dot_tn(wb, do2[s])
                dqs[s] = dqs[s] + _dot(dzb, kj)
                cgs[s] = cgs[s] + jnp.sum(g, axis=-1, keepdims=True)
            return [None if c is None else (cls[s], cgs[s], dqs[s]) for s, c in enumerate(carries)]

        zc = jnp.zeros((2 * _TQ, 1), _F32)
        outs = _sweep(i, step, (zc, zc, jnp.zeros((2 * _TQ, _LANES), _F32)))
        for s in range(nsub):
            dq_ref[part[s], :] = (_unstack_heads(outs[s][3], masks) * scale).astype(dq_ref.dtype)

        @pl.when(i == nq - 1)
        def _():
            dk_ref[...] = dk_acc[...].astype(dk_ref.dtype)
            dv_ref[...] = dv_acc[...].astype(dv_ref.dtype)

    tile_spec = pl.BlockSpec((tq, _LANES), lambda h, i: (i, h))
    column = pl.BlockSpec((T, _LANES), lambda h, i: (0, h))
    outs, extra = _call(
        kern, comm, name=f"stick_attn_bwd_{layer}", grid=(npair, nq),
        in_specs=_qkv_specs(T, col0 // _LANES, npair, tq) + [tile_spec, tile_spec],
        out_specs=[tile_spec, column, column],
        out_shape=[jax.ShapeDtypeStruct((T, width), _ACT)] * 3,
        scratch_shapes=[pltpu.VMEM((T, _LANES), _F32), pltpu.VMEM((T, _LANES), _F32)],
        args=(hq, hq, hq, o, do), semantics=("arbitrary", "arbitrary"))
    return outs, extra


_DENSE = ("w_in", "w_proj_a", "w_proj_b", "w_out", "w_ffn_in", "w_ffn_out")
_COL_SHARDED = {"w_in": True, "w_proj_a": True, "w_proj_b": True, "w_out": False, "w_ffn_in": True, "w_ffn_out": False}
_SMALL = ("b_gate", "rel_bias", "ln1_g", "ln1_b", "ln2_g", "ln2_b")


class _Plans:
    def __init__(self, plans=None, own_w_in=None):
        self.plans = plans or {}
        self.own_w_in = own_w_in or {}

    def start(self, key):
        if key not in self.plans:
            return None, None
        return self.plans[key]()

    @staticmethod
    def finish(done, extra):
        if done is not None:
            done(extra)


def _layer_fwd(x, W, small, l, alpha, plans):
    WA = small["rel_bias"].shape[1] * _HEAD
    row = lambda v: v[l].reshape(1, -1)
    if l in plans.own_w_in:
        h, xb, W["w_in"] = _in_proj_gathering(x, plans.own_w_in[l], l)
    else:
        comm, done = plans.start(f"in_proj_{l}")
        (h, xb), extra = _in_proj(x, W["w_in"], l, comm)
        plans.finish(done, extra)
    WB = (h.shape[1] - 2 * x.shape[1] - 3 * WA) // 3
    bias = _bias_tiles(small["rel_bias"][l])
    comm, done = plans.start(f"band_fwd_{l}")
    oa, extra = _attn_a_fwd(h, bias, 0, WA, l, comm)
    plans.finish(done, extra)
    comm, done = plans.start(f"stick_fwd_{l}")
    ob, extra = _sb_fwd(h, 3 * WA, WB, l, comm)
    plans.finish(done, extra)
    comm, done = plans.start(f"mix_fwd_{l}")
    (x1, u1, pre), extra = _mix_fwd(oa, ob, h, x, W["w_proj_a"], W["w_proj_b"], W["w_out"], row(small["b_gate"]),
                                            row(small["ln1_g"]), row(small["ln1_b"]), alpha, l, comm)
    plans.finish(done, extra)
    comm, done = plans.start(f"ffn_fwd_{l}")
    (x2, u2, act, gu, x1b), extra = _ffn_fwd(x1, W["w_ffn_in"], W["w_ffn_out"], row(small["ln2_g"]),
                                             row(small["ln2_b"]), alpha, l, comm)
    plans.finish(done, extra)
    return x2, dict(xb=xb, h=h, bias=bias, oa=oa, ob=ob, x1b=x1b, u1=u1, pre=pre, u2=u2, act=act, gu=gu)


def _layer_bwd(dy_or_target, S, W, small, l, last, alpha, plans, gw):
    D = S["xb"].shape[1]
    WA, WB = S["oa"].shape[1], S["ob"].shape[1]
    row = lambda v: v[l].reshape(1, -1)

    def blocks(g, n):
        return g if _COL_SHARDED[n] else g.reshape(4, g.shape[0] // 4, g.shape[1])

    dx1, du2b, dgu, st2 = _ffn_bwd(S["u2"], dy_or_target, S["gu"], row(small["ln2_g"]), row(small["ln2_b"]),
                                   W["w_ffn_in"], W["w_ffn_out"], alpha, l, last)
    gw["w_ffn_in"] = blocks(_grad_w(S["x1b"], dgu, col_shards=True, name=f"grad_w_ffn_in_{l}")[0], "w_ffn_in")
    gw["w_ffn_out"] = blocks(_grad_w(S["act"], du2b, col_shards=False, name=f"grad_w_ffn_out_{l}")[0], "w_ffn_out")
    du1, du1b, dya, dyb, dhg, doa, dob, st1 = _mix_bwd(S["u1"], dx1, S["oa"], S["ob"], S["h"], W["w_proj_a"],
                                                       W["w_proj_b"], W["w_out"], row(small["b_gate"]),
                                                       row(small["ln1_g"]), l)
    gw["w_out"] = blocks(_grad_w(S["pre"], du1b, col_shards=False, name=f"grad_w_out_{l}")[0], "w_out")
    gw["w_proj_a"] = blocks(_grad_w(S["oa"], dya, col_shards=True, name=f"grad_w_proj_a_{l}")[0], "w_proj_a")
    gw["w_proj_b"] = blocks(_grad_w(S["ob"], dyb, col_shards=True, name=f"grad_w_proj_b_{l}")[0], "w_proj_b")
    comm, done = plans.start(f"band_bwd_{l}")
    (dqa, dka, dva, dbias), extra = _attn_a_bwd(S["h"], S["bias"], doa, 0, WA, l, comm)
    plans.finish(done, extra)
    comm, done = plans.start(f"stick_bwd_{l}")
    (dqb, dkb, dvb), extra = _sb_bwd(S["h"], S["ob"], dob, 3 * WA, WB, l, comm)
    plans.finish(done, extra)
    dh = [dqa, dka, dva, dqb, dkb, dvb, dhg]
    comm, done = plans.start(f"grad_w_in_{l}")
    gw["w_in"], extra = _grad_w_pieces(S["xb"], dh, f"grad_w_in_{l}", comm)
    plans.finish(done, extra)
    comm, done = plans.start(f"in_proj_bwd_{l}")
    dx, extra = _residual_nt(du1, alpha, dh, W["w_in"], f"in_proj_bwd_{l}", comm)
    plans.finish(done, extra)
    gs = dict(b_gate=st1[0], rel_bias=_fold_bias_grad(dbias), ln1_g=st1[1, :D], ln1_b=st1[1, D:],
              ln2_g=st2[0], ln2_b=st2[1])
    return dx, gs, st2[2]


def _local_step(x, target, W, small, plans=None, gws=None):
    depth = len(W)
    alpha = float((2 * depth) ** 0.25)
    plans = plans or _Plans()
    gws = gws if gws is not None else [dict() for _ in range(depth)]
    saved = []
    h = x
    for l in range(depth):
        h, S = _layer_fwd(h, W[l], small, l, alpha, plans)
        saved.append(S)
    gss = [None] * depth
    d = target
    sq = None
    for l in reversed(range(depth)):
        d, gss[l], sq_l = _layer_bwd(d, saved[l], W[l], small, l, l == depth - 1, alpha, plans, gws[l])
        if l == depth - 1:
            sq = sq_l
    return sq, d, gws, gss


def _place():
    return lax.axis_index("x"), lax.axis_index("y"), lax.axis_index("c")


def _remote(src, dst, send_sem, recv_sem, to):
    return pltpu.make_async_remote_copy(src_ref=src, dst_ref=dst, send_sem=send_sem, recv_sem=recv_sem,
                                        device_id=to, device_id_type=_MESH)


def _half(ref, hc):
    kh = ref.shape[0] // 2
    return ref.at[pl.ds(pl.multiple_of(hc * kh, 16), kh), :]


def _gather_plan(blocks, fractions):
    nt = len(blocks)

    def run(step, nsteps, ins, outs, sems):
        send_sems, recv_sems, loc_sems = sems
        x, y, c = _place()
        k = 2 * x + y
        me, sibling = (x, y, c), (x, y, 1 - c)
        chips = [(1 - x, y), (x, 1 - y), (1 - x, 1 - y)]
        chip_k = [2 * cx + cy for cx, cy in chips]

        def ici(t, s, owner_k, to, src=None):
            dst = _half(outs[t].at[owner_k], c)
            return _remote(dst if src is None else src, dst, send_sems.at[t, s], recv_sems.at[t, s], to)

        def passed(t, s, hc, to):
            blk = _half(outs[t].at[chip_k[s]], hc)
            return _remote(blk, blk, send_sems.at[t, 3 + s], recv_sems.at[t, 3 + s], to)

        def local(t):
            return pltpu.make_async_copy(ins[t], outs[t].at[k], loc_sems.at[t])

        @pl.when(step == 0)
        def _():
            for t in range(nt):
                local(t).start()
                for s, chip in enumerate(chips):
                    ici(t, s, k, (*chip, c), src=_half(ins[t], c)).start()

        for t in range(nt):
            @pl.when(step == min(nsteps - 1, int(fractions[t] * nsteps)))
            def _():
                for s in range(3):
                    ici(t, s, chip_k[s], me).wait_recv()
                    passed(t, s, c, sibling).start()

        @pl.when(step == nsteps - 1)
        def _():
            for t in range(nt):
                for s, chip in enumerate(chips):
                    passed(t, s, 1 - c, me).wait_recv()
            for t in range(nt):
                for s, chip in enumerate(chips):
                    ici(t, s, k, (*chip, c), src=_half(ins[t], c)).wait_send()
                    passed(t, s, c, sibling).wait_send()
                local(t).wait()

    return _Comm(blocks, [jax.ShapeDtypeStruct((4,) + b.shape, b.dtype) for b in blocks],
                 [pltpu.SemaphoreType.DMA((nt, 6)), pltpu.SemaphoreType.DMA((nt, 6)), pltpu.SemaphoreType.DMA((nt,))], run)


def _scatter_plan(grads, owners):
    nt = len(grads)
    shapes = [g.shape[1:] if g.ndim == 3 else (g.shape[0], g.shape[1] // 4) for g in grads]

    def run(step, nsteps, ins, outs, sems):
        send_sems, recv_sems, loc_sems = sems
        x, y, c = _place()
        me = 4 * x + 2 * y + c

        def target(r):
            tx = 1 - x if r & 2 else x
            ty = 1 - y if r & 1 else y
            return tx, ty

        def block(t, chip):
            if len(ins[t].shape) == 3:
                return ins[t].at[chip]
            n = shapes[t][1]
            return ins[t].at[:, pl.ds(pl.multiple_of(chip * n, _LANES), n)]

        def send(t, r):
            tx, ty = target(r)
            return _remote(block(t, 2 * tx + ty), outs[t].at[me], send_sems.at[t, r], recv_sems.at[t, 2 * r + c],
                           (tx, ty, owners[t]))

        def local(t):
            return pltpu.make_async_copy(block(t, 2 * x + y), outs[t].at[me], loc_sems.at[t])

        @pl.when(step == 0)
        def _():
            for t in range(nt):
                @pl.when(c == owners[t])
                def _():
                    local(t).start()

                @pl.when(c != owners[t])
                def _():
                    send(t, 0).start()

                for r in range(1, 4):
                    send(t, r).start()

        @pl.when(step == nsteps - 1)
        def _():
            for t in range(nt):
                @pl.when(c == owners[t])
                def _():
                    for r in range(4):
                        sx, sy = target(r)
                        for cs in range(2):
                            if r == 0 and cs == owners[t]:
                                continue
                            src_dev = 4 * sx + 2 * sy + cs
                            _remote(block(t, 0), outs[t].at[src_dev], send_sems.at[t, r], recv_sems.at[t, 2 * r + cs],
                                    (x, y, c)).wait_recv()
                    local(t).wait()

                @pl.when(c != owners[t])
                def _():
                    send(t, 0).wait_send()

                for r in range(1, 4):
                    send(t, r).wait_send()

    return _Comm(grads, [jax.ShapeDtypeStruct((8,) + s, g.dtype) for s, g in zip(shapes, grads)],
                 [pltpu.SemaphoreType.DMA((nt, 4)), pltpu.SemaphoreType.DMA((nt, 8)), pltpu.SemaphoreType.DMA((nt,))], run)


def _share_plan(reduced, owners):
    nt = len(reduced)

    def run(step, nsteps, ins, outs, sems):
        del ins
        send_sems, recv_sems = sems
        x, y, c = _place()

        def give(t, to):
            return _remote(outs[t], outs[t], send_sems.at[t], recv_sems.at[t], to)

        @pl.when(step == 0)
        def _():
            for t in range(nt):
                @pl.when(c == owners[t])
                def _():
                    give(t, (x, y, 1 - c)).start()

        @pl.when(step == nsteps - 1)
        def _():
            for t in range(nt):
                @pl.when(c == owners[t])
                def _():
                    give(t, (x, y, 1 - c)).wait_send()

                @pl.when(c != owners[t])
                def _():
                    give(t, (x, y, c)).wait_recv()

    return _Comm(reduced, [jax.ShapeDtypeStruct(r.shape, r.dtype) for r in reduced],
                 [pltpu.SemaphoreType.DMA((nt,)), pltpu.SemaphoreType.DMA((nt,))], run,
                 aliases={t: t for t in range(nt)})


def _join(a, b):
    ni, no, ns = len(a.inputs), len(a.out_shapes), len(a.sems)

    def run(step, nsteps, ins, outs, sems):
        a.run(step, nsteps, ins[:ni], outs[:no], sems[:ns])
        b.run(step, nsteps, ins[ni:], outs[no:], sems[ns:])

    aliases = dict(a.aliases)
    aliases.update({ni + i: no + o for i, o in b.aliases.items()})
    return _Comm(a.inputs + b.inputs, a.out_shapes + b.out_shapes, a.sems + b.sems, run, aliases)


def _peer(x, y, c, r):
    px = 1 - x if r & 4 else x
    py = 1 - y if r & 2 else y
    pc = 1 - c if r & 1 else c
    return (px, py, pc), 4 * px + 2 * py + pc


def _sum_slots(st, name):
    _, K, n = st.shape
    tr = next(t for t in (256, 128, 64, 32, 16) if K % t == 0)

    def kern(s_ref, o_ref):
        acc = s_ref[0].astype(_F32)
        for d in range(1, 8):
            acc = acc + s_ref[d].astype(_F32)
        o_ref[...] = acc.astype(o_ref.dtype)

    return pl.pallas_call(
        kern, name=name, grid=(K // tr,),
        in_specs=[pl.BlockSpec((8, tr, n), lambda i: (0, i, 0))], out_specs=_rows(tr, n),
        out_shape=jax.ShapeDtypeStruct((K, n), _ACT),
        compiler_params=_cparams("parallel"),
    )(st)


def _all_reduce_small(p):
    R = p.shape[0]

    def body(p_ref, o_ref, stage, send_sems, recv_sems):
        x, y, c = _place()
        me = 4 * x + 2 * y + c
        stage[me] = p_ref[...]
        sent = []
        for r in range(1, 8):
            to, _ = _peer(x, y, c, r)
            cp = _remote(p_ref, stage.at[me], send_sems.at[r - 1], recv_sems.at[r - 1], to)
            cp.start()
            sent.append(cp)
        for r in range(1, 8):
            _, src_dev = _peer(x, y, c, r)
            _remote(p_ref, stage.at[src_dev], send_sems.at[r - 1], recv_sems.at[r - 1], (x, y, c)).wait_recv()
        acc = stage[0]
        for d in range(1, 8):
            acc = acc + stage[d]
        o_ref[...] = acc
        for cp in sent:
            cp.wait_send()

    vm = pl.BlockSpec(memory_space=pltpu.VMEM)
    return pl.pallas_call(
        body, name="all_reduce_small",
        in_specs=[vm], out_specs=vm,
        out_shape=jax.ShapeDtypeStruct((R, _LANES), _F32),
        scratch_shapes=[pltpu.VMEM((8, R, _LANES), _F32), pltpu.SemaphoreType.DMA((7,)), pltpu.SemaphoreType.DMA((7,))],
    )(p)


def _adamw_update(gv, w_ref, m_ref, v_ref, gf_ref, d_ref, nm_ref, nv_ref):
    nm = _B1 * m_ref[...] + (1.0 - _B1) * gv
    nv = _B2 * v_ref[...] + (1.0 - _B2) * (gv * gv)
    m_hat = nm / (1.0 - _B1 ** _STEP)
    v_hat = nv / (1.0 - _B2 ** _STEP)
    gf_ref[...] = gv
    d_ref[...] = -_LR * (m_hat / (jnp.sqrt(v_hat) + _EPS) + _WD * w_ref[...])
    nm_ref[...] = nm
    nv_ref[...] = nv


def _adamw_layers(w, g_layers, m, v, name):
    _, K, n = w.shape
    tr = next(t for t in (256, 128, 64, 32, 16) if K % t == 0)

    def kern(w_ref, g0_ref, g1_ref, m_ref, v_ref, *out_refs):
        first = pl.program_id(0) == 0
        gv = jnp.where(first, g0_ref[...].astype(_F32), g1_ref[...].astype(_F32))
        _adamw_update(gv, w_ref, m_ref, v_ref, *out_refs)

    stacked = pl.BlockSpec((None, tr, n), lambda l, i: (l, i, 0))
    layer = pl.BlockSpec((tr, n), lambda l, i: (i, 0))
    return tuple(pl.pallas_call(
        kern, name=name, grid=(2, K // tr),
        in_specs=[stacked, layer, layer, stacked, stacked], out_specs=[stacked] * 4,
        out_shape=[jax.ShapeDtypeStruct(w.shape, _F32)] * 4,
        compiler_params=_cparams("parallel", "parallel"),
    )(w, g_layers[0], g_layers[1], m, v))


def _adamw(w, g, m, v, name):
    shape = w.shape
    w2, g2, m2, v2 = (a.reshape(-1, shape[-1]) for a in (w, g, m, v))
    R, C = w2.shape
    tr = next((t for t in (256, 128, 64, 32, 16) if R % t == 0), R)

    def kern(w_ref, g_ref, m_ref, v_ref, *out_refs):
        _adamw_update(g_ref[...].astype(_F32), w_ref, m_ref, v_ref, *out_refs)

    outs = pl.pallas_call(
        kern, name=name, grid=(R // tr,),
        in_specs=[_rows(tr, C)] * 4, out_specs=[_rows(tr, C)] * 4,
        out_shape=[jax.ShapeDtypeStruct((R, C), _F32)] * 4,
        compiler_params=_cparams("parallel"),
    )(w2, g2, m2, v2)
    return tuple(o.reshape(shape) for o in outs)


def _pack_small(gss, sq):
    parts = [gss[l][n].reshape(-1) for n in _SMALL for l in range(len(gss))] + [jnp.sum(sq).reshape(1)]
    flat = jnp.concatenate(parts)
    rows = -(-flat.shape[0] // (8 * _LANES)) * 8
    return jnp.pad(flat, (0, rows * _LANES - flat.shape[0])).reshape(rows, _LANES)


def _unpack_small(total, shapes):
    flat = total.reshape(-1)
    out, off = {}, 0
    for n in _SMALL:
        layers = []
        for _ in range(shapes[n][0]):
            size = 1
            for s in shapes[n][1:]:
                size *= s
            layers.append(flat[off:off + size].reshape(shapes[n][1:]))
            off += size
        out[n] = jnp.stack(layers)
    return out, flat[off]


_GATHER = {
    "band_fwd_0": [(0, "w_proj_a"), (0, "w_proj_b"), (0, "w_out"), (0, "w_ffn_out")],
    "stick_fwd_0": [(0, "w_ffn_in"), (1, "w_proj_a"), (1, "w_proj_b"), (1, "w_out")],
    "mix_fwd_0": [(1, "w_ffn_out")],
    "ffn_fwd_0": [(1, "w_in"), (1, "w_ffn_in")],
}
_SCATTER = {
    "band_bwd_1": [(1, "w_ffn_in"), (1, "w_ffn_out")],
    "stick_bwd_1": [(1, "w_proj_a"), (1, "w_proj_b"), (1, "w_out")],
    "band_bwd_0": [(1, "w_in"), (0, "w_ffn_in")],
    "stick_bwd_0": [(0, "w_ffn_out"), (0, "w_proj_a"), (0, "w_proj_b"), (0, "w_out")],
    "in_proj_bwd_0": [(0, "w_in")],
}
_SHARE = {"stick_bwd_1": "band_bwd_1", "band_bwd_0": "stick_bwd_1", "stick_bwd_0": "band_bwd_0", "grad_w_in_0": "stick_bwd_0"}


def _owner(key):
    del key
    return 1


def kernel(x, w_in, b_gate, rel_bias, w_proj_a, w_proj_b, w_out, ln1_g, ln1_b, w_ffn_in, w_ffn_out, ln2_g, ln2_b, loss_target, m_w_in, m_b_gate, m_rel_bias, m_w_proj_a, m_w_proj_b, m_w_out, m_ln1_g, m_ln1_b, m_w_ffn_in, m_w_ffn_out, m_ln2_g, m_ln2_b, v_w_in, v_b_gate, v_rel_bias, v_w_proj_a, v_w_proj_b, v_w_out, v_ln1_g, v_ln1_b, v_w_ffn_in, v_w_ffn_out, v_ln2_g, v_ln2_b):
    names = ("w_in", "b_gate", "rel_bias", "w_proj_a", "w_proj_b", "w_out", "ln1_g", "ln1_b", "w_ffn_in", "w_ffn_out", "ln2_g", "ln2_b")
    w = dict(zip(names, (w_in, b_gate, rel_bias, w_proj_a, w_proj_b, w_out, ln1_g, ln1_b, w_ffn_in, w_ffn_out, ln2_g, ln2_b)))
    m = dict(zip(names, (m_w_in, m_b_gate, m_rel_bias, m_w_proj_a, m_w_proj_b, m_w_out, m_ln1_g, m_ln1_b, m_w_ffn_in, m_w_ffn_out, m_ln2_g, m_ln2_b)))
    v = dict(zip(names, (v_w_in, v_b_gate, v_rel_bias, v_w_proj_a, v_w_proj_b, v_w_out, v_ln1_g, v_ln1_b, v_w_ffn_in, v_w_ffn_out, v_ln2_g, v_ln2_b)))
    T, D = x.shape[-2], x.shape[-1]
    assert w_in.shape[0] == 2, "the exchange schedule below is written for two layers"

    mine = [{n: w[n][l].astype(_MXU) for n in _DENSE} for l in range(2)]
    W = [dict(), dict()]
    gws = [dict(), dict()]
    slots, final = {}, {}

    def gather(keys):
        sizes = [mine[l][n].size for l, n in keys]
        passed, fractions = 0, []
        for s in sizes:
            passed += s
            fractions.append(0.15 + 0.6 * passed / sum(sizes))

        def done(outs):
            for (l, n), o in zip(keys, outs):
                W[l][n] = o
        return _gather_plan([mine[l][n] for l, n in keys], fractions), done

    def scatter(keys):
        comm = _scatter_plan([gws[l][n] for l, n in keys], [_owner(key) for key in keys])
        return comm, lambda outs: slots.update(zip(keys, outs))

    def share(keys):
        reduced = [_sum_slots(slots[key], f"sum_grad_{key[1]}_{key[0]}") for key in keys]
        comm = _share_plan(reduced, [_owner(key) for key in keys])
        return comm, lambda outs: final.update(zip(keys, outs))

    def both(first, second):
        (ca, da), (cb, db) = first, second
        na = len(ca.out_shapes)
        return _join(ca, cb), lambda outs: (da(outs[:na]), db(outs[na:]))

    plans = {key: functools.partial(gather, keys) for key, keys in _GATHER.items()}
    for key, keys in _SCATTER.items():
        plans[key] = functools.partial(scatter, keys)
    for key, scattered_under in _SHARE.items():
        handed = functools.partial(share, _SCATTER[scattered_under])
        carried = plans.get(key)
        plans[key] = handed if carried is None else (lambda carried=carried, handed=handed: both(carried(), handed()))
    small = {n: w[n] for n in _SMALL}
    sq, dx, _, gss = _local_step(x.reshape(T, D), loss_target.reshape(T, D), W, small,
                                  _Plans(plans, {0: mine[0]["w_in"]}), gws)

    comm, done = share(_SCATTER["in_proj_bwd_0"])
    done(_comm_only(comm, "share_last"))
    total = _all_reduce_small(_pack_small(gss, sq))
    small_grads, sq_all = _unpack_small(total, {n: w[n].shape for n in _SMALL})
    loss = 0.5 * sq_all / D

    grad, delta, new_m, new_v = {}, {}, {}, {}
    for n in names:
        if n in _DENSE:
            updated = _adamw_layers(w[n], [final[(l, n)] for l in range(2)], m[n], v[n], f"adamw_{n}")
        else:
            updated = _adamw(w[n], small_grads[n], m[n], v[n], f"adamw_{n}")
        grad[n], delta[n], new_m[n], new_v[n] = updated
    return (loss, dx.reshape(x.shape), *[grad[n] for n in names], *[delta[n] for n in names],
            *[new_m[n] for n in names], *[new_v[n] for n in names])
```

```python
import functools

import jax
import jax.numpy as jnp
from jax import lax
from jax.experimental import pallas as pl
from jax.experimental.pallas import tpu as pltpu

_MXU = jnp.bfloat16
_ACT = jnp.bfloat16
_F32 = jnp.float32

_HEAD = 64
_CHUNK = 64
_LANES = 128
_TQ = 128
_BAND_TILES = 5
_BIAS_TILES = 9
_REL_CLIP = 256
_LN_EPS = 1e-5
_MASKED = -1e30
_EXP_ZERO_BELOW = -87.34
_SB_WINDOW = 2
_SB_SUBTILES = 4
_BAND_SUBTILES = 8
_VMEM_LIMIT = 56 * 1024 * 1024
_GRAD_ACC_BYTES = 12 * 1024 * 1024

_LR, _B1, _B2, _EPS, _WD, _STEP = 0.001, 0.9, 0.999, 1e-08, 0.01, 10

_MESH = pl.DeviceIdType.MESH


def _dot(a, b):
    return jnp.dot(a, b, preferred_element_type=_F32)


def _dot_nt(a, b):
    return lax.dot_general(a, b, (((1,), (1,)), ((), ())), preferred_element_type=_F32)


def _dot_tn(a, b):
    return lax.dot_general(a, b, (((0,), (0,)), ((), ())), preferred_element_type=_F32)


def _cparams(*sem):
    return pltpu.CompilerParams(dimension_semantics=sem, vmem_limit_bytes=_VMEM_LIMIT)


def _rows(t, c):
    return pl.BlockSpec((t, c), lambda i: (i, 0))


def _whole(shape):
    return pl.BlockSpec(shape, lambda i: tuple(0 for _ in shape))


_ANY = pl.BlockSpec(memory_space=pl.ANY)


def _load_cols(w_hbm, w_vmem, sem):
    n = w_hbm.shape[-1]
    cps = [pltpu.make_async_copy(w_hbm.at[k], w_vmem.at[:, pl.ds(k * n, n)], sem.at[k]) for k in range(4)]
    for cp in cps:
        cp.start()
    for cp in cps:
        cp.wait()


def _load_rows(w_hbm, w_vmem, sem):
    r = w_hbm.shape[-2]
    cps = [pltpu.make_async_copy(w_hbm.at[k], w_vmem.at[pl.ds(k * r, r), :], sem.at[k]) for k in range(4)]
    for cp in cps:
        cp.start()
    for cp in cps:
        cp.wait()


def _ln_stats(u):
    mu = jnp.mean(u, axis=-1, keepdims=True)
    xc = u - mu
    var = jnp.mean(xc * xc, axis=-1, keepdims=True)
    rstd = lax.rsqrt(var + _LN_EPS)
    return xc * rstd, rstd


def _ln_bwd(u, dy, gamma):
    xhat, rstd = _ln_stats(u)
    dxh = dy * gamma
    m1 = jnp.mean(dxh, axis=-1, keepdims=True)
    m2 = jnp.mean(dxh * xhat, axis=-1, keepdims=True)
    du = rstd * (dxh - m1 - xhat * m2)
    return du, jnp.sum(dy * xhat, axis=0, keepdims=True), jnp.sum(dy, axis=0, keepdims=True), xhat


def _divisor_tile(n, cap):
    best = None
    for t in range(_LANES, min(n, cap) + 1, _LANES):
        if n % t == 0:
            best = t
    return best or n


class _Comm:
    def __init__(self, inputs, out_shapes, sems, run, aliases=None):
        self.inputs, self.out_shapes, self.sems, self.run = list(inputs), list(out_shapes), list(sems), run
        self.aliases = aliases or {}


def _call(kern, comm, *, name, grid, in_specs, out_specs, out_shape, scratch_shapes, args, semantics):
    in_specs, out_specs, out_shape, scratch_shapes = list(in_specs), list(out_specs), list(out_shape), list(scratch_shapes)
    if comm is None:
        outs = pl.pallas_call(kern, name=name, grid=grid, in_specs=in_specs, out_specs=out_specs, out_shape=out_shape,
                              scratch_shapes=scratch_shapes, compiler_params=_cparams(*semantics))(*args)
        return list(outs), []
    n_in, n_out, n_scr = len(in_specs), len(out_specs), len(scratch_shapes)
    ci, co = len(comm.inputs), len(comm.out_shapes)
    nsteps = functools.reduce(lambda a, b: a * b, grid, 1)

    def fused(*refs):
        a, b = n_in, n_in + ci
        c, d = b + n_out, b + n_out + co
        e = d + n_scr
        step = pl.program_id(0)
        for ax in range(1, len(grid)):
            step = step * grid[ax] + pl.program_id(ax)
        comm.run(step, nsteps, refs[a:b], refs[c:d], refs[e:])
        kern(*refs[:a], *refs[b:c], *refs[d:e])

    outs = pl.pallas_call(
        fused, name=name, grid=grid, in_specs=in_specs + [_ANY] * ci, out_specs=out_specs + [_ANY] * co,
        out_shape=out_shape + comm.out_shapes, scratch_shapes=scratch_shapes + comm.sems,
        input_output_aliases={n_in + i: n_out + o for i, o in comm.aliases.items()},
        compiler_params=_cparams(*("arbitrary" for _ in grid)))(*args, *comm.inputs)
    return list(outs[:n_out]), list(outs[n_out:])


def _comm_only(comm, name):
    def body(*refs):
        ci, co = len(comm.inputs), len(comm.out_shapes)
        comm.run(0, 1, refs[:ci], refs[ci:ci + co], refs[ci + co:])

    outs = pl.pallas_call(body, name=name, in_specs=[_ANY] * len(comm.inputs), out_specs=[_ANY] * len(comm.out_shapes),
                          out_shape=comm.out_shapes, scratch_shapes=comm.sems,
                          input_output_aliases=dict(comm.aliases))(*comm.inputs)
    return list(outs)


def _in_proj(x, w_in, layer, comm=None):
    T, D = x.shape
    N = 4 * w_in.shape[-1]
    tm = 512

    def kern(x_ref, w_hbm, h_ref, xb_ref, w_v, sem):
        @pl.when(pl.program_id(0) == 0)
        def _():
            _load_cols(w_hbm, w_v, sem)

        xb = x_ref[...].astype(_MXU)
        h_ref[...] = _dot(xb, w_v[...]).astype(h_ref.dtype)
        xb_ref[...] = xb.astype(xb_ref.dtype)

    return _call(
        kern, comm, name=f"in_proj_{layer}", grid=(T // tm,),
        in_specs=[_rows(tm, D), _ANY],
        out_specs=[_rows(tm, N), _rows(tm, D)],
        out_shape=[jax.ShapeDtypeStruct((T, N), _ACT), jax.ShapeDtypeStruct((T, D), _ACT)],
        scratch_shapes=[pltpu.VMEM((D, N), w_in.dtype), pltpu.SemaphoreType.DMA((4,))],
        args=(x, w_in), semantics=("arbitrary",))


def _in_proj_gathering(x, block, layer):
    T, D = x.shape
    n = block.shape[1]
    tm = min(T, 1024)
    nrows = T // tm
    pass_steps = [int(f * nrows) for f in (0.6, 1.0, 1.7)]
    px, py, _ = _place()
    order = jnp.stack([2 * px + py, 2 * (1 - px) + py, 2 * px + (1 - py), 2 * (1 - px) + (1 - py)]).astype(jnp.int32)

    def kern(order_ref, x_ref, blk_hbm, h_ref, xb_ref, w_hbm, w_v, send_sems, recv_sems, loc_sem, load_sem):
        del order_ref
        step = pl.program_id(0) * nrows + pl.program_id(1)
        x_, y_, c = _place()
        k = 2 * x_ + y_
        me, sibling = (x_, y_, c), (x_, y_, 1 - c)
        chips = [(1 - x_, y_), (x_, 1 - y_), (1 - x_, 1 - y_)]
        chip_k = [2 * cx + cy for cx, cy in chips]

        def ici(s, owner_k, to, src=None):
            dst = _half(w_hbm.at[owner_k], c)
            return _remote(dst if src is None else src, dst, send_sems.at[s], recv_sems.at[s], to)

        def passed(s, hc, to):
            blk = _half(w_hbm.at[chip_k[s]], hc)
            return _remote(blk, blk, send_sems.at[3 + s], recv_sems.at[3 + s], to)

        local = pltpu.make_async_copy(blk_hbm, w_hbm.at[k], loc_sem.at[0])

        def load(src):
            cp = pltpu.make_async_copy(src, w_v, load_sem.at[0])
            cp.start()
            cp.wait()

        @pl.when(step == 0)
        def _():
            for s, chip in enumerate(chips):
                ici(s, k, (*chip, c), src=_half(blk_hbm, c)).start()
            local.start()
            load(blk_hbm)

        for s in range(3):
            @pl.when(step == pass_steps[s])
            def _():
                ici(s, chip_k[s], me).wait_recv()
                passed(s, c, sibling).start()

            @pl.when(step == (s + 1) * nrows)
            def _():
                passed(s, 1 - c, me).wait_recv()
                load(w_hbm.at[chip_k[s]])

        xb = x_ref[...].astype(_MXU)
        h_ref[...] = _dot(xb, w_v[...]).astype(h_ref.dtype)

        @pl.when(pl.program_id(0) == 0)
        def _():
            xb_ref[...] = xb.astype(xb_ref.dtype)

        @pl.when(step == 4 * nrows - 1)
        def _():
            for s, chip in enumerate(chips):
                ici(s, k, (*chip, c), src=_half(blk_hbm, c)).wait_send()
                passed(s, c, sibling).wait_send()
            local.wait()

    assert all(pass_steps[s] <= (s + 1) * nrows for s in range(3))
    h, xb, w_in = pl.pallas_call(
        kern, name=f"in_proj_{layer}",
        grid_spec=pltpu.PrefetchScalarGridSpec(
            num_scalar_prefetch=1, grid=(4, nrows),
            in_specs=[pl.BlockSpec((tm, D), lambda j, i, o: (i, 0)), _ANY],
            out_specs=[pl.BlockSpec((tm, n), lambda j, i, o: (i, o[j])),
                       pl.BlockSpec((tm, D), lambda j, i, o: (jnp.where(j == 0, i, nrows - 1), 0)), _ANY],
            scratch_shapes=[pltpu.VMEM((D, n), block.dtype), pltpu.SemaphoreType.DMA((6,)),
                            pltpu.SemaphoreType.DMA((6,)), pltpu.SemaphoreType.DMA((1,)), pltpu.SemaphoreType.DMA((1,))]),
        out_shape=[jax.ShapeDtypeStruct((T, 4 * n), _ACT), jax.ShapeDtypeStruct((T, D), _ACT),
                   jax.ShapeDtypeStruct((4,) + block.shape, block.dtype)],
        compiler_params=_cparams("arbitrary", "arbitrary"))(order, x, block)
    return h, xb, w_in


def _gate_specs(h, tm, D):
    first = (h.shape[1] - 2 * D) // D
    assert first * D + 2 * D == h.shape[1]
    return [pl.BlockSpec((tm, D), lambda i: (i, first)), pl.BlockSpec((tm, D), lambda i: (i, first + 1))]


def _mix_fwd(oa, ob, h, x, wpa, wpb, wo, bg, gamma, beta, alpha, layer, comm=None):
    T, D = x.shape
    WA, WB = oa.shape[1], ob.shape[1]
    tm = 512

    def kern(oa_ref, ob_ref, hga_ref, hgb_ref, x_ref, bg_ref, g_ref, b_ref, wpa_h, wpb_h, wo_h,
             x1_ref, u1_ref, pre_ref, wpa_v, wpb_v, wo_v, sa, sb, so):
        @pl.when(pl.program_id(0) == 0)
        def _():
            _load_cols(wpa_h, wpa_v, sa)
            _load_cols(wpb_h, wpb_v, sb)
            _load_rows(wo_h, wo_v, so)

        ya = _dot(oa_ref[...].astype(_MXU), wpa_v[...])
        yb = _dot(ob_ref[...].astype(_MXU), wpb_v[...])
        bgv = bg_ref[...]
        ga = jax.nn.sigmoid(hga_ref[...].astype(_F32) + bgv[:, :D])
        gb = jax.nn.sigmoid(hgb_ref[...].astype(_F32) + bgv[:, D:])
        pre = ga * ya + gb * yb
        mix = _dot(pre.astype(_MXU), wo_v[...])
        u = alpha * x_ref[...] + mix
        xhat, _ = _ln_stats(u)
        x1_ref[...] = xhat * g_ref[...] + b_ref[...]
        u1_ref[...] = u
        pre_ref[...] = pre.astype(pre_ref.dtype)

    return _call(
        kern, comm, name=f"mix_fwd_{layer}", grid=(T // tm,),
        in_specs=[_rows(tm, WA), _rows(tm, WB), *_gate_specs(h, tm, D), _rows(tm, D),
                  _whole((1, 2 * D)), _whole((1, D)), _whole((1, D)), _ANY, _ANY, _ANY],
        out_specs=[_rows(tm, D)] * 3,
        out_shape=[jax.ShapeDtypeStruct((T, D), _F32), jax.ShapeDtypeStruct((T, D), _F32),
                   jax.ShapeDtypeStruct((T, D), _ACT)],
        scratch_shapes=[pltpu.VMEM((WA, D), wpa.dtype), pltpu.VMEM((WB, D), wpb.dtype), pltpu.VMEM((D, D), wo.dtype),
                        pltpu.SemaphoreType.DMA((4,)), pltpu.SemaphoreType.DMA((4,)), pltpu.SemaphoreType.DMA((4,))],
        args=(oa, ob, h, h, x, bg, gamma, beta, wpa, wpb, wo), semantics=("arbitrary",))


def _ffn_fwd(x1, wfi, wfo, gamma, beta, alpha, layer, comm=None):
    T, D = x1.shape
    F2 = 4 * wfi.shape[-1]
    F = F2 // 2
    tm = 512
    fc = F // 2

    def kern(x_ref, g_ref, b_ref, wi_h, wo_h, x2_ref, u2_ref, act_ref, gu_ref, xb_ref, wi_v, wo_v, si, so):
        @pl.when(pl.program_id(0) == 0)
        def _():
            _load_cols(wi_h, wi_v, si)
            _load_rows(wo_h, wo_v, so)

        x = x_ref[...]
        xb = x.astype(_MXU)
        xb_ref[...] = xb.astype(xb_ref.dtype)
        ffn = jnp.zeros((tm, D), _F32)
        for c in range(2):
            g = _dot(xb, wi_v[:, c * fc:(c + 1) * fc])
            u = _dot(xb, wi_v[:, F + c * fc:F + (c + 1) * fc])
            act = g * jax.nn.sigmoid(g) * u
            ab = act.astype(_MXU)
            ffn = ffn + _dot(ab, wo_v[c * fc:(c + 1) * fc, :])
            act_ref[:, c * fc:(c + 1) * fc] = ab.astype(act_ref.dtype)
            gu_ref[:, c * fc:(c + 1) * fc] = g.astype(gu_ref.dtype)
            gu_ref[:, F + c * fc:F + (c + 1) * fc] = u.astype(gu_ref.dtype)
        uu = alpha * x + ffn
        xhat, _ = _ln_stats(uu)
        x2_ref[...] = xhat * g_ref[...] + b_ref[...]
        u2_ref[...] = uu

    return _call(
        kern, comm, name=f"ffn_fwd_{layer}", grid=(T // tm,),
        in_specs=[_rows(tm, D), _whole((1, D)), _whole((1, D)), _ANY, _ANY],
        out_specs=[_rows(tm, D), _rows(tm, D), _rows(tm, F), _rows(tm, F2), _rows(tm, D)],
        out_shape=[jax.ShapeDtypeStruct((T, D), _F32), jax.ShapeDtypeStruct((T, D), _F32),
                   jax.ShapeDtypeStruct((T, F), _ACT), jax.ShapeDtypeStruct((T, F2), _ACT),
                   jax.ShapeDtypeStruct((T, D), _ACT)],
        scratch_shapes=[pltpu.VMEM((D, F2), wfi.dtype), pltpu.VMEM((F, D), wfo.dtype),
                        pltpu.SemaphoreType.DMA((4,)), pltpu.SemaphoreType.DMA((4,))],
        args=(x1, gamma, beta, wfi, wfo), semantics=("arbitrary",))


def _ffn_bwd(u2, dy_or_target, gu, gamma, beta, wfi, wfo, alpha, layer, last):
    T, D = u2.shape
    F2 = gu.shape[1]
    F = F2 // 2
    tm = 256
    fc = F // 2

    def kern(u_ref, dy_ref, gu_ref, g_ref, b_ref, wi_h, wo_h, dx_ref, dub_ref, dgu_ref, st_ref, wi_v, wo_v, si, so):
        @pl.when(pl.program_id(0) == 0)
        def _():
            _load_cols(wi_h, wi_v, si)
            _load_rows(wo_h, wo_v, so)
            st_ref[...] = jnp.zeros_like(st_ref)

        gam = g_ref[...]
        u = u_ref[...]
        if last:
            xhat0, _ = _ln_stats(u)
            err = xhat0 * gam + b_ref[...] - dy_ref[...]
            dy = err * (1.0 / D)
            st_ref[2:3, :] += jnp.sum(err * err, axis=0, keepdims=True)
        else:
            dy = dy_ref[...]
        du, dgam, dbet, _ = _ln_bwd(u, dy, gam)
        st_ref[0:1, :] += dgam
        st_ref[1:2, :] += dbet
        dub = du.astype(_MXU)
        dub_ref[...] = dub.astype(dub_ref.dtype)
        dx = alpha * du
        for c in range(2):
            dact = _dot_nt(dub, wo_v[c * fc:(c + 1) * fc, :])
            g = gu_ref[:, c * fc:(c + 1) * fc].astype(_F32)
            uu = gu_ref[:, F + c * fc:F + (c + 1) * fc].astype(_F32)
            sg = jax.nn.sigmoid(g)
            dg = (dact * uu * (sg * (1.0 + g * (1.0 - sg)))).astype(_MXU)
            dup = (dact * (g * sg)).astype(_MXU)
            dgu_ref[:, c * fc:(c + 1) * fc] = dg.astype(dgu_ref.dtype)
            dgu_ref[:, F + c * fc:F + (c + 1) * fc] = dup.astype(dgu_ref.dtype)
            dx = dx + _dot_nt(dg, wi_v[:, c * fc:(c + 1) * fc]) + _dot_nt(dup, wi_v[:, F + c * fc:F + (c + 1) * fc])
        dx_ref[...] = dx

    return pl.pallas_call(
        kern, name=f"ffn_bwd_{layer}", grid=(T // tm,),
        in_specs=[_rows(tm, D), _rows(tm, D), _rows(tm, F2), _whole((1, D)), _whole((1, D)), _ANY, _ANY],
        out_specs=[_rows(tm, D), _rows(tm, D), _rows(tm, F2), _whole((8, D))],
        out_shape=[jax.ShapeDtypeStruct((T, D), _F32), jax.ShapeDtypeStruct((T, D), _ACT),
                   jax.ShapeDtypeStruct((T, F2), _ACT), jax.ShapeDtypeStruct((8, D), _F32)],
        scratch_shapes=[pltpu.VMEM((D, F2), wfi.dtype), pltpu.VMEM((F, D), wfo.dtype),
                        pltpu.SemaphoreType.DMA((4,)), pltpu.SemaphoreType.DMA((4,))],
        compiler_params=_cparams("arbitrary"),
    )(u2, dy_or_target, gu, gamma, beta, wfi, wfo)


def _residual_nt(res, res_scale, pieces, w, name, comm=None):
    T, K = res.shape
    widths = [p.shape[1] for p in pieces]
    N = sum(widths)
    tm = 512

    def kern(r_ref, *refs):
        d_refs, (w_hbm, o_ref, w_v, sem) = refs[:len(pieces)], refs[len(pieces):]

        @pl.when(pl.program_id(0) == 0)
        def _():
            _load_cols(w_hbm, w_v, sem)

        acc = res_scale * r_ref[...]
        off = 0
        for d_ref, width in zip(d_refs, widths):
            acc = acc + _dot_nt(d_ref[...].astype(_MXU), w_v[:, off:off + width])
            off += width
        o_ref[...] = acc

    outs, extra = _call(
        kern, comm, name=name, grid=(T // tm,),
        in_specs=[_rows(tm, K)] + [_rows(tm, width) for width in widths] + [_ANY], out_specs=[_rows(tm, K)],
        out_shape=[jax.ShapeDtypeStruct((T, K), _F32)],
        scratch_shapes=[pltpu.VMEM((K, N), w.dtype), pltpu.SemaphoreType.DMA((4,))],
        args=(res, *pieces, w), semantics=("arbitrary",))
    return outs[0], extra


def _grad_w_pieces(a, pieces, name, comm=None):
    T, M = a.shape
    widths = [p.shape[1] for p in pieces]
    bw = functools.reduce(_gcd, widths + [512])
    first = [sum(widths[:p]) // bw for p in range(len(pieces))]
    count = [width // bw for width in widths]
    N = sum(widths)
    tk = 1024 if T % 1024 == 0 else 512
    nk = T // tk

    def kern(a_ref, *refs):
        b_refs, (o_ref, acc) = refs[:len(pieces)], refs[len(pieces):]
        j, k = pl.program_id(0), pl.program_id(1)

        @pl.when(k == 0)
        def _():
            acc[...] = jnp.zeros_like(acc)

        for b_ref, start, blocks in zip(b_refs, first, count):
            @pl.when(jnp.logical_and(j >= start, j < start + blocks))
            def _():
                acc[...] += _dot_tn(a_ref[...].astype(_MXU), b_ref[...].astype(_MXU))

        @pl.when(k == nk - 1)
        def _():
            o_ref[...] = acc[...].astype(o_ref.dtype)

    def piece_spec(start, blocks):
        def index(j, k):
            mine = jnp.logical_and(j >= start, j < start + blocks)
            return jnp.where(mine, k, 0), jnp.where(mine, j - start, 0)
        return pl.BlockSpec((tk, bw), index)

    outs, extra = _call(
        kern, comm, name=name, grid=(N // bw, nk),
        in_specs=[pl.BlockSpec((tk, M), lambda j, k: (k, 0))] + [piece_spec(s, c) for s, c in zip(first, count)],
        out_specs=[pl.BlockSpec((M, bw), lambda j, k: (0, j))],
        out_shape=[jax.ShapeDtypeStruct((M, N), _ACT)], scratch_shapes=[pltpu.VMEM((M, bw), _F32)],
        args=(a, *pieces), semantics=("parallel", "arbitrary"))
    return outs[0], extra


def _gcd(a, b):
    while b:
        a, b = b, a % b
    return a


def _mix_bwd(u1, dx1, oa, ob, h, wpa, wpb, wo, bg, gamma, layer):
    T, D = u1.shape
    WA, WB = wpa.shape[-2], wpb.shape[-2]
    tm = 512

    def kern(u_ref, dx_ref, oa_ref, ob_ref, hga_ref, hgb_ref, bg_ref, g_ref, wpa_h, wpb_h, wo_h,
             du_ref, dub_ref, dya_ref, dyb_ref, dhg_ref, doa_ref, dob_ref, st_ref,
             wpa_v, wpb_v, wo_v, sa, sb, so):
        @pl.when(pl.program_id(0) == 0)
        def _():
            _load_cols(wpa_h, wpa_v, sa)
            _load_cols(wpb_h, wpb_v, sb)
            _load_rows(wo_h, wo_v, so)
            st_ref[...] = jnp.zeros_like(st_ref)

        du, dgam, dbet, _ = _ln_bwd(u_ref[...], dx_ref[...], g_ref[...])
        st_ref[1:2, :D] += dgam
        st_ref[1:2, D:] += dbet
        du_ref[...] = du
        dub = du.astype(_MXU)
        dub_ref[...] = dub.astype(dub_ref.dtype)
        dpre = _dot_nt(dub, wo_v[...])
        bgv = bg_ref[...]
        ga = jax.nn.sigmoid(hga_ref[...].astype(_F32) + bgv[:, :D])
        gb = jax.nn.sigmoid(hgb_ref[...].astype(_F32) + bgv[:, D:])
        dya = (dpre * ga).astype(_MXU)
        dyb = (dpre * gb).astype(_MXU)
        dsa = dpre * _dot(oa_ref[...].astype(_MXU), wpa_v[...]) * (ga * (1.0 - ga))
        dsb = dpre * _dot(ob_ref[...].astype(_MXU), wpb_v[...]) * (gb * (1.0 - gb))
        st_ref[0:1, :D] += jnp.sum(dsa, axis=0, keepdims=True)
        st_ref[0:1, D:] += jnp.sum(dsb, axis=0, keepdims=True)
        dya_ref[...] = dya.astype(dya_ref.dtype)
        dyb_ref[...] = dyb.astype(dyb_ref.dtype)
        dhg_ref[:, :D] = dsa.astype(dhg_ref.dtype)
        dhg_ref[:, D:] = dsb.astype(dhg_ref.dtype)
        doa_ref[...] = _dot_nt(dya, wpa_v[...]).astype(doa_ref.dtype)
        dob_ref[...] = _dot_nt(dyb, wpb_v[...]).astype(dob_ref.dtype)

    return pl.pallas_call(
        kern, name=f"mix_bwd_{layer}", grid=(T // tm,),
        in_specs=[_rows(tm, D), _rows(tm, D), _rows(tm, WA), _rows(tm, WB), *_gate_specs(h, tm, D), _whole((1, 2 * D)),
                  _whole((1, D)), _ANY, _ANY, _ANY],
        out_specs=[_rows(tm, D)] * 4 + [_rows(tm, 2 * D), _rows(tm, WA), _rows(tm, WB), _whole((8, 2 * D))],
        out_shape=[jax.ShapeDtypeStruct((T, D), _F32)] + [jax.ShapeDtypeStruct((T, D), _ACT)] * 3
        + [jax.ShapeDtypeStruct((T, 2 * D), _ACT), jax.ShapeDtypeStruct((T, WA), _ACT),
           jax.ShapeDtypeStruct((T, WB), _ACT), jax.ShapeDtypeStruct((8, 2 * D), _F32)],
        scratch_shapes=[pltpu.VMEM((WA, D), wpa.dtype), pltpu.VMEM((WB, D), wpb.dtype), pltpu.VMEM((D, D), wo.dtype),
                        pltpu.SemaphoreType.DMA((4,)), pltpu.SemaphoreType.DMA((4,)), pltpu.SemaphoreType.DMA((4,))],
        compiler_params=_cparams("arbitrary"),
    )(u1, dx1, oa, ob, h, h, bg, gamma, wpa, wpb, wo)


def _grad_w(a, b, *, col_shards, name, comm=None):
    T, M = a.shape
    N = b.shape[1]
    tk = 1024 if T % 1024 == 0 else 512
    n = N // 4 if col_shards else N
    whole = M * N * 4 <= _GRAD_ACC_BYTES
    tn = N if whole else (n if col_shards else _divisor_tile(N, _GRAD_ACC_BYTES // (4 * M)))
    nk = T // tk

    def kern(a_ref, b_ref, o_ref, acc):
        k = pl.program_id(1)

        @pl.when(k == 0)
        def _():
            acc[...] = jnp.zeros_like(acc)

        acc[...] += _dot_tn(a_ref[...].astype(_MXU), b_ref[...].astype(_MXU))

        @pl.when(k == nk - 1)
        def _():
            if col_shards and whole:
                for s in range(4):
                    o_ref[s] = acc[:, s * n:(s + 1) * n].astype(o_ref.dtype)
            else:
                o_ref[...] = acc[...].astype(o_ref.dtype)

    if col_shards:
        out_spec = (pl.BlockSpec((4, M, n), lambda j, k: (0, 0, 0)) if whole
                    else pl.BlockSpec((None, M, n), lambda j, k: (j, 0, 0)))
        out_shape = jax.ShapeDtypeStruct((4, M, n), _ACT)
    else:
        out_spec = pl.BlockSpec((M, tn), lambda j, k: (0, j))
        out_shape = jax.ShapeDtypeStruct((M, N), _ACT)
    outs, extra = _call(
        kern, comm, name=name, grid=(N // tn, nk),
        in_specs=[pl.BlockSpec((tk, M), lambda j, k: (k, 0)), pl.BlockSpec((tk, tn), lambda j, k: (k, j))],
        out_specs=[out_spec], out_shape=[out_shape], scratch_shapes=[pltpu.VMEM((M, tn), _F32)],
        args=(a, b), semantics=("parallel", "arbitrary"))
    return outs[0], extra


def _bias_tiles(rel):
    H = rel.shape[0]
    span = _TQ * _BAND_TILES - 1
    edge = span - _REL_CLIP
    gvec = jnp.concatenate([jnp.broadcast_to(rel[:, :1], (H, edge)), rel, jnp.broadcast_to(rel[:, -1:], (H, edge))], axis=1)
    width = _BIAS_TILES * _TQ
    period = width + _TQ
    tiled = jnp.broadcast_to(jnp.pad(gvec[:, ::-1], ((0, 0), (0, 1)))[:, None, :], (H, _TQ, period))
    rows = tiled.reshape(H, _TQ * period)[:, :_TQ * (period - 1)].reshape(H, _TQ, period - 1)[:, :, _TQ - 1:]
    r = jnp.arange(_TQ)[:, None]
    u = jnp.arange(width)[None, :]
    d = 4 * _TQ + r - u
    rm = r % _CHUNK
    valid = (d >= rm - (_CHUNK - 1)) & (d <= rm + 8 * _CHUNK)
    tiles = jnp.where(valid[None], rows, _MASKED)
    return tiles.reshape(H // 2, 2 * _TQ, width)


def _strip_tiles(strip_ref, tiles_ref, to_strip=False):
    for ub in range(_BIAS_TILES):
        if to_strip:
            strip_ref[:, ub * _TQ:(ub + 1) * _TQ] = tiles_ref[ub]
        else:
            tiles_ref[ub] = strip_ref[:, ub * _TQ:(ub + 1) * _TQ]


def _fold_bias_grad(db):
    H = 2 * db.shape[0]
    width = _BIAS_TILES * _TQ
    period = width + _TQ
    x = jnp.pad(db.reshape(H, _TQ, width), ((0, 0), (0, 0), (_TQ - 1, 0)))
    skew = jnp.pad(x.reshape(H, _TQ * (period - 1)), ((0, 0), (0, _TQ))).reshape(H, _TQ, period)
    dg = skew.sum(axis=1)[:, :period - 1][:, ::-1]
    span = _TQ * _BAND_TILES - 1
    edge = span - _REL_CLIP
    mid = dg[:, edge:edge + 2 * _REL_CLIP + 1]
    lo = dg[:, :edge].sum(axis=1)
    hi = dg[:, edge + 2 * _REL_CLIP + 1:].sum(axis=1)
    return mid.at[:, 0].add(lo).at[:, -1].add(hi)


def _band_window(i):
    j0 = jnp.maximum(i - (_BAND_TILES - 1), 0)
    return j0, (_BAND_TILES - 1) - (i - j0)


def _head_masks():
    lane = lax.broadcasted_iota(jnp.int32, (1, _LANES), 1)
    return [(lane // _HEAD) == hh for hh in range(2)]


def _stack_heads(x, masks):
    return jnp.concatenate([jnp.where(m, x, jnp.zeros_like(x)) for m in masks], axis=0)


def _unstack_heads(y, masks):
    return jnp.where(masks[0], y[:_TQ], y[_TQ:])


def _scaled(q):
    return q * jnp.asarray(_HEAD ** -0.5, q.dtype)


def _band_probs(q2, k_ref, b_ref, j0, boff):
    s = []
    for j in range(_BAND_TILES):
        kj = k_ref[pl.ds(pl.multiple_of((j0 + j) * _TQ, _TQ), _TQ), :]
        s.append(_dot_nt(q2, kj) + b_ref[boff + j])
    m = jnp.max(functools.reduce(jnp.maximum, s), axis=-1, keepdims=True)
    p = [jnp.exp(x - m) for x in s]
    l = jnp.sum(functools.reduce(lambda a, b: a + b, p), axis=-1, keepdims=True)
    return p, 1.0 / l


def _qkv_specs(T, cb, npair, tq=_TQ):
    return [pl.BlockSpec((tq, _LANES), lambda h, i: (i, cb + h)),
            pl.BlockSpec((T, _LANES), lambda h, i: (0, cb + npair + h)),
            pl.BlockSpec((T, _LANES), lambda h, i: (0, cb + 2 * npair + h))]


def _attn_a_fwd(hq, bias, col0, width, layer, comm=None):
    T = hq.shape[0]
    npair = width // _LANES
    nsub = _BAND_SUBTILES
    tq = nsub * _TQ

    def kern(q_ref, k_ref, v_ref, b_ref, o_ref, b_tiles):
        @pl.when(pl.program_id(1) == 0)
        def _():
            _strip_tiles(b_ref, b_tiles)

        masks = _head_masks()
        q = _scaled(q_ref[...])
        for s in range(nsub):
            part = slice(s * _TQ, (s + 1) * _TQ)
            j0, boff = _band_window(nsub * pl.program_id(1) + s)
            p, inv = _band_probs(_stack_heads(q[part], masks), k_ref, b_tiles, j0, boff)
            o = jnp.zeros((2 * _TQ, _LANES), _F32)
            for j in range(_BAND_TILES):
                vj = v_ref[pl.ds(pl.multiple_of((j0 + j) * _TQ, _TQ), _TQ), :]
                o = o + _dot(p[j].astype(_MXU), vj)
            o_ref[part, :] = _unstack_heads(o * inv, masks).astype(o_ref.dtype)

    outs, extra = _call(
        kern, comm, name=f"band_attn_fwd_{layer}", grid=(npair, T // tq),
        in_specs=_qkv_specs(T, col0 // _LANES, npair, tq)
        + [pl.BlockSpec((None, 2 * _TQ, _BIAS_TILES * _TQ), lambda h, i: (h, 0, 0))],
        out_specs=[pl.BlockSpec((tq, _LANES), lambda h, i: (i, h))],
        out_shape=[jax.ShapeDtypeStruct((T, width), _ACT)],
        scratch_shapes=[pltpu.VMEM((_BIAS_TILES, 2 * _TQ, _TQ), _F32)],
        args=(hq, hq, hq, bias), semantics=("arbitrary", "arbitrary"))
    return outs[0], extra


def _attn_a_bwd(hq, bias, do, col0, width, layer, comm=None):
    T = hq.shape[0]
    npair = width // _LANES
    nsub = _BAND_SUBTILES
    tq = nsub * _TQ
    nq = T // tq
    scale = _HEAD ** -0.5

    def kern(q_ref, k_ref, v_ref, b_ref, do_ref, dq_ref, dk_ref, dv_ref, db_ref, dk_acc, dv_acc, b_tiles, db_acc):
        i = pl.program_id(1)

        @pl.when(i == 0)
        def _():
            _strip_tiles(b_ref, b_tiles)
            dk_acc[...] = jnp.zeros_like(dk_acc)
            dv_acc[...] = jnp.zeros_like(dv_acc)
            db_acc[...] = jnp.zeros_like(db_acc)

        masks = _head_masks()
        q = _scaled(q_ref[...])
        do_t = do_ref[...]
        for s in range(nsub):
            part = slice(s * _TQ, (s + 1) * _TQ)
            j0, boff = _band_window(nsub * i + s)
            q2 = _stack_heads(q[part], masks)
            do2 = _stack_heads(do_t[part], masks).astype(_MXU)
            p, inv = _band_probs(q2, k_ref, b_tiles, j0, boff)
            rows = [pl.ds(pl.multiple_of((j0 + j) * _TQ, _TQ), _TQ) for j in range(_BAND_TILES)]
            p = [x * inv for x in p]
            dp = [_dot_nt(do2, v_ref[rows[j], :]) for j in range(_BAND_TILES)]
            delta = jnp.sum(functools.reduce(lambda a, b: a + b, [p[j] * dp[j] for j in range(_BAND_TILES)]),
                            axis=-1, keepdims=True)
            dq = jnp.zeros((2 * _TQ, _LANES), _F32)
            for j in range(_BAND_TILES):
                ds = p[j] * (dp[j] - delta)
                db_acc[boff + j] += ds
                dsb = ds.astype(_MXU)
                dq = dq + _dot(dsb, k_ref[rows[j], :])
                dk_acc[rows[j], :] += _dot_tn(dsb, q2)
                dv_acc[rows[j], :] += _dot_tn(p[j].astype(_MXU), do2)
            dq_ref[part, :] = (_unstack_heads(dq, masks) * scale).astype(dq_ref.dtype)

        @pl.when(i == nq - 1)
        def _():
            dk_ref[...] = dk_acc[...].astype(dk_ref.dtype)
            dv_ref[...] = dv_acc[...].astype(dv_ref.dtype)
            _strip_tiles(db_ref, db_acc, to_strip=True)

    strip = pl.BlockSpec((None, 2 * _TQ, _BIAS_TILES * _TQ), lambda h, i: (h, 0, 0))
    tile = pl.BlockSpec((tq, _LANES), lambda h, i: (i, h))
    column = pl.BlockSpec((T, _LANES), lambda h, i: (0, h))
    outs, extra = _call(
        kern, comm, name=f"band_attn_bwd_{layer}", grid=(npair, nq),
        in_specs=_qkv_specs(T, col0 // _LANES, npair, tq) + [strip, tile],
        out_specs=[tile, column, column, strip],
        out_shape=[jax.ShapeDtypeStruct((T, width), _ACT)] * 3
        + [jax.ShapeDtypeStruct((npair, 2 * _TQ, _BIAS_TILES * _TQ), _F32)],
        scratch_shapes=[pltpu.VMEM((T, _LANES), _F32), pltpu.VMEM((T, _LANES), _F32),
                        pltpu.VMEM((_BIAS_TILES, 2 * _TQ, _TQ), _F32), pltpu.VMEM((_BIAS_TILES, 2 * _TQ, _TQ), _F32)],
        args=(hq, hq, hq, bias, do), semantics=("arbitrary", "arbitrary"))
    return outs, extra


def _suffix_matrix():
    r = lax.broadcasted_iota(jnp.int32, (_TQ, _TQ), 0)
    c = lax.broadcasted_iota(jnp.int32, (_TQ, _TQ), 1)
    r2 = lax.broadcasted_iota(jnp.int32, (2 * _TQ, _TQ), 0)
    c2 = lax.broadcasted_iota(jnp.int32, (2 * _TQ, _TQ), 1)
    return (r > c).astype(_MXU), c2 - (r2 & (_TQ - 1))


def _suffix_sums(xs, tri):
    n, k = xs[0].shape[0], len(xs)
    his = [x.astype(_MXU) for x in xs]
    los = [(x - h.astype(_F32)).astype(_MXU) for x, h in zip(xs, his)]
    y = _dot(jnp.concatenate(his + los, axis=0), tri)
    return [y[j * n:(j + 1) * n] + y[(k + j) * n:(k + j + 1) * n] for j in range(k)]


def _stick_tiles(tiles, rel, carry_l, tri):
    zs = [_dot_nt(qs, kj) for qs, kj, _, _ in tiles]
    Ls, masks = [], []
    for z, (_, _, jj, _) in zip(zs, tiles):
        nsp = -(jnp.maximum(z, 0.0) + jnp.log(1.0 + jnp.exp(-jnp.abs(z))))
        if isinstance(jj, int):
            mask = (rel < 0) if jj == 0 else None
        else:
            mask = rel < jnp.where(jj == 0, 0, _TQ)
        Ls.append(nsp if mask is None else jnp.where(mask, nsp, 0.0))
        masks.append(mask)
    carry_l = list(carry_l)
    ws = []
    for z, L, suffix, mask, (_, _, _, sub) in zip(zs, Ls, _suffix_sums(Ls, tri), masks, tiles):
        w = jnp.exp(z + L + suffix + carry_l[sub])
        ws.append(w if mask is None else jnp.where(mask, w, 0.0))
        carry_l[sub] = carry_l[sub] + jnp.sum(L, axis=-1, keepdims=True)
    return zs, Ls, ws, masks, carry_l


def _sweep(i, step, zero):
    nsub = _SB_SUBTILES

    def window():
        tiles = [(s, jj) for jj in range(_SB_WINDOW) for s in range(nsub)]
        return tuple((jnp.int32(_SB_WINDOW),) + c for c in step(tiles, [zero] * nsub))

    start = lax.cond(i >= -(-(_SB_WINDOW - 1) // nsub), window, lambda: tuple((jnp.int32(0),) + zero for _ in range(nsub)))
    outs = []
    for s in range(nsub):
        def done(c, s=s):
            return jnp.logical_or(c[0] > nsub * i + s, jnp.max(c[1]) < _EXP_ZERO_BELOW)

        def more(c, s=s):
            carries = [None] * nsub
            carries[s] = c[1:]
            return (c[0] + 1,) + step([(s, c[0])], carries)[s]

        outs.append(lax.while_loop(lambda c, done=done: jnp.logical_not(done(c)), more, start[s]))
    return outs


def _sb_fwd(hq, col0, width, layer, comm=None):
    T = hq.shape[0]
    npair = width // _LANES
    nsub = _SB_SUBTILES
    tq = nsub * _TQ

    def kern(q_ref, k_ref, v_ref, o_ref):
        i = pl.program_id(1)
        masks = _head_masks()
        tri, rel = _suffix_matrix()
        q = _scaled(q_ref[...])
        q2 = [_stack_heads(q[s * _TQ:(s + 1) * _TQ], masks) for s in range(nsub)]

        def step(tiles, carries):
            rows = [pl.ds(pl.multiple_of((nsub * i + s - jj) * _TQ, _TQ), _TQ) for s, jj in tiles]
            cls = [None if c is None else c[0] for c in carries]
            accs = [None if c is None else c[1] for c in carries]
            _, _, ws, _, cls = _stick_tiles([(q2[s], k_ref[r, :], jj, s) for (s, jj), r in zip(tiles, rows)], rel, cls, tri)
            for w, r, (s, _) in zip(ws, rows, tiles):
                accs[s] = accs[s] + _dot(w.astype(_MXU), v_ref[r, :])
            return [None if c is None else (cls[s], accs[s]) for s, c in enumerate(carries)]

        outs = _sweep(i, step, (jnp.zeros((2 * _TQ, 1), _F32), jnp.zeros((2 * _TQ, _LANES), _F32)))
        for s in range(nsub):
            o_ref[s * _TQ:(s + 1) * _TQ, :] = _unstack_heads(outs[s][2], masks)

    outs, extra = _call(
        kern, comm, name=f"stick_attn_fwd_{layer}", grid=(npair, T // tq),
        in_specs=_qkv_specs(T, col0 // _LANES, npair, tq),
        out_specs=[pl.BlockSpec((tq, _LANES), lambda h, i: (i, h))],
        out_shape=[jax.ShapeDtypeStruct((T, width), _F32)], scratch_shapes=[],
        args=(hq, hq, hq), semantics=("arbitrary", "arbitrary"))
    return outs[0], extra


def _sb_bwd(hq, o, do, col0, width, layer, comm=None):
    T = hq.shape[0]
    npair = width // _LANES
    nsub = _SB_SUBTILES
    tq = nsub * _TQ
    nq = T // tq
    scale = _HEAD ** -0.5

    def kern(q_ref, k_ref, v_ref, o_ref, do_ref, dq_ref, dk_ref, dv_ref, dk_acc, dv_acc):
        i = pl.program_id(1)

        @pl.when(i == 0)
        def _():
            dk_acc[...] = jnp.zeros_like(dk_acc)
            dv_acc[...] = jnp.zeros_like(dv_acc)

        masks = _head_masks()
        tri, rel = _suffix_matrix()
        q = _scaled(q_ref[...])
        do_t = do_ref[...]
        prod = do_t.astype(_F32) * o_ref[...]
        part = [slice(s * _TQ, (s + 1) * _TQ) for s in range(nsub)]
        q2 = [_stack_heads(q[p], masks) for p in part]
        do2 = [_stack_heads(do_t[p], masks).astype(_MXU) for p in part]
        dsum = [jnp.sum(_stack_heads(prod[p], masks), axis=-1, keepdims=True) for p in part]

        def step(tiles, carries):
            rows = [pl.ds(pl.multiple_of((nsub * i + s - jj) * _TQ, _TQ), _TQ) for s, jj in tiles]
            kjs = [k_ref[r, :] for r in rows]
            cls, cgs, dqs = ([None if c is None else c[n] for c in carries] for n in range(3))
            zs, Ls, ws, tile_masks, cls = _stick_tiles([(q2[s], kj, jj, s) for (s, jj), kj in zip(tiles, kjs)], rel, cls, tri)
            wbs = [w.astype(_MXU) for w in ws]
            gs = [wb.astype(_F32) * _dot_nt(do2[s], v_ref[r, :]) for wb, r, (s, _) in zip(wbs, rows, tiles)]
            for z, L, g, later, mask, wb, kj, r, (s, _) in zip(zs, Ls, gs, _suffix_sums(gs, tri), tile_masks, wbs, kjs,
                                                               rows, tiles):
                dz = g - jnp.exp(z + L) * (dsum[s] - (later + cgs[s]))
                if mask is not None:
                    dz = jnp.where(mask, dz, 0.0)
                dzb = dz.astype(_MXU)
                dk_acc[r, :] += _dot_tn(dzb, q2[s])
                dv_acc[r, :] += _dot_tn(wb, do2[s])
                dqs[s] = dqs[s] + _dot(dzb, kj)
                cgs[s] = cgs[s] + jnp.sum(g, axis=-1, keepdims=True)
            return [None if c is None else (cls[s], cgs[s], dqs[s]) for s, c in enumerate(carries)]

        zc = jnp.zeros((2 * _TQ, 1), _F32)
        outs = _sweep(i, step, (zc, zc, jnp.zeros((2 * _TQ, _LANES), _F32)))
        for s in range(nsub):
            dq_ref[part[s], :] = (_unstack_heads(outs[s][3], masks) * scale).astype(dq_ref.dtype)

        @pl.when(i == nq - 1)
        def _():
            dk_ref[...] = dk_acc[...].astype(dk_ref.dtype)
            dv_ref[...] = dv_acc[...].astype(dv_ref.dtype)

    tile_spec = pl.BlockSpec((tq, _LANES), lambda h, i: (i, h))
    column = pl.BlockSpec((T, _LANES), lambda h, i: (0, h))
    outs, extra = _call(
        kern, comm, name=f"stick_attn_bwd_{layer}", grid=(npair, nq),
        in_specs=_qkv_specs(T, col0 // _LANES, npair, tq) + [tile_spec, tile_spec],
        out_specs=[tile_spec, column, column],
        out_shape=[jax.ShapeDtypeStruct((T, width), _ACT)] * 3,
        scratch_shapes=[pltpu.VMEM((T, _LANES), _F32), pltpu.VMEM((T, _LANES), _F32)],
        args=(hq, hq, hq, o, do), semantics=("arbitrary", "arbitrary"))
    return outs, extra


_DENSE = ("w_in", "w_proj_a", "w_proj_b", "w_out", "w_ffn_in", "w_ffn_out")
_COL_SHARDED = {"w_in": True, "w_proj_a": True, "w_proj_b": True, "w_out": False, "w_ffn_in": True, "w_ffn_out": False}
_SMALL = ("b_gate", "rel_bias", "ln1_g", "ln1_b", "ln2_g", "ln2_b")


class _Plans:
    def __init__(self, plans=None, own_w_in=None):
        self.plans = plans or {}
        self.own_w_in = own_w_in or {}

    def start(self, key):
        if key not in self.plans:
            return None, None
        return self.plans[key]()

    @staticmethod
    def finish(done, extra):
        if done is not None:
            done(extra)


def _layer_fwd(x, W, small, l, alpha, plans):
    WA = small["rel_bias"].shape[1] * _HEAD
    row = lambda v: v[l].reshape(1, -1)
    if l in plans.own_w_in:
        h, xb, W["w_in"] = _in_proj_gathering(x, plans.own_w_in[l], l)
    else:
        comm, done = plans.start(f"in_proj_{l}")
        (h, xb), extra = _in_proj(x, W["w_in"], l, comm)
        plans.finish(done, extra)
    WB = (h.shape[1] - 2 * x.shape[1] - 3 * WA) // 3
    bias = _bias_tiles(small["rel_bias"][l])
    comm, done = plans.start(f"band_fwd_{l}")
    oa, extra = _attn_a_fwd(h, bias, 0, WA, l, comm)
    plans.finish(done, extra)
    comm, done = plans.start(f"stick_fwd_{l}")
    ob, extra = _sb_fwd(h, 3 * WA, WB, l, comm)
    plans.finish(done, extra)
    comm, done = plans.start(f"mix_fwd_{l}")
    (x1, u1, pre), extra = _mix_fwd(oa, ob, h, x, W["w_proj_a"], W["w_proj_b"], W["w_out"], row(small["b_gate"]),
                                            row(small["ln1_g"]), row(small["ln1_b"]), alpha, l, comm)
    plans.finish(done, extra)
    comm, done = plans.start(f"ffn_fwd_{l}")
    (x2, u2, act, gu, x1b), extra = _ffn_fwd(x1, W["w_ffn_in"], W["w_ffn_out"], row(small["ln2_g"]),
                                             row(small["ln2_b"]), alpha, l, comm)
    plans.finish(done, extra)
    return x2, dict(xb=xb, h=h, bias=bias, oa=oa, ob=ob, x1b=x1b, u1=u1, pre=pre, u2=u2, act=act, gu=gu)


def _layer_bwd(dy_or_target, S, W, small, l, last, alpha, plans, gw):
    D = S["xb"].shape[1]
    WA, WB = S["oa"].shape[1], S["ob"].shape[1]
    row = lambda v: v[l].reshape(1, -1)

    def blocks(g, n):
        return g if _COL_SHARDED[n] else g.reshape(4, g.shape[0] // 4, g.shape[1])

    dx1, du2b, dgu, st2 = _ffn_bwd(S["u2"], dy_or_target, S["gu"], row(small["ln2_g"]), row(small["ln2_b"]),
                                   W["w_ffn_in"], W["w_ffn_out"], alpha, l, last)
    gw["w_ffn_in"] = blocks(_grad_w(S["x1b"], dgu, col_shards=True, name=f"grad_w_ffn_in_{l}")[0], "w_ffn_in")
    gw["w_ffn_out"] = blocks(_grad_w(S["act"], du2b, col_shards=False, name=f"grad_w_ffn_out_{l}")[0], "w_ffn_out")
    du1, du1b, dya, dyb, dhg, doa, dob, st1 = _mix_bwd(S["u1"], dx1, S["oa"], S["ob"], S["h"], W["w_proj_a"],
                                                       W["w_proj_b"], W["w_out"], row(small["b_gate"]),
                                                       row(small["ln1_g"]), l)
    gw["w_out"] = blocks(_grad_w(S["pre"], du1b, col_shards=False, name=f"grad_w_out_{l}")[0], "w_out")
    gw["w_proj_a"] = blocks(_grad_w(S["oa"], dya, col_shards=True, name=f"grad_w_proj_a_{l}")[0], "w_proj_a")
    gw["w_proj_b"] = blocks(_grad_w(S["ob"], dyb, col_shards=True, name=f"grad_w_proj_b_{l}")[0], "w_proj_b")
    comm, done = plans.start(f"band_bwd_{l}")
    (dqa, dka, dva, dbias), extra = _attn_a_bwd(S["h"], S["bias"], doa, 0, WA, l, comm)
    plans.finish(done, extra)
    comm, done = plans.start(f"stick_bwd_{l}")
    (dqb, dkb, dvb), extra = _sb_bwd(S["h"], S["ob"], dob, 3 * WA, WB, l, comm)
    plans.finish(done, extra)
    dh = [dqa, dka, dva, dqb, dkb, dvb, dhg]
    comm, done = plans.start(f"grad_w_in_{l}")
    gw["w_in"], extra = _grad_w_pieces(S["xb"], dh, f"grad_w_in_{l}", comm)
    plans.finish(done, extra)
    comm, done = plans.start(f"in_proj_bwd_{l}")
    dx, extra = _residual_nt(du1, alpha, dh, W["w_in"], f"in_proj_bwd_{l}", comm)
    plans.finish(done, extra)
    gs = dict(b_gate=st1[0], rel_bias=_fold_bias_grad(dbias), ln1_g=st1[1, :D], ln1_b=st1[1, D:],
              ln2_g=st2[0], ln2_b=st2[1])
    return dx, gs, st2[2]


def _local_step(x, target, W, small, plans=None, gws=None):
    depth = len(W)
    alpha = float((2 * depth) ** 0.25)
    plans = plans or _Plans()
    gws = gws if gws is not None else [dict() for _ in range(depth)]
    saved = []
    h = x
    for l in range(depth):
        h, S = _layer_fwd(h, W[l], small, l, alpha, plans)
        saved.append(S)
    gss = [None] * depth
    d = target
    sq = None
    for l in reversed(range(depth)):
        d, gss[l], sq_l = _layer_bwd(d, saved[l], W[l], small, l, l == depth - 1, alpha, plans, gws[l])
        if l == depth - 1:
            sq = sq_l
    return sq, d, gws, gss


def _place():
    return lax.axis_index("x"), lax.axis_index("y"), lax.axis_index("c")


def _remote(src, dst, send_sem, recv_sem, to):
    return pltpu.make_async_remote_copy(src_ref=src, dst_ref=dst, send_sem=send_sem, recv_sem=recv_sem,
                                        device_id=to, device_id_type=_MESH)


def _half(ref, hc):
    kh = ref.shape[0] // 2
    return ref.at[pl.ds(pl.multiple_of(hc * kh, 16), kh), :]


def _gather_plan(blocks, fractions):
    nt = len(blocks)

    def run(step, nsteps, ins, outs, sems):
        send_sems, recv_sems, loc_sems = sems
        x, y, c = _place()
        k = 2 * x + y
        me, sibling = (x, y, c), (x, y, 1 - c)
        chips = [(1 - x, y), (x, 1 - y), (1 - x, 1 - y)]
        chip_k = [2 * cx + cy for cx, cy in chips]

        def ici(t, s, owner_k, to, src=None):
            dst = _half(outs[t].at[owner_k], c)
            return _remote(dst if src is None else src, dst, send_sems.at[t, s], recv_sems.at[t, s], to)

        def passed(t, s, hc, to):
            blk = _half(outs[t].at[chip_k[s]], hc)
            return _remote(blk, blk, send_sems.at[t, 3 + s], recv_sems.at[t, 3 + s], to)

        def local(t):
            return pltpu.make_async_copy(ins[t], outs[t].at[k], loc_sems.at[t])

        @pl.when(step == 0)
        def _():
            for t in range(nt):
                local(t).start()
                for s, chip in enumerate(chips):
                    ici(t, s, k, (*chip, c), src=_half(ins[t], c)).start()

        for t in range(nt):
            @pl.when(step == min(nsteps - 1, int(fractions[t] * nsteps)))
            def _():
                for s in range(3):
                    ici(t, s, chip_k[s], me).wait_recv()
                    passed(t, s, c, sibling).start()

        @pl.when(step == nsteps - 1)
        def _():
            for t in range(nt):
                for s, chip in enumerate(chips):
                    passed(t, s, 1 - c, me).wait_recv()
            for t in range(nt):
                for s, chip in enumerate(chips):
                    ici(t, s, k, (*chip, c), src=_half(ins[t], c)).wait_send()
                    passed(t, s, c, sibling).wait_send()
                local(t).wait()

    return _Comm(blocks, [jax.ShapeDtypeStruct((4,) + b.shape, b.dtype) for b in blocks],
                 [pltpu.SemaphoreType.DMA((nt, 6)), pltpu.SemaphoreType.DMA((nt, 6)), pltpu.SemaphoreType.DMA((nt,))], run)


def _scatter_plan(grads, owners):
    nt = len(grads)
    shapes = [g.shape[1:] if g.ndim == 3 else (g.shape[0], g.shape[1] // 4) for g in grads]

    def run(step, nsteps, ins, outs, sems):
        send_sems, recv_sems, loc_sems = sems
        x, y, c = _place()
        me = 4 * x + 2 * y + c

        def target(r):
            tx = 1 - x if r & 2 else x
            ty = 1 - y if r & 1 else y
            return tx, ty

        def block(t, chip):
            if len(ins[t].shape) == 3:
                return ins[t].at[chip]
            n = shapes[t][1]
            return ins[t].at[:, pl.ds(pl.multiple_of(chip * n, _LANES), n)]

        def send(t, r):
            tx, ty = target(r)
            return _remote(block(t, 2 * tx + ty), outs[t].at[me], send_sems.at[t, r], recv_sems.at[t, 2 * r + c],
                           (tx, ty, owners[t]))

        def local(t):
            return pltpu.make_async_copy(block(t, 2 * x + y), outs[t].at[me], loc_sems.at[t])

        @pl.when(step == 0)
        def _():
            for t in range(nt):
                @pl.when(c == owners[t])
                def _():
                    local(t).start()

                @pl.when(c != owners[t])
                def _():
                    send(t, 0).start()

                for r in range(1, 4):
                    send(t, r).start()

        @pl.when(step == nsteps - 1)
        def _():
            for t in range(nt):
                @pl.when(c == owners[t])
                def _():
                    for r in range(4):
                        sx, sy = target(r)
                        for cs in range(2):
                            if r == 0 and cs == owners[t]:
                                continue
                            src_dev = 4 * sx + 2 * sy + cs
                            _remote(block(t, 0), outs[t].at[src_dev], send_sems.at[t, r], recv_sems.at[t, 2 * r + cs],
                                    (x, y, c)).wait_recv()
                    local(t).wait()

                @pl.when(c != owners[t])
                def _():
                    send(t, 0).wait_send()

                for r in range(1, 4):
                    send(t, r).wait_send()

    return _Comm(grads, [jax.ShapeDtypeStruct((8,) + s, g.dtype) for s, g in zip(shapes, grads)],
                 [pltpu.SemaphoreType.DMA((nt, 4)), pltpu.SemaphoreType.DMA((nt, 8)), pltpu.SemaphoreType.DMA((nt,))], run)


def _share_plan(reduced, owners):
    nt = len(reduced)

    def run(step, nsteps, ins, outs, sems):
        del ins
        send_sems, recv_sems = sems
        x, y, c = _place()

        def give(t, to):
            return _remote(outs[t], outs[t], send_sems.at[t], recv_sems.at[t], to)

        @pl.when(step == 0)
        def _():
            for t in range(nt):
                @pl.when(c == owners[t])
                def _():
                    give(t, (x, y, 1 - c)).start()

        @pl.when(step == nsteps - 1)
        def _():
            for t in range(nt):
                @pl.when(c == owners[t])
                def _():
                    give(t, (x, y, 1 - c)).wait_send()

                @pl.when(c != owners[t])
                def _():
                    give(t, (x, y, c)).wait_recv()

    return _Comm(reduced, [jax.ShapeDtypeStruct(r.shape, r.dtype) for r in reduced],
                 [pltpu.SemaphoreType.DMA((nt,)), pltpu.SemaphoreType.DMA((nt,))], run,
                 aliases={t: t for t in range(nt)})


def _join(a, b):
    ni, no, ns = len(a.inputs), len(a.out_shapes), len(a.sems)

    def run(step, nsteps, ins, outs, sems):
        a.run(step, nsteps, ins[:ni], outs[:no], sems[:ns])
        b.run(step, nsteps, ins[ni:], outs[no:], sems[ns:])

    aliases = dict(a.aliases)
    aliases.update({ni + i: no + o for i, o in b.aliases.items()})
    return _Comm(a.inputs + b.inputs, a.out_shapes + b.out_shapes, a.sems + b.sems, run, aliases)


def _peer(x, y, c, r):
    px = 1 - x if r & 4 else x
    py = 1 - y if r & 2 else y
    pc = 1 - c if r & 1 else c
    return (px, py, pc), 4 * px + 2 * py + pc


def _sum_slots(st, name):
    _, K, n = st.shape
    tr = next(t for t in (256, 128, 64, 32, 16) if K % t == 0)

    def kern(s_ref, o_ref):
        acc = s_ref[0].astype(_F32)
        for d in range(1, 8):
            acc = acc + s_ref[d].astype(_F32)
        o_ref[...] = acc.astype(o_ref.dtype)

    return pl.pallas_call(
        kern, name=name, grid=(K // tr,),
        in_specs=[pl.BlockSpec((8, tr, n), lambda i: (0, i, 0))], out_specs=_rows(tr, n),
        out_shape=jax.ShapeDtypeStruct((K, n), _ACT),
        compiler_params=_cparams("parallel"),
    )(st)


def _all_reduce_small(p):
    R = p.shape[0]

    def body(p_ref, o_ref, stage, send_sems, recv_sems):
        x, y, c = _place()
        me = 4 * x + 2 * y + c
        stage[me] = p_ref[...]
        sent = []
        for r in range(1, 8):
            to, _ = _peer(x, y, c, r)
            cp = _remote(p_ref, stage.at[me], send_sems.at[r - 1], recv_sems.at[r - 1], to)
            cp.start()
            sent.append(cp)
        for r in range(1, 8):
            _, src_dev = _peer(x, y, c, r)
            _remote(p_ref, stage.at[src_dev], send_sems.at[r - 1], recv_sems.at[r - 1], (x, y, c)).wait_recv()
        acc = stage[0]
        for d in range(1, 8):
            acc = acc + stage[d]
        o_ref[...] = acc
        for cp in sent:
            cp.wait_send()

    vm = pl.BlockSpec(memory_space=pltpu.VMEM)
    return pl.pallas_call(
        body, name="all_reduce_small",
        in_specs=[vm], out_specs=vm,
        out_shape=jax.ShapeDtypeStruct((R, _LANES), _F32),
        scratch_shapes=[pltpu.VMEM((8, R, _LANES), _F32), pltpu.SemaphoreType.DMA((7,)), pltpu.SemaphoreType.DMA((7,))],
    )(p)


def _adamw_update(gv, w_ref, m_ref, v_ref, gf_ref, d_ref, nm_ref, nv_ref):
    nm = _B1 * m_ref[...] + (1.0 - _B1) * gv
    nv = _B2 * v_ref[...] + (1.0 - _B2) * (gv * gv)
    m_hat = nm / (1.0 - _B1 ** _STEP)
    v_hat = nv / (1.0 - _B2 ** _STEP)
    gf_ref[...] = gv
    d_ref[...] = -_LR * (m_hat / (jnp.sqrt(v_hat) + _EPS) + _WD * w_ref[...])
    nm_ref[...] = nm
    nv_ref[...] = nv


def _adamw_layers(w, g_layers, m, v, name):
    _, K, n = w.shape
    tr = next(t for t in (256, 128, 64, 32, 16) if K % t == 0)

    def kern(w_ref, g0_ref, g1_ref, m_ref, v_ref, *out_refs):
        first = pl.program_id(0) == 0
        gv = jnp.where(first, g0_ref[...].astype(_F32), g1_ref[...].astype(_F32))
        _adamw_update(gv, w_ref, m_ref, v_ref, *out_refs)

    stacked = pl.BlockSpec((None, tr, n), lambda l, i: (l, i, 0))
    layer = pl.BlockSpec((tr, n), lambda l, i: (i, 0))
    return tuple(pl.pallas_call(
        kern, name=name, grid=(2, K // tr),
        in_specs=[stacked, layer, layer, stacked, stacked], out_specs=[stacked] * 4,
        out_shape=[jax.ShapeDtypeStruct(w.shape, _F32)] * 4,
        compiler_params=_cparams("parallel", "parallel"),
    )(w, g_layers[0], g_layers[1], m, v))


def _adamw(w, g, m, v, name):
    shape = w.shape
    w2, g2, m2, v2 = (a.reshape(-1, shape[-1]) for a in (w, g, m, v))
    R, C = w2.shape
    tr = next((t for t in (256, 128, 64, 32, 16) if R % t == 0), R)

    def kern(w_ref, g_ref, m_ref, v_ref, *out_refs):
        _adamw_update(g_ref[...].astype(_F32), w_ref, m_ref, v_ref, *out_refs)

    outs = pl.pallas_call(
        kern, name=name, grid=(R // tr,),
        in_specs=[_rows(tr, C)] * 4, out_specs=[_rows(tr, C)] * 4,
        out_shape=[jax.ShapeDtypeStruct((R, C), _F32)] * 4,
        compiler_params=_cparams("parallel"),
    )(w2, g2, m2, v2)
    return tuple(o.reshape(shape) for o in outs)


def _pack_small(gss, sq):
    parts = [gss[l][n].reshape(-1) for n in _SMALL for l in range(len(gss))] + [jnp.sum(sq).reshape(1)]
    flat = jnp.concatenate(parts)
    rows = -(-flat.shape[0] // (8 * _LANES)) * 8
    return jnp.pad(flat, (0, rows * _LANES - flat.shape[0])).reshape(rows, _LANES)


def _unpack_small(total, shapes):
    flat = total.reshape(-1)
    out, off = {}, 0
    for n in _SMALL:
        layers = []
        for _ in range(shapes[n][0]):
            size = 1
            for s in shapes[n][1:]:
                size *= s
            layers.append(flat[off:off + size].reshape(shapes[n][1:]))
            off += size
        out[n] = jnp.stack(layers)
    return out, flat[off]


_GATHER = {
    "band_fwd_0": [(0, "w_proj_a"), (0, "w_proj_b"), (0, "w_out"), (0, "w_ffn_out")],
    "stick_fwd_0": [(0, "w_ffn_in"), (1, "w_proj_a"), (1, "w_proj_b"), (1, "w_out")],
    "mix_fwd_0": [(1, "w_ffn_out")],
    "ffn_fwd_0": [(1, "w_in"), (1, "w_ffn_in")],
}
_SCATTER = {
    "band_bwd_1": [(1, "w_ffn_in"), (1, "w_ffn_out")],
    "stick_bwd_1": [(1, "w_proj_a"), (1, "w_proj_b"), (1, "w_out")],
    "band_bwd_0": [(1, "w_in"), (0, "w_ffn_in")],
    "stick_bwd_0": [(0, "w_ffn_out"), (0, "w_proj_a"), (0, "w_proj_b"), (0, "w_out")],
    "in_proj_bwd_0": [(0, "w_in")],
}
_SHARE = {"stick_bwd_1": "band_bwd_1", "band_bwd_0": "stick_bwd_1", "stick_bwd_0": "band_bwd_0", "grad_w_in_0": "stick_bwd_0"}


def _owner(key):
    del key
    return 1


def kernel(x, w_in, b_gate, rel_bias, w_proj_a, w_proj_b, w_out, ln1_g, ln1_b, w_ffn_in, w_ffn_out, ln2_g, ln2_b, loss_target, m_w_in, m_b_gate, m_rel_bias, m_w_proj_a, m_w_proj_b, m_w_out, m_ln1_g, m_ln1_b, m_w_ffn_in, m_w_ffn_out, m_ln2_g, m_ln2_b, v_w_in, v_b_gate, v_rel_bias, v_w_proj_a, v_w_proj_b, v_w_out, v_ln1_g, v_ln1_b, v_w_ffn_in, v_w_ffn_out, v_ln2_g, v_ln2_b):
    names = ("w_in", "b_gate", "rel_bias", "w_proj_a", "w_proj_b", "w_out", "ln1_g", "ln1_b", "w_ffn_in", "w_ffn_out", "ln2_g", "ln2_b")
    w = dict(zip(names, (w_in, b_gate, rel_bias, w_proj_a, w_proj_b, w_out, ln1_g, ln1_b, w_ffn_in, w_ffn_out, ln2_g, ln2_b)))
    m = dict(zip(names, (m_w_in, m_b_gate, m_rel_bias, m_w_proj_a, m_w_proj_b, m_w_out, m_ln1_g, m_ln1_b, m_w_ffn_in, m_w_ffn_out, m_ln2_g, m_ln2_b)))
    v = dict(zip(names, (v_w_in, v_b_gate, v_rel_bias, v_w_proj_a, v_w_proj_b, v_w_out, v_ln1_g, v_ln1_b, v_w_ffn_in, v_w_ffn_out, v_ln2_g, v_ln2_b)))
    T, D = x.shape[-2], x.shape[-1]
    assert w_in.shape[0] == 2, "the exchange schedule below is written for two layers"

    mine = [{n: w[n][l].astype(_MXU) for n in _DENSE} for l in range(2)]
    W = [dict(), dict()]
    gws = [dict(), dict()]
    slots, final = {}, {}

    def gather(keys):
        sizes = [mine[l][n].size for l, n in keys]
        passed, fractions = 0, []
        for s in sizes:
            passed += s
            fractions.append(0.15 + 0.6 * passed / sum(sizes))

        def done(outs):
            for (l, n), o in zip(keys, outs):
                W[l][n] = o
        return _gather_plan([mine[l][n] for l, n in keys], fractions), done

    def scatter(keys):
        comm = _scatter_plan([gws[l][n] for l, n in keys], [_owner(key) for key in keys])
        return comm, lambda outs: slots.update(zip(keys, outs))

    def share(keys):
        reduced = [_sum_slots(slots[key], f"sum_grad_{key[1]}_{key[0]}") for key in keys]
        comm = _share_plan(reduced, [_owner(key) for key in keys])
        return comm, lambda outs: final.update(zip(keys, outs))

    def both(first, second):
        (ca, da), (cb, db) = first, second
        na = len(ca.out_shapes)
        return _join(ca, cb), lambda outs: (da(outs[:na]), db(outs[na:]))

    plans = {key: functools.partial(gather, keys) for key, keys in _GATHER.items()}
    for key, keys in _SCATTER.items():
        plans[key] = functools.partial(scatter, keys)
    for key, scattered_under in _SHARE.items():
        handed = functools.partial(share, _SCATTER[scattered_under])
        carried = plans.get(key)
        plans[key] = handed if carried is None else (lambda carried=carried, handed=handed: both(carried(), handed()))
    small = {n: w[n] for n in _SMALL}
    sq, dx, _, gss = _local_step(x.reshape(T, D), loss_target.reshape(T, D), W, small,
                                  _Plans(plans, {0: mine[0]["w_in"]}), gws)

    comm, done = share(_SCATTER["in_proj_bwd_0"])
    done(_comm_only(comm, "share_last"))
    total = _all_reduce_small(_pack_small(gss, sq))
    small_grads, sq_all = _unpack_small(total, {n: w[n].shape for n in _SMALL})
    loss = 0.5 * sq_all / D

    grad, delta, new_m, new_v = {}, {}, {}, {}
    for n in names:
        if n in _DENSE:
            updated = _adamw_layers(w[n], [final[(l, n)] for l in range(2)], m[n], v[n], f"adamw_{n}")
        else:
            updated = _adamw(w[n], small_grads[n], m[n], v[n], f"adamw_{n}")
        grad[n], delta[n], new_m[n], new_v[n] = updated
    return (loss, dx.reshape(x.shape), *[grad[n] for n in names], *[delta[n] for n in names],
            *[new_m[n] for n in names], *[new_v[n] for n in names])
```

```python
import functools

import jax
import jax.numpy as jnp
from jax import lax
from jax.experimental import pallas as pl
from jax.experimental.pallas import tpu as pltpu

_MXU = jnp.bfloat16
_ACT = jnp.bfloat16
_F32 = jnp.float32

_HEAD = 64
_CHUNK = 64
_LANES = 128
_TQ = 128
_BAND_TILES = 5
_BIAS_TILES = 9
_REL_CLIP = 256
_LN_EPS = 1e-5
_MASKED = -1e30
_EXP_ZERO_BELOW = -87.34
_SB_WINDOW = 2
_SB_SUBTILES = 4
_BAND_SUBTILES = 8
_VMEM_LIMIT = 56 * 1024 * 1024
_GRAD_ACC_BYTES = 12 * 1024 * 1024

_LR, _B1, _B2, _EPS, _WD, _STEP = 0.001, 0.9, 0.999, 1e-08, 0.01, 10

_MESH = pl.DeviceIdType.MESH


def _dot(a, b):
    return jnp.dot(a, b, preferred_element_type=_F32)


def _dot_nt(a, b):
    return lax.dot_general(a, b, (((1,), (1,)), ((), ())), preferred_element_type=_F32)


def _dot_tn(a, b):
    return lax.dot_general(a, b, (((0,), (0,)), ((), ())), preferred_element_type=_F32)


def _cparams(*sem):
    return pltpu.CompilerParams(dimension_semantics=sem, vmem_limit_bytes=_VMEM_LIMIT)


def _rows(t, c):
    return pl.BlockSpec((t, c), lambda i: (i, 0))


def _whole(shape):
    return pl.BlockSpec(shape, lambda i: tuple(0 for _ in shape))


_ANY = pl.BlockSpec(memory_space=pl.ANY)


def _load_cols(w_hbm, w_vmem, sem):
    n = w_hbm.shape[-1]
    cps = [pltpu.make_async_copy(w_hbm.at[k], w_vmem.at[:, pl.ds(k * n, n)], sem.at[k]) for k in range(4)]
    for cp in cps:
        cp.start()
    for cp in cps:
        cp.wait()


def _load_rows(w_hbm, w_vmem, sem):
    r = w_hbm.shape[-2]
    cps = [pltpu.make_async_copy(w_hbm.at[k], w_vmem.at[pl.ds(k * r, r), :], sem.at[k]) for k in range(4)]
    for cp in cps:
        cp.start()
    for cp in cps:
        cp.wait()


def _ln_stats(u):
    mu = jnp.mean(u, axis=-1, keepdims=True)
    xc = u - mu
    var = jnp.mean(xc * xc, axis=-1, keepdims=True)
    rstd = lax.rsqrt(var + _LN_EPS)
    return xc * rstd, rstd


def _ln_bwd(u, dy, gamma):
    xhat, rstd = _ln_stats(u)
    dxh = dy * gamma
    m1 = jnp.mean(dxh, axis=-1, keepdims=True)
    m2 = jnp.mean(dxh * xhat, axis=-1, keepdims=True)
    du = rstd * (dxh - m1 - xhat * m2)
    return du, jnp.sum(dy * xhat, axis=0, keepdims=True), jnp.sum(dy, axis=0, keepdims=True), xhat


def _divisor_tile(n, cap):
    best = None
    for t in range(_LANES, min(n, cap) + 1, _LANES):
        if n % t == 0:
            best = t
    return best or n


class _Comm:
    def __init__(self, inputs, out_shapes, sems, run, aliases=None):
        self.inputs, self.out_shapes, self.sems, self.run = list(inputs), list(out_shapes), list(sems), run
        self.aliases = aliases or {}


def _call(kern, comm, *, name, grid, in_specs, out_specs, out_shape, scratch_shapes, args, semantics):
    in_specs, out_specs, out_shape, scratch_shapes = list(in_specs), list(out_specs), list(out_shape), list(scratch_shapes)
    if comm is None:
        outs = pl.pallas_call(kern, name=name, grid=grid, in_specs=in_specs, out_specs=out_specs, out_shape=out_shape,
                              scratch_shapes=scratch_shapes, compiler_params=_cparams(*semantics))(*args)
        return list(outs), []
    n_in, n_out, n_scr = len(in_specs), len(out_specs), len(scratch_shapes)
    ci, co = len(comm.inputs), len(comm.out_shapes)
    nsteps = functools.reduce(lambda a, b: a * b, grid, 1)

    def fused(*refs):
        a, b = n_in, n_in + ci
        c, d = b + n_out, b + n_out + co
        e = d + n_scr
        step = pl.program_id(0)
        for ax in range(1, len(grid)):
            step = step * grid[ax] + pl.program_id(ax)
        comm.run(step, nsteps, refs[a:b], refs[c:d], refs[e:])
        kern(*refs[:a], *refs[b:c], *refs[d:e])

    outs = pl.pallas_call(
        fused, name=name, grid=grid, in_specs=in_specs + [_ANY] * ci, out_specs=out_specs + [_ANY] * co,
        out_shape=out_shape + comm.out_shapes, scratch_shapes=scratch_shapes + comm.sems,
        input_output_aliases={n_in + i: n_out + o for i, o in comm.aliases.items()},
        compiler_params=_cparams(*("arbitrary" for _ in grid)))(*args, *comm.inputs)
    return list(outs[:n_out]), list(outs[n_out:])


def _comm_only(comm, name):
    def body(*refs):
        ci, co = len(comm.inputs), len(comm.out_shapes)
        comm.run(0, 1, refs[:ci], refs[ci:ci + co], refs[ci + co:])

    outs = pl.pallas_call(body, name=name, in_specs=[_ANY] * len(comm.inputs), out_specs=[_ANY] * len(comm.out_shapes),
                          out_shape=comm.out_shapes, scratch_shapes=comm.sems,
                          input_output_aliases=dict(comm.aliases))(*comm.inputs)
    return list(outs)


def _in_proj(x, w_in, layer, comm=None):
    T, D = x.shape
    N = 4 * w_in.shape[-1]
    tm = 512

    def kern(x_ref, w_hbm, h_ref, xb_ref, w_v, sem):
        @pl.when(pl.program_id(0) == 0)
        def _():
            _load_cols(w_hbm, w_v, sem)

        xb = x_ref[...].astype(_MXU)
        h_ref[...] = _dot(xb, w_v[...]).astype(h_ref.dtype)
        xb_ref[...] = xb.astype(xb_ref.dtype)

    return _call(
        kern, comm, name=f"in_proj_{layer}", grid=(T // tm,),
        in_specs=[_rows(tm, D), _ANY],
        out_specs=[_rows(tm, N), _rows(tm, D)],
        out_shape=[jax.ShapeDtypeStruct((T, N), _ACT), jax.ShapeDtypeStruct((T, D), _ACT)],
        scratch_shapes=[pltpu.VMEM((D, N), w_in.dtype), pltpu.SemaphoreType.DMA((4,))],
        args=(x, w_in), semantics=("arbitrary",))


def _in_proj_gathering(x, block, layer):
    T, D = x.shape
    n = block.shape[1]
    tm = min(T, 1024)
    nrows = T // tm
    pass_steps = [int(f * nrows) for f in (0.6, 1.0, 1.7)]
    px, py, _ = _place()
    order = jnp.stack([2 * px + py, 2 * (1 - px) + py, 2 * px + (1 - py), 2 * (1 - px) + (1 - py)]).astype(jnp.int32)

    def kern(order_ref, x_ref, blk_hbm, h_ref, xb_ref, w_hbm, w_v, send_sems, recv_sems, loc_sem, load_sem):
        del order_ref
        step = pl.program_id(0) * nrows + pl.program_id(1)
        x_, y_, c = _place()
        k = 2 * x_ + y_
        me, sibling = (x_, y_, c), (x_, y_, 1 - c)
        chips = [(1 - x_, y_), (x_, 1 - y_), (1 - x_, 1 - y_)]
        chip_k = [2 * cx + cy for cx, cy in chips]

        def ici(s, owner_k, to, src=None):
            dst = _half(w_hbm.at[owner_k], c)
            return _remote(dst if src is None else src, dst, send_sems.at[s], recv_sems.at[s], to)

        def passed(s, hc, to):
            blk = _half(w_hbm.at[chip_k[s]], hc)
            return _remote(blk, blk, send_sems.at[3 + s], recv_sems.at[3 + s], to)

        local = pltpu.make_async_copy(blk_hbm, w_hbm.at[k], loc_sem.at[0])

        def load(src):
            cp = pltpu.make_async_copy(src, w_v, load_sem.at[0])
            cp.start()
            cp.wait()

        @pl.when(step == 0)
        def _():
            for s, chip in enumerate(chips):
                ici(s, k, (*chip, c), src=_half(blk_hbm, c)).start()
            local.start()
            load(blk_hbm)

        for s in range(3):
            @pl.when(step == pass_steps[s])
            def _():
                ici(s, chip_k[s], me).wait_recv()
                passed(s, c, sibling).start()

            @pl.when(step == (s + 1) * nrows)
            def _():
                passed(s, 1 - c, me).wait_recv()
                load(w_hbm.at[chip_k[s]])

        xb = x_ref[...].astype(_MXU)
        h_ref[...] = _dot(xb, w_v[...]).astype(h_ref.dtype)

        @pl.when(pl.program_id(0) == 0)
        def _():
            xb_ref[...] = xb.astype(xb_ref.dtype)

        @pl.when(step == 4 * nrows - 1)
        def _():
            for s, chip in enumerate(chips):
                ici(s, k, (*chip, c), src=_half(blk_hbm, c)).wait_send()
                passed(s, c, sibling).wait_send()
            local.wait()

    assert all(pass_steps[s] <= (s + 1) * nrows for s in range(3))
    h, xb, w_in = pl.pallas_call(
        kern, name=f"in_proj_{layer}",
        grid_spec=pltpu.PrefetchScalarGridSpec(
            num_scalar_prefetch=1, grid=(4, nrows),
            in_specs=[pl.BlockSpec((tm, D), lambda j, i, o: (i, 0)), _ANY],
            out_specs=[pl.BlockSpec((tm, n), lambda j, i, o: (i, o[j])),
                       pl.BlockSpec((tm, D), lambda j, i, o: (jnp.where(j == 0, i, nrows - 1), 0)), _ANY],
            scratch_shapes=[pltpu.VMEM((D, n), block.dtype), pltpu.SemaphoreType.DMA((6,)),
                            pltpu.SemaphoreType.DMA((6,)), pltpu.SemaphoreType.DMA((1,)), pltpu.SemaphoreType.DMA((1,))]),
        out_shape=[jax.ShapeDtypeStruct((T, 4 * n), _ACT), jax.ShapeDtypeStruct((T, D), _ACT),
                   jax.ShapeDtypeStruct((4,) + block.shape, block.dtype)],
        compiler_params=_cparams("arbitrary", "arbitrary"))(order, x, block)
    return h, xb, w_in


def _gate_specs(h, tm, D):
    first = (h.shape[1] - 2 * D) // D
    assert first * D + 2 * D == h.shape[1]
    return [pl.BlockSpec((tm, D), lambda i: (i, first)), pl.BlockSpec((tm, D), lambda i: (i, first + 1))]


def _mix_fwd(oa, ob, h, x, wpa, wpb, wo, bg, gamma, beta, alpha, layer, comm=None):
    T, D = x.shape
    WA, WB = oa.shape[1], ob.shape[1]
    tm = 512

    def kern(oa_ref, ob_ref, hga_ref, hgb_ref, x_ref, bg_ref, g_ref, b_ref, wpa_h, wpb_h, wo_h,
             x1_ref, u1_ref, pre_ref, wpa_v, wpb_v, wo_v, sa, sb, so):
        @pl.when(pl.program_id(0) == 0)
        def _():
            _load_cols(wpa_h, wpa_v, sa)
            _load_cols(wpb_h, wpb_v, sb)
            _load_rows(wo_h, wo_v, so)

        ya = _dot(oa_ref[...].astype(_MXU), wpa_v[...])
        yb = _dot(ob_ref[...].astype(_MXU), wpb_v[...])
        bgv = bg_ref[...]
        ga = jax.nn.sigmoid(hga_ref[...].astype(_F32) + bgv[:, :D])
        gb = jax.nn.sigmoid(hgb_ref[...].astype(_F32) + bgv[:, D:])
        pre = ga * ya + gb * yb
        mix = _dot(pre.astype(_MXU), wo_v[...])
        u = alpha * x_ref[...] + mix
        xhat, _ = _ln_stats(u)
        x1_ref[...] = xhat * g_ref[...] + b_ref[...]
        u1_ref[...] = u
        pre_ref[...] = pre.astype(pre_ref.dtype)

    return _call(
        kern, comm, name=f"mix_fwd_{layer}", grid=(T // tm,),
        in_specs=[_rows(tm, WA), _rows(tm, WB), *_gate_specs(h, tm, D), _rows(tm, D),
                  _whole((1, 2 * D)), _whole((1, D)), _whole((1, D)), _ANY, _ANY, _ANY],
        out_specs=[_rows(tm, D)] * 3,
        out_shape=[jax.ShapeDtypeStruct((T, D), _F32), jax.ShapeDtypeStruct((T, D), _F32),
                   jax.ShapeDtypeStruct((T, D), _ACT)],
        scratch_shapes=[pltpu.VMEM((WA, D), wpa.dtype), pltpu.VMEM((WB, D), wpb.dtype), pltpu.VMEM((D, D), wo.dtype),
                        pltpu.SemaphoreType.DMA((4,)), pltpu.SemaphoreType.DMA((4,)), pltpu.SemaphoreType.DMA((4,))],
        args=(oa, ob, h, h, x, bg, gamma, beta, wpa, wpb, wo), semantics=("arbitrary",))


def _ffn_fwd(x1, wfi, wfo, gamma, beta, alpha, layer, comm=None):
    T, D = x1.shape
    F2 = 4 * wfi.shape[-1]
    F = F2 // 2
    tm = 512
    fc = F // 2

    def kern(x_ref, g_ref, b_ref, wi_h, wo_h, x2_ref, u2_ref, act_ref, gu_ref, xb_ref, wi_v, wo_v, si, so):
        @pl.when(pl.program_id(0) == 0)
        def _():
            _load_cols(wi_h, wi_v, si)
            _load_rows(wo_h, wo_v, so)

        x = x_ref[...]
        xb = x.astype(_MXU)
        xb_ref[...] = xb.astype(xb_ref.dtype)
        ffn = jnp.zeros((tm, D), _F32)
        for c in range(2):
            g = _dot(xb, wi_v[:, c * fc:(c + 1) * fc])
            u = _dot(xb, wi_v[:, F + c * fc:F + (c + 1) * fc])
            act = g * jax.nn.sigmoid(g) * u
            ab = act.astype(_MXU)
            ffn = ffn + _dot(ab, wo_v[c * fc:(c + 1) * fc, :])
            act_ref[:, c * fc:(c + 1) * fc] = ab.astype(act_ref.dtype)
            gu_ref[:, c * fc:(c + 1) * fc] = g.astype(gu_ref.dtype)
            gu_ref[:, F + c * fc:F + (c + 1) * fc] = u.astype(gu_ref.dtype)
        uu = alpha * x + ffn
        xhat, _ = _ln_stats(uu)
        x2_ref[...] = xhat * g_ref[...] + b_ref[...]
        u2_ref[...] = uu

    return _call(
        kern, comm, name=f"ffn_fwd_{layer}", grid=(T // tm,),
        in_specs=[_rows(tm, D), _whole((1, D)), _whole((1, D)), _ANY, _ANY],
        out_specs=[_rows(tm, D), _rows(tm, D), _rows(tm, F), _rows(tm, F2), _rows(tm, D)],
        out_shape=[jax.ShapeDtypeStruct((T, D), _F32), jax.ShapeDtypeStruct((T, D), _F32),
                   jax.ShapeDtypeStruct((T, F), _ACT), jax.ShapeDtypeStruct((T, F2), _ACT),
                   jax.ShapeDtypeStruct((T, D), _ACT)],
        scratch_shapes=[pltpu.VMEM((D, F2), wfi.dtype), pltpu.VMEM((F, D), wfo.dtype),
                        pltpu.SemaphoreType.DMA((4,)), pltpu.SemaphoreType.DMA((4,))],
        args=(x1, gamma, beta, wfi, wfo), semantics=("arbitrary",))


def _ffn_bwd(u2, dy_or_target, gu, gamma, beta, wfi, wfo, alpha, layer, last):
    T, D = u2.shape
    F2 = gu.shape[1]
    F = F2 // 2
    tm = 256
    fc = F // 2

    def kern(u_ref, dy_ref, gu_ref, g_ref, b_ref, wi_h, wo_h, dx_ref, dub_ref, dgu_ref, st_ref, wi_v, wo_v, si, so):
        @pl.when(pl.program_id(0) == 0)
        def _():
            _load_cols(wi_h, wi_v, si)
            _load_rows(wo_h, wo_v, so)
            st_ref[...] = jnp.zeros_like(st_ref)

        gam = g_ref[...]
        u = u_ref[...]
        if last:
            xhat0, _ = _ln_stats(u)
            err = xhat0 * gam + b_ref[...] - dy_ref[...]
            dy = err * (1.0 / D)
            st_ref[2:3, :] += jnp.sum(err * err, axis=0, keepdims=True)
        else:
            dy = dy_ref[...]
        du, dgam, dbet, _ = _ln_bwd(u, dy, gam)
        st_ref[0:1, :] += dgam
        st_ref[1:2, :] += dbet
        dub = du.astype(_MXU)
        dub_ref[...] = dub.astype(dub_ref.dtype)
        dx = alpha * du
        for c in range(2):
            dact = _dot_nt(dub, wo_v[c * fc:(c + 1) * fc, :])
            g = gu_ref[:, c * fc:(c + 1) * fc].astype(_F32)
            uu = gu_ref[:, F + c * fc:F + (c + 1) * fc].astype(_F32)
            sg = jax.nn.sigmoid(g)
            dg = (dact * uu * (sg * (1.0 + g * (1.0 - sg)))).astype(_MXU)
            dup = (dact * (g * sg)).astype(_MXU)
            dgu_ref[:, c * fc:(c + 1) * fc] = dg.astype(dgu_ref.dtype)
            dgu_ref[:, F + c * fc:F + (c + 1) * fc] = dup.astype(dgu_ref.dtype)
            dx = dx + _dot_nt(dg, wi_v[:, c * fc:(c + 1) * fc]) + _dot_nt(dup, wi_v[:, F + c * fc:F + (c + 1) * fc])
        dx_ref[...] = dx

    return pl.pallas_call(
        kern, name=f"ffn_bwd_{layer}", grid=(T // tm,),
        in_specs=[_rows(tm, D), _rows(tm, D), _rows(tm, F2), _whole((1, D)), _whole((1, D)), _ANY, _ANY],
        out_specs=[_rows(tm, D), _rows(tm, D), _rows(tm, F2), _whole((8, D))],
        out_shape=[jax.ShapeDtypeStruct((T, D), _F32), jax.ShapeDtypeStruct((T, D), _ACT),
                   jax.ShapeDtypeStruct((T, F2), _ACT), jax.ShapeDtypeStruct((8, D), _F32)],
        scratch_shapes=[pltpu.VMEM((D, F2), wfi.dtype), pltpu.VMEM((F, D), wfo.dtype),
                        pltpu.SemaphoreType.DMA((4,)), pltpu.SemaphoreType.DMA((4,))],
        compiler_params=_cparams("arbitrary"),
    )(u2, dy_or_target, gu, gamma, beta, wfi, wfo)


def _residual_nt(res, res_scale, pieces, w, name, comm=None):
    T, K = res.shape
    widths = [p.shape[1] for p in pieces]
    N = sum(widths)
    tm = 512

    def kern(r_ref, *refs):
        d_refs, (w_hbm, o_ref, w_v, sem) = refs[:len(pieces)], refs[len(pieces):]

        @pl.when(pl.program_id(0) == 0)
        def _():
            _load_cols(w_hbm, w_v, sem)

        acc = res_scale * r_ref[...]
        off = 0
        for d_ref, width in zip(d_refs, widths):
            acc = acc + _dot_nt(d_ref[...].astype(_MXU), w_v[:, off:off + width])
            off += width
        o_ref[...] = acc

    outs, extra = _call(
        kern, comm, name=name, grid=(T // tm,),
        in_specs=[_rows(tm, K)] + [_rows(tm, width) for width in widths] + [_ANY], out_specs=[_rows(tm, K)],
        out_shape=[jax.ShapeDtypeStruct((T, K), _F32)],
        scratch_shapes=[pltpu.VMEM((K, N), w.dtype), pltpu.SemaphoreType.DMA((4,))],
        args=(res, *pieces, w), semantics=("arbitrary",))
    return outs[0], extra


def _grad_w_pieces(a, pieces, name, comm=None):
    T, M = a.shape
    widths = [p.shape[1] for p in pieces]
    offsets = [sum(widths[:p]) for p in range(len(pieces))]
    N = sum(widths)
    tk = 1024 if T % 1024 == 0 else 512
    nk = T // tk

    def fits(ow):
        inside = lambda off, width: width < ow and off // ow == (off + width - 1) // ow
        whole = lambda off, width: off % ow == 0 and width % ow == 0
        return N % ow == 0 and all(inside(o, w) or whole(o, w) for o, w in zip(offsets, widths))

    ow = next(c for c in (1024, 512, 256, _LANES) if fits(c))

    def kern(a_ref, *refs):
        b_refs, (o_ref, acc) = refs[:len(pieces)], refs[len(pieces):]
        j, k = pl.program_id(0), pl.program_id(1)

        @pl.when(k == 0)
        def _():
            acc[...] = jnp.zeros_like(acc)

        for b_ref, off, width in zip(b_refs, offsets, widths):
            if width < ow:
                @pl.when(j == off // ow)
                def _():
                    acc[:, off % ow:off % ow + width] += _dot_tn(a_ref[...].astype(_MXU), b_ref[...].astype(_MXU))
            else:
                @pl.when(jnp.logical_and(j >= off // ow, j < (off + width) // ow))
                def _():
                    acc[...] += _dot_tn(a_ref[...].astype(_MXU), b_ref[...].astype(_MXU))

        @pl.when(k == nk - 1)
        def _():
            o_ref[...] = acc[...].astype(o_ref.dtype)

    def piece_spec(off, width):
        first, blocks = off // ow, max(width // ow, 1)

        def index(j, k):
            mine = jnp.logical_and(j >= first, j < first + blocks)
            return jnp.where(mine, k, 0), jnp.where(mine, j - first, 0)
        return pl.BlockSpec((tk, min(width, ow)), index)

    outs, extra = _call(
        kern, comm, name=name, grid=(N // ow, nk),
        in_specs=[pl.BlockSpec((tk, M), lambda j, k: (k, 0))] + [piece_spec(o, w) for o, w in zip(offsets, widths)],
        out_specs=[pl.BlockSpec((M, ow), lambda j, k: (0, j))],
        out_shape=[jax.ShapeDtypeStruct((M, N), _ACT)], scratch_shapes=[pltpu.VMEM((M, ow), _F32)],
        args=(a, *pieces), semantics=("parallel", "arbitrary"))
    return outs[0], extra


def _gcd(a, b):
    while b:
        a, b = b, a % b
    return a


def _mix_bwd(u1, dx1, oa, ob, h, wpa, wpb, wo, bg, gamma, layer):
    T, D = u1.shape
    WA, WB = wpa.shape[-2], wpb.shape[-2]
    tm = 512

    def kern(u_ref, dx_ref, oa_ref, ob_ref, hga_ref, hgb_ref, bg_ref, g_ref, wpa_h, wpb_h, wo_h,
             du_ref, dub_ref, dya_ref, dyb_ref, dhg_ref, doa_ref, dob_ref, st_ref,
             wpa_v, wpb_v, wo_v, sa, sb, so):
        @pl.when(pl.program_id(0) == 0)
        def _():
            _load_cols(wpa_h, wpa_v, sa)
            _load_cols(wpb_h, wpb_v, sb)
            _load_rows(wo_h, wo_v, so)
            st_ref[...] = jnp.zeros_like(st_ref)

        du, dgam, dbet, _ = _ln_bwd(u_ref[...], dx_ref[...], g_ref[...])
        st_ref[1:2, :D] += dgam
        st_ref[1:2, D:] += dbet
        du_ref[...] = du
        dub = du.astype(_MXU)
        dub_ref[...] = dub.astype(dub_ref.dtype)
        dpre = _dot_nt(dub, wo_v[...])
        bgv = bg_ref[...]
        ga = jax.nn.sigmoid(hga_ref[...].astype(_F32) + bgv[:, :D])
        gb = jax.nn.sigmoid(hgb_ref[...].astype(_F32) + bgv[:, D:])
        dya = (dpre * ga).astype(_MXU)
        dyb = (dpre * gb).astype(_MXU)
        dsa = dpre * _dot(oa_ref[...].astype(_MXU), wpa_v[...]) * (ga * (1.0 - ga))
        dsb = dpre * _dot(ob_ref[...].astype(_MXU), wpb_v[...]) * (gb * (1.0 - gb))
        st_ref[0:1, :D] += jnp.sum(dsa, axis=0, keepdims=True)
        st_ref[0:1, D:] += jnp.sum(dsb, axis=0, keepdims=True)
        dya_ref[...] = dya.astype(dya_ref.dtype)
        dyb_ref[...] = dyb.astype(dyb_ref.dtype)
        dhg_ref[:, :D] = dsa.astype(dhg_ref.dtype)
        dhg_ref[:, D:] = dsb.astype(dhg_ref.dtype)
        doa_ref[...] = _dot_nt(dya, wpa_v[...]).astype(doa_ref.dtype)
        dob_ref[...] = _dot_nt(dyb, wpb_v[...]).astype(dob_ref.dtype)

    return pl.pallas_call(
        kern, name=f"mix_bwd_{layer}", grid=(T // tm,),
        in_specs=[_rows(tm, D), _rows(tm, D), _rows(tm, WA), _rows(tm, WB), *_gate_specs(h, tm, D), _whole((1, 2 * D)),
                  _whole((1, D)), _ANY, _ANY, _ANY],
        out_specs=[_rows(tm, D)] * 4 + [_rows(tm, 2 * D), _rows(tm, WA), _rows(tm, WB), _whole((8, 2 * D))],
        out_shape=[jax.ShapeDtypeStruct((T, D), _F32)] + [jax.ShapeDtypeStruct((T, D), _ACT)] * 3
        + [jax.ShapeDtypeStruct((T, 2 * D), _ACT), jax.ShapeDtypeStruct((T, WA), _ACT),
           jax.ShapeDtypeStruct((T, WB), _ACT), jax.ShapeDtypeStruct((8, 2 * D), _F32)],
        scratch_shapes=[pltpu.VMEM((WA, D), wpa.dtype), pltpu.VMEM((WB, D), wpb.dtype), pltpu.VMEM((D, D), wo.dtype),
                        pltpu.SemaphoreType.DMA((4,)), pltpu.SemaphoreType.DMA((4,)), pltpu.SemaphoreType.DMA((4,))],
        compiler_params=_cparams("arbitrary"),
    )(u1, dx1, oa, ob, h, h, bg, gamma, wpa, wpb, wo)


def _grad_w(a, b, *, col_shards, name, comm=None):
    T, M = a.shape
    N = b.shape[1]
    tk = 1024 if T % 1024 == 0 else 512
    n = N // 4 if col_shards else N
    whole = M * N * 4 <= _GRAD_ACC_BYTES
    tn = N if whole else (n if col_shards else _divisor_tile(N, _GRAD_ACC_BYTES // (4 * M)))
    nk = T // tk

    def kern(a_ref, b_ref, o_ref, acc):
        k = pl.program_id(1)

        @pl.when(k == 0)
        def _():
            acc[...] = jnp.zeros_like(acc)

        acc[...] += _dot_tn(a_ref[...].astype(_MXU), b_ref[...].astype(_MXU))

        @pl.when(k == nk - 1)
        def _():
            if col_shards and whole:
                for s in range(4):
                    o_ref[s] = acc[:, s * n:(s + 1) * n].astype(o_ref.dtype)
            else:
                o_ref[...] = acc[...].astype(o_ref.dtype)

    if col_shards:
        out_spec = (pl.BlockSpec((4, M, n), lambda j, k: (0, 0, 0)) if whole
                    else pl.BlockSpec((None, M, n), lambda j, k: (j, 0, 0)))
        out_shape = jax.ShapeDtypeStruct((4, M, n), _ACT)
    else:
        out_spec = pl.BlockSpec((M, tn), lambda j, k: (0, j))
        out_shape = jax.ShapeDtypeStruct((M, N), _ACT)
    outs, extra = _call(
        kern, comm, name=name, grid=(N // tn, nk),
        in_specs=[pl.BlockSpec((tk, M), lambda j, k: (k, 0)), pl.BlockSpec((tk, tn), lambda j, k: (k, j))],
        out_specs=[out_spec], out_shape=[out_shape], scratch_shapes=[pltpu.VMEM((M, tn), _F32)],
        args=(a, b), semantics=("parallel", "arbitrary"))
    return outs[0], extra


def _bias_tiles(rel):
    H = rel.shape[0]
    span = _TQ * _BAND_TILES - 1
    edge = span - _REL_CLIP
    gvec = jnp.concatenate([jnp.broadcast_to(rel[:, :1], (H, edge)), rel, jnp.broadcast_to(rel[:, -1:], (H, edge))], axis=1)
    width = _BIAS_TILES * _TQ
    period = width + _TQ
    tiled = jnp.broadcast_to(jnp.pad(gvec[:, ::-1], ((0, 0), (0, 1)))[:, None, :], (H, _TQ, period))
    rows = tiled.reshape(H, _TQ * period)[:, :_TQ * (period - 1)].reshape(H, _TQ, period - 1)[:, :, _TQ - 1:]
    r = jnp.arange(_TQ)[:, None]
    u = jnp.arange(width)[None, :]
    d = 4 * _TQ + r - u
    rm = r % _CHUNK
    valid = (d >= rm - (_CHUNK - 1)) & (d <= rm + 8 * _CHUNK)
    tiles = jnp.where(valid[None], rows, _MASKED)
    return tiles.reshape(H // 2, 2 * _TQ, width)


def _strip_tiles(strip_ref, tiles_ref, to_strip=False):
    for ub in range(_BIAS_TILES):
        if to_strip:
            strip_ref[:, ub * _TQ:(ub + 1) * _TQ] = tiles_ref[ub]
        else:
            tiles_ref[ub] = strip_ref[:, ub * _TQ:(ub + 1) * _TQ]


def _fold_bias_grad(db):
    H = 2 * db.shape[0]
    width = _BIAS_TILES * _TQ
    period = width + _TQ
    x = jnp.pad(db.reshape(H, _TQ, width), ((0, 0), (0, 0), (_TQ - 1, 0)))
    skew = jnp.pad(x.reshape(H, _TQ * (period - 1)), ((0, 0), (0, _TQ))).reshape(H, _TQ, period)
    dg = skew.sum(axis=1)[:, :period - 1][:, ::-1]
    span = _TQ * _BAND_TILES - 1
    edge = span - _REL_CLIP
    mid = dg[:, edge:edge + 2 * _REL_CLIP + 1]
    lo = dg[:, :edge].sum(axis=1)
    hi = dg[:, edge + 2 * _REL_CLIP + 1:].sum(axis=1)
    return mid.at[:, 0].add(lo).at[:, -1].add(hi)


def _band_window(i):
    j0 = jnp.maximum(i - (_BAND_TILES - 1), 0)
    return j0, (_BAND_TILES - 1) - (i - j0)


def _head_masks():
    lane = lax.broadcasted_iota(jnp.int32, (1, _LANES), 1)
    return [(lane // _HEAD) == hh for hh in range(2)]


def _stack_heads(x, masks):
    return jnp.concatenate([jnp.where(m, x, jnp.zeros_like(x)) for m in masks], axis=0)


def _unstack_heads(y, masks):
    return jnp.where(masks[0], y[:_TQ], y[_TQ:])


def _scaled(q):
    return q * jnp.asarray(_HEAD ** -0.5, q.dtype)


def _band_probs(q2, k_ref, b_ref, j0, boff):
    s = []
    for j in range(_BAND_TILES):
        kj = k_ref[pl.ds(pl.multiple_of((j0 + j) * _TQ, _TQ), _TQ), :]
        s.append(_dot_nt(q2, kj) + b_ref[boff + j])
    m = jnp.max(functools.reduce(jnp.maximum, s), axis=-1, keepdims=True)
    p = [jnp.exp(x - m) for x in s]
    l = jnp.sum(functools.reduce(lambda a, b: a + b, p), axis=-1, keepdims=True)
    return p, 1.0 / l


def _qkv_specs(T, cb, npair, tq=_TQ):
    return [pl.BlockSpec((tq, _LANES), lambda h, i: (i, cb + h)),
            pl.BlockSpec((T, _LANES), lambda h, i: (0, cb + npair + h)),
            pl.BlockSpec((T, _LANES), lambda h, i: (0, cb + 2 * npair + h))]


def _attn_a_fwd(hq, bias, col0, width, layer, comm=None):
    T = hq.shape[0]
    npair = width // _LANES
    nsub = _BAND_SUBTILES
    tq = nsub * _TQ

    def kern(q_ref, k_ref, v_ref, b_ref, o_ref, b_tiles):
        @pl.when(pl.program_id(1) == 0)
        def _():
            _strip_tiles(b_ref, b_tiles)

        masks = _head_masks()
        q = _scaled(q_ref[...])
        for s in range(nsub):
            part = slice(s * _TQ, (s + 1) * _TQ)
            j0, boff = _band_window(nsub * pl.program_id(1) + s)
            p, inv = _band_probs(_stack_heads(q[part], masks), k_ref, b_tiles, j0, boff)
            o = jnp.zeros((2 * _TQ, _LANES), _F32)
            for j in range(_BAND_TILES):
                vj = v_ref[pl.ds(pl.multiple_of((j0 + j) * _TQ, _TQ), _TQ), :]
                o = o + _dot(p[j].astype(_MXU), vj)
            o_ref[part, :] = _unstack_heads(o * inv, masks).astype(o_ref.dtype)

    outs, extra = _call(
        kern, comm, name=f"band_attn_fwd_{layer}", grid=(npair, T // tq),
        in_specs=_qkv_specs(T, col0 // _LANES, npair, tq)
        + [pl.BlockSpec((None, 2 * _TQ, _BIAS_TILES * _TQ), lambda h, i: (h, 0, 0))],
        out_specs=[pl.BlockSpec((tq, _LANES), lambda h, i: (i, h))],
        out_shape=[jax.ShapeDtypeStruct((T, width), _ACT)],
        scratch_shapes=[pltpu.VMEM((_BIAS_TILES, 2 * _TQ, _TQ), _F32)],
        args=(hq, hq, hq, bias), semantics=("arbitrary", "arbitrary"))
    return outs[0], extra


def _attn_a_bwd(hq, bias, do, col0, width, layer, comm=None):
    T = hq.shape[0]
    npair = width // _LANES
    nsub = _BAND_SUBTILES
    tq = nsub * _TQ
    nq = T // tq
    scale = _HEAD ** -0.5

    def kern(q_ref, k_ref, v_ref, b_ref, do_ref, dq_ref, dk_ref, dv_ref, db_ref, dk_acc, dv_acc, b_tiles, db_acc):
        i = pl.program_id(1)

        @pl.when(i == 0)
        def _():
            _strip_tiles(b_ref, b_tiles)
            dk_acc[...] = jnp.zeros_like(dk_acc)
            dv_acc[...] = jnp.zeros_like(dv_acc)
            db_acc[...] = jnp.zeros_like(db_acc)

        masks = _head_masks()
        q = _scaled(q_ref[...])
        do_t = do_ref[...]
        for s in range(nsub):
            part = slice(s * _TQ, (s + 1) * _TQ)
            j0, boff = _band_window(nsub * i + s)
            q2 = _stack_heads(q[part], masks)
            do2 = _stack_heads(do_t[part], masks).astype(_MXU)
            p, inv = _band_probs(q2, k_ref, b_tiles, j0, boff)
            rows = [pl.ds(pl.multiple_of((j0 + j) * _TQ, _TQ), _TQ) for j in range(_BAND_TILES)]
            p = [x * inv for x in p]
            dp = [_dot_nt(do2, v_ref[rows[j], :]) for j in range(_BAND_TILES)]
            delta = jnp.sum(functools.reduce(lambda a, b: a + b, [p[j] * dp[j] for j in range(_BAND_TILES)]),
                            axis=-1, keepdims=True)
            dq = jnp.zeros((2 * _TQ, _LANES), _F32)
            for j in range(_BAND_TILES):
                ds = p[j] * (dp[j] - delta)
                db_acc[boff + j] += ds
                dsb = ds.astype(_MXU)
                dq = dq + _dot(dsb, k_ref[rows[j], :])
                dk_acc[rows[j], :] += _dot_tn(dsb, q2)
                dv_acc[rows[j], :] += _dot_tn(p[j].astype(_MXU), do2)
            dq_ref[part, :] = (_unstack_heads(dq, masks) * scale).astype(dq_ref.dtype)

        @pl.when(i == nq - 1)
        def _():
            dk_ref[...] = dk_acc[...].astype(dk_ref.dtype)
            dv_ref[...] = dv_acc[...].astype(dv_ref.dtype)
            _strip_tiles(db_ref, db_acc, to_strip=True)

    strip = pl.BlockSpec((None, 2 * _TQ, _BIAS_TILES * _TQ), lambda h, i: (h, 0, 0))
    tile = pl.BlockSpec((tq, _LANES), lambda h, i: (i, h))
    column = pl.BlockSpec((T, _LANES), lambda h, i: (0, h))
    outs, extra = _call(
        kern, comm, name=f"band_attn_bwd_{layer}", grid=(npair, nq),
        in_specs=_qkv_specs(T, col0 // _LANES, npair, tq) + [strip, tile],
        out_specs=[tile, column, column, strip],
        out_shape=[jax.ShapeDtypeStruct((T, width), _ACT)] * 3
        + [jax.ShapeDtypeStruct((npair, 2 * _TQ, _BIAS_TILES * _TQ), _F32)],
        scratch_shapes=[pltpu.VMEM((T, _LANES), _F32), pltpu.VMEM((T, _LANES), _F32),
                        pltpu.VMEM((_BIAS_TILES, 2 * _TQ, _TQ), _F32), pltpu.VMEM((_BIAS_TILES, 2 * _TQ, _TQ), _F32)],
        args=(hq, hq, hq, bias, do), semantics=("arbitrary", "arbitrary"))
    return outs, extra


def _suffix_matrix():
    r = lax.broadcasted_iota(jnp.int32, (_TQ, _TQ), 0)
    c = lax.broadcasted_iota(jnp.int32, (_TQ, _TQ), 1)
    r2 = lax.broadcasted_iota(jnp.int32, (2 * _TQ, _TQ), 0)
    c2 = lax.broadcasted_iota(jnp.int32, (2 * _TQ, _TQ), 1)
    return (r > c).astype(_MXU), c2 - (r2 & (_TQ - 1))


def _suffix_sums(xs, tri):
    n, k = xs[0].shape[0], len(xs)
    his = [x.astype(_MXU) for x in xs]
    los = [(x - h.astype(_F32)).astype(_MXU) for x, h in zip(xs, his)]
    y = _dot(jnp.concatenate(his + los, axis=0), tri)
    return [y[j * n:(j + 1) * n] + y[(k + j) * n:(k + j + 1) * n] for j in range(k)]


def _stick_tiles(tiles, rel, carry_l, tri):
    zs = [_dot_nt(qs, kj) for qs, kj, _, _ in tiles]
    Ls, masks = [], []
    for z, (_, _, jj, _) in zip(zs, tiles):
        nsp = -(jnp.maximum(z, 0.0) + jnp.log(1.0 + jnp.exp(-jnp.abs(z))))
        if isinstance(jj, int):
            mask = (rel < 0) if jj == 0 else None
        else:
            mask = rel < jnp.where(jj == 0, 0, _TQ)
        Ls.append(nsp if mask is None else jnp.where(mask, nsp, 0.0))
        masks.append(mask)
    carry_l = list(carry_l)
    ws = []
    for z, L, suffix, mask, (_, _, _, sub) in zip(zs, Ls, _suffix_sums(Ls, tri), masks, tiles):
        w = jnp.exp(z + L + suffix + carry_l[sub])
        ws.append(w if mask is None else jnp.where(mask, w, 0.0))
        carry_l[sub] = carry_l[sub] + jnp.sum(L, axis=-1, keepdims=True)
    return zs, Ls, ws, masks, carry_l


def _sweep(i, step, zero):
    nsub = _SB_SUBTILES

    def window():
        tiles = [(s, jj) for jj in range(_SB_WINDOW) for s in range(nsub)]
        return tuple((jnp.int32(_SB_WINDOW),) + c for c in step(tiles, [zero] * nsub))

    start = lax.cond(i >= -(-(_SB_WINDOW - 1) // nsub), window, lambda: tuple((jnp.int32(0),) + zero for _ in range(nsub)))
    outs = []
    for s in range(nsub):
        def done(c, s=s):
            return jnp.logical_or(c[0] > nsub * i + s, jnp.max(c[1]) < _EXP_ZERO_BELOW)

        def more(c, s=s):
            carries = [None] * nsub
            carries[s] = c[1:]
            return (c[0] + 1,) + step([(s, c[0])], carries)[s]

        outs.append(lax.while_loop(lambda c, done=done: jnp.logical_not(done(c)), more, start[s]))
    return outs


def _sb_fwd(hq, col0, width, layer, comm=None):
    T = hq.shape[0]
    npair = width // _LANES
    nsub = _SB_SUBTILES
    tq = nsub * _TQ

    def kern(q_ref, k_ref, v_ref, o_ref):
        i = pl.program_id(1)
        masks = _head_masks()
        tri, rel = _suffix_matrix()
        q = _scaled(q_ref[...])
        q2 = [_stack_heads(q[s * _TQ:(s + 1) * _TQ], masks) for s in range(nsub)]

        def step(tiles, carries):
            rows = [pl.ds(pl.multiple_of((nsub * i + s - jj) * _TQ, _TQ), _TQ) for s, jj in tiles]
            cls = [None if c is None else c[0] for c in carries]
            accs = [None if c is None else c[1] for c in carries]
            _, _, ws, _, cls = _stick_tiles([(q2[s], k_ref[r, :], jj, s) for (s, jj), r in zip(tiles, rows)], rel, cls, tri)
            for w, r, (s, _) in zip(ws, rows, tiles):
                accs[s] = accs[s] + _dot(w.astype(_MXU), v_ref[r, :])
            return [None if c is None else (cls[s], accs[s]) for s, c in enumerate(carries)]

        outs = _sweep(i, step, (jnp.zeros((2 * _TQ, 1), _F32), jnp.zeros((2 * _TQ, _LANES), _F32)))
        for s in range(nsub):
            o_ref[s * _TQ:(s + 1) * _TQ, :] = _unstack_heads(outs[s][2], masks)

    outs, extra = _call(
        kern, comm, name=f"stick_attn_fwd_{layer}", grid=(npair, T // tq),
        in_specs=_qkv_specs(T, col0 // _LANES, npair, tq),
        out_specs=[pl.BlockSpec((tq, _LANES), lambda h, i: (i, h))],
        out_shape=[jax.ShapeDtypeStruct((T, width), _F32)], scratch_shapes=[],
        args=(hq, hq, hq), semantics=("arbitrary", "arbitrary"))
    return outs[0], extra


def _sb_bwd(hq, o, do, col0, width, layer, comm=None):
    T = hq.shape[0]
    npair = width // _LANES
    nsub = _SB_SUBTILES
    tq = nsub * _TQ
    nq = T // tq
    scale = _HEAD ** -0.5

    def kern(q_ref, k_ref, v_ref, o_ref, do_ref, dq_ref, dk_ref, dv_ref, dk_acc, dv_acc):
        i = pl.program_id(1)

        @pl.when(i == 0)
        def _():
            dk_acc[...] = jnp.zeros_like(dk_acc)
            dv_acc[...] = jnp.zeros_like(dv_acc)

        masks = _head_masks()
        tri, rel = _suffix_matrix()
        q = _scaled(q_ref[...])
        do_t = do_ref[...]
        prod = do_t.astype(_F32) * o_ref[...]
        part = [slice(s * _TQ, (s + 1) * _TQ) for s in range(nsub)]
        q2 = [_stack_heads(q[p], masks) for p in part]
        do2 = [_stack_heads(do_t[p], masks).astype(_MXU) for p in part]
        dsum = [jnp.sum(_stack_heads(prod[p], masks), axis=-1, keepdims=True) for p in part]

        def step(tiles, carries):
            rows = [pl.ds(pl.multiple_of((nsub * i + s - jj) * _TQ, _TQ), _TQ) for s, jj in tiles]
            kjs = [k_ref[r, :] for r in rows]
            cls, cgs, dqs = ([None if c is None else c[n] for c in carries] for n in range(3))
            zs, Ls, ws, tile_masks, cls = _stick_tiles([(q2[s], kj, jj, s) for (s, jj), kj in zip(tiles, kjs)], rel, cls, tri)
            wbs = [w.astype(_MXU) for w in ws]
            gs = [wb.astype(_F32) * _dot_nt(do2[s], v_ref[r, :]) for wb, r, (s, _) in zip(wbs, rows, tiles)]
            for z, L, g, later, mask, wb, kj, r, (s, _) in zip(zs, Ls, gs, _suffix_sums(gs, tri), tile_masks, wbs, kjs,
                                                               rows, tiles):
                dz = g - jnp.exp(z + L) * (dsum[s] - (later + cgs[s]))
                if mask is not None:
                    dz = jnp.where(mask, dz, 0.0)
                dzb = dz.astype(_MXU)
                dk_acc[r, :] += _dot_tn(dzb, q2[s])
                dv_acc[r, :] += _dot_tn(wb, do2[s])
                dqs[s] = dqs[s] + _dot(dzb, kj)
                cgs[s] = cgs[s] + jnp.sum(g, axis=-1, keepdims=True)
            return [None if c is None else (cls[s], cgs[s], dqs[s]) for s, c in enumerate(carries)]

        zc = jnp.zeros((2 * _TQ, 1), _F32)
        outs = _sweep(i, step, (zc, zc, jnp.zeros((2 * _TQ, _LANES), _F32)))
        for s in range(nsub):
            dq_ref[part[s], :] = (_unstack_heads(outs[s][3], masks) * scale).astype(dq_ref.dtype)

        @pl.when(i == nq - 1)
        def _():
            dk_ref[...] = dk_acc[...].astype(dk_ref.dtype)
            dv_ref[...] = dv_acc[...].astype(dv_ref.dtype)

    tile_spec = pl.BlockSpec((tq, _LANES), lambda h, i: (i, h))
    column = pl.BlockSpec((T, _LANES), lambda h, i: (0, h))
    outs, extra = _call(
        kern, comm, name=f"stick_attn_bwd_{layer}", grid=(npair, nq),
        in_specs=_qkv_specs(T, col0 // _LANES, npair, tq) + [tile_spec, tile_spec],
        out_specs=[tile_spec, column, column],
        out_shape=[jax.ShapeDtypeStruct((T, width), _ACT)] * 3,
        scratch_shapes=[pltpu.VMEM((T, _LANES), _F32), pltpu.VMEM((T, _LANES), _F32)],
        args=(hq, hq, hq, o, do), semantics=("arbitrary", "arbitrary"))
    return outs, extra


_DENSE = ("w_in", "w_proj_a", "w_proj_b", "w_out", "w_ffn_in", "w_ffn_out")
_COL_SHARDED = {"w_in": True, "w_proj_a": True, "w_proj_b": True, "w_out": False, "w_ffn_in": True, "w_ffn_out": False}
_SMALL = ("b_gate", "rel_bias", "ln1_g", "ln1_b", "ln2_g", "ln2_b")


class _Plans:
    def __init__(self, plans=None, own_w_in=None):
        self.plans = plans or {}
        self.own_w_in = own_w_in or {}

    def start(self, key):
        if key not in self.plans:
            return None, None
        return self.plans[key]()

    @staticmethod
    def finish(done, extra):
        if done is not None:
            done(extra)


def _layer_fwd(x, W, small, l, alpha, plans):
    WA = small["rel_bias"].shape[1] * _HEAD
    row = lambda v: v[l].reshape(1, -1)
    if l in plans.own_w_in:
        h, xb, W["w_in"] = _in_proj_gathering(x, plans.own_w_in[l], l)
    else:
        comm, done = plans.start(f"in_proj_{l}")
        (h, xb), extra = _in_proj(x, W["w_in"], l, comm)
        plans.finish(done, extra)
    WB = (h.shape[1] - 2 * x.shape[1] - 3 * WA) // 3
    bias = _bias_tiles(small["rel_bias"][l])
    comm, done = plans.start(f"band_fwd_{l}")
    oa, extra = _attn_a_fwd(h, bias, 0, WA, l, comm)
    plans.finish(done, extra)
    comm, done = plans.start(f"stick_fwd_{l}")
    ob, extra = _sb_fwd(h, 3 * WA, WB, l, comm)
    plans.finish(done, extra)
    comm, done = plans.start(f"mix_fwd_{l}")
    (x1, u1, pre), extra = _mix_fwd(oa, ob, h, x, W["w_proj_a"], W["w_proj_b"], W["w_out"], row(small["b_gate"]),
                                            row(small["ln1_g"]), row(small["ln1_b"]), alpha, l, comm)
    plans.finish(done, extra)
    comm, done = plans.start(f"ffn_fwd_{l}")
    (x2, u2, act, gu, x1b), extra = _ffn_fwd(x1, W["w_ffn_in"], W["w_ffn_out"], row(small["ln2_g"]),
                                             row(small["ln2_b"]), alpha, l, comm)
    plans.finish(done, extra)
    return x2, dict(xb=xb, h=h, bias=bias, oa=oa, ob=ob, x1b=x1b, u1=u1, pre=pre, u2=u2, act=act, gu=gu)


def _layer_bwd(dy_or_target, S, W, small, l, last, alpha, plans, gw):
    D = S["xb"].shape[1]
    WA, WB = S["oa"].shape[1], S["ob"].shape[1]
    row = lambda v: v[l].reshape(1, -1)

    def blocks(g, n):
        return g if _COL_SHARDED[n] else g.reshape(4, g.shape[0] // 4, g.shape[1])

    dx1, du2b, dgu, st2 = _ffn_bwd(S["u2"], dy_or_target, S["gu"], row(small["ln2_g"]), row(small["ln2_b"]),
                                   W["w_ffn_in"], W["w_ffn_out"], alpha, l, last)
    gw["w_ffn_in"] = blocks(_grad_w(S["x1b"], dgu, col_shards=True, name=f"grad_w_ffn_in_{l}")[0], "w_ffn_in")
    gw["w_ffn_out"] = blocks(_grad_w(S["act"], du2b, col_shards=False, name=f"grad_w_ffn_out_{l}")[0], "w_ffn_out")
    du1, du1b, dya, dyb, dhg, doa, dob, st1 = _mix_bwd(S["u1"], dx1, S["oa"], S["ob"], S["h"], W["w_proj_a"],
                                                       W["w_proj_b"], W["w_out"], row(small["b_gate"]),
                                                       row(small["ln1_g"]), l)
    gw["w_out"] = blocks(_grad_w(S["pre"], du1b, col_shards=False, name=f"grad_w_out_{l}")[0], "w_out")
    gw["w_proj_a"] = blocks(_grad_w(S["oa"], dya, col_shards=True, name=f"grad_w_proj_a_{l}")[0], "w_proj_a")
    gw["w_proj_b"] = blocks(_grad_w(S["ob"], dyb, col_shards=True, name=f"grad_w_proj_b_{l}")[0], "w_proj_b")
    comm, done = plans.start(f"band_bwd_{l}")
    (dqa, dka, dva, dbias), extra = _attn_a_bwd(S["h"], S["bias"], doa, 0, WA, l, comm)
    plans.finish(done, extra)
    comm, done = plans.start(f"stick_bwd_{l}")
    (dqb, dkb, dvb), extra = _sb_bwd(S["h"], S["ob"], dob, 3 * WA, WB, l, comm)
    plans.finish(done, extra)
    dh = [dqa, dka, dva, dqb, dkb, dvb, dhg]
    comm, done = plans.start(f"grad_w_in_{l}")
    gw["w_in"], extra = _grad_w_pieces(S["xb"], dh, f"grad_w_in_{l}", comm)
    plans.finish(done, extra)
    comm, done = plans.start(f"in_proj_bwd_{l}")
    dx, extra = _residual_nt(du1, alpha, dh, W["w_in"], f"in_proj_bwd_{l}", comm)
    plans.finish(done, extra)
    gs = dict(b_gate=st1[0], rel_bias=_fold_bias_grad(dbias), ln1_g=st1[1, :D], ln1_b=st1[1, D:],
              ln2_g=st2[0], ln2_b=st2[1])
    return dx, gs, st2[2]


def _local_step(x, target, W, small, plans=None, gws=None):
    depth = len(W)
    alpha = float((2 * depth) ** 0.25)
    plans = plans or _Plans()
    gws = gws if gws is not None else [dict() for _ in range(depth)]
    saved = []
    h = x
    for l in range(depth):
        h, S = _layer_fwd(h, W[l], small, l, alpha, plans)
        saved.append(S)
    gss = [None] * depth
    d = target
    sq = None
    for l in reversed(range(depth)):
        d, gss[l], sq_l = _layer_bwd(d, saved[l], W[l], small, l, l == depth - 1, alpha, plans, gws[l])
        if l == depth - 1:
            sq = sq_l
    return sq, d, gws, gss


def _place():
    return lax.axis_index("x"), lax.axis_index("y"), lax.axis_index("c")


def _remote(src, dst, send_sem, recv_sem, to):
    return pltpu.make_async_remote_copy(src_ref=src, dst_ref=dst, send_sem=send_sem, recv_sem=recv_sem,
                                        device_id=to, device_id_type=_MESH)


def _half(ref, hc):
    kh = ref.shape[0] // 2
    return ref.at[pl.ds(pl.multiple_of(hc * kh, 16), kh), :]


def _gather_plan(blocks, fractions):
    nt = len(blocks)

    def run(step, nsteps, ins, outs, sems):
        send_sems, recv_sems, loc_sems = sems
        x, y, c = _place()
        k = 2 * x + y
        me, sibling = (x, y, c), (x, y, 1 - c)
        chips = [(1 - x, y), (x, 1 - y), (1 - x, 1 - y)]
        chip_k = [2 * cx + cy for cx, cy in chips]

        def ici(t, s, owner_k, to, src=None):
            dst = _half(outs[t].at[owner_k], c)
            return _remote(dst if src is None else src, dst, send_sems.at[t, s], recv_sems.at[t, s], to)

        def passed(t, s, hc, to):
            blk = _half(outs[t].at[chip_k[s]], hc)
            return _remote(blk, blk, send_sems.at[t, 3 + s], recv_sems.at[t, 3 + s], to)

        def local(t):
            return pltpu.make_async_copy(ins[t], outs[t].at[k], loc_sems.at[t])

        @pl.when(step == 0)
        def _():
            for t in range(nt):
                local(t).start()
                for s, chip in enumerate(chips):
                    ici(t, s, k, (*chip, c), src=_half(ins[t], c)).start()

        for t in range(nt):
            @pl.when(step == min(nsteps - 1, int(fractions[t] * nsteps)))
            def _():
                for s in range(3):
                    ici(t, s, chip_k[s], me).wait_recv()
                    passed(t, s, c, sibling).start()

        @pl.when(step == nsteps - 1)
        def _():
            for t in range(nt):
                for s, chip in enumerate(chips):
                    passed(t, s, 1 - c, me).wait_recv()
            for t in range(nt):
                for s, chip in enumerate(chips):
                    ici(t, s, k, (*chip, c), src=_half(ins[t], c)).wait_send()
                    passed(t, s, c, sibling).wait_send()
                local(t).wait()

    return _Comm(blocks, [jax.ShapeDtypeStruct((4,) + b.shape, b.dtype) for b in blocks],
                 [pltpu.SemaphoreType.DMA((nt, 6)), pltpu.SemaphoreType.DMA((nt, 6)), pltpu.SemaphoreType.DMA((nt,))], run)


def _scatter_plan(grads, owners):
    nt = len(grads)
    shapes = [g.shape[1:] if g.ndim == 3 else (g.shape[0], g.shape[1] // 4) for g in grads]

    def run(step, nsteps, ins, outs, sems):
        send_sems, recv_sems, loc_sems = sems
        x, y, c = _place()
        me = 4 * x + 2 * y + c

        def target(r):
            tx = 1 - x if r & 2 else x
            ty = 1 - y if r & 1 else y
            return tx, ty

        def block(t, chip):
            if len(ins[t].shape) == 3:
                return ins[t].at[chip]
            n = shapes[t][1]
            return ins[t].at[:, pl.ds(pl.multiple_of(chip * n, _LANES), n)]

        def send(t, r):
            tx, ty = target(r)
            return _remote(block(t, 2 * tx + ty), outs[t].at[me], send_sems.at[t, r], recv_sems.at[t, 2 * r + c],
                           (tx, ty, owners[t]))

        def local(t):
            return pltpu.make_async_copy(block(t, 2 * x + y), outs[t].at[me], loc_sems.at[t])

        @pl.when(step == 0)
        def _():
            for t in range(nt):
                @pl.when(c == owners[t])
                def _():
                    local(t).start()

                @pl.when(c != owners[t])
                def _():
                    send(t, 0).start()

                for r in range(1, 4):
                    send(t, r).start()

        @pl.when(step == nsteps - 1)
        def _():
            for t in range(nt):
                @pl.when(c == owners[t])
                def _():
                    for r in range(4):
                        sx, sy = target(r)
                        for cs in range(2):
                            if r == 0 and cs == owners[t]:
                                continue
                            src_dev = 4 * sx + 2 * sy + cs
                            _remote(block(t, 0), outs[t].at[src_dev], send_sems.at[t, r], recv_sems.at[t, 2 * r + cs],
                                    (x, y, c)).wait_recv()
                    local(t).wait()

                @pl.when(c != owners[t])
                def _():
                    send(t, 0).wait_send()

                for r in range(1, 4):
                    send(t, r).wait_send()

    return _Comm(grads, [jax.ShapeDtypeStruct((8,) + s, g.dtype) for s, g in zip(shapes, grads)],
                 [pltpu.SemaphoreType.DMA((nt, 4)), pltpu.SemaphoreType.DMA((nt, 8)), pltpu.SemaphoreType.DMA((nt,))], run)


def _share_plan(reduced, owners):
    nt = len(reduced)

    def run(step, nsteps, ins, outs, sems):
        del ins
        send_sems, recv_sems = sems
        x, y, c = _place()

        def give(t, to):
            return _remote(outs[t], outs[t], send_sems.at[t], recv_sems.at[t], to)

        @pl.when(step == 0)
        def _():
            for t in range(nt):
                @pl.when(c == owners[t])
                def _():
                    give(t, (x, y, 1 - c)).start()

        @pl.when(step == nsteps - 1)
        def _():
            for t in range(nt):
                @pl.when(c == owners[t])
                def _():
                    give(t, (x, y, 1 - c)).wait_send()

                @pl.when(c != owners[t])
                def _():
                    give(t, (x, y, c)).wait_recv()

    return _Comm(reduced, [jax.ShapeDtypeStruct(r.shape, r.dtype) for r in reduced],
                 [pltpu.SemaphoreType.DMA((nt,)), pltpu.SemaphoreType.DMA((nt,))], run,
                 aliases={t: t for t in range(nt)})


def _join(a, b):
    ni, no, ns = len(a.inputs), len(a.out_shapes), len(a.sems)

    def run(step, nsteps, ins, outs, sems):
        a.run(step, nsteps, ins[:ni], outs[:no], sems[:ns])
        b.run(step, nsteps, ins[ni:], outs[no:], sems[ns:])

    aliases = dict(a.aliases)
    aliases.update({ni + i: no + o for i, o in b.aliases.items()})
    return _Comm(a.inputs + b.inputs, a.out_shapes + b.out_shapes, a.sems + b.sems, run, aliases)


def _peer(x, y, c, r):
    px = 1 - x if r & 4 else x
    py = 1 - y if r & 2 else y
    pc = 1 - c if r & 1 else c
    return (px, py, pc), 4 * px + 2 * py + pc


def _sum_slots(st, name):
    _, K, n = st.shape
    tr = next(t for t in (256, 128, 64, 32, 16) if K % t == 0)

    def kern(s_ref, o_ref):
        acc = s_ref[0].astype(_F32)
        for d in range(1, 8):
            acc = acc + s_ref[d].astype(_F32)
        o_ref[...] = acc.astype(o_ref.dtype)

    return pl.pallas_call(
        kern, name=name, grid=(K // tr,),
        in_specs=[pl.BlockSpec((8, tr, n), lambda i: (0, i, 0))], out_specs=_rows(tr, n),
        out_shape=jax.ShapeDtypeStruct((K, n), _ACT),
        compiler_params=_cparams("parallel"),
    )(st)


def _all_reduce_small(p):
    R = p.shape[0]

    def body(p_ref, o_ref, stage, send_sems, recv_sems):
        x, y, c = _place()
        me = 4 * x + 2 * y + c
        stage[me] = p_ref[...]
        sent = []
        for r in range(1, 8):
            to, _ = _peer(x, y, c, r)
            cp = _remote(p_ref, stage.at[me], send_sems.at[r - 1], recv_sems.at[r - 1], to)
            cp.start()
            sent.append(cp)
        for r in range(1, 8):
            _, src_dev = _peer(x, y, c, r)
            _remote(p_ref, stage.at[src_dev], send_sems.at[r - 1], recv_sems.at[r - 1], (x, y, c)).wait_recv()
        acc = stage[0]
        for d in range(1, 8):
            acc = acc + stage[d]
        o_ref[...] = acc
        for cp in sent:
            cp.wait_send()

    vm = pl.BlockSpec(memory_space=pltpu.VMEM)
    return pl.pallas_call(
        body, name="all_reduce_small",
        in_specs=[vm], out_specs=vm,
        out_shape=jax.ShapeDtypeStruct((R, _LANES), _F32),
        scratch_shapes=[pltpu.VMEM((8, R, _LANES), _F32), pltpu.SemaphoreType.DMA((7,)), pltpu.SemaphoreType.DMA((7,))],
    )(p)


def _adamw_update(gv, w_ref, m_ref, v_ref, gf_ref, d_ref, nm_ref, nv_ref):
    nm = _B1 * m_ref[...] + (1.0 - _B1) * gv
    nv = _B2 * v_ref[...] + (1.0 - _B2) * (gv * gv)
    m_hat = nm / (1.0 - _B1 ** _STEP)
    v_hat = nv / (1.0 - _B2 ** _STEP)
    gf_ref[...] = gv
    d_ref[...] = -_LR * (m_hat / (jnp.sqrt(v_hat) + _EPS) + _WD * w_ref[...])
    nm_ref[...] = nm
    nv_ref[...] = nv


def _adamw_layers(w, g_layers, m, v, name):
    _, K, n = w.shape
    tr = next(t for t in (256, 128, 64, 32, 16) if K % t == 0)

    def kern(w_ref, g0_ref, g1_ref, m_ref, v_ref, *out_refs):
        first = pl.program_id(0) == 0
        gv = jnp.where(first, g0_ref[...].astype(_F32), g1_ref[...].astype(_F32))
        _adamw_update(gv, w_ref, m_ref, v_ref, *out_refs)

    stacked = pl.BlockSpec((None, tr, n), lambda l, i: (l, i, 0))
    layer = pl.BlockSpec((tr, n), lambda l, i: (i, 0))
    return tuple(pl.pallas_call(
        kern, name=name, grid=(2, K // tr),
        in_specs=[stacked, layer, layer, stacked, stacked], out_specs=[stacked] * 4,
        out_shape=[jax.ShapeDtypeStruct(w.shape, _F32)] * 4,
        compiler_params=_cparams("parallel", "parallel"),
    )(w, g_layers[0], g_layers[1], m, v))


def _adamw(w, g, m, v, name):
    shape = w.shape
    w2, g2, m2, v2 = (a.reshape(-1, shape[-1]) for a in (w, g, m, v))
    R, C = w2.shape
    tr = next((t for t in (256, 128, 64, 32, 16) if R % t == 0), R)

    def kern(w_ref, g_ref, m_ref, v_ref, *out_refs):
        _adamw_update(g_ref[...].astype(_F32), w_ref, m_ref, v_ref, *out_refs)

    outs = pl.pallas_call(
        kern, name=name, grid=(R // tr,),
        in_specs=[_rows(tr, C)] * 4, out_specs=[_rows(tr, C)] * 4,
        out_shape=[jax.ShapeDtypeStruct((R, C), _F32)] * 4,
        compiler_params=_cparams("parallel"),
    )(w2, g2, m2, v2)
    return tuple(o.reshape(shape) for o in outs)


def _pack_small(gss, sq):
    parts = [gss[l][n].reshape(-1) for n in _SMALL for l in range(len(gss))] + [jnp.sum(sq).reshape(1)]
    flat = jnp.concatenate(parts)
    rows = -(-flat.shape[0] // (8 * _LANES)) * 8
    return jnp.pad(flat, (0, rows * _LANES - flat.shape[0])).reshape(rows, _LANES)


def _unpack_small(total, shapes):
    flat = total.reshape(-1)
    out, off = {}, 0
    for n in _SMALL:
        layers = []
        for _ in range(shapes[n][0]):
            size = 1
            for s in shapes[n][1:]:
                size *= s
            layers.append(flat[off:off + size].reshape(shapes[n][1:]))
            off += size
        out[n] = jnp.stack(layers)
    return out, flat[off]


_GATHER = {
    "band_fwd_0": [(0, "w_proj_a"), (0, "w_proj_b"), (0, "w_out"), (0, "w_ffn_out")],
    "stick_fwd_0": [(0, "w_ffn_in"), (1, "w_proj_a"), (1, "w_proj_b"), (1, "w_out")],
    "mix_fwd_0": [(1, "w_ffn_out")],
    "ffn_fwd_0": [(1, "w_in"), (1, "w_ffn_in")],
}
_SCATTER = {
    "band_bwd_1": [(1, "w_ffn_in"), (1, "w_ffn_out")],
    "stick_bwd_1": [(1, "w_proj_a"), (1, "w_proj_b"), (1, "w_out")],
    "band_bwd_0": [(1, "w_in"), (0, "w_ffn_in")],
    "stick_bwd_0": [(0, "w_ffn_out"), (0, "w_proj_a"), (0, "w_proj_b"), (0, "w_out")],
    "in_proj_bwd_0": [(0, "w_in")],
}
_SHARE = {"stick_bwd_1": "band_bwd_1", "band_bwd_0": "stick_bwd_1", "stick_bwd_0": "band_bwd_0", "grad_w_in_0": "stick_bwd_0"}


def _owner(key):
    del key
    return 1


def kernel(x, w_in, b_gate, rel_bias, w_proj_a, w_proj_b, w_out, ln1_g, ln1_b, w_ffn_in, w_ffn_out, ln2_g, ln2_b, loss_target, m_w_in, m_b_gate, m_rel_bias, m_w_proj_a, m_w_proj_b, m_w_out, m_ln1_g, m_ln1_b, m_w_ffn_in, m_w_ffn_out, m_ln2_g, m_ln2_b, v_w_in, v_b_gate, v_rel_bias, v_w_proj_a, v_w_proj_b, v_w_out, v_ln1_g, v_ln1_b, v_w_ffn_in, v_w_ffn_out, v_ln2_g, v_ln2_b):
    names = ("w_in", "b_gate", "rel_bias", "w_proj_a", "w_proj_b", "w_out", "ln1_g", "ln1_b", "w_ffn_in", "w_ffn_out", "ln2_g", "ln2_b")
    w = dict(zip(names, (w_in, b_gate, rel_bias, w_proj_a, w_proj_b, w_out, ln1_g, ln1_b, w_ffn_in, w_ffn_out, ln2_g, ln2_b)))
    m = dict(zip(names, (m_w_in, m_b_gate, m_rel_bias, m_w_proj_a, m_w_proj_b, m_w_out, m_ln1_g, m_ln1_b, m_w_ffn_in, m_w_ffn_out, m_ln2_g, m_ln2_b)))
    v = dict(zip(names, (v_w_in, v_b_gate, v_rel_bias, v_w_proj_a, v_w_proj_b, v_w_out, v_ln1_g, v_ln1_b, v_w_ffn_in, v_w_ffn_out, v_ln2_g, v_ln2_b)))
    T, D = x.shape[-2], x.shape[-1]
    assert w_in.shape[0] == 2, "the exchange schedule below is written for two layers"

    mine = [{n: w[n][l].astype(_MXU) for n in _DENSE} for l in range(2)]
    W = [dict(), dict()]
    gws = [dict(), dict()]
    slots, final = {}, {}

    def gather(keys):
        sizes = [mine[l][n].size for l, n in keys]
        passed, fractions = 0, []
        for s in sizes:
            passed += s
            fractions.append(0.15 + 0.6 * passed / sum(sizes))

        def done(outs):
            for (l, n), o in zip(keys, outs):
                W[l][n] = o
        return _gather_plan([mine[l][n] for l, n in keys], fractions), done

    def scatter(keys):
        comm = _scatter_plan([gws[l][n] for l, n in keys], [_owner(key) for key in keys])
        return comm, lambda outs: slots.update(zip(keys, outs))

    def share(keys):
        reduced = [_sum_slots(slots[key], f"sum_grad_{key[1]}_{key[0]}") for key in keys]
        comm = _share_plan(reduced, [_owner(key) for key in keys])
        return comm, lambda outs: final.update(zip(keys, outs))

    def both(first, second):
        (ca, da), (cb, db) = first, second
        na = len(ca.out_shapes)
        return _join(ca, cb), lambda outs: (da(outs[:na]), db(outs[na:]))

    plans = {key: functools.partial(gather, keys) for key, keys in _GATHER.items()}
    for key, keys in _SCATTER.items():
        plans[key] = functools.partial(scatter, keys)
    for key, scattered_under in _SHARE.items():
        handed = functools.partial(share, _SCATTER[scattered_under])
        carried = plans.get(key)
        plans[key] = handed if carried is None else (lambda carried=carried, handed=handed: both(carried(), handed()))
    small = {n: w[n] for n in _SMALL}
    sq, dx, _, gss = _local_step(x.reshape(T, D), loss_target.reshape(T, D), W, small,
                                  _Plans(plans, {0: mine[0]["w_in"]}), gws)

    comm, done = share(_SCATTER["in_proj_bwd_0"])
    done(_comm_only(comm, "share_last"))
    total = _all_reduce_small(_pack_small(gss, sq))
    small_grads, sq_all = _unpack_small(total, {n: w[n].shape for n in _SMALL})
    loss = 0.5 * sq_all / D

    grad, delta, new_m, new_v = {}, {}, {}, {}
    for n in names:
        if n in _DENSE:
            updated = _adamw_layers(w[n], [final[(l, n)] for l in range(2)], m[n], v[n], f"adamw_{n}")
        else:
            updated = _adamw(w[n], small_grads[n], m[n], v[n], f"adamw_{n}")
        grad[n], delta[n], new_m[n], new_v[n] = updated
    return (loss, dx.reshape(x.shape), *[grad[n] for n in names], *[delta[n] for n in names],
            *[new_m[n] for n in names], *[new_v[n] for n in names])
```

```python
import functools

import jax
import jax.numpy as jnp
from jax import lax
from jax.experimental import pallas as pl
from jax.experimental.pallas import tpu as pltpu

_MXU = jnp.bfloat16
_ACT = jnp.bfloat16
_F32 = jnp.float32

_HEAD = 64
_CHUNK = 64
_LANES = 128
_TQ = 128
_BAND_TILES = 5
_BIAS_TILES = 9
_REL_CLIP = 256
_LN_EPS = 1e-5
_MASKED = -1e30
_EXP_ZERO_BELOW = -87.34
_SB_WINDOW = 2
_SB_SUBTILES = 4
_BAND_SUBTILES = 8
_VMEM_LIMIT = 56 * 1024 * 1024
_GRAD_ACC_BYTES = 12 * 1024 * 1024

_LR, _B1, _B2, _EPS, _WD, _STEP = 0.001, 0.9, 0.999, 1e-08, 0.01, 10

_MESH = pl.DeviceIdType.MESH


def _dot(a, b):
    return jnp.dot(a, b, preferred_element_type=_F32)


def _dot_nt(a, b):
    return lax.dot_general(a, b, (((1,), (1,)), ((), ())), preferred_element_type=_F32)


def _dot_tn(a, b):
    return lax.dot_general(a, b, (((0,), (0,)), ((), ())), preferred_element_type=_F32)


def _cparams(*sem):
    return pltpu.CompilerParams(dimension_semantics=sem, vmem_limit_bytes=_VMEM_LIMIT)


def _rows(t, c):
    return pl.BlockSpec((t, c), lambda i: (i, 0))


def _whole(shape):
    return pl.BlockSpec(shape, lambda i: tuple(0 for _ in shape))


_ANY = pl.BlockSpec(memory_space=pl.ANY)


def _load_cols(w_hbm, w_vmem, sem):
    n = w_hbm.shape[-1]
    cps = [pltpu.make_async_copy(w_hbm.at[k], w_vmem.at[:, pl.ds(k * n, n)], sem.at[k]) for k in range(4)]
    for cp in cps:
        cp.start()
    for cp in cps:
        cp.wait()


def _load_rows(w_hbm, w_vmem, sem):
    r = w_hbm.shape[-2]
    cps = [pltpu.make_async_copy(w_hbm.at[k], w_vmem.at[pl.ds(k * r, r), :], sem.at[k]) for k in range(4)]
    for cp in cps:
        cp.start()
    for cp in cps:
        cp.wait()


def _ln_stats(u):
    mu = jnp.mean(u, axis=-1, keepdims=True)
    xc = u - mu
    var = jnp.mean(xc * xc, axis=-1, keepdims=True)
    rstd = lax.rsqrt(var + _LN_EPS)
    return xc * rstd, rstd


def _ln_bwd(u, dy, gamma):
    xhat, rstd = _ln_stats(u)
    dxh = dy * gamma
    m1 = jnp.mean(dxh, axis=-1, keepdims=True)
    m2 = jnp.mean(dxh * xhat, axis=-1, keepdims=True)
    du = rstd * (dxh - m1 - xhat * m2)
    return du, jnp.sum(dy * xhat, axis=0, keepdims=True), jnp.sum(dy, axis=0, keepdims=True), xhat


def _divisor_tile(n, cap):
    best = None
    for t in range(_LANES, min(n, cap) + 1, _LANES):
        if n % t == 0:
            best = t
    return best or n


class _Comm:
    def __init__(self, inputs, out_shapes, sems, run, aliases=None):
        self.inputs, self.out_shapes, self.sems, self.run = list(inputs), list(out_shapes), list(sems), run
        self.aliases = aliases or {}


def _call(kern, comm, *, name, grid, in_specs, out_specs, out_shape, scratch_shapes, args, semantics):
    in_specs, out_specs, out_shape, scratch_shapes = list(in_specs), list(out_specs), list(out_shape), list(scratch_shapes)
    if comm is None:
        outs = pl.pallas_call(kern, name=name, grid=grid, in_specs=in_specs, out_specs=out_specs, out_shape=out_shape,
                              scratch_shapes=scratch_shapes, compiler_params=_cparams(*semantics))(*args)
        return list(outs), []
    n_in, n_out, n_scr = len(in_specs), len(out_specs), len(scratch_shapes)
    ci, co = len(comm.inputs), len(comm.out_shapes)
    nsteps = functools.reduce(lambda a, b: a * b, grid, 1)

    def fused(*refs):
        a, b = n_in, n_in + ci
        c, d = b + n_out, b + n_out + co
        e = d + n_scr
        step = pl.program_id(0)
        for ax in range(1, len(grid)):
            step = step * grid[ax] + pl.program_id(ax)
        comm.run(step, nsteps, refs[a:b], refs[c:d], refs[e:])
        kern(*refs[:a], *refs[b:c], *refs[d:e])

    outs = pl.pallas_call(
        fused, name=name, grid=grid, in_specs=in_specs + [_ANY] * ci, out_specs=out_specs + [_ANY] * co,
        out_shape=out_shape + comm.out_shapes, scratch_shapes=scratch_shapes + comm.sems,
        input_output_aliases={n_in + i: n_out + o for i, o in comm.aliases.items()},
        compiler_params=_cparams(*("arbitrary" for _ in grid)))(*args, *comm.inputs)
    return list(outs[:n_out]), list(outs[n_out:])


def _comm_only(comm, name):
    def body(*refs):
        ci, co = len(comm.inputs), len(comm.out_shapes)
        comm.run(0, 1, refs[:ci], refs[ci:ci + co], refs[ci + co:])

    outs = pl.pallas_call(body, name=name, in_specs=[_ANY] * len(comm.inputs), out_specs=[_ANY] * len(comm.out_shapes),
                          out_shape=comm.out_shapes, scratch_shapes=comm.sems,
                          input_output_aliases=dict(comm.aliases))(*comm.inputs)
    return list(outs)


def _in_proj(x, w_in, layer, comm=None):
    T, D = x.shape
    N = 4 * w_in.shape[-1]
    tm = 512

    def kern(x_ref, w_hbm, h_ref, xb_ref, w_v, sem):
        @pl.when(pl.program_id(0) == 0)
        def _():
            _load_cols(w_hbm, w_v, sem)

        xb = x_ref[...].astype(_MXU)
        h_ref[...] = _dot(xb, w_v[...]).astype(h_ref.dtype)
        xb_ref[...] = xb.astype(xb_ref.dtype)

    return _call(
        kern, comm, name=f"in_proj_{layer}", grid=(T // tm,),
        in_specs=[_rows(tm, D), _ANY],
        out_specs=[_rows(tm, N), _rows(tm, D)],
        out_shape=[jax.ShapeDtypeStruct((T, N), _ACT), jax.ShapeDtypeStruct((T, D), _ACT)],
        scratch_shapes=[pltpu.VMEM((D, N), w_in.dtype), pltpu.SemaphoreType.DMA((4,))],
        args=(x, w_in), semantics=("arbitrary",))


def _in_proj_gathering(x, block, layer):
    T, D = x.shape
    n = block.shape[1]
    tm = min(T, 1024)
    nrows = T // tm
    pass_steps = [int(f * nrows) for f in (0.6, 1.0, 1.7)]
    px, py, _ = _place()
    order = jnp.stack([2 * px + py, 2 * (1 - px) + py, 2 * px + (1 - py), 2 * (1 - px) + (1 - py)]).astype(jnp.int32)

    def kern(order_ref, x_ref, blk_hbm, h_ref, xb_ref, w_hbm, w_v, send_sems, recv_sems, loc_sem, load_sem):
        del order_ref
        step = pl.program_id(0) * nrows + pl.program_id(1)
        x_, y_, c = _place()
        k = 2 * x_ + y_
        me, sibling = (x_, y_, c), (x_, y_, 1 - c)
        chips = [(1 - x_, y_), (x_, 1 - y_), (1 - x_, 1 - y_)]
        chip_k = [2 * cx + cy for cx, cy in chips]

        def ici(s, owner_k, to, src=None):
            dst = _half(w_hbm.at[owner_k], c)
            return _remote(dst if src is None else src, dst, send_sems.at[s], recv_sems.at[s], to)

        def passed(s, hc, to):
            blk = _half(w_hbm.at[chip_k[s]], hc)
            return _remote(blk, blk, send_sems.at[3 + s], recv_sems.at[3 + s], to)

        local = pltpu.make_async_copy(blk_hbm, w_hbm.at[k], loc_sem.at[0])

        def load(src):
            cp = pltpu.make_async_copy(src, w_v, load_sem.at[0])
            cp.start()
            cp.wait()

        @pl.when(step == 0)
        def _():
            for s, chip in enumerate(chips):
                ici(s, k, (*chip, c), src=_half(blk_hbm, c)).start()
            local.start()
            load(blk_hbm)

        for s in range(3):
            @pl.when(step == pass_steps[s])
            def _():
                ici(s, chip_k[s], me).wait_recv()
                passed(s, c, sibling).start()

            @pl.when(step == (s + 1) * nrows)
            def _():
                passed(s, 1 - c, me).wait_recv()
                load(w_hbm.at[chip_k[s]])

        xb = x_ref[...].astype(_MXU)
        h_ref[...] = _dot(xb, w_v[...]).astype(h_ref.dtype)

        @pl.when(pl.program_id(0) == 0)
        def _():
            xb_ref[...] = xb.astype(xb_ref.dtype)

        @pl.when(step == 4 * nrows - 1)
        def _():
            for s, chip in enumerate(chips):
                ici(s, k, (*chip, c), src=_half(blk_hbm, c)).wait_send()
                passed(s, c, sibling).wait_send()
            local.wait()

    assert all(pass_steps[s] <= (s + 1) * nrows for s in range(3))
    h, xb, w_in = pl.pallas_call(
        kern, name=f"in_proj_{layer}",
        grid_spec=pltpu.PrefetchScalarGridSpec(
            num_scalar_prefetch=1, grid=(4, nrows),
            in_specs=[pl.BlockSpec((tm, D), lambda j, i, o: (i, 0)), _ANY],
            out_specs=[pl.BlockSpec((tm, n), lambda j, i, o: (i, o[j])),
                       pl.BlockSpec((tm, D), lambda j, i, o: (jnp.where(j == 0, i, nrows - 1), 0)), _ANY],
            scratch_shapes=[pltpu.VMEM((D, n), block.dtype), pltpu.SemaphoreType.DMA((6,)),
                            pltpu.SemaphoreType.DMA((6,)), pltpu.SemaphoreType.DMA((1,)), pltpu.SemaphoreType.DMA((1,))]),
        out_shape=[jax.ShapeDtypeStruct((T, 4 * n), _ACT), jax.ShapeDtypeStruct((T, D), _ACT),
                   jax.ShapeDtypeStruct((4,) + block.shape, block.dtype)],
        compiler_params=_cparams("arbitrary", "arbitrary"))(order, x, block)
    return h, xb, w_in


def _gate_specs(h, tm, D):
    first = (h.shape[1] - 2 * D) // D
    assert first * D + 2 * D == h.shape[1]
    return [pl.BlockSpec((tm, D), lambda i: (i, first)), pl.BlockSpec((tm, D), lambda i: (i, first + 1))]


def _mix_fwd(oa, ob, h, x, wpa, wpb, wo, bg, gamma, beta, alpha, layer, comm=None):
    T, D = x.shape
    WA, WB = oa.shape[1], ob.shape[1]
    tm = 512

    def kern(oa_ref, ob_ref, hga_ref, hgb_ref, x_ref, bg_ref, g_ref, b_ref, wpa_h, wpb_h, wo_h,
             x1_ref, u1_ref, pre_ref, wpa_v, wpb_v, wo_v, sa, sb, so):
        @pl.when(pl.program_id(0) == 0)
        def _():
            _load_cols(wpa_h, wpa_v, sa)
            _load_cols(wpb_h, wpb_v, sb)
            _load_rows(wo_h, wo_v, so)

        ya = _dot(oa_ref[...].astype(_MXU), wpa_v[...])
        yb = _dot(ob_ref[...].astype(_MXU), wpb_v[...])
        bgv = bg_ref[...]
        ga = jax.nn.sigmoid(hga_ref[...].astype(_F32) + bgv[:, :D])
        gb = jax.nn.sigmoid(hgb_ref[...].astype(_F32) + bgv[:, D:])
        pre = ga * ya + gb * yb
        mix = _dot(pre.astype(_MXU), wo_v[...])
        u = alpha * x_ref[...] + mix
        xhat, _ = _ln_stats(u)
        x1_ref[...] = xhat * g_ref[...] + b_ref[...]
        u1_ref[...] = u
        pre_ref[...] = pre.astype(pre_ref.dtype)

    return _call(
        kern, comm, name=f"mix_fwd_{layer}", grid=(T // tm,),
        in_specs=[_rows(tm, WA), _rows(tm, WB), *_gate_specs(h, tm, D), _rows(tm, D),
                  _whole((1, 2 * D)), _whole((1, D)), _whole((1, D)), _ANY, _ANY, _ANY],
        out_specs=[_rows(tm, D)] * 3,
        out_shape=[jax.ShapeDtypeStruct((T, D), _F32), jax.ShapeDtypeStruct((T, D), _F32),
                   jax.ShapeDtypeStruct((T, D), _ACT)],
        scratch_shapes=[pltpu.VMEM((WA, D), wpa.dtype), pltpu.VMEM((WB, D), wpb.dtype), pltpu.VMEM((D, D), wo.dtype),
                        pltpu.SemaphoreType.DMA((4,)), pltpu.SemaphoreType.DMA((4,)), pltpu.SemaphoreType.DMA((4,))],
        args=(oa, ob, h, h, x, bg, gamma, beta, wpa, wpb, wo), semantics=("arbitrary",))


def _ffn_fwd(x1, wfi, wfo, gamma, beta, alpha, layer, comm=None):
    T, D = x1.shape
    F2 = 4 * wfi.shape[-1]
    F = F2 // 2
    tm = 512
    fc = F // 2

    def kern(x_ref, g_ref, b_ref, wi_h, wo_h, x2_ref, u2_ref, act_ref, gu_ref, xb_ref, wi_v, wo_v, si, so):
        @pl.when(pl.program_id(0) == 0)
        def _():
            _load_cols(wi_h, wi_v, si)
            _load_rows(wo_h, wo_v, so)

        x = x_ref[...]
        xb = x.astype(_MXU)
        xb_ref[...] = xb.astype(xb_ref.dtype)
        ffn = jnp.zeros((tm, D), _F32)
        for c in range(2):
            g = _dot(xb, wi_v[:, c * fc:(c + 1) * fc])
            u = _dot(xb, wi_v[:, F + c * fc:F + (c + 1) * fc])
            act = g * jax.nn.sigmoid(g) * u
            ab = act.astype(_MXU)
            ffn = ffn + _dot(ab, wo_v[c * fc:(c + 1) * fc, :])
            act_ref[:, c * fc:(c + 1) * fc] = ab.astype(act_ref.dtype)
            gu_ref[:, c * fc:(c + 1) * fc] = g.astype(gu_ref.dtype)
            gu_ref[:, F + c * fc:F + (c + 1) * fc] = u.astype(gu_ref.dtype)
        uu = alpha * x + ffn
        xhat, _ = _ln_stats(uu)
        x2_ref[...] = xhat * g_ref[...] + b_ref[...]
        u2_ref[...] = uu

    return _call(
        kern, comm, name=f"ffn_fwd_{layer}", grid=(T // tm,),
        in_specs=[_rows(tm, D), _whole((1, D)), _whole((1, D)), _ANY, _ANY],
        out_specs=[_rows(tm, D), _rows(tm, D), _rows(tm, F), _rows(tm, F2), _rows(tm, D)],
        out_shape=[jax.ShapeDtypeStruct((T, D), _F32), jax.ShapeDtypeStruct((T, D), _F32),
                   jax.ShapeDtypeStruct((T, F), _ACT), jax.ShapeDtypeStruct((T, F2), _ACT),
                   jax.ShapeDtypeStruct((T, D), _ACT)],
        scratch_shapes=[pltpu.VMEM((D, F2), wfi.dtype), pltpu.VMEM((F, D), wfo.dtype),
                        pltpu.SemaphoreType.DMA((4,)), pltpu.SemaphoreType.DMA((4,))],
        args=(x1, gamma, beta, wfi, wfo), semantics=("arbitrary",))


def _ffn_bwd(u2, dy_or_target, gu, gamma, beta, wfi, wfo, alpha, layer, last):
    T, D = u2.shape
    F2 = gu.shape[1]
    F = F2 // 2
    tm = 256
    fc = F // 2

    def kern(u_ref, dy_ref, gu_ref, g_ref, b_ref, wi_h, wo_h, dx_ref, dub_ref, dgu_ref, st_ref, wi_v, wo_v, si, so):
        @pl.when(pl.program_id(0) == 0)
        def _():
            _load_cols(wi_h, wi_v, si)
            _load_rows(wo_h, wo_v, so)
            st_ref[...] = jnp.zeros_like(st_ref)

        gam = g_ref[...]
        u = u_ref[...]
        if last:
            xhat0, _ = _ln_stats(u)
            err = xhat0 * gam + b_ref[...] - dy_ref[...]
            dy = err * (1.0 / D)
            st_ref[2:3, :] += jnp.sum(err * err, axis=0, keepdims=True)
        else:
            dy = dy_ref[...]
        du, dgam, dbet, _ = _ln_bwd(u, dy, gam)
        st_ref[0:1, :] += dgam
        st_ref[1:2, :] += dbet
        dub = du.astype(_MXU)
        dub_ref[...] = dub.astype(dub_ref.dtype)
        dx = alpha * du
        for c in range(2):
            dact = _dot_nt(dub, wo_v[c * fc:(c + 1) * fc, :])
            g = gu_ref[:, c * fc:(c + 1) * fc].astype(_F32)
            uu = gu_ref[:, F + c * fc:F + (c + 1) * fc].astype(_F32)
            sg = jax.nn.sigmoid(g)
            dg = (dact * uu * (sg * (1.0 + g * (1.0 - sg)))).astype(_MXU)
            dup = (dact * (g * sg)).astype(_MXU)
            dgu_ref[:, c * fc:(c + 1) * fc] = dg.astype(dgu_ref.dtype)
            dgu_ref[:, F + c * fc:F + (c + 1) * fc] = dup.astype(dgu_ref.dtype)
            dx = dx + _dot_nt(dg, wi_v[:, c * fc:(c + 1) * fc]) + _dot_nt(dup, wi_v[:, F + c * fc:F + (c + 1) * fc])
        dx_ref[...] = dx

    return pl.pallas_call(
        kern, name=f"ffn_bwd_{layer}", grid=(T // tm,),
        in_specs=[_rows(tm, D), _rows(tm, D), _rows(tm, F2), _whole((1, D)), _whole((1, D)), _ANY, _ANY],
        out_specs=[_rows(tm, D), _rows(tm, D), _rows(tm, F2), _whole((8, D))],
        out_shape=[jax.ShapeDtypeStruct((T, D), _F32), jax.ShapeDtypeStruct((T, D), _ACT),
                   jax.ShapeDtypeStruct((T, F2), _ACT), jax.ShapeDtypeStruct((8, D), _F32)],
        scratch_shapes=[pltpu.VMEM((D, F2), wfi.dtype), pltpu.VMEM((F, D), wfo.dtype),
                        pltpu.SemaphoreType.DMA((4,)), pltpu.SemaphoreType.DMA((4,))],
        compiler_params=_cparams("arbitrary"),
    )(u2, dy_or_target, gu, gamma, beta, wfi, wfo)


def _residual_nt(res, res_scale, pieces, w, name, comm=None):
    T, K = res.shape
    widths = [p.shape[1] for p in pieces]
    N = sum(widths)
    tm = 512

    def kern(r_ref, *refs):
        d_refs, (w_hbm, o_ref, w_v, sem) = refs[:len(pieces)], refs[len(pieces):]

        @pl.when(pl.program_id(0) == 0)
        def _():
            _load_cols(w_hbm, w_v, sem)

        acc = res_scale * r_ref[...]
        off = 0
        for d_ref, width in zip(d_refs, widths):
            acc = acc + _dot_nt(d_ref[...].astype(_MXU), w_v[:, off:off + width])
            off += width
        o_ref[...] = acc

    outs, extra = _call(
        kern, comm, name=name, grid=(T // tm,),
        in_specs=[_rows(tm, K)] + [_rows(tm, width) for width in widths] + [_ANY], out_specs=[_rows(tm, K)],
        out_shape=[jax.ShapeDtypeStruct((T, K), _F32)],
        scratch_shapes=[pltpu.VMEM((K, N), w.dtype), pltpu.SemaphoreType.DMA((4,))],
        args=(res, *pieces, w), semantics=("arbitrary",))
    return outs[0], extra


def _grad_w_pieces(a, pieces, name, comm=None):
    T, M = a.shape
    widths = [p.shape[1] for p in pieces]
    offsets = [sum(widths[:p]) for p in range(len(pieces))]
    N = sum(widths)
    tk = 1024 if T % 1024 == 0 else 512
    nk = T // tk

    def fits(ow):
        inside = lambda off, width: width < ow and off // ow == (off + width - 1) // ow
        whole = lambda off, width: off % ow == 0 and width % ow == 0
        return N % ow == 0 and all(inside(o, w) or whole(o, w) for o, w in zip(offsets, widths))

    ow = next(c for c in (1024, 512, 256, _LANES) if fits(c))

    def kern(a_ref, *refs):
        b_refs, (o_ref, acc) = refs[:len(pieces)], refs[len(pieces):]
        j, k = pl.program_id(0), pl.program_id(1)

        @pl.when(k == 0)
        def _():
            acc[...] = jnp.zeros_like(acc)

        for b_ref, off, width in zip(b_refs, offsets, widths):
            if width < ow:
                @pl.when(j == off // ow)
                def _():
                    acc[:, off % ow:off % ow + width] += _dot_tn(a_ref[...].astype(_MXU), b_ref[...].astype(_MXU))
            else:
                @pl.when(jnp.logical_and(j >= off // ow, j < (off + width) // ow))
                def _():
                    acc[...] += _dot_tn(a_ref[...].astype(_MXU), b_ref[...].astype(_MXU))

        @pl.when(k == nk - 1)
        def _():
            o_ref[...] = acc[...].astype(o_ref.dtype)

    def piece_spec(off, width):
        first, blocks = off // ow, max(width // ow, 1)

        def index(j, k):
            mine = jnp.logical_and(j >= first, j < first + blocks)
            return jnp.where(mine, k, 0), jnp.where(mine, j - first, 0)
        return pl.BlockSpec((tk, min(width, ow)), index)

    outs, extra = _call(
        kern, comm, name=name, grid=(N // ow, nk),
        in_specs=[pl.BlockSpec((tk, M), lambda j, k: (k, 0))] + [piece_spec(o, w) for o, w in zip(offsets, widths)],
        out_specs=[pl.BlockSpec((M, ow), lambda j, k: (0, j))],
        out_shape=[jax.ShapeDtypeStruct((M, N), _ACT)], scratch_shapes=[pltpu.VMEM((M, ow), _F32)],
        args=(a, *pieces), semantics=("parallel", "arbitrary"))
    return outs[0], extra


def _gcd(a, b):
    while b:
        a, b = b, a % b
    return a


def _mix_bwd(u1, dx1, oa, ob, h, wpa, wpb, wo, bg, gamma, layer):
    T, D = u1.shape
    WA, WB = wpa.shape[-2], wpb.shape[-2]
    tm = 512

    def kern(u_ref, dx_ref, oa_ref, ob_ref, hga_ref, hgb_ref, bg_ref, g_ref, wpa_h, wpb_h, wo_h,
             du_ref, dub_ref, dya_ref, dyb_ref, dhg_ref, doa_ref, dob_ref, st_ref,
             wpa_v, wpb_v, wo_v, sa, sb, so):
        @pl.when(pl.program_id(0) == 0)
        def _():
            _load_cols(wpa_h, wpa_v, sa)
            _load_cols(wpb_h, wpb_v, sb)
            _load_rows(wo_h, wo_v, so)
            st_ref[...] = jnp.zeros_like(st_ref)

        du, dgam, dbet, _ = _ln_bwd(u_ref[...], dx_ref[...], g_ref[...])
        st_ref[1:2, :D] += dgam
        st_ref[1:2, D:] += dbet
        du_ref[...] = du
        dub = du.astype(_MXU)
        dub_ref[...] = dub.astype(dub_ref.dtype)
        dpre = _dot_nt(dub, wo_v[...])
        bgv = bg_ref[...]
        ga = jax.nn.sigmoid(hga_ref[...].astype(_F32) + bgv[:, :D])
        gb = jax.nn.sigmoid(hgb_ref[...].astype(_F32) + bgv[:, D:])
        dya = (dpre * ga).astype(_MXU)
        dyb = (dpre * gb).astype(_MXU)
        dsa = dpre * _dot(oa_ref[...].astype(_MXU), wpa_v[...]) * (ga * (1.0 - ga))
        dsb = dpre * _dot(ob_ref[...].astype(_MXU), wpb_v[...]) * (gb * (1.0 - gb))
        st_ref[0:1, :D] += jnp.sum(dsa, axis=0, keepdims=True)
        st_ref[0:1, D:] += jnp.sum(dsb, axis=0, keepdims=True)
        dya_ref[...] = dya.astype(dya_ref.dtype)
        dyb_ref[...] = dyb.astype(dyb_ref.dtype)
        dhg_ref[:, :D] = dsa.astype(dhg_ref.dtype)
        dhg_ref[:, D:] = dsb.astype(dhg_ref.dtype)
        doa_ref[...] = _dot_nt(dya, wpa_v[...]).astype(doa_ref.dtype)
        dob_ref[...] = _dot_nt(dyb, wpb_v[...]).astype(dob_ref.dtype)

    return pl.pallas_call(
        kern, name=f"mix_bwd_{layer}", grid=(T // tm,),
        in_specs=[_rows(tm, D), _rows(tm, D), _rows(tm, WA), _rows(tm, WB), *_gate_specs(h, tm, D), _whole((1, 2 * D)),
                  _whole((1, D)), _ANY, _ANY, _ANY],
        out_specs=[_rows(tm, D)] * 4 + [_rows(tm, 2 * D), _rows(tm, WA), _rows(tm, WB), _whole((8, 2 * D))],
        out_shape=[jax.ShapeDtypeStruct((T, D), _F32)] + [jax.ShapeDtypeStruct((T, D), _ACT)] * 3
        + [jax.ShapeDtypeStruct((T, 2 * D), _ACT), jax.ShapeDtypeStruct((T, WA), _ACT),
           jax.ShapeDtypeStruct((T, WB), _ACT), jax.ShapeDtypeStruct((8, 2 * D), _F32)],
        scratch_shapes=[pltpu.VMEM((WA, D), wpa.dtype), pltpu.VMEM((WB, D), wpb.dtype), pltpu.VMEM((D, D), wo.dtype),
                        pltpu.SemaphoreType.DMA((4,)), pltpu.SemaphoreType.DMA((4,)), pltpu.SemaphoreType.DMA((4,))],
        compiler_params=_cparams("arbitrary"),
    )(u1, dx1, oa, ob, h, h, bg, gamma, wpa, wpb, wo)


def _grad_w(a, b, *, col_shards, name, comm=None):
    T, M = a.shape
    N = b.shape[1]
    tk = 1024 if T % 1024 == 0 else 512
    n = N // 4 if col_shards else N
    group = max(g for g in (1, 2, 4) if g == 1 or M * n * g * 4 <= _GRAD_ACC_BYTES)
    tn = n * group if col_shards else (N if M * N * 4 <= _GRAD_ACC_BYTES else _divisor_tile(N, _GRAD_ACC_BYTES // (4 * M)))
    nk = T // tk

    def kern(a_ref, b_ref, o_ref, acc):
        k = pl.program_id(1)

        @pl.when(k == 0)
        def _():
            acc[...] = jnp.zeros_like(acc)

        acc[...] += _dot_tn(a_ref[...].astype(_MXU), b_ref[...].astype(_MXU))

        @pl.when(k == nk - 1)
        def _():
            if col_shards:
                for s in range(group):
                    o_ref[s] = acc[:, s * n:(s + 1) * n].astype(o_ref.dtype)
            else:
                o_ref[...] = acc[...].astype(o_ref.dtype)

    if col_shards:
        out_spec = pl.BlockSpec((group, M, n), lambda j, k: (j, 0, 0))
        out_shape = jax.ShapeDtypeStruct((4, M, n), _ACT)
    else:
        out_spec = pl.BlockSpec((M, tn), lambda j, k: (0, j))
        out_shape = jax.ShapeDtypeStruct((M, N), _ACT)
    outs, extra = _call(
        kern, comm, name=name, grid=(N // tn, nk),
        in_specs=[pl.BlockSpec((tk, M), lambda j, k: (k, 0)), pl.BlockSpec((tk, tn), lambda j, k: (k, j))],
        out_specs=[out_spec], out_shape=[out_shape], scratch_shapes=[pltpu.VMEM((M, tn), _F32)],
        args=(a, b), semantics=("parallel", "arbitrary"))
    return outs[0], extra


def _bias_tiles(rel):
    H = rel.shape[0]
    span = _TQ * _BAND_TILES - 1
    edge = span - _REL_CLIP
    gvec = jnp.concatenate([jnp.broadcast_to(rel[:, :1], (H, edge)), rel, jnp.broadcast_to(rel[:, -1:], (H, edge))], axis=1)
    width = _BIAS_TILES * _TQ
    period = width + _TQ
    tiled = jnp.broadcast_to(jnp.pad(gvec[:, ::-1], ((0, 0), (0, 1)))[:, None, :], (H, _TQ, period))
    rows = tiled.reshape(H, _TQ * period)[:, :_TQ * (period - 1)].reshape(H, _TQ, period - 1)[:, :, _TQ - 1:]
    r = jnp.arange(_TQ)[:, None]
    u = jnp.arange(width)[None, :]
    d = 4 * _TQ + r - u
    rm = r % _CHUNK
    valid = (d >= rm - (_CHUNK - 1)) & (d <= rm + 8 * _CHUNK)
    tiles = jnp.where(valid[None], rows, _MASKED)
    return tiles.reshape(H // 2, 2 * _TQ, width)


def _strip_tiles(strip_ref, tiles_ref, to_strip=False):
    for ub in range(_BIAS_TILES):
        if to_strip:
            strip_ref[:, ub * _TQ:(ub + 1) * _TQ] = tiles_ref[ub]
        else:
            tiles_ref[ub] = strip_ref[:, ub * _TQ:(ub + 1) * _TQ]


def _fold_bias_grad(db):
    H = 2 * db.shape[0]
    width = _BIAS_TILES * _TQ
    period = width + _TQ
    x = jnp.pad(db.reshape(H, _TQ, width), ((0, 0), (0, 0), (_TQ - 1, 0)))
    skew = jnp.pad(x.reshape(H, _TQ * (period - 1)), ((0, 0), (0, _TQ))).reshape(H, _TQ, period)
    dg = skew.sum(axis=1)[:, :period - 1][:, ::-1]
    span = _TQ * _BAND_TILES - 1
    edge = span - _REL_CLIP
    mid = dg[:, edge:edge + 2 * _REL_CLIP + 1]
    lo = dg[:, :edge].sum(axis=1)
    hi = dg[:, edge + 2 * _REL_CLIP + 1:].sum(axis=1)
    return mid.at[:, 0].add(lo).at[:, -1].add(hi)


def _band_window(i):
    j0 = jnp.maximum(i - (_BAND_TILES - 1), 0)
    return j0, (_BAND_TILES - 1) - (i - j0)


def _head_masks():
    lane = lax.broadcasted_iota(jnp.int32, (1, _LANES), 1)
    return [(lane // _HEAD) == hh for hh in range(2)]


def _stack_heads(x, masks):
    return jnp.concatenate([jnp.where(m, x, jnp.zeros_like(x)) for m in masks], axis=0)


def _unstack_heads(y, masks):
    return jnp.where(masks[0], y[:_TQ], y[_TQ:])


def _scaled(q):
    return q * jnp.asarray(_HEAD ** -0.5, q.dtype)


def _band_probs(q2, k_ref, b_ref, j0, boff):
    s = []
    for j in range(_BAND_TILES):
        kj = k_ref[pl.ds(pl.multiple_of((j0 + j) * _TQ, _TQ), _TQ), :]
        s.append(_dot_nt(q2, kj) + b_ref[boff + j])
    m = jnp.max(functools.reduce(jnp.maximum, s), axis=-1, keepdims=True)
    p = [jnp.exp(x - m) for x in s]
    l = jnp.sum(functools.reduce(lambda a, b: a + b, p), axis=-1, keepdims=True)
    return p, 1.0 / l


def _qkv_specs(T, cb, npair, tq=_TQ):
    return [pl.BlockSpec((tq, _LANES), lambda h, i: (i, cb + h)),
            pl.BlockSpec((T, _LANES), lambda h, i: (0, cb + npair + h)),
            pl.BlockSpec((T, _LANES), lambda h, i: (0, cb + 2 * npair + h))]


def _attn_a_fwd(hq, bias, col0, width, layer, comm=None):
    T = hq.shape[0]
    npair = width // _LANES
    nsub = _BAND_SUBTILES
    tq = nsub * _TQ

    def kern(q_ref, k_ref, v_ref, b_ref, o_ref, b_tiles):
        @pl.when(pl.program_id(1) == 0)
        def _():
            _strip_tiles(b_ref, b_tiles)

        masks = _head_masks()
        q = _scaled(q_ref[...])
        for s in range(nsub):
            part = slice(s * _TQ, (s + 1) * _TQ)
            j0, boff = _band_window(nsub * pl.program_id(1) + s)
            p, inv = _band_probs(_stack_heads(q[part], masks), k_ref, b_tiles, j0, boff)
            o = jnp.zeros((2 * _TQ, _LANES), _F32)
            for j in range(_BAND_TILES):
                vj = v_ref[pl.ds(pl.multiple_of((j0 + j) * _TQ, _TQ), _TQ), :]
                o = o + _dot(p[j].astype(_MXU), vj)
            o_ref[part, :] = _unstack_heads(o * inv, masks).astype(o_ref.dtype)

    outs, extra = _call(
        kern, comm, name=f"band_attn_fwd_{layer}", grid=(npair, T // tq),
        in_specs=_qkv_specs(T, col0 // _LANES, npair, tq)
        + [pl.BlockSpec((None, 2 * _TQ, _BIAS_TILES * _TQ), lambda h, i: (h, 0, 0))],
        out_specs=[pl.BlockSpec((tq, _LANES), lambda h, i: (i, h))],
        out_shape=[jax.ShapeDtypeStruct((T, width), _ACT)],
        scratch_shapes=[pltpu.VMEM((_BIAS_TILES, 2 * _TQ, _TQ), _F32)],
        args=(hq, hq, hq, bias), semantics=("arbitrary", "arbitrary"))
    return outs[0], extra


def _attn_a_bwd(hq, bias, do, col0, width, layer, comm=None):
    T = hq.shape[0]
    npair = width // _LANES
    nsub = _BAND_SUBTILES
    tq = nsub * _TQ
    nq = T // tq
    scale = _HEAD ** -0.5

    def kern(q_ref, k_ref, v_ref, b_ref, do_ref, dq_ref, dk_ref, dv_ref, db_ref, dk_acc, dv_acc, b_tiles, db_acc):
        i = pl.program_id(1)

        @pl.when(i == 0)
        def _():
            _strip_tiles(b_ref, b_tiles)
            dk_acc[...] = jnp.zeros_like(dk_acc)
            dv_acc[...] = jnp.zeros_like(dv_acc)
            db_acc[...] = jnp.zeros_like(db_acc)

        masks = _head_masks()
        q = _scaled(q_ref[...])
        do_t = do_ref[...]
        for s in range(nsub):
            part = slice(s * _TQ, (s + 1) * _TQ)
            j0, boff = _band_window(nsub * i + s)
            q2 = _stack_heads(q[part], masks)
            do2 = _stack_heads(do_t[part], masks).astype(_MXU)
            p, inv = _band_probs(q2, k_ref, b_tiles, j0, boff)
            rows = [pl.ds(pl.multiple_of((j0 + j) * _TQ, _TQ), _TQ) for j in range(_BAND_TILES)]
            p = [x * inv for x in p]
            dp = [_dot_nt(do2, v_ref[rows[j], :]) for j in range(_BAND_TILES)]
            delta = jnp.sum(functools.reduce(lambda a, b: a + b, [p[j] * dp[j] for j in range(_BAND_TILES)]),
                            axis=-1, keepdims=True)
            dq = jnp.zeros((2 * _TQ, _LANES), _F32)
            for j in range(_BAND_TILES):
                ds = p[j] * (dp[j] - delta)
                db_acc[boff + j] += ds
                dsb = ds.astype(_MXU)
                dq = dq + _dot(dsb, k_ref[rows[j], :])
                dk_acc[rows[j], :] += _dot_tn(dsb, q2)
                dv_acc[rows[j], :] += _dot_tn(p[j].astype(_MXU), do2)
            dq_ref[part, :] = (_unstack_heads(dq, masks) * scale).astype(dq_ref.dtype)

        @pl.when(i == nq - 1)
        def _():
            dk_ref[...] = dk_acc[...].astype(dk_ref.dtype)
            dv_ref[...] = dv_acc[...].astype(dv_ref.dtype)
            _strip_tiles(db_ref, db_acc, to_strip=True)

    strip = pl.BlockSpec((None, 2 * _TQ, _BIAS_TILES * _TQ), lambda h, i: (h, 0, 0))
    tile = pl.BlockSpec((tq, _LANES), lambda h, i: (i, h))
    column = pl.BlockSpec((T, _LANES), lambda h, i: (0, h))
    outs, extra = _call(
        kern, comm, name=f"band_attn_bwd_{layer}", grid=(npair, nq),
        in_specs=_qkv_specs(T, col0 // _LANES, npair, tq) + [strip, tile],
        out_specs=[tile, column, column, strip],
        out_shape=[jax.ShapeDtypeStruct((T, width), _ACT)] * 3
        + [jax.ShapeDtypeStruct((npair, 2 * _TQ, _BIAS_TILES * _TQ), _F32)],
        scratch_shapes=[pltpu.VMEM((T, _LANES), _F32), pltpu.VMEM((T, _LANES), _F32),
                        pltpu.VMEM((_BIAS_TILES, 2 * _TQ, _TQ), _F32), pltpu.VMEM((_BIAS_TILES, 2 * _TQ, _TQ), _F32)],
        args=(hq, hq, hq, bias, do), semantics=("arbitrary", "arbitrary"))
    return outs, extra


def _suffix_matrix():
    r = lax.broadcasted_iota(jnp.int32, (_TQ, _TQ), 0)
    c = lax.broadcasted_iota(jnp.int32, (_TQ, _TQ), 1)
    r2 = lax.broadcasted_iota(jnp.int32, (2 * _TQ, _TQ), 0)
    c2 = lax.broadcasted_iota(jnp.int32, (2 * _TQ, _TQ), 1)
    return (r > c).astype(_MXU), c2 - (r2 & (_TQ - 1))


def _suffix_sums(xs, tri):
    n, k = xs[0].shape[0], len(xs)
    his = [x.astype(_MXU) for x in xs]
    los = [(x - h.astype(_F32)).astype(_MXU) for x, h in zip(xs, his)]
    y = _dot(jnp.concatenate(his + los, axis=0), tri)
    return [y[j * n:(j + 1) * n] + y[(k + j) * n:(k + j + 1) * n] for j in range(k)]


def _stick_tiles(tiles, rel, carry_l, tri):
    zs = [_dot_nt(qs, kj) for qs, kj, _, _ in tiles]
    Ls, masks = [], []
    for z, (_, _, jj, _) in zip(zs, tiles):
        nsp = -(jnp.maximum(z, 0.0) + jnp.log(1.0 + jnp.exp(-jnp.abs(z))))
        if isinstance(jj, int):
            mask = (rel < 0) if jj == 0 else None
        else:
            mask = rel < jnp.where(jj == 0, 0, _TQ)
        Ls.append(nsp if mask is None else jnp.where(mask, nsp, 0.0))
        masks.append(mask)
    carry_l = list(carry_l)
    ws = []
    for z, L, suffix, mask, (_, _, _, sub) in zip(zs, Ls, _suffix_sums(Ls, tri), masks, tiles):
        w = jnp.exp(z + L + suffix + carry_l[sub])
        ws.append(w if mask is None else jnp.where(mask, w, 0.0))
        carry_l[sub] = carry_l[sub] + jnp.sum(L, axis=-1, keepdims=True)
    return zs, Ls, ws, masks, carry_l


def _sweep(i, step, zero):
    nsub = _SB_SUBTILES

    def window():
        tiles = [(s, jj) for jj in range(_SB_WINDOW) for s in range(nsub)]
        return tuple((jnp.int32(_SB_WINDOW),) + c for c in step(tiles, [zero] * nsub))

    start = lax.cond(i >= -(-(_SB_WINDOW - 1) // nsub), window, lambda: tuple((jnp.int32(0),) + zero for _ in range(nsub)))
    outs = []
    for s in range(nsub):
        def done(c, s=s):
            return jnp.logical_or(c[0] > nsub * i + s, jnp.max(c[1]) < _EXP_ZERO_BELOW)

        def more(c, s=s):
            carries = [None] * nsub
            carries[s] = c[1:]
            return (c[0] + 1,) + step([(s, c[0])], carries)[s]

        outs.append(lax.while_loop(lambda c, done=done: jnp.logical_not(done(c)), more, start[s]))
    return outs


def _sb_fwd(hq, col0, width, layer, comm=None):
    T = hq.shape[0]
    npair = width // _LANES
    nsub = _SB_SUBTILES
    tq = nsub * _TQ

    def kern(q_ref, k_ref, v_ref, o_ref):
        i = pl.program_id(1)
        masks = _head_masks()
        tri, rel = _suffix_matrix()
        q = _scaled(q_ref[...])
        q2 = [_stack_heads(q[s * _TQ:(s + 1) * _TQ], masks) for s in range(nsub)]

        def step(tiles, carries):
            rows = [pl.ds(pl.multiple_of((nsub * i + s - jj) * _TQ, _TQ), _TQ) for s, jj in tiles]
            cls = [None if c is None else c[0] for c in carries]
            accs = [None if c is None else c[1] for c in carries]
            _, _, ws, _, cls = _stick_tiles([(q2[s], k_ref[r, :], jj, s) for (s, jj), r in zip(tiles, rows)], rel, cls, tri)
            for w, r, (s, _) in zip(ws, rows, tiles):
                accs[s] = accs[s] + _dot(w.astype(_MXU), v_ref[r, :])
            return [None if c is None else (cls[s], accs[s]) for s, c in enumerate(carries)]

        outs = _sweep(i, step, (jnp.zeros((2 * _TQ, 1), _F32), jnp.zeros((2 * _TQ, _LANES), _F32)))
        for s in range(nsub):
            o_ref[s * _TQ:(s + 1) * _TQ, :] = _unstack_heads(outs[s][2], masks)

    outs, extra = _call(
        kern, comm, name=f"stick_attn_fwd_{layer}", grid=(npair, T // tq),
        in_specs=_qkv_specs(T, col0 // _LANES, npair, tq),
        out_specs=[pl.BlockSpec((tq, _LANES), lambda h, i: (i, h))],
        out_shape=[jax.ShapeDtypeStruct((T, width), _F32)], scratch_shapes=[],
        args=(hq, hq, hq), semantics=("arbitrary", "arbitrary"))
    return outs[0], extra


def _sb_bwd(hq, o, do, col0, width, layer, comm=None):
    T = hq.shape[0]
    npair = width // _LANES
    nsub = _SB_SUBTILES
    tq = nsub * _TQ
    nq = T // tq
    scale = _HEAD ** -0.5

    def kern(q_ref, k_ref, v_ref, o_ref, do_ref, dq_ref, dk_ref, dv_ref, dk_acc, dv_acc):
        i = pl.program_id(1)

        @pl.when(i == 0)
        def _():
            dk_acc[...] = jnp.zeros_like(dk_acc)
            dv_acc[...] = jnp.zeros_like(dv_acc)

        masks = _head_masks()
        tri, rel = _suffix_matrix()
        q = _scaled(q_ref[...])
        do_t = do_ref[...]
        prod = do_t.astype(_F32) * o_ref[...]
        part = [slice(s * _TQ, (s + 1) * _TQ) for s in range(nsub)]
        q2 = [_stack_heads(q[p], masks) for p in part]
        do2 = [_stack_heads(do_t[p], masks).astype(_MXU) for p in part]
        dsum = [jnp.sum(_stack_heads(prod[p], masks), axis=-1, keepdims=True) for p in part]

        def step(tiles, carries):
            rows = [pl.ds(pl.multiple_of((nsub * i + s - jj) * _TQ, _TQ), _TQ) for s, jj in tiles]
            kjs = [k_ref[r, :] for r in rows]
            cls, cgs, dqs = ([None if c is None else c[n] for c in carries] for n in range(3))
            zs, Ls, ws, tile_masks, cls = _stick_tiles([(q2[s], kj, jj, s) for (s, jj), kj in zip(tiles, kjs)], rel, cls, tri)
            wbs = [w.astype(_MXU) for w in ws]
            gs = [wb.astype(_F32) * _dot_nt(do2[s], v_ref[r, :]) for wb, r, (s, _) in zip(wbs, rows, tiles)]
            for z, L, g, later, mask, wb, kj, r, (s, _) in zip(zs, Ls, gs, _suffix_sums(gs, tri), tile_masks, wbs, kjs,
                                                               rows, tiles):
                dz = g - jnp.exp(z + L) * (dsum[s] - (later + cgs[s]))
                if mask is not None:
                    dz = jnp.where(mask, dz, 0.0)
                dzb = dz.astype(_MXU)
                dk_acc[r, :] += _dot_tn(dzb, q2[s])
                dv_acc[r, :] += _dot_tn(wb, do2[s])
                dqs[s] = dqs[s] + _dot(dzb, kj)
                cgs[s] = cgs[s] + jnp.sum(g, axis=-1, keepdims=True)
            return [None if c is None else (cls[s], cgs[s], dqs[s]) for s, c in enumerate(carries)]

        zc = jnp.zeros((2 * _TQ, 1), _F32)
        outs = _sweep(i, step, (zc, zc, jnp.zeros((2 * _TQ, _LANES), _F32)))
        for s in range(nsub):
            dq_ref[part[s], :] = (_unstack_heads(outs[s][3], masks) * scale).astype(dq_ref.dtype)

        @pl.when(i == nq - 1)
        def _():
            dk_ref[...] = dk_acc[...].astype(dk_ref.dtype)
            dv_ref[...] = dv_acc[...].astype(dv_ref.dtype)

    tile_spec = pl.BlockSpec((tq, _LANES), lambda h, i: (i, h))
    column = pl.BlockSpec((T, _LANES), lambda h, i: (0, h))
    outs, extra = _call(
        kern, comm, name=f"stick_attn_bwd_{layer}", grid=(npair, nq),
        in_specs=_qkv_specs(T, col0 // _LANES, npair, tq) + [tile_spec, tile_spec],
        out_specs=[tile_spec, column, column],
        out_shape=[jax.ShapeDtypeStruct((T, width), _ACT)] * 3,
        scratch_shapes=[pltpu.VMEM((T, _LANES), _F32), pltpu.VMEM((T, _LANES), _F32)],
        args=(hq, hq, hq, o, do), semantics=("arbitrary", "arbitrary"))
    return outs, extra


_DENSE = ("w_in", "w_proj_a", "w_proj_b", "w_out", "w_ffn_in", "w_ffn_out")
_COL_SHARDED = {"w_in": True, "w_proj_a": True, "w_proj_b": True, "w_out": False, "w_ffn_in": True, "w_ffn_out": False}
_SMALL = ("b_gate", "rel_bias", "ln1_g", "ln1_b", "ln2_g", "ln2_b")


class _Plans:
    def __init__(self, plans=None, own_w_in=None):
        self.plans = plans or {}
        self.own_w_in = own_w_in or {}

    def start(self, key):
        if key not in self.plans:
            return None, None
        return self.plans[key]()

    @staticmethod
    def finish(done, extra):
        if done is not None:
            done(extra)


def _layer_fwd(x, W, small, l, alpha, plans):
    WA = small["rel_bias"].shape[1] * _HEAD
    row = lambda v: v[l].reshape(1, -1)
    if l in plans.own_w_in:
        h, xb, W["w_in"] = _in_proj_gathering(x, plans.own_w_in[l], l)
    else:
        comm, done = plans.start(f"in_proj_{l}")
        (h, xb), extra = _in_proj(x, W["w_in"], l, comm)
        plans.finish(done, extra)
    WB = (h.shape[1] - 2 * x.shape[1] - 3 * WA) // 3
    bias = _bias_tiles(small["rel_bias"][l])
    comm, done = plans.start(f"band_fwd_{l}")
    oa, extra = _attn_a_fwd(h, bias, 0, WA, l, comm)
    plans.finish(done, extra)
    comm, done = plans.start(f"stick_fwd_{l}")
    ob, extra = _sb_fwd(h, 3 * WA, WB, l, comm)
    plans.finish(done, extra)
    comm, done = plans.start(f"mix_fwd_{l}")
    (x1, u1, pre), extra = _mix_fwd(oa, ob, h, x, W["w_proj_a"], W["w_proj_b"], W["w_out"], row(small["b_gate"]),
                                            row(small["ln1_g"]), row(small["ln1_b"]), alpha, l, comm)
    plans.finish(done, extra)
    comm, done = plans.start(f"ffn_fwd_{l}")
    (x2, u2, act, gu, x1b), extra = _ffn_fwd(x1, W["w_ffn_in"], W["w_ffn_out"], row(small["ln2_g"]),
                                             row(small["ln2_b"]), alpha, l, comm)
    plans.finish(done, extra)
    return x2, dict(xb=xb, h=h, bias=bias, oa=oa, ob=ob, x1b=x1b, u1=u1, pre=pre, u2=u2, act=act, gu=gu)


def _layer_bwd(dy_or_target, S, W, small, l, last, alpha, plans, gw):
    D = S["xb"].shape[1]
    WA, WB = S["oa"].shape[1], S["ob"].shape[1]
    row = lambda v: v[l].reshape(1, -1)

    def blocks(g, n):
        return g if _COL_SHARDED[n] else g.reshape(4, g.shape[0] // 4, g.shape[1])

    dx1, du2b, dgu, st2 = _ffn_bwd(S["u2"], dy_or_target, S["gu"], row(small["ln2_g"]), row(small["ln2_b"]),
                                   W["w_ffn_in"], W["w_ffn_out"], alpha, l, last)
    gw["w_ffn_in"] = blocks(_grad_w(S["x1b"], dgu, col_shards=True, name=f"grad_w_ffn_in_{l}")[0], "w_ffn_in")
    gw["w_ffn_out"] = blocks(_grad_w(S["act"], du2b, col_shards=False, name=f"grad_w_ffn_out_{l}")[0], "w_ffn_out")
    du1, du1b, dya, dyb, dhg, doa, dob, st1 = _mix_bwd(S["u1"], dx1, S["oa"], S["ob"], S["h"], W["w_proj_a"],
                                                       W["w_proj_b"], W["w_out"], row(small["b_gate"]),
                                                       row(small["ln1_g"]), l)
    gw["w_out"] = blocks(_grad_w(S["pre"], du1b, col_shards=False, name=f"grad_w_out_{l}")[0], "w_out")
    gw["w_proj_a"] = blocks(_grad_w(S["oa"], dya, col_shards=True, name=f"grad_w_proj_a_{l}")[0], "w_proj_a")
    gw["w_proj_b"] = blocks(_grad_w(S["ob"], dyb, col_shards=True, name=f"grad_w_proj_b_{l}")[0], "w_proj_b")
    comm, done = plans.start(f"band_bwd_{l}")
    (dqa, dka, dva, dbias), extra = _attn_a_bwd(S["h"], S["bias"], doa, 0, WA, l, comm)
    plans.finish(done, extra)
    comm, done = plans.start(f"stick_bwd_{l}")
    (dqb, dkb, dvb), extra = _sb_bwd(S["h"], S["ob"], dob, 3 * WA, WB, l, comm)
    plans.finish(done, extra)
    dh = [dqa, dka, dva, dqb, dkb, dvb, dhg]
    comm, done = plans.start(f"grad_w_in_{l}")
    gw["w_in"], extra = _grad_w_pieces(S["xb"], dh, f"grad_w_in_{l}", comm)
    plans.finish(done, extra)
    comm, done = plans.start(f"in_proj_bwd_{l}")
    dx, extra = _residual_nt(du1, alpha, dh, W["w_in"], f"in_proj_bwd_{l}", comm)
    plans.finish(done, extra)
    gs = dict(b_gate=st1[0], rel_bias=_fold_bias_grad(dbias), ln1_g=st1[1, :D], ln1_b=st1[1, D:],
              ln2_g=st2[0], ln2_b=st2[1])
    return dx, gs, st2[2]


def _local_step(x, target, W, small, plans=None, gws=None):
    depth = len(W)
    alpha = float((2 * depth) ** 0.25)
    plans = plans or _Plans()
    gws = gws if gws is not None else [dict() for _ in range(depth)]
    saved = []
    h = x
    for l in range(depth):
        h, S = _layer_fwd(h, W[l], small, l, alpha, plans)
        saved.append(S)
    gss = [None] * depth
    d = target
    sq = None
    for l in reversed(range(depth)):
        d, gss[l], sq_l = _layer_bwd(d, saved[l], W[l], small, l, l == depth - 1, alpha, plans, gws[l])
        if l == depth - 1:
            sq = sq_l
    return sq, d, gws, gss


def _place():
    return lax.axis_index("x"), lax.axis_index("y"), lax.axis_index("c")


def _remote(src, dst, send_sem, recv_sem, to):
    return pltpu.make_async_remote_copy(src_ref=src, dst_ref=dst, send_sem=send_sem, recv_sem=recv_sem,
                                        device_id=to, device_id_type=_MESH)


def _half(ref, hc):
    kh = ref.shape[0] // 2
    return ref.at[pl.ds(pl.multiple_of(hc * kh, 16), kh), :]


def _gather_plan(blocks, fractions):
    nt = len(blocks)

    def run(step, nsteps, ins, outs, sems):
        send_sems, recv_sems, loc_sems = sems
        x, y, c = _place()
        k = 2 * x + y
        me, sibling = (x, y, c), (x, y, 1 - c)
        chips = [(1 - x, y), (x, 1 - y), (1 - x, 1 - y)]
        chip_k = [2 * cx + cy for cx, cy in chips]

        def ici(t, s, owner_k, to, src=None):
            dst = _half(outs[t].at[owner_k], c)
            return _remote(dst if src is None else src, dst, send_sems.at[t, s], recv_sems.at[t, s], to)

        def passed(t, s, hc, to):
            blk = _half(outs[t].at[chip_k[s]], hc)
            return _remote(blk, blk, send_sems.at[t, 3 + s], recv_sems.at[t, 3 + s], to)

        def local(t):
            return pltpu.make_async_copy(ins[t], outs[t].at[k], loc_sems.at[t])

        @pl.when(step == 0)
        def _():
            for t in range(nt):
                local(t).start()
                for s, chip in enumerate(chips):
                    ici(t, s, k, (*chip, c), src=_half(ins[t], c)).start()

        for t in range(nt):
            @pl.when(step == min(nsteps - 1, int(fractions[t] * nsteps)))
            def _():
                for s in range(3):
                    ici(t, s, chip_k[s], me).wait_recv()
                    passed(t, s, c, sibling).start()

        @pl.when(step == nsteps - 1)
        def _():
            for t in range(nt):
                for s, chip in enumerate(chips):
                    passed(t, s, 1 - c, me).wait_recv()
            for t in range(nt):
                for s, chip in enumerate(chips):
                    ici(t, s, k, (*chip, c), src=_half(ins[t], c)).wait_send()
                    passed(t, s, c, sibling).wait_send()
                local(t).wait()

    return _Comm(blocks, [jax.ShapeDtypeStruct((4,) + b.shape, b.dtype) for b in blocks],
                 [pltpu.SemaphoreType.DMA((nt, 6)), pltpu.SemaphoreType.DMA((nt, 6)), pltpu.SemaphoreType.DMA((nt,))], run)


def _scatter_plan(grads, owners):
    nt = len(grads)
    shapes = [g.shape[1:] if g.ndim == 3 else (g.shape[0], g.shape[1] // 4) for g in grads]

    def run(step, nsteps, ins, outs, sems):
        send_sems, recv_sems, loc_sems = sems
        x, y, c = _place()
        me = 4 * x + 2 * y + c

        def target(r):
            tx = 1 - x if r & 2 else x
            ty = 1 - y if r & 1 else y
            return tx, ty

        def block(t, chip):
            if len(ins[t].shape) == 3:
                return ins[t].at[chip]
            n = shapes[t][1]
            return ins[t].at[:, pl.ds(pl.multiple_of(chip * n, _LANES), n)]

        def send(t, r):
            tx, ty = target(r)
            return _remote(block(t, 2 * tx + ty), outs[t].at[me], send_sems.at[t, r], recv_sems.at[t, 2 * r + c],
                           (tx, ty, owners[t]))

        def local(t):
            return pltpu.make_async_copy(block(t, 2 * x + y), outs[t].at[me], loc_sems.at[t])

        @pl.when(step == 0)
        def _():
            for t in range(nt):
                @pl.when(c == owners[t])
                def _():
                    local(t).start()

                @pl.when(c != owners[t])
                def _():
                    send(t, 0).start()

                for r in range(1, 4):
                    send(t, r).start()

        @pl.when(step == nsteps - 1)
        def _():
            for t in range(nt):
                @pl.when(c == owners[t])
                def _():
                    for r in range(4):
                        sx, sy = target(r)
                        for cs in range(2):
                            if r == 0 and cs == owners[t]:
                                continue
                            src_dev = 4 * sx + 2 * sy + cs
                            _remote(block(t, 0), outs[t].at[src_dev], send_sems.at[t, r], recv_sems.at[t, 2 * r + cs],
                                    (x, y, c)).wait_recv()
                    local(t).wait()

                @pl.when(c != owners[t])
                def _():
                    send(t, 0).wait_send()

                for r in range(1, 4):
                    send(t, r).wait_send()

    return _Comm(grads, [jax.ShapeDtypeStruct((8,) + s, g.dtype) for s, g in zip(shapes, grads)],
                 [pltpu.SemaphoreType.DMA((nt, 4)), pltpu.SemaphoreType.DMA((nt, 8)), pltpu.SemaphoreType.DMA((nt,))], run)


def _share_plan(reduced, owners):
    nt = len(reduced)

    def run(step, nsteps, ins, outs, sems):
        del ins
        send_sems, recv_sems = sems
        x, y, c = _place()

        def give(t, to):
            return _remote(outs[t], outs[t], send_sems.at[t], recv_sems.at[t], to)

        @pl.when(step == 0)
        def _():
            for t in range(nt):
                @pl.when(c == owners[t])
                def _():
                    give(t, (x, y, 1 - c)).start()

        @pl.when(step == nsteps - 1)
        def _():
            for t in range(nt):
                @pl.when(c == owners[t])
                def _():
                    give(t, (x, y, 1 - c)).wait_send()

                @pl.when(c != owners[t])
                def _():
                    give(t, (x, y, c)).wait_recv()

    return _Comm(reduced, [jax.ShapeDtypeStruct(r.shape, r.dtype) for r in reduced],
                 [pltpu.SemaphoreType.DMA((nt,)), pltpu.SemaphoreType.DMA((nt,))], run,
                 aliases={t: t for t in range(nt)})


def _join(a, b):
    ni, no, ns = len(a.inputs), len(a.out_shapes), len(a.sems)

    def run(step, nsteps, ins, outs, sems):
        a.run(step, nsteps, ins[:ni], outs[:no], sems[:ns])
        b.run(step, nsteps, ins[ni:], outs[no:], sems[ns:])

    aliases = dict(a.aliases)
    aliases.update({ni + i: no + o for i, o in b.aliases.items()})
    return _Comm(a.inputs + b.inputs, a.out_shapes + b.out_shapes, a.sems + b.sems, run, aliases)


def _peer(x, y, c, r):
    px = 1 - x if r & 4 else x
    py = 1 - y if r & 2 else y
    pc = 1 - c if r & 1 else c
    return (px, py, pc), 4 * px + 2 * py + pc


def _sum_slots(st, name):
    _, K, n = st.shape
    tr = next(t for t in (256, 128, 64, 32, 16) if K % t == 0)

    def kern(s_ref, o_ref):
        acc = s_ref[0].astype(_F32)
        for d in range(1, 8):
            acc = acc + s_ref[d].astype(_F32)
        o_ref[...] = acc.astype(o_ref.dtype)

    return pl.pallas_call(
        kern, name=name, grid=(K // tr,),
        in_specs=[pl.BlockSpec((8, tr, n), lambda i: (0, i, 0))], out_specs=_rows(tr, n),
        out_shape=jax.ShapeDtypeStruct((K, n), _ACT),
        compiler_params=_cparams("parallel"),
    )(st)


def _all_reduce_small(p):
    R = p.shape[0]

    def body(p_ref, o_ref, stage, send_sems, recv_sems):
        x, y, c = _place()
        me = 4 * x + 2 * y + c
        stage[me] = p_ref[...]
        sent = []
        for r in range(1, 8):
            to, _ = _peer(x, y, c, r)
            cp = _remote(p_ref, stage.at[me], send_sems.at[r - 1], recv_sems.at[r - 1], to)
            cp.start()
            sent.append(cp)
        for r in range(1, 8):
            _, src_dev = _peer(x, y, c, r)
            _remote(p_ref, stage.at[src_dev], send_sems.at[r - 1], recv_sems.at[r - 1], (x, y, c)).wait_recv()
        acc = stage[0]
        for d in range(1, 8):
            acc = acc + stage[d]
        o_ref[...] = acc
        for cp in sent:
            cp.wait_send()

    vm = pl.BlockSpec(memory_space=pltpu.VMEM)
    return pl.pallas_call(
        body, name="all_reduce_small",
        in_specs=[vm], out_specs=vm,
        out_shape=jax.ShapeDtypeStruct((R, _LANES), _F32),
        scratch_shapes=[pltpu.VMEM((8, R, _LANES), _F32), pltpu.SemaphoreType.DMA((7,)), pltpu.SemaphoreType.DMA((7,))],
    )(p)


def _adamw_update(gv, w_ref, m_ref, v_ref, gf_ref, d_ref, nm_ref, nv_ref):
    nm = _B1 * m_ref[...] + (1.0 - _B1) * gv
    nv = _B2 * v_ref[...] + (1.0 - _B2) * (gv * gv)
    m_hat = nm / (1.0 - _B1 ** _STEP)
    v_hat = nv / (1.0 - _B2 ** _STEP)
    gf_ref[...] = gv
    d_ref[...] = -_LR * (m_hat / (jnp.sqrt(v_hat) + _EPS) + _WD * w_ref[...])
    nm_ref[...] = nm
    nv_ref[...] = nv


def _adamw_layers(w, g_layers, m, v, name):
    _, K, n = w.shape
    tr = next(t for t in (256, 128, 64, 32, 16) if K % t == 0)

    def kern(w_ref, g0_ref, g1_ref, m_ref, v_ref, *out_refs):
        first = pl.program_id(0) == 0
        gv = jnp.where(first, g0_ref[...].astype(_F32), g1_ref[...].astype(_F32))
        _adamw_update(gv, w_ref, m_ref, v_ref, *out_refs)

    stacked = pl.BlockSpec((None, tr, n), lambda l, i: (l, i, 0))
    layer = pl.BlockSpec((tr, n), lambda l, i: (i, 0))
    return tuple(pl.pallas_call(
        kern, name=name, grid=(2, K // tr),
        in_specs=[stacked, layer, layer, stacked, stacked], out_specs=[stacked] * 4,
        out_shape=[jax.ShapeDtypeStruct(w.shape, _F32)] * 4,
        compiler_params=_cparams("parallel", "parallel"),
    )(w, g_layers[0], g_layers[1], m, v))


def _adamw(w, g, m, v, name):
    shape = w.shape
    w2, g2, m2, v2 = (a.reshape(-1, shape[-1]) for a in (w, g, m, v))
    R, C = w2.shape
    tr = next((t for t in (256, 128, 64, 32, 16) if R % t == 0), R)

    def kern(w_ref, g_ref, m_ref, v_ref, *out_refs):
        _adamw_update(g_ref[...].astype(_F32), w_ref, m_ref, v_ref, *out_refs)

    outs = pl.pallas_call(
        kern, name=name, grid=(R // tr,),
        in_specs=[_rows(tr, C)] * 4, out_specs=[_rows(tr, C)] * 4,
        out_shape=[jax.ShapeDtypeStruct((R, C), _F32)] * 4,
        compiler_params=_cparams("parallel"),
    )(w2, g2, m2, v2)
    return tuple(o.reshape(shape) for o in outs)


def _pack_small(gss, sq):
    parts = [gss[l][n].reshape(-1) for n in _SMALL for l in range(len(gss))] + [jnp.sum(sq).reshape(1)]
    flat = jnp.concatenate(parts)
    rows = -(-flat.shape[0] // (8 * _LANES)) * 8
    return jnp.pad(flat, (0, rows * _LANES - flat.shape[0])).reshape(rows, _LANES)


def _unpack_small(total, shapes):
    flat = total.reshape(-1)
    out, off = {}, 0
    for n in _SMALL:
        layers = []
        for _ in range(shapes[n][0]):
            size = 1
            for s in shapes[n][1:]:
                size *= s
            layers.append(flat[off:off + size].reshape(shapes[n][1:]))
            off += size
        out[n] = jnp.stack(layers)
    return out, flat[off]


_GATHER = {
    "band_fwd_0": [(0, "w_proj_a"), (0, "w_proj_b"), (0, "w_out"), (0, "w_ffn_out")],
    "stick_fwd_0": [(0, "w_ffn_in"), (1, "w_proj_a"), (1, "w_proj_b"), (1, "w_out")],
    "mix_fwd_0": [(1, "w_ffn_out")],
    "ffn_fwd_0": [(1, "w_in"), (1, "w_ffn_in")],
}
_SCATTER = {
    "band_bwd_1": [(1, "w_ffn_in"), (1, "w_ffn_out")],
    "stick_bwd_1": [(1, "w_proj_a"), (1, "w_proj_b"), (1, "w_out")],
    "band_bwd_0": [(1, "w_in"), (0, "w_ffn_in")],
    "stick_bwd_0": [(0, "w_ffn_out"), (0, "w_proj_a"), (0, "w_proj_b"), (0, "w_out")],
    "in_proj_bwd_0": [(0, "w_in")],
}
_SHARE = {"stick_bwd_1": "band_bwd_1", "band_bwd_0": "stick_bwd_1", "stick_bwd_0": "band_bwd_0", "grad_w_in_0": "stick_bwd_0"}


def _owner(key):
    del key
    return 1


def kernel(x, w_in, b_gate, rel_bias, w_proj_a, w_proj_b, w_out, ln1_g, ln1_b, w_ffn_in, w_ffn_out, ln2_g, ln2_b, loss_target, m_w_in, m_b_gate, m_rel_bias, m_w_proj_a, m_w_proj_b, m_w_out, m_ln1_g, m_ln1_b, m_w_ffn_in, m_w_ffn_out, m_ln2_g, m_ln2_b, v_w_in, v_b_gate, v_rel_bias, v_w_proj_a, v_w_proj_b, v_w_out, v_ln1_g, v_ln1_b, v_w_ffn_in, v_w_ffn_out, v_ln2_g, v_ln2_b):
    names = ("w_in", "b_gate", "rel_bias", "w_proj_a", "w_proj_b", "w_out", "ln1_g", "ln1_b", "w_ffn_in", "w_ffn_out", "ln2_g", "ln2_b")
    w = dict(zip(names, (w_in, b_gate, rel_bias, w_proj_a, w_proj_b, w_out, ln1_g, ln1_b, w_ffn_in, w_ffn_out, ln2_g, ln2_b)))
    m = dict(zip(names, (m_w_in, m_b_gate, m_rel_bias, m_w_proj_a, m_w_proj_b, m_w_out, m_ln1_g, m_ln1_b, m_w_ffn_in, m_w_ffn_out, m_ln2_g, m_ln2_b)))
    v = dict(zip(names, (v_w_in, v_b_gate, v_rel_bias, v_w_proj_a, v_w_proj_b, v_w_out, v_ln1_g, v_ln1_b, v_w_ffn_in, v_w_ffn_out, v_ln2_g, v_ln2_b)))
    T, D = x.shape[-2], x.shape[-1]
    assert w_in.shape[0] == 2, "the exchange schedule below is written for two layers"

    mine = [{n: w[n][l].astype(_MXU) for n in _DENSE} for l in range(2)]
    W = [dict(), dict()]
    gws = [dict(), dict()]
    slots, final = {}, {}

    def gather(keys):
        sizes = [mine[l][n].size for l, n in keys]
        passed, fractions = 0, []
        for s in sizes:
            passed += s
            fractions.append(0.15 + 0.6 * passed / sum(sizes))

        def done(outs):
            for (l, n), o in zip(keys, outs):
                W[l][n] = o
        return _gather_plan([mine[l][n] for l, n in keys], fractions), done

    def scatter(keys):
        comm = _scatter_plan([gws[l][n] for l, n in keys], [_owner(key) for key in keys])
        return comm, lambda outs: slots.update(zip(keys, outs))

    def share(keys):
        reduced = [_sum_slots(slots[key], f"sum_grad_{key[1]}_{key[0]}") for key in keys]
        comm = _share_plan(reduced, [_owner(key) for key in keys])
        return comm, lambda outs: final.update(zip(keys, outs))

    def both(first, second):
        (ca, da), (cb, db) = first, second
        na = len(ca.out_shapes)
        return _join(ca, cb), lambda outs: (da(outs[:na]), db(outs[na:]))

    plans = {key: functools.partial(gather, keys) for key, keys in _GATHER.items()}
    for key, keys in _SCATTER.items():
        plans[key] = functools.partial(scatter, keys)
    for key, scattered_under in _SHARE.items():
        handed = functools.partial(share, _SCATTER[scattered_under])
        carried = plans.get(key)
        plans[key] = handed if carried is None else (lambda carried=carried, handed=handed: both(carried(), handed()))
    small = {n: w[n] for n in _SMALL}
    sq, dx, _, gss = _local_step(x.reshape(T, D), loss_target.reshape(T, D), W, small,
                                  _Plans(plans, {0: mine[0]["w_in"]}), gws)

    comm, done = share(_SCATTER["in_proj_bwd_0"])
    done(_comm_only(comm, "share_last"))
    total = _all_reduce_small(_pack_small(gss, sq))
    small_grads, sq_all = _unpack_small(total, {n: w[n].shape for n in _SMALL})
    loss = 0.5 * sq_all / D

    grad, delta, new_m, new_v = {}, {}, {}, {}
    for n in names:
        if n in _DENSE:
            updated = _adamw_layers(w[n], [final[(l, n)] for l in range(2)], m[n], v[n], f"adamw_{n}")
        else:
            updated = _adamw(w[n], small_grads[n], m[n], v[n], f"adamw_{n}")
        grad[n], delta[n], new_m[n], new_v[n] = updated
    return (loss, dx.reshape(x.shape), *[grad[n] for n in names], *[delta[n] for n in names],
            *[new_m[n] for n in names], *[new_v[n] for n in names])
```

```python
import functools

import jax
import jax.numpy as jnp
from jax import lax
from jax.experimental import pallas as pl
from jax.experimental.pallas import tpu as pltpu

_MXU = jnp.bfloat16
_ACT = jnp.bfloat16
_F32 = jnp.float32

_HEAD = 64
_CHUNK = 64
_LANES = 128
_TQ = 128
_BAND_TILES = 5
_BIAS_TILES = 9
_REL_CLIP = 256
_LN_EPS = 1e-5
_MASKED = -1e30
_EXP_ZERO_BELOW = -87.34
_SB_WINDOW = 2
_SB_SUBTILES = 4
_BAND_SUBTILES = 8
_VMEM_LIMIT = 56 * 1024 * 1024
_GRAD_ACC_BYTES = 12 * 1024 * 1024

_LR, _B1, _B2, _EPS, _WD, _STEP = 0.001, 0.9, 0.999, 1e-08, 0.01, 10

_MESH = pl.DeviceIdType.MESH


def _dot(a, b):
    return jnp.dot(a, b, preferred_element_type=_F32)


def _dot_nt(a, b):
    return lax.dot_general(a, b, (((1,), (1,)), ((), ())), preferred_element_type=_F32)


def _dot_tn(a, b):
    return lax.dot_general(a, b, (((0,), (0,)), ((), ())), preferred_element_type=_F32)


def _cparams(*sem):
    return pltpu.CompilerParams(dimension_semantics=sem, vmem_limit_bytes=_VMEM_LIMIT)


def _rows(t, c):
    return pl.BlockSpec((t, c), lambda i: (i, 0))


def _whole(shape):
    return pl.BlockSpec(shape, lambda i: tuple(0 for _ in shape))


_ANY = pl.BlockSpec(memory_space=pl.ANY)


def _load_cols(w_hbm, w_vmem, sem):
    n = w_hbm.shape[-1]
    cps = [pltpu.make_async_copy(w_hbm.at[k], w_vmem.at[:, pl.ds(k * n, n)], sem.at[k]) for k in range(4)]
    for cp in cps:
        cp.start()
    for cp in cps:
        cp.wait()


def _load_rows(w_hbm, w_vmem, sem):
    r = w_hbm.shape[-2]
    cps = [pltpu.make_async_copy(w_hbm.at[k], w_vmem.at[pl.ds(k * r, r), :], sem.at[k]) for k in range(4)]
    for cp in cps:
        cp.start()
    for cp in cps:
        cp.wait()


def _ln_stats(u):
    mu = jnp.mean(u, axis=-1, keepdims=True)
    xc = u - mu
    var = jnp.mean(xc * xc, axis=-1, keepdims=True)
    rstd = lax.rsqrt(var + _LN_EPS)
    return xc * rstd, rstd


def _ln_bwd(u, dy, gamma):
    xhat, rstd = _ln_stats(u)
    dxh = dy * gamma
    m1 = jnp.mean(dxh, axis=-1, keepdims=True)
    m2 = jnp.mean(dxh * xhat, axis=-1, keepdims=True)
    du = rstd * (dxh - m1 - xhat * m2)
    return du, jnp.sum(dy * xhat, axis=0, keepdims=True), jnp.sum(dy, axis=0, keepdims=True), xhat


def _divisor_tile(n, cap):
    best = None
    for t in range(_LANES, min(n, cap) + 1, _LANES):
        if n % t == 0:
            best = t
    return best or n


class _Comm:
    def __init__(self, inputs, out_shapes, sems, run, aliases=None):
        self.inputs, self.out_shapes, self.sems, self.run = list(inputs), list(out_shapes), list(sems), run
        self.aliases = aliases or {}


def _call(kern, comm, *, name, grid, in_specs, out_specs, out_shape, scratch_shapes, args, semantics):
    in_specs, out_specs, out_shape, scratch_shapes = list(in_specs), list(out_specs), list(out_shape), list(scratch_shapes)
    if comm is None:
        outs = pl.pallas_call(kern, name=name, grid=grid, in_specs=in_specs, out_specs=out_specs, out_shape=out_shape,
                              scratch_shapes=scratch_shapes, compiler_params=_cparams(*semantics))(*args)
        return list(outs), []
    n_in, n_out, n_scr = len(in_specs), len(out_specs), len(scratch_shapes)
    ci, co = len(comm.inputs), len(comm.out_shapes)
    nsteps = functools.reduce(lambda a, b: a * b, grid, 1)

    def fused(*refs):
        a, b = n_in, n_in + ci
        c, d = b + n_out, b + n_out + co
        e = d + n_scr
        step = pl.program_id(0)
        for ax in range(1, len(grid)):
            step = step * grid[ax] + pl.program_id(ax)
        comm.run(step, nsteps, refs[a:b], refs[c:d], refs[e:])
        kern(*refs[:a], *refs[b:c], *refs[d:e])

    outs = pl.pallas_call(
        fused, name=name, grid=grid, in_specs=in_specs + [_ANY] * ci, out_specs=out_specs + [_ANY] * co,
        out_shape=out_shape + comm.out_shapes, scratch_shapes=scratch_shapes + comm.sems,
        input_output_aliases={n_in + i: n_out + o for i, o in comm.aliases.items()},
        compiler_params=_cparams(*("arbitrary" for _ in grid)))(*args, *comm.inputs)
    return list(outs[:n_out]), list(outs[n_out:])


def _comm_only(comm, name):
    def body(*refs):
        ci, co = len(comm.inputs), len(comm.out_shapes)
        comm.run(0, 1, refs[:ci], refs[ci:ci + co], refs[ci + co:])

    outs = pl.pallas_call(body, name=name, in_specs=[_ANY] * len(comm.inputs), out_specs=[_ANY] * len(comm.out_shapes),
                          out_shape=comm.out_shapes, scratch_shapes=comm.sems,
                          input_output_aliases=dict(comm.aliases))(*comm.inputs)
    return list(outs)


def _in_proj(x, w_in, layer, comm=None):
    T, D = x.shape
    N = 4 * w_in.shape[-1]
    tm = 512

    def kern(x_ref, w_hbm, h_ref, xb_ref, w_v, sem):
        @pl.when(pl.program_id(0) == 0)
        def _():
            _load_cols(w_hbm, w_v, sem)

        xb = x_ref[...].astype(_MXU)
        h_ref[...] = _dot(xb, w_v[...]).astype(h_ref.dtype)
        xb_ref[...] = xb.astype(xb_ref.dtype)

    return _call(
        kern, comm, name=f"in_proj_{layer}", grid=(T // tm,),
        in_specs=[_rows(tm, D), _ANY],
        out_specs=[_rows(tm, N), _rows(tm, D)],
        out_shape=[jax.ShapeDtypeStruct((T, N), _ACT), jax.ShapeDtypeStruct((T, D), _ACT)],
        scratch_shapes=[pltpu.VMEM((D, N), w_in.dtype), pltpu.SemaphoreType.DMA((4,))],
        args=(x, w_in), semantics=("arbitrary",))


def _in_proj_gathering(x, block, layer):
    T, D = x.shape
    n = block.shape[1]
    tm = min(T, 1024)
    nrows = T // tm
    pass_steps = [int(f * nrows) for f in (0.6, 1.0, 1.7)]
    px, py, _ = _place()
    order = jnp.stack([2 * px + py, 2 * (1 - px) + py, 2 * px + (1 - py), 2 * (1 - px) + (1 - py)]).astype(jnp.int32)

    def kern(order_ref, x_ref, blk_hbm, h_ref, xb_ref, w_hbm, w_v, send_sems, recv_sems, loc_sem, load_sem):
        del order_ref
        step = pl.program_id(0) * nrows + pl.program_id(1)
        x_, y_, c = _place()
        k = 2 * x_ + y_
        me, sibling = (x_, y_, c), (x_, y_, 1 - c)
        chips = [(1 - x_, y_), (x_, 1 - y_), (1 - x_, 1 - y_)]
        chip_k = [2 * cx + cy for cx, cy in chips]

        def ici(s, owner_k, to, src=None):
            dst = _half(w_hbm.at[owner_k], c)
            return _remote(dst if src is None else src, dst, send_sems.at[s], recv_sems.at[s], to)

        def passed(s, hc, to):
            blk = _half(w_hbm.at[chip_k[s]], hc)
            return _remote(blk, blk, send_sems.at[3 + s], recv_sems.at[3 + s], to)

        local = pltpu.make_async_copy(blk_hbm, w_hbm.at[k], loc_sem.at[0])

        def load(src):
            cp = pltpu.make_async_copy(src, w_v, load_sem.at[0])
            cp.start()
            cp.wait()

        @pl.when(step == 0)
        def _():
            for s, chip in enumerate(chips):
                ici(s, k, (*chip, c), src=_half(blk_hbm, c)).start()
            local.start()
            load(blk_hbm)

        for s in range(3):
            @pl.when(step == pass_steps[s])
            def _():
                ici(s, chip_k[s], me).wait_recv()
                passed(s, c, sibling).start()

            @pl.when(step == (s + 1) * nrows)
            def _():
                passed(s, 1 - c, me).wait_recv()
                load(w_hbm.at[chip_k[s]])

        xb = x_ref[...].astype(_MXU)
        h_ref[...] = _dot(xb, w_v[...]).astype(h_ref.dtype)

        @pl.when(pl.program_id(0) == 0)
        def _():
            xb_ref[...] = xb.astype(xb_ref.dtype)

        @pl.when(step == 4 * nrows - 1)
        def _():
            for s, chip in enumerate(chips):
                ici(s, k, (*chip, c), src=_half(blk_hbm, c)).wait_send()
                passed(s, c, sibling).wait_send()
            local.wait()

    assert all(pass_steps[s] <= (s + 1) * nrows for s in range(3))
    h, xb, w_in = pl.pallas_call(
        kern, name=f"in_proj_{layer}",
        grid_spec=pltpu.PrefetchScalarGridSpec(
            num_scalar_prefetch=1, grid=(4, nrows),
            in_specs=[pl.BlockSpec((tm, D), lambda j, i, o: (i, 0)), _ANY],
            out_specs=[pl.BlockSpec((tm, n), lambda j, i, o: (i, o[j])),
                       pl.BlockSpec((tm, D), lambda j, i, o: (jnp.where(j == 0, i, nrows - 1), 0)), _ANY],
            scratch_shapes=[pltpu.VMEM((D, n), block.dtype), pltpu.SemaphoreType.DMA((6,)),
                            pltpu.SemaphoreType.DMA((6,)), pltpu.SemaphoreType.DMA((1,)), pltpu.SemaphoreType.DMA((1,))]),
        out_shape=[jax.ShapeDtypeStruct((T, 4 * n), _ACT), jax.ShapeDtypeStruct((T, D), _ACT),
                   jax.ShapeDtypeStruct((4,) + block.shape, block.dtype)],
        compiler_params=_cparams("arbitrary", "arbitrary"))(order, x, block)
    return h, xb, w_in


def _gate_specs(h, tm, D):
    first = (h.shape[1] - 2 * D) // D
    assert first * D + 2 * D == h.shape[1]
    return [pl.BlockSpec((tm, D), lambda i: (i, first)), pl.BlockSpec((tm, D), lambda i: (i, first + 1))]


def _mix_fwd(oa, ob, h, x, wpa, wpb, wo, bg, gamma, beta, alpha, layer, comm=None):
    T, D = x.shape
    WA, WB = oa.shape[1], ob.shape[1]
    tm = 512

    def kern(oa_ref, ob_ref, hga_ref, hgb_ref, x_ref, bg_ref, g_ref, b_ref, wpa_h, wpb_h, wo_h,
             x1_ref, u1_ref, pre_ref, wpa_v, wpb_v, wo_v, sa, sb, so):
        @pl.when(pl.program_id(0) == 0)
        def _():
            _load_cols(wpa_h, wpa_v, sa)
            _load_cols(wpb_h, wpb_v, sb)
            _load_rows(wo_h, wo_v, so)

        ya = _dot(oa_ref[...].astype(_MXU), wpa_v[...])
        yb = _dot(ob_ref[...].astype(_MXU), wpb_v[...])
        bgv = bg_ref[...]
        ga = jax.nn.sigmoid(hga_ref[...].astype(_F32) + bgv[:, :D])
        gb = jax.nn.sigmoid(hgb_ref[...].astype(_F32) + bgv[:, D:])
        pre = ga * ya + gb * yb
        mix = _dot(pre.astype(_MXU), wo_v[...])
        u = alpha * x_ref[...] + mix
        xhat, _ = _ln_stats(u)
        x1_ref[...] = xhat * g_ref[...] + b_ref[...]
        u1_ref[...] = u
        pre_ref[...] = pre.astype(pre_ref.dtype)

    return _call(
        kern, comm, name=f"mix_fwd_{layer}", grid=(T // tm,),
        in_specs=[_rows(tm, WA), _rows(tm, WB), *_gate_specs(h, tm, D), _rows(tm, D),
                  _whole((1, 2 * D)), _whole((1, D)), _whole((1, D)), _ANY, _ANY, _ANY],
        out_specs=[_rows(tm, D)] * 3,
        out_shape=[jax.ShapeDtypeStruct((T, D), _F32), jax.ShapeDtypeStruct((T, D), _F32),
                   jax.ShapeDtypeStruct((T, D), _ACT)],
        scratch_shapes=[pltpu.VMEM((WA, D), wpa.dtype), pltpu.VMEM((WB, D), wpb.dtype), pltpu.VMEM((D, D), wo.dtype),
                        pltpu.SemaphoreType.DMA((4,)), pltpu.SemaphoreType.DMA((4,)), pltpu.SemaphoreType.DMA((4,))],
        args=(oa, ob, h, h, x, bg, gamma, beta, wpa, wpb, wo), semantics=("arbitrary",))


def _ffn_fwd(x1, wfi, wfo, gamma, beta, alpha, layer, comm=None):
    T, D = x1.shape
    F2 = 4 * wfi.shape[-1]
    F = F2 // 2
    tm = 512
    fc = F // 2

    def kern(x_ref, g_ref, b_ref, wi_h, wo_h, x2_ref, u2_ref, act_ref, gu_ref, xb_ref, wi_v, wo_v, si, so):
        @pl.when(pl.program_id(0) == 0)
        def _():
            _load_cols(wi_h, wi_v, si)
            _load_rows(wo_h, wo_v, so)

        x = x_ref[...]
        xb = x.astype(_MXU)
        xb_ref[...] = xb.astype(xb_ref.dtype)
        ffn = jnp.zeros((tm, D), _F32)
        for c in range(2):
            g = _dot(xb, wi_v[:, c * fc:(c + 1) * fc])
            u = _dot(xb, wi_v[:, F + c * fc:F + (c + 1) * fc])
            act = g * jax.nn.sigmoid(g) * u
            ab = act.astype(_MXU)
            ffn = ffn + _dot(ab, wo_v[c * fc:(c + 1) * fc, :])
            act_ref[:, c * fc:(c + 1) * fc] = ab.astype(act_ref.dtype)
            gu_ref[:, c * fc:(c + 1) * fc] = g.astype(gu_ref.dtype)
            gu_ref[:, F + c * fc:F + (c + 1) * fc] = u.astype(gu_ref.dtype)
        uu = alpha * x + ffn
        xhat, _ = _ln_stats(uu)
        x2_ref[...] = xhat * g_ref[...] + b_ref[...]
        u2_ref[...] = uu

    return _call(
        kern, comm, name=f"ffn_fwd_{layer}", grid=(T // tm,),
        in_specs=[_rows(tm, D), _whole((1, D)), _whole((1, D)), _ANY, _ANY],
        out_specs=[_rows(tm, D), _rows(tm, D), _rows(tm, F), _rows(tm, F2), _rows(tm, D)],
        out_shape=[jax.ShapeDtypeStruct((T, D), _F32), jax.ShapeDtypeStruct((T, D), _F32),
                   jax.ShapeDtypeStruct((T, F), _ACT), jax.ShapeDtypeStruct((T, F2), _ACT),
                   jax.ShapeDtypeStruct((T, D), _ACT)],
        scratch_shapes=[pltpu.VMEM((D, F2), wfi.dtype), pltpu.VMEM((F, D), wfo.dtype),
                        pltpu.SemaphoreType.DMA((4,)), pltpu.SemaphoreType.DMA((4,))],
        args=(x1, gamma, beta, wfi, wfo), semantics=("arbitrary",))


def _ffn_bwd(u2, dy_or_target, gu, gamma, beta, wfi, wfo, alpha, layer, last):
    T, D = u2.shape
    F2 = gu.shape[1]
    F = F2 // 2
    tm = 256
    fc = F // 2

    def kern(u_ref, dy_ref, gu_ref, g_ref, b_ref, wi_h, wo_h, dx_ref, dub_ref, dgu_ref, st_ref, wi_v, wo_v, si, so):
        @pl.when(pl.program_id(0) == 0)
        def _():
            _load_cols(wi_h, wi_v, si)
            _load_rows(wo_h, wo_v, so)
            st_ref[...] = jnp.zeros_like(st_ref)

        gam = g_ref[...]
        u = u_ref[...]
        if last:
            xhat0, _ = _ln_stats(u)
            err = xhat0 * gam + b_ref[...] - dy_ref[...]
            dy = err * (1.0 / D)
            st_ref[2:3, :] += jnp.sum(err * err, axis=0, keepdims=True)
        else:
            dy = dy_ref[...]
        du, dgam, dbet, _ = _ln_bwd(u, dy, gam)
        st_ref[0:1, :] += dgam
        st_ref[1:2, :] += dbet
        dub = du.astype(_MXU)
        dub_ref[...] = dub.astype(dub_ref.dtype)
        dx = alpha * du
        for c in range(2):
            dact = _dot_nt(dub, wo_v[c * fc:(c + 1) * fc, :])
            g = gu_ref[:, c * fc:(c + 1) * fc].astype(_F32)
            uu = gu_ref[:, F + c * fc:F + (c + 1) * fc].astype(_F32)
            sg = jax.nn.sigmoid(g)
            dg = (dact * uu * (sg * (1.0 + g * (1.0 - sg)))).astype(_MXU)
            dup = (dact * (g * sg)).astype(_MXU)
            dgu_ref[:, c * fc:(c + 1) * fc] = dg.astype(dgu_ref.dtype)
            dgu_ref[:, F + c * fc:F + (c + 1) * fc] = dup.astype(dgu_ref.dtype)
            dx = dx + _dot_nt(dg, wi_v[:, c * fc:(c + 1) * fc]) + _dot_nt(dup, wi_v[:, F + c * fc:F + (c + 1) * fc])
        dx_ref[...] = dx

    return pl.pallas_call(
        kern, name=f"ffn_bwd_{layer}", grid=(T // tm,),
        in_specs=[_rows(tm, D), _rows(tm, D), _rows(tm, F2), _whole((1, D)), _whole((1, D)), _ANY, _ANY],
        out_specs=[_rows(tm, D), _rows(tm, D), _rows(tm, F2), _whole((8, D))],
        out_shape=[jax.ShapeDtypeStruct((T, D), _F32), jax.ShapeDtypeStruct((T, D), _ACT),
                   jax.ShapeDtypeStruct((T, F2), _ACT), jax.ShapeDtypeStruct((8, D), _F32)],
        scratch_shapes=[pltpu.VMEM((D, F2), wfi.dtype), pltpu.VMEM((F, D), wfo.dtype),
                        pltpu.SemaphoreType.DMA((4,)), pltpu.SemaphoreType.DMA((4,))],
        compiler_params=_cparams("arbitrary"),
    )(u2, dy_or_target, gu, gamma, beta, wfi, wfo)


def _residual_nt(res, res_scale, pieces, w, name, comm=None):
    T, K = res.shape
    widths = [p.shape[1] for p in pieces]
    N = sum(widths)
    tm = 512

    def kern(r_ref, *refs):
        d_refs, (w_hbm, o_ref, w_v, sem) = refs[:len(pieces)], refs[len(pieces):]

        @pl.when(pl.program_id(0) == 0)
        def _():
            _load_cols(w_hbm, w_v, sem)

        acc = res_scale * r_ref[...]
        off = 0
        for d_ref, width in zip(d_refs, widths):
            acc = acc + _dot_nt(d_ref[...].astype(_MXU), w_v[:, off:off + width])
            off += width
        o_ref[...] = acc

    outs, extra = _call(
        kern, comm, name=name, grid=(T // tm,),
        in_specs=[_rows(tm, K)] + [_rows(tm, width) for width in widths] + [_ANY], out_specs=[_rows(tm, K)],
        out_shape=[jax.ShapeDtypeStruct((T, K), _F32)],
        scratch_shapes=[pltpu.VMEM((K, N), w.dtype), pltpu.SemaphoreType.DMA((4,))],
        args=(res, *pieces, w), semantics=("arbitrary",))
    return outs[0], extra


def _grad_w_pieces(a, pieces, name, comm=None):
    T, M = a.shape
    widths = [p.shape[1] for p in pieces]
    offsets = [sum(widths[:p]) for p in range(len(pieces))]
    N = sum(widths)
    tk = 1024 if T % 1024 == 0 else 512
    nk = T // tk

    def fits(ow):
        inside = lambda off, width: width < ow and off // ow == (off + width - 1) // ow
        whole = lambda off, width: off % ow == 0 and width % ow == 0
        return (N % ow == 0 and ow % _LANES == 0 and M * ow * 4 <= _GRAD_ACC_BYTES
                and all(inside(o, w) or whole(o, w) for o, w in zip(offsets, widths)))

    ow = next(c for c in (N // 2, 1024, 512, 256, _LANES) if fits(c))

    def kern(a_ref, *refs):
        b_refs, (o_ref, acc) = refs[:len(pieces)], refs[len(pieces):]
        j, k = pl.program_id(0), pl.program_id(1)

        @pl.when(k == 0)
        def _():
            acc[...] = jnp.zeros_like(acc)

        for b_ref, off, width in zip(b_refs, offsets, widths):
            if width < ow:
                @pl.when(j == off // ow)
                def _():
                    acc[:, off % ow:off % ow + width] += _dot_tn(a_ref[...].astype(_MXU), b_ref[...].astype(_MXU))
            else:
                @pl.when(jnp.logical_and(j >= off // ow, j < (off + width) // ow))
                def _():
                    acc[...] += _dot_tn(a_ref[...].astype(_MXU), b_ref[...].astype(_MXU))

        @pl.when(k == nk - 1)
        def _():
            o_ref[...] = acc[...].astype(o_ref.dtype)

    def piece_spec(off, width):
        first, blocks = off // ow, max(width // ow, 1)

        def index(j, k):
            mine = jnp.logical_and(j >= first, j < first + blocks)
            return jnp.where(mine, k, 0), jnp.where(mine, j - first, 0)
        return pl.BlockSpec((tk, min(width, ow)), index)

    outs, extra = _call(
        kern, comm, name=name, grid=(N // ow, nk),
        in_specs=[pl.BlockSpec((tk, M), lambda j, k: (k, 0))] + [piece_spec(o, w) for o, w in zip(offsets, widths)],
        out_specs=[pl.BlockSpec((M, ow), lambda j, k: (0, j))],
        out_shape=[jax.ShapeDtypeStruct((M, N), _ACT)], scratch_shapes=[pltpu.VMEM((M, ow), _F32)],
        args=(a, *pieces), semantics=("parallel", "arbitrary"))
    return outs[0], extra


def _gcd(a, b):
    while b:
        a, b = b, a % b
    return a


def _mix_bwd(u1, dx1, oa, ob, h, wpa, wpb, wo, bg, gamma, layer):
    T, D = u1.shape
    WA, WB = wpa.shape[-2], wpb.shape[-2]
    tm = 512

    def kern(u_ref, dx_ref, oa_ref, ob_ref, hga_ref, hgb_ref, bg_ref, g_ref, wpa_h, wpb_h, wo_h,
             du_ref, dub_ref, dya_ref, dyb_ref, dhg_ref, doa_ref, dob_ref, st_ref,
             wpa_v, wpb_v, wo_v, sa, sb, so):
        @pl.when(pl.program_id(0) == 0)
        def _():
            _load_cols(wpa_h, wpa_v, sa)
            _load_cols(wpb_h, wpb_v, sb)
            _load_rows(wo_h, wo_v, so)
            st_ref[...] = jnp.zeros_like(st_ref)

        du, dgam, dbet, _ = _ln_bwd(u_ref[...], dx_ref[...], g_ref[...])
        st_ref[1:2, :D] += dgam
        st_ref[1:2, D:] += dbet
        du_ref[...] = du
        dub = du.astype(_MXU)
        dub_ref[...] = dub.astype(dub_ref.dtype)
        dpre = _dot_nt(dub, wo_v[...])
        bgv = bg_ref[...]
        ga = jax.nn.sigmoid(hga_ref[...].astype(_F32) + bgv[:, :D])
        gb = jax.nn.sigmoid(hgb_ref[...].astype(_F32) + bgv[:, D:])
        dya = (dpre * ga).astype(_MXU)
        dyb = (dpre * gb).astype(_MXU)
        dsa = dpre * _dot(oa_ref[...].astype(_MXU), wpa_v[...]) * (ga * (1.0 - ga))
        dsb = dpre * _dot(ob_ref[...].astype(_MXU), wpb_v[...]) * (gb * (1.0 - gb))
        st_ref[0:1, :D] += jnp.sum(dsa, axis=0, keepdims=True)
        st_ref[0:1, D:] += jnp.sum(dsb, axis=0, keepdims=True)
        dya_ref[...] = dya.astype(dya_ref.dtype)
        dyb_ref[...] = dyb.astype(dyb_ref.dtype)
        dhg_ref[:, :D] = dsa.astype(dhg_ref.dtype)
        dhg_ref[:, D:] = dsb.astype(dhg_ref.dtype)
        doa_ref[...] = _dot_nt(dya, wpa_v[...]).astype(doa_ref.dtype)
        dob_ref[...] = _dot_nt(dyb, wpb_v[...]).astype(dob_ref.dtype)

    return pl.pallas_call(
        kern, name=f"mix_bwd_{layer}", grid=(T // tm,),
        in_specs=[_rows(tm, D), _rows(tm, D), _rows(tm, WA), _rows(tm, WB), *_gate_specs(h, tm, D), _whole((1, 2 * D)),
                  _whole((1, D)), _ANY, _ANY, _ANY],
        out_specs=[_rows(tm, D)] * 4 + [_rows(tm, 2 * D), _rows(tm, WA), _rows(tm, WB), _whole((8, 2 * D))],
        out_shape=[jax.ShapeDtypeStruct((T, D), _F32)] + [jax.ShapeDtypeStruct((T, D), _ACT)] * 3
        + [jax.ShapeDtypeStruct((T, 2 * D), _ACT), jax.ShapeDtypeStruct((T, WA), _ACT),
           jax.ShapeDtypeStruct((T, WB), _ACT), jax.ShapeDtypeStruct((8, 2 * D), _F32)],
        scratch_shapes=[pltpu.VMEM((WA, D), wpa.dtype), pltpu.VMEM((WB, D), wpb.dtype), pltpu.VMEM((D, D), wo.dtype),
                        pltpu.SemaphoreType.DMA((4,)), pltpu.SemaphoreType.DMA((4,)), pltpu.SemaphoreType.DMA((4,))],
        compiler_params=_cparams("arbitrary"),
    )(u1, dx1, oa, ob, h, h, bg, gamma, wpa, wpb, wo)


def _grad_w(a, b, *, col_shards, name, comm=None):
    T, M = a.shape
    N = b.shape[1]
    tk = 1024 if T % 1024 == 0 else 512
    n = N // 4 if col_shards else N
    group = max(g for g in (1, 2, 4) if g == 1 or M * n * g * 4 <= _GRAD_ACC_BYTES)
    tn = n * group if col_shards else (N if M * N * 4 <= _GRAD_ACC_BYTES else _divisor_tile(N, _GRAD_ACC_BYTES // (4 * M)))
    nk = T // tk

    def kern(a_ref, b_ref, o_ref, acc):
        k = pl.program_id(1)

        @pl.when(k == 0)
        def _():
            acc[...] = jnp.zeros_like(acc)

        acc[...] += _dot_tn(a_ref[...].astype(_MXU), b_ref[...].astype(_MXU))

        @pl.when(k == nk - 1)
        def _():
            if col_shards:
                for s in range(group):
                    o_ref[s] = acc[:, s * n:(s + 1) * n].astype(o_ref.dtype)
            else:
                o_ref[...] = acc[...].astype(o_ref.dtype)

    if col_shards:
        out_spec = pl.BlockSpec((group, M, n), lambda j, k: (j, 0, 0))
        out_shape = jax.ShapeDtypeStruct((4, M, n), _ACT)
    else:
        out_spec = pl.BlockSpec((M, tn), lambda j, k: (0, j))
        out_shape = jax.ShapeDtypeStruct((M, N), _ACT)
    outs, extra = _call(
        kern, comm, name=name, grid=(N // tn, nk),
        in_specs=[pl.BlockSpec((tk, M), lambda j, k: (k, 0)), pl.BlockSpec((tk, tn), lambda j, k: (k, j))],
        out_specs=[out_spec], out_shape=[out_shape], scratch_shapes=[pltpu.VMEM((M, tn), _F32)],
        args=(a, b), semantics=("parallel", "arbitrary"))
    return outs[0], extra


def _bias_tiles(rel):
    H = rel.shape[0]
    span = _TQ * _BAND_TILES - 1
    edge = span - _REL_CLIP
    gvec = jnp.concatenate([jnp.broadcast_to(rel[:, :1], (H, edge)), rel, jnp.broadcast_to(rel[:, -1:], (H, edge))], axis=1)
    width = _BIAS_TILES * _TQ
    period = width + _TQ
    tiled = jnp.broadcast_to(jnp.pad(gvec[:, ::-1], ((0, 0), (0, 1)))[:, None, :], (H, _TQ, period))
    rows = tiled.reshape(H, _TQ * period)[:, :_TQ * (period - 1)].reshape(H, _TQ, period - 1)[:, :, _TQ - 1:]
    r = jnp.arange(_TQ)[:, None]
    u = jnp.arange(width)[None, :]
    d = 4 * _TQ + r - u
    rm = r % _CHUNK
    valid = (d >= rm - (_CHUNK - 1)) & (d <= rm + 8 * _CHUNK)
    tiles = jnp.where(valid[None], rows, _MASKED)
    return tiles.reshape(H // 2, 2 * _TQ, width)


def _strip_tiles(strip_ref, tiles_ref, to_strip=False):
    for ub in range(_BIAS_TILES):
        if to_strip:
            strip_ref[:, ub * _TQ:(ub + 1) * _TQ] = tiles_ref[ub]
        else:
            tiles_ref[ub] = strip_ref[:, ub * _TQ:(ub + 1) * _TQ]


def _fold_bias_grad(db):
    H = 2 * db.shape[0]
    width = _BIAS_TILES * _TQ
    period = width + _TQ
    x = jnp.pad(db.reshape(H, _TQ, width), ((0, 0), (0, 0), (_TQ - 1, 0)))
    skew = jnp.pad(x.reshape(H, _TQ * (period - 1)), ((0, 0), (0, _TQ))).reshape(H, _TQ, period)
    dg = skew.sum(axis=1)[:, :period - 1][:, ::-1]
    span = _TQ * _BAND_TILES - 1
    edge = span - _REL_CLIP
    mid = dg[:, edge:edge + 2 * _REL_CLIP + 1]
    lo = dg[:, :edge].sum(axis=1)
    hi = dg[:, edge + 2 * _REL_CLIP + 1:].sum(axis=1)
    return mid.at[:, 0].add(lo).at[:, -1].add(hi)


def _band_window(i):
    j0 = jnp.maximum(i - (_BAND_TILES - 1), 0)
    return j0, (_BAND_TILES - 1) - (i - j0)


def _head_masks():
    lane = lax.broadcasted_iota(jnp.int32, (1, _LANES), 1)
    return [(lane // _HEAD) == hh for hh in range(2)]


def _stack_heads(x, masks):
    return jnp.concatenate([jnp.where(m, x, jnp.zeros_like(x)) for m in masks], axis=0)


def _unstack_heads(y, masks):
    return jnp.where(masks[0], y[:_TQ], y[_TQ:])


def _scaled(q):
    return q * jnp.asarray(_HEAD ** -0.5, q.dtype)


def _band_probs(q2, k_ref, b_ref, j0, boff):
    s = []
    for j in range(_BAND_TILES):
        kj = k_ref[pl.ds(pl.multiple_of((j0 + j) * _TQ, _TQ), _TQ), :]
        s.append(_dot_nt(q2, kj) + b_ref[boff + j])
    m = jnp.max(functools.reduce(jnp.maximum, s), axis=-1, keepdims=True)
    p = [jnp.exp(x - m) for x in s]
    l = jnp.sum(functools.reduce(lambda a, b: a + b, p), axis=-1, keepdims=True)
    return p, 1.0 / l


def _qkv_specs(T, cb, npair, tq=_TQ):
    return [pl.BlockSpec((tq, _LANES), lambda h, i: (i, cb + h)),
            pl.BlockSpec((T, _LANES), lambda h, i: (0, cb + npair + h)),
            pl.BlockSpec((T, _LANES), lambda h, i: (0, cb + 2 * npair + h))]


def _attn_a_fwd(hq, bias, col0, width, layer, comm=None):
    T = hq.shape[0]
    npair = width // _LANES
    nsub = _BAND_SUBTILES
    tq = nsub * _TQ

    def kern(q_ref, k_ref, v_ref, b_ref, o_ref, b_tiles):
        @pl.when(pl.program_id(1) == 0)
        def _():
            _strip_tiles(b_ref, b_tiles)

        masks = _head_masks()
        q = _scaled(q_ref[...])
        for s in range(nsub):
            part = slice(s * _TQ, (s + 1) * _TQ)
            j0, boff = _band_window(nsub * pl.program_id(1) + s)
            p, inv = _band_probs(_stack_heads(q[part], masks), k_ref, b_tiles, j0, boff)
            o = jnp.zeros((2 * _TQ, _LANES), _F32)
            for j in range(_BAND_TILES):
                vj = v_ref[pl.ds(pl.multiple_of((j0 + j) * _TQ, _TQ), _TQ), :]
                o = o + _dot(p[j].astype(_MXU), vj)
            o_ref[part, :] = _unstack_heads(o * inv, masks).astype(o_ref.dtype)

    outs, extra = _call(
        kern, comm, name=f"band_attn_fwd_{layer}", grid=(npair, T // tq),
        in_specs=_qkv_specs(T, col0 // _LANES, npair, tq)
        + [pl.BlockSpec((None, 2 * _TQ, _BIAS_TILES * _TQ), lambda h, i: (h, 0, 0))],
        out_specs=[pl.BlockSpec((tq, _LANES), lambda h, i: (i, h))],
        out_shape=[jax.ShapeDtypeStruct((T, width), _ACT)],
        scratch_shapes=[pltpu.VMEM((_BIAS_TILES, 2 * _TQ, _TQ), _F32)],
        args=(hq, hq, hq, bias), semantics=("arbitrary", "arbitrary"))
    return outs[0], extra


def _attn_a_bwd(hq, bias, do, col0, width, layer, comm=None):
    T = hq.shape[0]
    npair = width // _LANES
    nsub = _BAND_SUBTILES
    tq = nsub * _TQ
    nq = T // tq
    scale = _HEAD ** -0.5

    def kern(q_ref, k_ref, v_ref, b_ref, do_ref, dq_ref, dk_ref, dv_ref, db_ref, dk_acc, dv_acc, b_tiles, db_acc):
        i = pl.program_id(1)

        @pl.when(i == 0)
        def _():
            _strip_tiles(b_ref, b_tiles)
            dk_acc[...] = jnp.zeros_like(dk_acc)
            dv_acc[...] = jnp.zeros_like(dv_acc)
            db_acc[...] = jnp.zeros_like(db_acc)

        masks = _head_masks()
        q = _scaled(q_ref[...])
        do_t = do_ref[...]
        for s in range(nsub):
            part = slice(s * _TQ, (s + 1) * _TQ)
            j0, boff = _band_window(nsub * i + s)
            q2 = _stack_heads(q[part], masks)
            do2 = _stack_heads(do_t[part], masks).astype(_MXU)
            p, inv = _band_probs(q2, k_ref, b_tiles, j0, boff)
            rows = [pl.ds(pl.multiple_of((j0 + j) * _TQ, _TQ), _TQ) for j in range(_BAND_TILES)]
            p = [x * inv for x in p]
            dp = [_dot_nt(do2, v_ref[rows[j], :]) for j in range(_BAND_TILES)]
            delta = jnp.sum(functools.reduce(lambda a, b: a + b, [p[j] * dp[j] for j in range(_BAND_TILES)]),
                            axis=-1, keepdims=True)
            dq = jnp.zeros((2 * _TQ, _LANES), _F32)
            for j in range(_BAND_TILES):
                ds = p[j] * (dp[j] - delta)
                db_acc[boff + j] += ds
                dsb = ds.astype(_MXU)
                dq = dq + _dot(dsb, k_ref[rows[j], :])
                dk_acc[rows[j], :] += _dot_tn(dsb, q2)
                dv_acc[rows[j], :] += _dot_tn(p[j].astype(_MXU), do2)
            dq_ref[part, :] = (_unstack_heads(dq, masks) * scale).astype(dq_ref.dtype)

        @pl.when(i == nq - 1)
        def _():
            dk_ref[...] = dk_acc[...].astype(dk_ref.dtype)
            dv_ref[...] = dv_acc[...].astype(dv_ref.dtype)
            _strip_tiles(db_ref, db_acc, to_strip=True)

    strip = pl.BlockSpec((None, 2 * _TQ, _BIAS_TILES * _TQ), lambda h, i: (h, 0, 0))
    tile = pl.BlockSpec((tq, _LANES), lambda h, i: (i, h))
    column = pl.BlockSpec((T, _LANES), lambda h, i: (0, h))
    outs, extra = _call(
        kern, comm, name=f"band_attn_bwd_{layer}", grid=(npair, nq),
        in_specs=_qkv_specs(T, col0 // _LANES, npair, tq) + [strip, tile],
        out_specs=[tile, column, column, strip],
        out_shape=[jax.ShapeDtypeStruct((T, width), _ACT)] * 3
        + [jax.ShapeDtypeStruct((npair, 2 * _TQ, _BIAS_TILES * _TQ), _F32)],
        scratch_shapes=[pltpu.VMEM((T, _LANES), _F32), pltpu.VMEM((T, _LANES), _F32),
                        pltpu.VMEM((_BIAS_TILES, 2 * _TQ, _TQ), _F32), pltpu.VMEM((_BIAS_TILES, 2 * _TQ, _TQ), _F32)],
        args=(hq, hq, hq, bias, do), semantics=("arbitrary", "arbitrary"))
    return outs, extra


def _suffix_matrix():
    r = lax.broadcasted_iota(jnp.int32, (_TQ, _TQ), 0)
    c = lax.broadcasted_iota(jnp.int32, (_TQ, _TQ), 1)
    r2 = lax.broadcasted_iota(jnp.int32, (2 * _TQ, _TQ), 0)
    c2 = lax.broadcasted_iota(jnp.int32, (2 * _TQ, _TQ), 1)
    return (r > c).astype(_MXU), c2 - (r2 & (_TQ - 1))


def _suffix_sums(xs, tri):
    n, k = xs[0].shape[0], len(xs)
    his = [x.astype(_MXU) for x in xs]
    los = [(x - h.astype(_F32)).astype(_MXU) for x, h in zip(xs, his)]
    y = _dot(jnp.concatenate(his + los, axis=0), tri)
    return [y[j * n:(j + 1) * n] + y[(k + j) * n:(k + j + 1) * n] for j in range(k)]


def _stick_tiles(tiles, rel, carry_l, tri):
    zs = [_dot_nt(qs, kj) for qs, kj, _, _ in tiles]
    Ls, masks = [], []
    for z, (_, _, jj, _) in zip(zs, tiles):
        nsp = -(jnp.maximum(z, 0.0) + jnp.log(1.0 + jnp.exp(-jnp.abs(z))))
        if isinstance(jj, int):
            mask = (rel < 0) if jj == 0 else None
        else:
            mask = rel < jnp.where(jj == 0, 0, _TQ)
        Ls.append(nsp if mask is None else jnp.where(mask, nsp, 0.0))
        masks.append(mask)
    carry_l = list(carry_l)
    ws = []
    for z, L, suffix, mask, (_, _, _, sub) in zip(zs, Ls, _suffix_sums(Ls, tri), masks, tiles):
        w = jnp.exp(z + L + suffix + carry_l[sub])
        ws.append(w if mask is None else jnp.where(mask, w, 0.0))
        carry_l[sub] = carry_l[sub] + jnp.sum(L, axis=-1, keepdims=True)
    return zs, Ls, ws, masks, carry_l


def _sweep(i, step, zero):
    nsub = _SB_SUBTILES

    def window():
        tiles = [(s, jj) for jj in range(_SB_WINDOW) for s in range(nsub)]
        return tuple((jnp.int32(_SB_WINDOW),) + c for c in step(tiles, [zero] * nsub))

    start = lax.cond(i >= -(-(_SB_WINDOW - 1) // nsub), window, lambda: tuple((jnp.int32(0),) + zero for _ in range(nsub)))
    outs = []
    for s in range(nsub):
        def done(c, s=s):
            return jnp.logical_or(c[0] > nsub * i + s, jnp.max(c[1]) < _EXP_ZERO_BELOW)

        def more(c, s=s):
            carries = [None] * nsub
            carries[s] = c[1:]
            return (c[0] + 1,) + step([(s, c[0])], carries)[s]

        outs.append(lax.while_loop(lambda c, done=done: jnp.logical_not(done(c)), more, start[s]))
    return outs


def _sb_fwd(hq, col0, width, layer, comm=None):
    T = hq.shape[0]
    npair = width // _LANES
    nsub = _SB_SUBTILES
    tq = nsub * _TQ

    def kern(q_ref, k_ref, v_ref, o_ref):
        i = pl.program_id(1)
        masks = _head_masks()
        tri, rel = _suffix_matrix()
        q = _scaled(q_ref[...])
        q2 = [_stack_heads(q[s * _TQ:(s + 1) * _TQ], masks) for s in range(nsub)]

        def step(tiles, carries):
            rows = [pl.ds(pl.multiple_of((nsub * i + s - jj) * _TQ, _TQ), _TQ) for s, jj in tiles]
            cls = [None if c is None else c[0] for c in carries]
            accs = [None if c is None else c[1] for c in carries]
            _, _, ws, _, cls = _stick_tiles([(q2[s], k_ref[r, :], jj, s) for (s, jj), r in zip(tiles, rows)], rel, cls, tri)
            for w, r, (s, _) in zip(ws, rows, tiles):
                accs[s] = accs[s] + _dot(w.astype(_MXU), v_ref[r, :])
            return [None if c is None else (cls[s], accs[s]) for s, c in enumerate(carries)]

        outs = _sweep(i, step, (jnp.zeros((2 * _TQ, 1), _F32), jnp.zeros((2 * _TQ, _LANES), _F32)))
        for s in range(nsub):
            o_ref[s * _TQ:(s + 1) * _TQ, :] = _unstack_heads(outs[s][2], masks)

    outs, extra = _call(
        kern, comm, name=f"stick_attn_fwd_{layer}", grid=(npair, T // tq),
        in_specs=_qkv_specs(T, col0 // _LANES, npair, tq),
        out_specs=[pl.BlockSpec((tq, _LANES), lambda h, i: (i, h))],
        out_shape=[jax.ShapeDtypeStruct((T, width), _F32)], scratch_shapes=[],
        args=(hq, hq, hq), semantics=("arbitrary", "arbitrary"))
    return outs[0], extra


def _sb_bwd(hq, o, do, col0, width, layer, comm=None):
    T = hq.shape[0]
    npair = width // _LANES
    nsub = _SB_SUBTILES
    tq = nsub * _TQ
    nq = T // tq
    scale = _HEAD ** -0.5

    def kern(q_ref, k_ref, v_ref, o_ref, do_ref, dq_ref, dk_ref, dv_ref, dk_acc, dv_acc):
        i = pl.program_id(1)

        @pl.when(i == 0)
        def _():
            dk_acc[...] = jnp.zeros_like(dk_acc)
            dv_acc[...] = jnp.zeros_like(dv_acc)

        masks = _head_masks()
        tri, rel = _suffix_matrix()
        q = _scaled(q_ref[...])
        do_t = do_ref[...]
        prod = do_t.astype(_F32) * o_ref[...]
        part = [slice(s * _TQ, (s + 1) * _TQ) for s in range(nsub)]
        q2 = [_stack_heads(q[p], masks) for p in part]
        do2 = [_stack_heads(do_t[p], masks).astype(_MXU) for p in part]
        dsum = [jnp.sum(_stack_heads(prod[p], masks), axis=-1, keepdims=True) for p in part]

        def step(tiles, carries):
            rows = [pl.ds(pl.multiple_of((nsub * i + s - jj) * _TQ, _TQ), _TQ) for s, jj in tiles]
            kjs = [k_ref[r, :] for r in rows]
            cls, cgs, dqs = ([None if c is None else c[n] for c in carries] for n in range(3))
            zs, Ls, ws, tile_masks, cls = _stick_tiles([(q2[s], kj, jj, s) for (s, jj), kj in zip(tiles, kjs)], rel, cls, tri)
            wbs = [w.astype(_MXU) for w in ws]
            gs = [wb.astype(_F32) * _dot_nt(do2[s], v_ref[r, :]) for wb, r, (s, _) in zip(wbs, rows, tiles)]
            for z, L, g, later, mask, wb, kj, r, (s, _) in zip(zs, Ls, gs, _suffix_sums(gs, tri), tile_masks, wbs, kjs,
                                                               rows, tiles):
                dz = g - jnp.exp(z + L) * (dsum[s] - (later + cgs[s]))
                if mask is not None:
                    dz = jnp.where(mask, dz, 0.0)
                dzb = dz.astype(_MXU)
                dk_acc[r, :] += _dot_tn(dzb, q2[s])
                dv_acc[r, :] += _dot_tn(wb, do2[s])
                dqs[s] = dqs[s] + _dot(dzb, kj)
                cgs[s] = cgs[s] + jnp.sum(g, axis=-1, keepdims=True)
            return [None if c is None else (cls[s], cgs[s], dqs[s]) for s, c in enumerate(carries)]

        zc = jnp.zeros((2 * _TQ, 1), _F32)
        outs = _sweep(i, step, (zc, zc, jnp.zeros((2 * _TQ, _LANES), _F32)))
        for s in range(nsub):
            dq_ref[part[s], :] = (_unstack_heads(outs[s][3], masks) * scale).astype(dq_ref.dtype)

        @pl.when(i == nq - 1)
        def _():
            dk_ref[...] = dk_acc[...].astype(dk_ref.dtype)
            dv_ref[...] = dv_acc[...].astype(dv_ref.dtype)

    tile_spec = pl.BlockSpec((tq, _LANES), lambda h, i: (i, h))
    column = pl.BlockSpec((T, _LANES), lambda h, i: (0, h))
    outs, extra = _call(
        kern, comm, name=f"stick_attn_bwd_{layer}", grid=(npair, nq),
        in_specs=_qkv_specs(T, col0 // _LANES, npair, tq) + [tile_spec, tile_spec],
        out_specs=[tile_spec, column, column],
        out_shape=[jax.ShapeDtypeStruct((T, width), _ACT)] * 3,
        scratch_shapes=[pltpu.VMEM((T, _LANES), _F32), pltpu.VMEM((T, _LANES), _F32)],
        args=(hq, hq, hq, o, do), semantics=("arbitrary", "arbitrary"))
    return outs, extra


_DENSE = ("w_in", "w_proj_a", "w_proj_b", "w_out", "w_ffn_in", "w_ffn_out")
_COL_SHARDED = {"w_in": True, "w_proj_a": True, "w_proj_b": True, "w_out": False, "w_ffn_in": True, "w_ffn_out": False}
_SMALL = ("b_gate", "rel_bias", "ln1_g", "ln1_b", "ln2_g", "ln2_b")


class _Plans:
    def __init__(self, plans=None, own_w_in=None):
        self.plans = plans or {}
        self.own_w_in = own_w_in or {}

    def start(self, key):
        if key not in self.plans:
            return None, None
        return self.plans[key]()

    @staticmethod
    def finish(done, extra):
        if done is not None:
            done(extra)


def _layer_fwd(x, W, small, l, alpha, plans):
    WA = small["rel_bias"].shape[1] * _HEAD
    row = lambda v: v[l].reshape(1, -1)
    if l in plans.own_w_in:
        h, xb, W["w_in"] = _in_proj_gathering(x, plans.own_w_in[l], l)
    else:
        comm, done = plans.start(f"in_proj_{l}")
        (h, xb), extra = _in_proj(x, W["w_in"], l, comm)
        plans.finish(done, extra)
    WB = (h.shape[1] - 2 * x.shape[1] - 3 * WA) // 3
    bias = _bias_tiles(small["rel_bias"][l])
    comm, done = plans.start(f"band_fwd_{l}")
    oa, extra = _attn_a_fwd(h, bias, 0, WA, l, comm)
    plans.finish(done, extra)
    comm, done = plans.start(f"stick_fwd_{l}")
    ob, extra = _sb_fwd(h, 3 * WA, WB, l, comm)
    plans.finish(done, extra)
    comm, done = plans.start(f"mix_fwd_{l}")
    (x1, u1, pre), extra = _mix_fwd(oa, ob, h, x, W["w_proj_a"], W["w_proj_b"], W["w_out"], row(small["b_gate"]),
                                            row(small["ln1_g"]), row(small["ln1_b"]), alpha, l, comm)
    plans.finish(done, extra)
    comm, done = plans.start(f"ffn_fwd_{l}")
    (x2, u2, act, gu, x1b), extra = _ffn_fwd(x1, W["w_ffn_in"], W["w_ffn_out"], row(small["ln2_g"]),
                                             row(small["ln2_b"]), alpha, l, comm)
    plans.finish(done, extra)
    return x2, dict(xb=xb, h=h, bias=bias, oa=oa, ob=ob, x1b=x1b, u1=u1, pre=pre, u2=u2, act=act, gu=gu)


def _layer_bwd(dy_or_target, S, W, small, l, last, alpha, plans, gw):
    D = S["xb"].shape[1]
    WA, WB = S["oa"].shape[1], S["ob"].shape[1]
    row = lambda v: v[l].reshape(1, -1)

    def blocks(g, n):
        return g if _COL_SHARDED[n] else g.reshape(4, g.shape[0] // 4, g.shape[1])

    dx1, du2b, dgu, st2 = _ffn_bwd(S["u2"], dy_or_target, S["gu"], row(small["ln2_g"]), row(small["ln2_b"]),
                                   W["w_ffn_in"], W["w_ffn_out"], alpha, l, last)
    gw["w_ffn_in"] = blocks(_grad_w(S["x1b"], dgu, col_shards=True, name=f"grad_w_ffn_in_{l}")[0], "w_ffn_in")
    gw["w_ffn_out"] = blocks(_grad_w(S["act"], du2b, col_shards=False, name=f"grad_w_ffn_out_{l}")[0], "w_ffn_out")
    du1, du1b, dya, dyb, dhg, doa, dob, st1 = _mix_bwd(S["u1"], dx1, S["oa"], S["ob"], S["h"], W["w_proj_a"],
                                                       W["w_proj_b"], W["w_out"], row(small["b_gate"]),
                                                       row(small["ln1_g"]), l)
    gw["w_out"] = blocks(_grad_w(S["pre"], du1b, col_shards=False, name=f"grad_w_out_{l}")[0], "w_out")
    gw["w_proj_a"] = blocks(_grad_w(S["oa"], dya, col_shards=True, name=f"grad_w_proj_a_{l}")[0], "w_proj_a")
    gw["w_proj_b"] = blocks(_grad_w(S["ob"], dyb, col_shards=True, name=f"grad_w_proj_b_{l}")[0], "w_proj_b")
    comm, done = plans.start(f"band_bwd_{l}")
    (dqa, dka, dva, dbias), extra = _attn_a_bwd(S["h"], S["bias"], doa, 0, WA, l, comm)
    plans.finish(done, extra)
    comm, done = plans.start(f"stick_bwd_{l}")
    (dqb, dkb, dvb), extra = _sb_bwd(S["h"], S["ob"], dob, 3 * WA, WB, l, comm)
    plans.finish(done, extra)
    dh = [dqa, dka, dva, dqb, dkb, dvb, dhg]
    comm, done = plans.start(f"grad_w_in_{l}")
    gw["w_in"], extra = _grad_w_pieces(S["xb"], dh, f"grad_w_in_{l}", comm)
    plans.finish(done, extra)
    comm, done = plans.start(f"in_proj_bwd_{l}")
    dx, extra = _residual_nt(du1, alpha, dh, W["w_in"], f"in_proj_bwd_{l}", comm)
    plans.finish(done, extra)
    gs = dict(b_gate=st1[0], rel_bias=_fold_bias_grad(dbias), ln1_g=st1[1, :D], ln1_b=st1[1, D:],
              ln2_g=st2[0], ln2_b=st2[1])
    return dx, gs, st2[2]


def _local_step(x, target, W, small, plans=None, gws=None):
    depth = len(W)
    alpha = float((2 * depth) ** 0.25)
    plans = plans or _Plans()
    gws = gws if gws is not None else [dict() for _ in range(depth)]
    saved = []
    h = x
    for l in range(depth):
        h, S = _layer_fwd(h, W[l], small, l, alpha, plans)
        saved.append(S)
    gss = [None] * depth
    d = target
    sq = None
    for l in reversed(range(depth)):
        d, gss[l], sq_l = _layer_bwd(d, saved[l], W[l], small, l, l == depth - 1, alpha, plans, gws[l])
        if l == depth - 1:
            sq = sq_l
    return sq, d, gws, gss


def _place():
    return lax.axis_index("x"), lax.axis_index("y"), lax.axis_index("c")


def _remote(src, dst, send_sem, recv_sem, to):
    return pltpu.make_async_remote_copy(src_ref=src, dst_ref=dst, send_sem=send_sem, recv_sem=recv_sem,
                                        device_id=to, device_id_type=_MESH)


def _half(ref, hc):
    kh = ref.shape[0] // 2
    return ref.at[pl.ds(pl.multiple_of(hc * kh, 16), kh), :]


def _gather_plan(blocks, fractions):
    nt = len(blocks)

    def run(step, nsteps, ins, outs, sems):
        send_sems, recv_sems, loc_sems = sems
        x, y, c = _place()
        k = 2 * x + y
        me, sibling = (x, y, c), (x, y, 1 - c)
        chips = [(1 - x, y), (x, 1 - y), (1 - x, 1 - y)]
        chip_k = [2 * cx + cy for cx, cy in chips]

        def ici(t, s, owner_k, to, src=None):
            dst = _half(outs[t].at[owner_k], c)
            return _remote(dst if src is None else src, dst, send_sems.at[t, s], recv_sems.at[t, s], to)

        def passed(t, s, hc, to):
            blk = _half(outs[t].at[chip_k[s]], hc)
            return _remote(blk, blk, send_sems.at[t, 3 + s], recv_sems.at[t, 3 + s], to)

        def local(t):
            return pltpu.make_async_copy(ins[t], outs[t].at[k], loc_sems.at[t])

        @pl.when(step == 0)
        def _():
            for t in range(nt):
                local(t).start()
                for s, chip in enumerate(chips):
                    ici(t, s, k, (*chip, c), src=_half(ins[t], c)).start()

        for t in range(nt):
            @pl.when(step == min(nsteps - 1, int(fractions[t] * nsteps)))
            def _():
                for s in range(3):
                    ici(t, s, chip_k[s], me).wait_recv()
                    passed(t, s, c, sibling).start()

        @pl.when(step == nsteps - 1)
        def _():
            for t in range(nt):
                for s, chip in enumerate(chips):
                    passed(t, s, 1 - c, me).wait_recv()
            for t in range(nt):
                for s, chip in enumerate(chips):
                    ici(t, s, k, (*chip, c), src=_half(ins[t], c)).wait_send()
                    passed(t, s, c, sibling).wait_send()
                local(t).wait()

    return _Comm(blocks, [jax.ShapeDtypeStruct((4,) + b.shape, b.dtype) for b in blocks],
                 [pltpu.SemaphoreType.DMA((nt, 6)), pltpu.SemaphoreType.DMA((nt, 6)), pltpu.SemaphoreType.DMA((nt,))], run)


def _scatter_plan(grads, owners):
    nt = len(grads)
    shapes = [g.shape[1:] if g.ndim == 3 else (g.shape[0], g.shape[1] // 4) for g in grads]

    def run(step, nsteps, ins, outs, sems):
        send_sems, recv_sems, loc_sems = sems
        x, y, c = _place()
        me = 4 * x + 2 * y + c

        def target(r):
            tx = 1 - x if r & 2 else x
            ty = 1 - y if r & 1 else y
            return tx, ty

        def block(t, chip):
            if len(ins[t].shape) == 3:
                return ins[t].at[chip]
            n = shapes[t][1]
            return ins[t].at[:, pl.ds(pl.multiple_of(chip * n, _LANES), n)]

        def send(t, r):
            tx, ty = target(r)
            return _remote(block(t, 2 * tx + ty), outs[t].at[me], send_sems.at[t, r], recv_sems.at[t, 2 * r + c],
                           (tx, ty, owners[t]))

        def local(t):
            return pltpu.make_async_copy(block(t, 2 * x + y), outs[t].at[me], loc_sems.at[t])

        @pl.when(step == 0)
        def _():
            for t in range(nt):
                @pl.when(c == owners[t])
                def _():
                    local(t).start()

                @pl.when(c != owners[t])
                def _():
                    send(t, 0).start()

                for r in range(1, 4):
                    send(t, r).start()

        @pl.when(step == nsteps - 1)
        def _():
            for t in range(nt):
                @pl.when(c == owners[t])
                def _():
                    for r in range(4):
                        sx, sy = target(r)
                        for cs in range(2):
                            if r == 0 and cs == owners[t]:
                                continue
                            src_dev = 4 * sx + 2 * sy + cs
                            _remote(block(t, 0), outs[t].at[src_dev], send_sems.at[t, r], recv_sems.at[t, 2 * r + cs],
                                    (x, y, c)).wait_recv()
                    local(t).wait()

                @pl.when(c != owners[t])
                def _():
                    send(t, 0).wait_send()

                for r in range(1, 4):
                    send(t, r).wait_send()

    return _Comm(grads, [jax.ShapeDtypeStruct((8,) + s, g.dtype) for s, g in zip(shapes, grads)],
                 [pltpu.SemaphoreType.DMA((nt, 4)), pltpu.SemaphoreType.DMA((nt, 8)), pltpu.SemaphoreType.DMA((nt,))], run)


def _share_plan(reduced, owners):
    nt = len(reduced)

    def run(step, nsteps, ins, outs, sems):
        del ins
        send_sems, recv_sems = sems
        x, y, c = _place()

        def give(t, to):
            return _remote(outs[t], outs[t], send_sems.at[t], recv_sems.at[t], to)

        @pl.when(step == 0)
        def _():
            for t in range(nt):
                @pl.when(c == owners[t])
                def _():
                    give(t, (x, y, 1 - c)).start()

        @pl.when(step == nsteps - 1)
        def _():
            for t in range(nt):
                @pl.when(c == owners[t])
                def _():
                    give(t, (x, y, 1 - c)).wait_send()

                @pl.when(c != owners[t])
                def _():
                    give(t, (x, y, c)).wait_recv()

    return _Comm(reduced, [jax.ShapeDtypeStruct(r.shape, r.dtype) for r in reduced],
                 [pltpu.SemaphoreType.DMA((nt,)), pltpu.SemaphoreType.DMA((nt,))], run,
                 aliases={t: t for t in range(nt)})


def _join(a, b):
    ni, no, ns = len(a.inputs), len(a.out_shapes), len(a.sems)

    def run(step, nsteps, ins, outs, sems):
        a.run(step, nsteps, ins[:ni], outs[:no], sems[:ns])
        b.run(step, nsteps, ins[ni:], outs[no:], sems[ns:])

    aliases = dict(a.aliases)
    aliases.update({ni + i: no + o for i, o in b.aliases.items()})
    return _Comm(a.inputs + b.inputs, a.out_shapes + b.out_shapes, a.sems + b.sems, run, aliases)


def _peer(x, y, c, r):
    px = 1 - x if r & 4 else x
    py = 1 - y if r & 2 else y
    pc = 1 - c if r & 1 else c
    return (px, py, pc), 4 * px + 2 * py + pc


def _sum_slots(st, name):
    _, K, n = st.shape
    tr = next(t for t in (256, 128, 64, 32, 16) if K % t == 0)

    def kern(s_ref, o_ref):
        acc = s_ref[0].astype(_F32)
        for d in range(1, 8):
            acc = acc + s_ref[d].astype(_F32)
        o_ref[...] = acc.astype(o_ref.dtype)

    return pl.pallas_call(
        kern, name=name, grid=(K // tr,),
        in_specs=[pl.BlockSpec((8, tr, n), lambda i: (0, i, 0))], out_specs=_rows(tr, n),
        out_shape=jax.ShapeDtypeStruct((K, n), _ACT),
        compiler_params=_cparams("parallel"),
    )(st)


def _all_reduce_small(p):
    R = p.shape[0]

    def body(p_ref, o_ref, stage, send_sems, recv_sems):
        x, y, c = _place()
        me = 4 * x + 2 * y + c
        stage[me] = p_ref[...]
        sent = []
        for r in range(1, 8):
            to, _ = _peer(x, y, c, r)
            cp = _remote(p_ref, stage.at[me], send_sems.at[r - 1], recv_sems.at[r - 1], to)
            cp.start()
            sent.append(cp)
        for r in range(1, 8):
            _, src_dev = _peer(x, y, c, r)
            _remote(p_ref, stage.at[src_dev], send_sems.at[r - 1], recv_sems.at[r - 1], (x, y, c)).wait_recv()
        acc = stage[0]
        for d in range(1, 8):
            acc = acc + stage[d]
        o_ref[...] = acc
        for cp in sent:
            cp.wait_send()

    vm = pl.BlockSpec(memory_space=pltpu.VMEM)
    return pl.pallas_call(
        body, name="all_reduce_small",
        in_specs=[vm], out_specs=vm,
        out_shape=jax.ShapeDtypeStruct((R, _LANES), _F32),
        scratch_shapes=[pltpu.VMEM((8, R, _LANES), _F32), pltpu.SemaphoreType.DMA((7,)), pltpu.SemaphoreType.DMA((7,))],
    )(p)


def _adamw_update(gv, w_ref, m_ref, v_ref, gf_ref, d_ref, nm_ref, nv_ref):
    nm = _B1 * m_ref[...] + (1.0 - _B1) * gv
    nv = _B2 * v_ref[...] + (1.0 - _B2) * (gv * gv)
    m_hat = nm / (1.0 - _B1 ** _STEP)
    v_hat = nv / (1.0 - _B2 ** _STEP)
    gf_ref[...] = gv
    d_ref[...] = -_LR * (m_hat / (jnp.sqrt(v_hat) + _EPS) + _WD * w_ref[...])
    nm_ref[...] = nm
    nv_ref[...] = nv


def _adamw_layers(w, g_layers, m, v, name):
    _, K, n = w.shape
    tr = next(t for t in (256, 128, 64, 32, 16) if K % t == 0)

    def kern(w_ref, g0_ref, g1_ref, m_ref, v_ref, *out_refs):
        first = pl.program_id(0) == 0
        gv = jnp.where(first, g0_ref[...].astype(_F32), g1_ref[...].astype(_F32))
        _adamw_update(gv, w_ref, m_ref, v_ref, *out_refs)

    stacked = pl.BlockSpec((None, tr, n), lambda l, i: (l, i, 0))
    layer = pl.BlockSpec((tr, n), lambda l, i: (i, 0))
    return tuple(pl.pallas_call(
        kern, name=name, grid=(2, K // tr),
        in_specs=[stacked, layer, layer, stacked, stacked], out_specs=[stacked] * 4,
        out_shape=[jax.ShapeDtypeStruct(w.shape, _F32)] * 4,
        compiler_params=_cparams("parallel", "parallel"),
    )(w, g_layers[0], g_layers[1], m, v))


def _adamw(w, g, m, v, name):
    shape = w.shape
    w2, g2, m2, v2 = (a.reshape(-1, shape[-1]) for a in (w, g, m, v))
    R, C = w2.shape
    tr = next((t for t in (256, 128, 64, 32, 16) if R % t == 0), R)

    def kern(w_ref, g_ref, m_ref, v_ref, *out_refs):
        _adamw_update(g_ref[...].astype(_F32), w_ref, m_ref, v_ref, *out_refs)

    outs = pl.pallas_call(
        kern, name=name, grid=(R // tr,),
        in_specs=[_rows(tr, C)] * 4, out_specs=[_rows(tr, C)] * 4,
        out_shape=[jax.ShapeDtypeStruct((R, C), _F32)] * 4,
        compiler_params=_cparams("parallel"),
    )(w2, g2, m2, v2)
    return tuple(o.reshape(shape) for o in outs)


def _pack_small(gss, sq):
    parts = [gss[l][n].reshape(-1) for n in _SMALL for l in range(len(gss))] + [jnp.sum(sq).reshape(1)]
    flat = jnp.concatenate(parts)
    rows = -(-flat.shape[0] // (8 * _LANES)) * 8
    return jnp.pad(flat, (0, rows * _LANES - flat.shape[0])).reshape(rows, _LANES)


def _unpack_small(total, shapes):
    flat = total.reshape(-1)
    out, off = {}, 0
    for n in _SMALL:
        layers = []
        for _ in range(shapes[n][0]):
            size = 1
            for s in shapes[n][1:]:
                size *= s
            layers.append(flat[off:off + size].reshape(shapes[n][1:]))
            off += size
        out[n] = jnp.stack(layers)
    return out, flat[off]


_GATHER = {
    "band_fwd_0": [(0, "w_proj_a"), (0, "w_proj_b"), (0, "w_out"), (0, "w_ffn_out")],
    "stick_fwd_0": [(0, "w_ffn_in"), (1, "w_proj_a"), (1, "w_proj_b"), (1, "w_out")],
    "mix_fwd_0": [(1, "w_ffn_out")],
    "ffn_fwd_0": [(1, "w_in"), (1, "w_ffn_in")],
}
_SCATTER = {
    "band_bwd_1": [(1, "w_ffn_in"), (1, "w_ffn_out")],
    "stick_bwd_1": [(1, "w_proj_a"), (1, "w_proj_b"), (1, "w_out")],
    "band_bwd_0": [(1, "w_in"), (0, "w_ffn_in")],
    "stick_bwd_0": [(0, "w_ffn_out"), (0, "w_proj_a"), (0, "w_proj_b"), (0, "w_out")],
    "in_proj_bwd_0": [(0, "w_in")],
}
_SHARE = {"stick_bwd_1": "band_bwd_1", "band_bwd_0": "stick_bwd_1", "stick_bwd_0": "band_bwd_0", "grad_w_in_0": "stick_bwd_0"}


def _owner(key):
    del key
    return 1


def kernel(x, w_in, b_gate, rel_bias, w_proj_a, w_proj_b, w_out, ln1_g, ln1_b, w_ffn_in, w_ffn_out, ln2_g, ln2_b, loss_target, m_w_in, m_b_gate, m_rel_bias, m_w_proj_a, m_w_proj_b, m_w_out, m_ln1_g, m_ln1_b, m_w_ffn_in, m_w_ffn_out, m_ln2_g, m_ln2_b, v_w_in, v_b_gate, v_rel_bias, v_w_proj_a, v_w_proj_b, v_w_out, v_ln1_g, v_ln1_b, v_w_ffn_in, v_w_ffn_out, v_ln2_g, v_ln2_b):
    names = ("w_in", "b_gate", "rel_bias", "w_proj_a", "w_proj_b", "w_out", "ln1_g", "ln1_b", "w_ffn_in", "w_ffn_out", "ln2_g", "ln2_b")
    w = dict(zip(names, (w_in, b_gate, rel_bias, w_proj_a, w_proj_b, w_out, ln1_g, ln1_b, w_ffn_in, w_ffn_out, ln2_g, ln2_b)))
    m = dict(zip(names, (m_w_in, m_b_gate, m_rel_bias, m_w_proj_a, m_w_proj_b, m_w_out, m_ln1_g, m_ln1_b, m_w_ffn_in, m_w_ffn_out, m_ln2_g, m_ln2_b)))
    v = dict(zip(names, (v_w_in, v_b_gate, v_rel_bias, v_w_proj_a, v_w_proj_b, v_w_out, v_ln1_g, v_ln1_b, v_w_ffn_in, v_w_ffn_out, v_ln2_g, v_ln2_b)))
    T, D = x.shape[-2], x.shape[-1]
    assert w_in.shape[0] == 2, "the exchange schedule below is written for two layers"

    mine = [{n: w[n][l].astype(_MXU) for n in _DENSE} for l in range(2)]
    W = [dict(), dict()]
    gws = [dict(), dict()]
    slots, final = {}, {}

    def gather(keys):
        sizes = [mine[l][n].size for l, n in keys]
        passed, fractions = 0, []
        for s in sizes:
            passed += s
            fractions.append(0.15 + 0.6 * passed / sum(sizes))

        def done(outs):
            for (l, n), o in zip(keys, outs):
                W[l][n] = o
        return _gather_plan([mine[l][n] for l, n in keys], fractions), done

    def scatter(keys):
        comm = _scatter_plan([gws[l][n] for l, n in keys], [_owner(key) for key in keys])
        return comm, lambda outs: slots.update(zip(keys, outs))

    def share(keys):
        reduced = [_sum_slots(slots[key], f"sum_grad_{key[1]}_{key[0]}") for key in keys]
        comm = _share_plan(reduced, [_owner(key) for key in keys])
        return comm, lambda outs: final.update(zip(keys, outs))

    def both(first, second):
        (ca, da), (cb, db) = first, second
        na = len(ca.out_shapes)
        return _join(ca, cb), lambda outs: (da(outs[:na]), db(outs[na:]))

    plans = {key: functools.partial(gather, keys) for key, keys in _GATHER.items()}
    for key, keys in _SCATTER.items():
        plans[key] = functools.partial(scatter, keys)
    for key, scattered_under in _SHARE.items():
        handed = functools.partial(share, _SCATTER[scattered_under])
        carried = plans.get(key)
        plans[key] = handed if carried is None else (lambda carried=carried, handed=handed: both(carried(), handed()))
    small = {n: w[n] for n in _SMALL}
    sq, dx, _, gss = _local_step(x.reshape(T, D), loss_target.reshape(T, D), W, small,
                                  _Plans(plans, {0: mine[0]["w_in"]}), gws)

    comm, done = share(_SCATTER["in_proj_bwd_0"])
    done(_comm_only(comm, "share_last"))
    total = _all_reduce_small(_pack_small(gss, sq))
    small_grads, sq_all = _unpack_small(total, {n: w[n].shape for n in _SMALL})
    loss = 0.5 * sq_all / D

    grad, delta, new_m, new_v = {}, {}, {}, {}
    for n in names:
        if n in _DENSE:
            updated = _adamw_layers(w[n], [final[(l, n)] for l in range(2)], m[n], v[n], f"adamw_{n}")
        else:
            updated = _adamw(w[n], small_grads[n], m[n], v[n], f"adamw_{n}")
        grad[n], delta[n], new_m[n], new_v[n] = updated
    return (loss, dx.reshape(x.shape), *[grad[n] for n in names], *[delta[n] for n in names],
            *[new_m[n] for n in names], *[new_v[n] for n in names])
```

```python
import functools

import jax
import jax.numpy as jnp
from jax import lax
from jax.experimental import pallas as pl
from jax.experimental.pallas import tpu as pltpu

_MXU = jnp.bfloat16
_ACT = jnp.bfloat16
_F32 = jnp.float32

_HEAD = 64
_CHUNK = 64
_LANES = 128
_TQ = 128
_BAND_TILES = 5
_BIAS_TILES = 9
_REL_CLIP = 256
_LN_EPS = 1e-5
_MASKED = -1e30
_EXP_ZERO_BELOW = -87.34
_SB_WINDOW = 2
_SB_SUBTILES = 4
_BAND_SUBTILES = 8
_VMEM_LIMIT = 56 * 1024 * 1024
_GRAD_ACC_BYTES = 12 * 1024 * 1024

_LR, _B1, _B2, _EPS, _WD, _STEP = 0.001, 0.9, 0.999, 1e-08, 0.01, 10

_MESH = pl.DeviceIdType.MESH


def _dot(a, b):
    return jnp.dot(a, b, preferred_element_type=_F32)


def _dot_nt(a, b):
    return lax.dot_general(a, b, (((1,), (1,)), ((), ())), preferred_element_type=_F32)


def _dot_tn(a, b):
    return lax.dot_general(a, b, (((0,), (0,)), ((), ())), preferred_element_type=_F32)


def _cparams(*sem):
    return pltpu.CompilerParams(dimension_semantics=sem, vmem_limit_bytes=_VMEM_LIMIT)


def _rows(t, c):
    return pl.BlockSpec((t, c), lambda i: (i, 0))


def _whole(shape):
    return pl.BlockSpec(shape, lambda i: tuple(0 for _ in shape))


_ANY = pl.BlockSpec(memory_space=pl.ANY)


def _load_cols(w_hbm, w_vmem, sem):
    n = w_hbm.shape[-1]
    cps = [pltpu.make_async_copy(w_hbm.at[k], w_vmem.at[:, pl.ds(k * n, n)], sem.at[k]) for k in range(4)]
    for cp in cps:
        cp.start()
    for cp in cps:
        cp.wait()


def _load_rows(w_hbm, w_vmem, sem):
    r = w_hbm.shape[-2]
    cps = [pltpu.make_async_copy(w_hbm.at[k], w_vmem.at[pl.ds(k * r, r), :], sem.at[k]) for k in range(4)]
    for cp in cps:
        cp.start()
    for cp in cps:
        cp.wait()


def _ln_stats(u):
    mu = jnp.mean(u, axis=-1, keepdims=True)
    xc = u - mu
    var = jnp.mean(xc * xc, axis=-1, keepdims=True)
    rstd = lax.rsqrt(var + _LN_EPS)
    return xc * rstd, rstd


def _ln_bwd(u, dy, gamma):
    xhat, rstd = _ln_stats(u)
    dxh = dy * gamma
    m1 = jnp.mean(dxh, axis=-1, keepdims=True)
    m2 = jnp.mean(dxh * xhat, axis=-1, keepdims=True)
    du = rstd * (dxh - m1 - xhat * m2)
    return du, jnp.sum(dy * xhat, axis=0, keepdims=True), jnp.sum(dy, axis=0, keepdims=True), xhat


def _divisor_tile(n, cap):
    best = None
    for t in range(_LANES, min(n, cap) + 1, _LANES):
        if n % t == 0:
            best = t
    return best or n


class _Comm:
    def __init__(self, inputs, out_shapes, sems, run, aliases=None):
        self.inputs, self.out_shapes, self.sems, self.run = list(inputs), list(out_shapes), list(sems), run
        self.aliases = aliases or {}


def _call(kern, comm, *, name, grid, in_specs, out_specs, out_shape, scratch_shapes, args, semantics):
    in_specs, out_specs, out_shape, scratch_shapes = list(in_specs), list(out_specs), list(out_shape), list(scratch_shapes)
    if comm is None:
        outs = pl.pallas_call(kern, name=name, grid=grid, in_specs=in_specs, out_specs=out_specs, out_shape=out_shape,
                              scratch_shapes=scratch_shapes, compiler_params=_cparams(*semantics))(*args)
        return list(outs), []
    n_in, n_out, n_scr = len(in_specs), len(out_specs), len(scratch_shapes)
    ci, co = len(comm.inputs), len(comm.out_shapes)
    nsteps = functools.reduce(lambda a, b: a * b, grid, 1)

    def fused(*refs):
        a, b = n_in, n_in + ci
        c, d = b + n_out, b + n_out + co
        e = d + n_scr
        step = pl.program_id(0)
        for ax in range(1, len(grid)):
            step = step * grid[ax] + pl.program_id(ax)
        comm.run(step, nsteps, refs[a:b], refs[c:d], refs[e:])
        kern(*refs[:a], *refs[b:c], *refs[d:e])

    outs = pl.pallas_call(
        fused, name=name, grid=grid, in_specs=in_specs + [_ANY] * ci, out_specs=out_specs + [_ANY] * co,
        out_shape=out_shape + comm.out_shapes, scratch_shapes=scratch_shapes + comm.sems,
        input_output_aliases={n_in + i: n_out + o for i, o in comm.aliases.items()},
        compiler_params=_cparams(*("arbitrary" for _ in grid)))(*args, *comm.inputs)
    return list(outs[:n_out]), list(outs[n_out:])


def _comm_only(comm, name):
    def body(*refs):
        ci, co = len(comm.inputs), len(comm.out_shapes)
        comm.run(0, 1, refs[:ci], refs[ci:ci + co], refs[ci + co:])

    outs = pl.pallas_call(body, name=name, in_specs=[_ANY] * len(comm.inputs), out_specs=[_ANY] * len(comm.out_shapes),
                          out_shape=comm.out_shapes, scratch_shapes=comm.sems,
                          input_output_aliases=dict(comm.aliases))(*comm.inputs)
    return list(outs)


def _in_proj(x, w_in, layer, comm=None):
    T, D = x.shape
    N = 4 * w_in.shape[-1]
    tm = 512

    def kern(x_ref, w_hbm, h_ref, xb_ref, w_v, sem):
        @pl.when(pl.program_id(0) == 0)
        def _():
            _load_cols(w_hbm, w_v, sem)

        xb = x_ref[...].astype(_MXU)
        h_ref[...] = _dot(xb, w_v[...]).astype(h_ref.dtype)
        xb_ref[...] = xb.astype(xb_ref.dtype)

    return _call(
        kern, comm, name=f"in_proj_{layer}", grid=(T // tm,),
        in_specs=[_rows(tm, D), _ANY],
        out_specs=[_rows(tm, N), _rows(tm, D)],
        out_shape=[jax.ShapeDtypeStruct((T, N), _ACT), jax.ShapeDtypeStruct((T, D), _ACT)],
        scratch_shapes=[pltpu.VMEM((D, N), w_in.dtype), pltpu.SemaphoreType.DMA((4,))],
        args=(x, w_in), semantics=("arbitrary",))


def _in_proj_gathering(x, block, layer):
    T, D = x.shape
    n = block.shape[1]
    tm = min(T, 1024)
    nrows = T // tm
    pass_steps = [int(f * nrows) for f in (0.6, 1.0, 1.7)]
    px, py, _ = _place()
    order = jnp.stack([2 * px + py, 2 * (1 - px) + py, 2 * px + (1 - py), 2 * (1 - px) + (1 - py)]).astype(jnp.int32)

    def kern(order_ref, x_ref, blk_hbm, h_ref, xb_ref, w_hbm, w_v, send_sems, recv_sems, loc_sem, load_sem):
        del order_ref
        step = pl.program_id(0) * nrows + pl.program_id(1)
        x_, y_, c = _place()
        k = 2 * x_ + y_
        me, sibling = (x_, y_, c), (x_, y_, 1 - c)
        chips = [(1 - x_, y_), (x_, 1 - y_), (1 - x_, 1 - y_)]
        chip_k = [2 * cx + cy for cx, cy in chips]

        def ici(s, owner_k, to, src=None):
            dst = _half(w_hbm.at[owner_k], c)
            return _remote(dst if src is None else src, dst, send_sems.at[s], recv_sems.at[s], to)

        def passed(s, hc, to):
            blk = _half(w_hbm.at[chip_k[s]], hc)
            return _remote(blk, blk, send_sems.at[3 + s], recv_sems.at[3 + s], to)

        local = pltpu.make_async_copy(blk_hbm, w_hbm.at[k], loc_sem.at[0])

        def load(src):
            cp = pltpu.make_async_copy(src, w_v, load_sem.at[0])
            cp.start()
            cp.wait()

        @pl.when(step == 0)
        def _():
            for s, chip in enumerate(chips):
                ici(s, k, (*chip, c), src=_half(blk_hbm, c)).start()
            local.start()
            load(blk_hbm)

        for s in range(3):
            @pl.when(step == pass_steps[s])
            def _():
                ici(s, chip_k[s], me).wait_recv()
                passed(s, c, sibling).start()

            @pl.when(step == (s + 1) * nrows)
            def _():
                passed(s, 1 - c, me).wait_recv()
                load(w_hbm.at[chip_k[s]])

        xb = x_ref[...].astype(_MXU)
        h_ref[...] = _dot(xb, w_v[...]).astype(h_ref.dtype)

        @pl.when(pl.program_id(0) == 0)
        def _():
            xb_ref[...] = xb.astype(xb_ref.dtype)

        @pl.when(step == 4 * nrows - 1)
        def _():
            for s, chip in enumerate(chips):
                ici(s, k, (*chip, c), src=_half(blk_hbm, c)).wait_send()
                passed(s, c, sibling).wait_send()
            local.wait()

    assert all(pass_steps[s] <= (s + 1) * nrows for s in range(3))
    h, xb, w_in = pl.pallas_call(
        kern, name=f"in_proj_{layer}",
        grid_spec=pltpu.PrefetchScalarGridSpec(
            num_scalar_prefetch=1, grid=(4, nrows),
            in_specs=[pl.BlockSpec((tm, D), lambda j, i, o: (i, 0)), _ANY],
            out_specs=[pl.BlockSpec((tm, n), lambda j, i, o: (i, o[j])),
                       pl.BlockSpec((tm, D), lambda j, i, o: (jnp.where(j == 0, i, nrows - 1), 0)), _ANY],
            scratch_shapes=[pltpu.VMEM((D, n), block.dtype), pltpu.SemaphoreType.DMA((6,)),
                            pltpu.SemaphoreType.DMA((6,)), pltpu.SemaphoreType.DMA((1,)), pltpu.SemaphoreType.DMA((1,))]),
        out_shape=[jax.ShapeDtypeStruct((T, 4 * n), _ACT), jax.ShapeDtypeStruct((T, D), _ACT),
                   jax.ShapeDtypeStruct((4,) + block.shape, block.dtype)],
        compiler_params=_cparams("arbitrary", "arbitrary"))(order, x, block)
    return h, xb, w_in


def _gate_specs(h, tm, D):
    first = (h.shape[1] - 2 * D) // D
    assert first * D + 2 * D == h.shape[1]
    return [pl.BlockSpec((tm, D), lambda i: (i, first)), pl.BlockSpec((tm, D), lambda i: (i, first + 1))]


def _mix_fwd(oa, ob, h, x, wpa, wpb, wo, bg, gamma, beta, alpha, layer, comm=None):
    T, D = x.shape
    WA, WB = oa.shape[1], ob.shape[1]
    tm = 512

    def kern(oa_ref, ob_ref, hga_ref, hgb_ref, x_ref, bg_ref, g_ref, b_ref, wpa_h, wpb_h, wo_h,
             x1_ref, u1_ref, pre_ref, wpa_v, wpb_v, wo_v, sa, sb, so):
        @pl.when(pl.program_id(0) == 0)
        def _():
            _load_cols(wpa_h, wpa_v, sa)
            _load_cols(wpb_h, wpb_v, sb)
            _load_rows(wo_h, wo_v, so)

        ya = _dot(oa_ref[...].astype(_MXU), wpa_v[...])
        yb = _dot(ob_ref[...].astype(_MXU), wpb_v[...])
        bgv = bg_ref[...]
        ga = jax.nn.sigmoid(hga_ref[...].astype(_F32) + bgv[:, :D])
        gb = jax.nn.sigmoid(hgb_ref[...].astype(_F32) + bgv[:, D:])
        pre = ga * ya + gb * yb
        mix = _dot(pre.astype(_MXU), wo_v[...])
        u = alpha * x_ref[...] + mix
        xhat, _ = _ln_stats(u)
        x1_ref[...] = xhat * g_ref[...] + b_ref[...]
        u1_ref[...] = u
        pre_ref[...] = pre.astype(pre_ref.dtype)

    return _call(
        kern, comm, name=f"mix_fwd_{layer}", grid=(T // tm,),
        in_specs=[_rows(tm, WA), _rows(tm, WB), *_gate_specs(h, tm, D), _rows(tm, D),
                  _whole((1, 2 * D)), _whole((1, D)), _whole((1, D)), _ANY, _ANY, _ANY],
        out_specs=[_rows(tm, D)] * 3,
        out_shape=[jax.ShapeDtypeStruct((T, D), _F32), jax.ShapeDtypeStruct((T, D), _F32),
                   jax.ShapeDtypeStruct((T, D), _ACT)],
        scratch_shapes=[pltpu.VMEM((WA, D), wpa.dtype), pltpu.VMEM((WB, D), wpb.dtype), pltpu.VMEM((D, D), wo.dtype),
                        pltpu.SemaphoreType.DMA((4,)), pltpu.SemaphoreType.DMA((4,)), pltpu.SemaphoreType.DMA((4,))],
        args=(oa, ob, h, h, x, bg, gamma, beta, wpa, wpb, wo), semantics=("arbitrary",))


def _ffn_fwd(x1, wfi, wfo, gamma, beta, alpha, layer, comm=None):
    T, D = x1.shape
    F2 = 4 * wfi.shape[-1]
    F = F2 // 2
    tm = 512
    fc = F // 2

    def kern(x_ref, g_ref, b_ref, wi_h, wo_h, x2_ref, u2_ref, act_ref, gu_ref, xb_ref, wi_v, wo_v, si, so):
        @pl.when(pl.program_id(0) == 0)
        def _():
            _load_cols(wi_h, wi_v, si)
            _load_rows(wo_h, wo_v, so)

        x = x_ref[...]
        xb = x.astype(_MXU)
        xb_ref[...] = xb.astype(xb_ref.dtype)
        ffn = jnp.zeros((tm, D), _F32)
        for c in range(2):
            g = _dot(xb, wi_v[:, c * fc:(c + 1) * fc])
            u = _dot(xb, wi_v[:, F + c * fc:F + (c + 1) * fc])
            act = g * jax.nn.sigmoid(g) * u
            ab = act.astype(_MXU)
            ffn = ffn + _dot(ab, wo_v[c * fc:(c + 1) * fc, :])
            act_ref[:, c * fc:(c + 1) * fc] = ab.astype(act_ref.dtype)
            gu_ref[:, c * fc:(c + 1) * fc] = g.astype(gu_ref.dtype)
            gu_ref[:, F + c * fc:F + (c + 1) * fc] = u.astype(gu_ref.dtype)
        uu = alpha * x + ffn
        xhat, _ = _ln_stats(uu)
        x2_ref[...] = xhat * g_ref[...] + b_ref[...]
        u2_ref[...] = uu

    return _call(
        kern, comm, name=f"ffn_fwd_{layer}", grid=(T // tm,),
        in_specs=[_rows(tm, D), _whole((1, D)), _whole((1, D)), _ANY, _ANY],
        out_specs=[_rows(tm, D), _rows(tm, D), _rows(tm, F), _rows(tm, F2), _rows(tm, D)],
        out_shape=[jax.ShapeDtypeStruct((T, D), _F32), jax.ShapeDtypeStruct((T, D), _F32),
                   jax.ShapeDtypeStruct((T, F), _ACT), jax.ShapeDtypeStruct((T, F2), _ACT),
                   jax.ShapeDtypeStruct((T, D), _ACT)],
        scratch_shapes=[pltpu.VMEM((D, F2), wfi.dtype), pltpu.VMEM((F, D), wfo.dtype),
                        pltpu.SemaphoreType.DMA((4,)), pltpu.SemaphoreType.DMA((4,))],
        args=(x1, gamma, beta, wfi, wfo), semantics=("arbitrary",))


def _ffn_bwd(u2, dy_or_target, gu, gamma, beta, wfi, wfo, alpha, layer, last):
    T, D = u2.shape
    F2 = gu.shape[1]
    F = F2 // 2
    tm = 256
    fc = F // 2

    def kern(u_ref, dy_ref, gu_ref, g_ref, b_ref, wi_h, wo_h, dx_ref, dub_ref, dgu_ref, st_ref, wi_v, wo_v, si, so):
        @pl.when(pl.program_id(0) == 0)
        def _():
            _load_cols(wi_h, wi_v, si)
            _load_rows(wo_h, wo_v, so)
            st_ref[...] = jnp.zeros_like(st_ref)

        gam = g_ref[...]
        u = u_ref[...]
        if last:
            xhat0, _ = _ln_stats(u)
            err = xhat0 * gam + b_ref[...] - dy_ref[...]
            dy = err * (1.0 / D)
            st_ref[2:3, :] += jnp.sum(err * err, axis=0, keepdims=True)
        else:
            dy = dy_ref[...]
        du, dgam, dbet, _ = _ln_bwd(u, dy, gam)
        st_ref[0:1, :] += dgam
        st_ref[1:2, :] += dbet
        dub = du.astype(_MXU)
        dub_ref[...] = dub.astype(dub_ref.dtype)
        dx = alpha * du
        for c in range(2):
            dact = _dot_nt(dub, wo_v[c * fc:(c + 1) * fc, :])
            g = gu_ref[:, c * fc:(c + 1) * fc].astype(_F32)
            uu = gu_ref[:, F + c * fc:F + (c + 1) * fc].astype(_F32)
            sg = jax.nn.sigmoid(g)
            dg = (dact * uu * (sg * (1.0 + g * (1.0 - sg)))).astype(_MXU)
            dup = (dact * (g * sg)).astype(_MXU)
            dgu_ref[:, c * fc:(c + 1) * fc] = dg.astype(dgu_ref.dtype)
            dgu_ref[:, F + c * fc:F + (c + 1) * fc] = dup.astype(dgu_ref.dtype)
            dx = dx + _dot_nt(dg, wi_v[:, c * fc:(c + 1) * fc]) + _dot_nt(dup, wi_v[:, F + c * fc:F + (c + 1) * fc])
        dx_ref[...] = dx

    return pl.pallas_call(
        kern, name=f"ffn_bwd_{layer}", grid=(T // tm,),
        in_specs=[_rows(tm, D), _rows(tm, D), _rows(tm, F2), _whole((1, D)), _whole((1, D)), _ANY, _ANY],
        out_specs=[_rows(tm, D), _rows(tm, D), _rows(tm, F2), _whole((8, D))],
        out_shape=[jax.ShapeDtypeStruct((T, D), _F32), jax.ShapeDtypeStruct((T, D), _ACT),
                   jax.ShapeDtypeStruct((T, F2), _ACT), jax.ShapeDtypeStruct((8, D), _F32)],
        scratch_shapes=[pltpu.VMEM((D, F2), wfi.dtype), pltpu.VMEM((F, D), wfo.dtype),
                        pltpu.SemaphoreType.DMA((4,)), pltpu.SemaphoreType.DMA((4,))],
        compiler_params=_cparams("arbitrary"),
    )(u2, dy_or_target, gu, gamma, beta, wfi, wfo)


def _residual_nt(res, res_scale, pieces, w, name, comm=None):
    T, K = res.shape
    widths = [p.shape[1] for p in pieces]
    N = sum(widths)
    tm = 512

    def kern(r_ref, *refs):
        d_refs, (w_hbm, o_ref, w_v, sem) = refs[:len(pieces)], refs[len(pieces):]

        @pl.when(pl.program_id(0) == 0)
        def _():
            _load_cols(w_hbm, w_v, sem)

        acc = res_scale * r_ref[...]
        off = 0
        for d_ref, width in zip(d_refs, widths):
            acc = acc + _dot_nt(d_ref[...].astype(_MXU), w_v[:, off:off + width])
            off += width
        o_ref[...] = acc

    outs, extra = _call(
        kern, comm, name=name, grid=(T // tm,),
        in_specs=[_rows(tm, K)] + [_rows(tm, width) for width in widths] + [_ANY], out_specs=[_rows(tm, K)],
        out_shape=[jax.ShapeDtypeStruct((T, K), _F32)],
        scratch_shapes=[pltpu.VMEM((K, N), w.dtype), pltpu.SemaphoreType.DMA((4,))],
        args=(res, *pieces, w), semantics=("arbitrary",))
    return outs[0], extra


def _grad_w_pieces(a, pieces, name, comm=None):
    T, M = a.shape
    widths = [p.shape[1] for p in pieces]
    offsets = [sum(widths[:p]) for p in range(len(pieces))]
    N = sum(widths)
    tk = 1024 if T % 1024 == 0 else 512
    nk = T // tk

    def fits(ow):
        inside = lambda off, width: width < ow and off // ow == (off + width - 1) // ow
        whole = lambda off, width: off % ow == 0 and width % ow == 0
        return (N % ow == 0 and ow % _LANES == 0 and M * ow * 4 <= _GRAD_ACC_BYTES
                and all(inside(o, w) or whole(o, w) for o, w in zip(offsets, widths)))

    ow = next(c for c in (N // 2, 1024, 512, 256, _LANES) if fits(c))

    def kern(a_ref, *refs):
        b_refs, (o_ref, acc) = refs[:len(pieces)], refs[len(pieces):]
        j, k = pl.program_id(0), pl.program_id(1)

        @pl.when(k == 0)
        def _():
            acc[...] = jnp.zeros_like(acc)

        for b_ref, off, width in zip(b_refs, offsets, widths):
            if width < ow:
                @pl.when(j == off // ow)
                def _():
                    acc[:, off % ow:off % ow + width] += _dot_tn(a_ref[...].astype(_MXU), b_ref[...].astype(_MXU))
            else:
                @pl.when(jnp.logical_and(j >= off // ow, j < (off + width) // ow))
                def _():
                    acc[...] += _dot_tn(a_ref[...].astype(_MXU), b_ref[...].astype(_MXU))

        @pl.when(k == nk - 1)
        def _():
            o_ref[...] = acc[...].astype(o_ref.dtype)

    def piece_spec(off, width):
        first, blocks = off // ow, max(width // ow, 1)

        def index(j, k):
            mine = jnp.logical_and(j >= first, j < first + blocks)
            return jnp.where(mine, k, 0), jnp.where(mine, j - first, 0)
        return pl.BlockSpec((tk, min(width, ow)), index)

    outs, extra = _call(
        kern, comm, name=name, grid=(N // ow, nk),
        in_specs=[pl.BlockSpec((tk, M), lambda j, k: (k, 0))] + [piece_spec(o, w) for o, w in zip(offsets, widths)],
        out_specs=[pl.BlockSpec((M, ow), lambda j, k: (0, j))],
        out_shape=[jax.ShapeDtypeStruct((M, N), _ACT)], scratch_shapes=[pltpu.VMEM((M, ow), _F32)],
        args=(a, *pieces), semantics=("parallel", "arbitrary"))
    return outs[0], extra


def _gcd(a, b):
    while b:
        a, b = b, a % b
    return a


def _mix_bwd(u1, dx1, oa, ob, h, wpa, wpb, wo, bg, gamma, layer):
    T, D = u1.shape
    WA, WB = wpa.shape[-2], wpb.shape[-2]
    tm = 512

    def kern(u_ref, dx_ref, oa_ref, ob_ref, hga_ref, hgb_ref, bg_ref, g_ref, wpa_h, wpb_h, wo_h,
             du_ref, dub_ref, dya_ref, dyb_ref, dhg_ref, doa_ref, dob_ref, st_ref,
             wpa_v, wpb_v, wo_v, sa, sb, so):
        @pl.when(pl.program_id(0) == 0)
        def _():
            _load_cols(wpa_h, wpa_v, sa)
            _load_cols(wpb_h, wpb_v, sb)
            _load_rows(wo_h, wo_v, so)
            st_ref[...] = jnp.zeros_like(st_ref)

        du, dgam, dbet, _ = _ln_bwd(u_ref[...], dx_ref[...], g_ref[...])
        st_ref[1:2, :D] += dgam
        st_ref[1:2, D:] += dbet
        du_ref[...] = du
        dub = du.astype(_MXU)
        dub_ref[...] = dub.astype(dub_ref.dtype)
        dpre = _dot_nt(dub, wo_v[...])
        bgv = bg_ref[...]
        ga = jax.nn.sigmoid(hga_ref[...].astype(_F32) + bgv[:, :D])
        gb = jax.nn.sigmoid(hgb_ref[...].astype(_F32) + bgv[:, D:])
        dya = (dpre * ga).astype(_MXU)
        dyb = (dpre * gb).astype(_MXU)
        dsa = dpre * _dot(oa_ref[...].astype(_MXU), wpa_v[...]) * (ga * (1.0 - ga))
        dsb = dpre * _dot(ob_ref[...].astype(_MXU), wpb_v[...]) * (gb * (1.0 - gb))
        st_ref[0:1, :D] += jnp.sum(dsa, axis=0, keepdims=True)
        st_ref[0:1, D:] += jnp.sum(dsb, axis=0, keepdims=True)
        dya_ref[...] = dya.astype(dya_ref.dtype)
        dyb_ref[...] = dyb.astype(dyb_ref.dtype)
        dhg_ref[:, :D] = dsa.astype(dhg_ref.dtype)
        dhg_ref[:, D:] = dsb.astype(dhg_ref.dtype)
        doa_ref[...] = _dot_nt(dya, wpa_v[...]).astype(doa_ref.dtype)
        dob_ref[...] = _dot_nt(dyb, wpb_v[...]).astype(dob_ref.dtype)

    return pl.pallas_call(
        kern, name=f"mix_bwd_{layer}", grid=(T // tm,),
        in_specs=[_rows(tm, D), _rows(tm, D), _rows(tm, WA), _rows(tm, WB), *_gate_specs(h, tm, D), _whole((1, 2 * D)),
                  _whole((1, D)), _ANY, _ANY, _ANY],
        out_specs=[_rows(tm, D)] * 4 + [_rows(tm, 2 * D), _rows(tm, WA), _rows(tm, WB), _whole((8, 2 * D))],
        out_shape=[jax.ShapeDtypeStruct((T, D), _F32)] + [jax.ShapeDtypeStruct((T, D), _ACT)] * 3
        + [jax.ShapeDtypeStruct((T, 2 * D), _ACT), jax.ShapeDtypeStruct((T, WA), _ACT),
           jax.ShapeDtypeStruct((T, WB), _ACT), jax.ShapeDtypeStruct((8, 2 * D), _F32)],
        scratch_shapes=[pltpu.VMEM((WA, D), wpa.dtype), pltpu.VMEM((WB, D), wpb.dtype), pltpu.VMEM((D, D), wo.dtype),
                        pltpu.SemaphoreType.DMA((4,)), pltpu.SemaphoreType.DMA((4,)), pltpu.SemaphoreType.DMA((4,))],
        compiler_params=_cparams("arbitrary"),
    )(u1, dx1, oa, ob, h, h, bg, gamma, wpa, wpb, wo)


def _grad_w(a, b, *, col_shards, name, comm=None):
    T, M = a.shape
    N = b.shape[1]
    tk = 1024 if T % 1024 == 0 else 512
    n = N // 4 if col_shards else N
    group = max(g for g in (1, 2, 4) if g == 1 or M * n * g * 4 <= _GRAD_ACC_BYTES)
    tn = n * group if col_shards else (N if M * N * 4 <= _GRAD_ACC_BYTES else _divisor_tile(N, _GRAD_ACC_BYTES // (4 * M)))
    nk = T // tk

    def kern(a_ref, b_ref, o_ref, acc):
        k = pl.program_id(1)

        @pl.when(k == 0)
        def _():
            acc[...] = jnp.zeros_like(acc)

        acc[...] += _dot_tn(a_ref[...].astype(_MXU), b_ref[...].astype(_MXU))

        @pl.when(k == nk - 1)
        def _():
            if col_shards:
                for s in range(group):
                    o_ref[s] = acc[:, s * n:(s + 1) * n].astype(o_ref.dtype)
            else:
                o_ref[...] = acc[...].astype(o_ref.dtype)

    if col_shards:
        out_spec = pl.BlockSpec((group, M, n), lambda j, k: (j, 0, 0))
        out_shape = jax.ShapeDtypeStruct((4, M, n), _ACT)
    else:
        out_spec = pl.BlockSpec((M, tn), lambda j, k: (0, j))
        out_shape = jax.ShapeDtypeStruct((M, N), _ACT)
    outs, extra = _call(
        kern, comm, name=name, grid=(N // tn, nk),
        in_specs=[pl.BlockSpec((tk, M), lambda j, k: (k, 0)), pl.BlockSpec((tk, tn), lambda j, k: (k, j))],
        out_specs=[out_spec], out_shape=[out_shape], scratch_shapes=[pltpu.VMEM((M, tn), _F32)],
        args=(a, b), semantics=("parallel", "arbitrary"))
    return outs[0], extra


def _bias_tiles(rel):
    H = rel.shape[0]
    span = _TQ * _BAND_TILES - 1
    edge = span - _REL_CLIP
    gvec = jnp.concatenate([jnp.broadcast_to(rel[:, :1], (H, edge)), rel, jnp.broadcast_to(rel[:, -1:], (H, edge))], axis=1)
    width = _BIAS_TILES * _TQ
    period = width + _TQ
    tiled = jnp.broadcast_to(jnp.pad(gvec[:, ::-1], ((0, 0), (0, 1)))[:, None, :], (H, _TQ, period))
    rows = tiled.reshape(H, _TQ * period)[:, :_TQ * (period - 1)].reshape(H, _TQ, period - 1)[:, :, _TQ - 1:]
    r = jnp.arange(_TQ)[:, None]
    u = jnp.arange(width)[None, :]
    d = 4 * _TQ + r - u
    rm = r % _CHUNK
    valid = (d >= rm - (_CHUNK - 1)) & (d <= rm + 8 * _CHUNK)
    tiles = jnp.where(valid[None], rows, _MASKED)
    return tiles.reshape(H // 2, 2 * _TQ, width)


def _strip_tiles(strip_ref, tiles_ref, to_strip=False):
    for ub in range(_BIAS_TILES):
        if to_strip:
            strip_ref[:, ub * _TQ:(ub + 1) * _TQ] = tiles_ref[ub]
        else:
            tiles_ref[ub] = strip_ref[:, ub * _TQ:(ub + 1) * _TQ]


def _fold_bias_grad(db):
    H = 2 * db.shape[0]
    width = _BIAS_TILES * _TQ
    period = width + _TQ
    x = jnp.pad(db.reshape(H, _TQ, width), ((0, 0), (0, 0), (_TQ - 1, 0)))
    skew = jnp.pad(x.reshape(H, _TQ * (period - 1)), ((0, 0), (0, _TQ))).reshape(H, _TQ, period)
    dg = skew.sum(axis=1)[:, :period - 1][:, ::-1]
    span = _TQ * _BAND_TILES - 1
    edge = span - _REL_CLIP
    mid = dg[:, edge:edge + 2 * _REL_CLIP + 1]
    lo = dg[:, :edge].sum(axis=1)
    hi = dg[:, edge + 2 * _REL_CLIP + 1:].sum(axis=1)
    return mid.at[:, 0].add(lo).at[:, -1].add(hi)


def _band_window(i):
    j0 = jnp.maximum(i - (_BAND_TILES - 1), 0)
    return j0, (_BAND_TILES - 1) - (i - j0)


def _head_masks():
    lane = lax.broadcasted_iota(jnp.int32, (1, _LANES), 1)
    return [(lane // _HEAD) == hh for hh in range(2)]


def _stack_heads(x, masks):
    return jnp.concatenate([jnp.where(m, x, jnp.zeros_like(x)) for m in masks], axis=0)


def _unstack_heads(y, masks):
    return jnp.where(masks[0], y[:_TQ], y[_TQ:])


def _scaled(q):
    return q * jnp.asarray(_HEAD ** -0.5, q.dtype)


def _band_probs(q2, k_ref, b_ref, j0, boff):
    s = []
    for j in range(_BAND_TILES):
        kj = k_ref[pl.ds(pl.multiple_of((j0 + j) * _TQ, _TQ), _TQ), :]
        s.append(_dot_nt(q2, kj) + b_ref[boff + j])
    m = jnp.max(functools.reduce(jnp.maximum, s), axis=-1, keepdims=True)
    p = [jnp.exp(x - m) for x in s]
    l = jnp.sum(functools.reduce(lambda a, b: a + b, p), axis=-1, keepdims=True)
    return p, 1.0 / l


def _qkv_specs(T, cb, npair, tq=_TQ):
    return [pl.BlockSpec((tq, _LANES), lambda h, i: (i, cb + h)),
            pl.BlockSpec((T, _LANES), lambda h, i: (0, cb + npair + h)),
            pl.BlockSpec((T, _LANES), lambda h, i: (0, cb + 2 * npair + h))]


def _attn_a_fwd(hq, bias, col0, width, layer, comm=None):
    T = hq.shape[0]
    npair = width // _LANES
    nsub = _BAND_SUBTILES
    tq = nsub * _TQ

    def kern(q_ref, k_ref, v_ref, b_ref, o_ref, b_tiles):
        @pl.when(pl.program_id(1) == 0)
        def _():
            _strip_tiles(b_ref, b_tiles)

        masks = _head_masks()
        q = _scaled(q_ref[...])
        for s in range(nsub):
            part = slice(s * _TQ, (s + 1) * _TQ)
            j0, boff = _band_window(nsub * pl.program_id(1) + s)
            p, inv = _band_probs(_stack_heads(q[part], masks), k_ref, b_tiles, j0, boff)
            o = jnp.zeros((2 * _TQ, _LANES), _F32)
            for j in range(_BAND_TILES):
                vj = v_ref[pl.ds(pl.multiple_of((j0 + j) * _TQ, _TQ), _TQ), :]
                o = o + _dot(p[j].astype(_MXU), vj)
            o_ref[part, :] = _unstack_heads(o * inv, masks).astype(o_ref.dtype)

    outs, extra = _call(
        kern, comm, name=f"band_attn_fwd_{layer}", grid=(npair, T // tq),
        in_specs=_qkv_specs(T, col0 // _LANES, npair, tq)
        + [pl.BlockSpec((None, 2 * _TQ, _BIAS_TILES * _TQ), lambda h, i: (h, 0, 0))],
        out_specs=[pl.BlockSpec((tq, _LANES), lambda h, i: (i, h))],
        out_shape=[jax.ShapeDtypeStruct((T, width), _ACT)],
        scratch_shapes=[pltpu.VMEM((_BIAS_TILES, 2 * _TQ, _TQ), _F32)],
        args=(hq, hq, hq, bias), semantics=("arbitrary", "arbitrary"))
    return outs[0], extra


def _attn_a_bwd(hq, bias, do, col0, width, layer, comm=None):
    T = hq.shape[0]
    npair = width // _LANES
    nsub = _BAND_SUBTILES
    tq = nsub * _TQ
    nq = T // tq
    scale = _HEAD ** -0.5

    def kern(q_ref, k_ref, v_ref, b_ref, do_ref, dq_ref, dk_ref, dv_ref, db_ref, dk_acc, dv_acc, b_tiles, db_acc):
        i = pl.program_id(1)

        @pl.when(i == 0)
        def _():
            _strip_tiles(b_ref, b_tiles)
            dk_acc[...] = jnp.zeros_like(dk_acc)
            dv_acc[...] = jnp.zeros_like(dv_acc)
            db_acc[...] = jnp.zeros_like(db_acc)

        masks = _head_masks()
        q = _scaled(q_ref[...])
        do_t = do_ref[...]
        for s in range(nsub):
            part = slice(s * _TQ, (s + 1) * _TQ)
            j0, boff = _band_window(nsub * i + s)
            q2 = _stack_heads(q[part], masks)
            do2 = _stack_heads(do_t[part], masks).astype(_MXU)
            p, inv = _band_probs(q2, k_ref, b_tiles, j0, boff)
            rows = [pl.ds(pl.multiple_of((j0 + j) * _TQ, _TQ), _TQ) for j in range(_BAND_TILES)]
            p = [x * inv for x in p]
            dp = [_dot_nt(do2, v_ref[rows[j], :]) for j in range(_BAND_TILES)]
            delta = jnp.sum(functools.reduce(lambda a, b: a + b, [p[j] * dp[j] for j in range(_BAND_TILES)]),
                            axis=-1, keepdims=True)
            dq = jnp.zeros((2 * _TQ, _LANES), _F32)
            for j in range(_BAND_TILES):
                ds = p[j] * (dp[j] - delta)
                db_acc[boff + j] += ds
                dsb = ds.astype(_MXU)
                dq = dq + _dot(dsb, k_ref[rows[j], :])
                dk_acc[rows[j], :] += _dot_tn(dsb, q2)
                dv_acc[rows[j], :] += _dot_tn(p[j].astype(_MXU), do2)
            dq_ref[part, :] = (_unstack_heads(dq, masks) * scale).astype(dq_ref.dtype)

        @pl.when(i == nq - 1)
        def _():
            dk_ref[...] = dk_acc[...].astype(dk_ref.dtype)
            dv_ref[...] = dv_acc[...].astype(dv_ref.dtype)
            _strip_tiles(db_ref, db_acc, to_strip=True)

    strip = pl.BlockSpec((None, 2 * _TQ, _BIAS_TILES * _TQ), lambda h, i: (h, 0, 0))
    tile = pl.BlockSpec((tq, _LANES), lambda h, i: (i, h))
    column = pl.BlockSpec((T, _LANES), lambda h, i: (0, h))
    outs, extra = _call(
        kern, comm, name=f"band_attn_bwd_{layer}", grid=(npair, nq),
        in_specs=_qkv_specs(T, col0 // _LANES, npair, tq) + [strip, tile],
        out_specs=[tile, column, column, strip],
        out_shape=[jax.ShapeDtypeStruct((T, width), _ACT)] * 3
        + [jax.ShapeDtypeStruct((npair, 2 * _TQ, _BIAS_TILES * _TQ), _F32)],
        scratch_shapes=[pltpu.VMEM((T, _LANES), _F32), pltpu.VMEM((T, _LANES), _F32),
                        pltpu.VMEM((_BIAS_TILES, 2 * _TQ, _TQ), _F32), pltpu.VMEM((_BIAS_TILES, 2 * _TQ, _TQ), _F32)],
        args=(hq, hq, hq, bias, do), semantics=("arbitrary", "arbitrary"))
    return outs, extra


def _suffix_matrix():
    r = lax.broadcasted_iota(jnp.int32, (_TQ, _TQ), 0)
    c = lax.broadcasted_iota(jnp.int32, (_TQ, _TQ), 1)
    r2 = lax.broadcasted_iota(jnp.int32, (2 * _TQ, _TQ), 0)
    c2 = lax.broadcasted_iota(jnp.int32, (2 * _TQ, _TQ), 1)
    return (r > c).astype(_MXU), c2 - (r2 & (_TQ - 1))


def _suffix_sums(xs, tri):
    n, k = xs[0].shape[0], len(xs)
    his = [x.astype(_MXU) for x in xs]
    los = [(x - h.astype(_F32)).astype(_MXU) for x, h in zip(xs, his)]
    y = _dot(jnp.concatenate(his + los, axis=0), tri)
    return [y[j * n:(j + 1) * n] + y[(k + j) * n:(k + j + 1) * n] for j in range(k)]


def _stick_tiles(tiles, rel, carry_l, tri):
    zs = [_dot_nt(qs, kj) for qs, kj, _, _ in tiles]
    Ls, masks = [], []
    for z, (_, _, jj, _) in zip(zs, tiles):
        nsp = -(jnp.maximum(z, 0.0) + jnp.log(1.0 + jnp.exp(-jnp.abs(z))))
        if isinstance(jj, int):
            mask = (rel < 0) if jj == 0 else None
        else:
            mask = rel < jnp.where(jj == 0, 0, _TQ)
        Ls.append(nsp if mask is None else jnp.where(mask, nsp, 0.0))
        masks.append(mask)
    carry_l = list(carry_l)
    ws = []
    for z, L, suffix, mask, (_, _, _, sub) in zip(zs, Ls, _suffix_sums(Ls, tri), masks, tiles):
        w = jnp.exp(z + L + suffix + carry_l[sub])
        ws.append(w if mask is None else jnp.where(mask, w, 0.0))
        carry_l[sub] = carry_l[sub] + jnp.sum(L, axis=-1, keepdims=True)
    return zs, Ls, ws, masks, carry_l


def _sweep(i, step, zero):
    nsub = _SB_SUBTILES

    def window():
        tiles = [(s, jj) for jj in range(_SB_WINDOW) for s in range(nsub)]
        return tuple((jnp.int32(_SB_WINDOW),) + c for c in step(tiles, [zero] * nsub))

    start = lax.cond(i >= -(-(_SB_WINDOW - 1) // nsub), window, lambda: tuple((jnp.int32(0),) + zero for _ in range(nsub)))
    outs = []
    for s in range(nsub):
        def done(c, s=s):
            return jnp.logical_or(c[0] > nsub * i + s, jnp.max(c[1]) < _EXP_ZERO_BELOW)

        def more(c, s=s):
            carries = [None] * nsub
            carries[s] = c[1:]
            return (c[0] + 1,) + step([(s, c[0])], carries)[s]

        outs.append(lax.while_loop(lambda c, done=done: jnp.logical_not(done(c)), more, start[s]))
    return outs


def _sb_fwd(hq, col0, width, layer, comm=None):
    T = hq.shape[0]
    npair = width // _LANES
    nsub = _SB_SUBTILES
    tq = nsub * _TQ

    def kern(q_ref, k_ref, v_ref, o_ref):
        i = pl.program_id(1)
        masks = _head_masks()
        tri, rel = _suffix_matrix()
        q = _scaled(q_ref[...])
        q2 = [_stack_heads(q[s * _TQ:(s + 1) * _TQ], masks) for s in range(nsub)]

        def step(tiles, carries):
            rows = [pl.ds(pl.multiple_of((nsub * i + s - jj) * _TQ, _TQ), _TQ) for s, jj in tiles]
            cls = [None if c is None else c[0] for c in carries]
            accs = [None if c is None else c[1] for c in carries]
            _, _, ws, _, cls = _stick_tiles([(q2[s], k_ref[r, :], jj, s) for (s, jj), r in zip(tiles, rows)], rel, cls, tri)
            for w, r, (s, _) in zip(ws, rows, tiles):
                accs[s] = accs[s] + _dot(w.astype(_MXU), v_ref[r, :])
            return [None if c is None else (cls[s], accs[s]) for s, c in enumerate(carries)]

        outs = _sweep(i, step, (jnp.zeros((2 * _TQ, 1), _F32), jnp.zeros((2 * _TQ, _LANES), _F32)))
        for s in range(nsub):
            o_ref[s * _TQ:(s + 1) * _TQ, :] = _unstack_heads(outs[s][2], masks)

    outs, extra = _call(
        kern, comm, name=f"stick_attn_fwd_{layer}", grid=(npair, T // tq),
        in_specs=_qkv_specs(T, col0 // _LANES, npair, tq),
        out_specs=[pl.BlockSpec((tq, _LANES), lambda h, i: (i, h))],
        out_shape=[jax.ShapeDtypeStruct((T, width), _F32)], scratch_shapes=[],
        args=(hq, hq, hq), semantics=("arbitrary", "arbitrary"))
    return outs[0], extra


def _sb_bwd(hq, o, do, col0, width, layer, comm=None):
    T = hq.shape[0]
    npair = width // _LANES
    nsub = _SB_SUBTILES
    tq = nsub * _TQ
    nq = T // tq
    scale = _HEAD ** -0.5

    def kern(q_ref, k_ref, v_ref, o_ref, do_ref, dq_ref, dk_ref, dv_ref, dk_acc, dv_acc):
        i = pl.program_id(1)

        @pl.when(i == 0)
        def _():
            dk_acc[...] = jnp.zeros_like(dk_acc)
            dv_acc[...] = jnp.zeros_like(dv_acc)

        masks = _head_masks()
        tri, rel = _suffix_matrix()
        q = _scaled(q_ref[...])
        do_t = do_ref[...]
        prod = do_t.astype(_F32) * o_ref[...]
        part = [slice(s * _TQ, (s + 1) * _TQ) for s in range(nsub)]
        q2 = [_stack_heads(q[p], masks) for p in part]
        do2 = [_stack_heads(do_t[p], masks).astype(_MXU) for p in part]
        dsum = [jnp.sum(_stack_heads(prod[p], masks), axis=-1, keepdims=True) for p in part]

        def step(tiles, carries):
            rows = [pl.ds(pl.multiple_of((nsub * i + s - jj) * _TQ, _TQ), _TQ) for s, jj in tiles]
            kjs = [k_ref[r, :] for r in rows]
            cls, cgs, dqs = ([None if c is None else c[n] for c in carries] for n in range(3))
            zs, Ls, ws, tile_masks, cls = _stick_tiles([(q2[s], kj, jj, s) for (s, jj), kj in zip(tiles, kjs)], rel, cls, tri)
            wbs = [w.astype(_MXU) for w in ws]
            gs = [wb.astype(_F32) * _dot_nt(do2[s], v_ref[r, :]) for wb, r, (s, _) in zip(wbs, rows, tiles)]
            for z, L, g, later, mask, wb, kj, r, (s, _) in zip(zs, Ls, gs, _suffix_sums(gs, tri), tile_masks, wbs, kjs,
                                                               rows, tiles):
                dz = g - jnp.exp(z + L) * (dsum[s] - (later + cgs[s]))
                if mask is not None:
                    dz = jnp.where(mask, dz, 0.0)
                dzb = dz.astype(_MXU)
                dk_acc[r, :] += _dot_tn(dzb, q2[s])
                dv_acc[r, :] += _dot_tn(wb, do2[s])
                dqs[s] = dqs[s] + _dot(dzb, kj)
                cgs[s] = cgs[s] + jnp.sum(g, axis=-1, keepdims=True)
            return [None if c is None else (cls[s], cgs[s], dqs[s]) for s, c in enumerate(carries)]

        zc = jnp.zeros((2 * _TQ, 1), _F32)
        outs = _sweep(i, step, (zc, zc, jnp.zeros((2 * _TQ, _LANES), _F32)))
        for s in range(nsub):
            dq_ref[part[s], :] = (_unstack_heads(outs[s][3], masks) * scale).astype(dq_ref.dtype)

        @pl.when(i == nq - 1)
        def _():
            dk_ref[...] = dk_acc[...].astype(dk_ref.dtype)
            dv_ref[...] = dv_acc[...].astype(dv_ref.dtype)

    tile_spec = pl.BlockSpec((tq, _LANES), lambda h, i: (i, h))
    column = pl.BlockSpec((T, _LANES), lambda h, i: (0, h))
    outs, extra = _call(
        kern, comm, name=f"stick_attn_bwd_{layer}", grid=(npair, nq),
        in_specs=_qkv_specs(T, col0 // _LANES, npair, tq) + [tile_spec, tile_spec],
        out_specs=[tile_spec, column, column],
        out_shape=[jax.ShapeDtypeStruct((T, width), _ACT)] * 3,
        scratch_shapes=[pltpu.VMEM((T, _LANES), _F32), pltpu.VMEM((T, _LANES), _F32)],
        args=(hq, hq, hq, o, do), semantics=("arbitrary", "arbitrary"))
    return outs, extra


_DENSE = ("w_in", "w_proj_a", "w_proj_b", "w_out", "w_ffn_in", "w_ffn_out")
_COL_SHARDED = {"w_in": True, "w_proj_a": True, "w_proj_b": True, "w_out": False, "w_ffn_in": True, "w_ffn_out": False}
_SMALL = ("b_gate", "rel_bias", "ln1_g", "ln1_b", "ln2_g", "ln2_b")


class _Plans:
    def __init__(self, plans=None, own_w_in=None):
        self.plans = plans or {}
        self.own_w_in = own_w_in or {}

    def start(self, key):
        if key not in self.plans:
            return None, None
        return self.plans[key]()

    @staticmethod
    def finish(done, extra):
        if done is not None:
            done(extra)


def _layer_fwd(x, W, small, l, alpha, plans):
    WA = small["rel_bias"].shape[1] * _HEAD
    row = lambda v: v[l].reshape(1, -1)
    if l in plans.own_w_in:
        h, xb, W["w_in"] = _in_proj_gathering(x, plans.own_w_in[l], l)
    else:
        comm, done = plans.start(f"in_proj_{l}")
        (h, xb), extra = _in_proj(x, W["w_in"], l, comm)
        plans.finish(done, extra)
    WB = (h.shape[1] - 2 * x.shape[1] - 3 * WA) // 3
    bias = _bias_tiles(small["rel_bias"][l])
    comm, done = plans.start(f"band_fwd_{l}")
    oa, extra = _attn_a_fwd(h, bias, 0, WA, l, comm)
    plans.finish(done, extra)
    comm, done = plans.start(f"stick_fwd_{l}")
    ob, extra = _sb_fwd(h, 3 * WA, WB, l, comm)
    plans.finish(done, extra)
    comm, done = plans.start(f"mix_fwd_{l}")
    (x1, u1, pre), extra = _mix_fwd(oa, ob, h, x, W["w_proj_a"], W["w_proj_b"], W["w_out"], row(small["b_gate"]),
                                            row(small["ln1_g"]), row(small["ln1_b"]), alpha, l, comm)
    plans.finish(done, extra)
    comm, done = plans.start(f"ffn_fwd_{l}")
    (x2, u2, act, gu, x1b), extra = _ffn_fwd(x1, W["w_ffn_in"], W["w_ffn_out"], row(small["ln2_g"]),
                                             row(small["ln2_b"]), alpha, l, comm)
    plans.finish(done, extra)
    return x2, dict(xb=xb, h=h, bias=bias, oa=oa, ob=ob, x1b=x1b, u1=u1, pre=pre, u2=u2, act=act, gu=gu)


def _layer_bwd(dy_or_target, S, W, small, l, last, alpha, plans, gw):
    D = S["xb"].shape[1]
    WA, WB = S["oa"].shape[1], S["ob"].shape[1]
    row = lambda v: v[l].reshape(1, -1)

    def blocks(g, n):
        return g if _COL_SHARDED[n] else g.reshape(4, g.shape[0] // 4, g.shape[1])

    dx1, du2b, dgu, st2 = _ffn_bwd(S["u2"], dy_or_target, S["gu"], row(small["ln2_g"]), row(small["ln2_b"]),
                                   W["w_ffn_in"], W["w_ffn_out"], alpha, l, last)
    gw["w_ffn_in"] = blocks(_grad_w(S["x1b"], dgu, col_shards=True, name=f"grad_w_ffn_in_{l}")[0], "w_ffn_in")
    gw["w_ffn_out"] = blocks(_grad_w(S["act"], du2b, col_shards=False, name=f"grad_w_ffn_out_{l}")[0], "w_ffn_out")
    du1, du1b, dya, dyb, dhg, doa, dob, st1 = _mix_bwd(S["u1"], dx1, S["oa"], S["ob"], S["h"], W["w_proj_a"],
                                                       W["w_proj_b"], W["w_out"], row(small["b_gate"]),
                                                       row(small["ln1_g"]), l)
    gw["w_out"] = blocks(_grad_w(S["pre"], du1b, col_shards=False, name=f"grad_w_out_{l}")[0], "w_out")
    gw["w_proj_a"] = blocks(_grad_w(S["oa"], dya, col_shards=True, name=f"grad_w_proj_a_{l}")[0], "w_proj_a")
    gw["w_proj_b"] = blocks(_grad_w(S["ob"], dyb, col_shards=True, name=f"grad_w_proj_b_{l}")[0], "w_proj_b")
    comm, done = plans.start(f"band_bwd_{l}")
    (dqa, dka, dva, dbias), extra = _attn_a_bwd(S["h"], S["bias"], doa, 0, WA, l, comm)
    plans.finish(done, extra)
    comm, done = plans.start(f"stick_bwd_{l}")
    (dqb, dkb, dvb), extra = _sb_bwd(S["h"], S["ob"], dob, 3 * WA, WB, l, comm)
    plans.finish(done, extra)
    dh = [dqa, dka, dva, dqb, dkb, dvb, dhg]
    comm, done = plans.start(f"grad_w_in_{l}")
    gw["w_in"], extra = _grad_w_pieces(S["xb"], dh, f"grad_w_in_{l}", comm)
    plans.finish(done, extra)
    comm, done = plans.start(f"in_proj_bwd_{l}")
    dx, extra = _residual_nt(du1, alpha, dh, W["w_in"], f"in_proj_bwd_{l}", comm)
    plans.finish(done, extra)
    gs = dict(b_gate=st1[0], rel_bias=_fold_bias_grad(dbias), ln1_g=st1[1, :D], ln1_b=st1[1, D:],
              ln2_g=st2[0], ln2_b=st2[1])
    return dx, gs, st2[2]


def _local_step(x, target, W, small, plans=None, gws=None):
    depth = len(W)
    alpha = float((2 * depth) ** 0.25)
    plans = plans or _Plans()
    gws = gws if gws is not None else [dict() for _ in range(depth)]
    saved = []
    h = x
    for l in range(depth):
        h, S = _layer_fwd(h, W[l], small, l, alpha, plans)
        saved.append(S)
    gss = [None] * depth
    d = target
    sq = None
    for l in reversed(range(depth)):
        d, gss[l], sq_l = _layer_bwd(d, saved[l], W[l], small, l, l == depth - 1, alpha, plans, gws[l])
        if l == depth - 1:
            sq = sq_l
    return sq, d, gws, gss


def _place():
    return lax.axis_index("x"), lax.axis_index("y"), lax.axis_index("c")


def _remote(src, dst, send_sem, recv_sem, to):
    return pltpu.make_async_remote_copy(src_ref=src, dst_ref=dst, send_sem=send_sem, recv_sem=recv_sem,
                                        device_id=to, device_id_type=_MESH)


def _half(ref, hc):
    kh = ref.shape[0] // 2
    return ref.at[pl.ds(pl.multiple_of(hc * kh, 16), kh), :]


def _gather_plan(blocks, fractions):
    nt = len(blocks)

    def run(step, nsteps, ins, outs, sems):
        send_sems, recv_sems, loc_sems = sems
        x, y, c = _place()
        k = 2 * x + y
        me, sibling = (x, y, c), (x, y, 1 - c)
        chips = [(1 - x, y), (x, 1 - y), (1 - x, 1 - y)]
        chip_k = [2 * cx + cy for cx, cy in chips]

        def ici(t, s, owner_k, to, src=None):
            dst = _half(outs[t].at[owner_k], c)
            return _remote(dst if src is None else src, dst, send_sems.at[t, s], recv_sems.at[t, s], to)

        def passed(t, s, hc, to):
            blk = _half(outs[t].at[chip_k[s]], hc)
            return _remote(blk, blk, send_sems.at[t, 3 + s], recv_sems.at[t, 3 + s], to)

        def local(t):
            return pltpu.make_async_copy(ins[t], outs[t].at[k], loc_sems.at[t])

        @pl.when(step == 0)
        def _():
            for t in range(nt):
                local(t).start()
                for s, chip in enumerate(chips):
                    ici(t, s, k, (*chip, c), src=_half(ins[t], c)).start()

        for t in range(nt):
            @pl.when(step == min(nsteps - 1, int(fractions[t] * nsteps)))
            def _():
                for s in range(3):
                    ici(t, s, chip_k[s], me).wait_recv()
                    passed(t, s, c, sibling).start()

        @pl.when(step == nsteps - 1)
        def _():
            for t in range(nt):
                for s, chip in enumerate(chips):
                    passed(t, s, 1 - c, me).wait_recv()
            for t in range(nt):
                for s, chip in enumerate(chips):
                    ici(t, s, k, (*chip, c), src=_half(ins[t], c)).wait_send()
                    passed(t, s, c, sibling).wait_send()
                local(t).wait()

    return _Comm(blocks, [jax.ShapeDtypeStruct((4,) + b.shape, b.dtype) for b in blocks],
                 [pltpu.SemaphoreType.DMA((nt, 6)), pltpu.SemaphoreType.DMA((nt, 6)), pltpu.SemaphoreType.DMA((nt,))], run)


def _scatter_plan(grads, owners):
    nt = len(grads)
    shapes = [g.shape[1:] if g.ndim == 3 else (g.shape[0], g.shape[1] // 4) for g in grads]

    def run(step, nsteps, ins, outs, sems):
        send_sems, recv_sems, loc_sems = sems
        x, y, c = _place()
        me = 4 * x + 2 * y + c

        def target(r):
            tx = 1 - x if r & 2 else x
            ty = 1 - y if r & 1 else y
            return tx, ty

        def block(t, chip):
            if len(ins[t].shape) == 3:
                return ins[t].at[chip]
            n = shapes[t][1]
            return ins[t].at[:, pl.ds(pl.multiple_of(chip * n, _LANES), n)]

        def send(t, r):
            tx, ty = target(r)
            return _remote(block(t, 2 * tx + ty), outs[t].at[me], send_sems.at[t, r], recv_sems.at[t, 2 * r + c],
                           (tx, ty, owners[t]))

        def local(t):
            return pltpu.make_async_copy(block(t, 2 * x + y), outs[t].at[me], loc_sems.at[t])

        @pl.when(step == 0)
        def _():
            for t in range(nt):
                @pl.when(c == owners[t])
                def _():
                    local(t).start()

                @pl.when(c != owners[t])
                def _():
                    send(t, 0).start()

                for r in range(1, 4):
                    send(t, r).start()

        @pl.when(step == nsteps - 1)
        def _():
            for t in range(nt):
                @pl.when(c == owners[t])
                def _():
                    for r in range(4):
                        sx, sy = target(r)
                        for cs in range(2):
                            if r == 0 and cs == owners[t]:
                                continue
                            src_dev = 4 * sx + 2 * sy + cs
                            _remote(block(t, 0), outs[t].at[src_dev], send_sems.at[t, r], recv_sems.at[t, 2 * r + cs],
                                    (x, y, c)).wait_recv()
                    local(t).wait()

                @pl.when(c != owners[t])
                def _():
                    send(t, 0).wait_send()

                for r in range(1, 4):
                    send(t, r).wait_send()

    return _Comm(grads, [jax.ShapeDtypeStruct((8,) + s, g.dtype) for s, g in zip(shapes, grads)],
                 [pltpu.SemaphoreType.DMA((nt, 4)), pltpu.SemaphoreType.DMA((nt, 8)), pltpu.SemaphoreType.DMA((nt,))], run)


def _share_plan(reduced, owners):
    nt = len(reduced)

    def run(step, nsteps, ins, outs, sems):
        del ins
        send_sems, recv_sems = sems
        x, y, c = _place()

        def give(t, to):
            return _remote(outs[t], outs[t], send_sems.at[t], recv_sems.at[t], to)

        @pl.when(step == 0)
        def _():
            for t in range(nt):
                @pl.when(c == owners[t])
                def _():
                    give(t, (x, y, 1 - c)).start()

        @pl.when(step == nsteps - 1)
        def _():
            for t in range(nt):
                @pl.when(c == owners[t])
                def _():
                    give(t, (x, y, 1 - c)).wait_send()

                @pl.when(c != owners[t])
                def _():
                    give(t, (x, y, c)).wait_recv()

    return _Comm(reduced, [jax.ShapeDtypeStruct(r.shape, r.dtype) for r in reduced],
                 [pltpu.SemaphoreType.DMA((nt,)), pltpu.SemaphoreType.DMA((nt,))], run,
                 aliases={t: t for t in range(nt)})


def _join(a, b):
    ni, no, ns = len(a.inputs), len(a.out_shapes), len(a.sems)

    def run(step, nsteps, ins, outs, sems):
        a.run(step, nsteps, ins[:ni], outs[:no], sems[:ns])
        b.run(step, nsteps, ins[ni:], outs[no:], sems[ns:])

    aliases = dict(a.aliases)
    aliases.update({ni + i: no + o for i, o in b.aliases.items()})
    return _Comm(a.inputs + b.inputs, a.out_shapes + b.out_shapes, a.sems + b.sems, run, aliases)


def _peer(x, y, c, r):
    px = 1 - x if r & 4 else x
    py = 1 - y if r & 2 else y
    pc = 1 - c if r & 1 else c
    return (px, py, pc), 4 * px + 2 * py + pc


def _row_tile(rows, cap=512):
    return max((t for t in range(16, min(rows, cap) + 1, 16) if rows % t == 0), default=rows)


def _sum_slots(st, name):
    _, K, n = st.shape
    tr = _row_tile(K)

    def kern(s_ref, o_ref):
        acc = s_ref[0].astype(_F32)
        for d in range(1, 8):
            acc = acc + s_ref[d].astype(_F32)
        o_ref[...] = acc.astype(o_ref.dtype)

    return pl.pallas_call(
        kern, name=name, grid=(K // tr,),
        in_specs=[pl.BlockSpec((8, tr, n), lambda i: (0, i, 0))], out_specs=_rows(tr, n),
        out_shape=jax.ShapeDtypeStruct((K, n), _ACT),
        compiler_params=_cparams("parallel"),
    )(st)


def _all_reduce_small(p):
    R = p.shape[0]

    def body(p_ref, o_ref, stage, send_sems, recv_sems):
        x, y, c = _place()
        me = 4 * x + 2 * y + c
        stage[me] = p_ref[...]
        sent = []
        for r in range(1, 8):
            to, _ = _peer(x, y, c, r)
            cp = _remote(p_ref, stage.at[me], send_sems.at[r - 1], recv_sems.at[r - 1], to)
            cp.start()
            sent.append(cp)
        for r in range(1, 8):
            _, src_dev = _peer(x, y, c, r)
            _remote(p_ref, stage.at[src_dev], send_sems.at[r - 1], recv_sems.at[r - 1], (x, y, c)).wait_recv()
        acc = stage[0]
        for d in range(1, 8):
            acc = acc + stage[d]
        o_ref[...] = acc
        for cp in sent:
            cp.wait_send()

    vm = pl.BlockSpec(memory_space=pltpu.VMEM)
    return pl.pallas_call(
        body, name="all_reduce_small",
        in_specs=[vm], out_specs=vm,
        out_shape=jax.ShapeDtypeStruct((R, _LANES), _F32),
        scratch_shapes=[pltpu.VMEM((8, R, _LANES), _F32), pltpu.SemaphoreType.DMA((7,)), pltpu.SemaphoreType.DMA((7,))],
    )(p)


def _adamw_update(gv, w_ref, m_ref, v_ref, gf_ref, d_ref, nm_ref, nv_ref):
    nm = _B1 * m_ref[...] + (1.0 - _B1) * gv
    nv = _B2 * v_ref[...] + (1.0 - _B2) * (gv * gv)
    m_hat = nm / (1.0 - _B1 ** _STEP)
    v_hat = nv / (1.0 - _B2 ** _STEP)
    gf_ref[...] = gv
    d_ref[...] = -_LR * (m_hat / (jnp.sqrt(v_hat) + _EPS) + _WD * w_ref[...])
    nm_ref[...] = nm
    nv_ref[...] = nv


def _adamw_layers(w, g_layers, m, v, name):
    _, K, n = w.shape
    tr = _row_tile(K)

    def kern(w_ref, g0_ref, g1_ref, m_ref, v_ref, *out_refs):
        first = pl.program_id(0) == 0
        gv = jnp.where(first, g0_ref[...].astype(_F32), g1_ref[...].astype(_F32))
        _adamw_update(gv, w_ref, m_ref, v_ref, *out_refs)

    stacked = pl.BlockSpec((None, tr, n), lambda l, i: (l, i, 0))
    layer = pl.BlockSpec((tr, n), lambda l, i: (i, 0))
    return tuple(pl.pallas_call(
        kern, name=name, grid=(2, K // tr),
        in_specs=[stacked, layer, layer, stacked, stacked], out_specs=[stacked] * 4,
        out_shape=[jax.ShapeDtypeStruct(w.shape, _F32)] * 4,
        compiler_params=_cparams("parallel", "parallel"),
    )(w, g_layers[0], g_layers[1], m, v))


def _adamw(w, g, m, v, name):
    shape = w.shape
    w2, g2, m2, v2 = (a.reshape(-1, shape[-1]) for a in (w, g, m, v))
    R, C = w2.shape
    tr = next((t for t in (256, 128, 64, 32, 16) if R % t == 0), R)

    def kern(w_ref, g_ref, m_ref, v_ref, *out_refs):
        _adamw_update(g_ref[...].astype(_F32), w_ref, m_ref, v_ref, *out_refs)

    outs = pl.pallas_call(
        kern, name=name, grid=(R // tr,),
        in_specs=[_rows(tr, C)] * 4, out_specs=[_rows(tr, C)] * 4,
        out_shape=[jax.ShapeDtypeStruct((R, C), _F32)] * 4,
        compiler_params=_cparams("parallel"),
    )(w2, g2, m2, v2)
    return tuple(o.reshape(shape) for o in outs)


def _pack_small(gss, sq):
    parts = [gss[l][n].reshape(-1) for n in _SMALL for l in range(len(gss))] + [jnp.sum(sq).reshape(1)]
    flat = jnp.concatenate(parts)
    rows = -(-flat.shape[0] // (8 * _LANES)) * 8
    return jnp.pad(flat, (0, rows * _LANES - flat.shape[0])).reshape(rows, _LANES)


def _unpack_small(total, shapes):
    flat = total.reshape(-1)
    out, off = {}, 0
    for n in _SMALL:
        layers = []
        for _ in range(shapes[n][0]):
            size = 1
            for s in shapes[n][1:]:
                size *= s
            layers.append(flat[off:off + size].reshape(shapes[n][1:]))
            off += size
        out[n] = jnp.stack(layers)
    return out, flat[off]


_GATHER = {
    "band_fwd_0": [(0, "w_proj_a"), (0, "w_proj_b"), (0, "w_out"), (0, "w_ffn_out")],
    "stick_fwd_0": [(0, "w_ffn_in"), (1, "w_proj_a"), (1, "w_proj_b"), (1, "w_out")],
    "mix_fwd_0": [(1, "w_ffn_out")],
    "ffn_fwd_0": [(1, "w_in"), (1, "w_ffn_in")],
}
_SCATTER = {
    "band_bwd_1": [(1, "w_ffn_in"), (1, "w_ffn_out")],
    "stick_bwd_1": [(1, "w_proj_a"), (1, "w_proj_b"), (1, "w_out")],
    "band_bwd_0": [(1, "w_in"), (0, "w_ffn_in")],
    "stick_bwd_0": [(0, "w_ffn_out"), (0, "w_proj_a"), (0, "w_proj_b"), (0, "w_out")],
    "in_proj_bwd_0": [(0, "w_in")],
}
_SHARE = {"stick_bwd_1": "band_bwd_1", "band_bwd_0": "stick_bwd_1", "stick_bwd_0": "band_bwd_0", "grad_w_in_0": "stick_bwd_0"}


def _owner(key):
    del key
    return 1


def kernel(x, w_in, b_gate, rel_bias, w_proj_a, w_proj_b, w_out, ln1_g, ln1_b, w_ffn_in, w_ffn_out, ln2_g, ln2_b, loss_target, m_w_in, m_b_gate, m_rel_bias, m_w_proj_a, m_w_proj_b, m_w_out, m_ln1_g, m_ln1_b, m_w_ffn_in, m_w_ffn_out, m_ln2_g, m_ln2_b, v_w_in, v_b_gate, v_rel_bias, v_w_proj_a, v_w_proj_b, v_w_out, v_ln1_g, v_ln1_b, v_w_ffn_in, v_w_ffn_out, v_ln2_g, v_ln2_b):
    names = ("w_in", "b_gate", "rel_bias", "w_proj_a", "w_proj_b", "w_out", "ln1_g", "ln1_b", "w_ffn_in", "w_ffn_out", "ln2_g", "ln2_b")
    w = dict(zip(names, (w_in, b_gate, rel_bias, w_proj_a, w_proj_b, w_out, ln1_g, ln1_b, w_ffn_in, w_ffn_out, ln2_g, ln2_b)))
    m = dict(zip(names, (m_w_in, m_b_gate, m_rel_bias, m_w_proj_a, m_w_proj_b, m_w_out, m_ln1_g, m_ln1_b, m_w_ffn_in, m_w_ffn_out, m_ln2_g, m_ln2_b)))
    v = dict(zip(names, (v_w_in, v_b_gate, v_rel_bias, v_w_proj_a, v_w_proj_b, v_w_out, v_ln1_g, v_ln1_b, v_w_ffn_in, v_w_ffn_out, v_ln2_g, v_ln2_b)))
    T, D = x.shape[-2], x.shape[-1]
    assert w_in.shape[0] == 2, "the exchange schedule below is written for two layers"

    mine = [{n: w[n][l].astype(_MXU) for n in _DENSE} for l in range(2)]
    W = [dict(), dict()]
    gws = [dict(), dict()]
    slots, final = {}, {}

    def gather(keys):
        sizes = [mine[l][n].size for l, n in keys]
        passed, fractions = 0, []
        for s in sizes:
            passed += s
            fractions.append(0.15 + 0.6 * passed / sum(sizes))

        def done(outs):
            for (l, n), o in zip(keys, outs):
                W[l][n] = o
        return _gather_plan([mine[l][n] for l, n in keys], fractions), done

    def scatter(keys):
        comm = _scatter_plan([gws[l][n] for l, n in keys], [_owner(key) for key in keys])
        return comm, lambda outs: slots.update(zip(keys, outs))

    def share(keys):
        reduced = [_sum_slots(slots[key], f"sum_grad_{key[1]}_{key[0]}") for key in keys]
        comm = _share_plan(reduced, [_owner(key) for key in keys])
        return comm, lambda outs: final.update(zip(keys, outs))

    def both(first, second):
        (ca, da), (cb, db) = first, second
        na = len(ca.out_shapes)
        return _join(ca, cb), lambda outs: (da(outs[:na]), db(outs[na:]))

    plans = {key: functools.partial(gather, keys) for key, keys in _GATHER.items()}
    for key, keys in _SCATTER.items():
        plans[key] = functools.partial(scatter, keys)
    for key, scattered_under in _SHARE.items():
        handed = functools.partial(share, _SCATTER[scattered_under])
        carried = plans.get(key)
        plans[key] = handed if carried is None else (lambda carried=carried, handed=handed: both(carried(), handed()))
    small = {n: w[n] for n in _SMALL}
    sq, dx, _, gss = _local_step(x.reshape(T, D), loss_target.reshape(T, D), W, small,
                                  _Plans(plans, {0: mine[0]["w_in"]}), gws)

    comm, done = share(_SCATTER["in_proj_bwd_0"])
    done(_comm_only(comm, "share_last"))
    total = _all_reduce_small(_pack_small(gss, sq))
    small_grads, sq_all = _unpack_small(total, {n: w[n].shape for n in _SMALL})
    loss = 0.5 * sq_all / D

    grad, delta, new_m, new_v = {}, {}, {}, {}
    for n in names:
        if n in _DENSE:
            updated = _adamw_layers(w[n], [final[(l, n)] for l in range(2)], m[n], v[n], f"adamw_{n}")
        else:
            updated = _adamw(w[n], small_grads[n], m[n], v[n], f"adamw_{n}")
        grad[n], delta[n], new_m[n], new_v[n] = updated
    return (loss, dx.reshape(x.shape), *[grad[n] for n in names], *[delta[n] for n in names],
            *[new_m[n] for n in names], *[new_v[n] for n in names])
```

```python
import functools

import jax
import jax.numpy as jnp
from jax import lax
from jax.experimental import pallas as pl
from jax.experimental.pallas import tpu as pltpu

_MXU = jnp.bfloat16
_ACT = jnp.bfloat16
_F32 = jnp.float32

_HEAD = 64
_CHUNK = 64
_LANES = 128
_TQ = 128
_BAND_TILES = 5
_BIAS_TILES = 9
_REL_CLIP = 256
_LN_EPS = 1e-5
_MASKED = -1e30
_EXP_ZERO_BELOW = -87.34
_SB_WINDOW = 2
_SB_SUBTILES = 4
_BAND_SUBTILES = 8
_VMEM_LIMIT = 56 * 1024 * 1024
_GRAD_ACC_BYTES = 12 * 1024 * 1024

_LR, _B1, _B2, _EPS, _WD, _STEP = 0.001, 0.9, 0.999, 1e-08, 0.01, 10

_MESH = pl.DeviceIdType.MESH


def _dot(a, b):
    return jnp.dot(a, b, preferred_element_type=_F32)


def _dot_nt(a, b):
    return lax.dot_general(a, b, (((1,), (1,)), ((), ())), preferred_element_type=_F32)


def _dot_tn(a, b):
    return lax.dot_general(a, b, (((0,), (0,)), ((), ())), preferred_element_type=_F32)


def _cparams(*sem):
    return pltpu.CompilerParams(dimension_semantics=sem, vmem_limit_bytes=_VMEM_LIMIT)


def _rows(t, c):
    return pl.BlockSpec((t, c), lambda i: (i, 0))


def _whole(shape):
    return pl.BlockSpec(shape, lambda i: tuple(0 for _ in shape))


_ANY = pl.BlockSpec(memory_space=pl.ANY)


def _cols_copies(w_hbm, w_vmem, sem):
    n = w_hbm.shape[-1]
    return [pltpu.make_async_copy(w_hbm.at[k], w_vmem.at[:, pl.ds(k * n, n)], sem.at[k]) for k in range(4)]


def _rows_copies(w_hbm, w_vmem, sem):
    r = w_hbm.shape[-2]
    return [pltpu.make_async_copy(w_hbm.at[k], w_vmem.at[pl.ds(k * r, r), :], sem.at[k]) for k in range(4)]


def _start(copies):
    for cp in copies:
        cp.start()


def _wait(copies):
    for cp in copies:
        cp.wait()


def _load_together(*groups):
    for copies in groups:
        _start(copies)
    for copies in groups:
        _wait(copies)


def _load_cols(w_hbm, w_vmem, sem):
    copies = _cols_copies(w_hbm, w_vmem, sem)
    _start(copies)
    _wait(copies)


def _load_rows(w_hbm, w_vmem, sem):
    copies = _rows_copies(w_hbm, w_vmem, sem)
    _start(copies)
    _wait(copies)


def _ln_stats(u):
    mu = jnp.mean(u, axis=-1, keepdims=True)
    xc = u - mu
    var = jnp.mean(xc * xc, axis=-1, keepdims=True)
    rstd = lax.rsqrt(var + _LN_EPS)
    return xc * rstd, rstd


def _ln_bwd(u, dy, gamma):
    xhat, rstd = _ln_stats(u)
    dxh = dy * gamma
    m1 = jnp.mean(dxh, axis=-1, keepdims=True)
    m2 = jnp.mean(dxh * xhat, axis=-1, keepdims=True)
    du = rstd * (dxh - m1 - xhat * m2)
    return du, jnp.sum(dy * xhat, axis=0, keepdims=True), jnp.sum(dy, axis=0, keepdims=True), xhat


def _divisor_tile(n, cap):
    best = None
    for t in range(_LANES, min(n, cap) + 1, _LANES):
        if n % t == 0:
            best = t
    return best or n


class _Comm:
    def __init__(self, inputs, out_shapes, sems, run, aliases=None):
        self.inputs, self.out_shapes, self.sems, self.run = list(inputs), list(out_shapes), list(sems), run
        self.aliases = aliases or {}


def _call(kern, comm, *, name, grid, in_specs, out_specs, out_shape, scratch_shapes, args, semantics):
    in_specs, out_specs, out_shape, scratch_shapes = list(in_specs), list(out_specs), list(out_shape), list(scratch_shapes)
    if comm is None:
        outs = pl.pallas_call(kern, name=name, grid=grid, in_specs=in_specs, out_specs=out_specs, out_shape=out_shape,
                              scratch_shapes=scratch_shapes, compiler_params=_cparams(*semantics))(*args)
        return list(outs), []
    n_in, n_out, n_scr = len(in_specs), len(out_specs), len(scratch_shapes)
    ci, co = len(comm.inputs), len(comm.out_shapes)
    nsteps = functools.reduce(lambda a, b: a * b, grid, 1)

    def fused(*refs):
        a, b = n_in, n_in + ci
        c, d = b + n_out, b + n_out + co
        e = d + n_scr
        step = pl.program_id(0)
        for ax in range(1, len(grid)):
            step = step * grid[ax] + pl.program_id(ax)
        comm.run(step, nsteps, refs[a:b], refs[c:d], refs[e:])
        kern(*refs[:a], *refs[b:c], *refs[d:e])

    outs = pl.pallas_call(
        fused, name=name, grid=grid, in_specs=in_specs + [_ANY] * ci, out_specs=out_specs + [_ANY] * co,
        out_shape=out_shape + comm.out_shapes, scratch_shapes=scratch_shapes + comm.sems,
        input_output_aliases={n_in + i: n_out + o for i, o in comm.aliases.items()},
        compiler_params=_cparams(*("arbitrary" for _ in grid)))(*args, *comm.inputs)
    return list(outs[:n_out]), list(outs[n_out:])


def _comm_only(comm, name):
    def body(*refs):
        ci, co = len(comm.inputs), len(comm.out_shapes)
        comm.run(0, 1, refs[:ci], refs[ci:ci + co], refs[ci + co:])

    outs = pl.pallas_call(body, name=name, in_specs=[_ANY] * len(comm.inputs), out_specs=[_ANY] * len(comm.out_shapes),
                          out_shape=comm.out_shapes, scratch_shapes=comm.sems,
                          input_output_aliases=dict(comm.aliases))(*comm.inputs)
    return list(outs)


def _in_proj(x, w_in, layer, comm=None):
    T, D = x.shape
    N = 4 * w_in.shape[-1]
    tm = 512

    def kern(x_ref, w_hbm, h_ref, xb_ref, w_v, sem):
        @pl.when(pl.program_id(0) == 0)
        def _():
            _load_cols(w_hbm, w_v, sem)

        xb = x_ref[...].astype(_MXU)
        h_ref[...] = _dot(xb, w_v[...]).astype(h_ref.dtype)
        xb_ref[...] = xb.astype(xb_ref.dtype)

    return _call(
        kern, comm, name=f"in_proj_{layer}", grid=(T // tm,),
        in_specs=[_rows(tm, D), _ANY],
        out_specs=[_rows(tm, N), _rows(tm, D)],
        out_shape=[jax.ShapeDtypeStruct((T, N), _ACT), jax.ShapeDtypeStruct((T, D), _ACT)],
        scratch_shapes=[pltpu.VMEM((D, N), w_in.dtype), pltpu.SemaphoreType.DMA((4,))],
        args=(x, w_in), semantics=("arbitrary",))


def _in_proj_gathering(x, block, layer):
    T, D = x.shape
    n = block.shape[1]
    tm = min(T, 1024)
    nrows = T // tm
    pass_steps = [int(f * nrows) for f in (0.6, 1.0, 1.7)]
    px, py, _ = _place()
    order = jnp.stack([2 * px + py, 2 * (1 - px) + py, 2 * px + (1 - py), 2 * (1 - px) + (1 - py)]).astype(jnp.int32)

    def kern(order_ref, x_ref, blk_hbm, h_ref, xb_ref, w_hbm, w_v, send_sems, recv_sems, loc_sem, load_sem):
        del order_ref
        step = pl.program_id(0) * nrows + pl.program_id(1)
        x_, y_, c = _place()
        k = 2 * x_ + y_
        me, sibling = (x_, y_, c), (x_, y_, 1 - c)
        chips = [(1 - x_, y_), (x_, 1 - y_), (1 - x_, 1 - y_)]
        chip_k = [2 * cx + cy for cx, cy in chips]

        def ici(s, owner_k, to, src=None):
            dst = _half(w_hbm.at[owner_k], c)
            return _remote(dst if src is None else src, dst, send_sems.at[s], recv_sems.at[s], to)

        def passed(s, hc, to):
            blk = _half(w_hbm.at[chip_k[s]], hc)
            return _remote(blk, blk, send_sems.at[3 + s], recv_sems.at[3 + s], to)

        local = pltpu.make_async_copy(blk_hbm, w_hbm.at[k], loc_sem.at[0])

        def load(src):
            cp = pltpu.make_async_copy(src, w_v, load_sem.at[0])
            cp.start()
            cp.wait()

        @pl.when(step == 0)
        def _():
            for s, chip in enumerate(chips):
                ici(s, k, (*chip, c), src=_half(blk_hbm, c)).start()
            local.start()
            load(blk_hbm)

        for s in range(3):
            @pl.when(step == pass_steps[s])
            def _():
                ici(s, chip_k[s], me).wait_recv()
                passed(s, c, sibling).start()

            @pl.when(step == (s + 1) * nrows)
            def _():
                passed(s, 1 - c, me).wait_recv()
                load(w_hbm.at[chip_k[s]])

        xb = x_ref[...].astype(_MXU)
        h_ref[...] = _dot(xb, w_v[...]).astype(h_ref.dtype)

        @pl.when(pl.program_id(0) == 0)
        def _():
            xb_ref[...] = xb.astype(xb_ref.dtype)

        @pl.when(step == 4 * nrows - 1)
        def _():
            for s, chip in enumerate(chips):
                ici(s, k, (*chip, c), src=_half(blk_hbm, c)).wait_send()
                passed(s, c, sibling).wait_send()
            local.wait()

    assert all(pass_steps[s] <= (s + 1) * nrows for s in range(3))
    h, xb, w_in = pl.pallas_call(
        kern, name=f"in_proj_{layer}",
        grid_spec=pltpu.PrefetchScalarGridSpec(
            num_scalar_prefetch=1, grid=(4, nrows),
            in_specs=[pl.BlockSpec((tm, D), lambda j, i, o: (i, 0)), _ANY],
            out_specs=[pl.BlockSpec((tm, n), lambda j, i, o: (i, o[j])),
                       pl.BlockSpec((tm, D), lambda j, i, o: (jnp.where(j == 0, i, nrows - 1), 0)), _ANY],
            scratch_shapes=[pltpu.VMEM((D, n), block.dtype), pltpu.SemaphoreType.DMA((6,)),
                            pltpu.SemaphoreType.DMA((6,)), pltpu.SemaphoreType.DMA((1,)), pltpu.SemaphoreType.DMA((1,))]),
        out_shape=[jax.ShapeDtypeStruct((T, 4 * n), _ACT), jax.ShapeDtypeStruct((T, D), _ACT),
                   jax.ShapeDtypeStruct((4,) + block.shape, block.dtype)],
        compiler_params=_cparams("arbitrary", "arbitrary"))(order, x, block)
    return h, xb, w_in


def _gate_specs(h, tm, D):
    first = (h.shape[1] - 2 * D) // D
    assert first * D + 2 * D == h.shape[1]
    return [pl.BlockSpec((tm, D), lambda i: (i, first)), pl.BlockSpec((tm, D), lambda i: (i, first + 1))]


def _mix_fwd(oa, ob, h, x, wpa, wpb, wo, bg, gamma, beta, alpha, layer, comm=None):
    T, D = x.shape
    WA, WB = oa.shape[1], ob.shape[1]
    tm = 512

    def kern(oa_ref, ob_ref, hga_ref, hgb_ref, x_ref, bg_ref, g_ref, b_ref, wpa_h, wpb_h, wo_h,
             x1_ref, u1_ref, pre_ref, wpa_v, wpb_v, wo_v, sa, sb, so):
        @pl.when(pl.program_id(0) == 0)
        def _():
            _load_together(_cols_copies(wpa_h, wpa_v, sa), _cols_copies(wpb_h, wpb_v, sb), _rows_copies(wo_h, wo_v, so))

        ya = _dot(oa_ref[...].astype(_MXU), wpa_v[...])
        yb = _dot(ob_ref[...].astype(_MXU), wpb_v[...])
        bgv = bg_ref[...]
        ga = jax.nn.sigmoid(hga_ref[...].astype(_F32) + bgv[:, :D])
        gb = jax.nn.sigmoid(hgb_ref[...].astype(_F32) + bgv[:, D:])
        pre = ga * ya + gb * yb
        mix = _dot(pre.astype(_MXU), wo_v[...])
        u = alpha * x_ref[...] + mix
        xhat, _ = _ln_stats(u)
        x1_ref[...] = xhat * g_ref[...] + b_ref[...]
        u1_ref[...] = u
        pre_ref[...] = pre.astype(pre_ref.dtype)

    return _call(
        kern, comm, name=f"mix_fwd_{layer}", grid=(T // tm,),
        in_specs=[_rows(tm, WA), _rows(tm, WB), *_gate_specs(h, tm, D), _rows(tm, D),
                  _whole((1, 2 * D)), _whole((1, D)), _whole((1, D)), _ANY, _ANY, _ANY],
        out_specs=[_rows(tm, D)] * 3,
        out_shape=[jax.ShapeDtypeStruct((T, D), _F32), jax.ShapeDtypeStruct((T, D), _F32),
                   jax.ShapeDtypeStruct((T, D), _ACT)],
        scratch_shapes=[pltpu.VMEM((WA, D), wpa.dtype), pltpu.VMEM((WB, D), wpb.dtype), pltpu.VMEM((D, D), wo.dtype),
                        pltpu.SemaphoreType.DMA((4,)), pltpu.SemaphoreType.DMA((4,)), pltpu.SemaphoreType.DMA((4,))],
        args=(oa, ob, h, h, x, bg, gamma, beta, wpa, wpb, wo), semantics=("arbitrary",))


def _ffn_fwd(x1, wfi, wfo, gamma, beta, alpha, layer, comm=None):
    T, D = x1.shape
    F2 = 4 * wfi.shape[-1]
    F = F2 // 2
    tm = 512
    fc = F // 2

    def kern(x_ref, g_ref, b_ref, wi_h, wo_h, x2_ref, u2_ref, act_ref, gu_ref, xb_ref, wi_v, wo_v, si, so):
        @pl.when(pl.program_id(0) == 0)
        def _():
            _load_together(_cols_copies(wi_h, wi_v, si), _rows_copies(wo_h, wo_v, so))

        x = x_ref[...]
        xb = x.astype(_MXU)
        xb_ref[...] = xb.astype(xb_ref.dtype)
        ffn = jnp.zeros((tm, D), _F32)
        for c in range(2):
            g = _dot(xb, wi_v[:, c * fc:(c + 1) * fc])
            u = _dot(xb, wi_v[:, F + c * fc:F + (c + 1) * fc])
            act = g * jax.nn.sigmoid(g) * u
            ab = act.astype(_MXU)
            ffn = ffn + _dot(ab, wo_v[c * fc:(c + 1) * fc, :])
            act_ref[:, c * fc:(c + 1) * fc] = ab.astype(act_ref.dtype)
            gu_ref[:, c * fc:(c + 1) * fc] = g.astype(gu_ref.dtype)
            gu_ref[:, F + c * fc:F + (c + 1) * fc] = u.astype(gu_ref.dtype)
        uu = alpha * x + ffn
        xhat, _ = _ln_stats(uu)
        x2_ref[...] = xhat * g_ref[...] + b_ref[...]
        u2_ref[...] = uu

    return _call(
        kern, comm, name=f"ffn_fwd_{layer}", grid=(T // tm,),
        in_specs=[_rows(tm, D), _whole((1, D)), _whole((1, D)), _ANY, _ANY],
        out_specs=[_rows(tm, D), _rows(tm, D), _rows(tm, F), _rows(tm, F2), _rows(tm, D)],
        out_shape=[jax.ShapeDtypeStruct((T, D), _F32), jax.ShapeDtypeStruct((T, D), _F32),
                   jax.ShapeDtypeStruct((T, F), _ACT), jax.ShapeDtypeStruct((T, F2), _ACT),
                   jax.ShapeDtypeStruct((T, D), _ACT)],
        scratch_shapes=[pltpu.VMEM((D, F2), wfi.dtype), pltpu.VMEM((F, D), wfo.dtype),
                        pltpu.SemaphoreType.DMA((4,)), pltpu.SemaphoreType.DMA((4,))],
        args=(x1, gamma, beta, wfi, wfo), semantics=("arbitrary",))


def _ffn_bwd(u2, dy_or_target, gu, gamma, beta, wfi, wfo, alpha, layer, last):
    T, D = u2.shape
    F2 = gu.shape[1]
    F = F2 // 2
    tm = 256
    fc = F // 2

    def kern(u_ref, dy_ref, gu_ref, g_ref, b_ref, wi_h, wo_h, dx_ref, dub_ref, dgu_ref, st_ref, wi_v, wo_v, si, so):
        @pl.when(pl.program_id(0) == 0)
        def _():
            _load_together(_rows_copies(wo_h, wo_v, so), _cols_copies(wi_h, wi_v, si))
            st_ref[...] = jnp.zeros_like(st_ref)

        gam = g_ref[...]
        u = u_ref[...]
        if last:
            xhat0, _ = _ln_stats(u)
            err = xhat0 * gam + b_ref[...] - dy_ref[...]
            dy = err * (1.0 / D)
            st_ref[2:3, :] += jnp.sum(err * err, axis=0, keepdims=True)
        else:
            dy = dy_ref[...]
        du, dgam, dbet, _ = _ln_bwd(u, dy, gam)
        st_ref[0:1, :] += dgam
        st_ref[1:2, :] += dbet
        dub = du.astype(_MXU)
        dub_ref[...] = dub.astype(dub_ref.dtype)
        dx = alpha * du
        for c in range(2):
            dact = _dot_nt(dub, wo_v[c * fc:(c + 1) * fc, :])
            g = gu_ref[:, c * fc:(c + 1) * fc].astype(_F32)
            uu = gu_ref[:, F + c * fc:F + (c + 1) * fc].astype(_F32)
            sg = jax.nn.sigmoid(g)
            dg = (dact * uu * (sg * (1.0 + g * (1.0 - sg)))).astype(_MXU)
            dup = (dact * (g * sg)).astype(_MXU)
            dgu_ref[:, c * fc:(c + 1) * fc] = dg.astype(dgu_ref.dtype)
            dgu_ref[:, F + c * fc:F + (c + 1) * fc] = dup.astype(dgu_ref.dtype)
            dx = dx + _dot_nt(dg, wi_v[:, c * fc:(c + 1) * fc]) + _dot_nt(dup, wi_v[:, F + c * fc:F + (c + 1) * fc])
        dx_ref[...] = dx

    return pl.pallas_call(
        kern, name=f"ffn_bwd_{layer}", grid=(T // tm,),
        in_specs=[_rows(tm, D), _rows(tm, D), _rows(tm, F2), _whole((1, D)), _whole((1, D)), _ANY, _ANY],
        out_specs=[_rows(tm, D), _rows(tm, D), _rows(tm, F2), _whole((8, D))],
        out_shape=[jax.ShapeDtypeStruct((T, D), _F32), jax.ShapeDtypeStruct((T, D), _ACT),
                   jax.ShapeDtypeStruct((T, F2), _ACT), jax.ShapeDtypeStruct((8, D), _F32)],
        scratch_shapes=[pltpu.VMEM((D, F2), wfi.dtype), pltpu.VMEM((F, D), wfo.dtype),
                        pltpu.SemaphoreType.DMA((4,)), pltpu.SemaphoreType.DMA((4,))],
        compiler_params=_cparams("arbitrary"),
    )(u2, dy_or_target, gu, gamma, beta, wfi, wfo)


def _residual_nt(res, res_scale, pieces, w, name, comm=None):
    T, K = res.shape
    widths = [p.shape[1] for p in pieces]
    N = sum(widths)
    tm = 512

    def kern(r_ref, *refs):
        d_refs, (w_hbm, o_ref, w_v, sem) = refs[:len(pieces)], refs[len(pieces):]

        @pl.when(pl.program_id(0) == 0)
        def _():
            _load_cols(w_hbm, w_v, sem)

        acc = res_scale * r_ref[...]
        off = 0
        for d_ref, width in zip(d_refs, widths):
            acc = acc + _dot_nt(d_ref[...].astype(_MXU), w_v[:, off:off + width])
            off += width
        o_ref[...] = acc

    outs, extra = _call(
        kern, comm, name=name, grid=(T // tm,),
        in_specs=[_rows(tm, K)] + [_rows(tm, width) for width in widths] + [_ANY], out_specs=[_rows(tm, K)],
        out_shape=[jax.ShapeDtypeStruct((T, K), _F32)],
        scratch_shapes=[pltpu.VMEM((K, N), w.dtype), pltpu.SemaphoreType.DMA((4,))],
        args=(res, *pieces, w), semantics=("arbitrary",))
    return outs[0], extra


def _grad_w_pieces(a, pieces, name, comm=None):
    T, M = a.shape
    widths = [p.shape[1] for p in pieces]
    offsets = [sum(widths[:p]) for p in range(len(pieces))]
    N = sum(widths)
    tk = 1024 if T % 1024 == 0 else 512
    nk = T // tk

    def fits(ow):
        inside = lambda off, width: width < ow and off // ow == (off + width - 1) // ow
        whole = lambda off, width: off % ow == 0 and width % ow == 0
        return (N % ow == 0 and ow % _LANES == 0 and M * ow * 4 <= _GRAD_ACC_BYTES
                and all(inside(o, w) or whole(o, w) for o, w in zip(offsets, widths)))

    ow = next(c for c in (N // 2, 1024, 512, 256, _LANES) if fits(c))

    def kern(a_ref, *refs):
        b_refs, (o_ref, acc) = refs[:len(pieces)], refs[len(pieces):]
        j, k = pl.program_id(0), pl.program_id(1)

        @pl.when(k == 0)
        def _():
            acc[...] = jnp.zeros_like(acc)

        for b_ref, off, width in zip(b_refs, offsets, widths):
            if width < ow:
                @pl.when(j == off // ow)
                def _():
                    acc[:, off % ow:off % ow + width] += _dot_tn(a_ref[...].astype(_MXU), b_ref[...].astype(_MXU))
            else:
                @pl.when(jnp.logical_and(j >= off // ow, j < (off + width) // ow))
                def _():
                    acc[...] += _dot_tn(a_ref[...].astype(_MXU), b_ref[...].astype(_MXU))

        @pl.when(k == nk - 1)
        def _():
            o_ref[...] = acc[...].astype(o_ref.dtype)

    def piece_spec(off, width):
        first, blocks = off // ow, max(width // ow, 1)

        def index(j, k):
            mine = jnp.logical_and(j >= first, j < first + blocks)
            return jnp.where(mine, k, 0), jnp.where(mine, j - first, 0)
        return pl.BlockSpec((tk, min(width, ow)), index)

    outs, extra = _call(
        kern, comm, name=name, grid=(N // ow, nk),
        in_specs=[pl.BlockSpec((tk, M), lambda j, k: (k, 0))] + [piece_spec(o, w) for o, w in zip(offsets, widths)],
        out_specs=[pl.BlockSpec((M, ow), lambda j, k: (0, j))],
        out_shape=[jax.ShapeDtypeStruct((M, N), _ACT)], scratch_shapes=[pltpu.VMEM((M, ow), _F32)],
        args=(a, *pieces), semantics=("parallel", "arbitrary"))
    return outs[0], extra


def _gcd(a, b):
    while b:
        a, b = b, a % b
    return a


def _mix_bwd(u1, dx1, oa, ob, h, wpa, wpb, wo, bg, gamma, layer):
    T, D = u1.shape
    WA, WB = wpa.shape[-2], wpb.shape[-2]
    tm = 512

    def kern(u_ref, dx_ref, oa_ref, ob_ref, hga_ref, hgb_ref, bg_ref, g_ref, wpa_h, wpb_h, wo_h,
             du_ref, dub_ref, dya_ref, dyb_ref, dhg_ref, doa_ref, dob_ref, st_ref,
             wpa_v, wpb_v, wo_v, sa, sb, so):
        @pl.when(pl.program_id(0) == 0)
        def _():
            _load_together(_cols_copies(wpa_h, wpa_v, sa), _cols_copies(wpb_h, wpb_v, sb), _rows_copies(wo_h, wo_v, so))
            st_ref[...] = jnp.zeros_like(st_ref)

        du, dgam, dbet, _ = _ln_bwd(u_ref[...], dx_ref[...], g_ref[...])
        st_ref[1:2, :D] += dgam
        st_ref[1:2, D:] += dbet
        du_ref[...] = du
        dub = du.astype(_MXU)
        dub_ref[...] = dub.astype(dub_ref.dtype)
        dpre = _dot_nt(dub, wo_v[...])
        bgv = bg_ref[...]
        ga = jax.nn.sigmoid(hga_ref[...].astype(_F32) + bgv[:, :D])
        gb = jax.nn.sigmoid(hgb_ref[...].astype(_F32) + bgv[:, D:])
        dya = (dpre * ga).astype(_MXU)
        dyb = (dpre * gb).astype(_MXU)
        dsa = dpre * _dot(oa_ref[...].astype(_MXU), wpa_v[...]) * (ga * (1.0 - ga))
        dsb = dpre * _dot(ob_ref[...].astype(_MXU), wpb_v[...]) * (gb * (1.0 - gb))
        st_ref[0:1, :D] += jnp.sum(dsa, axis=0, keepdims=True)
        st_ref[0:1, D:] += jnp.sum(dsb, axis=0, keepdims=True)
        dya_ref[...] = dya.astype(dya_ref.dtype)
        dyb_ref[...] = dyb.astype(dyb_ref.dtype)
        dhg_ref[:, :D] = dsa.astype(dhg_ref.dtype)
        dhg_ref[:, D:] = dsb.astype(dhg_ref.dtype)
        doa_ref[...] = _dot_nt(dya, wpa_v[...]).astype(doa_ref.dtype)
        dob_ref[...] = _dot_nt(dyb, wpb_v[...]).astype(dob_ref.dtype)

    return pl.pallas_call(
        kern, name=f"mix_bwd_{layer}", grid=(T // tm,),
        in_specs=[_rows(tm, D), _rows(tm, D), _rows(tm, WA), _rows(tm, WB), *_gate_specs(h, tm, D), _whole((1, 2 * D)),
                  _whole((1, D)), _ANY, _ANY, _ANY],
        out_specs=[_rows(tm, D)] * 4 + [_rows(tm, 2 * D), _rows(tm, WA), _rows(tm, WB), _whole((8, 2 * D))],
        out_shape=[jax.ShapeDtypeStruct((T, D), _F32)] + [jax.ShapeDtypeStruct((T, D), _ACT)] * 3
        + [jax.ShapeDtypeStruct((T, 2 * D), _ACT), jax.ShapeDtypeStruct((T, WA), _ACT),
           jax.ShapeDtypeStruct((T, WB), _ACT), jax.ShapeDtypeStruct((8, 2 * D), _F32)],
        scratch_shapes=[pltpu.VMEM((WA, D), wpa.dtype), pltpu.VMEM((WB, D), wpb.dtype), pltpu.VMEM((D, D), wo.dtype),
                        pltpu.SemaphoreType.DMA((4,)), pltpu.SemaphoreType.DMA((4,)), pltpu.SemaphoreType.DMA((4,))],
        compiler_params=_cparams("arbitrary"),
    )(u1, dx1, oa, ob, h, h, bg, gamma, wpa, wpb, wo)


def _grad_w(a, b, *, col_shards, name, comm=None):
    T, M = a.shape
    N = b.shape[1]
    tk = 1024 if T % 1024 == 0 else 512
    n = N // 4 if col_shards else N
    group = max(g for g in (1, 2, 4) if g == 1 or M * n * g * 4 <= _GRAD_ACC_BYTES)
    tn = n * group if col_shards else (N if M * N * 4 <= _GRAD_ACC_BYTES else _divisor_tile(N, _GRAD_ACC_BYTES // (4 * M)))
    nk = T // tk

    def kern(a_ref, b_ref, o_ref, acc):
        k = pl.program_id(1)

        @pl.when(k == 0)
        def _():
            acc[...] = jnp.zeros_like(acc)

        acc[...] += _dot_tn(a_ref[...].astype(_MXU), b_ref[...].astype(_MXU))

        @pl.when(k == nk - 1)
        def _():
            if col_shards:
                for s in range(group):
                    o_ref[s] = acc[:, s * n:(s + 1) * n].astype(o_ref.dtype)
            else:
                o_ref[...] = acc[...].astype(o_ref.dtype)

    if col_shards:
        out_spec = pl.BlockSpec((group, M, n), lambda j, k: (j, 0, 0))
        out_shape = jax.ShapeDtypeStruct((4, M, n), _ACT)
    else:
        out_spec = pl.BlockSpec((M, tn), lambda j, k: (0, j))
        out_shape = jax.ShapeDtypeStruct((M, N), _ACT)
    outs, extra = _call(
        kern, comm, name=name, grid=(N // tn, nk),
        in_specs=[pl.BlockSpec((tk, M), lambda j, k: (k, 0)), pl.BlockSpec((tk, tn), lambda j, k: (k, j))],
        out_specs=[out_spec], out_shape=[out_shape], scratch_shapes=[pltpu.VMEM((M, tn), _F32)],
        args=(a, b), semantics=("parallel", "arbitrary"))
    return outs[0], extra


def _bias_tiles(rel):
    H = rel.shape[0]
    span = _TQ * _BAND_TILES - 1
    edge = span - _REL_CLIP
    gvec = jnp.concatenate([jnp.broadcast_to(rel[:, :1], (H, edge)), rel, jnp.broadcast_to(rel[:, -1:], (H, edge))], axis=1)
    width = _BIAS_TILES * _TQ
    period = width + _TQ
    tiled = jnp.broadcast_to(jnp.pad(gvec[:, ::-1], ((0, 0), (0, 1)))[:, None, :], (H, _TQ, period))
    rows = tiled.reshape(H, _TQ * period)[:, :_TQ * (period - 1)].reshape(H, _TQ, period - 1)[:, :, _TQ - 1:]
    r = jnp.arange(_TQ)[:, None]
    u = jnp.arange(width)[None, :]
    d = 4 * _TQ + r - u
    rm = r % _CHUNK
    valid = (d >= rm - (_CHUNK - 1)) & (d <= rm + 8 * _CHUNK)
    tiles = jnp.where(valid[None], rows, _MASKED)
    return tiles.reshape(H // 2, 2 * _TQ, width)


def _strip_tiles(strip_ref, tiles_ref, to_strip=False):
    for ub in range(_BIAS_TILES):
        if to_strip:
            strip_ref[:, ub * _TQ:(ub + 1) * _TQ] = tiles_ref[ub]
        else:
            tiles_ref[ub] = strip_ref[:, ub * _TQ:(ub + 1) * _TQ]


def _fold_bias_grad(db):
    H = 2 * db.shape[0]
    width = _BIAS_TILES * _TQ
    period = width + _TQ
    x = jnp.pad(db.reshape(H, _TQ, width), ((0, 0), (0, 0), (_TQ - 1, 0)))
    skew = jnp.pad(x.reshape(H, _TQ * (period - 1)), ((0, 0), (0, _TQ))).reshape(H, _TQ, period)
    dg = skew.sum(axis=1)[:, :period - 1][:, ::-1]
    span = _TQ * _BAND_TILES - 1
    edge = span - _REL_CLIP
    mid = dg[:, edge:edge + 2 * _REL_CLIP + 1]
    lo = dg[:, :edge].sum(axis=1)
    hi = dg[:, edge + 2 * _REL_CLIP + 1:].sum(axis=1)
    return mid.at[:, 0].add(lo).at[:, -1].add(hi)


def _band_window(i):
    j0 = jnp.maximum(i - (_BAND_TILES - 1), 0)
    return j0, (_BAND_TILES - 1) - (i - j0)


def _head_masks():
    lane = lax.broadcasted_iota(jnp.int32, (1, _LANES), 1)
    return [(lane // _HEAD) == hh for hh in range(2)]


def _stack_heads(x, masks):
    return jnp.concatenate([jnp.where(m, x, jnp.zeros_like(x)) for m in masks], axis=0)


def _unstack_heads(y, masks):
    return jnp.where(masks[0], y[:_TQ], y[_TQ:])


def _scaled(q):
    return q * jnp.asarray(_HEAD ** -0.5, q.dtype)


def _band_probs(q2, k_ref, b_ref, j0, boff):
    s = []
    for j in range(_BAND_TILES):
        kj = k_ref[pl.ds(pl.multiple_of((j0 + j) * _TQ, _TQ), _TQ), :]
        s.append(_dot_nt(q2, kj) + b_ref[boff + j])
    m = jnp.max(functools.reduce(jnp.maximum, s), axis=-1, keepdims=True)
    p = [jnp.exp(x - m) for x in s]
    l = jnp.sum(functools.reduce(lambda a, b: a + b, p), axis=-1, keepdims=True)
    return p, 1.0 / l


def _qkv_specs(T, cb, npair, tq=_TQ):
    return [pl.BlockSpec((tq, _LANES), lambda h, i: (i, cb + h)),
            pl.BlockSpec((T, _LANES), lambda h, i: (0, cb + npair + h)),
            pl.BlockSpec((T, _LANES), lambda h, i: (0, cb + 2 * npair + h))]


def _attn_a_fwd(hq, bias, col0, width, layer, comm=None):
    T = hq.shape[0]
    npair = width // _LANES
    nsub = _BAND_SUBTILES
    tq = nsub * _TQ

    def kern(q_ref, k_ref, v_ref, b_ref, o_ref, b_tiles):
        @pl.when(pl.program_id(1) == 0)
        def _():
            _strip_tiles(b_ref, b_tiles)

        masks = _head_masks()
        q = _scaled(q_ref[...])
        for s in range(nsub):
            part = slice(s * _TQ, (s + 1) * _TQ)
            j0, boff = _band_window(nsub * pl.program_id(1) + s)
            p, inv = _band_probs(_stack_heads(q[part], masks), k_ref, b_tiles, j0, boff)
            o = jnp.zeros((2 * _TQ, _LANES), _F32)
            for j in range(_BAND_TILES):
                vj = v_ref[pl.ds(pl.multiple_of((j0 + j) * _TQ, _TQ), _TQ), :]
                o = o + _dot(p[j].astype(_MXU), vj)
            o_ref[part, :] = _unstack_heads(o * inv, masks).astype(o_ref.dtype)

    outs, extra = _call(
        kern, comm, name=f"band_attn_fwd_{layer}", grid=(npair, T // tq),
        in_specs=_qkv_specs(T, col0 // _LANES, npair, tq)
        + [pl.BlockSpec((None, 2 * _TQ, _BIAS_TILES * _TQ), lambda h, i: (h, 0, 0))],
        out_specs=[pl.BlockSpec((tq, _LANES), lambda h, i: (i, h))],
        out_shape=[jax.ShapeDtypeStruct((T, width), _ACT)],
        scratch_shapes=[pltpu.VMEM((_BIAS_TILES, 2 * _TQ, _TQ), _F32)],
        args=(hq, hq, hq, bias), semantics=("arbitrary", "arbitrary"))
    return outs[0], extra


def _attn_a_bwd(hq, bias, do, col0, width, layer, comm=None):
    T = hq.shape[0]
    npair = width // _LANES
    nsub = _BAND_SUBTILES
    tq = nsub * _TQ
    nq = T // tq
    scale = _HEAD ** -0.5

    def kern(q_ref, k_ref, v_ref, b_ref, do_ref, dq_ref, dk_ref, dv_ref, db_ref, dk_acc, dv_acc, b_tiles, db_acc):
        i = pl.program_id(1)

        @pl.when(i == 0)
        def _():
            _strip_tiles(b_ref, b_tiles)
            dk_acc[...] = jnp.zeros_like(dk_acc)
            dv_acc[...] = jnp.zeros_like(dv_acc)
            db_acc[...] = jnp.zeros_like(db_acc)

        masks = _head_masks()
        q = _scaled(q_ref[...])
        do_t = do_ref[...]
        for s in range(nsub):
            part = slice(s * _TQ, (s + 1) * _TQ)
            j0, boff = _band_window(nsub * i + s)
            q2 = _stack_heads(q[part], masks)
            do2 = _stack_heads(do_t[part], masks).astype(_MXU)
            p, inv = _band_probs(q2, k_ref, b_tiles, j0, boff)
            rows = [pl.ds(pl.multiple_of((j0 + j) * _TQ, _TQ), _TQ) for j in range(_BAND_TILES)]
            p = [x * inv for x in p]
            dp = [_dot_nt(do2, v_ref[rows[j], :]) for j in range(_BAND_TILES)]
            delta = jnp.sum(functools.reduce(lambda a, b: a + b, [p[j] * dp[j] for j in range(_BAND_TILES)]),
                            axis=-1, keepdims=True)
            dq = jnp.zeros((2 * _TQ, _LANES), _F32)
            for j in range(_BAND_TILES):
                ds = p[j] * (dp[j] - delta)
                db_acc[boff + j] += ds
                dsb = ds.astype(_MXU)
                dq = dq + _dot(dsb, k_ref[rows[j], :])
                dk_acc[rows[j], :] += _dot_tn(dsb, q2)
                dv_acc[rows[j], :] += _dot_tn(p[j].astype(_MXU), do2)
            dq_ref[part, :] = (_unstack_heads(dq, masks) * scale).astype(dq_ref.dtype)

        @pl.when(i == nq - 1)
        def _():
            dk_ref[...] = dk_acc[...].astype(dk_ref.dtype)
            dv_ref[...] = dv_acc[...].astype(dv_ref.dtype)
            _strip_tiles(db_ref, db_acc, to_strip=True)

    strip = pl.BlockSpec((None, 2 * _TQ, _BIAS_TILES * _TQ), lambda h, i: (h, 0, 0))
    tile = pl.BlockSpec((tq, _LANES), lambda h, i: (i, h))
    column = pl.BlockSpec((T, _LANES), lambda h, i: (0, h))
    outs, extra = _call(
        kern, comm, name=f"band_attn_bwd_{layer}", grid=(npair, nq),
        in_specs=_qkv_specs(T, col0 // _LANES, npair, tq) + [strip, tile],
        out_specs=[tile, column, column, strip],
        out_shape=[jax.ShapeDtypeStruct((T, width), _ACT)] * 3
        + [jax.ShapeDtypeStruct((npair, 2 * _TQ, _BIAS_TILES * _TQ), _F32)],
        scratch_shapes=[pltpu.VMEM((T, _LANES), _F32), pltpu.VMEM((T, _LANES), _F32),
                        pltpu.VMEM((_BIAS_TILES, 2 * _TQ, _TQ), _F32), pltpu.VMEM((_BIAS_TILES, 2 * _TQ, _TQ), _F32)],
        args=(hq, hq, hq, bias, do), semantics=("arbitrary", "arbitrary"))
    return outs, extra


def _suffix_matrix():
    r = lax.broadcasted_iota(jnp.int32, (_TQ, _TQ), 0)
    c = lax.broadcasted_iota(jnp.int32, (_TQ, _TQ), 1)
    r2 = lax.broadcasted_iota(jnp.int32, (2 * _TQ, _TQ), 0)
    c2 = lax.broadcasted_iota(jnp.int32, (2 * _TQ, _TQ), 1)
    return (r > c).astype(_MXU), c2 - (r2 & (_TQ - 1))


def _suffix_sums(xs, tri):
    n, k = xs[0].shape[0], len(xs)
    his = [x.astype(_MXU) for x in xs]
    los = [(x - h.astype(_F32)).astype(_MXU) for x, h in zip(xs, his)]
    y = _dot(jnp.concatenate(his + los, axis=0), tri)
    return [y[j * n:(j + 1) * n] + y[(k + j) * n:(k + j + 1) * n] for j in range(k)]


def _stick_tiles(tiles, rel, carry_l, tri):
    zs = [_dot_nt(qs, kj) for qs, kj, _, _ in tiles]
    Ls, masks = [], []
    for z, (_, _, jj, _) in zip(zs, tiles):
        nsp = -(jnp.maximum(z, 0.0) + jnp.log(1.0 + jnp.exp(-jnp.abs(z))))
        if isinstance(jj, int):
            mask = (rel < 0) if jj == 0 else None
        else:
            mask = rel < jnp.where(jj == 0, 0, _TQ)
        Ls.append(nsp if mask is None else jnp.where(mask, nsp, 0.0))
        masks.append(mask)
    carry_l = list(carry_l)
    ws = []
    for z, L, suffix, mask, (_, _, _, sub) in zip(zs, Ls, _suffix_sums(Ls, tri), masks, tiles):
        w = jnp.exp(z + L + suffix + carry_l[sub])
        ws.append(w if mask is None else jnp.where(mask, w, 0.0))
        carry_l[sub] = carry_l[sub] + jnp.sum(L, axis=-1, keepdims=True)
    return zs, Ls, ws, masks, carry_l


def _sweep(i, step, zero):
    nsub = _SB_SUBTILES

    def window():
        tiles = [(s, jj) for jj in range(_SB_WINDOW) for s in range(nsub)]
        return tuple((jnp.int32(_SB_WINDOW),) + c for c in step(tiles, [zero] * nsub))

    start = lax.cond(i >= -(-(_SB_WINDOW - 1) // nsub), window, lambda: tuple((jnp.int32(0),) + zero for _ in range(nsub)))
    outs = []
    for s in range(nsub):
        def done(c, s=s):
            return jnp.logical_or(c[0] > nsub * i + s, jnp.max(c[1]) < _EXP_ZERO_BELOW)

        def more(c, s=s):
            carries = [None] * nsub
            carries[s] = c[1:]
            return (c[0] + 1,) + step([(s, c[0])], carries)[s]

        outs.append(lax.while_loop(lambda c, done=done: jnp.logical_not(done(c)), more, start[s]))
    return outs


def _sb_fwd(hq, col0, width, layer, comm=None):
    T = hq.shape[0]
    npair = width // _LANES
    nsub = _SB_SUBTILES
    tq = nsub * _TQ

    def kern(q_ref, k_ref, v_ref, o_ref):
        i = pl.program_id(1)
        masks = _head_masks()
        tri, rel = _suffix_matrix()
        q = _scaled(q_ref[...])
        q2 = [_stack_heads(q[s * _TQ:(s + 1) * _TQ], masks) for s in range(nsub)]

        def step(tiles, carries):
            rows = [pl.ds(pl.multiple_of((nsub * i + s - jj) * _TQ, _TQ), _TQ) for s, jj in tiles]
            cls = [None if c is None else c[0] for c in carries]
            accs = [None if c is None else c[1] for c in carries]
            _, _, ws, _, cls = _stick_tiles([(q2[s], k_ref[r, :], jj, s) for (s, jj), r in zip(tiles, rows)], rel, cls, tri)
            for w, r, (s, _) in zip(ws, rows, tiles):
                accs[s] = accs[s] + _dot(w.astype(_MXU), v_ref[r, :])
            return [None if c is None else (cls[s], accs[s]) for s, c in enumerate(carries)]

        outs = _sweep(i, step, (jnp.zeros((2 * _TQ, 1), _F32), jnp.zeros((2 * _TQ, _LANES), _F32)))
        for s in range(nsub):
            o_ref[s * _TQ:(s + 1) * _TQ, :] = _unstack_heads(outs[s][2], masks)

    outs, extra = _call(
        kern, comm, name=f"stick_attn_fwd_{layer}", grid=(npair, T // tq),
        in_specs=_qkv_specs(T, col0 // _LANES, npair, tq),
        out_specs=[pl.BlockSpec((tq, _LANES), lambda h, i: (i, h))],
        out_shape=[jax.ShapeDtypeStruct((T, width), _F32)], scratch_shapes=[],
        args=(hq, hq, hq), semantics=("arbitrary", "arbitrary"))
    return outs[0], extra


def _sb_bwd(hq, o, do, col0, width, layer, comm=None):
    T = hq.shape[0]
    npair = width // _LANES
    nsub = _SB_SUBTILES
    tq = nsub * _TQ
    nq = T // tq
    scale = _HEAD ** -0.5

    def kern(q_ref, k_ref, v_ref, o_ref, do_ref, dq_ref, dk_ref, dv_ref, dk_acc, dv_acc):
        i = pl.program_id(1)

        @pl.when(i == 0)
        def _():
            dk_acc[...] = jnp.zeros_like(dk_acc)
            dv_acc[...] = jnp.zeros_like(dv_acc)

        masks = _head_masks()
        tri, rel = _suffix_matrix()
        q = _scaled(q_ref[...])
        do_t = do_ref[...]
        prod = do_t.astype(_F32) * o_ref[...]
        part = [slice(s * _TQ, (s + 1) * _TQ) for s in range(nsub)]
        q2 = [_stack_heads(q[p], masks) for p in part]
        do2 = [_stack_heads(do_t[p], masks).astype(_MXU) for p in part]
        dsum = [jnp.sum(_stack_heads(prod[p], masks), axis=-1, keepdims=True) for p in part]

        def step(tiles, carries):
            rows = [pl.ds(pl.multiple_of((nsub * i + s - jj) * _TQ, _TQ), _TQ) for s, jj in tiles]
            kjs = [k_ref[r, :] for r in rows]
            cls, cgs, dqs = ([None if c is None else c[n] for c in carries] for n in range(3))
            zs, Ls, ws, tile_masks, cls = _stick_tiles([(q2[s], kj, jj, s) for (s, jj), kj in zip(tiles, kjs)], rel, cls, tri)
            wbs = [w.astype(_MXU) for w in ws]
            gs = [wb.astype(_F32) * _dot_nt(do2[s], v_ref[r, :]) for wb, r, (s, _) in zip(wbs, rows, tiles)]
            for z, L, g, later, mask, wb, kj, r, (s, _) in zip(zs, Ls, gs, _suffix_sums(gs, tri), tile_masks, wbs, kjs,
                                                               rows, tiles):
                dz = g - jnp.exp(z + L) * (dsum[s] - (later + cgs[s]))
                if mask is not None:
                    dz = jnp.where(mask, dz, 0.0)
                dzb = dz.astype(_MXU)
                dk_acc[r, :] += _dot_tn(dzb, q2[s])
                dv_acc[r, :] += _dot_tn(wb, do2[s])
                dqs[s] = dqs[s] + _dot(dzb, kj)
                cgs[s] = cgs[s] + jnp.sum(g, axis=-1, keepdims=True)
            return [None if c is None else (cls[s], cgs[s], dqs[s]) for s, c in enumerate(carries)]

        zc = jnp.zeros((2 * _TQ, 1), _F32)
        outs = _sweep(i, step, (zc, zc, jnp.zeros((2 * _TQ, _LANES), _F32)))
        for s in range(nsub):
            dq_ref[part[s], :] = (_unstack_heads(outs[s][3], masks) * scale).astype(dq_ref.dtype)

        @pl.when(i == nq - 1)
        def _():
            dk_ref[...] = dk_acc[...].astype(dk_ref.dtype)
            dv_ref[...] = dv_acc[...].astype(dv_ref.dtype)

    tile_spec = pl.BlockSpec((tq, _LANES), lambda h, i: (i, h))
    column = pl.BlockSpec((T, _LANES), lambda h, i: (0, h))
    outs, extra = _call(
        kern, comm, name=f"stick_attn_bwd_{layer}", grid=(npair, nq),
        in_specs=_qkv_specs(T, col0 // _LANES, npair, tq) + [tile_spec, tile_spec],
        out_specs=[tile_spec, column, column],
        out_shape=[jax.ShapeDtypeStruct((T, width), _ACT)] * 3,
        scratch_shapes=[pltpu.VMEM((T, _LANES), _F32), pltpu.VMEM((T, _LANES), _F32)],
        args=(hq, hq, hq, o, do), semantics=("arbitrary", "arbitrary"))
    return outs, extra


_DENSE = ("w_in", "w_proj_a", "w_proj_b", "w_out", "w_ffn_in", "w_ffn_out")
_COL_SHARDED = {"w_in": True, "w_proj_a": True, "w_proj_b": True, "w_out": False, "w_ffn_in": True, "w_ffn_out": False}
_SMALL = ("b_gate", "rel_bias", "ln1_g", "ln1_b", "ln2_g", "ln2_b")


class _Plans:
    def __init__(self, plans=None, own_w_in=None):
        self.plans = plans or {}
        self.own_w_in = own_w_in or {}

    def start(self, key):
        if key not in self.plans:
            return None, None
        return self.plans[key]()

    @staticmethod
    def finish(done, extra):
        if done is not None:
            done(extra)


def _layer_fwd(x, W, small, l, alpha, plans):
    WA = small["rel_bias"].shape[1] * _HEAD
    row = lambda v: v[l].reshape(1, -1)
    if l in plans.own_w_in:
        h, xb, W["w_in"] = _in_proj_gathering(x, plans.own_w_in[l], l)
    else:
        comm, done = plans.start(f"in_proj_{l}")
        (h, xb), extra = _in_proj(x, W["w_in"], l, comm)
        plans.finish(done, extra)
    WB = (h.shape[1] - 2 * x.shape[1] - 3 * WA) // 3
    bias = _bias_tiles(small["rel_bias"][l])
    comm, done = plans.start(f"band_fwd_{l}")
    oa, extra = _attn_a_fwd(h, bias, 0, WA, l, comm)
    plans.finish(done, extra)
    comm, done = plans.start(f"stick_fwd_{l}")
    ob, extra = _sb_fwd(h, 3 * WA, WB, l, comm)
    plans.finish(done, extra)
    comm, done = plans.start(f"mix_fwd_{l}")
    (x1, u1, pre), extra = _mix_fwd(oa, ob, h, x, W["w_proj_a"], W["w_proj_b"], W["w_out"], row(small["b_gate"]),
                                            row(small["ln1_g"]), row(small["ln1_b"]), alpha, l, comm)
    plans.finish(done, extra)
    comm, done = plans.start(f"ffn_fwd_{l}")
    (x2, u2, act, gu, x1b), extra = _ffn_fwd(x1, W["w_ffn_in"], W["w_ffn_out"], row(small["ln2_g"]),
                                             row(small["ln2_b"]), alpha, l, comm)
    plans.finish(done, extra)
    return x2, dict(xb=xb, h=h, bias=bias, oa=oa, ob=ob, x1b=x1b, u1=u1, pre=pre, u2=u2, act=act, gu=gu)


def _layer_bwd(dy_or_target, S, W, small, l, last, alpha, plans, gw):
    D = S["xb"].shape[1]
    WA, WB = S["oa"].shape[1], S["ob"].shape[1]
    row = lambda v: v[l].reshape(1, -1)

    def blocks(g, n):
        return g if _COL_SHARDED[n] else g.reshape(4, g.shape[0] // 4, g.shape[1])

    dx1, du2b, dgu, st2 = _ffn_bwd(S["u2"], dy_or_target, S["gu"], row(small["ln2_g"]), row(small["ln2_b"]),
                                   W["w_ffn_in"], W["w_ffn_out"], alpha, l, last)
    gw["w_ffn_in"] = blocks(_grad_w(S["x1b"], dgu, col_shards=True, name=f"grad_w_ffn_in_{l}")[0], "w_ffn_in")
    gw["w_ffn_out"] = blocks(_grad_w(S["act"], du2b, col_shards=False, name=f"grad_w_ffn_out_{l}")[0], "w_ffn_out")
    du1, du1b, dya, dyb, dhg, doa, dob, st1 = _mix_bwd(S["u1"], dx1, S["oa"], S["ob"], S["h"], W["w_proj_a"],
                                                       W["w_proj_b"], W["w_out"], row(small["b_gate"]),
                                                       row(small["ln1_g"]), l)
    gw["w_out"] = blocks(_grad_w(S["pre"], du1b, col_shards=False, name=f"grad_w_out_{l}")[0], "w_out")
    gw["w_proj_a"] = blocks(_grad_w(S["oa"], dya, col_shards=True, name=f"grad_w_proj_a_{l}")[0], "w_proj_a")
    gw["w_proj_b"] = blocks(_grad_w(S["ob"], dyb, col_shards=True, name=f"grad_w_proj_b_{l}")[0], "w_proj_b")
    comm, done = plans.start(f"band_bwd_{l}")
    (dqa, dka, dva, dbias), extra = _attn_a_bwd(S["h"], S["bias"], doa, 0, WA, l, comm)
    plans.finish(done, extra)
    comm, done = plans.start(f"stick_bwd_{l}")
    (dqb, dkb, dvb), extra = _sb_bwd(S["h"], S["ob"], dob, 3 * WA, WB, l, comm)
    plans.finish(done, extra)
    dh = [dqa, dka, dva, dqb, dkb, dvb, dhg]
    comm, done = plans.start(f"grad_w_in_{l}")
    gw["w_in"], extra = _grad_w_pieces(S["xb"], dh, f"grad_w_in_{l}", comm)
    plans.finish(done, extra)
    comm, done = plans.start(f"in_proj_bwd_{l}")
    dx, extra = _residual_nt(du1, alpha, dh, W["w_in"], f"in_proj_bwd_{l}", comm)
    plans.finish(done, extra)
    gs = dict(b_gate=st1[0], rel_bias=_fold_bias_grad(dbias), ln1_g=st1[1, :D], ln1_b=st1[1, D:],
              ln2_g=st2[0], ln2_b=st2[1])
    return dx, gs, st2[2]


def _local_step(x, target, W, small, plans=None, gws=None):
    depth = len(W)
    alpha = float((2 * depth) ** 0.25)
    plans = plans or _Plans()
    gws = gws if gws is not None else [dict() for _ in range(depth)]
    saved = []
    h = x
    for l in range(depth):
        h, S = _layer_fwd(h, W[l], small, l, alpha, plans)
        saved.append(S)
    gss = [None] * depth
    d = target
    sq = None
    for l in reversed(range(depth)):
        d, gss[l], sq_l = _layer_bwd(d, saved[l], W[l], small, l, l == depth - 1, alpha, plans, gws[l])
        if l == depth - 1:
            sq = sq_l
    return sq, d, gws, gss


def _place():
    return lax.axis_index("x"), lax.axis_index("y"), lax.axis_index("c")


def _remote(src, dst, send_sem, recv_sem, to):
    return pltpu.make_async_remote_copy(src_ref=src, dst_ref=dst, send_sem=send_sem, recv_sem=recv_sem,
                                        device_id=to, device_id_type=_MESH)


def _half(ref, hc):
    kh = ref.shape[0] // 2
    return ref.at[pl.ds(pl.multiple_of(hc * kh, 16), kh), :]


def _gather_plan(blocks, fractions):
    nt = len(blocks)

    def run(step, nsteps, ins, outs, sems):
        send_sems, recv_sems, loc_sems = sems
        x, y, c = _place()
        k = 2 * x + y
        me, sibling = (x, y, c), (x, y, 1 - c)
        chips = [(1 - x, y), (x, 1 - y), (1 - x, 1 - y)]
        chip_k = [2 * cx + cy for cx, cy in chips]

        def ici(t, s, owner_k, to, src=None):
            dst = _half(outs[t].at[owner_k], c)
            return _remote(dst if src is None else src, dst, send_sems.at[t, s], recv_sems.at[t, s], to)

        def passed(t, s, hc, to):
            blk = _half(outs[t].at[chip_k[s]], hc)
            return _remote(blk, blk, send_sems.at[t, 3 + s], recv_sems.at[t, 3 + s], to)

        def local(t):
            return pltpu.make_async_copy(ins[t], outs[t].at[k], loc_sems.at[t])

        @pl.when(step == 0)
        def _():
            for t in range(nt):
                local(t).start()
                for s, chip in enumerate(chips):
                    ici(t, s, k, (*chip, c), src=_half(ins[t], c)).start()

        for t in range(nt):
            @pl.when(step == min(nsteps - 1, int(fractions[t] * nsteps)))
            def _():
                for s in range(3):
                    ici(t, s, chip_k[s], me).wait_recv()
                    passed(t, s, c, sibling).start()

        @pl.when(step == nsteps - 1)
        def _():
            for t in range(nt):
                for s, chip in enumerate(chips):
                    passed(t, s, 1 - c, me).wait_recv()
            for t in range(nt):
                for s, chip in enumerate(chips):
                    ici(t, s, k, (*chip, c), src=_half(ins[t], c)).wait_send()
                    passed(t, s, c, sibling).wait_send()
                local(t).wait()

    return _Comm(blocks, [jax.ShapeDtypeStruct((4,) + b.shape, b.dtype) for b in blocks],
                 [pltpu.SemaphoreType.DMA((nt, 6)), pltpu.SemaphoreType.DMA((nt, 6)), pltpu.SemaphoreType.DMA((nt,))], run)


def _scatter_plan(grads, owners):
    nt = len(grads)
    shapes = [g.shape[1:] if g.ndim == 3 else (g.shape[0], g.shape[1] // 4) for g in grads]

    def run(step, nsteps, ins, outs, sems):
        send_sems, recv_sems, loc_sems = sems
        x, y, c = _place()
        me = 4 * x + 2 * y + c

        def target(r):
            tx = 1 - x if r & 2 else x
            ty = 1 - y if r & 1 else y
            return tx, ty

        def block(t, chip):
            if len(ins[t].shape) == 3:
                return ins[t].at[chip]
            n = shapes[t][1]
            return ins[t].at[:, pl.ds(pl.multiple_of(chip * n, _LANES), n)]

        def send(t, r):
            tx, ty = target(r)
            return _remote(block(t, 2 * tx + ty), outs[t].at[me], send_sems.at[t, r], recv_sems.at[t, 2 * r + c],
                           (tx, ty, owners[t]))

        def local(t):
            return pltpu.make_async_copy(block(t, 2 * x + y), outs[t].at[me], loc_sems.at[t])

        @pl.when(step == 0)
        def _():
            for t in range(nt):
                @pl.when(c == owners[t])
                def _():
                    local(t).start()

                @pl.when(c != owners[t])
                def _():
                    send(t, 0).start()

                for r in range(1, 4):
                    send(t, r).start()

        @pl.when(step == nsteps - 1)
        def _():
            for t in range(nt):
                @pl.when(c == owners[t])
                def _():
                    for r in range(4):
                        sx, sy = target(r)
                        for cs in range(2):
                            if r == 0 and cs == owners[t]:
                                continue
                            src_dev = 4 * sx + 2 * sy + cs
                            _remote(block(t, 0), outs[t].at[src_dev], send_sems.at[t, r], recv_sems.at[t, 2 * r + cs],
                                    (x, y, c)).wait_recv()
                    local(t).wait()

                @pl.when(c != owners[t])
                def _():
                    send(t, 0).wait_send()

                for r in range(1, 4):
                    send(t, r).wait_send()

    return _Comm(grads, [jax.ShapeDtypeStruct((8,) + s, g.dtype) for s, g in zip(shapes, grads)],
                 [pltpu.SemaphoreType.DMA((nt, 4)), pltpu.SemaphoreType.DMA((nt, 8)), pltpu.SemaphoreType.DMA((nt,))], run)


def _share_plan(reduced, owners):
    nt = len(reduced)

    def run(step, nsteps, ins, outs, sems):
        del ins
        send_sems, recv_sems = sems
        x, y, c = _place()

        def give(t, to):
            return _remote(outs[t], outs[t], send_sems.at[t], recv_sems.at[t], to)

        @pl.when(step == 0)
        def _():
            for t in range(nt):
                @pl.when(c == owners[t])
                def _():
                    give(t, (x, y, 1 - c)).start()

        @pl.when(step == nsteps - 1)
        def _():
            for t in range(nt):
                @pl.when(c == owners[t])
                def _():
                    give(t, (x, y, 1 - c)).wait_send()

                @pl.when(c != owners[t])
                def _():
                    give(t, (x, y, c)).wait_recv()

    return _Comm(reduced, [jax.ShapeDtypeStruct(r.shape, r.dtype) for r in reduced],
                 [pltpu.SemaphoreType.DMA((nt,)), pltpu.SemaphoreType.DMA((nt,))], run,
                 aliases={t: t for t in range(nt)})


def _join(a, b):
    ni, no, ns = len(a.inputs), len(a.out_shapes), len(a.sems)

    def run(step, nsteps, ins, outs, sems):
        a.run(step, nsteps, ins[:ni], outs[:no], sems[:ns])
        b.run(step, nsteps, ins[ni:], outs[no:], sems[ns:])

    aliases = dict(a.aliases)
    aliases.update({ni + i: no + o for i, o in b.aliases.items()})
    return _Comm(a.inputs + b.inputs, a.out_shapes + b.out_shapes, a.sems + b.sems, run, aliases)


def _peer(x, y, c, r):
    px = 1 - x if r & 4 else x
    py = 1 - y if r & 2 else y
    pc = 1 - c if r & 1 else c
    return (px, py, pc), 4 * px + 2 * py + pc


def _row_tile(rows, cap=512):
    return max((t for t in range(16, min(rows, cap) + 1, 16) if rows % t == 0), default=rows)


def _sum_slots(st, name):
    _, K, n = st.shape
    tr = _row_tile(K)

    def kern(s_ref, o_ref):
        acc = s_ref[0].astype(_F32)
        for d in range(1, 8):
            acc = acc + s_ref[d].astype(_F32)
        o_ref[...] = acc.astype(o_ref.dtype)

    return pl.pallas_call(
        kern, name=name, grid=(K // tr,),
        in_specs=[pl.BlockSpec((8, tr, n), lambda i: (0, i, 0))], out_specs=_rows(tr, n),
        out_shape=jax.ShapeDtypeStruct((K, n), _ACT),
        compiler_params=_cparams("parallel"),
    )(st)


def _all_reduce_small(p):
    R = p.shape[0]

    def body(p_ref, o_ref, stage, send_sems, recv_sems):
        x, y, c = _place()
        me = 4 * x + 2 * y + c
        stage[me] = p_ref[...]
        sent = []
        for r in range(1, 8):
            to, _ = _peer(x, y, c, r)
            cp = _remote(p_ref, stage.at[me], send_sems.at[r - 1], recv_sems.at[r - 1], to)
            cp.start()
            sent.append(cp)
        for r in range(1, 8):
            _, src_dev = _peer(x, y, c, r)
            _remote(p_ref, stage.at[src_dev], send_sems.at[r - 1], recv_sems.at[r - 1], (x, y, c)).wait_recv()
        acc = stage[0]
        for d in range(1, 8):
            acc = acc + stage[d]
        o_ref[...] = acc
        for cp in sent:
            cp.wait_send()

    vm = pl.BlockSpec(memory_space=pltpu.VMEM)
    return pl.pallas_call(
        body, name="all_reduce_small",
        in_specs=[vm], out_specs=vm,
        out_shape=jax.ShapeDtypeStruct((R, _LANES), _F32),
        scratch_shapes=[pltpu.VMEM((8, R, _LANES), _F32), pltpu.SemaphoreType.DMA((7,)), pltpu.SemaphoreType.DMA((7,))],
    )(p)


def _adamw_update(gv, w_ref, m_ref, v_ref, gf_ref, d_ref, nm_ref, nv_ref):
    nm = _B1 * m_ref[...] + (1.0 - _B1) * gv
    nv = _B2 * v_ref[...] + (1.0 - _B2) * (gv * gv)
    m_hat = nm / (1.0 - _B1 ** _STEP)
    v_hat = nv / (1.0 - _B2 ** _STEP)
    gf_ref[...] = gv
    d_ref[...] = -_LR * (m_hat / (jnp.sqrt(v_hat) + _EPS) + _WD * w_ref[...])
    nm_ref[...] = nm
    nv_ref[...] = nv


def _adamw_layers(w, g_layers, m, v, name):
    _, K, n = w.shape
    tr = _row_tile(K)

    def kern(w_ref, g0_ref, g1_ref, m_ref, v_ref, *out_refs):
        first = pl.program_id(0) == 0
        gv = jnp.where(first, g0_ref[...].astype(_F32), g1_ref[...].astype(_F32))
        _adamw_update(gv, w_ref, m_ref, v_ref, *out_refs)

    stacked = pl.BlockSpec((None, tr, n), lambda l, i: (l, i, 0))
    layer = pl.BlockSpec((tr, n), lambda l, i: (i, 0))
    return tuple(pl.pallas_call(
        kern, name=name, grid=(2, K // tr),
        in_specs=[stacked, layer, layer, stacked, stacked], out_specs=[stacked] * 4,
        out_shape=[jax.ShapeDtypeStruct(w.shape, _F32)] * 4,
        compiler_params=_cparams("parallel", "parallel"),
    )(w, g_layers[0], g_layers[1], m, v))


def _adamw(w, g, m, v, name):
    shape = w.shape
    w2, g2, m2, v2 = (a.reshape(-1, shape[-1]) for a in (w, g, m, v))
    R, C = w2.shape
    tr = next((t for t in (256, 128, 64, 32, 16) if R % t == 0), R)

    def kern(w_ref, g_ref, m_ref, v_ref, *out_refs):
        _adamw_update(g_ref[...].astype(_F32), w_ref, m_ref, v_ref, *out_refs)

    outs = pl.pallas_call(
        kern, name=name, grid=(R // tr,),
        in_specs=[_rows(tr, C)] * 4, out_specs=[_rows(tr, C)] * 4,
        out_shape=[jax.ShapeDtypeStruct((R, C), _F32)] * 4,
        compiler_params=_cparams("parallel"),
    )(w2, g2, m2, v2)
    return tuple(o.reshape(shape) for o in outs)


def _pack_small(gss, sq):
    parts = [gss[l][n].reshape(-1) for n in _SMALL for l in range(len(gss))] + [jnp.sum(sq).reshape(1)]
    flat = jnp.concatenate(parts)
    rows = -(-flat.shape[0] // (8 * _LANES)) * 8
    return jnp.pad(flat, (0, rows * _LANES - flat.shape[0])).reshape(rows, _LANES)


def _unpack_small(total, shapes):
    flat = total.reshape(-1)
    out, off = {}, 0
    for n in _SMALL:
        layers = []
        for _ in range(shapes[n][0]):
            size = 1
            for s in shapes[n][1:]:
                size *= s
            layers.append(flat[off:off + size].reshape(shapes[n][1:]))
            off += size
        out[n] = jnp.stack(layers)
    return out, flat[off]


_GATHER = {
    "band_fwd_0": [(0, "w_proj_a"), (0, "w_proj_b"), (0, "w_out"), (0, "w_ffn_out")],
    "stick_fwd_0": [(0, "w_ffn_in"), (1, "w_proj_a"), (1, "w_proj_b"), (1, "w_out")],
    "mix_fwd_0": [(1, "w_ffn_out")],
    "ffn_fwd_0": [(1, "w_in"), (1, "w_ffn_in")],
}
_SCATTER = {
    "band_bwd_1": [(1, "w_ffn_in"), (1, "w_ffn_out")],
    "stick_bwd_1": [(1, "w_proj_a"), (1, "w_proj_b"), (1, "w_out")],
    "band_bwd_0": [(1, "w_in"), (0, "w_ffn_in")],
    "stick_bwd_0": [(0, "w_ffn_out"), (0, "w_proj_a"), (0, "w_proj_b"), (0, "w_out")],
    "in_proj_bwd_0": [(0, "w_in")],
}
_SHARE = {"stick_bwd_1": "band_bwd_1", "band_bwd_0": "stick_bwd_1", "stick_bwd_0": "band_bwd_0", "grad_w_in_0": "stick_bwd_0"}


def _owner(key):
    del key
    return 1


def kernel(x, w_in, b_gate, rel_bias, w_proj_a, w_proj_b, w_out, ln1_g, ln1_b, w_ffn_in, w_ffn_out, ln2_g, ln2_b, loss_target, m_w_in, m_b_gate, m_rel_bias, m_w_proj_a, m_w_proj_b, m_w_out, m_ln1_g, m_ln1_b, m_w_ffn_in, m_w_ffn_out, m_ln2_g, m_ln2_b, v_w_in, v_b_gate, v_rel_bias, v_w_proj_a, v_w_proj_b, v_w_out, v_ln1_g, v_ln1_b, v_w_ffn_in, v_w_ffn_out, v_ln2_g, v_ln2_b):
    names = ("w_in", "b_gate", "rel_bias", "w_proj_a", "w_proj_b", "w_out", "ln1_g", "ln1_b", "w_ffn_in", "w_ffn_out", "ln2_g", "ln2_b")
    w = dict(zip(names, (w_in, b_gate, rel_bias, w_proj_a, w_proj_b, w_out, ln1_g, ln1_b, w_ffn_in, w_ffn_out, ln2_g, ln2_b)))
    m = dict(zip(names, (m_w_in, m_b_gate, m_rel_bias, m_w_proj_a, m_w_proj_b, m_w_out, m_ln1_g, m_ln1_b, m_w_ffn_in, m_w_ffn_out, m_ln2_g, m_ln2_b)))
    v = dict(zip(names, (v_w_in, v_b_gate, v_rel_bias, v_w_proj_a, v_w_proj_b, v_w_out, v_ln1_g, v_ln1_b, v_w_ffn_in, v_w_ffn_out, v_ln2_g, v_ln2_b)))
    T, D = x.shape[-2], x.shape[-1]
    assert w_in.shape[0] == 2, "the exchange schedule below is written for two layers"

    mine = [{n: w[n][l].astype(_MXU) for n in _DENSE} for l in range(2)]
    W = [dict(), dict()]
    gws = [dict(), dict()]
    slots, final = {}, {}

    def gather(keys):
        sizes = [mine[l][n].size for l, n in keys]
        passed, fractions = 0, []
        for s in sizes:
            passed += s
            fractions.append(0.15 + 0.6 * passed / sum(sizes))

        def done(outs):
            for (l, n), o in zip(keys, outs):
                W[l][n] = o
        return _gather_plan([mine[l][n] for l, n in keys], fractions), done

    def scatter(keys):
        comm = _scatter_plan([gws[l][n] for l, n in keys], [_owner(key) for key in keys])
        return comm, lambda outs: slots.update(zip(keys, outs))

    def share(keys):
        reduced = [_sum_slots(slots[key], f"sum_grad_{key[1]}_{key[0]}") for key in keys]
        comm = _share_plan(reduced, [_owner(key) for key in keys])
        return comm, lambda outs: final.update(zip(keys, outs))

    def both(first, second):
        (ca, da), (cb, db) = first, second
        na = len(ca.out_shapes)
        return _join(ca, cb), lambda outs: (da(outs[:na]), db(outs[na:]))

    plans = {key: functools.partial(gather, keys) for key, keys in _GATHER.items()}
    for key, keys in _SCATTER.items():
        plans[key] = functools.partial(scatter, keys)
    for key, scattered_under in _SHARE.items():
        handed = functools.partial(share, _SCATTER[scattered_under])
        carried = plans.get(key)
        plans[key] = handed if carried is None else (lambda carried=carried, handed=handed: both(carried(), handed()))
    small = {n: w[n] for n in _SMALL}
    sq, dx, _, gss = _local_step(x.reshape(T, D), loss_target.reshape(T, D), W, small,
                                  _Plans(plans, {0: mine[0]["w_in"]}), gws)

    comm, done = share(_SCATTER["in_proj_bwd_0"])
    done(_comm_only(comm, "share_last"))
    total = _all_reduce_small(_pack_small(gss, sq))
    small_grads, sq_all = _unpack_small(total, {n: w[n].shape for n in _SMALL})
    loss = 0.5 * sq_all / D

    grad, delta, new_m, new_v = {}, {}, {}, {}
    for n in names:
        if n in _DENSE:
            updated = _adamw_layers(w[n], [final[(l, n)] for l in range(2)], m[n], v[n], f"adamw_{n}")
        else:
            updated = _adamw(w[n], small_grads[n], m[n], v[n], f"adamw_{n}")
        grad[n], delta[n], new_m[n], new_v[n] = updated
    return (loss, dx.reshape(x.shape), *[grad[n] for n in names], *[delta[n] for n in names],
            *[new_m[n] for n in names], *[new_v[n] for n in names])
```
